```python
import math
import jax, jax.numpy as jnp
from jax import lax
import numpy as np

D_MODEL = 1024
BATCH = 8
SEQ = 4096
DEPTH = 2

ATTN_HEADS = 8
ATTN_HEAD_DIM = 64
ATTN_WIDTH = ATTN_HEADS * ATTN_HEAD_DIM
HGRN_HEADS = 4
HGRN_HEAD_DIM = 128
HGRN_WIDTH = HGRN_HEADS * HGRN_HEAD_DIM
MIX_WIDTH = ATTN_WIDTH + HGRN_WIDTH
IN_PROJ_WIDTH = 3 * ATTN_WIDTH + 4 * HGRN_WIDTH
DILATED_PATTERNS = ((128, 1), (512, 4), (2048, 16))
ROPE_THETA = 10000.0
HGRN_CHUNK = 16
MLP_HIDDEN = 4 * D_MODEL
NORM_EPS = 1e-6
MASK_VALUE = -1e30

kernel_name = 'hymba_hgrn2_dilated_swa_hybrid'


def rms_norm(x, gain):
    xf = x.astype(jnp.float32)
    xf = xf * lax.rsqrt(jnp.mean(xf * xf, axis=-1, keepdims=True) + NORM_EPS)
    return (xf * gain.astype(jnp.float32)).astype(x.dtype)


def split_heads(a, n_heads, head_dim):
    b, s, _ = a.shape
    return a.reshape(b, s, n_heads, head_dim).transpose(0, 2, 1, 3)


def merge_heads(a):
    b, h, s, d = a.shape
    return a.transpose(0, 2, 1, 3).reshape(b, s, h * d)


def rotary(x, positions):
    half = x.shape[-1] // 2
    inv_freq = ROPE_THETA ** (-jnp.arange(half, dtype=jnp.float32) / half)
    ang = positions.astype(jnp.float32)[:, None] * inv_freq[None, :]
    cos, sin = jnp.cos(ang), jnp.sin(ang)
    x1, x2 = x[..., :half], x[..., half:]
    return jnp.concatenate([x1 * cos - x2 * sin, x1 * sin + x2 * cos], axis=-1)


def dilated_window_attention(q, k, v, window, dilation):
    b, h, s, d = q.shape
    span = window // dilation
    unit = dilation * span
    s_pad = -(-s // unit) * unit
    pad = ((0, 0), (0, 0), (0, s_pad - s), (0, 0))
    q, k, v = jnp.pad(q, pad), jnp.pad(k, pad), jnp.pad(v, pad)
    n_sub = s_pad // dilation
    n_blk = n_sub // span

    def to_sub(a):
        a = a.reshape(b, h, n_sub, dilation, d).transpose(0, 1, 3, 2, 4)
        return a.reshape(b, h, dilation, n_blk, span, d)

    def with_prev(a):
        prev = jnp.pad(a, ((0, 0), (0, 0), (0, 0), (1, 0), (0, 0), (0, 0)))[:, :, :, :-1]
        return jnp.concatenate([prev, a], axis=4)

    qs = to_sub(q)
    kc, vc = with_prev(to_sub(k)), with_prev(to_sub(v))
    scores = jnp.einsum('bhrnid,bhrnjd->bhrnij', qs, kc)
    i = jnp.arange(span)[:, None]
    j = jnp.arange(2 * span)[None, :]
    dist = span + i - j
    band = (dist >= 0) & (dist <= span)
    blk = jnp.arange(n_blk)[:, None, None]
    valid = band[None] & ((blk > 0) | (j >= span)[None])
    scores = jnp.where(valid, scores, MASK_VALUE)
    m = jnp.max(scores, axis=-1, keepdims=True)
    p = jnp.where(valid, jnp.exp(scores - m), 0.0)
    l = jnp.sum(p, axis=-1, keepdims=True)
    o = jnp.einsum('bhrnij,bhrnjd->bhrnid', p, vc) / l
    lse = (m + jnp.log(l))[..., 0]

    def from_sub(a):
        tail = a.shape[5:]
        a = a.reshape((b, h, dilation, n_sub) + tail)
        a = jnp.moveaxis(a, 2, 3)
        return a.reshape((b, h, s_pad) + tail)[:, :, :s]

    return from_sub(o), from_sub(lse)


def dilated_attention_group(q_a, k_a, v_a, positions):
    q = rotary(split_heads(q_a, ATTN_HEADS, ATTN_HEAD_DIM).astype(jnp.float32), positions)
    q = q * (ATTN_HEAD_DIM ** -0.5)
    k = rotary(split_heads(k_a, ATTN_HEADS, ATTN_HEAD_DIM).astype(jnp.float32), positions)
    v = split_heads(v_a, ATTN_HEADS, ATTN_HEAD_DIM).astype(jnp.float32)
    outs, lses = [], []
    for window, dilation in DILATED_PATTERNS:
        o, lse = dilated_window_attention(q, k, v, window, dilation)
        outs.append(o)
        lses.append(lse)
    weights = jax.nn.softmax(jnp.stack(lses, axis=0), axis=0)
    o = jnp.einsum('pbhs,pbhsd->bhsd', weights, jnp.stack(outs, axis=0))
    return merge_heads(o)


def hgrn_lower_bounds(lb_logits):
    p = jax.nn.softmax(lb_logits.astype(jnp.float32), axis=0)
    return jnp.cumsum(p, axis=0) - p[0]


def hgrn2_chunkwise(q, k, v, log_f):
    b, h, s, kd = q.shape
    vd = v.shape[-1]
    c = HGRN_CHUNK
    n = s // c
    q = q.reshape(b, h, n, c, kd)
    k = k.reshape(b, h, n, c, kd)
    v = v.reshape(b, h, n, c, vd)
    g = jnp.cumsum(log_f.reshape(b, h, n, c, kd), axis=3)
    g_last = g[:, :, :, -1:]
    causal = jnp.tril(jnp.ones((c, c), dtype=bool))[:, :, None]
    diff = g[:, :, :, :, None, :] - g[:, :, :, None, :, :]
    decay = jnp.where(causal, jnp.exp(jnp.where(causal, diff, 0.0)), 0.0)
    a = jnp.einsum('bhnik,bhnjk,bhnijk->bhnij', q, k, decay)
    o_intra = jnp.einsum('bhnij,bhnjv->bhniv', a, v)
    q_in = q * jnp.exp(g)
    k_out = k * jnp.exp(g_last - g)
    chunk_decay = jnp.exp(g_last[:, :, :, 0])

    def step(state, xs):
        q_n, k_n, v_n, dec_n = xs
        o_n = jnp.einsum('bhik,bhkv->bhiv', q_n, state)
        state = dec_n[..., None] * state + jnp.einsum('bhjk,bhjv->bhkv', k_n, v_n)
        return state, o_n

    xs = (jnp.moveaxis(q_in, 2, 0), jnp.moveaxis(k_out, 2, 0),
          jnp.moveaxis(v, 2, 0), jnp.moveaxis(chunk_decay, 2, 0))
    state0 = jnp.zeros((b, h, kd, vd), dtype=jnp.float32)
    _, o_inter = lax.scan(step, state0, xs)
    o = o_intra + jnp.moveaxis(o_inter, 0, 2)
    return o.reshape(b, h, s, vd)


def hgrn2_group(q_h, f_h, i_h, g_h, lower_bound, out_gain):
    q = jax.nn.silu(split_heads(q_h, HGRN_HEADS, HGRN_HEAD_DIM).astype(jnp.float32))
    q = q * (HGRN_HEAD_DIM ** -0.5)
    z = split_heads(f_h, HGRN_HEADS, HGRN_HEAD_DIM).astype(jnp.float32)
    v = split_heads(i_h, HGRN_HEADS, HGRN_HEAD_DIM).astype(jnp.float32)
    lb = lower_bound.reshape(HGRN_HEADS, 1, HGRN_HEAD_DIM)
    log_f = jnp.log(lb + (1.0 - lb) * jax.nn.sigmoid(z))
    k = (1.0 - lb) * jax.nn.sigmoid(-z)
    o = hgrn2_chunkwise(q, k, v, log_f)
    o = rms_norm(o, out_gain)
    gate = jax.nn.silu(split_heads(g_h, HGRN_HEADS, HGRN_HEAD_DIM).astype(jnp.float32))
    return merge_heads(o * gate)


def _fwd_setup_inputs(seed: int = 0) -> dict:
    key = jax.random.key(seed)
    ks = jax.random.split(key, 12)
    f32 = jnp.float32
    x = jax.random.normal(ks[0], (BATCH, SEQ, D_MODEL), f32)
    norm_mix = 1.0 + 0.02 * jax.random.normal(ks[1], (DEPTH, D_MODEL), f32)
    w_in = jax.random.normal(ks[2], (DEPTH, D_MODEL, IN_PROJ_WIDTH), f32) * D_MODEL ** -0.5
    attn_out_gain = 1.0 + 0.02 * jax.random.normal(ks[3], (DEPTH, ATTN_WIDTH), f32)
    hgrn_lb_logits = 0.1 * jax.random.normal(ks[4], (DEPTH, HGRN_WIDTH), f32)
    hgrn_out_gain = 1.0 + 0.02 * jax.random.normal(ks[5], (DEPTH, HGRN_HEAD_DIM), f32)
    w_out = jax.random.normal(ks[6], (DEPTH, MIX_WIDTH, D_MODEL), f32) * MIX_WIDTH ** -0.5
    norm_mlp = 1.0 + 0.02 * jax.random.normal(ks[7], (DEPTH, D_MODEL), f32)
    w_up = jax.random.normal(ks[8], (DEPTH, D_MODEL, MLP_HIDDEN), f32) * D_MODEL ** -0.5
    w_down = jax.random.normal(ks[9], (DEPTH, MLP_HIDDEN, D_MODEL), f32) * MLP_HIDDEN ** -0.5
    norm_final = 1.0 + 0.02 * jax.random.normal(ks[10], (D_MODEL,), f32)
    return {'x': x, 'norm_mix': norm_mix, 'w_in': w_in, 'attn_out_gain': attn_out_gain,
            'hgrn_lb_logits': hgrn_lb_logits, 'hgrn_out_gain': hgrn_out_gain,
            'w_out': w_out, 'norm_mlp': norm_mlp, 'w_up': w_up, 'w_down': w_down,
            'norm_final': norm_final}


def _fwd_reference(x, norm_mix, w_in, attn_out_gain, hgrn_lb_logits, hgrn_out_gain,
              w_out, norm_mlp, w_up, w_down, norm_final):
    seq = x.shape[1]
    positions = jnp.arange(seq, dtype=jnp.int32)
    lower_bounds = hgrn_lower_bounds(hgrn_lb_logits)
    split_points = [ATTN_WIDTH, 2 * ATTN_WIDTH, 3 * ATTN_WIDTH,
                    3 * ATTN_WIDTH + HGRN_WIDTH, 3 * ATTN_WIDTH + 2 * HGRN_WIDTH,
                    3 * ATTN_WIDTH + 3 * HGRN_WIDTH]
    for layer in range(DEPTH):
        h = rms_norm(x, norm_mix[layer])
        proj = h @ w_in[layer]
        q_a, k_a, v_a, q_h, f_h, i_h, g_h = jnp.split(proj, split_points, axis=-1)
        attn = dilated_attention_group(q_a, k_a, v_a, positions)
        attn = rms_norm(attn, attn_out_gain[layer])
        rec = hgrn2_group(q_h, f_h, i_h, g_h, lower_bounds[layer], hgrn_out_gain[layer])
        mixed = jnp.concatenate([attn, rec], axis=-1).astype(x.dtype)
        x = x + mixed @ w_out[layer]
        h = rms_norm(x, norm_mlp[layer])
        x = x + jnp.square(jax.nn.relu(h @ w_up[layer])) @ w_down[layer]
    return rms_norm(x, norm_final)


import jax as _jax
import jax.numpy as _jnp

TWIN_FORMAT = 'train_step'
FWD_PARAMS = ['x', 'norm_mix', 'w_in', 'attn_out_gain', 'hgrn_lb_logits', 'hgrn_out_gain', 'w_out', 'norm_mlp', 'w_up', 'w_down', 'norm_final']
TWIN_WEIGHTS = ['norm_mix', 'w_in', 'attn_out_gain', 'hgrn_lb_logits', 'hgrn_out_gain', 'w_out', 'norm_mlp', 'w_up', 'w_down', 'norm_final']
TWIN_DIFF_INPUT = 'x'
TWIN_INPUTS = ['x', 'norm_mix', 'w_in', 'attn_out_gain', 'hgrn_lb_logits', 'hgrn_out_gain', 'w_out', 'norm_mlp', 'w_up', 'w_down', 'norm_final', 'loss_target', 'm_norm_mix', 'm_w_in', 'm_attn_out_gain', 'm_hgrn_lb_logits', 'm_hgrn_out_gain', 'm_w_out', 'm_norm_mlp', 'm_w_up', 'm_w_down', 'm_norm_final', 'v_norm_mix', 'v_w_in', 'v_attn_out_gain', 'v_hgrn_lb_logits', 'v_hgrn_out_gain', 'v_w_out', 'v_norm_mlp', 'v_w_up', 'v_w_down', 'v_norm_final']
TWIN_OUTPUTS = ['loss', 'grad_x', 'grad_norm_mix', 'grad_w_in', 'grad_attn_out_gain', 'grad_hgrn_lb_logits', 'grad_hgrn_out_gain', 'grad_w_out', 'grad_norm_mlp', 'grad_w_up', 'grad_w_down', 'grad_norm_final', 'delta_norm_mix', 'delta_w_in', 'delta_attn_out_gain', 'delta_hgrn_lb_logits', 'delta_hgrn_out_gain', 'delta_w_out', 'delta_norm_mlp', 'delta_w_up', 'delta_w_down', 'delta_norm_final', 'new_m_norm_mix', 'new_m_w_in', 'new_m_attn_out_gain', 'new_m_hgrn_lb_logits', 'new_m_hgrn_out_gain', 'new_m_w_out', 'new_m_norm_mlp', 'new_m_w_up', 'new_m_w_down', 'new_m_norm_final', 'new_v_norm_mix', 'new_v_w_in', 'new_v_attn_out_gain', 'new_v_hgrn_lb_logits', 'new_v_hgrn_out_gain', 'new_v_w_out', 'new_v_norm_mlp', 'new_v_w_up', 'new_v_w_down', 'new_v_norm_final']
TWIN_LEAF_KINDS = {'loss': 'loss', 'grad_x': 'grad_x', 'grad_norm_mix': 'grad_w', 'grad_w_in': 'grad_w', 'grad_attn_out_gain': 'grad_w', 'grad_hgrn_lb_logits': 'grad_w', 'grad_hgrn_out_gain': 'grad_w', 'grad_w_out': 'grad_w', 'grad_norm_mlp': 'grad_w', 'grad_w_up': 'grad_w', 'grad_w_down': 'grad_w', 'grad_norm_final': 'grad_w', 'delta_norm_mix': 'delta_w', 'delta_w_in': 'delta_w', 'delta_attn_out_gain': 'delta_w', 'delta_hgrn_lb_logits': 'delta_w', 'delta_hgrn_out_gain': 'delta_w', 'delta_w_out': 'delta_w', 'delta_norm_mlp': 'delta_w', 'delta_w_up': 'delta_w', 'delta_w_down': 'delta_w', 'delta_norm_final': 'delta_w', 'new_m_norm_mix': 'new_m', 'new_m_w_in': 'new_m', 'new_m_attn_out_gain': 'new_m', 'new_m_hgrn_lb_logits': 'new_m', 'new_m_hgrn_out_gain': 'new_m', 'new_m_w_out': 'new_m', 'new_m_norm_mlp': 'new_m', 'new_m_w_up': 'new_m', 'new_m_w_down': 'new_m', 'new_m_norm_final': 'new_m', 'new_v_norm_mix': 'new_v', 'new_v_w_in': 'new_v', 'new_v_attn_out_gain': 'new_v', 'new_v_hgrn_lb_logits': 'new_v', 'new_v_hgrn_out_gain': 'new_v', 'new_v_w_out': 'new_v', 'new_v_norm_mlp': 'new_v', 'new_v_w_up': 'new_v', 'new_v_w_down': 'new_v', 'new_v_norm_final': 'new_v'}


def _forward(args):
    return _fwd_reference(*[args[k] for k in FWD_PARAMS])


def _output_shape():
    out = _jax.eval_shape(lambda: _forward(_fwd_setup_inputs(0)))
    return out.shape, out.dtype

N_MICROBATCH = 1
ADAM_LR = 0.001
ADAM_B1 = 0.9
ADAM_B2 = 0.999
ADAM_EPS = 1e-08
ADAM_WD = 0.01
ADAM_STEP = 10
PER_EXAMPLE_BATCH_AXIS = {'x': 0, 'loss_target': 0}
SHARED_INPUTS = []
_WEIGHT_DTYPES = {'norm_mix': _jnp.float32, 'w_in': _jnp.float32, 'attn_out_gain': _jnp.float32, 'hgrn_lb_logits': _jnp.float32, 'hgrn_out_gain': _jnp.float32, 'w_out': _jnp.float32, 'norm_mlp': _jnp.float32, 'w_up': _jnp.float32, 'w_down': _jnp.float32, 'norm_final': _jnp.float32}
MOMENT_SCALE = {'norm_mix': 1.643063e-01, 'w_in': 8.821577e-02, 'attn_out_gain': 1.468567e-01, 'hgrn_lb_logits': 4.071343e-03, 'hgrn_out_gain': 1.557307e-01, 'w_out': 1.157432e-01, 'norm_mlp': 1.284034e-01, 'w_up': 6.484535e-02, 'w_down': 1.446917e-01, 'norm_final': 3.297652e+01}


def _to_microbatches(a, axis):
    t = _jnp.moveaxis(a, axis, 0)
    t = t.reshape((N_MICROBATCH, t.shape[0] // N_MICROBATCH) + t.shape[1:])
    return _jnp.moveaxis(t, 1, axis + 1)


def setup_inputs(seed: int = 0) -> dict:
    inp = _fwd_setup_inputs(seed)
    key = _jax.random.fold_in(_jax.random.key(seed), 7919)
    shape, _ = _output_shape()
    out = dict(inp)
    out["loss_target"] = _jax.random.normal(_jax.random.fold_in(key, 0), shape, _jnp.float32)
    for i, name in enumerate(TWIN_WEIGHTS):
        w = inp[name].astype(_jnp.float32)
        if MOMENT_SCALE is None:
            s = _jnp.sqrt(_jnp.mean(_jnp.square(w)) + 1e-30)
        else:
            s = MOMENT_SCALE[name]
        km, kv = _jax.random.split(_jax.random.fold_in(key, i + 1))
        out[name] = w
        out["m_" + name] = s * _jax.random.normal(km, w.shape, _jnp.float32)
        out["v_" + name] = (s * s) * _jax.random.uniform(kv, w.shape, _jnp.float32, 0.5, 1.5)
    if N_MICROBATCH > 1:
        for name, axis in PER_EXAMPLE_BATCH_AXIS.items():
            out[name] = _to_microbatches(out[name], axis)
    return {'x': out['x'], 'norm_mix': out['norm_mix'], 'w_in': out['w_in'], 'attn_out_gain': out['attn_out_gain'], 'hgrn_lb_logits': out['hgrn_lb_logits'], 'hgrn_out_gain': out['hgrn_out_gain'], 'w_out': out['w_out'], 'norm_mlp': out['norm_mlp'], 'w_up': out['w_up'], 'w_down': out['w_down'], 'norm_final': out['norm_final'], 'loss_target': out['loss_target'], 'm_norm_mix': out['m_norm_mix'], 'm_w_in': out['m_w_in'], 'm_attn_out_gain': out['m_attn_out_gain'], 'm_hgrn_lb_logits': out['m_hgrn_lb_logits'], 'm_hgrn_out_gain': out['m_hgrn_out_gain'], 'm_w_out': out['m_w_out'], 'm_norm_mlp': out['m_norm_mlp'], 'm_w_up': out['m_w_up'], 'm_w_down': out['m_w_down'], 'm_norm_final': out['m_norm_final'], 'v_norm_mix': out['v_norm_mix'], 'v_w_in': out['v_w_in'], 'v_attn_out_gain': out['v_attn_out_gain'], 'v_hgrn_lb_logits': out['v_hgrn_lb_logits'], 'v_hgrn_out_gain': out['v_hgrn_out_gain'], 'v_w_out': out['v_w_out'], 'v_norm_mlp': out['v_norm_mlp'], 'v_w_up': out['v_w_up'], 'v_w_down': out['v_w_down'], 'v_norm_final': out['v_norm_final']}


def _loss(weights, diff, rest, loss_target):
    with _jax.named_scope("forward"):
        args = {**rest, TWIN_DIFF_INPUT: diff, **{k: w.astype(_WEIGHT_DTYPES[k]) for k, w in weights.items()}}
        y = _forward(args)
    with _jax.named_scope("loss_head"):
        err = _jnp.square(y.astype(_jnp.float32) - loss_target)
        return 0.5 * _jnp.sum(_jnp.mean(err, axis=-1)) if err.ndim else 0.5 * err


def _adamw(w, g, m, v):
    m = ADAM_B1 * m + (1.0 - ADAM_B1) * g
    v = ADAM_B2 * v + (1.0 - ADAM_B2) * _jnp.square(g)
    m_hat = m / (1.0 - ADAM_B1 ** ADAM_STEP)
    v_hat = v / (1.0 - ADAM_B2 ** ADAM_STEP)
    delta = -ADAM_LR * (m_hat / (_jnp.sqrt(v_hat) + ADAM_EPS) + ADAM_WD * w)
    return delta, m, v


def reference(x, norm_mix, w_in, attn_out_gain, hgrn_lb_logits, hgrn_out_gain, w_out, norm_mlp, w_up, w_down, norm_final, loss_target, m_norm_mix, m_w_in, m_attn_out_gain, m_hgrn_lb_logits, m_hgrn_out_gain, m_w_out, m_norm_mlp, m_w_up, m_w_down, m_norm_final, v_norm_mix, v_w_in, v_attn_out_gain, v_hgrn_lb_logits, v_hgrn_out_gain, v_w_out, v_norm_mlp, v_w_up, v_w_down, v_norm_final):
    given = dict(x=x, norm_mix=norm_mix, w_in=w_in, attn_out_gain=attn_out_gain, hgrn_lb_logits=hgrn_lb_logits, hgrn_out_gain=hgrn_out_gain, w_out=w_out, norm_mlp=norm_mlp, w_up=w_up, w_down=w_down, norm_final=norm_final, loss_target=loss_target, m_norm_mix=m_norm_mix, m_w_in=m_w_in, m_attn_out_gain=m_attn_out_gain, m_hgrn_lb_logits=m_hgrn_lb_logits, m_hgrn_out_gain=m_hgrn_out_gain, m_w_out=m_w_out, m_norm_mlp=m_norm_mlp, m_w_up=m_w_up, m_w_down=m_w_down, m_norm_final=m_norm_final, v_norm_mix=v_norm_mix, v_w_in=v_w_in, v_attn_out_gain=v_attn_out_gain, v_hgrn_lb_logits=v_hgrn_lb_logits, v_hgrn_out_gain=v_hgrn_out_gain, v_w_out=v_w_out, v_norm_mlp=v_norm_mlp, v_w_up=v_w_up, v_w_down=v_w_down, v_norm_final=v_norm_final)
    weights = {n: given[n] for n in TWIN_WEIGHTS}
    shared = {n: given[n] for n in SHARED_INPUTS}
    per_example = {n: given[n] for n in ['x']}
    grad_fn = _jax.value_and_grad(_loss, argnums=(0, 1))

    def one_microbatch(ex, loss_target):
        ex = dict(ex)
        diff = ex.pop(TWIN_DIFF_INPUT)
        return grad_fn(weights, diff, {**shared, **ex}, loss_target)

    if N_MICROBATCH == 1:
        loss, (grad_w, grad_x) = one_microbatch(per_example, given["loss_target"])
    else:
        def body(carry, xs):
            loss_sum, grad_sum = carry
            l_k, (gw_k, gx_k) = one_microbatch(xs[0], xs[1])
            with _jax.named_scope("update"):
                return (loss_sum + l_k, _jax.tree.map(_jnp.add, grad_sum, gw_k)), gx_k

        init = (_jnp.zeros((), _jnp.float32), _jax.tree.map(_jnp.zeros_like, weights))
        (loss, grad_w), grad_x = _jax.lax.scan(body, init, (per_example, given["loss_target"]))
    with _jax.named_scope("update"):
        delta_w, new_m, new_v = {}, {}, {}
        for n in TWIN_WEIGHTS:
            delta_w[n], new_m[n], new_v[n] = _adamw(weights[n], grad_w[n], given["m_" + n], given["v_" + n])
    return (loss, grad_x, *[grad_w[n] for n in TWIN_WEIGHTS], *[delta_w[n] for n in TWIN_WEIGHTS],
            *[new_m[n] for n in TWIN_WEIGHTS], *[new_v[n] for n in TWIN_WEIGHTS])
```

```python
import functools
import math

import numpy as np
import jax
import jax.numpy as jnp
from jax import lax
from jax.experimental import pallas as pl
from jax.experimental.pallas import tpu as pltpu

F32 = jnp.float32
BF16 = jnp.bfloat16
MESH = pl.DeviceIdType.MESH

SEQ = 4096
D_MODEL = 1024
DEPTH = 2
ATTN_WIDTH = 512
HEAD_DIM = 64
HGRN_HEADS = 4
HGRN_DIM = 128
HGRN_WIDTH = 512
IN_W = 3584
MLP_HIDDEN = 4096
N_CHIPS = 4
N_DEV = 8
SHARD_IN = IN_W // N_CHIPS
SHARD_OUT = D_MODEL // N_CHIPS
SHARD_MLP = MLP_HIDDEN // N_CHIPS
DILATIONS = (1, 4, 16)
SPAN = 128
ROPE_THETA = 10000.0
NORM_EPS = 1e-6
MASK_VALUE = -1e30
CHUNK = 128
ROW_TILE = 512
MM_TILE = 1024
VMEM_LIMIT = 52 * 1024 * 1024

ADAM_LR = 0.001
ADAM_B1 = 0.9
ADAM_B2 = 0.999
ADAM_EPS = 1e-08
ADAM_WD = 0.01
ADAM_STEP = 10

PACK_ROWS = 16


def _params(n_axes):
    return pltpu.CompilerParams(dimension_semantics=("arbitrary",) * n_axes,
                                vmem_limit_bytes=VMEM_LIMIT)


def _dot(a, b):
    return jnp.dot(a.astype(BF16), b.astype(BF16), preferred_element_type=F32)


def _dot_nt(a, b):
    return lax.dot_general(a.astype(BF16), b.astype(BF16), (((1,), (1,)), ((), ())),
                           preferred_element_type=F32)


def _sigmoid(x):
    return 1.0 / (1.0 + jnp.exp(-x))


def _matmul(name, a, b, a_spec, b_spec, out_shape, out_dtype, out_spec, grid, acc_shape,
            nt=False, extra=None, extra_spec=None, epilogue="none"):
    nk = grid[2]

    def body(*refs):
        if extra is None:
            a_ref, b_ref, o_ref, acc = refs
            e_ref = None
        else:
            a_ref, b_ref, e_ref, o_ref, acc = refs
        kk = pl.program_id(2)

        @pl.when(kk == 0)
        def _():
            acc[...] = jnp.zeros_like(acc)

        if nt:
            acc[...] += _dot_nt(a_ref[...], b_ref[...])
        else:
            acc[...] += _dot(a_ref[...], b_ref[...])

        @pl.when(kk == nk - 1)
        def _():
            r = acc[...]
            if epilogue == "add":
                r = r + e_ref[...]
            elif epilogue == "relu2_grad":
                r = r * (2.0 * jnp.maximum(e_ref[...], 0.0))
            o_ref[...] = r.astype(o_ref.dtype)

    in_specs = [a_spec, b_spec] + ([] if extra is None else [extra_spec])
    args = (a, b) + (() if extra is None else (extra,))
    return pl.pallas_call(
        body, name=name, grid=grid, in_specs=in_specs, out_specs=out_spec,
        out_shape=jax.ShapeDtypeStruct(out_shape, out_dtype),
        scratch_shapes=[pltpu.VMEM(acc_shape, F32)],
        compiler_params=_params(3),
    )(*args)


def _spec(shape, index_map):
    return pl.BlockSpec(shape, index_map)


def _rms_fwd(name, x, gain):
    s, d = x.shape
    t = ROW_TILE

    def body(x_ref, g_ref, h_ref, ht_ref):
        xv = x_ref[...]
        r = lax.rsqrt(jnp.mean(xv * xv, axis=1, keepdims=True) + NORM_EPS)
        h = xv * r * g_ref[...]
        h_ref[...] = h.astype(BF16)
        ht_ref[...] = h.T.astype(BF16)

    return pl.pallas_call(
        body, name=name, grid=(s // t,),
        in_specs=[_spec((t, d), lambda i: (i, 0)), _spec((1, d), lambda i: (0, 0))],
        out_specs=[_spec((t, d), lambda i: (i, 0)), _spec((d, t), lambda i: (0, i))],
        out_shape=[jax.ShapeDtypeStruct((s, d), BF16), jax.ShapeDtypeStruct((d, s), BF16)],
        compiler_params=_params(1),
    )(x, gain)


def _rms_bwd(name, dh, x, gain, dres):
    s, d = x.shape
    t = ROW_TILE

    def body(dh_ref, x_ref, g_ref, dres_ref, dx_ref, dg_ref):
        @pl.when(pl.program_id(0) == 0)
        def _():
            dg_ref[...] = jnp.zeros_like(dg_ref)

        xv = x_ref[...]
        dhv = dh_ref[...]
        r = lax.rsqrt(jnp.mean(xv * xv, axis=1, keepdims=True) + NORM_EPS)
        xhat = xv * r
        dhg = dhv * g_ref[...]
        proj = jnp.mean(dhg * xhat, axis=1, keepdims=True)
        dx_ref[...] = dres_ref[...] + r * (dhg - xhat * proj)
        dg_ref[...] += jnp.sum(dhv * xhat, axis=0, keepdims=True)

    return pl.pallas_call(
        body, name=name, grid=(s // t,),
        in_specs=[_spec((t, d), lambda i: (i, 0)), _spec((t, d), lambda i: (i, 0)),
                  _spec((1, d), lambda i: (0, 0)), _spec((t, d), lambda i: (i, 0))],
        out_specs=[_spec((t, d), lambda i: (i, 0)), _spec((1, d), lambda i: (0, 0))],
        out_shape=[jax.ShapeDtypeStruct((s, d), F32), jax.ShapeDtypeStruct((1, d), F32)],
        compiler_params=_params(1),
    )(dh, x, gain, dres)


def _loss_head(x, gain, target):
    s, d = x.shape
    t = ROW_TILE
    n_steps = s // t

    def body(x_ref, g_ref, t_ref, dx_ref, dg_ref, loss_ref, acc):
        i = pl.program_id(0)

        @pl.when(i == 0)
        def _():
            dg_ref[...] = jnp.zeros_like(dg_ref)
            acc[...] = jnp.zeros_like(acc)

        xv = x_ref[...]
        g = g_ref[...]
        r = lax.rsqrt(jnp.mean(xv * xv, axis=1, keepdims=True) + NORM_EPS)
        xhat = xv * r
        err = xhat * g - t_ref[...]
        acc[...] += jnp.sum(err * err, axis=0, keepdims=True)
        dy = err * (1.0 / d)
        dyg = dy * g
        proj = jnp.mean(dyg * xhat, axis=1, keepdims=True)
        dx_ref[...] = r * (dyg - xhat * proj)
        dg_ref[...] += jnp.sum(dy * xhat, axis=0, keepdims=True)

        @pl.when(i == n_steps - 1)
        def _():
            total = jnp.sum(acc[...], axis=1, keepdims=True) * (0.5 / d)
            loss_ref[...] = jnp.broadcast_to(total, loss_ref.shape)

    return pl.pallas_call(
        body, name="loss_head", grid=(n_steps,),
        in_specs=[_spec((t, d), lambda i: (i, 0)), _spec((1, d), lambda i: (0, 0)),
                  _spec((t, d), lambda i: (i, 0))],
        out_specs=[_spec((t, d), lambda i: (i, 0)), _spec((1, d), lambda i: (0, 0)),
                   _spec((1, 128), lambda i: (0, 0))],
        out_shape=[jax.ShapeDtypeStruct((s, d), F32), jax.ShapeDtypeStruct((1, d), F32),
                   jax.ShapeDtypeStruct((1, 128), F32)],
        scratch_shapes=[pltpu.VMEM((1, d), F32)],
        compiler_params=_params(1),
    )(x, gain, target)


def _relu2(name, u):
    s, f = u.shape
    t = ROW_TILE

    def body(u_ref, a_ref, at_ref):
        r = jnp.maximum(u_ref[...], 0.0)
        a = r * r
        a_ref[...] = a.astype(BF16)
        at_ref[...] = a.T.astype(BF16)

    return pl.pallas_call(
        body, name=name, grid=(s // t, f // t),
        in_specs=[_spec((t, t), lambda i, j: (i, j))],
        out_specs=[_spec((t, t), lambda i, j: (i, j)), _spec((t, t), lambda i, j: (j, i))],
        out_shape=[jax.ShapeDtypeStruct((s, f), BF16), jax.ShapeDtypeStruct((f, s), BF16)],
        compiler_params=_params(2),
    )(u)


def _rope_tables():
    half = HEAD_DIM // 2
    inv_freq = ROPE_THETA ** (-jnp.arange(half, dtype=F32) / half)
    ang = jnp.arange(SEQ, dtype=jnp.int32).astype(F32)[:, None] * inv_freq[None, :]
    cos, sin = jnp.cos(ang), jnp.sin(ang)
    cos_t = jnp.concatenate([cos, cos, cos, cos], axis=1)
    sin_t = jnp.concatenate([-sin, sin, -sin, sin], axis=1)
    return cos_t, sin_t


def _swap_halves(x):
    lane = lax.broadcasted_iota(jnp.int32, x.shape, 1)
    first = (lane % HEAD_DIM) < (HEAD_DIM // 2)
    return jnp.where(first, pltpu.roll(x, 128 - HEAD_DIM // 2, 1), pltpu.roll(x, HEAD_DIM // 2, 1))


def _permuted_specs(t, width):
    specs = [_spec((t, width), lambda i: (i, 0))]
    for d in DILATIONS[1:]:
        specs.append(_spec((d, t // d, width), lambda i: (0, i, 0)))
    return specs


def _permuted_shapes(width, dtype):
    shapes = [jax.ShapeDtypeStruct((SEQ, width), dtype)]
    for d in DILATIONS[1:]:
        shapes.append(jax.ShapeDtypeStruct((d, SEQ // d, width), dtype))
    return shapes


def _attn_prep(name, proj, cos_t, sin_t):
    t = ROW_TILE
    w = ATTN_WIDTH

    def body(q_ref, k_ref, v_ref, cos_ref, sin_ref, *rest):
        outs, scr = rest[:9], rest[9]
        cosv, sinv = cos_ref[...], sin_ref[...]
        for a, (src, roped, scale) in enumerate(((q_ref, True, HEAD_DIM ** -0.5),
                                                 (k_ref, True, 1.0), (v_ref, False, 1.0))):
            o1, o4, o16 = outs[3 * a:3 * a + 3]
            for cb in range(w // 128):
                cols = slice(cb * 128, (cb + 1) * 128)
                val = src[:, cols]
                if roped:
                    val = (val * cosv + _swap_halves(val) * sinv) * scale
                scr[...] = val
                o1[:, cols] = val.astype(BF16)
                for o_ref, d in ((o4, 4), (o16, 16)):
                    for r in range(d):
                        o_ref[r, :, cols] = scr[pl.ds(r, t // d, stride=d), :].astype(BF16)

    out_specs = _permuted_specs(t, w) * 3
    out_shape = _permuted_shapes(w, BF16) * 3
    outs = pl.pallas_call(
        body, name=name, grid=(SEQ // t,),
        in_specs=[_spec((t, w), lambda i: (i, 0)), _spec((t, w), lambda i: (i, 1)),
                  _spec((t, w), lambda i: (i, 2)),
                  _spec((t, 128), lambda i: (i, 0)), _spec((t, 128), lambda i: (i, 0))],
        out_specs=out_specs, out_shape=out_shape,
        scratch_shapes=[pltpu.VMEM((t, 128), F32)],
        compiler_params=_params(1),
    )(proj, proj, proj, cos_t, sin_t)
    q, k, v = outs[0:3], outs[3:6], outs[6:9]
    flat = lambda arr: arr.reshape(SEQ, w)
    return [(flat(q[p]), flat(k[p]), flat(v[p])) for p in range(3)]


def _band_masks():
    row = lax.broadcasted_iota(jnp.int32, (SPAN, SPAN), 0)
    col = lax.broadcasted_iota(jnp.int32, (SPAN, SPAN), 1)
    return col >= row, col <= row, col < HEAD_DIM


def _attn_fwd(name, q, k, v, seg_blocks):
    n_blocks = SEQ // SPAN

    def body(q_ref, k_ref, v_ref, o_ref, lse_ref):
        prev_band, cur_ok, head0 = _band_masks()

        def step(b, carry):
            r0 = pl.multiple_of(b * SPAN, SPAN)
            p0 = pl.multiple_of(jnp.maximum(b - 1, 0) * SPAN, SPAN)
            qb = q_ref[pl.ds(r0, SPAN), :]
            kc, vc = k_ref[pl.ds(r0, SPAN), :], v_ref[pl.ds(r0, SPAN), :]
            kp, vp = k_ref[pl.ds(p0, SPAN), :], v_ref[pl.ds(p0, SPAN), :]
            prev_ok = prev_band & ((b % seg_blocks) != 0)
            o_acc = jnp.zeros((SPAN, 128), F32)
            lse_acc = jnp.zeros((SPAN, 128), F32)
            for mh in (head0, ~head0):
                qh = jnp.where(mh, qb, jnp.zeros_like(qb))
                sp = jnp.where(prev_ok, _dot_nt(qh, kp), MASK_VALUE)
                sc = jnp.where(cur_ok, _dot_nt(qh, kc), MASK_VALUE)
                m = jnp.maximum(jnp.max(sp, axis=1, keepdims=True), jnp.max(sc, axis=1, keepdims=True))
                pp = jnp.where(prev_ok, jnp.exp(sp - m), 0.0)
                pc = jnp.where(cur_ok, jnp.exp(sc - m), 0.0)
                l = jnp.sum(pp, axis=1, keepdims=True) + jnp.sum(pc, axis=1, keepdims=True)
                vph = jnp.where(mh, vp, jnp.zeros_like(vp))
                vch = jnp.where(mh, vc, jnp.zeros_like(vc))
                o_acc = o_acc + (_dot(pp, vph) + _dot(pc, vch)) / l
                lse_acc = jnp.where(mh, m + jnp.log(l), lse_acc)
            o_ref[pl.ds(r0, SPAN), :] = o_acc
            lse_ref[pl.ds(r0, SPAN), :] = lse_acc
            return carry

        lax.fori_loop(0, n_blocks, step, 0)

    col = _spec((SEQ, 128), lambda j: (0, j))
    return pl.pallas_call(
        body, name=name, grid=(ATTN_WIDTH // 128,),
        in_specs=[col, col, col], out_specs=[col, col],
        out_shape=[jax.ShapeDtypeStruct((SEQ, ATTN_WIDTH), F32)] * 2,
        compiler_params=_params(1),
    )(q, k, v)


def _unpermute(dst, src_ref, d, cols):
    n = dst.shape[0] // d
    for r in range(d):
        dst[pl.ds(r, n, stride=d), :] = src_ref[r, :, cols]


def _attn_merge(name, outs, lses, gain):
    t = ROW_TILE
    w = ATTN_WIDTH

    def body(o1, o4, o16, l1, l4, l16, g_ref, an_ref, ant_ref, attn_ref, lse_ref, so4, so16, sl4, sl16):
        for cb in range(w // 128):
            cols = slice(cb * 128, (cb + 1) * 128)
            _unpermute(so4, o4, 4, cols)
            _unpermute(so16, o16, 16, cols)
            _unpermute(sl4, l4, 4, cols)
            _unpermute(sl16, l16, 16, cols)
            la, lb, lc = l1[:, cols], sl4[...], sl16[...]
            m = jnp.maximum(jnp.maximum(la, lb), lc)
            ea, eb, ec = jnp.exp(la - m), jnp.exp(lb - m), jnp.exp(lc - m)
            tot = ea + eb + ec
            attn_ref[:, cols] = (ea * o1[:, cols] + eb * so4[...] + ec * so16[...]) / tot
            lse_ref[:, cols] = m + jnp.log(tot)
        attn = attn_ref[...]
        r = lax.rsqrt(jnp.mean(attn * attn, axis=1, keepdims=True) + NORM_EPS)
        an = attn * r * g_ref[...]
        an_ref[...] = an.astype(BF16)
        ant_ref[...] = an.T.astype(BF16)

    views = lambda arrs: [arrs[0], arrs[1].reshape(4, SEQ // 4, w), arrs[2].reshape(16, SEQ // 16, w)]
    row = _spec((t, w), lambda i: (i, 0))
    return pl.pallas_call(
        body, name=name, grid=(SEQ // t,),
        in_specs=_permuted_specs(t, w) * 2 + [_spec((1, w), lambda i: (0, 0))],
        out_specs=[row, _spec((w, t), lambda i: (0, i)), row, row],
        out_shape=[jax.ShapeDtypeStruct((SEQ, w), BF16), jax.ShapeDtypeStruct((w, SEQ), BF16),
                   jax.ShapeDtypeStruct((SEQ, w), F32), jax.ShapeDtypeStruct((SEQ, w), F32)],
        scratch_shapes=[pltpu.VMEM((t, 128), F32)] * 4,
        compiler_params=_params(1),
    )(*views(outs), *views(lses), gain)


def _head_sum_matrix():
    i = np.arange(ATTN_WIDTH)
    return jnp.asarray((i[:, None] // HEAD_DIM) == (i[None, :] // HEAD_DIM), dtype=F32)


def _attn_bwd_prep(name, d_an, attn, lse, gain, head_sum):
    t = ROW_TILE
    w = ATTN_WIDTH

    def body(dan_ref, attn_ref, lse_ref, g_ref, hs_ref, *rest):
        (do1, do4, do16, dl1, dl4, dl16, ls4, ls16, dg_ref), (sdo, sdl, sls) = rest[:9], rest[9:]

        @pl.when(pl.program_id(0) == 0)
        def _():
            dg_ref[...] = jnp.zeros_like(dg_ref)

        attn = attn_ref[...]
        dan = dan_ref[...]
        r = lax.rsqrt(jnp.mean(attn * attn, axis=1, keepdims=True) + NORM_EPS)
        xhat = attn * r
        dg_ref[...] += jnp.sum(dan * xhat, axis=0, keepdims=True)
        dang = dan * g_ref[...]
        d_o = r * (dang - xhat * jnp.mean(dang * xhat, axis=1, keepdims=True))
        delta = jnp.dot(d_o * attn, hs_ref[...], preferred_element_type=F32,
                        precision=lax.Precision.HIGHEST)
        do1[...] = d_o.astype(BF16)
        dl1[...] = delta
        for cb in range(w // 128):
            cols = slice(cb * 128, (cb + 1) * 128)
            sdo[...] = d_o[:, cols]
            sdl[...] = delta[:, cols]
            sls[...] = lse_ref[:, cols]
            for d, o_do, o_dl, o_ls in ((4, do4, dl4, ls4), (16, do16, dl16, ls16)):
                for rr in range(d):
                    rows = pl.ds(rr, t // d, stride=d)
                    o_do[rr, :, cols] = sdo[rows, :].astype(BF16)
                    o_dl[rr, :, cols] = sdl[rows, :]
                    o_ls[rr, :, cols] = sls[rows, :]

    row = _spec((t, w), lambda i: (i, 0))
    perm = _permuted_specs(t, w)
    outs = pl.pallas_call(
        body, name=name, grid=(SEQ // t,),
        in_specs=[row, row, row, _spec((1, w), lambda i: (0, 0)), _spec((w, w), lambda i: (0, 0))],
        out_specs=perm + perm + perm[1:] + [_spec((1, w), lambda i: (0, 0))],
        out_shape=(_permuted_shapes(w, BF16) + _permuted_shapes(w, F32) + _permuted_shapes(w, F32)[1:]
                   + [jax.ShapeDtypeStruct((1, w), F32)]),
        scratch_shapes=[pltpu.VMEM((t, 128), F32)] * 3,
        compiler_params=_params(1),
    )(d_an, attn, lse, gain, head_sum)
    flat = lambda arr: arr.reshape(SEQ, w)
    d_out = [flat(a) for a in outs[0:3]]
    delta = [flat(a) for a in outs[3:6]]
    lses = [lse, flat(outs[6]), flat(outs[7])]
    return d_out, delta, lses, outs[8]


def _attn_bwd(name, q, k, v, d_out, delta, lse, seg_blocks):
    n_blocks = SEQ // SPAN

    def body(q_ref, k_ref, v_ref, do_ref, dl_ref, lse_ref, dq_ref, dk_ref, dv_ref):
        prev_band, cur_ok, head0 = _band_masks()
        dk_ref[...] = jnp.zeros_like(dk_ref)
        dv_ref[...] = jnp.zeros_like(dv_ref)

        def step(b, carry):
            r0 = pl.multiple_of(b * SPAN, SPAN)
            p0 = pl.multiple_of(jnp.maximum(b - 1, 0) * SPAN, SPAN)
            cur, prev = pl.ds(r0, SPAN), pl.ds(p0, SPAN)
            qb, dob = q_ref[cur, :], do_ref[cur, :]
            kc, vc, kp, vp = k_ref[cur, :], v_ref[cur, :], k_ref[prev, :], v_ref[prev, :]
            lse_b, dl_b = lse_ref[cur, :], dl_ref[cur, :]
            prev_ok = prev_band & ((b % seg_blocks) != 0)
            dq = jnp.zeros((SPAN, 128), F32)
            dkp = jnp.zeros((SPAN, 128), F32)
            dkc = jnp.zeros((SPAN, 128), F32)
            dvp = jnp.zeros((SPAN, 128), F32)
            dvc = jnp.zeros((SPAN, 128), F32)
            for h, mh in enumerate((head0, ~head0)):
                lane = h * HEAD_DIM
                lse_h, dl_h = lse_b[:, lane:lane + 1], dl_b[:, lane:lane + 1]
                qh = jnp.where(mh, qb, jnp.zeros_like(qb))
                doh = jnp.where(mh, dob, jnp.zeros_like(dob))
                pp = jnp.where(prev_ok, jnp.exp(_dot_nt(qh, kp) - lse_h), 0.0)
                pc = jnp.where(cur_ok, jnp.exp(_dot_nt(qh, kc) - lse_h), 0.0)
                dsp = pp * (_dot_nt(doh, vp) - dl_h)
                dsc = pc * (_dot_nt(doh, vc) - dl_h)
                dq = dq + jnp.where(mh, _dot(dsp, kp) + _dot(dsc, kc), 0.0)
                dkp = dkp + _dot(dsp.T, qh)
                dkc = dkc + _dot(dsc.T, qh)
                dvp = dvp + _dot(pp.T, doh)
                dvc = dvc + _dot(pc.T, doh)
            dq_ref[cur, :] = dq
            dk_ref[prev, :] += dkp
            dv_ref[prev, :] += dvp
            dk_ref[cur, :] += dkc
            dv_ref[cur, :] += dvc
            return carry

        lax.fori_loop(0, n_blocks, step, 0)

    col = _spec((SEQ, 128), lambda j: (0, j))
    return pl.pallas_call(
        body, name=name, grid=(ATTN_WIDTH // 128,),
        in_specs=[col] * 6, out_specs=[col] * 3,
        out_shape=[jax.ShapeDtypeStruct((SEQ, ATTN_WIDTH), F32)] * 3,
        compiler_params=_params(1),
    )(q, k, v, d_out, delta, lse)


def _attn_bwd_post(name, grads, cos_t, sin_t):
    t = ROW_TILE
    w = ATTN_WIDTH

    def body(*refs):
        ins, cos_ref, sin_ref, out_ref, s4, s16 = refs[:9], refs[9], refs[10], refs[11], refs[12], refs[13]
        cosv, sinv = cos_ref[...], sin_ref[...]
        for a in range(3):
            g1, g4, g16 = ins[a], ins[3 + a], ins[6 + a]
            for cb in range(w // 128):
                cols = slice(cb * 128, (cb + 1) * 128)
                _unpermute(s4, g4, 4, cols)
                _unpermute(s16, g16, 16, cols)
                val = g1[:, cols] + s4[...] + s16[...]
                if a < 2:
                    val = val * cosv + _swap_halves(val * sinv)
                if a == 0:
                    val = val * (HEAD_DIM ** -0.5)
                out_ref[:, a * w + cb * 128:a * w + (cb + 1) * 128] = val

    views = []
    for p, d in enumerate(DILATIONS):
        for a in range(3):
            views.append(grads[p][a] if d == 1 else grads[p][a].reshape(d, SEQ // d, w))
    perm = _permuted_specs(t, w)
    in_specs = [perm[0]] * 3 + [perm[1]] * 3 + [perm[2]] * 3
    return pl.pallas_call(
        body, name=name, grid=(SEQ // t,),
        in_specs=in_specs + [_spec((t, 128), lambda i: (i, 0))] * 2,
        out_specs=_spec((t, 3 * w), lambda i: (i, 0)),
        out_shape=jax.ShapeDtypeStruct((SEQ, 3 * w), F32),
        scratch_shapes=[pltpu.VMEM((t, 128), F32)] * 2,
        compiler_params=_params(1),
    )(*views, cos_t, sin_t)


N_LEVELS = 7


def _hgrn_consts():
    c = CHUNK
    i = np.arange(c)[:, None]
    s = np.arange(c)[None, :]
    blocks = [s <= i]
    for lv in range(N_LEVELS):
        bs = c >> lv
        h = bs // 2
        m = (i // bs) * bs + h - 1
        second = (i % bs) >= h
        blocks.append((second & (s > m) & (s <= i)) | (~second & (s > i) & (s <= m)))
    blocks.append(s > i)
    stack = np.concatenate(blocks, axis=0).astype(np.float32)
    return jnp.asarray(stack, dtype=BF16), jnp.asarray(stack.T, dtype=BF16)


def _exact_dot(m01, x):
    hi = x.astype(BF16)
    r1 = x - hi.astype(F32)
    mid = r1.astype(BF16)
    lo = (r1 - mid.astype(F32)).astype(BF16)
    n = x.shape[1]
    full = jnp.dot(m01, jnp.concatenate([hi, mid, lo], axis=1), preferred_element_type=F32)
    return (full[:, :n] + full[:, n:2 * n]) + full[:, 2 * n:]


def _hgrn_gates(qh, z, lb):
    sq = _sigmoid(qh)
    q = qh * sq * (HGRN_DIM ** -0.5)
    sig = _sigmoid(z)
    sigm = _sigmoid(-z)
    f = lb + (1.0 - lb) * sig
    k = (1.0 - lb) * sigm
    return q, k, f, sq, sig, sigm


def _level_masks(lv):
    row = lax.broadcasted_iota(jnp.int32, (CHUNK, CHUNK), 0)
    col = lax.broadcasted_iota(jnp.int32, (CHUNK, CHUNK), 1)
    shift = N_LEVELS - lv
    second = (row & (CHUNK >> (lv + 1))) != 0
    same = (row >> shift) == (col >> shift)
    return second, same


def _hgrn_fwd(name, proj, lb, gain, stack):
    t = ROW_TILE
    per = t // CHUNK
    n_rb = SEQ // t
    n_chunks = SEQ // CHUNK
    col0 = 3 * ATTN_WIDTH // 128

    def body(q_ref, f_ref, i_ref, g_ref, lb_ref, gain_ref, stack_ref,
             rec_ref, rect_ref, o_ref, st_out, a_out, st):
        @pl.when(pl.program_id(1) == 0)
        def _():
            st[...] = jnp.zeros_like(st)

        lbv = lb_ref[...]
        row = lax.broadcasted_iota(jnp.int32, (CHUNK, CHUNK), 0)
        col = lax.broadcasted_iota(jnp.int32, (CHUNK, CHUNK), 1)
        for c in range(per):
            rows = slice(c * CHUNK, (c + 1) * CHUNK)
            qh, z, v, gh = q_ref[rows, :], f_ref[rows, :], i_ref[rows, :], g_ref[rows, :]
            q, k, f, _, _, _ = _hgrn_gates(qh, z, lbv)
            dec = _exact_dot(stack_ref[...], jnp.log(f))
            g = dec[0:CHUNK]
            to_end = dec[(N_LEVELS + 1) * CHUNK:(N_LEVELS + 2) * CHUNK]
            a = jnp.where(row == col, jnp.sum(q * k, axis=1, keepdims=True), 0.0)
            for lv in range(N_LEVELS):
                e = jnp.exp(dec[(lv + 1) * CHUNK:(lv + 2) * CHUNK])
                second, same = _level_masks(lv)
                qt = jnp.where(second, q * e, 0.0)
                kt = jnp.where(second, 0.0, k * e)
                a = a + jnp.where(same, _dot_nt(qt, kt), 0.0)
            st_prev = st[...]
            st_out[c] = st_prev
            a_out[c] = a
            o = _dot(a, v) + _dot_nt(q * jnp.exp(g), st_prev)
            k_end = k * jnp.exp(to_end)
            st[...] = st_prev * jnp.exp(g[CHUNK - 1:CHUNK, :]) + _dot(v.T, k_end)
            o_ref[rows, :] = o
            r = lax.rsqrt(jnp.mean(o * o, axis=1, keepdims=True) + NORM_EPS)
            rec = o * r * gain_ref[...] * (gh * _sigmoid(gh))
            rec_ref[rows, :] = rec.astype(BF16)
            rect_ref[:, rows] = rec.T.astype(BF16)

    def col_spec(tt):
        return _spec((t, HGRN_DIM), lambda h, rb: (rb, col0 + HGRN_HEADS * tt + h))

    chunk_spec = _spec((None, per, CHUNK, CHUNK), lambda h, rb: (h, rb, 0, 0))
    return pl.pallas_call(
        body, name=name, grid=(HGRN_HEADS, n_rb),
        in_specs=[col_spec(0), col_spec(1), col_spec(2), col_spec(3),
                  _spec((None, 1, HGRN_DIM), lambda h, rb: (h, 0, 0)),
                  _spec((1, HGRN_DIM), lambda h, rb: (0, 0)),
                  _spec(stack.shape, lambda h, rb: (0, 0))],
        out_specs=[_spec((t, HGRN_DIM), lambda h, rb: (rb, h)),
                   _spec((HGRN_DIM, t), lambda h, rb: (h, rb)),
                   _spec((t, HGRN_DIM), lambda h, rb: (rb, h)),
                   chunk_spec, chunk_spec],
        out_shape=[jax.ShapeDtypeStruct((SEQ, HGRN_WIDTH), BF16),
                   jax.ShapeDtypeStruct((HGRN_WIDTH, SEQ), BF16),
                   jax.ShapeDtypeStruct((SEQ, HGRN_WIDTH), F32),
                   jax.ShapeDtypeStruct((HGRN_HEADS, n_chunks, CHUNK, CHUNK), F32),
                   jax.ShapeDtypeStruct((HGRN_HEADS, n_chunks, CHUNK, CHUNK), F32)],
        scratch_shapes=[pltpu.VMEM((CHUNK, CHUNK), F32)],
        compiler_params=_params(2),
    )(proj, proj, proj, proj, lb, gain, stack)


def _hgrn_bwd(name, proj, d_rec, o_pre, states, scores, lb, gain, stack, stack_t):
    t = ROW_TILE
    per = t // CHUNK
    n_rb = SEQ // t
    col0 = 3 * ATTN_WIDTH // 128

    def body(q_ref, f_ref, i_ref, g_ref, drec_ref, o_ref, st_ref, a_ref, lb_ref, gain_ref,
             stack_ref, stack_t_ref, dq_ref, df_ref, di_ref, dg_ref, dlb_ref, dgain_ref, dst):
        @pl.when(pl.program_id(1) == 0)
        def _():
            dst[...] = jnp.zeros_like(dst)
            dlb_ref[...] = jnp.zeros_like(dlb_ref)
            dgain_ref[...] = jnp.zeros_like(dgain_ref)

        lbv = lb_ref[...]
        gain_v = gain_ref[...]
        row = lax.broadcasted_iota(jnp.int32, (CHUNK, CHUNK), 0)
        col = lax.broadcasted_iota(jnp.int32, (CHUNK, CHUNK), 1)
        for c in reversed(range(per)):
            rows = slice(c * CHUNK, (c + 1) * CHUNK)
            qh, z, v, gh = q_ref[rows, :], f_ref[rows, :], i_ref[rows, :], g_ref[rows, :]
            q, k, f, sq, sig, sigm = _hgrn_gates(qh, z, lbv)
            dec = _exact_dot(stack_ref[...], jnp.log(f))
            g = dec[0:CHUNK]
            to_end = dec[(N_LEVELS + 1) * CHUNK:(N_LEVELS + 2) * CHUNK]
            e_g = jnp.exp(g)
            e_end = jnp.exp(to_end)
            e_last = jnp.exp(g[CHUNK - 1:CHUNK, :])
            q_in = q * e_g
            k_end = k * e_end
            st_prev = st_ref[c]
            a = a_ref[c]
            dst_new = dst[...]

            o = o_ref[rows, :]
            drec = drec_ref[rows, :]
            sg = _sigmoid(gh)
            r = lax.rsqrt(jnp.mean(o * o, axis=1, keepdims=True) + NORM_EPS)
            ohat = o * r
            d_gh = drec * (ohat * gain_v) * (sg * (1.0 + gh * (1.0 - sg)))
            d_on = drec * (gh * sg)
            dgain_ref[...] += jnp.sum(d_on * ohat, axis=0, keepdims=True)
            d_ohat = d_on * gain_v
            d_o = r * (d_ohat - ohat * jnp.mean(d_ohat * ohat, axis=1, keepdims=True))

            d_a = jnp.where(row >= col, _dot_nt(d_o, v), 0.0)
            d_at = jnp.where(col >= row, _dot_nt(v, d_o), 0.0)
            d_v = _dot(a.T, d_o) + _dot_nt(k_end, dst_new)
            d_q_in = _dot(d_o, st_prev)
            d_k_end = _dot(v, dst_new)
            d_q = d_q_in * e_g
            d_k = d_k_end * e_end
            diag = jnp.sum(d_o * v, axis=1, keepdims=True)
            d_q = d_q + diag * k
            d_k = d_k + diag * q
            d_dec = [q_in * d_q_in]
            for lv in range(N_LEVELS):
                e = jnp.exp(dec[(lv + 1) * CHUNK:(lv + 2) * CHUNK])
                second, same = _level_masks(lv)
                qt = jnp.where(second, q * e, 0.0)
                kt = jnp.where(second, 0.0, k * e)
                d_qt = _dot(jnp.where(same, d_a, 0.0), kt)
                d_kt = _dot(jnp.where(same, d_at, 0.0), qt)
                d_q = d_q + jnp.where(second, d_qt * e, 0.0)
                d_k = d_k + jnp.where(second, 0.0, d_kt * e)
                d_dec.append(jnp.where(second, qt * d_qt, kt * d_kt))
            d_dec.append(k_end * d_k_end)
            flux = jnp.sum(dst_new * st_prev, axis=0, keepdims=True) * e_last
            d_lf = _exact_dot(stack_t_ref[...], jnp.concatenate(d_dec, axis=0)) + flux
            dst[...] = dst_new * e_last + _dot(d_o.T, q_in)

            d_f = d_lf / f - d_k
            dlb_ref[...] += jnp.sum(d_f * sigm, axis=0, keepdims=True)
            dq_ref[rows, :] = d_q * (HGRN_DIM ** -0.5) * (sq * (1.0 + qh * (1.0 - sq)))
            df_ref[rows, :] = d_f * (1.0 - lbv) * sig * sigm
            di_ref[rows, :] = d_v
            dg_ref[rows, :] = d_gh

    last = n_rb - 1

    def col_spec(tt):
        return _spec((t, HGRN_DIM), lambda h, rb: (last - rb, col0 + HGRN_HEADS * tt + h))

    head_col = _spec((t, HGRN_DIM), lambda h, rb: (last - rb, h))
    chunk_spec = _spec((None, per, CHUNK, CHUNK), lambda h, rb: (h, last - rb, 0, 0))
    vec_spec = _spec((None, 1, HGRN_DIM), lambda h, rb: (h, 0, 0))
    outs = pl.pallas_call(
        body, name=name, grid=(HGRN_HEADS, n_rb),
        in_specs=[col_spec(0), col_spec(1), col_spec(2), col_spec(3), head_col, head_col,
                  chunk_spec, chunk_spec, vec_spec,
                  _spec((1, HGRN_DIM), lambda h, rb: (0, 0)),
                  _spec(stack.shape, lambda h, rb: (0, 0)), _spec(stack_t.shape, lambda h, rb: (0, 0))],
        out_specs=[head_col] * 4 + [vec_spec, vec_spec],
        out_shape=[jax.ShapeDtypeStruct((SEQ, HGRN_WIDTH), F32)] * 4
                  + [jax.ShapeDtypeStruct((HGRN_HEADS, 1, HGRN_DIM), F32)] * 2,
        scratch_shapes=[pltpu.VMEM((CHUNK, CHUNK), F32)],
        compiler_params=_params(2),
    )(proj, proj, proj, proj, d_rec, o_pre, states, scores, lb, gain, stack, stack_t)
    return outs


ANY_SPEC = pl.BlockSpec(memory_space=pl.ANY)


def _my_place():
    return lax.axis_index("x"), lax.axis_index("y"), lax.axis_index("c")


def _other_chips(x, y):
    return [(1 - x, y), (x, 1 - y), (1 - x, 1 - y)]


def _remote(src, dst, send_sem, recv_sem, device):
    return pltpu.make_async_remote_copy(src_ref=src, dst_ref=dst, send_sem=send_sem, recv_sem=recv_sem,
                                        device_id=device, device_id_type=MESH)


def _gather_weights(shards):
    n = len(shards)

    def body(*refs):
        ins, outs = refs[:n], refs[n:2 * n]
        send_sems, recv_sems, local_sems = refs[2 * n:]
        x, y, c = _my_place()
        me = 2 * x + y
        chips = _other_chips(x, y)
        local = [pltpu.make_async_copy(ins[i], outs[i].at[me], local_sems.at[i]) for i in range(n)]
        for cp in local:
            cp.start()
        sends = []
        for i in range(n):
            for j, (px, py) in enumerate(chips):
                sends.append(_remote(ins[i], outs[i].at[me], send_sems.at[3 * i + j],
                                     recv_sems.at[3 * i + j], (px, py, c)))
        for cp in sends:
            cp.start()
        for i in range(n):
            for j, (px, py) in enumerate(chips):
                _remote(ins[i], outs[i].at[2 * px + py], send_sems.at[3 * i + j],
                        recv_sems.at[3 * i + j], (px, py, c)).wait_recv()
        for cp in sends:
            cp.wait_send()
        for cp in local:
            cp.wait()

    return pl.pallas_call(
        body, name="gather_weights", in_specs=[ANY_SPEC] * n, out_specs=[ANY_SPEC] * n,
        out_shape=[jax.ShapeDtypeStruct((N_CHIPS,) + s.shape, s.dtype) for s in shards],
        scratch_shapes=[pltpu.SemaphoreType.DMA((3 * n,)), pltpu.SemaphoreType.DMA((3 * n,)),
                        pltpu.SemaphoreType.DMA((n,))],
    )(*shards)


def _exchange_halves(grads):
    n = len(grads)

    def body(*refs):
        ins, outs = refs[:n], refs[n:2 * n]
        send_sems, recv_sems = refs[2 * n:]
        x, y, c = _my_place()
        copies = []
        for i in range(n):
            h = ins[i].shape[1] // 2
            copies.append(_remote(ins[i].at[:, pl.ds((1 - c) * h, h), :], outs[i],
                                  send_sems.at[i], recv_sems.at[i], (x, y, 1 - c)))
        for cp in copies:
            cp.start()
        for cp in copies:
            cp.wait()

    return pl.pallas_call(
        body, name="exchange_halves", in_specs=[ANY_SPEC] * n, out_specs=[ANY_SPEC] * n,
        out_shape=[jax.ShapeDtypeStruct((g.shape[0], g.shape[1] // 2, g.shape[2]), g.dtype) for g in grads],
        scratch_shapes=[pltpu.SemaphoreType.DMA((n,)), pltpu.SemaphoreType.DMA((n,))],
    )(*grads)


def _add_own_half(name, g, received, core):
    n_sh, r, cc = g.shape
    h = r // 2
    th = min(h, 256)
    nb = h // th

    def body(core_ref, g_ref, r_ref, o_ref):
        del core_ref
        o_ref[...] = g_ref[...] + r_ref[...]

    grid_spec = pltpu.PrefetchScalarGridSpec(
        num_scalar_prefetch=1, grid=(n_sh, nb),
        in_specs=[pl.BlockSpec((None, th, cc), lambda j, i, core_ref: (j, core_ref[0] * nb + i, 0)),
                  pl.BlockSpec((None, th, cc), lambda j, i, core_ref: (j, i, 0))],
        out_specs=pl.BlockSpec((None, th, cc), lambda j, i, core_ref: (j, i, 0)))
    return pl.pallas_call(
        body, name=name, grid_spec=grid_spec,
        out_shape=jax.ShapeDtypeStruct((n_sh, h, cc), F32), compiler_params=_params(2),
    )(core, g, received)


def _exchange_chips(parts):
    n = len(parts)

    def body(*refs):
        ins, outs = refs[:n], refs[n:2 * n]
        send_sems, recv_sems, local_sems = refs[2 * n:]
        x, y, c = _my_place()
        me = 2 * x + y
        chips = _other_chips(x, y)
        local = [pltpu.make_async_copy(ins[i].at[me], outs[i].at[me], local_sems.at[i]) for i in range(n)]
        for cp in local:
            cp.start()
        sends = []
        for i in range(n):
            for j, (px, py) in enumerate(chips):
                sends.append(_remote(ins[i].at[2 * px + py], outs[i].at[me], send_sems.at[3 * i + j],
                                     recv_sems.at[3 * i + j], (px, py, c)))
        for cp in sends:
            cp.start()
        for i in range(n):
            for j, (px, py) in enumerate(chips):
                _remote(ins[i].at[me], outs[i].at[2 * px + py], send_sems.at[3 * i + j],
                        recv_sems.at[3 * i + j], (px, py, c)).wait_recv()
        for cp in sends:
            cp.wait_send()
        for cp in local:
            cp.wait()

    return pl.pallas_call(
        body, name="exchange_chips", in_specs=[ANY_SPEC] * n, out_specs=[ANY_SPEC] * n,
        out_shape=[jax.ShapeDtypeStruct(p.shape, p.dtype) for p in parts],
        scratch_shapes=[pltpu.SemaphoreType.DMA((3 * n,)), pltpu.SemaphoreType.DMA((3 * n,)),
                        pltpu.SemaphoreType.DMA((n,))],
    )(*parts)


def _sum_chips(name, parts):
    n_sh, h, cc = parts.shape
    th = min(h, 256)

    def body(p_ref, o_ref):
        o_ref[...] = ((p_ref[0] + p_ref[1]) + p_ref[2]) + p_ref[3]

    return pl.pallas_call(
        body, name=name, grid=(h // th,),
        in_specs=[_spec((n_sh, th, cc), lambda i: (0, i, 0))],
        out_specs=_spec((th, cc), lambda i: (i, 0)),
        out_shape=jax.ShapeDtypeStruct((h, cc), F32), compiler_params=_params(1),
    )(parts)


def _share_halves(halves):
    flat = [t for per_weight in halves for t in per_weight]
    n = len(flat)
    n_w = len(halves)

    def body(*refs):
        ins, outs = refs[:n], refs[n:n + n_w]
        send_sems, recv_sems, local_sems = refs[n + n_w:]
        x, y, c = _my_place()
        local, sends = [], []
        for i in range(n):
            w, l = divmod(i, DEPTH)
            h = ins[i].shape[0]
            local.append(pltpu.make_async_copy(ins[i], outs[w].at[l, pl.ds(c * h, h), :], local_sems.at[i]))
            sends.append(_remote(ins[i], outs[w].at[l, pl.ds(c * h, h), :], send_sems.at[i],
                                 recv_sems.at[i], (x, y, 1 - c)))
        for cp in local + sends:
            cp.start()
        for i in range(n):
            w, l = divmod(i, DEPTH)
            h = ins[i].shape[0]
            _remote(ins[i], outs[w].at[l, pl.ds((1 - c) * h, h), :], send_sems.at[i], recv_sems.at[i],
                    (x, y, 1 - c)).wait_recv()
        for cp in sends:
            cp.wait_send()
        for cp in local:
            cp.wait()

    return pl.pallas_call(
        body, name="share_halves", in_specs=[ANY_SPEC] * n, out_specs=[ANY_SPEC] * n_w,
        out_shape=[jax.ShapeDtypeStruct((DEPTH, 2 * per_weight[0].shape[0], per_weight[0].shape[1]), F32)
                   for per_weight in halves],
        scratch_shapes=[pltpu.SemaphoreType.DMA((n,)), pltpu.SemaphoreType.DMA((n,)),
                        pltpu.SemaphoreType.DMA((n,))],
    )(*flat)


def _all_reduce_small(pack):
    def body(p_ref, o_ref, recv, send_sems, recv_sems):
        x, y, c = _my_place()
        me = 4 * x + 2 * y + c
        recv[me] = p_ref[...]
        peers = []
        for k in range(1, N_DEV):
            px, py, pc = (x + (k >> 2)) % 2, (y + ((k >> 1) & 1)) % 2, (c + (k & 1)) % 2
            peers.append((px, py, pc))
        sends = [_remote(p_ref, recv.at[me], send_sems.at[k], recv_sems.at[k], peer)
                 for k, peer in enumerate(peers)]
        for cp in sends:
            cp.start()
        for k, (px, py, pc) in enumerate(peers):
            _remote(p_ref, recv.at[4 * px + 2 * py + pc], send_sems.at[k], recv_sems.at[k],
                    (px, py, pc)).wait_recv()
        for cp in sends:
            cp.wait_send()
        total = recv[0]
        for d in range(1, N_DEV):
            total = total + recv[d]
        o_ref[...] = total

    vmem = pl.BlockSpec(memory_space=pltpu.VMEM)
    return pl.pallas_call(
        body, name="all_reduce_small", in_specs=[vmem], out_specs=vmem,
        out_shape=jax.ShapeDtypeStruct(pack.shape, F32),
        scratch_shapes=[pltpu.VMEM((N_DEV,) + pack.shape, F32),
                        pltpu.SemaphoreType.DMA((N_DEV - 1,)), pltpu.SemaphoreType.DMA((N_DEV - 1,))],
    )(pack)


def _adamw(name, w, g, m, v):
    r, cc = w.shape
    th = min(r, 256)

    def body(w_ref, g_ref, m_ref, v_ref, d_ref, m_out, v_out):
        gv = g_ref[...]
        m2 = ADAM_B1 * m_ref[...] + (1.0 - ADAM_B1) * gv
        v2 = ADAM_B2 * v_ref[...] + (1.0 - ADAM_B2) * (gv * gv)
        m_hat = m2 / (1.0 - ADAM_B1 ** ADAM_STEP)
        v_hat = v2 / (1.0 - ADAM_B2 ** ADAM_STEP)
        d_ref[...] = -ADAM_LR * (m_hat / (jnp.sqrt(v_hat) + ADAM_EPS) + ADAM_WD * w_ref[...])
        m_out[...] = m2
        v_out[...] = v2

    tile = _spec((th, cc), lambda i: (i, 0))
    return pl.pallas_call(
        body, name=name, grid=(r // th,), in_specs=[tile] * 4, out_specs=[tile] * 3,
        out_shape=[jax.ShapeDtypeStruct((r, cc), F32)] * 3, compiler_params=_params(1),
    )(w, g, m, v)


def _lower_bounds(lb_logits):
    p = jax.nn.softmax(lb_logits.astype(F32), axis=0)
    return jnp.cumsum(p, axis=0) - p[0]


def _row_tile_specs(tm, width):
    return _spec((tm, width), lambda i, j, k: (i, 0))


def _layer_forward(l, x_in, small, weights, consts):
    win, wo, wu, wd = weights
    cos_t, sin_t, stack, _, _ = consts
    tm = MM_TILE
    n_row = SEQ // tm
    saved = {"x_in": x_in}

    h, h_t = _rms_fwd(f"norm_mix{l}", x_in, small["norm_mix"][l][None, :])
    proj = _matmul(f"proj{l}", h, win,
                   _spec((tm, D_MODEL), lambda i, j, k: (i, 0)),
                   _spec((None, None, D_MODEL, SHARD_IN), lambda i, j, k: (j, l, 0, 0)),
                   (SEQ, IN_W), F32, _spec((tm, SHARD_IN), lambda i, j, k: (i, j)),
                   (n_row, N_CHIPS, 1), (tm, SHARD_IN))
    saved.update(h_t=h_t, proj=proj)

    qkv = _attn_prep(f"attn_prep{l}", proj, cos_t, sin_t)
    outs, lses = [], []
    for p, d in enumerate(DILATIONS):
        o, lse = _attn_fwd(f"attn_fwd{l}_{d}", *qkv[p], SEQ // d // SPAN)
        outs.append(o)
        lses.append(lse)
    an, an_t, attn, lse = _attn_merge(f"attn_merge{l}", outs, lses, small["attn_out_gain"][l][None, :])
    saved.update(qkv=qkv, an_t=an_t, attn=attn, lse=lse)

    lb3 = small["lower"][l].reshape(HGRN_HEADS, 1, HGRN_DIM)
    rec, rec_t, o_pre, states, scores = _hgrn_fwd(f"hgrn_fwd{l}", proj, lb3,
                                                  small["hgrn_out_gain"][l][None, :], stack)
    saved.update(rec_t=rec_t, o_pre=o_pre, states=states, scores=scores, lb3=lb3)

    def out_proj(name, a, part, resid):
        return _matmul(name, a, wo,
                       _spec((tm, SHARD_OUT), lambda i, j, k: (i, k)),
                       _spec((None, None, SHARD_OUT, D_MODEL), lambda i, j, k: (k + 2 * part, l, 0, 0)),
                       (SEQ, D_MODEL), F32, _spec((tm, D_MODEL), lambda i, j, k: (i, 0)),
                       (n_row, 1, 2), (tm, D_MODEL),
                       extra=resid, extra_spec=_spec((tm, D_MODEL), lambda i, j, k: (i, 0)), epilogue="add")

    x_mid = out_proj(f"out_rec{l}", rec, 1, out_proj(f"out_attn{l}", an, 0, x_in))
    saved["x_mid"] = x_mid

    h2, h2_t = _rms_fwd(f"norm_mlp{l}", x_mid, small["norm_mlp"][l][None, :])
    u = _matmul(f"up{l}", h2, wu,
                _spec((tm, D_MODEL), lambda i, j, k: (i, 0)),
                _spec((None, None, D_MODEL, SHARD_MLP), lambda i, j, k: (j, l, 0, 0)),
                (SEQ, MLP_HIDDEN), F32, _spec((tm, SHARD_MLP), lambda i, j, k: (i, j)),
                (n_row, N_CHIPS, 1), (tm, SHARD_MLP))
    a, a_t = _relu2(f"relu2_{l}", u)
    x_out = _matmul(f"down{l}", a, wd,
                    _spec((tm, SHARD_MLP), lambda i, j, k: (i, k)),
                    _spec((None, None, SHARD_MLP, D_MODEL), lambda i, j, k: (k, l, 0, 0)),
                    (SEQ, D_MODEL), F32, _spec((tm, D_MODEL), lambda i, j, k: (i, 0)),
                    (n_row, 1, N_CHIPS), (tm, D_MODEL),
                    extra=x_mid, extra_spec=_spec((tm, D_MODEL), lambda i, j, k: (i, 0)), epilogue="add")
    saved.update(h2_t=h2_t, u=u, a_t=a_t)
    return x_out, saved


def _layer_backward(l, dx, saved, small, weights, consts):
    win, wo, wu, wd = weights
    cos_t, sin_t, stack, stack_t, head_sum = consts
    tm = MM_TILE
    n_row = SEQ // tm
    n_k = SEQ // tm

    du = _matmul(f"d_u{l}", dx, wd,
                 _spec((tm, D_MODEL), lambda i, j, k: (i, 0)),
                 _spec((None, None, SHARD_MLP, D_MODEL), lambda i, j, k: (j, l, 0, 0)),
                 (SEQ, MLP_HIDDEN), BF16, _spec((tm, SHARD_MLP), lambda i, j, k: (i, j)),
                 (n_row, N_CHIPS, 1), (tm, SHARD_MLP), nt=True,
                 extra=saved["u"], extra_spec=_spec((tm, SHARD_MLP), lambda i, j, k: (i, j)),
                 epilogue="relu2_grad")
    d_wd = _matmul(f"d_wdown{l}", saved["a_t"], dx,
                   _spec((SHARD_MLP, tm), lambda i, j, k: (i, k)),
                   _spec((tm, D_MODEL), lambda i, j, k: (k, 0)),
                   (N_CHIPS, SHARD_MLP, D_MODEL), F32, _spec((None, SHARD_MLP, D_MODEL), lambda i, j, k: (i, 0, 0)),
                   (N_CHIPS, 1, n_k), (SHARD_MLP, D_MODEL))
    dh2 = _matmul(f"d_h2_{l}", du, wu,
                  _spec((tm, SHARD_MLP), lambda i, j, k: (i, k)),
                  _spec((None, None, D_MODEL, SHARD_MLP), lambda i, j, k: (k, l, 0, 0)),
                  (SEQ, D_MODEL), F32, _spec((tm, D_MODEL), lambda i, j, k: (i, 0)),
                  (n_row, 1, N_CHIPS), (tm, D_MODEL), nt=True)
    d_wu = _matmul(f"d_wup{l}", saved["h2_t"], du,
                   _spec((D_MODEL, tm), lambda i, j, k: (0, k)),
                   _spec((tm, SHARD_MLP), lambda i, j, k: (k, j)),
                   (N_CHIPS, D_MODEL, SHARD_MLP), F32, _spec((None, D_MODEL, SHARD_MLP), lambda i, j, k: (j, 0, 0)),
                   (1, N_CHIPS, n_k), (D_MODEL, SHARD_MLP))
    dxm, dg_mlp = _rms_bwd(f"norm_mlp_bwd{l}", dh2, saved["x_mid"], small["norm_mlp"][l][None, :], dx)

    def d_mixed(name, part):
        return _matmul(name, dxm, wo,
                       _spec((tm, D_MODEL), lambda i, j, k: (i, 0)),
                       _spec((None, None, SHARD_OUT, D_MODEL), lambda i, j, k: (j + 2 * part, l, 0, 0)),
                       (SEQ, ATTN_WIDTH), F32, _spec((tm, SHARD_OUT), lambda i, j, k: (i, j)),
                       (n_row, 2, 1), (tm, SHARD_OUT), nt=True)

    def d_wout(name, a_t):
        return _matmul(name, a_t, dxm,
                       _spec((SHARD_OUT, tm), lambda i, j, k: (i, k)),
                       _spec((tm, D_MODEL), lambda i, j, k: (k, 0)),
                       (2, SHARD_OUT, D_MODEL), F32, _spec((None, SHARD_OUT, D_MODEL), lambda i, j, k: (i, 0, 0)),
                       (2, 1, n_k), (SHARD_OUT, D_MODEL))

    d_an = d_mixed(f"d_attn_n{l}", 0)
    d_rec = d_mixed(f"d_rec{l}", 1)
    d_wo = jnp.concatenate([d_wout(f"d_wout_attn{l}", saved["an_t"]),
                            d_wout(f"d_wout_rec{l}", saved["rec_t"])], axis=0)

    d_out, delta, lses, dg_attn = _attn_bwd_prep(f"attn_bwd_prep{l}", d_an, saved["attn"], saved["lse"],
                                                 small["attn_out_gain"][l][None, :], head_sum)
    grads = []
    for p, d in enumerate(DILATIONS):
        grads.append(_attn_bwd(f"attn_bwd{l}_{d}", *saved["qkv"][p], d_out[p], delta[p], lses[p],
                               SEQ // d // SPAN))
    dp_attn = _attn_bwd_post(f"attn_bwd_post{l}", grads, cos_t, sin_t)

    dq_h, df_h, di_h, dg_h, d_lower, dg_hgrn = _hgrn_bwd(
        f"hgrn_bwd{l}", saved["proj"], d_rec, saved["o_pre"], saved["states"], saved["scores"],
        saved["lb3"], small["hgrn_out_gain"][l][None, :], stack, stack_t)
    dproj = jnp.concatenate([dp_attn, dq_h, df_h, di_h, dg_h], axis=1)

    dh = _matmul(f"d_h{l}", dproj, win,
                 _spec((tm, SHARD_IN), lambda i, j, k: (i, k)),
                 _spec((None, None, D_MODEL, SHARD_IN), lambda i, j, k: (k, l, 0, 0)),
                 (SEQ, D_MODEL), F32, _spec((tm, D_MODEL), lambda i, j, k: (i, 0)),
                 (n_row, 1, N_CHIPS), (tm, D_MODEL), nt=True)
    d_win = _matmul(f"d_win{l}", saved["h_t"], dproj,
                    _spec((D_MODEL, tm), lambda i, j, k: (0, k)),
                    _spec((tm, SHARD_IN), lambda i, j, k: (k, j)),
                    (N_CHIPS, D_MODEL, SHARD_IN), F32, _spec((None, D_MODEL, SHARD_IN), lambda i, j, k: (j, 0, 0)),
                    (1, N_CHIPS, n_k), (D_MODEL, SHARD_IN))
    dx_in, dg_mix = _rms_bwd(f"norm_mix_bwd{l}", dh, saved["x_in"], small["norm_mix"][l][None, :], dxm)

    small_grads = {"norm_mix": dg_mix[0], "attn_out_gain": dg_attn[0],
                   "lower": d_lower.reshape(HGRN_WIDTH),
                   "hgrn_out_gain": jnp.sum(dg_hgrn, axis=0).reshape(HGRN_DIM), "norm_mlp": dg_mlp[0]}
    return dx_in, (d_win, d_wo, d_wu, d_wd), small_grads


def _local_step(xs, target, small, weights):
    consts = _rope_tables() + _hgrn_consts() + (_head_sum_matrix(),)
    stream = xs
    saved = []
    for l in range(DEPTH):
        stream, s = _layer_forward(l, stream, small, weights, consts)
        saved.append(s)
    dx, dg_final, loss = _loss_head(stream, small["norm_final"][None, :], target)
    big = [None] * DEPTH
    small_grads = [None] * DEPTH
    for l in reversed(range(DEPTH)):
        dx, big[l], small_grads[l] = _layer_backward(l, dx, saved[l], small, weights, consts)
    return loss, dx, dg_final[0], big, small_grads


def _pack_small(norm_mix, attn_out_gain, lb, hgrn_out_gain, norm_mlp, norm_final, last_row):
    rows = [norm_mix, attn_out_gain.reshape(1, D_MODEL), lb.reshape(1, D_MODEL),
            jnp.pad(hgrn_out_gain.reshape(1, DEPTH * HGRN_DIM), ((0, 0), (0, D_MODEL - DEPTH * HGRN_DIM))),
            norm_mlp, norm_final.reshape(1, D_MODEL), last_row.reshape(1, D_MODEL)]
    pack = jnp.concatenate(rows, axis=0)
    return jnp.pad(pack, ((0, PACK_ROWS - pack.shape[0]), (0, 0)))


def _unpack_small(pack):
    return (pack[0:2], pack[2].reshape(DEPTH, ATTN_WIDTH), pack[3].reshape(DEPTH, HGRN_WIDTH),
            pack[4, :DEPTH * HGRN_DIM].reshape(DEPTH, HGRN_DIM), pack[5:7], pack[7], pack[8])


def kernel(x, norm_mix, w_in, attn_out_gain, hgrn_lb_logits, hgrn_out_gain, w_out, norm_mlp, w_up, w_down, norm_final, loss_target, m_norm_mix, m_w_in, m_attn_out_gain, m_hgrn_lb_logits, m_hgrn_out_gain, m_w_out, m_norm_mlp, m_w_up, m_w_down, m_norm_final, v_norm_mix, v_w_in, v_attn_out_gain, v_hgrn_lb_logits, v_hgrn_out_gain, v_w_out, v_norm_mlp, v_w_up, v_w_down, v_norm_final):
    core = lax.axis_index("c").astype(jnp.int32).reshape(1)
    weights = _gather_weights([w.astype(BF16) for w in (w_in, w_out, w_up, w_down)])
    lower, lower_vjp = jax.vjp(_lower_bounds, hgrn_lb_logits)
    small = {"norm_mix": norm_mix, "attn_out_gain": attn_out_gain, "lower": lower,
             "hgrn_out_gain": hgrn_out_gain, "norm_mlp": norm_mlp, "norm_final": norm_final}

    loss, dx, dg_final, big, sg = _local_step(x[0], loss_target[0], small, weights)

    stack2 = lambda key: jnp.stack([sg[l][key] for l in range(DEPTH)])
    pack = _pack_small(stack2("norm_mix"), stack2("attn_out_gain"), stack2("lower"), stack2("hgrn_out_gain"),
                       stack2("norm_mlp"), dg_final, jnp.broadcast_to(loss[0, 0], (D_MODEL,)))
    g_mix, g_attn, g_lower, g_hgrn, g_mlp, g_final, loss_row = _unpack_small(_all_reduce_small(pack))
    (g_logits,) = lower_vjp(g_lower)

    zeros_row = jnp.zeros((D_MODEL,), F32)
    small_w = (norm_mix, attn_out_gain, hgrn_lb_logits, hgrn_out_gain, norm_mlp, norm_final)
    small_m = (m_norm_mix, m_attn_out_gain, m_hgrn_lb_logits, m_hgrn_out_gain, m_norm_mlp, m_norm_final)
    small_v = (v_norm_mix, v_attn_out_gain, v_hgrn_lb_logits, v_hgrn_out_gain, v_norm_mlp, v_norm_final)
    small_g = (g_mix, g_attn, g_logits, g_hgrn, g_mlp, g_final)
    packs = [_pack_small(*t, zeros_row) for t in (small_w, small_g, small_m, small_v)]
    small_delta, small_new_m, small_new_v = [_unpack_small(p)[:6] for p in _adamw("adamw_small", *packs)]

    partial = [big[l][w] for w in range(4) for l in range(DEPTH)]
    received = _exchange_halves(partial)
    halves = [_add_own_half(f"add_halves{i}", g, r, core) for i, (g, r) in enumerate(zip(partial, received))]
    from_chips = _exchange_chips(halves)
    reduced = [_sum_chips(f"sum_chips{i}", p) for i, p in enumerate(from_chips)]
    big_g = _share_halves([reduced[DEPTH * w:DEPTH * (w + 1)] for w in range(4)])

    big_w = (w_in, w_out, w_up, w_down)
    big_m = (m_w_in, m_w_out, m_w_up, m_w_down)
    big_v = (v_w_in, v_w_out, v_w_up, v_w_down)
    big_delta, big_new_m, big_new_v = [], [], []
    for i, name in enumerate(("w_in", "w_out", "w_up", "w_down")):
        shape = big_w[i].shape
        flat = lambda arr: arr.reshape(shape[0] * shape[1], shape[2])
        d, m2, v2 = _adamw(f"adamw_{name}", flat(big_w[i]), flat(big_g[i]), flat(big_m[i]), flat(big_v[i]))
        big_delta.append(d.reshape(shape))
        big_new_m.append(m2.reshape(shape))
        big_new_v.append(v2.reshape(shape))

    def ordered(small6, big4):
        mix, attn, lbl, hg, mlp, fin = small6
        return (mix, big4[0], attn, lbl, hg, big4[1], mlp, big4[2], big4[3], fin)

    return ((loss_row[0], dx[None]) + ordered(small_g, big_g) + ordered(small_delta, big_delta)
            + ordered(small_new_m, big_new_m) + ordered(small_new_v, big_new_v))
```

```python
import functools
import math

import numpy as np
import jax
import jax.numpy as jnp
from jax import lax
from jax.experimental import pallas as pl
from jax.experimental.pallas import tpu as pltpu

F32 = jnp.float32
BF16 = jnp.bfloat16
MESH = pl.DeviceIdType.MESH

SEQ = 4096
D_MODEL = 1024
DEPTH = 2
ATTN_WIDTH = 512
HEAD_DIM = 64
HGRN_HEADS = 4
HGRN_DIM = 128
HGRN_WIDTH = 512
IN_W = 3584
MLP_HIDDEN = 4096
N_CHIPS = 4
N_DEV = 8
SHARD_IN = IN_W // N_CHIPS
SHARD_OUT = D_MODEL // N_CHIPS
SHARD_MLP = MLP_HIDDEN // N_CHIPS
DILATIONS = (1, 4, 16)
SPAN = 128
ROPE_THETA = 10000.0
NORM_EPS = 1e-6
MASK_VALUE = -1e30
CHUNK = 128
ROW_TILE = 512
MM_TILE = 1024
VMEM_LIMIT = 52 * 1024 * 1024

ADAM_LR = 0.001
ADAM_B1 = 0.9
ADAM_B2 = 0.999
ADAM_EPS = 1e-08
ADAM_WD = 0.01
ADAM_STEP = 10

PACK_ROWS = 16


def _params(n_axes):
    return pltpu.CompilerParams(dimension_semantics=("arbitrary",) * n_axes,
                                vmem_limit_bytes=VMEM_LIMIT)


def _dot(a, b):
    return jnp.dot(a.astype(BF16), b.astype(BF16), preferred_element_type=F32)


def _dot_nt(a, b):
    return lax.dot_general(a.astype(BF16), b.astype(BF16), (((1,), (1,)), ((), ())),
                           preferred_element_type=F32)


def _sigmoid(x):
    return 1.0 / (1.0 + jnp.exp(-x))


def _matmul(name, a, b, a_spec, b_spec, out_shape, out_dtype, out_spec, grid, acc_shape,
            nt=False, extra=None, extra_spec=None, epilogue="none"):
    nk = grid[2]

    def body(*refs):
        if extra is None:
            a_ref, b_ref, o_ref, acc = refs
            e_ref = None
        else:
            a_ref, b_ref, e_ref, o_ref, acc = refs
        kk = pl.program_id(2)

        @pl.when(kk == 0)
        def _():
            acc[...] = jnp.zeros_like(acc)

        if nt:
            acc[...] += _dot_nt(a_ref[...], b_ref[...])
        else:
            acc[...] += _dot(a_ref[...], b_ref[...])

        @pl.when(kk == nk - 1)
        def _():
            r = acc[...]
            if epilogue == "add":
                r = r + e_ref[...]
            elif epilogue == "relu2_grad":
                r = r * (2.0 * jnp.maximum(e_ref[...], 0.0))
            o_ref[...] = r.astype(o_ref.dtype)

    in_specs = [a_spec, b_spec] + ([] if extra is None else [extra_spec])
    args = (a, b) + (() if extra is None else (extra,))
    return pl.pallas_call(
        body, name=name, grid=grid, in_specs=in_specs, out_specs=out_spec,
        out_shape=jax.ShapeDtypeStruct(out_shape, out_dtype),
        scratch_shapes=[pltpu.VMEM(acc_shape, F32)],
        compiler_params=_params(3),
    )(*args)


def _spec(shape, index_map):
    return pl.BlockSpec(shape, index_map)


def _rms_fwd(name, x, gain):
    s, d = x.shape
    t = ROW_TILE

    def body(x_ref, g_ref, h_ref, ht_ref):
        xv = x_ref[...]
        r = lax.rsqrt(jnp.mean(xv * xv, axis=1, keepdims=True) + NORM_EPS)
        h = xv * r * g_ref[...]
        h_ref[...] = h.astype(BF16)
        ht_ref[...] = h.T.astype(BF16)

    return pl.pallas_call(
        body, name=name, grid=(s // t,),
        in_specs=[_spec((t, d), lambda i: (i, 0)), _spec((1, d), lambda i: (0, 0))],
        out_specs=[_spec((t, d), lambda i: (i, 0)), _spec((d, t), lambda i: (0, i))],
        out_shape=[jax.ShapeDtypeStruct((s, d), BF16), jax.ShapeDtypeStruct((d, s), BF16)],
        compiler_params=_params(1),
    )(x, gain)


def _rms_bwd(name, dh, x, gain, dres):
    s, d = x.shape
    t = ROW_TILE

    def body(dh_ref, x_ref, g_ref, dres_ref, dx_ref, dg_ref):
        @pl.when(pl.program_id(0) == 0)
        def _():
            dg_ref[...] = jnp.zeros_like(dg_ref)

        xv = x_ref[...]
        dhv = dh_ref[...]
        r = lax.rsqrt(jnp.mean(xv * xv, axis=1, keepdims=True) + NORM_EPS)
        xhat = xv * r
        dhg = dhv * g_ref[...]
        proj = jnp.mean(dhg * xhat, axis=1, keepdims=True)
        dx_ref[...] = dres_ref[...] + r * (dhg - xhat * proj)
        dg_ref[...] += jnp.sum(dhv * xhat, axis=0, keepdims=True)

    return pl.pallas_call(
        body, name=name, grid=(s // t,),
        in_specs=[_spec((t, d), lambda i: (i, 0)), _spec((t, d), lambda i: (i, 0)),
                  _spec((1, d), lambda i: (0, 0)), _spec((t, d), lambda i: (i, 0))],
        out_specs=[_spec((t, d), lambda i: (i, 0)), _spec((1, d), lambda i: (0, 0))],
        out_shape=[jax.ShapeDtypeStruct((s, d), F32), jax.ShapeDtypeStruct((1, d), F32)],
        compiler_params=_params(1),
    )(dh, x, gain, dres)


def _loss_head(x, gain, target):
    s, d = x.shape
    t = ROW_TILE
    n_steps = s // t

    def body(x_ref, g_ref, t_ref, dx_ref, dg_ref, loss_ref, acc):
        i = pl.program_id(0)

        @pl.when(i == 0)
        def _():
            dg_ref[...] = jnp.zeros_like(dg_ref)
            acc[...] = jnp.zeros_like(acc)

        xv = x_ref[...]
        g = g_ref[...]
        r = lax.rsqrt(jnp.mean(xv * xv, axis=1, keepdims=True) + NORM_EPS)
        xhat = xv * r
        err = xhat * g - t_ref[...]
        acc[...] += jnp.sum(err * err, axis=0, keepdims=True)
        dy = err * (1.0 / d)
        dyg = dy * g
        proj = jnp.mean(dyg * xhat, axis=1, keepdims=True)
        dx_ref[...] = r * (dyg - xhat * proj)
        dg_ref[...] += jnp.sum(dy * xhat, axis=0, keepdims=True)

        @pl.when(i == n_steps - 1)
        def _():
            total = jnp.sum(acc[...], axis=1, keepdims=True) * (0.5 / d)
            loss_ref[...] = jnp.broadcast_to(total, loss_ref.shape)

    return pl.pallas_call(
        body, name="loss_head", grid=(n_steps,),
        in_specs=[_spec((t, d), lambda i: (i, 0)), _spec((1, d), lambda i: (0, 0)),
                  _spec((t, d), lambda i: (i, 0))],
        out_specs=[_spec((t, d), lambda i: (i, 0)), _spec((1, d), lambda i: (0, 0)),
                   _spec((1, 128), lambda i: (0, 0))],
        out_shape=[jax.ShapeDtypeStruct((s, d), F32), jax.ShapeDtypeStruct((1, d), F32),
                   jax.ShapeDtypeStruct((1, 128), F32)],
        scratch_shapes=[pltpu.VMEM((1, d), F32)],
        compiler_params=_params(1),
    )(x, gain, target)


def _relu2(name, u):
    s, f = u.shape
    t = ROW_TILE

    def body(u_ref, a_ref, at_ref):
        r = jnp.maximum(u_ref[...], 0.0)
        a = r * r
        a_ref[...] = a.astype(BF16)
        at_ref[...] = a.T.astype(BF16)

    return pl.pallas_call(
        body, name=name, grid=(s // t, f // t),
        in_specs=[_spec((t, t), lambda i, j: (i, j))],
        out_specs=[_spec((t, t), lambda i, j: (i, j)), _spec((t, t), lambda i, j: (j, i))],
        out_shape=[jax.ShapeDtypeStruct((s, f), BF16), jax.ShapeDtypeStruct((f, s), BF16)],
        compiler_params=_params(2),
    )(u)


def _rope_tables():
    half = HEAD_DIM // 2
    inv_freq = ROPE_THETA ** (-jnp.arange(half, dtype=F32) / half)
    ang = jnp.arange(SEQ, dtype=jnp.int32).astype(F32)[:, None] * inv_freq[None, :]
    cos, sin = jnp.cos(ang), jnp.sin(ang)
    cos_t = jnp.concatenate([cos, cos, cos, cos], axis=1)
    sin_t = jnp.concatenate([-sin, sin, -sin, sin], axis=1)
    return cos_t, sin_t


def _swap_halves(x):
    lane = lax.broadcasted_iota(jnp.int32, x.shape, 1)
    first = (lane % HEAD_DIM) < (HEAD_DIM // 2)
    return jnp.where(first, pltpu.roll(x, 128 - HEAD_DIM // 2, 1), pltpu.roll(x, HEAD_DIM // 2, 1))


def _permuted_specs(t, width):
    specs = [_spec((t, width), lambda i: (i, 0))]
    for d in DILATIONS[1:]:
        specs.append(_spec((d, t // d, width), lambda i: (0, i, 0)))
    return specs


def _permuted_shapes(width, dtype):
    shapes = [jax.ShapeDtypeStruct((SEQ, width), dtype)]
    for d in DILATIONS[1:]:
        shapes.append(jax.ShapeDtypeStruct((d, SEQ // d, width), dtype))
    return shapes


def _attn_prep(name, proj, cos_t, sin_t):
    t = ROW_TILE
    w = ATTN_WIDTH

    def body(q_ref, k_ref, v_ref, cos_ref, sin_ref, *rest):
        outs, scr = rest[:9], rest[9]
        cosv, sinv = cos_ref[...], sin_ref[...]
        for a, (src, roped, scale) in enumerate(((q_ref, True, HEAD_DIM ** -0.5),
                                                 (k_ref, True, 1.0), (v_ref, False, 1.0))):
            o1, o4, o16 = outs[3 * a:3 * a + 3]
            for cb in range(w // 128):
                cols = slice(cb * 128, (cb + 1) * 128)
                val = src[:, cols]
                if roped:
                    val = (val * cosv + _swap_halves(val) * sinv) * scale
                scr[...] = val
                o1[:, cols] = val.astype(BF16)
                for o_ref, d in ((o4, 4), (o16, 16)):
                    for r in range(d):
                        o_ref[r, :, cols] = scr[pl.ds(r, t // d, stride=d), :].astype(BF16)

    out_specs = _permuted_specs(t, w) * 3
    out_shape = _permuted_shapes(w, BF16) * 3
    outs = pl.pallas_call(
        body, name=name, grid=(SEQ // t,),
        in_specs=[_spec((t, w), lambda i: (i, 0)), _spec((t, w), lambda i: (i, 1)),
                  _spec((t, w), lambda i: (i, 2)),
                  _spec((t, 128), lambda i: (i, 0)), _spec((t, 128), lambda i: (i, 0))],
        out_specs=out_specs, out_shape=out_shape,
        scratch_shapes=[pltpu.VMEM((t, 128), F32)],
        compiler_params=_params(1),
    )(proj, proj, proj, cos_t, sin_t)
    q, k, v = outs[0:3], outs[3:6], outs[6:9]
    flat = lambda arr: arr.reshape(SEQ, w)
    return [(flat(q[p]), flat(k[p]), flat(v[p])) for p in range(3)]


def _band_masks():
    row = lax.broadcasted_iota(jnp.int32, (SPAN, SPAN), 0)
    col = lax.broadcasted_iota(jnp.int32, (SPAN, SPAN), 1)
    return col >= row, col <= row, col < HEAD_DIM


def _attn_fwd(name, q, k, v, seg_blocks):
    n_blocks = SEQ // SPAN

    def body(q_ref, k_ref, v_ref, o_ref, lse_ref):
        prev_band, cur_ok, head0 = _band_masks()

        def step(b, carry):
            r0 = pl.multiple_of(b * SPAN, SPAN)
            p0 = pl.multiple_of(jnp.maximum(b - 1, 0) * SPAN, SPAN)
            qb = q_ref[pl.ds(r0, SPAN), :]
            kc, vc = k_ref[pl.ds(r0, SPAN), :], v_ref[pl.ds(r0, SPAN), :]
            kp, vp = k_ref[pl.ds(p0, SPAN), :], v_ref[pl.ds(p0, SPAN), :]
            prev_ok = prev_band & ((b % seg_blocks) != 0)
            o_acc = jnp.zeros((SPAN, 128), F32)
            lse_acc = jnp.zeros((SPAN, 128), F32)
            for mh in (head0, ~head0):
                qh = jnp.where(mh, qb, jnp.zeros_like(qb))
                sp = jnp.where(prev_ok, _dot_nt(qh, kp), MASK_VALUE)
                sc = jnp.where(cur_ok, _dot_nt(qh, kc), MASK_VALUE)
                m = jnp.maximum(jnp.max(sp, axis=1, keepdims=True), jnp.max(sc, axis=1, keepdims=True))
                pp = jnp.where(prev_ok, jnp.exp(sp - m), 0.0)
                pc = jnp.where(cur_ok, jnp.exp(sc - m), 0.0)
                l = jnp.sum(pp, axis=1, keepdims=True) + jnp.sum(pc, axis=1, keepdims=True)
                vph = jnp.where(mh, vp, jnp.zeros_like(vp))
                vch = jnp.where(mh, vc, jnp.zeros_like(vc))
                o_acc = o_acc + (_dot(pp, vph) + _dot(pc, vch)) / l
                lse_acc = jnp.where(mh, m + jnp.log(l), lse_acc)
            o_ref[pl.ds(r0, SPAN), :] = o_acc
            lse_ref[pl.ds(r0, SPAN), :] = lse_acc
            return carry

        lax.fori_loop(0, n_blocks, step, 0)

    col = _spec((SEQ, 128), lambda j: (0, j))
    return pl.pallas_call(
        body, name=name, grid=(ATTN_WIDTH // 128,),
        in_specs=[col, col, col], out_specs=[col, col],
        out_shape=[jax.ShapeDtypeStruct((SEQ, ATTN_WIDTH), F32)] * 2,
        compiler_params=_params(1),
    )(q, k, v)


def _unpermute(dst, src_ref, d, cols):
    n = dst.shape[0] // d
    for r in range(d):
        dst[pl.ds(r, n, stride=d), :] = src_ref[r, :, cols]


def _attn_merge(name, outs, lses, gain):
    t = ROW_TILE
    w = ATTN_WIDTH

    def body(o1, o4, o16, l1, l4, l16, g_ref, an_ref, ant_ref, attn_ref, lse_ref, so4, so16, sl4, sl16):
        for cb in range(w // 128):
            cols = slice(cb * 128, (cb + 1) * 128)
            _unpermute(so4, o4, 4, cols)
            _unpermute(so16, o16, 16, cols)
            _unpermute(sl4, l4, 4, cols)
            _unpermute(sl16, l16, 16, cols)
            la, lb, lc = l1[:, cols], sl4[...], sl16[...]
            m = jnp.maximum(jnp.maximum(la, lb), lc)
            ea, eb, ec = jnp.exp(la - m), jnp.exp(lb - m), jnp.exp(lc - m)
            tot = ea + eb + ec
            attn_ref[:, cols] = (ea * o1[:, cols] + eb * so4[...] + ec * so16[...]) / tot
            lse_ref[:, cols] = m + jnp.log(tot)
        attn = attn_ref[...]
        r = lax.rsqrt(jnp.mean(attn * attn, axis=1, keepdims=True) + NORM_EPS)
        an = attn * r * g_ref[...]
        an_ref[...] = an.astype(BF16)
        ant_ref[...] = an.T.astype(BF16)

    views = lambda arrs: [arrs[0], arrs[1].reshape(4, SEQ // 4, w), arrs[2].reshape(16, SEQ // 16, w)]
    row = _spec((t, w), lambda i: (i, 0))
    return pl.pallas_call(
        body, name=name, grid=(SEQ // t,),
        in_specs=_permuted_specs(t, w) * 2 + [_spec((1, w), lambda i: (0, 0))],
        out_specs=[row, _spec((w, t), lambda i: (0, i)), row, row],
        out_shape=[jax.ShapeDtypeStruct((SEQ, w), BF16), jax.ShapeDtypeStruct((w, SEQ), BF16),
                   jax.ShapeDtypeStruct((SEQ, w), F32), jax.ShapeDtypeStruct((SEQ, w), F32)],
        scratch_shapes=[pltpu.VMEM((t, 128), F32)] * 4,
        compiler_params=_params(1),
    )(*views(outs), *views(lses), gain)


def _head_sum_matrix():
    i = np.arange(ATTN_WIDTH)
    return jnp.asarray((i[:, None] // HEAD_DIM) == (i[None, :] // HEAD_DIM), dtype=F32)


def _attn_bwd_prep(name, d_an, attn, lse, gain, head_sum):
    t = ROW_TILE
    w = ATTN_WIDTH

    def body(dan_ref, attn_ref, lse_ref, g_ref, hs_ref, *rest):
        (do1, do4, do16, dl1, dl4, dl16, ls4, ls16, dg_ref), (sdo, sdl, sls) = rest[:9], rest[9:]

        @pl.when(pl.program_id(0) == 0)
        def _():
            dg_ref[...] = jnp.zeros_like(dg_ref)

        attn = attn_ref[...]
        dan = dan_ref[...]
        r = lax.rsqrt(jnp.mean(attn * attn, axis=1, keepdims=True) + NORM_EPS)
        xhat = attn * r
        dg_ref[...] += jnp.sum(dan * xhat, axis=0, keepdims=True)
        dang = dan * g_ref[...]
        d_o = r * (dang - xhat * jnp.mean(dang * xhat, axis=1, keepdims=True))
        delta = jnp.dot(d_o * attn, hs_ref[...], preferred_element_type=F32,
                        precision=lax.Precision.HIGHEST)
        do1[...] = d_o.astype(BF16)
        dl1[...] = delta
        for cb in range(w // 128):
            cols = slice(cb * 128, (cb + 1) * 128)
            sdo[...] = d_o[:, cols]
            sdl[...] = delta[:, cols]
            sls[...] = lse_ref[:, cols]
            for d, o_do, o_dl, o_ls in ((4, do4, dl4, ls4), (16, do16, dl16, ls16)):
                for rr in range(d):
                    rows = pl.ds(rr, t // d, stride=d)
                    o_do[rr, :, cols] = sdo[rows, :].astype(BF16)
                    o_dl[rr, :, cols] = sdl[rows, :]
                    o_ls[rr, :, cols] = sls[rows, :]

    row = _spec((t, w), lambda i: (i, 0))
    perm = _permuted_specs(t, w)
    outs = pl.pallas_call(
        body, name=name, grid=(SEQ // t,),
        in_specs=[row, row, row, _spec((1, w), lambda i: (0, 0)), _spec((w, w), lambda i: (0, 0))],
        out_specs=perm + perm + perm[1:] + [_spec((1, w), lambda i: (0, 0))],
        out_shape=(_permuted_shapes(w, BF16) + _permuted_shapes(w, F32) + _permuted_shapes(w, F32)[1:]
                   + [jax.ShapeDtypeStruct((1, w), F32)]),
        scratch_shapes=[pltpu.VMEM((t, 128), F32)] * 3,
        compiler_params=_params(1),
    )(d_an, attn, lse, gain, head_sum)
    flat = lambda arr: arr.reshape(SEQ, w)
    d_out = [flat(a) for a in outs[0:3]]
    delta = [flat(a) for a in outs[3:6]]
    lses = [lse, flat(outs[6]), flat(outs[7])]
    return d_out, delta, lses, outs[8]


def _attn_bwd(name, q, k, v, d_out, delta, lse, seg_blocks):
    n_blocks = SEQ // SPAN

    def body(q_ref, k_ref, v_ref, do_ref, dl_ref, lse_ref, dq_ref, dk_ref, dv_ref):
        prev_band, cur_ok, head0 = _band_masks()
        dk_ref[...] = jnp.zeros_like(dk_ref)
        dv_ref[...] = jnp.zeros_like(dv_ref)

        def step(b, carry):
            r0 = pl.multiple_of(b * SPAN, SPAN)
            p0 = pl.multiple_of(jnp.maximum(b - 1, 0) * SPAN, SPAN)
            cur, prev = pl.ds(r0, SPAN), pl.ds(p0, SPAN)
            qb, dob = q_ref[cur, :], do_ref[cur, :]
            kc, vc, kp, vp = k_ref[cur, :], v_ref[cur, :], k_ref[prev, :], v_ref[prev, :]
            lse_b, dl_b = lse_ref[cur, :], dl_ref[cur, :]
            prev_ok = prev_band & ((b % seg_blocks) != 0)
            dq = jnp.zeros((SPAN, 128), F32)
            dkp = jnp.zeros((SPAN, 128), F32)
            dkc = jnp.zeros((SPAN, 128), F32)
            dvp = jnp.zeros((SPAN, 128), F32)
            dvc = jnp.zeros((SPAN, 128), F32)
            for h, mh in enumerate((head0, ~head0)):
                lane = h * HEAD_DIM
                lse_h, dl_h = lse_b[:, lane:lane + 1], dl_b[:, lane:lane + 1]
                qh = jnp.where(mh, qb, jnp.zeros_like(qb))
                doh = jnp.where(mh, dob, jnp.zeros_like(dob))
                pp = jnp.where(prev_ok, jnp.exp(_dot_nt(qh, kp) - lse_h), 0.0)
                pc = jnp.where(cur_ok, jnp.exp(_dot_nt(qh, kc) - lse_h), 0.0)
                dsp = pp * (_dot_nt(doh, vp) - dl_h)
                dsc = pc * (_dot_nt(doh, vc) - dl_h)
                dq = dq + jnp.where(mh, _dot(dsp, kp) + _dot(dsc, kc), 0.0)
                dkp = dkp + _dot(dsp.T, qh)
                dkc = dkc + _dot(dsc.T, qh)
                dvp = dvp + _dot(pp.T, doh)
                dvc = dvc + _dot(pc.T, doh)
            dq_ref[cur, :] = dq
            dk_ref[prev, :] += dkp
            dv_ref[prev, :] += dvp
            dk_ref[cur, :] += dkc
            dv_ref[cur, :] += dvc
            return carry

        lax.fori_loop(0, n_blocks, step, 0)

    col = _spec((SEQ, 128), lambda j: (0, j))
    return pl.pallas_call(
        body, name=name, grid=(ATTN_WIDTH // 128,),
        in_specs=[col] * 6, out_specs=[col] * 3,
        out_shape=[jax.ShapeDtypeStruct((SEQ, ATTN_WIDTH), F32)] * 3,
        compiler_params=_params(1),
    )(q, k, v, d_out, delta, lse)


def _attn_bwd_post(name, grads, cos_t, sin_t):
    t = ROW_TILE
    w = ATTN_WIDTH

    def body(*refs):
        ins, cos_ref, sin_ref, out_ref, s4, s16 = refs[:9], refs[9], refs[10], refs[11], refs[12], refs[13]
        cosv, sinv = cos_ref[...], sin_ref[...]
        for a in range(3):
            g1, g4, g16 = ins[a], ins[3 + a], ins[6 + a]
            for cb in range(w // 128):
                cols = slice(cb * 128, (cb + 1) * 128)
                _unpermute(s4, g4, 4, cols)
                _unpermute(s16, g16, 16, cols)
                val = g1[:, cols] + s4[...] + s16[...]
                if a < 2:
                    val = val * cosv + _swap_halves(val * sinv)
                if a == 0:
                    val = val * (HEAD_DIM ** -0.5)
                out_ref[:, a * w + cb * 128:a * w + (cb + 1) * 128] = val

    views = []
    for p, d in enumerate(DILATIONS):
        for a in range(3):
            views.append(grads[p][a] if d == 1 else grads[p][a].reshape(d, SEQ // d, w))
    perm = _permuted_specs(t, w)
    in_specs = [perm[0]] * 3 + [perm[1]] * 3 + [perm[2]] * 3
    return pl.pallas_call(
        body, name=name, grid=(SEQ // t,),
        in_specs=in_specs + [_spec((t, 128), lambda i: (i, 0))] * 2,
        out_specs=_spec((t, 3 * w), lambda i: (i, 0)),
        out_shape=jax.ShapeDtypeStruct((SEQ, 3 * w), F32),
        scratch_shapes=[pltpu.VMEM((t, 128), F32)] * 2,
        compiler_params=_params(1),
    )(*views, cos_t, sin_t)


N_LEVELS = 7


def _hgrn_consts():
    c = CHUNK
    i = np.arange(c)[:, None]
    s = np.arange(c)[None, :]
    blocks = [s <= i]
    for lv in range(N_LEVELS):
        bs = c >> lv
        h = bs // 2
        m = (i // bs) * bs + h - 1
        second = (i % bs) >= h
        blocks.append((second & (s > m) & (s <= i)) | (~second & (s > i) & (s <= m)))
    blocks.append(s > i)
    stack = np.concatenate(blocks, axis=0).astype(np.float32)
    return jnp.asarray(stack, dtype=BF16), jnp.asarray(stack.T, dtype=BF16)


def _exact_dot(m01, x):
    hi = x.astype(BF16)
    r1 = x - hi.astype(F32)
    mid = r1.astype(BF16)
    lo = (r1 - mid.astype(F32)).astype(BF16)
    n = x.shape[1]
    full = jnp.dot(m01, jnp.concatenate([hi, mid, lo], axis=1), preferred_element_type=F32)
    return (full[:, :n] + full[:, n:2 * n]) + full[:, 2 * n:]


def _hgrn_gates(qh, z, lb):
    sq = _sigmoid(qh)
    q = qh * sq * (HGRN_DIM ** -0.5)
    sig = _sigmoid(z)
    sigm = _sigmoid(-z)
    f = lb + (1.0 - lb) * sig
    k = (1.0 - lb) * sigm
    return q, k, f, sq, sig, sigm


def _level_masks(lv):
    row = lax.broadcasted_iota(jnp.int32, (CHUNK, CHUNK), 0)
    col = lax.broadcasted_iota(jnp.int32, (CHUNK, CHUNK), 1)
    shift = N_LEVELS - lv
    second = (row & (CHUNK >> (lv + 1))) != 0
    same = (row >> shift) == (col >> shift)
    return second, same


def _hgrn_fwd(name, proj, lb, gain, stack):
    t = ROW_TILE
    per = t // CHUNK
    n_rb = SEQ // t
    n_chunks = SEQ // CHUNK
    col0 = 3 * ATTN_WIDTH // 128

    def body(q_ref, f_ref, i_ref, g_ref, lb_ref, gain_ref, stack_ref,
             rec_ref, rect_ref, o_ref, st_out, a_out, st):
        @pl.when(pl.program_id(1) == 0)
        def _():
            st[...] = jnp.zeros_like(st)

        lbv = lb_ref[...]
        row = lax.broadcasted_iota(jnp.int32, (CHUNK, CHUNK), 0)
        col = lax.broadcasted_iota(jnp.int32, (CHUNK, CHUNK), 1)
        for c in range(per):
            rows = slice(c * CHUNK, (c + 1) * CHUNK)
            qh, z, v, gh = q_ref[rows, :], f_ref[rows, :], i_ref[rows, :], g_ref[rows, :]
            q, k, f, _, _, _ = _hgrn_gates(qh, z, lbv)
            dec = _exact_dot(stack_ref[...], jnp.log(f))
            g = dec[0:CHUNK]
            to_end = dec[(N_LEVELS + 1) * CHUNK:(N_LEVELS + 2) * CHUNK]
            a = jnp.where(row == col, jnp.sum(q * k, axis=1, keepdims=True), 0.0)
            for lv in range(N_LEVELS):
                e = jnp.exp(dec[(lv + 1) * CHUNK:(lv + 2) * CHUNK])
                second, same = _level_masks(lv)
                qt = jnp.where(second, q * e, 0.0)
                kt = jnp.where(second, 0.0, k * e)
                a = a + jnp.where(same, _dot_nt(qt, kt), 0.0)
            st_prev = st[...]
            st_out[c] = st_prev
            a_out[c] = a
            o = _dot(a, v) + _dot_nt(q * jnp.exp(g), st_prev)
            k_end = k * jnp.exp(to_end)
            st[...] = st_prev * jnp.exp(g[CHUNK - 1:CHUNK, :]) + _dot(v.T, k_end)
            o_ref[rows, :] = o
            r = lax.rsqrt(jnp.mean(o * o, axis=1, keepdims=True) + NORM_EPS)
            rec = o * r * gain_ref[...] * (gh * _sigmoid(gh))
            rec_ref[rows, :] = rec.astype(BF16)
            rect_ref[:, rows] = rec.T.astype(BF16)

    def col_spec(tt):
        return _spec((t, HGRN_DIM), lambda h, rb: (rb, col0 + HGRN_HEADS * tt + h))

    chunk_spec = _spec((None, per, CHUNK, CHUNK), lambda h, rb: (h, rb, 0, 0))
    return pl.pallas_call(
        body, name=name, grid=(HGRN_HEADS, n_rb),
        in_specs=[col_spec(0), col_spec(1), col_spec(2), col_spec(3),
                  _spec((None, 1, HGRN_DIM), lambda h, rb: (h, 0, 0)),
                  _spec((1, HGRN_DIM), lambda h, rb: (0, 0)),
                  _spec(stack.shape, lambda h, rb: (0, 0))],
        out_specs=[_spec((t, HGRN_DIM), lambda h, rb: (rb, h)),
                   _spec((HGRN_DIM, t), lambda h, rb: (h, rb)),
                   _spec((t, HGRN_DIM), lambda h, rb: (rb, h)),
                   chunk_spec, chunk_spec],
        out_shape=[jax.ShapeDtypeStruct((SEQ, HGRN_WIDTH), BF16),
                   jax.ShapeDtypeStruct((HGRN_WIDTH, SEQ), BF16),
                   jax.ShapeDtypeStruct((SEQ, HGRN_WIDTH), F32),
                   jax.ShapeDtypeStruct((HGRN_HEADS, n_chunks, CHUNK, CHUNK), F32),
                   jax.ShapeDtypeStruct((HGRN_HEADS, n_chunks, CHUNK, CHUNK), F32)],
        scratch_shapes=[pltpu.VMEM((CHUNK, CHUNK), F32)],
        compiler_params=_params(2),
    )(proj, proj, proj, proj, lb, gain, stack)


def _hgrn_bwd(name, proj, d_rec, o_pre, states, scores, lb, gain, stack, stack_t):
    t = ROW_TILE
    per = t // CHUNK
    n_rb = SEQ // t
    col0 = 3 * ATTN_WIDTH // 128

    def body(q_ref, f_ref, i_ref, g_ref, drec_ref, o_ref, st_ref, a_ref, lb_ref, gain_ref,
             stack_ref, stack_t_ref, dq_ref, df_ref, di_ref, dg_ref, dlb_ref, dgain_ref, dst):
        @pl.when(pl.program_id(1) == 0)
        def _():
            dst[...] = jnp.zeros_like(dst)
            dlb_ref[...] = jnp.zeros_like(dlb_ref)
            dgain_ref[...] = jnp.zeros_like(dgain_ref)

        lbv = lb_ref[...]
        gain_v = gain_ref[...]
        row = lax.broadcasted_iota(jnp.int32, (CHUNK, CHUNK), 0)
        col = lax.broadcasted_iota(jnp.int32, (CHUNK, CHUNK), 1)
        for c in reversed(range(per)):
            rows = slice(c * CHUNK, (c + 1) * CHUNK)
            qh, z, v, gh = q_ref[rows, :], f_ref[rows, :], i_ref[rows, :], g_ref[rows, :]
            q, k, f, sq, sig, sigm = _hgrn_gates(qh, z, lbv)
            dec = _exact_dot(stack_ref[...], jnp.log(f))
            g = dec[0:CHUNK]
            to_end = dec[(N_LEVELS + 1) * CHUNK:(N_LEVELS + 2) * CHUNK]
            e_g = jnp.exp(g)
            e_end = jnp.exp(to_end)
            e_last = jnp.exp(g[CHUNK - 1:CHUNK, :])
            q_in = q * e_g
            k_end = k * e_end
            st_prev = st_ref[c]
            a = a_ref[c]
            dst_new = dst[...]

            o = o_ref[rows, :]
            drec = drec_ref[rows, :]
            sg = _sigmoid(gh)
            r = lax.rsqrt(jnp.mean(o * o, axis=1, keepdims=True) + NORM_EPS)
            ohat = o * r
            d_gh = drec * (ohat * gain_v) * (sg * (1.0 + gh * (1.0 - sg)))
            d_on = drec * (gh * sg)
            dgain_ref[...] += jnp.sum(d_on * ohat, axis=0, keepdims=True)
            d_ohat = d_on * gain_v
            d_o = r * (d_ohat - ohat * jnp.mean(d_ohat * ohat, axis=1, keepdims=True))

            d_a = jnp.where(row >= col, _dot_nt(d_o, v), 0.0)
            d_at = jnp.where(col >= row, _dot_nt(v, d_o), 0.0)
            d_v = _dot(a.T, d_o) + _dot_nt(k_end, dst_new)
            d_q_in = _dot(d_o, st_prev)
            d_k_end = _dot(v, dst_new)
            d_q = d_q_in * e_g
            d_k = d_k_end * e_end
            diag = jnp.sum(d_o * v, axis=1, keepdims=True)
            d_q = d_q + diag * k
            d_k = d_k + diag * q
            d_dec = [q_in * d_q_in]
            for lv in range(N_LEVELS):
                e = jnp.exp(dec[(lv + 1) * CHUNK:(lv + 2) * CHUNK])
                second, same = _level_masks(lv)
                qt = jnp.where(second, q * e, 0.0)
                kt = jnp.where(second, 0.0, k * e)
                d_qt = _dot(jnp.where(same, d_a, 0.0), kt)
                d_kt = _dot(jnp.where(same, d_at, 0.0), qt)
                d_q = d_q + jnp.where(second, d_qt * e, 0.0)
                d_k = d_k + jnp.where(second, 0.0, d_kt * e)
                d_dec.append(jnp.where(second, qt * d_qt, kt * d_kt))
            d_dec.append(k_end * d_k_end)
            flux = jnp.sum(dst_new * st_prev, axis=0, keepdims=True) * e_last
            d_lf = _exact_dot(stack_t_ref[...], jnp.concatenate(d_dec, axis=0)) + flux
            dst[...] = dst_new * e_last + _dot(d_o.T, q_in)

            d_f = d_lf / f - d_k
            dlb_ref[...] += jnp.sum(d_f * sigm, axis=0, keepdims=True)
            dq_ref[rows, :] = d_q * (HGRN_DIM ** -0.5) * (sq * (1.0 + qh * (1.0 - sq)))
            df_ref[rows, :] = d_f * (1.0 - lbv) * sig * sigm
            di_ref[rows, :] = d_v
            dg_ref[rows, :] = d_gh

    last = n_rb - 1

    def col_spec(tt):
        return _spec((t, HGRN_DIM), lambda h, rb: (last - rb, col0 + HGRN_HEADS * tt + h))

    head_col = _spec((t, HGRN_DIM), lambda h, rb: (last - rb, h))
    chunk_spec = _spec((None, per, CHUNK, CHUNK), lambda h, rb: (h, last - rb, 0, 0))
    vec_spec = _spec((None, 1, HGRN_DIM), lambda h, rb: (h, 0, 0))
    outs = pl.pallas_call(
        body, name=name, grid=(HGRN_HEADS, n_rb),
        in_specs=[col_spec(0), col_spec(1), col_spec(2), col_spec(3), head_col, head_col,
                  chunk_spec, chunk_spec, vec_spec,
                  _spec((1, HGRN_DIM), lambda h, rb: (0, 0)),
                  _spec(stack.shape, lambda h, rb: (0, 0)), _spec(stack_t.shape, lambda h, rb: (0, 0))],
        out_specs=[head_col] * 4 + [vec_spec, vec_spec],
        out_shape=[jax.ShapeDtypeStruct((SEQ, HGRN_WIDTH), F32)] * 4
                  + [jax.ShapeDtypeStruct((HGRN_HEADS, 1, HGRN_DIM), F32)] * 2,
        scratch_shapes=[pltpu.VMEM((CHUNK, CHUNK), F32)],
        compiler_params=_params(2),
    )(proj, proj, proj, proj, d_rec, o_pre, states, scores, lb, gain, stack, stack_t)
    return outs


ANY_SPEC = pl.BlockSpec(memory_space=pl.ANY)


def _my_place():
    return lax.axis_index("x"), lax.axis_index("y"), lax.axis_index("c")


def _other_chips(x, y):
    return [(1 - x, y), (x, 1 - y), (1 - x, 1 - y)]


def _remote(src, dst, send_sem, recv_sem, device):
    return pltpu.make_async_remote_copy(src_ref=src, dst_ref=dst, send_sem=send_sem, recv_sem=recv_sem,
                                        device_id=device, device_id_type=MESH)


def _gather_weights(name, shards):
    n = len(shards)

    def body(*refs):
        ins, outs = refs[:n], refs[n:2 * n]
        ici_send, ici_recv, d2d_send, d2d_recv, local_sems = refs[2 * n:]
        x, y, c = _my_place()
        me = 2 * x + y
        chips = _other_chips(x, y)
        local = [pltpu.make_async_copy(ins[i], outs[i].at[me], local_sems.at[i]) for i in range(n)]
        for cp in local:
            cp.start()

        def half(i, which):
            h = ins[i].shape[0] // 2
            return pl.ds(which * h, h)

        sends = []
        for i in range(n):
            for j, (px, py) in enumerate(chips):
                sends.append(_remote(ins[i].at[half(i, c), :], outs[i].at[me, half(i, c), :],
                                     ici_send.at[3 * i + j], ici_recv.at[3 * i + j], (px, py, c)))
        for cp in sends:
            cp.start()
        for i in range(n):
            for j, (px, py) in enumerate(chips):
                landed = outs[i].at[2 * px + py, half(i, c), :]
                _remote(landed, landed, ici_send.at[3 * i + j], ici_recv.at[3 * i + j], (px, py, c)).wait_recv()
                forward = _remote(landed, landed, d2d_send.at[3 * i + j], d2d_recv.at[3 * i + j], (x, y, 1 - c))
                forward.start()
                sends.append(forward)
        for i in range(n):
            for j, (px, py) in enumerate(chips):
                other = outs[i].at[2 * px + py, half(i, 1 - c), :]
                _remote(other, other, d2d_send.at[3 * i + j], d2d_recv.at[3 * i + j], (x, y, 1 - c)).wait_recv()
        for cp in sends:
            cp.wait_send()
        for cp in local:
            cp.wait()

    return pl.pallas_call(
        body, name=name, in_specs=[ANY_SPEC] * n, out_specs=[ANY_SPEC] * n,
        out_shape=[jax.ShapeDtypeStruct((N_CHIPS,) + s.shape, s.dtype) for s in shards],
        scratch_shapes=[pltpu.SemaphoreType.DMA((3 * n,))] * 4 + [pltpu.SemaphoreType.DMA((n,))],
    )(*shards)


def _exchange_halves(name, grads):
    n = len(grads)

    def body(*refs):
        ins, outs = refs[:n], refs[n:2 * n]
        send_sems, recv_sems = refs[2 * n:]
        x, y, c = _my_place()
        copies = []
        for i in range(n):
            h = ins[i].shape[1] // 2
            copies.append(_remote(ins[i].at[:, pl.ds((1 - c) * h, h), :], outs[i],
                                  send_sems.at[i], recv_sems.at[i], (x, y, 1 - c)))
        for cp in copies:
            cp.start()
        for cp in copies:
            cp.wait()

    return pl.pallas_call(
        body, name=name, in_specs=[ANY_SPEC] * n, out_specs=[ANY_SPEC] * n,
        out_shape=[jax.ShapeDtypeStruct((g.shape[0], g.shape[1] // 2, g.shape[2]), g.dtype) for g in grads],
        scratch_shapes=[pltpu.SemaphoreType.DMA((n,)), pltpu.SemaphoreType.DMA((n,))],
    )(*grads)


def _add_own_half(name, g, received, core):
    n_sh, r, cc = g.shape
    h = r // 2
    th = min(h, 256)
    nb = h // th

    def body(core_ref, g_ref, r_ref, o_ref):
        del core_ref
        o_ref[...] = (g_ref[...] + r_ref[...]).astype(BF16)

    grid_spec = pltpu.PrefetchScalarGridSpec(
        num_scalar_prefetch=1, grid=(n_sh, nb),
        in_specs=[pl.BlockSpec((None, th, cc), lambda j, i, core_ref: (j, core_ref[0] * nb + i, 0)),
                  pl.BlockSpec((None, th, cc), lambda j, i, core_ref: (j, i, 0))],
        out_specs=pl.BlockSpec((None, th, cc), lambda j, i, core_ref: (j, i, 0)))
    return pl.pallas_call(
        body, name=name, grid_spec=grid_spec,
        out_shape=jax.ShapeDtypeStruct((n_sh, h, cc), BF16), compiler_params=_params(2),
    )(core, g, received)


def _exchange_chips(name, parts):
    n = len(parts)

    def body(*refs):
        ins, outs = refs[:n], refs[n:2 * n]
        send_sems, recv_sems, local_sems = refs[2 * n:]
        x, y, c = _my_place()
        me = 2 * x + y
        chips = _other_chips(x, y)
        local = [pltpu.make_async_copy(ins[i].at[me], outs[i].at[me], local_sems.at[i]) for i in range(n)]
        for cp in local:
            cp.start()
        sends = []
        for i in range(n):
            for j, (px, py) in enumerate(chips):
                sends.append(_remote(ins[i].at[2 * px + py], outs[i].at[me], send_sems.at[3 * i + j],
                                     recv_sems.at[3 * i + j], (px, py, c)))
        for cp in sends:
            cp.start()
        for i in range(n):
            for j, (px, py) in enumerate(chips):
                _remote(ins[i].at[me], outs[i].at[2 * px + py], send_sems.at[3 * i + j],
                        recv_sems.at[3 * i + j], (px, py, c)).wait_recv()
        for cp in sends:
            cp.wait_send()
        for cp in local:
            cp.wait()

    return pl.pallas_call(
        body, name=name, in_specs=[ANY_SPEC] * n, out_specs=[ANY_SPEC] * n,
        out_shape=[jax.ShapeDtypeStruct(p.shape, p.dtype) for p in parts],
        scratch_shapes=[pltpu.SemaphoreType.DMA((3 * n,)), pltpu.SemaphoreType.DMA((3 * n,)),
                        pltpu.SemaphoreType.DMA((n,))],
    )(*parts)


def _sum_chips(name, parts):
    n_sh, h, cc = parts.shape
    th = min(h, 256)

    def body(p_ref, o_ref):
        p = [p_ref[j].astype(F32) for j in range(n_sh)]
        o_ref[...] = ((p[0] + p[1]) + p[2]) + p[3]

    return pl.pallas_call(
        body, name=name, grid=(h // th,),
        in_specs=[_spec((n_sh, th, cc), lambda i: (0, i, 0))],
        out_specs=_spec((th, cc), lambda i: (i, 0)),
        out_shape=jax.ShapeDtypeStruct((h, cc), F32), compiler_params=_params(1),
    )(parts)


def _share_halves(halves):
    flat = [t for per_weight in halves for t in per_weight]
    n = len(flat)
    n_w = len(halves)

    def body(*refs):
        ins, outs = refs[:n], refs[n:n + n_w]
        send_sems, recv_sems, local_sems = refs[n + n_w:]
        x, y, c = _my_place()
        local, sends = [], []
        for i in range(n):
            w, l = divmod(i, DEPTH)
            h = ins[i].shape[0]
            local.append(pltpu.make_async_copy(ins[i], outs[w].at[l, pl.ds(c * h, h), :], local_sems.at[i]))
            sends.append(_remote(ins[i], outs[w].at[l, pl.ds(c * h, h), :], send_sems.at[i],
                                 recv_sems.at[i], (x, y, 1 - c)))
        for cp in local + sends:
            cp.start()
        for i in range(n):
            w, l = divmod(i, DEPTH)
            h = ins[i].shape[0]
            _remote(ins[i], outs[w].at[l, pl.ds((1 - c) * h, h), :], send_sems.at[i], recv_sems.at[i],
                    (x, y, 1 - c)).wait_recv()
        for cp in sends:
            cp.wait_send()
        for cp in local:
            cp.wait()

    return pl.pallas_call(
        body, name="share_halves", in_specs=[ANY_SPEC] * n, out_specs=[ANY_SPEC] * n_w,
        out_shape=[jax.ShapeDtypeStruct((DEPTH, 2 * per_weight[0].shape[0], per_weight[0].shape[1]), F32)
                   for per_weight in halves],
        scratch_shapes=[pltpu.SemaphoreType.DMA((n,)), pltpu.SemaphoreType.DMA((n,)),
                        pltpu.SemaphoreType.DMA((n,))],
    )(*flat)


def _all_reduce_small(pack):
    def body(p_ref, o_ref, recv, send_sems, recv_sems):
        x, y, c = _my_place()
        me = 4 * x + 2 * y + c
        recv[me] = p_ref[...]
        peers = []
        for k in range(1, N_DEV):
            px, py, pc = (x + (k >> 2)) % 2, (y + ((k >> 1) & 1)) % 2, (c + (k & 1)) % 2
            peers.append((px, py, pc))
        sends = [_remote(p_ref, recv.at[me], send_sems.at[k], recv_sems.at[k], peer)
                 for k, peer in enumerate(peers)]
        for cp in sends:
            cp.start()
        for k, (px, py, pc) in enumerate(peers):
            _remote(p_ref, recv.at[4 * px + 2 * py + pc], send_sems.at[k], recv_sems.at[k],
                    (px, py, pc)).wait_recv()
        for cp in sends:
            cp.wait_send()
        total = recv[0]
        for d in range(1, N_DEV):
            total = total + recv[d]
        o_ref[...] = total

    vmem = pl.BlockSpec(memory_space=pltpu.VMEM)
    return pl.pallas_call(
        body, name="all_reduce_small", in_specs=[vmem], out_specs=vmem,
        out_shape=jax.ShapeDtypeStruct(pack.shape, F32),
        scratch_shapes=[pltpu.VMEM((N_DEV,) + pack.shape, F32),
                        pltpu.SemaphoreType.DMA((N_DEV - 1,)), pltpu.SemaphoreType.DMA((N_DEV - 1,))],
    )(pack)


def _adamw(name, w, g, m, v):
    r, cc = w.shape
    th = min(r, 256)

    def body(w_ref, g_ref, m_ref, v_ref, d_ref, m_out, v_out):
        gv = g_ref[...]
        m2 = ADAM_B1 * m_ref[...] + (1.0 - ADAM_B1) * gv
        v2 = ADAM_B2 * v_ref[...] + (1.0 - ADAM_B2) * (gv * gv)
        m_hat = m2 / (1.0 - ADAM_B1 ** ADAM_STEP)
        v_hat = v2 / (1.0 - ADAM_B2 ** ADAM_STEP)
        d_ref[...] = -ADAM_LR * (m_hat / (jnp.sqrt(v_hat) + ADAM_EPS) + ADAM_WD * w_ref[...])
        m_out[...] = m2
        v_out[...] = v2

    tile = _spec((th, cc), lambda i: (i, 0))
    return pl.pallas_call(
        body, name=name, grid=(r // th,), in_specs=[tile] * 4, out_specs=[tile] * 3,
        out_shape=[jax.ShapeDtypeStruct((r, cc), F32)] * 3, compiler_params=_params(1),
    )(w, g, m, v)


def _lower_bounds(lb_logits):
    p = jax.nn.softmax(lb_logits.astype(F32), axis=0)
    return jnp.cumsum(p, axis=0) - p[0]


def _row_tile_specs(tm, width):
    return _spec((tm, width), lambda i, j, k: (i, 0))


def _layer_forward(l, x_in, small, weights, consts):
    win, wo, wu, wd = weights
    cos_t, sin_t, stack, _, _ = consts
    tm = MM_TILE
    n_row = SEQ // tm
    saved = {"x_in": x_in}

    h, h_t = _rms_fwd(f"norm_mix{l}", x_in, small["norm_mix"][l][None, :])
    proj = _matmul(f"proj{l}", h, win,
                   _spec((tm, D_MODEL), lambda i, j, k: (i, 0)),
                   _spec((None, D_MODEL, SHARD_IN), lambda i, j, k: (j, 0, 0)),
                   (SEQ, IN_W), F32, _spec((tm, SHARD_IN), lambda i, j, k: (i, j)),
                   (n_row, N_CHIPS, 1), (tm, SHARD_IN))
    saved.update(h_t=h_t, proj=proj)

    qkv = _attn_prep(f"attn_prep{l}", proj, cos_t, sin_t)
    outs, lses = [], []
    for p, d in enumerate(DILATIONS):
        o, lse = _attn_fwd(f"attn_fwd{l}_{d}", *qkv[p], SEQ // d // SPAN)
        outs.append(o)
        lses.append(lse)
    an, an_t, attn, lse = _attn_merge(f"attn_merge{l}", outs, lses, small["attn_out_gain"][l][None, :])
    saved.update(qkv=qkv, an_t=an_t, attn=attn, lse=lse)

    lb3 = small["lower"][l].reshape(HGRN_HEADS, 1, HGRN_DIM)
    rec, rec_t, o_pre, states, scores = _hgrn_fwd(f"hgrn_fwd{l}", proj, lb3,
                                                  small["hgrn_out_gain"][l][None, :], stack)
    saved.update(rec_t=rec_t, o_pre=o_pre, states=states, scores=scores, lb3=lb3)

    def out_proj(name, a, part, resid):
        return _matmul(name, a, wo,
                       _spec((tm, SHARD_OUT), lambda i, j, k: (i, k)),
                       _spec((None, SHARD_OUT, D_MODEL), lambda i, j, k: (k + 2 * part, 0, 0)),
                       (SEQ, D_MODEL), F32, _spec((tm, D_MODEL), lambda i, j, k: (i, 0)),
                       (n_row, 1, 2), (tm, D_MODEL),
                       extra=resid, extra_spec=_spec((tm, D_MODEL), lambda i, j, k: (i, 0)), epilogue="add")

    x_mid = out_proj(f"out_rec{l}", rec, 1, out_proj(f"out_attn{l}", an, 0, x_in))
    saved["x_mid"] = x_mid

    h2, h2_t = _rms_fwd(f"norm_mlp{l}", x_mid, small["norm_mlp"][l][None, :])
    u = _matmul(f"up{l}", h2, wu,
                _spec((tm, D_MODEL), lambda i, j, k: (i, 0)),
                _spec((None, D_MODEL, SHARD_MLP), lambda i, j, k: (j, 0, 0)),
                (SEQ, MLP_HIDDEN), F32, _spec((tm, SHARD_MLP), lambda i, j, k: (i, j)),
                (n_row, N_CHIPS, 1), (tm, SHARD_MLP))
    a, a_t = _relu2(f"relu2_{l}", u)
    x_out = _matmul(f"down{l}", a, wd,
                    _spec((tm, SHARD_MLP), lambda i, j, k: (i, k)),
                    _spec((None, SHARD_MLP, D_MODEL), lambda i, j, k: (k, 0, 0)),
                    (SEQ, D_MODEL), F32, _spec((tm, D_MODEL), lambda i, j, k: (i, 0)),
                    (n_row, 1, N_CHIPS), (tm, D_MODEL),
                    extra=x_mid, extra_spec=_spec((tm, D_MODEL), lambda i, j, k: (i, 0)), epilogue="add")
    saved.update(h2_t=h2_t, u=u, a_t=a_t)
    return x_out, saved


def _layer_backward(l, dx, saved, small, weights, consts):
    win, wo, wu, wd = weights
    cos_t, sin_t, stack, stack_t, head_sum = consts
    tm = MM_TILE
    n_row = SEQ // tm
    n_k = SEQ // tm

    du = _matmul(f"d_u{l}", dx, wd,
                 _spec((tm, D_MODEL), lambda i, j, k: (i, 0)),
                 _spec((None, SHARD_MLP, D_MODEL), lambda i, j, k: (j, 0, 0)),
                 (SEQ, MLP_HIDDEN), BF16, _spec((tm, SHARD_MLP), lambda i, j, k: (i, j)),
                 (n_row, N_CHIPS, 1), (tm, SHARD_MLP), nt=True,
                 extra=saved["u"], extra_spec=_spec((tm, SHARD_MLP), lambda i, j, k: (i, j)),
                 epilogue="relu2_grad")
    d_wd = _matmul(f"d_wdown{l}", saved["a_t"], dx,
                   _spec((SHARD_MLP, tm), lambda i, j, k: (i, k)),
                   _spec((tm, D_MODEL), lambda i, j, k: (k, 0)),
                   (N_CHIPS, SHARD_MLP, D_MODEL), F32, _spec((None, SHARD_MLP, D_MODEL), lambda i, j, k: (i, 0, 0)),
                   (N_CHIPS, 1, n_k), (SHARD_MLP, D_MODEL))
    dh2 = _matmul(f"d_h2_{l}", du, wu,
                  _spec((tm, SHARD_MLP), lambda i, j, k: (i, k)),
                  _spec((None, D_MODEL, SHARD_MLP), lambda i, j, k: (k, 0, 0)),
                  (SEQ, D_MODEL), F32, _spec((tm, D_MODEL), lambda i, j, k: (i, 0)),
                  (n_row, 1, N_CHIPS), (tm, D_MODEL), nt=True)
    d_wu = _matmul(f"d_wup{l}", saved["h2_t"], du,
                   _spec((D_MODEL, tm), lambda i, j, k: (0, k)),
                   _spec((tm, SHARD_MLP), lambda i, j, k: (k, j)),
                   (N_CHIPS, D_MODEL, SHARD_MLP), F32, _spec((None, D_MODEL, SHARD_MLP), lambda i, j, k: (j, 0, 0)),
                   (1, N_CHIPS, n_k), (D_MODEL, SHARD_MLP))
    dxm, dg_mlp = _rms_bwd(f"norm_mlp_bwd{l}", dh2, saved["x_mid"], small["norm_mlp"][l][None, :], dx)

    def d_mixed(name, part):
        return _matmul(name, dxm, wo,
                       _spec((tm, D_MODEL), lambda i, j, k: (i, 0)),
                       _spec((None, SHARD_OUT, D_MODEL), lambda i, j, k: (j + 2 * part, 0, 0)),
                       (SEQ, ATTN_WIDTH), F32, _spec((tm, SHARD_OUT), lambda i, j, k: (i, j)),
                       (n_row, 2, 1), (tm, SHARD_OUT), nt=True)

    def d_wout(name, a_t):
        return _matmul(name, a_t, dxm,
                       _spec((SHARD_OUT, tm), lambda i, j, k: (i, k)),
                       _spec((tm, D_MODEL), lambda i, j, k: (k, 0)),
                       (2, SHARD_OUT, D_MODEL), F32, _spec((None, SHARD_OUT, D_MODEL), lambda i, j, k: (i, 0, 0)),
                       (2, 1, n_k), (SHARD_OUT, D_MODEL))

    d_an = d_mixed(f"d_attn_n{l}", 0)
    d_rec = d_mixed(f"d_rec{l}", 1)
    d_wo = jnp.concatenate([d_wout(f"d_wout_attn{l}", saved["an_t"]),
                            d_wout(f"d_wout_rec{l}", saved["rec_t"])], axis=0)

    d_out, delta, lses, dg_attn = _attn_bwd_prep(f"attn_bwd_prep{l}", d_an, saved["attn"], saved["lse"],
                                                 small["attn_out_gain"][l][None, :], head_sum)
    grads = []
    for p, d in enumerate(DILATIONS):
        grads.append(_attn_bwd(f"attn_bwd{l}_{d}", *saved["qkv"][p], d_out[p], delta[p], lses[p],
                               SEQ // d // SPAN))
    dp_attn = _attn_bwd_post(f"attn_bwd_post{l}", grads, cos_t, sin_t)

    dq_h, df_h, di_h, dg_h, d_lower, dg_hgrn = _hgrn_bwd(
        f"hgrn_bwd{l}", saved["proj"], d_rec, saved["o_pre"], saved["states"], saved["scores"],
        saved["lb3"], small["hgrn_out_gain"][l][None, :], stack, stack_t)
    dproj = jnp.concatenate([dp_attn, dq_h, df_h, di_h, dg_h], axis=1)

    dh = _matmul(f"d_h{l}", dproj, win,
                 _spec((tm, SHARD_IN), lambda i, j, k: (i, k)),
                 _spec((None, D_MODEL, SHARD_IN), lambda i, j, k: (k, 0, 0)),
                 (SEQ, D_MODEL), F32, _spec((tm, D_MODEL), lambda i, j, k: (i, 0)),
                 (n_row, 1, N_CHIPS), (tm, D_MODEL), nt=True)
    d_win = _matmul(f"d_win{l}", saved["h_t"], dproj,
                    _spec((D_MODEL, tm), lambda i, j, k: (0, k)),
                    _spec((tm, SHARD_IN), lambda i, j, k: (k, j)),
                    (N_CHIPS, D_MODEL, SHARD_IN), F32, _spec((None, D_MODEL, SHARD_IN), lambda i, j, k: (j, 0, 0)),
                    (1, N_CHIPS, n_k), (D_MODEL, SHARD_IN))
    dx_in, dg_mix = _rms_bwd(f"norm_mix_bwd{l}", dh, saved["x_in"], small["norm_mix"][l][None, :], dxm)

    small_grads = {"norm_mix": dg_mix[0], "attn_out_gain": dg_attn[0],
                   "lower": d_lower.reshape(HGRN_WIDTH),
                   "hgrn_out_gain": jnp.sum(dg_hgrn, axis=0).reshape(HGRN_DIM), "norm_mlp": dg_mlp[0]}
    return dx_in, (d_win, d_wo, d_wu, d_wd), small_grads


def _local_step(xs, target, small, get_weights, on_grads):
    consts = _rope_tables() + _hgrn_consts() + (_head_sum_matrix(),)
    stream = xs
    saved, weights = [], []
    for l in range(DEPTH):
        weights.append(get_weights(l, stream))
        stream, s = _layer_forward(l, stream, small, weights[l], consts)
        saved.append(s)
    dx, dg_final, loss = _loss_head(stream, small["norm_final"][None, :], target)
    small_grads = [None] * DEPTH
    for l in reversed(range(DEPTH)):
        dx, big, small_grads[l] = _layer_backward(l, dx, saved[l], small, weights[l], consts)
        on_grads(l, big)
    return loss, dx, dg_final[0], small_grads


def _pack_small(norm_mix, attn_out_gain, lb, hgrn_out_gain, norm_mlp, norm_final, last_row):
    rows = [norm_mix, attn_out_gain.reshape(1, D_MODEL), lb.reshape(1, D_MODEL),
            jnp.pad(hgrn_out_gain.reshape(1, DEPTH * HGRN_DIM), ((0, 0), (0, D_MODEL - DEPTH * HGRN_DIM))),
            norm_mlp, norm_final.reshape(1, D_MODEL), last_row.reshape(1, D_MODEL)]
    pack = jnp.concatenate(rows, axis=0)
    return jnp.pad(pack, ((0, PACK_ROWS - pack.shape[0]), (0, 0)))


def _unpack_small(pack):
    return (pack[0:2], pack[2].reshape(DEPTH, ATTN_WIDTH), pack[3].reshape(DEPTH, HGRN_WIDTH),
            pack[4, :DEPTH * HGRN_DIM].reshape(DEPTH, HGRN_DIM), pack[5:7], pack[7], pack[8])


def kernel(x, norm_mix, w_in, attn_out_gain, hgrn_lb_logits, hgrn_out_gain, w_out, norm_mlp, w_up, w_down, norm_final, loss_target, m_norm_mix, m_w_in, m_attn_out_gain, m_hgrn_lb_logits, m_hgrn_out_gain, m_w_out, m_norm_mlp, m_w_up, m_w_down, m_norm_final, v_norm_mix, v_w_in, v_attn_out_gain, v_hgrn_lb_logits, v_hgrn_out_gain, v_w_out, v_norm_mlp, v_w_up, v_w_down, v_norm_final):
    core = lax.axis_index("c").astype(jnp.int32).reshape(1)
    lower, lower_vjp = jax.vjp(_lower_bounds, hgrn_lb_logits)
    small = {"norm_mix": norm_mix, "attn_out_gain": attn_out_gain, "lower": lower,
             "hgrn_out_gain": hgrn_out_gain, "norm_mlp": norm_mlp, "norm_final": norm_final}
    big_w = (w_in, w_out, w_up, w_down)

    def get_weights(l, stream):
        del stream
        return _gather_weights(f"gather_weights{l}", [w[l].astype(BF16) for w in big_w])

    reduced = [None] * DEPTH

    def on_grads(l, grads):
        received = _exchange_halves(f"exchange_halves{l}", grads)
        halves = [_add_own_half(f"add_halves{l}_{i}", g, r, core)
                  for i, (g, r) in enumerate(zip(grads, received))]
        from_chips = _exchange_chips(f"exchange_chips{l}", halves)
        reduced[l] = [_sum_chips(f"sum_chips{l}_{i}", p) for i, p in enumerate(from_chips)]

    loss, dx, dg_final, sg = _local_step(x[0], loss_target[0], small, get_weights, on_grads)

    stack2 = lambda key: jnp.stack([sg[l][key] for l in range(DEPTH)])
    pack = _pack_small(stack2("norm_mix"), stack2("attn_out_gain"), stack2("lower"), stack2("hgrn_out_gain"),
                       stack2("norm_mlp"), dg_final, jnp.broadcast_to(loss[0, 0], (D_MODEL,)))
    g_mix, g_attn, g_lower, g_hgrn, g_mlp, g_final, loss_row = _unpack_small(_all_reduce_small(pack))
    (g_logits,) = lower_vjp(g_lower)

    zeros_row = jnp.zeros((D_MODEL,), F32)
    small_w = (norm_mix, attn_out_gain, hgrn_lb_logits, hgrn_out_gain, norm_mlp, norm_final)
    small_m = (m_norm_mix, m_attn_out_gain, m_hgrn_lb_logits, m_hgrn_out_gain, m_norm_mlp, m_norm_final)
    small_v = (v_norm_mix, v_attn_out_gain, v_hgrn_lb_logits, v_hgrn_out_gain, v_norm_mlp, v_norm_final)
    small_g = (g_mix, g_attn, g_logits, g_hgrn, g_mlp, g_final)
    packs = [_pack_small(*t, zeros_row) for t in (small_w, small_g, small_m, small_v)]
    small_delta, small_new_m, small_new_v = [_unpack_small(p)[:6] for p in _adamw("adamw_small", *packs)]

    big_g = _share_halves([[reduced[l][w] for l in range(DEPTH)] for w in range(4)])

    big_m = (m_w_in, m_w_out, m_w_up, m_w_down)
    big_v = (v_w_in, v_w_out, v_w_up, v_w_down)
    big_delta, big_new_m, big_new_v = [], [], []
    for i, name in enumerate(("w_in", "w_out", "w_up", "w_down")):
        shape = big_w[i].shape
        flat = lambda arr: arr.reshape(shape[0] * shape[1], shape[2])
        d, m2, v2 = _adamw(f"adamw_{name}", flat(big_w[i]), flat(big_g[i]), flat(big_m[i]), flat(big_v[i]))
        big_delta.append(d.reshape(shape))
        big_new_m.append(m2.reshape(shape))
        big_new_v.append(v2.reshape(shape))

    def ordered(small6, big4):
        mix, attn, lbl, hg, mlp, fin = small6
        return (mix, big4[0], attn, lbl, hg, big4[1], mlp, big4[2], big4[3], fin)

    return ((loss_row[0], dx[None]) + ordered(small_g, big_g) + ordered(small_delta, big_delta)
            + ordered(small_new_m, big_new_m) + ordered(small_new_v, big_new_v))
```

```python
import functools
import math

import numpy as np
import jax
import jax.numpy as jnp
from jax import lax
from jax.experimental import pallas as pl
from jax.experimental.pallas import tpu as pltpu

F32 = jnp.float32
BF16 = jnp.bfloat16
MESH = pl.DeviceIdType.MESH

SEQ = 4096
D_MODEL = 1024
DEPTH = 2
ATTN_WIDTH = 512
HEAD_DIM = 64
HGRN_HEADS = 4
HGRN_DIM = 128
HGRN_WIDTH = 512
IN_W = 3584
MLP_HIDDEN = 4096
N_CHIPS = 4
N_DEV = 8
SHARD_IN = IN_W // N_CHIPS
SHARD_OUT = D_MODEL // N_CHIPS
SHARD_MLP = MLP_HIDDEN // N_CHIPS
DILATIONS = (1, 4, 16)
SPAN = 128
ROPE_THETA = 10000.0
NORM_EPS = 1e-6
MASK_VALUE = -1e30
CHUNK = 128
ROW_TILE = 512
MM_TILE = 1024
VMEM_LIMIT = 52 * 1024 * 1024

ADAM_LR = 0.001
ADAM_B1 = 0.9
ADAM_B2 = 0.999
ADAM_EPS = 1e-08
ADAM_WD = 0.01
ADAM_STEP = 10

PACK_ROWS = 16


def _params(n_axes):
    return pltpu.CompilerParams(dimension_semantics=("arbitrary",) * n_axes,
                                vmem_limit_bytes=VMEM_LIMIT)


def _dot(a, b):
    return jnp.dot(a.astype(BF16), b.astype(BF16), preferred_element_type=F32)


def _dot_nt(a, b):
    return lax.dot_general(a.astype(BF16), b.astype(BF16), (((1,), (1,)), ((), ())),
                           preferred_element_type=F32)


def _sigmoid(x):
    return 1.0 / (1.0 + jnp.exp(-x))


def _matmul(name, a, b, a_spec, b_spec, out_shape, out_dtype, out_spec, grid, acc_shape,
            nt=False, extra=None, extra_spec=None, epilogue="none"):
    nk = grid[2]

    def body(*refs):
        if extra is None:
            a_ref, b_ref, o_ref, acc = refs
            e_ref = None
        else:
            a_ref, b_ref, e_ref, o_ref, acc = refs
        kk = pl.program_id(2)

        @pl.when(kk == 0)
        def _():
            acc[...] = jnp.zeros_like(acc)

        if nt:
            acc[...] += _dot_nt(a_ref[...], b_ref[...])
        else:
            acc[...] += _dot(a_ref[...], b_ref[...])

        @pl.when(kk == nk - 1)
        def _():
            r = acc[...]
            if epilogue == "add":
                r = r + e_ref[...]
            elif epilogue == "relu2_grad":
                r = r * (2.0 * jnp.maximum(e_ref[...], 0.0))
            o_ref[...] = r.astype(o_ref.dtype)

    in_specs = [a_spec, b_spec] + ([] if extra is None else [extra_spec])
    args = (a, b) + (() if extra is None else (extra,))
    return pl.pallas_call(
        body, name=name, grid=grid, in_specs=in_specs, out_specs=out_spec,
        out_shape=jax.ShapeDtypeStruct(out_shape, out_dtype),
        scratch_shapes=[pltpu.VMEM(acc_shape, F32)],
        compiler_params=_params(3),
    )(*args)


def _spec(shape, index_map):
    return pl.BlockSpec(shape, index_map)


def _rms_fwd(name, x, gain):
    s, d = x.shape
    t = ROW_TILE

    def body(x_ref, g_ref, h_ref, ht_ref):
        xv = x_ref[...]
        r = lax.rsqrt(jnp.mean(xv * xv, axis=1, keepdims=True) + NORM_EPS)
        h = xv * r * g_ref[...]
        h_ref[...] = h.astype(BF16)
        ht_ref[...] = h.T.astype(BF16)

    return pl.pallas_call(
        body, name=name, grid=(s // t,),
        in_specs=[_spec((t, d), lambda i: (i, 0)), _spec((1, d), lambda i: (0, 0))],
        out_specs=[_spec((t, d), lambda i: (i, 0)), _spec((d, t), lambda i: (0, i))],
        out_shape=[jax.ShapeDtypeStruct((s, d), BF16), jax.ShapeDtypeStruct((d, s), BF16)],
        compiler_params=_params(1),
    )(x, gain)


def _rms_bwd(name, dh, x, gain, dres):
    s, d = x.shape
    t = ROW_TILE

    def body(dh_ref, x_ref, g_ref, dres_ref, dx_ref, dg_ref):
        @pl.when(pl.program_id(0) == 0)
        def _():
            dg_ref[...] = jnp.zeros_like(dg_ref)

        xv = x_ref[...]
        dhv = dh_ref[...]
        r = lax.rsqrt(jnp.mean(xv * xv, axis=1, keepdims=True) + NORM_EPS)
        xhat = xv * r
        dhg = dhv * g_ref[...]
        proj = jnp.mean(dhg * xhat, axis=1, keepdims=True)
        dx_ref[...] = dres_ref[...] + r * (dhg - xhat * proj)
        dg_ref[...] += jnp.sum(dhv * xhat, axis=0, keepdims=True)

    return pl.pallas_call(
        body, name=name, grid=(s // t,),
        in_specs=[_spec((t, d), lambda i: (i, 0)), _spec((t, d), lambda i: (i, 0)),
                  _spec((1, d), lambda i: (0, 0)), _spec((t, d), lambda i: (i, 0))],
        out_specs=[_spec((t, d), lambda i: (i, 0)), _spec((1, d), lambda i: (0, 0))],
        out_shape=[jax.ShapeDtypeStruct((s, d), F32), jax.ShapeDtypeStruct((1, d), F32)],
        compiler_params=_params(1),
    )(dh, x, gain, dres)


def _loss_head(x, gain, target):
    s, d = x.shape
    t = ROW_TILE
    n_steps = s // t

    def body(x_ref, g_ref, t_ref, dx_ref, dg_ref, loss_ref, acc):
        i = pl.program_id(0)

        @pl.when(i == 0)
        def _():
            dg_ref[...] = jnp.zeros_like(dg_ref)
            acc[...] = jnp.zeros_like(acc)

        xv = x_ref[...]
        g = g_ref[...]
        r = lax.rsqrt(jnp.mean(xv * xv, axis=1, keepdims=True) + NORM_EPS)
        xhat = xv * r
        err = xhat * g - t_ref[...]
        acc[...] += jnp.sum(err * err, axis=0, keepdims=True)
        dy = err * (1.0 / d)
        dyg = dy * g
        proj = jnp.mean(dyg * xhat, axis=1, keepdims=True)
        dx_ref[...] = r * (dyg - xhat * proj)
        dg_ref[...] += jnp.sum(dy * xhat, axis=0, keepdims=True)

        @pl.when(i == n_steps - 1)
        def _():
            total = jnp.sum(acc[...], axis=1, keepdims=True) * (0.5 / d)
            loss_ref[...] = jnp.broadcast_to(total, loss_ref.shape)

    return pl.pallas_call(
        body, name="loss_head", grid=(n_steps,),
        in_specs=[_spec((t, d), lambda i: (i, 0)), _spec((1, d), lambda i: (0, 0)),
                  _spec((t, d), lambda i: (i, 0))],
        out_specs=[_spec((t, d), lambda i: (i, 0)), _spec((1, d), lambda i: (0, 0)),
                   _spec((1, 128), lambda i: (0, 0))],
        out_shape=[jax.ShapeDtypeStruct((s, d), F32), jax.ShapeDtypeStruct((1, d), F32),
                   jax.ShapeDtypeStruct((1, 128), F32)],
        scratch_shapes=[pltpu.VMEM((1, d), F32)],
        compiler_params=_params(1),
    )(x, gain, target)


def _relu2(name, u):
    s, f = u.shape
    t = ROW_TILE

    def body(u_ref, a_ref, at_ref):
        r = jnp.maximum(u_ref[...], 0.0)
        a = r * r
        a_ref[...] = a.astype(BF16)
        at_ref[...] = a.T.astype(BF16)

    return pl.pallas_call(
        body, name=name, grid=(s // t, f // t),
        in_specs=[_spec((t, t), lambda i, j: (i, j))],
        out_specs=[_spec((t, t), lambda i, j: (i, j)), _spec((t, t), lambda i, j: (j, i))],
        out_shape=[jax.ShapeDtypeStruct((s, f), BF16), jax.ShapeDtypeStruct((f, s), BF16)],
        compiler_params=_params(2),
    )(u)


def _rope_tables():
    half = HEAD_DIM // 2
    inv_freq = ROPE_THETA ** (-jnp.arange(half, dtype=F32) / half)
    ang = jnp.arange(SEQ, dtype=jnp.int32).astype(F32)[:, None] * inv_freq[None, :]
    cos, sin = jnp.cos(ang), jnp.sin(ang)
    cos_t = jnp.concatenate([cos, cos, cos, cos], axis=1)
    sin_t = jnp.concatenate([-sin, sin, -sin, sin], axis=1)
    return cos_t, sin_t


def _swap_halves(x):
    lane = lax.broadcasted_iota(jnp.int32, x.shape, 1)
    first = (lane % HEAD_DIM) < (HEAD_DIM // 2)
    return jnp.where(first, pltpu.roll(x, 128 - HEAD_DIM // 2, 1), pltpu.roll(x, HEAD_DIM // 2, 1))


def _permuted_specs(t, width):
    specs = [_spec((t, width), lambda i: (i, 0))]
    for d in DILATIONS[1:]:
        specs.append(_spec((d, t // d, width), lambda i: (0, i, 0)))
    return specs


def _permuted_shapes(width, dtype):
    shapes = [jax.ShapeDtypeStruct((SEQ, width), dtype)]
    for d in DILATIONS[1:]:
        shapes.append(jax.ShapeDtypeStruct((d, SEQ // d, width), dtype))
    return shapes


def _attn_prep(name, proj, cos_t, sin_t):
    t = ROW_TILE
    w = ATTN_WIDTH

    def body(q_ref, k_ref, v_ref, cos_ref, sin_ref, *rest):
        outs, scr = rest[:9], rest[9]
        cosv, sinv = cos_ref[...], sin_ref[...]
        for a, (src, roped, scale) in enumerate(((q_ref, True, HEAD_DIM ** -0.5),
                                                 (k_ref, True, 1.0), (v_ref, False, 1.0))):
            o1, o4, o16 = outs[3 * a:3 * a + 3]
            for cb in range(w // 128):
                cols = slice(cb * 128, (cb + 1) * 128)
                val = src[:, cols]
                if roped:
                    val = (val * cosv + _swap_halves(val) * sinv) * scale
                scr[...] = val
                o1[:, cols] = val.astype(BF16)
                for o_ref, d in ((o4, 4), (o16, 16)):
                    for r in range(d):
                        o_ref[r, :, cols] = scr[pl.ds(r, t // d, stride=d), :].astype(BF16)

    out_specs = _permuted_specs(t, w) * 3
    out_shape = _permuted_shapes(w, BF16) * 3
    outs = pl.pallas_call(
        body, name=name, grid=(SEQ // t,),
        in_specs=[_spec((t, w), lambda i: (i, 0)), _spec((t, w), lambda i: (i, 1)),
                  _spec((t, w), lambda i: (i, 2)),
                  _spec((t, 128), lambda i: (i, 0)), _spec((t, 128), lambda i: (i, 0))],
        out_specs=out_specs, out_shape=out_shape,
        scratch_shapes=[pltpu.VMEM((t, 128), F32)],
        compiler_params=_params(1),
    )(proj, proj, proj, cos_t, sin_t)
    q, k, v = outs[0:3], outs[3:6], outs[6:9]
    flat = lambda arr: arr.reshape(SEQ, w)
    return [(flat(q[p]), flat(k[p]), flat(v[p])) for p in range(3)]


def _band_masks():
    row = lax.broadcasted_iota(jnp.int32, (SPAN, SPAN), 0)
    col = lax.broadcasted_iota(jnp.int32, (SPAN, SPAN), 1)
    return col >= row, col <= row, col < HEAD_DIM


def _attn_fwd(name, q, k, v, seg_blocks):
    n_blocks = SEQ // SPAN

    def body(q_ref, k_ref, v_ref, o_ref, lse_ref):
        prev_band, cur_ok, head0 = _band_masks()

        def step(b, carry):
            r0 = pl.multiple_of(b * SPAN, SPAN)
            p0 = pl.multiple_of(jnp.maximum(b - 1, 0) * SPAN, SPAN)
            qb = q_ref[pl.ds(r0, SPAN), :]
            kc, vc = k_ref[pl.ds(r0, SPAN), :], v_ref[pl.ds(r0, SPAN), :]
            kp, vp = k_ref[pl.ds(p0, SPAN), :], v_ref[pl.ds(p0, SPAN), :]
            prev_ok = prev_band & ((b % seg_blocks) != 0)
            o_acc = jnp.zeros((SPAN, 128), F32)
            lse_acc = jnp.zeros((SPAN, 128), F32)
            for mh in (head0, ~head0):
                qh = jnp.where(mh, qb, jnp.zeros_like(qb))
                sp = jnp.where(prev_ok, _dot_nt(qh, kp), MASK_VALUE)
                sc = jnp.where(cur_ok, _dot_nt(qh, kc), MASK_VALUE)
                m = jnp.maximum(jnp.max(sp, axis=1, keepdims=True), jnp.max(sc, axis=1, keepdims=True))
                pp = jnp.where(prev_ok, jnp.exp(sp - m), 0.0)
                pc = jnp.where(cur_ok, jnp.exp(sc - m), 0.0)
                l = jnp.sum(pp, axis=1, keepdims=True) + jnp.sum(pc, axis=1, keepdims=True)
                vph = jnp.where(mh, vp, jnp.zeros_like(vp))
                vch = jnp.where(mh, vc, jnp.zeros_like(vc))
                o_acc = o_acc + (_dot(pp, vph) + _dot(pc, vch)) / l
                lse_acc = jnp.where(mh, m + jnp.log(l), lse_acc)
            o_ref[pl.ds(r0, SPAN), :] = o_acc
            lse_ref[pl.ds(r0, SPAN), :] = lse_acc
            return carry

        lax.fori_loop(0, n_blocks, step, 0)

    col = _spec((SEQ, 128), lambda j: (0, j))
    return pl.pallas_call(
        body, name=name, grid=(ATTN_WIDTH // 128,),
        in_specs=[col, col, col], out_specs=[col, col],
        out_shape=[jax.ShapeDtypeStruct((SEQ, ATTN_WIDTH), F32)] * 2,
        compiler_params=_params(1),
    )(q, k, v)


def _unpermute(dst, src_ref, d, cols):
    n = dst.shape[0] // d
    for r in range(d):
        dst[pl.ds(r, n, stride=d), :] = src_ref[r, :, cols]


def _attn_merge(name, outs, lses, gain):
    t = ROW_TILE
    w = ATTN_WIDTH

    def body(o1, o4, o16, l1, l4, l16, g_ref, an_ref, ant_ref, attn_ref, lse_ref, so4, so16, sl4, sl16):
        for cb in range(w // 128):
            cols = slice(cb * 128, (cb + 1) * 128)
            _unpermute(so4, o4, 4, cols)
            _unpermute(so16, o16, 16, cols)
            _unpermute(sl4, l4, 4, cols)
            _unpermute(sl16, l16, 16, cols)
            la, lb, lc = l1[:, cols], sl4[...], sl16[...]
            m = jnp.maximum(jnp.maximum(la, lb), lc)
            ea, eb, ec = jnp.exp(la - m), jnp.exp(lb - m), jnp.exp(lc - m)
            tot = ea + eb + ec
            attn_ref[:, cols] = (ea * o1[:, cols] + eb * so4[...] + ec * so16[...]) / tot
            lse_ref[:, cols] = m + jnp.log(tot)
        attn = attn_ref[...]
        r = lax.rsqrt(jnp.mean(attn * attn, axis=1, keepdims=True) + NORM_EPS)
        an = attn * r * g_ref[...]
        an_ref[...] = an.astype(BF16)
        ant_ref[...] = an.T.astype(BF16)

    views = lambda arrs: [arrs[0], arrs[1].reshape(4, SEQ // 4, w), arrs[2].reshape(16, SEQ // 16, w)]
    row = _spec((t, w), lambda i: (i, 0))
    return pl.pallas_call(
        body, name=name, grid=(SEQ // t,),
        in_specs=_permuted_specs(t, w) * 2 + [_spec((1, w), lambda i: (0, 0))],
        out_specs=[row, _spec((w, t), lambda i: (0, i)), row, row],
        out_shape=[jax.ShapeDtypeStruct((SEQ, w), BF16), jax.ShapeDtypeStruct((w, SEQ), BF16),
                   jax.ShapeDtypeStruct((SEQ, w), F32), jax.ShapeDtypeStruct((SEQ, w), F32)],
        scratch_shapes=[pltpu.VMEM((t, 128), F32)] * 4,
        compiler_params=_params(1),
    )(*views(outs), *views(lses), gain)


def _head_sum_matrix():
    i = np.arange(ATTN_WIDTH)
    return jnp.asarray((i[:, None] // HEAD_DIM) == (i[None, :] // HEAD_DIM), dtype=F32)


def _attn_bwd_prep(name, d_an, attn, lse, gain, head_sum):
    t = ROW_TILE
    w = ATTN_WIDTH

    def body(dan_ref, attn_ref, lse_ref, g_ref, hs_ref, *rest):
        (do1, do4, do16, dl1, dl4, dl16, ls4, ls16, dg_ref), (sdo, sdl, sls) = rest[:9], rest[9:]

        @pl.when(pl.program_id(0) == 0)
        def _():
            dg_ref[...] = jnp.zeros_like(dg_ref)

        attn = attn_ref[...]
        dan = dan_ref[...]
        r = lax.rsqrt(jnp.mean(attn * attn, axis=1, keepdims=True) + NORM_EPS)
        xhat = attn * r
        dg_ref[...] += jnp.sum(dan * xhat, axis=0, keepdims=True)
        dang = dan * g_ref[...]
        d_o = r * (dang - xhat * jnp.mean(dang * xhat, axis=1, keepdims=True))
        delta = jnp.dot(d_o * attn, hs_ref[...], preferred_element_type=F32,
                        precision=lax.Precision.HIGHEST)
        do1[...] = d_o.astype(BF16)
        dl1[...] = delta
        for cb in range(w // 128):
            cols = slice(cb * 128, (cb + 1) * 128)
            sdo[...] = d_o[:, cols]
            sdl[...] = delta[:, cols]
            sls[...] = lse_ref[:, cols]
            for d, o_do, o_dl, o_ls in ((4, do4, dl4, ls4), (16, do16, dl16, ls16)):
                for rr in range(d):
                    rows = pl.ds(rr, t // d, stride=d)
                    o_do[rr, :, cols] = sdo[rows, :].astype(BF16)
                    o_dl[rr, :, cols] = sdl[rows, :]
                    o_ls[rr, :, cols] = sls[rows, :]

    row = _spec((t, w), lambda i: (i, 0))
    perm = _permuted_specs(t, w)
    outs = pl.pallas_call(
        body, name=name, grid=(SEQ // t,),
        in_specs=[row, row, row, _spec((1, w), lambda i: (0, 0)), _spec((w, w), lambda i: (0, 0))],
        out_specs=perm + perm + perm[1:] + [_spec((1, w), lambda i: (0, 0))],
        out_shape=(_permuted_shapes(w, BF16) + _permuted_shapes(w, F32) + _permuted_shapes(w, F32)[1:]
                   + [jax.ShapeDtypeStruct((1, w), F32)]),
        scratch_shapes=[pltpu.VMEM((t, 128), F32)] * 3,
        compiler_params=_params(1),
    )(d_an, attn, lse, gain, head_sum)
    flat = lambda arr: arr.reshape(SEQ, w)
    d_out = [flat(a) for a in outs[0:3]]
    delta = [flat(a) for a in outs[3:6]]
    lses = [lse, flat(outs[6]), flat(outs[7])]
    return d_out, delta, lses, outs[8]


def _attn_bwd(name, q, k, v, d_out, delta, lse, seg_blocks):
    n_blocks = SEQ // SPAN

    def body(q_ref, k_ref, v_ref, do_ref, dl_ref, lse_ref, dq_ref, dk_ref, dv_ref):
        prev_band, cur_ok, head0 = _band_masks()
        dk_ref[...] = jnp.zeros_like(dk_ref)
        dv_ref[...] = jnp.zeros_like(dv_ref)

        def step(b, carry):
            r0 = pl.multiple_of(b * SPAN, SPAN)
            p0 = pl.multiple_of(jnp.maximum(b - 1, 0) * SPAN, SPAN)
            cur, prev = pl.ds(r0, SPAN), pl.ds(p0, SPAN)
            qb, dob = q_ref[cur, :], do_ref[cur, :]
            kc, vc, kp, vp = k_ref[cur, :], v_ref[cur, :], k_ref[prev, :], v_ref[prev, :]
            lse_b, dl_b = lse_ref[cur, :], dl_ref[cur, :]
            prev_ok = prev_band & ((b % seg_blocks) != 0)
            dq = jnp.zeros((SPAN, 128), F32)
            dkp = jnp.zeros((SPAN, 128), F32)
            dkc = jnp.zeros((SPAN, 128), F32)
            dvp = jnp.zeros((SPAN, 128), F32)
            dvc = jnp.zeros((SPAN, 128), F32)
            for h, mh in enumerate((head0, ~head0)):
                lane = h * HEAD_DIM
                lse_h, dl_h = lse_b[:, lane:lane + 1], dl_b[:, lane:lane + 1]
                qh = jnp.where(mh, qb, jnp.zeros_like(qb))
                doh = jnp.where(mh, dob, jnp.zeros_like(dob))
                pp = jnp.where(prev_ok, jnp.exp(_dot_nt(qh, kp) - lse_h), 0.0)
                pc = jnp.where(cur_ok, jnp.exp(_dot_nt(qh, kc) - lse_h), 0.0)
                dsp = pp * (_dot_nt(doh, vp) - dl_h)
                dsc = pc * (_dot_nt(doh, vc) - dl_h)
                dq = dq + jnp.where(mh, _dot(dsp, kp) + _dot(dsc, kc), 0.0)
                dkp = dkp + _dot(dsp.T, qh)
                dkc = dkc + _dot(dsc.T, qh)
                dvp = dvp + _dot(pp.T, doh)
                dvc = dvc + _dot(pc.T, doh)
            dq_ref[cur, :] = dq
            dk_ref[prev, :] += dkp
            dv_ref[prev, :] += dvp
            dk_ref[cur, :] += dkc
            dv_ref[cur, :] += dvc
            return carry

        lax.fori_loop(0, n_blocks, step, 0)

    col = _spec((SEQ, 128), lambda j: (0, j))
    return pl.pallas_call(
        body, name=name, grid=(ATTN_WIDTH // 128,),
        in_specs=[col] * 6, out_specs=[col] * 3,
        out_shape=[jax.ShapeDtypeStruct((SEQ, ATTN_WIDTH), F32)] * 3,
        compiler_params=_params(1),
    )(q, k, v, d_out, delta, lse)


def _attn_bwd_post(name, grads, cos_t, sin_t):
    t = ROW_TILE
    w = ATTN_WIDTH

    def body(*refs):
        ins, cos_ref, sin_ref, out_ref, s4, s16 = refs[:9], refs[9], refs[10], refs[11], refs[12], refs[13]
        cosv, sinv = cos_ref[...], sin_ref[...]
        for a in range(3):
            g1, g4, g16 = ins[a], ins[3 + a], ins[6 + a]
            for cb in range(w // 128):
                cols = slice(cb * 128, (cb + 1) * 128)
                _unpermute(s4, g4, 4, cols)
                _unpermute(s16, g16, 16, cols)
                val = g1[:, cols] + s4[...] + s16[...]
                if a < 2:
                    val = val * cosv + _swap_halves(val * sinv)
                if a == 0:
                    val = val * (HEAD_DIM ** -0.5)
                out_ref[:, a * w + cb * 128:a * w + (cb + 1) * 128] = val

    views = []
    for p, d in enumerate(DILATIONS):
        for a in range(3):
            views.append(grads[p][a] if d == 1 else grads[p][a].reshape(d, SEQ // d, w))
    perm = _permuted_specs(t, w)
    in_specs = [perm[0]] * 3 + [perm[1]] * 3 + [perm[2]] * 3
    return pl.pallas_call(
        body, name=name, grid=(SEQ // t,),
        in_specs=in_specs + [_spec((t, 128), lambda i: (i, 0))] * 2,
        out_specs=_spec((t, 3 * w), lambda i: (i, 0)),
        out_shape=jax.ShapeDtypeStruct((SEQ, 3 * w), F32),
        scratch_shapes=[pltpu.VMEM((t, 128), F32)] * 2,
        compiler_params=_params(1),
    )(*views, cos_t, sin_t)


N_LEVELS = 7


def _hgrn_consts():
    c = CHUNK
    i = np.arange(c)[:, None]
    s = np.arange(c)[None, :]
    blocks = [s <= i]
    for lv in range(N_LEVELS):
        bs = c >> lv
        h = bs // 2
        m = (i // bs) * bs + h - 1
        second = (i % bs) >= h
        blocks.append((second & (s > m) & (s <= i)) | (~second & (s > i) & (s <= m)))
    blocks.append(s > i)
    stack = np.concatenate(blocks, axis=0).astype(np.float32)
    return jnp.asarray(stack, dtype=BF16), jnp.asarray(stack.T, dtype=BF16)


def _exact_dot(m01, x):
    hi = x.astype(BF16)
    r1 = x - hi.astype(F32)
    mid = r1.astype(BF16)
    lo = (r1 - mid.astype(F32)).astype(BF16)
    n = x.shape[1]
    full = jnp.dot(m01, jnp.concatenate([hi, mid, lo], axis=1), preferred_element_type=F32)
    return (full[:, :n] + full[:, n:2 * n]) + full[:, 2 * n:]


def _hgrn_gates(qh, z, lb):
    sq = _sigmoid(qh)
    q = qh * sq * (HGRN_DIM ** -0.5)
    sig = _sigmoid(z)
    sigm = _sigmoid(-z)
    f = lb + (1.0 - lb) * sig
    k = (1.0 - lb) * sigm
    return q, k, f, sq, sig, sigm


def _level_masks(lv):
    row = lax.broadcasted_iota(jnp.int32, (CHUNK, CHUNK), 0)
    col = lax.broadcasted_iota(jnp.int32, (CHUNK, CHUNK), 1)
    shift = N_LEVELS - lv
    second = (row & (CHUNK >> (lv + 1))) != 0
    same = (row >> shift) == (col >> shift)
    return second, same


def _hgrn_fwd(name, proj, lb, gain, stack):
    t = ROW_TILE
    per = t // CHUNK
    n_rb = SEQ // t
    n_chunks = SEQ // CHUNK
    col0 = 3 * ATTN_WIDTH // 128

    def body(q_ref, f_ref, i_ref, g_ref, lb_ref, gain_ref, stack_ref,
             rec_ref, rect_ref, o_ref, st_out, a_out, st):
        @pl.when(pl.program_id(1) == 0)
        def _():
            st[...] = jnp.zeros_like(st)

        lbv = lb_ref[...]
        row = lax.broadcasted_iota(jnp.int32, (CHUNK, CHUNK), 0)
        col = lax.broadcasted_iota(jnp.int32, (CHUNK, CHUNK), 1)
        for c in range(per):
            rows = slice(c * CHUNK, (c + 1) * CHUNK)
            qh, z, v, gh = q_ref[rows, :], f_ref[rows, :], i_ref[rows, :], g_ref[rows, :]
            q, k, f, _, _, _ = _hgrn_gates(qh, z, lbv)
            dec = _exact_dot(stack_ref[...], jnp.log(f))
            g = dec[0:CHUNK]
            to_end = dec[(N_LEVELS + 1) * CHUNK:(N_LEVELS + 2) * CHUNK]
            a = jnp.where(row == col, jnp.sum(q * k, axis=1, keepdims=True), 0.0)
            for lv in range(N_LEVELS):
                e = jnp.exp(dec[(lv + 1) * CHUNK:(lv + 2) * CHUNK])
                second, same = _level_masks(lv)
                qt = jnp.where(second, q * e, 0.0)
                kt = jnp.where(second, 0.0, k * e)
                a = a + jnp.where(same, _dot_nt(qt, kt), 0.0)
            st_prev = st[...]
            st_out[c] = st_prev
            a_out[c] = a
            o = _dot(a, v) + _dot_nt(q * jnp.exp(g), st_prev)
            k_end = k * jnp.exp(to_end)
            st[...] = st_prev * jnp.exp(g[CHUNK - 1:CHUNK, :]) + _dot(v.T, k_end)
            o_ref[rows, :] = o
            r = lax.rsqrt(jnp.mean(o * o, axis=1, keepdims=True) + NORM_EPS)
            rec = o * r * gain_ref[...] * (gh * _sigmoid(gh))
            rec_ref[rows, :] = rec.astype(BF16)
            rect_ref[:, rows] = rec.T.astype(BF16)

    def col_spec(tt):
        return _spec((t, HGRN_DIM), lambda h, rb: (rb, col0 + HGRN_HEADS * tt + h))

    chunk_spec = _spec((None, per, CHUNK, CHUNK), lambda h, rb: (h, rb, 0, 0))
    return pl.pallas_call(
        body, name=name, grid=(HGRN_HEADS, n_rb),
        in_specs=[col_spec(0), col_spec(1), col_spec(2), col_spec(3),
                  _spec((None, 1, HGRN_DIM), lambda h, rb: (h, 0, 0)),
                  _spec((1, HGRN_DIM), lambda h, rb: (0, 0)),
                  _spec(stack.shape, lambda h, rb: (0, 0))],
        out_specs=[_spec((t, HGRN_DIM), lambda h, rb: (rb, h)),
                   _spec((HGRN_DIM, t), lambda h, rb: (h, rb)),
                   _spec((t, HGRN_DIM), lambda h, rb: (rb, h)),
                   chunk_spec, chunk_spec],
        out_shape=[jax.ShapeDtypeStruct((SEQ, HGRN_WIDTH), BF16),
                   jax.ShapeDtypeStruct((HGRN_WIDTH, SEQ), BF16),
                   jax.ShapeDtypeStruct((SEQ, HGRN_WIDTH), F32),
                   jax.ShapeDtypeStruct((HGRN_HEADS, n_chunks, CHUNK, CHUNK), F32),
                   jax.ShapeDtypeStruct((HGRN_HEADS, n_chunks, CHUNK, CHUNK), F32)],
        scratch_shapes=[pltpu.VMEM((CHUNK, CHUNK), F32)],
        compiler_params=_params(2),
    )(proj, proj, proj, proj, lb, gain, stack)


def _hgrn_bwd(name, proj, d_rec, o_pre, states, scores, lb, gain, stack, stack_t):
    t = ROW_TILE
    per = t // CHUNK
    n_rb = SEQ // t
    col0 = 3 * ATTN_WIDTH // 128

    def body(q_ref, f_ref, i_ref, g_ref, drec_ref, o_ref, st_ref, a_ref, lb_ref, gain_ref,
             stack_ref, stack_t_ref, dq_ref, df_ref, di_ref, dg_ref, dlb_ref, dgain_ref, dst):
        @pl.when(pl.program_id(1) == 0)
        def _():
            dst[...] = jnp.zeros_like(dst)
            dlb_ref[...] = jnp.zeros_like(dlb_ref)
            dgain_ref[...] = jnp.zeros_like(dgain_ref)

        lbv = lb_ref[...]
        gain_v = gain_ref[...]
        row = lax.broadcasted_iota(jnp.int32, (CHUNK, CHUNK), 0)
        col = lax.broadcasted_iota(jnp.int32, (CHUNK, CHUNK), 1)
        for c in reversed(range(per)):
            rows = slice(c * CHUNK, (c + 1) * CHUNK)
            qh, z, v, gh = q_ref[rows, :], f_ref[rows, :], i_ref[rows, :], g_ref[rows, :]
            q, k, f, sq, sig, sigm = _hgrn_gates(qh, z, lbv)
            dec = _exact_dot(stack_ref[...], jnp.log(f))
            g = dec[0:CHUNK]
            to_end = dec[(N_LEVELS + 1) * CHUNK:(N_LEVELS + 2) * CHUNK]
            e_g = jnp.exp(g)
            e_end = jnp.exp(to_end)
            e_last = jnp.exp(g[CHUNK - 1:CHUNK, :])
            q_in = q * e_g
            k_end = k * e_end
            st_prev = st_ref[c]
            a = a_ref[c]
            dst_new = dst[...]

            o = o_ref[rows, :]
            drec = drec_ref[rows, :]
            sg = _sigmoid(gh)
            r = lax.rsqrt(jnp.mean(o * o, axis=1, keepdims=True) + NORM_EPS)
            ohat = o * r
            d_gh = drec * (ohat * gain_v) * (sg * (1.0 + gh * (1.0 - sg)))
            d_on = drec * (gh * sg)
            dgain_ref[...] += jnp.sum(d_on * ohat, axis=0, keepdims=True)
            d_ohat = d_on * gain_v
            d_o = r * (d_ohat - ohat * jnp.mean(d_ohat * ohat, axis=1, keepdims=True))

            d_a = jnp.where(row >= col, _dot_nt(d_o, v), 0.0)
            d_at = jnp.where(col >= row, _dot_nt(v, d_o), 0.0)
            d_v = _dot(a.T, d_o) + _dot_nt(k_end, dst_new)
            d_q_in = _dot(d_o, st_prev)
            d_k_end = _dot(v, dst_new)
            d_q = d_q_in * e_g
            d_k = d_k_end * e_end
            diag = jnp.sum(d_o * v, axis=1, keepdims=True)
            d_q = d_q + diag * k
            d_k = d_k + diag * q
            d_dec = [q_in * d_q_in]
            for lv in range(N_LEVELS):
                e = jnp.exp(dec[(lv + 1) * CHUNK:(lv + 2) * CHUNK])
                second, same = _level_masks(lv)
                qt = jnp.where(second, q * e, 0.0)
                kt = jnp.where(second, 0.0, k * e)
                d_qt = _dot(jnp.where(same, d_a, 0.0), kt)
                d_kt = _dot(jnp.where(same, d_at, 0.0), qt)
                d_q = d_q + jnp.where(second, d_qt * e, 0.0)
                d_k = d_k + jnp.where(second, 0.0, d_kt * e)
                d_dec.append(jnp.where(second, qt * d_qt, kt * d_kt))
            d_dec.append(k_end * d_k_end)
            flux = jnp.sum(dst_new * st_prev, axis=0, keepdims=True) * e_last
            d_lf = _exact_dot(stack_t_ref[...], jnp.concatenate(d_dec, axis=0)) + flux
            dst[...] = dst_new * e_last + _dot(d_o.T, q_in)

            d_f = d_lf / f - d_k
            dlb_ref[...] += jnp.sum(d_f * sigm, axis=0, keepdims=True)
            dq_ref[rows, :] = d_q * (HGRN_DIM ** -0.5) * (sq * (1.0 + qh * (1.0 - sq)))
            df_ref[rows, :] = d_f * (1.0 - lbv) * sig * sigm
            di_ref[rows, :] = d_v
            dg_ref[rows, :] = d_gh

    last = n_rb - 1

    def col_spec(tt):
        return _spec((t, HGRN_DIM), lambda h, rb: (last - rb, col0 + HGRN_HEADS * tt + h))

    head_col = _spec((t, HGRN_DIM), lambda h, rb: (last - rb, h))
    chunk_spec = _spec((None, per, CHUNK, CHUNK), lambda h, rb: (h, last - rb, 0, 0))
    vec_spec = _spec((None, 1, HGRN_DIM), lambda h, rb: (h, 0, 0))
    outs = pl.pallas_call(
        body, name=name, grid=(HGRN_HEADS, n_rb),
        in_specs=[col_spec(0), col_spec(1), col_spec(2), col_spec(3), head_col, head_col,
                  chunk_spec, chunk_spec, vec_spec,
                  _spec((1, HGRN_DIM), lambda h, rb: (0, 0)),
                  _spec(stack.shape, lambda h, rb: (0, 0)), _spec(stack_t.shape, lambda h, rb: (0, 0))],
        out_specs=[head_col] * 4 + [vec_spec, vec_spec],
        out_shape=[jax.ShapeDtypeStruct((SEQ, HGRN_WIDTH), F32)] * 4
                  + [jax.ShapeDtypeStruct((HGRN_HEADS, 1, HGRN_DIM), F32)] * 2,
        scratch_shapes=[pltpu.VMEM((CHUNK, CHUNK), F32)],
        compiler_params=_params(2),
    )(proj, proj, proj, proj, d_rec, o_pre, states, scores, lb, gain, stack, stack_t)
    return outs


ANY_SPEC = pl.BlockSpec(memory_space=pl.ANY)


def _my_place():
    return lax.axis_index("x"), lax.axis_index("y"), lax.axis_index("c")


def _other_chips(x, y):
    return [(1 - x, y), (x, 1 - y), (1 - x, 1 - y)]


def _remote(src, dst, send_sem, recv_sem, device):
    return pltpu.make_async_remote_copy(src_ref=src, dst_ref=dst, send_sem=send_sem, recv_sem=recv_sem,
                                        device_id=device, device_id_type=MESH)


def _staged_copies(srcs, dsts, stage, sems):
    loads = [pltpu.make_async_copy(srcs[i], stage[i], sems.at[i]) for i in range(len(srcs))]
    for cp in loads:
        cp.start()
    stores = []
    for i, cp in enumerate(loads):
        cp.wait()
        stores.append(pltpu.make_async_copy(stage[i], dsts[i], sems.at[i]))
        stores[-1].start()
    return stores


def _gather_weights(name, shards):
    n = len(shards)

    def body(*refs):
        ins, outs = refs[:n], refs[n:2 * n]
        ici_send, ici_recv, d2d_send, d2d_recv, local_sems = refs[2 * n:2 * n + 5]
        stage = refs[2 * n + 5:]
        x, y, c = _my_place()
        me = 2 * x + y
        chips = _other_chips(x, y)

        def half(i, which):
            h = ins[i].shape[0] // 2
            return pl.ds(which * h, h)

        sends = []
        for i in range(n):
            for j, (px, py) in enumerate(chips):
                sends.append(_remote(ins[i].at[half(i, c), :], outs[i].at[me, half(i, c), :],
                                     ici_send.at[3 * i + j], ici_recv.at[3 * i + j], (px, py, c)))
        for cp in sends:
            cp.start()
        local = _staged_copies(ins, [outs[i].at[me] for i in range(n)], stage, local_sems)
        for i in range(n):
            for j, (px, py) in enumerate(chips):
                landed = outs[i].at[2 * px + py, half(i, c), :]
                _remote(landed, landed, ici_send.at[3 * i + j], ici_recv.at[3 * i + j], (px, py, c)).wait_recv()
                forward = _remote(landed, landed, d2d_send.at[3 * i + j], d2d_recv.at[3 * i + j], (x, y, 1 - c))
                forward.start()
                sends.append(forward)
        for i in range(n):
            for j, (px, py) in enumerate(chips):
                other = outs[i].at[2 * px + py, half(i, 1 - c), :]
                _remote(other, other, d2d_send.at[3 * i + j], d2d_recv.at[3 * i + j], (x, y, 1 - c)).wait_recv()
        for cp in sends:
            cp.wait_send()
        for cp in local:
            cp.wait()

    return pl.pallas_call(
        body, name=name, in_specs=[ANY_SPEC] * n, out_specs=[ANY_SPEC] * n,
        out_shape=[jax.ShapeDtypeStruct((N_CHIPS,) + s.shape, s.dtype) for s in shards],
        scratch_shapes=([pltpu.SemaphoreType.DMA((3 * n,))] * 4 + [pltpu.SemaphoreType.DMA((n,))]
                        + [pltpu.VMEM(s.shape, s.dtype) for s in shards]),
        compiler_params=pltpu.CompilerParams(vmem_limit_bytes=VMEM_LIMIT),
    )(*shards)


def _exchange_halves(name, grads):
    n = len(grads)

    def body(*refs):
        ins, outs = refs[:n], refs[n:2 * n]
        send_sems, recv_sems = refs[2 * n:]
        x, y, c = _my_place()
        copies = []
        for i in range(n):
            h = ins[i].shape[1] // 2
            copies.append(_remote(ins[i].at[:, pl.ds((1 - c) * h, h), :], outs[i],
                                  send_sems.at[i], recv_sems.at[i], (x, y, 1 - c)))
        for cp in copies:
            cp.start()
        for cp in copies:
            cp.wait()

    return pl.pallas_call(
        body, name=name, in_specs=[ANY_SPEC] * n, out_specs=[ANY_SPEC] * n,
        out_shape=[jax.ShapeDtypeStruct((g.shape[0], g.shape[1] // 2, g.shape[2]), g.dtype) for g in grads],
        scratch_shapes=[pltpu.SemaphoreType.DMA((n,)), pltpu.SemaphoreType.DMA((n,))],
    )(*grads)


def _add_own_half(name, g, received, core):
    n_sh, r, cc = g.shape
    h = r // 2
    th = min(h, 256)
    nb = h // th

    def body(core_ref, g_ref, r_ref, o_ref):
        del core_ref
        o_ref[...] = (g_ref[...] + r_ref[...]).astype(BF16)

    grid_spec = pltpu.PrefetchScalarGridSpec(
        num_scalar_prefetch=1, grid=(n_sh, nb),
        in_specs=[pl.BlockSpec((None, th, cc), lambda j, i, core_ref: (j, core_ref[0] * nb + i, 0)),
                  pl.BlockSpec((None, th, cc), lambda j, i, core_ref: (j, i, 0))],
        out_specs=pl.BlockSpec((None, th, cc), lambda j, i, core_ref: (j, i, 0)))
    return pl.pallas_call(
        body, name=name, grid_spec=grid_spec,
        out_shape=jax.ShapeDtypeStruct((n_sh, h, cc), BF16), compiler_params=_params(2),
    )(core, g, received)


def _exchange_chips(name, parts):
    n = len(parts)

    def body(*refs):
        ins, outs = refs[:n], refs[n:2 * n]
        send_sems, recv_sems, local_sems = refs[2 * n:2 * n + 3]
        stage = refs[2 * n + 3:]
        x, y, c = _my_place()
        me = 2 * x + y
        chips = _other_chips(x, y)
        sends = []
        for i in range(n):
            for j, (px, py) in enumerate(chips):
                sends.append(_remote(ins[i].at[2 * px + py], outs[i].at[me], send_sems.at[3 * i + j],
                                     recv_sems.at[3 * i + j], (px, py, c)))
        for cp in sends:
            cp.start()
        local = _staged_copies([ins[i].at[me] for i in range(n)], [outs[i].at[me] for i in range(n)],
                               stage, local_sems)
        for i in range(n):
            for j, (px, py) in enumerate(chips):
                _remote(ins[i].at[me], outs[i].at[2 * px + py], send_sems.at[3 * i + j],
                        recv_sems.at[3 * i + j], (px, py, c)).wait_recv()
        for cp in sends:
            cp.wait_send()
        for cp in local:
            cp.wait()

    return pl.pallas_call(
        body, name=name, in_specs=[ANY_SPEC] * n, out_specs=[ANY_SPEC] * n,
        out_shape=[jax.ShapeDtypeStruct(p.shape, p.dtype) for p in parts],
        scratch_shapes=([pltpu.SemaphoreType.DMA((3 * n,)), pltpu.SemaphoreType.DMA((3 * n,)),
                         pltpu.SemaphoreType.DMA((n,))]
                        + [pltpu.VMEM(p.shape[1:], p.dtype) for p in parts]),
        compiler_params=pltpu.CompilerParams(vmem_limit_bytes=VMEM_LIMIT),
    )(*parts)


def _sum_chips(name, parts):
    n_sh, h, cc = parts.shape
    th = min(h, 256)

    def body(p_ref, o_ref):
        p = [p_ref[j].astype(F32) for j in range(n_sh)]
        o_ref[...] = ((p[0] + p[1]) + p[2]) + p[3]

    return pl.pallas_call(
        body, name=name, grid=(h // th,),
        in_specs=[_spec((n_sh, th, cc), lambda i: (0, i, 0))],
        out_specs=_spec((th, cc), lambda i: (i, 0)),
        out_shape=jax.ShapeDtypeStruct((h, cc), F32), compiler_params=_params(1),
    )(parts)


def _share_halves(halves):
    flat = [t for per_weight in halves for t in per_weight]
    n = len(flat)
    n_w = len(halves)

    def body(*refs):
        ins, outs = refs[:n], refs[n:n + n_w]
        send_sems, recv_sems, local_sems = refs[n + n_w:n + n_w + 3]
        stage = refs[n + n_w + 3:]
        x, y, c = _my_place()
        sends, own = [], []
        for i in range(n):
            w, l = divmod(i, DEPTH)
            h = ins[i].shape[0]
            own.append(outs[w].at[l, pl.ds(c * h, h), :])
            sends.append(_remote(ins[i], own[i], send_sems.at[i], recv_sems.at[i], (x, y, 1 - c)))
        for cp in sends:
            cp.start()
        local = _staged_copies(ins, own, stage, local_sems)
        for i in range(n):
            w, l = divmod(i, DEPTH)
            h = ins[i].shape[0]
            _remote(ins[i], outs[w].at[l, pl.ds((1 - c) * h, h), :], send_sems.at[i], recv_sems.at[i],
                    (x, y, 1 - c)).wait_recv()
        for cp in sends:
            cp.wait_send()
        for cp in local:
            cp.wait()

    return pl.pallas_call(
        body, name="share_halves", in_specs=[ANY_SPEC] * n, out_specs=[ANY_SPEC] * n_w,
        out_shape=[jax.ShapeDtypeStruct((DEPTH, 2 * per_weight[0].shape[0], per_weight[0].shape[1]), F32)
                   for per_weight in halves],
        scratch_shapes=([pltpu.SemaphoreType.DMA((n,))] * 3 + [pltpu.VMEM(t.shape, t.dtype) for t in flat]),
        compiler_params=pltpu.CompilerParams(vmem_limit_bytes=VMEM_LIMIT),
    )(*flat)


def _all_reduce_small(pack):
    def body(p_ref, o_ref, recv, send_sems, recv_sems):
        x, y, c = _my_place()
        me = 4 * x + 2 * y + c
        recv[me] = p_ref[...]
        peers = []
        for k in range(1, N_DEV):
            px, py, pc = (x + (k >> 2)) % 2, (y + ((k >> 1) & 1)) % 2, (c + (k & 1)) % 2
            peers.append((px, py, pc))
        sends = [_remote(p_ref, recv.at[me], send_sems.at[k], recv_sems.at[k], peer)
                 for k, peer in enumerate(peers)]
        for cp in sends:
            cp.start()
        for k, (px, py, pc) in enumerate(peers):
            _remote(p_ref, recv.at[4 * px + 2 * py + pc], send_sems.at[k], recv_sems.at[k],
                    (px, py, pc)).wait_recv()
        for cp in sends:
            cp.wait_send()
        total = recv[0]
        for d in range(1, N_DEV):
            total = total + recv[d]
        o_ref[...] = total

    vmem = pl.BlockSpec(memory_space=pltpu.VMEM)
    return pl.pallas_call(
        body, name="all_reduce_small", in_specs=[vmem], out_specs=vmem,
        out_shape=jax.ShapeDtypeStruct(pack.shape, F32),
        scratch_shapes=[pltpu.VMEM((N_DEV,) + pack.shape, F32),
                        pltpu.SemaphoreType.DMA((N_DEV - 1,)), pltpu.SemaphoreType.DMA((N_DEV - 1,))],
    )(pack)


def _adamw(name, w, g, m, v):
    r, cc = w.shape
    th = min(r, 256)

    def body(w_ref, g_ref, m_ref, v_ref, d_ref, m_out, v_out):
        gv = g_ref[...]
        m2 = ADAM_B1 * m_ref[...] + (1.0 - ADAM_B1) * gv
        v2 = ADAM_B2 * v_ref[...] + (1.0 - ADAM_B2) * (gv * gv)
        m_hat = m2 / (1.0 - ADAM_B1 ** ADAM_STEP)
        v_hat = v2 / (1.0 - ADAM_B2 ** ADAM_STEP)
        d_ref[...] = -ADAM_LR * (m_hat / (jnp.sqrt(v_hat) + ADAM_EPS) + ADAM_WD * w_ref[...])
        m_out[...] = m2
        v_out[...] = v2

    tile = _spec((th, cc), lambda i: (i, 0))
    return pl.pallas_call(
        body, name=name, grid=(r // th,), in_specs=[tile] * 4, out_specs=[tile] * 3,
        out_shape=[jax.ShapeDtypeStruct((r, cc), F32)] * 3, compiler_params=_params(1),
    )(w, g, m, v)


def _lower_bounds(lb_logits):
    p = jax.nn.softmax(lb_logits.astype(F32), axis=0)
    return jnp.cumsum(p, axis=0) - p[0]


def _row_tile_specs(tm, width):
    return _spec((tm, width), lambda i, j, k: (i, 0))


def _layer_forward(l, x_in, small, weights, consts):
    win, wo, wu, wd = weights
    cos_t, sin_t, stack, _, _ = consts
    tm = MM_TILE
    n_row = SEQ // tm
    saved = {"x_in": x_in}

    h, h_t = _rms_fwd(f"norm_mix{l}", x_in, small["norm_mix"][l][None, :])
    proj = _matmul(f"proj{l}", h, win,
                   _spec((tm, D_MODEL), lambda i, j, k: (i, 0)),
                   _spec((None, D_MODEL, SHARD_IN), lambda i, j, k: (j, 0, 0)),
                   (SEQ, IN_W), F32, _spec((tm, SHARD_IN), lambda i, j, k: (i, j)),
                   (n_row, N_CHIPS, 1), (tm, SHARD_IN))
    saved.update(h_t=h_t, proj=proj)

    qkv = _attn_prep(f"attn_prep{l}", proj, cos_t, sin_t)
    outs, lses = [], []
    for p, d in enumerate(DILATIONS):
        o, lse = _attn_fwd(f"attn_fwd{l}_{d}", *qkv[p], SEQ // d // SPAN)
        outs.append(o)
        lses.append(lse)
    an, an_t, attn, lse = _attn_merge(f"attn_merge{l}", outs, lses, small["attn_out_gain"][l][None, :])
    saved.update(qkv=qkv, an_t=an_t, attn=attn, lse=lse)

    lb3 = small["lower"][l].reshape(HGRN_HEADS, 1, HGRN_DIM)
    rec, rec_t, o_pre, states, scores = _hgrn_fwd(f"hgrn_fwd{l}", proj, lb3,
                                                  small["hgrn_out_gain"][l][None, :], stack)
    saved.update(rec_t=rec_t, o_pre=o_pre, states=states, scores=scores, lb3=lb3)

    def out_proj(name, a, part, resid):
        return _matmul(name, a, wo,
                       _spec((tm, SHARD_OUT), lambda i, j, k: (i, k)),
                       _spec((None, SHARD_OUT, D_MODEL), lambda i, j, k: (k + 2 * part, 0, 0)),
                       (SEQ, D_MODEL), F32, _spec((tm, D_MODEL), lambda i, j, k: (i, 0)),
                       (n_row, 1, 2), (tm, D_MODEL),
                       extra=resid, extra_spec=_spec((tm, D_MODEL), lambda i, j, k: (i, 0)), epilogue="add")

    x_mid = out_proj(f"out_rec{l}", rec, 1, out_proj(f"out_attn{l}", an, 0, x_in))
    saved["x_mid"] = x_mid

    h2, h2_t = _rms_fwd(f"norm_mlp{l}", x_mid, small["norm_mlp"][l][None, :])
    u = _matmul(f"up{l}", h2, wu,
                _spec((tm, D_MODEL), lambda i, j, k: (i, 0)),
                _spec((None, D_MODEL, SHARD_MLP), lambda i, j, k: (j, 0, 0)),
                (SEQ, MLP_HIDDEN), F32, _spec((tm, SHARD_MLP), lambda i, j, k: (i, j)),
                (n_row, N_CHIPS, 1), (tm, SHARD_MLP))
    a, a_t = _relu2(f"relu2_{l}", u)
    x_out = _matmul(f"down{l}", a, wd,
                    _spec((tm, SHARD_MLP), lambda i, j, k: (i, k)),
                    _spec((None, SHARD_MLP, D_MODEL), lambda i, j, k: (k, 0, 0)),
                    (SEQ, D_MODEL), F32, _spec((tm, D_MODEL), lambda i, j, k: (i, 0)),
                    (n_row, 1, N_CHIPS), (tm, D_MODEL),
                    extra=x_mid, extra_spec=_spec((tm, D_MODEL), lambda i, j, k: (i, 0)), epilogue="add")
    saved.update(h2_t=h2_t, u=u, a_t=a_t)
    return x_out, saved


def _layer_backward(l, dx, saved, small, weights, consts):
    win, wo, wu, wd = weights
    cos_t, sin_t, stack, stack_t, head_sum = consts
    tm = MM_TILE
    n_row = SEQ // tm
    n_k = SEQ // tm

    du = _matmul(f"d_u{l}", dx, wd,
                 _spec((tm, D_MODEL), lambda i, j, k: (i, 0)),
                 _spec((None, SHARD_MLP, D_MODEL), lambda i, j, k: (j, 0, 0)),
                 (SEQ, MLP_HIDDEN), BF16, _spec((tm, SHARD_MLP), lambda i, j, k: (i, j)),
                 (n_row, N_CHIPS, 1), (tm, SHARD_MLP), nt=True,
                 extra=saved["u"], extra_spec=_spec((tm, SHARD_MLP), lambda i, j, k: (i, j)),
                 epilogue="relu2_grad")
    d_wd = _matmul(f"d_wdown{l}", saved["a_t"], dx,
                   _spec((SHARD_MLP, tm), lambda i, j, k: (i, k)),
                   _spec((tm, D_MODEL), lambda i, j, k: (k, 0)),
                   (N_CHIPS, SHARD_MLP, D_MODEL), F32, _spec((None, SHARD_MLP, D_MODEL), lambda i, j, k: (i, 0, 0)),
                   (N_CHIPS, 1, n_k), (SHARD_MLP, D_MODEL))
    dh2 = _matmul(f"d_h2_{l}", du, wu,
                  _spec((tm, SHARD_MLP), lambda i, j, k: (i, k)),
                  _spec((None, D_MODEL, SHARD_MLP), lambda i, j, k: (k, 0, 0)),
                  (SEQ, D_MODEL), F32, _spec((tm, D_MODEL), lambda i, j, k: (i, 0)),
                  (n_row, 1, N_CHIPS), (tm, D_MODEL), nt=True)
    d_wu = _matmul(f"d_wup{l}", saved["h2_t"], du,
                   _spec((D_MODEL, tm), lambda i, j, k: (0, k)),
                   _spec((tm, SHARD_MLP), lambda i, j, k: (k, j)),
                   (N_CHIPS, D_MODEL, SHARD_MLP), F32, _spec((None, D_MODEL, SHARD_MLP), lambda i, j, k: (j, 0, 0)),
                   (1, N_CHIPS, n_k), (D_MODEL, SHARD_MLP))
    dxm, dg_mlp = _rms_bwd(f"norm_mlp_bwd{l}", dh2, saved["x_mid"], small["norm_mlp"][l][None, :], dx)

    def d_mixed(name, part):
        return _matmul(name, dxm, wo,
                       _spec((tm, D_MODEL), lambda i, j, k: (i, 0)),
                       _spec((None, SHARD_OUT, D_MODEL), lambda i, j, k: (j + 2 * part, 0, 0)),
                       (SEQ, ATTN_WIDTH), F32, _spec((tm, SHARD_OUT), lambda i, j, k: (i, j)),
                       (n_row, 2, 1), (tm, SHARD_OUT), nt=True)

    def d_wout(name, a_t):
        return _matmul(name, a_t, dxm,
                       _spec((SHARD_OUT, tm), lambda i, j, k: (i, k)),
                       _spec((tm, D_MODEL), lambda i, j, k: (k, 0)),
                       (2, SHARD_OUT, D_MODEL), F32, _spec((None, SHARD_OUT, D_MODEL), lambda i, j, k: (i, 0, 0)),
                       (2, 1, n_k), (SHARD_OUT, D_MODEL))

    d_an = d_mixed(f"d_attn_n{l}", 0)
    d_rec = d_mixed(f"d_rec{l}", 1)
    d_wo = jnp.concatenate([d_wout(f"d_wout_attn{l}", saved["an_t"]),
                            d_wout(f"d_wout_rec{l}", saved["rec_t"])], axis=0)

    d_out, delta, lses, dg_attn = _attn_bwd_prep(f"attn_bwd_prep{l}", d_an, saved["attn"], saved["lse"],
                                                 small["attn_out_gain"][l][None, :], head_sum)
    grads = []
    for p, d in enumerate(DILATIONS):
        grads.append(_attn_bwd(f"attn_bwd{l}_{d}", *saved["qkv"][p], d_out[p], delta[p], lses[p],
                               SEQ // d // SPAN))
    dp_attn = _attn_bwd_post(f"attn_bwd_post{l}", grads, cos_t, sin_t)

    dq_h, df_h, di_h, dg_h, d_lower, dg_hgrn = _hgrn_bwd(
        f"hgrn_bwd{l}", saved["proj"], d_rec, saved["o_pre"], saved["states"], saved["scores"],
        saved["lb3"], small["hgrn_out_gain"][l][None, :], stack, stack_t)
    dproj = jnp.concatenate([dp_attn, dq_h, df_h, di_h, dg_h], axis=1)

    dh = _matmul(f"d_h{l}", dproj, win,
                 _spec((tm, SHARD_IN), lambda i, j, k: (i, k)),
                 _spec((None, D_MODEL, SHARD_IN), lambda i, j, k: (k, 0, 0)),
                 (SEQ, D_MODEL), F32, _spec((tm, D_MODEL), lambda i, j, k: (i, 0)),
                 (n_row, 1, N_CHIPS), (tm, D_MODEL), nt=True)
    d_win = _matmul(f"d_win{l}", saved["h_t"], dproj,
                    _spec((D_MODEL, tm), lambda i, j, k: (0, k)),
                    _spec((tm, SHARD_IN), lambda i, j, k: (k, j)),
                    (N_CHIPS, D_MODEL, SHARD_IN), F32, _spec((None, D_MODEL, SHARD_IN), lambda i, j, k: (j, 0, 0)),
                    (1, N_CHIPS, n_k), (D_MODEL, SHARD_IN))
    dx_in, dg_mix = _rms_bwd(f"norm_mix_bwd{l}", dh, saved["x_in"], small["norm_mix"][l][None, :], dxm)

    small_grads = {"norm_mix": dg_mix[0], "attn_out_gain": dg_attn[0],
                   "lower": d_lower.reshape(HGRN_WIDTH),
                   "hgrn_out_gain": jnp.sum(dg_hgrn, axis=0).reshape(HGRN_DIM), "norm_mlp": dg_mlp[0]}
    return dx_in, (d_win, d_wo, d_wu, d_wd), small_grads


def _local_step(xs, target, small, get_weights, on_grads):
    consts = _rope_tables() + _hgrn_consts() + (_head_sum_matrix(),)
    stream = xs
    saved, weights = [], []
    for l in range(DEPTH):
        weights.append(get_weights(l, stream))
        stream, s = _layer_forward(l, stream, small, weights[l], consts)
        saved.append(s)
    dx, dg_final, loss = _loss_head(stream, small["norm_final"][None, :], target)
    small_grads = [None] * DEPTH
    for l in reversed(range(DEPTH)):
        dx, big, small_grads[l] = _layer_backward(l, dx, saved[l], small, weights[l], consts)
        on_grads(l, big)
    return loss, dx, dg_final[0], small_grads


def _pack_small(norm_mix, attn_out_gain, lb, hgrn_out_gain, norm_mlp, norm_final, last_row):
    rows = [norm_mix, attn_out_gain.reshape(1, D_MODEL), lb.reshape(1, D_MODEL),
            jnp.pad(hgrn_out_gain.reshape(1, DEPTH * HGRN_DIM), ((0, 0), (0, D_MODEL - DEPTH * HGRN_DIM))),
            norm_mlp, norm_final.reshape(1, D_MODEL), last_row.reshape(1, D_MODEL)]
    pack = jnp.concatenate(rows, axis=0)
    return jnp.pad(pack, ((0, PACK_ROWS - pack.shape[0]), (0, 0)))


def _unpack_small(pack):
    return (pack[0:2], pack[2].reshape(DEPTH, ATTN_WIDTH), pack[3].reshape(DEPTH, HGRN_WIDTH),
            pack[4, :DEPTH * HGRN_DIM].reshape(DEPTH, HGRN_DIM), pack[5:7], pack[7], pack[8])


def kernel(x, norm_mix, w_in, attn_out_gain, hgrn_lb_logits, hgrn_out_gain, w_out, norm_mlp, w_up, w_down, norm_final, loss_target, m_norm_mix, m_w_in, m_attn_out_gain, m_hgrn_lb_logits, m_hgrn_out_gain, m_w_out, m_norm_mlp, m_w_up, m_w_down, m_norm_final, v_norm_mix, v_w_in, v_attn_out_gain, v_hgrn_lb_logits, v_hgrn_out_gain, v_w_out, v_norm_mlp, v_w_up, v_w_down, v_norm_final):
    core = lax.axis_index("c").astype(jnp.int32).reshape(1)
    lower, lower_vjp = jax.vjp(_lower_bounds, hgrn_lb_logits)
    small = {"norm_mix": norm_mix, "attn_out_gain": attn_out_gain, "lower": lower,
             "hgrn_out_gain": hgrn_out_gain, "norm_mlp": norm_mlp, "norm_final": norm_final}
    big_w = (w_in, w_out, w_up, w_down)

    def get_weights(l, stream):
        del stream
        return _gather_weights(f"gather_weights{l}", [w[l].astype(BF16) for w in big_w])

    reduced = [None] * DEPTH

    def on_grads(l, grads):
        received = _exchange_halves(f"exchange_halves{l}", grads)
        halves = [_add_own_half(f"add_halves{l}_{i}", g, r, core)
                  for i, (g, r) in enumerate(zip(grads, received))]
        from_chips = _exchange_chips(f"exchange_chips{l}", halves)
        reduced[l] = [_sum_chips(f"sum_chips{l}_{i}", p) for i, p in enumerate(from_chips)]

    loss, dx, dg_final, sg = _local_step(x[0], loss_target[0], small, get_weights, on_grads)

    stack2 = lambda key: jnp.stack([sg[l][key] for l in range(DEPTH)])
    pack = _pack_small(stack2("norm_mix"), stack2("attn_out_gain"), stack2("lower"), stack2("hgrn_out_gain"),
                       stack2("norm_mlp"), dg_final, jnp.broadcast_to(loss[0, 0], (D_MODEL,)))
    g_mix, g_attn, g_lower, g_hgrn, g_mlp, g_final, loss_row = _unpack_small(_all_reduce_small(pack))
    (g_logits,) = lower_vjp(g_lower)

    zeros_row = jnp.zeros((D_MODEL,), F32)
    small_w = (norm_mix, attn_out_gain, hgrn_lb_logits, hgrn_out_gain, norm_mlp, norm_final)
    small_m = (m_norm_mix, m_attn_out_gain, m_hgrn_lb_logits, m_hgrn_out_gain, m_norm_mlp, m_norm_final)
    small_v = (v_norm_mix, v_attn_out_gain, v_hgrn_lb_logits, v_hgrn_out_gain, v_norm_mlp, v_norm_final)
    small_g = (g_mix, g_attn, g_logits, g_hgrn, g_mlp, g_final)
    packs = [_pack_small(*t, zeros_row) for t in (small_w, small_g, small_m, small_v)]
    small_delta, small_new_m, small_new_v = [_unpack_small(p)[:6] for p in _adamw("adamw_small", *packs)]

    big_g = _share_halves([[reduced[l][w] for l in range(DEPTH)] for w in range(4)])

    big_m = (m_w_in, m_w_out, m_w_up, m_w_down)
    big_v = (v_w_in, v_w_out, v_w_up, v_w_down)
    big_delta, big_new_m, big_new_v = [], [], []
    for i, name in enumerate(("w_in", "w_out", "w_up", "w_down")):
        shape = big_w[i].shape
        flat = lambda arr: arr.reshape(shape[0] * shape[1], shape[2])
        d, m2, v2 = _adamw(f"adamw_{name}", flat(big_w[i]), flat(big_g[i]), flat(big_m[i]), flat(big_v[i]))
        big_delta.append(d.reshape(shape))
        big_new_m.append(m2.reshape(shape))
        big_new_v.append(v2.reshape(shape))

    def ordered(small6, big4):
        mix, attn, lbl, hg, mlp, fin = small6
        return (mix, big4[0], attn, lbl, hg, big4[1], mlp, big4[2], big4[3], fin)

    return ((loss_row[0], dx[None]) + ordered(small_g, big_g) + ordered(small_delta, big_delta)
            + ordered(small_new_m, big_new_m) + ordered(small_new_v, big_new_v))
```

```python
import functools
import math

import numpy as np
import jax
import jax.numpy as jnp
from jax import lax
from jax.experimental import pallas as pl
from jax.experimental.pallas import tpu as pltpu

F32 = jnp.float32
BF16 = jnp.bfloat16
MESH = pl.DeviceIdType.MESH

SEQ = 4096
D_MODEL = 1024
DEPTH = 2
ATTN_WIDTH = 512
HEAD_DIM = 64
HGRN_HEADS = 4
HGRN_DIM = 128
HGRN_WIDTH = 512
IN_W = 3584
MLP_HIDDEN = 4096
N_CHIPS = 4
N_DEV = 8
SHARD_IN = IN_W // N_CHIPS
SHARD_OUT = D_MODEL // N_CHIPS
SHARD_MLP = MLP_HIDDEN // N_CHIPS
DILATIONS = (1, 4, 16)
SPAN = 128
ROPE_THETA = 10000.0
NORM_EPS = 1e-6
MASK_VALUE = -1e30
CHUNK = 128
ROW_TILE = 512
MM_TILE = 1024
VMEM_LIMIT = 52 * 1024 * 1024

ADAM_LR = 0.001
ADAM_B1 = 0.9
ADAM_B2 = 0.999
ADAM_EPS = 1e-08
ADAM_WD = 0.01
ADAM_STEP = 10

PACK_ROWS = 16


def _params(n_axes):
    return pltpu.CompilerParams(dimension_semantics=("arbitrary",) * n_axes,
                                vmem_limit_bytes=VMEM_LIMIT)


def _dot(a, b):
    return jnp.dot(a.astype(BF16), b.astype(BF16), preferred_element_type=F32)


def _dot_nt(a, b):
    return lax.dot_general(a.astype(BF16), b.astype(BF16), (((1,), (1,)), ((), ())),
                           preferred_element_type=F32)


def _sigmoid(x):
    return 1.0 / (1.0 + jnp.exp(-x))


def _matmul(name, a, b, a_spec, b_spec, out_shape, out_dtype, out_spec, grid, acc_shape,
            nt=False, extra=None, extra_spec=None, epilogue="none", after=None):
    nk = grid[2]

    def body(*refs):
        a_ref, b_ref = refs[:2]
        e_ref = None if extra is None else refs[2]
        o_ref, acc = refs[-2:]
        kk = pl.program_id(2)

        @pl.when(kk == 0)
        def _():
            acc[...] = jnp.zeros_like(acc)

        if nt:
            acc[...] += _dot_nt(a_ref[...], b_ref[...])
        else:
            acc[...] += _dot(a_ref[...], b_ref[...])

        @pl.when(kk == nk - 1)
        def _():
            r = acc[...]
            if epilogue == "add":
                r = r + e_ref[...]
            elif epilogue == "relu2_grad":
                r = r * (2.0 * jnp.maximum(e_ref[...], 0.0))
            o_ref[...] = r.astype(o_ref.dtype)

    in_specs = [a_spec, b_spec] + ([] if extra is None else [extra_spec])
    args = (a, b) + (() if extra is None else (extra,))
    if after is not None:
        in_specs.append(pl.BlockSpec(memory_space=pl.ANY))
        args += (after,)
    return pl.pallas_call(
        body, name=name, grid=grid, in_specs=in_specs, out_specs=out_spec,
        out_shape=jax.ShapeDtypeStruct(out_shape, out_dtype),
        scratch_shapes=[pltpu.VMEM(acc_shape, F32)],
        compiler_params=_params(3),
    )(*args)


def _spec(shape, index_map):
    return pl.BlockSpec(shape, index_map)


def _rms_fwd(name, x, gain, after=None):
    s, d = x.shape
    t = ROW_TILE

    def body(x_ref, g_ref, *rest):
        h_ref, ht_ref = rest[-2:]
        xv = x_ref[...]
        r = lax.rsqrt(jnp.mean(xv * xv, axis=1, keepdims=True) + NORM_EPS)
        h = xv * r * g_ref[...]
        h_ref[...] = h.astype(BF16)
        ht_ref[...] = h.T.astype(BF16)

    in_specs = [_spec((t, d), lambda i: (i, 0)), _spec((1, d), lambda i: (0, 0))]
    args = (x, gain)
    if after is not None:
        in_specs.append(pl.BlockSpec(memory_space=pl.ANY))
        args += (after,)
    return pl.pallas_call(
        body, name=name, grid=(s // t,), in_specs=in_specs,
        out_specs=[_spec((t, d), lambda i: (i, 0)), _spec((d, t), lambda i: (0, i))],
        out_shape=[jax.ShapeDtypeStruct((s, d), BF16), jax.ShapeDtypeStruct((d, s), BF16)],
        compiler_params=_params(1),
    )(*args)


def _rms_bwd(name, dh, x, gain, dres):
    s, d = x.shape
    t = ROW_TILE

    def body(dh_ref, x_ref, g_ref, dres_ref, dx_ref, dg_ref):
        @pl.when(pl.program_id(0) == 0)
        def _():
            dg_ref[...] = jnp.zeros_like(dg_ref)

        xv = x_ref[...]
        dhv = dh_ref[...]
        r = lax.rsqrt(jnp.mean(xv * xv, axis=1, keepdims=True) + NORM_EPS)
        xhat = xv * r
        dhg = dhv * g_ref[...]
        proj = jnp.mean(dhg * xhat, axis=1, keepdims=True)
        dx_ref[...] = dres_ref[...] + r * (dhg - xhat * proj)
        dg_ref[...] += jnp.sum(dhv * xhat, axis=0, keepdims=True)

    return pl.pallas_call(
        body, name=name, grid=(s // t,),
        in_specs=[_spec((t, d), lambda i: (i, 0)), _spec((t, d), lambda i: (i, 0)),
                  _spec((1, d), lambda i: (0, 0)), _spec((t, d), lambda i: (i, 0))],
        out_specs=[_spec((t, d), lambda i: (i, 0)), _spec((1, d), lambda i: (0, 0))],
        out_shape=[jax.ShapeDtypeStruct((s, d), F32), jax.ShapeDtypeStruct((1, d), F32)],
        compiler_params=_params(1),
    )(dh, x, gain, dres)


def _loss_head(x, gain, target):
    s, d = x.shape
    t = ROW_TILE
    n_steps = s // t

    def body(x_ref, g_ref, t_ref, dx_ref, dg_ref, loss_ref, acc):
        i = pl.program_id(0)

        @pl.when(i == 0)
        def _():
            dg_ref[...] = jnp.zeros_like(dg_ref)
            acc[...] = jnp.zeros_like(acc)

        xv = x_ref[...]
        g = g_ref[...]
        r = lax.rsqrt(jnp.mean(xv * xv, axis=1, keepdims=True) + NORM_EPS)
        xhat = xv * r
        err = xhat * g - t_ref[...]
        acc[...] += jnp.sum(err * err, axis=0, keepdims=True)
        dy = err * (1.0 / d)
        dyg = dy * g
        proj = jnp.mean(dyg * xhat, axis=1, keepdims=True)
        dx_ref[...] = r * (dyg - xhat * proj)
        dg_ref[...] += jnp.sum(dy * xhat, axis=0, keepdims=True)

        @pl.when(i == n_steps - 1)
        def _():
            total = jnp.sum(acc[...], axis=1, keepdims=True) * (0.5 / d)
            loss_ref[...] = jnp.broadcast_to(total, loss_ref.shape)

    return pl.pallas_call(
        body, name="loss_head", grid=(n_steps,),
        in_specs=[_spec((t, d), lambda i: (i, 0)), _spec((1, d), lambda i: (0, 0)),
                  _spec((t, d), lambda i: (i, 0))],
        out_specs=[_spec((t, d), lambda i: (i, 0)), _spec((1, d), lambda i: (0, 0)),
                   _spec((1, 128), lambda i: (0, 0))],
        out_shape=[jax.ShapeDtypeStruct((s, d), F32), jax.ShapeDtypeStruct((1, d), F32),
                   jax.ShapeDtypeStruct((1, 128), F32)],
        scratch_shapes=[pltpu.VMEM((1, d), F32)],
        compiler_params=_params(1),
    )(x, gain, target)


def _relu2(name, u):
    s, f = u.shape
    t = ROW_TILE

    def body(u_ref, a_ref, at_ref):
        r = jnp.maximum(u_ref[...], 0.0)
        a = r * r
        a_ref[...] = a.astype(BF16)
        at_ref[...] = a.T.astype(BF16)

    return pl.pallas_call(
        body, name=name, grid=(s // t, f // t),
        in_specs=[_spec((t, t), lambda i, j: (i, j))],
        out_specs=[_spec((t, t), lambda i, j: (i, j)), _spec((t, t), lambda i, j: (j, i))],
        out_shape=[jax.ShapeDtypeStruct((s, f), BF16), jax.ShapeDtypeStruct((f, s), BF16)],
        compiler_params=_params(2),
    )(u)


def _rope_tables():
    half = HEAD_DIM // 2
    inv_freq = ROPE_THETA ** (-jnp.arange(half, dtype=F32) / half)
    ang = jnp.arange(SEQ, dtype=jnp.int32).astype(F32)[:, None] * inv_freq[None, :]
    cos, sin = jnp.cos(ang), jnp.sin(ang)
    cos_t = jnp.concatenate([cos, cos, cos, cos], axis=1)
    sin_t = jnp.concatenate([-sin, sin, -sin, sin], axis=1)
    return cos_t, sin_t


def _swap_halves(x):
    lane = lax.broadcasted_iota(jnp.int32, x.shape, 1)
    first = (lane % HEAD_DIM) < (HEAD_DIM // 2)
    return jnp.where(first, pltpu.roll(x, 128 - HEAD_DIM // 2, 1), pltpu.roll(x, HEAD_DIM // 2, 1))


def _permuted_specs(t, width):
    specs = [_spec((t, width), lambda i: (i, 0))]
    for d in DILATIONS[1:]:
        specs.append(_spec((d, t // d, width), lambda i: (0, i, 0)))
    return specs


def _permuted_shapes(width, dtype):
    shapes = [jax.ShapeDtypeStruct((SEQ, width), dtype)]
    for d in DILATIONS[1:]:
        shapes.append(jax.ShapeDtypeStruct((d, SEQ // d, width), dtype))
    return shapes


def _attn_prep(name, proj, cos_t, sin_t):
    t = ROW_TILE
    w = ATTN_WIDTH

    def body(q_ref, k_ref, v_ref, cos_ref, sin_ref, *rest):
        outs, scr = rest[:9], rest[9]
        cosv, sinv = cos_ref[...], sin_ref[...]
        for a, (src, roped, scale) in enumerate(((q_ref, True, HEAD_DIM ** -0.5),
                                                 (k_ref, True, 1.0), (v_ref, False, 1.0))):
            o1, o4, o16 = outs[3 * a:3 * a + 3]
            for cb in range(w // 128):
                cols = slice(cb * 128, (cb + 1) * 128)
                val = src[:, cols]
                if roped:
                    val = (val * cosv + _swap_halves(val) * sinv) * scale
                scr[...] = val
                o1[:, cols] = val.astype(BF16)
                for o_ref, d in ((o4, 4), (o16, 16)):
                    for r in range(d):
                        o_ref[r, :, cols] = scr[pl.ds(r, t // d, stride=d), :].astype(BF16)

    out_specs = _permuted_specs(t, w) * 3
    out_shape = _permuted_shapes(w, BF16) * 3
    outs = pl.pallas_call(
        body, name=name, grid=(SEQ // t,),
        in_specs=[_spec((t, w), lambda i: (i, 0)), _spec((t, w), lambda i: (i, 1)),
                  _spec((t, w), lambda i: (i, 2)),
                  _spec((t, 128), lambda i: (i, 0)), _spec((t, 128), lambda i: (i, 0))],
        out_specs=out_specs, out_shape=out_shape,
        scratch_shapes=[pltpu.VMEM((t, 128), F32)],
        compiler_params=_params(1),
    )(proj, proj, proj, cos_t, sin_t)
    q, k, v = outs[0:3], outs[3:6], outs[6:9]
    flat = lambda arr: arr.reshape(SEQ, w)
    return [(flat(q[p]), flat(k[p]), flat(v[p])) for p in range(3)]


def _band_masks():
    row = lax.broadcasted_iota(jnp.int32, (SPAN, SPAN), 0)
    col = lax.broadcasted_iota(jnp.int32, (SPAN, SPAN), 1)
    return col >= row, col <= row, col < HEAD_DIM


def _attn_fwd(name, q, k, v, seg_blocks):
    n_blocks = SEQ // SPAN

    def body(q_ref, k_ref, v_ref, o_ref, lse_ref):
        prev_band, cur_ok, head0 = _band_masks()

        def step(b, carry):
            r0 = pl.multiple_of(b * SPAN, SPAN)
            p0 = pl.multiple_of(jnp.maximum(b - 1, 0) * SPAN, SPAN)
            qb = q_ref[pl.ds(r0, SPAN), :]
            kc, vc = k_ref[pl.ds(r0, SPAN), :], v_ref[pl.ds(r0, SPAN), :]
            kp, vp = k_ref[pl.ds(p0, SPAN), :], v_ref[pl.ds(p0, SPAN), :]
            prev_ok = prev_band & ((b % seg_blocks) != 0)
            o_acc = jnp.zeros((SPAN, 128), F32)
            lse_acc = jnp.zeros((SPAN, 128), F32)
            for mh in (head0, ~head0):
                qh = jnp.where(mh, qb, jnp.zeros_like(qb))
                sp = jnp.where(prev_ok, _dot_nt(qh, kp), MASK_VALUE)
                sc = jnp.where(cur_ok, _dot_nt(qh, kc), MASK_VALUE)
                m = jnp.maximum(jnp.max(sp, axis=1, keepdims=True), jnp.max(sc, axis=1, keepdims=True))
                pp = jnp.where(prev_ok, jnp.exp(sp - m), 0.0)
                pc = jnp.where(cur_ok, jnp.exp(sc - m), 0.0)
                l = jnp.sum(pp, axis=1, keepdims=True) + jnp.sum(pc, axis=1, keepdims=True)
                vph = jnp.where(mh, vp, jnp.zeros_like(vp))
                vch = jnp.where(mh, vc, jnp.zeros_like(vc))
                o_acc = o_acc + (_dot(pp, vph) + _dot(pc, vch)) / l
                lse_acc = jnp.where(mh, m + jnp.log(l), lse_acc)
            o_ref[pl.ds(r0, SPAN), :] = o_acc
            lse_ref[pl.ds(r0, SPAN), :] = lse_acc
            return carry

        lax.fori_loop(0, n_blocks, step, 0)

    col = _spec((SEQ, 128), lambda j: (0, j))
    return pl.pallas_call(
        body, name=name, grid=(ATTN_WIDTH // 128,),
        in_specs=[col, col, col], out_specs=[col, col],
        out_shape=[jax.ShapeDtypeStruct((SEQ, ATTN_WIDTH), F32)] * 2,
        compiler_params=_params(1),
    )(q, k, v)


def _unpermute(dst, src_ref, d, cols):
    n = dst.shape[0] // d
    for r in range(d):
        dst[pl.ds(r, n, stride=d), :] = src_ref[r, :, cols]


def _attn_merge(name, outs, lses, gain):
    t = ROW_TILE
    w = ATTN_WIDTH

    def body(o1, o4, o16, l1, l4, l16, g_ref, an_ref, ant_ref, attn_ref, lse_ref, so4, so16, sl4, sl16):
        for cb in range(w // 128):
            cols = slice(cb * 128, (cb + 1) * 128)
            _unpermute(so4, o4, 4, cols)
            _unpermute(so16, o16, 16, cols)
            _unpermute(sl4, l4, 4, cols)
            _unpermute(sl16, l16, 16, cols)
            la, lb, lc = l1[:, cols], sl4[...], sl16[...]
            m = jnp.maximum(jnp.maximum(la, lb), lc)
            ea, eb, ec = jnp.exp(la - m), jnp.exp(lb - m), jnp.exp(lc - m)
            tot = ea + eb + ec
            attn_ref[:, cols] = (ea * o1[:, cols] + eb * so4[...] + ec * so16[...]) / tot
            lse_ref[:, cols] = m + jnp.log(tot)
        attn = attn_ref[...]
        r = lax.rsqrt(jnp.mean(attn * attn, axis=1, keepdims=True) + NORM_EPS)
        an = attn * r * g_ref[...]
        an_ref[...] = an.astype(BF16)
        ant_ref[...] = an.T.astype(BF16)

    views = lambda arrs: [arrs[0], arrs[1].reshape(4, SEQ // 4, w), arrs[2].reshape(16, SEQ // 16, w)]
    row = _spec((t, w), lambda i: (i, 0))
    return pl.pallas_call(
        body, name=name, grid=(SEQ // t,),
        in_specs=_permuted_specs(t, w) * 2 + [_spec((1, w), lambda i: (0, 0))],
        out_specs=[row, _spec((w, t), lambda i: (0, i)), row, row],
        out_shape=[jax.ShapeDtypeStruct((SEQ, w), BF16), jax.ShapeDtypeStruct((w, SEQ), BF16),
                   jax.ShapeDtypeStruct((SEQ, w), F32), jax.ShapeDtypeStruct((SEQ, w), F32)],
        scratch_shapes=[pltpu.VMEM((t, 128), F32)] * 4,
        compiler_params=_params(1),
    )(*views(outs), *views(lses), gain)


def _head_sum_matrix():
    i = np.arange(ATTN_WIDTH)
    return jnp.asarray((i[:, None] // HEAD_DIM) == (i[None, :] // HEAD_DIM), dtype=F32)


def _attn_bwd_prep(name, d_an, attn, lse, gain, head_sum):
    t = ROW_TILE
    w = ATTN_WIDTH

    def body(dan_ref, attn_ref, lse_ref, g_ref, hs_ref, *rest):
        (do1, do4, do16, dl1, dl4, dl16, ls4, ls16, dg_ref), (sdo, sdl, sls) = rest[:9], rest[9:]

        @pl.when(pl.program_id(0) == 0)
        def _():
            dg_ref[...] = jnp.zeros_like(dg_ref)

        attn = attn_ref[...]
        dan = dan_ref[...]
        r = lax.rsqrt(jnp.mean(attn * attn, axis=1, keepdims=True) + NORM_EPS)
        xhat = attn * r
        dg_ref[...] += jnp.sum(dan * xhat, axis=0, keepdims=True)
        dang = dan * g_ref[...]
        d_o = r * (dang - xhat * jnp.mean(dang * xhat, axis=1, keepdims=True))
        delta = jnp.dot(d_o * attn, hs_ref[...], preferred_element_type=F32,
                        precision=lax.Precision.HIGHEST)
        do1[...] = d_o.astype(BF16)
        dl1[...] = delta
        for cb in range(w // 128):
            cols = slice(cb * 128, (cb + 1) * 128)
            sdo[...] = d_o[:, cols]
            sdl[...] = delta[:, cols]
            sls[...] = lse_ref[:, cols]
            for d, o_do, o_dl, o_ls in ((4, do4, dl4, ls4), (16, do16, dl16, ls16)):
                for rr in range(d):
                    rows = pl.ds(rr, t // d, stride=d)
                    o_do[rr, :, cols] = sdo[rows, :].astype(BF16)
                    o_dl[rr, :, cols] = sdl[rows, :]
                    o_ls[rr, :, cols] = sls[rows, :]

    row = _spec((t, w), lambda i: (i, 0))
    perm = _permuted_specs(t, w)
    outs = pl.pallas_call(
        body, name=name, grid=(SEQ // t,),
        in_specs=[row, row, row, _spec((1, w), lambda i: (0, 0)), _spec((w, w), lambda i: (0, 0))],
        out_specs=perm + perm + perm[1:] + [_spec((1, w), lambda i: (0, 0))],
        out_shape=(_permuted_shapes(w, BF16) + _permuted_shapes(w, F32) + _permuted_shapes(w, F32)[1:]
                   + [jax.ShapeDtypeStruct((1, w), F32)]),
        scratch_shapes=[pltpu.VMEM((t, 128), F32)] * 3,
        compiler_params=_params(1),
    )(d_an, attn, lse, gain, head_sum)
    flat = lambda arr: arr.reshape(SEQ, w)
    d_out = [flat(a) for a in outs[0:3]]
    delta = [flat(a) for a in outs[3:6]]
    lses = [lse, flat(outs[6]), flat(outs[7])]
    return d_out, delta, lses, outs[8]


def _attn_bwd(name, q, k, v, d_out, delta, lse, seg_blocks):
    n_blocks = SEQ // SPAN

    def body(q_ref, k_ref, v_ref, do_ref, dl_ref, lse_ref, dq_ref, dk_ref, dv_ref):
        prev_band, cur_ok, head0 = _band_masks()
        dk_ref[...] = jnp.zeros_like(dk_ref)
        dv_ref[...] = jnp.zeros_like(dv_ref)

        def step(b, carry):
            r0 = pl.multiple_of(b * SPAN, SPAN)
            p0 = pl.multiple_of(jnp.maximum(b - 1, 0) * SPAN, SPAN)
            cur, prev = pl.ds(r0, SPAN), pl.ds(p0, SPAN)
            qb, dob = q_ref[cur, :], do_ref[cur, :]
            kc, vc, kp, vp = k_ref[cur, :], v_ref[cur, :], k_ref[prev, :], v_ref[prev, :]
            lse_b, dl_b = lse_ref[cur, :], dl_ref[cur, :]
            prev_ok = prev_band & ((b % seg_blocks) != 0)
            dq = jnp.zeros((SPAN, 128), F32)
            dkp = jnp.zeros((SPAN, 128), F32)
            dkc = jnp.zeros((SPAN, 128), F32)
            dvp = jnp.zeros((SPAN, 128), F32)
            dvc = jnp.zeros((SPAN, 128), F32)
            for h, mh in enumerate((head0, ~head0)):
                lane = h * HEAD_DIM
                lse_h, dl_h = lse_b[:, lane:lane + 1], dl_b[:, lane:lane + 1]
                qh = jnp.where(mh, qb, jnp.zeros_like(qb))
                doh = jnp.where(mh, dob, jnp.zeros_like(dob))
                pp = jnp.where(prev_ok, jnp.exp(_dot_nt(qh, kp) - lse_h), 0.0)
                pc = jnp.where(cur_ok, jnp.exp(_dot_nt(qh, kc) - lse_h), 0.0)
                dsp = pp * (_dot_nt(doh, vp) - dl_h)
                dsc = pc * (_dot_nt(doh, vc) - dl_h)
                dq = dq + jnp.where(mh, _dot(dsp, kp) + _dot(dsc, kc), 0.0)
                dkp = dkp + _dot(dsp.T, qh)
                dkc = dkc + _dot(dsc.T, qh)
                dvp = dvp + _dot(pp.T, doh)
                dvc = dvc + _dot(pc.T, doh)
            dq_ref[cur, :] = dq
            dk_ref[prev, :] += dkp
            dv_ref[prev, :] += dvp
            dk_ref[cur, :] += dkc
            dv_ref[cur, :] += dvc
            return carry

        lax.fori_loop(0, n_blocks, step, 0)

    col = _spec((SEQ, 128), lambda j: (0, j))
    return pl.pallas_call(
        body, name=name, grid=(ATTN_WIDTH // 128,),
        in_specs=[col] * 6, out_specs=[col] * 3,
        out_shape=[jax.ShapeDtypeStruct((SEQ, ATTN_WIDTH), F32)] * 3,
        compiler_params=_params(1),
    )(q, k, v, d_out, delta, lse)


def _attn_bwd_post(name, grads, cos_t, sin_t):
    t = ROW_TILE
    w = ATTN_WIDTH

    def body(*refs):
        ins, cos_ref, sin_ref, out_ref, s4, s16 = refs[:9], refs[9], refs[10], refs[11], refs[12], refs[13]
        cosv, sinv = cos_ref[...], sin_ref[...]
        for a in range(3):
            g1, g4, g16 = ins[a], ins[3 + a], ins[6 + a]
            for cb in range(w // 128):
                cols = slice(cb * 128, (cb + 1) * 128)
                _unpermute(s4, g4, 4, cols)
                _unpermute(s16, g16, 16, cols)
                val = g1[:, cols] + s4[...] + s16[...]
                if a < 2:
                    val = val * cosv + _swap_halves(val * sinv)
                if a == 0:
                    val = val * (HEAD_DIM ** -0.5)
                out_ref[:, a * w + cb * 128:a * w + (cb + 1) * 128] = val

    views = []
    for p, d in enumerate(DILATIONS):
        for a in range(3):
            views.append(grads[p][a] if d == 1 else grads[p][a].reshape(d, SEQ // d, w))
    perm = _permuted_specs(t, w)
    in_specs = [perm[0]] * 3 + [perm[1]] * 3 + [perm[2]] * 3
    return pl.pallas_call(
        body, name=name, grid=(SEQ // t,),
        in_specs=in_specs + [_spec((t, 128), lambda i: (i, 0))] * 2,
        out_specs=_spec((t, 3 * w), lambda i: (i, 0)),
        out_shape=jax.ShapeDtypeStruct((SEQ, 3 * w), F32),
        scratch_shapes=[pltpu.VMEM((t, 128), F32)] * 2,
        compiler_params=_params(1),
    )(*views, cos_t, sin_t)


N_LEVELS = 7


def _hgrn_consts():
    c = CHUNK
    i = np.arange(c)[:, None]
    s = np.arange(c)[None, :]
    blocks = [s <= i]
    for lv in range(N_LEVELS):
        bs = c >> lv
        h = bs // 2
        m = (i // bs) * bs + h - 1
        second = (i % bs) >= h
        blocks.append((second & (s > m) & (s <= i)) | (~second & (s > i) & (s <= m)))
    blocks.append(s > i)
    stack = np.concatenate(blocks, axis=0).astype(np.float32)
    return jnp.asarray(stack, dtype=BF16), jnp.asarray(stack.T, dtype=BF16)


def _exact_dot(m01, x):
    hi = x.astype(BF16)
    r1 = x - hi.astype(F32)
    mid = r1.astype(BF16)
    lo = (r1 - mid.astype(F32)).astype(BF16)
    n = x.shape[1]
    full = jnp.dot(m01, jnp.concatenate([hi, mid, lo], axis=1), preferred_element_type=F32)
    return (full[:, :n] + full[:, n:2 * n]) + full[:, 2 * n:]


def _hgrn_gates(qh, z, lb):
    sq = _sigmoid(qh)
    q = qh * sq * (HGRN_DIM ** -0.5)
    sig = _sigmoid(z)
    sigm = _sigmoid(-z)
    f = lb + (1.0 - lb) * sig
    k = (1.0 - lb) * sigm
    return q, k, f, sq, sig, sigm


def _level_masks(lv):
    row = lax.broadcasted_iota(jnp.int32, (CHUNK, CHUNK), 0)
    col = lax.broadcasted_iota(jnp.int32, (CHUNK, CHUNK), 1)
    shift = N_LEVELS - lv
    second = (row & (CHUNK >> (lv + 1))) != 0
    same = (row >> shift) == (col >> shift)
    return second, same


def _hgrn_fwd(name, proj, lb, gain, stack):
    t = ROW_TILE
    per = t // CHUNK
    n_rb = SEQ // t
    n_chunks = SEQ // CHUNK
    col0 = 3 * ATTN_WIDTH // 128

    def body(q_ref, f_ref, i_ref, g_ref, lb_ref, gain_ref, stack_ref,
             rec_ref, rect_ref, o_ref, st_out, a_out, st):
        @pl.when(pl.program_id(1) == 0)
        def _():
            st[...] = jnp.zeros_like(st)

        lbv = lb_ref[...]
        row = lax.broadcasted_iota(jnp.int32, (CHUNK, CHUNK), 0)
        col = lax.broadcasted_iota(jnp.int32, (CHUNK, CHUNK), 1)
        for c in range(per):
            rows = slice(c * CHUNK, (c + 1) * CHUNK)
            qh, z, v, gh = q_ref[rows, :], f_ref[rows, :], i_ref[rows, :], g_ref[rows, :]
            q, k, f, _, _, _ = _hgrn_gates(qh, z, lbv)
            dec = _exact_dot(stack_ref[...], jnp.log(f))
            g = dec[0:CHUNK]
            to_end = dec[(N_LEVELS + 1) * CHUNK:(N_LEVELS + 2) * CHUNK]
            a = jnp.where(row == col, jnp.sum(q * k, axis=1, keepdims=True), 0.0)
            for lv in range(N_LEVELS):
                e = jnp.exp(dec[(lv + 1) * CHUNK:(lv + 2) * CHUNK])
                second, same = _level_masks(lv)
                qt = jnp.where(second, q * e, 0.0)
                kt = jnp.where(second, 0.0, k * e)
                a = a + jnp.where(same, _dot_nt(qt, kt), 0.0)
            st_prev = st[...]
            st_out[c] = st_prev
            a_out[c] = a
            o = _dot(a, v) + _dot_nt(q * jnp.exp(g), st_prev)
            k_end = k * jnp.exp(to_end)
            st[...] = st_prev * jnp.exp(g[CHUNK - 1:CHUNK, :]) + _dot(v.T, k_end)
            o_ref[rows, :] = o
            r = lax.rsqrt(jnp.mean(o * o, axis=1, keepdims=True) + NORM_EPS)
            rec = o * r * gain_ref[...] * (gh * _sigmoid(gh))
            rec_ref[rows, :] = rec.astype(BF16)
            rect_ref[:, rows] = rec.T.astype(BF16)

    def col_spec(tt):
        return _spec((t, HGRN_DIM), lambda h, rb: (rb, col0 + HGRN_HEADS * tt + h))

    chunk_spec = _spec((None, per, CHUNK, CHUNK), lambda h, rb: (h, rb, 0, 0))
    return pl.pallas_call(
        body, name=name, grid=(HGRN_HEADS, n_rb),
        in_specs=[col_spec(0), col_spec(1), col_spec(2), col_spec(3),
                  _spec((None, 1, HGRN_DIM), lambda h, rb: (h, 0, 0)),
                  _spec((1, HGRN_DIM), lambda h, rb: (0, 0)),
                  _spec(stack.shape, lambda h, rb: (0, 0))],
        out_specs=[_spec((t, HGRN_DIM), lambda h, rb: (rb, h)),
                   _spec((HGRN_DIM, t), lambda h, rb: (h, rb)),
                   _spec((t, HGRN_DIM), lambda h, rb: (rb, h)),
                   chunk_spec, chunk_spec],
        out_shape=[jax.ShapeDtypeStruct((SEQ, HGRN_WIDTH), BF16),
                   jax.ShapeDtypeStruct((HGRN_WIDTH, SEQ), BF16),
                   jax.ShapeDtypeStruct((SEQ, HGRN_WIDTH), F32),
                   jax.ShapeDtypeStruct((HGRN_HEADS, n_chunks, CHUNK, CHUNK), F32),
                   jax.ShapeDtypeStruct((HGRN_HEADS, n_chunks, CHUNK, CHUNK), F32)],
        scratch_shapes=[pltpu.VMEM((CHUNK, CHUNK), F32)],
        compiler_params=_params(2),
    )(proj, proj, proj, proj, lb, gain, stack)


def _hgrn_bwd(name, proj, d_rec, o_pre, states, scores, lb, gain, stack, stack_t):
    t = ROW_TILE
    per = t // CHUNK
    n_rb = SEQ // t
    col0 = 3 * ATTN_WIDTH // 128

    def body(q_ref, f_ref, i_ref, g_ref, drec_ref, o_ref, st_ref, a_ref, lb_ref, gain_ref,
             stack_ref, stack_t_ref, dq_ref, df_ref, di_ref, dg_ref, dlb_ref, dgain_ref, dst):
        @pl.when(pl.program_id(1) == 0)
        def _():
            dst[...] = jnp.zeros_like(dst)
            dlb_ref[...] = jnp.zeros_like(dlb_ref)
            dgain_ref[...] = jnp.zeros_like(dgain_ref)

        lbv = lb_ref[...]
        gain_v = gain_ref[...]
        row = lax.broadcasted_iota(jnp.int32, (CHUNK, CHUNK), 0)
        col = lax.broadcasted_iota(jnp.int32, (CHUNK, CHUNK), 1)
        for c in reversed(range(per)):
            rows = slice(c * CHUNK, (c + 1) * CHUNK)
            qh, z, v, gh = q_ref[rows, :], f_ref[rows, :], i_ref[rows, :], g_ref[rows, :]
            q, k, f, sq, sig, sigm = _hgrn_gates(qh, z, lbv)
            dec = _exact_dot(stack_ref[...], jnp.log(f))
            g = dec[0:CHUNK]
            to_end = dec[(N_LEVELS + 1) * CHUNK:(N_LEVELS + 2) * CHUNK]
            e_g = jnp.exp(g)
            e_end = jnp.exp(to_end)
            e_last = jnp.exp(g[CHUNK - 1:CHUNK, :])
            q_in = q * e_g
            k_end = k * e_end
            st_prev = st_ref[c]
            a = a_ref[c]
            dst_new = dst[...]

            o = o_ref[rows, :]
            drec = drec_ref[rows, :]
            sg = _sigmoid(gh)
            r = lax.rsqrt(jnp.mean(o * o, axis=1, keepdims=True) + NORM_EPS)
            ohat = o * r
            d_gh = drec * (ohat * gain_v) * (sg * (1.0 + gh * (1.0 - sg)))
            d_on = drec * (gh * sg)
            dgain_ref[...] += jnp.sum(d_on * ohat, axis=0, keepdims=True)
            d_ohat = d_on * gain_v
            d_o = r * (d_ohat - ohat * jnp.mean(d_ohat * ohat, axis=1, keepdims=True))

            d_a = jnp.where(row >= col, _dot_nt(d_o, v), 0.0)
            d_at = jnp.where(col >= row, _dot_nt(v, d_o), 0.0)
            d_v = _dot(a.T, d_o) + _dot_nt(k_end, dst_new)
            d_q_in = _dot(d_o, st_prev)
            d_k_end = _dot(v, dst_new)
            d_q = d_q_in * e_g
            d_k = d_k_end * e_end
            diag = jnp.sum(d_o * v, axis=1, keepdims=True)
            d_q = d_q + diag * k
            d_k = d_k + diag * q
            d_dec = [q_in * d_q_in]
            for lv in range(N_LEVELS):
                e = jnp.exp(dec[(lv + 1) * CHUNK:(lv + 2) * CHUNK])
                second, same = _level_masks(lv)
                qt = jnp.where(second, q * e, 0.0)
                kt = jnp.where(second, 0.0, k * e)
                d_qt = _dot(jnp.where(same, d_a, 0.0), kt)
                d_kt = _dot(jnp.where(same, d_at, 0.0), qt)
                d_q = d_q + jnp.where(second, d_qt * e, 0.0)
                d_k = d_k + jnp.where(second, 0.0, d_kt * e)
                d_dec.append(jnp.where(second, qt * d_qt, kt * d_kt))
            d_dec.append(k_end * d_k_end)
            flux = jnp.sum(dst_new * st_prev, axis=0, keepdims=True) * e_last
            d_lf = _exact_dot(stack_t_ref[...], jnp.concatenate(d_dec, axis=0)) + flux
            dst[...] = dst_new * e_last + _dot(d_o.T, q_in)

            d_f = d_lf / f - d_k
            dlb_ref[...] += jnp.sum(d_f * sigm, axis=0, keepdims=True)
            dq_ref[rows, :] = d_q * (HGRN_DIM ** -0.5) * (sq * (1.0 + qh * (1.0 - sq)))
            df_ref[rows, :] = d_f * (1.0 - lbv) * sig * sigm
            di_ref[rows, :] = d_v
            dg_ref[rows, :] = d_gh

    last = n_rb - 1

    def col_spec(tt):
        return _spec((t, HGRN_DIM), lambda h, rb: (last - rb, col0 + HGRN_HEADS * tt + h))

    head_col = _spec((t, HGRN_DIM), lambda h, rb: (last - rb, h))
    chunk_spec = _spec((None, per, CHUNK, CHUNK), lambda h, rb: (h, last - rb, 0, 0))
    vec_spec = _spec((None, 1, HGRN_DIM), lambda h, rb: (h, 0, 0))
    outs = pl.pallas_call(
        body, name=name, grid=(HGRN_HEADS, n_rb),
        in_specs=[col_spec(0), col_spec(1), col_spec(2), col_spec(3), head_col, head_col,
                  chunk_spec, chunk_spec, vec_spec,
                  _spec((1, HGRN_DIM), lambda h, rb: (0, 0)),
                  _spec(stack.shape, lambda h, rb: (0, 0)), _spec(stack_t.shape, lambda h, rb: (0, 0))],
        out_specs=[head_col] * 4 + [vec_spec, vec_spec],
        out_shape=[jax.ShapeDtypeStruct((SEQ, HGRN_WIDTH), F32)] * 4
                  + [jax.ShapeDtypeStruct((HGRN_HEADS, 1, HGRN_DIM), F32)] * 2,
        scratch_shapes=[pltpu.VMEM((CHUNK, CHUNK), F32)],
        compiler_params=_params(2),
    )(proj, proj, proj, proj, d_rec, o_pre, states, scores, lb, gain, stack, stack_t)
    return outs


ANY_SPEC = pl.BlockSpec(memory_space=pl.ANY)


def _my_place():
    return lax.axis_index("x"), lax.axis_index("y"), lax.axis_index("c")


def _other_chips(x, y):
    return [(1 - x, y), (x, 1 - y), (1 - x, 1 - y)]


def _remote(src, dst, send_sem, recv_sem, device):
    return pltpu.make_async_remote_copy(src_ref=src, dst_ref=dst, send_sem=send_sem, recv_sem=recv_sem,
                                        device_id=device, device_id_type=MESH)


def _staged_copies(srcs, dsts, stage, sems):
    loads = [pltpu.make_async_copy(srcs[i], stage[i], sems.at[i]) for i in range(len(srcs))]
    for cp in loads:
        cp.start()
    stores = []
    for i, cp in enumerate(loads):
        cp.wait()
        stores.append(pltpu.make_async_copy(stage[i], dsts[i], sems.at[i]))
        stores[-1].start()
    return stores


def _gather_weights(name, shards):
    n = len(shards)

    def body(*refs):
        ins, outs = refs[:n], refs[n:2 * n]
        ici_send, ici_recv, d2d_send, d2d_recv, local_sems = refs[2 * n:2 * n + 5]
        stage = refs[2 * n + 5:]
        x, y, c = _my_place()
        me = 2 * x + y
        chips = _other_chips(x, y)

        def half(i, which):
            h = ins[i].shape[0] // 2
            return pl.ds(which * h, h)

        sends = []
        for i in range(n):
            for j, (px, py) in enumerate(chips):
                sends.append(_remote(ins[i].at[half(i, c), :], outs[i].at[me, half(i, c), :],
                                     ici_send.at[3 * i + j], ici_recv.at[3 * i + j], (px, py, c)))
        for cp in sends:
            cp.start()
        local = _staged_copies(ins, [outs[i].at[me] for i in range(n)], stage, local_sems)
        for i in range(n):
            for j, (px, py) in enumerate(chips):
                landed = outs[i].at[2 * px + py, half(i, c), :]
                _remote(landed, landed, ici_send.at[3 * i + j], ici_recv.at[3 * i + j], (px, py, c)).wait_recv()
                forward = _remote(landed, landed, d2d_send.at[3 * i + j], d2d_recv.at[3 * i + j], (x, y, 1 - c))
                forward.start()
                sends.append(forward)
        for i in range(n):
            for j, (px, py) in enumerate(chips):
                other = outs[i].at[2 * px + py, half(i, 1 - c), :]
                _remote(other, other, d2d_send.at[3 * i + j], d2d_recv.at[3 * i + j], (x, y, 1 - c)).wait_recv()
        for cp in sends:
            cp.wait_send()
        for cp in local:
            cp.wait()

    return pl.pallas_call(
        body, name=name, in_specs=[ANY_SPEC] * n, out_specs=[ANY_SPEC] * n,
        out_shape=[jax.ShapeDtypeStruct((N_CHIPS,) + s.shape, s.dtype) for s in shards],
        scratch_shapes=([pltpu.SemaphoreType.DMA((3 * n,))] * 4 + [pltpu.SemaphoreType.DMA((n,))]
                        + [pltpu.VMEM(s.shape, s.dtype) for s in shards]),
        compiler_params=pltpu.CompilerParams(vmem_limit_bytes=VMEM_LIMIT),
    )(*shards)


def _exchange_halves(name, grads):
    n = len(grads)

    def body(*refs):
        ins, outs = refs[:n], refs[n:2 * n]
        send_sems, recv_sems = refs[2 * n:]
        x, y, c = _my_place()
        copies = []
        for i in range(n):
            h = ins[i].shape[1] // 2
            copies.append(_remote(ins[i].at[:, pl.ds((1 - c) * h, h), :], outs[i],
                                  send_sems.at[i], recv_sems.at[i], (x, y, 1 - c)))
        for cp in copies:
            cp.start()
        for cp in copies:
            cp.wait()

    return pl.pallas_call(
        body, name=name, in_specs=[ANY_SPEC] * n, out_specs=[ANY_SPEC] * n,
        out_shape=[jax.ShapeDtypeStruct((g.shape[0], g.shape[1] // 2, g.shape[2]), g.dtype) for g in grads],
        scratch_shapes=[pltpu.SemaphoreType.DMA((n,)), pltpu.SemaphoreType.DMA((n,))],
    )(*grads)


def _add_own_half(name, g, received, core):
    n_sh, r, cc = g.shape
    h = r // 2
    th = min(h, 256)
    nb = h // th

    def body(core_ref, g_ref, r_ref, o_ref):
        del core_ref
        o_ref[...] = (g_ref[...] + r_ref[...]).astype(BF16)

    grid_spec = pltpu.PrefetchScalarGridSpec(
        num_scalar_prefetch=1, grid=(n_sh, nb),
        in_specs=[pl.BlockSpec((None, th, cc), lambda j, i, core_ref: (j, core_ref[0] * nb + i, 0)),
                  pl.BlockSpec((None, th, cc), lambda j, i, core_ref: (j, i, 0))],
        out_specs=pl.BlockSpec((None, th, cc), lambda j, i, core_ref: (j, i, 0)))
    return pl.pallas_call(
        body, name=name, grid_spec=grid_spec,
        out_shape=jax.ShapeDtypeStruct((n_sh, h, cc), BF16), compiler_params=_params(2),
    )(core, g, received)


def _exchange_chips(name, parts):
    n = len(parts)

    def body(*refs):
        ins, outs = refs[:n], refs[n:2 * n]
        send_sems, recv_sems, local_sems = refs[2 * n:2 * n + 3]
        stage = refs[2 * n + 3:]
        x, y, c = _my_place()
        me = 2 * x + y
        chips = _other_chips(x, y)
        sends = []
        for i in range(n):
            for j, (px, py) in enumerate(chips):
                sends.append(_remote(ins[i].at[2 * px + py], outs[i].at[me], send_sems.at[3 * i + j],
                                     recv_sems.at[3 * i + j], (px, py, c)))
        for cp in sends:
            cp.start()
        local = _staged_copies([ins[i].at[me] for i in range(n)], [outs[i].at[me] for i in range(n)],
                               stage, local_sems)
        for i in range(n):
            for j, (px, py) in enumerate(chips):
                _remote(ins[i].at[me], outs[i].at[2 * px + py], send_sems.at[3 * i + j],
                        recv_sems.at[3 * i + j], (px, py, c)).wait_recv()
        for cp in sends:
            cp.wait_send()
        for cp in local:
            cp.wait()

    return pl.pallas_call(
        body, name=name, in_specs=[ANY_SPEC] * n, out_specs=[ANY_SPEC] * n,
        out_shape=[jax.ShapeDtypeStruct(p.shape, p.dtype) for p in parts],
        scratch_shapes=([pltpu.SemaphoreType.DMA((3 * n,)), pltpu.SemaphoreType.DMA((3 * n,)),
                         pltpu.SemaphoreType.DMA((n,))]
                        + [pltpu.VMEM(p.shape[1:], p.dtype) for p in parts]),
        compiler_params=pltpu.CompilerParams(vmem_limit_bytes=VMEM_LIMIT),
    )(*parts)


HBM_SPEC = pl.BlockSpec(memory_space=pltpu.HBM)
SEM_SPEC = pl.BlockSpec(memory_space=pltpu.SEMAPHORE)
SPLIT_PARAMS = pltpu.CompilerParams(has_side_effects=pltpu.SideEffectType.DATAFLOW_SIDE_EFFECTING)


def _chip_copies(ins, lands, send_sems, recv_sems, sliced):
    x, y, c = _my_place()
    me = 2 * x + y
    pairs = []
    for i in range(len(ins)):
        for j, (px, py) in enumerate(_other_chips(x, y)):
            theirs = 2 * px + py
            src = ins[i].at[theirs] if sliced else ins[i]
            sems = (send_sems.at[3 * i + j], recv_sems.at[3 * i + j], (px, py, c))
            pairs.append((_remote(src, lands[i].at[me], *sems), _remote(src, lands[i].at[theirs], *sems)))
    return pairs


def _exchange_start(name, srcs, lands, sliced, after):
    n = len(srcs)

    def body(*refs):
        ins, land_refs = refs[:n], refs[n:2 * n]
        send_sems, recv_sems = refs[2 * n + 1:2 * n + 3]
        token = refs[-1]
        for send, _ in _chip_copies(ins, land_refs, send_sems, recv_sems, sliced):
            send.start()
        token[...] = jnp.zeros_like(token)

    arrays = list(srcs) + list(lands)
    outs = pl.pallas_call(
        body, name=name,
        in_specs=[HBM_SPEC] * (2 * n) + [ANY_SPEC],
        out_shape=([pltpu.SemaphoreType.DMA((3 * n,))] * 2 + [pltpu.HBM(a.shape, a.dtype) for a in arrays]
                   + [jax.ShapeDtypeStruct((8, 128), F32)]),
        out_specs=[SEM_SPEC] * 2 + [HBM_SPEC] * (2 * n) + [pl.BlockSpec(memory_space=pltpu.VMEM)],
        input_output_aliases={i: 2 + i for i in range(2 * n)},
        compiler_params=SPLIT_PARAMS,
    )(*[pltpu.with_memory_space_constraint(a, pltpu.HBM) for a in arrays], after)
    return outs[:2], outs[2:2 + 2 * n], outs[-1]


def _exchange_wait(name, sems, passed, sliced, after):
    n = len(passed) // 2

    def body(*refs):
        ins, land_refs = refs[:n], refs[n:2 * n]
        send_sems, recv_sems = refs[2 * n:2 * n + 2]
        for send, arrive in _chip_copies(ins, land_refs, send_sems, recv_sems, sliced):
            send.wait_send()
            arrive.wait_recv()

    outs = pl.pallas_call(
        body, name=name,
        in_specs=[HBM_SPEC] * (2 * n) + [SEM_SPEC] * 2 + [ANY_SPEC],
        out_shape=[pltpu.HBM(a.shape, a.dtype) for a in passed],
        out_specs=[HBM_SPEC] * (2 * n),
        input_output_aliases={i: i for i in range(2 * n)},
        compiler_params=SPLIT_PARAMS,
    )(*passed, *sems, after)
    return outs[n:]


def _own_slot(own, me):
    land = lax.empty((N_CHIPS,) + own.shape, own.dtype)
    return lax.dynamic_update_index_in_dim(land, own, me, 0)


def _sum_chips(name, parts):
    n_sh, h, cc = parts.shape
    th = min(h, 256)

    def body(p_ref, o_ref):
        p = [p_ref[j].astype(F32) for j in range(n_sh)]
        o_ref[...] = ((p[0] + p[1]) + p[2]) + p[3]

    return pl.pallas_call(
        body, name=name, grid=(h // th,),
        in_specs=[_spec((n_sh, th, cc), lambda i: (0, i, 0))],
        out_specs=_spec((th, cc), lambda i: (i, 0)),
        out_shape=jax.ShapeDtypeStruct((h, cc), F32), compiler_params=_params(1),
    )(parts)


def _share_halves(halves):
    flat = [t for per_weight in halves for t in per_weight]
    n = len(flat)
    n_w = len(halves)

    def body(*refs):
        ins, outs = refs[:n], refs[n:n + n_w]
        send_sems, recv_sems, local_sems = refs[n + n_w:n + n_w + 3]
        stage = refs[n + n_w + 3:]
        x, y, c = _my_place()
        sends, own = [], []
        for i in range(n):
            w, l = divmod(i, DEPTH)
            h = ins[i].shape[0]
            own.append(outs[w].at[l, pl.ds(c * h, h), :])
            sends.append(_remote(ins[i], own[i], send_sems.at[i], recv_sems.at[i], (x, y, 1 - c)))
        for cp in sends:
            cp.start()
        local = _staged_copies(ins, own, stage, local_sems)
        for i in range(n):
            w, l = divmod(i, DEPTH)
            h = ins[i].shape[0]
            _remote(ins[i], outs[w].at[l, pl.ds((1 - c) * h, h), :], send_sems.at[i], recv_sems.at[i],
                    (x, y, 1 - c)).wait_recv()
        for cp in sends:
            cp.wait_send()
        for cp in local:
            cp.wait()

    return pl.pallas_call(
        body, name="share_halves", in_specs=[ANY_SPEC] * n, out_specs=[ANY_SPEC] * n_w,
        out_shape=[jax.ShapeDtypeStruct((DEPTH, 2 * per_weight[0].shape[0], per_weight[0].shape[1]), F32)
                   for per_weight in halves],
        scratch_shapes=([pltpu.SemaphoreType.DMA((n,))] * 3 + [pltpu.VMEM(t.shape, t.dtype) for t in flat]),
        compiler_params=pltpu.CompilerParams(vmem_limit_bytes=VMEM_LIMIT),
    )(*flat)


def _all_reduce_small(pack):
    def body(p_ref, o_ref, recv, send_sems, recv_sems):
        x, y, c = _my_place()
        me = 4 * x + 2 * y + c
        recv[me] = p_ref[...]
        peers = []
        for k in range(1, N_DEV):
            px, py, pc = (x + (k >> 2)) % 2, (y + ((k >> 1) & 1)) % 2, (c + (k & 1)) % 2
            peers.append((px, py, pc))
        sends = [_remote(p_ref, recv.at[me], send_sems.at[k], recv_sems.at[k], peer)
                 for k, peer in enumerate(peers)]
        for cp in sends:
            cp.start()
        for k, (px, py, pc) in enumerate(peers):
            _remote(p_ref, recv.at[4 * px + 2 * py + pc], send_sems.at[k], recv_sems.at[k],
                    (px, py, pc)).wait_recv()
        for cp in sends:
            cp.wait_send()
        total = recv[0]
        for d in range(1, N_DEV):
            total = total + recv[d]
        o_ref[...] = total

    vmem = pl.BlockSpec(memory_space=pltpu.VMEM)
    return pl.pallas_call(
        body, name="all_reduce_small", in_specs=[vmem], out_specs=vmem,
        out_shape=jax.ShapeDtypeStruct(pack.shape, F32),
        scratch_shapes=[pltpu.VMEM((N_DEV,) + pack.shape, F32),
                        pltpu.SemaphoreType.DMA((N_DEV - 1,)), pltpu.SemaphoreType.DMA((N_DEV - 1,))],
    )(pack)


def _adamw(name, w, g, m, v):
    r, cc = w.shape
    th = min(r, 256)

    def body(w_ref, g_ref, m_ref, v_ref, d_ref, m_out, v_out):
        gv = g_ref[...]
        m2 = ADAM_B1 * m_ref[...] + (1.0 - ADAM_B1) * gv
        v2 = ADAM_B2 * v_ref[...] + (1.0 - ADAM_B2) * (gv * gv)
        m_hat = m2 / (1.0 - ADAM_B1 ** ADAM_STEP)
        v_hat = v2 / (1.0 - ADAM_B2 ** ADAM_STEP)
        d_ref[...] = -ADAM_LR * (m_hat / (jnp.sqrt(v_hat) + ADAM_EPS) + ADAM_WD * w_ref[...])
        m_out[...] = m2
        v_out[...] = v2

    tile = _spec((th, cc), lambda i: (i, 0))
    return pl.pallas_call(
        body, name=name, grid=(r // th,), in_specs=[tile] * 4, out_specs=[tile] * 3,
        out_shape=[jax.ShapeDtypeStruct((r, cc), F32)] * 3, compiler_params=_params(1),
    )(w, g, m, v)


def _lower_bounds(lb_logits):
    p = jax.nn.softmax(lb_logits.astype(F32), axis=0)
    return jnp.cumsum(p, axis=0) - p[0]


def _row_tile_specs(tm, width):
    return _spec((tm, width), lambda i, j, k: (i, 0))


def _layer_forward(l, x_in, small, weights, consts, after=None):
    win, wo, wu, wd = weights
    cos_t, sin_t, stack, _, _ = consts
    tm = MM_TILE
    n_row = SEQ // tm
    saved = {"x_in": x_in}

    h, h_t = _rms_fwd(f"norm_mix{l}", x_in, small["norm_mix"][l][None, :], after=after)
    proj = _matmul(f"proj{l}", h, win,
                   _spec((tm, D_MODEL), lambda i, j, k: (i, 0)),
                   _spec((None, D_MODEL, SHARD_IN), lambda i, j, k: (j, 0, 0)),
                   (SEQ, IN_W), F32, _spec((tm, SHARD_IN), lambda i, j, k: (i, j)),
                   (n_row, N_CHIPS, 1), (tm, SHARD_IN))
    saved.update(h_t=h_t, proj=proj)

    qkv = _attn_prep(f"attn_prep{l}", proj, cos_t, sin_t)
    outs, lses = [], []
    for p, d in enumerate(DILATIONS):
        o, lse = _attn_fwd(f"attn_fwd{l}_{d}", *qkv[p], SEQ // d // SPAN)
        outs.append(o)
        lses.append(lse)
    an, an_t, attn, lse = _attn_merge(f"attn_merge{l}", outs, lses, small["attn_out_gain"][l][None, :])
    saved.update(qkv=qkv, an_t=an_t, attn=attn, lse=lse)

    lb3 = small["lower"][l].reshape(HGRN_HEADS, 1, HGRN_DIM)
    rec, rec_t, o_pre, states, scores = _hgrn_fwd(f"hgrn_fwd{l}", proj, lb3,
                                                  small["hgrn_out_gain"][l][None, :], stack)
    saved.update(rec_t=rec_t, o_pre=o_pre, states=states, scores=scores, lb3=lb3)

    def out_proj(name, a, part, resid):
        return _matmul(name, a, wo,
                       _spec((tm, SHARD_OUT), lambda i, j, k: (i, k)),
                       _spec((None, SHARD_OUT, D_MODEL), lambda i, j, k: (k + 2 * part, 0, 0)),
                       (SEQ, D_MODEL), F32, _spec((tm, D_MODEL), lambda i, j, k: (i, 0)),
                       (n_row, 1, 2), (tm, D_MODEL),
                       extra=resid, extra_spec=_spec((tm, D_MODEL), lambda i, j, k: (i, 0)), epilogue="add")

    x_mid = out_proj(f"out_rec{l}", rec, 1, out_proj(f"out_attn{l}", an, 0, x_in))
    saved["x_mid"] = x_mid

    h2, h2_t = _rms_fwd(f"norm_mlp{l}", x_mid, small["norm_mlp"][l][None, :])
    u = _matmul(f"up{l}", h2, wu,
                _spec((tm, D_MODEL), lambda i, j, k: (i, 0)),
                _spec((None, D_MODEL, SHARD_MLP), lambda i, j, k: (j, 0, 0)),
                (SEQ, MLP_HIDDEN), F32, _spec((tm, SHARD_MLP), lambda i, j, k: (i, j)),
                (n_row, N_CHIPS, 1), (tm, SHARD_MLP))
    a, a_t = _relu2(f"relu2_{l}", u)
    x_out = _matmul(f"down{l}", a, wd,
                    _spec((tm, SHARD_MLP), lambda i, j, k: (i, k)),
                    _spec((None, SHARD_MLP, D_MODEL), lambda i, j, k: (k, 0, 0)),
                    (SEQ, D_MODEL), F32, _spec((tm, D_MODEL), lambda i, j, k: (i, 0)),
                    (n_row, 1, N_CHIPS), (tm, D_MODEL),
                    extra=x_mid, extra_spec=_spec((tm, D_MODEL), lambda i, j, k: (i, 0)), epilogue="add")
    saved.update(h2_t=h2_t, u=u, a_t=a_t)
    return x_out, saved


def _layer_backward(l, dx, saved, small, weights, consts, after=None):
    win, wo, wu, wd = weights
    cos_t, sin_t, stack, stack_t, head_sum = consts
    tm = MM_TILE
    n_row = SEQ // tm
    n_k = SEQ // tm

    du = _matmul(f"d_u{l}", dx, wd,
                 _spec((tm, D_MODEL), lambda i, j, k: (i, 0)),
                 _spec((None, SHARD_MLP, D_MODEL), lambda i, j, k: (j, 0, 0)),
                 (SEQ, MLP_HIDDEN), BF16, _spec((tm, SHARD_MLP), lambda i, j, k: (i, j)),
                 (n_row, N_CHIPS, 1), (tm, SHARD_MLP), nt=True,
                 extra=saved["u"], extra_spec=_spec((tm, SHARD_MLP), lambda i, j, k: (i, j)),
                 epilogue="relu2_grad", after=after)
    d_wd = _matmul(f"d_wdown{l}", saved["a_t"], dx,
                   _spec((SHARD_MLP, tm), lambda i, j, k: (i, k)),
                   _spec((tm, D_MODEL), lambda i, j, k: (k, 0)),
                   (N_CHIPS, SHARD_MLP, D_MODEL), F32, _spec((None, SHARD_MLP, D_MODEL), lambda i, j, k: (i, 0, 0)),
                   (N_CHIPS, 1, n_k), (SHARD_MLP, D_MODEL))
    dh2 = _matmul(f"d_h2_{l}", du, wu,
                  _spec((tm, SHARD_MLP), lambda i, j, k: (i, k)),
                  _spec((None, D_MODEL, SHARD_MLP), lambda i, j, k: (k, 0, 0)),
                  (SEQ, D_MODEL), F32, _spec((tm, D_MODEL), lambda i, j, k: (i, 0)),
                  (n_row, 1, N_CHIPS), (tm, D_MODEL), nt=True)
    d_wu = _matmul(f"d_wup{l}", saved["h2_t"], du,
                   _spec((D_MODEL, tm), lambda i, j, k: (0, k)),
                   _spec((tm, SHARD_MLP), lambda i, j, k: (k, j)),
                   (N_CHIPS, D_MODEL, SHARD_MLP), F32, _spec((None, D_MODEL, SHARD_MLP), lambda i, j, k: (j, 0, 0)),
                   (1, N_CHIPS, n_k), (D_MODEL, SHARD_MLP))
    dxm, dg_mlp = _rms_bwd(f"norm_mlp_bwd{l}", dh2, saved["x_mid"], small["norm_mlp"][l][None, :], dx)

    def d_mixed(name, part):
        return _matmul(name, dxm, wo,
                       _spec((tm, D_MODEL), lambda i, j, k: (i, 0)),
                       _spec((None, SHARD_OUT, D_MODEL), lambda i, j, k: (j + 2 * part, 0, 0)),
                       (SEQ, ATTN_WIDTH), F32, _spec((tm, SHARD_OUT), lambda i, j, k: (i, j)),
                       (n_row, 2, 1), (tm, SHARD_OUT), nt=True)

    def d_wout(name, a_t):
        return _matmul(name, a_t, dxm,
                       _spec((SHARD_OUT, tm), lambda i, j, k: (i, k)),
                       _spec((tm, D_MODEL), lambda i, j, k: (k, 0)),
                       (2, SHARD_OUT, D_MODEL), F32, _spec((None, SHARD_OUT, D_MODEL), lambda i, j, k: (i, 0, 0)),
                       (2, 1, n_k), (SHARD_OUT, D_MODEL))

    d_an = d_mixed(f"d_attn_n{l}", 0)
    d_rec = d_mixed(f"d_rec{l}", 1)
    d_wo = jnp.concatenate([d_wout(f"d_wout_attn{l}", saved["an_t"]),
                            d_wout(f"d_wout_rec{l}", saved["rec_t"])], axis=0)

    d_out, delta, lses, dg_attn = _attn_bwd_prep(f"attn_bwd_prep{l}", d_an, saved["attn"], saved["lse"],
                                                 small["attn_out_gain"][l][None, :], head_sum)
    grads = []
    for p, d in enumerate(DILATIONS):
        grads.append(_attn_bwd(f"attn_bwd{l}_{d}", *saved["qkv"][p], d_out[p], delta[p], lses[p],
                               SEQ // d // SPAN))
    dp_attn = _attn_bwd_post(f"attn_bwd_post{l}", grads, cos_t, sin_t)

    dq_h, df_h, di_h, dg_h, d_lower, dg_hgrn = _hgrn_bwd(
        f"hgrn_bwd{l}", saved["proj"], d_rec, saved["o_pre"], saved["states"], saved["scores"],
        saved["lb3"], small["hgrn_out_gain"][l][None, :], stack, stack_t)
    dproj = jnp.concatenate([dp_attn, dq_h, df_h, di_h, dg_h], axis=1)

    dh = _matmul(f"d_h{l}", dproj, win,
                 _spec((tm, SHARD_IN), lambda i, j, k: (i, k)),
                 _spec((None, D_MODEL, SHARD_IN), lambda i, j, k: (k, 0, 0)),
                 (SEQ, D_MODEL), F32, _spec((tm, D_MODEL), lambda i, j, k: (i, 0)),
                 (n_row, 1, N_CHIPS), (tm, D_MODEL), nt=True)
    d_win = _matmul(f"d_win{l}", saved["h_t"], dproj,
                    _spec((D_MODEL, tm), lambda i, j, k: (0, k)),
                    _spec((tm, SHARD_IN), lambda i, j, k: (k, j)),
                    (N_CHIPS, D_MODEL, SHARD_IN), F32, _spec((None, D_MODEL, SHARD_IN), lambda i, j, k: (j, 0, 0)),
                    (1, N_CHIPS, n_k), (D_MODEL, SHARD_IN))
    dx_in, dg_mix = _rms_bwd(f"norm_mix_bwd{l}", dh, saved["x_in"], small["norm_mix"][l][None, :], dxm)

    small_grads = {"norm_mix": dg_mix[0], "attn_out_gain": dg_attn[0],
                   "lower": d_lower.reshape(HGRN_WIDTH),
                   "hgrn_out_gain": jnp.sum(dg_hgrn, axis=0).reshape(HGRN_DIM), "norm_mlp": dg_mlp[0]}
    return dx_in, (d_win, d_wo, d_wu, d_wd), small_grads


def _local_step(xs, target, small, get_weights, on_grads):
    consts = _rope_tables() + _hgrn_consts() + (_head_sum_matrix(),)
    stream = xs
    saved, weights = [], []
    for l in range(DEPTH):
        w, after = get_weights(l, stream)
        weights.append(w)
        stream, s = _layer_forward(l, stream, small, w, consts, after=after)
        saved.append(s)
    dx, dg_final, loss = _loss_head(stream, small["norm_final"][None, :], target)
    small_grads = [None] * DEPTH
    after = None
    for l in reversed(range(DEPTH)):
        dx, big, small_grads[l] = _layer_backward(l, dx, saved[l], small, weights[l], consts, after=after)
        after = on_grads(l, big)
    return loss, dx, dg_final[0], small_grads


def _pack_small(norm_mix, attn_out_gain, lb, hgrn_out_gain, norm_mlp, norm_final, last_row):
    rows = [norm_mix, attn_out_gain.reshape(1, D_MODEL), lb.reshape(1, D_MODEL),
            jnp.pad(hgrn_out_gain.reshape(1, DEPTH * HGRN_DIM), ((0, 0), (0, D_MODEL - DEPTH * HGRN_DIM))),
            norm_mlp, norm_final.reshape(1, D_MODEL), last_row.reshape(1, D_MODEL)]
    pack = jnp.concatenate(rows, axis=0)
    return jnp.pad(pack, ((0, PACK_ROWS - pack.shape[0]), (0, 0)))


def _unpack_small(pack):
    return (pack[0:2], pack[2].reshape(DEPTH, ATTN_WIDTH), pack[3].reshape(DEPTH, HGRN_WIDTH),
            pack[4, :DEPTH * HGRN_DIM].reshape(DEPTH, HGRN_DIM), pack[5:7], pack[7], pack[8])


def kernel(x, norm_mix, w_in, attn_out_gain, hgrn_lb_logits, hgrn_out_gain, w_out, norm_mlp, w_up, w_down, norm_final, loss_target, m_norm_mix, m_w_in, m_attn_out_gain, m_hgrn_lb_logits, m_hgrn_out_gain, m_w_out, m_norm_mlp, m_w_up, m_w_down, m_norm_final, v_norm_mix, v_w_in, v_attn_out_gain, v_hgrn_lb_logits, v_hgrn_out_gain, v_w_out, v_norm_mlp, v_w_up, v_w_down, v_norm_final):
    core = lax.axis_index("c").astype(jnp.int32).reshape(1)
    lower, lower_vjp = jax.vjp(_lower_bounds, hgrn_lb_logits)
    small = {"norm_mix": norm_mix, "attn_out_gain": attn_out_gain, "lower": lower,
             "hgrn_out_gain": hgrn_out_gain, "norm_mlp": norm_mlp, "norm_final": norm_final}
    big_w = (w_in, w_out, w_up, w_down)

    me = 2 * lax.axis_index("x") + lax.axis_index("y")
    shards = [[w[l].astype(BF16) for w in big_w] for l in range(DEPTH)]
    in_flight = {}

    def get_weights(l, stream):
        if l == 0:
            weights = _gather_weights("gather_weights0", shards[0])
            lands = [_own_slot(s, me) for s in shards[1]]
            sems, passed, token = _exchange_start("gather_start1", shards[1], lands, False, weights[0])
            in_flight["weights"] = (sems, passed)
            return weights, token
        sems, passed = in_flight.pop("weights")
        return _exchange_wait("gather_wait1", sems, passed, False, stream), None

    reduced = [None] * DEPTH

    def halves_of(l, grads):
        received = _exchange_halves(f"exchange_halves{l}", grads)
        return [_add_own_half(f"add_halves{l}_{i}", g, r, core) for i, (g, r) in enumerate(zip(grads, received))]

    def on_grads(l, grads):
        halves = halves_of(l, grads)
        if l == 1:
            lands = [_own_slot(lax.dynamic_index_in_dim(h, me, 0, keepdims=False), me) for h in halves]
            sems, passed, token = _exchange_start("exchange_start1", halves, lands, True, halves[0])
            in_flight["grads"] = (sems, passed)
            return token
        sems, passed = in_flight.pop("grads")
        landed = _exchange_wait("exchange_wait1", sems, passed, True, grads[0])
        reduced[1] = [_sum_chips(f"sum_chips1_{i}", p) for i, p in enumerate(landed)]
        from_chips = _exchange_chips("exchange_chips0", halves)
        reduced[0] = [_sum_chips(f"sum_chips0_{i}", p) for i, p in enumerate(from_chips)]
        return None

    loss, dx, dg_final, sg = _local_step(x[0], loss_target[0], small, get_weights, on_grads)

    stack2 = lambda key: jnp.stack([sg[l][key] for l in range(DEPTH)])
    pack = _pack_small(stack2("norm_mix"), stack2("attn_out_gain"), stack2("lower"), stack2("hgrn_out_gain"),
                       stack2("norm_mlp"), dg_final, jnp.broadcast_to(loss[0, 0], (D_MODEL,)))
    g_mix, g_attn, g_lower, g_hgrn, g_mlp, g_final, loss_row = _unpack_small(_all_reduce_small(pack))
    (g_logits,) = lower_vjp(g_lower)

    zeros_row = jnp.zeros((D_MODEL,), F32)
    small_w = (norm_mix, attn_out_gain, hgrn_lb_logits, hgrn_out_gain, norm_mlp, norm_final)
    small_m = (m_norm_mix, m_attn_out_gain, m_hgrn_lb_logits, m_hgrn_out_gain, m_norm_mlp, m_norm_final)
    small_v = (v_norm_mix, v_attn_out_gain, v_hgrn_lb_logits, v_hgrn_out_gain, v_norm_mlp, v_norm_final)
    small_g = (g_mix, g_attn, g_logits, g_hgrn, g_mlp, g_final)
    packs = [_pack_small(*t, zeros_row) for t in (small_w, small_g, small_m, small_v)]
    small_delta, small_new_m, small_new_v = [_unpack_small(p)[:6] for p in _adamw("adamw_small", *packs)]

    big_g = _share_halves([[reduced[l][w] for l in range(DEPTH)] for w in range(4)])

    big_m = (m_w_in, m_w_out, m_w_up, m_w_down)
    big_v = (v_w_in, v_w_out, v_w_up, v_w_down)
    big_delta, big_new_m, big_new_v = [], [], []
    for i, name in enumerate(("w_in", "w_out", "w_up", "w_down")):
        shape = big_w[i].shape
        flat = lambda arr: arr.reshape(shape[0] * shape[1], shape[2])
        d, m2, v2 = _adamw(f"adamw_{name}", flat(big_w[i]), flat(big_g[i]), flat(big_m[i]), flat(big_v[i]))
        big_delta.append(d.reshape(shape))
        big_new_m.append(m2.reshape(shape))
        big_new_v.append(v2.reshape(shape))

    def ordered(small6, big4):
        mix, attn, lbl, hg, mlp, fin = small6
        return (mix, big4[0], attn, lbl, hg, big4[1], mlp, big4[2], big4[3], fin)

    return ((loss_row[0], dx[None]) + ordered(small_g, big_g) + ordered(small_delta, big_delta)
            + ordered(small_new_m, big_new_m) + ordered(small_new_v, big_new_v))
```

```python
import functools
import math

import numpy as np
import jax
import jax.numpy as jnp
from jax import lax
from jax.experimental import pallas as pl
from jax.experimental.pallas import tpu as pltpu

F32 = jnp.float32
BF16 = jnp.bfloat16
MESH = pl.DeviceIdType.MESH

SEQ = 4096
D_MODEL = 1024
DEPTH = 2
ATTN_WIDTH = 512
HEAD_DIM = 64
HGRN_HEADS = 4
HGRN_DIM = 128
HGRN_WIDTH = 512
IN_W = 3584
MLP_HIDDEN = 4096
N_CHIPS = 4
N_DEV = 8
SHARD_IN = IN_W // N_CHIPS
SHARD_OUT = D_MODEL // N_CHIPS
SHARD_MLP = MLP_HIDDEN // N_CHIPS
DILATIONS = (1, 4, 16)
SPAN = 128
ROPE_THETA = 10000.0
NORM_EPS = 1e-6
MASK_VALUE = -1e30
CHUNK = 128
ROW_TILE = 512
MM_TILE = 1024
VMEM_LIMIT = 52 * 1024 * 1024

ADAM_LR = 0.001
ADAM_B1 = 0.9
ADAM_B2 = 0.999
ADAM_EPS = 1e-08
ADAM_WD = 0.01
ADAM_STEP = 10

PACK_ROWS = 16


def _params(n_axes):
    return pltpu.CompilerParams(dimension_semantics=("arbitrary",) * n_axes,
                                vmem_limit_bytes=VMEM_LIMIT)


def _dot(a, b):
    return jnp.dot(a.astype(BF16), b.astype(BF16), preferred_element_type=F32)


def _dot_nt(a, b):
    return lax.dot_general(a.astype(BF16), b.astype(BF16), (((1,), (1,)), ((), ())),
                           preferred_element_type=F32)


def _sigmoid(x):
    return 1.0 / (1.0 + jnp.exp(-x))


def _matmul(name, a, b, a_spec, b_spec, out_shape, out_dtype, out_spec, grid, acc_shape,
            nt=False, extra=None, extra_spec=None, epilogue="none", after=None, relu_outs=None):
    nk = grid[2]
    n_out = 1 if relu_outs is None else 3

    def body(*refs):
        a_ref, b_ref = refs[:2]
        e_ref = None if extra is None else refs[2]
        o_ref = refs[-1 - n_out]
        acc = refs[-1]
        kk = pl.program_id(2)

        @pl.when(kk == 0)
        def _():
            acc[...] = jnp.zeros_like(acc)

        if nt:
            acc[...] += _dot_nt(a_ref[...], b_ref[...])
        else:
            acc[...] += _dot(a_ref[...], b_ref[...])

        @pl.when(kk == nk - 1)
        def _():
            r = acc[...]
            if epilogue == "add":
                r = r + e_ref[...]
            elif epilogue == "relu2_grad":
                r = r * (2.0 * e_ref[...].astype(F32))
            elif epilogue == "relu2":
                s = jnp.maximum(r, 0.0)
                r = s * s
                refs[-3][...] = s.astype(out_dtype)
                refs[-2][...] = r.T.astype(out_dtype)
            o_ref[...] = r.astype(o_ref.dtype)

    in_specs = [a_spec, b_spec] + ([] if extra is None else [extra_spec])
    args = (a, b) + (() if extra is None else (extra,))
    if after is not None:
        in_specs.append(pl.BlockSpec(memory_space=pl.ANY))
        args += (after,)
    out_specs, out_shapes = out_spec, jax.ShapeDtypeStruct(out_shape, out_dtype)
    if relu_outs is not None:
        out_specs = [out_spec] + [spec for _, spec in relu_outs]
        out_shapes = [out_shapes] + [jax.ShapeDtypeStruct(shape, out_dtype) for shape, _ in relu_outs]
    return pl.pallas_call(
        body, name=name, grid=grid, in_specs=in_specs, out_specs=out_specs, out_shape=out_shapes,
        scratch_shapes=[pltpu.VMEM(acc_shape, F32)],
        compiler_params=_params(3),
    )(*args)


def _spec(shape, index_map):
    return pl.BlockSpec(shape, index_map)


def _rms_fwd(name, x, gain, after=None):
    s, d = x.shape
    t = ROW_TILE

    def body(x_ref, g_ref, *rest):
        h_ref, ht_ref = rest[-2:]
        xv = x_ref[...]
        r = lax.rsqrt(jnp.mean(xv * xv, axis=1, keepdims=True) + NORM_EPS)
        h = xv * r * g_ref[...]
        h_ref[...] = h.astype(BF16)
        ht_ref[...] = h.T.astype(BF16)

    in_specs = [_spec((t, d), lambda i: (i, 0)), _spec((1, d), lambda i: (0, 0))]
    args = (x, gain)
    if after is not None:
        in_specs.append(pl.BlockSpec(memory_space=pl.ANY))
        args += (after,)
    return pl.pallas_call(
        body, name=name, grid=(s // t,), in_specs=in_specs,
        out_specs=[_spec((t, d), lambda i: (i, 0)), _spec((d, t), lambda i: (0, i))],
        out_shape=[jax.ShapeDtypeStruct((s, d), BF16), jax.ShapeDtypeStruct((d, s), BF16)],
        compiler_params=_params(1),
    )(*args)


def _rms_bwd(name, dh, x, gain, dres):
    s, d = x.shape
    t = ROW_TILE

    def body(dh_ref, x_ref, g_ref, dres_ref, dx_ref, dg_ref):
        @pl.when(pl.program_id(0) == 0)
        def _():
            dg_ref[...] = jnp.zeros_like(dg_ref)

        xv = x_ref[...]
        dhv = dh_ref[...]
        r = lax.rsqrt(jnp.mean(xv * xv, axis=1, keepdims=True) + NORM_EPS)
        xhat = xv * r
        dhg = dhv * g_ref[...]
        proj = jnp.mean(dhg * xhat, axis=1, keepdims=True)
        dx_ref[...] = dres_ref[...] + r * (dhg - xhat * proj)
        dg_ref[...] += jnp.sum(dhv * xhat, axis=0, keepdims=True)

    return pl.pallas_call(
        body, name=name, grid=(s // t,),
        in_specs=[_spec((t, d), lambda i: (i, 0)), _spec((t, d), lambda i: (i, 0)),
                  _spec((1, d), lambda i: (0, 0)), _spec((t, d), lambda i: (i, 0))],
        out_specs=[_spec((t, d), lambda i: (i, 0)), _spec((1, d), lambda i: (0, 0))],
        out_shape=[jax.ShapeDtypeStruct((s, d), F32), jax.ShapeDtypeStruct((1, d), F32)],
        compiler_params=_params(1),
    )(dh, x, gain, dres)


def _loss_head(x, gain, target):
    s, d = x.shape
    t = ROW_TILE
    n_steps = s // t

    def body(x_ref, g_ref, t_ref, dx_ref, dg_ref, loss_ref, acc):
        i = pl.program_id(0)

        @pl.when(i == 0)
        def _():
            dg_ref[...] = jnp.zeros_like(dg_ref)
            acc[...] = jnp.zeros_like(acc)

        xv = x_ref[...]
        g = g_ref[...]
        r = lax.rsqrt(jnp.mean(xv * xv, axis=1, keepdims=True) + NORM_EPS)
        xhat = xv * r
        err = xhat * g - t_ref[...]
        acc[...] += jnp.sum(err * err, axis=0, keepdims=True)
        dy = err * (1.0 / d)
        dyg = dy * g
        proj = jnp.mean(dyg * xhat, axis=1, keepdims=True)
        dx_ref[...] = r * (dyg - xhat * proj)
        dg_ref[...] += jnp.sum(dy * xhat, axis=0, keepdims=True)

        @pl.when(i == n_steps - 1)
        def _():
            total = jnp.sum(acc[...], axis=1, keepdims=True) * (0.5 / d)
            loss_ref[...] = jnp.broadcast_to(total, loss_ref.shape)

    return pl.pallas_call(
        body, name="loss_head", grid=(n_steps,),
        in_specs=[_spec((t, d), lambda i: (i, 0)), _spec((1, d), lambda i: (0, 0)),
                  _spec((t, d), lambda i: (i, 0))],
        out_specs=[_spec((t, d), lambda i: (i, 0)), _spec((1, d), lambda i: (0, 0)),
                   _spec((1, 128), lambda i: (0, 0))],
        out_shape=[jax.ShapeDtypeStruct((s, d), F32), jax.ShapeDtypeStruct((1, d), F32),
                   jax.ShapeDtypeStruct((1, 128), F32)],
        scratch_shapes=[pltpu.VMEM((1, d), F32)],
        compiler_params=_params(1),
    )(x, gain, target)


def _rope_tables():
    half = HEAD_DIM // 2
    inv_freq = ROPE_THETA ** (-jnp.arange(half, dtype=F32) / half)
    ang = jnp.arange(SEQ, dtype=jnp.int32).astype(F32)[:, None] * inv_freq[None, :]
    cos, sin = jnp.cos(ang), jnp.sin(ang)
    cos_t = jnp.concatenate([cos, cos, cos, cos], axis=1)
    sin_t = jnp.concatenate([-sin, sin, -sin, sin], axis=1)
    return cos_t, sin_t


def _swap_halves(x):
    lane = lax.broadcasted_iota(jnp.int32, x.shape, 1)
    first = (lane % HEAD_DIM) < (HEAD_DIM // 2)
    return jnp.where(first, pltpu.roll(x, 128 - HEAD_DIM // 2, 1), pltpu.roll(x, HEAD_DIM // 2, 1))


def _permuted_specs(t, width):
    specs = [_spec((t, width), lambda i: (i, 0))]
    for d in DILATIONS[1:]:
        specs.append(_spec((d, t // d, width), lambda i: (0, i, 0)))
    return specs


def _permuted_shapes(width, dtype):
    shapes = [jax.ShapeDtypeStruct((SEQ, width), dtype)]
    for d in DILATIONS[1:]:
        shapes.append(jax.ShapeDtypeStruct((d, SEQ // d, width), dtype))
    return shapes


def _attn_prep(name, proj, cos_t, sin_t):
    t = ROW_TILE
    w = ATTN_WIDTH

    def body(q_ref, k_ref, v_ref, cos_ref, sin_ref, *rest):
        outs, scr = rest[:9], rest[9]
        cosv, sinv = cos_ref[...], sin_ref[...]
        for a, (src, roped, scale) in enumerate(((q_ref, True, HEAD_DIM ** -0.5),
                                                 (k_ref, True, 1.0), (v_ref, False, 1.0))):
            o1, o4, o16 = outs[3 * a:3 * a + 3]
            for cb in range(w // 128):
                cols = slice(cb * 128, (cb + 1) * 128)
                val = src[:, cols]
                if roped:
                    val = (val * cosv + _swap_halves(val) * sinv) * scale
                scr[...] = val
                o1[:, cols] = val.astype(BF16)
                for o_ref, d in ((o4, 4), (o16, 16)):
                    for r in range(d):
                        o_ref[r, :, cols] = scr[pl.ds(r, t // d, stride=d), :].astype(BF16)

    out_specs = _permuted_specs(t, w) * 3
    out_shape = _permuted_shapes(w, BF16) * 3
    outs = pl.pallas_call(
        body, name=name, grid=(SEQ // t,),
        in_specs=[_spec((t, w), lambda i: (i, 0)), _spec((t, w), lambda i: (i, 1)),
                  _spec((t, w), lambda i: (i, 2)),
                  _spec((t, 128), lambda i: (i, 0)), _spec((t, 128), lambda i: (i, 0))],
        out_specs=out_specs, out_shape=out_shape,
        scratch_shapes=[pltpu.VMEM((t, 128), F32)],
        compiler_params=_params(1),
    )(proj, proj, proj, cos_t, sin_t)
    q, k, v = outs[0:3], outs[3:6], outs[6:9]
    flat = lambda arr: arr.reshape(SEQ, w)
    return [(flat(q[p]), flat(k[p]), flat(v[p])) for p in range(3)]


def _band_masks():
    row = lax.broadcasted_iota(jnp.int32, (2 * SPAN, 2 * SPAN), 0) % SPAN
    col = lax.broadcasted_iota(jnp.int32, (2 * SPAN, 2 * SPAN), 1)
    is_prev = col < SPAN
    band = (is_prev & (col >= row)) | (~is_prev & (col - SPAN <= row))
    head0 = lax.broadcasted_iota(jnp.int32, (SPAN, 128), 1) < HEAD_DIM
    return band, is_prev, head0


def _stack_heads(x, head0):
    zero = jnp.zeros_like(x)
    return jnp.concatenate([jnp.where(head0, x, zero), jnp.where(head0, zero, x)], axis=0)


def _attn_fwd(name, q, k, v, seg_blocks):
    n_blocks = SEQ // SPAN

    def body(q_ref, k_ref, v_ref, o_ref, lse_ref):
        band, is_prev, head0 = _band_masks()

        def step(b, carry):
            cur = pl.ds(pl.multiple_of(b * SPAN, SPAN), SPAN)
            prev = pl.ds(pl.multiple_of(jnp.maximum(b - 1, 0) * SPAN, SPAN), SPAN)
            qs = _stack_heads(q_ref[cur, :], head0)
            kcat = jnp.concatenate([k_ref[prev, :], k_ref[cur, :]], axis=0)
            vcat = jnp.concatenate([v_ref[prev, :], v_ref[cur, :]], axis=0)
            ok = band & (((b % seg_blocks) != 0) | ~is_prev)
            s = jnp.where(ok, _dot_nt(qs, kcat), MASK_VALUE)
            m = jnp.max(s, axis=1, keepdims=True)
            p = jnp.where(ok, jnp.exp(s - m), 0.0)
            l = jnp.sum(p, axis=1, keepdims=True)
            pv = _dot(p, vcat) / l
            lse = m + jnp.log(l)
            o_ref[cur, :] = jnp.where(head0, pv[:SPAN], pv[SPAN:])
            lse_ref[cur, :] = jnp.where(head0, lse[:SPAN], lse[SPAN:])
            return carry

        lax.fori_loop(0, n_blocks, step, 0, unroll=2)

    col = _spec((SEQ, 128), lambda j: (0, j))
    return pl.pallas_call(
        body, name=name, grid=(ATTN_WIDTH // 128,),
        in_specs=[col, col, col], out_specs=[col, col],
        out_shape=[jax.ShapeDtypeStruct((SEQ, ATTN_WIDTH), F32)] * 2,
        compiler_params=_params(1),
    )(q, k, v)


def _unpermute(dst, src_ref, d, cols):
    n = dst.shape[0] // d
    for r in range(d):
        dst[pl.ds(r, n, stride=d), :] = src_ref[r, :, cols]


def _attn_merge(name, outs, lses, gain):
    t = ROW_TILE
    w = ATTN_WIDTH

    def body(o1, o4, o16, l1, l4, l16, g_ref, an_ref, ant_ref, attn_ref, lse_ref, so4, so16, sl4, sl16):
        for cb in range(w // 128):
            cols = slice(cb * 128, (cb + 1) * 128)
            _unpermute(so4, o4, 4, cols)
            _unpermute(so16, o16, 16, cols)
            _unpermute(sl4, l4, 4, cols)
            _unpermute(sl16, l16, 16, cols)
            la, lb, lc = l1[:, cols], sl4[...], sl16[...]
            m = jnp.maximum(jnp.maximum(la, lb), lc)
            ea, eb, ec = jnp.exp(la - m), jnp.exp(lb - m), jnp.exp(lc - m)
            tot = ea + eb + ec
            attn_ref[:, cols] = (ea * o1[:, cols] + eb * so4[...] + ec * so16[...]) / tot
            lse_ref[:, cols] = m + jnp.log(tot)
        attn = attn_ref[...]
        r = lax.rsqrt(jnp.mean(attn * attn, axis=1, keepdims=True) + NORM_EPS)
        an = attn * r * g_ref[...]
        an_ref[...] = an.astype(BF16)
        ant_ref[...] = an.T.astype(BF16)

    views = lambda arrs: [arrs[0], arrs[1].reshape(4, SEQ // 4, w), arrs[2].reshape(16, SEQ // 16, w)]
    row = _spec((t, w), lambda i: (i, 0))
    return pl.pallas_call(
        body, name=name, grid=(SEQ // t,),
        in_specs=_permuted_specs(t, w) * 2 + [_spec((1, w), lambda i: (0, 0))],
        out_specs=[row, _spec((w, t), lambda i: (0, i)), row, row],
        out_shape=[jax.ShapeDtypeStruct((SEQ, w), BF16), jax.ShapeDtypeStruct((w, SEQ), BF16),
                   jax.ShapeDtypeStruct((SEQ, w), F32), jax.ShapeDtypeStruct((SEQ, w), F32)],
        scratch_shapes=[pltpu.VMEM((t, 128), F32)] * 4,
        compiler_params=_params(1),
    )(*views(outs), *views(lses), gain)


def _head_sum_matrix():
    i = np.arange(ATTN_WIDTH)
    return jnp.asarray((i[:, None] // HEAD_DIM) == (i[None, :] // HEAD_DIM), dtype=F32)


def _attn_bwd_prep(name, d_an, attn, lse, gain, head_sum):
    t = ROW_TILE
    w = ATTN_WIDTH

    def body(dan_ref, attn_ref, lse_ref, g_ref, hs_ref, *rest):
        (do1, do4, do16, dl1, dl4, dl16, ls4, ls16, dg_ref), (sdo, sdl, sls) = rest[:9], rest[9:]

        @pl.when(pl.program_id(0) == 0)
        def _():
            dg_ref[...] = jnp.zeros_like(dg_ref)

        attn = attn_ref[...]
        dan = dan_ref[...]
        r = lax.rsqrt(jnp.mean(attn * attn, axis=1, keepdims=True) + NORM_EPS)
        xhat = attn * r
        dg_ref[...] += jnp.sum(dan * xhat, axis=0, keepdims=True)
        dang = dan * g_ref[...]
        d_o = r * (dang - xhat * jnp.mean(dang * xhat, axis=1, keepdims=True))
        delta = jnp.dot(d_o * attn, hs_ref[...], preferred_element_type=F32,
                        precision=lax.Precision.HIGHEST)
        do1[...] = d_o.astype(BF16)
        dl1[...] = delta
        for cb in range(w // 128):
            cols = slice(cb * 128, (cb + 1) * 128)
            sdo[...] = d_o[:, cols]
            sdl[...] = delta[:, cols]
            sls[...] = lse_ref[:, cols]
            for d, o_do, o_dl, o_ls in ((4, do4, dl4, ls4), (16, do16, dl16, ls16)):
                for rr in range(d):
                    rows = pl.ds(rr, t // d, stride=d)
                    o_do[rr, :, cols] = sdo[rows, :].astype(BF16)
                    o_dl[rr, :, cols] = sdl[rows, :]
                    o_ls[rr, :, cols] = sls[rows, :]

    row = _spec((t, w), lambda i: (i, 0))
    perm = _permuted_specs(t, w)
    outs = pl.pallas_call(
        body, name=name, grid=(SEQ // t,),
        in_specs=[row, row, row, _spec((1, w), lambda i: (0, 0)), _spec((w, w), lambda i: (0, 0))],
        out_specs=perm + perm + perm[1:] + [_spec((1, w), lambda i: (0, 0))],
        out_shape=(_permuted_shapes(w, BF16) + _permuted_shapes(w, F32) + _permuted_shapes(w, F32)[1:]
                   + [jax.ShapeDtypeStruct((1, w), F32)]),
        scratch_shapes=[pltpu.VMEM((t, 128), F32)] * 3,
        compiler_params=_params(1),
    )(d_an, attn, lse, gain, head_sum)
    flat = lambda arr: arr.reshape(SEQ, w)
    d_out = [flat(a) for a in outs[0:3]]
    delta = [flat(a) for a in outs[3:6]]
    lses = [lse, flat(outs[6]), flat(outs[7])]
    return d_out, delta, lses, outs[8]


def _attn_bwd(name, q, k, v, d_out, delta, lse, seg_blocks):
    n_blocks = SEQ // SPAN

    def body(q_ref, k_ref, v_ref, do_ref, dl_ref, lse_ref, dq_ref, dk_ref, dv_ref):
        band, is_prev, head0 = _band_masks()
        dk_ref[...] = jnp.zeros_like(dk_ref)
        dv_ref[...] = jnp.zeros_like(dv_ref)

        def per_head(x):
            return jnp.concatenate([x[:, 0:1], x[:, HEAD_DIM:HEAD_DIM + 1]], axis=0)

        def step(b, carry):
            cur = pl.ds(pl.multiple_of(b * SPAN, SPAN), SPAN)
            prev = pl.ds(pl.multiple_of(jnp.maximum(b - 1, 0) * SPAN, SPAN), SPAN)
            qs = _stack_heads(q_ref[cur, :], head0)
            dos = _stack_heads(do_ref[cur, :], head0)
            kcat = jnp.concatenate([k_ref[prev, :], k_ref[cur, :]], axis=0)
            vcat = jnp.concatenate([v_ref[prev, :], v_ref[cur, :]], axis=0)
            ok = band & (((b % seg_blocks) != 0) | ~is_prev)
            p = jnp.where(ok, jnp.exp(_dot_nt(qs, kcat) - per_head(lse_ref[cur, :])), 0.0)
            ds = p * (_dot_nt(dos, vcat) - per_head(dl_ref[cur, :]))
            dq = _dot(ds, kcat)
            dq_ref[cur, :] = jnp.where(head0, dq[:SPAN], dq[SPAN:])
            dk = _dot(ds.T, qs)
            dv = _dot(p.T, dos)
            dk_ref[prev, :] += dk[:SPAN]
            dv_ref[prev, :] += dv[:SPAN]
            dk_ref[cur, :] += dk[SPAN:]
            dv_ref[cur, :] += dv[SPAN:]
            return carry

        lax.fori_loop(0, n_blocks, step, 0, unroll=2)

    col = _spec((SEQ, 128), lambda j: (0, j))
    return pl.pallas_call(
        body, name=name, grid=(ATTN_WIDTH // 128,),
        in_specs=[col] * 6, out_specs=[col] * 3,
        out_shape=[jax.ShapeDtypeStruct((SEQ, ATTN_WIDTH), F32)] * 3,
        compiler_params=_params(1),
    )(q, k, v, d_out, delta, lse)


def _attn_bwd_post(name, grads, cos_t, sin_t):
    t = ROW_TILE
    w = ATTN_WIDTH

    def body(*refs):
        ins, cos_ref, sin_ref, out_ref, s4, s16 = refs[:9], refs[9], refs[10], refs[11], refs[12], refs[13]
        cosv, sinv = cos_ref[...], sin_ref[...]
        for a in range(3):
            g1, g4, g16 = ins[a], ins[3 + a], ins[6 + a]
            for cb in range(w // 128):
                cols = slice(cb * 128, (cb + 1) * 128)
                _unpermute(s4, g4, 4, cols)
                _unpermute(s16, g16, 16, cols)
                val = g1[:, cols] + s4[...] + s16[...]
                if a < 2:
                    val = val * cosv + _swap_halves(val * sinv)
                if a == 0:
                    val = val * (HEAD_DIM ** -0.5)
                out_ref[:, a * w + cb * 128:a * w + (cb + 1) * 128] = val

    views = []
    for p, d in enumerate(DILATIONS):
        for a in range(3):
            views.append(grads[p][a] if d == 1 else grads[p][a].reshape(d, SEQ // d, w))
    perm = _permuted_specs(t, w)
    in_specs = [perm[0]] * 3 + [perm[1]] * 3 + [perm[2]] * 3
    return pl.pallas_call(
        body, name=name, grid=(SEQ // t,),
        in_specs=in_specs + [_spec((t, 128), lambda i: (i, 0))] * 2,
        out_specs=_spec((t, 3 * w), lambda i: (i, 0)),
        out_shape=jax.ShapeDtypeStruct((SEQ, 3 * w), F32),
        scratch_shapes=[pltpu.VMEM((t, 128), F32)] * 2,
        compiler_params=_params(1),
    )(*views, cos_t, sin_t)


N_LEVELS = 7


def _hgrn_consts():
    c = CHUNK
    i = np.arange(c)[:, None]
    s = np.arange(c)[None, :]
    blocks = [s <= i]
    for lv in range(N_LEVELS):
        bs = c >> lv
        h = bs // 2
        m = (i // bs) * bs + h - 1
        second = (i % bs) >= h
        blocks.append((second & (s > m) & (s <= i)) | (~second & (s > i) & (s <= m)))
    blocks.append(s > i)
    stack = np.concatenate(blocks, axis=0).astype(np.float32)
    return jnp.asarray(stack, dtype=BF16), jnp.asarray(stack.T, dtype=BF16)


def _exact_dot(m01, x):
    hi = x.astype(BF16)
    r1 = x - hi.astype(F32)
    mid = r1.astype(BF16)
    lo = (r1 - mid.astype(F32)).astype(BF16)
    n = x.shape[1]
    full = jnp.dot(m01, jnp.concatenate([hi, mid, lo], axis=1), preferred_element_type=F32)
    return (full[:, :n] + full[:, n:2 * n]) + full[:, 2 * n:]


def _hgrn_gates(qh, z, lb):
    sq = _sigmoid(qh)
    q = qh * sq * (HGRN_DIM ** -0.5)
    sig = _sigmoid(z)
    sigm = _sigmoid(-z)
    f = lb + (1.0 - lb) * sig
    k = (1.0 - lb) * sigm
    return q, k, f, sq, sig, sigm


def _level_masks(lv):
    row = lax.broadcasted_iota(jnp.int32, (CHUNK, CHUNK), 0)
    col = lax.broadcasted_iota(jnp.int32, (CHUNK, CHUNK), 1)
    shift = N_LEVELS - lv
    second = (row & (CHUNK >> (lv + 1))) != 0
    same = (row >> shift) == (col >> shift)
    return second, same


def _hgrn_fwd(name, proj, lb, gain, stack):
    t = ROW_TILE
    per = t // CHUNK
    n_rb = SEQ // t
    n_chunks = SEQ // CHUNK
    col0 = 3 * ATTN_WIDTH // 128

    def body(q_ref, f_ref, i_ref, g_ref, lb_ref, gain_ref, stack_ref,
             rec_ref, rect_ref, o_ref, st_out, a_out, st):
        @pl.when(pl.program_id(1) == 0)
        def _():
            st[...] = jnp.zeros_like(st)

        lbv = lb_ref[...]
        row = lax.broadcasted_iota(jnp.int32, (CHUNK, CHUNK), 0)
        col = lax.broadcasted_iota(jnp.int32, (CHUNK, CHUNK), 1)
        for c in range(per):
            rows = slice(c * CHUNK, (c + 1) * CHUNK)
            qh, z, v, gh = q_ref[rows, :], f_ref[rows, :], i_ref[rows, :], g_ref[rows, :]
            q, k, f, _, _, _ = _hgrn_gates(qh, z, lbv)
            dec = _exact_dot(stack_ref[...], jnp.log(f))
            g = dec[0:CHUNK]
            to_end = dec[(N_LEVELS + 1) * CHUNK:(N_LEVELS + 2) * CHUNK]
            a = jnp.where(row == col, jnp.sum(q * k, axis=1, keepdims=True), 0.0)
            for lv in range(N_LEVELS):
                e = jnp.exp(dec[(lv + 1) * CHUNK:(lv + 2) * CHUNK])
                second, same = _level_masks(lv)
                qt = jnp.where(second, q * e, 0.0)
                kt = jnp.where(second, 0.0, k * e)
                a = a + jnp.where(same, _dot_nt(qt, kt), 0.0)
            st_prev = st[...]
            st_out[c] = st_prev
            a_out[c] = a
            o = _dot(a, v) + _dot_nt(q * jnp.exp(g), st_prev)
            k_end = k * jnp.exp(to_end)
            st[...] = st_prev * jnp.exp(g[CHUNK - 1:CHUNK, :]) + _dot(v.T, k_end)
            o_ref[rows, :] = o
            r = lax.rsqrt(jnp.mean(o * o, axis=1, keepdims=True) + NORM_EPS)
            rec = o * r * gain_ref[...] * (gh * _sigmoid(gh))
            rec_ref[rows, :] = rec.astype(BF16)
            rect_ref[:, rows] = rec.T.astype(BF16)

    def col_spec(tt):
        return _spec((t, HGRN_DIM), lambda h, rb: (rb, col0 + HGRN_HEADS * tt + h))

    chunk_spec = _spec((None, per, CHUNK, CHUNK), lambda h, rb: (h, rb, 0, 0))
    return pl.pallas_call(
        body, name=name, grid=(HGRN_HEADS, n_rb),
        in_specs=[col_spec(0), col_spec(1), col_spec(2), col_spec(3),
                  _spec((None, 1, HGRN_DIM), lambda h, rb: (h, 0, 0)),
                  _spec((1, HGRN_DIM), lambda h, rb: (0, 0)),
                  _spec(stack.shape, lambda h, rb: (0, 0))],
        out_specs=[_spec((t, HGRN_DIM), lambda h, rb: (rb, h)),
                   _spec((HGRN_DIM, t), lambda h, rb: (h, rb)),
                   _spec((t, HGRN_DIM), lambda h, rb: (rb, h)),
                   chunk_spec, chunk_spec],
        out_shape=[jax.ShapeDtypeStruct((SEQ, HGRN_WIDTH), BF16),
                   jax.ShapeDtypeStruct((HGRN_WIDTH, SEQ), BF16),
                   jax.ShapeDtypeStruct((SEQ, HGRN_WIDTH), F32),
                   jax.ShapeDtypeStruct((HGRN_HEADS, n_chunks, CHUNK, CHUNK), F32),
                   jax.ShapeDtypeStruct((HGRN_HEADS, n_chunks, CHUNK, CHUNK), F32)],
        scratch_shapes=[pltpu.VMEM((CHUNK, CHUNK), F32)],
        compiler_params=_params(2),
    )(proj, proj, proj, proj, lb, gain, stack)


def _hgrn_bwd(name, proj, d_rec, o_pre, states, scores, lb, gain, stack, stack_t):
    t = ROW_TILE
    per = t // CHUNK
    n_rb = SEQ // t
    col0 = 3 * ATTN_WIDTH // 128

    def body(q_ref, f_ref, i_ref, g_ref, drec_ref, o_ref, st_ref, a_ref, lb_ref, gain_ref,
             stack_ref, stack_t_ref, dq_ref, df_ref, di_ref, dg_ref, dlb_ref, dgain_ref, dst):
        @pl.when(pl.program_id(1) == 0)
        def _():
            dst[...] = jnp.zeros_like(dst)
            dlb_ref[...] = jnp.zeros_like(dlb_ref)
            dgain_ref[...] = jnp.zeros_like(dgain_ref)

        lbv = lb_ref[...]
        gain_v = gain_ref[...]
        row = lax.broadcasted_iota(jnp.int32, (CHUNK, CHUNK), 0)
        col = lax.broadcasted_iota(jnp.int32, (CHUNK, CHUNK), 1)
        for c in reversed(range(per)):
            rows = slice(c * CHUNK, (c + 1) * CHUNK)
            qh, z, v, gh = q_ref[rows, :], f_ref[rows, :], i_ref[rows, :], g_ref[rows, :]
            q, k, f, sq, sig, sigm = _hgrn_gates(qh, z, lbv)
            dec = _exact_dot(stack_ref[...], jnp.log(f))
            g = dec[0:CHUNK]
            to_end = dec[(N_LEVELS + 1) * CHUNK:(N_LEVELS + 2) * CHUNK]
            e_g = jnp.exp(g)
            e_end = jnp.exp(to_end)
            e_last = jnp.exp(g[CHUNK - 1:CHUNK, :])
            q_in = q * e_g
            k_end = k * e_end
            st_prev = st_ref[c]
            a = a_ref[c]
            dst_new = dst[...]

            o = o_ref[rows, :]
            drec = drec_ref[rows, :]
            sg = _sigmoid(gh)
            r = lax.rsqrt(jnp.mean(o * o, axis=1, keepdims=True) + NORM_EPS)
            ohat = o * r
            d_gh = drec * (ohat * gain_v) * (sg * (1.0 + gh * (1.0 - sg)))
            d_on = drec * (gh * sg)
            dgain_ref[...] += jnp.sum(d_on * ohat, axis=0, keepdims=True)
            d_ohat = d_on * gain_v
            d_o = r * (d_ohat - ohat * jnp.mean(d_ohat * ohat, axis=1, keepdims=True))

            d_a = jnp.where(row >= col, _dot_nt(d_o, v), 0.0)
            d_at = jnp.where(col >= row, _dot_nt(v, d_o), 0.0)
            d_v = _dot(a.T, d_o) + _dot_nt(k_end, dst_new)
            d_q_in = _dot(d_o, st_prev)
            d_k_end = _dot(v, dst_new)
            d_q = d_q_in * e_g
            d_k = d_k_end * e_end
            diag = jnp.sum(d_o * v, axis=1, keepdims=True)
            d_q = d_q + diag * k
            d_k = d_k + diag * q
            d_dec = [q_in * d_q_in]
            for lv in range(N_LEVELS):
                e = jnp.exp(dec[(lv + 1) * CHUNK:(lv + 2) * CHUNK])
                second, same = _level_masks(lv)
                qt = jnp.where(second, q * e, 0.0)
                kt = jnp.where(second, 0.0, k * e)
                d_qt = _dot(jnp.where(same, d_a, 0.0), kt)
                d_kt = _dot(jnp.where(same, d_at, 0.0), qt)
                d_q = d_q + jnp.where(second, d_qt * e, 0.0)
                d_k = d_k + jnp.where(second, 0.0, d_kt * e)
                d_dec.append(jnp.where(second, qt * d_qt, kt * d_kt))
            d_dec.append(k_end * d_k_end)
            flux = jnp.sum(dst_new * st_prev, axis=0, keepdims=True) * e_last
            d_lf = _exact_dot(stack_t_ref[...], jnp.concatenate(d_dec, axis=0)) + flux
            dst[...] = dst_new * e_last + _dot(d_o.T, q_in)

            d_f = d_lf / f - d_k
            dlb_ref[...] += jnp.sum(d_f * sigm, axis=0, keepdims=True)
            dq_ref[rows, :] = d_q * (HGRN_DIM ** -0.5) * (sq * (1.0 + qh * (1.0 - sq)))
            df_ref[rows, :] = d_f * (1.0 - lbv) * sig * sigm
            di_ref[rows, :] = d_v
            dg_ref[rows, :] = d_gh

    last = n_rb - 1

    def col_spec(tt):
        return _spec((t, HGRN_DIM), lambda h, rb: (last - rb, col0 + HGRN_HEADS * tt + h))

    head_col = _spec((t, HGRN_DIM), lambda h, rb: (last - rb, h))
    chunk_spec = _spec((None, per, CHUNK, CHUNK), lambda h, rb: (h, last - rb, 0, 0))
    vec_spec = _spec((None, 1, HGRN_DIM), lambda h, rb: (h, 0, 0))
    outs = pl.pallas_call(
        body, name=name, grid=(HGRN_HEADS, n_rb),
        in_specs=[col_spec(0), col_spec(1), col_spec(2), col_spec(3), head_col, head_col,
                  chunk_spec, chunk_spec, vec_spec,
                  _spec((1, HGRN_DIM), lambda h, rb: (0, 0)),
                  _spec(stack.shape, lambda h, rb: (0, 0)), _spec(stack_t.shape, lambda h, rb: (0, 0))],
        out_specs=[head_col] * 4 + [vec_spec, vec_spec],
        out_shape=[jax.ShapeDtypeStruct((SEQ, HGRN_WIDTH), F32)] * 4
                  + [jax.ShapeDtypeStruct((HGRN_HEADS, 1, HGRN_DIM), F32)] * 2,
        scratch_shapes=[pltpu.VMEM((CHUNK, CHUNK), F32)],
        compiler_params=_params(2),
    )(proj, proj, proj, proj, d_rec, o_pre, states, scores, lb, gain, stack, stack_t)
    return outs


ANY_SPEC = pl.BlockSpec(memory_space=pl.ANY)


def _my_place():
    return lax.axis_index("x"), lax.axis_index("y"), lax.axis_index("c")


def _other_chips(x, y):
    return [(1 - x, y), (x, 1 - y), (1 - x, 1 - y)]


def _remote(src, dst, send_sem, recv_sem, device):
    return pltpu.make_async_remote_copy(src_ref=src, dst_ref=dst, send_sem=send_sem, recv_sem=recv_sem,
                                        device_id=device, device_id_type=MESH)


def _staged_copies(srcs, dsts, stage, sems):
    loads = [pltpu.make_async_copy(srcs[i], stage[i], sems.at[i]) for i in range(len(srcs))]
    for cp in loads:
        cp.start()
    stores = []
    for i, cp in enumerate(loads):
        cp.wait()
        stores.append(pltpu.make_async_copy(stage[i], dsts[i], sems.at[i]))
        stores[-1].start()
    return stores


def _gather_weights(name, shards):
    n = len(shards)

    def body(*refs):
        ins, outs = refs[:n], refs[n:2 * n]
        ici_send, ici_recv, d2d_send, d2d_recv, local_sems = refs[2 * n:2 * n + 5]
        stage = refs[2 * n + 5:]
        x, y, c = _my_place()
        me = 2 * x + y
        chips = _other_chips(x, y)

        def half(i, which):
            h = ins[i].shape[0] // 2
            return pl.ds(which * h, h)

        sends = []
        for i in range(n):
            for j, (px, py) in enumerate(chips):
                sends.append(_remote(ins[i].at[half(i, c), :], outs[i].at[me, half(i, c), :],
                                     ici_send.at[3 * i + j], ici_recv.at[3 * i + j], (px, py, c)))
        for cp in sends:
            cp.start()
        local = _staged_copies(ins, [outs[i].at[me] for i in range(n)], stage, local_sems)
        for i in range(n):
            for j, (px, py) in enumerate(chips):
                landed = outs[i].at[2 * px + py, half(i, c), :]
                _remote(landed, landed, ici_send.at[3 * i + j], ici_recv.at[3 * i + j], (px, py, c)).wait_recv()
                forward = _remote(landed, landed, d2d_send.at[3 * i + j], d2d_recv.at[3 * i + j], (x, y, 1 - c))
                forward.start()
                sends.append(forward)
        for i in range(n):
            for j, (px, py) in enumerate(chips):
                other = outs[i].at[2 * px + py, half(i, 1 - c), :]
                _remote(other, other, d2d_send.at[3 * i + j], d2d_recv.at[3 * i + j], (x, y, 1 - c)).wait_recv()
        for cp in sends:
            cp.wait_send()
        for cp in local:
            cp.wait()

    return pl.pallas_call(
        body, name=name, in_specs=[ANY_SPEC] * n, out_specs=[ANY_SPEC] * n,
        out_shape=[jax.ShapeDtypeStruct((N_CHIPS,) + s.shape, s.dtype) for s in shards],
        scratch_shapes=([pltpu.SemaphoreType.DMA((3 * n,))] * 4 + [pltpu.SemaphoreType.DMA((n,))]
                        + [pltpu.VMEM(s.shape, s.dtype) for s in shards]),
        compiler_params=pltpu.CompilerParams(vmem_limit_bytes=VMEM_LIMIT),
    )(*shards)


def _exchange_halves(name, grads):
    n = len(grads)

    def body(*refs):
        ins, outs = refs[:n], refs[n:2 * n]
        send_sems, recv_sems = refs[2 * n:]
        x, y, c = _my_place()
        copies = []
        for i in range(n):
            h = ins[i].shape[1] // 2
            copies.append(_remote(ins[i].at[:, pl.ds((1 - c) * h, h), :], outs[i],
                                  send_sems.at[i], recv_sems.at[i], (x, y, 1 - c)))
        for cp in copies:
            cp.start()
        for cp in copies:
            cp.wait()

    return pl.pallas_call(
        body, name=name, in_specs=[ANY_SPEC] * n, out_specs=[ANY_SPEC] * n,
        out_shape=[jax.ShapeDtypeStruct((g.shape[0], g.shape[1] // 2, g.shape[2]), g.dtype) for g in grads],
        scratch_shapes=[pltpu.SemaphoreType.DMA((n,)), pltpu.SemaphoreType.DMA((n,))],
    )(*grads)


def _add_own_half(name, g, received, core):
    n_sh, r, cc = g.shape
    h = r // 2
    th = min(h, 256)
    nb = h // th

    def body(core_ref, g_ref, r_ref, o_ref):
        del core_ref
        o_ref[...] = (g_ref[...] + r_ref[...]).astype(BF16)

    grid_spec = pltpu.PrefetchScalarGridSpec(
        num_scalar_prefetch=1, grid=(n_sh, nb),
        in_specs=[pl.BlockSpec((None, th, cc), lambda j, i, core_ref: (j, core_ref[0] * nb + i, 0)),
                  pl.BlockSpec((None, th, cc), lambda j, i, core_ref: (j, i, 0))],
        out_specs=pl.BlockSpec((None, th, cc), lambda j, i, core_ref: (j, i, 0)))
    return pl.pallas_call(
        body, name=name, grid_spec=grid_spec,
        out_shape=jax.ShapeDtypeStruct((n_sh, h, cc), BF16), compiler_params=_params(2),
    )(core, g, received)


def _exchange_chips(name, parts):
    n = len(parts)

    def body(*refs):
        ins, outs = refs[:n], refs[n:2 * n]
        send_sems, recv_sems, local_sems = refs[2 * n:2 * n + 3]
        stage = refs[2 * n + 3:]
        x, y, c = _my_place()
        me = 2 * x + y
        chips = _other_chips(x, y)
        sends = []
        for i in range(n):
            for j, (px, py) in enumerate(chips):
                sends.append(_remote(ins[i].at[2 * px + py], outs[i].at[me], send_sems.at[3 * i + j],
                                     recv_sems.at[3 * i + j], (px, py, c)))
        for cp in sends:
            cp.start()
        local = _staged_copies([ins[i].at[me] for i in range(n)], [outs[i].at[me] for i in range(n)],
                               stage, local_sems)
        for i in range(n):
            for j, (px, py) in enumerate(chips):
                _remote(ins[i].at[me], outs[i].at[2 * px + py], send_sems.at[3 * i + j],
                        recv_sems.at[3 * i + j], (px, py, c)).wait_recv()
        for cp in sends:
            cp.wait_send()
        for cp in local:
            cp.wait()

    return pl.pallas_call(
        body, name=name, in_specs=[ANY_SPEC] * n, out_specs=[ANY_SPEC] * n,
        out_shape=[jax.ShapeDtypeStruct(p.shape, p.dtype) for p in parts],
        scratch_shapes=([pltpu.SemaphoreType.DMA((3 * n,)), pltpu.SemaphoreType.DMA((3 * n,)),
                         pltpu.SemaphoreType.DMA((n,))]
                        + [pltpu.VMEM(p.shape[1:], p.dtype) for p in parts]),
        compiler_params=pltpu.CompilerParams(vmem_limit_bytes=VMEM_LIMIT),
    )(*parts)


HBM_SPEC = pl.BlockSpec(memory_space=pltpu.HBM)
SEM_SPEC = pl.BlockSpec(memory_space=pltpu.SEMAPHORE)
SPLIT_PARAMS = pltpu.CompilerParams(has_side_effects=pltpu.SideEffectType.DATAFLOW_SIDE_EFFECTING)


def _chip_copies(ins, lands, send_sems, recv_sems, sliced):
    x, y, c = _my_place()
    me = 2 * x + y
    pairs = []
    for i in range(len(ins)):
        for j, (px, py) in enumerate(_other_chips(x, y)):
            theirs = 2 * px + py
            src = ins[i].at[theirs] if sliced else ins[i]
            sems = (send_sems.at[3 * i + j], recv_sems.at[3 * i + j], (px, py, c))
            pairs.append((_remote(src, lands[i].at[me], *sems), _remote(src, lands[i].at[theirs], *sems)))
    return pairs


def _exchange_start(name, srcs, lands, sliced, after):
    n = len(srcs)

    def body(*refs):
        ins, land_refs = refs[:n], refs[n:2 * n]
        send_sems, recv_sems = refs[2 * n + 1:2 * n + 3]
        token = refs[-1]
        for send, _ in _chip_copies(ins, land_refs, send_sems, recv_sems, sliced):
            send.start()
        token[...] = jnp.zeros_like(token)

    arrays = list(srcs) + list(lands)
    outs = pl.pallas_call(
        body, name=name,
        in_specs=[HBM_SPEC] * (2 * n) + [ANY_SPEC],
        out_shape=([pltpu.SemaphoreType.DMA((3 * n,))] * 2 + [pltpu.HBM(a.shape, a.dtype) for a in arrays]
                   + [jax.ShapeDtypeStruct((8, 128), F32)]),
        out_specs=[SEM_SPEC] * 2 + [HBM_SPEC] * (2 * n) + [pl.BlockSpec(memory_space=pltpu.VMEM)],
        input_output_aliases={i: 2 + i for i in range(2 * n)},
        compiler_params=SPLIT_PARAMS,
    )(*[pltpu.with_memory_space_constraint(a, pltpu.HBM) for a in arrays], after)
    return outs[:2], outs[2:2 + 2 * n], outs[-1]


def _exchange_wait(name, sems, passed, sliced, after):
    n = len(passed) // 2

    def body(*refs):
        ins, land_refs = refs[:n], refs[n:2 * n]
        send_sems, recv_sems = refs[2 * n:2 * n + 2]
        for send, arrive in _chip_copies(ins, land_refs, send_sems, recv_sems, sliced):
            send.wait_send()
            arrive.wait_recv()

    outs = pl.pallas_call(
        body, name=name,
        in_specs=[HBM_SPEC] * (2 * n) + [SEM_SPEC] * 2 + [ANY_SPEC],
        out_shape=[pltpu.HBM(a.shape, a.dtype) for a in passed],
        out_specs=[HBM_SPEC] * (2 * n),
        input_output_aliases={i: i for i in range(2 * n)},
        compiler_params=SPLIT_PARAMS,
    )(*passed, *sems, after)
    return outs[n:]


def _own_slot(own, me):
    land = lax.empty((N_CHIPS,) + own.shape, own.dtype)
    return lax.dynamic_update_index_in_dim(land, own, me, 0)


def _sum_chips(name, parts):
    n_sh, h, cc = parts.shape
    th = min(h, 256)

    def body(p_ref, o_ref):
        p = [p_ref[j].astype(F32) for j in range(n_sh)]
        o_ref[...] = ((p[0] + p[1]) + p[2]) + p[3]

    return pl.pallas_call(
        body, name=name, grid=(h // th,),
        in_specs=[_spec((n_sh, th, cc), lambda i: (0, i, 0))],
        out_specs=_spec((th, cc), lambda i: (i, 0)),
        out_shape=jax.ShapeDtypeStruct((h, cc), F32), compiler_params=_params(1),
    )(parts)


def _share_halves(halves):
    flat = [t for per_weight in halves for t in per_weight]
    n = len(flat)
    n_w = len(halves)

    def body(*refs):
        ins, outs = refs[:n], refs[n:n + n_w]
        send_sems, recv_sems, local_sems = refs[n + n_w:n + n_w + 3]
        stage = refs[n + n_w + 3:]
        x, y, c = _my_place()
        sends, own = [], []
        for i in range(n):
            w, l = divmod(i, DEPTH)
            h = ins[i].shape[0]
            own.append(outs[w].at[l, pl.ds(c * h, h), :])
            sends.append(_remote(ins[i], own[i], send_sems.at[i], recv_sems.at[i], (x, y, 1 - c)))
        for cp in sends:
            cp.start()
        local = _staged_copies(ins, own, stage, local_sems)
        for i in range(n):
            w, l = divmod(i, DEPTH)
            h = ins[i].shape[0]
            _remote(ins[i], outs[w].at[l, pl.ds((1 - c) * h, h), :], send_sems.at[i], recv_sems.at[i],
                    (x, y, 1 - c)).wait_recv()
        for cp in sends:
            cp.wait_send()
        for cp in local:
            cp.wait()

    return pl.pallas_call(
        body, name="share_halves", in_specs=[ANY_SPEC] * n, out_specs=[ANY_SPEC] * n_w,
        out_shape=[jax.ShapeDtypeStruct((DEPTH, 2 * per_weight[0].shape[0], per_weight[0].shape[1]), F32)
                   for per_weight in halves],
        scratch_shapes=([pltpu.SemaphoreType.DMA((n,))] * 3 + [pltpu.VMEM(t.shape, t.dtype) for t in flat]),
        compiler_params=pltpu.CompilerParams(vmem_limit_bytes=VMEM_LIMIT),
    )(*flat)


def _all_reduce_small(pack):
    def body(p_ref, o_ref, recv, send_sems, recv_sems):
        x, y, c = _my_place()
        me = 4 * x + 2 * y + c
        recv[me] = p_ref[...]
        peers = []
        for k in range(1, N_DEV):
            px, py, pc = (x + (k >> 2)) % 2, (y + ((k >> 1) & 1)) % 2, (c + (k & 1)) % 2
            peers.append((px, py, pc))
        sends = [_remote(p_ref, recv.at[me], send_sems.at[k], recv_sems.at[k], peer)
                 for k, peer in enumerate(peers)]
        for cp in sends:
            cp.start()
        for k, (px, py, pc) in enumerate(peers):
            _remote(p_ref, recv.at[4 * px + 2 * py + pc], send_sems.at[k], recv_sems.at[k],
                    (px, py, pc)).wait_recv()
        for cp in sends:
            cp.wait_send()
        total = recv[0]
        for d in range(1, N_DEV):
            total = total + recv[d]
        o_ref[...] = total

    vmem = pl.BlockSpec(memory_space=pltpu.VMEM)
    return pl.pallas_call(
        body, name="all_reduce_small", in_specs=[vmem], out_specs=vmem,
        out_shape=jax.ShapeDtypeStruct(pack.shape, F32),
        scratch_shapes=[pltpu.VMEM((N_DEV,) + pack.shape, F32),
                        pltpu.SemaphoreType.DMA((N_DEV - 1,)), pltpu.SemaphoreType.DMA((N_DEV - 1,))],
    )(pack)


def _adamw(name, w, g, m, v):
    r, cc = w.shape
    th = min(r, 256)

    def body(w_ref, g_ref, m_ref, v_ref, d_ref, m_out, v_out):
        gv = g_ref[...]
        m2 = ADAM_B1 * m_ref[...] + (1.0 - ADAM_B1) * gv
        v2 = ADAM_B2 * v_ref[...] + (1.0 - ADAM_B2) * (gv * gv)
        m_hat = m2 / (1.0 - ADAM_B1 ** ADAM_STEP)
        v_hat = v2 / (1.0 - ADAM_B2 ** ADAM_STEP)
        d_ref[...] = -ADAM_LR * (m_hat / (jnp.sqrt(v_hat) + ADAM_EPS) + ADAM_WD * w_ref[...])
        m_out[...] = m2
        v_out[...] = v2

    tile = _spec((th, cc), lambda i: (i, 0))
    return pl.pallas_call(
        body, name=name, grid=(r // th,), in_specs=[tile] * 4, out_specs=[tile] * 3,
        out_shape=[jax.ShapeDtypeStruct((r, cc), F32)] * 3, compiler_params=_params(1),
    )(w, g, m, v)


def _lower_bounds(lb_logits):
    p = jax.nn.softmax(lb_logits.astype(F32), axis=0)
    return jnp.cumsum(p, axis=0) - p[0]


def _row_tile_specs(tm, width):
    return _spec((tm, width), lambda i, j, k: (i, 0))


def _layer_forward(l, x_in, small, weights, consts, after=None):
    win, wo, wu, wd = weights
    cos_t, sin_t, stack, _, _ = consts
    tm = MM_TILE
    n_row = SEQ // tm
    saved = {"x_in": x_in}

    h, h_t = _rms_fwd(f"norm_mix{l}", x_in, small["norm_mix"][l][None, :], after=after)
    proj = _matmul(f"proj{l}", h, win,
                   _spec((tm, D_MODEL), lambda i, j, k: (i, 0)),
                   _spec((None, D_MODEL, SHARD_IN), lambda i, j, k: (j, 0, 0)),
                   (SEQ, IN_W), F32, _spec((tm, SHARD_IN), lambda i, j, k: (i, j)),
                   (n_row, N_CHIPS, 1), (tm, SHARD_IN))
    saved.update(h_t=h_t, proj=proj)

    qkv = _attn_prep(f"attn_prep{l}", proj, cos_t, sin_t)
    outs, lses = [], []
    for p, d in enumerate(DILATIONS):
        o, lse = _attn_fwd(f"attn_fwd{l}_{d}", *qkv[p], SEQ // d // SPAN)
        outs.append(o)
        lses.append(lse)
    an, an_t, attn, lse = _attn_merge(f"attn_merge{l}", outs, lses, small["attn_out_gain"][l][None, :])
    saved.update(qkv=qkv, an_t=an_t, attn=attn, lse=lse)

    lb3 = small["lower"][l].reshape(HGRN_HEADS, 1, HGRN_DIM)
    rec, rec_t, o_pre, states, scores = _hgrn_fwd(f"hgrn_fwd{l}", proj, lb3,
                                                  small["hgrn_out_gain"][l][None, :], stack)
    saved.update(rec_t=rec_t, o_pre=o_pre, states=states, scores=scores, lb3=lb3)

    def out_proj(name, a, part, resid):
        return _matmul(name, a, wo,
                       _spec((tm, SHARD_OUT), lambda i, j, k: (i, k)),
                       _spec((None, SHARD_OUT, D_MODEL), lambda i, j, k: (k + 2 * part, 0, 0)),
                       (SEQ, D_MODEL), F32, _spec((tm, D_MODEL), lambda i, j, k: (i, 0)),
                       (n_row, 1, 2), (tm, D_MODEL),
                       extra=resid, extra_spec=_spec((tm, D_MODEL), lambda i, j, k: (i, 0)), epilogue="add")

    x_mid = out_proj(f"out_rec{l}", rec, 1, out_proj(f"out_attn{l}", an, 0, x_in))
    saved["x_mid"] = x_mid

    h2, h2_t = _rms_fwd(f"norm_mlp{l}", x_mid, small["norm_mlp"][l][None, :])
    a, relu_u, a_t = _matmul(
        f"up{l}", h2, wu,
        _spec((tm, D_MODEL), lambda i, j, k: (i, 0)),
        _spec((None, D_MODEL, SHARD_MLP), lambda i, j, k: (j, 0, 0)),
        (SEQ, MLP_HIDDEN), BF16, _spec((tm, SHARD_MLP), lambda i, j, k: (i, j)),
        (n_row, N_CHIPS, 1), (tm, SHARD_MLP), epilogue="relu2",
        relu_outs=[((SEQ, MLP_HIDDEN), _spec((tm, SHARD_MLP), lambda i, j, k: (i, j))),
                   ((MLP_HIDDEN, SEQ), _spec((SHARD_MLP, tm), lambda i, j, k: (j, i)))])
    x_out = _matmul(f"down{l}", a, wd,
                    _spec((tm, SHARD_MLP), lambda i, j, k: (i, k)),
                    _spec((None, SHARD_MLP, D_MODEL), lambda i, j, k: (k, 0, 0)),
                    (SEQ, D_MODEL), F32, _spec((tm, D_MODEL), lambda i, j, k: (i, 0)),
                    (n_row, 1, N_CHIPS), (tm, D_MODEL),
                    extra=x_mid, extra_spec=_spec((tm, D_MODEL), lambda i, j, k: (i, 0)), epilogue="add")
    saved.update(h2_t=h2_t, relu_u=relu_u, a_t=a_t)
    return x_out, saved


def _layer_backward(l, dx, saved, small, weights, consts, after=None):
    win, wo, wu, wd = weights
    cos_t, sin_t, stack, stack_t, head_sum = consts
    tm = MM_TILE
    n_row = SEQ // tm
    n_k = SEQ // tm

    du = _matmul(f"d_u{l}", dx, wd,
                 _spec((tm, D_MODEL), lambda i, j, k: (i, 0)),
                 _spec((None, SHARD_MLP, D_MODEL), lambda i, j, k: (j, 0, 0)),
                 (SEQ, MLP_HIDDEN), BF16, _spec((tm, SHARD_MLP), lambda i, j, k: (i, j)),
                 (n_row, N_CHIPS, 1), (tm, SHARD_MLP), nt=True,
                 extra=saved["relu_u"], extra_spec=_spec((tm, SHARD_MLP), lambda i, j, k: (i, j)),
                 epilogue="relu2_grad", after=after)
    d_wd = _matmul(f"d_wdown{l}", saved["a_t"], dx,
                   _spec((SHARD_MLP, tm), lambda i, j, k: (i, k)),
                   _spec((tm, D_MODEL), lambda i, j, k: (k, 0)),
                   (N_CHIPS, SHARD_MLP, D_MODEL), F32, _spec((None, SHARD_MLP, D_MODEL), lambda i, j, k: (i, 0, 0)),
                   (N_CHIPS, 1, n_k), (SHARD_MLP, D_MODEL))
    dh2 = _matmul(f"d_h2_{l}", du, wu,
                  _spec((tm, SHARD_MLP), lambda i, j, k: (i, k)),
                  _spec((None, D_MODEL, SHARD_MLP), lambda i, j, k: (k, 0, 0)),
                  (SEQ, D_MODEL), F32, _spec((tm, D_MODEL), lambda i, j, k: (i, 0)),
                  (n_row, 1, N_CHIPS), (tm, D_MODEL), nt=True)
    d_wu = _matmul(f"d_wup{l}", saved["h2_t"], du,
                   _spec((D_MODEL, tm), lambda i, j, k: (0, k)),
                   _spec((tm, SHARD_MLP), lambda i, j, k: (k, j)),
                   (N_CHIPS, D_MODEL, SHARD_MLP), F32, _spec((None, D_MODEL, SHARD_MLP), lambda i, j, k: (j, 0, 0)),
                   (1, N_CHIPS, n_k), (D_MODEL, SHARD_MLP))
    dxm, dg_mlp = _rms_bwd(f"norm_mlp_bwd{l}", dh2, saved["x_mid"], small["norm_mlp"][l][None, :], dx)

    def d_mixed(name, part):
        return _matmul(name, dxm, wo,
                       _spec((tm, D_MODEL), lambda i, j, k: (i, 0)),
                       _spec((None, SHARD_OUT, D_MODEL), lambda i, j, k: (j + 2 * part, 0, 0)),
                       (SEQ, ATTN_WIDTH), F32, _spec((tm, SHARD_OUT), lambda i, j, k: (i, j)),
                       (n_row, 2, 1), (tm, SHARD_OUT), nt=True)

    def d_wout(name, a_t):
        return _matmul(name, a_t, dxm,
                       _spec((SHARD_OUT, tm), lambda i, j, k: (i, k)),
                       _spec((tm, D_MODEL), lambda i, j, k: (k, 0)),
                       (2, SHARD_OUT, D_MODEL), F32, _spec((None, SHARD_OUT, D_MODEL), lambda i, j, k: (i, 0, 0)),
                       (2, 1, n_k), (SHARD_OUT, D_MODEL))

    d_an = d_mixed(f"d_attn_n{l}", 0)
    d_rec = d_mixed(f"d_rec{l}", 1)
    d_wo = jnp.concatenate([d_wout(f"d_wout_attn{l}", saved["an_t"]),
                            d_wout(f"d_wout_rec{l}", saved["rec_t"])], axis=0)

    d_out, delta, lses, dg_attn = _attn_bwd_prep(f"attn_bwd_prep{l}", d_an, saved["attn"], saved["lse"],
                                                 small["attn_out_gain"][l][None, :], head_sum)
    grads = []
    for p, d in enumerate(DILATIONS):
        grads.append(_attn_bwd(f"attn_bwd{l}_{d}", *saved["qkv"][p], d_out[p], delta[p], lses[p],
                               SEQ // d // SPAN))
    dp_attn = _attn_bwd_post(f"attn_bwd_post{l}", grads, cos_t, sin_t)

    dq_h, df_h, di_h, dg_h, d_lower, dg_hgrn = _hgrn_bwd(
        f"hgrn_bwd{l}", saved["proj"], d_rec, saved["o_pre"], saved["states"], saved["scores"],
        saved["lb3"], small["hgrn_out_gain"][l][None, :], stack, stack_t)
    dproj = jnp.concatenate([dp_attn, dq_h, df_h, di_h, dg_h], axis=1)

    dh = _matmul(f"d_h{l}", dproj, win,
                 _spec((tm, SHARD_IN), lambda i, j, k: (i, k)),
                 _spec((None, D_MODEL, SHARD_IN), lambda i, j, k: (k, 0, 0)),
                 (SEQ, D_MODEL), F32, _spec((tm, D_MODEL), lambda i, j, k: (i, 0)),
                 (n_row, 1, N_CHIPS), (tm, D_MODEL), nt=True)
    d_win = _matmul(f"d_win{l}", saved["h_t"], dproj,
                    _spec((D_MODEL, tm), lambda i, j, k: (0, k)),
                    _spec((tm, SHARD_IN), lambda i, j, k: (k, j)),
                    (N_CHIPS, D_MODEL, SHARD_IN), F32, _spec((None, D_MODEL, SHARD_IN), lambda i, j, k: (j, 0, 0)),
                    (1, N_CHIPS, n_k), (D_MODEL, SHARD_IN))
    dx_in, dg_mix = _rms_bwd(f"norm_mix_bwd{l}", dh, saved["x_in"], small["norm_mix"][l][None, :], dxm)

    small_grads = {"norm_mix": dg_mix[0], "attn_out_gain": dg_attn[0],
                   "lower": d_lower.reshape(HGRN_WIDTH),
                   "hgrn_out_gain": jnp.sum(dg_hgrn, axis=0).reshape(HGRN_DIM), "norm_mlp": dg_mlp[0]}
    return dx_in, (d_win, d_wo, d_wu, d_wd), small_grads


def _local_step(xs, target, small, get_weights, on_grads):
    consts = _rope_tables() + _hgrn_consts() + (_head_sum_matrix(),)
    stream = xs
    saved, weights = [], []
    for l in range(DEPTH):
        w, after = get_weights(l, stream)
        weights.append(w)
        stream, s = _layer_forward(l, stream, small, w, consts, after=after)
        saved.append(s)
    dx, dg_final, loss = _loss_head(stream, small["norm_final"][None, :], target)
    small_grads = [None] * DEPTH
    after = None
    for l in reversed(range(DEPTH)):
        dx, big, small_grads[l] = _layer_backward(l, dx, saved[l], small, weights[l], consts, after=after)
        after = on_grads(l, big)
    return loss, dx, dg_final[0], small_grads


def _pack_small(norm_mix, attn_out_gain, lb, hgrn_out_gain, norm_mlp, norm_final, last_row):
    rows = [norm_mix, attn_out_gain.reshape(1, D_MODEL), lb.reshape(1, D_MODEL),
            jnp.pad(hgrn_out_gain.reshape(1, DEPTH * HGRN_DIM), ((0, 0), (0, D_MODEL - DEPTH * HGRN_DIM))),
            norm_mlp, norm_final.reshape(1, D_MODEL), last_row.reshape(1, D_MODEL)]
    pack = jnp.concatenate(rows, axis=0)
    return jnp.pad(pack, ((0, PACK_ROWS - pack.shape[0]), (0, 0)))


def _unpack_small(pack):
    return (pack[0:2], pack[2].reshape(DEPTH, ATTN_WIDTH), pack[3].reshape(DEPTH, HGRN_WIDTH),
            pack[4, :DEPTH * HGRN_DIM].reshape(DEPTH, HGRN_DIM), pack[5:7], pack[7], pack[8])


def kernel(x, norm_mix, w_in, attn_out_gain, hgrn_lb_logits, hgrn_out_gain, w_out, norm_mlp, w_up, w_down, norm_final, loss_target, m_norm_mix, m_w_in, m_attn_out_gain, m_hgrn_lb_logits, m_hgrn_out_gain, m_w_out, m_norm_mlp, m_w_up, m_w_down, m_norm_final, v_norm_mix, v_w_in, v_attn_out_gain, v_hgrn_lb_logits, v_hgrn_out_gain, v_w_out, v_norm_mlp, v_w_up, v_w_down, v_norm_final):
    core = lax.axis_index("c").astype(jnp.int32).reshape(1)
    lower, lower_vjp = jax.vjp(_lower_bounds, hgrn_lb_logits)
    small = {"norm_mix": norm_mix, "attn_out_gain": attn_out_gain, "lower": lower,
             "hgrn_out_gain": hgrn_out_gain, "norm_mlp": norm_mlp, "norm_final": norm_final}
    big_w = (w_in, w_out, w_up, w_down)

    me = 2 * lax.axis_index("x") + lax.axis_index("y")
    shards = [[w[l].astype(BF16) for w in big_w] for l in range(DEPTH)]
    in_flight = {}

    def get_weights(l, stream):
        if l == 0:
            weights = _gather_weights("gather_weights0", shards[0])
            lands = [_own_slot(s, me) for s in shards[1]]
            sems, passed, token = _exchange_start("gather_start1", shards[1], lands, False, weights[0])
            in_flight["weights"] = (sems, passed)
            return weights, token
        sems, passed = in_flight.pop("weights")
        return _exchange_wait("gather_wait1", sems, passed, False, stream), None

    reduced = [None] * DEPTH

    def halves_of(l, grads):
        received = _exchange_halves(f"exchange_halves{l}", grads)
        return [_add_own_half(f"add_halves{l}_{i}", g, r, core) for i, (g, r) in enumerate(zip(grads, received))]

    def on_grads(l, grads):
        halves = halves_of(l, grads)
        if l == 1:
            lands = [_own_slot(lax.dynamic_index_in_dim(h, me, 0, keepdims=False), me) for h in halves]
            sems, passed, token = _exchange_start("exchange_start1", halves, lands, True, halves[0])
            in_flight["grads"] = (sems, passed)
            return token
        sems, passed = in_flight.pop("grads")
        landed = _exchange_wait("exchange_wait1", sems, passed, True, grads[0])
        reduced[1] = [_sum_chips(f"sum_chips1_{i}", p) for i, p in enumerate(landed)]
        from_chips = _exchange_chips("exchange_chips0", halves)
        reduced[0] = [_sum_chips(f"sum_chips0_{i}", p) for i, p in enumerate(from_chips)]
        return None

    loss, dx, dg_final, sg = _local_step(x[0], loss_target[0], small, get_weights, on_grads)

    stack2 = lambda key: jnp.stack([sg[l][key] for l in range(DEPTH)])
    pack = _pack_small(stack2("norm_mix"), stack2("attn_out_gain"), stack2("lower"), stack2("hgrn_out_gain"),
                       stack2("norm_mlp"), dg_final, jnp.broadcast_to(loss[0, 0], (D_MODEL,)))
    g_mix, g_attn, g_lower, g_hgrn, g_mlp, g_final, loss_row = _unpack_small(_all_reduce_small(pack))
    (g_logits,) = lower_vjp(g_lower)

    zeros_row = jnp.zeros((D_MODEL,), F32)
    small_w = (norm_mix, attn_out_gain, hgrn_lb_logits, hgrn_out_gain, norm_mlp, norm_final)
    small_m = (m_norm_mix, m_attn_out_gain, m_hgrn_lb_logits, m_hgrn_out_gain, m_norm_mlp, m_norm_final)
    small_v = (v_norm_mix, v_attn_out_gain, v_hgrn_lb_logits, v_hgrn_out_gain, v_norm_mlp, v_norm_final)
    small_g = (g_mix, g_attn, g_logits, g_hgrn, g_mlp, g_final)
    packs = [_pack_small(*t, zeros_row) for t in (small_w, small_g, small_m, small_v)]
    small_delta, small_new_m, small_new_v = [_unpack_small(p)[:6] for p in _adamw("adamw_small", *packs)]

    big_g = _share_halves([[reduced[l][w] for l in range(DEPTH)] for w in range(4)])

    big_m = (m_w_in, m_w_out, m_w_up, m_w_down)
    big_v = (v_w_in, v_w_out, v_w_up, v_w_down)
    big_delta, big_new_m, big_new_v = [], [], []
    for i, name in enumerate(("w_in", "w_out", "w_up", "w_down")):
        shape = big_w[i].shape
        flat = lambda arr: arr.reshape(shape[0] * shape[1], shape[2])
        d, m2, v2 = _adamw(f"adamw_{name}", flat(big_w[i]), flat(big_g[i]), flat(big_m[i]), flat(big_v[i]))
        big_delta.append(d.reshape(shape))
        big_new_m.append(m2.reshape(shape))
        big_new_v.append(v2.reshape(shape))

    def ordered(small6, big4):
        mix, attn, lbl, hg, mlp, fin = small6
        return (mix, big4[0], attn, lbl, hg, big4[1], mlp, big4[2], big4[3], fin)

    return ((loss_row[0], dx[None]) + ordered(small_g, big_g) + ordered(small_delta, big_delta)
            + ordered(small_new_m, big_new_m) + ordered(small_new_v, big_new_v))
```

```python
import functools
import math

import numpy as np
import jax
import jax.numpy as jnp
from jax import lax
from jax.experimental import pallas as pl
from jax.experimental.pallas import tpu as pltpu

F32 = jnp.float32
BF16 = jnp.bfloat16
MESH = pl.DeviceIdType.MESH

SEQ = 4096
D_MODEL = 1024
DEPTH = 2
ATTN_WIDTH = 512
HEAD_DIM = 64
HGRN_HEADS = 4
HGRN_DIM = 128
HGRN_WIDTH = 512
IN_W = 3584
MLP_HIDDEN = 4096
N_CHIPS = 4
N_DEV = 8
SHARD_IN = IN_W // N_CHIPS
SHARD_OUT = D_MODEL // N_CHIPS
SHARD_MLP = MLP_HIDDEN // N_CHIPS
DILATIONS = (1, 4, 16)
SPAN = 128
ROPE_THETA = 10000.0
NORM_EPS = 1e-6
MASK_VALUE = -1e30
CHUNK = 128
ROW_TILE = 512
MM_TILE = 1024
VMEM_LIMIT = 52 * 1024 * 1024

ADAM_LR = 0.001
ADAM_B1 = 0.9
ADAM_B2 = 0.999
ADAM_EPS = 1e-08
ADAM_WD = 0.01
ADAM_STEP = 10

PACK_ROWS = 16


def _params(n_axes):
    return pltpu.CompilerParams(dimension_semantics=("arbitrary",) * n_axes,
                                vmem_limit_bytes=VMEM_LIMIT)


def _dot(a, b):
    return jnp.dot(a.astype(BF16), b.astype(BF16), preferred_element_type=F32)


def _dot_nt(a, b):
    return lax.dot_general(a.astype(BF16), b.astype(BF16), (((1,), (1,)), ((), ())),
                           preferred_element_type=F32)


def _sigmoid(x):
    return 1.0 / (1.0 + jnp.exp(-x))


def _matmul(name, a, b, a_spec, b_spec, out_shape, out_dtype, out_spec, grid, acc_shape,
            nt=False, extra=None, extra_spec=None, epilogue="none", after=None, relu_outs=None):
    nk = grid[2]
    n_out = 1 if relu_outs is None else 3

    def body(*refs):
        a_ref, b_ref = refs[:2]
        e_ref = None if extra is None else refs[2]
        o_ref = refs[-1 - n_out]
        acc = refs[-1]
        kk = pl.program_id(2)

        @pl.when(kk == 0)
        def _():
            acc[...] = jnp.zeros_like(acc)

        if nt:
            acc[...] += _dot_nt(a_ref[...], b_ref[...])
        else:
            acc[...] += _dot(a_ref[...], b_ref[...])

        @pl.when(kk == nk - 1)
        def _():
            r = acc[...]
            if epilogue == "add":
                r = r + e_ref[...]
            elif epilogue == "relu2_grad":
                r = r * (2.0 * e_ref[...].astype(F32))
            elif epilogue == "relu2":
                s = jnp.maximum(r, 0.0)
                r = s * s
                refs[-3][...] = s.astype(out_dtype)
                refs[-2][...] = r.T.astype(out_dtype)
            o_ref[...] = r.astype(o_ref.dtype)

    in_specs = [a_spec, b_spec] + ([] if extra is None else [extra_spec])
    args = (a, b) + (() if extra is None else (extra,))
    if after is not None:
        in_specs.append(pl.BlockSpec(memory_space=pl.ANY))
        args += (after,)
    out_specs, out_shapes = out_spec, jax.ShapeDtypeStruct(out_shape, out_dtype)
    if relu_outs is not None:
        out_specs = [out_spec] + [spec for _, spec in relu_outs]
        out_shapes = [out_shapes] + [jax.ShapeDtypeStruct(shape, out_dtype) for shape, _ in relu_outs]
    return pl.pallas_call(
        body, name=name, grid=grid, in_specs=in_specs, out_specs=out_specs, out_shape=out_shapes,
        scratch_shapes=[pltpu.VMEM(acc_shape, F32)],
        compiler_params=_params(3),
    )(*args)


def _spec(shape, index_map):
    return pl.BlockSpec(shape, index_map)


def _rms_fwd(name, x, gain, after=None):
    s, d = x.shape
    t = ROW_TILE

    def body(x_ref, g_ref, *rest):
        h_ref, ht_ref = rest[-2:]
        xv = x_ref[...]
        r = lax.rsqrt(jnp.mean(xv * xv, axis=1, keepdims=True) + NORM_EPS)
        h = xv * r * g_ref[...]
        h_ref[...] = h.astype(BF16)
        ht_ref[...] = h.T.astype(BF16)

    in_specs = [_spec((t, d), lambda i: (i, 0)), _spec((1, d), lambda i: (0, 0))]
    args = (x, gain)
    if after is not None:
        in_specs.append(pl.BlockSpec(memory_space=pl.ANY))
        args += (after,)
    return pl.pallas_call(
        body, name=name, grid=(s // t,), in_specs=in_specs,
        out_specs=[_spec((t, d), lambda i: (i, 0)), _spec((d, t), lambda i: (0, i))],
        out_shape=[jax.ShapeDtypeStruct((s, d), BF16), jax.ShapeDtypeStruct((d, s), BF16)],
        compiler_params=_params(1),
    )(*args)


def _rms_bwd(name, dh, x, gain, dres):
    s, d = x.shape
    t = ROW_TILE

    def body(dh_ref, x_ref, g_ref, dres_ref, dx_ref, dg_ref):
        @pl.when(pl.program_id(0) == 0)
        def _():
            dg_ref[...] = jnp.zeros_like(dg_ref)

        xv = x_ref[...]
        dhv = dh_ref[...]
        r = lax.rsqrt(jnp.mean(xv * xv, axis=1, keepdims=True) + NORM_EPS)
        xhat = xv * r
        dhg = dhv * g_ref[...]
        proj = jnp.mean(dhg * xhat, axis=1, keepdims=True)
        dx_ref[...] = dres_ref[...] + r * (dhg - xhat * proj)
        dg_ref[...] += jnp.sum(dhv * xhat, axis=0, keepdims=True)

    return pl.pallas_call(
        body, name=name, grid=(s // t,),
        in_specs=[_spec((t, d), lambda i: (i, 0)), _spec((t, d), lambda i: (i, 0)),
                  _spec((1, d), lambda i: (0, 0)), _spec((t, d), lambda i: (i, 0))],
        out_specs=[_spec((t, d), lambda i: (i, 0)), _spec((1, d), lambda i: (0, 0))],
        out_shape=[jax.ShapeDtypeStruct((s, d), F32), jax.ShapeDtypeStruct((1, d), F32)],
        compiler_params=_params(1),
    )(dh, x, gain, dres)


def _loss_head(x, gain, target):
    s, d = x.shape
    t = ROW_TILE
    n_steps = s // t

    def body(x_ref, g_ref, t_ref, dx_ref, dg_ref, loss_ref, acc):
        i = pl.program_id(0)

        @pl.when(i == 0)
        def _():
            dg_ref[...] = jnp.zeros_like(dg_ref)
            acc[...] = jnp.zeros_like(acc)

        xv = x_ref[...]
        g = g_ref[...]
        r = lax.rsqrt(jnp.mean(xv * xv, axis=1, keepdims=True) + NORM_EPS)
        xhat = xv * r
        err = xhat * g - t_ref[...]
        acc[...] += jnp.sum(err * err, axis=0, keepdims=True)
        dy = err * (1.0 / d)
        dyg = dy * g
        proj = jnp.mean(dyg * xhat, axis=1, keepdims=True)
        dx_ref[...] = r * (dyg - xhat * proj)
        dg_ref[...] += jnp.sum(dy * xhat, axis=0, keepdims=True)

        @pl.when(i == n_steps - 1)
        def _():
            total = jnp.sum(acc[...], axis=1, keepdims=True) * (0.5 / d)
            loss_ref[...] = jnp.broadcast_to(total, loss_ref.shape)

    return pl.pallas_call(
        body, name="loss_head", grid=(n_steps,),
        in_specs=[_spec((t, d), lambda i: (i, 0)), _spec((1, d), lambda i: (0, 0)),
                  _spec((t, d), lambda i: (i, 0))],
        out_specs=[_spec((t, d), lambda i: (i, 0)), _spec((1, d), lambda i: (0, 0)),
                   _spec((1, 128), lambda i: (0, 0))],
        out_shape=[jax.ShapeDtypeStruct((s, d), F32), jax.ShapeDtypeStruct((1, d), F32),
                   jax.ShapeDtypeStruct((1, 128), F32)],
        scratch_shapes=[pltpu.VMEM((1, d), F32)],
        compiler_params=_params(1),
    )(x, gain, target)


def _rope_tables():
    half = HEAD_DIM // 2
    inv_freq = ROPE_THETA ** (-jnp.arange(half, dtype=F32) / half)
    ang = jnp.arange(SEQ, dtype=jnp.int32).astype(F32)[:, None] * inv_freq[None, :]
    cos, sin = jnp.cos(ang), jnp.sin(ang)
    cos_t = jnp.concatenate([cos, cos, cos, cos], axis=1)
    sin_t = jnp.concatenate([-sin, sin, -sin, sin], axis=1)
    return cos_t, sin_t


def _swap_halves(x):
    lane = lax.broadcasted_iota(jnp.int32, x.shape, 1)
    first = (lane % HEAD_DIM) < (HEAD_DIM // 2)
    return jnp.where(first, pltpu.roll(x, 128 - HEAD_DIM // 2, 1), pltpu.roll(x, HEAD_DIM // 2, 1))


def _permuted_specs(t, width):
    specs = [_spec((t, width), lambda i: (i, 0))]
    for d in DILATIONS[1:]:
        specs.append(_spec((d, t // d, width), lambda i: (0, i, 0)))
    return specs


def _permuted_shapes(width, dtype):
    shapes = [jax.ShapeDtypeStruct((SEQ, width), dtype)]
    for d in DILATIONS[1:]:
        shapes.append(jax.ShapeDtypeStruct((d, SEQ // d, width), dtype))
    return shapes


def _attn_prep(name, proj, cos_t, sin_t):
    t = ROW_TILE
    w = ATTN_WIDTH

    def body(q_ref, k_ref, v_ref, cos_ref, sin_ref, *rest):
        outs, scr = rest[:9], rest[9]
        cosv, sinv = cos_ref[...], sin_ref[...]
        for a, (src, roped, scale) in enumerate(((q_ref, True, HEAD_DIM ** -0.5),
                                                 (k_ref, True, 1.0), (v_ref, False, 1.0))):
            o1, o4, o16 = outs[3 * a:3 * a + 3]
            for cb in range(w // 128):
                cols = slice(cb * 128, (cb + 1) * 128)
                val = src[:, cols]
                if roped:
                    val = (val * cosv + _swap_halves(val) * sinv) * scale
                scr[...] = val
                o1[:, cols] = val.astype(BF16)
                for o_ref, d in ((o4, 4), (o16, 16)):
                    for r in range(d):
                        o_ref[r, :, cols] = scr[pl.ds(r, t // d, stride=d), :].astype(BF16)

    out_specs = _permuted_specs(t, w) * 3
    out_shape = _permuted_shapes(w, BF16) * 3
    outs = pl.pallas_call(
        body, name=name, grid=(SEQ // t,),
        in_specs=[_spec((t, w), lambda i: (i, 0)), _spec((t, w), lambda i: (i, 1)),
                  _spec((t, w), lambda i: (i, 2)),
                  _spec((t, 128), lambda i: (i, 0)), _spec((t, 128), lambda i: (i, 0))],
        out_specs=out_specs, out_shape=out_shape,
        scratch_shapes=[pltpu.VMEM((t, 128), F32)],
        compiler_params=_params(1),
    )(proj, proj, proj, cos_t, sin_t)
    q, k, v = outs[0:3], outs[3:6], outs[6:9]
    flat = lambda arr: arr.reshape(SEQ, w)
    return [(flat(q[p]), flat(k[p]), flat(v[p])) for p in range(3)]


def _band_masks():
    row = lax.broadcasted_iota(jnp.int32, (2 * SPAN, 2 * SPAN), 0) % SPAN
    col = lax.broadcasted_iota(jnp.int32, (2 * SPAN, 2 * SPAN), 1)
    is_prev = col < SPAN
    band = (is_prev & (col >= row)) | (~is_prev & (col - SPAN <= row))
    head0 = lax.broadcasted_iota(jnp.int32, (SPAN, 128), 1) < HEAD_DIM
    return band, is_prev, head0


def _stack_heads(x, head0):
    zero = jnp.zeros_like(x)
    return jnp.concatenate([jnp.where(head0, x, zero), jnp.where(head0, zero, x)], axis=0)


def _attn_fwd(name, q, k, v, seg_blocks):
    n_blocks = SEQ // SPAN

    def body(q_ref, k_ref, v_ref, o_ref, lse_ref):
        band, is_prev, head0 = _band_masks()

        def step(b, carry):
            cur = pl.ds(pl.multiple_of(b * SPAN, SPAN), SPAN)
            prev = pl.ds(pl.multiple_of(jnp.maximum(b - 1, 0) * SPAN, SPAN), SPAN)
            qs = _stack_heads(q_ref[cur, :], head0)
            kcat = jnp.concatenate([k_ref[prev, :], k_ref[cur, :]], axis=0)
            vcat = jnp.concatenate([v_ref[prev, :], v_ref[cur, :]], axis=0)
            ok = band & (((b % seg_blocks) != 0) | ~is_prev)
            s = jnp.where(ok, _dot_nt(qs, kcat), MASK_VALUE)
            m = jnp.max(s, axis=1, keepdims=True)
            p = jnp.where(ok, jnp.exp(s - m), 0.0)
            l = jnp.sum(p, axis=1, keepdims=True)
            pv = _dot(p, vcat) / l
            lse = m + jnp.log(l)
            o_ref[cur, :] = jnp.where(head0, pv[:SPAN], pv[SPAN:])
            lse_ref[cur, :] = jnp.where(head0, lse[:SPAN], lse[SPAN:])
            return carry

        lax.fori_loop(0, n_blocks, step, 0, unroll=2)

    col = _spec((SEQ, 128), lambda j: (0, j))
    return pl.pallas_call(
        body, name=name, grid=(ATTN_WIDTH // 128,),
        in_specs=[col, col, col], out_specs=[col, col],
        out_shape=[jax.ShapeDtypeStruct((SEQ, ATTN_WIDTH), F32)] * 2,
        compiler_params=_params(1),
    )(q, k, v)


def _unpermute(dst, src_ref, d, cols):
    n = dst.shape[0] // d
    for r in range(d):
        dst[pl.ds(r, n, stride=d), :] = src_ref[r, :, cols]


def _attn_merge(name, outs, lses, gain):
    t = ROW_TILE
    w = ATTN_WIDTH

    def body(o1, o4, o16, l1, l4, l16, g_ref, an_ref, ant_ref, attn_ref, lse_ref, so4, so16, sl4, sl16):
        for cb in range(w // 128):
            cols = slice(cb * 128, (cb + 1) * 128)
            _unpermute(so4, o4, 4, cols)
            _unpermute(so16, o16, 16, cols)
            _unpermute(sl4, l4, 4, cols)
            _unpermute(sl16, l16, 16, cols)
            la, lb, lc = l1[:, cols], sl4[...], sl16[...]
            m = jnp.maximum(jnp.maximum(la, lb), lc)
            ea, eb, ec = jnp.exp(la - m), jnp.exp(lb - m), jnp.exp(lc - m)
            tot = ea + eb + ec
            attn_ref[:, cols] = (ea * o1[:, cols] + eb * so4[...] + ec * so16[...]) / tot
            lse_ref[:, cols] = m + jnp.log(tot)
        attn = attn_ref[...]
        r = lax.rsqrt(jnp.mean(attn * attn, axis=1, keepdims=True) + NORM_EPS)
        an = attn * r * g_ref[...]
        an_ref[...] = an.astype(BF16)
        ant_ref[...] = an.T.astype(BF16)

    views = lambda arrs: [arrs[0], arrs[1].reshape(4, SEQ // 4, w), arrs[2].reshape(16, SEQ // 16, w)]
    row = _spec((t, w), lambda i: (i, 0))
    return pl.pallas_call(
        body, name=name, grid=(SEQ // t,),
        in_specs=_permuted_specs(t, w) * 2 + [_spec((1, w), lambda i: (0, 0))],
        out_specs=[row, _spec((w, t), lambda i: (0, i)), row, row],
        out_shape=[jax.ShapeDtypeStruct((SEQ, w), BF16), jax.ShapeDtypeStruct((w, SEQ), BF16),
                   jax.ShapeDtypeStruct((SEQ, w), F32), jax.ShapeDtypeStruct((SEQ, w), F32)],
        scratch_shapes=[pltpu.VMEM((t, 128), F32)] * 4,
        compiler_params=_params(1),
    )(*views(outs), *views(lses), gain)


def _head_sum_matrix():
    i = np.arange(ATTN_WIDTH)
    return jnp.asarray((i[:, None] // HEAD_DIM) == (i[None, :] // HEAD_DIM), dtype=F32)


def _attn_bwd_prep(name, d_an, attn, lse, gain, head_sum):
    t = ROW_TILE
    w = ATTN_WIDTH

    def body(dan_ref, attn_ref, lse_ref, g_ref, hs_ref, *rest):
        (do1, do4, do16, dl1, dl4, dl16, ls4, ls16, dg_ref), (sdo, sdl, sls) = rest[:9], rest[9:]

        @pl.when(pl.program_id(0) == 0)
        def _():
            dg_ref[...] = jnp.zeros_like(dg_ref)

        attn = attn_ref[...]
        dan = dan_ref[...]
        r = lax.rsqrt(jnp.mean(attn * attn, axis=1, keepdims=True) + NORM_EPS)
        xhat = attn * r
        dg_ref[...] += jnp.sum(dan * xhat, axis=0, keepdims=True)
        dang = dan * g_ref[...]
        d_o = r * (dang - xhat * jnp.mean(dang * xhat, axis=1, keepdims=True))
        delta = jnp.dot(d_o * attn, hs_ref[...], preferred_element_type=F32,
                        precision=lax.Precision.HIGHEST)
        do1[...] = d_o.astype(BF16)
        dl1[...] = delta
        for cb in range(w // 128):
            cols = slice(cb * 128, (cb + 1) * 128)
            sdo[...] = d_o[:, cols]
            sdl[...] = delta[:, cols]
            sls[...] = lse_ref[:, cols]
            for d, o_do, o_dl, o_ls in ((4, do4, dl4, ls4), (16, do16, dl16, ls16)):
                for rr in range(d):
                    rows = pl.ds(rr, t // d, stride=d)
                    o_do[rr, :, cols] = sdo[rows, :].astype(BF16)
                    o_dl[rr, :, cols] = sdl[rows, :]
                    o_ls[rr, :, cols] = sls[rows, :]

    row = _spec((t, w), lambda i: (i, 0))
    perm = _permuted_specs(t, w)
    outs = pl.pallas_call(
        body, name=name, grid=(SEQ // t,),
        in_specs=[row, row, row, _spec((1, w), lambda i: (0, 0)), _spec((w, w), lambda i: (0, 0))],
        out_specs=perm + perm + perm[1:] + [_spec((1, w), lambda i: (0, 0))],
        out_shape=(_permuted_shapes(w, BF16) + _permuted_shapes(w, F32) + _permuted_shapes(w, F32)[1:]
                   + [jax.ShapeDtypeStruct((1, w), F32)]),
        scratch_shapes=[pltpu.VMEM((t, 128), F32)] * 3,
        compiler_params=_params(1),
    )(d_an, attn, lse, gain, head_sum)
    flat = lambda arr: arr.reshape(SEQ, w)
    d_out = [flat(a) for a in outs[0:3]]
    delta = [flat(a) for a in outs[3:6]]
    lses = [lse, flat(outs[6]), flat(outs[7])]
    return d_out, delta, lses, outs[8]


def _attn_bwd(name, q, k, v, d_out, delta, lse, seg_blocks):
    n_blocks = SEQ // SPAN

    def body(q_ref, k_ref, v_ref, do_ref, dl_ref, lse_ref, dq_ref, dk_ref, dv_ref):
        band, is_prev, head0 = _band_masks()
        dk_ref[...] = jnp.zeros_like(dk_ref)
        dv_ref[...] = jnp.zeros_like(dv_ref)

        def per_head(x):
            return jnp.concatenate([x[:, 0:1], x[:, HEAD_DIM:HEAD_DIM + 1]], axis=0)

        def step(b, carry):
            cur = pl.ds(pl.multiple_of(b * SPAN, SPAN), SPAN)
            prev = pl.ds(pl.multiple_of(jnp.maximum(b - 1, 0) * SPAN, SPAN), SPAN)
            qs = _stack_heads(q_ref[cur, :], head0)
            dos = _stack_heads(do_ref[cur, :], head0)
            kcat = jnp.concatenate([k_ref[prev, :], k_ref[cur, :]], axis=0)
            vcat = jnp.concatenate([v_ref[prev, :], v_ref[cur, :]], axis=0)
            ok = band & (((b % seg_blocks) != 0) | ~is_prev)
            p = jnp.where(ok, jnp.exp(_dot_nt(qs, kcat) - per_head(lse_ref[cur, :])), 0.0)
            ds = p * (_dot_nt(dos, vcat) - per_head(dl_ref[cur, :]))
            dq = _dot(ds, kcat)
            dq_ref[cur, :] = jnp.where(head0, dq[:SPAN], dq[SPAN:])
            dk = _dot(ds.T, qs)
            dv = _dot(p.T, dos)
            dk_ref[prev, :] += dk[:SPAN]
            dv_ref[prev, :] += dv[:SPAN]
            dk_ref[cur, :] += dk[SPAN:]
            dv_ref[cur, :] += dv[SPAN:]
            return carry

        lax.fori_loop(0, n_blocks, step, 0, unroll=2)

    col = _spec((SEQ, 128), lambda j: (0, j))
    return pl.pallas_call(
        body, name=name, grid=(ATTN_WIDTH // 128,),
        in_specs=[col] * 6, out_specs=[col] * 3,
        out_shape=[jax.ShapeDtypeStruct((SEQ, ATTN_WIDTH), F32)] * 3,
        compiler_params=_params(1),
    )(q, k, v, d_out, delta, lse)


def _attn_bwd_post(name, grads, cos_t, sin_t):
    t = ROW_TILE
    w = ATTN_WIDTH

    def body(*refs):
        ins, cos_ref, sin_ref, out_ref, s4, s16 = refs[:9], refs[9], refs[10], refs[11], refs[12], refs[13]
        cosv, sinv = cos_ref[...], sin_ref[...]
        for a in range(3):
            g1, g4, g16 = ins[a], ins[3 + a], ins[6 + a]
            for cb in range(w // 128):
                cols = slice(cb * 128, (cb + 1) * 128)
                _unpermute(s4, g4, 4, cols)
                _unpermute(s16, g16, 16, cols)
                val = g1[:, cols] + s4[...] + s16[...]
                if a < 2:
                    val = val * cosv + _swap_halves(val * sinv)
                if a == 0:
                    val = val * (HEAD_DIM ** -0.5)
                out_ref[:, a * w + cb * 128:a * w + (cb + 1) * 128] = val

    views = []
    for p, d in enumerate(DILATIONS):
        for a in range(3):
            views.append(grads[p][a] if d == 1 else grads[p][a].reshape(d, SEQ // d, w))
    perm = _permuted_specs(t, w)
    in_specs = [perm[0]] * 3 + [perm[1]] * 3 + [perm[2]] * 3
    return pl.pallas_call(
        body, name=name, grid=(SEQ // t,),
        in_specs=in_specs + [_spec((t, 128), lambda i: (i, 0))] * 2,
        out_specs=_spec((t, 3 * w), lambda i: (i, 0)),
        out_shape=jax.ShapeDtypeStruct((SEQ, 3 * w), F32),
        scratch_shapes=[pltpu.VMEM((t, 128), F32)] * 2,
        compiler_params=_params(1),
    )(*views, cos_t, sin_t)


N_LEVELS = 7


def _hgrn_consts():
    c = CHUNK
    i = np.arange(c)[:, None]
    s = np.arange(c)[None, :]
    blocks = [s <= i]
    for lv in range(N_LEVELS):
        bs = c >> lv
        h = bs // 2
        m = (i // bs) * bs + h - 1
        second = (i % bs) >= h
        blocks.append((second & (s > m) & (s <= i)) | (~second & (s > i) & (s <= m)))
    blocks.append(s > i)
    stack = np.concatenate(blocks, axis=0).astype(np.float32)
    return jnp.asarray(stack, dtype=BF16), jnp.asarray(stack.T, dtype=BF16)


def _exact_dot(m01, x):
    hi = x.astype(BF16)
    r1 = x - hi.astype(F32)
    mid = r1.astype(BF16)
    lo = (r1 - mid.astype(F32)).astype(BF16)
    n = x.shape[1]
    full = jnp.dot(m01, jnp.concatenate([hi, mid, lo], axis=1), preferred_element_type=F32)
    return (full[:, :n] + full[:, n:2 * n]) + full[:, 2 * n:]


def _hgrn_gates(qh, z, lb):
    sq = _sigmoid(qh)
    q = qh * sq * (HGRN_DIM ** -0.5)
    sig = _sigmoid(z)
    sigm = _sigmoid(-z)
    f = lb + (1.0 - lb) * sig
    k = (1.0 - lb) * sigm
    return q, k, f, sq, sig, sigm


def _level_masks(lv):
    row = lax.broadcasted_iota(jnp.int32, (CHUNK, CHUNK), 0)
    col = lax.broadcasted_iota(jnp.int32, (CHUNK, CHUNK), 1)
    shift = N_LEVELS - lv
    second = (row & (CHUNK >> (lv + 1))) != 0
    same = (row >> shift) == (col >> shift)
    return second, same


def _hgrn_fwd(name, proj, lb, gain, stack):
    t = ROW_TILE
    per = t // CHUNK
    n_rb = SEQ // t
    n_chunks = SEQ // CHUNK
    col0 = 3 * ATTN_WIDTH // 128

    def body(q_ref, f_ref, i_ref, g_ref, lb_ref, gain_ref, stack_ref,
             rec_ref, rect_ref, o_ref, st_out, a_out, st):
        @pl.when(pl.program_id(1) == 0)
        def _():
            st[...] = jnp.zeros_like(st)

        lbv = lb_ref[...]
        row = lax.broadcasted_iota(jnp.int32, (CHUNK, CHUNK), 0)
        col = lax.broadcasted_iota(jnp.int32, (CHUNK, CHUNK), 1)
        for c in range(per):
            rows = slice(c * CHUNK, (c + 1) * CHUNK)
            qh, z, v, gh = q_ref[rows, :], f_ref[rows, :], i_ref[rows, :], g_ref[rows, :]
            q, k, f, _, _, _ = _hgrn_gates(qh, z, lbv)
            dec = _exact_dot(stack_ref[...], jnp.log(f))
            g = dec[0:CHUNK]
            to_end = dec[(N_LEVELS + 1) * CHUNK:(N_LEVELS + 2) * CHUNK]
            a = jnp.where(row == col, jnp.sum(q * k, axis=1, keepdims=True), 0.0)
            for lv in range(N_LEVELS):
                e = jnp.exp(dec[(lv + 1) * CHUNK:(lv + 2) * CHUNK])
                second, same = _level_masks(lv)
                qt = jnp.where(second, q * e, 0.0)
                kt = jnp.where(second, 0.0, k * e)
                a = a + jnp.where(same, _dot_nt(qt, kt), 0.0)
            st_prev = st[...]
            st_out[c] = st_prev
            a_out[c] = a
            o = _dot(a, v) + _dot_nt(q * jnp.exp(g), st_prev)
            k_end = k * jnp.exp(to_end)
            st[...] = st_prev * jnp.exp(g[CHUNK - 1:CHUNK, :]) + _dot(v.T, k_end)
            o_ref[rows, :] = o
            r = lax.rsqrt(jnp.mean(o * o, axis=1, keepdims=True) + NORM_EPS)
            rec = o * r * gain_ref[...] * (gh * _sigmoid(gh))
            rec_ref[rows, :] = rec.astype(BF16)
            rect_ref[:, rows] = rec.T.astype(BF16)

    def col_spec(tt):
        return _spec((t, HGRN_DIM), lambda h, rb: (rb, col0 + HGRN_HEADS * tt + h))

    chunk_spec = _spec((None, per, CHUNK, CHUNK), lambda h, rb: (h, rb, 0, 0))
    return pl.pallas_call(
        body, name=name, grid=(HGRN_HEADS, n_rb),
        in_specs=[col_spec(0), col_spec(1), col_spec(2), col_spec(3),
                  _spec((None, 1, HGRN_DIM), lambda h, rb: (h, 0, 0)),
                  _spec((1, HGRN_DIM), lambda h, rb: (0, 0)),
                  _spec(stack.shape, lambda h, rb: (0, 0))],
        out_specs=[_spec((t, HGRN_DIM), lambda h, rb: (rb, h)),
                   _spec((HGRN_DIM, t), lambda h, rb: (h, rb)),
                   _spec((t, HGRN_DIM), lambda h, rb: (rb, h)),
                   chunk_spec, chunk_spec],
        out_shape=[jax.ShapeDtypeStruct((SEQ, HGRN_WIDTH), BF16),
                   jax.ShapeDtypeStruct((HGRN_WIDTH, SEQ), BF16),
                   jax.ShapeDtypeStruct((SEQ, HGRN_WIDTH), F32),
                   jax.ShapeDtypeStruct((HGRN_HEADS, n_chunks, CHUNK, CHUNK), F32),
                   jax.ShapeDtypeStruct((HGRN_HEADS, n_chunks, CHUNK, CHUNK), F32)],
        scratch_shapes=[pltpu.VMEM((CHUNK, CHUNK), F32)],
        compiler_params=_params(2),
    )(proj, proj, proj, proj, lb, gain, stack)


def _hgrn_bwd(name, proj, d_rec, o_pre, states, scores, lb, gain, stack, stack_t):
    t = ROW_TILE
    per = t // CHUNK
    n_rb = SEQ // t
    col0 = 3 * ATTN_WIDTH // 128

    def body(q_ref, f_ref, i_ref, g_ref, drec_ref, o_ref, st_ref, a_ref, lb_ref, gain_ref,
             stack_ref, stack_t_ref, dq_ref, df_ref, di_ref, dg_ref, dlb_ref, dgain_ref, dst):
        @pl.when(pl.program_id(1) == 0)
        def _():
            dst[...] = jnp.zeros_like(dst)
            dlb_ref[...] = jnp.zeros_like(dlb_ref)
            dgain_ref[...] = jnp.zeros_like(dgain_ref)

        lbv = lb_ref[...]
        gain_v = gain_ref[...]
        row = lax.broadcasted_iota(jnp.int32, (CHUNK, CHUNK), 0)
        col = lax.broadcasted_iota(jnp.int32, (CHUNK, CHUNK), 1)
        for c in reversed(range(per)):
            rows = slice(c * CHUNK, (c + 1) * CHUNK)
            qh, z, v, gh = q_ref[rows, :], f_ref[rows, :], i_ref[rows, :], g_ref[rows, :]
            q, k, f, sq, sig, sigm = _hgrn_gates(qh, z, lbv)
            dec = _exact_dot(stack_ref[...], jnp.log(f))
            g = dec[0:CHUNK]
            to_end = dec[(N_LEVELS + 1) * CHUNK:(N_LEVELS + 2) * CHUNK]
            e_g = jnp.exp(g)
            e_end = jnp.exp(to_end)
            e_last = jnp.exp(g[CHUNK - 1:CHUNK, :])
            q_in = q * e_g
            k_end = k * e_end
            st_prev = st_ref[c]
            a = a_ref[c]
            dst_new = dst[...]

            o = o_ref[rows, :]
            drec = drec_ref[rows, :]
            sg = _sigmoid(gh)
            r = lax.rsqrt(jnp.mean(o * o, axis=1, keepdims=True) + NORM_EPS)
            ohat = o * r
            d_gh = drec * (ohat * gain_v) * (sg * (1.0 + gh * (1.0 - sg)))
            d_on = drec * (gh * sg)
            dgain_ref[...] += jnp.sum(d_on * ohat, axis=0, keepdims=True)
            d_ohat = d_on * gain_v
            d_o = r * (d_ohat - ohat * jnp.mean(d_ohat * ohat, axis=1, keepdims=True))

            d_a = jnp.where(row >= col, _dot_nt(d_o, v), 0.0)
            d_at = jnp.where(col >= row, _dot_nt(v, d_o), 0.0)
            d_v = _dot(a.T, d_o) + _dot_nt(k_end, dst_new)
            d_q_in = _dot(d_o, st_prev)
            d_k_end = _dot(v, dst_new)
            d_q = d_q_in * e_g
            d_k = d_k_end * e_end
            diag = jnp.sum(d_o * v, axis=1, keepdims=True)
            d_q = d_q + diag * k
            d_k = d_k + diag * q
            d_dec = [q_in * d_q_in]
            for lv in range(N_LEVELS):
                e = jnp.exp(dec[(lv + 1) * CHUNK:(lv + 2) * CHUNK])
                second, same = _level_masks(lv)
                qt = jnp.where(second, q * e, 0.0)
                kt = jnp.where(second, 0.0, k * e)
                d_qt = _dot(jnp.where(same, d_a, 0.0), kt)
                d_kt = _dot(jnp.where(same, d_at, 0.0), qt)
                d_q = d_q + jnp.where(second, d_qt * e, 0.0)
                d_k = d_k + jnp.where(second, 0.0, d_kt * e)
                d_dec.append(jnp.where(second, qt * d_qt, kt * d_kt))
            d_dec.append(k_end * d_k_end)
            flux = jnp.sum(dst_new * st_prev, axis=0, keepdims=True) * e_last
            d_lf = _exact_dot(stack_t_ref[...], jnp.concatenate(d_dec, axis=0)) + flux
            dst[...] = dst_new * e_last + _dot(d_o.T, q_in)

            d_f = d_lf / f - d_k
            dlb_ref[...] += jnp.sum(d_f * sigm, axis=0, keepdims=True)
            dq_ref[rows, :] = d_q * (HGRN_DIM ** -0.5) * (sq * (1.0 + qh * (1.0 - sq)))
            df_ref[rows, :] = d_f * (1.0 - lbv) * sig * sigm
            di_ref[rows, :] = d_v
            dg_ref[rows, :] = d_gh

    last = n_rb - 1

    def col_spec(tt):
        return _spec((t, HGRN_DIM), lambda h, rb: (last - rb, col0 + HGRN_HEADS * tt + h))

    head_col = _spec((t, HGRN_DIM), lambda h, rb: (last - rb, h))
    chunk_spec = _spec((None, per, CHUNK, CHUNK), lambda h, rb: (h, last - rb, 0, 0))
    vec_spec = _spec((None, 1, HGRN_DIM), lambda h, rb: (h, 0, 0))
    outs = pl.pallas_call(
        body, name=name, grid=(HGRN_HEADS, n_rb),
        in_specs=[col_spec(0), col_spec(1), col_spec(2), col_spec(3), head_col, head_col,
                  chunk_spec, chunk_spec, vec_spec,
                  _spec((1, HGRN_DIM), lambda h, rb: (0, 0)),
                  _spec(stack.shape, lambda h, rb: (0, 0)), _spec(stack_t.shape, lambda h, rb: (0, 0))],
        out_specs=[head_col] * 4 + [vec_spec, vec_spec],
        out_shape=[jax.ShapeDtypeStruct((SEQ, HGRN_WIDTH), F32)] * 4
                  + [jax.ShapeDtypeStruct((HGRN_HEADS, 1, HGRN_DIM), F32)] * 2,
        scratch_shapes=[pltpu.VMEM((CHUNK, CHUNK), F32)],
        compiler_params=_params(2),
    )(proj, proj, proj, proj, d_rec, o_pre, states, scores, lb, gain, stack, stack_t)
    return outs


ANY_SPEC = pl.BlockSpec(memory_space=pl.ANY)


def _my_place():
    return lax.axis_index("x"), lax.axis_index("y"), lax.axis_index("c")


def _other_chips(x, y):
    return [(1 - x, y), (x, 1 - y), (1 - x, 1 - y)]


def _remote(src, dst, send_sem, recv_sem, device):
    return pltpu.make_async_remote_copy(src_ref=src, dst_ref=dst, send_sem=send_sem, recv_sem=recv_sem,
                                        device_id=device, device_id_type=MESH)


def _staged_copies(srcs, dsts, stage, sems):
    loads = [pltpu.make_async_copy(srcs[i], stage[i], sems.at[i]) for i in range(len(srcs))]
    for cp in loads:
        cp.start()
    stores = []
    for i, cp in enumerate(loads):
        cp.wait()
        stores.append(pltpu.make_async_copy(stage[i], dsts[i], sems.at[i]))
        stores[-1].start()
    return stores


def _gather_weights(name, shards):
    n = len(shards)

    def body(*refs):
        ins, outs = refs[:n], refs[n:2 * n]
        ici_send, ici_recv, d2d_send, d2d_recv, local_sems = refs[2 * n:2 * n + 5]
        stage = refs[2 * n + 5:]
        x, y, c = _my_place()
        me = 2 * x + y
        chips = _other_chips(x, y)

        def half(i, which):
            h = ins[i].shape[0] // 2
            return pl.ds(which * h, h)

        sends = []
        for i in range(n):
            for j, (px, py) in enumerate(chips):
                sends.append(_remote(ins[i].at[half(i, c), :], outs[i].at[me, half(i, c), :],
                                     ici_send.at[3 * i + j], ici_recv.at[3 * i + j], (px, py, c)))
        for cp in sends:
            cp.start()
        local = _staged_copies(ins, [outs[i].at[me] for i in range(n)], stage, local_sems)
        for i in range(n):
            for j, (px, py) in enumerate(chips):
                landed = outs[i].at[2 * px + py, half(i, c), :]
                _remote(landed, landed, ici_send.at[3 * i + j], ici_recv.at[3 * i + j], (px, py, c)).wait_recv()
                forward = _remote(landed, landed, d2d_send.at[3 * i + j], d2d_recv.at[3 * i + j], (x, y, 1 - c))
                forward.start()
                sends.append(forward)
        for i in range(n):
            for j, (px, py) in enumerate(chips):
                other = outs[i].at[2 * px + py, half(i, 1 - c), :]
                _remote(other, other, d2d_send.at[3 * i + j], d2d_recv.at[3 * i + j], (x, y, 1 - c)).wait_recv()
        for cp in sends:
            cp.wait_send()
        for cp in local:
            cp.wait()

    return pl.pallas_call(
        body, name=name, in_specs=[ANY_SPEC] * n, out_specs=[ANY_SPEC] * n,
        out_shape=[jax.ShapeDtypeStruct((N_CHIPS,) + s.shape, s.dtype) for s in shards],
        scratch_shapes=([pltpu.SemaphoreType.DMA((3 * n,))] * 4 + [pltpu.SemaphoreType.DMA((n,))]
                        + [pltpu.VMEM(s.shape, s.dtype) for s in shards]),
        compiler_params=pltpu.CompilerParams(vmem_limit_bytes=VMEM_LIMIT),
    )(*shards)


def _exchange_halves(name, grads):
    n = len(grads)

    def body(*refs):
        ins, outs = refs[:n], refs[n:2 * n]
        send_sems, recv_sems = refs[2 * n:]
        x, y, c = _my_place()
        copies = []
        for i in range(n):
            h = ins[i].shape[1] // 2
            copies.append(_remote(ins[i].at[:, pl.ds((1 - c) * h, h), :], outs[i],
                                  send_sems.at[i], recv_sems.at[i], (x, y, 1 - c)))
        for cp in copies:
            cp.start()
        for cp in copies:
            cp.wait()

    return pl.pallas_call(
        body, name=name, in_specs=[ANY_SPEC] * n, out_specs=[ANY_SPEC] * n,
        out_shape=[jax.ShapeDtypeStruct((g.shape[0], g.shape[1] // 2, g.shape[2]), g.dtype) for g in grads],
        scratch_shapes=[pltpu.SemaphoreType.DMA((n,)), pltpu.SemaphoreType.DMA((n,))],
    )(*grads)


def _add_own_half(name, g, received, core):
    n_sh, r, cc = g.shape
    h = r // 2
    th = min(h, 256)
    nb = h // th

    def body(core_ref, g_ref, r_ref, o_ref):
        del core_ref
        o_ref[...] = (g_ref[...] + r_ref[...]).astype(BF16)

    grid_spec = pltpu.PrefetchScalarGridSpec(
        num_scalar_prefetch=1, grid=(n_sh, nb),
        in_specs=[pl.BlockSpec((None, th, cc), lambda j, i, core_ref: (j, core_ref[0] * nb + i, 0)),
                  pl.BlockSpec((None, th, cc), lambda j, i, core_ref: (j, i, 0))],
        out_specs=pl.BlockSpec((None, th, cc), lambda j, i, core_ref: (j, i, 0)))
    return pl.pallas_call(
        body, name=name, grid_spec=grid_spec,
        out_shape=jax.ShapeDtypeStruct((n_sh, h, cc), BF16), compiler_params=_params(2),
    )(core, g, received)


def _exchange_chips(name, parts):
    n = len(parts)

    def body(*refs):
        ins, outs = refs[:n], refs[n:2 * n]
        send_sems, recv_sems, local_sems = refs[2 * n:2 * n + 3]
        stage = refs[2 * n + 3:]
        x, y, c = _my_place()
        me = 2 * x + y
        chips = _other_chips(x, y)
        sends = []
        for i in range(n):
            for j, (px, py) in enumerate(chips):
                sends.append(_remote(ins[i].at[2 * px + py], outs[i].at[me], send_sems.at[3 * i + j],
                                     recv_sems.at[3 * i + j], (px, py, c)))
        for cp in sends:
            cp.start()
        local = _staged_copies([ins[i].at[me] for i in range(n)], [outs[i].at[me] for i in range(n)],
                               stage, local_sems)
        for i in range(n):
            for j, (px, py) in enumerate(chips):
                _remote(ins[i].at[me], outs[i].at[2 * px + py], send_sems.at[3 * i + j],
                        recv_sems.at[3 * i + j], (px, py, c)).wait_recv()
        for cp in sends:
            cp.wait_send()
        for cp in local:
            cp.wait()

    return pl.pallas_call(
        body, name=name, in_specs=[ANY_SPEC] * n, out_specs=[ANY_SPEC] * n,
        out_shape=[jax.ShapeDtypeStruct(p.shape, p.dtype) for p in parts],
        scratch_shapes=([pltpu.SemaphoreType.DMA((3 * n,)), pltpu.SemaphoreType.DMA((3 * n,)),
                         pltpu.SemaphoreType.DMA((n,))]
                        + [pltpu.VMEM(p.shape[1:], p.dtype) for p in parts]),
        compiler_params=pltpu.CompilerParams(vmem_limit_bytes=VMEM_LIMIT),
    )(*parts)


HBM_SPEC = pl.BlockSpec(memory_space=pltpu.HBM)
SEM_SPEC = pl.BlockSpec(memory_space=pltpu.SEMAPHORE)
SPLIT_PARAMS = pltpu.CompilerParams(has_side_effects=pltpu.SideEffectType.DATAFLOW_SIDE_EFFECTING)


def _chip_copies(ins, lands, send_sems, recv_sems, sliced):
    x, y, c = _my_place()
    me = 2 * x + y
    pairs = []
    for i in range(len(ins)):
        for j, (px, py) in enumerate(_other_chips(x, y)):
            theirs = 2 * px + py
            src = ins[i].at[theirs] if sliced else ins[i]
            sems = (send_sems.at[3 * i + j], recv_sems.at[3 * i + j], (px, py, c))
            pairs.append((_remote(src, lands[i].at[me], *sems), _remote(src, lands[i].at[theirs], *sems)))
    return pairs


def _exchange_start(name, srcs, lands, sliced, after):
    n = len(srcs)

    def body(*refs):
        ins, land_refs = refs[:n], refs[n:2 * n]
        send_sems, recv_sems = refs[2 * n + 1:2 * n + 3]
        token = refs[-1]
        for send, _ in _chip_copies(ins, land_refs, send_sems, recv_sems, sliced):
            send.start()
        token[...] = jnp.zeros_like(token)

    arrays = list(srcs) + list(lands)
    outs = pl.pallas_call(
        body, name=name,
        in_specs=[HBM_SPEC] * (2 * n) + [ANY_SPEC],
        out_shape=([pltpu.SemaphoreType.DMA((3 * n,))] * 2 + [pltpu.HBM(a.shape, a.dtype) for a in arrays]
                   + [jax.ShapeDtypeStruct((8, 128), F32)]),
        out_specs=[SEM_SPEC] * 2 + [HBM_SPEC] * (2 * n) + [pl.BlockSpec(memory_space=pltpu.VMEM)],
        input_output_aliases={i: 2 + i for i in range(2 * n)},
        compiler_params=SPLIT_PARAMS,
    )(*[pltpu.with_memory_space_constraint(a, pltpu.HBM) for a in arrays], after)
    return outs[:2], outs[2:2 + 2 * n], outs[-1]


def _exchange_wait(name, sems, passed, sliced, after):
    n = len(passed) // 2

    def body(*refs):
        ins, land_refs = refs[:n], refs[n:2 * n]
        send_sems, recv_sems = refs[2 * n:2 * n + 2]
        for send, arrive in _chip_copies(ins, land_refs, send_sems, recv_sems, sliced):
            send.wait_send()
            arrive.wait_recv()

    outs = pl.pallas_call(
        body, name=name,
        in_specs=[HBM_SPEC] * (2 * n) + [SEM_SPEC] * 2 + [ANY_SPEC],
        out_shape=[pltpu.HBM(a.shape, a.dtype) for a in passed],
        out_specs=[HBM_SPEC] * (2 * n),
        input_output_aliases={i: i for i in range(2 * n)},
        compiler_params=SPLIT_PARAMS,
    )(*passed, *sems, after)
    return outs[n:]


def _own_slot(own, me):
    land = lax.empty((N_CHIPS,) + own.shape, own.dtype)
    return lax.dynamic_update_index_in_dim(land, own, me, 0)


def _sum_chips(name, parts):
    n_sh, h, cc = parts.shape
    th = min(h, 256)

    def body(p_ref, o_ref):
        p = [p_ref[j].astype(F32) for j in range(n_sh)]
        o_ref[...] = ((p[0] + p[1]) + p[2]) + p[3]

    return pl.pallas_call(
        body, name=name, grid=(h // th,),
        in_specs=[_spec((n_sh, th, cc), lambda i: (0, i, 0))],
        out_specs=_spec((th, cc), lambda i: (i, 0)),
        out_shape=jax.ShapeDtypeStruct((h, cc), F32), compiler_params=_params(1),
    )(parts)


def _share_halves(halves):
    flat = [t for per_weight in halves for t in per_weight]
    n = len(flat)
    n_w = len(halves)

    def body(*refs):
        ins, outs = refs[:n], refs[n:n + n_w]
        send_sems, recv_sems, local_sems = refs[n + n_w:n + n_w + 3]
        stage = refs[n + n_w + 3:]
        x, y, c = _my_place()
        sends, own = [], []
        for i in range(n):
            w, l = divmod(i, DEPTH)
            h = ins[i].shape[0]
            own.append(outs[w].at[l, pl.ds(c * h, h), :])
            sends.append(_remote(ins[i], own[i], send_sems.at[i], recv_sems.at[i], (x, y, 1 - c)))
        for cp in sends:
            cp.start()
        local = _staged_copies(ins, own, stage, local_sems)
        for i in range(n):
            w, l = divmod(i, DEPTH)
            h = ins[i].shape[0]
            _remote(ins[i], outs[w].at[l, pl.ds((1 - c) * h, h), :], send_sems.at[i], recv_sems.at[i],
                    (x, y, 1 - c)).wait_recv()
        for cp in sends:
            cp.wait_send()
        for cp in local:
            cp.wait()

    return pl.pallas_call(
        body, name="share_halves", in_specs=[ANY_SPEC] * n, out_specs=[ANY_SPEC] * n_w,
        out_shape=[jax.ShapeDtypeStruct((DEPTH, 2 * per_weight[0].shape[0], per_weight[0].shape[1]), F32)
                   for per_weight in halves],
        scratch_shapes=([pltpu.SemaphoreType.DMA((n,))] * 3 + [pltpu.VMEM(t.shape, t.dtype) for t in flat]),
        compiler_params=pltpu.CompilerParams(vmem_limit_bytes=VMEM_LIMIT),
    )(*flat)


def _all_reduce_small(pack):
    def body(p_ref, o_ref, recv, send_sems, recv_sems):
        x, y, c = _my_place()
        me = 4 * x + 2 * y + c
        recv[me] = p_ref[...]
        peers = []
        for k in range(1, N_DEV):
            px, py, pc = (x + (k >> 2)) % 2, (y + ((k >> 1) & 1)) % 2, (c + (k & 1)) % 2
            peers.append((px, py, pc))
        sends = [_remote(p_ref, recv.at[me], send_sems.at[k], recv_sems.at[k], peer)
                 for k, peer in enumerate(peers)]
        for cp in sends:
            cp.start()
        for k, (px, py, pc) in enumerate(peers):
            _remote(p_ref, recv.at[4 * px + 2 * py + pc], send_sems.at[k], recv_sems.at[k],
                    (px, py, pc)).wait_recv()
        for cp in sends:
            cp.wait_send()
        total = recv[0]
        for d in range(1, N_DEV):
            total = total + recv[d]
        o_ref[...] = total

    vmem = pl.BlockSpec(memory_space=pltpu.VMEM)
    return pl.pallas_call(
        body, name="all_reduce_small", in_specs=[vmem], out_specs=vmem,
        out_shape=jax.ShapeDtypeStruct(pack.shape, F32),
        scratch_shapes=[pltpu.VMEM((N_DEV,) + pack.shape, F32),
                        pltpu.SemaphoreType.DMA((N_DEV - 1,)), pltpu.SemaphoreType.DMA((N_DEV - 1,))],
    )(pack)


def _adamw(name, w, g, m, v):
    r, cc = w.shape
    th = min(r, 256)

    def body(w_ref, g_ref, m_ref, v_ref, d_ref, m_out, v_out):
        gv = g_ref[...]
        m2 = ADAM_B1 * m_ref[...] + (1.0 - ADAM_B1) * gv
        v2 = ADAM_B2 * v_ref[...] + (1.0 - ADAM_B2) * (gv * gv)
        m_hat = m2 / (1.0 - ADAM_B1 ** ADAM_STEP)
        v_hat = v2 / (1.0 - ADAM_B2 ** ADAM_STEP)
        d_ref[...] = -ADAM_LR * (m_hat / (jnp.sqrt(v_hat) + ADAM_EPS) + ADAM_WD * w_ref[...])
        m_out[...] = m2
        v_out[...] = v2

    tile = _spec((th, cc), lambda i: (i, 0))
    return pl.pallas_call(
        body, name=name, grid=(r // th,), in_specs=[tile] * 4, out_specs=[tile] * 3,
        out_shape=[jax.ShapeDtypeStruct((r, cc), F32)] * 3, compiler_params=_params(1),
    )(w, g, m, v)


def _lower_bounds(lb_logits):
    p = jax.nn.softmax(lb_logits.astype(F32), axis=0)
    return jnp.cumsum(p, axis=0) - p[0]


def _row_tile_specs(tm, width):
    return _spec((tm, width), lambda i, j, k: (i, 0))


def _layer_forward(l, x_in, small, weights, consts, after=None):
    win, rest = weights
    cos_t, sin_t, stack, _, _ = consts
    tm = MM_TILE
    n_row = SEQ // tm
    saved = {"x_in": x_in}

    h, h_t = _rms_fwd(f"norm_mix{l}", x_in, small["norm_mix"][l][None, :], after=after)
    proj = _matmul(f"proj{l}", h, win,
                   _spec((tm, D_MODEL), lambda i, j, k: (i, 0)),
                   _spec((None, D_MODEL, SHARD_IN), lambda i, j, k: (j, 0, 0)),
                   (SEQ, IN_W), F32, _spec((tm, SHARD_IN), lambda i, j, k: (i, j)),
                   (n_row, N_CHIPS, 1), (tm, SHARD_IN))
    saved.update(h_t=h_t, proj=proj)

    qkv = _attn_prep(f"attn_prep{l}", proj, cos_t, sin_t)
    outs, lses = [], []
    for p, d in enumerate(DILATIONS):
        o, lse = _attn_fwd(f"attn_fwd{l}_{d}", *qkv[p], SEQ // d // SPAN)
        outs.append(o)
        lses.append(lse)
    an, an_t, attn, lse = _attn_merge(f"attn_merge{l}", outs, lses, small["attn_out_gain"][l][None, :])
    saved.update(qkv=qkv, an_t=an_t, attn=attn, lse=lse)

    lb3 = small["lower"][l].reshape(HGRN_HEADS, 1, HGRN_DIM)
    rec, rec_t, o_pre, states, scores = _hgrn_fwd(f"hgrn_fwd{l}", proj, lb3,
                                                  small["hgrn_out_gain"][l][None, :], stack)
    wo, wu, wd = rest(rec)
    saved.update(rec_t=rec_t, o_pre=o_pre, states=states, scores=scores, lb3=lb3, weights=(win, wo, wu, wd))

    def out_proj(name, a, part, resid):
        return _matmul(name, a, wo,
                       _spec((tm, SHARD_OUT), lambda i, j, k: (i, k)),
                       _spec((None, SHARD_OUT, D_MODEL), lambda i, j, k: (k + 2 * part, 0, 0)),
                       (SEQ, D_MODEL), F32, _spec((tm, D_MODEL), lambda i, j, k: (i, 0)),
                       (n_row, 1, 2), (tm, D_MODEL),
                       extra=resid, extra_spec=_spec((tm, D_MODEL), lambda i, j, k: (i, 0)), epilogue="add")

    x_mid = out_proj(f"out_rec{l}", rec, 1, out_proj(f"out_attn{l}", an, 0, x_in))
    saved["x_mid"] = x_mid

    h2, h2_t = _rms_fwd(f"norm_mlp{l}", x_mid, small["norm_mlp"][l][None, :])
    a, relu_u, a_t = _matmul(
        f"up{l}", h2, wu,
        _spec((tm, D_MODEL), lambda i, j, k: (i, 0)),
        _spec((None, D_MODEL, SHARD_MLP), lambda i, j, k: (j, 0, 0)),
        (SEQ, MLP_HIDDEN), BF16, _spec((tm, SHARD_MLP), lambda i, j, k: (i, j)),
        (n_row, N_CHIPS, 1), (tm, SHARD_MLP), epilogue="relu2",
        relu_outs=[((SEQ, MLP_HIDDEN), _spec((tm, SHARD_MLP), lambda i, j, k: (i, j))),
                   ((MLP_HIDDEN, SEQ), _spec((SHARD_MLP, tm), lambda i, j, k: (j, i)))])
    x_out = _matmul(f"down{l}", a, wd,
                    _spec((tm, SHARD_MLP), lambda i, j, k: (i, k)),
                    _spec((None, SHARD_MLP, D_MODEL), lambda i, j, k: (k, 0, 0)),
                    (SEQ, D_MODEL), F32, _spec((tm, D_MODEL), lambda i, j, k: (i, 0)),
                    (n_row, 1, N_CHIPS), (tm, D_MODEL),
                    extra=x_mid, extra_spec=_spec((tm, D_MODEL), lambda i, j, k: (i, 0)), epilogue="add")
    saved.update(h2_t=h2_t, relu_u=relu_u, a_t=a_t)
    return x_out, saved


def _layer_backward(l, dx, saved, small, consts, on_grads, after=None):
    win, wo, wu, wd = saved["weights"]
    cos_t, sin_t, stack, stack_t, head_sum = consts
    tm = MM_TILE
    n_row = SEQ // tm
    n_k = SEQ // tm

    du = _matmul(f"d_u{l}", dx, wd,
                 _spec((tm, D_MODEL), lambda i, j, k: (i, 0)),
                 _spec((None, SHARD_MLP, D_MODEL), lambda i, j, k: (j, 0, 0)),
                 (SEQ, MLP_HIDDEN), BF16, _spec((tm, SHARD_MLP), lambda i, j, k: (i, j)),
                 (n_row, N_CHIPS, 1), (tm, SHARD_MLP), nt=True,
                 extra=saved["relu_u"], extra_spec=_spec((tm, SHARD_MLP), lambda i, j, k: (i, j)),
                 epilogue="relu2_grad", after=after)
    d_wd = _matmul(f"d_wdown{l}", saved["a_t"], dx,
                   _spec((SHARD_MLP, tm), lambda i, j, k: (i, k)),
                   _spec((tm, D_MODEL), lambda i, j, k: (k, 0)),
                   (N_CHIPS, SHARD_MLP, D_MODEL), F32, _spec((None, SHARD_MLP, D_MODEL), lambda i, j, k: (i, 0, 0)),
                   (N_CHIPS, 1, n_k), (SHARD_MLP, D_MODEL))
    dh2 = _matmul(f"d_h2_{l}", du, wu,
                  _spec((tm, SHARD_MLP), lambda i, j, k: (i, k)),
                  _spec((None, D_MODEL, SHARD_MLP), lambda i, j, k: (k, 0, 0)),
                  (SEQ, D_MODEL), F32, _spec((tm, D_MODEL), lambda i, j, k: (i, 0)),
                  (n_row, 1, N_CHIPS), (tm, D_MODEL), nt=True)
    d_wu = _matmul(f"d_wup{l}", saved["h2_t"], du,
                   _spec((D_MODEL, tm), lambda i, j, k: (0, k)),
                   _spec((tm, SHARD_MLP), lambda i, j, k: (k, j)),
                   (N_CHIPS, D_MODEL, SHARD_MLP), F32, _spec((None, D_MODEL, SHARD_MLP), lambda i, j, k: (j, 0, 0)),
                   (1, N_CHIPS, n_k), (D_MODEL, SHARD_MLP))
    dxm, dg_mlp = _rms_bwd(f"norm_mlp_bwd{l}", dh2, saved["x_mid"], small["norm_mlp"][l][None, :], dx)
    after_mlp = on_grads(l, "mlp", (d_wu, d_wd))

    def d_mixed(name, part):
        return _matmul(name, dxm, wo,
                       _spec((tm, D_MODEL), lambda i, j, k: (i, 0)),
                       _spec((None, SHARD_OUT, D_MODEL), lambda i, j, k: (j + 2 * part, 0, 0)),
                       (SEQ, ATTN_WIDTH), F32, _spec((tm, SHARD_OUT), lambda i, j, k: (i, j)),
                       (n_row, 2, 1), (tm, SHARD_OUT), nt=True, after=after_mlp)

    def d_wout(name, a_t):
        return _matmul(name, a_t, dxm,
                       _spec((SHARD_OUT, tm), lambda i, j, k: (i, k)),
                       _spec((tm, D_MODEL), lambda i, j, k: (k, 0)),
                       (2, SHARD_OUT, D_MODEL), F32, _spec((None, SHARD_OUT, D_MODEL), lambda i, j, k: (i, 0, 0)),
                       (2, 1, n_k), (SHARD_OUT, D_MODEL))

    d_an = d_mixed(f"d_attn_n{l}", 0)
    d_rec = d_mixed(f"d_rec{l}", 1)
    d_wo = jnp.concatenate([d_wout(f"d_wout_attn{l}", saved["an_t"]),
                            d_wout(f"d_wout_rec{l}", saved["rec_t"])], axis=0)

    d_out, delta, lses, dg_attn = _attn_bwd_prep(f"attn_bwd_prep{l}", d_an, saved["attn"], saved["lse"],
                                                 small["attn_out_gain"][l][None, :], head_sum)
    grads = []
    for p, d in enumerate(DILATIONS):
        grads.append(_attn_bwd(f"attn_bwd{l}_{d}", *saved["qkv"][p], d_out[p], delta[p], lses[p],
                               SEQ // d // SPAN))
    dp_attn = _attn_bwd_post(f"attn_bwd_post{l}", grads, cos_t, sin_t)

    dq_h, df_h, di_h, dg_h, d_lower, dg_hgrn = _hgrn_bwd(
        f"hgrn_bwd{l}", saved["proj"], d_rec, saved["o_pre"], saved["states"], saved["scores"],
        saved["lb3"], small["hgrn_out_gain"][l][None, :], stack, stack_t)
    dproj = jnp.concatenate([dp_attn, dq_h, df_h, di_h, dg_h], axis=1)

    dh = _matmul(f"d_h{l}", dproj, win,
                 _spec((tm, SHARD_IN), lambda i, j, k: (i, k)),
                 _spec((None, D_MODEL, SHARD_IN), lambda i, j, k: (k, 0, 0)),
                 (SEQ, D_MODEL), F32, _spec((tm, D_MODEL), lambda i, j, k: (i, 0)),
                 (n_row, 1, N_CHIPS), (tm, D_MODEL), nt=True)
    d_win = _matmul(f"d_win{l}", saved["h_t"], dproj,
                    _spec((D_MODEL, tm), lambda i, j, k: (0, k)),
                    _spec((tm, SHARD_IN), lambda i, j, k: (k, j)),
                    (N_CHIPS, D_MODEL, SHARD_IN), F32, _spec((None, D_MODEL, SHARD_IN), lambda i, j, k: (j, 0, 0)),
                    (1, N_CHIPS, n_k), (D_MODEL, SHARD_IN))
    dx_in, dg_mix = _rms_bwd(f"norm_mix_bwd{l}", dh, saved["x_in"], small["norm_mix"][l][None, :], dxm)

    small_grads = {"norm_mix": dg_mix[0], "attn_out_gain": dg_attn[0],
                   "lower": d_lower.reshape(HGRN_WIDTH),
                   "hgrn_out_gain": jnp.sum(dg_hgrn, axis=0).reshape(HGRN_DIM), "norm_mlp": dg_mlp[0]}
    return dx_in, on_grads(l, "mix", (d_win, d_wo)), small_grads


def _local_step(xs, target, small, get_weights, on_grads):
    consts = _rope_tables() + _hgrn_consts() + (_head_sum_matrix(),)
    stream = xs
    saved = []
    for l in range(DEPTH):
        w, after = get_weights(l, stream)
        stream, s = _layer_forward(l, stream, small, w, consts, after=after)
        saved.append(s)
    dx, dg_final, loss = _loss_head(stream, small["norm_final"][None, :], target)
    small_grads = [None] * DEPTH
    after = None
    for l in reversed(range(DEPTH)):
        dx, after, small_grads[l] = _layer_backward(l, dx, saved[l], small, consts, on_grads, after=after)
    return loss, dx, dg_final[0], small_grads


def _pack_small(norm_mix, attn_out_gain, lb, hgrn_out_gain, norm_mlp, norm_final, last_row):
    rows = [norm_mix, attn_out_gain.reshape(1, D_MODEL), lb.reshape(1, D_MODEL),
            jnp.pad(hgrn_out_gain.reshape(1, DEPTH * HGRN_DIM), ((0, 0), (0, D_MODEL - DEPTH * HGRN_DIM))),
            norm_mlp, norm_final.reshape(1, D_MODEL), last_row.reshape(1, D_MODEL)]
    pack = jnp.concatenate(rows, axis=0)
    return jnp.pad(pack, ((0, PACK_ROWS - pack.shape[0]), (0, 0)))


def _unpack_small(pack):
    return (pack[0:2], pack[2].reshape(DEPTH, ATTN_WIDTH), pack[3].reshape(DEPTH, HGRN_WIDTH),
            pack[4, :DEPTH * HGRN_DIM].reshape(DEPTH, HGRN_DIM), pack[5:7], pack[7], pack[8])


def kernel(x, norm_mix, w_in, attn_out_gain, hgrn_lb_logits, hgrn_out_gain, w_out, norm_mlp, w_up, w_down, norm_final, loss_target, m_norm_mix, m_w_in, m_attn_out_gain, m_hgrn_lb_logits, m_hgrn_out_gain, m_w_out, m_norm_mlp, m_w_up, m_w_down, m_norm_final, v_norm_mix, v_w_in, v_attn_out_gain, v_hgrn_lb_logits, v_hgrn_out_gain, v_w_out, v_norm_mlp, v_w_up, v_w_down, v_norm_final):
    core = lax.axis_index("c").astype(jnp.int32).reshape(1)
    lower, lower_vjp = jax.vjp(_lower_bounds, hgrn_lb_logits)
    small = {"norm_mix": norm_mix, "attn_out_gain": attn_out_gain, "lower": lower,
             "hgrn_out_gain": hgrn_out_gain, "norm_mlp": norm_mlp, "norm_final": norm_final}
    big_w = (w_in, w_out, w_up, w_down)

    me = 2 * lax.axis_index("x") + lax.axis_index("y")
    shards = [[w[l].astype(BF16) for w in big_w] for l in range(DEPTH)]
    in_flight = {}

    def start_gather(name, some, after):
        lands = [_own_slot(s, me) for s in some]
        sems, passed, token = _exchange_start(name, some, lands, False, after)
        return (sems, passed), token

    def get_weights(l, stream):
        if l == 0:
            (win,) = _gather_weights("gather_w_in0", shards[0][:1])
            in_flight["rest0"], token = start_gather("gather_start0", shards[0][1:], win)
            in_flight["weights1"], token = start_gather("gather_start1", shards[1], token)
            return (win, lambda after: _exchange_wait("gather_wait0", *in_flight.pop("rest0"), False, after)), token
        weights = _exchange_wait("gather_wait1", *in_flight.pop("weights1"), False, stream)
        return (weights[0], lambda after: weights[1:]), None

    reduced = {}

    def start_exchange(name, grads):
        received = _exchange_halves(f"halves_{name}", grads)
        halves = [_add_own_half(f"add_{name}_{i}", g, r, core) for i, (g, r) in enumerate(zip(grads, received))]
        lands = [_own_slot(lax.dynamic_index_in_dim(h, me, 0, keepdims=False), me) for h in halves]
        sems, passed, token = _exchange_start(f"start_{name}", halves, lands, True, halves[0])
        in_flight[name] = (sems, passed)
        return token

    def finish_exchange(name, after):
        landed = _exchange_wait(f"wait_{name}", *in_flight.pop(name), True, after)
        return [_sum_chips(f"sum_{name}_{i}", p) for i, p in enumerate(landed)]

    def on_grads(l, group, grads):
        if (l, group) == (1, "mlp"):
            return start_exchange("mlp1", grads)
        if (l, group) == (1, "mix"):
            return start_exchange("mix1", grads)
        if (l, group) == (0, "mlp"):
            reduced[(1, "mlp")] = finish_exchange("mlp1", grads[0])
            reduced[(1, "mix")] = finish_exchange("mix1", grads[0])
            return start_exchange("mlp0", grads)
        received = _exchange_halves("halves_mix0", grads)
        halves = [_add_own_half(f"add_mix0_{i}", g, r, core) for i, (g, r) in enumerate(zip(grads, received))]
        from_chips = _exchange_chips("exchange_chips_mix0", halves)
        reduced[(0, "mix")] = [_sum_chips(f"sum_mix0_{i}", p) for i, p in enumerate(from_chips)]
        reduced[(0, "mlp")] = finish_exchange("mlp0", from_chips[0])
        return None

    loss, dx, dg_final, sg = _local_step(x[0], loss_target[0], small, get_weights, on_grads)

    stack2 = lambda key: jnp.stack([sg[l][key] for l in range(DEPTH)])
    pack = _pack_small(stack2("norm_mix"), stack2("attn_out_gain"), stack2("lower"), stack2("hgrn_out_gain"),
                       stack2("norm_mlp"), dg_final, jnp.broadcast_to(loss[0, 0], (D_MODEL,)))
    g_mix, g_attn, g_lower, g_hgrn, g_mlp, g_final, loss_row = _unpack_small(_all_reduce_small(pack))
    (g_logits,) = lower_vjp(g_lower)

    zeros_row = jnp.zeros((D_MODEL,), F32)
    small_w = (norm_mix, attn_out_gain, hgrn_lb_logits, hgrn_out_gain, norm_mlp, norm_final)
    small_m = (m_norm_mix, m_attn_out_gain, m_hgrn_lb_logits, m_hgrn_out_gain, m_norm_mlp, m_norm_final)
    small_v = (v_norm_mix, v_attn_out_gain, v_hgrn_lb_logits, v_hgrn_out_gain, v_norm_mlp, v_norm_final)
    small_g = (g_mix, g_attn, g_logits, g_hgrn, g_mlp, g_final)
    packs = [_pack_small(*t, zeros_row) for t in (small_w, small_g, small_m, small_v)]
    small_delta, small_new_m, small_new_v = [_unpack_small(p)[:6] for p in _adamw("adamw_small", *packs)]

    big_g = _share_halves([[reduced[(l, group)][i] for l in range(DEPTH)]
                           for group, i in (("mix", 0), ("mix", 1), ("mlp", 0), ("mlp", 1))])

    big_m = (m_w_in, m_w_out, m_w_up, m_w_down)
    big_v = (v_w_in, v_w_out, v_w_up, v_w_down)
    big_delta, big_new_m, big_new_v = [], [], []
    for i, name in enumerate(("w_in", "w_out", "w_up", "w_down")):
        shape = big_w[i].shape
        flat = lambda arr: arr.reshape(shape[0] * shape[1], shape[2])
        d, m2, v2 = _adamw(f"adamw_{name}", flat(big_w[i]), flat(big_g[i]), flat(big_m[i]), flat(big_v[i]))
        big_delta.append(d.reshape(shape))
        big_new_m.append(m2.reshape(shape))
        big_new_v.append(v2.reshape(shape))

    def ordered(small6, big4):
        mix, attn, lbl, hg, mlp, fin = small6
        return (mix, big4[0], attn, lbl, hg, big4[1], mlp, big4[2], big4[3], fin)

    return ((loss_row[0], dx[None]) + ordered(small_g, big_g) + ordered(small_delta, big_delta)
            + ordered(small_new_m, big_new_m) + ordered(small_new_v, big_new_v))
```

```python
import functools
import math

import numpy as np
import jax
import jax.numpy as jnp
from jax import lax
from jax.experimental import pallas as pl
from jax.experimental.pallas import tpu as pltpu

F32 = jnp.float32
BF16 = jnp.bfloat16
MESH = pl.DeviceIdType.MESH

SEQ = 4096
D_MODEL = 1024
DEPTH = 2
ATTN_WIDTH = 512
HEAD_DIM = 64
HGRN_HEADS = 4
HGRN_DIM = 128
HGRN_WIDTH = 512
IN_W = 3584
MLP_HIDDEN = 4096
N_CHIPS = 4
N_DEV = 8
SHARD_IN = IN_W // N_CHIPS
SHARD_OUT = D_MODEL // N_CHIPS
SHARD_MLP = MLP_HIDDEN // N_CHIPS
DILATIONS = (1, 4, 16)
SPAN = 128
ROPE_THETA = 10000.0
NORM_EPS = 1e-6
MASK_VALUE = -1e30
CHUNK = 128
ROW_TILE = 512
MM_TILE = 1024
VMEM_LIMIT = 52 * 1024 * 1024

ADAM_LR = 0.001
ADAM_B1 = 0.9
ADAM_B2 = 0.999
ADAM_EPS = 1e-08
ADAM_WD = 0.01
ADAM_STEP = 10

PACK_ROWS = 16


def _params(n_axes):
    return pltpu.CompilerParams(dimension_semantics=("arbitrary",) * n_axes,
                                vmem_limit_bytes=VMEM_LIMIT)


def _dot(a, b):
    return jnp.dot(a.astype(BF16), b.astype(BF16), preferred_element_type=F32)


def _dot_nt(a, b):
    return lax.dot_general(a.astype(BF16), b.astype(BF16), (((1,), (1,)), ((), ())),
                           preferred_element_type=F32)


def _sigmoid(x):
    return 1.0 / (1.0 + jnp.exp(-x))


def _matmul(name, a, b, a_spec, b_spec, out_shape, out_dtype, out_spec, grid, acc_shape,
            nt=False, extra=None, extra_spec=None, epilogue="none", after=None, relu_outs=None):
    nk = grid[2]
    n_out = 1 if relu_outs is None else 3

    def body(*refs):
        a_ref, b_ref = refs[:2]
        e_ref = None if extra is None else refs[2]
        o_ref = refs[-1 - n_out]
        acc = refs[-1]
        kk = pl.program_id(2)
        part = _dot_nt(a_ref[...], b_ref[...]) if nt else _dot(a_ref[...], b_ref[...])

        @pl.when(kk == 0)
        def _():
            acc[...] = part

        @pl.when(kk > 0)
        def _():
            acc[...] += part

        @pl.when(kk == nk - 1)
        def _():
            r = acc[...]
            if epilogue == "add":
                r = r + e_ref[...]
            elif epilogue == "relu2_grad":
                r = r * (2.0 * e_ref[...].astype(F32))
            elif epilogue == "relu2":
                s = jnp.maximum(r, 0.0)
                r = s * s
                refs[-3][...] = s.astype(out_dtype)
                refs[-2][...] = r.T.astype(out_dtype)
            o_ref[...] = r.astype(o_ref.dtype)

    in_specs = [a_spec, b_spec] + ([] if extra is None else [extra_spec])
    args = (a, b) + (() if extra is None else (extra,))
    if after is not None:
        in_specs.append(pl.BlockSpec(memory_space=pl.ANY))
        args += (after,)
    out_specs, out_shapes = out_spec, jax.ShapeDtypeStruct(out_shape, out_dtype)
    if relu_outs is not None:
        out_specs = [out_spec] + [spec for _, spec in relu_outs]
        out_shapes = [out_shapes] + [jax.ShapeDtypeStruct(shape, out_dtype) for shape, _ in relu_outs]
    return pl.pallas_call(
        body, name=name, grid=grid, in_specs=in_specs, out_specs=out_specs, out_shape=out_shapes,
        scratch_shapes=[pltpu.VMEM(acc_shape, F32)],
        compiler_params=_params(3),
    )(*args)


def _spec(shape, index_map):
    return pl.BlockSpec(shape, index_map)


def _rms_fwd(name, x, gain, after=None):
    s, d = x.shape
    t = ROW_TILE

    def body(x_ref, g_ref, *rest):
        h_ref, ht_ref = rest[-2:]
        xv = x_ref[...]
        r = lax.rsqrt(jnp.mean(xv * xv, axis=1, keepdims=True) + NORM_EPS)
        h = xv * r * g_ref[...]
        h_ref[...] = h.astype(BF16)
        ht_ref[...] = h.T.astype(BF16)

    in_specs = [_spec((t, d), lambda i: (i, 0)), _spec((1, d), lambda i: (0, 0))]
    args = (x, gain)
    if after is not None:
        in_specs.append(pl.BlockSpec(memory_space=pl.ANY))
        args += (after,)
    return pl.pallas_call(
        body, name=name, grid=(s // t,), in_specs=in_specs,
        out_specs=[_spec((t, d), lambda i: (i, 0)), _spec((d, t), lambda i: (0, i))],
        out_shape=[jax.ShapeDtypeStruct((s, d), BF16), jax.ShapeDtypeStruct((d, s), BF16)],
        compiler_params=_params(1),
    )(*args)


def _rms_bwd(name, dh, x, gain, dres):
    s, d = x.shape
    t = ROW_TILE

    def body(dh_ref, x_ref, g_ref, dres_ref, dx_ref, dxb_ref, dg_ref):
        @pl.when(pl.program_id(0) == 0)
        def _():
            dg_ref[...] = jnp.zeros_like(dg_ref)

        xv = x_ref[...]
        dhv = dh_ref[...]
        r = lax.rsqrt(jnp.mean(xv * xv, axis=1, keepdims=True) + NORM_EPS)
        xhat = xv * r
        dhg = dhv * g_ref[...]
        proj = jnp.mean(dhg * xhat, axis=1, keepdims=True)
        dx = dres_ref[...] + r * (dhg - xhat * proj)
        dx_ref[...] = dx
        dxb_ref[...] = dx.astype(BF16)
        dg_ref[...] += jnp.sum(dhv * xhat, axis=0, keepdims=True)

    return pl.pallas_call(
        body, name=name, grid=(s // t,),
        in_specs=[_spec((t, d), lambda i: (i, 0)), _spec((t, d), lambda i: (i, 0)),
                  _spec((1, d), lambda i: (0, 0)), _spec((t, d), lambda i: (i, 0))],
        out_specs=[_spec((t, d), lambda i: (i, 0)), _spec((t, d), lambda i: (i, 0)),
                   _spec((1, d), lambda i: (0, 0))],
        out_shape=[jax.ShapeDtypeStruct((s, d), F32), jax.ShapeDtypeStruct((s, d), BF16),
                   jax.ShapeDtypeStruct((1, d), F32)],
        compiler_params=_params(1),
    )(dh, x, gain, dres)


def _loss_head(x, gain, target):
    s, d = x.shape
    t = ROW_TILE
    n_steps = s // t

    def body(x_ref, g_ref, t_ref, dx_ref, dxb_ref, dg_ref, loss_ref, acc):
        i = pl.program_id(0)

        @pl.when(i == 0)
        def _():
            dg_ref[...] = jnp.zeros_like(dg_ref)
            acc[...] = jnp.zeros_like(acc)

        xv = x_ref[...]
        g = g_ref[...]
        r = lax.rsqrt(jnp.mean(xv * xv, axis=1, keepdims=True) + NORM_EPS)
        xhat = xv * r
        err = xhat * g - t_ref[...]
        acc[...] += jnp.sum(err * err, axis=0, keepdims=True)
        dy = err * (1.0 / d)
        dyg = dy * g
        proj = jnp.mean(dyg * xhat, axis=1, keepdims=True)
        dx = r * (dyg - xhat * proj)
        dx_ref[...] = dx
        dxb_ref[...] = dx.astype(BF16)
        dg_ref[...] += jnp.sum(dy * xhat, axis=0, keepdims=True)

        @pl.when(i == n_steps - 1)
        def _():
            total = jnp.sum(acc[...], axis=1, keepdims=True) * (0.5 / d)
            loss_ref[...] = jnp.broadcast_to(total, loss_ref.shape)

    return pl.pallas_call(
        body, name="loss_head", grid=(n_steps,),
        in_specs=[_spec((t, d), lambda i: (i, 0)), _spec((1, d), lambda i: (0, 0)),
                  _spec((t, d), lambda i: (i, 0))],
        out_specs=[_spec((t, d), lambda i: (i, 0)), _spec((t, d), lambda i: (i, 0)),
                   _spec((1, d), lambda i: (0, 0)), _spec((1, 128), lambda i: (0, 0))],
        out_shape=[jax.ShapeDtypeStruct((s, d), F32), jax.ShapeDtypeStruct((s, d), BF16),
                   jax.ShapeDtypeStruct((1, d), F32), jax.ShapeDtypeStruct((1, 128), F32)],
        scratch_shapes=[pltpu.VMEM((1, d), F32)],
        compiler_params=_params(1),
    )(x, gain, target)


def _rope_tables():
    half = HEAD_DIM // 2
    inv_freq = ROPE_THETA ** (-jnp.arange(half, dtype=F32) / half)
    ang = jnp.arange(SEQ, dtype=jnp.int32).astype(F32)[:, None] * inv_freq[None, :]
    cos, sin = jnp.cos(ang), jnp.sin(ang)
    cos_t = jnp.concatenate([cos, cos, cos, cos], axis=1)
    sin_t = jnp.concatenate([-sin, sin, -sin, sin], axis=1)
    return cos_t, sin_t


def _swap_halves(x):
    lane = lax.broadcasted_iota(jnp.int32, x.shape, 1)
    first = (lane % HEAD_DIM) < (HEAD_DIM // 2)
    return jnp.where(first, pltpu.roll(x, 128 - HEAD_DIM // 2, 1), pltpu.roll(x, HEAD_DIM // 2, 1))


def _permuted_specs(t, width):
    specs = [_spec((t, width), lambda i: (i, 0))]
    for d in DILATIONS[1:]:
        specs.append(_spec((d, t // d, width), lambda i: (0, i, 0)))
    return specs


def _permuted_shapes(width, dtype):
    shapes = [jax.ShapeDtypeStruct((SEQ, width), dtype)]
    for d in DILATIONS[1:]:
        shapes.append(jax.ShapeDtypeStruct((d, SEQ // d, width), dtype))
    return shapes


def _attn_prep(name, proj, cos_t, sin_t):
    t = ROW_TILE
    w = ATTN_WIDTH

    def body(q_ref, k_ref, v_ref, cos_ref, sin_ref, *rest):
        outs, scr = rest[:9], rest[9]
        cosv, sinv = cos_ref[...], sin_ref[...]
        for a, (src, roped, scale) in enumerate(((q_ref, True, HEAD_DIM ** -0.5),
                                                 (k_ref, True, 1.0), (v_ref, False, 1.0))):
            o1, o4, o16 = outs[3 * a:3 * a + 3]
            for cb in range(w // 128):
                cols = slice(cb * 128, (cb + 1) * 128)
                val = src[:, cols]
                if roped:
                    val = (val * cosv + _swap_halves(val) * sinv) * scale
                scr[...] = val
                o1[:, cols] = val.astype(BF16)
                for o_ref, d in ((o4, 4), (o16, 16)):
                    for r in range(d):
                        o_ref[r, :, cols] = scr[pl.ds(r, t // d, stride=d), :].astype(BF16)

    out_specs = _permuted_specs(t, w) * 3
    out_shape = _permuted_shapes(w, BF16) * 3
    outs = pl.pallas_call(
        body, name=name, grid=(SEQ // t,),
        in_specs=[_spec((t, w), lambda i: (i, 0)), _spec((t, w), lambda i: (i, 1)),
                  _spec((t, w), lambda i: (i, 2)),
                  _spec((t, 128), lambda i: (i, 0)), _spec((t, 128), lambda i: (i, 0))],
        out_specs=out_specs, out_shape=out_shape,
        scratch_shapes=[pltpu.VMEM((t, 128), F32)],
        compiler_params=_params(1),
    )(proj, proj, proj, cos_t, sin_t)
    q, k, v = outs[0:3], outs[3:6], outs[6:9]
    flat = lambda arr: arr.reshape(SEQ, w)
    return [(flat(q[p]), flat(k[p]), flat(v[p])) for p in range(3)]


def _band_masks():
    row = lax.broadcasted_iota(jnp.int32, (2 * SPAN, 2 * SPAN), 0) % SPAN
    col = lax.broadcasted_iota(jnp.int32, (2 * SPAN, 2 * SPAN), 1)
    is_prev = col < SPAN
    band = (is_prev & (col >= row)) | (~is_prev & (col - SPAN <= row))
    head0 = lax.broadcasted_iota(jnp.int32, (SPAN, 128), 1) < HEAD_DIM
    return band, is_prev, head0


def _stack_heads(x, head0):
    zero = jnp.zeros_like(x)
    return jnp.concatenate([jnp.where(head0, x, zero), jnp.where(head0, zero, x)], axis=0)


def _attn_fwd(name, q, k, v, seg_blocks):
    n_blocks = SEQ // SPAN

    def body(q_ref, k_ref, v_ref, o_ref, lse_ref):
        band, is_prev, head0 = _band_masks()

        def step(b, carry):
            cur = pl.ds(pl.multiple_of(b * SPAN, SPAN), SPAN)
            prev = pl.ds(pl.multiple_of(jnp.maximum(b - 1, 0) * SPAN, SPAN), SPAN)
            qs = _stack_heads(q_ref[cur, :], head0)
            kcat = jnp.concatenate([k_ref[prev, :], k_ref[cur, :]], axis=0)
            vcat = jnp.concatenate([v_ref[prev, :], v_ref[cur, :]], axis=0)
            ok = band & (((b % seg_blocks) != 0) | ~is_prev)
            s = jnp.where(ok, _dot_nt(qs, kcat), MASK_VALUE)
            m = jnp.max(s, axis=1, keepdims=True)
            p = jnp.where(ok, jnp.exp(s - m), 0.0)
            l = jnp.sum(p, axis=1, keepdims=True)
            pv = _dot(p, vcat) / l
            lse = m + jnp.log(l)
            o_ref[cur, :] = jnp.where(head0, pv[:SPAN], pv[SPAN:])
            lse_ref[cur, :] = jnp.where(head0, lse[:SPAN], lse[SPAN:])
            return carry

        lax.fori_loop(0, n_blocks, step, 0, unroll=2)

    col = _spec((SEQ, 128), lambda j: (0, j))
    return pl.pallas_call(
        body, name=name, grid=(ATTN_WIDTH // 128,),
        in_specs=[col, col, col], out_specs=[col, col],
        out_shape=[jax.ShapeDtypeStruct((SEQ, ATTN_WIDTH), F32)] * 2,
        compiler_params=_params(1),
    )(q, k, v)


def _unpermute(dst, src_ref, d, cols):
    n = dst.shape[0] // d
    for r in range(d):
        dst[pl.ds(r, n, stride=d), :] = src_ref[r, :, cols]


def _attn_merge(name, outs, lses, gain):
    t = ROW_TILE
    w = ATTN_WIDTH

    def body(o1, o4, o16, l1, l4, l16, g_ref, an_ref, ant_ref, attn_ref, lse_ref, so4, so16, sl4, sl16):
        for cb in range(w // 128):
            cols = slice(cb * 128, (cb + 1) * 128)
            _unpermute(so4, o4, 4, cols)
            _unpermute(so16, o16, 16, cols)
            _unpermute(sl4, l4, 4, cols)
            _unpermute(sl16, l16, 16, cols)
            la, lb, lc = l1[:, cols], sl4[...], sl16[...]
            m = jnp.maximum(jnp.maximum(la, lb), lc)
            ea, eb, ec = jnp.exp(la - m), jnp.exp(lb - m), jnp.exp(lc - m)
            tot = ea + eb + ec
            attn_ref[:, cols] = (ea * o1[:, cols] + eb * so4[...] + ec * so16[...]) / tot
            lse_ref[:, cols] = m + jnp.log(tot)
        attn = attn_ref[...]
        r = lax.rsqrt(jnp.mean(attn * attn, axis=1, keepdims=True) + NORM_EPS)
        an = attn * r * g_ref[...]
        an_ref[...] = an.astype(BF16)
        ant_ref[...] = an.T.astype(BF16)

    views = lambda arrs: [arrs[0], arrs[1].reshape(4, SEQ // 4, w), arrs[2].reshape(16, SEQ // 16, w)]
    row = _spec((t, w), lambda i: (i, 0))
    return pl.pallas_call(
        body, name=name, grid=(SEQ // t,),
        in_specs=_permuted_specs(t, w) * 2 + [_spec((1, w), lambda i: (0, 0))],
        out_specs=[row, _spec((w, t), lambda i: (0, i)), row, row],
        out_shape=[jax.ShapeDtypeStruct((SEQ, w), BF16), jax.ShapeDtypeStruct((w, SEQ), BF16),
                   jax.ShapeDtypeStruct((SEQ, w), F32), jax.ShapeDtypeStruct((SEQ, w), F32)],
        scratch_shapes=[pltpu.VMEM((t, 128), F32)] * 4,
        compiler_params=_params(1),
    )(*views(outs), *views(lses), gain)


def _head_sum_matrix():
    i = np.arange(ATTN_WIDTH)
    return jnp.asarray((i[:, None] // HEAD_DIM) == (i[None, :] // HEAD_DIM), dtype=F32)


def _attn_bwd_prep(name, d_an, attn, lse, gain, head_sum):
    t = ROW_TILE
    w = ATTN_WIDTH

    def body(dan_ref, attn_ref, lse_ref, g_ref, hs_ref, *rest):
        (do1, do4, do16, dl1, dl4, dl16, ls4, ls16, dg_ref), (sdo, sdl, sls) = rest[:9], rest[9:]

        @pl.when(pl.program_id(0) == 0)
        def _():
            dg_ref[...] = jnp.zeros_like(dg_ref)

        attn = attn_ref[...]
        dan = dan_ref[...]
        r = lax.rsqrt(jnp.mean(attn * attn, axis=1, keepdims=True) + NORM_EPS)
        xhat = attn * r
        dg_ref[...] += jnp.sum(dan * xhat, axis=0, keepdims=True)
        dang = dan * g_ref[...]
        d_o = r * (dang - xhat * jnp.mean(dang * xhat, axis=1, keepdims=True))
        delta = jnp.dot(d_o * attn, hs_ref[...], preferred_element_type=F32,
                        precision=lax.Precision.HIGHEST)
        do1[...] = d_o.astype(BF16)
        dl1[...] = delta
        for cb in range(w // 128):
            cols = slice(cb * 128, (cb + 1) * 128)
            sdo[...] = d_o[:, cols]
            sdl[...] = delta[:, cols]
            sls[...] = lse_ref[:, cols]
            for d, o_do, o_dl, o_ls in ((4, do4, dl4, ls4), (16, do16, dl16, ls16)):
                for rr in range(d):
                    rows = pl.ds(rr, t // d, stride=d)
                    o_do[rr, :, cols] = sdo[rows, :].astype(BF16)
                    o_dl[rr, :, cols] = sdl[rows, :]
                    o_ls[rr, :, cols] = sls[rows, :]

    row = _spec((t, w), lambda i: (i, 0))
    perm = _permuted_specs(t, w)
    outs = pl.pallas_call(
        body, name=name, grid=(SEQ // t,),
        in_specs=[row, row, row, _spec((1, w), lambda i: (0, 0)), _spec((w, w), lambda i: (0, 0))],
        out_specs=perm + perm + perm[1:] + [_spec((1, w), lambda i: (0, 0))],
        out_shape=(_permuted_shapes(w, BF16) + _permuted_shapes(w, F32) + _permuted_shapes(w, F32)[1:]
                   + [jax.ShapeDtypeStruct((1, w), F32)]),
        scratch_shapes=[pltpu.VMEM((t, 128), F32)] * 3,
        compiler_params=_params(1),
    )(d_an, attn, lse, gain, head_sum)
    flat = lambda arr: arr.reshape(SEQ, w)
    d_out = [flat(a) for a in outs[0:3]]
    delta = [flat(a) for a in outs[3:6]]
    lses = [lse, flat(outs[6]), flat(outs[7])]
    return d_out, delta, lses, outs[8]


def _attn_bwd(name, q, k, v, d_out, delta, lse, seg_blocks):
    n_blocks = SEQ // SPAN

    def body(q_ref, k_ref, v_ref, do_ref, dl_ref, lse_ref, dq_ref, dk_ref, dv_ref):
        band, is_prev, head0 = _band_masks()
        dk_ref[...] = jnp.zeros_like(dk_ref)
        dv_ref[...] = jnp.zeros_like(dv_ref)

        def per_head(x):
            return jnp.concatenate([x[:, 0:1], x[:, HEAD_DIM:HEAD_DIM + 1]], axis=0)

        def step(b, carry):
            cur = pl.ds(pl.multiple_of(b * SPAN, SPAN), SPAN)
            prev = pl.ds(pl.multiple_of(jnp.maximum(b - 1, 0) * SPAN, SPAN), SPAN)
            qs = _stack_heads(q_ref[cur, :], head0)
            dos = _stack_heads(do_ref[cur, :], head0)
            kcat = jnp.concatenate([k_ref[prev, :], k_ref[cur, :]], axis=0)
            vcat = jnp.concatenate([v_ref[prev, :], v_ref[cur, :]], axis=0)
            ok = band & (((b % seg_blocks) != 0) | ~is_prev)
            p = jnp.where(ok, jnp.exp(_dot_nt(qs, kcat) - per_head(lse_ref[cur, :])), 0.0)
            ds = p * (_dot_nt(dos, vcat) - per_head(dl_ref[cur, :]))
            dq = _dot(ds, kcat)
            dq_ref[cur, :] = jnp.where(head0, dq[:SPAN], dq[SPAN:])
            dk = _dot(ds.T, qs)
            dv = _dot(p.T, dos)
            dk_ref[prev, :] += dk[:SPAN]
            dv_ref[prev, :] += dv[:SPAN]
            dk_ref[cur, :] += dk[SPAN:]
            dv_ref[cur, :] += dv[SPAN:]
            return carry

        lax.fori_loop(0, n_blocks, step, 0, unroll=2)

    col = _spec((SEQ, 128), lambda j: (0, j))
    return pl.pallas_call(
        body, name=name, grid=(ATTN_WIDTH // 128,),
        in_specs=[col] * 6, out_specs=[col] * 3,
        out_shape=[jax.ShapeDtypeStruct((SEQ, ATTN_WIDTH), F32)] * 3,
        compiler_params=_params(1),
    )(q, k, v, d_out, delta, lse)


def _attn_bwd_post(name, grads, cos_t, sin_t):
    t = ROW_TILE
    w = ATTN_WIDTH

    def body(*refs):
        ins, cos_ref, sin_ref, out_ref, s4, s16 = refs[:9], refs[9], refs[10], refs[11], refs[12], refs[13]
        cosv, sinv = cos_ref[...], sin_ref[...]
        for a in range(3):
            g1, g4, g16 = ins[a], ins[3 + a], ins[6 + a]
            for cb in range(w // 128):
                cols = slice(cb * 128, (cb + 1) * 128)
                _unpermute(s4, g4, 4, cols)
                _unpermute(s16, g16, 16, cols)
                val = g1[:, cols] + s4[...] + s16[...]
                if a < 2:
                    val = val * cosv + _swap_halves(val * sinv)
                if a == 0:
                    val = val * (HEAD_DIM ** -0.5)
                out_ref[:, a * w + cb * 128:a * w + (cb + 1) * 128] = val.astype(BF16)

    views = []
    for p, d in enumerate(DILATIONS):
        for a in range(3):
            views.append(grads[p][a] if d == 1 else grads[p][a].reshape(d, SEQ // d, w))
    perm = _permuted_specs(t, w)
    in_specs = [perm[0]] * 3 + [perm[1]] * 3 + [perm[2]] * 3
    return pl.pallas_call(
        body, name=name, grid=(SEQ // t,),
        in_specs=in_specs + [_spec((t, 128), lambda i: (i, 0))] * 2,
        out_specs=_spec((t, 3 * w), lambda i: (i, 0)),
        out_shape=jax.ShapeDtypeStruct((SEQ, 3 * w), BF16),
        scratch_shapes=[pltpu.VMEM((t, 128), F32)] * 2,
        compiler_params=_params(1),
    )(*views, cos_t, sin_t)


N_LEVELS = 7


def _hgrn_consts():
    c = CHUNK
    i = np.arange(c)[:, None]
    s = np.arange(c)[None, :]
    blocks = [s <= i]
    for lv in range(N_LEVELS):
        bs = c >> lv
        h = bs // 2
        m = (i // bs) * bs + h - 1
        second = (i % bs) >= h
        blocks.append((second & (s > m) & (s <= i)) | (~second & (s > i) & (s <= m)))
    blocks.append(s > i)
    stack = np.concatenate(blocks, axis=0).astype(np.float32)
    return jnp.asarray(stack, dtype=BF16), jnp.asarray(stack.T, dtype=BF16)


def _exact_dot(m01, x):
    hi = x.astype(BF16)
    r1 = x - hi.astype(F32)
    mid = r1.astype(BF16)
    lo = (r1 - mid.astype(F32)).astype(BF16)
    n = x.shape[1]
    full = jnp.dot(m01, jnp.concatenate([hi, mid, lo], axis=1), preferred_element_type=F32)
    return (full[:, :n] + full[:, n:2 * n]) + full[:, 2 * n:]


def _hgrn_gates(qh, z, lb):
    sq = _sigmoid(qh)
    q = qh * sq * (HGRN_DIM ** -0.5)
    sig = _sigmoid(z)
    sigm = _sigmoid(-z)
    f = lb + (1.0 - lb) * sig
    k = (1.0 - lb) * sigm
    return q, k, f, sq, sig, sigm


def _level_masks(lv):
    row = lax.broadcasted_iota(jnp.int32, (CHUNK, CHUNK), 0)
    col = lax.broadcasted_iota(jnp.int32, (CHUNK, CHUNK), 1)
    shift = N_LEVELS - lv
    second = (row & (CHUNK >> (lv + 1))) != 0
    same = (row >> shift) == (col >> shift)
    return second, same


def _hgrn_fwd(name, proj, lb, gain, stack):
    t = ROW_TILE
    per = t // CHUNK
    n_rb = SEQ // t
    n_chunks = SEQ // CHUNK
    col0 = 3 * ATTN_WIDTH // 128

    def body(q_ref, f_ref, i_ref, g_ref, lb_ref, gain_ref, stack_ref,
             rec_ref, rect_ref, o_ref, st_out, a_out, st):
        @pl.when(pl.program_id(1) == 0)
        def _():
            st[...] = jnp.zeros_like(st)

        lbv = lb_ref[...]
        row = lax.broadcasted_iota(jnp.int32, (CHUNK, CHUNK), 0)
        col = lax.broadcasted_iota(jnp.int32, (CHUNK, CHUNK), 1)
        for c in range(per):
            rows = slice(c * CHUNK, (c + 1) * CHUNK)
            qh, z, v, gh = q_ref[rows, :], f_ref[rows, :], i_ref[rows, :], g_ref[rows, :]
            q, k, f, _, _, _ = _hgrn_gates(qh, z, lbv)
            dec = _exact_dot(stack_ref[...], jnp.log(f))
            g = dec[0:CHUNK]
            to_end = dec[(N_LEVELS + 1) * CHUNK:(N_LEVELS + 2) * CHUNK]
            a = jnp.where(row == col, jnp.sum(q * k, axis=1, keepdims=True), 0.0)
            for lv in range(N_LEVELS):
                e = jnp.exp(dec[(lv + 1) * CHUNK:(lv + 2) * CHUNK])
                second, same = _level_masks(lv)
                qt = jnp.where(second, q * e, 0.0)
                kt = jnp.where(second, 0.0, k * e)
                a = a + jnp.where(same, _dot_nt(qt, kt), 0.0)
            st_prev = st[...]
            st_out[c] = st_prev
            a_out[c] = a
            o = _dot(a, v) + _dot_nt(q * jnp.exp(g), st_prev)
            k_end = k * jnp.exp(to_end)
            st[...] = st_prev * jnp.exp(g[CHUNK - 1:CHUNK, :]) + _dot(v.T, k_end)
            o_ref[rows, :] = o
            r = lax.rsqrt(jnp.mean(o * o, axis=1, keepdims=True) + NORM_EPS)
            rec = o * r * gain_ref[...] * (gh * _sigmoid(gh))
            rec_ref[rows, :] = rec.astype(BF16)
            rect_ref[:, rows] = rec.T.astype(BF16)

    def col_spec(tt):
        return _spec((t, HGRN_DIM), lambda h, rb: (rb, col0 + HGRN_HEADS * tt + h))

    chunk_spec = _spec((None, per, CHUNK, CHUNK), lambda h, rb: (h, rb, 0, 0))
    return pl.pallas_call(
        body, name=name, grid=(HGRN_HEADS, n_rb),
        in_specs=[col_spec(0), col_spec(1), col_spec(2), col_spec(3),
                  _spec((None, 1, HGRN_DIM), lambda h, rb: (h, 0, 0)),
                  _spec((1, HGRN_DIM), lambda h, rb: (0, 0)),
                  _spec(stack.shape, lambda h, rb: (0, 0))],
        out_specs=[_spec((t, HGRN_DIM), lambda h, rb: (rb, h)),
                   _spec((HGRN_DIM, t), lambda h, rb: (h, rb)),
                   _spec((t, HGRN_DIM), lambda h, rb: (rb, h)),
                   chunk_spec, chunk_spec],
        out_shape=[jax.ShapeDtypeStruct((SEQ, HGRN_WIDTH), BF16),
                   jax.ShapeDtypeStruct((HGRN_WIDTH, SEQ), BF16),
                   jax.ShapeDtypeStruct((SEQ, HGRN_WIDTH), F32),
                   jax.ShapeDtypeStruct((HGRN_HEADS, n_chunks, CHUNK, CHUNK), F32),
                   jax.ShapeDtypeStruct((HGRN_HEADS, n_chunks, CHUNK, CHUNK), F32)],
        scratch_shapes=[pltpu.VMEM((CHUNK, CHUNK), F32)],
        compiler_params=_params(2),
    )(proj, proj, proj, proj, lb, gain, stack)


def _hgrn_bwd(name, proj, d_rec, o_pre, states, scores, lb, gain, stack, stack_t):
    t = ROW_TILE
    per = t // CHUNK
    n_rb = SEQ // t
    col0 = 3 * ATTN_WIDTH // 128

    def body(q_ref, f_ref, i_ref, g_ref, drec_ref, o_ref, st_ref, a_ref, lb_ref, gain_ref,
             stack_ref, stack_t_ref, dq_ref, df_ref, di_ref, dg_ref, dlb_ref, dgain_ref, dst):
        @pl.when(pl.program_id(1) == 0)
        def _():
            dst[...] = jnp.zeros_like(dst)
            dlb_ref[...] = jnp.zeros_like(dlb_ref)
            dgain_ref[...] = jnp.zeros_like(dgain_ref)

        lbv = lb_ref[...]
        gain_v = gain_ref[...]
        row = lax.broadcasted_iota(jnp.int32, (CHUNK, CHUNK), 0)
        col = lax.broadcasted_iota(jnp.int32, (CHUNK, CHUNK), 1)
        for c in reversed(range(per)):
            rows = slice(c * CHUNK, (c + 1) * CHUNK)
            qh, z, v, gh = q_ref[rows, :], f_ref[rows, :], i_ref[rows, :], g_ref[rows, :]
            q, k, f, sq, sig, sigm = _hgrn_gates(qh, z, lbv)
            dec = _exact_dot(stack_ref[...], jnp.log(f))
            g = dec[0:CHUNK]
            to_end = dec[(N_LEVELS + 1) * CHUNK:(N_LEVELS + 2) * CHUNK]
            e_g = jnp.exp(g)
            e_end = jnp.exp(to_end)
            e_last = jnp.exp(g[CHUNK - 1:CHUNK, :])
            q_in = q * e_g
            k_end = k * e_end
            st_prev = st_ref[c]
            a = a_ref[c]
            dst_new = dst[...]

            o = o_ref[rows, :]
            drec = drec_ref[rows, :]
            sg = _sigmoid(gh)
            r = lax.rsqrt(jnp.mean(o * o, axis=1, keepdims=True) + NORM_EPS)
            ohat = o * r
            d_gh = drec * (ohat * gain_v) * (sg * (1.0 + gh * (1.0 - sg)))
            d_on = drec * (gh * sg)
            dgain_ref[...] += jnp.sum(d_on * ohat, axis=0, keepdims=True)
            d_ohat = d_on * gain_v
            d_o = r * (d_ohat - ohat * jnp.mean(d_ohat * ohat, axis=1, keepdims=True))

            d_a = jnp.where(row >= col, _dot_nt(d_o, v), 0.0)
            d_at = jnp.where(col >= row, _dot_nt(v, d_o), 0.0)
            d_v = _dot(a.T, d_o) + _dot_nt(k_end, dst_new)
            d_q_in = _dot(d_o, st_prev)
            d_k_end = _dot(v, dst_new)
            d_q = d_q_in * e_g
            d_k = d_k_end * e_end
            diag = jnp.sum(d_o * v, axis=1, keepdims=True)
            d_q = d_q + diag * k
            d_k = d_k + diag * q
            d_dec = [q_in * d_q_in]
            for lv in range(N_LEVELS):
                e = jnp.exp(dec[(lv + 1) * CHUNK:(lv + 2) * CHUNK])
                second, same = _level_masks(lv)
                qt = jnp.where(second, q * e, 0.0)
                kt = jnp.where(second, 0.0, k * e)
                d_qt = _dot(jnp.where(same, d_a, 0.0), kt)
                d_kt = _dot(jnp.where(same, d_at, 0.0), qt)
                d_q = d_q + jnp.where(second, d_qt * e, 0.0)
                d_k = d_k + jnp.where(second, 0.0, d_kt * e)
                d_dec.append(jnp.where(second, qt * d_qt, kt * d_kt))
            d_dec.append(k_end * d_k_end)
            flux = jnp.sum(dst_new * st_prev, axis=0, keepdims=True) * e_last
            d_lf = _exact_dot(stack_t_ref[...], jnp.concatenate(d_dec, axis=0)) + flux
            dst[...] = dst_new * e_last + _dot(d_o.T, q_in)

            d_f = d_lf / f - d_k
            dlb_ref[...] += jnp.sum(d_f * sigm, axis=0, keepdims=True)
            dq_ref[rows, :] = (d_q * (HGRN_DIM ** -0.5) * (sq * (1.0 + qh * (1.0 - sq)))).astype(BF16)
            df_ref[rows, :] = (d_f * (1.0 - lbv) * sig * sigm).astype(BF16)
            di_ref[rows, :] = d_v.astype(BF16)
            dg_ref[rows, :] = d_gh.astype(BF16)

    last = n_rb - 1

    def col_spec(tt):
        return _spec((t, HGRN_DIM), lambda h, rb: (last - rb, col0 + HGRN_HEADS * tt + h))

    head_col = _spec((t, HGRN_DIM), lambda h, rb: (last - rb, h))
    chunk_spec = _spec((None, per, CHUNK, CHUNK), lambda h, rb: (h, last - rb, 0, 0))
    vec_spec = _spec((None, 1, HGRN_DIM), lambda h, rb: (h, 0, 0))
    outs = pl.pallas_call(
        body, name=name, grid=(HGRN_HEADS, n_rb),
        in_specs=[col_spec(0), col_spec(1), col_spec(2), col_spec(3), head_col, head_col,
                  chunk_spec, chunk_spec, vec_spec,
                  _spec((1, HGRN_DIM), lambda h, rb: (0, 0)),
                  _spec(stack.shape, lambda h, rb: (0, 0)), _spec(stack_t.shape, lambda h, rb: (0, 0))],
        out_specs=[head_col] * 4 + [vec_spec, vec_spec],
        out_shape=[jax.ShapeDtypeStruct((SEQ, HGRN_WIDTH), BF16)] * 4
                  + [jax.ShapeDtypeStruct((HGRN_HEADS, 1, HGRN_DIM), F32)] * 2,
        scratch_shapes=[pltpu.VMEM((CHUNK, CHUNK), F32)],
        compiler_params=_params(2),
    )(proj, proj, proj, proj, d_rec, o_pre, states, scores, lb, gain, stack, stack_t)
    return outs


ANY_SPEC = pl.BlockSpec(memory_space=pl.ANY)


def _my_place():
    return lax.axis_index("x"), lax.axis_index("y"), lax.axis_index("c")


def _other_chips(x, y):
    return [(1 - x, y), (x, 1 - y), (1 - x, 1 - y)]


def _remote(src, dst, send_sem, recv_sem, device):
    return pltpu.make_async_remote_copy(src_ref=src, dst_ref=dst, send_sem=send_sem, recv_sem=recv_sem,
                                        device_id=device, device_id_type=MESH)


def _staged_copies(srcs, dsts, stage, sems):
    loads = [pltpu.make_async_copy(srcs[i], stage[i], sems.at[i]) for i in range(len(srcs))]
    for cp in loads:
        cp.start()
    stores = []
    for i, cp in enumerate(loads):
        cp.wait()
        stores.append(pltpu.make_async_copy(stage[i], dsts[i], sems.at[i]))
        stores[-1].start()
    return stores


def _gather_weights(name, shards):
    n = len(shards)

    def body(*refs):
        ins, outs = refs[:n], refs[n:2 * n]
        ici_send, ici_recv, d2d_send, d2d_recv, local_sems = refs[2 * n:2 * n + 5]
        stage = refs[2 * n + 5:]
        x, y, c = _my_place()
        me = 2 * x + y
        chips = _other_chips(x, y)

        def half(i, which):
            h = ins[i].shape[0] // 2
            return pl.ds(which * h, h)

        sends = []
        for i in range(n):
            for j, (px, py) in enumerate(chips):
                sends.append(_remote(ins[i].at[half(i, c), :], outs[i].at[me, half(i, c), :],
                                     ici_send.at[3 * i + j], ici_recv.at[3 * i + j], (px, py, c)))
        for cp in sends:
            cp.start()
        local = _staged_copies(ins, [outs[i].at[me] for i in range(n)], stage, local_sems)
        for i in range(n):
            for j, (px, py) in enumerate(chips):
                landed = outs[i].at[2 * px + py, half(i, c), :]
                _remote(landed, landed, ici_send.at[3 * i + j], ici_recv.at[3 * i + j], (px, py, c)).wait_recv()
                forward = _remote(landed, landed, d2d_send.at[3 * i + j], d2d_recv.at[3 * i + j], (x, y, 1 - c))
                forward.start()
                sends.append(forward)
        for i in range(n):
            for j, (px, py) in enumerate(chips):
                other = outs[i].at[2 * px + py, half(i, 1 - c), :]
                _remote(other, other, d2d_send.at[3 * i + j], d2d_recv.at[3 * i + j], (x, y, 1 - c)).wait_recv()
        for cp in sends:
            cp.wait_send()
        for cp in local:
            cp.wait()

    return pl.pallas_call(
        body, name=name, in_specs=[ANY_SPEC] * n, out_specs=[ANY_SPEC] * n,
        out_shape=[jax.ShapeDtypeStruct((N_CHIPS,) + s.shape, s.dtype) for s in shards],
        scratch_shapes=([pltpu.SemaphoreType.DMA((3 * n,))] * 4 + [pltpu.SemaphoreType.DMA((n,))]
                        + [pltpu.VMEM(s.shape, s.dtype) for s in shards]),
        compiler_params=pltpu.CompilerParams(vmem_limit_bytes=VMEM_LIMIT),
    )(*shards)


def _exchange_halves(name, grads):
    n = len(grads)

    def body(*refs):
        ins, outs = refs[:n], refs[n:2 * n]
        send_sems, recv_sems = refs[2 * n:]
        x, y, c = _my_place()
        copies = []
        for i in range(n):
            h = ins[i].shape[1] // 2
            copies.append(_remote(ins[i].at[:, pl.ds((1 - c) * h, h), :], outs[i],
                                  send_sems.at[i], recv_sems.at[i], (x, y, 1 - c)))
        for cp in copies:
            cp.start()
        for cp in copies:
            cp.wait()

    return pl.pallas_call(
        body, name=name, in_specs=[ANY_SPEC] * n, out_specs=[ANY_SPEC] * n,
        out_shape=[jax.ShapeDtypeStruct((g.shape[0], g.shape[1] // 2, g.shape[2]), g.dtype) for g in grads],
        scratch_shapes=[pltpu.SemaphoreType.DMA((n,)), pltpu.SemaphoreType.DMA((n,))],
    )(*grads)


def _add_own_half(name, g, received, core):
    n_sh, r, cc = g.shape
    h = r // 2
    th = min(h, 256)
    nb = h // th

    def body(core_ref, g_ref, r_ref, o_ref):
        del core_ref
        o_ref[...] = (g_ref[...] + r_ref[...]).astype(BF16)

    grid_spec = pltpu.PrefetchScalarGridSpec(
        num_scalar_prefetch=1, grid=(n_sh, nb),
        in_specs=[pl.BlockSpec((None, th, cc), lambda j, i, core_ref: (j, core_ref[0] * nb + i, 0)),
                  pl.BlockSpec((None, th, cc), lambda j, i, core_ref: (j, i, 0))],
        out_specs=pl.BlockSpec((None, th, cc), lambda j, i, core_ref: (j, i, 0)))
    return pl.pallas_call(
        body, name=name, grid_spec=grid_spec,
        out_shape=jax.ShapeDtypeStruct((n_sh, h, cc), BF16), compiler_params=_params(2),
    )(core, g, received)


def _exchange_chips(name, parts):
    n = len(parts)

    def body(*refs):
        ins, outs = refs[:n], refs[n:2 * n]
        send_sems, recv_sems, local_sems = refs[2 * n:2 * n + 3]
        stage = refs[2 * n + 3:]
        x, y, c = _my_place()
        me = 2 * x + y
        chips = _other_chips(x, y)
        sends = []
        for i in range(n):
            for j, (px, py) in enumerate(chips):
                sends.append(_remote(ins[i].at[2 * px + py], outs[i].at[me], send_sems.at[3 * i + j],
                                     recv_sems.at[3 * i + j], (px, py, c)))
        for cp in sends:
            cp.start()
        local = _staged_copies([ins[i].at[me] for i in range(n)], [outs[i].at[me] for i in range(n)],
                               stage, local_sems)
        for i in range(n):
            for j, (px, py) in enumerate(chips):
                _remote(ins[i].at[me], outs[i].at[2 * px + py], send_sems.at[3 * i + j],
                        recv_sems.at[3 * i + j], (px, py, c)).wait_recv()
        for cp in sends:
            cp.wait_send()
        for cp in local:
            cp.wait()

    return pl.pallas_call(
        body, name=name, in_specs=[ANY_SPEC] * n, out_specs=[ANY_SPEC] * n,
        out_shape=[jax.ShapeDtypeStruct(p.shape, p.dtype) for p in parts],
        scratch_shapes=([pltpu.SemaphoreType.DMA((3 * n,)), pltpu.SemaphoreType.DMA((3 * n,)),
                         pltpu.SemaphoreType.DMA((n,))]
                        + [pltpu.VMEM(p.shape[1:], p.dtype) for p in parts]),
        compiler_params=pltpu.CompilerParams(vmem_limit_bytes=VMEM_LIMIT),
    )(*parts)


HBM_SPEC = pl.BlockSpec(memory_space=pltpu.HBM)
SEM_SPEC = pl.BlockSpec(memory_space=pltpu.SEMAPHORE)
SPLIT_PARAMS = pltpu.CompilerParams(has_side_effects=pltpu.SideEffectType.DATAFLOW_SIDE_EFFECTING)


def _chip_copies(ins, lands, send_sems, recv_sems, sliced):
    x, y, c = _my_place()
    me = 2 * x + y
    pairs = []
    for i in range(len(ins)):
        for j, (px, py) in enumerate(_other_chips(x, y)):
            theirs = 2 * px + py
            src = ins[i].at[theirs] if sliced else ins[i]
            sems = (send_sems.at[3 * i + j], recv_sems.at[3 * i + j], (px, py, c))
            pairs.append((_remote(src, lands[i].at[me], *sems), _remote(src, lands[i].at[theirs], *sems)))
    return pairs


def _exchange_start(name, srcs, lands, sliced, after):
    n = len(srcs)

    def body(*refs):
        ins, land_refs = refs[:n], refs[n:2 * n]
        send_sems, recv_sems = refs[2 * n + 1:2 * n + 3]
        token = refs[-1]
        for send, _ in _chip_copies(ins, land_refs, send_sems, recv_sems, sliced):
            send.start()
        token[...] = jnp.zeros_like(token)

    arrays = list(srcs) + list(lands)
    outs = pl.pallas_call(
        body, name=name,
        in_specs=[HBM_SPEC] * (2 * n) + [ANY_SPEC],
        out_shape=([pltpu.SemaphoreType.DMA((3 * n,))] * 2 + [pltpu.HBM(a.shape, a.dtype) for a in arrays]
                   + [jax.ShapeDtypeStruct((8, 128), F32)]),
        out_specs=[SEM_SPEC] * 2 + [HBM_SPEC] * (2 * n) + [pl.BlockSpec(memory_space=pltpu.VMEM)],
        input_output_aliases={i: 2 + i for i in range(2 * n)},
        compiler_params=SPLIT_PARAMS,
    )(*[pltpu.with_memory_space_constraint(a, pltpu.HBM) for a in arrays], after)
    return outs[:2], outs[2:2 + 2 * n], outs[-1]


def _exchange_wait(name, sems, passed, sliced, after):
    n = len(passed) // 2

    def body(*refs):
        ins, land_refs = refs[:n], refs[n:2 * n]
        send_sems, recv_sems = refs[2 * n:2 * n + 2]
        for send, arrive in _chip_copies(ins, land_refs, send_sems, recv_sems, sliced):
            send.wait_send()
            arrive.wait_recv()

    outs = pl.pallas_call(
        body, name=name,
        in_specs=[HBM_SPEC] * (2 * n) + [SEM_SPEC] * 2 + [ANY_SPEC],
        out_shape=[pltpu.HBM(a.shape, a.dtype) for a in passed],
        out_specs=[HBM_SPEC] * (2 * n),
        input_output_aliases={i: i for i in range(2 * n)},
        compiler_params=SPLIT_PARAMS,
    )(*passed, *sems, after)
    return outs[n:]


def _own_slot(name, own, me):
    r, cc = own.shape[-2:]
    th = min(r, 512)

    def body(me_ref, x_ref, o_ref):
        del me_ref
        o_ref[...] = x_ref[...]

    if own.ndim == 3:
        in_spec = pl.BlockSpec((None, th, cc), lambda i, me_ref: (me_ref[0], i, 0))
    else:
        in_spec = pl.BlockSpec((th, cc), lambda i, me_ref: (i, 0))
    grid_spec = pltpu.PrefetchScalarGridSpec(
        num_scalar_prefetch=1, grid=(r // th,), in_specs=[in_spec],
        out_specs=pl.BlockSpec((None, th, cc), lambda i, me_ref: (me_ref[0], i, 0)))
    return pl.pallas_call(
        body, name=name, grid_spec=grid_spec,
        out_shape=jax.ShapeDtypeStruct((N_CHIPS, r, cc), own.dtype), compiler_params=_params(1),
    )(me, own)


def _sum_chips(name, parts):
    n_sh, h, cc = parts.shape
    th = min(h, 256)

    def body(p_ref, o_ref):
        p = [p_ref[j].astype(F32) for j in range(n_sh)]
        o_ref[...] = ((p[0] + p[1]) + p[2]) + p[3]

    return pl.pallas_call(
        body, name=name, grid=(h // th,),
        in_specs=[_spec((n_sh, th, cc), lambda i: (0, i, 0))],
        out_specs=_spec((th, cc), lambda i: (i, 0)),
        out_shape=jax.ShapeDtypeStruct((h, cc), F32), compiler_params=_params(1),
    )(parts)


def _share_halves(halves):
    flat = [t for per_weight in halves for t in per_weight]
    n = len(flat)
    n_w = len(halves)

    def body(*refs):
        ins, outs = refs[:n], refs[n:n + n_w]
        send_sems, recv_sems, local_sems = refs[n + n_w:n + n_w + 3]
        stage = refs[n + n_w + 3:]
        x, y, c = _my_place()
        sends, own = [], []
        for i in range(n):
            w, l = divmod(i, DEPTH)
            h = ins[i].shape[0]
            own.append(outs[w].at[l, pl.ds(c * h, h), :])
            sends.append(_remote(ins[i], own[i], send_sems.at[i], recv_sems.at[i], (x, y, 1 - c)))
        for cp in sends:
            cp.start()
        local = _staged_copies(ins, own, stage, local_sems)
        for i in range(n):
            w, l = divmod(i, DEPTH)
            h = ins[i].shape[0]
            _remote(ins[i], outs[w].at[l, pl.ds((1 - c) * h, h), :], send_sems.at[i], recv_sems.at[i],
                    (x, y, 1 - c)).wait_recv()
        for cp in sends:
            cp.wait_send()
        for cp in local:
            cp.wait()

    return pl.pallas_call(
        body, name="share_halves", in_specs=[ANY_SPEC] * n, out_specs=[ANY_SPEC] * n_w,
        out_shape=[jax.ShapeDtypeStruct((DEPTH, 2 * per_weight[0].shape[0], per_weight[0].shape[1]), F32)
                   for per_weight in halves],
        scratch_shapes=([pltpu.SemaphoreType.DMA((n,))] * 3 + [pltpu.VMEM(t.shape, t.dtype) for t in flat]),
        compiler_params=pltpu.CompilerParams(vmem_limit_bytes=VMEM_LIMIT),
    )(*flat)


def _all_reduce_small(pack):
    def body(p_ref, o_ref, recv, send_sems, recv_sems):
        x, y, c = _my_place()
        me = 4 * x + 2 * y + c
        recv[me] = p_ref[...]
        peers = []
        for k in range(1, N_DEV):
            px, py, pc = (x + (k >> 2)) % 2, (y + ((k >> 1) & 1)) % 2, (c + (k & 1)) % 2
            peers.append((px, py, pc))
        sends = [_remote(p_ref, recv.at[me], send_sems.at[k], recv_sems.at[k], peer)
                 for k, peer in enumerate(peers)]
        for cp in sends:
            cp.start()
        for k, (px, py, pc) in enumerate(peers):
            _remote(p_ref, recv.at[4 * px + 2 * py + pc], send_sems.at[k], recv_sems.at[k],
                    (px, py, pc)).wait_recv()
        for cp in sends:
            cp.wait_send()
        total = recv[0]
        for d in range(1, N_DEV):
            total = total + recv[d]
        o_ref[...] = total

    vmem = pl.BlockSpec(memory_space=pltpu.VMEM)
    return pl.pallas_call(
        body, name="all_reduce_small", in_specs=[vmem], out_specs=vmem,
        out_shape=jax.ShapeDtypeStruct(pack.shape, F32),
        scratch_shapes=[pltpu.VMEM((N_DEV,) + pack.shape, F32),
                        pltpu.SemaphoreType.DMA((N_DEV - 1,)), pltpu.SemaphoreType.DMA((N_DEV - 1,))],
    )(pack)


def _adamw(name, w, g, m, v):
    r, cc = w.shape
    th = min(r, 256)

    def body(w_ref, g_ref, m_ref, v_ref, d_ref, m_out, v_out):
        gv = g_ref[...]
        m2 = ADAM_B1 * m_ref[...] + (1.0 - ADAM_B1) * gv
        v2 = ADAM_B2 * v_ref[...] + (1.0 - ADAM_B2) * (gv * gv)
        m_hat = m2 / (1.0 - ADAM_B1 ** ADAM_STEP)
        v_hat = v2 / (1.0 - ADAM_B2 ** ADAM_STEP)
        d_ref[...] = -ADAM_LR * (m_hat / (jnp.sqrt(v_hat) + ADAM_EPS) + ADAM_WD * w_ref[...])
        m_out[...] = m2
        v_out[...] = v2

    tile = _spec((th, cc), lambda i: (i, 0))
    return pl.pallas_call(
        body, name=name, grid=(r // th,), in_specs=[tile] * 4, out_specs=[tile] * 3,
        out_shape=[jax.ShapeDtypeStruct((r, cc), F32)] * 3, compiler_params=_params(1),
    )(w, g, m, v)


def _lower_bounds(lb_logits):
    p = jax.nn.softmax(lb_logits.astype(F32), axis=0)
    return jnp.cumsum(p, axis=0) - p[0]


def _row_tile_specs(tm, width):
    return _spec((tm, width), lambda i, j, k: (i, 0))


def _layer_forward(l, x_in, small, weights, consts, after=None):
    win, rest = weights
    cos_t, sin_t, stack, _, _ = consts
    tm = MM_TILE
    n_row = SEQ // tm
    saved = {"x_in": x_in}

    h, h_t = _rms_fwd(f"norm_mix{l}", x_in, small["norm_mix"][l][None, :], after=after)
    proj = _matmul(f"proj{l}", h, win,
                   _spec((tm, D_MODEL), lambda i, j, k: (i, 0)),
                   _spec((None, D_MODEL, SHARD_IN), lambda i, j, k: (j, 0, 0)),
                   (SEQ, IN_W), F32, _spec((tm, SHARD_IN), lambda i, j, k: (i, j)),
                   (n_row, N_CHIPS, 1), (tm, SHARD_IN))
    saved.update(h_t=h_t, proj=proj)

    qkv = _attn_prep(f"attn_prep{l}", proj, cos_t, sin_t)
    outs, lses = [], []
    for p, d in enumerate(DILATIONS):
        o, lse = _attn_fwd(f"attn_fwd{l}_{d}", *qkv[p], SEQ // d // SPAN)
        outs.append(o)
        lses.append(lse)
    an, an_t, attn, lse = _attn_merge(f"attn_merge{l}", outs, lses, small["attn_out_gain"][l][None, :])
    saved.update(qkv=qkv, an_t=an_t, attn=attn, lse=lse)

    lb3 = small["lower"][l].reshape(HGRN_HEADS, 1, HGRN_DIM)
    rec, rec_t, o_pre, states, scores = _hgrn_fwd(f"hgrn_fwd{l}", proj, lb3,
                                                  small["hgrn_out_gain"][l][None, :], stack)
    wo, wu, wd = rest(rec)
    saved.update(rec_t=rec_t, o_pre=o_pre, states=states, scores=scores, lb3=lb3, weights=(win, wo, wu, wd))

    def out_proj(name, a, part, resid):
        return _matmul(name, a, wo,
                       _spec((tm, SHARD_OUT), lambda i, j, k: (i, k)),
                       _spec((None, SHARD_OUT, D_MODEL), lambda i, j, k: (k + 2 * part, 0, 0)),
                       (SEQ, D_MODEL), F32, _spec((tm, D_MODEL), lambda i, j, k: (i, 0)),
                       (n_row, 1, 2), (tm, D_MODEL),
                       extra=resid, extra_spec=_spec((tm, D_MODEL), lambda i, j, k: (i, 0)), epilogue="add")

    x_mid = out_proj(f"out_rec{l}", rec, 1, out_proj(f"out_attn{l}", an, 0, x_in))
    saved["x_mid"] = x_mid

    h2, h2_t = _rms_fwd(f"norm_mlp{l}", x_mid, small["norm_mlp"][l][None, :])
    a, relu_u, a_t = _matmul(
        f"up{l}", h2, wu,
        _spec((tm, D_MODEL), lambda i, j, k: (i, 0)),
        _spec((None, D_MODEL, SHARD_MLP), lambda i, j, k: (j, 0, 0)),
        (SEQ, MLP_HIDDEN), BF16, _spec((tm, SHARD_MLP), lambda i, j, k: (i, j)),
        (n_row, N_CHIPS, 1), (tm, SHARD_MLP), epilogue="relu2",
        relu_outs=[((SEQ, MLP_HIDDEN), _spec((tm, SHARD_MLP), lambda i, j, k: (i, j))),
                   ((MLP_HIDDEN, SEQ), _spec((SHARD_MLP, tm), lambda i, j, k: (j, i)))])
    x_out = _matmul(f"down{l}", a, wd,
                    _spec((tm, SHARD_MLP), lambda i, j, k: (i, k)),
                    _spec((None, SHARD_MLP, D_MODEL), lambda i, j, k: (k, 0, 0)),
                    (SEQ, D_MODEL), F32, _spec((tm, D_MODEL), lambda i, j, k: (i, 0)),
                    (n_row, 1, N_CHIPS), (tm, D_MODEL),
                    extra=x_mid, extra_spec=_spec((tm, D_MODEL), lambda i, j, k: (i, 0)), epilogue="add")
    saved.update(h2_t=h2_t, relu_u=relu_u, a_t=a_t)
    return x_out, saved


def _layer_backward(l, dx, saved, small, consts, on_grads, after=None):
    win, wo, wu, wd = saved["weights"]
    cos_t, sin_t, stack, stack_t, head_sum = consts
    tm = MM_TILE
    n_row = SEQ // tm
    n_k = SEQ // tm

    dx, dx_b = dx
    du = _matmul(f"d_u{l}", dx_b, wd,
                 _spec((tm, D_MODEL), lambda i, j, k: (i, 0)),
                 _spec((None, SHARD_MLP, D_MODEL), lambda i, j, k: (j, 0, 0)),
                 (SEQ, MLP_HIDDEN), BF16, _spec((tm, SHARD_MLP), lambda i, j, k: (i, j)),
                 (n_row, N_CHIPS, 1), (tm, SHARD_MLP), nt=True,
                 extra=saved["relu_u"], extra_spec=_spec((tm, SHARD_MLP), lambda i, j, k: (i, j)),
                 epilogue="relu2_grad", after=after)
    d_wd = _matmul(f"d_wdown{l}", saved["a_t"], dx_b,
                   _spec((SHARD_MLP, tm), lambda i, j, k: (i, k)),
                   _spec((tm, D_MODEL), lambda i, j, k: (k, 0)),
                   (N_CHIPS, SHARD_MLP, D_MODEL), F32, _spec((None, SHARD_MLP, D_MODEL), lambda i, j, k: (i, 0, 0)),
                   (N_CHIPS, 1, n_k), (SHARD_MLP, D_MODEL))
    dh2 = _matmul(f"d_h2_{l}", du, wu,
                  _spec((tm, SHARD_MLP), lambda i, j, k: (i, k)),
                  _spec((None, D_MODEL, SHARD_MLP), lambda i, j, k: (k, 0, 0)),
                  (SEQ, D_MODEL), F32, _spec((tm, D_MODEL), lambda i, j, k: (i, 0)),
                  (n_row, 1, N_CHIPS), (tm, D_MODEL), nt=True)
    d_wu = _matmul(f"d_wup{l}", saved["h2_t"], du,
                   _spec((D_MODEL, tm), lambda i, j, k: (0, k)),
                   _spec((tm, SHARD_MLP), lambda i, j, k: (k, j)),
                   (N_CHIPS, D_MODEL, SHARD_MLP), F32, _spec((None, D_MODEL, SHARD_MLP), lambda i, j, k: (j, 0, 0)),
                   (1, N_CHIPS, n_k), (D_MODEL, SHARD_MLP))
    dxm, dxm_b, dg_mlp = _rms_bwd(f"norm_mlp_bwd{l}", dh2, saved["x_mid"], small["norm_mlp"][l][None, :], dx)
    after_mlp = on_grads(l, "mlp", (d_wu, d_wd))

    def d_mixed(name, part):
        return _matmul(name, dxm_b, wo,
                       _spec((tm, D_MODEL), lambda i, j, k: (i, 0)),
                       _spec((None, SHARD_OUT, D_MODEL), lambda i, j, k: (j + 2 * part, 0, 0)),
                       (SEQ, ATTN_WIDTH), F32, _spec((tm, SHARD_OUT), lambda i, j, k: (i, j)),
                       (n_row, 2, 1), (tm, SHARD_OUT), nt=True, after=after_mlp)

    def d_wout(name, a_t):
        return _matmul(name, a_t, dxm_b,
                       _spec((SHARD_OUT, tm), lambda i, j, k: (i, k)),
                       _spec((tm, D_MODEL), lambda i, j, k: (k, 0)),
                       (2, SHARD_OUT, D_MODEL), F32, _spec((None, SHARD_OUT, D_MODEL), lambda i, j, k: (i, 0, 0)),
                       (2, 1, n_k), (SHARD_OUT, D_MODEL))

    d_an = d_mixed(f"d_attn_n{l}", 0)
    d_rec = d_mixed(f"d_rec{l}", 1)
    d_wo = jnp.concatenate([d_wout(f"d_wout_attn{l}", saved["an_t"]),
                            d_wout(f"d_wout_rec{l}", saved["rec_t"])], axis=0)

    d_out, delta, lses, dg_attn = _attn_bwd_prep(f"attn_bwd_prep{l}", d_an, saved["attn"], saved["lse"],
                                                 small["attn_out_gain"][l][None, :], head_sum)
    grads = []
    for p, d in enumerate(DILATIONS):
        grads.append(_attn_bwd(f"attn_bwd{l}_{d}", *saved["qkv"][p], d_out[p], delta[p], lses[p],
                               SEQ // d // SPAN))
    dp_attn = _attn_bwd_post(f"attn_bwd_post{l}", grads, cos_t, sin_t)

    dq_h, df_h, di_h, dg_h, d_lower, dg_hgrn = _hgrn_bwd(
        f"hgrn_bwd{l}", saved["proj"], d_rec, saved["o_pre"], saved["states"], saved["scores"],
        saved["lb3"], small["hgrn_out_gain"][l][None, :], stack, stack_t)
    dproj = jnp.concatenate([dp_attn, dq_h, df_h, di_h, dg_h], axis=1)

    dh = _matmul(f"d_h{l}", dproj, win,
                 _spec((tm, SHARD_IN), lambda i, j, k: (i, k)),
                 _spec((None, D_MODEL, SHARD_IN), lambda i, j, k: (k, 0, 0)),
                 (SEQ, D_MODEL), F32, _spec((tm, D_MODEL), lambda i, j, k: (i, 0)),
                 (n_row, 1, N_CHIPS), (tm, D_MODEL), nt=True)
    d_win = _matmul(f"d_win{l}", saved["h_t"], dproj,
                    _spec((D_MODEL, tm), lambda i, j, k: (0, k)),
                    _spec((tm, SHARD_IN), lambda i, j, k: (k, j)),
                    (N_CHIPS, D_MODEL, SHARD_IN), F32, _spec((None, D_MODEL, SHARD_IN), lambda i, j, k: (j, 0, 0)),
                    (1, N_CHIPS, n_k), (D_MODEL, SHARD_IN))
    dx_in, dx_in_b, dg_mix = _rms_bwd(f"norm_mix_bwd{l}", dh, saved["x_in"], small["norm_mix"][l][None, :], dxm)

    small_grads = {"norm_mix": dg_mix[0], "attn_out_gain": dg_attn[0],
                   "lower": d_lower.reshape(HGRN_WIDTH),
                   "hgrn_out_gain": jnp.sum(dg_hgrn, axis=0).reshape(HGRN_DIM), "norm_mlp": dg_mlp[0]}
    return (dx_in, dx_in_b), on_grads(l, "mix", (d_win, d_wo)), small_grads


def _local_step(xs, target, small, get_weights, on_grads):
    consts = _rope_tables() + _hgrn_consts() + (_head_sum_matrix(),)
    stream = xs
    saved = []
    for l in range(DEPTH):
        w, after = get_weights(l, stream)
        stream, s = _layer_forward(l, stream, small, w, consts, after=after)
        saved.append(s)
    dx_f, dx_b, dg_final, loss = _loss_head(stream, small["norm_final"][None, :], target)
    dx = (dx_f, dx_b)
    small_grads = [None] * DEPTH
    after = None
    for l in reversed(range(DEPTH)):
        dx, after, small_grads[l] = _layer_backward(l, dx, saved[l], small, consts, on_grads, after=after)
    return loss, dx[0], dg_final[0], small_grads


def _pack_small(norm_mix, attn_out_gain, lb, hgrn_out_gain, norm_mlp, norm_final, last_row):
    rows = [norm_mix, attn_out_gain.reshape(1, D_MODEL), lb.reshape(1, D_MODEL),
            jnp.pad(hgrn_out_gain.reshape(1, DEPTH * HGRN_DIM), ((0, 0), (0, D_MODEL - DEPTH * HGRN_DIM))),
            norm_mlp, norm_final.reshape(1, D_MODEL), last_row.reshape(1, D_MODEL)]
    pack = jnp.concatenate(rows, axis=0)
    return jnp.pad(pack, ((0, PACK_ROWS - pack.shape[0]), (0, 0)))


def _unpack_small(pack):
    return (pack[0:2], pack[2].reshape(DEPTH, ATTN_WIDTH), pack[3].reshape(DEPTH, HGRN_WIDTH),
            pack[4, :DEPTH * HGRN_DIM].reshape(DEPTH, HGRN_DIM), pack[5:7], pack[7], pack[8])


def kernel(x, norm_mix, w_in, attn_out_gain, hgrn_lb_logits, hgrn_out_gain, w_out, norm_mlp, w_up, w_down, norm_final, loss_target, m_norm_mix, m_w_in, m_attn_out_gain, m_hgrn_lb_logits, m_hgrn_out_gain, m_w_out, m_norm_mlp, m_w_up, m_w_down, m_norm_final, v_norm_mix, v_w_in, v_attn_out_gain, v_hgrn_lb_logits, v_hgrn_out_gain, v_w_out, v_norm_mlp, v_w_up, v_w_down, v_norm_final):
    core = lax.axis_index("c").astype(jnp.int32).reshape(1)
    lower, lower_vjp = jax.vjp(_lower_bounds, hgrn_lb_logits)
    small = {"norm_mix": norm_mix, "attn_out_gain": attn_out_gain, "lower": lower,
             "hgrn_out_gain": hgrn_out_gain, "norm_mlp": norm_mlp, "norm_final": norm_final}
    big_w = (w_in, w_out, w_up, w_down)

    me = (2 * lax.axis_index("x") + lax.axis_index("y")).astype(jnp.int32).reshape(1)
    shards =[[w[l].astype(BF16) for w in big_w] for l in range(DEPTH)]
    in_flight = {}

    def start_gather(name, some, after):
        lands = [_own_slot(f"own_{name}_{i}", s, me) for i, s in enumerate(some)]
        sems, passed, token = _exchange_start(name, some, lands, False, after)
        return (sems, passed), token

    def get_weights(l, stream):
        if l == 0:
            (win,) = _gather_weights("gather_w_in0", shards[0][:1])
            in_flight["rest0"], token = start_gather("gather_start0", shards[0][1:], win)
            in_flight["weights1"], token = start_gather("gather_start1", shards[1], token)
            return (win, lambda after: _exchange_wait("gather_wait0", *in_flight.pop("rest0"), False, after)), token
        weights = _exchange_wait("gather_wait1", *in_flight.pop("weights1"), False, stream)
        return (weights[0], lambda after: weights[1:]), None

    reduced = {}

    def start_exchange(name, grads):
        received = _exchange_halves(f"halves_{name}", grads)
        halves = [_add_own_half(f"add_{name}_{i}", g, r, core) for i, (g, r) in enumerate(zip(grads, received))]
        lands = [_own_slot(f"own_{name}_{i}", h, me) for i, h in enumerate(halves)]
        sems, passed, token = _exchange_start(f"start_{name}", halves, lands, True, halves[0])
        in_flight[name] = (sems, passed)
        return token

    def finish_exchange(name, after):
        landed = _exchange_wait(f"wait_{name}", *in_flight.pop(name), True, after)
        return [_sum_chips(f"sum_{name}_{i}", p) for i, p in enumerate(landed)]

    def on_grads(l, group, grads):
        if (l, group) == (1, "mlp"):
            return start_exchange("mlp1", grads)
        if (l, group) == (1, "mix"):
            return start_exchange("mix1", grads)
        if (l, group) == (0, "mlp"):
            reduced[(1, "mlp")] = finish_exchange("mlp1", grads[0])
            reduced[(1, "mix")] = finish_exchange("mix1", grads[0])
            return start_exchange("mlp0", grads)
        received = _exchange_halves("halves_mix0", grads)
        halves = [_add_own_half(f"add_mix0_{i}", g, r, core) for i, (g, r) in enumerate(zip(grads, received))]
        from_chips = _exchange_chips("exchange_chips_mix0", halves)
        reduced[(0, "mix")] = [_sum_chips(f"sum_mix0_{i}", p) for i, p in enumerate(from_chips)]
        reduced[(0, "mlp")] = finish_exchange("mlp0", from_chips[0])
        return None

    loss, dx, dg_final, sg = _local_step(x[0], loss_target[0], small, get_weights, on_grads)

    stack2 = lambda key: jnp.stack([sg[l][key] for l in range(DEPTH)])
    pack = _pack_small(stack2("norm_mix"), stack2("attn_out_gain"), stack2("lower"), stack2("hgrn_out_gain"),
                       stack2("norm_mlp"), dg_final, jnp.broadcast_to(loss[0, 0], (D_MODEL,)))
    g_mix, g_attn, g_lower, g_hgrn, g_mlp, g_final, loss_row = _unpack_small(_all_reduce_small(pack))
    (g_logits,) = lower_vjp(g_lower)

    zeros_row = jnp.zeros((D_MODEL,), F32)
    small_w = (norm_mix, attn_out_gain, hgrn_lb_logits, hgrn_out_gain, norm_mlp, norm_final)
    small_m = (m_norm_mix, m_attn_out_gain, m_hgrn_lb_logits, m_hgrn_out_gain, m_norm_mlp, m_norm_final)
    small_v = (v_norm_mix, v_attn_out_gain, v_hgrn_lb_logits, v_hgrn_out_gain, v_norm_mlp, v_norm_final)
    small_g = (g_mix, g_attn, g_logits, g_hgrn, g_mlp, g_final)
    packs = [_pack_small(*t, zeros_row) for t in (small_w, small_g, small_m, small_v)]
    small_delta, small_new_m, small_new_v = [_unpack_small(p)[:6] for p in _adamw("adamw_small", *packs)]

    big_g = _share_halves([[reduced[(l, group)][i] for l in range(DEPTH)]
                           for group, i in (("mix", 0), ("mix", 1), ("mlp", 0), ("mlp", 1))])

    big_m = (m_w_in, m_w_out, m_w_up, m_w_down)
    big_v = (v_w_in, v_w_out, v_w_up, v_w_down)
    big_delta, big_new_m, big_new_v = [], [], []
    for i, name in enumerate(("w_in", "w_out", "w_up", "w_down")):
        shape = big_w[i].shape
        flat = lambda arr: arr.reshape(shape[0] * shape[1], shape[2])
        d, m2, v2 = _adamw(f"adamw_{name}", flat(big_w[i]), flat(big_g[i]), flat(big_m[i]), flat(big_v[i]))
        big_delta.append(d.reshape(shape))
        big_new_m.append(m2.reshape(shape))
        big_new_v.append(v2.reshape(shape))

    def ordered(small6, big4):
        mix, attn, lbl, hg, mlp, fin = small6
        return (mix, big4[0], attn, lbl, hg, big4[1], mlp, big4[2], big4[3], fin)

    return ((loss_row[0], dx[None]) + ordered(small_g, big_g) + ordered(small_delta, big_delta)
            + ordered(small_new_m, big_new_m) + ordered(small_new_v, big_new_v))
```

```python
import functools
import math

import numpy as np
import jax
import jax.numpy as jnp
from jax import lax
from jax.experimental import pallas as pl
from jax.experimental.pallas import tpu as pltpu

F32 = jnp.float32
BF16 = jnp.bfloat16
MESH = pl.DeviceIdType.MESH

SEQ = 4096
D_MODEL = 1024
DEPTH = 2
ATTN_WIDTH = 512
HEAD_DIM = 64
HGRN_HEADS = 4
HGRN_DIM = 128
HGRN_WIDTH = 512
IN_W = 3584
MLP_HIDDEN = 4096
N_CHIPS = 4
N_DEV = 8
SHARD_IN = IN_W // N_CHIPS
SHARD_OUT = D_MODEL // N_CHIPS
SHARD_MLP = MLP_HIDDEN // N_CHIPS
DILATIONS = (1, 4, 16)
SPAN = 128
ROPE_THETA = 10000.0
NORM_EPS = 1e-6
MASK_VALUE = -1e30
CHUNK = 128
ROW_TILE = 512
MM_TILE = 1024
VMEM_LIMIT = 52 * 1024 * 1024

ADAM_LR = 0.001
ADAM_B1 = 0.9
ADAM_B2 = 0.999
ADAM_EPS = 1e-08
ADAM_WD = 0.01
ADAM_STEP = 10

PACK_ROWS = 16


def _params(n_axes):
    return pltpu.CompilerParams(dimension_semantics=("arbitrary",) * n_axes,
                                vmem_limit_bytes=VMEM_LIMIT)


def _dot(a, b):
    return jnp.dot(a.astype(BF16), b.astype(BF16), preferred_element_type=F32)


def _dot_nt(a, b):
    return lax.dot_general(a.astype(BF16), b.astype(BF16), (((1,), (1,)), ((), ())),
                           preferred_element_type=F32)


def _sigmoid(x):
    return 1.0 / (1.0 + jnp.exp(-x))


def _matmul(name, a, b, a_spec, b_spec, out_shape, out_dtype, out_spec, grid, acc_shape,
            nt=False, extra=None, extra_spec=None, epilogue="none", after=None, relu_outs=None):
    nk = grid[2]
    n_out = 1 if relu_outs is None else 3

    def body(*refs):
        a_ref, b_ref = refs[:2]
        e_ref = None if extra is None else refs[2]
        o_ref = refs[-1 - n_out]
        acc = refs[-1]
        kk = pl.program_id(2)

        def product():
            return _dot_nt(a_ref[...], b_ref[...]) if nt else _dot(a_ref[...], b_ref[...])

        if nk > 1:
            @pl.when(kk == 0)
            def _():
                acc[...] = jnp.zeros_like(acc)

            acc[...] += product()

        @pl.when(kk == nk - 1)
        def _():
            r = acc[...] if nk > 1 else product()
            if epilogue == "add":
                r = r + e_ref[...]
            elif epilogue == "relu2_grad":
                r = r * (2.0 * e_ref[...].astype(F32))
            elif epilogue == "relu2":
                s = jnp.maximum(r, 0.0)
                r = s * s
                refs[-3][...] = s.astype(out_dtype)
                refs[-2][...] = r.T.astype(out_dtype)
            o_ref[...] = r.astype(o_ref.dtype)

    in_specs = [a_spec, b_spec] + ([] if extra is None else [extra_spec])
    args = (a, b) + (() if extra is None else (extra,))
    if after is not None:
        in_specs.append(pl.BlockSpec(memory_space=pl.ANY))
        args += (after,)
    out_specs, out_shapes = out_spec, jax.ShapeDtypeStruct(out_shape, out_dtype)
    if relu_outs is not None:
        out_specs = [out_spec] + [spec for _, spec in relu_outs]
        out_shapes = [out_shapes] + [jax.ShapeDtypeStruct(shape, out_dtype) for shape, _ in relu_outs]
    return pl.pallas_call(
        body, name=name, grid=grid, in_specs=in_specs, out_specs=out_specs, out_shape=out_shapes,
        scratch_shapes=[pltpu.VMEM(acc_shape, F32)],
        compiler_params=_params(3),
    )(*args)


def _spec(shape, index_map):
    return pl.BlockSpec(shape, index_map)


def _rms_fwd(name, x, gain, after=None):
    s, d = x.shape
    t = ROW_TILE

    def body(x_ref, g_ref, *rest):
        h_ref, ht_ref = rest[-2:]
        xv = x_ref[...]
        r = lax.rsqrt(jnp.mean(xv * xv, axis=1, keepdims=True) + NORM_EPS)
        h = xv * r * g_ref[...]
        h_ref[...] = h.astype(BF16)
        ht_ref[...] = h.T.astype(BF16)

    in_specs = [_spec((t, d), lambda i: (i, 0)), _spec((1, d), lambda i: (0, 0))]
    args = (x, gain)
    if after is not None:
        in_specs.append(pl.BlockSpec(memory_space=pl.ANY))
        args += (after,)
    return pl.pallas_call(
        body, name=name, grid=(s // t,), in_specs=in_specs,
        out_specs=[_spec((t, d), lambda i: (i, 0)), _spec((d, t), lambda i: (0, i))],
        out_shape=[jax.ShapeDtypeStruct((s, d), BF16), jax.ShapeDtypeStruct((d, s), BF16)],
        compiler_params=_params(1),
    )(*args)


def _rms_bwd(name, dh, x, gain, dres):
    s, d = x.shape
    t = ROW_TILE

    def body(dh_ref, x_ref, g_ref, dres_ref, dx_ref, dxb_ref, dg_ref):
        @pl.when(pl.program_id(0) == 0)
        def _():
            dg_ref[...] = jnp.zeros_like(dg_ref)

        xv = x_ref[...]
        dhv = dh_ref[...]
        r = lax.rsqrt(jnp.mean(xv * xv, axis=1, keepdims=True) + NORM_EPS)
        xhat = xv * r
        dhg = dhv * g_ref[...]
        proj = jnp.mean(dhg * xhat, axis=1, keepdims=True)
        dx = dres_ref[...] + r * (dhg - xhat * proj)
        dx_ref[...] = dx
        dxb_ref[...] = dx.astype(BF16)
        dg_ref[...] += jnp.sum(dhv * xhat, axis=0, keepdims=True)

    return pl.pallas_call(
        body, name=name, grid=(s // t,),
        in_specs=[_spec((t, d), lambda i: (i, 0)), _spec((t, d), lambda i: (i, 0)),
                  _spec((1, d), lambda i: (0, 0)), _spec((t, d), lambda i: (i, 0))],
        out_specs=[_spec((t, d), lambda i: (i, 0)), _spec((t, d), lambda i: (i, 0)),
                   _spec((1, d), lambda i: (0, 0))],
        out_shape=[jax.ShapeDtypeStruct((s, d), F32), jax.ShapeDtypeStruct((s, d), BF16),
                   jax.ShapeDtypeStruct((1, d), F32)],
        compiler_params=_params(1),
    )(dh, x, gain, dres)


def _loss_head(x, gain, target):
    s, d = x.shape
    t = ROW_TILE
    n_steps = s // t

    def body(x_ref, g_ref, t_ref, dx_ref, dxb_ref, dg_ref, loss_ref, acc):
        i = pl.program_id(0)

        @pl.when(i == 0)
        def _():
            dg_ref[...] = jnp.zeros_like(dg_ref)
            acc[...] = jnp.zeros_like(acc)

        xv = x_ref[...]
        g = g_ref[...]
        r = lax.rsqrt(jnp.mean(xv * xv, axis=1, keepdims=True) + NORM_EPS)
        xhat = xv * r
        err = xhat * g - t_ref[...]
        acc[...] += jnp.sum(err * err, axis=0, keepdims=True)
        dy = err * (1.0 / d)
        dyg = dy * g
        proj = jnp.mean(dyg * xhat, axis=1, keepdims=True)
        dx = r * (dyg - xhat * proj)
        dx_ref[...] = dx
        dxb_ref[...] = dx.astype(BF16)
        dg_ref[...] += jnp.sum(dy * xhat, axis=0, keepdims=True)

        @pl.when(i == n_steps - 1)
        def _():
            total = jnp.sum(acc[...], axis=1, keepdims=True) * (0.5 / d)
            loss_ref[...] = jnp.broadcast_to(total, loss_ref.shape)

    return pl.pallas_call(
        body, name="loss_head", grid=(n_steps,),
        in_specs=[_spec((t, d), lambda i: (i, 0)), _spec((1, d), lambda i: (0, 0)),
                  _spec((t, d), lambda i: (i, 0))],
        out_specs=[_spec((t, d), lambda i: (i, 0)), _spec((t, d), lambda i: (i, 0)),
                   _spec((1, d), lambda i: (0, 0)), _spec((1, 128), lambda i: (0, 0))],
        out_shape=[jax.ShapeDtypeStruct((s, d), F32), jax.ShapeDtypeStruct((s, d), BF16),
                   jax.ShapeDtypeStruct((1, d), F32), jax.ShapeDtypeStruct((1, 128), F32)],
        scratch_shapes=[pltpu.VMEM((1, d), F32)],
        compiler_params=_params(1),
    )(x, gain, target)


def _rope_tables():
    half = HEAD_DIM // 2
    inv_freq = ROPE_THETA ** (-jnp.arange(half, dtype=F32) / half)
    ang = jnp.arange(SEQ, dtype=jnp.int32).astype(F32)[:, None] * inv_freq[None, :]
    cos, sin = jnp.cos(ang), jnp.sin(ang)
    cos_t = jnp.concatenate([cos, cos, cos, cos], axis=1)
    sin_t = jnp.concatenate([-sin, sin, -sin, sin], axis=1)
    return cos_t, sin_t


def _swap_halves(x):
    lane = lax.broadcasted_iota(jnp.int32, x.shape, 1)
    first = (lane % HEAD_DIM) < (HEAD_DIM // 2)
    return jnp.where(first, pltpu.roll(x, 128 - HEAD_DIM // 2, 1), pltpu.roll(x, HEAD_DIM // 2, 1))


def _permuted_specs(t, width):
    specs = [_spec((t, width), lambda i: (i, 0))]
    for d in DILATIONS[1:]:
        specs.append(_spec((d, t // d, width), lambda i: (0, i, 0)))
    return specs


def _permuted_shapes(width, dtype):
    shapes = [jax.ShapeDtypeStruct((SEQ, width), dtype)]
    for d in DILATIONS[1:]:
        shapes.append(jax.ShapeDtypeStruct((d, SEQ // d, width), dtype))
    return shapes


def _attn_prep(name, proj, cos_t, sin_t):
    t = ROW_TILE
    w = ATTN_WIDTH

    def body(q_ref, k_ref, v_ref, cos_ref, sin_ref, *rest):
        outs, scr = rest[:9], rest[9]
        cosv, sinv = cos_ref[...], sin_ref[...]
        for a, (src, roped, scale) in enumerate(((q_ref, True, HEAD_DIM ** -0.5),
                                                 (k_ref, True, 1.0), (v_ref, False, 1.0))):
            o1, o4, o16 = outs[3 * a:3 * a + 3]
            for cb in range(w // 128):
                cols = slice(cb * 128, (cb + 1) * 128)
                val = src[:, cols]
                if roped:
                    val = (val * cosv + _swap_halves(val) * sinv) * scale
                scr[...] = val
                o1[:, cols] = val.astype(BF16)
                for o_ref, d in ((o4, 4), (o16, 16)):
                    for r in range(d):
                        o_ref[r, :, cols] = scr[pl.ds(r, t // d, stride=d), :].astype(BF16)

    out_specs = _permuted_specs(t, w) * 3
    out_shape = _permuted_shapes(w, BF16) * 3
    outs = pl.pallas_call(
        body, name=name, grid=(SEQ // t,),
        in_specs=[_spec((t, w), lambda i: (i, 0)), _spec((t, w), lambda i: (i, 1)),
                  _spec((t, w), lambda i: (i, 2)),
                  _spec((t, 128), lambda i: (i, 0)), _spec((t, 128), lambda i: (i, 0))],
        out_specs=out_specs, out_shape=out_shape,
        scratch_shapes=[pltpu.VMEM((t, 128), F32)],
        compiler_params=_params(1),
    )(proj, proj, proj, cos_t, sin_t)
    q, k, v = outs[0:3], outs[3:6], outs[6:9]
    flat = lambda arr: arr.reshape(SEQ, w)
    return [(flat(q[p]), flat(k[p]), flat(v[p])) for p in range(3)]


def _band_masks():
    row = lax.broadcasted_iota(jnp.int32, (2 * SPAN, 2 * SPAN), 0) % SPAN
    col = lax.broadcasted_iota(jnp.int32, (2 * SPAN, 2 * SPAN), 1)
    is_prev = col < SPAN
    band = (is_prev & (col >= row)) | (~is_prev & (col - SPAN <= row))
    head0 = lax.broadcasted_iota(jnp.int32, (SPAN, 128), 1) < HEAD_DIM
    return band, is_prev, head0


def _stack_heads(x, head0):
    zero = jnp.zeros_like(x)
    return jnp.concatenate([jnp.where(head0, x, zero), jnp.where(head0, zero, x)], axis=0)


def _attn_fwd(name, q, k, v, seg_blocks):
    n_blocks = SEQ // SPAN

    def body(q_ref, k_ref, v_ref, o_ref, lse_ref):
        band, is_prev, head0 = _band_masks()

        def step(b, carry):
            cur = pl.ds(pl.multiple_of(b * SPAN, SPAN), SPAN)
            prev = pl.ds(pl.multiple_of(jnp.maximum(b - 1, 0) * SPAN, SPAN), SPAN)
            qs = _stack_heads(q_ref[cur, :], head0)
            kcat = jnp.concatenate([k_ref[prev, :], k_ref[cur, :]], axis=0)
            vcat = jnp.concatenate([v_ref[prev, :], v_ref[cur, :]], axis=0)
            ok = band & (((b % seg_blocks) != 0) | ~is_prev)
            s = jnp.where(ok, _dot_nt(qs, kcat), MASK_VALUE)
            m = jnp.max(s, axis=1, keepdims=True)
            p = jnp.where(ok, jnp.exp(s - m), 0.0)
            l = jnp.sum(p, axis=1, keepdims=True)
            pv = _dot(p, vcat) / l
            lse = m + jnp.log(l)
            o_ref[cur, :] = jnp.where(head0, pv[:SPAN], pv[SPAN:])
            lse_ref[cur, :] = jnp.where(head0, lse[:SPAN], lse[SPAN:])
            return carry

        lax.fori_loop(0, n_blocks, step, 0, unroll=2)

    col = _spec((SEQ, 128), lambda j: (0, j))
    return pl.pallas_call(
        body, name=name, grid=(ATTN_WIDTH // 128,),
        in_specs=[col, col, col], out_specs=[col, col],
        out_shape=[jax.ShapeDtypeStruct((SEQ, ATTN_WIDTH), F32)] * 2,
        compiler_params=_params(1),
    )(q, k, v)


def _unpermute(dst, src_ref, d, cols):
    n = dst.shape[0] // d
    for r in range(d):
        dst[pl.ds(r, n, stride=d), :] = src_ref[r, :, cols]


def _attn_merge(name, outs, lses, gain):
    t = ROW_TILE
    w = ATTN_WIDTH

    def body(o1, o4, o16, l1, l4, l16, g_ref, an_ref, ant_ref, attn_ref, lse_ref, so4, so16, sl4, sl16):
        for cb in range(w // 128):
            cols = slice(cb * 128, (cb + 1) * 128)
            _unpermute(so4, o4, 4, cols)
            _unpermute(so16, o16, 16, cols)
            _unpermute(sl4, l4, 4, cols)
            _unpermute(sl16, l16, 16, cols)
            la, lb, lc = l1[:, cols], sl4[...], sl16[...]
            m = jnp.maximum(jnp.maximum(la, lb), lc)
            ea, eb, ec = jnp.exp(la - m), jnp.exp(lb - m), jnp.exp(lc - m)
            tot = ea + eb + ec
            attn_ref[:, cols] = (ea * o1[:, cols] + eb * so4[...] + ec * so16[...]) / tot
            lse_ref[:, cols] = m + jnp.log(tot)
        attn = attn_ref[...]
        r = lax.rsqrt(jnp.mean(attn * attn, axis=1, keepdims=True) + NORM_EPS)
        an = attn * r * g_ref[...]
        an_ref[...] = an.astype(BF16)
        ant_ref[...] = an.T.astype(BF16)

    views = lambda arrs: [arrs[0], arrs[1].reshape(4, SEQ // 4, w), arrs[2].reshape(16, SEQ // 16, w)]
    row = _spec((t, w), lambda i: (i, 0))
    return pl.pallas_call(
        body, name=name, grid=(SEQ // t,),
        in_specs=_permuted_specs(t, w) * 2 + [_spec((1, w), lambda i: (0, 0))],
        out_specs=[row, _spec((w, t), lambda i: (0, i)), row, row],
        out_shape=[jax.ShapeDtypeStruct((SEQ, 2 * w), BF16), jax.ShapeDtypeStruct((2 * w, SEQ), BF16),
                   jax.ShapeDtypeStruct((SEQ, w), F32), jax.ShapeDtypeStruct((SEQ, w), F32)],
        scratch_shapes=[pltpu.VMEM((t, 128), F32)] * 4,
        compiler_params=_params(1),
    )(*views(outs), *views(lses), gain)


def _head_sum_matrix():
    i = np.arange(ATTN_WIDTH)
    return jnp.asarray((i[:, None] // HEAD_DIM) == (i[None, :] // HEAD_DIM), dtype=F32)


def _attn_bwd_prep(name, d_an, attn, lse, gain, head_sum):
    t = ROW_TILE
    w = ATTN_WIDTH

    def body(dan_ref, attn_ref, lse_ref, g_ref, hs_ref, *rest):
        (do1, do4, do16, dl1, dl4, dl16, ls4, ls16, dg_ref), (sdo, sdl, sls) = rest[:9], rest[9:]

        @pl.when(pl.program_id(0) == 0)
        def _():
            dg_ref[...] = jnp.zeros_like(dg_ref)

        attn = attn_ref[...]
        dan = dan_ref[...]
        r = lax.rsqrt(jnp.mean(attn * attn, axis=1, keepdims=True) + NORM_EPS)
        xhat = attn * r
        dg_ref[...] += jnp.sum(dan * xhat, axis=0, keepdims=True)
        dang = dan * g_ref[...]
        d_o = r * (dang - xhat * jnp.mean(dang * xhat, axis=1, keepdims=True))
        delta = jnp.dot(d_o * attn, hs_ref[...], preferred_element_type=F32,
                        precision=lax.Precision.HIGHEST)
        do1[...] = d_o.astype(BF16)
        dl1[...] = delta
        for cb in range(w // 128):
            cols = slice(cb * 128, (cb + 1) * 128)
            sdo[...] = d_o[:, cols]
            sdl[...] = delta[:, cols]
            sls[...] = lse_ref[:, cols]
            for d, o_do, o_dl, o_ls in ((4, do4, dl4, ls4), (16, do16, dl16, ls16)):
                for rr in range(d):
                    rows = pl.ds(rr, t // d, stride=d)
                    o_do[rr, :, cols] = sdo[rows, :].astype(BF16)
                    o_dl[rr, :, cols] = sdl[rows, :]
                    o_ls[rr, :, cols] = sls[rows, :]

    row = _spec((t, w), lambda i: (i, 0))
    perm = _permuted_specs(t, w)
    outs = pl.pallas_call(
        body, name=name, grid=(SEQ // t,),
        in_specs=[row, row, row, _spec((1, w), lambda i: (0, 0)), _spec((w, w), lambda i: (0, 0))],
        out_specs=perm + perm + perm[1:] + [_spec((1, w), lambda i: (0, 0))],
        out_shape=(_permuted_shapes(w, BF16) + _permuted_shapes(w, F32) + _permuted_shapes(w, F32)[1:]
                   + [jax.ShapeDtypeStruct((1, w), F32)]),
        scratch_shapes=[pltpu.VMEM((t, 128), F32)] * 3,
        compiler_params=_params(1),
    )(d_an, attn, lse, gain, head_sum)
    flat = lambda arr: arr.reshape(SEQ, w)
    d_out = [flat(a) for a in outs[0:3]]
    delta = [flat(a) for a in outs[3:6]]
    lses = [lse, flat(outs[6]), flat(outs[7])]
    return d_out, delta, lses, outs[8]


def _attn_bwd(name, q, k, v, d_out, delta, lse, seg_blocks):
    n_blocks = SEQ // SPAN

    def body(q_ref, k_ref, v_ref, do_ref, dl_ref, lse_ref, dq_ref, dk_ref, dv_ref):
        band, is_prev, head0 = _band_masks()
        dk_ref[...] = jnp.zeros_like(dk_ref)
        dv_ref[...] = jnp.zeros_like(dv_ref)

        def per_head(x):
            return jnp.concatenate([x[:, 0:1], x[:, HEAD_DIM:HEAD_DIM + 1]], axis=0)

        def step(b, carry):
            cur = pl.ds(pl.multiple_of(b * SPAN, SPAN), SPAN)
            prev = pl.ds(pl.multiple_of(jnp.maximum(b - 1, 0) * SPAN, SPAN), SPAN)
            qs = _stack_heads(q_ref[cur, :], head0)
            dos = _stack_heads(do_ref[cur, :], head0)
            kcat = jnp.concatenate([k_ref[prev, :], k_ref[cur, :]], axis=0)
            vcat = jnp.concatenate([v_ref[prev, :], v_ref[cur, :]], axis=0)
            ok = band & (((b % seg_blocks) != 0) | ~is_prev)
            p = jnp.where(ok, jnp.exp(_dot_nt(qs, kcat) - per_head(lse_ref[cur, :])), 0.0)
            ds = p * (_dot_nt(dos, vcat) - per_head(dl_ref[cur, :]))
            dq = _dot(ds, kcat)
            dq_ref[cur, :] = jnp.where(head0, dq[:SPAN], dq[SPAN:])
            dk = _dot(ds.T, qs)
            dv = _dot(p.T, dos)
            dk_ref[prev, :] += dk[:SPAN]
            dv_ref[prev, :] += dv[:SPAN]
            dk_ref[cur, :] += dk[SPAN:]
            dv_ref[cur, :] += dv[SPAN:]
            return carry

        lax.fori_loop(0, n_blocks, step, 0, unroll=2)

    col = _spec((SEQ, 128), lambda j: (0, j))
    return pl.pallas_call(
        body, name=name, grid=(ATTN_WIDTH // 128,),
        in_specs=[col] * 6, out_specs=[col] * 3,
        out_shape=[jax.ShapeDtypeStruct((SEQ, ATTN_WIDTH), F32)] * 3,
        compiler_params=_params(1),
    )(q, k, v, d_out, delta, lse)


def _attn_bwd_post(name, grads, cos_t, sin_t):
    t = ROW_TILE
    w = ATTN_WIDTH

    def body(*refs):
        ins, cos_ref, sin_ref, out_ref, s4, s16 = refs[:9], refs[9], refs[10], refs[11], refs[12], refs[13]
        cosv, sinv = cos_ref[...], sin_ref[...]
        for a in range(3):
            g1, g4, g16 = ins[a], ins[3 + a], ins[6 + a]
            for cb in range(w // 128):
                cols = slice(cb * 128, (cb + 1) * 128)
                _unpermute(s4, g4, 4, cols)
                _unpermute(s16, g16, 16, cols)
                val = g1[:, cols] + s4[...] + s16[...]
                if a < 2:
                    val = val * cosv + _swap_halves(val * sinv)
                if a == 0:
                    val = val * (HEAD_DIM ** -0.5)
                out_ref[:, a * w + cb * 128:a * w + (cb + 1) * 128] = val.astype(BF16)

    views = []
    for p, d in enumerate(DILATIONS):
        for a in range(3):
            views.append(grads[p][a] if d == 1 else grads[p][a].reshape(d, SEQ // d, w))
    perm = _permuted_specs(t, w)
    in_specs = [perm[0]] * 3 + [perm[1]] * 3 + [perm[2]] * 3
    return pl.pallas_call(
        body, name=name, grid=(SEQ // t,),
        in_specs=in_specs + [_spec((t, 128), lambda i: (i, 0))] * 2,
        out_specs=_spec((t, 3 * w), lambda i: (i, 0)),
        out_shape=jax.ShapeDtypeStruct((SEQ, 3 * w), BF16),
        scratch_shapes=[pltpu.VMEM((t, 128), F32)] * 2,
        compiler_params=_params(1),
    )(*views, cos_t, sin_t)


N_LEVELS = 7


def _hgrn_consts():
    c = CHUNK
    i = np.arange(c)[:, None]
    s = np.arange(c)[None, :]
    blocks = [s <= i]
    for lv in range(N_LEVELS):
        bs = c >> lv
        h = bs // 2
        m = (i // bs) * bs + h - 1
        second = (i % bs) >= h
        blocks.append((second & (s > m) & (s <= i)) | (~second & (s > i) & (s <= m)))
    blocks.append(s > i)
    stack = np.concatenate(blocks, axis=0).astype(np.float32)
    return jnp.asarray(stack, dtype=BF16), jnp.asarray(stack.T, dtype=BF16)


def _exact_dot(m01, x):
    hi = x.astype(BF16)
    lo = (x - hi.astype(F32)).astype(BF16)
    n = x.shape[1]
    full = jnp.dot(m01, jnp.concatenate([hi, lo], axis=1), preferred_element_type=F32)
    return full[:, :n] + full[:, n:]


def _hgrn_gates(qh, z, lb):
    sq = _sigmoid(qh)
    q = qh * sq * (HGRN_DIM ** -0.5)
    sig = _sigmoid(z)
    sigm = _sigmoid(-z)
    f = lb + (1.0 - lb) * sig
    k = (1.0 - lb) * sigm
    return q, k, f, sq, sig, sigm


def _level_masks(lv):
    row = lax.broadcasted_iota(jnp.int32, (CHUNK, CHUNK), 0)
    col = lax.broadcasted_iota(jnp.int32, (CHUNK, CHUNK), 1)
    shift = N_LEVELS - lv
    second = (row & (CHUNK >> (lv + 1))) != 0
    same = (row >> shift) == (col >> shift)
    return second, same


def _hgrn_fwd(name, proj, lb, gain, stack, mixed, mixed_t):
    t = ROW_TILE
    per = t // CHUNK
    n_rb = SEQ // t
    n_chunks = SEQ // CHUNK
    col0 = 3 * ATTN_WIDTH // 128

    def body(q_ref, f_ref, i_ref, g_ref, lb_ref, gain_ref, stack_ref, mixed_in, mixed_t_in,
             rec_ref, rect_ref, o_ref, st_out, a_out, st):
        del mixed_in, mixed_t_in

        @pl.when(pl.program_id(1) == 0)
        def _():
            st[...] = jnp.zeros_like(st)

        lbv = lb_ref[...]
        row = lax.broadcasted_iota(jnp.int32, (CHUNK, CHUNK), 0)
        col = lax.broadcasted_iota(jnp.int32, (CHUNK, CHUNK), 1)
        for c in range(per):
            rows = slice(c * CHUNK, (c + 1) * CHUNK)
            qh, z, v, gh = q_ref[rows, :], f_ref[rows, :], i_ref[rows, :], g_ref[rows, :]
            q, k, f, _, _, _ = _hgrn_gates(qh, z, lbv)
            dec = _exact_dot(stack_ref[...], jnp.log(f))
            g = dec[0:CHUNK]
            to_end = dec[(N_LEVELS + 1) * CHUNK:(N_LEVELS + 2) * CHUNK]
            a = jnp.where(row == col, jnp.sum(q * k, axis=1, keepdims=True), 0.0)
            for lv in range(N_LEVELS):
                e = jnp.exp(dec[(lv + 1) * CHUNK:(lv + 2) * CHUNK])
                second, same = _level_masks(lv)
                qt = jnp.where(second, q * e, 0.0)
                kt = jnp.where(second, 0.0, k * e)
                a = a + jnp.where(same, _dot_nt(qt, kt), 0.0)
            st_prev = st[...]
            st_out[c] = st_prev
            a_out[c] = a
            o = _dot(a, v) + _dot_nt(q * jnp.exp(g), st_prev)
            k_end = k * jnp.exp(to_end)
            st[...] = st_prev * jnp.exp(g[CHUNK - 1:CHUNK, :]) + _dot(v.T, k_end)
            o_ref[rows, :] = o
            r = lax.rsqrt(jnp.mean(o * o, axis=1, keepdims=True) + NORM_EPS)
            rec = o * r * gain_ref[...] * (gh * _sigmoid(gh))
            rec_ref[rows, :] = rec.astype(BF16)
            rect_ref[:, rows] = rec.T.astype(BF16)

    def col_spec(tt):
        return _spec((t, HGRN_DIM), lambda h, rb: (rb, col0 + HGRN_HEADS * tt + h))

    chunk_spec = _spec((None, per, CHUNK, CHUNK), lambda h, rb: (h, rb, 0, 0))
    return pl.pallas_call(
        body, name=name, grid=(HGRN_HEADS, n_rb),
        in_specs=[col_spec(0), col_spec(1), col_spec(2), col_spec(3),
                  _spec((None, 1, HGRN_DIM), lambda h, rb: (h, 0, 0)),
                  _spec((1, HGRN_DIM), lambda h, rb: (0, 0)),
                  _spec(stack.shape, lambda h, rb: (0, 0)), ANY_SPEC, ANY_SPEC],
        out_specs=[_spec((t, HGRN_DIM), lambda h, rb: (rb, ATTN_WIDTH // HGRN_DIM + h)),
                   _spec((HGRN_DIM, t), lambda h, rb: (ATTN_WIDTH // HGRN_DIM + h, rb)),
                   _spec((t, HGRN_DIM), lambda h, rb: (rb, h)),
                   chunk_spec, chunk_spec],
        out_shape=[jax.ShapeDtypeStruct(mixed.shape, BF16),
                   jax.ShapeDtypeStruct(mixed_t.shape, BF16),
                   jax.ShapeDtypeStruct((SEQ, HGRN_WIDTH), F32),
                   jax.ShapeDtypeStruct((HGRN_HEADS, n_chunks, CHUNK, CHUNK), F32),
                   jax.ShapeDtypeStruct((HGRN_HEADS, n_chunks, CHUNK, CHUNK), F32)],
        scratch_shapes=[pltpu.VMEM((CHUNK, CHUNK), F32)],
        input_output_aliases={7: 0, 8: 1},
        compiler_params=_params(2),
    )(proj, proj, proj, proj, lb, gain, stack, mixed, mixed_t)


def _hgrn_bwd(name, proj, d_rec, o_pre, states, scores, lb, gain, stack, stack_t):
    t = ROW_TILE
    per = t // CHUNK
    n_rb = SEQ // t
    col0 = 3 * ATTN_WIDTH // 128

    def body(q_ref, f_ref, i_ref, g_ref, drec_ref, o_ref, st_ref, a_ref, lb_ref, gain_ref,
             stack_ref, stack_t_ref, dq_ref, df_ref, di_ref, dg_ref, dlb_ref, dgain_ref, dst):
        @pl.when(pl.program_id(1) == 0)
        def _():
            dst[...] = jnp.zeros_like(dst)
            dlb_ref[...] = jnp.zeros_like(dlb_ref)
            dgain_ref[...] = jnp.zeros_like(dgain_ref)

        lbv = lb_ref[...]
        gain_v = gain_ref[...]
        row = lax.broadcasted_iota(jnp.int32, (CHUNK, CHUNK), 0)
        col = lax.broadcasted_iota(jnp.int32, (CHUNK, CHUNK), 1)
        for c in reversed(range(per)):
            rows = slice(c * CHUNK, (c + 1) * CHUNK)
            qh, z, v, gh = q_ref[rows, :], f_ref[rows, :], i_ref[rows, :], g_ref[rows, :]
            q, k, f, sq, sig, sigm = _hgrn_gates(qh, z, lbv)
            dec = _exact_dot(stack_ref[...], jnp.log(f))
            g = dec[0:CHUNK]
            to_end = dec[(N_LEVELS + 1) * CHUNK:(N_LEVELS + 2) * CHUNK]
            e_g = jnp.exp(g)
            e_end = jnp.exp(to_end)
            e_last = jnp.exp(g[CHUNK - 1:CHUNK, :])
            q_in = q * e_g
            k_end = k * e_end
            st_prev = st_ref[c]
            a = a_ref[c]
            dst_new = dst[...]

            o = o_ref[rows, :]
            drec = drec_ref[rows, :]
            sg = _sigmoid(gh)
            r = lax.rsqrt(jnp.mean(o * o, axis=1, keepdims=True) + NORM_EPS)
            ohat = o * r
            d_gh = drec * (ohat * gain_v) * (sg * (1.0 + gh * (1.0 - sg)))
            d_on = drec * (gh * sg)
            dgain_ref[...] += jnp.sum(d_on * ohat, axis=0, keepdims=True)
            d_ohat = d_on * gain_v
            d_o = r * (d_ohat - ohat * jnp.mean(d_ohat * ohat, axis=1, keepdims=True))

            d_a = jnp.where(row >= col, _dot_nt(d_o, v), 0.0)
            d_at = jnp.where(col >= row, _dot_nt(v, d_o), 0.0)
            d_v = _dot(a.T, d_o) + _dot_nt(k_end, dst_new)
            d_q_in = _dot(d_o, st_prev)
            d_k_end = _dot(v, dst_new)
            d_q = d_q_in * e_g
            d_k = d_k_end * e_end
            diag = jnp.sum(d_o * v, axis=1, keepdims=True)
            d_q = d_q + diag * k
            d_k = d_k + diag * q
            d_dec = [q_in * d_q_in]
            for lv in range(N_LEVELS):
                e = jnp.exp(dec[(lv + 1) * CHUNK:(lv + 2) * CHUNK])
                second, same = _level_masks(lv)
                qt = jnp.where(second, q * e, 0.0)
                kt = jnp.where(second, 0.0, k * e)
                d_qt = _dot(jnp.where(same, d_a, 0.0), kt)
                d_kt = _dot(jnp.where(same, d_at, 0.0), qt)
                d_q = d_q + jnp.where(second, d_qt * e, 0.0)
                d_k = d_k + jnp.where(second, 0.0, d_kt * e)
                d_dec.append(jnp.where(second, qt * d_qt, kt * d_kt))
            d_dec.append(k_end * d_k_end)
            flux = jnp.sum(dst_new * st_prev, axis=0, keepdims=True) * e_last
            d_lf = _exact_dot(stack_t_ref[...], jnp.concatenate(d_dec, axis=0)) + flux
            dst[...] = dst_new * e_last + _dot(d_o.T, q_in)

            d_f = d_lf / f - d_k
            dlb_ref[...] += jnp.sum(d_f * sigm, axis=0, keepdims=True)
            dq_ref[rows, :] = (d_q * (HGRN_DIM ** -0.5) * (sq * (1.0 + qh * (1.0 - sq)))).astype(BF16)
            df_ref[rows, :] = (d_f * (1.0 - lbv) * sig * sigm).astype(BF16)
            di_ref[rows, :] = d_v.astype(BF16)
            dg_ref[rows, :] = d_gh.astype(BF16)

    last = n_rb - 1

    def col_spec(tt):
        return _spec((t, HGRN_DIM), lambda h, rb: (last - rb, col0 + HGRN_HEADS * tt + h))

    head_col = _spec((t, HGRN_DIM), lambda h, rb: (last - rb, h))
    rec_col0 = d_rec.shape[1] // HGRN_DIM - HGRN_HEADS
    d_rec_col = _spec((t, HGRN_DIM), lambda h, rb: (last - rb, rec_col0 + h))
    chunk_spec = _spec((None, per, CHUNK, CHUNK), lambda h, rb: (h, last - rb, 0, 0))
    vec_spec = _spec((None, 1, HGRN_DIM), lambda h, rb: (h, 0, 0))
    outs = pl.pallas_call(
        body, name=name, grid=(HGRN_HEADS, n_rb),
        in_specs=[col_spec(0), col_spec(1), col_spec(2), col_spec(3), d_rec_col, head_col,
                  chunk_spec, chunk_spec, vec_spec,
                  _spec((1, HGRN_DIM), lambda h, rb: (0, 0)),
                  _spec(stack.shape, lambda h, rb: (0, 0)), _spec(stack_t.shape, lambda h, rb: (0, 0))],
        out_specs=[head_col] * 4 + [vec_spec, vec_spec],
        out_shape=[jax.ShapeDtypeStruct((SEQ, HGRN_WIDTH), BF16)] * 4
                  + [jax.ShapeDtypeStruct((HGRN_HEADS, 1, HGRN_DIM), F32)] * 2,
        scratch_shapes=[pltpu.VMEM((CHUNK, CHUNK), F32)],
        compiler_params=_params(2),
    )(proj, proj, proj, proj, d_rec, o_pre, states, scores, lb, gain, stack, stack_t)
    return outs


ANY_SPEC = pl.BlockSpec(memory_space=pl.ANY)


def _my_place():
    return lax.axis_index("x"), lax.axis_index("y"), lax.axis_index("c")


def _other_chips(x, y):
    return [(1 - x, y), (x, 1 - y), (1 - x, 1 - y)]


def _remote(src, dst, send_sem, recv_sem, device):
    return pltpu.make_async_remote_copy(src_ref=src, dst_ref=dst, send_sem=send_sem, recv_sem=recv_sem,
                                        device_id=device, device_id_type=MESH)


def _staged_copies(srcs, dsts, stage, sems):
    loads = [pltpu.make_async_copy(srcs[i], stage[i], sems.at[i]) for i in range(len(srcs))]
    for cp in loads:
        cp.start()
    stores = []
    for i, cp in enumerate(loads):
        cp.wait()
        stores.append(pltpu.make_async_copy(stage[i], dsts[i], sems.at[i]))
        stores[-1].start()
    return stores


def _gather_weights(name, shards):
    n = len(shards)

    def body(*refs):
        ins, outs = refs[:n], refs[n:2 * n]
        ici_send, ici_recv, d2d_send, d2d_recv, local_sems = refs[2 * n:2 * n + 5]
        stage = refs[2 * n + 5:]
        x, y, c = _my_place()
        me = 2 * x + y
        chips = _other_chips(x, y)

        def half(i, which):
            h = ins[i].shape[0] // 2
            return pl.ds(which * h, h)

        sends = []
        for i in range(n):
            for j, (px, py) in enumerate(chips):
                sends.append(_remote(ins[i].at[half(i, c), :], outs[i].at[me, half(i, c), :],
                                     ici_send.at[3 * i + j], ici_recv.at[3 * i + j], (px, py, c)))
        for cp in sends:
            cp.start()
        local = _staged_copies(ins, [outs[i].at[me] for i in range(n)], stage, local_sems)
        for i in range(n):
            for j, (px, py) in enumerate(chips):
                landed = outs[i].at[2 * px + py, half(i, c), :]
                _remote(landed, landed, ici_send.at[3 * i + j], ici_recv.at[3 * i + j], (px, py, c)).wait_recv()
                forward = _remote(landed, landed, d2d_send.at[3 * i + j], d2d_recv.at[3 * i + j], (x, y, 1 - c))
                forward.start()
                sends.append(forward)
        for i in range(n):
            for j, (px, py) in enumerate(chips):
                other = outs[i].at[2 * px + py, half(i, 1 - c), :]
                _remote(other, other, d2d_send.at[3 * i + j], d2d_recv.at[3 * i + j], (x, y, 1 - c)).wait_recv()
        for cp in sends:
            cp.wait_send()
        for cp in local:
            cp.wait()

    return pl.pallas_call(
        body, name=name, in_specs=[ANY_SPEC] * n, out_specs=[ANY_SPEC] * n,
        out_shape=[jax.ShapeDtypeStruct((N_CHIPS,) + s.shape, s.dtype) for s in shards],
        scratch_shapes=([pltpu.SemaphoreType.DMA((3 * n,))] * 4 + [pltpu.SemaphoreType.DMA((n,))]
                        + [pltpu.VMEM(s.shape, s.dtype) for s in shards]),
        compiler_params=pltpu.CompilerParams(vmem_limit_bytes=VMEM_LIMIT),
    )(*shards)


def _exchange_halves(name, grads):
    n = len(grads)

    def body(*refs):
        ins, outs = refs[:n], refs[n:2 * n]
        send_sems, recv_sems = refs[2 * n:]
        x, y, c = _my_place()
        copies = []
        for i in range(n):
            h = ins[i].shape[1] // 2
            copies.append(_remote(ins[i].at[:, pl.ds((1 - c) * h, h), :], outs[i],
                                  send_sems.at[i], recv_sems.at[i], (x, y, 1 - c)))
        for cp in copies:
            cp.start()
        for cp in copies:
            cp.wait()

    return pl.pallas_call(
        body, name=name, in_specs=[ANY_SPEC] * n, out_specs=[ANY_SPEC] * n,
        out_shape=[jax.ShapeDtypeStruct((g.shape[0], g.shape[1] // 2, g.shape[2]), g.dtype) for g in grads],
        scratch_shapes=[pltpu.SemaphoreType.DMA((n,)), pltpu.SemaphoreType.DMA((n,))],
    )(*grads)


def _add_own_half(name, g, received, core):
    n_sh, r, cc = g.shape
    h = r // 2
    th = min(h, 256)
    nb = h // th

    def body(core_ref, g_ref, r_ref, o_ref):
        del core_ref
        o_ref[...] = (g_ref[...] + r_ref[...]).astype(BF16)

    grid_spec = pltpu.PrefetchScalarGridSpec(
        num_scalar_prefetch=1, grid=(n_sh, nb),
        in_specs=[pl.BlockSpec((None, th, cc), lambda j, i, core_ref: (j, core_ref[0] * nb + i, 0)),
                  pl.BlockSpec((None, th, cc), lambda j, i, core_ref: (j, i, 0))],
        out_specs=pl.BlockSpec((None, th, cc), lambda j, i, core_ref: (j, i, 0)))
    return pl.pallas_call(
        body, name=name, grid_spec=grid_spec,
        out_shape=jax.ShapeDtypeStruct((n_sh, h, cc), BF16), compiler_params=_params(2),
    )(core, g, received)


def _exchange_chips(name, parts):
    n = len(parts)

    def body(*refs):
        ins, outs = refs[:n], refs[n:2 * n]
        send_sems, recv_sems, local_sems = refs[2 * n:2 * n + 3]
        stage = refs[2 * n + 3:]
        x, y, c = _my_place()
        me = 2 * x + y
        chips = _other_chips(x, y)
        sends = []
        for i in range(n):
            for j, (px, py) in enumerate(chips):
                sends.append(_remote(ins[i].at[2 * px + py], outs[i].at[me], send_sems.at[3 * i + j],
                                     recv_sems.at[3 * i + j], (px, py, c)))
        for cp in sends:
            cp.start()
        local = _staged_copies([ins[i].at[me] for i in range(n)], [outs[i].at[me] for i in range(n)],
                               stage, local_sems)
        for i in range(n):
            for j, (px, py) in enumerate(chips):
                _remote(ins[i].at[me], outs[i].at[2 * px + py], send_sems.at[3 * i + j],
                        recv_sems.at[3 * i + j], (px, py, c)).wait_recv()
        for cp in sends:
            cp.wait_send()
        for cp in local:
            cp.wait()

    return pl.pallas_call(
        body, name=name, in_specs=[ANY_SPEC] * n, out_specs=[ANY_SPEC] * n,
        out_shape=[jax.ShapeDtypeStruct(p.shape, p.dtype) for p in parts],
        scratch_shapes=([pltpu.SemaphoreType.DMA((3 * n,)), pltpu.SemaphoreType.DMA((3 * n,)),
                         pltpu.SemaphoreType.DMA((n,))]
                        + [pltpu.VMEM(p.shape[1:], p.dtype) for p in parts]),
        compiler_params=pltpu.CompilerParams(vmem_limit_bytes=VMEM_LIMIT),
    )(*parts)


HBM_SPEC = pl.BlockSpec(memory_space=pltpu.HBM)
SEM_SPEC = pl.BlockSpec(memory_space=pltpu.SEMAPHORE)
SPLIT_PARAMS = pltpu.CompilerParams(has_side_effects=pltpu.SideEffectType.DATAFLOW_SIDE_EFFECTING)


def _chip_copies(ins, lands, send_sems, recv_sems, sliced):
    x, y, c = _my_place()
    me = 2 * x + y
    pairs = []
    for i in range(len(ins)):
        for j, (px, py) in enumerate(_other_chips(x, y)):
            theirs = 2 * px + py
            src = ins[i].at[theirs] if sliced else ins[i]
            sems = (send_sems.at[3 * i + j], recv_sems.at[3 * i + j], (px, py, c))
            pairs.append((_remote(src, lands[i].at[me], *sems), _remote(src, lands[i].at[theirs], *sems)))
    return pairs


def _exchange_start(name, srcs, lands, sliced, after):
    n = len(srcs)

    def body(*refs):
        ins, land_refs = refs[:n], refs[n:2 * n]
        send_sems, recv_sems = refs[2 * n + 1:2 * n + 3]
        token = refs[-1]
        for send, _ in _chip_copies(ins, land_refs, send_sems, recv_sems, sliced):
            send.start()
        token[...] = jnp.zeros_like(token)

    arrays = list(srcs) + list(lands)
    outs = pl.pallas_call(
        body, name=name,
        in_specs=[HBM_SPEC] * (2 * n) + [ANY_SPEC],
        out_shape=([pltpu.SemaphoreType.DMA((3 * n,))] * 2 + [pltpu.HBM(a.shape, a.dtype) for a in arrays]
                   + [jax.ShapeDtypeStruct((8, 128), F32)]),
        out_specs=[SEM_SPEC] * 2 + [HBM_SPEC] * (2 * n) + [pl.BlockSpec(memory_space=pltpu.VMEM)],
        input_output_aliases={i: 2 + i for i in range(2 * n)},
        compiler_params=SPLIT_PARAMS,
    )(*[pltpu.with_memory_space_constraint(a, pltpu.HBM) for a in arrays], after)
    return outs[:2], outs[2:2 + 2 * n], outs[-1]


def _exchange_wait(name, sems, passed, sliced, after):
    n = len(passed) // 2

    def body(*refs):
        ins, land_refs = refs[:n], refs[n:2 * n]
        send_sems, recv_sems = refs[2 * n:2 * n + 2]
        for send, arrive in _chip_copies(ins, land_refs, send_sems, recv_sems, sliced):
            send.wait_send()
            arrive.wait_recv()

    outs = pl.pallas_call(
        body, name=name,
        in_specs=[HBM_SPEC] * (2 * n) + [SEM_SPEC] * 2 + [ANY_SPEC],
        out_shape=[pltpu.HBM(a.shape, a.dtype) for a in passed],
        out_specs=[HBM_SPEC] * (2 * n),
        input_output_aliases={i: i for i in range(2 * n)},
        compiler_params=SPLIT_PARAMS,
    )(*passed, *sems, after)
    return outs[n:]


def _own_slot(name, own, me):
    r, cc = own.shape[-2:]
    th = min(r, 512)

    def body(me_ref, x_ref, o_ref):
        del me_ref
        o_ref[...] = x_ref[...]

    if own.ndim == 3:
        in_spec = pl.BlockSpec((None, th, cc), lambda i, me_ref: (me_ref[0], i, 0))
    else:
        in_spec = pl.BlockSpec((th, cc), lambda i, me_ref: (i, 0))
    grid_spec = pltpu.PrefetchScalarGridSpec(
        num_scalar_prefetch=1, grid=(r // th,), in_specs=[in_spec],
        out_specs=pl.BlockSpec((None, th, cc), lambda i, me_ref: (me_ref[0], i, 0)))
    return pl.pallas_call(
        body, name=name, grid_spec=grid_spec,
        out_shape=jax.ShapeDtypeStruct((N_CHIPS, r, cc), own.dtype), compiler_params=_params(1),
    )(me, own)


def _sum_chips(name, parts):
    n_sh, h, cc = parts.shape
    th = min(h, 256)

    def body(p_ref, o_ref):
        p = [p_ref[j].astype(F32) for j in range(n_sh)]
        o_ref[...] = ((p[0] + p[1]) + p[2]) + p[3]

    return pl.pallas_call(
        body, name=name, grid=(h // th,),
        in_specs=[_spec((n_sh, th, cc), lambda i: (0, i, 0))],
        out_specs=_spec((th, cc), lambda i: (i, 0)),
        out_shape=jax.ShapeDtypeStruct((h, cc), F32), compiler_params=_params(1),
    )(parts)


def _share_halves(halves):
    flat = [t for per_weight in halves for t in per_weight]
    n = len(flat)
    n_w = len(halves)

    def body(*refs):
        ins, outs = refs[:n], refs[n:n + n_w]
        send_sems, recv_sems, local_sems = refs[n + n_w:n + n_w + 3]
        stage = refs[n + n_w + 3:]
        x, y, c = _my_place()
        sends, own = [], []
        for i in range(n):
            w, l = divmod(i, DEPTH)
            h = ins[i].shape[0]
            own.append(outs[w].at[l, pl.ds(c * h, h), :])
            sends.append(_remote(ins[i], own[i], send_sems.at[i], recv_sems.at[i], (x, y, 1 - c)))
        for cp in sends:
            cp.start()
        local = _staged_copies(ins, own, stage, local_sems)
        for i in range(n):
            w, l = divmod(i, DEPTH)
            h = ins[i].shape[0]
            _remote(ins[i], outs[w].at[l, pl.ds((1 - c) * h, h), :], send_sems.at[i], recv_sems.at[i],
                    (x, y, 1 - c)).wait_recv()
        for cp in sends:
            cp.wait_send()
        for cp in local:
            cp.wait()

    return pl.pallas_call(
        body, name="share_halves", in_specs=[ANY_SPEC] * n, out_specs=[ANY_SPEC] * n_w,
        out_shape=[jax.ShapeDtypeStruct((DEPTH, 2 * per_weight[0].shape[0], per_weight[0].shape[1]), F32)
                   for per_weight in halves],
        scratch_shapes=([pltpu.SemaphoreType.DMA((n,))] * 3 + [pltpu.VMEM(t.shape, t.dtype) for t in flat]),
        compiler_params=pltpu.CompilerParams(vmem_limit_bytes=VMEM_LIMIT),
    )(*flat)


def _all_reduce_small(pack):
    def body(p_ref, o_ref, recv, send_sems, recv_sems):
        x, y, c = _my_place()
        me = 4 * x + 2 * y + c
        recv[me] = p_ref[...]
        peers = []
        for k in range(1, N_DEV):
            px, py, pc = (x + (k >> 2)) % 2, (y + ((k >> 1) & 1)) % 2, (c + (k & 1)) % 2
            peers.append((px, py, pc))
        sends = [_remote(p_ref, recv.at[me], send_sems.at[k], recv_sems.at[k], peer)
                 for k, peer in enumerate(peers)]
        for cp in sends:
            cp.start()
        for k, (px, py, pc) in enumerate(peers):
            _remote(p_ref, recv.at[4 * px + 2 * py + pc], send_sems.at[k], recv_sems.at[k],
                    (px, py, pc)).wait_recv()
        for cp in sends:
            cp.wait_send()
        total = recv[0]
        for d in range(1, N_DEV):
            total = total + recv[d]
        o_ref[...] = total

    vmem = pl.BlockSpec(memory_space=pltpu.VMEM)
    return pl.pallas_call(
        body, name="all_reduce_small", in_specs=[vmem], out_specs=vmem,
        out_shape=jax.ShapeDtypeStruct(pack.shape, F32),
        scratch_shapes=[pltpu.VMEM((N_DEV,) + pack.shape, F32),
                        pltpu.SemaphoreType.DMA((N_DEV - 1,)), pltpu.SemaphoreType.DMA((N_DEV - 1,))],
    )(pack)


def _adamw(name, w, g, m, v):
    r, cc = w.shape
    th = min(r, 256)

    def body(w_ref, g_ref, m_ref, v_ref, d_ref, m_out, v_out):
        gv = g_ref[...]
        m2 = ADAM_B1 * m_ref[...] + (1.0 - ADAM_B1) * gv
        v2 = ADAM_B2 * v_ref[...] + (1.0 - ADAM_B2) * (gv * gv)
        m_hat = m2 / (1.0 - ADAM_B1 ** ADAM_STEP)
        v_hat = v2 / (1.0 - ADAM_B2 ** ADAM_STEP)
        d_ref[...] = -ADAM_LR * (m_hat / (jnp.sqrt(v_hat) + ADAM_EPS) + ADAM_WD * w_ref[...])
        m_out[...] = m2
        v_out[...] = v2

    tile = _spec((th, cc), lambda i: (i, 0))
    return pl.pallas_call(
        body, name=name, grid=(r // th,), in_specs=[tile] * 4, out_specs=[tile] * 3,
        out_shape=[jax.ShapeDtypeStruct((r, cc), F32)] * 3, compiler_params=_params(1),
    )(w, g, m, v)


def _lower_bounds(lb_logits):
    p = jax.nn.softmax(lb_logits.astype(F32), axis=0)
    return jnp.cumsum(p, axis=0) - p[0]


def _row_tile_specs(tm, width):
    return _spec((tm, width), lambda i, j, k: (i, 0))


def _layer_forward(l, x_in, small, weights, consts, after=None):
    win, rest = weights
    cos_t, sin_t, stack, _, _ = consts
    tm = MM_TILE
    n_row = SEQ // tm
    saved = {"x_in": x_in}

    h, h_t = _rms_fwd(f"norm_mix{l}", x_in, small["norm_mix"][l][None, :], after=after)
    proj = _matmul(f"proj{l}", h, win,
                   _spec((tm, D_MODEL), lambda i, j, k: (i, 0)),
                   _spec((None, D_MODEL, SHARD_IN), lambda i, j, k: (j, 0, 0)),
                   (SEQ, IN_W), F32, _spec((tm, SHARD_IN), lambda i, j, k: (i, j)),
                   (n_row, N_CHIPS, 1), (tm, SHARD_IN))
    saved.update(h_t=h_t, proj=proj)

    qkv = _attn_prep(f"attn_prep{l}", proj, cos_t, sin_t)
    outs, lses = [], []
    for p, d in enumerate(DILATIONS):
        o, lse = _attn_fwd(f"attn_fwd{l}_{d}", *qkv[p], SEQ // d // SPAN)
        outs.append(o)
        lses.append(lse)
    mixed, mixed_t, attn, lse = _attn_merge(f"attn_merge{l}", outs, lses, small["attn_out_gain"][l][None, :])
    saved.update(qkv=qkv, attn=attn, lse=lse)

    lb3 = small["lower"][l].reshape(HGRN_HEADS, 1, HGRN_DIM)
    mixed, mixed_t, o_pre, states, scores = _hgrn_fwd(f"hgrn_fwd{l}", proj, lb3, small["hgrn_out_gain"][l][None, :],
                                                      stack, mixed, mixed_t)
    wo, wu, wd = rest(mixed)
    saved.update(mixed_t=mixed_t, o_pre=o_pre, states=states, scores=scores, lb3=lb3, weights=(win, wo, wu, wd))

    x_mid = _matmul(f"out_proj{l}", mixed, wo,
                    _spec((tm, SHARD_OUT), lambda i, j, k: (i, k)),
                    _spec((None, SHARD_OUT, D_MODEL), lambda i, j, k: (k, 0, 0)),
                    (SEQ, D_MODEL), F32, _spec((tm, D_MODEL), lambda i, j, k: (i, 0)),
                    (n_row, 1, N_CHIPS), (tm, D_MODEL),
                    extra=x_in, extra_spec=_spec((tm, D_MODEL), lambda i, j, k: (i, 0)), epilogue="add")
    saved["x_mid"] = x_mid

    h2, h2_t = _rms_fwd(f"norm_mlp{l}", x_mid, small["norm_mlp"][l][None, :])
    a, relu_u, a_t = _matmul(
        f"up{l}", h2, wu,
        _spec((tm, D_MODEL), lambda i, j, k: (i, 0)),
        _spec((None, D_MODEL, SHARD_MLP), lambda i, j, k: (j, 0, 0)),
        (SEQ, MLP_HIDDEN), BF16, _spec((tm, SHARD_MLP), lambda i, j, k: (i, j)),
        (n_row, N_CHIPS, 1), (tm, SHARD_MLP), epilogue="relu2",
        relu_outs=[((SEQ, MLP_HIDDEN), _spec((tm, SHARD_MLP), lambda i, j, k: (i, j))),
                   ((MLP_HIDDEN, SEQ), _spec((SHARD_MLP, tm), lambda i, j, k: (j, i)))])
    x_out = _matmul(f"down{l}", a, wd,
                    _spec((tm, SHARD_MLP), lambda i, j, k: (i, k)),
                    _spec((None, SHARD_MLP, D_MODEL), lambda i, j, k: (k, 0, 0)),
                    (SEQ, D_MODEL), F32, _spec((tm, D_MODEL), lambda i, j, k: (i, 0)),
                    (n_row, 1, N_CHIPS), (tm, D_MODEL),
                    extra=x_mid, extra_spec=_spec((tm, D_MODEL), lambda i, j, k: (i, 0)), epilogue="add")
    saved.update(h2_t=h2_t, relu_u=relu_u, a_t=a_t)
    return x_out, saved


def _layer_backward(l, dx, saved, small, consts, on_grads, after=None):
    win, wo, wu, wd = saved["weights"]
    cos_t, sin_t, stack, stack_t, head_sum = consts
    tm = MM_TILE
    n_row = SEQ // tm
    n_k = SEQ // tm

    dx, dx_b = dx
    du = _matmul(f"d_u{l}", dx_b, wd,
                 _spec((tm, D_MODEL), lambda i, j, k: (i, 0)),
                 _spec((None, SHARD_MLP, D_MODEL), lambda i, j, k: (j, 0, 0)),
                 (SEQ, MLP_HIDDEN), BF16, _spec((tm, SHARD_MLP), lambda i, j, k: (i, j)),
                 (n_row, N_CHIPS, 1), (tm, SHARD_MLP), nt=True,
                 extra=saved["relu_u"], extra_spec=_spec((tm, SHARD_MLP), lambda i, j, k: (i, j)),
                 epilogue="relu2_grad", after=after)
    d_wd = _matmul(f"d_wdown{l}", saved["a_t"], dx_b,
                   _spec((SHARD_MLP, tm), lambda i, j, k: (i, k)),
                   _spec((tm, D_MODEL), lambda i, j, k: (k, 0)),
                   (N_CHIPS, SHARD_MLP, D_MODEL), F32, _spec((None, SHARD_MLP, D_MODEL), lambda i, j, k: (i, 0, 0)),
                   (N_CHIPS, 1, n_k), (SHARD_MLP, D_MODEL))
    dh2 = _matmul(f"d_h2_{l}", du, wu,
                  _spec((tm, SHARD_MLP), lambda i, j, k: (i, k)),
                  _spec((None, D_MODEL, SHARD_MLP), lambda i, j, k: (k, 0, 0)),
                  (SEQ, D_MODEL), F32, _spec((tm, D_MODEL), lambda i, j, k: (i, 0)),
                  (n_row, 1, N_CHIPS), (tm, D_MODEL), nt=True)
    d_wu = _matmul(f"d_wup{l}", saved["h2_t"], du,
                   _spec((D_MODEL, tm), lambda i, j, k: (0, k)),
                   _spec((tm, SHARD_MLP), lambda i, j, k: (k, j)),
                   (N_CHIPS, D_MODEL, SHARD_MLP), F32, _spec((None, D_MODEL, SHARD_MLP), lambda i, j, k: (j, 0, 0)),
                   (1, N_CHIPS, n_k), (D_MODEL, SHARD_MLP))
    dxm, dxm_b, dg_mlp = _rms_bwd(f"norm_mlp_bwd{l}", dh2, saved["x_mid"], small["norm_mlp"][l][None, :], dx)
    after_mlp = on_grads(l, "mlp", (d_wu, d_wd))

    d_mixed = _matmul(f"d_mixed{l}", dxm_b, wo,
                      _spec((tm, D_MODEL), lambda i, j, k: (i, 0)),
                      _spec((None, SHARD_OUT, D_MODEL), lambda i, j, k: (j, 0, 0)),
                      (SEQ, D_MODEL), F32, _spec((tm, SHARD_OUT), lambda i, j, k: (i, j)),
                      (n_row, N_CHIPS, 1), (tm, SHARD_OUT), nt=True, after=after_mlp)
    d_wo = _matmul(f"d_wout{l}", saved["mixed_t"], dxm_b,
                   _spec((SHARD_OUT, tm), lambda i, j, k: (i, k)),
                   _spec((tm, D_MODEL), lambda i, j, k: (k, 0)),
                   (N_CHIPS, SHARD_OUT, D_MODEL), F32, _spec((None, SHARD_OUT, D_MODEL), lambda i, j, k: (i, 0, 0)),
                   (N_CHIPS, 1, n_k), (SHARD_OUT, D_MODEL))
    d_rec = d_mixed

    d_out, delta, lses, dg_attn = _attn_bwd_prep(f"attn_bwd_prep{l}", d_mixed, saved["attn"], saved["lse"],
                                                 small["attn_out_gain"][l][None, :], head_sum)
    grads = []
    for p, d in enumerate(DILATIONS):
        grads.append(_attn_bwd(f"attn_bwd{l}_{d}", *saved["qkv"][p], d_out[p], delta[p], lses[p],
                               SEQ // d // SPAN))
    dp_attn = _attn_bwd_post(f"attn_bwd_post{l}", grads, cos_t, sin_t)

    dq_h, df_h, di_h, dg_h, d_lower, dg_hgrn = _hgrn_bwd(
        f"hgrn_bwd{l}", saved["proj"], d_rec, saved["o_pre"], saved["states"], saved["scores"],
        saved["lb3"], small["hgrn_out_gain"][l][None, :], stack, stack_t)
    dproj = jnp.concatenate([dp_attn, dq_h, df_h, di_h, dg_h], axis=1)

    dh = _matmul(f"d_h{l}", dproj, win,
                 _spec((tm, SHARD_IN), lambda i, j, k: (i, k)),
                 _spec((None, D_MODEL, SHARD_IN), lambda i, j, k: (k, 0, 0)),
                 (SEQ, D_MODEL), F32, _spec((tm, D_MODEL), lambda i, j, k: (i, 0)),
                 (n_row, 1, N_CHIPS), (tm, D_MODEL), nt=True)
    d_win = _matmul(f"d_win{l}", saved["h_t"], dproj,
                    _spec((D_MODEL, tm), lambda i, j, k: (0, k)),
                    _spec((tm, SHARD_IN), lambda i, j, k: (k, j)),
                    (N_CHIPS, D_MODEL, SHARD_IN), F32, _spec((None, D_MODEL, SHARD_IN), lambda i, j, k: (j, 0, 0)),
                    (1, N_CHIPS, n_k), (D_MODEL, SHARD_IN))
    dx_in, dx_in_b, dg_mix = _rms_bwd(f"norm_mix_bwd{l}", dh, saved["x_in"], small["norm_mix"][l][None, :], dxm)

    small_grads = {"norm_mix": dg_mix[0], "attn_out_gain": dg_attn[0],
                   "lower": d_lower.reshape(HGRN_WIDTH),
                   "hgrn_out_gain": jnp.sum(dg_hgrn, axis=0).reshape(HGRN_DIM), "norm_mlp": dg_mlp[0]}
    return (dx_in, dx_in_b), on_grads(l, "mix", (d_win, d_wo)), small_grads


def _local_step(xs, target, small, get_weights, on_grads):
    consts = _rope_tables() + _hgrn_consts() + (_head_sum_matrix(),)
    stream = xs
    saved = []
    for l in range(DEPTH):
        w, after = get_weights(l, stream)
        stream, s = _layer_forward(l, stream, small, w, consts, after=after)
        saved.append(s)
    dx_f, dx_b, dg_final, loss = _loss_head(stream, small["norm_final"][None, :], target)
    dx = (dx_f, dx_b)
    small_grads = [None] * DEPTH
    after = None
    for l in reversed(range(DEPTH)):
        dx, after, small_grads[l] = _layer_backward(l, dx, saved[l], small, consts, on_grads, after=after)
    return loss, dx[0], dg_final[0], small_grads


def _pack_small(norm_mix, attn_out_gain, lb, hgrn_out_gain, norm_mlp, norm_final, last_row):
    rows = [norm_mix, attn_out_gain.reshape(1, D_MODEL), lb.reshape(1, D_MODEL),
            jnp.pad(hgrn_out_gain.reshape(1, DEPTH * HGRN_DIM), ((0, 0), (0, D_MODEL - DEPTH * HGRN_DIM))),
            norm_mlp, norm_final.reshape(1, D_MODEL), last_row.reshape(1, D_MODEL)]
    pack = jnp.concatenate(rows, axis=0)
    return jnp.pad(pack, ((0, PACK_ROWS - pack.shape[0]), (0, 0)))


def _unpack_small(pack):
    return (pack[0:2], pack[2].reshape(DEPTH, ATTN_WIDTH), pack[3].reshape(DEPTH, HGRN_WIDTH),
            pack[4, :DEPTH * HGRN_DIM].reshape(DEPTH, HGRN_DIM), pack[5:7], pack[7], pack[8])


def kernel(x, norm_mix, w_in, attn_out_gain, hgrn_lb_logits, hgrn_out_gain, w_out, norm_mlp, w_up, w_down, norm_final, loss_target, m_norm_mix, m_w_in, m_attn_out_gain, m_hgrn_lb_logits, m_hgrn_out_gain, m_w_out, m_norm_mlp, m_w_up, m_w_down, m_norm_final, v_norm_mix, v_w_in, v_attn_out_gain, v_hgrn_lb_logits, v_hgrn_out_gain, v_w_out, v_norm_mlp, v_w_up, v_w_down, v_norm_final):
    core = lax.axis_index("c").astype(jnp.int32).reshape(1)
    lower, lower_vjp = jax.vjp(_lower_bounds, hgrn_lb_logits)
    small = {"norm_mix": norm_mix, "attn_out_gain": attn_out_gain, "lower": lower,
             "hgrn_out_gain": hgrn_out_gain, "norm_mlp": norm_mlp, "norm_final": norm_final}
    big_w = (w_in, w_out, w_up, w_down)

    me = (2 * lax.axis_index("x") + lax.axis_index("y")).astype(jnp.int32).reshape(1)
    shards =[[w[l].astype(BF16) for w in big_w] for l in range(DEPTH)]
    in_flight = {}

    def start_gather(name, some, after):
        lands = [_own_slot(f"own_{name}_{i}", s, me) for i, s in enumerate(some)]
        sems, passed, token = _exchange_start(name, some, lands, False, after)
        return (sems, passed), token

    def get_weights(l, stream):
        if l == 0:
            (win,) = _gather_weights("gather_w_in0", shards[0][:1])
            in_flight["rest0"], token = start_gather("gather_start0", shards[0][1:], win)
            in_flight["weights1"], token = start_gather("gather_start1", shards[1], token)
            return (win, lambda after: _exchange_wait("gather_wait0", *in_flight.pop("rest0"), False, after)), token
        weights = _exchange_wait("gather_wait1", *in_flight.pop("weights1"), False, stream)
        return (weights[0], lambda after: weights[1:]), None

    reduced = {}

    def start_exchange(name, grads):
        received = _exchange_halves(f"halves_{name}", grads)
        halves = [_add_own_half(f"add_{name}_{i}", g, r, core) for i, (g, r) in enumerate(zip(grads, received))]
        lands = [_own_slot(f"own_{name}_{i}", h, me) for i, h in enumerate(halves)]
        sems, passed, token = _exchange_start(f"start_{name}", halves, lands, True, halves[0])
        in_flight[name] = (sems, passed)
        return token

    def finish_exchange(name, after):
        landed = _exchange_wait(f"wait_{name}", *in_flight.pop(name), True, after)
        return [_sum_chips(f"sum_{name}_{i}", p) for i, p in enumerate(landed)]

    def on_grads(l, group, grads):
        if (l, group) == (1, "mlp"):
            return start_exchange("mlp1", grads)
        if (l, group) == (1, "mix"):
            return start_exchange("mix1", grads)
        if (l, group) == (0, "mlp"):
            reduced[(1, "mlp")] = finish_exchange("mlp1", grads[0])
            reduced[(1, "mix")] = finish_exchange("mix1", grads[0])
            return start_exchange("mlp0", grads)
        received = _exchange_halves("halves_mix0", grads)
        halves = [_add_own_half(f"add_mix0_{i}", g, r, core) for i, (g, r) in enumerate(zip(grads, received))]
        from_chips = _exchange_chips("exchange_chips_mix0", halves)
        reduced[(0, "mix")] = [_sum_chips(f"sum_mix0_{i}", p) for i, p in enumerate(from_chips)]
        reduced[(0, "mlp")] = finish_exchange("mlp0", from_chips[0])
        return None

    loss, dx, dg_final, sg = _local_step(x[0], loss_target[0], small, get_weights, on_grads)

    stack2 = lambda key: jnp.stack([sg[l][key] for l in range(DEPTH)])
    pack = _pack_small(stack2("norm_mix"), stack2("attn_out_gain"), stack2("lower"), stack2("hgrn_out_gain"),
                       stack2("norm_mlp"), dg_final, jnp.broadcast_to(loss[0, 0], (D_MODEL,)))
    g_mix, g_attn, g_lower, g_hgrn, g_mlp, g_final, loss_row = _unpack_small(_all_reduce_small(pack))
    (g_logits,) = lower_vjp(g_lower)

    zeros_row = jnp.zeros((D_MODEL,), F32)
    small_w = (norm_mix, attn_out_gain, hgrn_lb_logits, hgrn_out_gain, norm_mlp, norm_final)
    small_m = (m_norm_mix, m_attn_out_gain, m_hgrn_lb_logits, m_hgrn_out_gain, m_norm_mlp, m_norm_final)
    small_v = (v_norm_mix, v_attn_out_gain, v_hgrn_lb_logits, v_hgrn_out_gain, v_norm_mlp, v_norm_final)
    small_g = (g_mix, g_attn, g_logits, g_hgrn, g_mlp, g_final)
    packs = [_pack_small(*t, zeros_row) for t in (small_w, small_g, small_m, small_v)]
    small_delta, small_new_m, small_new_v = [_unpack_small(p)[:6] for p in _adamw("adamw_small", *packs)]

    big_g = _share_halves([[reduced[(l, group)][i] for l in range(DEPTH)]
                           for group, i in (("mix", 0), ("mix", 1), ("mlp", 0), ("mlp", 1))])

    big_m = (m_w_in, m_w_out, m_w_up, m_w_down)
    big_v = (v_w_in, v_w_out, v_w_up, v_w_down)
    big_delta, big_new_m, big_new_v = [], [], []
    for i, name in enumerate(("w_in", "w_out", "w_up", "w_down")):
        shape = big_w[i].shape
        flat = lambda arr: arr.reshape(shape[0] * shape[1], shape[2])
        d, m2, v2 = _adamw(f"adamw_{name}", flat(big_w[i]), flat(big_g[i]), flat(big_m[i]), flat(big_v[i]))
        big_delta.append(d.reshape(shape))
        big_new_m.append(m2.reshape(shape))
        big_new_v.append(v2.reshape(shape))

    def ordered(small6, big4):
        mix, attn, lbl, hg, mlp, fin = small6
        return (mix, big4[0], attn, lbl, hg, big4[1], mlp, big4[2], big4[3], fin)

    return ((loss_row[0], dx[None]) + ordered(small_g, big_g) + ordered(small_delta, big_delta)
            + ordered(small_new_m, big_new_m) + ordered(small_new_v, big_new_v))
```

```python
import functools
import math

import numpy as np
import jax
import jax.numpy as jnp
from jax import lax
from jax.experimental import pallas as pl
from jax.experimental.pallas import tpu as pltpu

F32 = jnp.float32
BF16 = jnp.bfloat16
MESH = pl.DeviceIdType.MESH

SEQ = 4096
D_MODEL = 1024
DEPTH = 2
ATTN_WIDTH = 512
HEAD_DIM = 64
HGRN_HEADS = 4
HGRN_DIM = 128
HGRN_WIDTH = 512
IN_W = 3584
MLP_HIDDEN = 4096
N_CHIPS = 4
N_DEV = 8
SHARD_IN = IN_W // N_CHIPS
SHARD_OUT = D_MODEL // N_CHIPS
SHARD_MLP = MLP_HIDDEN // N_CHIPS
DILATIONS = (1, 4, 16)
SPAN = 128
ROPE_THETA = 10000.0
NORM_EPS = 1e-6
MASK_VALUE = -1e30
CHUNK = 128
ROW_TILE = 512
MM_TILE = 1024
VMEM_LIMIT = 52 * 1024 * 1024

ADAM_LR = 0.001
ADAM_B1 = 0.9
ADAM_B2 = 0.999
ADAM_EPS = 1e-08
ADAM_WD = 0.01
ADAM_STEP = 10

PACK_ROWS = 16


def _params(n_axes):
    return pltpu.CompilerParams(dimension_semantics=("arbitrary",) * n_axes,
                                vmem_limit_bytes=VMEM_LIMIT)


def _dot(a, b):
    return jnp.dot(a.astype(BF16), b.astype(BF16), preferred_element_type=F32)


def _dot_nt(a, b):
    return lax.dot_general(a.astype(BF16), b.astype(BF16), (((1,), (1,)), ((), ())),
                           preferred_element_type=F32)


def _dot_tn(a, b):
    return lax.dot_general(a.astype(BF16), b.astype(BF16), (((0,), (0,)), ((), ())),
                           preferred_element_type=F32)


def _sigmoid(x):
    return 1.0 / (1.0 + jnp.exp(-x))


def _matmul(name, a, b, a_spec, b_spec, out_shape, out_dtype, out_spec, grid, acc_shape,
            nt=False, extra=None, extra_spec=None, epilogue="none", after=None, relu_outs=None):
    nk = grid[2]
    n_out = 1 if relu_outs is None else 3

    def body(*refs):
        a_ref, b_ref = refs[:2]
        e_ref = None if extra is None else refs[2]
        o_ref = refs[-1 - n_out]
        acc = refs[-1]
        kk = pl.program_id(2)

        def product():
            return _dot_nt(a_ref[...], b_ref[...]) if nt else _dot(a_ref[...], b_ref[...])

        if nk > 1:
            @pl.when(kk == 0)
            def _():
                acc[...] = jnp.zeros_like(acc)

            acc[...] += product()

        @pl.when(kk == nk - 1)
        def _():
            r = acc[...] if nk > 1 else product()
            if epilogue == "add":
                r = r + e_ref[...]
            elif epilogue == "relu2_grad":
                r = r * (2.0 * e_ref[...].astype(F32))
            elif epilogue == "relu2":
                s = jnp.maximum(r, 0.0)
                r = s * s
                refs[-3][...] = s.astype(out_dtype)
                refs[-2][...] = r.T.astype(out_dtype)
            o_ref[...] = r.astype(o_ref.dtype)

    in_specs = [a_spec, b_spec] + ([] if extra is None else [extra_spec])
    args = (a, b) + (() if extra is None else (extra,))
    if after is not None:
        in_specs.append(pl.BlockSpec(memory_space=pl.ANY))
        args += (after,)
    out_specs, out_shapes = out_spec, jax.ShapeDtypeStruct(out_shape, out_dtype)
    if relu_outs is not None:
        out_specs = [out_spec] + [spec for _, spec in relu_outs]
        out_shapes = [out_shapes] + [jax.ShapeDtypeStruct(shape, out_dtype) for shape, _ in relu_outs]
    return pl.pallas_call(
        body, name=name, grid=grid, in_specs=in_specs, out_specs=out_specs, out_shape=out_shapes,
        scratch_shapes=[pltpu.VMEM(acc_shape, F32)],
        compiler_params=_params(3),
    )(*args)


def _spec(shape, index_map):
    return pl.BlockSpec(shape, index_map)


def _rms_fwd(name, x, gain, after=None):
    s, d = x.shape
    t = ROW_TILE

    def body(x_ref, g_ref, *rest):
        h_ref, ht_ref = rest[-2:]
        xv = x_ref[...]
        r = lax.rsqrt(jnp.mean(xv * xv, axis=1, keepdims=True) + NORM_EPS)
        h = xv * r * g_ref[...]
        h_ref[...] = h.astype(BF16)
        ht_ref[...] = h.T.astype(BF16)

    in_specs = [_spec((t, d), lambda i: (i, 0)), _spec((1, d), lambda i: (0, 0))]
    args = (x, gain)
    if after is not None:
        in_specs.append(pl.BlockSpec(memory_space=pl.ANY))
        args += (after,)
    return pl.pallas_call(
        body, name=name, grid=(s // t,), in_specs=in_specs,
        out_specs=[_spec((t, d), lambda i: (i, 0)), _spec((d, t), lambda i: (0, i))],
        out_shape=[jax.ShapeDtypeStruct((s, d), BF16), jax.ShapeDtypeStruct((d, s), BF16)],
        compiler_params=_params(1),
    )(*args)


def _rms_bwd(name, dh, x, gain, dres):
    s, d = x.shape
    t = ROW_TILE

    def body(dh_ref, x_ref, g_ref, dres_ref, dx_ref, dxb_ref, dg_ref):
        @pl.when(pl.program_id(0) == 0)
        def _():
            dg_ref[...] = jnp.zeros_like(dg_ref)

        xv = x_ref[...]
        dhv = dh_ref[...]
        r = lax.rsqrt(jnp.mean(xv * xv, axis=1, keepdims=True) + NORM_EPS)
        xhat = xv * r
        dhg = dhv * g_ref[...]
        proj = jnp.mean(dhg * xhat, axis=1, keepdims=True)
        dx = dres_ref[...] + r * (dhg - xhat * proj)
        dx_ref[...] = dx
        dxb_ref[...] = dx.astype(BF16)
        dg_ref[...] += jnp.sum(dhv * xhat, axis=0, keepdims=True)

    return pl.pallas_call(
        body, name=name, grid=(s // t,),
        in_specs=[_spec((t, d), lambda i: (i, 0)), _spec((t, d), lambda i: (i, 0)),
                  _spec((1, d), lambda i: (0, 0)), _spec((t, d), lambda i: (i, 0))],
        out_specs=[_spec((t, d), lambda i: (i, 0)), _spec((t, d), lambda i: (i, 0)),
                   _spec((1, d), lambda i: (0, 0))],
        out_shape=[jax.ShapeDtypeStruct((s, d), F32), jax.ShapeDtypeStruct((s, d), BF16),
                   jax.ShapeDtypeStruct((1, d), F32)],
        compiler_params=_params(1),
    )(dh, x, gain, dres)


def _loss_head(x, gain, target):
    s, d = x.shape
    t = ROW_TILE
    n_steps = s // t

    def body(x_ref, g_ref, t_ref, dx_ref, dxb_ref, dg_ref, loss_ref, acc):
        i = pl.program_id(0)

        @pl.when(i == 0)
        def _():
            dg_ref[...] = jnp.zeros_like(dg_ref)
            acc[...] = jnp.zeros_like(acc)

        xv = x_ref[...]
        g = g_ref[...]
        r = lax.rsqrt(jnp.mean(xv * xv, axis=1, keepdims=True) + NORM_EPS)
        xhat = xv * r
        err = xhat * g - t_ref[...]
        acc[...] += jnp.sum(err * err, axis=0, keepdims=True)
        dy = err * (1.0 / d)
        dyg = dy * g
        proj = jnp.mean(dyg * xhat, axis=1, keepdims=True)
        dx = r * (dyg - xhat * proj)
        dx_ref[...] = dx
        dxb_ref[...] = dx.astype(BF16)
        dg_ref[...] += jnp.sum(dy * xhat, axis=0, keepdims=True)

        @pl.when(i == n_steps - 1)
        def _():
            total = jnp.sum(acc[...], axis=1, keepdims=True) * (0.5 / d)
            loss_ref[...] = jnp.broadcast_to(total, loss_ref.shape)

    return pl.pallas_call(
        body, name="loss_head", grid=(n_steps,),
        in_specs=[_spec((t, d), lambda i: (i, 0)), _spec((1, d), lambda i: (0, 0)),
                  _spec((t, d), lambda i: (i, 0))],
        out_specs=[_spec((t, d), lambda i: (i, 0)), _spec((t, d), lambda i: (i, 0)),
                   _spec((1, d), lambda i: (0, 0)), _spec((1, 128), lambda i: (0, 0))],
        out_shape=[jax.ShapeDtypeStruct((s, d), F32), jax.ShapeDtypeStruct((s, d), BF16),
                   jax.ShapeDtypeStruct((1, d), F32), jax.ShapeDtypeStruct((1, 128), F32)],
        scratch_shapes=[pltpu.VMEM((1, d), F32)],
        compiler_params=_params(1),
    )(x, gain, target)


def _rope_tables():
    half = HEAD_DIM // 2
    inv_freq = ROPE_THETA ** (-jnp.arange(half, dtype=F32) / half)
    ang = jnp.arange(SEQ, dtype=jnp.int32).astype(F32)[:, None] * inv_freq[None, :]
    cos, sin = jnp.cos(ang), jnp.sin(ang)
    cos_t = jnp.concatenate([cos, cos, cos, cos], axis=1)
    sin_t = jnp.concatenate([-sin, sin, -sin, sin], axis=1)
    return cos_t, sin_t


def _swap_halves(x):
    lane = lax.broadcasted_iota(jnp.int32, x.shape, 1)
    first = (lane % HEAD_DIM) < (HEAD_DIM // 2)
    return jnp.where(first, pltpu.roll(x, 128 - HEAD_DIM // 2, 1), pltpu.roll(x, HEAD_DIM // 2, 1))


def _permuted_specs(t, width):
    specs = [_spec((t, width), lambda i: (i, 0))]
    for d in DILATIONS[1:]:
        specs.append(_spec((d, t // d, width), lambda i: (0, i, 0)))
    return specs


def _permuted_shapes(width, dtype):
    shapes = [jax.ShapeDtypeStruct((SEQ, width), dtype)]
    for d in DILATIONS[1:]:
        shapes.append(jax.ShapeDtypeStruct((d, SEQ // d, width), dtype))
    return shapes


def _attn_prep(name, proj, cos_t, sin_t):
    t = ROW_TILE
    w = ATTN_WIDTH

    def body(q_ref, k_ref, v_ref, cos_ref, sin_ref, *rest):
        outs, scr = rest[:9], rest[9]
        cosv, sinv = cos_ref[...], sin_ref[...]
        for a, (src, roped, scale) in enumerate(((q_ref, True, HEAD_DIM ** -0.5),
                                                 (k_ref, True, 1.0), (v_ref, False, 1.0))):
            o1, o4, o16 = outs[3 * a:3 * a + 3]
            for cb in range(w // 128):
                cols = slice(cb * 128, (cb + 1) * 128)
                val = src[:, cols]
                if roped:
                    val = (val * cosv + _swap_halves(val) * sinv) * scale
                scr[...] = val
                o1[:, cols] = val.astype(BF16)
                for o_ref, d in ((o4, 4), (o16, 16)):
                    for r in range(d):
                        o_ref[r, :, cols] = scr[pl.ds(r, t // d, stride=d), :].astype(BF16)

    out_specs = _permuted_specs(t, w) * 3
    out_shape = _permuted_shapes(w, BF16) * 3
    outs = pl.pallas_call(
        body, name=name, grid=(SEQ // t,),
        in_specs=[_spec((t, w), lambda i: (i, 0)), _spec((t, w), lambda i: (i, 1)),
                  _spec((t, w), lambda i: (i, 2)),
                  _spec((t, 128), lambda i: (i, 0)), _spec((t, 128), lambda i: (i, 0))],
        out_specs=out_specs, out_shape=out_shape,
        scratch_shapes=[pltpu.VMEM((t, 128), F32)],
        compiler_params=_params(1),
    )(proj, proj, proj, cos_t, sin_t)
    q, k, v = outs[0:3], outs[3:6], outs[6:9]
    flat = lambda arr: arr.reshape(SEQ, w)
    return [(flat(q[p]), flat(k[p]), flat(v[p])) for p in range(3)]


def _band_masks():
    row = lax.broadcasted_iota(jnp.int32, (2 * SPAN, 2 * SPAN), 0) % SPAN
    col = lax.broadcasted_iota(jnp.int32, (2 * SPAN, 2 * SPAN), 1)
    is_prev = col < SPAN
    band = (is_prev & (col >= row)) | (~is_prev & (col - SPAN <= row))
    head0 = lax.broadcasted_iota(jnp.int32, (SPAN, 128), 1) < HEAD_DIM
    return band, is_prev, head0


def _stack_heads(x, head0):
    zero = jnp.zeros_like(x)
    return jnp.concatenate([jnp.where(head0, x, zero), jnp.where(head0, zero, x)], axis=0)


def _attn_fwd(name, q, k, v, seg_blocks):
    n_blocks = SEQ // SPAN

    def body(q_ref, k_ref, v_ref, o_ref, lse_ref):
        band, is_prev, head0 = _band_masks()

        def step(b, carry):
            cur = pl.ds(pl.multiple_of(b * SPAN, SPAN), SPAN)
            prev = pl.ds(pl.multiple_of(jnp.maximum(b - 1, 0) * SPAN, SPAN), SPAN)
            qs = _stack_heads(q_ref[cur, :], head0)
            kcat = jnp.concatenate([k_ref[prev, :], k_ref[cur, :]], axis=0)
            vcat = jnp.concatenate([v_ref[prev, :], v_ref[cur, :]], axis=0)
            ok = band & (((b % seg_blocks) != 0) | ~is_prev)
            s = jnp.where(ok, _dot_nt(qs, kcat), MASK_VALUE)
            m = jnp.max(s, axis=1, keepdims=True)
            p = jnp.exp(s - m)
            l = jnp.sum(p, axis=1, keepdims=True)
            pv = _dot(p, vcat) * (1.0 / l)
            lse = m + jnp.log(l)
            o_ref[cur, :] = jnp.where(head0, pv[:SPAN], pv[SPAN:])
            lse_ref[cur, :] = jnp.where(head0, lse[:SPAN], lse[SPAN:])
            return carry

        lax.fori_loop(0, n_blocks, step, 0, unroll=4)

    col = _spec((SEQ, 128), lambda j: (0, j))
    return pl.pallas_call(
        body, name=name, grid=(ATTN_WIDTH // 128,),
        in_specs=[col, col, col], out_specs=[col, col],
        out_shape=[jax.ShapeDtypeStruct((SEQ, ATTN_WIDTH), F32)] * 2,
        compiler_params=_params(1),
    )(q, k, v)


def _unpermute(dst, src_ref, d, cols):
    n = dst.shape[0] // d
    for r in range(d):
        dst[pl.ds(r, n, stride=d), :] = src_ref[r, :, cols]


def _attn_merge(name, outs, lses, gain):
    t = ROW_TILE
    w = ATTN_WIDTH

    def body(o1, o4, o16, l1, l4, l16, g_ref, an_ref, ant_ref, attn_ref, lse_ref, so4, so16, sl4, sl16):
        for cb in range(w // 128):
            cols = slice(cb * 128, (cb + 1) * 128)
            _unpermute(so4, o4, 4, cols)
            _unpermute(so16, o16, 16, cols)
            _unpermute(sl4, l4, 4, cols)
            _unpermute(sl16, l16, 16, cols)
            la, lb, lc = l1[:, cols], sl4[...], sl16[...]
            m = jnp.maximum(jnp.maximum(la, lb), lc)
            ea, eb, ec = jnp.exp(la - m), jnp.exp(lb - m), jnp.exp(lc - m)
            tot = ea + eb + ec
            attn_ref[:, cols] = (ea * o1[:, cols] + eb * so4[...] + ec * so16[...]) / tot
            lse_ref[:, cols] = m + jnp.log(tot)
        attn = attn_ref[...]
        r = lax.rsqrt(jnp.mean(attn * attn, axis=1, keepdims=True) + NORM_EPS)
        an = attn * r * g_ref[...]
        an_ref[...] = an.astype(BF16)
        ant_ref[...] = an.T.astype(BF16)

    views = lambda arrs: [arrs[0], arrs[1].reshape(4, SEQ // 4, w), arrs[2].reshape(16, SEQ // 16, w)]
    row = _spec((t, w), lambda i: (i, 0))
    return pl.pallas_call(
        body, name=name, grid=(SEQ // t,),
        in_specs=_permuted_specs(t, w) * 2 + [_spec((1, w), lambda i: (0, 0))],
        out_specs=[row, _spec((w, t), lambda i: (0, i)), row, row],
        out_shape=[jax.ShapeDtypeStruct((SEQ, 2 * w), BF16), jax.ShapeDtypeStruct((2 * w, SEQ), BF16),
                   jax.ShapeDtypeStruct((SEQ, w), F32), jax.ShapeDtypeStruct((SEQ, w), F32)],
        scratch_shapes=[pltpu.VMEM((t, 128), F32)] * 4,
        compiler_params=_params(1),
    )(*views(outs), *views(lses), gain)


def _head_sum_matrix():
    i = np.arange(ATTN_WIDTH)
    return jnp.asarray((i[:, None] // HEAD_DIM) == (i[None, :] // HEAD_DIM), dtype=F32)


def _attn_bwd_prep(name, d_an, attn, lse, gain, head_sum):
    t = ROW_TILE
    w = ATTN_WIDTH

    def body(dan_ref, attn_ref, lse_ref, g_ref, hs_ref, *rest):
        (do1, do4, do16, dl1, dl4, dl16, ls4, ls16, dg_ref), (sdo, sdl, sls) = rest[:9], rest[9:]

        @pl.when(pl.program_id(0) == 0)
        def _():
            dg_ref[...] = jnp.zeros_like(dg_ref)

        attn = attn_ref[...]
        dan = dan_ref[...]
        r = lax.rsqrt(jnp.mean(attn * attn, axis=1, keepdims=True) + NORM_EPS)
        xhat = attn * r
        dg_ref[...] += jnp.sum(dan * xhat, axis=0, keepdims=True)
        dang = dan * g_ref[...]
        d_o = r * (dang - xhat * jnp.mean(dang * xhat, axis=1, keepdims=True))
        delta = jnp.dot(d_o * attn, hs_ref[...], preferred_element_type=F32,
                        precision=lax.Precision.HIGHEST)
        do1[...] = d_o.astype(BF16)
        dl1[...] = delta
        for cb in range(w // 128):
            cols = slice(cb * 128, (cb + 1) * 128)
            sdo[...] = d_o[:, cols]
            sdl[...] = delta[:, cols]
            sls[...] = lse_ref[:, cols]
            for d, o_do, o_dl, o_ls in ((4, do4, dl4, ls4), (16, do16, dl16, ls16)):
                for rr in range(d):
                    rows = pl.ds(rr, t // d, stride=d)
                    o_do[rr, :, cols] = sdo[rows, :].astype(BF16)
                    o_dl[rr, :, cols] = sdl[rows, :]
                    o_ls[rr, :, cols] = sls[rows, :]

    row = _spec((t, w), lambda i: (i, 0))
    perm = _permuted_specs(t, w)
    outs = pl.pallas_call(
        body, name=name, grid=(SEQ // t,),
        in_specs=[row, row, row, _spec((1, w), lambda i: (0, 0)), _spec((w, w), lambda i: (0, 0))],
        out_specs=perm + perm + perm[1:] + [_spec((1, w), lambda i: (0, 0))],
        out_shape=(_permuted_shapes(w, BF16) + _permuted_shapes(w, F32) + _permuted_shapes(w, F32)[1:]
                   + [jax.ShapeDtypeStruct((1, w), F32)]),
        scratch_shapes=[pltpu.VMEM((t, 128), F32)] * 3,
        compiler_params=_params(1),
    )(d_an, attn, lse, gain, head_sum)
    flat = lambda arr: arr.reshape(SEQ, w)
    d_out = [flat(a) for a in outs[0:3]]
    delta = [flat(a) for a in outs[3:6]]
    lses = [lse, flat(outs[6]), flat(outs[7])]
    return d_out, delta, lses, outs[8]


def _attn_bwd(name, q, k, v, d_out, delta, lse, seg_blocks):
    n_blocks = SEQ // SPAN

    def body(q_ref, k_ref, v_ref, do_ref, dl_ref, lse_ref, dq_ref, dk_ref, dv_ref):
        band, is_prev, head0 = _band_masks()
        dk_ref[...] = jnp.zeros_like(dk_ref)
        dv_ref[...] = jnp.zeros_like(dv_ref)

        def per_head(x):
            return jnp.concatenate([x[:, 0:1], x[:, HEAD_DIM:HEAD_DIM + 1]], axis=0)

        def step(b, carry):
            cur = pl.ds(pl.multiple_of(b * SPAN, SPAN), SPAN)
            prev = pl.ds(pl.multiple_of(jnp.maximum(b - 1, 0) * SPAN, SPAN), SPAN)
            qs = _stack_heads(q_ref[cur, :], head0)
            dos = _stack_heads(do_ref[cur, :], head0)
            kcat = jnp.concatenate([k_ref[prev, :], k_ref[cur, :]], axis=0)
            vcat = jnp.concatenate([v_ref[prev, :], v_ref[cur, :]], axis=0)
            ok = band & (((b % seg_blocks) != 0) | ~is_prev)
            p = jnp.where(ok, jnp.exp(_dot_nt(qs, kcat) - per_head(lse_ref[cur, :])), 0.0)
            ds = p * (_dot_nt(dos, vcat) - per_head(dl_ref[cur, :]))
            dq = _dot(ds, kcat)
            dq_ref[cur, :] = jnp.where(head0, dq[:SPAN], dq[SPAN:])
            dk = _dot_tn(ds, qs)
            dv = _dot_tn(p, dos)
            dk_ref[prev, :] += dk[:SPAN]
            dv_ref[prev, :] += dv[:SPAN]
            dk_ref[cur, :] += dk[SPAN:]
            dv_ref[cur, :] += dv[SPAN:]
            return carry

        lax.fori_loop(0, n_blocks, step, 0, unroll=4)

    col = _spec((SEQ, 128), lambda j: (0, j))
    return pl.pallas_call(
        body, name=name, grid=(ATTN_WIDTH // 128,),
        in_specs=[col] * 6, out_specs=[col] * 3,
        out_shape=[jax.ShapeDtypeStruct((SEQ, ATTN_WIDTH), F32)] * 3,
        compiler_params=_params(1),
    )(q, k, v, d_out, delta, lse)


def _attn_bwd_post(name, grads, cos_t, sin_t):
    t = ROW_TILE
    w = ATTN_WIDTH

    def body(*refs):
        ins, cos_ref, sin_ref, out_ref, s4, s16 = refs[:9], refs[9], refs[10], refs[11], refs[12], refs[13]
        cosv, sinv = cos_ref[...], sin_ref[...]
        for a in range(3):
            g1, g4, g16 = ins[a], ins[3 + a], ins[6 + a]
            for cb in range(w // 128):
                cols = slice(cb * 128, (cb + 1) * 128)
                _unpermute(s4, g4, 4, cols)
                _unpermute(s16, g16, 16, cols)
                val = g1[:, cols] + s4[...] + s16[...]
                if a < 2:
                    val = val * cosv + _swap_halves(val * sinv)
                if a == 0:
                    val = val * (HEAD_DIM ** -0.5)
                out_ref[:, a * w + cb * 128:a * w + (cb + 1) * 128] = val.astype(BF16)

    views = []
    for p, d in enumerate(DILATIONS):
        for a in range(3):
            views.append(grads[p][a] if d == 1 else grads[p][a].reshape(d, SEQ // d, w))
    perm = _permuted_specs(t, w)
    in_specs = [perm[0]] * 3 + [perm[1]] * 3 + [perm[2]] * 3
    return pl.pallas_call(
        body, name=name, grid=(SEQ // t,),
        in_specs=in_specs + [_spec((t, 128), lambda i: (i, 0))] * 2,
        out_specs=_spec((t, 3 * w), lambda i: (i, 0)),
        out_shape=jax.ShapeDtypeStruct((SEQ, 3 * w), BF16),
        scratch_shapes=[pltpu.VMEM((t, 128), F32)] * 2,
        compiler_params=_params(1),
    )(*views, cos_t, sin_t)


N_LEVELS = 7


def _hgrn_consts():
    c = CHUNK
    i = np.arange(c)[:, None]
    s = np.arange(c)[None, :]
    blocks = [s <= i]
    for lv in range(N_LEVELS):
        bs = c >> lv
        h = bs // 2
        m = (i // bs) * bs + h - 1
        second = (i % bs) >= h
        blocks.append((second & (s > m) & (s <= i)) | (~second & (s > i) & (s <= m)))
    blocks.append(s > i)
    stack = np.concatenate(blocks, axis=0).astype(np.float32)
    return jnp.asarray(stack, dtype=BF16), jnp.asarray(stack.T, dtype=BF16)


def _exact_dot(m01, x):
    hi = x.astype(BF16)
    lo = (x - hi.astype(F32)).astype(BF16)
    n = x.shape[1]
    full = jnp.dot(m01, jnp.concatenate([hi, lo], axis=1), preferred_element_type=F32)
    return full[:, :n] + full[:, n:]


def _hgrn_gates(qh, z, lb):
    sq = _sigmoid(qh)
    q = qh * sq * (HGRN_DIM ** -0.5)
    sig = _sigmoid(z)
    sigm = _sigmoid(-z)
    f = lb + (1.0 - lb) * sig
    k = (1.0 - lb) * sigm
    return q, k, f, sq, sig, sigm


def _level_masks(lv):
    row = lax.broadcasted_iota(jnp.int32, (CHUNK, CHUNK), 0)
    col = lax.broadcasted_iota(jnp.int32, (CHUNK, CHUNK), 1)
    shift = N_LEVELS - lv
    second = (row & (CHUNK >> (lv + 1))) != 0
    same = (row >> shift) == (col >> shift)
    return second, same


def _hgrn_fwd(name, proj, lb, gain, stack, mixed, mixed_t):
    t = ROW_TILE
    per = t // CHUNK
    n_rb = SEQ // t
    n_chunks = SEQ // CHUNK
    col0 = 3 * ATTN_WIDTH // 128

    def body(q_ref, f_ref, i_ref, g_ref, lb_ref, gain_ref, stack_ref, mixed_in, mixed_t_in,
             rec_ref, rect_ref, o_ref, st_out, a_out, st):
        del mixed_in, mixed_t_in

        @pl.when(pl.program_id(1) == 0)
        def _():
            st[...] = jnp.zeros_like(st)

        lbv = lb_ref[...]
        row = lax.broadcasted_iota(jnp.int32, (CHUNK, CHUNK), 0)
        col = lax.broadcasted_iota(jnp.int32, (CHUNK, CHUNK), 1)
        for c in range(per):
            rows = slice(c * CHUNK, (c + 1) * CHUNK)
            qh, z, v, gh = q_ref[rows, :], f_ref[rows, :], i_ref[rows, :], g_ref[rows, :]
            q, k, f, _, _, _ = _hgrn_gates(qh, z, lbv)
            dec = _exact_dot(stack_ref[...], jnp.log(f))
            g = dec[0:CHUNK]
            to_end = dec[(N_LEVELS + 1) * CHUNK:(N_LEVELS + 2) * CHUNK]
            a = jnp.where(row == col, jnp.sum(q * k, axis=1, keepdims=True), 0.0)
            for lv in range(N_LEVELS):
                e = jnp.exp(dec[(lv + 1) * CHUNK:(lv + 2) * CHUNK])
                second, same = _level_masks(lv)
                qt = jnp.where(second, q * e, 0.0)
                kt = jnp.where(second, 0.0, k * e)
                a = a + jnp.where(same, _dot_nt(qt, kt), 0.0)
            st_prev = st[...]
            st_out[c] = st_prev
            a_out[c] = a
            o = _dot(a, v) + _dot_nt(q * jnp.exp(g), st_prev)
            k_end = k * jnp.exp(to_end)
            st[...] = st_prev * jnp.exp(g[CHUNK - 1:CHUNK, :]) + _dot(v.T, k_end)
            o_ref[rows, :] = o
            r = lax.rsqrt(jnp.mean(o * o, axis=1, keepdims=True) + NORM_EPS)
            rec = o * r * gain_ref[...] * (gh * _sigmoid(gh))
            rec_ref[rows, :] = rec.astype(BF16)
            rect_ref[:, rows] = rec.T.astype(BF16)

    def col_spec(tt):
        return _spec((t, HGRN_DIM), lambda h, rb: (rb, col0 + HGRN_HEADS * tt + h))

    chunk_spec = _spec((None, per, CHUNK, CHUNK), lambda h, rb: (h, rb, 0, 0))
    return pl.pallas_call(
        body, name=name, grid=(HGRN_HEADS, n_rb),
        in_specs=[col_spec(0), col_spec(1), col_spec(2), col_spec(3),
                  _spec((None, 1, HGRN_DIM), lambda h, rb: (h, 0, 0)),
                  _spec((1, HGRN_DIM), lambda h, rb: (0, 0)),
                  _spec(stack.shape, lambda h, rb: (0, 0)), ANY_SPEC, ANY_SPEC],
        out_specs=[_spec((t, HGRN_DIM), lambda h, rb: (rb, ATTN_WIDTH // HGRN_DIM + h)),
                   _spec((HGRN_DIM, t), lambda h, rb: (ATTN_WIDTH // HGRN_DIM + h, rb)),
                   _spec((t, HGRN_DIM), lambda h, rb: (rb, h)),
                   chunk_spec, chunk_spec],
        out_shape=[jax.ShapeDtypeStruct(mixed.shape, BF16),
                   jax.ShapeDtypeStruct(mixed_t.shape, BF16),
                   jax.ShapeDtypeStruct((SEQ, HGRN_WIDTH), F32),
                   jax.ShapeDtypeStruct((HGRN_HEADS, n_chunks, CHUNK, CHUNK), F32),
                   jax.ShapeDtypeStruct((HGRN_HEADS, n_chunks, CHUNK, CHUNK), F32)],
        scratch_shapes=[pltpu.VMEM((CHUNK, CHUNK), F32)],
        input_output_aliases={7: 0, 8: 1},
        compiler_params=_params(2),
    )(proj, proj, proj, proj, lb, gain, stack, mixed, mixed_t)


def _hgrn_bwd(name, proj, d_rec, o_pre, states, scores, lb, gain, stack, stack_t):
    t = ROW_TILE
    per = t // CHUNK
    n_rb = SEQ // t
    col0 = 3 * ATTN_WIDTH // 128

    def body(q_ref, f_ref, i_ref, g_ref, drec_ref, o_ref, st_ref, a_ref, lb_ref, gain_ref,
             stack_ref, stack_t_ref, dq_ref, df_ref, di_ref, dg_ref, dlb_ref, dgain_ref, dst):
        @pl.when(pl.program_id(1) == 0)
        def _():
            dst[...] = jnp.zeros_like(dst)
            dlb_ref[...] = jnp.zeros_like(dlb_ref)
            dgain_ref[...] = jnp.zeros_like(dgain_ref)

        lbv = lb_ref[...]
        gain_v = gain_ref[...]
        row = lax.broadcasted_iota(jnp.int32, (CHUNK, CHUNK), 0)
        col = lax.broadcasted_iota(jnp.int32, (CHUNK, CHUNK), 1)
        for c in reversed(range(per)):
            rows = slice(c * CHUNK, (c + 1) * CHUNK)
            qh, z, v, gh = q_ref[rows, :], f_ref[rows, :], i_ref[rows, :], g_ref[rows, :]
            q, k, f, sq, sig, sigm = _hgrn_gates(qh, z, lbv)
            dec = _exact_dot(stack_ref[...], jnp.log(f))
            g = dec[0:CHUNK]
            to_end = dec[(N_LEVELS + 1) * CHUNK:(N_LEVELS + 2) * CHUNK]
            e_g = jnp.exp(g)
            e_end = jnp.exp(to_end)
            e_last = jnp.exp(g[CHUNK - 1:CHUNK, :])
            q_in = q * e_g
            k_end = k * e_end
            st_prev = st_ref[c]
            a = a_ref[c]
            dst_new = dst[...]

            o = o_ref[rows, :]
            drec = drec_ref[rows, :]
            sg = _sigmoid(gh)
            r = lax.rsqrt(jnp.mean(o * o, axis=1, keepdims=True) + NORM_EPS)
            ohat = o * r
            d_gh = drec * (ohat * gain_v) * (sg * (1.0 + gh * (1.0 - sg)))
            d_on = drec * (gh * sg)
            dgain_ref[...] += jnp.sum(d_on * ohat, axis=0, keepdims=True)
            d_ohat = d_on * gain_v
            d_o = r * (d_ohat - ohat * jnp.mean(d_ohat * ohat, axis=1, keepdims=True))

            d_a = jnp.where(row >= col, _dot_nt(d_o, v), 0.0)
            d_at = jnp.where(col >= row, _dot_nt(v, d_o), 0.0)
            d_v = _dot(a.T, d_o) + _dot_nt(k_end, dst_new)
            d_q_in = _dot(d_o, st_prev)
            d_k_end = _dot(v, dst_new)
            d_q = d_q_in * e_g
            d_k = d_k_end * e_end
            diag = jnp.sum(d_o * v, axis=1, keepdims=True)
            d_q = d_q + diag * k
            d_k = d_k + diag * q
            d_dec = [q_in * d_q_in]
            for lv in range(N_LEVELS):
                e = jnp.exp(dec[(lv + 1) * CHUNK:(lv + 2) * CHUNK])
                second, same = _level_masks(lv)
                qt = jnp.where(second, q * e, 0.0)
                kt = jnp.where(second, 0.0, k * e)
                d_qt = _dot(jnp.where(same, d_a, 0.0), kt)
                d_kt = _dot(jnp.where(same, d_at, 0.0), qt)
                d_q = d_q + jnp.where(second, d_qt * e, 0.0)
                d_k = d_k + jnp.where(second, 0.0, d_kt * e)
                d_dec.append(jnp.where(second, qt * d_qt, kt * d_kt))
            d_dec.append(k_end * d_k_end)
            flux = jnp.sum(dst_new * st_prev, axis=0, keepdims=True) * e_last
            d_lf = _exact_dot(stack_t_ref[...], jnp.concatenate(d_dec, axis=0)) + flux
            dst[...] = dst_new * e_last + _dot(d_o.T, q_in)

            d_f = d_lf / f - d_k
            dlb_ref[...] += jnp.sum(d_f * sigm, axis=0, keepdims=True)
            dq_ref[rows, :] = (d_q * (HGRN_DIM ** -0.5) * (sq * (1.0 + qh * (1.0 - sq)))).astype(BF16)
            df_ref[rows, :] = (d_f * (1.0 - lbv) * sig * sigm).astype(BF16)
            di_ref[rows, :] = d_v.astype(BF16)
            dg_ref[rows, :] = d_gh.astype(BF16)

    last = n_rb - 1

    def col_spec(tt):
        return _spec((t, HGRN_DIM), lambda h, rb: (last - rb, col0 + HGRN_HEADS * tt + h))

    head_col = _spec((t, HGRN_DIM), lambda h, rb: (last - rb, h))
    rec_col0 = d_rec.shape[1] // HGRN_DIM - HGRN_HEADS
    d_rec_col = _spec((t, HGRN_DIM), lambda h, rb: (last - rb, rec_col0 + h))
    chunk_spec = _spec((None, per, CHUNK, CHUNK), lambda h, rb: (h, last - rb, 0, 0))
    vec_spec = _spec((None, 1, HGRN_DIM), lambda h, rb: (h, 0, 0))
    outs = pl.pallas_call(
        body, name=name, grid=(HGRN_HEADS, n_rb),
        in_specs=[col_spec(0), col_spec(1), col_spec(2), col_spec(3), d_rec_col, head_col,
                  chunk_spec, chunk_spec, vec_spec,
                  _spec((1, HGRN_DIM), lambda h, rb: (0, 0)),
                  _spec(stack.shape, lambda h, rb: (0, 0)), _spec(stack_t.shape, lambda h, rb: (0, 0))],
        out_specs=[head_col] * 4 + [vec_spec, vec_spec],
        out_shape=[jax.ShapeDtypeStruct((SEQ, HGRN_WIDTH), BF16)] * 4
                  + [jax.ShapeDtypeStruct((HGRN_HEADS, 1, HGRN_DIM), F32)] * 2,
        scratch_shapes=[pltpu.VMEM((CHUNK, CHUNK), F32)],
        compiler_params=_params(2),
    )(proj, proj, proj, proj, d_rec, o_pre, states, scores, lb, gain, stack, stack_t)
    return outs


ANY_SPEC = pl.BlockSpec(memory_space=pl.ANY)


def _my_place():
    return lax.axis_index("x"), lax.axis_index("y"), lax.axis_index("c")


def _other_chips(x, y):
    return [(1 - x, y), (x, 1 - y), (1 - x, 1 - y)]


def _remote(src, dst, send_sem, recv_sem, device):
    return pltpu.make_async_remote_copy(src_ref=src, dst_ref=dst, send_sem=send_sem, recv_sem=recv_sem,
                                        device_id=device, device_id_type=MESH)


def _staged_copies(srcs, dsts, stage, sems):
    loads = [pltpu.make_async_copy(srcs[i], stage[i], sems.at[i]) for i in range(len(srcs))]
    for cp in loads:
        cp.start()
    stores = []
    for i, cp in enumerate(loads):
        cp.wait()
        stores.append(pltpu.make_async_copy(stage[i], dsts[i], sems.at[i]))
        stores[-1].start()
    return stores


def _gather_weights(name, shards):
    n = len(shards)

    def body(*refs):
        ins, outs = refs[:n], refs[n:2 * n]
        ici_send, ici_recv, d2d_send, d2d_recv, local_sems = refs[2 * n:2 * n + 5]
        stage = refs[2 * n + 5:]
        x, y, c = _my_place()
        me = 2 * x + y
        chips = _other_chips(x, y)

        def half(i, which):
            h = ins[i].shape[0] // 2
            return pl.ds(which * h, h)

        sends = []
        for i in range(n):
            for j, (px, py) in enumerate(chips):
                sends.append(_remote(ins[i].at[half(i, c), :], outs[i].at[me, half(i, c), :],
                                     ici_send.at[3 * i + j], ici_recv.at[3 * i + j], (px, py, c)))
        for cp in sends:
            cp.start()
        local = _staged_copies(ins, [outs[i].at[me] for i in range(n)], stage, local_sems)
        for i in range(n):
            for j, (px, py) in enumerate(chips):
                landed = outs[i].at[2 * px + py, half(i, c), :]
                _remote(landed, landed, ici_send.at[3 * i + j], ici_recv.at[3 * i + j], (px, py, c)).wait_recv()
                forward = _remote(landed, landed, d2d_send.at[3 * i + j], d2d_recv.at[3 * i + j], (x, y, 1 - c))
                forward.start()
                sends.append(forward)
        for i in range(n):
            for j, (px, py) in enumerate(chips):
                other = outs[i].at[2 * px + py, half(i, 1 - c), :]
                _remote(other, other, d2d_send.at[3 * i + j], d2d_recv.at[3 * i + j], (x, y, 1 - c)).wait_recv()
        for cp in sends:
            cp.wait_send()
        for cp in local:
            cp.wait()

    return pl.pallas_call(
        body, name=name, in_specs=[ANY_SPEC] * n, out_specs=[ANY_SPEC] * n,
        out_shape=[jax.ShapeDtypeStruct((N_CHIPS,) + s.shape, s.dtype) for s in shards],
        scratch_shapes=([pltpu.SemaphoreType.DMA((3 * n,))] * 4 + [pltpu.SemaphoreType.DMA((n,))]
                        + [pltpu.VMEM(s.shape, s.dtype) for s in shards]),
        compiler_params=pltpu.CompilerParams(vmem_limit_bytes=VMEM_LIMIT),
    )(*shards)


def _exchange_halves(name, grads):
    n = len(grads)

    def body(*refs):
        ins, outs = refs[:n], refs[n:2 * n]
        send_sems, recv_sems = refs[2 * n:]
        x, y, c = _my_place()
        copies = []
        for i in range(n):
            h = ins[i].shape[1] // 2
            copies.append(_remote(ins[i].at[:, pl.ds((1 - c) * h, h), :], outs[i],
                                  send_sems.at[i], recv_sems.at[i], (x, y, 1 - c)))
        for cp in copies:
            cp.start()
        for cp in copies:
            cp.wait()

    return pl.pallas_call(
        body, name=name, in_specs=[ANY_SPEC] * n, out_specs=[ANY_SPEC] * n,
        out_shape=[jax.ShapeDtypeStruct((g.shape[0], g.shape[1] // 2, g.shape[2]), g.dtype) for g in grads],
        scratch_shapes=[pltpu.SemaphoreType.DMA((n,)), pltpu.SemaphoreType.DMA((n,))],
    )(*grads)


def _add_own_half(name, g, received, core):
    n_sh, r, cc = g.shape
    h = r // 2
    th = min(h, 256)
    nb = h // th

    def body(core_ref, g_ref, r_ref, o_ref):
        del core_ref
        o_ref[...] = (g_ref[...] + r_ref[...]).astype(BF16)

    grid_spec = pltpu.PrefetchScalarGridSpec(
        num_scalar_prefetch=1, grid=(n_sh, nb),
        in_specs=[pl.BlockSpec((None, th, cc), lambda j, i, core_ref: (j, core_ref[0] * nb + i, 0)),
                  pl.BlockSpec((None, th, cc), lambda j, i, core_ref: (j, i, 0))],
        out_specs=pl.BlockSpec((None, th, cc), lambda j, i, core_ref: (j, i, 0)))
    return pl.pallas_call(
        body, name=name, grid_spec=grid_spec,
        out_shape=jax.ShapeDtypeStruct((n_sh, h, cc), BF16), compiler_params=_params(2),
    )(core, g, received)


def _exchange_chips(name, parts):
    n = len(parts)

    def body(*refs):
        ins, outs = refs[:n], refs[n:2 * n]
        send_sems, recv_sems, local_sems = refs[2 * n:2 * n + 3]
        stage = refs[2 * n + 3:]
        x, y, c = _my_place()
        me = 2 * x + y
        chips = _other_chips(x, y)
        sends = []
        for i in range(n):
            for j, (px, py) in enumerate(chips):
                sends.append(_remote(ins[i].at[2 * px + py], outs[i].at[me], send_sems.at[3 * i + j],
                                     recv_sems.at[3 * i + j], (px, py, c)))
        for cp in sends:
            cp.start()
        local = _staged_copies([ins[i].at[me] for i in range(n)], [outs[i].at[me] for i in range(n)],
                               stage, local_sems)
        for i in range(n):
            for j, (px, py) in enumerate(chips):
                _remote(ins[i].at[me], outs[i].at[2 * px + py], send_sems.at[3 * i + j],
                        recv_sems.at[3 * i + j], (px, py, c)).wait_recv()
        for cp in sends:
            cp.wait_send()
        for cp in local:
            cp.wait()

    return pl.pallas_call(
        body, name=name, in_specs=[ANY_SPEC] * n, out_specs=[ANY_SPEC] * n,
        out_shape=[jax.ShapeDtypeStruct(p.shape, p.dtype) for p in parts],
        scratch_shapes=([pltpu.SemaphoreType.DMA((3 * n,)), pltpu.SemaphoreType.DMA((3 * n,)),
                         pltpu.SemaphoreType.DMA((n,))]
                        + [pltpu.VMEM(p.shape[1:], p.dtype) for p in parts]),
        compiler_params=pltpu.CompilerParams(vmem_limit_bytes=VMEM_LIMIT),
    )(*parts)


HBM_SPEC = pl.BlockSpec(memory_space=pltpu.HBM)
SEM_SPEC = pl.BlockSpec(memory_space=pltpu.SEMAPHORE)
SPLIT_PARAMS = pltpu.CompilerParams(has_side_effects=pltpu.SideEffectType.DATAFLOW_SIDE_EFFECTING)


def _chip_copies(ins, lands, send_sems, recv_sems, sliced):
    x, y, c = _my_place()
    me = 2 * x + y
    pairs = []
    for i in range(len(ins)):
        for j, (px, py) in enumerate(_other_chips(x, y)):
            theirs = 2 * px + py
            src = ins[i].at[theirs] if sliced else ins[i]
            sems = (send_sems.at[3 * i + j], recv_sems.at[3 * i + j], (px, py, c))
            pairs.append((_remote(src, lands[i].at[me], *sems), _remote(src, lands[i].at[theirs], *sems)))
    return pairs


def _exchange_start(name, srcs, lands, sliced, after):
    n = len(srcs)

    def body(*refs):
        ins, land_refs = refs[:n], refs[n:2 * n]
        send_sems, recv_sems = refs[2 * n + 1:2 * n + 3]
        token = refs[-1]
        for send, _ in _chip_copies(ins, land_refs, send_sems, recv_sems, sliced):
            send.start()
        token[...] = jnp.zeros_like(token)

    arrays = list(srcs) + list(lands)
    outs = pl.pallas_call(
        body, name=name,
        in_specs=[HBM_SPEC] * (2 * n) + [ANY_SPEC],
        out_shape=([pltpu.SemaphoreType.DMA((3 * n,))] * 2 + [pltpu.HBM(a.shape, a.dtype) for a in arrays]
                   + [jax.ShapeDtypeStruct((8, 128), F32)]),
        out_specs=[SEM_SPEC] * 2 + [HBM_SPEC] * (2 * n) + [pl.BlockSpec(memory_space=pltpu.VMEM)],
        input_output_aliases={i: 2 + i for i in range(2 * n)},
        compiler_params=SPLIT_PARAMS,
    )(*[pltpu.with_memory_space_constraint(a, pltpu.HBM) for a in arrays], after)
    return outs[:2], outs[2:2 + 2 * n], outs[-1]


def _exchange_wait(name, sems, passed, sliced, after):
    n = len(passed) // 2

    def body(*refs):
        ins, land_refs = refs[:n], refs[n:2 * n]
        send_sems, recv_sems = refs[2 * n:2 * n + 2]
        for send, arrive in _chip_copies(ins, land_refs, send_sems, recv_sems, sliced):
            send.wait_send()
            arrive.wait_recv()

    outs = pl.pallas_call(
        body, name=name,
        in_specs=[HBM_SPEC] * (2 * n) + [SEM_SPEC] * 2 + [ANY_SPEC],
        out_shape=[pltpu.HBM(a.shape, a.dtype) for a in passed],
        out_specs=[HBM_SPEC] * (2 * n),
        input_output_aliases={i: i for i in range(2 * n)},
        compiler_params=SPLIT_PARAMS,
    )(*passed, *sems, after)
    return outs[n:]


def _own_slot(name, own, me):
    r, cc = own.shape[-2:]
    th = min(r, 512)

    def body(me_ref, x_ref, o_ref):
        del me_ref
        o_ref[...] = x_ref[...]

    if own.ndim == 3:
        in_spec = pl.BlockSpec((None, th, cc), lambda i, me_ref: (me_ref[0], i, 0))
    else:
        in_spec = pl.BlockSpec((th, cc), lambda i, me_ref: (i, 0))
    grid_spec = pltpu.PrefetchScalarGridSpec(
        num_scalar_prefetch=1, grid=(r // th,), in_specs=[in_spec],
        out_specs=pl.BlockSpec((None, th, cc), lambda i, me_ref: (me_ref[0], i, 0)))
    return pl.pallas_call(
        body, name=name, grid_spec=grid_spec,
        out_shape=jax.ShapeDtypeStruct((N_CHIPS, r, cc), own.dtype), compiler_params=_params(1),
    )(me, own)


def _sum_chips(name, parts):
    n_sh, h, cc = parts.shape
    th = min(h, 256)

    def body(p_ref, o_ref):
        p = [p_ref[j].astype(F32) for j in range(n_sh)]
        o_ref[...] = ((p[0] + p[1]) + p[2]) + p[3]

    return pl.pallas_call(
        body, name=name, grid=(h // th,),
        in_specs=[_spec((n_sh, th, cc), lambda i: (0, i, 0))],
        out_specs=_spec((th, cc), lambda i: (i, 0)),
        out_shape=jax.ShapeDtypeStruct((h, cc), F32), compiler_params=_params(1),
    )(parts)


def _share_halves(halves):
    flat = [t for per_weight in halves for t in per_weight]
    n = len(flat)
    n_w = len(halves)

    def body(*refs):
        ins, outs = refs[:n], refs[n:n + n_w]
        send_sems, recv_sems, local_sems = refs[n + n_w:n + n_w + 3]
        stage = refs[n + n_w + 3:]
        x, y, c = _my_place()
        sends, own = [], []
        for i in range(n):
            w, l = divmod(i, DEPTH)
            h = ins[i].shape[0]
            own.append(outs[w].at[l, pl.ds(c * h, h), :])
            sends.append(_remote(ins[i], own[i], send_sems.at[i], recv_sems.at[i], (x, y, 1 - c)))
        for cp in sends:
            cp.start()
        local = _staged_copies(ins, own, stage, local_sems)
        for i in range(n):
            w, l = divmod(i, DEPTH)
            h = ins[i].shape[0]
            _remote(ins[i], outs[w].at[l, pl.ds((1 - c) * h, h), :], send_sems.at[i], recv_sems.at[i],
                    (x, y, 1 - c)).wait_recv()
        for cp in sends:
            cp.wait_send()
        for cp in local:
            cp.wait()

    return pl.pallas_call(
        body, name="share_halves", in_specs=[ANY_SPEC] * n, out_specs=[ANY_SPEC] * n_w,
        out_shape=[jax.ShapeDtypeStruct((DEPTH, 2 * per_weight[0].shape[0], per_weight[0].shape[1]), F32)
                   for per_weight in halves],
        scratch_shapes=([pltpu.SemaphoreType.DMA((n,))] * 3 + [pltpu.VMEM(t.shape, t.dtype) for t in flat]),
        compiler_params=pltpu.CompilerParams(vmem_limit_bytes=VMEM_LIMIT),
    )(*flat)


def _all_reduce_small(pack):
    def body(p_ref, o_ref, recv, send_sems, recv_sems):
        x, y, c = _my_place()
        me = 4 * x + 2 * y + c
        recv[me] = p_ref[...]
        peers = []
        for k in range(1, N_DEV):
            px, py, pc = (x + (k >> 2)) % 2, (y + ((k >> 1) & 1)) % 2, (c + (k & 1)) % 2
            peers.append((px, py, pc))
        sends = [_remote(p_ref, recv.at[me], send_sems.at[k], recv_sems.at[k], peer)
                 for k, peer in enumerate(peers)]
        for cp in sends:
            cp.start()
        for k, (px, py, pc) in enumerate(peers):
            _remote(p_ref, recv.at[4 * px + 2 * py + pc], send_sems.at[k], recv_sems.at[k],
                    (px, py, pc)).wait_recv()
        for cp in sends:
            cp.wait_send()
        total = recv[0]
        for d in range(1, N_DEV):
            total = total + recv[d]
        o_ref[...] = total

    vmem = pl.BlockSpec(memory_space=pltpu.VMEM)
    return pl.pallas_call(
        body, name="all_reduce_small", in_specs=[vmem], out_specs=vmem,
        out_shape=jax.ShapeDtypeStruct(pack.shape, F32),
        scratch_shapes=[pltpu.VMEM((N_DEV,) + pack.shape, F32),
                        pltpu.SemaphoreType.DMA((N_DEV - 1,)), pltpu.SemaphoreType.DMA((N_DEV - 1,))],
    )(pack)


def _adamw(name, w, g, m, v):
    r, cc = w.shape
    th = min(r, 256)

    def body(w_ref, g_ref, m_ref, v_ref, d_ref, m_out, v_out):
        gv = g_ref[...]
        m2 = ADAM_B1 * m_ref[...] + (1.0 - ADAM_B1) * gv
        v2 = ADAM_B2 * v_ref[...] + (1.0 - ADAM_B2) * (gv * gv)
        m_hat = m2 / (1.0 - ADAM_B1 ** ADAM_STEP)
        v_hat = v2 / (1.0 - ADAM_B2 ** ADAM_STEP)
        d_ref[...] = -ADAM_LR * (m_hat / (jnp.sqrt(v_hat) + ADAM_EPS) + ADAM_WD * w_ref[...])
        m_out[...] = m2
        v_out[...] = v2

    tile = _spec((th, cc), lambda i: (i, 0))
    return pl.pallas_call(
        body, name=name, grid=(r // th,), in_specs=[tile] * 4, out_specs=[tile] * 3,
        out_shape=[jax.ShapeDtypeStruct((r, cc), F32)] * 3, compiler_params=_params(1),
    )(w, g, m, v)


def _lower_bounds(lb_logits):
    p = jax.nn.softmax(lb_logits.astype(F32), axis=0)
    return jnp.cumsum(p, axis=0) - p[0]


def _row_tile_specs(tm, width):
    return _spec((tm, width), lambda i, j, k: (i, 0))


def _layer_forward(l, x_in, small, weights, consts, after=None):
    win, rest = weights
    cos_t, sin_t, stack, _, _ = consts
    tm = MM_TILE
    n_row = SEQ // tm
    saved = {"x_in": x_in}

    h, h_t = _rms_fwd(f"norm_mix{l}", x_in, small["norm_mix"][l][None, :], after=after)
    proj = _matmul(f"proj{l}", h, win,
                   _spec((tm, D_MODEL), lambda i, j, k: (i, 0)),
                   _spec((None, D_MODEL, SHARD_IN), lambda i, j, k: (j, 0, 0)),
                   (SEQ, IN_W), F32, _spec((tm, SHARD_IN), lambda i, j, k: (i, j)),
                   (n_row, N_CHIPS, 1), (tm, SHARD_IN))
    saved.update(h_t=h_t, proj=proj)

    qkv = _attn_prep(f"attn_prep{l}", proj, cos_t, sin_t)
    outs, lses = [], []
    for p, d in enumerate(DILATIONS):
        o, lse = _attn_fwd(f"attn_fwd{l}_{d}", *qkv[p], SEQ // d // SPAN)
        outs.append(o)
        lses.append(lse)
    mixed, mixed_t, attn, lse = _attn_merge(f"attn_merge{l}", outs, lses, small["attn_out_gain"][l][None, :])
    saved.update(qkv=qkv, attn=attn, lse=lse)

    lb3 = small["lower"][l].reshape(HGRN_HEADS, 1, HGRN_DIM)
    mixed, mixed_t, o_pre, states, scores = _hgrn_fwd(f"hgrn_fwd{l}", proj, lb3, small["hgrn_out_gain"][l][None, :],
                                                      stack, mixed, mixed_t)
    wo, wu, wd = rest(mixed)
    saved.update(mixed_t=mixed_t, o_pre=o_pre, states=states, scores=scores, lb3=lb3, weights=(win, wo, wu, wd))

    x_mid = _matmul(f"out_proj{l}", mixed, wo,
                    _spec((tm, SHARD_OUT), lambda i, j, k: (i, k)),
                    _spec((None, SHARD_OUT, D_MODEL), lambda i, j, k: (k, 0, 0)),
                    (SEQ, D_MODEL), F32, _spec((tm, D_MODEL), lambda i, j, k: (i, 0)),
                    (n_row, 1, N_CHIPS), (tm, D_MODEL),
                    extra=x_in, extra_spec=_spec((tm, D_MODEL), lambda i, j, k: (i, 0)), epilogue="add")
    saved["x_mid"] = x_mid

    h2, h2_t = _rms_fwd(f"norm_mlp{l}", x_mid, small["norm_mlp"][l][None, :])
    a, relu_u, a_t = _matmul(
        f"up{l}", h2, wu,
        _spec((tm, D_MODEL), lambda i, j, k: (i, 0)),
        _spec((None, D_MODEL, SHARD_MLP), lambda i, j, k: (j, 0, 0)),
        (SEQ, MLP_HIDDEN), BF16, _spec((tm, SHARD_MLP), lambda i, j, k: (i, j)),
        (n_row, N_CHIPS, 1), (tm, SHARD_MLP), epilogue="relu2",
        relu_outs=[((SEQ, MLP_HIDDEN), _spec((tm, SHARD_MLP), lambda i, j, k: (i, j))),
                   ((MLP_HIDDEN, SEQ), _spec((SHARD_MLP, tm), lambda i, j, k: (j, i)))])
    x_out = _matmul(f"down{l}", a, wd,
                    _spec((tm, SHARD_MLP), lambda i, j, k: (i, k)),
                    _spec((None, SHARD_MLP, D_MODEL), lambda i, j, k: (k, 0, 0)),
                    (SEQ, D_MODEL), F32, _spec((tm, D_MODEL), lambda i, j, k: (i, 0)),
                    (n_row, 1, N_CHIPS), (tm, D_MODEL),
                    extra=x_mid, extra_spec=_spec((tm, D_MODEL), lambda i, j, k: (i, 0)), epilogue="add")
    saved.update(h2_t=h2_t, relu_u=relu_u, a_t=a_t)
    return x_out, saved


def _layer_backward(l, dx, saved, small, consts, on_grads, after=None):
    win, wo, wu, wd = saved["weights"]
    cos_t, sin_t, stack, stack_t, head_sum = consts
    tm = MM_TILE
    n_row = SEQ // tm
    n_k = SEQ // tm

    dx, dx_b = dx
    du = _matmul(f"d_u{l}", dx_b, wd,
                 _spec((tm, D_MODEL), lambda i, j, k: (i, 0)),
                 _spec((None, SHARD_MLP, D_MODEL), lambda i, j, k: (j, 0, 0)),
                 (SEQ, MLP_HIDDEN), BF16, _spec((tm, SHARD_MLP), lambda i, j, k: (i, j)),
                 (n_row, N_CHIPS, 1), (tm, SHARD_MLP), nt=True,
                 extra=saved["relu_u"], extra_spec=_spec((tm, SHARD_MLP), lambda i, j, k: (i, j)),
                 epilogue="relu2_grad", after=after)
    d_wd = _matmul(f"d_wdown{l}", saved["a_t"], dx_b,
                   _spec((SHARD_MLP, tm), lambda i, j, k: (i, k)),
                   _spec((tm, D_MODEL), lambda i, j, k: (k, 0)),
                   (N_CHIPS, SHARD_MLP, D_MODEL), F32, _spec((None, SHARD_MLP, D_MODEL), lambda i, j, k: (i, 0, 0)),
                   (N_CHIPS, 1, n_k), (SHARD_MLP, D_MODEL))
    dh2 = _matmul(f"d_h2_{l}", du, wu,
                  _spec((tm, SHARD_MLP), lambda i, j, k: (i, k)),
                  _spec((None, D_MODEL, SHARD_MLP), lambda i, j, k: (k, 0, 0)),
                  (SEQ, D_MODEL), F32, _spec((tm, D_MODEL), lambda i, j, k: (i, 0)),
                  (n_row, 1, N_CHIPS), (tm, D_MODEL), nt=True)
    d_wu = _matmul(f"d_wup{l}", saved["h2_t"], du,
                   _spec((D_MODEL, tm), lambda i, j, k: (0, k)),
                   _spec((tm, SHARD_MLP), lambda i, j, k: (k, j)),
                   (N_CHIPS, D_MODEL, SHARD_MLP), F32, _spec((None, D_MODEL, SHARD_MLP), lambda i, j, k: (j, 0, 0)),
                   (1, N_CHIPS, n_k), (D_MODEL, SHARD_MLP))
    dxm, dxm_b, dg_mlp = _rms_bwd(f"norm_mlp_bwd{l}", dh2, saved["x_mid"], small["norm_mlp"][l][None, :], dx)
    after_mlp = on_grads(l, "mlp", (d_wu, d_wd))

    d_mixed = _matmul(f"d_mixed{l}", dxm_b, wo,
                      _spec((tm, D_MODEL), lambda i, j, k: (i, 0)),
                      _spec((None, SHARD_OUT, D_MODEL), lambda i, j, k: (j, 0, 0)),
                      (SEQ, D_MODEL), F32, _spec((tm, SHARD_OUT), lambda i, j, k: (i, j)),
                      (n_row, N_CHIPS, 1), (tm, SHARD_OUT), nt=True, after=after_mlp)
    d_wo = _matmul(f"d_wout{l}", saved["mixed_t"], dxm_b,
                   _spec((SHARD_OUT, tm), lambda i, j, k: (i, k)),
                   _spec((tm, D_MODEL), lambda i, j, k: (k, 0)),
                   (N_CHIPS, SHARD_OUT, D_MODEL), F32, _spec((None, SHARD_OUT, D_MODEL), lambda i, j, k: (i, 0, 0)),
                   (N_CHIPS, 1, n_k), (SHARD_OUT, D_MODEL))
    d_rec = d_mixed

    d_out, delta, lses, dg_attn = _attn_bwd_prep(f"attn_bwd_prep{l}", d_mixed, saved["attn"], saved["lse"],
                                                 small["attn_out_gain"][l][None, :], head_sum)
    grads = []
    for p, d in enumerate(DILATIONS):
        grads.append(_attn_bwd(f"attn_bwd{l}_{d}", *saved["qkv"][p], d_out[p], delta[p], lses[p],
                               SEQ // d // SPAN))
    dp_attn = _attn_bwd_post(f"attn_bwd_post{l}", grads, cos_t, sin_t)

    dq_h, df_h, di_h, dg_h, d_lower, dg_hgrn = _hgrn_bwd(
        f"hgrn_bwd{l}", saved["proj"], d_rec, saved["o_pre"], saved["states"], saved["scores"],
        saved["lb3"], small["hgrn_out_gain"][l][None, :], stack, stack_t)
    dproj = jnp.concatenate([dp_attn, dq_h, df_h, di_h, dg_h], axis=1)

    dh = _matmul(f"d_h{l}", dproj, win,
                 _spec((tm, SHARD_IN), lambda i, j, k: (i, k)),
                 _spec((None, D_MODEL, SHARD_IN), lambda i, j, k: (k, 0, 0)),
                 (SEQ, D_MODEL), F32, _spec((tm, D_MODEL), lambda i, j, k: (i, 0)),
                 (n_row, 1, N_CHIPS), (tm, D_MODEL), nt=True)
    d_win = _matmul(f"d_win{l}", saved["h_t"], dproj,
                    _spec((D_MODEL, tm), lambda i, j, k: (0, k)),
                    _spec((tm, SHARD_IN), lambda i, j, k: (k, j)),
                    (N_CHIPS, D_MODEL, SHARD_IN), F32, _spec((None, D_MODEL, SHARD_IN), lambda i, j, k: (j, 0, 0)),
                    (1, N_CHIPS, n_k), (D_MODEL, SHARD_IN))
    dx_in, dx_in_b, dg_mix = _rms_bwd(f"norm_mix_bwd{l}", dh, saved["x_in"], small["norm_mix"][l][None, :], dxm)

    small_grads = {"norm_mix": dg_mix[0], "attn_out_gain": dg_attn[0],
                   "lower": d_lower.reshape(HGRN_WIDTH),
                   "hgrn_out_gain": jnp.sum(dg_hgrn, axis=0).reshape(HGRN_DIM), "norm_mlp": dg_mlp[0]}
    return (dx_in, dx_in_b), on_grads(l, "mix", (d_win, d_wo)), small_grads


def _local_step(xs, target, small, get_weights, on_grads):
    consts = _rope_tables() + _hgrn_consts() + (_head_sum_matrix(),)
    stream = xs
    saved = []
    for l in range(DEPTH):
        w, after = get_weights(l, stream)
        stream, s = _layer_forward(l, stream, small, w, consts, after=after)
        saved.append(s)
    dx_f, dx_b, dg_final, loss = _loss_head(stream, small["norm_final"][None, :], target)
    dx = (dx_f, dx_b)
    small_grads = [None] * DEPTH
    after = None
    for l in reversed(range(DEPTH)):
        dx, after, small_grads[l] = _layer_backward(l, dx, saved[l], small, consts, on_grads, after=after)
    return loss, dx[0], dg_final[0], small_grads


def _pack_small(norm_mix, attn_out_gain, lb, hgrn_out_gain, norm_mlp, norm_final, last_row):
    rows = [norm_mix, attn_out_gain.reshape(1, D_MODEL), lb.reshape(1, D_MODEL),
            jnp.pad(hgrn_out_gain.reshape(1, DEPTH * HGRN_DIM), ((0, 0), (0, D_MODEL - DEPTH * HGRN_DIM))),
            norm_mlp, norm_final.reshape(1, D_MODEL), last_row.reshape(1, D_MODEL)]
    pack = jnp.concatenate(rows, axis=0)
    return jnp.pad(pack, ((0, PACK_ROWS - pack.shape[0]), (0, 0)))


def _unpack_small(pack):
    return (pack[0:2], pack[2].reshape(DEPTH, ATTN_WIDTH), pack[3].reshape(DEPTH, HGRN_WIDTH),
            pack[4, :DEPTH * HGRN_DIM].reshape(DEPTH, HGRN_DIM), pack[5:7], pack[7], pack[8])


def kernel(x, norm_mix, w_in, attn_out_gain, hgrn_lb_logits, hgrn_out_gain, w_out, norm_mlp, w_up, w_down, norm_final, loss_target, m_norm_mix, m_w_in, m_attn_out_gain, m_hgrn_lb_logits, m_hgrn_out_gain, m_w_out, m_norm_mlp, m_w_up, m_w_down, m_norm_final, v_norm_mix, v_w_in, v_attn_out_gain, v_hgrn_lb_logits, v_hgrn_out_gain, v_w_out, v_norm_mlp, v_w_up, v_w_down, v_norm_final):
    core = lax.axis_index("c").astype(jnp.int32).reshape(1)
    lower, lower_vjp = jax.vjp(_lower_bounds, hgrn_lb_logits)
    small = {"norm_mix": norm_mix, "attn_out_gain": attn_out_gain, "lower": lower,
             "hgrn_out_gain": hgrn_out_gain, "norm_mlp": norm_mlp, "norm_final": norm_final}
    big_w = (w_in, w_out, w_up, w_down)

    me = (2 * lax.axis_index("x") + lax.axis_index("y")).astype(jnp.int32).reshape(1)
    shards =[[w[l].astype(BF16) for w in big_w] for l in range(DEPTH)]
    in_flight = {}

    def start_gather(name, some, after):
        lands = [_own_slot(f"own_{name}_{i}", s, me) for i, s in enumerate(some)]
        sems, passed, token = _exchange_start(name, some, lands, False, after)
        return (sems, passed), token

    def get_weights(l, stream):
        if l == 0:
            (win,) = _gather_weights("gather_w_in0", shards[0][:1])
            in_flight["rest0"], token = start_gather("gather_start0", shards[0][1:], win)
            in_flight["weights1"], token = start_gather("gather_start1", shards[1], token)
            return (win, lambda after: _exchange_wait("gather_wait0", *in_flight.pop("rest0"), False, after)), token
        weights = _exchange_wait("gather_wait1", *in_flight.pop("weights1"), False, stream)
        return (weights[0], lambda after: weights[1:]), None

    reduced = {}

    def start_exchange(name, grads):
        received = _exchange_halves(f"halves_{name}", grads)
        halves = [_add_own_half(f"add_{name}_{i}", g, r, core) for i, (g, r) in enumerate(zip(grads, received))]
        lands = [_own_slot(f"own_{name}_{i}", h, me) for i, h in enumerate(halves)]
        sems, passed, token = _exchange_start(f"start_{name}", halves, lands, True, halves[0])
        in_flight[name] = (sems, passed)
        return token

    def finish_exchange(name, after):
        landed = _exchange_wait(f"wait_{name}", *in_flight.pop(name), True, after)
        return [_sum_chips(f"sum_{name}_{i}", p) for i, p in enumerate(landed)]

    def on_grads(l, group, grads):
        if (l, group) == (1, "mlp"):
            return start_exchange("mlp1", grads)
        if (l, group) == (1, "mix"):
            return start_exchange("mix1", grads)
        if (l, group) == (0, "mlp"):
            reduced[(1, "mlp")] = finish_exchange("mlp1", grads[0])
            reduced[(1, "mix")] = finish_exchange("mix1", grads[0])
            return start_exchange("mlp0", grads)
        received = _exchange_halves("halves_mix0", grads)
        halves = [_add_own_half(f"add_mix0_{i}", g, r, core) for i, (g, r) in enumerate(zip(grads, received))]
        from_chips = _exchange_chips("exchange_chips_mix0", halves)
        reduced[(0, "mix")] = [_sum_chips(f"sum_mix0_{i}", p) for i, p in enumerate(from_chips)]
        reduced[(0, "mlp")] = finish_exchange("mlp0", from_chips[0])
        return None

    loss, dx, dg_final, sg = _local_step(x[0], loss_target[0], small, get_weights, on_grads)

    stack2 = lambda key: jnp.stack([sg[l][key] for l in range(DEPTH)])
    pack = _pack_small(stack2("norm_mix"), stack2("attn_out_gain"), stack2("lower"), stack2("hgrn_out_gain"),
                       stack2("norm_mlp"), dg_final, jnp.broadcast_to(loss[0, 0], (D_MODEL,)))
    g_mix, g_attn, g_lower, g_hgrn, g_mlp, g_final, loss_row = _unpack_small(_all_reduce_small(pack))
    (g_logits,) = lower_vjp(g_lower)

    zeros_row = jnp.zeros((D_MODEL,), F32)
    small_w = (norm_mix, attn_out_gain, hgrn_lb_logits, hgrn_out_gain, norm_mlp, norm_final)
    small_m = (m_norm_mix, m_attn_out_gain, m_hgrn_lb_logits, m_hgrn_out_gain, m_norm_mlp, m_norm_final)
    small_v = (v_norm_mix, v_attn_out_gain, v_hgrn_lb_logits, v_hgrn_out_gain, v_norm_mlp, v_norm_final)
    small_g = (g_mix, g_attn, g_logits, g_hgrn, g_mlp, g_final)
    packs = [_pack_small(*t, zeros_row) for t in (small_w, small_g, small_m, small_v)]
    small_delta, small_new_m, small_new_v = [_unpack_small(p)[:6] for p in _adamw("adamw_small", *packs)]

    big_g = _share_halves([[reduced[(l, group)][i] for l in range(DEPTH)]
                           for group, i in (("mix", 0), ("mix", 1), ("mlp", 0), ("mlp", 1))])

    big_m = (m_w_in, m_w_out, m_w_up, m_w_down)
    big_v = (v_w_in, v_w_out, v_w_up, v_w_down)
    big_delta, big_new_m, big_new_v = [], [], []
    for i, name in enumerate(("w_in", "w_out", "w_up", "w_down")):
        shape = big_w[i].shape
        flat = lambda arr: arr.reshape(shape[0] * shape[1], shape[2])
        d, m2, v2 = _adamw(f"adamw_{name}", flat(big_w[i]), flat(big_g[i]), flat(big_m[i]), flat(big_v[i]))
        big_delta.append(d.reshape(shape))
        big_new_m.append(m2.reshape(shape))
        big_new_v.append(v2.reshape(shape))

    def ordered(small6, big4):
        mix, attn, lbl, hg, mlp, fin = small6
        return (mix, big4[0], attn, lbl, hg, big4[1], mlp, big4[2], big4[3], fin)

    return ((loss_row[0], dx[None]) + ordered(small_g, big_g) + ordered(small_delta, big_delta)
            + ordered(small_new_m, big_new_m) + ordered(small_new_v, big_new_v))
```

```python
import functools
import math

import numpy as np
import jax
import jax.numpy as jnp
from jax import lax
from jax.experimental import pallas as pl
from jax.experimental.pallas import tpu as pltpu

F32 = jnp.float32
BF16 = jnp.bfloat16
MESH = pl.DeviceIdType.MESH

SEQ = 4096
D_MODEL = 1024
DEPTH = 2
ATTN_WIDTH = 512
HEAD_DIM = 64
HGRN_HEADS = 4
HGRN_DIM = 128
HGRN_WIDTH = 512
IN_W = 3584
MLP_HIDDEN = 4096
N_CHIPS = 4
N_DEV = 8
SHARD_IN = IN_W // N_CHIPS
SHARD_OUT = D_MODEL // N_CHIPS
SHARD_MLP = MLP_HIDDEN // N_CHIPS
DILATIONS = (1, 4, 16)
SPAN = 128
ROPE_THETA = 10000.0
NORM_EPS = 1e-6
MASK_VALUE = -1e30
CHUNK = 128
ROW_TILE = 512
MM_TILE = 1024
VMEM_LIMIT = 52 * 1024 * 1024

ADAM_LR = 0.001
ADAM_B1 = 0.9
ADAM_B2 = 0.999
ADAM_EPS = 1e-08
ADAM_WD = 0.01
ADAM_STEP = 10

PACK_ROWS = 16


def _params(n_axes):
    return pltpu.CompilerParams(dimension_semantics=("arbitrary",) * n_axes,
                                vmem_limit_bytes=VMEM_LIMIT)


def _dot(a, b):
    return jnp.dot(a.astype(BF16), b.astype(BF16), preferred_element_type=F32)


def _dot_nt(a, b):
    return lax.dot_general(a.astype(BF16), b.astype(BF16), (((1,), (1,)), ((), ())),
                           preferred_element_type=F32)


def _dot_tn(a, b):
    return lax.dot_general(a.astype(BF16), b.astype(BF16), (((0,), (0,)), ((), ())),
                           preferred_element_type=F32)


def _sigmoid(x):
    return 1.0 / (1.0 + jnp.exp(-x))


def _matmul(name, a, b, a_spec, b_spec, out_shape, out_dtype, out_spec, grid, acc_shape,
            nt=False, extra=None, extra_spec=None, epilogue="none", after=None, relu_outs=None):
    nk = grid[2]
    n_out = 1 if relu_outs is None else 3

    def body(*refs):
        a_ref, b_ref = refs[:2]
        e_ref = None if extra is None else refs[2]
        o_ref = refs[-1 - n_out]
        acc = refs[-1]
        kk = pl.program_id(2)

        def product():
            return _dot_nt(a_ref[...], b_ref[...]) if nt else _dot(a_ref[...], b_ref[...])

        if nk > 1:
            @pl.when(kk == 0)
            def _():
                acc[...] = jnp.zeros_like(acc)

            acc[...] += product()

        @pl.when(kk == nk - 1)
        def _():
            r = acc[...] if nk > 1 else product()
            if epilogue == "add":
                r = r + e_ref[...]
            elif epilogue == "relu2_grad":
                r = r * (2.0 * e_ref[...].astype(F32))
            elif epilogue == "relu2":
                s = jnp.maximum(r, 0.0)
                r = s * s
                refs[-3][...] = s.astype(out_dtype)
                refs[-2][...] = r.T.astype(out_dtype)
            o_ref[...] = r.astype(o_ref.dtype)

    in_specs = [a_spec, b_spec] + ([] if extra is None else [extra_spec])
    args = (a, b) + (() if extra is None else (extra,))
    if after is not None:
        in_specs.append(pl.BlockSpec(memory_space=pl.ANY))
        args += (after,)
    out_specs, out_shapes = out_spec, jax.ShapeDtypeStruct(out_shape, out_dtype)
    if relu_outs is not None:
        out_specs = [out_spec] + [spec for _, spec in relu_outs]
        out_shapes = [out_shapes] + [jax.ShapeDtypeStruct(shape, out_dtype) for shape, _ in relu_outs]
    return pl.pallas_call(
        body, name=name, grid=grid, in_specs=in_specs, out_specs=out_specs, out_shape=out_shapes,
        scratch_shapes=[pltpu.VMEM(acc_shape, F32)],
        compiler_params=_params(3),
    )(*args)


def _spec(shape, index_map):
    return pl.BlockSpec(shape, index_map)


def _rms_fwd(name, x, gain, after=None):
    s, d = x.shape
    t = ROW_TILE

    def body(x_ref, g_ref, *rest):
        h_ref, ht_ref = rest[-2:]
        xv = x_ref[...]
        r = lax.rsqrt(jnp.mean(xv * xv, axis=1, keepdims=True) + NORM_EPS)
        h = xv * r * g_ref[...]
        h_ref[...] = h.astype(BF16)
        ht_ref[...] = h.T.astype(BF16)

    in_specs = [_spec((t, d), lambda i: (i, 0)), _spec((1, d), lambda i: (0, 0))]
    args = (x, gain)
    if after is not None:
        in_specs.append(pl.BlockSpec(memory_space=pl.ANY))
        args += (after,)
    return pl.pallas_call(
        body, name=name, grid=(s // t,), in_specs=in_specs,
        out_specs=[_spec((t, d), lambda i: (i, 0)), _spec((d, t), lambda i: (0, i))],
        out_shape=[jax.ShapeDtypeStruct((s, d), BF16), jax.ShapeDtypeStruct((d, s), BF16)],
        compiler_params=_params(1),
    )(*args)


def _rms_bwd(name, dh, x, gain, dres):
    s, d = x.shape
    t = ROW_TILE

    def body(dh_ref, x_ref, g_ref, dres_ref, dx_ref, dxb_ref, dg_ref):
        @pl.when(pl.program_id(0) == 0)
        def _():
            dg_ref[...] = jnp.zeros_like(dg_ref)

        xv = x_ref[...]
        dhv = dh_ref[...]
        r = lax.rsqrt(jnp.mean(xv * xv, axis=1, keepdims=True) + NORM_EPS)
        xhat = xv * r
        dhg = dhv * g_ref[...]
        proj = jnp.mean(dhg * xhat, axis=1, keepdims=True)
        dx = dres_ref[...] + r * (dhg - xhat * proj)
        dx_ref[...] = dx
        dxb_ref[...] = dx.astype(BF16)
        dg_ref[...] += jnp.sum(dhv * xhat, axis=0, keepdims=True)

    return pl.pallas_call(
        body, name=name, grid=(s // t,),
        in_specs=[_spec((t, d), lambda i: (i, 0)), _spec((t, d), lambda i: (i, 0)),
                  _spec((1, d), lambda i: (0, 0)), _spec((t, d), lambda i: (i, 0))],
        out_specs=[_spec((t, d), lambda i: (i, 0)), _spec((t, d), lambda i: (i, 0)),
                   _spec((1, d), lambda i: (0, 0))],
        out_shape=[jax.ShapeDtypeStruct((s, d), F32), jax.ShapeDtypeStruct((s, d), BF16),
                   jax.ShapeDtypeStruct((1, d), F32)],
        compiler_params=_params(1),
    )(dh, x, gain, dres)


def _loss_head(x, gain, target):
    s, d = x.shape
    t = ROW_TILE
    n_steps = s // t

    def body(x_ref, g_ref, t_ref, dx_ref, dxb_ref, dg_ref, loss_ref, acc):
        i = pl.program_id(0)

        @pl.when(i == 0)
        def _():
            dg_ref[...] = jnp.zeros_like(dg_ref)
            acc[...] = jnp.zeros_like(acc)

        xv = x_ref[...]
        g = g_ref[...]
        r = lax.rsqrt(jnp.mean(xv * xv, axis=1, keepdims=True) + NORM_EPS)
        xhat = xv * r
        err = xhat * g - t_ref[...]
        acc[...] += jnp.sum(err * err, axis=0, keepdims=True)
        dy = err * (1.0 / d)
        dyg = dy * g
        proj = jnp.mean(dyg * xhat, axis=1, keepdims=True)
        dx = r * (dyg - xhat * proj)
        dx_ref[...] = dx
        dxb_ref[...] = dx.astype(BF16)
        dg_ref[...] += jnp.sum(dy * xhat, axis=0, keepdims=True)

        @pl.when(i == n_steps - 1)
        def _():
            total = jnp.sum(acc[...], axis=1, keepdims=True) * (0.5 / d)
            loss_ref[...] = jnp.broadcast_to(total, loss_ref.shape)

    return pl.pallas_call(
        body, name="loss_head", grid=(n_steps,),
        in_specs=[_spec((t, d), lambda i: (i, 0)), _spec((1, d), lambda i: (0, 0)),
                  _spec((t, d), lambda i: (i, 0))],
        out_specs=[_spec((t, d), lambda i: (i, 0)), _spec((t, d), lambda i: (i, 0)),
                   _spec((1, d), lambda i: (0, 0)), _spec((1, 128), lambda i: (0, 0))],
        out_shape=[jax.ShapeDtypeStruct((s, d), F32), jax.ShapeDtypeStruct((s, d), BF16),
                   jax.ShapeDtypeStruct((1, d), F32), jax.ShapeDtypeStruct((1, 128), F32)],
        scratch_shapes=[pltpu.VMEM((1, d), F32)],
        compiler_params=_params(1),
    )(x, gain, target)


def _rope_tables():
    half = HEAD_DIM // 2
    inv_freq = ROPE_THETA ** (-jnp.arange(half, dtype=F32) / half)
    ang = jnp.arange(SEQ, dtype=jnp.int32).astype(F32)[:, None] * inv_freq[None, :]
    cos, sin = jnp.cos(ang), jnp.sin(ang)
    cos_t = jnp.concatenate([cos, cos, cos, cos], axis=1)
    sin_t = jnp.concatenate([-sin, sin, -sin, sin], axis=1)
    return cos_t, sin_t


def _swap_halves(x):
    lane = lax.broadcasted_iota(jnp.int32, x.shape, 1)
    first = (lane % HEAD_DIM) < (HEAD_DIM // 2)
    return jnp.where(first, pltpu.roll(x, 128 - HEAD_DIM // 2, 1), pltpu.roll(x, HEAD_DIM // 2, 1))


def _permuted_specs(t, width):
    specs = [_spec((t, width), lambda i: (i, 0))]
    for d in DILATIONS[1:]:
        specs.append(_spec((d, t // d, width), lambda i: (0, i, 0)))
    return specs


def _permuted_shapes(width, dtype):
    shapes = [jax.ShapeDtypeStruct((SEQ, width), dtype)]
    for d in DILATIONS[1:]:
        shapes.append(jax.ShapeDtypeStruct((d, SEQ // d, width), dtype))
    return shapes


def _attn_prep(name, proj, cos_t, sin_t):
    t = ROW_TILE
    w = ATTN_WIDTH

    def body(q_ref, k_ref, v_ref, cos_ref, sin_ref, *rest):
        outs, scr = rest[:9], rest[9]
        cosv, sinv = cos_ref[...], sin_ref[...]
        for a, (src, roped, scale) in enumerate(((q_ref, True, HEAD_DIM ** -0.5),
                                                 (k_ref, True, 1.0), (v_ref, False, 1.0))):
            o1, o4, o16 = outs[3 * a:3 * a + 3]
            for cb in range(w // 128):
                cols = slice(cb * 128, (cb + 1) * 128)
                val = src[:, cols]
                if roped:
                    val = (val * cosv + _swap_halves(val) * sinv) * scale
                scr[...] = val
                o1[:, cols] = val.astype(BF16)
                for o_ref, d in ((o4, 4), (o16, 16)):
                    for r in range(d):
                        o_ref[r, :, cols] = scr[pl.ds(r, t // d, stride=d), :].astype(BF16)

    out_specs = _permuted_specs(t, w) * 3
    out_shape = _permuted_shapes(w, BF16) * 3
    outs = pl.pallas_call(
        body, name=name, grid=(SEQ // t,),
        in_specs=[_spec((t, w), lambda i: (i, 0)), _spec((t, w), lambda i: (i, 1)),
                  _spec((t, w), lambda i: (i, 2)),
                  _spec((t, 128), lambda i: (i, 0)), _spec((t, 128), lambda i: (i, 0))],
        out_specs=out_specs, out_shape=out_shape,
        scratch_shapes=[pltpu.VMEM((t, 128), F32)],
        compiler_params=_params(1),
    )(proj, proj, proj, cos_t, sin_t)
    q, k, v = outs[0:3], outs[3:6], outs[6:9]
    flat = lambda arr: arr.reshape(SEQ, w)
    return [(flat(q[p]), flat(k[p]), flat(v[p])) for p in range(3)]


def _band_masks():
    row = lax.broadcasted_iota(jnp.int32, (2 * SPAN, 2 * SPAN), 0) % SPAN
    col = lax.broadcasted_iota(jnp.int32, (2 * SPAN, 2 * SPAN), 1)
    is_prev = col < SPAN
    band = (is_prev & (col >= row)) | (~is_prev & (col - SPAN <= row))
    head0 = lax.broadcasted_iota(jnp.int32, (SPAN, 128), 1) < HEAD_DIM
    return band, is_prev, head0


def _stack_heads(x, head0):
    zero = jnp.zeros_like(x)
    return jnp.concatenate([jnp.where(head0, x, zero), jnp.where(head0, zero, x)], axis=0)


def _attn_fwd(name, q, k, v, seg_blocks):
    n_blocks = SEQ // SPAN

    def body(q_ref, k_ref, v_ref, o_ref, lse_ref):
        band, is_prev, head0 = _band_masks()

        def step(b, carry):
            cur = pl.ds(pl.multiple_of(b * SPAN, SPAN), SPAN)
            prev = pl.ds(pl.multiple_of(jnp.maximum(b - 1, 0) * SPAN, SPAN), SPAN)
            qs = _stack_heads(q_ref[cur, :], head0)
            kcat = jnp.concatenate([k_ref[prev, :], k_ref[cur, :]], axis=0)
            vcat = jnp.concatenate([v_ref[prev, :], v_ref[cur, :]], axis=0)
            ok = band & (((b % seg_blocks) != 0) | ~is_prev)
            s = jnp.where(ok, _dot_nt(qs, kcat), MASK_VALUE)
            m = jnp.max(s, axis=1, keepdims=True)
            p = jnp.exp(s - m)
            l = jnp.sum(p, axis=1, keepdims=True)
            pv = _dot(p, vcat) * (1.0 / l)
            lse = m + jnp.log(l)
            o_ref[cur, :] = jnp.where(head0, pv[:SPAN], pv[SPAN:])
            lse_ref[cur, :] = jnp.where(head0, lse[:SPAN], lse[SPAN:])
            return carry

        lax.fori_loop(0, n_blocks, step, 0, unroll=4)

    col = _spec((SEQ, 128), lambda j: (0, j))
    return pl.pallas_call(
        body, name=name, grid=(ATTN_WIDTH // 128,),
        in_specs=[col, col, col], out_specs=[col, col],
        out_shape=[jax.ShapeDtypeStruct((SEQ, ATTN_WIDTH), F32)] * 2,
        compiler_params=_params(1),
    )(q, k, v)


def _unpermute(dst, src_ref, d, cols):
    n = dst.shape[0] // d
    for r in range(d):
        dst[pl.ds(r, n, stride=d), :] = src_ref[r, :, cols]


def _attn_merge(name, outs, lses, gain):
    t = ROW_TILE
    w = ATTN_WIDTH

    def body(o1, o4, o16, l1, l4, l16, g_ref, an_ref, ant_ref, attn_ref, lse_ref, so4, so16, sl4, sl16):
        for cb in range(w // 128):
            cols = slice(cb * 128, (cb + 1) * 128)
            _unpermute(so4, o4, 4, cols)
            _unpermute(so16, o16, 16, cols)
            _unpermute(sl4, l4, 4, cols)
            _unpermute(sl16, l16, 16, cols)
            la, lb, lc = l1[:, cols], sl4[...], sl16[...]
            m = jnp.maximum(jnp.maximum(la, lb), lc)
            ea, eb, ec = jnp.exp(la - m), jnp.exp(lb - m), jnp.exp(lc - m)
            tot = ea + eb + ec
            attn_ref[:, cols] = (ea * o1[:, cols] + eb * so4[...] + ec * so16[...]) / tot
            lse_ref[:, cols] = m + jnp.log(tot)
        attn = attn_ref[...]
        r = lax.rsqrt(jnp.mean(attn * attn, axis=1, keepdims=True) + NORM_EPS)
        an = attn * r * g_ref[...]
        an_ref[...] = an.astype(BF16)
        ant_ref[...] = an.T.astype(BF16)

    views = lambda arrs: [arrs[0], arrs[1].reshape(4, SEQ // 4, w), arrs[2].reshape(16, SEQ // 16, w)]
    row = _spec((t, w), lambda i: (i, 0))
    return pl.pallas_call(
        body, name=name, grid=(SEQ // t,),
        in_specs=_permuted_specs(t, w) * 2 + [_spec((1, w), lambda i: (0, 0))],
        out_specs=[row, _spec((w, t), lambda i: (0, i)), row, row],
        out_shape=[jax.ShapeDtypeStruct((SEQ, 2 * w), BF16), jax.ShapeDtypeStruct((2 * w, SEQ), BF16),
                   jax.ShapeDtypeStruct((SEQ, w), F32), jax.ShapeDtypeStruct((SEQ, w), F32)],
        scratch_shapes=[pltpu.VMEM((t, 128), F32)] * 4,
        compiler_params=_params(1),
    )(*views(outs), *views(lses), gain)


def _head_sum_matrix():
    i = np.arange(ATTN_WIDTH)
    return jnp.asarray((i[:, None] // HEAD_DIM) == (i[None, :] // HEAD_DIM), dtype=F32)


def _attn_bwd_prep(name, d_an, attn, lse, gain, head_sum):
    t = ROW_TILE
    w = ATTN_WIDTH

    def body(dan_ref, attn_ref, lse_ref, g_ref, hs_ref, *rest):
        (do1, do4, do16, dl1, dl4, dl16, ls4, ls16, dg_ref), (sdo, sdl, sls) = rest[:9], rest[9:]

        @pl.when(pl.program_id(0) == 0)
        def _():
            dg_ref[...] = jnp.zeros_like(dg_ref)

        attn = attn_ref[...]
        dan = dan_ref[...]
        r = lax.rsqrt(jnp.mean(attn * attn, axis=1, keepdims=True) + NORM_EPS)
        xhat = attn * r
        dg_ref[...] += jnp.sum(dan * xhat, axis=0, keepdims=True)
        dang = dan * g_ref[...]
        d_o = r * (dang - xhat * jnp.mean(dang * xhat, axis=1, keepdims=True))
        delta = jnp.dot(d_o * attn, hs_ref[...], preferred_element_type=F32,
                        precision=lax.Precision.HIGHEST)
        do1[...] = d_o.astype(BF16)
        dl1[...] = delta
        for cb in range(w // 128):
            cols = slice(cb * 128, (cb + 1) * 128)
            sdo[...] = d_o[:, cols]
            sdl[...] = delta[:, cols]
            sls[...] = lse_ref[:, cols]
            for d, o_do, o_dl, o_ls in ((4, do4, dl4, ls4), (16, do16, dl16, ls16)):
                for rr in range(d):
                    rows = pl.ds(rr, t // d, stride=d)
                    o_do[rr, :, cols] = sdo[rows, :].astype(BF16)
                    o_dl[rr, :, cols] = sdl[rows, :]
                    o_ls[rr, :, cols] = sls[rows, :]

    row = _spec((t, w), lambda i: (i, 0))
    perm = _permuted_specs(t, w)
    outs = pl.pallas_call(
        body, name=name, grid=(SEQ // t,),
        in_specs=[row, row, row, _spec((1, w), lambda i: (0, 0)), _spec((w, w), lambda i: (0, 0))],
        out_specs=perm + perm + perm[1:] + [_spec((1, w), lambda i: (0, 0))],
        out_shape=(_permuted_shapes(w, BF16) + _permuted_shapes(w, F32) + _permuted_shapes(w, F32)[1:]
                   + [jax.ShapeDtypeStruct((1, w), F32)]),
        scratch_shapes=[pltpu.VMEM((t, 128), F32)] * 3,
        compiler_params=_params(1),
    )(d_an, attn, lse, gain, head_sum)
    flat = lambda arr: arr.reshape(SEQ, w)
    d_out = [flat(a) for a in outs[0:3]]
    delta = [flat(a) for a in outs[3:6]]
    lses = [lse, flat(outs[6]), flat(outs[7])]
    return d_out, delta, lses, outs[8]


def _attn_bwd(name, q, k, v, d_out, delta, lse, seg_blocks):
    n_blocks = SEQ // SPAN

    def body(q_ref, k_ref, v_ref, do_ref, dl_ref, lse_ref, dq_ref, dk_ref, dv_ref):
        band, is_prev, head0 = _band_masks()
        dk_ref[...] = jnp.zeros_like(dk_ref)
        dv_ref[...] = jnp.zeros_like(dv_ref)

        def per_head(x):
            return jnp.concatenate([x[:, 0:1], x[:, HEAD_DIM:HEAD_DIM + 1]], axis=0)

        def step(b, carry):
            cur = pl.ds(pl.multiple_of(b * SPAN, SPAN), SPAN)
            prev = pl.ds(pl.multiple_of(jnp.maximum(b - 1, 0) * SPAN, SPAN), SPAN)
            qs = _stack_heads(q_ref[cur, :], head0)
            dos = _stack_heads(do_ref[cur, :], head0)
            kcat = jnp.concatenate([k_ref[prev, :], k_ref[cur, :]], axis=0)
            vcat = jnp.concatenate([v_ref[prev, :], v_ref[cur, :]], axis=0)
            ok = band & (((b % seg_blocks) != 0) | ~is_prev)
            p = jnp.where(ok, jnp.exp(_dot_nt(qs, kcat) - per_head(lse_ref[cur, :])), 0.0)
            ds = p * (_dot_nt(dos, vcat) - per_head(dl_ref[cur, :]))
            dq = _dot(ds, kcat)
            dq_ref[cur, :] = jnp.where(head0, dq[:SPAN], dq[SPAN:])
            dk = _dot_tn(ds, qs)
            dv = _dot_tn(p, dos)
            dk_ref[prev, :] += dk[:SPAN]
            dv_ref[prev, :] += dv[:SPAN]
            dk_ref[cur, :] += dk[SPAN:]
            dv_ref[cur, :] += dv[SPAN:]
            return carry

        lax.fori_loop(0, n_blocks, step, 0, unroll=4)

    col = _spec((SEQ, 128), lambda j: (0, j))
    return pl.pallas_call(
        body, name=name, grid=(ATTN_WIDTH // 128,),
        in_specs=[col] * 6, out_specs=[col] * 3,
        out_shape=[jax.ShapeDtypeStruct((SEQ, ATTN_WIDTH), F32)] * 3,
        compiler_params=_params(1),
    )(q, k, v, d_out, delta, lse)


def _attn_bwd_post(name, grads, cos_t, sin_t):
    t = ROW_TILE
    w = ATTN_WIDTH

    def body(*refs):
        ins, cos_ref, sin_ref, out_ref, s4, s16 = refs[:9], refs[9], refs[10], refs[11], refs[12], refs[13]
        cosv, sinv = cos_ref[...], sin_ref[...]
        for a in range(3):
            g1, g4, g16 = ins[a], ins[3 + a], ins[6 + a]
            for cb in range(w // 128):
                cols = slice(cb * 128, (cb + 1) * 128)
                _unpermute(s4, g4, 4, cols)
                _unpermute(s16, g16, 16, cols)
                val = g1[:, cols] + s4[...] + s16[...]
                if a < 2:
                    val = val * cosv + _swap_halves(val * sinv)
                if a == 0:
                    val = val * (HEAD_DIM ** -0.5)
                out_ref[:, a * w + cb * 128:a * w + (cb + 1) * 128] = val.astype(BF16)

    views = []
    for p, d in enumerate(DILATIONS):
        for a in range(3):
            views.append(grads[p][a] if d == 1 else grads[p][a].reshape(d, SEQ // d, w))
    perm = _permuted_specs(t, w)
    in_specs = [perm[0]] * 3 + [perm[1]] * 3 + [perm[2]] * 3
    return pl.pallas_call(
        body, name=name, grid=(SEQ // t,),
        in_specs=in_specs + [_spec((t, 128), lambda i: (i, 0))] * 2,
        out_specs=_spec((t, 3 * w), lambda i: (i, 0)),
        out_shape=jax.ShapeDtypeStruct((SEQ, 3 * w), BF16),
        scratch_shapes=[pltpu.VMEM((t, 128), F32)] * 2,
        compiler_params=_params(1),
    )(*views, cos_t, sin_t)


N_LEVELS = 7


def _hgrn_consts():
    c = CHUNK
    i = np.arange(c)[:, None]
    s = np.arange(c)[None, :]
    blocks = [s <= i]
    for lv in range(N_LEVELS):
        bs = c >> lv
        h = bs // 2
        m = (i // bs) * bs + h - 1
        second = (i % bs) >= h
        blocks.append((second & (s > m) & (s <= i)) | (~second & (s > i) & (s <= m)))
    blocks.append(s > i)
    stack = np.concatenate(blocks, axis=0).astype(np.float32)
    return jnp.asarray(stack, dtype=BF16), jnp.asarray(stack.T, dtype=BF16)


def _exact_dot(m01, x):
    hi = x.astype(BF16)
    lo = (x - hi.astype(F32)).astype(BF16)
    n = x.shape[1]
    full = jnp.dot(m01, jnp.concatenate([hi, lo], axis=1), preferred_element_type=F32)
    return full[:, :n] + full[:, n:]


def _hgrn_gates(qh, z, lb):
    sq = _sigmoid(qh)
    q = qh * sq * (HGRN_DIM ** -0.5)
    sig = _sigmoid(z)
    sigm = _sigmoid(-z)
    f = lb + (1.0 - lb) * sig
    k = (1.0 - lb) * sigm
    return q, k, f, sq, sig, sigm


def _level_masks(lv):
    row = lax.broadcasted_iota(jnp.int32, (CHUNK, CHUNK), 0)
    col = lax.broadcasted_iota(jnp.int32, (CHUNK, CHUNK), 1)
    shift = N_LEVELS - lv
    second = (row & (CHUNK >> (lv + 1))) != 0
    same = (row >> shift) == (col >> shift)
    return second, same


def _hgrn_fwd(name, proj, lb, gain, stack, mixed, mixed_t):
    t = ROW_TILE
    per = t // CHUNK
    n_rb = SEQ // t
    n_chunks = SEQ // CHUNK
    col0 = 3 * ATTN_WIDTH // 128

    def body(q_ref, f_ref, i_ref, g_ref, lb_ref, gain_ref, stack_ref, mixed_in, mixed_t_in,
             rec_ref, rect_ref, o_ref, st_out, a_out, st):
        del mixed_in, mixed_t_in

        @pl.when(pl.program_id(1) == 0)
        def _():
            st[...] = jnp.zeros_like(st)

        lbv = lb_ref[...]
        row = lax.broadcasted_iota(jnp.int32, (CHUNK, CHUNK), 0)
        col = lax.broadcasted_iota(jnp.int32, (CHUNK, CHUNK), 1)
        for c in range(per):
            rows = slice(c * CHUNK, (c + 1) * CHUNK)
            qh, z, v, gh = q_ref[rows, :], f_ref[rows, :], i_ref[rows, :], g_ref[rows, :]
            q, k, f, _, _, _ = _hgrn_gates(qh, z, lbv)
            dec = _exact_dot(stack_ref[...], jnp.log(f))
            g = dec[0:CHUNK]
            to_end = dec[(N_LEVELS + 1) * CHUNK:(N_LEVELS + 2) * CHUNK]
            a = jnp.where(row == col, jnp.sum(q * k, axis=1, keepdims=True), 0.0)
            for lv in range(N_LEVELS):
                e = jnp.exp(dec[(lv + 1) * CHUNK:(lv + 2) * CHUNK])
                second, same = _level_masks(lv)
                qt = jnp.where(second, q * e, 0.0)
                kt = jnp.where(second, 0.0, k * e)
                a = a + jnp.where(same, _dot_nt(qt, kt), 0.0)
            st_prev = st[...]
            st_out[c] = st_prev
            a_out[c] = a
            o = _dot(a, v) + _dot_nt(q * jnp.exp(g), st_prev)
            k_end = k * jnp.exp(to_end)
            st[...] = st_prev * jnp.exp(g[CHUNK - 1:CHUNK, :]) + _dot(v.T, k_end)
            o_ref[rows, :] = o
            r = lax.rsqrt(jnp.mean(o * o, axis=1, keepdims=True) + NORM_EPS)
            rec = o * r * gain_ref[...] * (gh * _sigmoid(gh))
            rec_ref[rows, :] = rec.astype(BF16)
            rect_ref[:, rows] = rec.T.astype(BF16)

    def col_spec(tt):
        return _spec((t, HGRN_DIM), lambda h, rb: (rb, col0 + HGRN_HEADS * tt + h))

    chunk_spec = _spec((None, per, CHUNK, CHUNK), lambda h, rb: (h, rb, 0, 0))
    return pl.pallas_call(
        body, name=name, grid=(HGRN_HEADS, n_rb),
        in_specs=[col_spec(0), col_spec(1), col_spec(2), col_spec(3),
                  _spec((None, 1, HGRN_DIM), lambda h, rb: (h, 0, 0)),
                  _spec((1, HGRN_DIM), lambda h, rb: (0, 0)),
                  _spec(stack.shape, lambda h, rb: (0, 0)), ANY_SPEC, ANY_SPEC],
        out_specs=[_spec((t, HGRN_DIM), lambda h, rb: (rb, ATTN_WIDTH // HGRN_DIM + h)),
                   _spec((HGRN_DIM, t), lambda h, rb: (ATTN_WIDTH // HGRN_DIM + h, rb)),
                   _spec((t, HGRN_DIM), lambda h, rb: (rb, h)),
                   chunk_spec, chunk_spec],
        out_shape=[jax.ShapeDtypeStruct(mixed.shape, BF16),
                   jax.ShapeDtypeStruct(mixed_t.shape, BF16),
                   jax.ShapeDtypeStruct((SEQ, HGRN_WIDTH), F32),
                   jax.ShapeDtypeStruct((HGRN_HEADS, n_chunks, CHUNK, CHUNK), F32),
                   jax.ShapeDtypeStruct((HGRN_HEADS, n_chunks, CHUNK, CHUNK), F32)],
        scratch_shapes=[pltpu.VMEM((CHUNK, CHUNK), F32)],
        input_output_aliases={7: 0, 8: 1},
        compiler_params=_params(2),
    )(proj, proj, proj, proj, lb, gain, stack, mixed, mixed_t)


def _hgrn_bwd(name, proj, d_rec, o_pre, states, scores, lb, gain, stack, stack_t):
    t = ROW_TILE
    per = t // CHUNK
    n_rb = SEQ // t
    col0 = 3 * ATTN_WIDTH // 128

    def body(q_ref, f_ref, i_ref, g_ref, drec_ref, o_ref, st_ref, a_ref, lb_ref, gain_ref,
             stack_ref, stack_t_ref, dq_ref, df_ref, di_ref, dg_ref, dlb_ref, dgain_ref, dst):
        @pl.when(pl.program_id(1) == 0)
        def _():
            dst[...] = jnp.zeros_like(dst)
            dlb_ref[...] = jnp.zeros_like(dlb_ref)
            dgain_ref[...] = jnp.zeros_like(dgain_ref)

        lbv = lb_ref[...]
        gain_v = gain_ref[...]
        row = lax.broadcasted_iota(jnp.int32, (CHUNK, CHUNK), 0)
        col = lax.broadcasted_iota(jnp.int32, (CHUNK, CHUNK), 1)
        for c in reversed(range(per)):
            rows = slice(c * CHUNK, (c + 1) * CHUNK)
            qh, z, v, gh = q_ref[rows, :], f_ref[rows, :], i_ref[rows, :], g_ref[rows, :]
            q, k, f, sq, sig, sigm = _hgrn_gates(qh, z, lbv)
            dec = _exact_dot(stack_ref[...], jnp.log(f))
            g = dec[0:CHUNK]
            to_end = dec[(N_LEVELS + 1) * CHUNK:(N_LEVELS + 2) * CHUNK]
            e_g = jnp.exp(g)
            e_end = jnp.exp(to_end)
            e_last = jnp.exp(g[CHUNK - 1:CHUNK, :])
            q_in = q * e_g
            k_end = k * e_end
            st_prev = st_ref[c]
            a = a_ref[c]
            dst_new = dst[...]

            o = o_ref[rows, :]
            drec = drec_ref[rows, :]
            sg = _sigmoid(gh)
            r = lax.rsqrt(jnp.mean(o * o, axis=1, keepdims=True) + NORM_EPS)
            ohat = o * r
            d_gh = drec * (ohat * gain_v) * (sg * (1.0 + gh * (1.0 - sg)))
            d_on = drec * (gh * sg)
            dgain_ref[...] += jnp.sum(d_on * ohat, axis=0, keepdims=True)
            d_ohat = d_on * gain_v
            d_o = r * (d_ohat - ohat * jnp.mean(d_ohat * ohat, axis=1, keepdims=True))

            d_a = jnp.where(row >= col, _dot_nt(d_o, v), 0.0)
            d_at = jnp.where(col >= row, _dot_nt(v, d_o), 0.0)
            d_v = _dot(a.T, d_o) + _dot_nt(k_end, dst_new)
            d_q_in = _dot(d_o, st_prev)
            d_k_end = _dot(v, dst_new)
            d_q = d_q_in * e_g
            d_k = d_k_end * e_end
            diag = jnp.sum(d_o * v, axis=1, keepdims=True)
            d_q = d_q + diag * k
            d_k = d_k + diag * q
            d_dec = [q_in * d_q_in]
            for lv in range(N_LEVELS):
                e = jnp.exp(dec[(lv + 1) * CHUNK:(lv + 2) * CHUNK])
                second, same = _level_masks(lv)
                qt = jnp.where(second, q * e, 0.0)
                kt = jnp.where(second, 0.0, k * e)
                d_qt = _dot(jnp.where(same, d_a, 0.0), kt)
                d_kt = _dot(jnp.where(same, d_at, 0.0), qt)
                d_q = d_q + jnp.where(second, d_qt * e, 0.0)
                d_k = d_k + jnp.where(second, 0.0, d_kt * e)
                d_dec.append(jnp.where(second, qt * d_qt, kt * d_kt))
            d_dec.append(k_end * d_k_end)
            flux = jnp.sum(dst_new * st_prev, axis=0, keepdims=True) * e_last
            d_lf = _exact_dot(stack_t_ref[...], jnp.concatenate(d_dec, axis=0)) + flux
            dst[...] = dst_new * e_last + _dot(d_o.T, q_in)

            d_f = d_lf / f - d_k
            dlb_ref[...] += jnp.sum(d_f * sigm, axis=0, keepdims=True)
            dq_ref[rows, :] = (d_q * (HGRN_DIM ** -0.5) * (sq * (1.0 + qh * (1.0 - sq)))).astype(BF16)
            df_ref[rows, :] = (d_f * (1.0 - lbv) * sig * sigm).astype(BF16)
            di_ref[rows, :] = d_v.astype(BF16)
            dg_ref[rows, :] = d_gh.astype(BF16)

    last = n_rb - 1

    def col_spec(tt):
        return _spec((t, HGRN_DIM), lambda h, rb: (last - rb, col0 + HGRN_HEADS * tt + h))

    head_col = _spec((t, HGRN_DIM), lambda h, rb: (last - rb, h))
    rec_col0 = d_rec.shape[1] // HGRN_DIM - HGRN_HEADS
    d_rec_col = _spec((t, HGRN_DIM), lambda h, rb: (last - rb, rec_col0 + h))
    chunk_spec = _spec((None, per, CHUNK, CHUNK), lambda h, rb: (h, last - rb, 0, 0))
    vec_spec = _spec((None, 1, HGRN_DIM), lambda h, rb: (h, 0, 0))
    outs = pl.pallas_call(
        body, name=name, grid=(HGRN_HEADS, n_rb),
        in_specs=[col_spec(0), col_spec(1), col_spec(2), col_spec(3), d_rec_col, head_col,
                  chunk_spec, chunk_spec, vec_spec,
                  _spec((1, HGRN_DIM), lambda h, rb: (0, 0)),
                  _spec(stack.shape, lambda h, rb: (0, 0)), _spec(stack_t.shape, lambda h, rb: (0, 0))],
        out_specs=[head_col] * 4 + [vec_spec, vec_spec],
        out_shape=[jax.ShapeDtypeStruct((SEQ, HGRN_WIDTH), BF16)] * 4
                  + [jax.ShapeDtypeStruct((HGRN_HEADS, 1, HGRN_DIM), F32)] * 2,
        scratch_shapes=[pltpu.VMEM((CHUNK, CHUNK), F32)],
        compiler_params=_params(2),
    )(proj, proj, proj, proj, d_rec, o_pre, states, scores, lb, gain, stack, stack_t)
    return outs


ANY_SPEC = pl.BlockSpec(memory_space=pl.ANY)


def _my_place():
    return lax.axis_index("x"), lax.axis_index("y"), lax.axis_index("c")


def _other_chips(x, y):
    return [(1 - x, y), (x, 1 - y), (1 - x, 1 - y)]


def _remote(src, dst, send_sem, recv_sem, device):
    return pltpu.make_async_remote_copy(src_ref=src, dst_ref=dst, send_sem=send_sem, recv_sem=recv_sem,
                                        device_id=device, device_id_type=MESH)


def _staged_copies(srcs, dsts, stage, sems):
    loads = [pltpu.make_async_copy(srcs[i], stage[i], sems.at[i]) for i in range(len(srcs))]
    for cp in loads:
        cp.start()
    stores = []
    for i, cp in enumerate(loads):
        cp.wait()
        stores.append(pltpu.make_async_copy(stage[i], dsts[i], sems.at[i]))
        stores[-1].start()
    return stores


def _gather_weights(name, shards):
    n = len(shards)

    def body(*refs):
        ins, outs = refs[:n], refs[n:2 * n]
        ici_send, ici_recv, d2d_send, d2d_recv, local_sems = refs[2 * n:2 * n + 5]
        stage = refs[2 * n + 5:]
        x, y, c = _my_place()
        me = 2 * x + y
        chips = _other_chips(x, y)

        def half(i, which):
            h = ins[i].shape[0] // 2
            return pl.ds(which * h, h)

        sends = []
        for i in range(n):
            for j, (px, py) in enumerate(chips):
                sends.append(_remote(ins[i].at[half(i, c), :], outs[i].at[me, half(i, c), :],
                                     ici_send.at[3 * i + j], ici_recv.at[3 * i + j], (px, py, c)))
        for cp in sends:
            cp.start()
        local = _staged_copies(ins, [outs[i].at[me] for i in range(n)], stage, local_sems)
        for i in range(n):
            for j, (px, py) in enumerate(chips):
                landed = outs[i].at[2 * px + py, half(i, c), :]
                _remote(landed, landed, ici_send.at[3 * i + j], ici_recv.at[3 * i + j], (px, py, c)).wait_recv()
                forward = _remote(landed, landed, d2d_send.at[3 * i + j], d2d_recv.at[3 * i + j], (x, y, 1 - c))
                forward.start()
                sends.append(forward)
        for i in range(n):
            for j, (px, py) in enumerate(chips):
                other = outs[i].at[2 * px + py, half(i, 1 - c), :]
                _remote(other, other, d2d_send.at[3 * i + j], d2d_recv.at[3 * i + j], (x, y, 1 - c)).wait_recv()
        for cp in sends:
            cp.wait_send()
        for cp in local:
            cp.wait()

    return pl.pallas_call(
        body, name=name, in_specs=[ANY_SPEC] * n, out_specs=[ANY_SPEC] * n,
        out_shape=[jax.ShapeDtypeStruct((N_CHIPS,) + s.shape, s.dtype) for s in shards],
        scratch_shapes=([pltpu.SemaphoreType.DMA((3 * n,))] * 4 + [pltpu.SemaphoreType.DMA((n,))]
                        + [pltpu.VMEM(s.shape, s.dtype) for s in shards]),
        compiler_params=pltpu.CompilerParams(vmem_limit_bytes=VMEM_LIMIT),
    )(*shards)


def _exchange_halves(name, grads):
    n = len(grads)

    def body(*refs):
        ins, outs = refs[:n], refs[n:2 * n]
        send_sems, recv_sems = refs[2 * n:]
        x, y, c = _my_place()
        copies = []
        for i in range(n):
            h = ins[i].shape[1] // 2
            copies.append(_remote(ins[i].at[:, pl.ds((1 - c) * h, h), :], outs[i],
                                  send_sems.at[i], recv_sems.at[i], (x, y, 1 - c)))
        for cp in copies:
            cp.start()
        for cp in copies:
            cp.wait()

    return pl.pallas_call(
        body, name=name, in_specs=[ANY_SPEC] * n, out_specs=[ANY_SPEC] * n,
        out_shape=[jax.ShapeDtypeStruct((g.shape[0], g.shape[1] // 2, g.shape[2]), g.dtype) for g in grads],
        scratch_shapes=[pltpu.SemaphoreType.DMA((n,)), pltpu.SemaphoreType.DMA((n,))],
    )(*grads)


def _add_own_half(name, g, received, core):
    n_sh, r, cc = g.shape
    h = r // 2
    th = min(h, 256)
    nb = h // th

    def body(core_ref, g_ref, r_ref, o_ref):
        del core_ref
        o_ref[...] = (g_ref[...] + r_ref[...]).astype(BF16)

    grid_spec = pltpu.PrefetchScalarGridSpec(
        num_scalar_prefetch=1, grid=(n_sh, nb),
        in_specs=[pl.BlockSpec((None, th, cc), lambda j, i, core_ref: (j, core_ref[0] * nb + i, 0)),
                  pl.BlockSpec((None, th, cc), lambda j, i, core_ref: (j, i, 0))],
        out_specs=pl.BlockSpec((None, th, cc), lambda j, i, core_ref: (j, i, 0)))
    return pl.pallas_call(
        body, name=name, grid_spec=grid_spec,
        out_shape=jax.ShapeDtypeStruct((n_sh, h, cc), BF16), compiler_params=_params(2),
    )(core, g, received)


HBM_SPEC = pl.BlockSpec(memory_space=pltpu.HBM)
SEM_SPEC = pl.BlockSpec(memory_space=pltpu.SEMAPHORE)
SPLIT_PARAMS = pltpu.CompilerParams(has_side_effects=pltpu.SideEffectType.DATAFLOW_SIDE_EFFECTING)


def _chip_copies(ins, lands, send_sems, recv_sems, sliced):
    x, y, c = _my_place()
    me = 2 * x + y
    pairs = []
    for i in range(len(ins)):
        for j, (px, py) in enumerate(_other_chips(x, y)):
            theirs = 2 * px + py
            src = ins[i].at[theirs] if sliced else ins[i]
            sems = (send_sems.at[3 * i + j], recv_sems.at[3 * i + j], (px, py, c))
            pairs.append((_remote(src, lands[i].at[me], *sems), _remote(src, lands[i].at[theirs], *sems)))
    return pairs


def _exchange_start(name, srcs, lands, sliced, after):
    n = len(srcs)

    def body(*refs):
        ins, land_refs = refs[:n], refs[n:2 * n]
        send_sems, recv_sems = refs[2 * n + 1:2 * n + 3]
        token = refs[-1]
        for send, _ in _chip_copies(ins, land_refs, send_sems, recv_sems, sliced):
            send.start()
        token[...] = jnp.zeros_like(token)

    arrays = list(srcs) + list(lands)
    outs = pl.pallas_call(
        body, name=name,
        in_specs=[HBM_SPEC] * (2 * n) + [ANY_SPEC],
        out_shape=([pltpu.SemaphoreType.DMA((3 * n,))] * 2 + [pltpu.HBM(a.shape, a.dtype) for a in arrays]
                   + [jax.ShapeDtypeStruct((8, 128), F32)]),
        out_specs=[SEM_SPEC] * 2 + [HBM_SPEC] * (2 * n) + [pl.BlockSpec(memory_space=pltpu.VMEM)],
        input_output_aliases={i: 2 + i for i in range(2 * n)},
        compiler_params=SPLIT_PARAMS,
    )(*[pltpu.with_memory_space_constraint(a, pltpu.HBM) for a in arrays], after)
    return outs[:2], outs[2:2 + 2 * n], outs[-1]


def _exchange_wait(name, sems, passed, sliced, after):
    n = len(passed) // 2

    def body(*refs):
        ins, land_refs = refs[:n], refs[n:2 * n]
        send_sems, recv_sems = refs[2 * n:2 * n + 2]
        for send, arrive in _chip_copies(ins, land_refs, send_sems, recv_sems, sliced):
            send.wait_send()
            arrive.wait_recv()

    outs = pl.pallas_call(
        body, name=name,
        in_specs=[HBM_SPEC] * (2 * n) + [SEM_SPEC] * 2 + [ANY_SPEC],
        out_shape=[pltpu.HBM(a.shape, a.dtype) for a in passed],
        out_specs=[HBM_SPEC] * (2 * n),
        input_output_aliases={i: i for i in range(2 * n)},
        compiler_params=SPLIT_PARAMS,
    )(*passed, *sems, after)
    return outs[n:]


def _own_slot(name, own, me):
    r, cc = own.shape[-2:]
    th = min(r, 512)

    def body(me_ref, x_ref, o_ref):
        del me_ref
        o_ref[...] = x_ref[...]

    if own.ndim == 3:
        in_spec = pl.BlockSpec((None, th, cc), lambda i, me_ref: (me_ref[0], i, 0))
    else:
        in_spec = pl.BlockSpec((th, cc), lambda i, me_ref: (i, 0))
    grid_spec = pltpu.PrefetchScalarGridSpec(
        num_scalar_prefetch=1, grid=(r // th,), in_specs=[in_spec],
        out_specs=pl.BlockSpec((None, th, cc), lambda i, me_ref: (me_ref[0], i, 0)))
    return pl.pallas_call(
        body, name=name, grid_spec=grid_spec,
        out_shape=jax.ShapeDtypeStruct((N_CHIPS, r, cc), own.dtype), compiler_params=_params(1),
    )(me, own)


def _sum_chips(name, parts):
    n_sh, h, cc = parts.shape
    th = min(h, 256)

    def body(p_ref, o_ref):
        p = [p_ref[j].astype(F32) for j in range(n_sh)]
        o_ref[...] = ((p[0] + p[1]) + p[2]) + p[3]

    return pl.pallas_call(
        body, name=name, grid=(h // th,),
        in_specs=[_spec((n_sh, th, cc), lambda i: (0, i, 0))],
        out_specs=_spec((th, cc), lambda i: (i, 0)),
        out_shape=jax.ShapeDtypeStruct((h, cc), F32), compiler_params=_params(1),
    )(parts)


def _share_halves(name, halves):
    flat = [t for per_weight in halves for t in per_weight]
    n = len(flat)
    n_w = len(halves)

    def body(*refs):
        ins, outs = refs[:n], refs[n:n + n_w]
        send_sems, recv_sems, local_sems = refs[n + n_w:n + n_w + 3]
        stage = refs[n + n_w + 3:]
        x, y, c = _my_place()
        sends, own = [], []
        for i in range(n):
            w, l = divmod(i, DEPTH)
            h = ins[i].shape[0]
            own.append(outs[w].at[l, pl.ds(c * h, h), :])
            sends.append(_remote(ins[i], own[i], send_sems.at[i], recv_sems.at[i], (x, y, 1 - c)))
        for cp in sends:
            cp.start()
        local = _staged_copies(ins, own, stage, local_sems)
        for i in range(n):
            w, l = divmod(i, DEPTH)
            h = ins[i].shape[0]
            _remote(ins[i], outs[w].at[l, pl.ds((1 - c) * h, h), :], send_sems.at[i], recv_sems.at[i],
                    (x, y, 1 - c)).wait_recv()
        for cp in sends:
            cp.wait_send()
        for cp in local:
            cp.wait()

    return pl.pallas_call(
        body, name=name, in_specs=[ANY_SPEC] * n, out_specs=[ANY_SPEC] * n_w,
        out_shape=[jax.ShapeDtypeStruct((DEPTH, 2 * per_weight[0].shape[0], per_weight[0].shape[1]), F32)
                   for per_weight in halves],
        scratch_shapes=([pltpu.SemaphoreType.DMA((n,))] * 3 + [pltpu.VMEM(t.shape, t.dtype) for t in flat]),
        compiler_params=pltpu.CompilerParams(vmem_limit_bytes=VMEM_LIMIT),
    )(*flat)


def _all_reduce_small(pack):
    def body(p_ref, o_ref, recv, send_sems, recv_sems):
        x, y, c = _my_place()
        me = 4 * x + 2 * y + c
        recv[me] = p_ref[...]
        peers = []
        for k in range(1, N_DEV):
            px, py, pc = (x + (k >> 2)) % 2, (y + ((k >> 1) & 1)) % 2, (c + (k & 1)) % 2
            peers.append((px, py, pc))
        sends = [_remote(p_ref, recv.at[me], send_sems.at[k], recv_sems.at[k], peer)
                 for k, peer in enumerate(peers)]
        for cp in sends:
            cp.start()
        for k, (px, py, pc) in enumerate(peers):
            _remote(p_ref, recv.at[4 * px + 2 * py + pc], send_sems.at[k], recv_sems.at[k],
                    (px, py, pc)).wait_recv()
        for cp in sends:
            cp.wait_send()
        total = recv[0]
        for d in range(1, N_DEV):
            total = total + recv[d]
        o_ref[...] = total

    vmem = pl.BlockSpec(memory_space=pltpu.VMEM)
    return pl.pallas_call(
        body, name="all_reduce_small", in_specs=[vmem], out_specs=vmem,
        out_shape=jax.ShapeDtypeStruct(pack.shape, F32),
        scratch_shapes=[pltpu.VMEM((N_DEV,) + pack.shape, F32),
                        pltpu.SemaphoreType.DMA((N_DEV - 1,)), pltpu.SemaphoreType.DMA((N_DEV - 1,))],
    )(pack)


def _adamw(name, w, g, m, v):
    r, cc = w.shape
    th = min(r, 256)

    def body(w_ref, g_ref, m_ref, v_ref, d_ref, m_out, v_out):
        gv = g_ref[...]
        m2 = ADAM_B1 * m_ref[...] + (1.0 - ADAM_B1) * gv
        v2 = ADAM_B2 * v_ref[...] + (1.0 - ADAM_B2) * (gv * gv)
        m_hat = m2 / (1.0 - ADAM_B1 ** ADAM_STEP)
        v_hat = v2 / (1.0 - ADAM_B2 ** ADAM_STEP)
        d_ref[...] = -ADAM_LR * (m_hat / (jnp.sqrt(v_hat) + ADAM_EPS) + ADAM_WD * w_ref[...])
        m_out[...] = m2
        v_out[...] = v2

    tile = _spec((th, cc), lambda i: (i, 0))
    return pl.pallas_call(
        body, name=name, grid=(r // th,), in_specs=[tile] * 4, out_specs=[tile] * 3,
        out_shape=[jax.ShapeDtypeStruct((r, cc), F32)] * 3, compiler_params=_params(1),
    )(w, g, m, v)


def _lower_bounds(lb_logits):
    p = jax.nn.softmax(lb_logits.astype(F32), axis=0)
    return jnp.cumsum(p, axis=0) - p[0]


def _row_tile_specs(tm, width):
    return _spec((tm, width), lambda i, j, k: (i, 0))


def _layer_forward(l, x_in, small, weights, consts, after=None):
    win, rest = weights
    cos_t, sin_t, stack, _, _ = consts
    tm = MM_TILE
    n_row = SEQ // tm
    saved = {"x_in": x_in}

    h, h_t = _rms_fwd(f"norm_mix{l}", x_in, small["norm_mix"][l][None, :], after=after)
    proj = _matmul(f"proj{l}", h, win,
                   _spec((tm, D_MODEL), lambda i, j, k: (i, 0)),
                   _spec((None, D_MODEL, SHARD_IN), lambda i, j, k: (j, 0, 0)),
                   (SEQ, IN_W), F32, _spec((tm, SHARD_IN), lambda i, j, k: (i, j)),
                   (n_row, N_CHIPS, 1), (tm, SHARD_IN))
    saved.update(h_t=h_t, proj=proj)

    qkv = _attn_prep(f"attn_prep{l}", proj, cos_t, sin_t)
    outs, lses = [], []
    for p, d in enumerate(DILATIONS):
        o, lse = _attn_fwd(f"attn_fwd{l}_{d}", *qkv[p], SEQ // d // SPAN)
        outs.append(o)
        lses.append(lse)
    mixed, mixed_t, attn, lse = _attn_merge(f"attn_merge{l}", outs, lses, small["attn_out_gain"][l][None, :])
    saved.update(qkv=qkv, attn=attn, lse=lse)

    lb3 = small["lower"][l].reshape(HGRN_HEADS, 1, HGRN_DIM)
    mixed, mixed_t, o_pre, states, scores = _hgrn_fwd(f"hgrn_fwd{l}", proj, lb3, small["hgrn_out_gain"][l][None, :],
                                                      stack, mixed, mixed_t)
    wo, wu, wd = rest(mixed)
    saved.update(mixed_t=mixed_t, o_pre=o_pre, states=states, scores=scores, lb3=lb3, weights=(win, wo, wu, wd))

    x_mid = _matmul(f"out_proj{l}", mixed, wo,
                    _spec((tm, SHARD_OUT), lambda i, j, k: (i, k)),
                    _spec((None, SHARD_OUT, D_MODEL), lambda i, j, k: (k, 0, 0)),
                    (SEQ, D_MODEL), F32, _spec((tm, D_MODEL), lambda i, j, k: (i, 0)),
                    (n_row, 1, N_CHIPS), (tm, D_MODEL),
                    extra=x_in, extra_spec=_spec((tm, D_MODEL), lambda i, j, k: (i, 0)), epilogue="add")
    saved["x_mid"] = x_mid

    h2, h2_t = _rms_fwd(f"norm_mlp{l}", x_mid, small["norm_mlp"][l][None, :])
    a, relu_u, a_t = _matmul(
        f"up{l}", h2, wu,
        _spec((tm, D_MODEL), lambda i, j, k: (i, 0)),
        _spec((None, D_MODEL, SHARD_MLP), lambda i, j, k: (j, 0, 0)),
        (SEQ, MLP_HIDDEN), BF16, _spec((tm, SHARD_MLP), lambda i, j, k: (i, j)),
        (n_row, N_CHIPS, 1), (tm, SHARD_MLP), epilogue="relu2",
        relu_outs=[((SEQ, MLP_HIDDEN), _spec((tm, SHARD_MLP), lambda i, j, k: (i, j))),
                   ((MLP_HIDDEN, SEQ), _spec((SHARD_MLP, tm), lambda i, j, k: (j, i)))])
    x_out = _matmul(f"down{l}", a, wd,
                    _spec((tm, SHARD_MLP), lambda i, j, k: (i, k)),
                    _spec((None, SHARD_MLP, D_MODEL), lambda i, j, k: (k, 0, 0)),
                    (SEQ, D_MODEL), F32, _spec((tm, D_MODEL), lambda i, j, k: (i, 0)),
                    (n_row, 1, N_CHIPS), (tm, D_MODEL),
                    extra=x_mid, extra_spec=_spec((tm, D_MODEL), lambda i, j, k: (i, 0)), epilogue="add")
    saved.update(h2_t=h2_t, relu_u=relu_u, a_t=a_t)
    return x_out, saved


def _layer_backward(l, dx, saved, small, consts, on_grads, after=None):
    win, wo, wu, wd = saved["weights"]
    cos_t, sin_t, stack, stack_t, head_sum = consts
    tm = MM_TILE
    n_row = SEQ // tm
    n_k = SEQ // tm

    dx, dx_b = dx
    du = _matmul(f"d_u{l}", dx_b, wd,
                 _spec((tm, D_MODEL), lambda i, j, k: (i, 0)),
                 _spec((None, SHARD_MLP, D_MODEL), lambda i, j, k: (j, 0, 0)),
                 (SEQ, MLP_HIDDEN), BF16, _spec((tm, SHARD_MLP), lambda i, j, k: (i, j)),
                 (n_row, N_CHIPS, 1), (tm, SHARD_MLP), nt=True,
                 extra=saved["relu_u"], extra_spec=_spec((tm, SHARD_MLP), lambda i, j, k: (i, j)),
                 epilogue="relu2_grad", after=after)
    d_wd = _matmul(f"d_wdown{l}", saved["a_t"], dx_b,
                   _spec((SHARD_MLP, tm), lambda i, j, k: (i, k)),
                   _spec((tm, D_MODEL), lambda i, j, k: (k, 0)),
                   (N_CHIPS, SHARD_MLP, D_MODEL), F32, _spec((None, SHARD_MLP, D_MODEL), lambda i, j, k: (i, 0, 0)),
                   (N_CHIPS, 1, n_k), (SHARD_MLP, D_MODEL))
    dh2 = _matmul(f"d_h2_{l}", du, wu,
                  _spec((tm, SHARD_MLP), lambda i, j, k: (i, k)),
                  _spec((None, D_MODEL, SHARD_MLP), lambda i, j, k: (k, 0, 0)),
                  (SEQ, D_MODEL), F32, _spec((tm, D_MODEL), lambda i, j, k: (i, 0)),
                  (n_row, 1, N_CHIPS), (tm, D_MODEL), nt=True)
    d_wu = _matmul(f"d_wup{l}", saved["h2_t"], du,
                   _spec((D_MODEL, tm), lambda i, j, k: (0, k)),
                   _spec((tm, SHARD_MLP), lambda i, j, k: (k, j)),
                   (N_CHIPS, D_MODEL, SHARD_MLP), F32, _spec((None, D_MODEL, SHARD_MLP), lambda i, j, k: (j, 0, 0)),
                   (1, N_CHIPS, n_k), (D_MODEL, SHARD_MLP))
    dxm, dxm_b, dg_mlp = _rms_bwd(f"norm_mlp_bwd{l}", dh2, saved["x_mid"], small["norm_mlp"][l][None, :], dx)
    after_mlp = on_grads(l, "mlp", (d_wu, d_wd))

    d_mixed = _matmul(f"d_mixed{l}", dxm_b, wo,
                      _spec((tm, D_MODEL), lambda i, j, k: (i, 0)),
                      _spec((None, SHARD_OUT, D_MODEL), lambda i, j, k: (j, 0, 0)),
                      (SEQ, D_MODEL), F32, _spec((tm, SHARD_OUT), lambda i, j, k: (i, j)),
                      (n_row, N_CHIPS, 1), (tm, SHARD_OUT), nt=True, after=after_mlp)
    d_wo = _matmul(f"d_wout{l}", saved["mixed_t"], dxm_b,
                   _spec((SHARD_OUT, tm), lambda i, j, k: (i, k)),
                   _spec((tm, D_MODEL), lambda i, j, k: (k, 0)),
                   (N_CHIPS, SHARD_OUT, D_MODEL), F32, _spec((None, SHARD_OUT, D_MODEL), lambda i, j, k: (i, 0, 0)),
                   (N_CHIPS, 1, n_k), (SHARD_OUT, D_MODEL))
    d_rec = d_mixed

    d_out, delta, lses, dg_attn = _attn_bwd_prep(f"attn_bwd_prep{l}", d_mixed, saved["attn"], saved["lse"],
                                                 small["attn_out_gain"][l][None, :], head_sum)
    grads = []
    for p, d in enumerate(DILATIONS):
        grads.append(_attn_bwd(f"attn_bwd{l}_{d}", *saved["qkv"][p], d_out[p], delta[p], lses[p],
                               SEQ // d // SPAN))
    dp_attn = _attn_bwd_post(f"attn_bwd_post{l}", grads, cos_t, sin_t)

    dq_h, df_h, di_h, dg_h, d_lower, dg_hgrn = _hgrn_bwd(
        f"hgrn_bwd{l}", saved["proj"], d_rec, saved["o_pre"], saved["states"], saved["scores"],
        saved["lb3"], small["hgrn_out_gain"][l][None, :], stack, stack_t)
    dproj = jnp.concatenate([dp_attn, dq_h, df_h, di_h, dg_h], axis=1)

    dh = _matmul(f"d_h{l}", dproj, win,
                 _spec((tm, SHARD_IN), lambda i, j, k: (i, k)),
                 _spec((None, D_MODEL, SHARD_IN), lambda i, j, k: (k, 0, 0)),
                 (SEQ, D_MODEL), F32, _spec((tm, D_MODEL), lambda i, j, k: (i, 0)),
                 (n_row, 1, N_CHIPS), (tm, D_MODEL), nt=True)
    d_win = _matmul(f"d_win{l}", saved["h_t"], dproj,
                    _spec((D_MODEL, tm), lambda i, j, k: (0, k)),
                    _spec((tm, SHARD_IN), lambda i, j, k: (k, j)),
                    (N_CHIPS, D_MODEL, SHARD_IN), F32, _spec((None, D_MODEL, SHARD_IN), lambda i, j, k: (j, 0, 0)),
                    (1, N_CHIPS, n_k), (D_MODEL, SHARD_IN))
    dx_in, dx_in_b, dg_mix = _rms_bwd(f"norm_mix_bwd{l}", dh, saved["x_in"], small["norm_mix"][l][None, :], dxm)

    small_grads = {"norm_mix": dg_mix[0], "attn_out_gain": dg_attn[0],
                   "lower": d_lower.reshape(HGRN_WIDTH),
                   "hgrn_out_gain": jnp.sum(dg_hgrn, axis=0).reshape(HGRN_DIM), "norm_mlp": dg_mlp[0]}
    return (dx_in, dx_in_b), on_grads(l, "mix", (d_win, d_wo)), small_grads


def _local_step(xs, target, small, get_weights, on_grads):
    consts = _rope_tables() + _hgrn_consts() + (_head_sum_matrix(),)
    stream = xs
    saved = []
    for l in range(DEPTH):
        w, after = get_weights(l, stream)
        stream, s = _layer_forward(l, stream, small, w, consts, after=after)
        saved.append(s)
    dx_f, dx_b, dg_final, loss = _loss_head(stream, small["norm_final"][None, :], target)
    dx = (dx_f, dx_b)
    small_grads = [None] * DEPTH
    after = None
    for l in reversed(range(DEPTH)):
        dx, after, small_grads[l] = _layer_backward(l, dx, saved[l], small, consts, on_grads, after=after)
    return loss, dx[0], dg_final[0], small_grads, after


def _pack_small(norm_mix, attn_out_gain, lb, hgrn_out_gain, norm_mlp, norm_final, last_row):
    rows = [norm_mix, attn_out_gain.reshape(1, D_MODEL), lb.reshape(1, D_MODEL),
            jnp.pad(hgrn_out_gain.reshape(1, DEPTH * HGRN_DIM), ((0, 0), (0, D_MODEL - DEPTH * HGRN_DIM))),
            norm_mlp, norm_final.reshape(1, D_MODEL), last_row.reshape(1, D_MODEL)]
    pack = jnp.concatenate(rows, axis=0)
    return jnp.pad(pack, ((0, PACK_ROWS - pack.shape[0]), (0, 0)))


def _unpack_small(pack):
    return (pack[0:2], pack[2].reshape(DEPTH, ATTN_WIDTH), pack[3].reshape(DEPTH, HGRN_WIDTH),
            pack[4, :DEPTH * HGRN_DIM].reshape(DEPTH, HGRN_DIM), pack[5:7], pack[7], pack[8])


def kernel(x, norm_mix, w_in, attn_out_gain, hgrn_lb_logits, hgrn_out_gain, w_out, norm_mlp, w_up, w_down, norm_final, loss_target, m_norm_mix, m_w_in, m_attn_out_gain, m_hgrn_lb_logits, m_hgrn_out_gain, m_w_out, m_norm_mlp, m_w_up, m_w_down, m_norm_final, v_norm_mix, v_w_in, v_attn_out_gain, v_hgrn_lb_logits, v_hgrn_out_gain, v_w_out, v_norm_mlp, v_w_up, v_w_down, v_norm_final):
    core = lax.axis_index("c").astype(jnp.int32).reshape(1)
    lower, lower_vjp = jax.vjp(_lower_bounds, hgrn_lb_logits)
    small = {"norm_mix": norm_mix, "attn_out_gain": attn_out_gain, "lower": lower,
             "hgrn_out_gain": hgrn_out_gain, "norm_mlp": norm_mlp, "norm_final": norm_final}
    big_w = (w_in, w_out, w_up, w_down)

    me = (2 * lax.axis_index("x") + lax.axis_index("y")).astype(jnp.int32).reshape(1)
    shards =[[w[l].astype(BF16) for w in big_w] for l in range(DEPTH)]
    in_flight = {}

    def start_gather(name, some, after):
        lands = [_own_slot(f"own_{name}_{i}", s, me) for i, s in enumerate(some)]
        sems, passed, token = _exchange_start(name, some, lands, False, after)
        return (sems, passed), token

    def get_weights(l, stream):
        if l == 0:
            (win,) = _gather_weights("gather_w_in0", shards[0][:1])
            in_flight["rest0"], token = start_gather("gather_start0", shards[0][1:], win)
            in_flight["weights1"], token = start_gather("gather_start1", shards[1], token)
            return (win, lambda after: _exchange_wait("gather_wait0", *in_flight.pop("rest0"), False, after)), token
        weights = _exchange_wait("gather_wait1", *in_flight.pop("weights1"), False, stream)
        return (weights[0], lambda after: weights[1:]), None

    reduced = {}

    def start_exchange(name, grads):
        received = _exchange_halves(f"halves_{name}", grads)
        halves = [_add_own_half(f"add_{name}_{i}", g, r, core) for i, (g, r) in enumerate(zip(grads, received))]
        lands = [_own_slot(f"own_{name}_{i}", h, me) for i, h in enumerate(halves)]
        sems, passed, token = _exchange_start(f"start_{name}", halves, lands, True, halves[0])
        in_flight[name] = (sems, passed)
        return token

    def finish_exchange(name, after):
        landed = _exchange_wait(f"wait_{name}", *in_flight.pop(name), True, after)
        return [_sum_chips(f"sum_{name}_{i}", p) for i, p in enumerate(landed)]

    def on_grads(l, group, grads):
        if (l, group) == (1, "mlp"):
            return start_exchange("mlp1", grads)
        if (l, group) == (1, "mix"):
            return start_exchange("mix1", grads)
        if (l, group) == (0, "mlp"):
            reduced[(1, "mlp")] = finish_exchange("mlp1", grads[0])
            reduced[(1, "mix")] = finish_exchange("mix1", grads[0])
            return start_exchange("mlp0", grads)
        token = start_exchange("mix0", grads)
        reduced[(0, "mlp")] = finish_exchange("mlp0", token)
        return token

    loss, dx, dg_final, sg, last_started = _local_step(x[0], loss_target[0], small, get_weights, on_grads)

    big_m = (m_w_in, m_w_out, m_w_up, m_w_down)
    big_v = (v_w_in, v_w_out, v_w_up, v_w_down)
    names = ("w_in", "w_out", "w_up", "w_down")
    big_g, big_delta, big_new_m, big_new_v = [None] * 4, [None] * 4, [None] * 4, [None] * 4

    def finish_weights(group, which):
        whole = _share_halves(f"share_{group}", [[reduced[(l, group)][i] for l in range(DEPTH)] for i in range(2)])
        for i, w in enumerate(which):
            shape = big_w[w].shape
            flat = lambda arr: arr.reshape(shape[0] * shape[1], shape[2])
            d, m2, v2 = _adamw(f"adamw_{names[w]}", flat(big_w[w]), flat(whole[i]), flat(big_m[w]), flat(big_v[w]))
            big_g[w], big_delta[w] = whole[i], d.reshape(shape)
            big_new_m[w], big_new_v[w] = m2.reshape(shape), v2.reshape(shape)

    finish_weights("mlp", (2, 3))

    stack2 = lambda key: jnp.stack([sg[l][key] for l in range(DEPTH)])
    pack = _pack_small(stack2("norm_mix"), stack2("attn_out_gain"), stack2("lower"), stack2("hgrn_out_gain"),
                       stack2("norm_mlp"), dg_final, jnp.broadcast_to(loss[0, 0] + last_started[0, 0], (D_MODEL,)))
    g_mix, g_attn, g_lower, g_hgrn, g_mlp, g_final, loss_row = _unpack_small(_all_reduce_small(pack))
    (g_logits,) = lower_vjp(g_lower)

    zeros_row = jnp.zeros((D_MODEL,), F32)
    small_w = (norm_mix, attn_out_gain, hgrn_lb_logits, hgrn_out_gain, norm_mlp, norm_final)
    small_m = (m_norm_mix, m_attn_out_gain, m_hgrn_lb_logits, m_hgrn_out_gain, m_norm_mlp, m_norm_final)
    small_v = (v_norm_mix, v_attn_out_gain, v_hgrn_lb_logits, v_hgrn_out_gain, v_norm_mlp, v_norm_final)
    small_g = (g_mix, g_attn, g_logits, g_hgrn, g_mlp, g_final)
    packs = [_pack_small(*t, zeros_row) for t in (small_w, small_g, small_m, small_v)]
    small_delta, small_new_m, small_new_v = [_unpack_small(p)[:6] for p in _adamw("adamw_small", *packs)]

    reduced[(0, "mix")] = finish_exchange("mix0", small_delta[0])
    finish_weights("mix", (0, 1))

    def ordered(small6, big4):
        mix, attn, lbl, hg, mlp, fin = small6
        return (mix, big4[0], attn, lbl, hg, big4[1], mlp, big4[2], big4[3], fin)

    return ((loss_row[0], dx[None]) + ordered(small_g, big_g) + ordered(small_delta, big_delta)
            + ordered(small_new_m, big_new_m) + ordered(small_new_v, big_new_v))
```

```python
import functools
import math

import numpy as np
import jax
import jax.numpy as jnp
from jax import lax
from jax.experimental import pallas as pl
from jax.experimental.pallas import tpu as pltpu

F32 = jnp.float32
BF16 = jnp.bfloat16
MESH = pl.DeviceIdType.MESH

SEQ = 4096
D_MODEL = 1024
DEPTH = 2
ATTN_WIDTH = 512
HEAD_DIM = 64
HGRN_HEADS = 4
HGRN_DIM = 128
HGRN_WIDTH = 512
IN_W = 3584
MLP_HIDDEN = 4096
N_CHIPS = 4
N_DEV = 8
SHARD_IN = IN_W // N_CHIPS
SHARD_OUT = D_MODEL // N_CHIPS
SHARD_MLP = MLP_HIDDEN // N_CHIPS
DILATIONS = (1, 4, 16)
SPAN = 128
ROPE_THETA = 10000.0
NORM_EPS = 1e-6
MASK_VALUE = -1e30
CHUNK = 128
ROW_TILE = 512
MM_TILE = 512
VMEM_LIMIT = 52 * 1024 * 1024

ADAM_LR = 0.001
ADAM_B1 = 0.9
ADAM_B2 = 0.999
ADAM_EPS = 1e-08
ADAM_WD = 0.01
ADAM_STEP = 10

PACK_ROWS = 16


def _params(n_axes):
    return pltpu.CompilerParams(dimension_semantics=("arbitrary",) * n_axes,
                                vmem_limit_bytes=VMEM_LIMIT)


def _dot(a, b):
    return jnp.dot(a.astype(BF16), b.astype(BF16), preferred_element_type=F32)


def _dot_nt(a, b):
    return lax.dot_general(a.astype(BF16), b.astype(BF16), (((1,), (1,)), ((), ())),
                           preferred_element_type=F32)


def _dot_tn(a, b):
    return lax.dot_general(a.astype(BF16), b.astype(BF16), (((0,), (0,)), ((), ())),
                           preferred_element_type=F32)


def _sigmoid(x):
    return 1.0 / (1.0 + jnp.exp(-x))


def _matmul(name, a, b, a_spec, b_spec, out_shape, out_dtype, out_spec, grid, acc_shape,
            nt=False, extra=None, extra_spec=None, epilogue="none", after=None, relu_outs=None):
    nk = grid[2]
    n_out = 1 if relu_outs is None else 3

    def body(*refs):
        a_ref, b_ref = refs[:2]
        e_ref = None if extra is None else refs[2]
        o_ref = refs[-1 - n_out]
        acc = refs[-1]
        kk = pl.program_id(2)

        def product():
            return _dot_nt(a_ref[...], b_ref[...]) if nt else _dot(a_ref[...], b_ref[...])

        if nk > 1:
            @pl.when(kk == 0)
            def _():
                acc[...] = jnp.zeros_like(acc)

            acc[...] += product()

        @pl.when(kk == nk - 1)
        def _():
            r = acc[...] if nk > 1 else product()
            if epilogue == "add":
                r = r + e_ref[...]
            elif epilogue == "relu2_grad":
                r = r * (2.0 * e_ref[...].astype(F32))
            elif epilogue == "relu2":
                s = jnp.maximum(r, 0.0)
                r = s * s
                refs[-3][...] = s.astype(out_dtype)
                refs[-2][...] = r.T.astype(out_dtype)
            o_ref[...] = r.astype(o_ref.dtype)

    in_specs = [a_spec, b_spec] + ([] if extra is None else [extra_spec])
    args = (a, b) + (() if extra is None else (extra,))
    if after is not None:
        in_specs.append(pl.BlockSpec(memory_space=pl.ANY))
        args += (after,)
    out_specs, out_shapes = out_spec, jax.ShapeDtypeStruct(out_shape, out_dtype)
    if relu_outs is not None:
        out_specs = [out_spec] + [spec for _, spec in relu_outs]
        out_shapes = [out_shapes] + [jax.ShapeDtypeStruct(shape, out_dtype) for shape, _ in relu_outs]
    return pl.pallas_call(
        body, name=name, grid=grid, in_specs=in_specs, out_specs=out_specs, out_shape=out_shapes,
        scratch_shapes=[pltpu.VMEM(acc_shape, F32)],
        compiler_params=_params(3),
    )(*args)


def _spec(shape, index_map):
    return pl.BlockSpec(shape, index_map)


def _mm_pieces(name, a, w, nt, tm, epilogue="none", extra=None, after=None):
    s = a.shape[0]
    pw = w.shape[1] if nt else w.shape[2]
    width = N_CHIPS * pw

    def body(a_ref, w_ref, *rest):
        e_ref = rest[0] if extra is not None else None
        outs = rest[-3:] if epilogue == "relu2" else rest[-1:]
        av = a_ref[...].astype(BF16)
        for j in range(N_CHIPS):
            cols = slice(j * pw, (j + 1) * pw)
            r = _dot_nt(av, w_ref[j]) if nt else _dot(av, w_ref[j])
            if epilogue == "relu2":
                relu = jnp.maximum(r, 0.0)
                r = relu * relu
                outs[1][:, cols] = relu.astype(BF16)
                outs[2][cols, :] = r.T.astype(BF16)
            elif epilogue == "relu2_grad":
                r = r * (2.0 * e_ref[:, cols].astype(F32))
            outs[0][:, cols] = r.astype(outs[0].dtype)

    row = lambda width_: _spec((tm, width_), lambda i: (i, 0))
    in_specs = [row(a.shape[1]), _spec(w.shape, lambda i: (0, 0, 0))]
    args = [a, w]
    if extra is not None:
        in_specs.append(row(width))
        args.append(extra)
    if after is not None:
        in_specs.append(pl.BlockSpec(memory_space=pl.ANY))
        args.append(after)
    if epilogue == "relu2":
        out_specs = [row(width), row(width), _spec((width, tm), lambda i: (0, i))]
        out_shape = [jax.ShapeDtypeStruct((s, width), BF16)] * 2 + [jax.ShapeDtypeStruct((width, s), BF16)]
    else:
        out_specs = row(width)
        out_shape = jax.ShapeDtypeStruct((s, width), BF16 if epilogue == "relu2_grad" else F32)
    return pl.pallas_call(body, name=name, grid=(s // tm,), in_specs=in_specs, out_specs=out_specs,
                          out_shape=out_shape, compiler_params=_params(1))(*args)


def _mm_accum(name, a, w, nt, tm, resid, norm=None):
    s = a.shape[0]
    pk = w.shape[2] if nt else w.shape[1]
    d = w.shape[1] if nt else w.shape[2]

    def body(a_ref, w_ref, resid_ref, *rest):
        r = None
        for j in range(N_CHIPS):
            piece = a_ref[:, j * pk:(j + 1) * pk].astype(BF16)
            term = _dot_nt(piece, w_ref[j]) if nt else _dot(piece, w_ref[j])
            r = term if r is None else r + term
        if norm is None:
            rest[0][...] = r + resid_ref[...]
            return
        x_ref, g_ref, dx_ref, dxb_ref, dg_ref = rest

        @pl.when(pl.program_id(0) == 0)
        def _():
            dg_ref[...] = jnp.zeros_like(dg_ref)

        xv = x_ref[...]
        rs = lax.rsqrt(jnp.mean(xv * xv, axis=1, keepdims=True) + NORM_EPS)
        xhat = xv * rs
        rg = r * g_ref[...]
        dx = resid_ref[...] + rs * (rg - xhat * jnp.mean(rg * xhat, axis=1, keepdims=True))
        dx_ref[...] = dx
        dxb_ref[...] = dx.astype(BF16)
        dg_ref[...] += jnp.sum(r * xhat, axis=0, keepdims=True)

    row = lambda width: _spec((tm, width), lambda i: (i, 0))
    in_specs = [row(a.shape[1]), _spec(w.shape, lambda i: (0, 0, 0)), row(d)]
    args = [a, w, resid]
    if norm is None:
        out_specs, out_shape = row(d), jax.ShapeDtypeStruct((s, d), F32)
    else:
        in_specs += [row(d), _spec((1, d), lambda i: (0, 0))]
        args += list(norm)
        out_specs = [row(d), row(d), _spec((1, d), lambda i: (0, 0))]
        out_shape = [jax.ShapeDtypeStruct((s, d), F32), jax.ShapeDtypeStruct((s, d), BF16),
                     jax.ShapeDtypeStruct((1, d), F32)]
    return pl.pallas_call(body, name=name, grid=(s // tm,), in_specs=in_specs, out_specs=out_specs,
                          out_shape=out_shape, compiler_params=_params(1))(*args)


def _mm_dw(name, a_t, b, by_cols, tk):
    m, s = a_t.shape
    n = b.shape[1]
    shape = (N_CHIPS, m, n // N_CHIPS) if by_cols else (N_CHIPS, m // N_CHIPS, n)

    def body(a_ref, b_ref, o_ref):
        @pl.when(pl.program_id(0) == 0)
        def _():
            o_ref[...] = jnp.zeros_like(o_ref)

        for j in range(N_CHIPS):
            if by_cols:
                o_ref[j] += _dot(a_ref[...], b_ref[:, j * shape[2]:(j + 1) * shape[2]])
            else:
                o_ref[j] += _dot(a_ref[j * shape[1]:(j + 1) * shape[1], :], b_ref[...])

    return pl.pallas_call(
        body, name=name, grid=(s // tk,),
        in_specs=[_spec((m, tk), lambda k: (0, k)), _spec((tk, n), lambda k: (k, 0))],
        out_specs=_spec(shape, lambda k: (0, 0, 0)), out_shape=jax.ShapeDtypeStruct(shape, F32),
        compiler_params=_params(1))(a_t, b)


def _rms_fwd(name, x, gain, after=None):
    s, d = x.shape
    t = ROW_TILE

    def body(x_ref, g_ref, *rest):
        h_ref, ht_ref = rest[-2:]
        xv = x_ref[...]
        r = lax.rsqrt(jnp.mean(xv * xv, axis=1, keepdims=True) + NORM_EPS)
        h = xv * r * g_ref[...]
        h_ref[...] = h.astype(BF16)
        ht_ref[...] = h.T.astype(BF16)

    in_specs = [_spec((t, d), lambda i: (i, 0)), _spec((1, d), lambda i: (0, 0))]
    args = (x, gain)
    if after is not None:
        in_specs.append(pl.BlockSpec(memory_space=pl.ANY))
        args += (after,)
    return pl.pallas_call(
        body, name=name, grid=(s // t,), in_specs=in_specs,
        out_specs=[_spec((t, d), lambda i: (i, 0)), _spec((d, t), lambda i: (0, i))],
        out_shape=[jax.ShapeDtypeStruct((s, d), BF16), jax.ShapeDtypeStruct((d, s), BF16)],
        compiler_params=_params(1),
    )(*args)


def _rms_bwd(name, dh, x, gain, dres):
    s, d = x.shape
    t = ROW_TILE

    def body(dh_ref, x_ref, g_ref, dres_ref, dx_ref, dxb_ref, dg_ref):
        @pl.when(pl.program_id(0) == 0)
        def _():
            dg_ref[...] = jnp.zeros_like(dg_ref)

        xv = x_ref[...]
        dhv = dh_ref[...]
        r = lax.rsqrt(jnp.mean(xv * xv, axis=1, keepdims=True) + NORM_EPS)
        xhat = xv * r
        dhg = dhv * g_ref[...]
        proj = jnp.mean(dhg * xhat, axis=1, keepdims=True)
        dx = dres_ref[...] + r * (dhg - xhat * proj)
        dx_ref[...] = dx
        dxb_ref[...] = dx.astype(BF16)
        dg_ref[...] += jnp.sum(dhv * xhat, axis=0, keepdims=True)

    return pl.pallas_call(
        body, name=name, grid=(s // t,),
        in_specs=[_spec((t, d), lambda i: (i, 0)), _spec((t, d), lambda i: (i, 0)),
                  _spec((1, d), lambda i: (0, 0)), _spec((t, d), lambda i: (i, 0))],
        out_specs=[_spec((t, d), lambda i: (i, 0)), _spec((t, d), lambda i: (i, 0)),
                   _spec((1, d), lambda i: (0, 0))],
        out_shape=[jax.ShapeDtypeStruct((s, d), F32), jax.ShapeDtypeStruct((s, d), BF16),
                   jax.ShapeDtypeStruct((1, d), F32)],
        compiler_params=_params(1),
    )(dh, x, gain, dres)


def _loss_head(x, gain, target):
    s, d = x.shape
    t = ROW_TILE
    n_steps = s // t

    def body(x_ref, g_ref, t_ref, dx_ref, dxb_ref, dg_ref, loss_ref, acc):
        i = pl.program_id(0)

        @pl.when(i == 0)
        def _():
            dg_ref[...] = jnp.zeros_like(dg_ref)
            acc[...] = jnp.zeros_like(acc)

        xv = x_ref[...]
        g = g_ref[...]
        r = lax.rsqrt(jnp.mean(xv * xv, axis=1, keepdims=True) + NORM_EPS)
        xhat = xv * r
        err = xhat * g - t_ref[...]
        acc[...] += jnp.sum(err * err, axis=0, keepdims=True)
        dy = err * (1.0 / d)
        dyg = dy * g
        proj = jnp.mean(dyg * xhat, axis=1, keepdims=True)
        dx = r * (dyg - xhat * proj)
        dx_ref[...] = dx
        dxb_ref[...] = dx.astype(BF16)
        dg_ref[...] += jnp.sum(dy * xhat, axis=0, keepdims=True)

        @pl.when(i == n_steps - 1)
        def _():
            total = jnp.sum(acc[...], axis=1, keepdims=True) * (0.5 / d)
            loss_ref[...] = jnp.broadcast_to(total, loss_ref.shape)

    return pl.pallas_call(
        body, name="loss_head", grid=(n_steps,),
        in_specs=[_spec((t, d), lambda i: (i, 0)), _spec((1, d), lambda i: (0, 0)),
                  _spec((t, d), lambda i: (i, 0))],
        out_specs=[_spec((t, d), lambda i: (i, 0)), _spec((t, d), lambda i: (i, 0)),
                   _spec((1, d), lambda i: (0, 0)), _spec((1, 128), lambda i: (0, 0))],
        out_shape=[jax.ShapeDtypeStruct((s, d), F32), jax.ShapeDtypeStruct((s, d), BF16),
                   jax.ShapeDtypeStruct((1, d), F32), jax.ShapeDtypeStruct((1, 128), F32)],
        scratch_shapes=[pltpu.VMEM((1, d), F32)],
        compiler_params=_params(1),
    )(x, gain, target)


def _rope_tables():
    half = HEAD_DIM // 2
    inv_freq = ROPE_THETA ** (-jnp.arange(half, dtype=F32) / half)
    ang = jnp.arange(SEQ, dtype=jnp.int32).astype(F32)[:, None] * inv_freq[None, :]
    cos, sin = jnp.cos(ang), jnp.sin(ang)
    cos_t = jnp.concatenate([cos, cos, cos, cos], axis=1)
    sin_t = jnp.concatenate([-sin, sin, -sin, sin], axis=1)
    return cos_t, sin_t


def _swap_halves(x):
    lane = lax.broadcasted_iota(jnp.int32, x.shape, 1)
    first = (lane % HEAD_DIM) < (HEAD_DIM // 2)
    return jnp.where(first, pltpu.roll(x, 128 - HEAD_DIM // 2, 1), pltpu.roll(x, HEAD_DIM // 2, 1))


def _permuted_specs(t, width):
    specs = [_spec((t, width), lambda i: (i, 0))]
    for d in DILATIONS[1:]:
        specs.append(_spec((d, t // d, width), lambda i: (0, i, 0)))
    return specs


def _permuted_shapes(width, dtype):
    shapes = [jax.ShapeDtypeStruct((SEQ, width), dtype)]
    for d in DILATIONS[1:]:
        shapes.append(jax.ShapeDtypeStruct((d, SEQ // d, width), dtype))
    return shapes


def _attn_prep(name, proj, cos_t, sin_t):
    t = ROW_TILE
    w = ATTN_WIDTH

    def body(q_ref, k_ref, v_ref, cos_ref, sin_ref, *rest):
        outs, scr = rest[:9], rest[9]
        cosv, sinv = cos_ref[...], sin_ref[...]
        for a, (src, roped, scale) in enumerate(((q_ref, True, HEAD_DIM ** -0.5),
                                                 (k_ref, True, 1.0), (v_ref, False, 1.0))):
            o1, o4, o16 = outs[3 * a:3 * a + 3]
            for cb in range(w // 128):
                cols = slice(cb * 128, (cb + 1) * 128)
                val = src[:, cols]
                if roped:
                    val = (val * cosv + _swap_halves(val) * sinv) * scale
                scr[...] = val
                o1[:, cols] = val.astype(BF16)
                for o_ref, d in ((o4, 4), (o16, 16)):
                    for r in range(d):
                        o_ref[r, :, cols] = scr[pl.ds(r, t // d, stride=d), :].astype(BF16)

    out_specs = _permuted_specs(t, w) * 3
    out_shape = _permuted_shapes(w, BF16) * 3
    outs = pl.pallas_call(
        body, name=name, grid=(SEQ // t,),
        in_specs=[_spec((t, w), lambda i: (i, 0)), _spec((t, w), lambda i: (i, 1)),
                  _spec((t, w), lambda i: (i, 2)),
                  _spec((t, 128), lambda i: (i, 0)), _spec((t, 128), lambda i: (i, 0))],
        out_specs=out_specs, out_shape=out_shape,
        scratch_shapes=[pltpu.VMEM((t, 128), F32)],
        compiler_params=_params(1),
    )(proj, proj, proj, cos_t, sin_t)
    q, k, v = outs[0:3], outs[3:6], outs[6:9]
    flat = lambda arr: arr.reshape(SEQ, w)
    return [(flat(q[p]), flat(k[p]), flat(v[p])) for p in range(3)]


def _band_masks():
    row = lax.broadcasted_iota(jnp.int32, (2 * SPAN, 2 * SPAN), 0) % SPAN
    col = lax.broadcasted_iota(jnp.int32, (2 * SPAN, 2 * SPAN), 1)
    is_prev = col < SPAN
    band = (is_prev & (col >= row)) | (~is_prev & (col - SPAN <= row))
    head0 = lax.broadcasted_iota(jnp.int32, (SPAN, 128), 1) < HEAD_DIM
    return band, is_prev, head0


def _stack_heads(x, head0):
    zero = jnp.zeros_like(x)
    return jnp.concatenate([jnp.where(head0, x, zero), jnp.where(head0, zero, x)], axis=0)


def _attn_fwd(name, q, k, v, seg_blocks):
    n_blocks = SEQ // SPAN

    def body(q_ref, k_ref, v_ref, o_ref, lse_ref):
        band, is_prev, head0 = _band_masks()

        def step(b, carry):
            cur = pl.ds(pl.multiple_of(b * SPAN, SPAN), SPAN)
            prev = pl.ds(pl.multiple_of(jnp.maximum(b - 1, 0) * SPAN, SPAN), SPAN)
            qs = _stack_heads(q_ref[cur, :], head0)
            kcat = jnp.concatenate([k_ref[prev, :], k_ref[cur, :]], axis=0)
            vcat = jnp.concatenate([v_ref[prev, :], v_ref[cur, :]], axis=0)
            ok = band & (((b % seg_blocks) != 0) | ~is_prev)
            s = jnp.where(ok, _dot_nt(qs, kcat), MASK_VALUE)
            m = jnp.max(s, axis=1, keepdims=True)
            p = jnp.exp(s - m)
            l = jnp.sum(p, axis=1, keepdims=True)
            pv = _dot(p, vcat) * (1.0 / l)
            lse = m + jnp.log(l)
            o_ref[cur, :] = jnp.where(head0, pv[:SPAN], pv[SPAN:])
            lse_ref[cur, :] = jnp.where(head0, lse[:SPAN], lse[SPAN:])
            return carry

        lax.fori_loop(0, n_blocks, step, 0, unroll=4)

    col = _spec((SEQ, 128), lambda j: (0, j))
    return pl.pallas_call(
        body, name=name, grid=(ATTN_WIDTH // 128,),
        in_specs=[col, col, col], out_specs=[col, col],
        out_shape=[jax.ShapeDtypeStruct((SEQ, ATTN_WIDTH), F32)] * 2,
        compiler_params=_params(1),
    )(q, k, v)


def _unpermute(dst, src_ref, d, cols):
    n = dst.shape[0] // d
    for r in range(d):
        dst[pl.ds(r, n, stride=d), :] = src_ref[r, :, cols]


def _attn_merge(name, outs, lses, gain):
    t = ROW_TILE
    w = ATTN_WIDTH

    def body(o1, o4, o16, l1, l4, l16, g_ref, an_ref, ant_ref, attn_ref, lse_ref, so4, so16, sl4, sl16):
        for cb in range(w // 128):
            cols = slice(cb * 128, (cb + 1) * 128)
            _unpermute(so4, o4, 4, cols)
            _unpermute(so16, o16, 16, cols)
            _unpermute(sl4, l4, 4, cols)
            _unpermute(sl16, l16, 16, cols)
            la, lb, lc = l1[:, cols], sl4[...], sl16[...]
            m = jnp.maximum(jnp.maximum(la, lb), lc)
            ea, eb, ec = jnp.exp(la - m), jnp.exp(lb - m), jnp.exp(lc - m)
            tot = ea + eb + ec
            attn_ref[:, cols] = (ea * o1[:, cols] + eb * so4[...] + ec * so16[...]) / tot
            lse_ref[:, cols] = m + jnp.log(tot)
        attn = attn_ref[...]
        r = lax.rsqrt(jnp.mean(attn * attn, axis=1, keepdims=True) + NORM_EPS)
        an = attn * r * g_ref[...]
        an_ref[...] = an.astype(BF16)
        ant_ref[...] = an.T.astype(BF16)

    views = lambda arrs: [arrs[0], arrs[1].reshape(4, SEQ // 4, w), arrs[2].reshape(16, SEQ // 16, w)]
    row = _spec((t, w), lambda i: (i, 0))
    return pl.pallas_call(
        body, name=name, grid=(SEQ // t,),
        in_specs=_permuted_specs(t, w) * 2 + [_spec((1, w), lambda i: (0, 0))],
        out_specs=[row, _spec((w, t), lambda i: (0, i)), row, row],
        out_shape=[jax.ShapeDtypeStruct((SEQ, 2 * w), BF16), jax.ShapeDtypeStruct((2 * w, SEQ), BF16),
                   jax.ShapeDtypeStruct((SEQ, w), F32), jax.ShapeDtypeStruct((SEQ, w), F32)],
        scratch_shapes=[pltpu.VMEM((t, 128), F32)] * 4,
        compiler_params=_params(1),
    )(*views(outs), *views(lses), gain)


def _head_sum_matrix():
    i = np.arange(ATTN_WIDTH)
    return jnp.asarray((i[:, None] // HEAD_DIM) == (i[None, :] // HEAD_DIM), dtype=F32)


def _attn_bwd_prep(name, d_an, attn, lse, gain, head_sum):
    t = ROW_TILE
    w = ATTN_WIDTH

    def body(dan_ref, attn_ref, lse_ref, g_ref, hs_ref, *rest):
        (do1, do4, do16, dl1, dl4, dl16, ls4, ls16, dg_ref), (sdo, sdl, sls) = rest[:9], rest[9:]

        @pl.when(pl.program_id(0) == 0)
        def _():
            dg_ref[...] = jnp.zeros_like(dg_ref)

        attn = attn_ref[...]
        dan = dan_ref[...]
        r = lax.rsqrt(jnp.mean(attn * attn, axis=1, keepdims=True) + NORM_EPS)
        xhat = attn * r
        dg_ref[...] += jnp.sum(dan * xhat, axis=0, keepdims=True)
        dang = dan * g_ref[...]
        d_o = r * (dang - xhat * jnp.mean(dang * xhat, axis=1, keepdims=True))
        delta = jnp.dot(d_o * attn, hs_ref[...], preferred_element_type=F32,
                        precision=lax.Precision.HIGHEST)
        do1[...] = d_o.astype(BF16)
        dl1[...] = delta
        for cb in range(w // 128):
            cols = slice(cb * 128, (cb + 1) * 128)
            sdo[...] = d_o[:, cols]
            sdl[...] = delta[:, cols]
            sls[...] = lse_ref[:, cols]
            for d, o_do, o_dl, o_ls in ((4, do4, dl4, ls4), (16, do16, dl16, ls16)):
                for rr in range(d):
                    rows = pl.ds(rr, t // d, stride=d)
                    o_do[rr, :, cols] = sdo[rows, :].astype(BF16)
                    o_dl[rr, :, cols] = sdl[rows, :]
                    o_ls[rr, :, cols] = sls[rows, :]

    row = _spec((t, w), lambda i: (i, 0))
    perm = _permuted_specs(t, w)
    outs = pl.pallas_call(
        body, name=name, grid=(SEQ // t,),
        in_specs=[row, row, row, _spec((1, w), lambda i: (0, 0)), _spec((w, w), lambda i: (0, 0))],
        out_specs=perm + perm + perm[1:] + [_spec((1, w), lambda i: (0, 0))],
        out_shape=(_permuted_shapes(w, BF16) + _permuted_shapes(w, F32) + _permuted_shapes(w, F32)[1:]
                   + [jax.ShapeDtypeStruct((1, w), F32)]),
        scratch_shapes=[pltpu.VMEM((t, 128), F32)] * 3,
        compiler_params=_params(1),
    )(d_an, attn, lse, gain, head_sum)
    flat = lambda arr: arr.reshape(SEQ, w)
    d_out = [flat(a) for a in outs[0:3]]
    delta = [flat(a) for a in outs[3:6]]
    lses = [lse, flat(outs[6]), flat(outs[7])]
    return d_out, delta, lses, outs[8]


def _attn_bwd(name, q, k, v, d_out, delta, lse, seg_blocks):
    n_blocks = SEQ // SPAN

    def body(q_ref, k_ref, v_ref, do_ref, dl_ref, lse_ref, dq_ref, dk_ref, dv_ref):
        band, is_prev, head0 = _band_masks()
        dk_ref[...] = jnp.zeros_like(dk_ref)
        dv_ref[...] = jnp.zeros_like(dv_ref)

        def per_head(x):
            return jnp.concatenate([x[:, 0:1], x[:, HEAD_DIM:HEAD_DIM + 1]], axis=0)

        def step(b, carry):
            cur = pl.ds(pl.multiple_of(b * SPAN, SPAN), SPAN)
            prev = pl.ds(pl.multiple_of(jnp.maximum(b - 1, 0) * SPAN, SPAN), SPAN)
            qs = _stack_heads(q_ref[cur, :], head0)
            dos = _stack_heads(do_ref[cur, :], head0)
            kcat = jnp.concatenate([k_ref[prev, :], k_ref[cur, :]], axis=0)
            vcat = jnp.concatenate([v_ref[prev, :], v_ref[cur, :]], axis=0)
            ok = band & (((b % seg_blocks) != 0) | ~is_prev)
            p = jnp.where(ok, jnp.exp(_dot_nt(qs, kcat) - per_head(lse_ref[cur, :])), 0.0)
            ds = p * (_dot_nt(dos, vcat) - per_head(dl_ref[cur, :]))
            dq = _dot(ds, kcat)
            dq_ref[cur, :] = jnp.where(head0, dq[:SPAN], dq[SPAN:])
            dk = _dot_tn(ds, qs)
            dv = _dot_tn(p, dos)
            dk_ref[prev, :] += dk[:SPAN]
            dv_ref[prev, :] += dv[:SPAN]
            dk_ref[cur, :] += dk[SPAN:]
            dv_ref[cur, :] += dv[SPAN:]
            return carry

        lax.fori_loop(0, n_blocks, step, 0, unroll=4)

    col = _spec((SEQ, 128), lambda j: (0, j))
    return pl.pallas_call(
        body, name=name, grid=(ATTN_WIDTH // 128,),
        in_specs=[col] * 6, out_specs=[col] * 3,
        out_shape=[jax.ShapeDtypeStruct((SEQ, ATTN_WIDTH), F32)] * 3,
        compiler_params=_params(1),
    )(q, k, v, d_out, delta, lse)


def _attn_bwd_post(name, grads, cos_t, sin_t):
    t = ROW_TILE
    w = ATTN_WIDTH

    def body(*refs):
        ins, cos_ref, sin_ref, out_ref, s4, s16 = refs[:9], refs[9], refs[10], refs[11], refs[12], refs[13]
        cosv, sinv = cos_ref[...], sin_ref[...]
        for a in range(3):
            g1, g4, g16 = ins[a], ins[3 + a], ins[6 + a]
            for cb in range(w // 128):
                cols = slice(cb * 128, (cb + 1) * 128)
                _unpermute(s4, g4, 4, cols)
                _unpermute(s16, g16, 16, cols)
                val = g1[:, cols] + s4[...] + s16[...]
                if a < 2:
                    val = val * cosv + _swap_halves(val * sinv)
                if a == 0:
                    val = val * (HEAD_DIM ** -0.5)
                out_ref[:, a * w + cb * 128:a * w + (cb + 1) * 128] = val.astype(BF16)

    views = []
    for p, d in enumerate(DILATIONS):
        for a in range(3):
            views.append(grads[p][a] if d == 1 else grads[p][a].reshape(d, SEQ // d, w))
    perm = _permuted_specs(t, w)
    in_specs = [perm[0]] * 3 + [perm[1]] * 3 + [perm[2]] * 3
    return pl.pallas_call(
        body, name=name, grid=(SEQ // t,),
        in_specs=in_specs + [_spec((t, 128), lambda i: (i, 0))] * 2,
        out_specs=_spec((t, 3 * w), lambda i: (i, 0)),
        out_shape=jax.ShapeDtypeStruct((SEQ, 3 * w), BF16),
        scratch_shapes=[pltpu.VMEM((t, 128), F32)] * 2,
        compiler_params=_params(1),
    )(*views, cos_t, sin_t)


N_LEVELS = 7


def _hgrn_consts():
    c = CHUNK
    i = np.arange(c)[:, None]
    s = np.arange(c)[None, :]
    blocks = [s <= i]
    for lv in range(N_LEVELS):
        bs = c >> lv
        h = bs // 2
        m = (i // bs) * bs + h - 1
        second = (i % bs) >= h
        blocks.append((second & (s > m) & (s <= i)) | (~second & (s > i) & (s <= m)))
    blocks.append(s > i)
    stack = np.concatenate(blocks, axis=0).astype(np.float32)
    return jnp.asarray(stack, dtype=BF16), jnp.asarray(stack.T, dtype=BF16)


def _exact_dot(m01, x):
    hi = x.astype(BF16)
    lo = (x - hi.astype(F32)).astype(BF16)
    n = x.shape[1]
    full = jnp.dot(m01, jnp.concatenate([hi, lo], axis=1), preferred_element_type=F32)
    return full[:, :n] + full[:, n:]


def _hgrn_gates(qh, z, lb):
    sq = _sigmoid(qh)
    q = qh * sq * (HGRN_DIM ** -0.5)
    sig = _sigmoid(z)
    sigm = _sigmoid(-z)
    f = lb + (1.0 - lb) * sig
    k = (1.0 - lb) * sigm
    return q, k, f, sq, sig, sigm


def _level_masks(lv):
    row = lax.broadcasted_iota(jnp.int32, (CHUNK, CHUNK), 0)
    col = lax.broadcasted_iota(jnp.int32, (CHUNK, CHUNK), 1)
    shift = N_LEVELS - lv
    second = (row & (CHUNK >> (lv + 1))) != 0
    same = (row >> shift) == (col >> shift)
    return second, same


def _hgrn_fwd(name, proj, lb, gain, stack, mixed, mixed_t):
    t = ROW_TILE
    per = t // CHUNK
    n_rb = SEQ // t
    n_chunks = SEQ // CHUNK
    col0 = 3 * ATTN_WIDTH // 128

    def body(q_ref, f_ref, i_ref, g_ref, lb_ref, gain_ref, stack_ref, mixed_in, mixed_t_in,
             rec_ref, rect_ref, o_ref, st_out, a_out, st):
        del mixed_in, mixed_t_in

        @pl.when(pl.program_id(1) == 0)
        def _():
            st[...] = jnp.zeros_like(st)

        lbv = lb_ref[...]
        row = lax.broadcasted_iota(jnp.int32, (CHUNK, CHUNK), 0)
        col = lax.broadcasted_iota(jnp.int32, (CHUNK, CHUNK), 1)
        for c in range(per):
            rows = slice(c * CHUNK, (c + 1) * CHUNK)
            qh, z, v, gh = q_ref[rows, :], f_ref[rows, :], i_ref[rows, :], g_ref[rows, :]
            q, k, f, _, _, _ = _hgrn_gates(qh, z, lbv)
            dec = _exact_dot(stack_ref[...], jnp.log(f))
            g = dec[0:CHUNK]
            to_end = dec[(N_LEVELS + 1) * CHUNK:(N_LEVELS + 2) * CHUNK]
            a = jnp.where(row == col, jnp.sum(q * k, axis=1, keepdims=True), 0.0)
            for lv in range(N_LEVELS):
                e = jnp.exp(dec[(lv + 1) * CHUNK:(lv + 2) * CHUNK])
                second, same = _level_masks(lv)
                qt = jnp.where(second, q * e, 0.0)
                kt = jnp.where(second, 0.0, k * e)
                a = a + jnp.where(same, _dot_nt(qt, kt), 0.0)
            st_prev = st[...]
            st_out[c] = st_prev
            a_out[c] = a
            o = _dot(a, v) + _dot_nt(q * jnp.exp(g), st_prev)
            k_end = k * jnp.exp(to_end)
            st[...] = st_prev * jnp.exp(g[CHUNK - 1:CHUNK, :]) + _dot(v.T, k_end)
            o_ref[rows, :] = o
            r = lax.rsqrt(jnp.mean(o * o, axis=1, keepdims=True) + NORM_EPS)
            rec = o * r * gain_ref[...] * (gh * _sigmoid(gh))
            rec_ref[rows, :] = rec.astype(BF16)
            rect_ref[:, rows] = rec.T.astype(BF16)

    def col_spec(tt):
        return _spec((t, HGRN_DIM), lambda h, rb: (rb, col0 + HGRN_HEADS * tt + h))

    chunk_spec = _spec((None, per, CHUNK, CHUNK), lambda h, rb: (h, rb, 0, 0))
    return pl.pallas_call(
        body, name=name, grid=(HGRN_HEADS, n_rb),
        in_specs=[col_spec(0), col_spec(1), col_spec(2), col_spec(3),
                  _spec((None, 1, HGRN_DIM), lambda h, rb: (h, 0, 0)),
                  _spec((1, HGRN_DIM), lambda h, rb: (0, 0)),
                  _spec(stack.shape, lambda h, rb: (0, 0)), ANY_SPEC, ANY_SPEC],
        out_specs=[_spec((t, HGRN_DIM), lambda h, rb: (rb, ATTN_WIDTH // HGRN_DIM + h)),
                   _spec((HGRN_DIM, t), lambda h, rb: (ATTN_WIDTH // HGRN_DIM + h, rb)),
                   _spec((t, HGRN_DIM), lambda h, rb: (rb, h)),
                   chunk_spec, chunk_spec],
        out_shape=[jax.ShapeDtypeStruct(mixed.shape, BF16),
                   jax.ShapeDtypeStruct(mixed_t.shape, BF16),
                   jax.ShapeDtypeStruct((SEQ, HGRN_WIDTH), F32),
                   jax.ShapeDtypeStruct((HGRN_HEADS, n_chunks, CHUNK, CHUNK), F32),
                   jax.ShapeDtypeStruct((HGRN_HEADS, n_chunks, CHUNK, CHUNK), F32)],
        scratch_shapes=[pltpu.VMEM((CHUNK, CHUNK), F32)],
        input_output_aliases={7: 0, 8: 1},
        compiler_params=_params(2),
    )(proj, proj, proj, proj, lb, gain, stack, mixed, mixed_t)


def _hgrn_bwd(name, proj, d_rec, o_pre, states, scores, lb, gain, stack, stack_t):
    t = ROW_TILE
    per = t // CHUNK
    n_rb = SEQ // t
    col0 = 3 * ATTN_WIDTH // 128

    def body(q_ref, f_ref, i_ref, g_ref, drec_ref, o_ref, st_ref, a_ref, lb_ref, gain_ref,
             stack_ref, stack_t_ref, dq_ref, df_ref, di_ref, dg_ref, dlb_ref, dgain_ref, dst):
        @pl.when(pl.program_id(1) == 0)
        def _():
            dst[...] = jnp.zeros_like(dst)
            dlb_ref[...] = jnp.zeros_like(dlb_ref)
            dgain_ref[...] = jnp.zeros_like(dgain_ref)

        lbv = lb_ref[...]
        gain_v = gain_ref[...]
        row = lax.broadcasted_iota(jnp.int32, (CHUNK, CHUNK), 0)
        col = lax.broadcasted_iota(jnp.int32, (CHUNK, CHUNK), 1)
        for c in reversed(range(per)):
            rows = slice(c * CHUNK, (c + 1) * CHUNK)
            qh, z, v, gh = q_ref[rows, :], f_ref[rows, :], i_ref[rows, :], g_ref[rows, :]
            q, k, f, sq, sig, sigm = _hgrn_gates(qh, z, lbv)
            dec = _exact_dot(stack_ref[...], jnp.log(f))
            g = dec[0:CHUNK]
            to_end = dec[(N_LEVELS + 1) * CHUNK:(N_LEVELS + 2) * CHUNK]
            e_g = jnp.exp(g)
            e_end = jnp.exp(to_end)
            e_last = jnp.exp(g[CHUNK - 1:CHUNK, :])
            q_in = q * e_g
            k_end = k * e_end
            st_prev = st_ref[c]
            a = a_ref[c]
            dst_new = dst[...]

            o = o_ref[rows, :]
            drec = drec_ref[rows, :]
            sg = _sigmoid(gh)
            r = lax.rsqrt(jnp.mean(o * o, axis=1, keepdims=True) + NORM_EPS)
            ohat = o * r
            d_gh = drec * (ohat * gain_v) * (sg * (1.0 + gh * (1.0 - sg)))
            d_on = drec * (gh * sg)
            dgain_ref[...] += jnp.sum(d_on * ohat, axis=0, keepdims=True)
            d_ohat = d_on * gain_v
            d_o = r * (d_ohat - ohat * jnp.mean(d_ohat * ohat, axis=1, keepdims=True))

            d_a = jnp.where(row >= col, _dot_nt(d_o, v), 0.0)
            d_at = jnp.where(col >= row, _dot_nt(v, d_o), 0.0)
            d_v = _dot(a.T, d_o) + _dot_nt(k_end, dst_new)
            d_q_in = _dot(d_o, st_prev)
            d_k_end = _dot(v, dst_new)
            d_q = d_q_in * e_g
            d_k = d_k_end * e_end
            diag = jnp.sum(d_o * v, axis=1, keepdims=True)
            d_q = d_q + diag * k
            d_k = d_k + diag * q
            d_dec = [q_in * d_q_in]
            for lv in range(N_LEVELS):
                e = jnp.exp(dec[(lv + 1) * CHUNK:(lv + 2) * CHUNK])
                second, same = _level_masks(lv)
                qt = jnp.where(second, q * e, 0.0)
                kt = jnp.where(second, 0.0, k * e)
                d_qt = _dot(jnp.where(same, d_a, 0.0), kt)
                d_kt = _dot(jnp.where(same, d_at, 0.0), qt)
                d_q = d_q + jnp.where(second, d_qt * e, 0.0)
                d_k = d_k + jnp.where(second, 0.0, d_kt * e)
                d_dec.append(jnp.where(second, qt * d_qt, kt * d_kt))
            d_dec.append(k_end * d_k_end)
            flux = jnp.sum(dst_new * st_prev, axis=0, keepdims=True) * e_last
            d_lf = _exact_dot(stack_t_ref[...], jnp.concatenate(d_dec, axis=0)) + flux
            dst[...] = dst_new * e_last + _dot(d_o.T, q_in)

            d_f = d_lf / f - d_k
            dlb_ref[...] += jnp.sum(d_f * sigm, axis=0, keepdims=True)
            dq_ref[rows, :] = (d_q * (HGRN_DIM ** -0.5) * (sq * (1.0 + qh * (1.0 - sq)))).astype(BF16)
            df_ref[rows, :] = (d_f * (1.0 - lbv) * sig * sigm).astype(BF16)
            di_ref[rows, :] = d_v.astype(BF16)
            dg_ref[rows, :] = d_gh.astype(BF16)

    last = n_rb - 1

    def col_spec(tt):
        return _spec((t, HGRN_DIM), lambda h, rb: (last - rb, col0 + HGRN_HEADS * tt + h))

    head_col = _spec((t, HGRN_DIM), lambda h, rb: (last - rb, h))
    rec_col0 = d_rec.shape[1] // HGRN_DIM - HGRN_HEADS
    d_rec_col = _spec((t, HGRN_DIM), lambda h, rb: (last - rb, rec_col0 + h))
    chunk_spec = _spec((None, per, CHUNK, CHUNK), lambda h, rb: (h, last - rb, 0, 0))
    vec_spec = _spec((None, 1, HGRN_DIM), lambda h, rb: (h, 0, 0))
    outs = pl.pallas_call(
        body, name=name, grid=(HGRN_HEADS, n_rb),
        in_specs=[col_spec(0), col_spec(1), col_spec(2), col_spec(3), d_rec_col, head_col,
                  chunk_spec, chunk_spec, vec_spec,
                  _spec((1, HGRN_DIM), lambda h, rb: (0, 0)),
                  _spec(stack.shape, lambda h, rb: (0, 0)), _spec(stack_t.shape, lambda h, rb: (0, 0))],
        out_specs=[head_col] * 4 + [vec_spec, vec_spec],
        out_shape=[jax.ShapeDtypeStruct((SEQ, HGRN_WIDTH), BF16)] * 4
                  + [jax.ShapeDtypeStruct((HGRN_HEADS, 1, HGRN_DIM), F32)] * 2,
        scratch_shapes=[pltpu.VMEM((CHUNK, CHUNK), F32)],
        compiler_params=_params(2),
    )(proj, proj, proj, proj, d_rec, o_pre, states, scores, lb, gain, stack, stack_t)
    return outs


ANY_SPEC = pl.BlockSpec(memory_space=pl.ANY)


def _my_place():
    return lax.axis_index("x"), lax.axis_index("y"), lax.axis_index("c")


def _other_chips(x, y):
    return [(1 - x, y), (x, 1 - y), (1 - x, 1 - y)]


def _remote(src, dst, send_sem, recv_sem, device):
    return pltpu.make_async_remote_copy(src_ref=src, dst_ref=dst, send_sem=send_sem, recv_sem=recv_sem,
                                        device_id=device, device_id_type=MESH)


def _staged_copies(srcs, dsts, stage, sems):
    loads = [pltpu.make_async_copy(srcs[i], stage[i], sems.at[i]) for i in range(len(srcs))]
    for cp in loads:
        cp.start()
    stores = []
    for i, cp in enumerate(loads):
        cp.wait()
        stores.append(pltpu.make_async_copy(stage[i], dsts[i], sems.at[i]))
        stores[-1].start()
    return stores


def _gather_weights(name, shards):
    n = len(shards)

    def body(*refs):
        ins, outs = refs[:n], refs[n:2 * n]
        ici_send, ici_recv, d2d_send, d2d_recv, local_sems = refs[2 * n:2 * n + 5]
        stage = refs[2 * n + 5:]
        x, y, c = _my_place()
        me = 2 * x + y
        chips = _other_chips(x, y)

        def half(i, which):
            h = ins[i].shape[0] // 2
            return pl.ds(which * h, h)

        sends = []
        for i in range(n):
            for j, (px, py) in enumerate(chips):
                sends.append(_remote(ins[i].at[half(i, c), :], outs[i].at[me, half(i, c), :],
                                     ici_send.at[3 * i + j], ici_recv.at[3 * i + j], (px, py, c)))
        for cp in sends:
            cp.start()
        local = _staged_copies(ins, [outs[i].at[me] for i in range(n)], stage, local_sems)
        for i in range(n):
            for j, (px, py) in enumerate(chips):
                landed = outs[i].at[2 * px + py, half(i, c), :]
                _remote(landed, landed, ici_send.at[3 * i + j], ici_recv.at[3 * i + j], (px, py, c)).wait_recv()
                forward = _remote(landed, landed, d2d_send.at[3 * i + j], d2d_recv.at[3 * i + j], (x, y, 1 - c))
                forward.start()
                sends.append(forward)
        for i in range(n):
            for j, (px, py) in enumerate(chips):
                other = outs[i].at[2 * px + py, half(i, 1 - c), :]
                _remote(other, other, d2d_send.at[3 * i + j], d2d_recv.at[3 * i + j], (x, y, 1 - c)).wait_recv()
        for cp in sends:
            cp.wait_send()
        for cp in local:
            cp.wait()

    return pl.pallas_call(
        body, name=name, in_specs=[ANY_SPEC] * n, out_specs=[ANY_SPEC] * n,
        out_shape=[jax.ShapeDtypeStruct((N_CHIPS,) + s.shape, s.dtype) for s in shards],
        scratch_shapes=([pltpu.SemaphoreType.DMA((3 * n,))] * 4 + [pltpu.SemaphoreType.DMA((n,))]
                        + [pltpu.VMEM(s.shape, s.dtype) for s in shards]),
        compiler_params=pltpu.CompilerParams(vmem_limit_bytes=VMEM_LIMIT),
    )(*shards)


def _exchange_halves(name, grads):
    n = len(grads)

    def body(*refs):
        ins, outs = refs[:n], refs[n:2 * n]
        send_sems, recv_sems = refs[2 * n:]
        x, y, c = _my_place()
        copies = []
        for i in range(n):
            h = ins[i].shape[1] // 2
            copies.append(_remote(ins[i].at[:, pl.ds((1 - c) * h, h), :], outs[i],
                                  send_sems.at[i], recv_sems.at[i], (x, y, 1 - c)))
        for cp in copies:
            cp.start()
        for cp in copies:
            cp.wait()

    return pl.pallas_call(
        body, name=name, in_specs=[ANY_SPEC] * n, out_specs=[ANY_SPEC] * n,
        out_shape=[jax.ShapeDtypeStruct((g.shape[0], g.shape[1] // 2, g.shape[2]), g.dtype) for g in grads],
        scratch_shapes=[pltpu.SemaphoreType.DMA((n,)), pltpu.SemaphoreType.DMA((n,))],
    )(*grads)


def _add_own_half(name, g, received, core):
    n_sh, r, cc = g.shape
    h = r // 2
    th = min(h, 256)
    nb = h // th

    def body(core_ref, g_ref, r_ref, o_ref):
        del core_ref
        o_ref[...] = (g_ref[...] + r_ref[...]).astype(BF16)

    grid_spec = pltpu.PrefetchScalarGridSpec(
        num_scalar_prefetch=1, grid=(n_sh, nb),
        in_specs=[pl.BlockSpec((None, th, cc), lambda j, i, core_ref: (j, core_ref[0] * nb + i, 0)),
                  pl.BlockSpec((None, th, cc), lambda j, i, core_ref: (j, i, 0))],
        out_specs=pl.BlockSpec((None, th, cc), lambda j, i, core_ref: (j, i, 0)))
    return pl.pallas_call(
        body, name=name, grid_spec=grid_spec,
        out_shape=jax.ShapeDtypeStruct((n_sh, h, cc), BF16), compiler_params=_params(2),
    )(core, g, received)


HBM_SPEC = pl.BlockSpec(memory_space=pltpu.HBM)
SEM_SPEC = pl.BlockSpec(memory_space=pltpu.SEMAPHORE)
SPLIT_PARAMS = pltpu.CompilerParams(has_side_effects=pltpu.SideEffectType.DATAFLOW_SIDE_EFFECTING)


def _chip_copies(ins, lands, send_sems, recv_sems, sliced):
    x, y, c = _my_place()
    me = 2 * x + y
    pairs = []
    for i in range(len(ins)):
        for j, (px, py) in enumerate(_other_chips(x, y)):
            theirs = 2 * px + py
            src = ins[i].at[theirs] if sliced else ins[i]
            sems = (send_sems.at[3 * i + j], recv_sems.at[3 * i + j], (px, py, c))
            pairs.append((_remote(src, lands[i].at[me], *sems), _remote(src, lands[i].at[theirs], *sems)))
    return pairs


def _exchange_start(name, srcs, lands, sliced, after):
    n = len(srcs)

    def body(*refs):
        ins, land_refs = refs[:n], refs[n:2 * n]
        send_sems, recv_sems = refs[2 * n + 1:2 * n + 3]
        token = refs[-1]
        for send, _ in _chip_copies(ins, land_refs, send_sems, recv_sems, sliced):
            send.start()
        token[...] = jnp.zeros_like(token)

    arrays = list(srcs) + list(lands)
    outs = pl.pallas_call(
        body, name=name,
        in_specs=[HBM_SPEC] * (2 * n) + [ANY_SPEC],
        out_shape=([pltpu.SemaphoreType.DMA((3 * n,))] * 2 + [pltpu.HBM(a.shape, a.dtype) for a in arrays]
                   + [jax.ShapeDtypeStruct((8, 128), F32)]),
        out_specs=[SEM_SPEC] * 2 + [HBM_SPEC] * (2 * n) + [pl.BlockSpec(memory_space=pltpu.VMEM)],
        input_output_aliases={i: 2 + i for i in range(2 * n)},
        compiler_params=SPLIT_PARAMS,
    )(*[pltpu.with_memory_space_constraint(a, pltpu.HBM) for a in arrays], after)
    return outs[:2], outs[2:2 + 2 * n], outs[-1]


def _exchange_wait(name, sems, passed, sliced, after):
    n = len(passed) // 2

    def body(*refs):
        ins, land_refs = refs[:n], refs[n:2 * n]
        send_sems, recv_sems = refs[2 * n:2 * n + 2]
        for send, arrive in _chip_copies(ins, land_refs, send_sems, recv_sems, sliced):
            send.wait_send()
            arrive.wait_recv()

    outs = pl.pallas_call(
        body, name=name,
        in_specs=[HBM_SPEC] * (2 * n) + [SEM_SPEC] * 2 + [ANY_SPEC],
        out_shape=[pltpu.HBM(a.shape, a.dtype) for a in passed],
        out_specs=[HBM_SPEC] * (2 * n),
        input_output_aliases={i: i for i in range(2 * n)},
        compiler_params=SPLIT_PARAMS,
    )(*passed, *sems, after)
    return outs[n:]


def _own_slot(name, own, me):
    r, cc = own.shape[-2:]
    th = min(r, 512)

    def body(me_ref, x_ref, o_ref):
        del me_ref
        o_ref[...] = x_ref[...]

    if own.ndim == 3:
        in_spec = pl.BlockSpec((None, th, cc), lambda i, me_ref: (me_ref[0], i, 0))
    else:
        in_spec = pl.BlockSpec((th, cc), lambda i, me_ref: (i, 0))
    grid_spec = pltpu.PrefetchScalarGridSpec(
        num_scalar_prefetch=1, grid=(r // th,), in_specs=[in_spec],
        out_specs=pl.BlockSpec((None, th, cc), lambda i, me_ref: (me_ref[0], i, 0)))
    return pl.pallas_call(
        body, name=name, grid_spec=grid_spec,
        out_shape=jax.ShapeDtypeStruct((N_CHIPS, r, cc), own.dtype), compiler_params=_params(1),
    )(me, own)


def _sum_chips(name, parts):
    n_sh, h, cc = parts.shape
    th = min(h, 256)

    def body(p_ref, o_ref):
        p = [p_ref[j].astype(F32) for j in range(n_sh)]
        o_ref[...] = ((p[0] + p[1]) + p[2]) + p[3]

    return pl.pallas_call(
        body, name=name, grid=(h // th,),
        in_specs=[_spec((n_sh, th, cc), lambda i: (0, i, 0))],
        out_specs=_spec((th, cc), lambda i: (i, 0)),
        out_shape=jax.ShapeDtypeStruct((h, cc), F32), compiler_params=_params(1),
    )(parts)


def _share_halves(name, halves):
    flat = [t for per_weight in halves for t in per_weight]
    n = len(flat)
    n_w = len(halves)

    def body(*refs):
        ins, outs = refs[:n], refs[n:n + n_w]
        send_sems, recv_sems, local_sems = refs[n + n_w:n + n_w + 3]
        stage = refs[n + n_w + 3:]
        x, y, c = _my_place()
        sends, own = [], []
        for i in range(n):
            w, l = divmod(i, DEPTH)
            h = ins[i].shape[0]
            own.append(outs[w].at[l, pl.ds(c * h, h), :])
            sends.append(_remote(ins[i], own[i], send_sems.at[i], recv_sems.at[i], (x, y, 1 - c)))
        for cp in sends:
            cp.start()
        local = _staged_copies(ins, own, stage, local_sems)
        for i in range(n):
            w, l = divmod(i, DEPTH)
            h = ins[i].shape[0]
            _remote(ins[i], outs[w].at[l, pl.ds((1 - c) * h, h), :], send_sems.at[i], recv_sems.at[i],
                    (x, y, 1 - c)).wait_recv()
        for cp in sends:
            cp.wait_send()
        for cp in local:
            cp.wait()

    return pl.pallas_call(
        body, name=name, in_specs=[ANY_SPEC] * n, out_specs=[ANY_SPEC] * n_w,
        out_shape=[jax.ShapeDtypeStruct((DEPTH, 2 * per_weight[0].shape[0], per_weight[0].shape[1]), F32)
                   for per_weight in halves],
        scratch_shapes=([pltpu.SemaphoreType.DMA((n,))] * 3 + [pltpu.VMEM(t.shape, t.dtype) for t in flat]),
        compiler_params=pltpu.CompilerParams(vmem_limit_bytes=VMEM_LIMIT),
    )(*flat)


def _all_reduce_small(pack):
    def body(p_ref, o_ref, recv, send_sems, recv_sems):
        x, y, c = _my_place()
        me = 4 * x + 2 * y + c
        recv[me] = p_ref[...]
        peers = []
        for k in range(1, N_DEV):
            px, py, pc = (x + (k >> 2)) % 2, (y + ((k >> 1) & 1)) % 2, (c + (k & 1)) % 2
            peers.append((px, py, pc))
        sends = [_remote(p_ref, recv.at[me], send_sems.at[k], recv_sems.at[k], peer)
                 for k, peer in enumerate(peers)]
        for cp in sends:
            cp.start()
        for k, (px, py, pc) in enumerate(peers):
            _remote(p_ref, recv.at[4 * px + 2 * py + pc], send_sems.at[k], recv_sems.at[k],
                    (px, py, pc)).wait_recv()
        for cp in sends:
            cp.wait_send()
        total = recv[0]
        for d in range(1, N_DEV):
            total = total + recv[d]
        o_ref[...] = total

    vmem = pl.BlockSpec(memory_space=pltpu.VMEM)
    return pl.pallas_call(
        body, name="all_reduce_small", in_specs=[vmem], out_specs=vmem,
        out_shape=jax.ShapeDtypeStruct(pack.shape, F32),
        scratch_shapes=[pltpu.VMEM((N_DEV,) + pack.shape, F32),
                        pltpu.SemaphoreType.DMA((N_DEV - 1,)), pltpu.SemaphoreType.DMA((N_DEV - 1,))],
    )(pack)


def _adamw(name, w, g, m, v):
    r, cc = w.shape
    th = min(r, 256)

    def body(w_ref, g_ref, m_ref, v_ref, d_ref, m_out, v_out):
        gv = g_ref[...]
        m2 = ADAM_B1 * m_ref[...] + (1.0 - ADAM_B1) * gv
        v2 = ADAM_B2 * v_ref[...] + (1.0 - ADAM_B2) * (gv * gv)
        m_hat = m2 / (1.0 - ADAM_B1 ** ADAM_STEP)
        v_hat = v2 / (1.0 - ADAM_B2 ** ADAM_STEP)
        d_ref[...] = -ADAM_LR * (m_hat / (jnp.sqrt(v_hat) + ADAM_EPS) + ADAM_WD * w_ref[...])
        m_out[...] = m2
        v_out[...] = v2

    tile = _spec((th, cc), lambda i: (i, 0))
    return pl.pallas_call(
        body, name=name, grid=(r // th,), in_specs=[tile] * 4, out_specs=[tile] * 3,
        out_shape=[jax.ShapeDtypeStruct((r, cc), F32)] * 3, compiler_params=_params(1),
    )(w, g, m, v)


def _lower_bounds(lb_logits):
    p = jax.nn.softmax(lb_logits.astype(F32), axis=0)
    return jnp.cumsum(p, axis=0) - p[0]


def _row_tile_specs(tm, width):
    return _spec((tm, width), lambda i, j, k: (i, 0))


def _layer_forward(l, x_in, small, weights, consts, after=None):
    win, rest = weights
    cos_t, sin_t, stack, _, _ = consts
    tm = MM_TILE
    saved = {"x_in": x_in}

    h, h_t = _rms_fwd(f"norm_mix{l}", x_in, small["norm_mix"][l][None, :], after=after)
    proj = _mm_pieces(f"proj{l}", h, win, False, tm)
    saved.update(h_t=h_t, proj=proj)

    qkv = _attn_prep(f"attn_prep{l}", proj, cos_t, sin_t)
    outs, lses = [], []
    for p, d in enumerate(DILATIONS):
        o, lse = _attn_fwd(f"attn_fwd{l}_{d}", *qkv[p], SEQ // d // SPAN)
        outs.append(o)
        lses.append(lse)
    mixed, mixed_t, attn, lse = _attn_merge(f"attn_merge{l}", outs, lses, small["attn_out_gain"][l][None, :])
    saved.update(qkv=qkv, attn=attn, lse=lse)

    lb3 = small["lower"][l].reshape(HGRN_HEADS, 1, HGRN_DIM)
    mixed, mixed_t, o_pre, states, scores = _hgrn_fwd(f"hgrn_fwd{l}", proj, lb3, small["hgrn_out_gain"][l][None, :],
                                                      stack, mixed, mixed_t)
    wo, wu, wd = rest(mixed)
    saved.update(mixed_t=mixed_t, o_pre=o_pre, states=states, scores=scores, lb3=lb3, weights=(win, wo, wu, wd))

    x_mid = _mm_accum(f"out_proj{l}", mixed, wo, False, tm, x_in)
    saved["x_mid"] = x_mid

    h2, h2_t = _rms_fwd(f"norm_mlp{l}", x_mid, small["norm_mlp"][l][None, :])
    a, relu_u, a_t = _mm_pieces(f"up{l}", h2, wu, False, tm, epilogue="relu2")
    x_out = _mm_accum(f"down{l}", a, wd, False, tm, x_mid)
    saved.update(h2_t=h2_t, relu_u=relu_u, a_t=a_t)
    return x_out, saved


def _layer_backward(l, dx, saved, small, consts, on_grads, after=None):
    win, wo, wu, wd = saved["weights"]
    cos_t, sin_t, stack, stack_t, head_sum = consts
    tm = MM_TILE

    dx, dx_b = dx
    du = _mm_pieces(f"d_u{l}", dx_b, wd, True, tm, epilogue="relu2_grad", extra=saved["relu_u"], after=after)
    d_wd = _mm_dw(f"d_wdown{l}", saved["a_t"], dx_b, False, tm)
    dxm, dxm_b, dg_mlp = _mm_accum(f"d_h2_{l}", du, wu, True, tm, dx,
                                   norm=(saved["x_mid"], small["norm_mlp"][l][None, :]))
    d_wu = _mm_dw(f"d_wup{l}", saved["h2_t"], du, True, tm)
    after_mlp = on_grads(l, "mlp", (d_wu, d_wd))

    d_mixed = _mm_pieces(f"d_mixed{l}", dxm_b, wo, True, tm, after=after_mlp)
    d_wo = _mm_dw(f"d_wout{l}", saved["mixed_t"], dxm_b, False, tm)
    d_rec = d_mixed

    d_out, delta, lses, dg_attn = _attn_bwd_prep(f"attn_bwd_prep{l}", d_mixed, saved["attn"], saved["lse"],
                                                 small["attn_out_gain"][l][None, :], head_sum)
    grads = []
    for p, d in enumerate(DILATIONS):
        grads.append(_attn_bwd(f"attn_bwd{l}_{d}", *saved["qkv"][p], d_out[p], delta[p], lses[p],
                               SEQ // d // SPAN))
    dp_attn = _attn_bwd_post(f"attn_bwd_post{l}", grads, cos_t, sin_t)

    dq_h, df_h, di_h, dg_h, d_lower, dg_hgrn = _hgrn_bwd(
        f"hgrn_bwd{l}", saved["proj"], d_rec, saved["o_pre"], saved["states"], saved["scores"],
        saved["lb3"], small["hgrn_out_gain"][l][None, :], stack, stack_t)
    dproj = jnp.concatenate([dp_attn, dq_h, df_h, di_h, dg_h], axis=1)

    dx_in, dx_in_b, dg_mix = _mm_accum(f"d_h{l}", dproj, win, True, tm, dxm,
                                       norm=(saved["x_in"], small["norm_mix"][l][None, :]))
    d_win = _mm_dw(f"d_win{l}", saved["h_t"], dproj, True, tm)

    small_grads = {"norm_mix": dg_mix[0], "attn_out_gain": dg_attn[0],
                   "lower": d_lower.reshape(HGRN_WIDTH),
                   "hgrn_out_gain": jnp.sum(dg_hgrn, axis=0).reshape(HGRN_DIM), "norm_mlp": dg_mlp[0]}
    return (dx_in, dx_in_b), on_grads(l, "mix", (d_win, d_wo)), small_grads


def _local_step(xs, target, small, get_weights, on_grads):
    consts = _rope_tables() + _hgrn_consts() + (_head_sum_matrix(),)
    stream = xs
    saved = []
    for l in range(DEPTH):
        w, after = get_weights(l, stream)
        stream, s = _layer_forward(l, stream, small, w, consts, after=after)
        saved.append(s)
    dx_f, dx_b, dg_final, loss = _loss_head(stream, small["norm_final"][None, :], target)
    dx = (dx_f, dx_b)
    small_grads = [None] * DEPTH
    after = None
    for l in reversed(range(DEPTH)):
        dx, after, small_grads[l] = _layer_backward(l, dx, saved[l], small, consts, on_grads, after=after)
    return loss, dx[0], dg_final[0], small_grads, after


def _pack_small(norm_mix, attn_out_gain, lb, hgrn_out_gain, norm_mlp, norm_final, last_row):
    rows = [norm_mix, attn_out_gain.reshape(1, D_MODEL), lb.reshape(1, D_MODEL),
            jnp.pad(hgrn_out_gain.reshape(1, DEPTH * HGRN_DIM), ((0, 0), (0, D_MODEL - DEPTH * HGRN_DIM))),
            norm_mlp, norm_final.reshape(1, D_MODEL), last_row.reshape(1, D_MODEL)]
    pack = jnp.concatenate(rows, axis=0)
    return jnp.pad(pack, ((0, PACK_ROWS - pack.shape[0]), (0, 0)))


def _unpack_small(pack):
    return (pack[0:2], pack[2].reshape(DEPTH, ATTN_WIDTH), pack[3].reshape(DEPTH, HGRN_WIDTH),
            pack[4, :DEPTH * HGRN_DIM].reshape(DEPTH, HGRN_DIM), pack[5:7], pack[7], pack[8])


def kernel(x, norm_mix, w_in, attn_out_gain, hgrn_lb_logits, hgrn_out_gain, w_out, norm_mlp, w_up, w_down, norm_final, loss_target, m_norm_mix, m_w_in, m_attn_out_gain, m_hgrn_lb_logits, m_hgrn_out_gain, m_w_out, m_norm_mlp, m_w_up, m_w_down, m_norm_final, v_norm_mix, v_w_in, v_attn_out_gain, v_hgrn_lb_logits, v_hgrn_out_gain, v_w_out, v_norm_mlp, v_w_up, v_w_down, v_norm_final):
    core = lax.axis_index("c").astype(jnp.int32).reshape(1)
    lower, lower_vjp = jax.vjp(_lower_bounds, hgrn_lb_logits)
    small = {"norm_mix": norm_mix, "attn_out_gain": attn_out_gain, "lower": lower,
             "hgrn_out_gain": hgrn_out_gain, "norm_mlp": norm_mlp, "norm_final": norm_final}
    big_w = (w_in, w_out, w_up, w_down)

    me = (2 * lax.axis_index("x") + lax.axis_index("y")).astype(jnp.int32).reshape(1)
    shards =[[w[l].astype(BF16) for w in big_w] for l in range(DEPTH)]
    in_flight = {}

    def start_gather(name, some, after):
        lands = [_own_slot(f"own_{name}_{i}", s, me) for i, s in enumerate(some)]
        sems, passed, token = _exchange_start(name, some, lands, False, after)
        return (sems, passed), token

    def get_weights(l, stream):
        if l == 0:
            (win,) = _gather_weights("gather_w_in0", shards[0][:1])
            in_flight["rest0"], token = start_gather("gather_start0", shards[0][1:], win)
            in_flight["weights1"], token = start_gather("gather_start1", shards[1], token)
            return (win, lambda after: _exchange_wait("gather_wait0", *in_flight.pop("rest0"), False, after)), token
        weights = _exchange_wait("gather_wait1", *in_flight.pop("weights1"), False, stream)
        return (weights[0], lambda after: weights[1:]), None

    reduced = {}

    def start_exchange(name, grads):
        received = _exchange_halves(f"halves_{name}", grads)
        halves = [_add_own_half(f"add_{name}_{i}", g, r, core) for i, (g, r) in enumerate(zip(grads, received))]
        lands = [_own_slot(f"own_{name}_{i}", h, me) for i, h in enumerate(halves)]
        sems, passed, token = _exchange_start(f"start_{name}", halves, lands, True, halves[0])
        in_flight[name] = (sems, passed)
        return token

    def finish_exchange(name, after):
        landed = _exchange_wait(f"wait_{name}", *in_flight.pop(name), True, after)
        return [_sum_chips(f"sum_{name}_{i}", p) for i, p in enumerate(landed)]

    def on_grads(l, group, grads):
        if (l, group) == (1, "mlp"):
            return start_exchange("mlp1", grads)
        if (l, group) == (1, "mix"):
            return start_exchange("mix1", grads)
        if (l, group) == (0, "mlp"):
            reduced[(1, "mlp")] = finish_exchange("mlp1", grads[0])
            reduced[(1, "mix")] = finish_exchange("mix1", grads[0])
            return start_exchange("mlp0", grads)
        token = start_exchange("mix0", grads)
        reduced[(0, "mlp")] = finish_exchange("mlp0", token)
        return token

    loss, dx, dg_final, sg, last_started = _local_step(x[0], loss_target[0], small, get_weights, on_grads)

    big_m = (m_w_in, m_w_out, m_w_up, m_w_down)
    big_v = (v_w_in, v_w_out, v_w_up, v_w_down)
    names = ("w_in", "w_out", "w_up", "w_down")
    big_g, big_delta, big_new_m, big_new_v = [None] * 4, [None] * 4, [None] * 4, [None] * 4

    def finish_weights(group, which):
        whole = _share_halves(f"share_{group}", [[reduced[(l, group)][i] for l in range(DEPTH)] for i in range(2)])
        for i, w in enumerate(which):
            shape = big_w[w].shape
            flat = lambda arr: arr.reshape(shape[0] * shape[1], shape[2])
            d, m2, v2 = _adamw(f"adamw_{names[w]}", flat(big_w[w]), flat(whole[i]), flat(big_m[w]), flat(big_v[w]))
            big_g[w], big_delta[w] = whole[i], d.reshape(shape)
            big_new_m[w], big_new_v[w] = m2.reshape(shape), v2.reshape(shape)

    finish_weights("mlp", (2, 3))

    stack2 = lambda key: jnp.stack([sg[l][key] for l in range(DEPTH)])
    pack = _pack_small(stack2("norm_mix"), stack2("attn_out_gain"), stack2("lower"), stack2("hgrn_out_gain"),
                       stack2("norm_mlp"), dg_final, jnp.broadcast_to(loss[0, 0] + last_started[0, 0], (D_MODEL,)))
    g_mix, g_attn, g_lower, g_hgrn, g_mlp, g_final, loss_row = _unpack_small(_all_reduce_small(pack))
    (g_logits,) = lower_vjp(g_lower)

    zeros_row = jnp.zeros((D_MODEL,), F32)
    small_w = (norm_mix, attn_out_gain, hgrn_lb_logits, hgrn_out_gain, norm_mlp, norm_final)
    small_m = (m_norm_mix, m_attn_out_gain, m_hgrn_lb_logits, m_hgrn_out_gain, m_norm_mlp, m_norm_final)
    small_v = (v_norm_mix, v_attn_out_gain, v_hgrn_lb_logits, v_hgrn_out_gain, v_norm_mlp, v_norm_final)
    small_g = (g_mix, g_attn, g_logits, g_hgrn, g_mlp, g_final)
    packs = [_pack_small(*t, zeros_row) for t in (small_w, small_g, small_m, small_v)]
    small_delta, small_new_m, small_new_v = [_unpack_small(p)[:6] for p in _adamw("adamw_small", *packs)]

    reduced[(0, "mix")] = finish_exchange("mix0", small_delta[0])
    finish_weights("mix", (0, 1))

    def ordered(small6, big4):
        mix, attn, lbl, hg, mlp, fin = small6
        return (mix, big4[0], attn, lbl, hg, big4[1], mlp, big4[2], big4[3], fin)

    return ((loss_row[0], dx[None]) + ordered(small_g, big_g) + ordered(small_delta, big_delta)
            + ordered(small_new_m, big_new_m) + ordered(small_new_v, big_new_v))
```

```python
import functools
import math

import numpy as np
import jax
import jax.numpy as jnp
from jax import lax
from jax.experimental import pallas as pl
from jax.experimental.pallas import tpu as pltpu

F32 = jnp.float32
BF16 = jnp.bfloat16
MESH = pl.DeviceIdType.MESH

SEQ = 4096
D_MODEL = 1024
DEPTH = 2
ATTN_WIDTH = 512
HEAD_DIM = 64
HGRN_HEADS = 4
HGRN_DIM = 128
HGRN_WIDTH = 512
IN_W = 3584
MLP_HIDDEN = 4096
N_CHIPS = 4
N_DEV = 8
SHARD_IN = IN_W // N_CHIPS
SHARD_OUT = D_MODEL // N_CHIPS
SHARD_MLP = MLP_HIDDEN // N_CHIPS
DILATIONS = (1, 4, 16)
SPAN = 128
ROPE_THETA = 10000.0
NORM_EPS = 1e-6
MASK_VALUE = -1e30
CHUNK = 128
ROW_TILE = 512
MM_TILE = 512
VMEM_LIMIT = 52 * 1024 * 1024

ADAM_LR = 0.001
ADAM_B1 = 0.9
ADAM_B2 = 0.999
ADAM_EPS = 1e-08
ADAM_WD = 0.01
ADAM_STEP = 10

PACK_ROWS = 16


def _params(n_axes):
    return pltpu.CompilerParams(dimension_semantics=("arbitrary",) * n_axes,
                                vmem_limit_bytes=VMEM_LIMIT)


def _dot(a, b):
    return jnp.dot(a.astype(BF16), b.astype(BF16), preferred_element_type=F32)


def _dot_nt(a, b):
    return lax.dot_general(a.astype(BF16), b.astype(BF16), (((1,), (1,)), ((), ())),
                           preferred_element_type=F32)


def _dot_tn(a, b):
    return lax.dot_general(a.astype(BF16), b.astype(BF16), (((0,), (0,)), ((), ())),
                           preferred_element_type=F32)


def _sigmoid(x):
    return 1.0 / (1.0 + jnp.exp(-x))


def _spec(shape, index_map):
    return pl.BlockSpec(shape, index_map)


def _mm_pieces(name, a, w, nt, tm, epilogue="none", extra=None, after=None):
    s = a.shape[0]
    pw = w.shape[1] if nt else w.shape[2]
    width = N_CHIPS * pw

    def body(a_ref, w_ref, *rest):
        e_ref = rest[0] if extra is not None else None
        outs = rest[-3:] if epilogue == "relu2" else rest[-1:]
        av = a_ref[...].astype(BF16)
        for j in range(N_CHIPS):
            cols = slice(j * pw, (j + 1) * pw)
            r = _dot_nt(av, w_ref[j]) if nt else _dot(av, w_ref[j])
            if epilogue == "relu2":
                relu = jnp.maximum(r, 0.0)
                r = relu * relu
                outs[1][:, cols] = relu.astype(BF16)
                outs[2][cols, :] = r.T.astype(BF16)
            elif epilogue == "relu2_grad":
                r = r * (2.0 * e_ref[:, cols].astype(F32))
            outs[0][:, cols] = r.astype(outs[0].dtype)

    row = lambda width_: _spec((tm, width_), lambda i: (i, 0))
    in_specs = [row(a.shape[1]), _spec(w.shape, lambda i: (0, 0, 0))]
    args = [a, w]
    if extra is not None:
        in_specs.append(row(width))
        args.append(extra)
    if after is not None:
        in_specs.append(pl.BlockSpec(memory_space=pl.ANY))
        args.append(after)
    if epilogue == "relu2":
        out_specs = [row(width), row(width), _spec((width, tm), lambda i: (0, i))]
        out_shape = [jax.ShapeDtypeStruct((s, width), BF16)] * 2 + [jax.ShapeDtypeStruct((width, s), BF16)]
    else:
        out_specs = row(width)
        out_shape = jax.ShapeDtypeStruct((s, width), BF16 if epilogue == "relu2_grad" else F32)
    return pl.pallas_call(body, name=name, grid=(s // tm,), in_specs=in_specs, out_specs=out_specs,
                          out_shape=out_shape, compiler_params=_params(1))(*args)


def _mm_accum(name, a, w, nt, tm, resid, norm=None):
    s = a.shape[0]
    pk = w.shape[2] if nt else w.shape[1]
    d = w.shape[1] if nt else w.shape[2]

    def body(a_ref, w_ref, resid_ref, *rest):
        r = None
        for j in range(N_CHIPS):
            piece = a_ref[:, j * pk:(j + 1) * pk].astype(BF16)
            term = _dot_nt(piece, w_ref[j]) if nt else _dot(piece, w_ref[j])
            r = term if r is None else r + term
        if norm is None:
            rest[0][...] = r + resid_ref[...]
            return
        x_ref, g_ref, dx_ref, dxb_ref, dg_ref = rest

        @pl.when(pl.program_id(0) == 0)
        def _():
            dg_ref[...] = jnp.zeros_like(dg_ref)

        xv = x_ref[...]
        rs = lax.rsqrt(jnp.mean(xv * xv, axis=1, keepdims=True) + NORM_EPS)
        xhat = xv * rs
        rg = r * g_ref[...]
        dx = resid_ref[...] + rs * (rg - xhat * jnp.mean(rg * xhat, axis=1, keepdims=True))
        dx_ref[...] = dx
        dxb_ref[...] = dx.astype(BF16)
        dg_ref[...] += jnp.sum(r * xhat, axis=0, keepdims=True)

    row = lambda width: _spec((tm, width), lambda i: (i, 0))
    in_specs = [row(a.shape[1]), _spec(w.shape, lambda i: (0, 0, 0)), row(d)]
    args = [a, w, resid]
    if norm is None:
        out_specs, out_shape = row(d), jax.ShapeDtypeStruct((s, d), F32)
    else:
        in_specs += [row(d), _spec((1, d), lambda i: (0, 0))]
        args += list(norm)
        out_specs = [row(d), row(d), _spec((1, d), lambda i: (0, 0))]
        out_shape = [jax.ShapeDtypeStruct((s, d), F32), jax.ShapeDtypeStruct((s, d), BF16),
                     jax.ShapeDtypeStruct((1, d), F32)]
    return pl.pallas_call(body, name=name, grid=(s // tm,), in_specs=in_specs, out_specs=out_specs,
                          out_shape=out_shape, compiler_params=_params(1))(*args)


def _mm_dw(name, a_t, b, by_cols, tk):
    m, s = a_t.shape
    n = b.shape[1]
    shape = (N_CHIPS, m, n // N_CHIPS) if by_cols else (N_CHIPS, m // N_CHIPS, n)

    def body(a_ref, b_ref, o_ref):
        @pl.when(pl.program_id(0) == 0)
        def _():
            o_ref[...] = jnp.zeros_like(o_ref)

        for j in range(N_CHIPS):
            if by_cols:
                o_ref[j] += _dot(a_ref[...], b_ref[:, j * shape[2]:(j + 1) * shape[2]])
            else:
                o_ref[j] += _dot(a_ref[j * shape[1]:(j + 1) * shape[1], :], b_ref[...])

    return pl.pallas_call(
        body, name=name, grid=(s // tk,),
        in_specs=[_spec((m, tk), lambda k: (0, k)), _spec((tk, n), lambda k: (k, 0))],
        out_specs=_spec(shape, lambda k: (0, 0, 0)), out_shape=jax.ShapeDtypeStruct(shape, F32),
        compiler_params=_params(1))(a_t, b)


def _rms_fwd(name, x, gain, after=None):
    s, d = x.shape
    t = ROW_TILE

    def body(x_ref, g_ref, *rest):
        h_ref, ht_ref = rest[-2:]
        xv = x_ref[...]
        r = lax.rsqrt(jnp.mean(xv * xv, axis=1, keepdims=True) + NORM_EPS)
        h = xv * r * g_ref[...]
        h_ref[...] = h.astype(BF16)
        ht_ref[...] = h.T.astype(BF16)

    in_specs = [_spec((t, d), lambda i: (i, 0)), _spec((1, d), lambda i: (0, 0))]
    args = (x, gain)
    if after is not None:
        in_specs.append(pl.BlockSpec(memory_space=pl.ANY))
        args += (after,)
    return pl.pallas_call(
        body, name=name, grid=(s // t,), in_specs=in_specs,
        out_specs=[_spec((t, d), lambda i: (i, 0)), _spec((d, t), lambda i: (0, i))],
        out_shape=[jax.ShapeDtypeStruct((s, d), BF16), jax.ShapeDtypeStruct((d, s), BF16)],
        compiler_params=_params(1),
    )(*args)


def _loss_head(x, gain, target):
    s, d = x.shape
    t = ROW_TILE
    n_steps = s // t

    def body(x_ref, g_ref, t_ref, dx_ref, dxb_ref, dg_ref, loss_ref, acc):
        i = pl.program_id(0)

        @pl.when(i == 0)
        def _():
            dg_ref[...] = jnp.zeros_like(dg_ref)
            acc[...] = jnp.zeros_like(acc)

        xv = x_ref[...]
        g = g_ref[...]
        r = lax.rsqrt(jnp.mean(xv * xv, axis=1, keepdims=True) + NORM_EPS)
        xhat = xv * r
        err = xhat * g - t_ref[...]
        acc[...] += jnp.sum(err * err, axis=0, keepdims=True)
        dy = err * (1.0 / d)
        dyg = dy * g
        proj = jnp.mean(dyg * xhat, axis=1, keepdims=True)
        dx = r * (dyg - xhat * proj)
        dx_ref[...] = dx
        dxb_ref[...] = dx.astype(BF16)
        dg_ref[...] += jnp.sum(dy * xhat, axis=0, keepdims=True)

        @pl.when(i == n_steps - 1)
        def _():
            total = jnp.sum(acc[...], axis=1, keepdims=True) * (0.5 / d)
            loss_ref[...] = jnp.broadcast_to(total, loss_ref.shape)

    return pl.pallas_call(
        body, name="loss_head", grid=(n_steps,),
        in_specs=[_spec((t, d), lambda i: (i, 0)), _spec((1, d), lambda i: (0, 0)),
                  _spec((t, d), lambda i: (i, 0))],
        out_specs=[_spec((t, d), lambda i: (i, 0)), _spec((t, d), lambda i: (i, 0)),
                   _spec((1, d), lambda i: (0, 0)), _spec((1, 128), lambda i: (0, 0))],
        out_shape=[jax.ShapeDtypeStruct((s, d), F32), jax.ShapeDtypeStruct((s, d), BF16),
                   jax.ShapeDtypeStruct((1, d), F32), jax.ShapeDtypeStruct((1, 128), F32)],
        scratch_shapes=[pltpu.VMEM((1, d), F32)],
        compiler_params=_params(1),
    )(x, gain, target)


def _rope_tables():
    half = HEAD_DIM // 2
    inv_freq = ROPE_THETA ** (-jnp.arange(half, dtype=F32) / half)
    ang = jnp.arange(SEQ, dtype=jnp.int32).astype(F32)[:, None] * inv_freq[None, :]
    cos, sin = jnp.cos(ang), jnp.sin(ang)
    cos_t = jnp.concatenate([cos, cos, cos, cos], axis=1)
    sin_t = jnp.concatenate([-sin, sin, -sin, sin], axis=1)
    return cos_t, sin_t


def _swap_halves(x):
    lane = lax.broadcasted_iota(jnp.int32, x.shape, 1)
    first = (lane % HEAD_DIM) < (HEAD_DIM // 2)
    return jnp.where(first, pltpu.roll(x, 128 - HEAD_DIM // 2, 1), pltpu.roll(x, HEAD_DIM // 2, 1))


def _permuted_specs(t, width):
    specs = [_spec((t, width), lambda i: (i, 0))]
    for d in DILATIONS[1:]:
        specs.append(_spec((d, t // d, width), lambda i: (0, i, 0)))
    return specs


def _permuted_shapes(width, dtype):
    shapes = [jax.ShapeDtypeStruct((SEQ, width), dtype)]
    for d in DILATIONS[1:]:
        shapes.append(jax.ShapeDtypeStruct((d, SEQ // d, width), dtype))
    return shapes


def _attn_prep(name, proj, cos_t, sin_t):
    t = ROW_TILE
    w = ATTN_WIDTH

    def body(q_ref, k_ref, v_ref, cos_ref, sin_ref, *rest):
        outs, scr = rest[:9], rest[9]
        cosv, sinv = cos_ref[...], sin_ref[...]
        for a, (src, roped, scale) in enumerate(((q_ref, True, HEAD_DIM ** -0.5),
                                                 (k_ref, True, 1.0), (v_ref, False, 1.0))):
            o1, o4, o16 = outs[3 * a:3 * a + 3]
            for cb in range(w // 128):
                cols = slice(cb * 128, (cb + 1) * 128)
                val = src[:, cols]
                if roped:
                    val = (val * cosv + _swap_halves(val) * sinv) * scale
                scr[...] = val
                o1[:, cols] = val.astype(BF16)
                for o_ref, d in ((o4, 4), (o16, 16)):
                    for r in range(d):
                        o_ref[r, :, cols] = scr[pl.ds(r, t // d, stride=d), :].astype(BF16)

    out_specs = _permuted_specs(t, w) * 3
    out_shape = _permuted_shapes(w, BF16) * 3
    outs = pl.pallas_call(
        body, name=name, grid=(SEQ // t,),
        in_specs=[_spec((t, w), lambda i: (i, 0)), _spec((t, w), lambda i: (i, 1)),
                  _spec((t, w), lambda i: (i, 2)),
                  _spec((t, 128), lambda i: (i, 0)), _spec((t, 128), lambda i: (i, 0))],
        out_specs=out_specs, out_shape=out_shape,
        scratch_shapes=[pltpu.VMEM((t, 128), F32)],
        compiler_params=_params(1),
    )(proj, proj, proj, cos_t, sin_t)
    q, k, v = outs[0:3], outs[3:6], outs[6:9]
    flat = lambda arr: arr.reshape(SEQ, w)
    return [(flat(q[p]), flat(k[p]), flat(v[p])) for p in range(3)]


def _band_masks():
    row = lax.broadcasted_iota(jnp.int32, (2 * SPAN, 2 * SPAN), 0) % SPAN
    col = lax.broadcasted_iota(jnp.int32, (2 * SPAN, 2 * SPAN), 1)
    is_prev = col < SPAN
    band = (is_prev & (col >= row)) | (~is_prev & (col - SPAN <= row))
    head0 = lax.broadcasted_iota(jnp.int32, (SPAN, 128), 1) < HEAD_DIM
    return band, is_prev, head0


def _stack_heads(x, head0):
    zero = jnp.zeros_like(x)
    return jnp.concatenate([jnp.where(head0, x, zero), jnp.where(head0, zero, x)], axis=0)


def _attn_fwd(name, q, k, v, seg_blocks):
    n_blocks = SEQ // SPAN

    def body(q_ref, k_ref, v_ref, o_ref, lse_ref):
        band, is_prev, head0 = _band_masks()

        def step(b, carry):
            cur = pl.ds(pl.multiple_of(b * SPAN, SPAN), SPAN)
            prev = pl.ds(pl.multiple_of(jnp.maximum(b - 1, 0) * SPAN, SPAN), SPAN)
            qs = _stack_heads(q_ref[cur, :], head0)
            kcat = jnp.concatenate([k_ref[prev, :], k_ref[cur, :]], axis=0)
            vcat = jnp.concatenate([v_ref[prev, :], v_ref[cur, :]], axis=0)
            ok = band & (((b % seg_blocks) != 0) | ~is_prev)
            s = jnp.where(ok, _dot_nt(qs, kcat), MASK_VALUE)
            m = jnp.max(s, axis=1, keepdims=True)
            p = jnp.exp(s - m)
            l = jnp.sum(p, axis=1, keepdims=True)
            pv = _dot(p, vcat) * (1.0 / l)
            lse = m + jnp.log(l)
            o_ref[cur, :] = jnp.where(head0, pv[:SPAN], pv[SPAN:])
            lse_ref[cur, :] = jnp.where(head0, lse[:SPAN], lse[SPAN:])
            return carry

        lax.fori_loop(0, n_blocks, step, 0, unroll=4)

    col = _spec((SEQ, 128), lambda j: (0, j))
    return pl.pallas_call(
        body, name=name, grid=(ATTN_WIDTH // 128,),
        in_specs=[col, col, col], out_specs=[col, col],
        out_shape=[jax.ShapeDtypeStruct((SEQ, ATTN_WIDTH), F32)] * 2,
        compiler_params=_params(1),
    )(q, k, v)


def _unpermute(dst, src_ref, d, cols):
    n = dst.shape[0] // d
    for r in range(d):
        dst[pl.ds(r, n, stride=d), :] = src_ref[r, :, cols]


def _attn_merge(name, outs, lses, gain):
    t = ROW_TILE
    w = ATTN_WIDTH

    def body(o1, o4, o16, l1, l4, l16, g_ref, an_ref, ant_ref, attn_ref, lse_ref, so4, so16, sl4, sl16):
        for cb in range(w // 128):
            cols = slice(cb * 128, (cb + 1) * 128)
            _unpermute(so4, o4, 4, cols)
            _unpermute(so16, o16, 16, cols)
            _unpermute(sl4, l4, 4, cols)
            _unpermute(sl16, l16, 16, cols)
            la, lb, lc = l1[:, cols], sl4[...], sl16[...]
            m = jnp.maximum(jnp.maximum(la, lb), lc)
            ea, eb, ec = jnp.exp(la - m), jnp.exp(lb - m), jnp.exp(lc - m)
            tot = ea + eb + ec
            attn_ref[:, cols] = (ea * o1[:, cols] + eb * so4[...] + ec * so16[...]) / tot
            lse_ref[:, cols] = m + jnp.log(tot)
        attn = attn_ref[...]
        r = lax.rsqrt(jnp.mean(attn * attn, axis=1, keepdims=True) + NORM_EPS)
        an = attn * r * g_ref[...]
        an_ref[...] = an.astype(BF16)
        ant_ref[...] = an.T.astype(BF16)

    views = lambda arrs: [arrs[0], arrs[1].reshape(4, SEQ // 4, w), arrs[2].reshape(16, SEQ // 16, w)]
    row = _spec((t, w), lambda i: (i, 0))
    return pl.pallas_call(
        body, name=name, grid=(SEQ // t,),
        in_specs=_permuted_specs(t, w) * 2 + [_spec((1, w), lambda i: (0, 0))],
        out_specs=[row, _spec((w, t), lambda i: (0, i)), row, row],
        out_shape=[jax.ShapeDtypeStruct((SEQ, 2 * w), BF16), jax.ShapeDtypeStruct((2 * w, SEQ), BF16),
                   jax.ShapeDtypeStruct((SEQ, w), F32), jax.ShapeDtypeStruct((SEQ, w), F32)],
        scratch_shapes=[pltpu.VMEM((t, 128), F32)] * 4,
        compiler_params=_params(1),
    )(*views(outs), *views(lses), gain)


def _head_sum_matrix():
    i = np.arange(ATTN_WIDTH)
    return jnp.asarray((i[:, None] // HEAD_DIM) == (i[None, :] // HEAD_DIM), dtype=F32)


def _attn_bwd_prep(name, d_an, attn, lse, gain, head_sum):
    t = ROW_TILE
    w = ATTN_WIDTH

    def body(dan_ref, attn_ref, lse_ref, g_ref, hs_ref, *rest):
        (do1, do4, do16, dl1, dl4, dl16, ls4, ls16, dg_ref), (sdo, sdl, sls) = rest[:9], rest[9:]

        @pl.when(pl.program_id(0) == 0)
        def _():
            dg_ref[...] = jnp.zeros_like(dg_ref)

        attn = attn_ref[...]
        dan = dan_ref[...]
        r = lax.rsqrt(jnp.mean(attn * attn, axis=1, keepdims=True) + NORM_EPS)
        xhat = attn * r
        dg_ref[...] += jnp.sum(dan * xhat, axis=0, keepdims=True)
        dang = dan * g_ref[...]
        d_o = r * (dang - xhat * jnp.mean(dang * xhat, axis=1, keepdims=True))
        delta = jnp.dot(d_o * attn, hs_ref[...], preferred_element_type=F32,
                        precision=lax.Precision.HIGHEST)
        do1[...] = d_o.astype(BF16)
        dl1[...] = delta
        for cb in range(w // 128):
            cols = slice(cb * 128, (cb + 1) * 128)
            sdo[...] = d_o[:, cols]
            sdl[...] = delta[:, cols]
            sls[...] = lse_ref[:, cols]
            for d, o_do, o_dl, o_ls in ((4, do4, dl4, ls4), (16, do16, dl16, ls16)):
                for rr in range(d):
                    rows = pl.ds(rr, t // d, stride=d)
                    o_do[rr, :, cols] = sdo[rows, :].astype(BF16)
                    o_dl[rr, :, cols] = sdl[rows, :]
                    o_ls[rr, :, cols] = sls[rows, :]

    row = _spec((t, w), lambda i: (i, 0))
    perm = _permuted_specs(t, w)
    outs = pl.pallas_call(
        body, name=name, grid=(SEQ // t,),
        in_specs=[row, row, row, _spec((1, w), lambda i: (0, 0)), _spec((w, w), lambda i: (0, 0))],
        out_specs=perm + perm + perm[1:] + [_spec((1, w), lambda i: (0, 0))],
        out_shape=(_permuted_shapes(w, BF16) + _permuted_shapes(w, F32) + _permuted_shapes(w, F32)[1:]
                   + [jax.ShapeDtypeStruct((1, w), F32)]),
        scratch_shapes=[pltpu.VMEM((t, 128), F32)] * 3,
        compiler_params=_params(1),
    )(d_an, attn, lse, gain, head_sum)
    flat = lambda arr: arr.reshape(SEQ, w)
    d_out = [flat(a) for a in outs[0:3]]
    delta = [flat(a) for a in outs[3:6]]
    lses = [lse, flat(outs[6]), flat(outs[7])]
    return d_out, delta, lses, outs[8]


def _attn_bwd(name, q, k, v, d_out, delta, lse, seg_blocks):
    n_blocks = SEQ // SPAN

    def body(q_ref, k_ref, v_ref, do_ref, dl_ref, lse_ref, dq_ref, dk_ref, dv_ref):
        band, is_prev, head0 = _band_masks()
        dk_ref[...] = jnp.zeros_like(dk_ref)
        dv_ref[...] = jnp.zeros_like(dv_ref)

        def per_head(x):
            return jnp.concatenate([x[:, 0:1], x[:, HEAD_DIM:HEAD_DIM + 1]], axis=0)

        def step(b, carry):
            cur = pl.ds(pl.multiple_of(b * SPAN, SPAN), SPAN)
            prev = pl.ds(pl.multiple_of(jnp.maximum(b - 1, 0) * SPAN, SPAN), SPAN)
            qs = _stack_heads(q_ref[cur, :], head0)
            dos = _stack_heads(do_ref[cur, :], head0)
            kcat = jnp.concatenate([k_ref[prev, :], k_ref[cur, :]], axis=0)
            vcat = jnp.concatenate([v_ref[prev, :], v_ref[cur, :]], axis=0)
            ok = band & (((b % seg_blocks) != 0) | ~is_prev)
            p = jnp.where(ok, jnp.exp(_dot_nt(qs, kcat) - per_head(lse_ref[cur, :])), 0.0)
            ds = p * (_dot_nt(dos, vcat) - per_head(dl_ref[cur, :]))
            dq = _dot(ds, kcat)
            dq_ref[cur, :] = jnp.where(head0, dq[:SPAN], dq[SPAN:])
            dk = _dot_tn(ds, qs)
            dv = _dot_tn(p, dos)
            dk_ref[prev, :] += dk[:SPAN]
            dv_ref[prev, :] += dv[:SPAN]
            dk_ref[cur, :] += dk[SPAN:]
            dv_ref[cur, :] += dv[SPAN:]
            return carry

        lax.fori_loop(0, n_blocks, step, 0, unroll=4)

    col = _spec((SEQ, 128), lambda j: (0, j))
    return pl.pallas_call(
        body, name=name, grid=(ATTN_WIDTH // 128,),
        in_specs=[col] * 6, out_specs=[col] * 3,
        out_shape=[jax.ShapeDtypeStruct((SEQ, ATTN_WIDTH), F32)] * 3,
        compiler_params=_params(1),
    )(q, k, v, d_out, delta, lse)


def _attn_bwd_post(name, grads, cos_t, sin_t):
    t = ROW_TILE
    w = ATTN_WIDTH

    def body(*refs):
        ins, cos_ref, sin_ref, out_ref, s4, s16 = refs[:9], refs[9], refs[10], refs[11], refs[12], refs[13]
        cosv, sinv = cos_ref[...], sin_ref[...]
        for a in range(3):
            g1, g4, g16 = ins[a], ins[3 + a], ins[6 + a]
            for cb in range(w // 128):
                cols = slice(cb * 128, (cb + 1) * 128)
                _unpermute(s4, g4, 4, cols)
                _unpermute(s16, g16, 16, cols)
                val = g1[:, cols] + s4[...] + s16[...]
                if a < 2:
                    val = val * cosv + _swap_halves(val * sinv)
                if a == 0:
                    val = val * (HEAD_DIM ** -0.5)
                out_ref[:, a * w + cb * 128:a * w + (cb + 1) * 128] = val.astype(BF16)

    views = []
    for p, d in enumerate(DILATIONS):
        for a in range(3):
            views.append(grads[p][a] if d == 1 else grads[p][a].reshape(d, SEQ // d, w))
    perm = _permuted_specs(t, w)
    in_specs = [perm[0]] * 3 + [perm[1]] * 3 + [perm[2]] * 3
    return pl.pallas_call(
        body, name=name, grid=(SEQ // t,),
        in_specs=in_specs + [_spec((t, 128), lambda i: (i, 0))] * 2,
        out_specs=_spec((t, 3 * w), lambda i: (i, 0)),
        out_shape=jax.ShapeDtypeStruct((SEQ, 3 * w), BF16),
        scratch_shapes=[pltpu.VMEM((t, 128), F32)] * 2,
        compiler_params=_params(1),
    )(*views, cos_t, sin_t)


N_LEVELS = 7
HGRN_PAIR = 2


def _hgrn_consts():
    c = CHUNK
    i = np.arange(c)[:, None]
    s = np.arange(c)[None, :]
    blocks = [s <= i]
    for lv in range(N_LEVELS):
        bs = c >> lv
        h = bs // 2
        m = (i // bs) * bs + h - 1
        second = (i % bs) >= h
        blocks.append((second & (s > m) & (s <= i)) | (~second & (s > i) & (s <= m)))
    blocks.append(s > i)
    stack = np.concatenate(blocks, axis=0).astype(np.float32)
    return jnp.asarray(stack, dtype=BF16), jnp.asarray(stack.T, dtype=BF16)


def _exact_dot(m01, x):
    hi = x.astype(BF16)
    lo = (x - hi.astype(F32)).astype(BF16)
    n = x.shape[1]
    full = jnp.dot(m01, jnp.concatenate([hi, lo], axis=1), preferred_element_type=F32)
    return full[:, :n] + full[:, n:]


def _hgrn_gates(qh, z, lb):
    sq = _sigmoid(qh)
    q = qh * sq * (HGRN_DIM ** -0.5)
    sig = _sigmoid(z)
    sigm = _sigmoid(-z)
    f = lb + (1.0 - lb) * sig
    k = (1.0 - lb) * sigm
    return q, k, f, sq, sig, sigm


def _level_masks(lv):
    row = lax.broadcasted_iota(jnp.int32, (CHUNK, CHUNK), 0)
    col = lax.broadcasted_iota(jnp.int32, (CHUNK, CHUNK), 1)
    shift = N_LEVELS - lv
    second = (row & (CHUNK >> (lv + 1))) != 0
    same = (row >> shift) == (col >> shift)
    return second, same


def _hgrn_fwd(name, proj, lb, gain, stack, mixed, mixed_t):
    t = ROW_TILE
    per = t // CHUNK
    n_rb = SEQ // t
    n_chunks = SEQ // CHUNK
    col0 = 3 * ATTN_WIDTH // 128
    pair_w = HGRN_PAIR * HGRN_DIM

    def body(q_ref, f_ref, i_ref, g_ref, lb_ref, gain_ref, stack_ref, mixed_in, mixed_t_in,
             rec_ref, rect_ref, o_ref, st_out, a_out, st):
        del mixed_in, mixed_t_in

        @pl.when(pl.program_id(1) == 0)
        def _():
            st[...] = jnp.zeros_like(st)

        row = lax.broadcasted_iota(jnp.int32, (CHUNK, CHUNK), 0)
        col = lax.broadcasted_iota(jnp.int32, (CHUNK, CHUNK), 1)
        for c, hh in [(c, hh) for c in range(per) for hh in range(HGRN_PAIR)]:
            rows = slice(c * CHUNK, (c + 1) * CHUNK)
            lanes = slice(hh * HGRN_DIM, (hh + 1) * HGRN_DIM)
            lbv = lb_ref[hh]
            qh, z, v, gh = q_ref[rows, lanes], f_ref[rows, lanes], i_ref[rows, lanes], g_ref[rows, lanes]
            q, k, f, _, _, _ = _hgrn_gates(qh, z, lbv)
            dec = _exact_dot(stack_ref[...], jnp.log(f))
            g = dec[0:CHUNK]
            to_end = dec[(N_LEVELS + 1) * CHUNK:(N_LEVELS + 2) * CHUNK]
            a = jnp.where(row == col, jnp.sum(q * k, axis=1, keepdims=True), 0.0)
            for lv in range(N_LEVELS):
                e = jnp.exp(dec[(lv + 1) * CHUNK:(lv + 2) * CHUNK])
                second, same = _level_masks(lv)
                qt = jnp.where(second, q * e, 0.0)
                kt = jnp.where(second, 0.0, k * e)
                a = a + jnp.where(same, _dot_nt(qt, kt), 0.0)
            st_prev = st[hh]
            st_out[hh, c] = st_prev
            a_out[hh, c] = a
            o = _dot(a, v) + _dot_nt(q * jnp.exp(g), st_prev)
            k_end = k * jnp.exp(to_end)
            st[hh] = st_prev * jnp.exp(g[CHUNK - 1:CHUNK, :]) + _dot(v.T, k_end)
            o_ref[rows, lanes] = o
            r = lax.rsqrt(jnp.mean(o * o, axis=1, keepdims=True) + NORM_EPS)
            rec = o * r * gain_ref[...] * (gh * _sigmoid(gh))
            rec_ref[rows, lanes] = rec.astype(BF16)
            rect_ref[lanes, rows] = rec.T.astype(BF16)

    def col_spec(tt):
        return _spec((t, pair_w), lambda h, rb: (rb, (col0 + HGRN_HEADS * tt) // HGRN_PAIR + h))

    chunk_spec = _spec((HGRN_PAIR, per, CHUNK, CHUNK), lambda h, rb: (h, rb, 0, 0))
    return pl.pallas_call(
        body, name=name, grid=(HGRN_HEADS // HGRN_PAIR, n_rb),
        in_specs=[col_spec(0), col_spec(1), col_spec(2), col_spec(3),
                  _spec((HGRN_PAIR, 1, HGRN_DIM), lambda h, rb: (h, 0, 0)),
                  _spec((1, HGRN_DIM), lambda h, rb: (0, 0)),
                  _spec(stack.shape, lambda h, rb: (0, 0)), ANY_SPEC, ANY_SPEC],
        out_specs=[_spec((t, pair_w), lambda h, rb: (rb, ATTN_WIDTH // pair_w + h)),
                   _spec((pair_w, t), lambda h, rb: (ATTN_WIDTH // pair_w + h, rb)),
                   _spec((t, pair_w), lambda h, rb: (rb, h)),
                   chunk_spec, chunk_spec],
        out_shape=[jax.ShapeDtypeStruct(mixed.shape, BF16),
                   jax.ShapeDtypeStruct(mixed_t.shape, BF16),
                   jax.ShapeDtypeStruct((SEQ, HGRN_WIDTH), F32),
                   jax.ShapeDtypeStruct((HGRN_HEADS, n_chunks, CHUNK, CHUNK), F32),
                   jax.ShapeDtypeStruct((HGRN_HEADS, n_chunks, CHUNK, CHUNK), F32)],
        scratch_shapes=[pltpu.VMEM((HGRN_PAIR, CHUNK, CHUNK), F32)],
        input_output_aliases={7: 0, 8: 1},
        compiler_params=_params(2),
    )(proj, proj, proj, proj, lb, gain, stack, mixed, mixed_t)


def _hgrn_bwd(name, proj, d_rec, o_pre, states, scores, lb, gain, stack, stack_t):
    t = ROW_TILE
    per = t // CHUNK
    n_rb = SEQ // t
    col0 = 3 * ATTN_WIDTH // 128
    pair_w = HGRN_PAIR * HGRN_DIM

    def body(q_ref, f_ref, i_ref, g_ref, drec_ref, o_ref, st_ref, a_ref, lb_ref, gain_ref,
             stack_ref, stack_t_ref, dq_ref, df_ref, di_ref, dg_ref, dlb_ref, dgain_ref, dst):
        @pl.when(pl.program_id(1) == 0)
        def _():
            dst[...] = jnp.zeros_like(dst)
            dlb_ref[...] = jnp.zeros_like(dlb_ref)
            dgain_ref[...] = jnp.zeros_like(dgain_ref)

        gain_v = gain_ref[...]
        row = lax.broadcasted_iota(jnp.int32, (CHUNK, CHUNK), 0)
        col = lax.broadcasted_iota(jnp.int32, (CHUNK, CHUNK), 1)
        for c, hh in [(c, hh) for c in reversed(range(per)) for hh in range(HGRN_PAIR)]:
            rows = slice(c * CHUNK, (c + 1) * CHUNK)
            lanes = slice(hh * HGRN_DIM, (hh + 1) * HGRN_DIM)
            lbv = lb_ref[hh]
            qh, z, v, gh = q_ref[rows, lanes], f_ref[rows, lanes], i_ref[rows, lanes], g_ref[rows, lanes]
            q, k, f, sq, sig, sigm = _hgrn_gates(qh, z, lbv)
            dec = _exact_dot(stack_ref[...], jnp.log(f))
            g = dec[0:CHUNK]
            to_end = dec[(N_LEVELS + 1) * CHUNK:(N_LEVELS + 2) * CHUNK]
            e_g = jnp.exp(g)
            e_end = jnp.exp(to_end)
            e_last = jnp.exp(g[CHUNK - 1:CHUNK, :])
            q_in = q * e_g
            k_end = k * e_end
            st_prev = st_ref[hh, c]
            a = a_ref[hh, c]
            dst_new = dst[hh]

            o = o_ref[rows, lanes]
            drec = drec_ref[rows, lanes]
            sg = _sigmoid(gh)
            r = lax.rsqrt(jnp.mean(o * o, axis=1, keepdims=True) + NORM_EPS)
            ohat = o * r
            d_gh = drec * (ohat * gain_v) * (sg * (1.0 + gh * (1.0 - sg)))
            d_on = drec * (gh * sg)
            dgain_ref[hh] += jnp.sum(d_on * ohat, axis=0, keepdims=True)
            d_ohat = d_on * gain_v
            d_o = r * (d_ohat - ohat * jnp.mean(d_ohat * ohat, axis=1, keepdims=True))

            d_a = jnp.where(row >= col, _dot_nt(d_o, v), 0.0)
            d_at = jnp.where(col >= row, _dot_nt(v, d_o), 0.0)
            d_v = _dot(a.T, d_o) + _dot_nt(k_end, dst_new)
            d_q_in = _dot(d_o, st_prev)
            d_k_end = _dot(v, dst_new)
            d_q = d_q_in * e_g
            d_k = d_k_end * e_end
            diag = jnp.sum(d_o * v, axis=1, keepdims=True)
            d_q = d_q + diag * k
            d_k = d_k + diag * q
            d_dec = [q_in * d_q_in]
            for lv in range(N_LEVELS):
                e = jnp.exp(dec[(lv + 1) * CHUNK:(lv + 2) * CHUNK])
                second, same = _level_masks(lv)
                qt = jnp.where(second, q * e, 0.0)
                kt = jnp.where(second, 0.0, k * e)
                d_qt = _dot(jnp.where(same, d_a, 0.0), kt)
                d_kt = _dot(jnp.where(same, d_at, 0.0), qt)
                d_q = d_q + jnp.where(second, d_qt * e, 0.0)
                d_k = d_k + jnp.where(second, 0.0, d_kt * e)
                d_dec.append(jnp.where(second, qt * d_qt, kt * d_kt))
            d_dec.append(k_end * d_k_end)
            flux = jnp.sum(dst_new * st_prev, axis=0, keepdims=True) * e_last
            d_lf = _exact_dot(stack_t_ref[...], jnp.concatenate(d_dec, axis=0)) + flux
            dst[hh] = dst_new * e_last + _dot(d_o.T, q_in)

            d_f = d_lf / f - d_k
            dlb_ref[hh] += jnp.sum(d_f * sigm, axis=0, keepdims=True)
            dq_ref[rows, lanes] = (d_q * (HGRN_DIM ** -0.5) * (sq * (1.0 + qh * (1.0 - sq)))).astype(BF16)
            df_ref[rows, lanes] = (d_f * (1.0 - lbv) * sig * sigm).astype(BF16)
            di_ref[rows, lanes] = d_v.astype(BF16)
            dg_ref[rows, lanes] = d_gh.astype(BF16)

    last = n_rb - 1

    def col_spec(tt):
        return _spec((t, pair_w), lambda h, rb: (last - rb, (col0 + HGRN_HEADS * tt) // HGRN_PAIR + h))

    head_col = _spec((t, pair_w), lambda h, rb: (last - rb, h))
    rec_col0 = (d_rec.shape[1] - HGRN_WIDTH) // pair_w
    d_rec_col = _spec((t, pair_w), lambda h, rb: (last - rb, rec_col0 + h))
    chunk_spec = _spec((HGRN_PAIR, per, CHUNK, CHUNK), lambda h, rb: (h, last - rb, 0, 0))
    vec_spec = _spec((HGRN_PAIR, 1, HGRN_DIM), lambda h, rb: (h, 0, 0))
    outs = pl.pallas_call(
        body, name=name, grid=(HGRN_HEADS // HGRN_PAIR, n_rb),
        in_specs=[col_spec(0), col_spec(1), col_spec(2), col_spec(3), d_rec_col, head_col,
                  chunk_spec, chunk_spec, vec_spec,
                  _spec((1, HGRN_DIM), lambda h, rb: (0, 0)),
                  _spec(stack.shape, lambda h, rb: (0, 0)), _spec(stack_t.shape, lambda h, rb: (0, 0))],
        out_specs=[head_col] * 4 + [vec_spec, vec_spec],
        out_shape=[jax.ShapeDtypeStruct((SEQ, HGRN_WIDTH), BF16)] * 4
                  + [jax.ShapeDtypeStruct((HGRN_HEADS, 1, HGRN_DIM), F32)] * 2,
        scratch_shapes=[pltpu.VMEM((HGRN_PAIR, CHUNK, CHUNK), F32)],
        compiler_params=_params(2),
    )(proj, proj, proj, proj, d_rec, o_pre, states, scores, lb, gain, stack, stack_t)
    return outs


ANY_SPEC = pl.BlockSpec(memory_space=pl.ANY)


def _my_place():
    return lax.axis_index("x"), lax.axis_index("y"), lax.axis_index("c")


def _other_chips(x, y):
    return [(1 - x, y), (x, 1 - y), (1 - x, 1 - y)]


def _remote(src, dst, send_sem, recv_sem, device):
    return pltpu.make_async_remote_copy(src_ref=src, dst_ref=dst, send_sem=send_sem, recv_sem=recv_sem,
                                        device_id=device, device_id_type=MESH)


def _staged_copies(srcs, dsts, stage, sems):
    loads = [pltpu.make_async_copy(srcs[i], stage[i], sems.at[i]) for i in range(len(srcs))]
    for cp in loads:
        cp.start()
    stores = []
    for i, cp in enumerate(loads):
        cp.wait()
        stores.append(pltpu.make_async_copy(stage[i], dsts[i], sems.at[i]))
        stores[-1].start()
    return stores


def _gather_weights(name, shards):
    n = len(shards)

    def body(*refs):
        ins, outs = refs[:n], refs[n:2 * n]
        ici_send, ici_recv, d2d_send, d2d_recv, local_sems = refs[2 * n:2 * n + 5]
        stage = refs[2 * n + 5:]
        x, y, c = _my_place()
        me = 2 * x + y
        chips = _other_chips(x, y)

        def half(i, which):
            h = ins[i].shape[0] // 2
            return pl.ds(which * h, h)

        sends = []
        for i in range(n):
            for j, (px, py) in enumerate(chips):
                sends.append(_remote(ins[i].at[half(i, c), :], outs[i].at[me, half(i, c), :],
                                     ici_send.at[3 * i + j], ici_recv.at[3 * i + j], (px, py, c)))
        for cp in sends:
            cp.start()
        local = _staged_copies(ins, [outs[i].at[me] for i in range(n)], stage, local_sems)
        for i in range(n):
            for j, (px, py) in enumerate(chips):
                landed = outs[i].at[2 * px + py, half(i, c), :]
                _remote(landed, landed, ici_send.at[3 * i + j], ici_recv.at[3 * i + j], (px, py, c)).wait_recv()
                forward = _remote(landed, landed, d2d_send.at[3 * i + j], d2d_recv.at[3 * i + j], (x, y, 1 - c))
                forward.start()
                sends.append(forward)
        for i in range(n):
            for j, (px, py) in enumerate(chips):
                other = outs[i].at[2 * px + py, half(i, 1 - c), :]
                _remote(other, other, d2d_send.at[3 * i + j], d2d_recv.at[3 * i + j], (x, y, 1 - c)).wait_recv()
        for cp in sends:
            cp.wait_send()
        for cp in local:
            cp.wait()

    return pl.pallas_call(
        body, name=name, in_specs=[ANY_SPEC] * n, out_specs=[ANY_SPEC] * n,
        out_shape=[jax.ShapeDtypeStruct((N_CHIPS,) + s.shape, s.dtype) for s in shards],
        scratch_shapes=([pltpu.SemaphoreType.DMA((3 * n,))] * 4 + [pltpu.SemaphoreType.DMA((n,))]
                        + [pltpu.VMEM(s.shape, s.dtype) for s in shards]),
        compiler_params=pltpu.CompilerParams(vmem_limit_bytes=VMEM_LIMIT),
    )(*shards)


def _exchange_halves(name, grads):
    n = len(grads)

    def body(*refs):
        ins, outs = refs[:n], refs[n:2 * n]
        send_sems, recv_sems = refs[2 * n:]
        x, y, c = _my_place()
        copies = []
        for i in range(n):
            h = ins[i].shape[1] // 2
            copies.append(_remote(ins[i].at[:, pl.ds((1 - c) * h, h), :], outs[i],
                                  send_sems.at[i], recv_sems.at[i], (x, y, 1 - c)))
        for cp in copies:
            cp.start()
        for cp in copies:
            cp.wait()

    return pl.pallas_call(
        body, name=name, in_specs=[ANY_SPEC] * n, out_specs=[ANY_SPEC] * n,
        out_shape=[jax.ShapeDtypeStruct((g.shape[0], g.shape[1] // 2, g.shape[2]), g.dtype) for g in grads],
        scratch_shapes=[pltpu.SemaphoreType.DMA((n,)), pltpu.SemaphoreType.DMA((n,))],
    )(*grads)


def _add_own_half(name, g, received, core):
    n_sh, r, cc = g.shape
    h = r // 2
    th = min(h, 256)
    nb = h // th

    def body(core_ref, g_ref, r_ref, o_ref):
        del core_ref
        o_ref[...] = (g_ref[...] + r_ref[...]).astype(BF16)

    grid_spec = pltpu.PrefetchScalarGridSpec(
        num_scalar_prefetch=1, grid=(n_sh, nb),
        in_specs=[pl.BlockSpec((None, th, cc), lambda j, i, core_ref: (j, core_ref[0] * nb + i, 0)),
                  pl.BlockSpec((None, th, cc), lambda j, i, core_ref: (j, i, 0))],
        out_specs=pl.BlockSpec((None, th, cc), lambda j, i, core_ref: (j, i, 0)))
    return pl.pallas_call(
        body, name=name, grid_spec=grid_spec,
        out_shape=jax.ShapeDtypeStruct((n_sh, h, cc), BF16), compiler_params=_params(2),
    )(core, g, received)


HBM_SPEC = pl.BlockSpec(memory_space=pltpu.HBM)
SEM_SPEC = pl.BlockSpec(memory_space=pltpu.SEMAPHORE)
SPLIT_PARAMS = pltpu.CompilerParams(has_side_effects=pltpu.SideEffectType.DATAFLOW_SIDE_EFFECTING)


def _chip_copies(ins, lands, send_sems, recv_sems, sliced):
    x, y, c = _my_place()
    me = 2 * x + y
    pairs = []
    for i in range(len(ins)):
        for j, (px, py) in enumerate(_other_chips(x, y)):
            theirs = 2 * px + py
            src = ins[i].at[theirs] if sliced else ins[i]
            sems = (send_sems.at[3 * i + j], recv_sems.at[3 * i + j], (px, py, c))
            pairs.append((_remote(src, lands[i].at[me], *sems), _remote(src, lands[i].at[theirs], *sems)))
    return pairs


def _exchange_start(name, srcs, lands, sliced, after):
    n = len(srcs)

    def body(*refs):
        ins, land_refs = refs[:n], refs[n:2 * n]
        send_sems, recv_sems = refs[2 * n + 1:2 * n + 3]
        token = refs[-1]
        for send, _ in _chip_copies(ins, land_refs, send_sems, recv_sems, sliced):
            send.start()
        token[...] = jnp.zeros_like(token)

    arrays = list(srcs) + list(lands)
    outs = pl.pallas_call(
        body, name=name,
        in_specs=[HBM_SPEC] * (2 * n) + [ANY_SPEC],
        out_shape=([pltpu.SemaphoreType.DMA((3 * n,))] * 2 + [pltpu.HBM(a.shape, a.dtype) for a in arrays]
                   + [jax.ShapeDtypeStruct((8, 128), F32)]),
        out_specs=[SEM_SPEC] * 2 + [HBM_SPEC] * (2 * n) + [pl.BlockSpec(memory_space=pltpu.VMEM)],
        input_output_aliases={i: 2 + i for i in range(2 * n)},
        compiler_params=SPLIT_PARAMS,
    )(*[pltpu.with_memory_space_constraint(a, pltpu.HBM) for a in arrays], after)
    return outs[:2], outs[2:2 + 2 * n], outs[-1]


def _exchange_wait(name, sems, passed, sliced, after):
    n = len(passed) // 2

    def body(*refs):
        ins, land_refs = refs[:n], refs[n:2 * n]
        send_sems, recv_sems = refs[2 * n:2 * n + 2]
        for send, arrive in _chip_copies(ins, land_refs, send_sems, recv_sems, sliced):
            send.wait_send()
            arrive.wait_recv()

    outs = pl.pallas_call(
        body, name=name,
        in_specs=[HBM_SPEC] * (2 * n) + [SEM_SPEC] * 2 + [ANY_SPEC],
        out_shape=[pltpu.HBM(a.shape, a.dtype) for a in passed],
        out_specs=[HBM_SPEC] * (2 * n),
        input_output_aliases={i: i for i in range(2 * n)},
        compiler_params=SPLIT_PARAMS,
    )(*passed, *sems, after)
    return outs[n:]


def _own_slot(name, own, me):
    r, cc = own.shape[-2:]
    th = min(r, 512)

    def body(me_ref, x_ref, o_ref):
        del me_ref
        o_ref[...] = x_ref[...]

    if own.ndim == 3:
        in_spec = pl.BlockSpec((None, th, cc), lambda i, me_ref: (me_ref[0], i, 0))
    else:
        in_spec = pl.BlockSpec((th, cc), lambda i, me_ref: (i, 0))
    grid_spec = pltpu.PrefetchScalarGridSpec(
        num_scalar_prefetch=1, grid=(r // th,), in_specs=[in_spec],
        out_specs=pl.BlockSpec((None, th, cc), lambda i, me_ref: (me_ref[0], i, 0)))
    return pl.pallas_call(
        body, name=name, grid_spec=grid_spec,
        out_shape=jax.ShapeDtypeStruct((N_CHIPS, r, cc), own.dtype), compiler_params=_params(1),
    )(me, own)


def _sum_chips(name, parts):
    n_sh, h, cc = parts.shape
    th = min(h, 256)

    def body(p_ref, o_ref):
        p = [p_ref[j].astype(F32) for j in range(n_sh)]
        o_ref[...] = ((p[0] + p[1]) + p[2]) + p[3]

    return pl.pallas_call(
        body, name=name, grid=(h // th,),
        in_specs=[_spec((n_sh, th, cc), lambda i: (0, i, 0))],
        out_specs=_spec((th, cc), lambda i: (i, 0)),
        out_shape=jax.ShapeDtypeStruct((h, cc), F32), compiler_params=_params(1),
    )(parts)


def _share_halves(name, halves):
    flat = [t for per_weight in halves for t in per_weight]
    n = len(flat)
    n_w = len(halves)

    def body(*refs):
        ins, outs = refs[:n], refs[n:n + n_w]
        send_sems, recv_sems, local_sems = refs[n + n_w:n + n_w + 3]
        stage = refs[n + n_w + 3:]
        x, y, c = _my_place()
        sends, own = [], []
        for i in range(n):
            w, l = divmod(i, DEPTH)
            h = ins[i].shape[0]
            own.append(outs[w].at[l, pl.ds(c * h, h), :])
            sends.append(_remote(ins[i], own[i], send_sems.at[i], recv_sems.at[i], (x, y, 1 - c)))
        for cp in sends:
            cp.start()
        local = _staged_copies(ins, own, stage, local_sems)
        for i in range(n):
            w, l = divmod(i, DEPTH)
            h = ins[i].shape[0]
            _remote(ins[i], outs[w].at[l, pl.ds((1 - c) * h, h), :], send_sems.at[i], recv_sems.at[i],
                    (x, y, 1 - c)).wait_recv()
        for cp in sends:
            cp.wait_send()
        for cp in local:
            cp.wait()

    return pl.pallas_call(
        body, name=name, in_specs=[ANY_SPEC] * n, out_specs=[ANY_SPEC] * n_w,
        out_shape=[jax.ShapeDtypeStruct((DEPTH, 2 * per_weight[0].shape[0], per_weight[0].shape[1]), F32)
                   for per_weight in halves],
        scratch_shapes=([pltpu.SemaphoreType.DMA((n,))] * 3 + [pltpu.VMEM(t.shape, t.dtype) for t in flat]),
        compiler_params=pltpu.CompilerParams(vmem_limit_bytes=VMEM_LIMIT),
    )(*flat)


def _all_reduce_small(pack):
    def body(p_ref, o_ref, recv, send_sems, recv_sems):
        x, y, c = _my_place()
        me = 4 * x + 2 * y + c
        recv[me] = p_ref[...]
        peers = []
        for k in range(1, N_DEV):
            px, py, pc = (x + (k >> 2)) % 2, (y + ((k >> 1) & 1)) % 2, (c + (k & 1)) % 2
            peers.append((px, py, pc))
        sends = [_remote(p_ref, recv.at[me], send_sems.at[k], recv_sems.at[k], peer)
                 for k, peer in enumerate(peers)]
        for cp in sends:
            cp.start()
        for k, (px, py, pc) in enumerate(peers):
            _remote(p_ref, recv.at[4 * px + 2 * py + pc], send_sems.at[k], recv_sems.at[k],
                    (px, py, pc)).wait_recv()
        for cp in sends:
            cp.wait_send()
        total = recv[0]
        for d in range(1, N_DEV):
            total = total + recv[d]
        o_ref[...] = total

    vmem = pl.BlockSpec(memory_space=pltpu.VMEM)
    return pl.pallas_call(
        body, name="all_reduce_small", in_specs=[vmem], out_specs=vmem,
        out_shape=jax.ShapeDtypeStruct(pack.shape, F32),
        scratch_shapes=[pltpu.VMEM((N_DEV,) + pack.shape, F32),
                        pltpu.SemaphoreType.DMA((N_DEV - 1,)), pltpu.SemaphoreType.DMA((N_DEV - 1,))],
    )(pack)


def _adamw(name, w, g, m, v):
    r, cc = w.shape
    th = min(r, 256)

    def body(w_ref, g_ref, m_ref, v_ref, d_ref, m_out, v_out):
        gv = g_ref[...]
        m2 = ADAM_B1 * m_ref[...] + (1.0 - ADAM_B1) * gv
        v2 = ADAM_B2 * v_ref[...] + (1.0 - ADAM_B2) * (gv * gv)
        m_hat = m2 / (1.0 - ADAM_B1 ** ADAM_STEP)
        v_hat = v2 / (1.0 - ADAM_B2 ** ADAM_STEP)
        d_ref[...] = -ADAM_LR * (m_hat / (jnp.sqrt(v_hat) + ADAM_EPS) + ADAM_WD * w_ref[...])
        m_out[...] = m2
        v_out[...] = v2

    tile = _spec((th, cc), lambda i: (i, 0))
    return pl.pallas_call(
        body, name=name, grid=(r // th,), in_specs=[tile] * 4, out_specs=[tile] * 3,
        out_shape=[jax.ShapeDtypeStruct((r, cc), F32)] * 3, compiler_params=_params(1),
    )(w, g, m, v)


def _lower_bounds(lb_logits):
    p = jax.nn.softmax(lb_logits.astype(F32), axis=0)
    return jnp.cumsum(p, axis=0) - p[0]


def _layer_forward(l, x_in, small, weights, consts, after=None):
    win, rest = weights
    cos_t, sin_t, stack, _, _ = consts
    tm = MM_TILE
    saved = {"x_in": x_in}

    h, h_t = _rms_fwd(f"norm_mix{l}", x_in, small["norm_mix"][l][None, :], after=after)
    proj = _mm_pieces(f"proj{l}", h, win, False, tm)
    saved.update(h_t=h_t, proj=proj)

    qkv = _attn_prep(f"attn_prep{l}", proj, cos_t, sin_t)
    outs, lses = [], []
    for p, d in enumerate(DILATIONS):
        o, lse = _attn_fwd(f"attn_fwd{l}_{d}", *qkv[p], SEQ // d // SPAN)
        outs.append(o)
        lses.append(lse)
    mixed, mixed_t, attn, lse = _attn_merge(f"attn_merge{l}", outs, lses, small["attn_out_gain"][l][None, :])
    saved.update(qkv=qkv, attn=attn, lse=lse)

    lb3 = small["lower"][l].reshape(HGRN_HEADS, 1, HGRN_DIM)
    mixed, mixed_t, o_pre, states, scores = _hgrn_fwd(f"hgrn_fwd{l}", proj, lb3, small["hgrn_out_gain"][l][None, :],
                                                      stack, mixed, mixed_t)
    wo, wu, wd = rest(mixed)
    saved.update(mixed_t=mixed_t, o_pre=o_pre, states=states, scores=scores, lb3=lb3, weights=(win, wo, wu, wd))

    x_mid = _mm_accum(f"out_proj{l}", mixed, wo, False, tm, x_in)
    saved["x_mid"] = x_mid

    h2, h2_t = _rms_fwd(f"norm_mlp{l}", x_mid, small["norm_mlp"][l][None, :])
    a, relu_u, a_t = _mm_pieces(f"up{l}", h2, wu, False, tm, epilogue="relu2")
    x_out = _mm_accum(f"down{l}", a, wd, False, tm, x_mid)
    saved.update(h2_t=h2_t, relu_u=relu_u, a_t=a_t)
    return x_out, saved


def _layer_backward(l, dx, saved, small, consts, on_grads, after=None):
    win, wo, wu, wd = saved["weights"]
    cos_t, sin_t, stack, stack_t, head_sum = consts
    tm = MM_TILE

    dx, dx_b = dx
    du = _mm_pieces(f"d_u{l}", dx_b, wd, True, tm, epilogue="relu2_grad", extra=saved["relu_u"], after=after)
    d_wd = _mm_dw(f"d_wdown{l}", saved["a_t"], dx_b, False, tm)
    dxm, dxm_b, dg_mlp = _mm_accum(f"d_h2_{l}", du, wu, True, tm, dx,
                                   norm=(saved["x_mid"], small["norm_mlp"][l][None, :]))
    d_wu = _mm_dw(f"d_wup{l}", saved["h2_t"], du, True, tm)
    after_mlp = on_grads(l, "mlp", (d_wu, d_wd))

    d_mixed = _mm_pieces(f"d_mixed{l}", dxm_b, wo, True, tm, after=after_mlp)
    d_wo = _mm_dw(f"d_wout{l}", saved["mixed_t"], dxm_b, False, tm)
    d_rec = d_mixed

    d_out, delta, lses, dg_attn = _attn_bwd_prep(f"attn_bwd_prep{l}", d_mixed, saved["attn"], saved["lse"],
                                                 small["attn_out_gain"][l][None, :], head_sum)
    grads = []
    for p, d in enumerate(DILATIONS):
        grads.append(_attn_bwd(f"attn_bwd{l}_{d}", *saved["qkv"][p], d_out[p], delta[p], lses[p],
                               SEQ // d // SPAN))
    dp_attn = _attn_bwd_post(f"attn_bwd_post{l}", grads, cos_t, sin_t)

    dq_h, df_h, di_h, dg_h, d_lower, dg_hgrn = _hgrn_bwd(
        f"hgrn_bwd{l}", saved["proj"], d_rec, saved["o_pre"], saved["states"], saved["scores"],
        saved["lb3"], small["hgrn_out_gain"][l][None, :], stack, stack_t)
    dproj = jnp.concatenate([dp_attn, dq_h, df_h, di_h, dg_h], axis=1)

    dx_in, dx_in_b, dg_mix = _mm_accum(f"d_h{l}", dproj, win, True, tm, dxm,
                                       norm=(saved["x_in"], small["norm_mix"][l][None, :]))
    d_win = _mm_dw(f"d_win{l}", saved["h_t"], dproj, True, tm)

    small_grads = {"norm_mix": dg_mix[0], "attn_out_gain": dg_attn[0],
                   "lower": d_lower.reshape(HGRN_WIDTH),
                   "hgrn_out_gain": jnp.sum(dg_hgrn, axis=0).reshape(HGRN_DIM), "norm_mlp": dg_mlp[0]}
    return (dx_in, dx_in_b), on_grads(l, "mix", (d_win, d_wo)), small_grads


def _local_step(xs, target, small, get_weights, on_grads):
    consts = _rope_tables() + _hgrn_consts() + (_head_sum_matrix(),)
    stream = xs
    saved = []
    for l in range(DEPTH):
        w, after = get_weights(l, stream)
        stream, s = _layer_forward(l, stream, small, w, consts, after=after)
        saved.append(s)
    dx_f, dx_b, dg_final, loss = _loss_head(stream, small["norm_final"][None, :], target)
    dx = (dx_f, dx_b)
    small_grads = [None] * DEPTH
    after = None
    for l in reversed(range(DEPTH)):
        dx, after, small_grads[l] = _layer_backward(l, dx, saved[l], small, consts, on_grads, after=after)
    return loss, dx[0], dg_final[0], small_grads, after


def _pack_small(norm_mix, attn_out_gain, lb, hgrn_out_gain, norm_mlp, norm_final, last_row):
    rows = [norm_mix, attn_out_gain.reshape(1, D_MODEL), lb.reshape(1, D_MODEL),
            jnp.pad(hgrn_out_gain.reshape(1, DEPTH * HGRN_DIM), ((0, 0), (0, D_MODEL - DEPTH * HGRN_DIM))),
            norm_mlp, norm_final.reshape(1, D_MODEL), last_row.reshape(1, D_MODEL)]
    pack = jnp.concatenate(rows, axis=0)
    return jnp.pad(pack, ((0, PACK_ROWS - pack.shape[0]), (0, 0)))


def _unpack_small(pack):
    return (pack[0:2], pack[2].reshape(DEPTH, ATTN_WIDTH), pack[3].reshape(DEPTH, HGRN_WIDTH),
            pack[4, :DEPTH * HGRN_DIM].reshape(DEPTH, HGRN_DIM), pack[5:7], pack[7], pack[8])


def kernel(x, norm_mix, w_in, attn_out_gain, hgrn_lb_logits, hgrn_out_gain, w_out, norm_mlp, w_up, w_down, norm_final, loss_target, m_norm_mix, m_w_in, m_attn_out_gain, m_hgrn_lb_logits, m_hgrn_out_gain, m_w_out, m_norm_mlp, m_w_up, m_w_down, m_norm_final, v_norm_mix, v_w_in, v_attn_out_gain, v_hgrn_lb_logits, v_hgrn_out_gain, v_w_out, v_norm_mlp, v_w_up, v_w_down, v_norm_final):
    core = lax.axis_index("c").astype(jnp.int32).reshape(1)
    lower, lower_vjp = jax.vjp(_lower_bounds, hgrn_lb_logits)
    small = {"norm_mix": norm_mix, "attn_out_gain": attn_out_gain, "lower": lower,
             "hgrn_out_gain": hgrn_out_gain, "norm_mlp": norm_mlp, "norm_final": norm_final}
    big_w = (w_in, w_out, w_up, w_down)

    me = (2 * lax.axis_index("x") + lax.axis_index("y")).astype(jnp.int32).reshape(1)
    shards =[[w[l].astype(BF16) for w in big_w] for l in range(DEPTH)]
    in_flight = {}

    def start_gather(name, some, after):
        lands = [_own_slot(f"own_{name}_{i}", s, me) for i, s in enumerate(some)]
        sems, passed, token = _exchange_start(name, some, lands, False, after)
        return (sems, passed), token

    def get_weights(l, stream):
        if l == 0:
            (win,) = _gather_weights("gather_w_in0", shards[0][:1])
            in_flight["rest0"], token = start_gather("gather_start0", shards[0][1:], win)
            in_flight["weights1"], token = start_gather("gather_start1", shards[1], token)
            return (win, lambda after: _exchange_wait("gather_wait0", *in_flight.pop("rest0"), False, after)), token
        weights = _exchange_wait("gather_wait1", *in_flight.pop("weights1"), False, stream)
        return (weights[0], lambda after: weights[1:]), None

    reduced = {}

    def start_exchange(name, grads):
        received = _exchange_halves(f"halves_{name}", grads)
        halves = [_add_own_half(f"add_{name}_{i}", g, r, core) for i, (g, r) in enumerate(zip(grads, received))]
        lands = [_own_slot(f"own_{name}_{i}", h, me) for i, h in enumerate(halves)]
        sems, passed, token = _exchange_start(f"start_{name}", halves, lands, True, halves[0])
        in_flight[name] = (sems, passed)
        return token

    def finish_exchange(name, after):
        landed = _exchange_wait(f"wait_{name}", *in_flight.pop(name), True, after)
        return [_sum_chips(f"sum_{name}_{i}", p) for i, p in enumerate(landed)]

    def on_grads(l, group, grads):
        if (l, group) == (1, "mlp"):
            return start_exchange("mlp1", grads)
        if (l, group) == (1, "mix"):
            return start_exchange("mix1", grads)
        if (l, group) == (0, "mlp"):
            reduced[(1, "mlp")] = finish_exchange("mlp1", grads[0])
            reduced[(1, "mix")] = finish_exchange("mix1", grads[0])
            return start_exchange("mlp0", grads)
        token = start_exchange("mix0", grads)
        reduced[(0, "mlp")] = finish_exchange("mlp0", token)
        return token

    loss, dx, dg_final, sg, last_started = _local_step(x[0], loss_target[0], small, get_weights, on_grads)

    big_m = (m_w_in, m_w_out, m_w_up, m_w_down)
    big_v = (v_w_in, v_w_out, v_w_up, v_w_down)
    names = ("w_in", "w_out", "w_up", "w_down")
    big_g, big_delta, big_new_m, big_new_v = [None] * 4, [None] * 4, [None] * 4, [None] * 4

    def finish_weights(group, which):
        whole = _share_halves(f"share_{group}", [[reduced[(l, group)][i] for l in range(DEPTH)] for i in range(2)])
        for i, w in enumerate(which):
            shape = big_w[w].shape
            flat = lambda arr: arr.reshape(shape[0] * shape[1], shape[2])
            d, m2, v2 = _adamw(f"adamw_{names[w]}", flat(big_w[w]), flat(whole[i]), flat(big_m[w]), flat(big_v[w]))
            big_g[w], big_delta[w] = whole[i], d.reshape(shape)
            big_new_m[w], big_new_v[w] = m2.reshape(shape), v2.reshape(shape)

    finish_weights("mlp", (2, 3))

    stack2 = lambda key: jnp.stack([sg[l][key] for l in range(DEPTH)])
    pack = _pack_small(stack2("norm_mix"), stack2("attn_out_gain"), stack2("lower"), stack2("hgrn_out_gain"),
                       stack2("norm_mlp"), dg_final, jnp.broadcast_to(loss[0, 0] + last_started[0, 0], (D_MODEL,)))
    g_mix, g_attn, g_lower, g_hgrn, g_mlp, g_final, loss_row = _unpack_small(_all_reduce_small(pack))
    (g_logits,) = lower_vjp(g_lower)

    zeros_row = jnp.zeros((D_MODEL,), F32)
    small_w = (norm_mix, attn_out_gain, hgrn_lb_logits, hgrn_out_gain, norm_mlp, norm_final)
    small_m = (m_norm_mix, m_attn_out_gain, m_hgrn_lb_logits, m_hgrn_out_gain, m_norm_mlp, m_norm_final)
    small_v = (v_norm_mix, v_attn_out_gain, v_hgrn_lb_logits, v_hgrn_out_gain, v_norm_mlp, v_norm_final)
    small_g = (g_mix, g_attn, g_logits, g_hgrn, g_mlp, g_final)
    packs = [_pack_small(*t, zeros_row) for t in (small_w, small_g, small_m, small_v)]
    small_delta, small_new_m, small_new_v = [_unpack_small(p)[:6] for p in _adamw("adamw_small", *packs)]

    reduced[(0, "mix")] = finish_exchange("mix0", small_delta[0])
    finish_weights("mix", (0, 1))

    def ordered(small6, big4):
        mix, attn, lbl, hg, mlp, fin = small6
        return (mix, big4[0], attn, lbl, hg, big4[1], mlp, big4[2], big4[3], fin)

    return ((loss_row[0], dx[None]) + ordered(small_g, big_g) + ordered(small_delta, big_delta)
            + ordered(small_new_m, big_new_m) + ordered(small_new_v, big_new_v))
```

```python
import functools
import math

import numpy as np
import jax
import jax.numpy as jnp
from jax import lax
from jax.experimental import pallas as pl
from jax.experimental.pallas import tpu as pltpu

F32 = jnp.float32
BF16 = jnp.bfloat16
MESH = pl.DeviceIdType.MESH

SEQ = 4096
D_MODEL = 1024
DEPTH = 2
ATTN_WIDTH = 512
HEAD_DIM = 64
HGRN_HEADS = 4
HGRN_DIM = 128
HGRN_WIDTH = 512
IN_W = 3584
MLP_HIDDEN = 4096
N_CHIPS = 4
N_DEV = 8
SHARD_IN = IN_W // N_CHIPS
SHARD_OUT = D_MODEL // N_CHIPS
SHARD_MLP = MLP_HIDDEN // N_CHIPS
DILATIONS = (1, 4, 16)
SPAN = 128
ROPE_THETA = 10000.0
NORM_EPS = 1e-6
MASK_VALUE = -1e30
CHUNK = 128
ROW_TILE = 512
MM_TILE = 512
VMEM_LIMIT = 52 * 1024 * 1024

ADAM_LR = 0.001
ADAM_B1 = 0.9
ADAM_B2 = 0.999
ADAM_EPS = 1e-08
ADAM_WD = 0.01
ADAM_STEP = 10

PACK_ROWS = 16


def _params(n_axes):
    return pltpu.CompilerParams(dimension_semantics=("arbitrary",) * n_axes,
                                vmem_limit_bytes=VMEM_LIMIT)


def _dot(a, b):
    return jnp.dot(a.astype(BF16), b.astype(BF16), preferred_element_type=F32)


def _dot_nt(a, b):
    return lax.dot_general(a.astype(BF16), b.astype(BF16), (((1,), (1,)), ((), ())),
                           preferred_element_type=F32)


def _dot_tn(a, b):
    return lax.dot_general(a.astype(BF16), b.astype(BF16), (((0,), (0,)), ((), ())),
                           preferred_element_type=F32)


def _sigmoid(x):
    return 1.0 / (1.0 + jnp.exp(-x))


def _spec(shape, index_map):
    return pl.BlockSpec(shape, index_map)


def _mm_pieces(name, a, w, nt, tm, epilogue="none", extra=None, after=None):
    s = a.shape[0]
    pw = w.shape[1] if nt else w.shape[2]
    width = N_CHIPS * pw

    def body(a_ref, w_ref, *rest):
        e_ref = rest[0] if extra is not None else None
        outs = rest[-3:] if epilogue == "relu2" else rest[-1:]
        av = a_ref[...].astype(BF16)
        for j in range(N_CHIPS):
            cols = slice(j * pw, (j + 1) * pw)
            r = _dot_nt(av, w_ref[j]) if nt else _dot(av, w_ref[j])
            if epilogue == "relu2":
                relu = jnp.maximum(r, 0.0)
                r = relu * relu
                outs[1][:, cols] = relu.astype(BF16)
                outs[2][cols, :] = r.T.astype(BF16)
            elif epilogue == "relu2_grad":
                r = r * (2.0 * e_ref[:, cols].astype(F32))
            outs[0][:, cols] = r.astype(outs[0].dtype)

    row = lambda width_: _spec((tm, width_), lambda i: (i, 0))
    in_specs = [row(a.shape[1]), _spec(w.shape, lambda i: (0, 0, 0))]
    args = [a, w]
    if extra is not None:
        in_specs.append(row(width))
        args.append(extra)
    if after is not None:
        in_specs.append(pl.BlockSpec(memory_space=pl.ANY))
        args.append(after)
    if epilogue == "relu2":
        out_specs = [row(width), row(width), _spec((width, tm), lambda i: (0, i))]
        out_shape = [jax.ShapeDtypeStruct((s, width), BF16)] * 2 + [jax.ShapeDtypeStruct((width, s), BF16)]
    else:
        out_specs = row(width)
        out_shape = jax.ShapeDtypeStruct((s, width), BF16 if epilogue == "relu2_grad" else F32)
    return pl.pallas_call(body, name=name, grid=(s // tm,), in_specs=in_specs, out_specs=out_specs,
                          out_shape=out_shape, compiler_params=_params(1))(*args)


def _mm_accum(name, a, w, nt, tm, resid, norm=None, after=None):
    s = a.shape[0]
    pk = w.shape[2] if nt else w.shape[1]
    d = w.shape[1] if nt else w.shape[2]

    def body(a_ref, w_ref, resid_ref, *rest):
        r = None
        for j in range(N_CHIPS):
            piece = a_ref[:, j * pk:(j + 1) * pk].astype(BF16)
            term = _dot_nt(piece, w_ref[j]) if nt else _dot(piece, w_ref[j])
            r = term if r is None else r + term
        if norm is None:
            rest[-1][...] = r + resid_ref[...]
            return
        x_ref, g_ref = rest[:2]
        dx_ref, dxb_ref, dg_ref = rest[-3:]

        @pl.when(pl.program_id(0) == 0)
        def _():
            dg_ref[...] = jnp.zeros_like(dg_ref)

        xv = x_ref[...]
        rs = lax.rsqrt(jnp.mean(xv * xv, axis=1, keepdims=True) + NORM_EPS)
        xhat = xv * rs
        rg = r * g_ref[...]
        dx = resid_ref[...] + rs * (rg - xhat * jnp.mean(rg * xhat, axis=1, keepdims=True))
        dx_ref[...] = dx
        dxb_ref[...] = dx.astype(BF16)
        dg_ref[...] += jnp.sum(r * xhat, axis=0, keepdims=True)

    row = lambda width: _spec((tm, width), lambda i: (i, 0))
    in_specs = [row(a.shape[1]), _spec(w.shape, lambda i: (0, 0, 0)), row(d)]
    args = [a, w, resid]
    if norm is None:
        out_specs, out_shape = row(d), jax.ShapeDtypeStruct((s, d), F32)
    else:
        in_specs += [row(d), _spec((1, d), lambda i: (0, 0))]
        args += list(norm)
        out_specs = [row(d), row(d), _spec((1, d), lambda i: (0, 0))]
        out_shape = [jax.ShapeDtypeStruct((s, d), F32), jax.ShapeDtypeStruct((s, d), BF16),
                     jax.ShapeDtypeStruct((1, d), F32)]
    if after is not None:
        in_specs.append(pl.BlockSpec(memory_space=pl.ANY))
        args.append(after)
    return pl.pallas_call(body, name=name, grid=(s // tm,), in_specs=in_specs, out_specs=out_specs,
                          out_shape=out_shape, compiler_params=_params(1))(*args)


def _mm_dw(name, a_t, b, by_cols, tk):
    m, s = a_t.shape
    n = b.shape[1]
    shape = (N_CHIPS, m, n // N_CHIPS) if by_cols else (N_CHIPS, m // N_CHIPS, n)
    n_steps = s // tk

    def body(a_ref, b_ref, o_ref, land_ref, acc):
        del land_ref

        @pl.when(pl.program_id(0) == 0)
        def _():
            acc[...] = jnp.zeros_like(acc)

        for j in range(N_CHIPS):
            if by_cols:
                acc[j] += _dot(a_ref[...], b_ref[:, j * shape[2]:(j + 1) * shape[2]])
            else:
                acc[j] += _dot(a_ref[j * shape[1]:(j + 1) * shape[1], :], b_ref[...])

        @pl.when(pl.program_id(0) == n_steps - 1)
        def _():
            o_ref[...] = acc[...].astype(BF16)

    return pl.pallas_call(
        body, name=name, grid=(n_steps,),
        in_specs=[_spec((m, tk), lambda k: (0, k)), _spec((tk, n), lambda k: (k, 0))],
        out_specs=[_spec(shape, lambda k: (0, 0, 0)), ANY_SPEC],
        out_shape=[jax.ShapeDtypeStruct(shape, BF16),
                   jax.ShapeDtypeStruct((N_DEV, shape[1] // 2, shape[2]), BF16)],
        scratch_shapes=[pltpu.VMEM(shape, F32)],
        compiler_params=_params(1))(a_t, b)


def _rms_fwd(name, x, gain, after=None):
    s, d = x.shape
    t = ROW_TILE

    def body(x_ref, g_ref, *rest):
        h_ref, ht_ref = rest[-2:]
        xv = x_ref[...]
        r = lax.rsqrt(jnp.mean(xv * xv, axis=1, keepdims=True) + NORM_EPS)
        h = xv * r * g_ref[...]
        h_ref[...] = h.astype(BF16)
        ht_ref[...] = h.T.astype(BF16)

    in_specs = [_spec((t, d), lambda i: (i, 0)), _spec((1, d), lambda i: (0, 0))]
    args = (x, gain)
    if after is not None:
        in_specs.append(pl.BlockSpec(memory_space=pl.ANY))
        args += (after,)
    return pl.pallas_call(
        body, name=name, grid=(s // t,), in_specs=in_specs,
        out_specs=[_spec((t, d), lambda i: (i, 0)), _spec((d, t), lambda i: (0, i))],
        out_shape=[jax.ShapeDtypeStruct((s, d), BF16), jax.ShapeDtypeStruct((d, s), BF16)],
        compiler_params=_params(1),
    )(*args)


def _loss_head(x, gain, target):
    s, d = x.shape
    t = ROW_TILE
    n_steps = s // t

    def body(x_ref, g_ref, t_ref, dx_ref, dxb_ref, dg_ref, loss_ref, acc):
        i = pl.program_id(0)

        @pl.when(i == 0)
        def _():
            dg_ref[...] = jnp.zeros_like(dg_ref)
            acc[...] = jnp.zeros_like(acc)

        xv = x_ref[...]
        g = g_ref[...]
        r = lax.rsqrt(jnp.mean(xv * xv, axis=1, keepdims=True) + NORM_EPS)
        xhat = xv * r
        err = xhat * g - t_ref[...]
        acc[...] += jnp.sum(err * err, axis=0, keepdims=True)
        dy = err * (1.0 / d)
        dyg = dy * g
        proj = jnp.mean(dyg * xhat, axis=1, keepdims=True)
        dx = r * (dyg - xhat * proj)
        dx_ref[...] = dx
        dxb_ref[...] = dx.astype(BF16)
        dg_ref[...] += jnp.sum(dy * xhat, axis=0, keepdims=True)

        @pl.when(i == n_steps - 1)
        def _():
            total = jnp.sum(acc[...], axis=1, keepdims=True) * (0.5 / d)
            loss_ref[...] = jnp.broadcast_to(total, loss_ref.shape)

    return pl.pallas_call(
        body, name="loss_head", grid=(n_steps,),
        in_specs=[_spec((t, d), lambda i: (i, 0)), _spec((1, d), lambda i: (0, 0)),
                  _spec((t, d), lambda i: (i, 0))],
        out_specs=[_spec((t, d), lambda i: (i, 0)), _spec((t, d), lambda i: (i, 0)),
                   _spec((1, d), lambda i: (0, 0)), _spec((1, 128), lambda i: (0, 0))],
        out_shape=[jax.ShapeDtypeStruct((s, d), F32), jax.ShapeDtypeStruct((s, d), BF16),
                   jax.ShapeDtypeStruct((1, d), F32), jax.ShapeDtypeStruct((1, 128), F32)],
        scratch_shapes=[pltpu.VMEM((1, d), F32)],
        compiler_params=_params(1),
    )(x, gain, target)


def _rope_tables():
    half = HEAD_DIM // 2
    inv_freq = ROPE_THETA ** (-jnp.arange(half, dtype=F32) / half)
    ang = jnp.arange(SEQ, dtype=jnp.int32).astype(F32)[:, None] * inv_freq[None, :]
    cos, sin = jnp.cos(ang), jnp.sin(ang)
    cos_t = jnp.concatenate([cos, cos, cos, cos], axis=1)
    sin_t = jnp.concatenate([-sin, sin, -sin, sin], axis=1)
    return cos_t, sin_t


def _swap_halves(x):
    lane = lax.broadcasted_iota(jnp.int32, x.shape, 1)
    first = (lane % HEAD_DIM) < (HEAD_DIM // 2)
    return jnp.where(first, pltpu.roll(x, 128 - HEAD_DIM // 2, 1), pltpu.roll(x, HEAD_DIM // 2, 1))


def _permuted_specs(t, width):
    specs = [_spec((t, width), lambda i: (i, 0))]
    for d in DILATIONS[1:]:
        specs.append(_spec((d, t // d, width), lambda i: (0, i, 0)))
    return specs


def _permuted_shapes(width, dtype):
    shapes = [jax.ShapeDtypeStruct((SEQ, width), dtype)]
    for d in DILATIONS[1:]:
        shapes.append(jax.ShapeDtypeStruct((d, SEQ // d, width), dtype))
    return shapes


def _attn_prep(name, proj, cos_t, sin_t):
    t = ROW_TILE
    w = ATTN_WIDTH

    def body(q_ref, k_ref, v_ref, cos_ref, sin_ref, *rest):
        outs, scr = rest[:9], rest[9]
        cosv, sinv = cos_ref[...], sin_ref[...]
        for a, (src, roped, scale) in enumerate(((q_ref, True, HEAD_DIM ** -0.5),
                                                 (k_ref, True, 1.0), (v_ref, False, 1.0))):
            o1, o4, o16 = outs[3 * a:3 * a + 3]
            for cb in range(w // 128):
                cols = slice(cb * 128, (cb + 1) * 128)
                val = src[:, cols]
                if roped:
                    val = (val * cosv + _swap_halves(val) * sinv) * scale
                scr[...] = val
                o1[:, cols] = val.astype(BF16)
                for o_ref, d in ((o4, 4), (o16, 16)):
                    for r in range(d):
                        o_ref[r, :, cols] = scr[pl.ds(r, t // d, stride=d), :].astype(BF16)

    out_specs = _permuted_specs(t, w) * 3
    out_shape = _permuted_shapes(w, BF16) * 3
    outs = pl.pallas_call(
        body, name=name, grid=(SEQ // t,),
        in_specs=[_spec((t, w), lambda i: (i, 0)), _spec((t, w), lambda i: (i, 1)),
                  _spec((t, w), lambda i: (i, 2)),
                  _spec((t, 128), lambda i: (i, 0)), _spec((t, 128), lambda i: (i, 0))],
        out_specs=out_specs, out_shape=out_shape,
        scratch_shapes=[pltpu.VMEM((t, 128), F32)],
        compiler_params=_params(1),
    )(proj, proj, proj, cos_t, sin_t)
    q, k, v = outs[0:3], outs[3:6], outs[6:9]
    flat = lambda arr: arr.reshape(SEQ, w)
    return [(flat(q[p]), flat(k[p]), flat(v[p])) for p in range(3)]


def _band_masks():
    row = lax.broadcasted_iota(jnp.int32, (2 * SPAN, 2 * SPAN), 0) % SPAN
    col = lax.broadcasted_iota(jnp.int32, (2 * SPAN, 2 * SPAN), 1)
    is_prev = col < SPAN
    band = (is_prev & (col >= row)) | (~is_prev & (col - SPAN <= row))
    head0 = lax.broadcasted_iota(jnp.int32, (SPAN, 128), 1) < HEAD_DIM
    return band, is_prev, head0


def _stack_heads(x, head0):
    zero = jnp.zeros_like(x)
    return jnp.concatenate([jnp.where(head0, x, zero), jnp.where(head0, zero, x)], axis=0)


def _attn_fwd(name, q, k, v, seg_blocks):
    n_blocks = SEQ // SPAN

    def body(q_ref, k_ref, v_ref, o_ref, lse_ref):
        band, is_prev, head0 = _band_masks()

        def step(b, carry):
            cur = pl.ds(pl.multiple_of(b * SPAN, SPAN), SPAN)
            prev = pl.ds(pl.multiple_of(jnp.maximum(b - 1, 0) * SPAN, SPAN), SPAN)
            qs = _stack_heads(q_ref[cur, :], head0)
            kcat = jnp.concatenate([k_ref[prev, :], k_ref[cur, :]], axis=0)
            vcat = jnp.concatenate([v_ref[prev, :], v_ref[cur, :]], axis=0)
            ok = band & (((b % seg_blocks) != 0) | ~is_prev)
            s = jnp.where(ok, _dot_nt(qs, kcat), MASK_VALUE)
            m = jnp.max(s, axis=1, keepdims=True)
            p = jnp.exp(s - m)
            l = jnp.sum(p, axis=1, keepdims=True)
            pv = _dot(p, vcat) * (1.0 / l)
            lse = m + jnp.log(l)
            o_ref[cur, :] = jnp.where(head0, pv[:SPAN], pv[SPAN:])
            lse_ref[cur, :] = jnp.where(head0, lse[:SPAN], lse[SPAN:])
            return carry

        lax.fori_loop(0, n_blocks, step, 0, unroll=4)

    col = _spec((SEQ, 128), lambda j: (0, j))
    return pl.pallas_call(
        body, name=name, grid=(ATTN_WIDTH // 128,),
        in_specs=[col, col, col], out_specs=[col, col],
        out_shape=[jax.ShapeDtypeStruct((SEQ, ATTN_WIDTH), F32)] * 2,
        compiler_params=_params(1),
    )(q, k, v)


def _unpermute(dst, src_ref, d, cols):
    n = dst.shape[0] // d
    for r in range(d):
        dst[pl.ds(r, n, stride=d), :] = src_ref[r, :, cols]


def _attn_merge(name, outs, lses, gain):
    t = ROW_TILE
    w = ATTN_WIDTH

    def body(o1, o4, o16, l1, l4, l16, g_ref, an_ref, ant_ref, attn_ref, lse_ref, so4, so16, sl4, sl16):
        for cb in range(w // 128):
            cols = slice(cb * 128, (cb + 1) * 128)
            _unpermute(so4, o4, 4, cols)
            _unpermute(so16, o16, 16, cols)
            _unpermute(sl4, l4, 4, cols)
            _unpermute(sl16, l16, 16, cols)
            la, lb, lc = l1[:, cols], sl4[...], sl16[...]
            m = jnp.maximum(jnp.maximum(la, lb), lc)
            ea, eb, ec = jnp.exp(la - m), jnp.exp(lb - m), jnp.exp(lc - m)
            tot = ea + eb + ec
            attn_ref[:, cols] = (ea * o1[:, cols] + eb * so4[...] + ec * so16[...]) / tot
            lse_ref[:, cols] = m + jnp.log(tot)
        attn = attn_ref[...]
        r = lax.rsqrt(jnp.mean(attn * attn, axis=1, keepdims=True) + NORM_EPS)
        an = attn * r * g_ref[...]
        an_ref[...] = an.astype(BF16)
        ant_ref[...] = an.T.astype(BF16)

    views = lambda arrs: [arrs[0], arrs[1].reshape(4, SEQ // 4, w), arrs[2].reshape(16, SEQ // 16, w)]
    row = _spec((t, w), lambda i: (i, 0))
    return pl.pallas_call(
        body, name=name, grid=(SEQ // t,),
        in_specs=_permuted_specs(t, w) * 2 + [_spec((1, w), lambda i: (0, 0))],
        out_specs=[row, _spec((w, t), lambda i: (0, i)), row, row],
        out_shape=[jax.ShapeDtypeStruct((SEQ, 2 * w), BF16), jax.ShapeDtypeStruct((2 * w, SEQ), BF16),
                   jax.ShapeDtypeStruct((SEQ, w), F32), jax.ShapeDtypeStruct((SEQ, w), F32)],
        scratch_shapes=[pltpu.VMEM((t, 128), F32)] * 4,
        compiler_params=_params(1),
    )(*views(outs), *views(lses), gain)


def _head_sum_matrix():
    i = np.arange(ATTN_WIDTH)
    return jnp.asarray((i[:, None] // HEAD_DIM) == (i[None, :] // HEAD_DIM), dtype=F32)


def _attn_bwd_prep(name, d_an, attn, lse, gain, head_sum):
    t = ROW_TILE
    w = ATTN_WIDTH

    def body(dan_ref, attn_ref, lse_ref, g_ref, hs_ref, *rest):
        (do1, do4, do16, dl1, dl4, dl16, ls4, ls16, dg_ref), (sdo, sdl, sls) = rest[:9], rest[9:]

        @pl.when(pl.program_id(0) == 0)
        def _():
            dg_ref[...] = jnp.zeros_like(dg_ref)

        attn = attn_ref[...]
        dan = dan_ref[...]
        r = lax.rsqrt(jnp.mean(attn * attn, axis=1, keepdims=True) + NORM_EPS)
        xhat = attn * r
        dg_ref[...] += jnp.sum(dan * xhat, axis=0, keepdims=True)
        dang = dan * g_ref[...]
        d_o = r * (dang - xhat * jnp.mean(dang * xhat, axis=1, keepdims=True))
        delta = jnp.dot(d_o * attn, hs_ref[...], preferred_element_type=F32,
                        precision=lax.Precision.HIGHEST)
        do1[...] = d_o.astype(BF16)
        dl1[...] = delta
        for cb in range(w // 128):
            cols = slice(cb * 128, (cb + 1) * 128)
            sdo[...] = d_o[:, cols]
            sdl[...] = delta[:, cols]
            sls[...] = lse_ref[:, cols]
            for d, o_do, o_dl, o_ls in ((4, do4, dl4, ls4), (16, do16, dl16, ls16)):
                for rr in range(d):
                    rows = pl.ds(rr, t // d, stride=d)
                    o_do[rr, :, cols] = sdo[rows, :].astype(BF16)
                    o_dl[rr, :, cols] = sdl[rows, :]
                    o_ls[rr, :, cols] = sls[rows, :]

    row = _spec((t, w), lambda i: (i, 0))
    perm = _permuted_specs(t, w)
    outs = pl.pallas_call(
        body, name=name, grid=(SEQ // t,),
        in_specs=[row, row, row, _spec((1, w), lambda i: (0, 0)), _spec((w, w), lambda i: (0, 0))],
        out_specs=perm + perm + perm[1:] + [_spec((1, w), lambda i: (0, 0))],
        out_shape=(_permuted_shapes(w, BF16) + _permuted_shapes(w, F32) + _permuted_shapes(w, F32)[1:]
                   + [jax.ShapeDtypeStruct((1, w), F32)]),
        scratch_shapes=[pltpu.VMEM((t, 128), F32)] * 3,
        compiler_params=_params(1),
    )(d_an, attn, lse, gain, head_sum)
    flat = lambda arr: arr.reshape(SEQ, w)
    d_out = [flat(a) for a in outs[0:3]]
    delta = [flat(a) for a in outs[3:6]]
    lses = [lse, flat(outs[6]), flat(outs[7])]
    return d_out, delta, lses, outs[8]


def _attn_bwd(name, q, k, v, d_out, delta, lse, seg_blocks):
    n_blocks = SEQ // SPAN

    def body(q_ref, k_ref, v_ref, do_ref, dl_ref, lse_ref, dq_ref, dk_ref, dv_ref):
        band, is_prev, head0 = _band_masks()
        dk_ref[...] = jnp.zeros_like(dk_ref)
        dv_ref[...] = jnp.zeros_like(dv_ref)

        def per_head(x):
            return jnp.concatenate([x[:, 0:1], x[:, HEAD_DIM:HEAD_DIM + 1]], axis=0)

        def step(b, carry):
            cur = pl.ds(pl.multiple_of(b * SPAN, SPAN), SPAN)
            prev = pl.ds(pl.multiple_of(jnp.maximum(b - 1, 0) * SPAN, SPAN), SPAN)
            qs = _stack_heads(q_ref[cur, :], head0)
            dos = _stack_heads(do_ref[cur, :], head0)
            kcat = jnp.concatenate([k_ref[prev, :], k_ref[cur, :]], axis=0)
            vcat = jnp.concatenate([v_ref[prev, :], v_ref[cur, :]], axis=0)
            ok = band & (((b % seg_blocks) != 0) | ~is_prev)
            p = jnp.where(ok, jnp.exp(_dot_nt(qs, kcat) - per_head(lse_ref[cur, :])), 0.0)
            ds = p * (_dot_nt(dos, vcat) - per_head(dl_ref[cur, :]))
            dq = _dot(ds, kcat)
            dq_ref[cur, :] = jnp.where(head0, dq[:SPAN], dq[SPAN:])
            dk = _dot_tn(ds, qs)
            dv = _dot_tn(p, dos)
            dk_ref[prev, :] += dk[:SPAN]
            dv_ref[prev, :] += dv[:SPAN]
            dk_ref[cur, :] += dk[SPAN:]
            dv_ref[cur, :] += dv[SPAN:]
            return carry

        lax.fori_loop(0, n_blocks, step, 0, unroll=4)

    col = _spec((SEQ, 128), lambda j: (0, j))
    return pl.pallas_call(
        body, name=name, grid=(ATTN_WIDTH // 128,),
        in_specs=[col] * 6, out_specs=[col] * 3,
        out_shape=[jax.ShapeDtypeStruct((SEQ, ATTN_WIDTH), F32)] * 3,
        compiler_params=_params(1),
    )(q, k, v, d_out, delta, lse)


def _attn_bwd_post(name, grads, cos_t, sin_t):
    t = ROW_TILE
    w = ATTN_WIDTH

    def body(*refs):
        ins, cos_ref, sin_ref, out_ref, s4, s16 = refs[:9], refs[9], refs[10], refs[11], refs[12], refs[13]
        cosv, sinv = cos_ref[...], sin_ref[...]
        for a in range(3):
            g1, g4, g16 = ins[a], ins[3 + a], ins[6 + a]
            for cb in range(w // 128):
                cols = slice(cb * 128, (cb + 1) * 128)
                _unpermute(s4, g4, 4, cols)
                _unpermute(s16, g16, 16, cols)
                val = g1[:, cols] + s4[...] + s16[...]
                if a < 2:
                    val = val * cosv + _swap_halves(val * sinv)
                if a == 0:
                    val = val * (HEAD_DIM ** -0.5)
                out_ref[:, a * w + cb * 128:a * w + (cb + 1) * 128] = val.astype(BF16)

    views = []
    for p, d in enumerate(DILATIONS):
        for a in range(3):
            views.append(grads[p][a] if d == 1 else grads[p][a].reshape(d, SEQ // d, w))
    perm = _permuted_specs(t, w)
    in_specs = [perm[0]] * 3 + [perm[1]] * 3 + [perm[2]] * 3
    return pl.pallas_call(
        body, name=name, grid=(SEQ // t,),
        in_specs=in_specs + [_spec((t, 128), lambda i: (i, 0))] * 2,
        out_specs=_spec((t, 3 * w), lambda i: (i, 0)),
        out_shape=jax.ShapeDtypeStruct((SEQ, 3 * w), BF16),
        scratch_shapes=[pltpu.VMEM((t, 128), F32)] * 2,
        compiler_params=_params(1),
    )(*views, cos_t, sin_t)


N_LEVELS = 7
HGRN_PAIR = 2


def _hgrn_consts():
    c = CHUNK
    i = np.arange(c)[:, None]
    s = np.arange(c)[None, :]
    blocks = [s <= i]
    for lv in range(N_LEVELS):
        bs = c >> lv
        h = bs // 2
        m = (i // bs) * bs + h - 1
        second = (i % bs) >= h
        blocks.append((second & (s > m) & (s <= i)) | (~second & (s > i) & (s <= m)))
    blocks.append(s > i)
    stack = np.concatenate(blocks, axis=0).astype(np.float32)
    return jnp.asarray(stack, dtype=BF16), jnp.asarray(stack.T, dtype=BF16)


def _exact_dot(m01, x):
    hi = x.astype(BF16)
    lo = (x - hi.astype(F32)).astype(BF16)
    n = x.shape[1]
    full = jnp.dot(m01, jnp.concatenate([hi, lo], axis=1), preferred_element_type=F32)
    return full[:, :n] + full[:, n:]


def _hgrn_gates(qh, z, lb):
    sq = _sigmoid(qh)
    q = qh * sq * (HGRN_DIM ** -0.5)
    sig = _sigmoid(z)
    sigm = _sigmoid(-z)
    f = lb + (1.0 - lb) * sig
    k = (1.0 - lb) * sigm
    return q, k, f, sq, sig, sigm


def _level_masks(lv):
    row = lax.broadcasted_iota(jnp.int32, (CHUNK, CHUNK), 0)
    col = lax.broadcasted_iota(jnp.int32, (CHUNK, CHUNK), 1)
    shift = N_LEVELS - lv
    second = (row & (CHUNK >> (lv + 1))) != 0
    same = (row >> shift) == (col >> shift)
    return second, same


def _hgrn_fwd(name, proj, lb, gain, stack, mixed, mixed_t):
    t = ROW_TILE
    per = t // CHUNK
    n_rb = SEQ // t
    n_chunks = SEQ // CHUNK
    col0 = 3 * ATTN_WIDTH // 128
    pair_w = HGRN_PAIR * HGRN_DIM

    def body(q_ref, f_ref, i_ref, g_ref, lb_ref, gain_ref, stack_ref, mixed_in, mixed_t_in,
             rec_ref, rect_ref, o_ref, st_out, a_out, st):
        del mixed_in, mixed_t_in

        @pl.when(pl.program_id(1) == 0)
        def _():
            st[...] = jnp.zeros_like(st)

        row = lax.broadcasted_iota(jnp.int32, (CHUNK, CHUNK), 0)
        col = lax.broadcasted_iota(jnp.int32, (CHUNK, CHUNK), 1)
        for c, hh in [(c, hh) for c in range(per) for hh in range(HGRN_PAIR)]:
            rows = slice(c * CHUNK, (c + 1) * CHUNK)
            lanes = slice(hh * HGRN_DIM, (hh + 1) * HGRN_DIM)
            lbv = lb_ref[hh]
            qh, z, v, gh = q_ref[rows, lanes], f_ref[rows, lanes], i_ref[rows, lanes], g_ref[rows, lanes]
            q, k, f, _, _, _ = _hgrn_gates(qh, z, lbv)
            dec = _exact_dot(stack_ref[...], jnp.log(f))
            g = dec[0:CHUNK]
            to_end = dec[(N_LEVELS + 1) * CHUNK:(N_LEVELS + 2) * CHUNK]
            a = jnp.where(row == col, jnp.sum(q * k, axis=1, keepdims=True), 0.0)
            for lv in range(N_LEVELS):
                e = jnp.exp(dec[(lv + 1) * CHUNK:(lv + 2) * CHUNK])
                second, same = _level_masks(lv)
                qt = jnp.where(second, q * e, 0.0)
                kt = jnp.where(second, 0.0, k * e)
                a = a + jnp.where(same, _dot_nt(qt, kt), 0.0)
            st_prev = st[hh]
            st_out[hh, c] = st_prev
            a_out[hh, c] = a
            o = _dot(a, v) + _dot_nt(q * jnp.exp(g), st_prev)
            k_end = k * jnp.exp(to_end)
            st[hh] = st_prev * jnp.exp(g[CHUNK - 1:CHUNK, :]) + _dot(v.T, k_end)
            o_ref[rows, lanes] = o
            r = lax.rsqrt(jnp.mean(o * o, axis=1, keepdims=True) + NORM_EPS)
            rec = o * r * gain_ref[...] * (gh * _sigmoid(gh))
            rec_ref[rows, lanes] = rec.astype(BF16)
            rect_ref[lanes, rows] = rec.T.astype(BF16)

    def col_spec(tt):
        return _spec((t, pair_w), lambda h, rb: (rb, (col0 + HGRN_HEADS * tt) // HGRN_PAIR + h))

    chunk_spec = _spec((HGRN_PAIR, per, CHUNK, CHUNK), lambda h, rb: (h, rb, 0, 0))
    return pl.pallas_call(
        body, name=name, grid=(HGRN_HEADS // HGRN_PAIR, n_rb),
        in_specs=[col_spec(0), col_spec(1), col_spec(2), col_spec(3),
                  _spec((HGRN_PAIR, 1, HGRN_DIM), lambda h, rb: (h, 0, 0)),
                  _spec((1, HGRN_DIM), lambda h, rb: (0, 0)),
                  _spec(stack.shape, lambda h, rb: (0, 0)), ANY_SPEC, ANY_SPEC],
        out_specs=[_spec((t, pair_w), lambda h, rb: (rb, ATTN_WIDTH // pair_w + h)),
                   _spec((pair_w, t), lambda h, rb: (ATTN_WIDTH // pair_w + h, rb)),
                   _spec((t, pair_w), lambda h, rb: (rb, h)),
                   chunk_spec, chunk_spec],
        out_shape=[jax.ShapeDtypeStruct(mixed.shape, BF16),
                   jax.ShapeDtypeStruct(mixed_t.shape, BF16),
                   jax.ShapeDtypeStruct((SEQ, HGRN_WIDTH), F32),
                   jax.ShapeDtypeStruct((HGRN_HEADS, n_chunks, CHUNK, CHUNK), F32),
                   jax.ShapeDtypeStruct((HGRN_HEADS, n_chunks, CHUNK, CHUNK), F32)],
        scratch_shapes=[pltpu.VMEM((HGRN_PAIR, CHUNK, CHUNK), F32)],
        input_output_aliases={7: 0, 8: 1},
        compiler_params=_params(2),
    )(proj, proj, proj, proj, lb, gain, stack, mixed, mixed_t)


def _hgrn_bwd(name, proj, d_rec, o_pre, states, scores, lb, gain, stack, stack_t):
    t = ROW_TILE
    per = t // CHUNK
    n_rb = SEQ // t
    col0 = 3 * ATTN_WIDTH // 128
    pair_w = HGRN_PAIR * HGRN_DIM

    def body(q_ref, f_ref, i_ref, g_ref, drec_ref, o_ref, st_ref, a_ref, lb_ref, gain_ref,
             stack_ref, stack_t_ref, dq_ref, df_ref, di_ref, dg_ref, dlb_ref, dgain_ref, dst):
        @pl.when(pl.program_id(1) == 0)
        def _():
            dst[...] = jnp.zeros_like(dst)
            dlb_ref[...] = jnp.zeros_like(dlb_ref)
            dgain_ref[...] = jnp.zeros_like(dgain_ref)

        gain_v = gain_ref[...]
        row = lax.broadcasted_iota(jnp.int32, (CHUNK, CHUNK), 0)
        col = lax.broadcasted_iota(jnp.int32, (CHUNK, CHUNK), 1)
        for c, hh in [(c, hh) for c in reversed(range(per)) for hh in range(HGRN_PAIR)]:
            rows = slice(c * CHUNK, (c + 1) * CHUNK)
            lanes = slice(hh * HGRN_DIM, (hh + 1) * HGRN_DIM)
            lbv = lb_ref[hh]
            qh, z, v, gh = q_ref[rows, lanes], f_ref[rows, lanes], i_ref[rows, lanes], g_ref[rows, lanes]
            q, k, f, sq, sig, sigm = _hgrn_gates(qh, z, lbv)
            dec = _exact_dot(stack_ref[...], jnp.log(f))
            g = dec[0:CHUNK]
            to_end = dec[(N_LEVELS + 1) * CHUNK:(N_LEVELS + 2) * CHUNK]
            e_g = jnp.exp(g)
            e_end = jnp.exp(to_end)
            e_last = jnp.exp(g[CHUNK - 1:CHUNK, :])
            q_in = q * e_g
            k_end = k * e_end
            st_prev = st_ref[hh, c]
            a = a_ref[hh, c]
            dst_new = dst[hh]

            o = o_ref[rows, lanes]
            drec = drec_ref[rows, lanes]
            sg = _sigmoid(gh)
            r = lax.rsqrt(jnp.mean(o * o, axis=1, keepdims=True) + NORM_EPS)
            ohat = o * r
            d_gh = drec * (ohat * gain_v) * (sg * (1.0 + gh * (1.0 - sg)))
            d_on = drec * (gh * sg)
            dgain_ref[hh] += jnp.sum(d_on * ohat, axis=0, keepdims=True)
            d_ohat = d_on * gain_v
            d_o = r * (d_ohat - ohat * jnp.mean(d_ohat * ohat, axis=1, keepdims=True))

            d_a = jnp.where(row >= col, _dot_nt(d_o, v), 0.0)
            d_at = jnp.where(col >= row, _dot_nt(v, d_o), 0.0)
            d_v = _dot(a.T, d_o) + _dot_nt(k_end, dst_new)
            d_q_in = _dot(d_o, st_prev)
            d_k_end = _dot(v, dst_new)
            d_q = d_q_in * e_g
            d_k = d_k_end * e_end
            diag = jnp.sum(d_o * v, axis=1, keepdims=True)
            d_q = d_q + diag * k
            d_k = d_k + diag * q
            d_dec = [q_in * d_q_in]
            for lv in range(N_LEVELS):
                e = jnp.exp(dec[(lv + 1) * CHUNK:(lv + 2) * CHUNK])
                second, same = _level_masks(lv)
                qt = jnp.where(second, q * e, 0.0)
                kt = jnp.where(second, 0.0, k * e)
                d_qt = _dot(jnp.where(same, d_a, 0.0), kt)
                d_kt = _dot(jnp.where(same, d_at, 0.0), qt)
                d_q = d_q + jnp.where(second, d_qt * e, 0.0)
                d_k = d_k + jnp.where(second, 0.0, d_kt * e)
                d_dec.append(jnp.where(second, qt * d_qt, kt * d_kt))
            d_dec.append(k_end * d_k_end)
            flux = jnp.sum(dst_new * st_prev, axis=0, keepdims=True) * e_last
            d_lf = _exact_dot(stack_t_ref[...], jnp.concatenate(d_dec, axis=0)) + flux
            dst[hh] = dst_new * e_last + _dot(d_o.T, q_in)

            d_f = d_lf / f - d_k
            dlb_ref[hh] += jnp.sum(d_f * sigm, axis=0, keepdims=True)
            dq_ref[rows, lanes] = (d_q * (HGRN_DIM ** -0.5) * (sq * (1.0 + qh * (1.0 - sq)))).astype(BF16)
            df_ref[rows, lanes] = (d_f * (1.0 - lbv) * sig * sigm).astype(BF16)
            di_ref[rows, lanes] = d_v.astype(BF16)
            dg_ref[rows, lanes] = d_gh.astype(BF16)

    last = n_rb - 1

    def col_spec(tt):
        return _spec((t, pair_w), lambda h, rb: (last - rb, (col0 + HGRN_HEADS * tt) // HGRN_PAIR + h))

    head_col = _spec((t, pair_w), lambda h, rb: (last - rb, h))
    rec_col0 = (d_rec.shape[1] - HGRN_WIDTH) // pair_w
    d_rec_col = _spec((t, pair_w), lambda h, rb: (last - rb, rec_col0 + h))
    chunk_spec = _spec((HGRN_PAIR, per, CHUNK, CHUNK), lambda h, rb: (h, last - rb, 0, 0))
    vec_spec = _spec((HGRN_PAIR, 1, HGRN_DIM), lambda h, rb: (h, 0, 0))
    outs = pl.pallas_call(
        body, name=name, grid=(HGRN_HEADS // HGRN_PAIR, n_rb),
        in_specs=[col_spec(0), col_spec(1), col_spec(2), col_spec(3), d_rec_col, head_col,
                  chunk_spec, chunk_spec, vec_spec,
                  _spec((1, HGRN_DIM), lambda h, rb: (0, 0)),
                  _spec(stack.shape, lambda h, rb: (0, 0)), _spec(stack_t.shape, lambda h, rb: (0, 0))],
        out_specs=[head_col] * 4 + [vec_spec, vec_spec],
        out_shape=[jax.ShapeDtypeStruct((SEQ, HGRN_WIDTH), BF16)] * 4
                  + [jax.ShapeDtypeStruct((HGRN_HEADS, 1, HGRN_DIM), F32)] * 2,
        scratch_shapes=[pltpu.VMEM((HGRN_PAIR, CHUNK, CHUNK), F32)],
        compiler_params=_params(2),
    )(proj, proj, proj, proj, d_rec, o_pre, states, scores, lb, gain, stack, stack_t)
    return outs


ANY_SPEC = pl.BlockSpec(memory_space=pl.ANY)


def _my_place():
    return lax.axis_index("x"), lax.axis_index("y"), lax.axis_index("c")


def _other_chips(x, y):
    return [(1 - x, y), (x, 1 - y), (1 - x, 1 - y)]


def _remote(src, dst, send_sem, recv_sem, device):
    return pltpu.make_async_remote_copy(src_ref=src, dst_ref=dst, send_sem=send_sem, recv_sem=recv_sem,
                                        device_id=device, device_id_type=MESH)


def _staged_copies(srcs, dsts, stage, sems):
    loads = [pltpu.make_async_copy(srcs[i], stage[i], sems.at[i]) for i in range(len(srcs))]
    for cp in loads:
        cp.start()
    stores = []
    for i, cp in enumerate(loads):
        cp.wait()
        stores.append(pltpu.make_async_copy(stage[i], dsts[i], sems.at[i]))
        stores[-1].start()
    return stores


def _gather_weights(name, shards):
    n = len(shards)

    def body(*refs):
        ins, outs = refs[:n], refs[n:2 * n]
        ici_send, ici_recv, d2d_send, d2d_recv, local_sems = refs[2 * n:2 * n + 5]
        stage = refs[2 * n + 5:]
        x, y, c = _my_place()
        me = 2 * x + y
        chips = _other_chips(x, y)

        def half(i, which):
            h = ins[i].shape[0] // 2
            return pl.ds(which * h, h)

        sends = []
        for i in range(n):
            for j, (px, py) in enumerate(chips):
                sends.append(_remote(ins[i].at[half(i, c), :], outs[i].at[me, half(i, c), :],
                                     ici_send.at[3 * i + j], ici_recv.at[3 * i + j], (px, py, c)))
        for cp in sends:
            cp.start()
        local = _staged_copies(ins, [outs[i].at[me] for i in range(n)], stage, local_sems)
        for i in range(n):
            for j, (px, py) in enumerate(chips):
                landed = outs[i].at[2 * px + py, half(i, c), :]
                _remote(landed, landed, ici_send.at[3 * i + j], ici_recv.at[3 * i + j], (px, py, c)).wait_recv()
                forward = _remote(landed, landed, d2d_send.at[3 * i + j], d2d_recv.at[3 * i + j], (x, y, 1 - c))
                forward.start()
                sends.append(forward)
        for i in range(n):
            for j, (px, py) in enumerate(chips):
                other = outs[i].at[2 * px + py, half(i, 1 - c), :]
                _remote(other, other, d2d_send.at[3 * i + j], d2d_recv.at[3 * i + j], (x, y, 1 - c)).wait_recv()
        for cp in sends:
            cp.wait_send()
        for cp in local:
            cp.wait()

    return pl.pallas_call(
        body, name=name, in_specs=[ANY_SPEC] * n, out_specs=[ANY_SPEC] * n,
        out_shape=[jax.ShapeDtypeStruct((N_CHIPS,) + s.shape, s.dtype) for s in shards],
        scratch_shapes=([pltpu.SemaphoreType.DMA((3 * n,))] * 4 + [pltpu.SemaphoreType.DMA((n,))]
                        + [pltpu.VMEM(s.shape, s.dtype) for s in shards]),
        compiler_params=pltpu.CompilerParams(vmem_limit_bytes=VMEM_LIMIT),
    )(*shards)


HBM_SPEC = pl.BlockSpec(memory_space=pltpu.HBM)
SEM_SPEC = pl.BlockSpec(memory_space=pltpu.SEMAPHORE)
SPLIT_PARAMS = pltpu.CompilerParams(has_side_effects=pltpu.SideEffectType.DATAFLOW_SIDE_EFFECTING)
N_PEERS = {"gather": N_CHIPS - 1, "scatter": N_DEV - 1}


def _split_copies(ins, lands, send_sems, recv_sems, kind):
    x, y, c = _my_place()
    pairs = []
    for i in range(len(ins)):
        if kind == "gather":
            me = 2 * x + y
            for j, (px, py) in enumerate(_other_chips(x, y)):
                sems = (send_sems.at[3 * i + j], recv_sems.at[3 * i + j], (px, py, c))
                pairs.append((_remote(ins[i], lands[i].at[me], *sems),
                              _remote(ins[i], lands[i].at[2 * px + py], *sems)))
        else:
            me = 4 * x + 2 * y + c
            h = ins[i].shape[1] // 2
            for k in range(1, N_DEV):
                px, py, pc = (x + (k >> 2)) % 2, (y + ((k >> 1) & 1)) % 2, (c + (k & 1)) % 2
                src = ins[i].at[2 * px + py, pl.ds(pc * h, h), :]
                sems = (send_sems.at[7 * i + k - 1], recv_sems.at[7 * i + k - 1], (px, py, pc))
                pairs.append((_remote(src, lands[i].at[me], *sems),
                              _remote(src, lands[i].at[4 * px + 2 * py + pc], *sems)))
    return pairs


def _exchange_start(name, srcs, lands, kind, after):
    n = len(srcs)
    n_sems = N_PEERS[kind] * n

    def body(*refs):
        ins, land_refs = refs[:n], refs[n:2 * n]
        send_sems, recv_sems = refs[2 * n + 1:2 * n + 3]
        token = refs[-1]
        for send, _ in _split_copies(ins, land_refs, send_sems, recv_sems, kind):
            send.start()
        token[...] = jnp.zeros_like(token)

    arrays = list(srcs) + list(lands)
    outs = pl.pallas_call(
        body, name=name,
        in_specs=[HBM_SPEC] * (2 * n) + [ANY_SPEC],
        out_shape=([pltpu.SemaphoreType.DMA((n_sems,))] * 2 + [pltpu.HBM(a.shape, a.dtype) for a in arrays]
                   + [jax.ShapeDtypeStruct((8, 128), F32)]),
        out_specs=[SEM_SPEC] * 2 + [HBM_SPEC] * (2 * n) + [pl.BlockSpec(memory_space=pltpu.VMEM)],
        input_output_aliases={i: 2 + i for i in range(2 * n)},
        compiler_params=SPLIT_PARAMS,
    )(*[pltpu.with_memory_space_constraint(a, pltpu.HBM) for a in arrays], after)
    return outs[:2], outs[2:2 + 2 * n], outs[-1]


def _exchange_wait(name, sems, passed, kind, after):
    n = len(passed) // 2

    def body(*refs):
        ins, land_refs = refs[:n], refs[n:2 * n]
        send_sems, recv_sems = refs[2 * n:2 * n + 2]
        for send, arrive in _split_copies(ins, land_refs, send_sems, recv_sems, kind):
            send.wait_send()
            arrive.wait_recv()

    outs = pl.pallas_call(
        body, name=name,
        in_specs=[HBM_SPEC] * (2 * n) + [SEM_SPEC] * 2 + [ANY_SPEC],
        out_shape=[pltpu.HBM(a.shape, a.dtype) for a in passed],
        out_specs=[HBM_SPEC] * (2 * n),
        input_output_aliases={i: i for i in range(2 * n)},
        compiler_params=SPLIT_PARAMS,
    )(*passed, *sems, after)
    return outs[:n], outs[n:]


def _own_slot(name, own, me):
    r, cc = own.shape
    th = min(r, 512)

    def body(me_ref, x_ref, o_ref):
        del me_ref
        o_ref[...] = x_ref[...]

    grid_spec = pltpu.PrefetchScalarGridSpec(
        num_scalar_prefetch=1, grid=(r // th,),
        in_specs=[pl.BlockSpec((th, cc), lambda i, me_ref: (i, 0))],
        out_specs=pl.BlockSpec((None, th, cc), lambda i, me_ref: (me_ref[0], i, 0)))
    return pl.pallas_call(
        body, name=name, grid_spec=grid_spec,
        out_shape=jax.ShapeDtypeStruct((N_CHIPS, r, cc), own.dtype), compiler_params=_params(1),
    )(me, own)


def _sum_devices(name, landed, own, place):
    n_dev, h, cc = landed.shape
    th = min(h, 256)
    nb = h // th

    def body(place_ref, l_ref, own_ref, o_ref):
        total = None
        for d in range(n_dev):
            piece = jnp.where(place_ref[0] == d, own_ref[...], l_ref[d]).astype(F32)
            total = piece if total is None else total + piece
        o_ref[...] = total

    grid_spec = pltpu.PrefetchScalarGridSpec(
        num_scalar_prefetch=1, grid=(nb,),
        in_specs=[pl.BlockSpec((n_dev, th, cc), lambda i, p: (0, i, 0)),
                  pl.BlockSpec((None, th, cc), lambda i, p: (p[1], p[2] * nb + i, 0))],
        out_specs=pl.BlockSpec((th, cc), lambda i, p: (i, 0)))
    return pl.pallas_call(
        body, name=name, grid_spec=grid_spec,
        out_shape=jax.ShapeDtypeStruct((h, cc), F32), compiler_params=_params(1),
    )(place, landed, own)


def _share_halves(name, halves):
    flat = [t for per_weight in halves for t in per_weight]
    n = len(flat)
    n_w = len(halves)

    def body(*refs):
        ins, outs = refs[:n], refs[n:n + n_w]
        send_sems, recv_sems, local_sems = refs[n + n_w:n + n_w + 3]
        stage = refs[n + n_w + 3:]
        x, y, c = _my_place()
        sends, own = [], []
        for i in range(n):
            w, l = divmod(i, DEPTH)
            h = ins[i].shape[0]
            own.append(outs[w].at[l, pl.ds(c * h, h), :])
            sends.append(_remote(ins[i], own[i], send_sems.at[i], recv_sems.at[i], (x, y, 1 - c)))
        for cp in sends:
            cp.start()
        local = _staged_copies(ins, own, stage, local_sems)
        for i in range(n):
            w, l = divmod(i, DEPTH)
            h = ins[i].shape[0]
            _remote(ins[i], outs[w].at[l, pl.ds((1 - c) * h, h), :], send_sems.at[i], recv_sems.at[i],
                    (x, y, 1 - c)).wait_recv()
        for cp in sends:
            cp.wait_send()
        for cp in local:
            cp.wait()

    return pl.pallas_call(
        body, name=name, in_specs=[ANY_SPEC] * n, out_specs=[ANY_SPEC] * n_w,
        out_shape=[jax.ShapeDtypeStruct((DEPTH, 2 * per_weight[0].shape[0], per_weight[0].shape[1]), F32)
                   for per_weight in halves],
        scratch_shapes=([pltpu.SemaphoreType.DMA((n,))] * 3 + [pltpu.VMEM(t.shape, t.dtype) for t in flat]),
        compiler_params=pltpu.CompilerParams(vmem_limit_bytes=VMEM_LIMIT),
    )(*flat)


def _all_reduce_small(pack):
    def body(p_ref, o_ref, recv, send_sems, recv_sems):
        x, y, c = _my_place()
        me = 4 * x + 2 * y + c
        recv[me] = p_ref[...]
        peers = []
        for k in range(1, N_DEV):
            px, py, pc = (x + (k >> 2)) % 2, (y + ((k >> 1) & 1)) % 2, (c + (k & 1)) % 2
            peers.append((px, py, pc))
        sends = [_remote(p_ref, recv.at[me], send_sems.at[k], recv_sems.at[k], peer)
                 for k, peer in enumerate(peers)]
        for cp in sends:
            cp.start()
        for k, (px, py, pc) in enumerate(peers):
            _remote(p_ref, recv.at[4 * px + 2 * py + pc], send_sems.at[k], recv_sems.at[k],
                    (px, py, pc)).wait_recv()
        for cp in sends:
            cp.wait_send()
        total = recv[0]
        for d in range(1, N_DEV):
            total = total + recv[d]
        o_ref[...] = total

    vmem = pl.BlockSpec(memory_space=pltpu.VMEM)
    return pl.pallas_call(
        body, name="all_reduce_small", in_specs=[vmem], out_specs=vmem,
        out_shape=jax.ShapeDtypeStruct(pack.shape, F32),
        scratch_shapes=[pltpu.VMEM((N_DEV,) + pack.shape, F32),
                        pltpu.SemaphoreType.DMA((N_DEV - 1,)), pltpu.SemaphoreType.DMA((N_DEV - 1,))],
    )(pack)


def _adamw(name, w, g, m, v):
    r, cc = w.shape
    th = min(r, 256)

    def body(w_ref, g_ref, m_ref, v_ref, d_ref, m_out, v_out):
        gv = g_ref[...]
        m2 = ADAM_B1 * m_ref[...] + (1.0 - ADAM_B1) * gv
        v2 = ADAM_B2 * v_ref[...] + (1.0 - ADAM_B2) * (gv * gv)
        m_hat = m2 / (1.0 - ADAM_B1 ** ADAM_STEP)
        v_hat = v2 / (1.0 - ADAM_B2 ** ADAM_STEP)
        d_ref[...] = -ADAM_LR * (m_hat / (jnp.sqrt(v_hat) + ADAM_EPS) + ADAM_WD * w_ref[...])
        m_out[...] = m2
        v_out[...] = v2

    tile = _spec((th, cc), lambda i: (i, 0))
    return pl.pallas_call(
        body, name=name, grid=(r // th,), in_specs=[tile] * 4, out_specs=[tile] * 3,
        out_shape=[jax.ShapeDtypeStruct((r, cc), F32)] * 3, compiler_params=_params(1),
    )(w, g, m, v)


def _lower_bounds(lb_logits):
    p = jax.nn.softmax(lb_logits.astype(F32), axis=0)
    return jnp.cumsum(p, axis=0) - p[0]


def _layer_forward(l, x_in, small, weights, consts, after=None):
    win, rest = weights
    cos_t, sin_t, stack, _, _ = consts
    tm = MM_TILE
    saved = {"x_in": x_in}

    h, h_t = _rms_fwd(f"norm_mix{l}", x_in, small["norm_mix"][l][None, :], after=after)
    proj = _mm_pieces(f"proj{l}", h, win, False, tm)
    saved.update(h_t=h_t, proj=proj)

    qkv = _attn_prep(f"attn_prep{l}", proj, cos_t, sin_t)
    outs, lses = [], []
    for p, d in enumerate(DILATIONS):
        o, lse = _attn_fwd(f"attn_fwd{l}_{d}", *qkv[p], SEQ // d // SPAN)
        outs.append(o)
        lses.append(lse)
    mixed, mixed_t, attn, lse = _attn_merge(f"attn_merge{l}", outs, lses, small["attn_out_gain"][l][None, :])
    saved.update(qkv=qkv, attn=attn, lse=lse)

    lb3 = small["lower"][l].reshape(HGRN_HEADS, 1, HGRN_DIM)
    mixed, mixed_t, o_pre, states, scores = _hgrn_fwd(f"hgrn_fwd{l}", proj, lb3, small["hgrn_out_gain"][l][None, :],
                                                      stack, mixed, mixed_t)
    wo, wu, wd = rest(mixed)
    saved.update(mixed_t=mixed_t, o_pre=o_pre, states=states, scores=scores, lb3=lb3, weights=(win, wo, wu, wd))

    x_mid = _mm_accum(f"out_proj{l}", mixed, wo, False, tm, x_in)
    saved["x_mid"] = x_mid

    h2, h2_t = _rms_fwd(f"norm_mlp{l}", x_mid, small["norm_mlp"][l][None, :])
    a, relu_u, a_t = _mm_pieces(f"up{l}", h2, wu, False, tm, epilogue="relu2")
    x_out = _mm_accum(f"down{l}", a, wd, False, tm, x_mid)
    saved.update(h2_t=h2_t, relu_u=relu_u, a_t=a_t)
    return x_out, saved


def _layer_backward(l, dx, saved, small, consts, on_grads, after=None):
    win, wo, wu, wd = saved["weights"]
    cos_t, sin_t, stack, stack_t, head_sum = consts
    tm = MM_TILE

    dx, dx_b = dx
    du = _mm_pieces(f"d_u{l}", dx_b, wd, True, tm, epilogue="relu2_grad", extra=saved["relu_u"], after=after)
    d_wd = _mm_dw(f"d_wdown{l}", saved["a_t"], dx_b, False, tm)
    dxm, dxm_b, dg_mlp = _mm_accum(f"d_h2_{l}", du, wu, True, tm, dx,
                                   norm=(saved["x_mid"], small["norm_mlp"][l][None, :]))
    d_wu = _mm_dw(f"d_wup{l}", saved["h2_t"], du, True, tm)
    after_mlp = on_grads(l, "mlp", (d_wu, d_wd))

    d_mixed = _mm_pieces(f"d_mixed{l}", dxm_b, wo, True, tm, after=after_mlp)
    d_wo = _mm_dw(f"d_wout{l}", saved["mixed_t"], dxm_b, False, tm)
    d_rec = d_mixed

    d_out, delta, lses, dg_attn = _attn_bwd_prep(f"attn_bwd_prep{l}", d_mixed, saved["attn"], saved["lse"],
                                                 small["attn_out_gain"][l][None, :], head_sum)
    grads = []
    for p, d in enumerate(DILATIONS):
        grads.append(_attn_bwd(f"attn_bwd{l}_{d}", *saved["qkv"][p], d_out[p], delta[p], lses[p],
                               SEQ // d // SPAN))
    dp_attn = _attn_bwd_post(f"attn_bwd_post{l}", grads, cos_t, sin_t)

    dq_h, df_h, di_h, dg_h, d_lower, dg_hgrn = _hgrn_bwd(
        f"hgrn_bwd{l}", saved["proj"], d_rec, saved["o_pre"], saved["states"], saved["scores"],
        saved["lb3"], small["hgrn_out_gain"][l][None, :], stack, stack_t)
    dproj = jnp.concatenate([dp_attn, dq_h, df_h, di_h, dg_h], axis=1)

    d_win = _mm_dw(f"d_win{l}", saved["h_t"], dproj, True, tm)
    after_mix = on_grads(l, "mix", (d_win, d_wo))
    dx_in, dx_in_b, dg_mix = _mm_accum(f"d_h{l}", dproj, win, True, tm, dxm,
                                       norm=(saved["x_in"], small["norm_mix"][l][None, :]), after=after_mix)

    small_grads = {"norm_mix": dg_mix[0], "attn_out_gain": dg_attn[0],
                   "lower": d_lower.reshape(HGRN_WIDTH),
                   "hgrn_out_gain": jnp.sum(dg_hgrn, axis=0).reshape(HGRN_DIM), "norm_mlp": dg_mlp[0]}
    return (dx_in, dx_in_b), after_mix, small_grads


def _local_step(xs, target, small, get_weights, on_grads):
    consts = _rope_tables() + _hgrn_consts() + (_head_sum_matrix(),)
    stream = xs
    saved = []
    for l in range(DEPTH):
        w, after = get_weights(l, stream)
        stream, s = _layer_forward(l, stream, small, w, consts, after=after)
        saved.append(s)
    dx_f, dx_b, dg_final, loss = _loss_head(stream, small["norm_final"][None, :], target)
    dx = (dx_f, dx_b)
    small_grads = [None] * DEPTH
    after = None
    for l in reversed(range(DEPTH)):
        dx, after, small_grads[l] = _layer_backward(l, dx, saved[l], small, consts, on_grads, after=after)
    return loss, dx[0], dg_final[0], small_grads, after


def _pack_small(norm_mix, attn_out_gain, lb, hgrn_out_gain, norm_mlp, norm_final, last_row):
    rows = [norm_mix, attn_out_gain.reshape(1, D_MODEL), lb.reshape(1, D_MODEL),
            jnp.pad(hgrn_out_gain.reshape(1, DEPTH * HGRN_DIM), ((0, 0), (0, D_MODEL - DEPTH * HGRN_DIM))),
            norm_mlp, norm_final.reshape(1, D_MODEL), last_row.reshape(1, D_MODEL)]
    pack = jnp.concatenate(rows, axis=0)
    return jnp.pad(pack, ((0, PACK_ROWS - pack.shape[0]), (0, 0)))


def _unpack_small(pack):
    return (pack[0:2], pack[2].reshape(DEPTH, ATTN_WIDTH), pack[3].reshape(DEPTH, HGRN_WIDTH),
            pack[4, :DEPTH * HGRN_DIM].reshape(DEPTH, HGRN_DIM), pack[5:7], pack[7], pack[8])


def kernel(x, norm_mix, w_in, attn_out_gain, hgrn_lb_logits, hgrn_out_gain, w_out, norm_mlp, w_up, w_down, norm_final, loss_target, m_norm_mix, m_w_in, m_attn_out_gain, m_hgrn_lb_logits, m_hgrn_out_gain, m_w_out, m_norm_mlp, m_w_up, m_w_down, m_norm_final, v_norm_mix, v_w_in, v_attn_out_gain, v_hgrn_lb_logits, v_hgrn_out_gain, v_w_out, v_norm_mlp, v_w_up, v_w_down, v_norm_final):
    lower, lower_vjp = jax.vjp(_lower_bounds, hgrn_lb_logits)
    small = {"norm_mix": norm_mix, "attn_out_gain": attn_out_gain, "lower": lower,
             "hgrn_out_gain": hgrn_out_gain, "norm_mlp": norm_mlp, "norm_final": norm_final}
    big_w = (w_in, w_out, w_up, w_down)

    x_pos, y_pos, core = lax.axis_index("x"), lax.axis_index("y"), lax.axis_index("c")
    me = (2 * x_pos + y_pos).astype(jnp.int32).reshape(1)
    place = jnp.stack([4 * x_pos + 2 * y_pos + core, 2 * x_pos + y_pos, core]).astype(jnp.int32)
    shards = [[w[l].astype(BF16) for w in big_w] for l in range(DEPTH)]
    in_flight = {}

    def start_gather(name, some, after):
        lands = [_own_slot(f"own_{name}_{i}", s, me) for i, s in enumerate(some)]
        sems, passed, token = _exchange_start(f"start_{name}", some, lands, "gather", after)
        in_flight[name] = (sems, passed)
        return token

    def finish_gather(name, after):
        return _exchange_wait(f"wait_{name}", *in_flight.pop(name), "gather", after)[1]

    def get_weights(l, stream):
        if l == 0:
            (win,) = _gather_weights("gather_w_in0", shards[0][:1])
            token = start_gather("gather_rest0", shards[0][1:], win)
            token = start_gather("gather_w_in1", shards[1][:1], token)
            token = start_gather("gather_rest1", shards[1][1:], token)
            return (win, lambda after: finish_gather("gather_rest0", after)), token
        (win,) = finish_gather("gather_w_in1", stream)
        return (win, lambda after: finish_gather("gather_rest1", after)), None

    reduced = {}

    def start_exchange(name, grads):
        srcs, lands = [g for g, _ in grads], [land for _, land in grads]
        sems, passed, token = _exchange_start(f"start_{name}", srcs, lands, "scatter", srcs[0])
        in_flight[name] = (sems, passed)
        return token

    def finish_exchange(name, after):
        own, landed = _exchange_wait(f"wait_{name}", *in_flight.pop(name), "scatter", after)
        return [_sum_devices(f"sum_{name}_{i}", p, g, place) for i, (p, g) in enumerate(zip(landed, own))]

    def on_grads(l, group, grads):
        if (l, group) == (1, "mlp"):
            return start_exchange("mlp1", grads)
        if (l, group) == (1, "mix"):
            return start_exchange("mix1", grads)
        if (l, group) == (0, "mlp"):
            reduced[(1, "mlp")] = finish_exchange("mlp1", grads[0][0])
            reduced[(1, "mix")] = finish_exchange("mix1", grads[0][0])
            return start_exchange("mlp0", grads)
        token = start_exchange("mix0", grads)
        reduced[(0, "mlp")] = finish_exchange("mlp0", token)
        return token

    loss, dx, dg_final, sg, last_started = _local_step(x[0], loss_target[0], small, get_weights, on_grads)

    big_m = (m_w_in, m_w_out, m_w_up, m_w_down)
    big_v = (v_w_in, v_w_out, v_w_up, v_w_down)
    names = ("w_in", "w_out", "w_up", "w_down")
    big_g, big_delta, big_new_m, big_new_v = [None] * 4, [None] * 4, [None] * 4, [None] * 4

    def finish_weights(group, which):
        whole = _share_halves(f"share_{group}", [[reduced[(l, group)][i] for l in range(DEPTH)] for i in range(2)])
        for i, w in enumerate(which):
            shape = big_w[w].shape
            flat = lambda arr: arr.reshape(shape[0] * shape[1], shape[2])
            d, m2, v2 = _adamw(f"adamw_{names[w]}", flat(big_w[w]), flat(whole[i]), flat(big_m[w]), flat(big_v[w]))
            big_g[w], big_delta[w] = whole[i], d.reshape(shape)
            big_new_m[w], big_new_v[w] = m2.reshape(shape), v2.reshape(shape)

    finish_weights("mlp", (2, 3))

    stack2 = lambda key: jnp.stack([sg[l][key] for l in range(DEPTH)])
    pack = _pack_small(stack2("norm_mix"), stack2("attn_out_gain"), stack2("lower"), stack2("hgrn_out_gain"),
                       stack2("norm_mlp"), dg_final, jnp.broadcast_to(loss[0, 0] + last_started[0, 0], (D_MODEL,)))
    g_mix, g_attn, g_lower, g_hgrn, g_mlp, g_final, loss_row = _unpack_small(_all_reduce_small(pack))
    (g_logits,) = lower_vjp(g_lower)

    zeros_row = jnp.zeros((D_MODEL,), F32)
    small_w = (norm_mix, attn_out_gain, hgrn_lb_logits, hgrn_out_gain, norm_mlp, norm_final)
    small_m = (m_norm_mix, m_attn_out_gain, m_hgrn_lb_logits, m_hgrn_out_gain, m_norm_mlp, m_norm_final)
    small_v = (v_norm_mix, v_attn_out_gain, v_hgrn_lb_logits, v_hgrn_out_gain, v_norm_mlp, v_norm_final)
    small_g = (g_mix, g_attn, g_logits, g_hgrn, g_mlp, g_final)
    packs = [_pack_small(*t, zeros_row) for t in (small_w, small_g, small_m, small_v)]
    small_delta, small_new_m, small_new_v = [_unpack_small(p)[:6] for p in _adamw("adamw_small", *packs)]

    reduced[(0, "mix")] = finish_exchange("mix0", small_delta[0])
    finish_weights("mix", (0, 1))

    def ordered(small6, big4):
        mix, attn, lbl, hg, mlp, fin = small6
        return (mix, big4[0], attn, lbl, hg, big4[1], mlp, big4[2], big4[3], fin)

    return ((loss_row[0], dx[None]) + ordered(small_g, big_g) + ordered(small_delta, big_delta)
            + ordered(small_new_m, big_new_m) + ordered(small_new_v, big_new_v))
```

```python
import functools
import math

import numpy as np
import jax
import jax.numpy as jnp
from jax import lax
from jax.experimental import pallas as pl
from jax.experimental.pallas import tpu as pltpu

F32 = jnp.float32
BF16 = jnp.bfloat16
MESH = pl.DeviceIdType.MESH

SEQ = 4096
D_MODEL = 1024
DEPTH = 2
ATTN_WIDTH = 512
HEAD_DIM = 64
HGRN_HEADS = 4
HGRN_DIM = 128
HGRN_WIDTH = 512
IN_W = 3584
MLP_HIDDEN = 4096
N_CHIPS = 4
N_DEV = 8
SHARD_IN = IN_W // N_CHIPS
SHARD_OUT = D_MODEL // N_CHIPS
SHARD_MLP = MLP_HIDDEN // N_CHIPS
DILATIONS = (1, 4, 16)
SPAN = 128
ROPE_THETA = 10000.0
NORM_EPS = 1e-6
MASK_VALUE = -1e30
CHUNK = 128
ROW_TILE = 512
MM_TILE = 512
VMEM_LIMIT = 52 * 1024 * 1024

ADAM_LR = 0.001
ADAM_B1 = 0.9
ADAM_B2 = 0.999
ADAM_EPS = 1e-08
ADAM_WD = 0.01
ADAM_STEP = 10

PACK_ROWS = 16


def _params(n_axes):
    return pltpu.CompilerParams(dimension_semantics=("arbitrary",) * n_axes,
                                vmem_limit_bytes=VMEM_LIMIT)


def _dot(a, b):
    return jnp.dot(a.astype(BF16), b.astype(BF16), preferred_element_type=F32)


def _dot_nt(a, b):
    return lax.dot_general(a.astype(BF16), b.astype(BF16), (((1,), (1,)), ((), ())),
                           preferred_element_type=F32)


def _dot_tn(a, b):
    return lax.dot_general(a.astype(BF16), b.astype(BF16), (((0,), (0,)), ((), ())),
                           preferred_element_type=F32)


def _sigmoid(x):
    return 1.0 / (1.0 + jnp.exp(-x))


def _spec(shape, index_map):
    return pl.BlockSpec(shape, index_map)


def _mm_pieces(name, a, w, nt, tm, epilogue="none", extra=None, after=None):
    s = a.shape[0]
    pw = w.shape[1] if nt else w.shape[2]
    width = N_CHIPS * pw

    def body(a_ref, w_ref, *rest):
        e_ref = rest[0] if extra is not None else None
        outs = rest[-3:] if epilogue == "relu2" else rest[-1:]
        av = a_ref[...].astype(BF16)
        for j in range(N_CHIPS):
            cols = slice(j * pw, (j + 1) * pw)
            r = _dot_nt(av, w_ref[j]) if nt else _dot(av, w_ref[j])
            if epilogue == "relu2":
                relu = jnp.maximum(r, 0.0)
                r = relu * relu
                outs[1][:, cols] = relu.astype(BF16)
                outs[2][cols, :] = r.T.astype(BF16)
            elif epilogue == "relu2_grad":
                r = r * (2.0 * e_ref[:, cols].astype(F32))
            outs[0][:, cols] = r.astype(outs[0].dtype)

    row = lambda width_: _spec((tm, width_), lambda i: (i, 0))
    in_specs = [row(a.shape[1]), _spec(w.shape, lambda i: (0, 0, 0))]
    args = [a, w]
    if extra is not None:
        in_specs.append(row(width))
        args.append(extra)
    if after is not None:
        in_specs.append(pl.BlockSpec(memory_space=pl.ANY))
        args.append(after)
    if epilogue == "relu2":
        out_specs = [row(width), row(width), _spec((width, tm), lambda i: (0, i))]
        out_shape = [jax.ShapeDtypeStruct((s, width), BF16)] * 2 + [jax.ShapeDtypeStruct((width, s), BF16)]
    else:
        out_specs = row(width)
        out_shape = jax.ShapeDtypeStruct((s, width), BF16 if epilogue == "relu2_grad" else F32)
    return pl.pallas_call(body, name=name, grid=(s // tm,), in_specs=in_specs, out_specs=out_specs,
                          out_shape=out_shape, compiler_params=_params(1))(*args)


def _mm_accum(name, a, w, nt, tm, resid, norm=None, after=None):
    s = a.shape[0]
    pk = w.shape[2] if nt else w.shape[1]
    d = w.shape[1] if nt else w.shape[2]

    def body(a_ref, w_ref, resid_ref, *rest):
        r = None
        for j in range(N_CHIPS):
            piece = a_ref[:, j * pk:(j + 1) * pk].astype(BF16)
            term = _dot_nt(piece, w_ref[j]) if nt else _dot(piece, w_ref[j])
            r = term if r is None else r + term
        if norm is None:
            rest[-1][...] = r + resid_ref[...]
            return
        x_ref, g_ref = rest[:2]
        dx_ref, dxb_ref, dg_ref = rest[-3:]

        @pl.when(pl.program_id(0) == 0)
        def _():
            dg_ref[...] = jnp.zeros_like(dg_ref)

        xv = x_ref[...]
        rs = lax.rsqrt(jnp.mean(xv * xv, axis=1, keepdims=True) + NORM_EPS)
        xhat = xv * rs
        rg = r * g_ref[...]
        dx = resid_ref[...] + rs * (rg - xhat * jnp.mean(rg * xhat, axis=1, keepdims=True))
        dx_ref[...] = dx
        dxb_ref[...] = dx.astype(BF16)
        dg_ref[...] += jnp.sum(r * xhat, axis=0, keepdims=True)

    row = lambda width: _spec((tm, width), lambda i: (i, 0))
    in_specs = [row(a.shape[1]), _spec(w.shape, lambda i: (0, 0, 0)), row(d)]
    args = [a, w, resid]
    if norm is None:
        out_specs, out_shape = row(d), jax.ShapeDtypeStruct((s, d), F32)
    else:
        in_specs += [row(d), _spec((1, d), lambda i: (0, 0))]
        args += list(norm)
        out_specs = [row(d), row(d), _spec((1, d), lambda i: (0, 0))]
        out_shape = [jax.ShapeDtypeStruct((s, d), F32), jax.ShapeDtypeStruct((s, d), BF16),
                     jax.ShapeDtypeStruct((1, d), F32)]
    if after is not None:
        in_specs.append(pl.BlockSpec(memory_space=pl.ANY))
        args.append(after)
    return pl.pallas_call(body, name=name, grid=(s // tm,), in_specs=in_specs, out_specs=out_specs,
                          out_shape=out_shape, compiler_params=_params(1))(*args)


def _mm_dw(name, a_t, b, by_cols, tk):
    m, s = a_t.shape
    n = b.shape[1]
    shape = (N_CHIPS, m, n // N_CHIPS) if by_cols else (N_CHIPS, m // N_CHIPS, n)
    n_steps = s // tk

    def body(a_ref, b_ref, o_ref, land_ref, acc):
        del land_ref

        @pl.when(pl.program_id(0) == 0)
        def _():
            acc[...] = jnp.zeros_like(acc)

        for j in range(N_CHIPS):
            if by_cols:
                acc[j] += _dot(a_ref[...], b_ref[:, j * shape[2]:(j + 1) * shape[2]])
            else:
                acc[j] += _dot(a_ref[j * shape[1]:(j + 1) * shape[1], :], b_ref[...])

        @pl.when(pl.program_id(0) == n_steps - 1)
        def _():
            o_ref[...] = acc[...].astype(BF16)

    return pl.pallas_call(
        body, name=name, grid=(n_steps,),
        in_specs=[_spec((m, tk), lambda k: (0, k)), _spec((tk, n), lambda k: (k, 0))],
        out_specs=[_spec(shape, lambda k: (0, 0, 0)), ANY_SPEC],
        out_shape=[jax.ShapeDtypeStruct(shape, BF16),
                   jax.ShapeDtypeStruct((N_DEV, shape[1] // 2, shape[2]), BF16)],
        scratch_shapes=[pltpu.VMEM(shape, F32)],
        compiler_params=_params(1))(a_t, b)


def _rms_fwd(name, x, gain, after=None):
    s, d = x.shape
    t = ROW_TILE

    def body(x_ref, g_ref, *rest):
        h_ref, ht_ref = rest[-2:]
        xv = x_ref[...]
        r = lax.rsqrt(jnp.mean(xv * xv, axis=1, keepdims=True) + NORM_EPS)
        h = xv * r * g_ref[...]
        h_ref[...] = h.astype(BF16)
        ht_ref[...] = h.T.astype(BF16)

    in_specs = [_spec((t, d), lambda i: (i, 0)), _spec((1, d), lambda i: (0, 0))]
    args = (x, gain)
    if after is not None:
        in_specs.append(pl.BlockSpec(memory_space=pl.ANY))
        args += (after,)
    return pl.pallas_call(
        body, name=name, grid=(s // t,), in_specs=in_specs,
        out_specs=[_spec((t, d), lambda i: (i, 0)), _spec((d, t), lambda i: (0, i))],
        out_shape=[jax.ShapeDtypeStruct((s, d), BF16), jax.ShapeDtypeStruct((d, s), BF16)],
        compiler_params=_params(1),
    )(*args)


def _loss_head(x, gain, target):
    s, d = x.shape
    t = ROW_TILE
    n_steps = s // t

    def body(x_ref, g_ref, t_ref, dx_ref, dxb_ref, dg_ref, loss_ref, acc):
        i = pl.program_id(0)

        @pl.when(i == 0)
        def _():
            dg_ref[...] = jnp.zeros_like(dg_ref)
            acc[...] = jnp.zeros_like(acc)

        xv = x_ref[...]
        g = g_ref[...]
        r = lax.rsqrt(jnp.mean(xv * xv, axis=1, keepdims=True) + NORM_EPS)
        xhat = xv * r
        err = xhat * g - t_ref[...]
        acc[...] += jnp.sum(err * err, axis=0, keepdims=True)
        dy = err * (1.0 / d)
        dyg = dy * g
        proj = jnp.mean(dyg * xhat, axis=1, keepdims=True)
        dx = r * (dyg - xhat * proj)
        dx_ref[...] = dx
        dxb_ref[...] = dx.astype(BF16)
        dg_ref[...] += jnp.sum(dy * xhat, axis=0, keepdims=True)

        @pl.when(i == n_steps - 1)
        def _():
            total = jnp.sum(acc[...], axis=1, keepdims=True) * (0.5 / d)
            loss_ref[...] = jnp.broadcast_to(total, loss_ref.shape)

    return pl.pallas_call(
        body, name="loss_head", grid=(n_steps,),
        in_specs=[_spec((t, d), lambda i: (i, 0)), _spec((1, d), lambda i: (0, 0)),
                  _spec((t, d), lambda i: (i, 0))],
        out_specs=[_spec((t, d), lambda i: (i, 0)), _spec((t, d), lambda i: (i, 0)),
                   _spec((1, d), lambda i: (0, 0)), _spec((1, 128), lambda i: (0, 0))],
        out_shape=[jax.ShapeDtypeStruct((s, d), F32), jax.ShapeDtypeStruct((s, d), BF16),
                   jax.ShapeDtypeStruct((1, d), F32), jax.ShapeDtypeStruct((1, 128), F32)],
        scratch_shapes=[pltpu.VMEM((1, d), F32)],
        compiler_params=_params(1),
    )(x, gain, target)


def _rope_tables():
    half = HEAD_DIM // 2
    inv_freq = ROPE_THETA ** (-jnp.arange(half, dtype=F32) / half)
    ang = jnp.arange(SEQ, dtype=jnp.int32).astype(F32)[:, None] * inv_freq[None, :]
    cos, sin = jnp.cos(ang), jnp.sin(ang)
    cos_t = jnp.concatenate([cos, cos, cos, cos], axis=1)
    sin_t = jnp.concatenate([-sin, sin, -sin, sin], axis=1)
    return cos_t, sin_t


def _swap_halves(x):
    lane = lax.broadcasted_iota(jnp.int32, x.shape, 1)
    first = (lane % HEAD_DIM) < (HEAD_DIM // 2)
    return jnp.where(first, pltpu.roll(x, 128 - HEAD_DIM // 2, 1), pltpu.roll(x, HEAD_DIM // 2, 1))


def _permuted_specs(t, width):
    specs = [_spec((t, width), lambda i: (i, 0))]
    for d in DILATIONS[1:]:
        specs.append(_spec((d, t // d, width), lambda i: (0, i, 0)))
    return specs


def _permuted_shapes(width, dtype):
    shapes = [jax.ShapeDtypeStruct((SEQ, width), dtype)]
    for d in DILATIONS[1:]:
        shapes.append(jax.ShapeDtypeStruct((d, SEQ // d, width), dtype))
    return shapes


def _attn_prep(name, proj, cos_t, sin_t):
    t = ROW_TILE
    w = ATTN_WIDTH

    def body(q_ref, k_ref, v_ref, cos_ref, sin_ref, *rest):
        outs, scr = rest[:9], rest[9]
        cosv, sinv = cos_ref[...], sin_ref[...]
        for a, (src, roped, scale) in enumerate(((q_ref, True, HEAD_DIM ** -0.5),
                                                 (k_ref, True, 1.0), (v_ref, False, 1.0))):
            o1, o4, o16 = outs[3 * a:3 * a + 3]
            for cb in range(w // 128):
                cols = slice(cb * 128, (cb + 1) * 128)
                val = src[:, cols]
                if roped:
                    val = (val * cosv + _swap_halves(val) * sinv) * scale
                scr[...] = val
                o1[:, cols] = val.astype(BF16)
                for o_ref, d in ((o4, 4), (o16, 16)):
                    for r in range(d):
                        o_ref[r, :, cols] = scr[pl.ds(r, t // d, stride=d), :].astype(BF16)

    out_specs = _permuted_specs(t, w) * 3
    out_shape = _permuted_shapes(w, BF16) * 3
    outs = pl.pallas_call(
        body, name=name, grid=(SEQ // t,),
        in_specs=[_spec((t, w), lambda i: (i, 0)), _spec((t, w), lambda i: (i, 1)),
                  _spec((t, w), lambda i: (i, 2)),
                  _spec((t, 128), lambda i: (i, 0)), _spec((t, 128), lambda i: (i, 0))],
        out_specs=out_specs, out_shape=out_shape,
        scratch_shapes=[pltpu.VMEM((t, 128), F32)],
        compiler_params=_params(1),
    )(proj, proj, proj, cos_t, sin_t)
    q, k, v = outs[0:3], outs[3:6], outs[6:9]
    flat = lambda arr: arr.reshape(SEQ, w)
    return [(flat(q[p]), flat(k[p]), flat(v[p])) for p in range(3)]


def _band_masks():
    row = lax.broadcasted_iota(jnp.int32, (2 * SPAN, 2 * SPAN), 0) % SPAN
    col = lax.broadcasted_iota(jnp.int32, (2 * SPAN, 2 * SPAN), 1)
    is_prev = col < SPAN
    band = (is_prev & (col >= row)) | (~is_prev & (col - SPAN <= row))
    head0 = lax.broadcasted_iota(jnp.int32, (SPAN, 128), 1) < HEAD_DIM
    return band, is_prev, head0


def _stack_heads(x, head0):
    zero = jnp.zeros_like(x)
    return jnp.concatenate([jnp.where(head0, x, zero), jnp.where(head0, zero, x)], axis=0)


def _attn_fwd(name, q, k, v, seg_blocks):
    n_blocks = SEQ // SPAN

    def body(q_ref, k_ref, v_ref, o_ref, lse_ref):
        band, is_prev, head0 = _band_masks()

        def step(b, carry):
            cur = pl.ds(pl.multiple_of(b * SPAN, SPAN), SPAN)
            prev = pl.ds(pl.multiple_of(jnp.maximum(b - 1, 0) * SPAN, SPAN), SPAN)
            qs = _stack_heads(q_ref[cur, :], head0)
            kcat = jnp.concatenate([k_ref[prev, :], k_ref[cur, :]], axis=0)
            vcat = jnp.concatenate([v_ref[prev, :], v_ref[cur, :]], axis=0)
            ok = band & (((b % seg_blocks) != 0) | ~is_prev)
            s = jnp.where(ok, _dot_nt(qs, kcat), MASK_VALUE)
            m = jnp.max(s, axis=1, keepdims=True)
            p = jnp.exp(s - m)
            l = jnp.sum(p, axis=1, keepdims=True)
            pv = _dot(p, vcat) * (1.0 / l)
            lse = m + jnp.log(l)
            o_ref[cur, :] = jnp.where(head0, pv[:SPAN], pv[SPAN:])
            lse_ref[cur, :] = jnp.where(head0, lse[:SPAN], lse[SPAN:])
            return carry

        lax.fori_loop(0, n_blocks, step, 0, unroll=4)

    col = _spec((SEQ, 128), lambda j: (0, j))
    return pl.pallas_call(
        body, name=name, grid=(ATTN_WIDTH // 128,),
        in_specs=[col, col, col], out_specs=[col, col],
        out_shape=[jax.ShapeDtypeStruct((SEQ, ATTN_WIDTH), F32)] * 2,
        compiler_params=_params(1),
    )(q, k, v)


def _unpermute(dst, src_ref, d, cols):
    n = dst.shape[0] // d
    for r in range(d):
        dst[pl.ds(r, n, stride=d), :] = src_ref[r, :, cols]


def _attn_merge(name, outs, lses, gain):
    t = ROW_TILE
    w = ATTN_WIDTH

    def body(o1, o4, o16, l1, l4, l16, g_ref, an_ref, ant_ref, attn_ref, lse_ref, so4, so16, sl4, sl16):
        for cb in range(w // 128):
            cols = slice(cb * 128, (cb + 1) * 128)
            _unpermute(so4, o4, 4, cols)
            _unpermute(so16, o16, 16, cols)
            _unpermute(sl4, l4, 4, cols)
            _unpermute(sl16, l16, 16, cols)
            la, lb, lc = l1[:, cols], sl4[...], sl16[...]
            m = jnp.maximum(jnp.maximum(la, lb), lc)
            ea, eb, ec = jnp.exp(la - m), jnp.exp(lb - m), jnp.exp(lc - m)
            tot = ea + eb + ec
            attn_ref[:, cols] = (ea * o1[:, cols] + eb * so4[...] + ec * so16[...]) / tot
            lse_ref[:, cols] = m + jnp.log(tot)
        attn = attn_ref[...]
        r = lax.rsqrt(jnp.mean(attn * attn, axis=1, keepdims=True) + NORM_EPS)
        an = attn * r * g_ref[...]
        an_ref[...] = an.astype(BF16)
        ant_ref[...] = an.T.astype(BF16)

    views = lambda arrs: [arrs[0], arrs[1].reshape(4, SEQ // 4, w), arrs[2].reshape(16, SEQ // 16, w)]
    row = _spec((t, w), lambda i: (i, 0))
    return pl.pallas_call(
        body, name=name, grid=(SEQ // t,),
        in_specs=_permuted_specs(t, w) * 2 + [_spec((1, w), lambda i: (0, 0))],
        out_specs=[row, _spec((w, t), lambda i: (0, i)), row, row],
        out_shape=[jax.ShapeDtypeStruct((SEQ, 2 * w), BF16), jax.ShapeDtypeStruct((2 * w, SEQ), BF16),
                   jax.ShapeDtypeStruct((SEQ, w), F32), jax.ShapeDtypeStruct((SEQ, w), F32)],
        scratch_shapes=[pltpu.VMEM((t, 128), F32)] * 4,
        compiler_params=_params(1),
    )(*views(outs), *views(lses), gain)


def _head_sum_matrix():
    i = np.arange(ATTN_WIDTH)
    return jnp.asarray((i[:, None] // HEAD_DIM) == (i[None, :] // HEAD_DIM), dtype=F32)


def _attn_bwd_prep(name, d_an, attn, lse, gain, head_sum):
    t = ROW_TILE
    w = ATTN_WIDTH

    def body(dan_ref, attn_ref, lse_ref, g_ref, hs_ref, *rest):
        (do1, do4, do16, dl1, dl4, dl16, ls4, ls16, dg_ref), (sdo, sdl, sls) = rest[:9], rest[9:]

        @pl.when(pl.program_id(0) == 0)
        def _():
            dg_ref[...] = jnp.zeros_like(dg_ref)

        attn = attn_ref[...]
        dan = dan_ref[...]
        r = lax.rsqrt(jnp.mean(attn * attn, axis=1, keepdims=True) + NORM_EPS)
        xhat = attn * r
        dg_ref[...] += jnp.sum(dan * xhat, axis=0, keepdims=True)
        dang = dan * g_ref[...]
        d_o = r * (dang - xhat * jnp.mean(dang * xhat, axis=1, keepdims=True))
        delta = jnp.dot(d_o * attn, hs_ref[...], preferred_element_type=F32,
                        precision=lax.Precision.HIGHEST)
        do1[...] = d_o.astype(BF16)
        dl1[...] = delta
        for cb in range(w // 128):
            cols = slice(cb * 128, (cb + 1) * 128)
            sdo[...] = d_o[:, cols]
            sdl[...] = delta[:, cols]
            sls[...] = lse_ref[:, cols]
            for d, o_do, o_dl, o_ls in ((4, do4, dl4, ls4), (16, do16, dl16, ls16)):
                for rr in range(d):
                    rows = pl.ds(rr, t // d, stride=d)
                    o_do[rr, :, cols] = sdo[rows, :].astype(BF16)
                    o_dl[rr, :, cols] = sdl[rows, :]
                    o_ls[rr, :, cols] = sls[rows, :]

    row = _spec((t, w), lambda i: (i, 0))
    perm = _permuted_specs(t, w)
    outs = pl.pallas_call(
        body, name=name, grid=(SEQ // t,),
        in_specs=[row, row, row, _spec((1, w), lambda i: (0, 0)), _spec((w, w), lambda i: (0, 0))],
        out_specs=perm + perm + perm[1:] + [_spec((1, w), lambda i: (0, 0))],
        out_shape=(_permuted_shapes(w, BF16) + _permuted_shapes(w, F32) + _permuted_shapes(w, F32)[1:]
                   + [jax.ShapeDtypeStruct((1, w), F32)]),
        scratch_shapes=[pltpu.VMEM((t, 128), F32)] * 3,
        compiler_params=_params(1),
    )(d_an, attn, lse, gain, head_sum)
    flat = lambda arr: arr.reshape(SEQ, w)
    d_out = [flat(a) for a in outs[0:3]]
    delta = [flat(a) for a in outs[3:6]]
    lses = [lse, flat(outs[6]), flat(outs[7])]
    return d_out, delta, lses, outs[8]


def _attn_bwd(name, q, k, v, d_out, delta, lse, seg_blocks):
    n_blocks = SEQ // SPAN

    def body(q_ref, k_ref, v_ref, do_ref, dl_ref, lse_ref, dq_ref, dk_ref, dv_ref):
        band, is_prev, head0 = _band_masks()
        dk_ref[...] = jnp.zeros_like(dk_ref)
        dv_ref[...] = jnp.zeros_like(dv_ref)

        def per_head(x):
            return jnp.concatenate([x[:, 0:1], x[:, HEAD_DIM:HEAD_DIM + 1]], axis=0)

        def step(b, carry):
            cur = pl.ds(pl.multiple_of(b * SPAN, SPAN), SPAN)
            prev = pl.ds(pl.multiple_of(jnp.maximum(b - 1, 0) * SPAN, SPAN), SPAN)
            qs = _stack_heads(q_ref[cur, :], head0)
            dos = _stack_heads(do_ref[cur, :], head0)
            kcat = jnp.concatenate([k_ref[prev, :], k_ref[cur, :]], axis=0)
            vcat = jnp.concatenate([v_ref[prev, :], v_ref[cur, :]], axis=0)
            ok = band & (((b % seg_blocks) != 0) | ~is_prev)
            p = jnp.where(ok, jnp.exp(_dot_nt(qs, kcat) - per_head(lse_ref[cur, :])), 0.0)
            ds = p * (_dot_nt(dos, vcat) - per_head(dl_ref[cur, :]))
            dq = _dot(ds, kcat)
            dq_ref[cur, :] = jnp.where(head0, dq[:SPAN], dq[SPAN:])
            dk = _dot_tn(ds, qs)
            dv = _dot_tn(p, dos)
            dk_ref[prev, :] += dk[:SPAN]
            dv_ref[prev, :] += dv[:SPAN]
            dk_ref[cur, :] += dk[SPAN:]
            dv_ref[cur, :] += dv[SPAN:]
            return carry

        lax.fori_loop(0, n_blocks, step, 0, unroll=4)

    col = _spec((SEQ, 128), lambda j: (0, j))
    return pl.pallas_call(
        body, name=name, grid=(ATTN_WIDTH // 128,),
        in_specs=[col] * 6, out_specs=[col] * 3,
        out_shape=[jax.ShapeDtypeStruct((SEQ, ATTN_WIDTH), F32)] * 3,
        compiler_params=_params(1),
    )(q, k, v, d_out, delta, lse)


def _attn_bwd_post(name, grads, cos_t, sin_t):
    t = ROW_TILE
    w = ATTN_WIDTH

    def body(*refs):
        ins, cos_ref, sin_ref, out_ref, s4, s16 = refs[:9], refs[9], refs[10], refs[11], refs[12], refs[13]
        cosv, sinv = cos_ref[...], sin_ref[...]
        for a in range(3):
            g1, g4, g16 = ins[a], ins[3 + a], ins[6 + a]
            for cb in range(w // 128):
                cols = slice(cb * 128, (cb + 1) * 128)
                _unpermute(s4, g4, 4, cols)
                _unpermute(s16, g16, 16, cols)
                val = g1[:, cols] + s4[...] + s16[...]
                if a < 2:
                    val = val * cosv + _swap_halves(val * sinv)
                if a == 0:
                    val = val * (HEAD_DIM ** -0.5)
                out_ref[:, a * w + cb * 128:a * w + (cb + 1) * 128] = val.astype(BF16)

    views = []
    for p, d in enumerate(DILATIONS):
        for a in range(3):
            views.append(grads[p][a] if d == 1 else grads[p][a].reshape(d, SEQ // d, w))
    perm = _permuted_specs(t, w)
    in_specs = [perm[0]] * 3 + [perm[1]] * 3 + [perm[2]] * 3
    return pl.pallas_call(
        body, name=name, grid=(SEQ // t,),
        in_specs=in_specs + [_spec((t, 128), lambda i: (i, 0))] * 2,
        out_specs=_spec((t, 3 * w), lambda i: (i, 0)),
        out_shape=jax.ShapeDtypeStruct((SEQ, 3 * w), BF16),
        scratch_shapes=[pltpu.VMEM((t, 128), F32)] * 2,
        compiler_params=_params(1),
    )(*views, cos_t, sin_t)


N_LEVELS = 7
HGRN_PAIR = 2


def _hgrn_consts():
    c = CHUNK
    i = np.arange(c)[:, None]
    s = np.arange(c)[None, :]
    blocks = [s <= i]
    for lv in range(N_LEVELS):
        bs = c >> lv
        h = bs // 2
        m = (i // bs) * bs + h - 1
        second = (i % bs) >= h
        blocks.append((second & (s > m) & (s <= i)) | (~second & (s > i) & (s <= m)))
    blocks.append(s > i)
    stack = np.concatenate(blocks, axis=0).astype(np.float32)
    return jnp.asarray(stack, dtype=BF16), jnp.asarray(stack.T, dtype=BF16)


def _exact_dot(m01, x):
    hi = x.astype(BF16)
    lo = (x - hi.astype(F32)).astype(BF16)
    n = x.shape[1]
    full = jnp.dot(m01, jnp.concatenate([hi, lo], axis=1), preferred_element_type=F32)
    return full[:, :n] + full[:, n:]


def _hgrn_gates(qh, z, lb):
    sq = _sigmoid(qh)
    q = qh * sq * (HGRN_DIM ** -0.5)
    sig = _sigmoid(z)
    sigm = _sigmoid(-z)
    f = lb + (1.0 - lb) * sig
    k = (1.0 - lb) * sigm
    return q, k, f, sq, sig, sigm


def _level_masks(lv):
    row = lax.broadcasted_iota(jnp.int32, (CHUNK, CHUNK), 0)
    col = lax.broadcasted_iota(jnp.int32, (CHUNK, CHUNK), 1)
    shift = N_LEVELS - lv
    second = (row & (CHUNK >> (lv + 1))) != 0
    same = (row >> shift) == (col >> shift)
    return second, same


def _hgrn_fwd(name, proj, lb, gain, stack, mixed, mixed_t):
    t = ROW_TILE
    per = t // CHUNK
    n_rb = SEQ // t
    n_chunks = SEQ // CHUNK
    col0 = 3 * ATTN_WIDTH // 128
    pair_w = HGRN_PAIR * HGRN_DIM

    def body(q_ref, f_ref, i_ref, g_ref, lb_ref, gain_ref, stack_ref, mixed_in, mixed_t_in,
             rec_ref, rect_ref, o_ref, st_out, a_out, st):
        del mixed_in, mixed_t_in

        @pl.when(pl.program_id(1) == 0)
        def _():
            st[...] = jnp.zeros_like(st)

        row = lax.broadcasted_iota(jnp.int32, (CHUNK, CHUNK), 0)
        col = lax.broadcasted_iota(jnp.int32, (CHUNK, CHUNK), 1)
        for c, hh in [(c, hh) for c in range(per) for hh in range(HGRN_PAIR)]:
            rows = slice(c * CHUNK, (c + 1) * CHUNK)
            lanes = slice(hh * HGRN_DIM, (hh + 1) * HGRN_DIM)
            lbv = lb_ref[hh]
            qh, z, v, gh = q_ref[rows, lanes], f_ref[rows, lanes], i_ref[rows, lanes], g_ref[rows, lanes]
            q, k, f, _, _, _ = _hgrn_gates(qh, z, lbv)
            dec = _exact_dot(stack_ref[...], jnp.log(f))
            g = dec[0:CHUNK]
            to_end = dec[(N_LEVELS + 1) * CHUNK:(N_LEVELS + 2) * CHUNK]
            a = jnp.where(row == col, jnp.sum(q * k, axis=1, keepdims=True), 0.0)
            for lv in range(N_LEVELS):
                e = jnp.exp(dec[(lv + 1) * CHUNK:(lv + 2) * CHUNK])
                second, same = _level_masks(lv)
                qt = jnp.where(second, q * e, 0.0)
                kt = jnp.where(second, 0.0, k * e)
                a = a + jnp.where(same, _dot_nt(qt, kt), 0.0)
            st_prev = st[hh]
            st_out[hh, c] = st_prev
            a_out[hh, c] = a
            o = _dot(a, v) + _dot_nt(q * jnp.exp(g), st_prev)
            k_end = k * jnp.exp(to_end)
            st[hh] = st_prev * jnp.exp(g[CHUNK - 1:CHUNK, :]) + _dot(v.T, k_end)
            o_ref[rows, lanes] = o
            r = lax.rsqrt(jnp.mean(o * o, axis=1, keepdims=True) + NORM_EPS)
            rec = o * r * gain_ref[...] * (gh * _sigmoid(gh))
            rec_ref[rows, lanes] = rec.astype(BF16)
            rect_ref[lanes, rows] = rec.T.astype(BF16)

    def col_spec(tt):
        return _spec((t, pair_w), lambda h, rb: (rb, (col0 + HGRN_HEADS * tt) // HGRN_PAIR + h))

    chunk_spec = _spec((HGRN_PAIR, per, CHUNK, CHUNK), lambda h, rb: (h, rb, 0, 0))
    return pl.pallas_call(
        body, name=name, grid=(HGRN_HEADS // HGRN_PAIR, n_rb),
        in_specs=[col_spec(0), col_spec(1), col_spec(2), col_spec(3),
                  _spec((HGRN_PAIR, 1, HGRN_DIM), lambda h, rb: (h, 0, 0)),
                  _spec((1, HGRN_DIM), lambda h, rb: (0, 0)),
                  _spec(stack.shape, lambda h, rb: (0, 0)), ANY_SPEC, ANY_SPEC],
        out_specs=[_spec((t, pair_w), lambda h, rb: (rb, ATTN_WIDTH // pair_w + h)),
                   _spec((pair_w, t), lambda h, rb: (ATTN_WIDTH // pair_w + h, rb)),
                   _spec((t, pair_w), lambda h, rb: (rb, h)),
                   chunk_spec, chunk_spec],
        out_shape=[jax.ShapeDtypeStruct(mixed.shape, BF16),
                   jax.ShapeDtypeStruct(mixed_t.shape, BF16),
                   jax.ShapeDtypeStruct((SEQ, HGRN_WIDTH), F32),
                   jax.ShapeDtypeStruct((HGRN_HEADS, n_chunks, CHUNK, CHUNK), F32),
                   jax.ShapeDtypeStruct((HGRN_HEADS, n_chunks, CHUNK, CHUNK), F32)],
        scratch_shapes=[pltpu.VMEM((HGRN_PAIR, CHUNK, CHUNK), F32)],
        input_output_aliases={7: 0, 8: 1},
        compiler_params=_params(2),
    )(proj, proj, proj, proj, lb, gain, stack, mixed, mixed_t)


def _hgrn_bwd(name, proj, d_rec, o_pre, states, scores, lb, gain, stack, stack_t):
    t = ROW_TILE
    per = t // CHUNK
    n_rb = SEQ // t
    col0 = 3 * ATTN_WIDTH // 128
    pair_w = HGRN_PAIR * HGRN_DIM

    def body(q_ref, f_ref, i_ref, g_ref, drec_ref, o_ref, st_ref, a_ref, lb_ref, gain_ref,
             stack_ref, stack_t_ref, dq_ref, df_ref, di_ref, dg_ref, dlb_ref, dgain_ref, dst):
        @pl.when(pl.program_id(1) == 0)
        def _():
            dst[...] = jnp.zeros_like(dst)
            dlb_ref[...] = jnp.zeros_like(dlb_ref)
            dgain_ref[...] = jnp.zeros_like(dgain_ref)

        gain_v = gain_ref[...]
        row = lax.broadcasted_iota(jnp.int32, (CHUNK, CHUNK), 0)
        col = lax.broadcasted_iota(jnp.int32, (CHUNK, CHUNK), 1)
        for c, hh in [(c, hh) for c in reversed(range(per)) for hh in range(HGRN_PAIR)]:
            rows = slice(c * CHUNK, (c + 1) * CHUNK)
            lanes = slice(hh * HGRN_DIM, (hh + 1) * HGRN_DIM)
            lbv = lb_ref[hh]
            qh, z, v, gh = q_ref[rows, lanes], f_ref[rows, lanes], i_ref[rows, lanes], g_ref[rows, lanes]
            q, k, f, sq, sig, sigm = _hgrn_gates(qh, z, lbv)
            dec = _exact_dot(stack_ref[...], jnp.log(f))
            g = dec[0:CHUNK]
            to_end = dec[(N_LEVELS + 1) * CHUNK:(N_LEVELS + 2) * CHUNK]
            e_g = jnp.exp(g)
            e_end = jnp.exp(to_end)
            e_last = jnp.exp(g[CHUNK - 1:CHUNK, :])
            q_in = q * e_g
            k_end = k * e_end
            st_prev = st_ref[hh, c]
            a = a_ref[hh, c]
            dst_new = dst[hh]

            o = o_ref[rows, lanes]
            drec = drec_ref[rows, lanes]
            sg = _sigmoid(gh)
            r = lax.rsqrt(jnp.mean(o * o, axis=1, keepdims=True) + NORM_EPS)
            ohat = o * r
            d_gh = drec * (ohat * gain_v) * (sg * (1.0 + gh * (1.0 - sg)))
            d_on = drec * (gh * sg)
            dgain_ref[hh] += jnp.sum(d_on * ohat, axis=0, keepdims=True)
            d_ohat = d_on * gain_v
            d_o = r * (d_ohat - ohat * jnp.mean(d_ohat * ohat, axis=1, keepdims=True))

            d_a = jnp.where(row >= col, _dot_nt(d_o, v), 0.0)
            d_at = jnp.where(col >= row, _dot_nt(v, d_o), 0.0)
            d_v = _dot(a.T, d_o) + _dot_nt(k_end, dst_new)
            d_q_in = _dot(d_o, st_prev)
            d_k_end = _dot(v, dst_new)
            d_q = d_q_in * e_g
            d_k = d_k_end * e_end
            diag = jnp.sum(d_o * v, axis=1, keepdims=True)
            d_q = d_q + diag * k
            d_k = d_k + diag * q
            d_dec = [q_in * d_q_in]
            for lv in range(N_LEVELS):
                e = jnp.exp(dec[(lv + 1) * CHUNK:(lv + 2) * CHUNK])
                second, same = _level_masks(lv)
                qt = jnp.where(second, q * e, 0.0)
                kt = jnp.where(second, 0.0, k * e)
                d_qt = _dot(jnp.where(same, d_a, 0.0), kt)
                d_kt = _dot(jnp.where(same, d_at, 0.0), qt)
                d_q = d_q + jnp.where(second, d_qt * e, 0.0)
                d_k = d_k + jnp.where(second, 0.0, d_kt * e)
                d_dec.append(jnp.where(second, qt * d_qt, kt * d_kt))
            d_dec.append(k_end * d_k_end)
            flux = jnp.sum(dst_new * st_prev, axis=0, keepdims=True) * e_last
            d_lf = _exact_dot(stack_t_ref[...], jnp.concatenate(d_dec, axis=0)) + flux
            dst[hh] = dst_new * e_last + _dot(d_o.T, q_in)

            d_f = d_lf / f - d_k
            dlb_ref[hh] += jnp.sum(d_f * sigm, axis=0, keepdims=True)
            dq_ref[rows, lanes] = (d_q * (HGRN_DIM ** -0.5) * (sq * (1.0 + qh * (1.0 - sq)))).astype(BF16)
            df_ref[rows, lanes] = (d_f * (1.0 - lbv) * sig * sigm).astype(BF16)
            di_ref[rows, lanes] = d_v.astype(BF16)
            dg_ref[rows, lanes] = d_gh.astype(BF16)

    last = n_rb - 1

    def col_spec(tt):
        return _spec((t, pair_w), lambda h, rb: (last - rb, (col0 + HGRN_HEADS * tt) // HGRN_PAIR + h))

    head_col = _spec((t, pair_w), lambda h, rb: (last - rb, h))
    rec_col0 = (d_rec.shape[1] - HGRN_WIDTH) // pair_w
    d_rec_col = _spec((t, pair_w), lambda h, rb: (last - rb, rec_col0 + h))
    chunk_spec = _spec((HGRN_PAIR, per, CHUNK, CHUNK), lambda h, rb: (h, last - rb, 0, 0))
    vec_spec = _spec((HGRN_PAIR, 1, HGRN_DIM), lambda h, rb: (h, 0, 0))
    outs = pl.pallas_call(
        body, name=name, grid=(HGRN_HEADS // HGRN_PAIR, n_rb),
        in_specs=[col_spec(0), col_spec(1), col_spec(2), col_spec(3), d_rec_col, head_col,
                  chunk_spec, chunk_spec, vec_spec,
                  _spec((1, HGRN_DIM), lambda h, rb: (0, 0)),
                  _spec(stack.shape, lambda h, rb: (0, 0)), _spec(stack_t.shape, lambda h, rb: (0, 0))],
        out_specs=[head_col] * 4 + [vec_spec, vec_spec],
        out_shape=[jax.ShapeDtypeStruct((SEQ, HGRN_WIDTH), BF16)] * 4
                  + [jax.ShapeDtypeStruct((HGRN_HEADS, 1, HGRN_DIM), F32)] * 2,
        scratch_shapes=[pltpu.VMEM((HGRN_PAIR, CHUNK, CHUNK), F32)],
        compiler_params=_params(2),
    )(proj, proj, proj, proj, d_rec, o_pre, states, scores, lb, gain, stack, stack_t)
    return outs


ANY_SPEC = pl.BlockSpec(memory_space=pl.ANY)


def _my_place():
    return lax.axis_index("x"), lax.axis_index("y"), lax.axis_index("c")


def _other_chips(x, y):
    return [(1 - x, y), (x, 1 - y), (1 - x, 1 - y)]


def _remote(src, dst, send_sem, recv_sem, device):
    return pltpu.make_async_remote_copy(src_ref=src, dst_ref=dst, send_sem=send_sem, recv_sem=recv_sem,
                                        device_id=device, device_id_type=MESH)


def _staged_copies(srcs, dsts, stage, sems):
    loads = [pltpu.make_async_copy(srcs[i], stage[i], sems.at[i]) for i in range(len(srcs))]
    for cp in loads:
        cp.start()
    stores = []
    for i, cp in enumerate(loads):
        cp.wait()
        stores.append(pltpu.make_async_copy(stage[i], dsts[i], sems.at[i]))
        stores[-1].start()
    return stores


def _gather_weights(name, shards):
    n = len(shards)

    def body(*refs):
        ins, outs = refs[:n], refs[n:2 * n]
        ici_send, ici_recv, d2d_send, d2d_recv, local_sems = refs[2 * n:2 * n + 5]
        stage = refs[2 * n + 5:]
        x, y, c = _my_place()
        me = 2 * x + y
        chips = _other_chips(x, y)

        def half(i, which):
            h = ins[i].shape[0] // 2
            return pl.ds(which * h, h)

        sends = []
        for i in range(n):
            for j, (px, py) in enumerate(chips):
                sends.append(_remote(ins[i].at[half(i, c), :], outs[i].at[me, half(i, c), :],
                                     ici_send.at[3 * i + j], ici_recv.at[3 * i + j], (px, py, c)))
        for cp in sends:
            cp.start()
        local = _staged_copies(ins, [outs[i].at[me] for i in range(n)], stage, local_sems)
        for i in range(n):
            for j, (px, py) in enumerate(chips):
                landed = outs[i].at[2 * px + py, half(i, c), :]
                _remote(landed, landed, ici_send.at[3 * i + j], ici_recv.at[3 * i + j], (px, py, c)).wait_recv()
                forward = _remote(landed, landed, d2d_send.at[3 * i + j], d2d_recv.at[3 * i + j], (x, y, 1 - c))
                forward.start()
                sends.append(forward)
        for i in range(n):
            for j, (px, py) in enumerate(chips):
                other = outs[i].at[2 * px + py, half(i, 1 - c), :]
                _remote(other, other, d2d_send.at[3 * i + j], d2d_recv.at[3 * i + j], (x, y, 1 - c)).wait_recv()
        for cp in sends:
            cp.wait_send()
        for cp in local:
            cp.wait()

    return pl.pallas_call(
        body, name=name, in_specs=[ANY_SPEC] * n, out_specs=[ANY_SPEC] * n,
        out_shape=[jax.ShapeDtypeStruct((N_CHIPS,) + s.shape, s.dtype) for s in shards],
        scratch_shapes=([pltpu.SemaphoreType.DMA((3 * n,))] * 4 + [pltpu.SemaphoreType.DMA((n,))]
                        + [pltpu.VMEM(s.shape, s.dtype) for s in shards]),
        compiler_params=pltpu.CompilerParams(vmem_limit_bytes=VMEM_LIMIT),
    )(*shards)


HBM_SPEC = pl.BlockSpec(memory_space=pltpu.HBM)
SEM_SPEC = pl.BlockSpec(memory_space=pltpu.SEMAPHORE)
SPLIT_PARAMS = pltpu.CompilerParams(has_side_effects=pltpu.SideEffectType.DATAFLOW_SIDE_EFFECTING)
N_PEERS = {"gather": N_CHIPS - 1, "scatter": N_DEV - 1}


def _split_copies(ins, lands, send_sems, recv_sems, kind):
    x, y, c = _my_place()
    pairs = []
    for i in range(len(ins)):
        if kind == "gather":
            me = 2 * x + y
            for j, (px, py) in enumerate(_other_chips(x, y)):
                sems = (send_sems.at[3 * i + j], recv_sems.at[3 * i + j], (px, py, c))
                pairs.append((_remote(ins[i], lands[i].at[me], *sems),
                              _remote(ins[i], lands[i].at[2 * px + py], *sems)))
        else:
            me = 4 * x + 2 * y + c
            h = ins[i].shape[1] // 2
            for k in range(1, N_DEV):
                px, py, pc = (x + (k >> 2)) % 2, (y + ((k >> 1) & 1)) % 2, (c + (k & 1)) % 2
                src = ins[i].at[2 * px + py, pl.ds(pc * h, h), :]
                sems = (send_sems.at[7 * i + k - 1], recv_sems.at[7 * i + k - 1], (px, py, pc))
                pairs.append((_remote(src, lands[i].at[me], *sems),
                              _remote(src, lands[i].at[4 * px + 2 * py + pc], *sems)))
    return pairs


def _exchange_start(name, srcs, lands, kind, after=None):
    n = len(srcs)
    n_sems = N_PEERS[kind] * n
    extra = [] if after is None else [after]

    def body(*refs):
        ins, land_refs = refs[:n], refs[n:2 * n]
        send_sems, recv_sems = refs[2 * n + len(extra):2 * n + len(extra) + 2]
        token = refs[-1]
        for send, _ in _split_copies(ins, land_refs, send_sems, recv_sems, kind):
            send.start()
        token[...] = jnp.zeros_like(token)

    arrays = list(srcs) + list(lands)
    outs = pl.pallas_call(
        body, name=name,
        in_specs=[HBM_SPEC] * (2 * n) + [ANY_SPEC] * len(extra),
        out_shape=([pltpu.SemaphoreType.DMA((n_sems,))] * 2 + [pltpu.HBM(a.shape, a.dtype) for a in arrays]
                   + [jax.ShapeDtypeStruct((8, 128), F32)]),
        out_specs=[SEM_SPEC] * 2 + [HBM_SPEC] * (2 * n) + [pl.BlockSpec(memory_space=pltpu.VMEM)],
        input_output_aliases={i: 2 + i for i in range(2 * n)},
        compiler_params=SPLIT_PARAMS,
    )(*[pltpu.with_memory_space_constraint(a, pltpu.HBM) for a in arrays], *extra)
    return outs[:2], outs[2:2 + 2 * n], outs[-1]


def _exchange_wait(name, sems, passed, kind, after):
    n = len(passed) // 2

    def body(*refs):
        ins, land_refs = refs[:n], refs[n:2 * n]
        send_sems, recv_sems = refs[2 * n:2 * n + 2]
        for send, arrive in _split_copies(ins, land_refs, send_sems, recv_sems, kind):
            send.wait_send()
            arrive.wait_recv()

    outs = pl.pallas_call(
        body, name=name,
        in_specs=[HBM_SPEC] * (2 * n) + [SEM_SPEC] * 2 + [ANY_SPEC],
        out_shape=[pltpu.HBM(a.shape, a.dtype) for a in passed],
        out_specs=[HBM_SPEC] * (2 * n),
        input_output_aliases={i: i for i in range(2 * n)},
        compiler_params=SPLIT_PARAMS,
    )(*passed, *sems, after)
    return outs[:n], outs[n:]


def _own_slot(name, own, me):
    r, cc = own.shape
    th = min(r, 512)

    def body(me_ref, x_ref, o_ref):
        del me_ref
        o_ref[...] = x_ref[...]

    grid_spec = pltpu.PrefetchScalarGridSpec(
        num_scalar_prefetch=1, grid=(r // th,),
        in_specs=[pl.BlockSpec((th, cc), lambda i, me_ref: (i, 0))],
        out_specs=pl.BlockSpec((None, th, cc), lambda i, me_ref: (me_ref[0], i, 0)))
    return pl.pallas_call(
        body, name=name, grid_spec=grid_spec,
        out_shape=jax.ShapeDtypeStruct((N_CHIPS, r, cc), own.dtype), compiler_params=_params(1),
    )(me, own)


def _sum_devices(name, landed, own, place):
    n_dev, h, cc = landed.shape
    th = min(h, 256)
    nb = h // th

    def body(place_ref, l_ref, own_ref, o_ref):
        total = None
        for d in range(n_dev):
            piece = jnp.where(place_ref[0] == d, own_ref[...], l_ref[d]).astype(F32)
            total = piece if total is None else total + piece
        o_ref[...] = total

    grid_spec = pltpu.PrefetchScalarGridSpec(
        num_scalar_prefetch=1, grid=(nb,),
        in_specs=[pl.BlockSpec((n_dev, th, cc), lambda i, p: (0, i, 0)),
                  pl.BlockSpec((None, th, cc), lambda i, p: (p[1], p[2] * nb + i, 0))],
        out_specs=pl.BlockSpec((th, cc), lambda i, p: (i, 0)))
    return pl.pallas_call(
        body, name=name, grid_spec=grid_spec,
        out_shape=jax.ShapeDtypeStruct((h, cc), F32), compiler_params=_params(1),
    )(place, landed, own)


def _share_halves(name, halves):
    flat = [t for per_weight in halves for t in per_weight]
    n = len(flat)
    n_w = len(halves)

    def body(*refs):
        ins, outs = refs[:n], refs[n:n + n_w]
        send_sems, recv_sems, local_sems = refs[n + n_w:n + n_w + 3]
        stage = refs[n + n_w + 3:]
        x, y, c = _my_place()
        sends, own = [], []
        for i in range(n):
            w, l = divmod(i, DEPTH)
            h = ins[i].shape[0]
            own.append(outs[w].at[l, pl.ds(c * h, h), :])
            sends.append(_remote(ins[i], own[i], send_sems.at[i], recv_sems.at[i], (x, y, 1 - c)))
        for cp in sends:
            cp.start()
        local = _staged_copies(ins, own, stage, local_sems)
        for i in range(n):
            w, l = divmod(i, DEPTH)
            h = ins[i].shape[0]
            _remote(ins[i], outs[w].at[l, pl.ds((1 - c) * h, h), :], send_sems.at[i], recv_sems.at[i],
                    (x, y, 1 - c)).wait_recv()
        for cp in sends:
            cp.wait_send()
        for cp in local:
            cp.wait()

    return pl.pallas_call(
        body, name=name, in_specs=[ANY_SPEC] * n, out_specs=[ANY_SPEC] * n_w,
        out_shape=[jax.ShapeDtypeStruct((DEPTH, 2 * per_weight[0].shape[0], per_weight[0].shape[1]), F32)
                   for per_weight in halves],
        scratch_shapes=([pltpu.SemaphoreType.DMA((n,))] * 3 + [pltpu.VMEM(t.shape, t.dtype) for t in flat]),
        compiler_params=pltpu.CompilerParams(vmem_limit_bytes=VMEM_LIMIT),
    )(*flat)


def _all_reduce_small(pack, after):
    def body(p_ref, after_ref, o_ref, recv, send_sems, recv_sems):
        del after_ref
        x, y, c = _my_place()
        me = 4 * x + 2 * y + c
        recv[me] = p_ref[...]
        peers = []
        for k in range(1, N_DEV):
            px, py, pc = (x + (k >> 2)) % 2, (y + ((k >> 1) & 1)) % 2, (c + (k & 1)) % 2
            peers.append((px, py, pc))
        sends = [_remote(p_ref, recv.at[me], send_sems.at[k], recv_sems.at[k], peer)
                 for k, peer in enumerate(peers)]
        for cp in sends:
            cp.start()
        for k, (px, py, pc) in enumerate(peers):
            _remote(p_ref, recv.at[4 * px + 2 * py + pc], send_sems.at[k], recv_sems.at[k],
                    (px, py, pc)).wait_recv()
        for cp in sends:
            cp.wait_send()
        total = recv[0]
        for d in range(1, N_DEV):
            total = total + recv[d]
        o_ref[...] = total

    vmem = pl.BlockSpec(memory_space=pltpu.VMEM)
    return pl.pallas_call(
        body, name="all_reduce_small", in_specs=[vmem, ANY_SPEC], out_specs=vmem,
        out_shape=jax.ShapeDtypeStruct(pack.shape, F32),
        scratch_shapes=[pltpu.VMEM((N_DEV,) + pack.shape, F32),
                        pltpu.SemaphoreType.DMA((N_DEV - 1,)), pltpu.SemaphoreType.DMA((N_DEV - 1,))],
    )(pack, after)


def _adamw(name, w, g, m, v):
    r, cc = w.shape
    th = min(r, 256)

    def body(w_ref, g_ref, m_ref, v_ref, d_ref, m_out, v_out):
        gv = g_ref[...]
        m2 = ADAM_B1 * m_ref[...] + (1.0 - ADAM_B1) * gv
        v2 = ADAM_B2 * v_ref[...] + (1.0 - ADAM_B2) * (gv * gv)
        m_hat = m2 / (1.0 - ADAM_B1 ** ADAM_STEP)
        v_hat = v2 / (1.0 - ADAM_B2 ** ADAM_STEP)
        d_ref[...] = -ADAM_LR * (m_hat / (jnp.sqrt(v_hat) + ADAM_EPS) + ADAM_WD * w_ref[...])
        m_out[...] = m2
        v_out[...] = v2

    tile = _spec((th, cc), lambda i: (i, 0))
    return pl.pallas_call(
        body, name=name, grid=(r // th,), in_specs=[tile] * 4, out_specs=[tile] * 3,
        out_shape=[jax.ShapeDtypeStruct((r, cc), F32)] * 3, compiler_params=_params(1),
    )(w, g, m, v)


def _lower_bounds(lb_logits):
    p = jax.nn.softmax(lb_logits.astype(F32), axis=0)
    return jnp.cumsum(p, axis=0) - p[0]


def _layer_forward(l, x_in, small, weights, consts, after=None):
    win, rest = weights
    cos_t, sin_t, stack, _, _ = consts
    tm = MM_TILE
    saved = {"x_in": x_in}

    h, h_t = _rms_fwd(f"norm_mix{l}", x_in, small["norm_mix"][l][None, :], after=after)
    proj = _mm_pieces(f"proj{l}", h, win, False, tm)
    saved.update(h_t=h_t, proj=proj)

    qkv = _attn_prep(f"attn_prep{l}", proj, cos_t, sin_t)
    outs, lses = [], []
    for p, d in enumerate(DILATIONS):
        o, lse = _attn_fwd(f"attn_fwd{l}_{d}", *qkv[p], SEQ // d // SPAN)
        outs.append(o)
        lses.append(lse)
    mixed, mixed_t, attn, lse = _attn_merge(f"attn_merge{l}", outs, lses, small["attn_out_gain"][l][None, :])
    saved.update(qkv=qkv, attn=attn, lse=lse)

    lb3 = small["lower"][l].reshape(HGRN_HEADS, 1, HGRN_DIM)
    mixed, mixed_t, o_pre, states, scores = _hgrn_fwd(f"hgrn_fwd{l}", proj, lb3, small["hgrn_out_gain"][l][None, :],
                                                      stack, mixed, mixed_t)
    wo, wu, wd = rest(mixed)
    saved.update(mixed_t=mixed_t, o_pre=o_pre, states=states, scores=scores, lb3=lb3, weights=(win, wo, wu, wd))

    x_mid = _mm_accum(f"out_proj{l}", mixed, wo, False, tm, x_in)
    saved["x_mid"] = x_mid

    h2, h2_t = _rms_fwd(f"norm_mlp{l}", x_mid, small["norm_mlp"][l][None, :])
    a, relu_u, a_t = _mm_pieces(f"up{l}", h2, wu, False, tm, epilogue="relu2")
    x_out = _mm_accum(f"down{l}", a, wd, False, tm, x_mid)
    saved.update(h2_t=h2_t, relu_u=relu_u, a_t=a_t)
    return x_out, saved


def _layer_backward(l, dx, saved, small, consts, on_grads, after=None):
    win, wo, wu, wd = saved["weights"]
    cos_t, sin_t, stack, stack_t, head_sum = consts
    tm = MM_TILE

    dx, dx_b = dx
    du = _mm_pieces(f"d_u{l}", dx_b, wd, True, tm, epilogue="relu2_grad", extra=saved["relu_u"], after=after)
    d_wd = _mm_dw(f"d_wdown{l}", saved["a_t"], dx_b, False, tm)
    dxm, dxm_b, dg_mlp = _mm_accum(f"d_h2_{l}", du, wu, True, tm, dx,
                                   norm=(saved["x_mid"], small["norm_mlp"][l][None, :]))
    d_wu = _mm_dw(f"d_wup{l}", saved["h2_t"], du, True, tm)
    after_mlp = on_grads(l, "mlp", (d_wu, d_wd))

    d_mixed = _mm_pieces(f"d_mixed{l}", dxm_b, wo, True, tm, after=after_mlp)
    d_wo = _mm_dw(f"d_wout{l}", saved["mixed_t"], dxm_b, False, tm)
    d_rec = d_mixed

    d_out, delta, lses, dg_attn = _attn_bwd_prep(f"attn_bwd_prep{l}", d_mixed, saved["attn"], saved["lse"],
                                                 small["attn_out_gain"][l][None, :], head_sum)
    grads = []
    for p, d in enumerate(DILATIONS):
        grads.append(_attn_bwd(f"attn_bwd{l}_{d}", *saved["qkv"][p], d_out[p], delta[p], lses[p],
                               SEQ // d // SPAN))
    dp_attn = _attn_bwd_post(f"attn_bwd_post{l}", grads, cos_t, sin_t)

    dq_h, df_h, di_h, dg_h, d_lower, dg_hgrn = _hgrn_bwd(
        f"hgrn_bwd{l}", saved["proj"], d_rec, saved["o_pre"], saved["states"], saved["scores"],
        saved["lb3"], small["hgrn_out_gain"][l][None, :], stack, stack_t)
    dproj = jnp.concatenate([dp_attn, dq_h, df_h, di_h, dg_h], axis=1)

    d_win = _mm_dw(f"d_win{l}", saved["h_t"], dproj, True, tm)
    after_mix = on_grads(l, "mix", (d_win, d_wo))
    dx_in, dx_in_b, dg_mix = _mm_accum(f"d_h{l}", dproj, win, True, tm, dxm,
                                       norm=(saved["x_in"], small["norm_mix"][l][None, :]), after=after_mix)

    small_grads = {"norm_mix": dg_mix[0], "attn_out_gain": dg_attn[0],
                   "lower": d_lower.reshape(HGRN_WIDTH),
                   "hgrn_out_gain": jnp.sum(dg_hgrn, axis=0).reshape(HGRN_DIM), "norm_mlp": dg_mlp[0]}
    return (dx_in, dx_in_b), after_mix, small_grads


def _local_step(xs, target, small, get_weights, on_grads):
    consts = _rope_tables() + _hgrn_consts() + (_head_sum_matrix(),)
    stream = xs
    saved = []
    for l in range(DEPTH):
        w, after = get_weights(l, stream)
        stream, s = _layer_forward(l, stream, small, w, consts, after=after)
        saved.append(s)
    dx_f, dx_b, dg_final, loss = _loss_head(stream, small["norm_final"][None, :], target)
    dx = (dx_f, dx_b)
    small_grads = [None] * DEPTH
    after = None
    for l in reversed(range(DEPTH)):
        dx, after, small_grads[l] = _layer_backward(l, dx, saved[l], small, consts, on_grads, after=after)
    return loss, dx[0], dg_final[0], small_grads, after


def _pack_small(norm_mix, attn_out_gain, lb, hgrn_out_gain, norm_mlp, norm_final, last_row):
    rows = [norm_mix, attn_out_gain.reshape(1, D_MODEL), lb.reshape(1, D_MODEL),
            jnp.pad(hgrn_out_gain.reshape(1, DEPTH * HGRN_DIM), ((0, 0), (0, D_MODEL - DEPTH * HGRN_DIM))),
            norm_mlp, norm_final.reshape(1, D_MODEL), last_row.reshape(1, D_MODEL)]
    pack = jnp.concatenate(rows, axis=0)
    return jnp.pad(pack, ((0, PACK_ROWS - pack.shape[0]), (0, 0)))


def _unpack_small(pack):
    return (pack[0:2], pack[2].reshape(DEPTH, ATTN_WIDTH), pack[3].reshape(DEPTH, HGRN_WIDTH),
            pack[4, :DEPTH * HGRN_DIM].reshape(DEPTH, HGRN_DIM), pack[5:7], pack[7], pack[8])


def kernel(x, norm_mix, w_in, attn_out_gain, hgrn_lb_logits, hgrn_out_gain, w_out, norm_mlp, w_up, w_down, norm_final, loss_target, m_norm_mix, m_w_in, m_attn_out_gain, m_hgrn_lb_logits, m_hgrn_out_gain, m_w_out, m_norm_mlp, m_w_up, m_w_down, m_norm_final, v_norm_mix, v_w_in, v_attn_out_gain, v_hgrn_lb_logits, v_hgrn_out_gain, v_w_out, v_norm_mlp, v_w_up, v_w_down, v_norm_final):
    lower, lower_vjp = jax.vjp(_lower_bounds, hgrn_lb_logits)
    small = {"norm_mix": norm_mix, "attn_out_gain": attn_out_gain, "lower": lower,
             "hgrn_out_gain": hgrn_out_gain, "norm_mlp": norm_mlp, "norm_final": norm_final}
    big_w = (w_in, w_out, w_up, w_down)

    x_pos, y_pos, core = lax.axis_index("x"), lax.axis_index("y"), lax.axis_index("c")
    me = (2 * x_pos + y_pos).astype(jnp.int32).reshape(1)
    place = jnp.stack([4 * x_pos + 2 * y_pos + core, 2 * x_pos + y_pos, core]).astype(jnp.int32)
    shards = [[w[l].astype(BF16) for w in big_w] for l in range(DEPTH)]
    in_flight = {}

    def start_gather(name, some, after):
        lands = [_own_slot(f"own_{name}_{i}", s, me) for i, s in enumerate(some)]
        sems, passed, token = _exchange_start(f"start_{name}", some, lands, "gather", after)
        in_flight[name] = (sems, passed)
        return token

    def finish_gather(name, after):
        return _exchange_wait(f"wait_{name}", *in_flight.pop(name), "gather", after)[1]

    def get_weights(l, stream):
        if l == 0:
            (win,) = _gather_weights("gather_w_in0", shards[0][:1])
            token = start_gather("gather_rest0", shards[0][1:], win)
            token = start_gather("gather_w_in1", shards[1][:1], token)
            token = start_gather("gather_rest1", shards[1][1:], token)
            return (win, lambda after: finish_gather("gather_rest0", after)), token
        (win,) = finish_gather("gather_w_in1", stream)
        return (win, lambda after: finish_gather("gather_rest1", after)), None

    reduced = {}

    def start_exchange(name, grads):
        srcs, lands = [g for g, _ in grads], [land for _, land in grads]
        sems, passed, token = _exchange_start(f"start_{name}", srcs, lands, "scatter")
        in_flight[name] = (sems, passed)
        return token

    def finish_exchange(name, after):
        own, landed = _exchange_wait(f"wait_{name}", *in_flight.pop(name), "scatter", after)
        return [_sum_devices(f"sum_{name}_{i}", p, g, place) for i, (p, g) in enumerate(zip(landed, own))]

    def on_grads(l, group, grads):
        if (l, group) == (1, "mlp"):
            return start_exchange("mlp1", grads)
        if (l, group) == (1, "mix"):
            return start_exchange("mix1", grads)
        if (l, group) == (0, "mlp"):
            token = start_exchange("mlp0", grads)
            reduced[(1, "mlp")] = finish_exchange("mlp1", token)
            reduced[(1, "mix")] = finish_exchange("mix1", token)
            return token
        token = start_exchange("mix0", grads)
        reduced[(0, "mlp")] = finish_exchange("mlp0", token)
        return token

    loss, dx, dg_final, sg, last_started = _local_step(x[0], loss_target[0], small, get_weights, on_grads)

    big_m = (m_w_in, m_w_out, m_w_up, m_w_down)
    big_v = (v_w_in, v_w_out, v_w_up, v_w_down)
    names = ("w_in", "w_out", "w_up", "w_down")
    big_g, big_delta, big_new_m, big_new_v = [None] * 4, [None] * 4, [None] * 4, [None] * 4

    def finish_weights(group, which):
        whole = _share_halves(f"share_{group}", [[reduced[(l, group)][i] for l in range(DEPTH)] for i in range(2)])
        for i, w in enumerate(which):
            shape = big_w[w].shape
            flat = lambda arr: arr.reshape(shape[0] * shape[1], shape[2])
            d, m2, v2 = _adamw(f"adamw_{names[w]}", flat(big_w[w]), flat(whole[i]), flat(big_m[w]), flat(big_v[w]))
            big_g[w], big_delta[w] = whole[i], d.reshape(shape)
            big_new_m[w], big_new_v[w] = m2.reshape(shape), v2.reshape(shape)

    finish_weights("mlp", (2, 3))
    reduced[(0, "mix")] = finish_exchange("mix0", big_delta[3])
    finish_weights("mix", (0, 1))

    stack2 = lambda key: jnp.stack([sg[l][key] for l in range(DEPTH)])
    pack = _pack_small(stack2("norm_mix"), stack2("attn_out_gain"), stack2("lower"), stack2("hgrn_out_gain"),
                       stack2("norm_mlp"), dg_final, jnp.broadcast_to(loss[0, 0], (D_MODEL,)))
    g_mix, g_attn, g_lower, g_hgrn, g_mlp, g_final, loss_row = _unpack_small(_all_reduce_small(pack, big_delta[0]))
    (g_logits,) = lower_vjp(g_lower)

    zeros_row = jnp.zeros((D_MODEL,), F32)
    small_w = (norm_mix, attn_out_gain, hgrn_lb_logits, hgrn_out_gain, norm_mlp, norm_final)
    small_m = (m_norm_mix, m_attn_out_gain, m_hgrn_lb_logits, m_hgrn_out_gain, m_norm_mlp, m_norm_final)
    small_v = (v_norm_mix, v_attn_out_gain, v_hgrn_lb_logits, v_hgrn_out_gain, v_norm_mlp, v_norm_final)
    small_g = (g_mix, g_attn, g_logits, g_hgrn, g_mlp, g_final)
    packs = [_pack_small(*t, zeros_row) for t in (small_w, small_g, small_m, small_v)]
    small_delta, small_new_m, small_new_v = [_unpack_small(p)[:6] for p in _adamw("adamw_small", *packs)]

    def ordered(small6, big4):
        mix, attn, lbl, hg, mlp, fin = small6
        return (mix, big4[0], attn, lbl, hg, big4[1], mlp, big4[2], big4[3], fin)

    return ((loss_row[0], dx[None]) + ordered(small_g, big_g) + ordered(small_delta, big_delta)
            + ordered(small_new_m, big_new_m) + ordered(small_new_v, big_new_v))
```

```python
import functools
import math

import numpy as np
import jax
import jax.numpy as jnp
from jax import lax
from jax.experimental import pallas as pl
from jax.experimental.pallas import tpu as pltpu

F32 = jnp.float32
BF16 = jnp.bfloat16
MESH = pl.DeviceIdType.MESH

SEQ = 4096
D_MODEL = 1024
DEPTH = 2
ATTN_WIDTH = 512
HEAD_DIM = 64
HGRN_HEADS = 4
HGRN_DIM = 128
HGRN_WIDTH = 512
IN_W = 3584
MLP_HIDDEN = 4096
N_CHIPS = 4
N_DEV = 8
SHARD_IN = IN_W // N_CHIPS
SHARD_OUT = D_MODEL // N_CHIPS
SHARD_MLP = MLP_HIDDEN // N_CHIPS
DILATIONS = (1, 4, 16)
SPAN = 128
ROPE_THETA = 10000.0
NORM_EPS = 1e-6
MASK_VALUE = -1e30
CHUNK = 128
ROW_TILE = 512
MM_TILE = 512
VMEM_LIMIT = 52 * 1024 * 1024

ADAM_LR = 0.001
ADAM_B1 = 0.9
ADAM_B2 = 0.999
ADAM_EPS = 1e-08
ADAM_WD = 0.01
ADAM_STEP = 10

PACK_ROWS = 16


def _params(n_axes):
    return pltpu.CompilerParams(dimension_semantics=("arbitrary",) * n_axes,
                                vmem_limit_bytes=VMEM_LIMIT)


def _dot(a, b):
    return jnp.dot(a.astype(BF16), b.astype(BF16), preferred_element_type=F32)


def _dot_nt(a, b):
    return lax.dot_general(a.astype(BF16), b.astype(BF16), (((1,), (1,)), ((), ())),
                           preferred_element_type=F32)


def _dot_tn(a, b):
    return lax.dot_general(a.astype(BF16), b.astype(BF16), (((0,), (0,)), ((), ())),
                           preferred_element_type=F32)


def _sigmoid(x):
    return 1.0 / (1.0 + jnp.exp(-x))


def _spec(shape, index_map):
    return pl.BlockSpec(shape, index_map)


def _mm_pieces(name, a, w, nt, tm, epilogue="none", extra=None, after=None):
    s = a.shape[0]
    pw = w.shape[1] if nt else w.shape[2]
    width = N_CHIPS * pw

    def body(a_ref, w_ref, *rest):
        e_ref = rest[0] if extra is not None else None
        outs = rest[-3:] if epilogue == "relu2" else rest[-1:]
        av = a_ref[...].astype(BF16)
        for j in range(N_CHIPS):
            cols = slice(j * pw, (j + 1) * pw)
            r = _dot_nt(av, w_ref[j]) if nt else _dot(av, w_ref[j])
            if epilogue == "relu2":
                relu = jnp.maximum(r, 0.0)
                r = relu * relu
                outs[1][:, cols] = relu.astype(BF16)
                outs[2][cols, :] = r.T.astype(BF16)
            elif epilogue == "relu2_grad":
                r = r * (2.0 * e_ref[:, cols].astype(F32))
            outs[0][:, cols] = r.astype(outs[0].dtype)

    row = lambda width_: _spec((tm, width_), lambda i: (i, 0))
    in_specs = [row(a.shape[1]), _spec(w.shape, lambda i: (0, 0, 0))]
    args = [a, w]
    if extra is not None:
        in_specs.append(row(width))
        args.append(extra)
    if after is not None:
        in_specs.append(pl.BlockSpec(memory_space=pl.ANY))
        args.append(after)
    if epilogue == "relu2":
        out_specs = [row(width), row(width), _spec((width, tm), lambda i: (0, i))]
        out_shape = [jax.ShapeDtypeStruct((s, width), BF16)] * 2 + [jax.ShapeDtypeStruct((width, s), BF16)]
    else:
        out_specs = row(width)
        out_shape = jax.ShapeDtypeStruct((s, width), BF16 if epilogue == "relu2_grad" else F32)
    return pl.pallas_call(body, name=name, grid=(s // tm,), in_specs=in_specs, out_specs=out_specs,
                          out_shape=out_shape, compiler_params=_params(1))(*args)


def _mm_accum(name, a, w, nt, tm, resid, norm=None, after=None):
    pieces = list(a) if isinstance(a, (list, tuple)) else [a]
    n_a = len(pieces)
    s = pieces[0].shape[0]
    pk = w.shape[2] if nt else w.shape[1]
    d = w.shape[1] if nt else w.shape[2]

    def body(*refs):
        a_refs, w_ref, resid_ref, rest = refs[:n_a], refs[n_a], refs[n_a + 1], refs[n_a + 2:]
        av = a_refs[0][...] if n_a == 1 else jnp.concatenate([ref[...] for ref in a_refs], axis=1)
        r = None
        for j in range(N_CHIPS):
            piece = av[:, j * pk:(j + 1) * pk].astype(BF16)
            term = _dot_nt(piece, w_ref[j]) if nt else _dot(piece, w_ref[j])
            r = term if r is None else r + term
        if norm is None:
            rest[-1][...] = r + resid_ref[...]
            return
        x_ref, g_ref = rest[:2]
        dx_ref, dxb_ref, dg_ref = rest[-3:]

        @pl.when(pl.program_id(0) == 0)
        def _():
            dg_ref[...] = jnp.zeros_like(dg_ref)

        xv = x_ref[...]
        rs = lax.rsqrt(jnp.mean(xv * xv, axis=1, keepdims=True) + NORM_EPS)
        xhat = xv * rs
        rg = r * g_ref[...]
        dx = resid_ref[...] + rs * (rg - xhat * jnp.mean(rg * xhat, axis=1, keepdims=True))
        dx_ref[...] = dx
        dxb_ref[...] = dx.astype(BF16)
        dg_ref[...] += jnp.sum(r * xhat, axis=0, keepdims=True)

    row = lambda width: _spec((tm, width), lambda i: (i, 0))
    in_specs = [row(p.shape[1]) for p in pieces] + [_spec(w.shape, lambda i: (0, 0, 0)), row(d)]
    args = pieces + [w, resid]
    if norm is None:
        out_specs, out_shape = row(d), jax.ShapeDtypeStruct((s, d), F32)
    else:
        in_specs += [row(d), _spec((1, d), lambda i: (0, 0))]
        args += list(norm)
        out_specs = [row(d), row(d), _spec((1, d), lambda i: (0, 0))]
        out_shape = [jax.ShapeDtypeStruct((s, d), F32), jax.ShapeDtypeStruct((s, d), BF16),
                     jax.ShapeDtypeStruct((1, d), F32)]
    if after is not None:
        in_specs.append(pl.BlockSpec(memory_space=pl.ANY))
        args.append(after)
    return pl.pallas_call(body, name=name, grid=(s // tm,), in_specs=in_specs, out_specs=out_specs,
                          out_shape=out_shape, compiler_params=_params(1))(*args)


def _mm_dw(name, a_t, b, by_cols, tk):
    pieces = list(b) if isinstance(b, (list, tuple)) else [b]
    n_b = len(pieces)
    m, s = a_t.shape
    n = sum(p.shape[1] for p in pieces)
    shape = (N_CHIPS, m, n // N_CHIPS) if by_cols else (N_CHIPS, m // N_CHIPS, n)
    n_steps = s // tk

    def body(a_ref, *rest):
        b_refs, o_ref, acc = rest[:n_b], rest[n_b], rest[-1]

        @pl.when(pl.program_id(0) == 0)
        def _():
            acc[...] = jnp.zeros_like(acc)

        bv = b_refs[0][...] if n_b == 1 else jnp.concatenate([ref[...] for ref in b_refs], axis=1)
        for j in range(N_CHIPS):
            if by_cols:
                acc[j] += _dot(a_ref[...], bv[:, j * shape[2]:(j + 1) * shape[2]])
            else:
                acc[j] += _dot(a_ref[j * shape[1]:(j + 1) * shape[1], :], bv)

        @pl.when(pl.program_id(0) == n_steps - 1)
        def _():
            o_ref[...] = acc[...].astype(BF16)

    return pl.pallas_call(
        body, name=name, grid=(n_steps,),
        in_specs=[_spec((m, tk), lambda k: (0, k))] + [_spec((tk, p.shape[1]), lambda k: (k, 0)) for p in pieces],
        out_specs=[_spec(shape, lambda k: (0, 0, 0)), ANY_SPEC],
        out_shape=[jax.ShapeDtypeStruct(shape, BF16),
                   jax.ShapeDtypeStruct((N_DEV, shape[1] // 2, shape[2]), BF16)],
        scratch_shapes=[pltpu.VMEM(shape, F32)],
        compiler_params=_params(1))(a_t, *pieces)


def _rms_fwd(name, x, gain, after=None):
    s, d = x.shape
    t = ROW_TILE

    def body(x_ref, g_ref, *rest):
        h_ref, ht_ref = rest[-2:]
        xv = x_ref[...]
        r = lax.rsqrt(jnp.mean(xv * xv, axis=1, keepdims=True) + NORM_EPS)
        h = xv * r * g_ref[...]
        h_ref[...] = h.astype(BF16)
        ht_ref[...] = h.T.astype(BF16)

    in_specs = [_spec((t, d), lambda i: (i, 0)), _spec((1, d), lambda i: (0, 0))]
    args = (x, gain)
    if after is not None:
        in_specs.append(pl.BlockSpec(memory_space=pl.ANY))
        args += (after,)
    return pl.pallas_call(
        body, name=name, grid=(s // t,), in_specs=in_specs,
        out_specs=[_spec((t, d), lambda i: (i, 0)), _spec((d, t), lambda i: (0, i))],
        out_shape=[jax.ShapeDtypeStruct((s, d), BF16), jax.ShapeDtypeStruct((d, s), BF16)],
        compiler_params=_params(1),
    )(*args)


def _loss_head(x, gain, target):
    s, d = x.shape
    t = ROW_TILE
    n_steps = s // t

    def body(x_ref, g_ref, t_ref, dx_ref, dxb_ref, dg_ref, loss_ref, acc):
        i = pl.program_id(0)

        @pl.when(i == 0)
        def _():
            dg_ref[...] = jnp.zeros_like(dg_ref)
            acc[...] = jnp.zeros_like(acc)

        xv = x_ref[...]
        g = g_ref[...]
        r = lax.rsqrt(jnp.mean(xv * xv, axis=1, keepdims=True) + NORM_EPS)
        xhat = xv * r
        err = xhat * g - t_ref[...]
        acc[...] += jnp.sum(err * err, axis=0, keepdims=True)
        dy = err * (1.0 / d)
        dyg = dy * g
        proj = jnp.mean(dyg * xhat, axis=1, keepdims=True)
        dx = r * (dyg - xhat * proj)
        dx_ref[...] = dx
        dxb_ref[...] = dx.astype(BF16)
        dg_ref[...] += jnp.sum(dy * xhat, axis=0, keepdims=True)

        @pl.when(i == n_steps - 1)
        def _():
            total = jnp.sum(acc[...], axis=1, keepdims=True) * (0.5 / d)
            loss_ref[...] = jnp.broadcast_to(total, loss_ref.shape)

    return pl.pallas_call(
        body, name="loss_head", grid=(n_steps,),
        in_specs=[_spec((t, d), lambda i: (i, 0)), _spec((1, d), lambda i: (0, 0)),
                  _spec((t, d), lambda i: (i, 0))],
        out_specs=[_spec((t, d), lambda i: (i, 0)), _spec((t, d), lambda i: (i, 0)),
                   _spec((1, d), lambda i: (0, 0)), _spec((1, 128), lambda i: (0, 0))],
        out_shape=[jax.ShapeDtypeStruct((s, d), F32), jax.ShapeDtypeStruct((s, d), BF16),
                   jax.ShapeDtypeStruct((1, d), F32), jax.ShapeDtypeStruct((1, 128), F32)],
        scratch_shapes=[pltpu.VMEM((1, d), F32)],
        compiler_params=_params(1),
    )(x, gain, target)


def _rope_tables():
    half = HEAD_DIM // 2
    inv_freq = ROPE_THETA ** (-jnp.arange(half, dtype=F32) / half)
    ang = jnp.arange(SEQ, dtype=jnp.int32).astype(F32)[:, None] * inv_freq[None, :]
    cos, sin = jnp.cos(ang), jnp.sin(ang)
    cos_t = jnp.concatenate([cos, cos, cos, cos], axis=1)
    sin_t = jnp.concatenate([-sin, sin, -sin, sin], axis=1)
    return cos_t, sin_t


def _swap_halves(x):
    lane = lax.broadcasted_iota(jnp.int32, x.shape, 1)
    first = (lane % HEAD_DIM) < (HEAD_DIM // 2)
    return jnp.where(first, pltpu.roll(x, 128 - HEAD_DIM // 2, 1), pltpu.roll(x, HEAD_DIM // 2, 1))


def _permuted_specs(t, width):
    specs = [_spec((t, width), lambda i: (i, 0))]
    for d in DILATIONS[1:]:
        specs.append(_spec((d, t // d, width), lambda i: (0, i, 0)))
    return specs


def _permuted_shapes(width, dtype):
    shapes = [jax.ShapeDtypeStruct((SEQ, width), dtype)]
    for d in DILATIONS[1:]:
        shapes.append(jax.ShapeDtypeStruct((d, SEQ // d, width), dtype))
    return shapes


def _attn_prep(name, proj, cos_t, sin_t):
    t = ROW_TILE
    w = ATTN_WIDTH

    def body(q_ref, k_ref, v_ref, cos_ref, sin_ref, *rest):
        outs, scr = rest[:9], rest[9]
        cosv, sinv = cos_ref[...], sin_ref[...]
        for a, (src, roped, scale) in enumerate(((q_ref, True, HEAD_DIM ** -0.5),
                                                 (k_ref, True, 1.0), (v_ref, False, 1.0))):
            o1, o4, o16 = outs[3 * a:3 * a + 3]
            for cb in range(w // 128):
                cols = slice(cb * 128, (cb + 1) * 128)
                val = src[:, cols]
                if roped:
                    val = (val * cosv + _swap_halves(val) * sinv) * scale
                scr[...] = val
                o1[:, cols] = val.astype(BF16)
                for o_ref, d in ((o4, 4), (o16, 16)):
                    for r in range(d):
                        o_ref[r, :, cols] = scr[pl.ds(r, t // d, stride=d), :].astype(BF16)

    out_specs = _permuted_specs(t, w) * 3
    out_shape = _permuted_shapes(w, BF16) * 3
    outs = pl.pallas_call(
        body, name=name, grid=(SEQ // t,),
        in_specs=[_spec((t, w), lambda i: (i, 0)), _spec((t, w), lambda i: (i, 1)),
                  _spec((t, w), lambda i: (i, 2)),
                  _spec((t, 128), lambda i: (i, 0)), _spec((t, 128), lambda i: (i, 0))],
        out_specs=out_specs, out_shape=out_shape,
        scratch_shapes=[pltpu.VMEM((t, 128), F32)],
        compiler_params=_params(1),
    )(proj, proj, proj, cos_t, sin_t)
    q, k, v = outs[0:3], outs[3:6], outs[6:9]
    flat = lambda arr: arr.reshape(SEQ, w)
    return [(flat(q[p]), flat(k[p]), flat(v[p])) for p in range(3)]


def _band_masks():
    row = lax.broadcasted_iota(jnp.int32, (2 * SPAN, 2 * SPAN), 0) % SPAN
    col = lax.broadcasted_iota(jnp.int32, (2 * SPAN, 2 * SPAN), 1)
    is_prev = col < SPAN
    band = (is_prev & (col >= row)) | (~is_prev & (col - SPAN <= row))
    head0 = lax.broadcasted_iota(jnp.int32, (SPAN, 128), 1) < HEAD_DIM
    return band, is_prev, head0


def _stack_heads(x, head0):
    zero = jnp.zeros_like(x)
    return jnp.concatenate([jnp.where(head0, x, zero), jnp.where(head0, zero, x)], axis=0)


def _attn_fwd(name, q, k, v, seg_blocks):
    n_blocks = SEQ // SPAN

    def body(q_ref, k_ref, v_ref, o_ref, lse_ref):
        band, is_prev, head0 = _band_masks()

        def step(b, carry):
            cur = pl.ds(pl.multiple_of(b * SPAN, SPAN), SPAN)
            prev = pl.ds(pl.multiple_of(jnp.maximum(b - 1, 0) * SPAN, SPAN), SPAN)
            qs = _stack_heads(q_ref[cur, :], head0)
            kcat = jnp.concatenate([k_ref[prev, :], k_ref[cur, :]], axis=0)
            vcat = jnp.concatenate([v_ref[prev, :], v_ref[cur, :]], axis=0)
            ok = band & (((b % seg_blocks) != 0) | ~is_prev)
            s = jnp.where(ok, _dot_nt(qs, kcat), MASK_VALUE)
            m = jnp.max(s, axis=1, keepdims=True)
            p = jnp.exp(s - m)
            l = jnp.sum(p, axis=1, keepdims=True)
            pv = _dot(p, vcat) * (1.0 / l)
            lse = m + jnp.log(l)
            o_ref[cur, :] = jnp.where(head0, pv[:SPAN], pv[SPAN:])
            lse_ref[cur, :] = jnp.where(head0, lse[:SPAN], lse[SPAN:])
            return carry

        lax.fori_loop(0, n_blocks, step, 0, unroll=4)

    col = _spec((SEQ, 128), lambda j: (0, j))
    return pl.pallas_call(
        body, name=name, grid=(ATTN_WIDTH // 128,),
        in_specs=[col, col, col], out_specs=[col, col],
        out_shape=[jax.ShapeDtypeStruct((SEQ, ATTN_WIDTH), F32)] * 2,
        compiler_params=_params(1),
    )(q, k, v)


def _unpermute(dst, src_ref, d, cols):
    n = dst.shape[0] // d
    for r in range(d):
        dst[pl.ds(r, n, stride=d), :] = src_ref[r, :, cols].astype(dst.dtype)


def _attn_merge(name, outs, lses, gain):
    t = ROW_TILE
    w = ATTN_WIDTH

    def body(o1, o4, o16, l1, l4, l16, g_ref, an_ref, ant_ref, attn_ref, lse_ref, so4, so16, sl4, sl16):
        for cb in range(w // 128):
            cols = slice(cb * 128, (cb + 1) * 128)
            _unpermute(so4, o4, 4, cols)
            _unpermute(so16, o16, 16, cols)
            _unpermute(sl4, l4, 4, cols)
            _unpermute(sl16, l16, 16, cols)
            la, lb, lc = l1[:, cols], sl4[...], sl16[...]
            m = jnp.maximum(jnp.maximum(la, lb), lc)
            ea, eb, ec = jnp.exp(la - m), jnp.exp(lb - m), jnp.exp(lc - m)
            tot = ea + eb + ec
            attn_ref[:, cols] = (ea * o1[:, cols] + eb * so4[...] + ec * so16[...]) / tot
            lse_ref[:, cols] = m + jnp.log(tot)
        attn = attn_ref[...]
        r = lax.rsqrt(jnp.mean(attn * attn, axis=1, keepdims=True) + NORM_EPS)
        an = attn * r * g_ref[...]
        an_ref[...] = an.astype(BF16)
        ant_ref[...] = an.T.astype(BF16)

    views = lambda arrs: [arrs[0], arrs[1].reshape(4, SEQ // 4, w), arrs[2].reshape(16, SEQ // 16, w)]
    row = _spec((t, w), lambda i: (i, 0))
    return pl.pallas_call(
        body, name=name, grid=(SEQ // t,),
        in_specs=_permuted_specs(t, w) * 2 + [_spec((1, w), lambda i: (0, 0))],
        out_specs=[row, _spec((w, t), lambda i: (0, i)), row, row],
        out_shape=[jax.ShapeDtypeStruct((SEQ, 2 * w), BF16), jax.ShapeDtypeStruct((2 * w, SEQ), BF16),
                   jax.ShapeDtypeStruct((SEQ, w), F32), jax.ShapeDtypeStruct((SEQ, w), F32)],
        scratch_shapes=[pltpu.VMEM((t, 128), F32)] * 4,
        compiler_params=_params(1),
    )(*views(outs), *views(lses), gain)


def _head_sum_matrix():
    i = np.arange(ATTN_WIDTH)
    return jnp.asarray((i[:, None] // HEAD_DIM) == (i[None, :] // HEAD_DIM), dtype=F32)


def _attn_bwd_prep(name, d_an, attn, lse, gain, head_sum):
    t = ROW_TILE
    w = ATTN_WIDTH

    def body(dan_ref, attn_ref, lse_ref, g_ref, hs_ref, *rest):
        (do1, do4, do16, dl1, dl4, dl16, ls4, ls16, dg_ref), (sdo, sdl, sls) = rest[:9], rest[9:]

        @pl.when(pl.program_id(0) == 0)
        def _():
            dg_ref[...] = jnp.zeros_like(dg_ref)

        attn = attn_ref[...]
        dan = dan_ref[...]
        r = lax.rsqrt(jnp.mean(attn * attn, axis=1, keepdims=True) + NORM_EPS)
        xhat = attn * r
        dg_ref[...] += jnp.sum(dan * xhat, axis=0, keepdims=True)
        dang = dan * g_ref[...]
        d_o = r * (dang - xhat * jnp.mean(dang * xhat, axis=1, keepdims=True))
        delta = jnp.dot(d_o * attn, hs_ref[...], preferred_element_type=F32,
                        precision=lax.Precision.HIGHEST)
        do1[...] = d_o.astype(BF16)
        dl1[...] = delta
        for cb in range(w // 128):
            cols = slice(cb * 128, (cb + 1) * 128)
            sdo[...] = d_o[:, cols]
            sdl[...] = delta[:, cols]
            sls[...] = lse_ref[:, cols]
            for d, o_do, o_dl, o_ls in ((4, do4, dl4, ls4), (16, do16, dl16, ls16)):
                for rr in range(d):
                    rows = pl.ds(rr, t // d, stride=d)
                    o_do[rr, :, cols] = sdo[rows, :].astype(BF16)
                    o_dl[rr, :, cols] = sdl[rows, :]
                    o_ls[rr, :, cols] = sls[rows, :]

    row = _spec((t, w), lambda i: (i, 0))
    perm = _permuted_specs(t, w)
    outs = pl.pallas_call(
        body, name=name, grid=(SEQ // t,),
        in_specs=[row, row, row, _spec((1, w), lambda i: (0, 0)), _spec((w, w), lambda i: (0, 0))],
        out_specs=perm + perm + perm[1:] + [_spec((1, w), lambda i: (0, 0))],
        out_shape=(_permuted_shapes(w, BF16) + _permuted_shapes(w, F32) + _permuted_shapes(w, F32)[1:]
                   + [jax.ShapeDtypeStruct((1, w), F32)]),
        scratch_shapes=[pltpu.VMEM((t, 128), F32)] * 3,
        compiler_params=_params(1),
    )(d_an, attn, lse, gain, head_sum)
    flat = lambda arr: arr.reshape(SEQ, w)
    d_out = [flat(a) for a in outs[0:3]]
    delta = [flat(a) for a in outs[3:6]]
    lses = [lse, flat(outs[6]), flat(outs[7])]
    return d_out, delta, lses, outs[8]


def _attn_bwd(name, q, k, v, d_out, delta, lse, seg_blocks):
    n_blocks = SEQ // SPAN

    def body(q_ref, k_ref, v_ref, do_ref, dl_ref, lse_ref, dq_ref, dk_out, dv_out, dk_ref, dv_ref):
        band, is_prev, head0 = _band_masks()
        dk_ref[...] = jnp.zeros_like(dk_ref)
        dv_ref[...] = jnp.zeros_like(dv_ref)

        def per_head(x):
            return jnp.concatenate([x[:, 0:1], x[:, HEAD_DIM:HEAD_DIM + 1]], axis=0)

        def step(b, carry):
            cur = pl.ds(pl.multiple_of(b * SPAN, SPAN), SPAN)
            prev = pl.ds(pl.multiple_of(jnp.maximum(b - 1, 0) * SPAN, SPAN), SPAN)
            qs = _stack_heads(q_ref[cur, :], head0)
            dos = _stack_heads(do_ref[cur, :], head0)
            kcat = jnp.concatenate([k_ref[prev, :], k_ref[cur, :]], axis=0)
            vcat = jnp.concatenate([v_ref[prev, :], v_ref[cur, :]], axis=0)
            ok = band & (((b % seg_blocks) != 0) | ~is_prev)
            p = jnp.where(ok, jnp.exp(_dot_nt(qs, kcat) - per_head(lse_ref[cur, :])), 0.0)
            ds = p * (_dot_nt(dos, vcat) - per_head(dl_ref[cur, :]))
            dq = _dot(ds, kcat)
            dq_ref[cur, :] = jnp.where(head0, dq[:SPAN], dq[SPAN:]).astype(BF16)
            dk = _dot_tn(ds, qs)
            dv = _dot_tn(p, dos)
            dk_ref[prev, :] += dk[:SPAN]
            dv_ref[prev, :] += dv[:SPAN]
            dk_ref[cur, :] += dk[SPAN:]
            dv_ref[cur, :] += dv[SPAN:]
            return carry

        lax.fori_loop(0, n_blocks, step, 0, unroll=4)
        dk_out[...] = dk_ref[...].astype(BF16)
        dv_out[...] = dv_ref[...].astype(BF16)

    col = _spec((SEQ, 128), lambda j: (0, j))
    return pl.pallas_call(
        body, name=name, grid=(ATTN_WIDTH // 128,),
        in_specs=[col] * 6, out_specs=[col] * 3,
        out_shape=[jax.ShapeDtypeStruct((SEQ, ATTN_WIDTH), BF16)] * 3,
        scratch_shapes=[pltpu.VMEM((SEQ, 128), F32)] * 2,
        compiler_params=_params(1),
    )(q, k, v, d_out, delta, lse)


def _attn_bwd_post(name, grads, cos_t, sin_t):
    t = ROW_TILE
    w = ATTN_WIDTH

    def body(*refs):
        ins, cos_ref, sin_ref, out_ref, s4, s16 = refs[:9], refs[9], refs[10], refs[11], refs[12], refs[13]
        cosv, sinv = cos_ref[...], sin_ref[...]
        for a in range(3):
            g1, g4, g16 = ins[a], ins[3 + a], ins[6 + a]
            for cb in range(w // 128):
                cols = slice(cb * 128, (cb + 1) * 128)
                _unpermute(s4, g4, 4, cols)
                _unpermute(s16, g16, 16, cols)
                val = g1[:, cols].astype(F32) + s4[...] + s16[...]
                if a < 2:
                    val = val * cosv + _swap_halves(val * sinv)
                if a == 0:
                    val = val * (HEAD_DIM ** -0.5)
                out_ref[:, a * w + cb * 128:a * w + (cb + 1) * 128] = val.astype(BF16)

    views = []
    for p, d in enumerate(DILATIONS):
        for a in range(3):
            views.append(grads[p][a] if d == 1 else grads[p][a].reshape(d, SEQ // d, w))
    perm = _permuted_specs(t, w)
    in_specs = [perm[0]] * 3 + [perm[1]] * 3 + [perm[2]] * 3
    return pl.pallas_call(
        body, name=name, grid=(SEQ // t,),
        in_specs=in_specs + [_spec((t, 128), lambda i: (i, 0))] * 2,
        out_specs=_spec((t, 3 * w), lambda i: (i, 0)),
        out_shape=jax.ShapeDtypeStruct((SEQ, 3 * w), BF16),
        scratch_shapes=[pltpu.VMEM((t, 128), F32)] * 2,
        compiler_params=_params(1),
    )(*views, cos_t, sin_t)


N_LEVELS = 7
HGRN_PAIR = 2


def _hgrn_consts():
    c = CHUNK
    i = np.arange(c)[:, None]
    s = np.arange(c)[None, :]
    blocks = [s <= i]
    for lv in range(N_LEVELS):
        bs = c >> lv
        h = bs // 2
        m = (i // bs) * bs + h - 1
        second = (i % bs) >= h
        blocks.append((second & (s > m) & (s <= i)) | (~second & (s > i) & (s <= m)))
    blocks.append(s > i)
    stack = np.concatenate(blocks, axis=0).astype(np.float32)
    return jnp.asarray(stack, dtype=BF16), jnp.asarray(stack.T, dtype=BF16)


def _exact_dot(m01, x):
    hi = x.astype(BF16)
    lo = (x - hi.astype(F32)).astype(BF16)
    n = x.shape[1]
    full = jnp.dot(m01, jnp.concatenate([hi, lo], axis=1), preferred_element_type=F32)
    return full[:, :n] + full[:, n:]


def _hgrn_gates(qh, z, lb):
    sq = _sigmoid(qh)
    q = qh * sq * (HGRN_DIM ** -0.5)
    sig = _sigmoid(z)
    sigm = _sigmoid(-z)
    f = lb + (1.0 - lb) * sig
    k = (1.0 - lb) * sigm
    return q, k, f, sq, sig, sigm


def _level_masks(lv):
    row = lax.broadcasted_iota(jnp.int32, (CHUNK, CHUNK), 0)
    col = lax.broadcasted_iota(jnp.int32, (CHUNK, CHUNK), 1)
    shift = N_LEVELS - lv
    second = (row & (CHUNK >> (lv + 1))) != 0
    same = (row >> shift) == (col >> shift)
    return second, same


def _hgrn_fwd(name, proj, lb, gain, stack, mixed, mixed_t):
    t = ROW_TILE
    per = t // CHUNK
    n_rb = SEQ // t
    n_chunks = SEQ // CHUNK
    col0 = 3 * ATTN_WIDTH // 128
    pair_w = HGRN_PAIR * HGRN_DIM

    def body(q_ref, f_ref, i_ref, g_ref, lb_ref, gain_ref, stack_ref, mixed_in, mixed_t_in,
             rec_ref, rect_ref, o_ref, st_out, a_out, st):
        del mixed_in, mixed_t_in

        @pl.when(pl.program_id(1) == 0)
        def _():
            st[...] = jnp.zeros_like(st)

        row = lax.broadcasted_iota(jnp.int32, (CHUNK, CHUNK), 0)
        col = lax.broadcasted_iota(jnp.int32, (CHUNK, CHUNK), 1)
        for c, hh in [(c, hh) for c in range(per) for hh in range(HGRN_PAIR)]:
            rows = slice(c * CHUNK, (c + 1) * CHUNK)
            lanes = slice(hh * HGRN_DIM, (hh + 1) * HGRN_DIM)
            lbv = lb_ref[hh]
            qh, z, v, gh = q_ref[rows, lanes], f_ref[rows, lanes], i_ref[rows, lanes], g_ref[rows, lanes]
            q, k, f, _, _, _ = _hgrn_gates(qh, z, lbv)
            dec = _exact_dot(stack_ref[...], jnp.log(f))
            g = dec[0:CHUNK]
            to_end = dec[(N_LEVELS + 1) * CHUNK:(N_LEVELS + 2) * CHUNK]
            a = jnp.where(row == col, jnp.sum(q * k, axis=1, keepdims=True), 0.0)
            for lv in range(N_LEVELS):
                e = jnp.exp(dec[(lv + 1) * CHUNK:(lv + 2) * CHUNK])
                second, same = _level_masks(lv)
                qt = jnp.where(second, q * e, 0.0)
                kt = jnp.where(second, 0.0, k * e)
                a = a + jnp.where(same, _dot_nt(qt, kt), 0.0)
            st_prev = st[hh]
            st_out[hh, c] = st_prev
            a_out[hh, c] = a
            o = _dot(a, v) + _dot_nt(q * jnp.exp(g), st_prev)
            k_end = k * jnp.exp(to_end)
            st[hh] = st_prev * jnp.exp(g[CHUNK - 1:CHUNK, :]) + _dot(v.T, k_end)
            o_ref[rows, lanes] = o
            r = lax.rsqrt(jnp.mean(o * o, axis=1, keepdims=True) + NORM_EPS)
            rec = o * r * gain_ref[...] * (gh * _sigmoid(gh))
            rec_ref[rows, lanes] = rec.astype(BF16)
            rect_ref[lanes, rows] = rec.T.astype(BF16)

    def col_spec(tt):
        return _spec((t, pair_w), lambda h, rb: (rb, (col0 + HGRN_HEADS * tt) // HGRN_PAIR + h))

    chunk_spec = _spec((HGRN_PAIR, per, CHUNK, CHUNK), lambda h, rb: (h, rb, 0, 0))
    return pl.pallas_call(
        body, name=name, grid=(HGRN_HEADS // HGRN_PAIR, n_rb),
        in_specs=[col_spec(0), col_spec(1), col_spec(2), col_spec(3),
                  _spec((HGRN_PAIR, 1, HGRN_DIM), lambda h, rb: (h, 0, 0)),
                  _spec((1, HGRN_DIM), lambda h, rb: (0, 0)),
                  _spec(stack.shape, lambda h, rb: (0, 0)), ANY_SPEC, ANY_SPEC],
        out_specs=[_spec((t, pair_w), lambda h, rb: (rb, ATTN_WIDTH // pair_w + h)),
                   _spec((pair_w, t), lambda h, rb: (ATTN_WIDTH // pair_w + h, rb)),
                   _spec((t, pair_w), lambda h, rb: (rb, h)),
                   chunk_spec, chunk_spec],
        out_shape=[jax.ShapeDtypeStruct(mixed.shape, BF16),
                   jax.ShapeDtypeStruct(mixed_t.shape, BF16),
                   jax.ShapeDtypeStruct((SEQ, HGRN_WIDTH), F32),
                   jax.ShapeDtypeStruct((HGRN_HEADS, n_chunks, CHUNK, CHUNK), F32),
                   jax.ShapeDtypeStruct((HGRN_HEADS, n_chunks, CHUNK, CHUNK), F32)],
        scratch_shapes=[pltpu.VMEM((HGRN_PAIR, CHUNK, CHUNK), F32)],
        input_output_aliases={7: 0, 8: 1},
        compiler_params=_params(2),
    )(proj, proj, proj, proj, lb, gain, stack, mixed, mixed_t)


def _hgrn_bwd(name, proj, d_rec, o_pre, states, scores, lb, gain, stack, stack_t):
    t = ROW_TILE
    per = t // CHUNK
    n_rb = SEQ // t
    col0 = 3 * ATTN_WIDTH // 128
    pair_w = HGRN_PAIR * HGRN_DIM

    def body(q_ref, f_ref, i_ref, g_ref, drec_ref, o_ref, st_ref, a_ref, lb_ref, gain_ref,
             stack_ref, stack_t_ref, dq_ref, df_ref, di_ref, dg_ref, dlb_ref, dgain_ref, dst):
        @pl.when(pl.program_id(1) == 0)
        def _():
            dst[...] = jnp.zeros_like(dst)
            dlb_ref[...] = jnp.zeros_like(dlb_ref)
            dgain_ref[...] = jnp.zeros_like(dgain_ref)

        gain_v = gain_ref[...]
        row = lax.broadcasted_iota(jnp.int32, (CHUNK, CHUNK), 0)
        col = lax.broadcasted_iota(jnp.int32, (CHUNK, CHUNK), 1)
        for c, hh in [(c, hh) for c in reversed(range(per)) for hh in range(HGRN_PAIR)]:
            rows = slice(c * CHUNK, (c + 1) * CHUNK)
            lanes = slice(hh * HGRN_DIM, (hh + 1) * HGRN_DIM)
            lbv = lb_ref[hh]
            qh, z, v, gh = q_ref[rows, lanes], f_ref[rows, lanes], i_ref[rows, lanes], g_ref[rows, lanes]
            q, k, f, sq, sig, sigm = _hgrn_gates(qh, z, lbv)
            dec = _exact_dot(stack_ref[...], jnp.log(f))
            g = dec[0:CHUNK]
            to_end = dec[(N_LEVELS + 1) * CHUNK:(N_LEVELS + 2) * CHUNK]
            e_g = jnp.exp(g)
            e_end = jnp.exp(to_end)
            e_last = jnp.exp(g[CHUNK - 1:CHUNK, :])
            q_in = q * e_g
            k_end = k * e_end
            st_prev = st_ref[hh, c]
            a = a_ref[hh, c]
            dst_new = dst[hh]

            o = o_ref[rows, lanes]
            drec = drec_ref[rows, lanes]
            sg = _sigmoid(gh)
            r = lax.rsqrt(jnp.mean(o * o, axis=1, keepdims=True) + NORM_EPS)
            ohat = o * r
            d_gh = drec * (ohat * gain_v) * (sg * (1.0 + gh * (1.0 - sg)))
            d_on = drec * (gh * sg)
            dgain_ref[hh] += jnp.sum(d_on * ohat, axis=0, keepdims=True)
            d_ohat = d_on * gain_v
            d_o = r * (d_ohat - ohat * jnp.mean(d_ohat * ohat, axis=1, keepdims=True))

            d_a = jnp.where(row >= col, _dot_nt(d_o, v), 0.0)
            d_at = jnp.where(col >= row, _dot_nt(v, d_o), 0.0)
            d_v = _dot(a.T, d_o) + _dot_nt(k_end, dst_new)
            d_q_in = _dot(d_o, st_prev)
            d_k_end = _dot(v, dst_new)
            d_q = d_q_in * e_g
            d_k = d_k_end * e_end
            diag = jnp.sum(d_o * v, axis=1, keepdims=True)
            d_q = d_q + diag * k
            d_k = d_k + diag * q
            d_dec = [q_in * d_q_in]
            for lv in range(N_LEVELS):
                e = jnp.exp(dec[(lv + 1) * CHUNK:(lv + 2) * CHUNK])
                second, same = _level_masks(lv)
                qt = jnp.where(second, q * e, 0.0)
                kt = jnp.where(second, 0.0, k * e)
                d_qt = _dot(jnp.where(same, d_a, 0.0), kt)
                d_kt = _dot(jnp.where(same, d_at, 0.0), qt)
                d_q = d_q + jnp.where(second, d_qt * e, 0.0)
                d_k = d_k + jnp.where(second, 0.0, d_kt * e)
                d_dec.append(jnp.where(second, qt * d_qt, kt * d_kt))
            d_dec.append(k_end * d_k_end)
            flux = jnp.sum(dst_new * st_prev, axis=0, keepdims=True) * e_last
            d_lf = _exact_dot(stack_t_ref[...], jnp.concatenate(d_dec, axis=0)) + flux
            dst[hh] = dst_new * e_last + _dot(d_o.T, q_in)

            d_f = d_lf / f - d_k
            dlb_ref[hh] += jnp.sum(d_f * sigm, axis=0, keepdims=True)
            dq_ref[rows, lanes] = (d_q * (HGRN_DIM ** -0.5) * (sq * (1.0 + qh * (1.0 - sq)))).astype(BF16)
            df_ref[rows, lanes] = (d_f * (1.0 - lbv) * sig * sigm).astype(BF16)
            di_ref[rows, lanes] = d_v.astype(BF16)
            dg_ref[rows, lanes] = d_gh.astype(BF16)

    last = n_rb - 1

    def col_spec(tt):
        return _spec((t, pair_w), lambda h, rb: (last - rb, (col0 + HGRN_HEADS * tt) // HGRN_PAIR + h))

    head_col = _spec((t, pair_w), lambda h, rb: (last - rb, h))
    rec_col0 = (d_rec.shape[1] - HGRN_WIDTH) // pair_w
    d_rec_col = _spec((t, pair_w), lambda h, rb: (last - rb, rec_col0 + h))
    chunk_spec = _spec((HGRN_PAIR, per, CHUNK, CHUNK), lambda h, rb: (h, last - rb, 0, 0))
    vec_spec = _spec((HGRN_PAIR, 1, HGRN_DIM), lambda h, rb: (h, 0, 0))
    outs = pl.pallas_call(
        body, name=name, grid=(HGRN_HEADS // HGRN_PAIR, n_rb),
        in_specs=[col_spec(0), col_spec(1), col_spec(2), col_spec(3), d_rec_col, head_col,
                  chunk_spec, chunk_spec, vec_spec,
                  _spec((1, HGRN_DIM), lambda h, rb: (0, 0)),
                  _spec(stack.shape, lambda h, rb: (0, 0)), _spec(stack_t.shape, lambda h, rb: (0, 0))],
        out_specs=[head_col] * 4 + [vec_spec, vec_spec],
        out_shape=[jax.ShapeDtypeStruct((SEQ, HGRN_WIDTH), BF16)] * 4
                  + [jax.ShapeDtypeStruct((HGRN_HEADS, 1, HGRN_DIM), F32)] * 2,
        scratch_shapes=[pltpu.VMEM((HGRN_PAIR, CHUNK, CHUNK), F32)],
        compiler_params=_params(2),
    )(proj, proj, proj, proj, d_rec, o_pre, states, scores, lb, gain, stack, stack_t)
    return outs


ANY_SPEC = pl.BlockSpec(memory_space=pl.ANY)


def _my_place():
    return lax.axis_index("x"), lax.axis_index("y"), lax.axis_index("c")


def _other_chips(x, y):
    return [(1 - x, y), (x, 1 - y), (1 - x, 1 - y)]


def _remote(src, dst, send_sem, recv_sem, device):
    return pltpu.make_async_remote_copy(src_ref=src, dst_ref=dst, send_sem=send_sem, recv_sem=recv_sem,
                                        device_id=device, device_id_type=MESH)


def _staged_copies(srcs, dsts, stage, sems):
    loads = [pltpu.make_async_copy(srcs[i], stage[i], sems.at[i]) for i in range(len(srcs))]
    for cp in loads:
        cp.start()
    stores = []
    for i, cp in enumerate(loads):
        cp.wait()
        stores.append(pltpu.make_async_copy(stage[i], dsts[i], sems.at[i]))
        stores[-1].start()
    return stores


def _gather_weights(name, shards):
    n = len(shards)

    def body(*refs):
        ins, outs = refs[:n], refs[n:2 * n]
        ici_send, ici_recv, d2d_send, d2d_recv, local_sems = refs[2 * n:2 * n + 5]
        stage = refs[2 * n + 5:]
        x, y, c = _my_place()
        me = 2 * x + y
        chips = _other_chips(x, y)

        def half(i, which):
            h = ins[i].shape[0] // 2
            return pl.ds(which * h, h)

        sends = []
        for i in range(n):
            for j, (px, py) in enumerate(chips):
                sends.append(_remote(ins[i].at[half(i, c), :], outs[i].at[me, half(i, c), :],
                                     ici_send.at[3 * i + j], ici_recv.at[3 * i + j], (px, py, c)))
        for cp in sends:
            cp.start()
        local = _staged_copies(ins, [outs[i].at[me] for i in range(n)], stage, local_sems)
        for i in range(n):
            for j, (px, py) in enumerate(chips):
                landed = outs[i].at[2 * px + py, half(i, c), :]
                _remote(landed, landed, ici_send.at[3 * i + j], ici_recv.at[3 * i + j], (px, py, c)).wait_recv()
                forward = _remote(landed, landed, d2d_send.at[3 * i + j], d2d_recv.at[3 * i + j], (x, y, 1 - c))
                forward.start()
                sends.append(forward)
        for i in range(n):
            for j, (px, py) in enumerate(chips):
                other = outs[i].at[2 * px + py, half(i, 1 - c), :]
                _remote(other, other, d2d_send.at[3 * i + j], d2d_recv.at[3 * i + j], (x, y, 1 - c)).wait_recv()
        for cp in sends:
            cp.wait_send()
        for cp in local:
            cp.wait()

    return pl.pallas_call(
        body, name=name, in_specs=[ANY_SPEC] * n, out_specs=[ANY_SPEC] * n,
        out_shape=[jax.ShapeDtypeStruct((N_CHIPS,) + s.shape, s.dtype) for s in shards],
        scratch_shapes=([pltpu.SemaphoreType.DMA((3 * n,))] * 4 + [pltpu.SemaphoreType.DMA((n,))]
                        + [pltpu.VMEM(s.shape, s.dtype) for s in shards]),
        compiler_params=pltpu.CompilerParams(vmem_limit_bytes=VMEM_LIMIT),
    )(*shards)


HBM_SPEC = pl.BlockSpec(memory_space=pltpu.HBM)
SEM_SPEC = pl.BlockSpec(memory_space=pltpu.SEMAPHORE)
SPLIT_PARAMS = pltpu.CompilerParams(has_side_effects=pltpu.SideEffectType.DATAFLOW_SIDE_EFFECTING)
N_PEERS = {"gather": N_CHIPS - 1, "scatter": N_DEV - 1}


def _split_copies(ins, lands, send_sems, recv_sems, kind):
    x, y, c = _my_place()
    pairs = []
    for i in range(len(ins)):
        if kind == "gather":
            me = 2 * x + y
            for j, (px, py) in enumerate(_other_chips(x, y)):
                sems = (send_sems.at[3 * i + j], recv_sems.at[3 * i + j], (px, py, c))
                pairs.append((_remote(ins[i], lands[i].at[me], *sems),
                              _remote(ins[i], lands[i].at[2 * px + py], *sems)))
        else:
            me = 4 * x + 2 * y + c
            h = ins[i].shape[1] // 2
            for k in range(1, N_DEV):
                px, py, pc = (x + (k >> 2)) % 2, (y + ((k >> 1) & 1)) % 2, (c + (k & 1)) % 2
                src = ins[i].at[2 * px + py, pl.ds(pc * h, h), :]
                sems = (send_sems.at[7 * i + k - 1], recv_sems.at[7 * i + k - 1], (px, py, pc))
                pairs.append((_remote(src, lands[i].at[me], *sems),
                              _remote(src, lands[i].at[4 * px + 2 * py + pc], *sems)))
    return pairs


def _exchange_start(name, srcs, lands, kind, after=None):
    n = len(srcs)
    n_sems = N_PEERS[kind] * n
    extra = [] if after is None else [after]

    def body(*refs):
        ins, land_refs = refs[:n], refs[n:2 * n]
        send_sems, recv_sems = refs[2 * n + len(extra):2 * n + len(extra) + 2]
        token = refs[-1]
        for send, _ in _split_copies(ins, land_refs, send_sems, recv_sems, kind):
            send.start()
        token[...] = jnp.zeros_like(token)

    arrays = list(srcs) + list(lands)
    outs = pl.pallas_call(
        body, name=name,
        in_specs=[HBM_SPEC] * (2 * n) + [ANY_SPEC] * len(extra),
        out_shape=([pltpu.SemaphoreType.DMA((n_sems,))] * 2 + [pltpu.HBM(a.shape, a.dtype) for a in arrays]
                   + [jax.ShapeDtypeStruct((8, 128), F32)]),
        out_specs=[SEM_SPEC] * 2 + [HBM_SPEC] * (2 * n) + [pl.BlockSpec(memory_space=pltpu.VMEM)],
        input_output_aliases={i: 2 + i for i in range(2 * n)},
        compiler_params=SPLIT_PARAMS,
    )(*[pltpu.with_memory_space_constraint(a, pltpu.HBM) for a in arrays], *extra)
    return outs[:2], outs[2:2 + 2 * n], outs[-1]


def _exchange_wait(name, sems, passed, kind, after):
    n = len(passed) // 2

    def body(*refs):
        ins, land_refs = refs[:n], refs[n:2 * n]
        send_sems, recv_sems = refs[2 * n:2 * n + 2]
        for send, arrive in _split_copies(ins, land_refs, send_sems, recv_sems, kind):
            send.wait_send()
            arrive.wait_recv()

    outs = pl.pallas_call(
        body, name=name,
        in_specs=[HBM_SPEC] * (2 * n) + [SEM_SPEC] * 2 + [ANY_SPEC],
        out_shape=[pltpu.HBM(a.shape, a.dtype) for a in passed],
        out_specs=[HBM_SPEC] * (2 * n),
        input_output_aliases={i: i for i in range(2 * n)},
        compiler_params=SPLIT_PARAMS,
    )(*passed, *sems, after)
    return outs[:n], outs[n:]


def _own_slot(name, own, me):
    r, cc = own.shape
    th = min(r, 512)

    def body(me_ref, x_ref, o_ref):
        del me_ref
        o_ref[...] = x_ref[...]

    grid_spec = pltpu.PrefetchScalarGridSpec(
        num_scalar_prefetch=1, grid=(r // th,),
        in_specs=[pl.BlockSpec((th, cc), lambda i, me_ref: (i, 0))],
        out_specs=pl.BlockSpec((None, th, cc), lambda i, me_ref: (me_ref[0], i, 0)))
    return pl.pallas_call(
        body, name=name, grid_spec=grid_spec,
        out_shape=jax.ShapeDtypeStruct((N_CHIPS, r, cc), own.dtype), compiler_params=_params(1),
    )(me, own)


def _sum_devices(name, landed, own, place):
    n_dev, h, cc = landed.shape
    th = min(h, 256)
    nb = h // th

    def body(place_ref, l_ref, own_ref, o_ref):
        total = None
        for d in range(n_dev):
            piece = jnp.where(place_ref[0] == d, own_ref[...], l_ref[d]).astype(F32)
            total = piece if total is None else total + piece
        o_ref[...] = total

    grid_spec = pltpu.PrefetchScalarGridSpec(
        num_scalar_prefetch=1, grid=(nb,),
        in_specs=[pl.BlockSpec((n_dev, th, cc), lambda i, p: (0, i, 0)),
                  pl.BlockSpec((None, th, cc), lambda i, p: (p[1], p[2] * nb + i, 0))],
        out_specs=pl.BlockSpec((th, cc), lambda i, p: (i, 0)))
    return pl.pallas_call(
        body, name=name, grid_spec=grid_spec,
        out_shape=jax.ShapeDtypeStruct((h, cc), F32), compiler_params=_params(1),
    )(place, landed, own)


def _share_halves(name, halves):
    flat = [t for per_weight in halves for t in per_weight]
    n = len(flat)
    n_w = len(halves)

    def body(*refs):
        ins, outs = refs[:n], refs[n:n + n_w]
        send_sems, recv_sems, local_sems = refs[n + n_w:n + n_w + 3]
        stage = refs[n + n_w + 3:]
        x, y, c = _my_place()
        sends, own = [], []
        for i in range(n):
            w, l = divmod(i, DEPTH)
            h = ins[i].shape[0]
            own.append(outs[w].at[l, pl.ds(c * h, h), :])
            sends.append(_remote(ins[i], own[i], send_sems.at[i], recv_sems.at[i], (x, y, 1 - c)))
        for cp in sends:
            cp.start()
        local = _staged_copies(ins, own, stage, local_sems)
        for i in range(n):
            w, l = divmod(i, DEPTH)
            h = ins[i].shape[0]
            _remote(ins[i], outs[w].at[l, pl.ds((1 - c) * h, h), :], send_sems.at[i], recv_sems.at[i],
                    (x, y, 1 - c)).wait_recv()
        for cp in sends:
            cp.wait_send()
        for cp in local:
            cp.wait()

    return pl.pallas_call(
        body, name=name, in_specs=[ANY_SPEC] * n, out_specs=[ANY_SPEC] * n_w,
        out_shape=[jax.ShapeDtypeStruct((DEPTH, 2 * per_weight[0].shape[0], per_weight[0].shape[1]), F32)
                   for per_weight in halves],
        scratch_shapes=([pltpu.SemaphoreType.DMA((n,))] * 3 + [pltpu.VMEM(t.shape, t.dtype) for t in flat]),
        compiler_params=pltpu.CompilerParams(vmem_limit_bytes=VMEM_LIMIT),
    )(*flat)


def _all_reduce_small(pack, after):
    def body(p_ref, after_ref, o_ref, recv, send_sems, recv_sems):
        del after_ref
        x, y, c = _my_place()
        me = 4 * x + 2 * y + c
        recv[me] = p_ref[...]
        peers = []
        for k in range(1, N_DEV):
            px, py, pc = (x + (k >> 2)) % 2, (y + ((k >> 1) & 1)) % 2, (c + (k & 1)) % 2
            peers.append((px, py, pc))
        sends = [_remote(p_ref, recv.at[me], send_sems.at[k], recv_sems.at[k], peer)
                 for k, peer in enumerate(peers)]
        for cp in sends:
            cp.start()
        for k, (px, py, pc) in enumerate(peers):
            _remote(p_ref, recv.at[4 * px + 2 * py + pc], send_sems.at[k], recv_sems.at[k],
                    (px, py, pc)).wait_recv()
        for cp in sends:
            cp.wait_send()
        total = recv[0]
        for d in range(1, N_DEV):
            total = total + recv[d]
        o_ref[...] = total

    vmem = pl.BlockSpec(memory_space=pltpu.VMEM)
    return pl.pallas_call(
        body, name="all_reduce_small", in_specs=[vmem, ANY_SPEC], out_specs=vmem,
        out_shape=jax.ShapeDtypeStruct(pack.shape, F32),
        scratch_shapes=[pltpu.VMEM((N_DEV,) + pack.shape, F32),
                        pltpu.SemaphoreType.DMA((N_DEV - 1,)), pltpu.SemaphoreType.DMA((N_DEV - 1,))],
    )(pack, after)


def _adamw(name, w, g, m, v):
    r, cc = w.shape
    th = min(r, 256)

    def body(w_ref, g_ref, m_ref, v_ref, d_ref, m_out, v_out):
        gv = g_ref[...]
        m2 = ADAM_B1 * m_ref[...] + (1.0 - ADAM_B1) * gv
        v2 = ADAM_B2 * v_ref[...] + (1.0 - ADAM_B2) * (gv * gv)
        m_hat = m2 / (1.0 - ADAM_B1 ** ADAM_STEP)
        v_hat = v2 / (1.0 - ADAM_B2 ** ADAM_STEP)
        d_ref[...] = -ADAM_LR * (m_hat / (jnp.sqrt(v_hat) + ADAM_EPS) + ADAM_WD * w_ref[...])
        m_out[...] = m2
        v_out[...] = v2

    tile = _spec((th, cc), lambda i: (i, 0))
    return pl.pallas_call(
        body, name=name, grid=(r // th,), in_specs=[tile] * 4, out_specs=[tile] * 3,
        out_shape=[jax.ShapeDtypeStruct((r, cc), F32)] * 3, compiler_params=_params(1),
    )(w, g, m, v)


def _lower_bounds(lb_logits):
    p = jax.nn.softmax(lb_logits.astype(F32), axis=0)
    return jnp.cumsum(p, axis=0) - p[0]


def _layer_forward(l, x_in, small, weights, consts, after=None):
    win, rest = weights
    cos_t, sin_t, stack, _, _ = consts
    tm = MM_TILE
    saved = {"x_in": x_in}

    h, h_t = _rms_fwd(f"norm_mix{l}", x_in, small["norm_mix"][l][None, :], after=after)
    proj = _mm_pieces(f"proj{l}", h, win, False, tm)
    saved.update(h_t=h_t, proj=proj)

    qkv = _attn_prep(f"attn_prep{l}", proj, cos_t, sin_t)
    outs, lses = [], []
    for p, d in enumerate(DILATIONS):
        o, lse = _attn_fwd(f"attn_fwd{l}_{d}", *qkv[p], SEQ // d // SPAN)
        outs.append(o)
        lses.append(lse)
    mixed, mixed_t, attn, lse = _attn_merge(f"attn_merge{l}", outs, lses, small["attn_out_gain"][l][None, :])
    saved.update(qkv=qkv, attn=attn, lse=lse)

    lb3 = small["lower"][l].reshape(HGRN_HEADS, 1, HGRN_DIM)
    mixed, mixed_t, o_pre, states, scores = _hgrn_fwd(f"hgrn_fwd{l}", proj, lb3, small["hgrn_out_gain"][l][None, :],
                                                      stack, mixed, mixed_t)
    wo, wu, wd = rest(mixed)
    saved.update(mixed_t=mixed_t, o_pre=o_pre, states=states, scores=scores, lb3=lb3, weights=(win, wo, wu, wd))

    x_mid = _mm_accum(f"out_proj{l}", mixed, wo, False, tm, x_in)
    saved["x_mid"] = x_mid

    h2, h2_t = _rms_fwd(f"norm_mlp{l}", x_mid, small["norm_mlp"][l][None, :])
    a, relu_u, a_t = _mm_pieces(f"up{l}", h2, wu, False, tm, epilogue="relu2")
    x_out = _mm_accum(f"down{l}", a, wd, False, tm, x_mid)
    saved.update(h2_t=h2_t, relu_u=relu_u, a_t=a_t)
    return x_out, saved


def _layer_backward(l, dx, saved, small, consts, on_grads, after=None):
    win, wo, wu, wd = saved["weights"]
    cos_t, sin_t, stack, stack_t, head_sum = consts
    tm = MM_TILE

    dx, dx_b = dx
    du = _mm_pieces(f"d_u{l}", dx_b, wd, True, tm, epilogue="relu2_grad", extra=saved["relu_u"], after=after)
    d_wd = _mm_dw(f"d_wdown{l}", saved["a_t"], dx_b, False, tm)
    dxm, dxm_b, dg_mlp = _mm_accum(f"d_h2_{l}", du, wu, True, tm, dx,
                                   norm=(saved["x_mid"], small["norm_mlp"][l][None, :]))
    d_wu = _mm_dw(f"d_wup{l}", saved["h2_t"], du, True, tm)
    after_mlp = on_grads(l, "mlp", (d_wu, d_wd))

    d_mixed = _mm_pieces(f"d_mixed{l}", dxm_b, wo, True, tm, after=after_mlp)
    d_wo = _mm_dw(f"d_wout{l}", saved["mixed_t"], dxm_b, False, tm)
    d_rec = d_mixed

    d_out, delta, lses, dg_attn = _attn_bwd_prep(f"attn_bwd_prep{l}", d_mixed, saved["attn"], saved["lse"],
                                                 small["attn_out_gain"][l][None, :], head_sum)
    grads = []
    for p, d in enumerate(DILATIONS):
        grads.append(_attn_bwd(f"attn_bwd{l}_{d}", *saved["qkv"][p], d_out[p], delta[p], lses[p],
                               SEQ // d // SPAN))
    dp_attn = _attn_bwd_post(f"attn_bwd_post{l}", grads, cos_t, sin_t)

    dq_h, df_h, di_h, dg_h, d_lower, dg_hgrn = _hgrn_bwd(
        f"hgrn_bwd{l}", saved["proj"], d_rec, saved["o_pre"], saved["states"], saved["scores"],
        saved["lb3"], small["hgrn_out_gain"][l][None, :], stack, stack_t)
    dproj = [dp_attn, dq_h, df_h, di_h, dg_h]

    d_win = _mm_dw(f"d_win{l}", saved["h_t"], dproj, True, tm)
    after_mix = on_grads(l, "mix", (d_win, d_wo))
    dx_in, dx_in_b, dg_mix = _mm_accum(f"d_h{l}", dproj, win, True, tm, dxm,
                                       norm=(saved["x_in"], small["norm_mix"][l][None, :]), after=after_mix)

    small_grads = {"norm_mix": dg_mix[0], "attn_out_gain": dg_attn[0],
                   "lower": d_lower.reshape(HGRN_WIDTH),
                   "hgrn_out_gain": jnp.sum(dg_hgrn, axis=0).reshape(HGRN_DIM), "norm_mlp": dg_mlp[0]}
    return (dx_in, dx_in_b), after_mix, small_grads


def _local_step(xs, target, small, get_weights, on_grads):
    consts = _rope_tables() + _hgrn_consts() + (_head_sum_matrix(),)
    stream = xs
    saved = []
    for l in range(DEPTH):
        w, after = get_weights(l, stream)
        stream, s = _layer_forward(l, stream, small, w, consts, after=after)
        saved.append(s)
    dx_f, dx_b, dg_final, loss = _loss_head(stream, small["norm_final"][None, :], target)
    dx = (dx_f, dx_b)
    small_grads = [None] * DEPTH
    after = None
    for l in reversed(range(DEPTH)):
        dx, after, small_grads[l] = _layer_backward(l, dx, saved[l], small, consts, on_grads, after=after)
    return loss, dx[0], dg_final[0], small_grads, after


def _pack_small(norm_mix, attn_out_gain, lb, hgrn_out_gain, norm_mlp, norm_final, last_row):
    rows = [norm_mix, attn_out_gain.reshape(1, D_MODEL), lb.reshape(1, D_MODEL),
            jnp.pad(hgrn_out_gain.reshape(1, DEPTH * HGRN_DIM), ((0, 0), (0, D_MODEL - DEPTH * HGRN_DIM))),
            norm_mlp, norm_final.reshape(1, D_MODEL), last_row.reshape(1, D_MODEL)]
    pack = jnp.concatenate(rows, axis=0)
    return jnp.pad(pack, ((0, PACK_ROWS - pack.shape[0]), (0, 0)))


def _unpack_small(pack):
    return (pack[0:2], pack[2].reshape(DEPTH, ATTN_WIDTH), pack[3].reshape(DEPTH, HGRN_WIDTH),
            pack[4, :DEPTH * HGRN_DIM].reshape(DEPTH, HGRN_DIM), pack[5:7], pack[7], pack[8])


def kernel(x, norm_mix, w_in, attn_out_gain, hgrn_lb_logits, hgrn_out_gain, w_out, norm_mlp, w_up, w_down, norm_final, loss_target, m_norm_mix, m_w_in, m_attn_out_gain, m_hgrn_lb_logits, m_hgrn_out_gain, m_w_out, m_norm_mlp, m_w_up, m_w_down, m_norm_final, v_norm_mix, v_w_in, v_attn_out_gain, v_hgrn_lb_logits, v_hgrn_out_gain, v_w_out, v_norm_mlp, v_w_up, v_w_down, v_norm_final):
    lower, lower_vjp = jax.vjp(_lower_bounds, hgrn_lb_logits)
    small = {"norm_mix": norm_mix, "attn_out_gain": attn_out_gain, "lower": lower,
             "hgrn_out_gain": hgrn_out_gain, "norm_mlp": norm_mlp, "norm_final": norm_final}
    big_w = (w_in, w_out, w_up, w_down)

    x_pos, y_pos, core = lax.axis_index("x"), lax.axis_index("y"), lax.axis_index("c")
    me = (2 * x_pos + y_pos).astype(jnp.int32).reshape(1)
    place = jnp.stack([4 * x_pos + 2 * y_pos + core, 2 * x_pos + y_pos, core]).astype(jnp.int32)
    shards = [[w[l].astype(BF16) for w in big_w] for l in range(DEPTH)]
    in_flight = {}

    def start_gather(name, some, after):
        lands = [_own_slot(f"own_{name}_{i}", s, me) for i, s in enumerate(some)]
        sems, passed, token = _exchange_start(f"start_{name}", some, lands, "gather", after)
        in_flight[name] = (sems, passed)
        return token

    def finish_gather(name, after):
        return _exchange_wait(f"wait_{name}", *in_flight.pop(name), "gather", after)[1]

    def get_weights(l, stream):
        if l == 0:
            (win,) = _gather_weights("gather_w_in0", shards[0][:1])
            token = start_gather("gather_rest0", shards[0][1:], win)
            token = start_gather("gather_w_in1", shards[1][:1], token)
            token = start_gather("gather_rest1", shards[1][1:], token)
            return (win, lambda after: finish_gather("gather_rest0", after)), token
        (win,) = finish_gather("gather_w_in1", stream)
        return (win, lambda after: finish_gather("gather_rest1", after)), None

    reduced = {}

    def start_exchange(name, grads):
        srcs, lands = [g for g, _ in grads], [land for _, land in grads]
        sems, passed, token = _exchange_start(f"start_{name}", srcs, lands, "scatter")
        in_flight[name] = (sems, passed)
        return token

    def finish_exchange(name, after):
        own, landed = _exchange_wait(f"wait_{name}", *in_flight.pop(name), "scatter", after)
        return [_sum_devices(f"sum_{name}_{i}", p, g, place) for i, (p, g) in enumerate(zip(landed, own))]

    def on_grads(l, group, grads):
        if (l, group) == (1, "mlp"):
            return start_exchange("mlp1", grads)
        if (l, group) == (1, "mix"):
            return start_exchange("mix1", grads)
        if (l, group) == (0, "mlp"):
            token = start_exchange("mlp0", grads)
            reduced[(1, "mlp")] = finish_exchange("mlp1", token)
            reduced[(1, "mix")] = finish_exchange("mix1", token)
            return token
        token = start_exchange("mix0", grads)
        reduced[(0, "mlp")] = finish_exchange("mlp0", token)
        return token

    loss, dx, dg_final, sg, last_started = _local_step(x[0], loss_target[0], small, get_weights, on_grads)

    big_m = (m_w_in, m_w_out, m_w_up, m_w_down)
    big_v = (v_w_in, v_w_out, v_w_up, v_w_down)
    names = ("w_in", "w_out", "w_up", "w_down")
    big_g, big_delta, big_new_m, big_new_v = [None] * 4, [None] * 4, [None] * 4, [None] * 4

    def finish_weights(group, which):
        whole = _share_halves(f"share_{group}", [[reduced[(l, group)][i] for l in range(DEPTH)] for i in range(2)])
        for i, w in enumerate(which):
            shape = big_w[w].shape
            flat = lambda arr: arr.reshape(shape[0] * shape[1], shape[2])
            d, m2, v2 = _adamw(f"adamw_{names[w]}", flat(big_w[w]), flat(whole[i]), flat(big_m[w]), flat(big_v[w]))
            big_g[w], big_delta[w] = whole[i], d.reshape(shape)
            big_new_m[w], big_new_v[w] = m2.reshape(shape), v2.reshape(shape)

    finish_weights("mlp", (2, 3))
    reduced[(0, "mix")] = finish_exchange("mix0", big_delta[3])
    finish_weights("mix", (0, 1))

    stack2 = lambda key: jnp.stack([sg[l][key] for l in range(DEPTH)])
    pack = _pack_small(stack2("norm_mix"), stack2("attn_out_gain"), stack2("lower"), stack2("hgrn_out_gain"),
                       stack2("norm_mlp"), dg_final, jnp.broadcast_to(loss[0, 0], (D_MODEL,)))
    g_mix, g_attn, g_lower, g_hgrn, g_mlp, g_final, loss_row = _unpack_small(_all_reduce_small(pack, big_delta[0]))
    (g_logits,) = lower_vjp(g_lower)

    zeros_row = jnp.zeros((D_MODEL,), F32)
    small_w = (norm_mix, attn_out_gain, hgrn_lb_logits, hgrn_out_gain, norm_mlp, norm_final)
    small_m = (m_norm_mix, m_attn_out_gain, m_hgrn_lb_logits, m_hgrn_out_gain, m_norm_mlp, m_norm_final)
    small_v = (v_norm_mix, v_attn_out_gain, v_hgrn_lb_logits, v_hgrn_out_gain, v_norm_mlp, v_norm_final)
    small_g = (g_mix, g_attn, g_logits, g_hgrn, g_mlp, g_final)
    packs = [_pack_small(*t, zeros_row) for t in (small_w, small_g, small_m, small_v)]
    small_delta, small_new_m, small_new_v = [_unpack_small(p)[:6] for p in _adamw("adamw_small", *packs)]

    def ordered(small6, big4):
        mix, attn, lbl, hg, mlp, fin = small6
        return (mix, big4[0], attn, lbl, hg, big4[1], mlp, big4[2], big4[3], fin)

    return ((loss_row[0], dx[None]) + ordered(small_g, big_g) + ordered(small_delta, big_delta)
            + ordered(small_new_m, big_new_m) + ordered(small_new_v, big_new_v))
```

```python
import functools
import math

import numpy as np
import jax
import jax.numpy as jnp
from jax import lax
from jax.experimental import pallas as pl
from jax.experimental.pallas import tpu as pltpu

F32 = jnp.float32
BF16 = jnp.bfloat16
MESH = pl.DeviceIdType.MESH

SEQ = 4096
D_MODEL = 1024
DEPTH = 2
ATTN_WIDTH = 512
HEAD_DIM = 64
HGRN_HEADS = 4
HGRN_DIM = 128
HGRN_WIDTH = 512
IN_W = 3584
MLP_HIDDEN = 4096
N_CHIPS = 4
N_DEV = 8
SHARD_IN = IN_W // N_CHIPS
SHARD_OUT = D_MODEL // N_CHIPS
SHARD_MLP = MLP_HIDDEN // N_CHIPS
DILATIONS = (1, 4, 16)
SPAN = 128
ROPE_THETA = 10000.0
NORM_EPS = 1e-6
MASK_VALUE = -1e30
CHUNK = 128
ROW_TILE = 512
MM_TILE = 512
VMEM_LIMIT = 52 * 1024 * 1024

ADAM_LR = 0.001
ADAM_B1 = 0.9
ADAM_B2 = 0.999
ADAM_EPS = 1e-08
ADAM_WD = 0.01
ADAM_STEP = 10

PACK_ROWS = 16


def _params(n_axes):
    return pltpu.CompilerParams(dimension_semantics=("arbitrary",) * n_axes,
                                vmem_limit_bytes=VMEM_LIMIT)


def _dot(a, b):
    return jnp.dot(a.astype(BF16), b.astype(BF16), preferred_element_type=F32)


def _dot_nt(a, b):
    return lax.dot_general(a.astype(BF16), b.astype(BF16), (((1,), (1,)), ((), ())),
                           preferred_element_type=F32)


def _dot_tn(a, b):
    return lax.dot_general(a.astype(BF16), b.astype(BF16), (((0,), (0,)), ((), ())),
                           preferred_element_type=F32)


def _sigmoid(x):
    return 1.0 / (1.0 + jnp.exp(-x))


def _spec(shape, index_map):
    return pl.BlockSpec(shape, index_map)


def _mm_pieces(name, a, w, nt, tm, epilogue="none", extra=None, after=None):
    s = a.shape[0]
    pw = w.shape[1] if nt else w.shape[2]
    width = N_CHIPS * pw

    def body(a_ref, w_ref, *rest):
        e_ref = rest[0] if extra is not None else None
        outs = rest[-3:] if epilogue == "relu2" else rest[-1:]
        av = a_ref[...].astype(BF16)
        for j in range(N_CHIPS):
            cols = slice(j * pw, (j + 1) * pw)
            r = _dot_nt(av, w_ref[j]) if nt else _dot(av, w_ref[j])
            if epilogue == "relu2":
                relu = jnp.maximum(r, 0.0)
                r = relu * relu
                outs[1][:, cols] = relu.astype(BF16)
                outs[2][cols, :] = r.T.astype(BF16)
            elif epilogue == "relu2_grad":
                r = r * (2.0 * e_ref[:, cols].astype(F32))
            outs[0][:, cols] = r.astype(outs[0].dtype)

    row = lambda width_: _spec((tm, width_), lambda i: (i, 0))
    in_specs = [row(a.shape[1]), _spec(w.shape, lambda i: (0, 0, 0))]
    args = [a, w]
    if extra is not None:
        in_specs.append(row(width))
        args.append(extra)
    if after is not None:
        in_specs.append(pl.BlockSpec(memory_space=pl.ANY))
        args.append(after)
    if epilogue == "relu2":
        out_specs = [row(width), row(width), _spec((width, tm), lambda i: (0, i))]
        out_shape = [jax.ShapeDtypeStruct((s, width), BF16)] * 2 + [jax.ShapeDtypeStruct((width, s), BF16)]
    else:
        out_specs = row(width)
        out_shape = jax.ShapeDtypeStruct((s, width), BF16 if epilogue == "relu2_grad" else F32)
    return pl.pallas_call(body, name=name, grid=(s // tm,), in_specs=in_specs, out_specs=out_specs,
                          out_shape=out_shape, compiler_params=_params(1))(*args)


def _mm_accum(name, a, w, nt, tm, resid, norm=None, after=None, next_gain=None):
    pieces = list(a) if isinstance(a, (list, tuple)) else [a]
    n_a = len(pieces)
    s = pieces[0].shape[0]
    pk = w.shape[2] if nt else w.shape[1]
    d = w.shape[1] if nt else w.shape[2]

    def body(*refs):
        a_refs, w_ref, resid_ref, rest = refs[:n_a], refs[n_a], refs[n_a + 1], refs[n_a + 2:]
        av = a_refs[0][...] if n_a == 1 else jnp.concatenate([ref[...] for ref in a_refs], axis=1)
        r = None
        for j in range(N_CHIPS):
            piece = av[:, j * pk:(j + 1) * pk].astype(BF16)
            term = _dot_nt(piece, w_ref[j]) if nt else _dot(piece, w_ref[j])
            r = term if r is None else r + term
        if norm is None and next_gain is None:
            rest[-1][...] = r + resid_ref[...]
            return
        if norm is None:
            g_ref = rest[0]
            x_out, h_out, ht_out = rest[-3:]
            xv = r + resid_ref[...]
            x_out[...] = xv
            h = xv * lax.rsqrt(jnp.mean(xv * xv, axis=1, keepdims=True) + NORM_EPS) * g_ref[...]
            h_out[...] = h.astype(BF16)
            ht_out[...] = h.T.astype(BF16)
            return
        x_ref, g_ref = rest[:2]
        dx_ref, dxb_ref, dg_ref = rest[-3:]

        @pl.when(pl.program_id(0) == 0)
        def _():
            dg_ref[...] = jnp.zeros_like(dg_ref)

        xv = x_ref[...]
        rs = lax.rsqrt(jnp.mean(xv * xv, axis=1, keepdims=True) + NORM_EPS)
        xhat = xv * rs
        rg = r * g_ref[...]
        dx = resid_ref[...] + rs * (rg - xhat * jnp.mean(rg * xhat, axis=1, keepdims=True))
        dx_ref[...] = dx
        dxb_ref[...] = dx.astype(BF16)
        dg_ref[...] += jnp.sum(r * xhat, axis=0, keepdims=True)

    row = lambda width: _spec((tm, width), lambda i: (i, 0))
    in_specs = [row(p.shape[1]) for p in pieces] + [_spec(w.shape, lambda i: (0, 0, 0)), row(d)]
    args = pieces + [w, resid]
    if norm is None and next_gain is None:
        out_specs, out_shape = row(d), jax.ShapeDtypeStruct((s, d), F32)
    elif norm is None:
        in_specs.append(_spec((1, d), lambda i: (0, 0)))
        args.append(next_gain)
        out_specs = [row(d), row(d), _spec((d, tm), lambda i: (0, i))]
        out_shape = [jax.ShapeDtypeStruct((s, d), F32), jax.ShapeDtypeStruct((s, d), BF16),
                     jax.ShapeDtypeStruct((d, s), BF16)]
    else:
        in_specs += [row(d), _spec((1, d), lambda i: (0, 0))]
        args += list(norm)
        out_specs = [row(d), row(d), _spec((1, d), lambda i: (0, 0))]
        out_shape = [jax.ShapeDtypeStruct((s, d), F32), jax.ShapeDtypeStruct((s, d), BF16),
                     jax.ShapeDtypeStruct((1, d), F32)]
    if after is not None:
        in_specs.append(pl.BlockSpec(memory_space=pl.ANY))
        args.append(after)
    return pl.pallas_call(body, name=name, grid=(s // tm,), in_specs=in_specs, out_specs=out_specs,
                          out_shape=out_shape, compiler_params=_params(1))(*args)


def _mm_dw(name, a_t, b, by_cols, tk):
    pieces = list(b) if isinstance(b, (list, tuple)) else [b]
    n_b = len(pieces)
    m, s = a_t.shape
    n = sum(p.shape[1] for p in pieces)
    shape = (N_CHIPS, m, n // N_CHIPS) if by_cols else (N_CHIPS, m // N_CHIPS, n)
    n_steps = s // tk

    def body(a_ref, *rest):
        b_refs, o_ref, acc = rest[:n_b], rest[n_b], rest[-1]

        @pl.when(pl.program_id(0) == 0)
        def _():
            acc[...] = jnp.zeros_like(acc)

        bv = b_refs[0][...] if n_b == 1 else jnp.concatenate([ref[...] for ref in b_refs], axis=1)
        for j in range(N_CHIPS):
            if by_cols:
                acc[j] += _dot(a_ref[...], bv[:, j * shape[2]:(j + 1) * shape[2]])
            else:
                acc[j] += _dot(a_ref[j * shape[1]:(j + 1) * shape[1], :], bv)

        @pl.when(pl.program_id(0) == n_steps - 1)
        def _():
            o_ref[...] = acc[...].astype(BF16)

    return pl.pallas_call(
        body, name=name, grid=(n_steps,),
        in_specs=[_spec((m, tk), lambda k: (0, k))] + [_spec((tk, p.shape[1]), lambda k: (k, 0)) for p in pieces],
        out_specs=[_spec(shape, lambda k: (0, 0, 0)), ANY_SPEC],
        out_shape=[jax.ShapeDtypeStruct(shape, BF16),
                   jax.ShapeDtypeStruct((N_DEV, shape[1] // 2, shape[2]), BF16)],
        scratch_shapes=[pltpu.VMEM(shape, F32)],
        compiler_params=_params(1))(a_t, *pieces)


def _rms_fwd(name, x, gain, after=None):
    s, d = x.shape
    t = ROW_TILE

    def body(x_ref, g_ref, *rest):
        h_ref, ht_ref = rest[-2:]
        xv = x_ref[...]
        r = lax.rsqrt(jnp.mean(xv * xv, axis=1, keepdims=True) + NORM_EPS)
        h = xv * r * g_ref[...]
        h_ref[...] = h.astype(BF16)
        ht_ref[...] = h.T.astype(BF16)

    in_specs = [_spec((t, d), lambda i: (i, 0)), _spec((1, d), lambda i: (0, 0))]
    args = (x, gain)
    if after is not None:
        in_specs.append(pl.BlockSpec(memory_space=pl.ANY))
        args += (after,)
    return pl.pallas_call(
        body, name=name, grid=(s // t,), in_specs=in_specs,
        out_specs=[_spec((t, d), lambda i: (i, 0)), _spec((d, t), lambda i: (0, i))],
        out_shape=[jax.ShapeDtypeStruct((s, d), BF16), jax.ShapeDtypeStruct((d, s), BF16)],
        compiler_params=_params(1),
    )(*args)


def _loss_head(x, gain, target):
    s, d = x.shape
    t = ROW_TILE
    n_steps = s // t

    def body(x_ref, g_ref, t_ref, dx_ref, dxb_ref, dg_ref, loss_ref, acc):
        i = pl.program_id(0)

        @pl.when(i == 0)
        def _():
            dg_ref[...] = jnp.zeros_like(dg_ref)
            acc[...] = jnp.zeros_like(acc)

        xv = x_ref[...]
        g = g_ref[...]
        r = lax.rsqrt(jnp.mean(xv * xv, axis=1, keepdims=True) + NORM_EPS)
        xhat = xv * r
        err = xhat * g - t_ref[...]
        acc[...] += jnp.sum(err * err, axis=0, keepdims=True)
        dy = err * (1.0 / d)
        dyg = dy * g
        proj = jnp.mean(dyg * xhat, axis=1, keepdims=True)
        dx = r * (dyg - xhat * proj)
        dx_ref[...] = dx
        dxb_ref[...] = dx.astype(BF16)
        dg_ref[...] += jnp.sum(dy * xhat, axis=0, keepdims=True)

        @pl.when(i == n_steps - 1)
        def _():
            total = jnp.sum(acc[...], axis=1, keepdims=True) * (0.5 / d)
            loss_ref[...] = jnp.broadcast_to(total, loss_ref.shape)

    return pl.pallas_call(
        body, name="loss_head", grid=(n_steps,),
        in_specs=[_spec((t, d), lambda i: (i, 0)), _spec((1, d), lambda i: (0, 0)),
                  _spec((t, d), lambda i: (i, 0))],
        out_specs=[_spec((t, d), lambda i: (i, 0)), _spec((t, d), lambda i: (i, 0)),
                   _spec((1, d), lambda i: (0, 0)), _spec((1, 128), lambda i: (0, 0))],
        out_shape=[jax.ShapeDtypeStruct((s, d), F32), jax.ShapeDtypeStruct((s, d), BF16),
                   jax.ShapeDtypeStruct((1, d), F32), jax.ShapeDtypeStruct((1, 128), F32)],
        scratch_shapes=[pltpu.VMEM((1, d), F32)],
        compiler_params=_params(1),
    )(x, gain, target)


def _rope_tables():
    half = HEAD_DIM // 2
    inv_freq = ROPE_THETA ** (-jnp.arange(half, dtype=F32) / half)
    ang = jnp.arange(SEQ, dtype=jnp.int32).astype(F32)[:, None] * inv_freq[None, :]
    cos, sin = jnp.cos(ang), jnp.sin(ang)
    cos_t = jnp.concatenate([cos, cos, cos, cos], axis=1)
    sin_t = jnp.concatenate([-sin, sin, -sin, sin], axis=1)
    return cos_t, sin_t


def _swap_halves(x):
    lane = lax.broadcasted_iota(jnp.int32, x.shape, 1)
    first = (lane % HEAD_DIM) < (HEAD_DIM // 2)
    return jnp.where(first, pltpu.roll(x, 128 - HEAD_DIM // 2, 1), pltpu.roll(x, HEAD_DIM // 2, 1))


def _permuted_specs(t, width):
    specs = [_spec((t, width), lambda i: (i, 0))]
    for d in DILATIONS[1:]:
        specs.append(_spec((d, t // d, width), lambda i: (0, i, 0)))
    return specs


def _permuted_shapes(width, dtype):
    shapes = [jax.ShapeDtypeStruct((SEQ, width), dtype)]
    for d in DILATIONS[1:]:
        shapes.append(jax.ShapeDtypeStruct((d, SEQ // d, width), dtype))
    return shapes


def _attn_prep(name, proj, cos_t, sin_t):
    t = ROW_TILE
    w = ATTN_WIDTH

    def body(q_ref, k_ref, v_ref, cos_ref, sin_ref, *rest):
        outs, scr = rest[:9], rest[9]
        cosv, sinv = cos_ref[...], sin_ref[...]
        for a, (src, roped, scale) in enumerate(((q_ref, True, HEAD_DIM ** -0.5),
                                                 (k_ref, True, 1.0), (v_ref, False, 1.0))):
            o1, o4, o16 = outs[3 * a:3 * a + 3]
            for cb in range(w // 128):
                cols = slice(cb * 128, (cb + 1) * 128)
                val = src[:, cols]
                if roped:
                    val = (val * cosv + _swap_halves(val) * sinv) * scale
                scr[...] = val
                o1[:, cols] = val.astype(BF16)
                for o_ref, d in ((o4, 4), (o16, 16)):
                    for r in range(d):
                        o_ref[r, :, cols] = scr[pl.ds(r, t // d, stride=d), :].astype(BF16)

    out_specs = _permuted_specs(t, w) * 3
    out_shape = _permuted_shapes(w, BF16) * 3
    outs = pl.pallas_call(
        body, name=name, grid=(SEQ // t,),
        in_specs=[_spec((t, w), lambda i: (i, 0)), _spec((t, w), lambda i: (i, 1)),
                  _spec((t, w), lambda i: (i, 2)),
                  _spec((t, 128), lambda i: (i, 0)), _spec((t, 128), lambda i: (i, 0))],
        out_specs=out_specs, out_shape=out_shape,
        scratch_shapes=[pltpu.VMEM((t, 128), F32)],
        compiler_params=_params(1),
    )(proj, proj, proj, cos_t, sin_t)
    q, k, v = outs[0:3], outs[3:6], outs[6:9]
    flat = lambda arr: arr.reshape(SEQ, w)
    return [(flat(q[p]), flat(k[p]), flat(v[p])) for p in range(3)]


def _band_masks():
    row = lax.broadcasted_iota(jnp.int32, (2 * SPAN, 2 * SPAN), 0) % SPAN
    col = lax.broadcasted_iota(jnp.int32, (2 * SPAN, 2 * SPAN), 1)
    is_prev = col < SPAN
    band = (is_prev & (col >= row)) | (~is_prev & (col - SPAN <= row))
    head0 = lax.broadcasted_iota(jnp.int32, (SPAN, 128), 1) < HEAD_DIM
    return band, is_prev, head0


def _stack_heads(x, head0):
    zero = jnp.zeros_like(x)
    return jnp.concatenate([jnp.where(head0, x, zero), jnp.where(head0, zero, x)], axis=0)


def _attn_fwd(name, q, k, v, seg_blocks):
    n_blocks = SEQ // SPAN

    def body(q_ref, k_ref, v_ref, o_ref, lse_ref):
        band, is_prev, head0 = _band_masks()

        def step(b, carry):
            cur = pl.ds(pl.multiple_of(b * SPAN, SPAN), SPAN)
            prev = pl.ds(pl.multiple_of(jnp.maximum(b - 1, 0) * SPAN, SPAN), SPAN)
            qs = _stack_heads(q_ref[cur, :], head0)
            kcat = jnp.concatenate([k_ref[prev, :], k_ref[cur, :]], axis=0)
            vcat = jnp.concatenate([v_ref[prev, :], v_ref[cur, :]], axis=0)
            ok = band & (((b % seg_blocks) != 0) | ~is_prev)
            s = jnp.where(ok, _dot_nt(qs, kcat), MASK_VALUE)
            m = jnp.max(s, axis=1, keepdims=True)
            p = jnp.exp(s - m)
            l = jnp.sum(p, axis=1, keepdims=True)
            pv = _dot(p, vcat) * (1.0 / l)
            lse = m + jnp.log(l)
            o_ref[cur, :] = jnp.where(head0, pv[:SPAN], pv[SPAN:])
            lse_ref[cur, :] = jnp.where(head0, lse[:SPAN], lse[SPAN:])
            return carry

        lax.fori_loop(0, n_blocks, step, 0, unroll=4)

    col = _spec((SEQ, 128), lambda j: (0, j))
    return pl.pallas_call(
        body, name=name, grid=(ATTN_WIDTH // 128,),
        in_specs=[col, col, col], out_specs=[col, col],
        out_shape=[jax.ShapeDtypeStruct((SEQ, ATTN_WIDTH), F32)] * 2,
        compiler_params=_params(1),
    )(q, k, v)


def _unpermute(dst, src_ref, d, cols):
    n = dst.shape[0] // d
    for r in range(d):
        dst[pl.ds(r, n, stride=d), :] = src_ref[r, :, cols].astype(dst.dtype)


def _attn_merge(name, outs, lses, gain):
    t = ROW_TILE
    w = ATTN_WIDTH

    def body(o1, o4, o16, l1, l4, l16, g_ref, an_ref, ant_ref, attn_ref, lse_ref, so4, so16, sl4, sl16):
        for cb in range(w // 128):
            cols = slice(cb * 128, (cb + 1) * 128)
            _unpermute(so4, o4, 4, cols)
            _unpermute(so16, o16, 16, cols)
            _unpermute(sl4, l4, 4, cols)
            _unpermute(sl16, l16, 16, cols)
            la, lb, lc = l1[:, cols], sl4[...], sl16[...]
            m = jnp.maximum(jnp.maximum(la, lb), lc)
            ea, eb, ec = jnp.exp(la - m), jnp.exp(lb - m), jnp.exp(lc - m)
            tot = ea + eb + ec
            attn_ref[:, cols] = (ea * o1[:, cols] + eb * so4[...] + ec * so16[...]) / tot
            lse_ref[:, cols] = m + jnp.log(tot)
        attn = attn_ref[...]
        r = lax.rsqrt(jnp.mean(attn * attn, axis=1, keepdims=True) + NORM_EPS)
        an = attn * r * g_ref[...]
        an_ref[...] = an.astype(BF16)
        ant_ref[...] = an.T.astype(BF16)

    views = lambda arrs: [arrs[0], arrs[1].reshape(4, SEQ // 4, w), arrs[2].reshape(16, SEQ // 16, w)]
    row = _spec((t, w), lambda i: (i, 0))
    return pl.pallas_call(
        body, name=name, grid=(SEQ // t,),
        in_specs=_permuted_specs(t, w) * 2 + [_spec((1, w), lambda i: (0, 0))],
        out_specs=[row, _spec((w, t), lambda i: (0, i)), row, row],
        out_shape=[jax.ShapeDtypeStruct((SEQ, 2 * w), BF16), jax.ShapeDtypeStruct((2 * w, SEQ), BF16),
                   jax.ShapeDtypeStruct((SEQ, w), F32), jax.ShapeDtypeStruct((SEQ, w), F32)],
        scratch_shapes=[pltpu.VMEM((t, 128), F32)] * 4,
        compiler_params=_params(1),
    )(*views(outs), *views(lses), gain)


def _head_sum_matrix():
    i = np.arange(ATTN_WIDTH)
    return jnp.asarray((i[:, None] // HEAD_DIM) == (i[None, :] // HEAD_DIM), dtype=F32)


def _attn_bwd_prep(name, d_an, attn, lse, gain, head_sum):
    t = ROW_TILE
    w = ATTN_WIDTH

    def body(dan_ref, attn_ref, lse_ref, g_ref, hs_ref, *rest):
        (do1, do4, do16, dl1, dl4, dl16, ls4, ls16, dg_ref), (sdo, sdl, sls) = rest[:9], rest[9:]

        @pl.when(pl.program_id(0) == 0)
        def _():
            dg_ref[...] = jnp.zeros_like(dg_ref)

        attn = attn_ref[...]
        dan = dan_ref[...]
        r = lax.rsqrt(jnp.mean(attn * attn, axis=1, keepdims=True) + NORM_EPS)
        xhat = attn * r
        dg_ref[...] += jnp.sum(dan * xhat, axis=0, keepdims=True)
        dang = dan * g_ref[...]
        d_o = r * (dang - xhat * jnp.mean(dang * xhat, axis=1, keepdims=True))
        delta = jnp.dot(d_o * attn, hs_ref[...], preferred_element_type=F32,
                        precision=lax.Precision.HIGHEST)
        do1[...] = d_o.astype(BF16)
        dl1[...] = delta
        for cb in range(w // 128):
            cols = slice(cb * 128, (cb + 1) * 128)
            sdo[...] = d_o[:, cols]
            sdl[...] = delta[:, cols]
            sls[...] = lse_ref[:, cols]
            for d, o_do, o_dl, o_ls in ((4, do4, dl4, ls4), (16, do16, dl16, ls16)):
                for rr in range(d):
                    rows = pl.ds(rr, t // d, stride=d)
                    o_do[rr, :, cols] = sdo[rows, :].astype(BF16)
                    o_dl[rr, :, cols] = sdl[rows, :]
                    o_ls[rr, :, cols] = sls[rows, :]

    row = _spec((t, w), lambda i: (i, 0))
    perm = _permuted_specs(t, w)
    outs = pl.pallas_call(
        body, name=name, grid=(SEQ // t,),
        in_specs=[row, row, row, _spec((1, w), lambda i: (0, 0)), _spec((w, w), lambda i: (0, 0))],
        out_specs=perm + perm + perm[1:] + [_spec((1, w), lambda i: (0, 0))],
        out_shape=(_permuted_shapes(w, BF16) + _permuted_shapes(w, F32) + _permuted_shapes(w, F32)[1:]
                   + [jax.ShapeDtypeStruct((1, w), F32)]),
        scratch_shapes=[pltpu.VMEM((t, 128), F32)] * 3,
        compiler_params=_params(1),
    )(d_an, attn, lse, gain, head_sum)
    flat = lambda arr: arr.reshape(SEQ, w)
    d_out = [flat(a) for a in outs[0:3]]
    delta = [flat(a) for a in outs[3:6]]
    lses = [lse, flat(outs[6]), flat(outs[7])]
    return d_out, delta, lses, outs[8]


def _attn_bwd(name, q, k, v, d_out, delta, lse, seg_blocks):
    n_blocks = SEQ // SPAN

    def body(q_ref, k_ref, v_ref, do_ref, dl_ref, lse_ref, dq_ref, dk_out, dv_out, dk_ref, dv_ref):
        band, is_prev, head0 = _band_masks()
        dk_ref[...] = jnp.zeros_like(dk_ref)
        dv_ref[...] = jnp.zeros_like(dv_ref)

        def per_head(x):
            return jnp.concatenate([x[:, 0:1], x[:, HEAD_DIM:HEAD_DIM + 1]], axis=0)

        def step(b, carry):
            cur = pl.ds(pl.multiple_of(b * SPAN, SPAN), SPAN)
            prev = pl.ds(pl.multiple_of(jnp.maximum(b - 1, 0) * SPAN, SPAN), SPAN)
            qs = _stack_heads(q_ref[cur, :], head0)
            dos = _stack_heads(do_ref[cur, :], head0)
            kcat = jnp.concatenate([k_ref[prev, :], k_ref[cur, :]], axis=0)
            vcat = jnp.concatenate([v_ref[prev, :], v_ref[cur, :]], axis=0)
            ok = band & (((b % seg_blocks) != 0) | ~is_prev)
            p = jnp.where(ok, jnp.exp(_dot_nt(qs, kcat) - per_head(lse_ref[cur, :])), 0.0)
            ds = p * (_dot_nt(dos, vcat) - per_head(dl_ref[cur, :]))
            dq = _dot(ds, kcat)
            dq_ref[cur, :] = jnp.where(head0, dq[:SPAN], dq[SPAN:]).astype(BF16)
            dk = _dot_tn(ds, qs)
            dv = _dot_tn(p, dos)
            dk_ref[prev, :] += dk[:SPAN]
            dv_ref[prev, :] += dv[:SPAN]
            dk_ref[cur, :] += dk[SPAN:]
            dv_ref[cur, :] += dv[SPAN:]
            return carry

        lax.fori_loop(0, n_blocks, step, 0, unroll=4)
        dk_out[...] = dk_ref[...].astype(BF16)
        dv_out[...] = dv_ref[...].astype(BF16)

    col = _spec((SEQ, 128), lambda j: (0, j))
    return pl.pallas_call(
        body, name=name, grid=(ATTN_WIDTH // 128,),
        in_specs=[col] * 6, out_specs=[col] * 3,
        out_shape=[jax.ShapeDtypeStruct((SEQ, ATTN_WIDTH), BF16)] * 3,
        scratch_shapes=[pltpu.VMEM((SEQ, 128), F32)] * 2,
        compiler_params=_params(1),
    )(q, k, v, d_out, delta, lse)


def _attn_bwd_post(name, grads, cos_t, sin_t):
    t = ROW_TILE
    w = ATTN_WIDTH

    def body(*refs):
        ins, cos_ref, sin_ref, out_ref, s4, s16 = refs[:9], refs[9], refs[10], refs[11], refs[12], refs[13]
        cosv, sinv = cos_ref[...], sin_ref[...]
        for a in range(3):
            g1, g4, g16 = ins[a], ins[3 + a], ins[6 + a]
            for cb in range(w // 128):
                cols = slice(cb * 128, (cb + 1) * 128)
                _unpermute(s4, g4, 4, cols)
                _unpermute(s16, g16, 16, cols)
                val = g1[:, cols].astype(F32) + s4[...] + s16[...]
                if a < 2:
                    val = val * cosv + _swap_halves(val * sinv)
                if a == 0:
                    val = val * (HEAD_DIM ** -0.5)
                out_ref[:, a * w + cb * 128:a * w + (cb + 1) * 128] = val.astype(BF16)

    views = []
    for p, d in enumerate(DILATIONS):
        for a in range(3):
            views.append(grads[p][a] if d == 1 else grads[p][a].reshape(d, SEQ // d, w))
    perm = _permuted_specs(t, w)
    in_specs = [perm[0]] * 3 + [perm[1]] * 3 + [perm[2]] * 3
    return pl.pallas_call(
        body, name=name, grid=(SEQ // t,),
        in_specs=in_specs + [_spec((t, 128), lambda i: (i, 0))] * 2,
        out_specs=_spec((t, 3 * w), lambda i: (i, 0)),
        out_shape=jax.ShapeDtypeStruct((SEQ, 3 * w), BF16),
        scratch_shapes=[pltpu.VMEM((t, 128), F32)] * 2,
        compiler_params=_params(1),
    )(*views, cos_t, sin_t)


N_LEVELS = 7
HGRN_PAIR = 2


def _hgrn_consts():
    c = CHUNK
    i = np.arange(c)[:, None]
    s = np.arange(c)[None, :]
    blocks = [s <= i]
    for lv in range(N_LEVELS):
        bs = c >> lv
        h = bs // 2
        m = (i // bs) * bs + h - 1
        second = (i % bs) >= h
        blocks.append((second & (s > m) & (s <= i)) | (~second & (s > i) & (s <= m)))
    blocks.append(s > i)
    stack = np.concatenate(blocks, axis=0).astype(np.float32)
    return jnp.asarray(stack, dtype=BF16), jnp.asarray(stack.T, dtype=BF16)


def _exact_dot(m01, x):
    hi = x.astype(BF16)
    lo = (x - hi.astype(F32)).astype(BF16)
    n = x.shape[1]
    full = jnp.dot(m01, jnp.concatenate([hi, lo], axis=1), preferred_element_type=F32)
    return full[:, :n] + full[:, n:]


def _hgrn_gates(qh, z, lb):
    sq = _sigmoid(qh)
    q = qh * sq * (HGRN_DIM ** -0.5)
    sig = _sigmoid(z)
    sigm = _sigmoid(-z)
    f = lb + (1.0 - lb) * sig
    k = (1.0 - lb) * sigm
    return q, k, f, sq, sig, sigm


def _level_masks(lv):
    row = lax.broadcasted_iota(jnp.int32, (CHUNK, CHUNK), 0)
    col = lax.broadcasted_iota(jnp.int32, (CHUNK, CHUNK), 1)
    shift = N_LEVELS - lv
    second = (row & (CHUNK >> (lv + 1))) != 0
    same = (row >> shift) == (col >> shift)
    return second, same


def _hgrn_fwd(name, proj, lb, gain, stack, mixed, mixed_t):
    t = ROW_TILE
    per = t // CHUNK
    n_rb = SEQ // t
    n_chunks = SEQ // CHUNK
    col0 = 3 * ATTN_WIDTH // 128
    pair_w = HGRN_PAIR * HGRN_DIM

    def body(q_ref, f_ref, i_ref, g_ref, lb_ref, gain_ref, stack_ref, mixed_in, mixed_t_in,
             rec_ref, rect_ref, o_ref, st_out, a_out, st):
        del mixed_in, mixed_t_in

        @pl.when(pl.program_id(1) == 0)
        def _():
            st[...] = jnp.zeros_like(st)

        row = lax.broadcasted_iota(jnp.int32, (CHUNK, CHUNK), 0)
        col = lax.broadcasted_iota(jnp.int32, (CHUNK, CHUNK), 1)
        for c, hh in [(c, hh) for c in range(per) for hh in range(HGRN_PAIR)]:
            rows = slice(c * CHUNK, (c + 1) * CHUNK)
            lanes = slice(hh * HGRN_DIM, (hh + 1) * HGRN_DIM)
            lbv = lb_ref[hh]
            qh, z, v, gh = q_ref[rows, lanes], f_ref[rows, lanes], i_ref[rows, lanes], g_ref[rows, lanes]
            q, k, f, _, _, _ = _hgrn_gates(qh, z, lbv)
            dec = _exact_dot(stack_ref[...], jnp.log(f))
            g = dec[0:CHUNK]
            to_end = dec[(N_LEVELS + 1) * CHUNK:(N_LEVELS + 2) * CHUNK]
            a = jnp.where(row == col, jnp.sum(q * k, axis=1, keepdims=True), 0.0)
            for lv in range(N_LEVELS):
                e = jnp.exp(dec[(lv + 1) * CHUNK:(lv + 2) * CHUNK])
                second, same = _level_masks(lv)
                qt = jnp.where(second, q * e, 0.0)
                kt = jnp.where(second, 0.0, k * e)
                a = a + jnp.where(same, _dot_nt(qt, kt), 0.0)
            st_prev = st[hh]
            st_out[hh, c] = st_prev
            a_out[hh, c] = a
            o = _dot(a, v) + _dot_nt(q * jnp.exp(g), st_prev)
            k_end = k * jnp.exp(to_end)
            st[hh] = st_prev * jnp.exp(g[CHUNK - 1:CHUNK, :]) + _dot(v.T, k_end)
            o_ref[rows, lanes] = o
            r = lax.rsqrt(jnp.mean(o * o, axis=1, keepdims=True) + NORM_EPS)
            rec = o * r * gain_ref[...] * (gh * _sigmoid(gh))
            rec_ref[rows, lanes] = rec.astype(BF16)
            rect_ref[lanes, rows] = rec.T.astype(BF16)

    def col_spec(tt):
        return _spec((t, pair_w), lambda h, rb: (rb, (col0 + HGRN_HEADS * tt) // HGRN_PAIR + h))

    chunk_spec = _spec((HGRN_PAIR, per, CHUNK, CHUNK), lambda h, rb: (h, rb, 0, 0))
    return pl.pallas_call(
        body, name=name, grid=(HGRN_HEADS // HGRN_PAIR, n_rb),
        in_specs=[col_spec(0), col_spec(1), col_spec(2), col_spec(3),
                  _spec((HGRN_PAIR, 1, HGRN_DIM), lambda h, rb: (h, 0, 0)),
                  _spec((1, HGRN_DIM), lambda h, rb: (0, 0)),
                  _spec(stack.shape, lambda h, rb: (0, 0)), ANY_SPEC, ANY_SPEC],
        out_specs=[_spec((t, pair_w), lambda h, rb: (rb, ATTN_WIDTH // pair_w + h)),
                   _spec((pair_w, t), lambda h, rb: (ATTN_WIDTH // pair_w + h, rb)),
                   _spec((t, pair_w), lambda h, rb: (rb, h)),
                   chunk_spec, chunk_spec],
        out_shape=[jax.ShapeDtypeStruct(mixed.shape, BF16),
                   jax.ShapeDtypeStruct(mixed_t.shape, BF16),
                   jax.ShapeDtypeStruct((SEQ, HGRN_WIDTH), F32),
                   jax.ShapeDtypeStruct((HGRN_HEADS, n_chunks, CHUNK, CHUNK), F32),
                   jax.ShapeDtypeStruct((HGRN_HEADS, n_chunks, CHUNK, CHUNK), F32)],
        scratch_shapes=[pltpu.VMEM((HGRN_PAIR, CHUNK, CHUNK), F32)],
        input_output_aliases={7: 0, 8: 1},
        compiler_params=_params(2),
    )(proj, proj, proj, proj, lb, gain, stack, mixed, mixed_t)


def _hgrn_bwd(name, proj, d_rec, o_pre, states, scores, lb, gain, stack, stack_t):
    t = ROW_TILE
    per = t // CHUNK
    n_rb = SEQ // t
    col0 = 3 * ATTN_WIDTH // 128
    pair_w = HGRN_PAIR * HGRN_DIM

    def body(q_ref, f_ref, i_ref, g_ref, drec_ref, o_ref, st_ref, a_ref, lb_ref, gain_ref,
             stack_ref, stack_t_ref, dq_ref, df_ref, di_ref, dg_ref, dlb_ref, dgain_ref, dst):
        @pl.when(pl.program_id(1) == 0)
        def _():
            dst[...] = jnp.zeros_like(dst)
            dlb_ref[...] = jnp.zeros_like(dlb_ref)
            dgain_ref[...] = jnp.zeros_like(dgain_ref)

        gain_v = gain_ref[...]
        row = lax.broadcasted_iota(jnp.int32, (CHUNK, CHUNK), 0)
        col = lax.broadcasted_iota(jnp.int32, (CHUNK, CHUNK), 1)
        for c, hh in [(c, hh) for c in reversed(range(per)) for hh in range(HGRN_PAIR)]:
            rows = slice(c * CHUNK, (c + 1) * CHUNK)
            lanes = slice(hh * HGRN_DIM, (hh + 1) * HGRN_DIM)
            lbv = lb_ref[hh]
            qh, z, v, gh = q_ref[rows, lanes], f_ref[rows, lanes], i_ref[rows, lanes], g_ref[rows, lanes]
            q, k, f, sq, sig, sigm = _hgrn_gates(qh, z, lbv)
            dec = _exact_dot(stack_ref[...], jnp.log(f))
            g = dec[0:CHUNK]
            to_end = dec[(N_LEVELS + 1) * CHUNK:(N_LEVELS + 2) * CHUNK]
            e_g = jnp.exp(g)
            e_end = jnp.exp(to_end)
            e_last = jnp.exp(g[CHUNK - 1:CHUNK, :])
            q_in = q * e_g
            k_end = k * e_end
            st_prev = st_ref[hh, c]
            a = a_ref[hh, c]
            dst_new = dst[hh]

            o = o_ref[rows, lanes]
            drec = drec_ref[rows, lanes]
            sg = _sigmoid(gh)
            r = lax.rsqrt(jnp.mean(o * o, axis=1, keepdims=True) + NORM_EPS)
            ohat = o * r
            d_gh = drec * (ohat * gain_v) * (sg * (1.0 + gh * (1.0 - sg)))
            d_on = drec * (gh * sg)
            dgain_ref[hh] += jnp.sum(d_on * ohat, axis=0, keepdims=True)
            d_ohat = d_on * gain_v
            d_o = r * (d_ohat - ohat * jnp.mean(d_ohat * ohat, axis=1, keepdims=True))

            d_a = jnp.where(row >= col, _dot_nt(d_o, v), 0.0)
            d_at = jnp.where(col >= row, _dot_nt(v, d_o), 0.0)
            d_v = _dot(a.T, d_o) + _dot_nt(k_end, dst_new)
            d_q_in = _dot(d_o, st_prev)
            d_k_end = _dot(v, dst_new)
            d_q = d_q_in * e_g
            d_k = d_k_end * e_end
            diag = jnp.sum(d_o * v, axis=1, keepdims=True)
            d_q = d_q + diag * k
            d_k = d_k + diag * q
            d_dec = [q_in * d_q_in]
            for lv in range(N_LEVELS):
                e = jnp.exp(dec[(lv + 1) * CHUNK:(lv + 2) * CHUNK])
                second, same = _level_masks(lv)
                qt = jnp.where(second, q * e, 0.0)
                kt = jnp.where(second, 0.0, k * e)
                d_qt = _dot(jnp.where(same, d_a, 0.0), kt)
                d_kt = _dot(jnp.where(same, d_at, 0.0), qt)
                d_q = d_q + jnp.where(second, d_qt * e, 0.0)
                d_k = d_k + jnp.where(second, 0.0, d_kt * e)
                d_dec.append(jnp.where(second, qt * d_qt, kt * d_kt))
            d_dec.append(k_end * d_k_end)
            flux = jnp.sum(dst_new * st_prev, axis=0, keepdims=True) * e_last
            d_lf = _exact_dot(stack_t_ref[...], jnp.concatenate(d_dec, axis=0)) + flux
            dst[hh] = dst_new * e_last + _dot(d_o.T, q_in)

            d_f = d_lf / f - d_k
            dlb_ref[hh] += jnp.sum(d_f * sigm, axis=0, keepdims=True)
            dq_ref[rows, lanes] = (d_q * (HGRN_DIM ** -0.5) * (sq * (1.0 + qh * (1.0 - sq)))).astype(BF16)
            df_ref[rows, lanes] = (d_f * (1.0 - lbv) * sig * sigm).astype(BF16)
            di_ref[rows, lanes] = d_v.astype(BF16)
            dg_ref[rows, lanes] = d_gh.astype(BF16)

    last = n_rb - 1

    def col_spec(tt):
        return _spec((t, pair_w), lambda h, rb: (last - rb, (col0 + HGRN_HEADS * tt) // HGRN_PAIR + h))

    head_col = _spec((t, pair_w), lambda h, rb: (last - rb, h))
    rec_col0 = (d_rec.shape[1] - HGRN_WIDTH) // pair_w
    d_rec_col = _spec((t, pair_w), lambda h, rb: (last - rb, rec_col0 + h))
    chunk_spec = _spec((HGRN_PAIR, per, CHUNK, CHUNK), lambda h, rb: (h, last - rb, 0, 0))
    vec_spec = _spec((HGRN_PAIR, 1, HGRN_DIM), lambda h, rb: (h, 0, 0))
    outs = pl.pallas_call(
        body, name=name, grid=(HGRN_HEADS // HGRN_PAIR, n_rb),
        in_specs=[col_spec(0), col_spec(1), col_spec(2), col_spec(3), d_rec_col, head_col,
                  chunk_spec, chunk_spec, vec_spec,
                  _spec((1, HGRN_DIM), lambda h, rb: (0, 0)),
                  _spec(stack.shape, lambda h, rb: (0, 0)), _spec(stack_t.shape, lambda h, rb: (0, 0))],
        out_specs=[head_col] * 4 + [vec_spec, vec_spec],
        out_shape=[jax.ShapeDtypeStruct((SEQ, HGRN_WIDTH), BF16)] * 4
                  + [jax.ShapeDtypeStruct((HGRN_HEADS, 1, HGRN_DIM), F32)] * 2,
        scratch_shapes=[pltpu.VMEM((HGRN_PAIR, CHUNK, CHUNK), F32)],
        compiler_params=_params(2),
    )(proj, proj, proj, proj, d_rec, o_pre, states, scores, lb, gain, stack, stack_t)
    return outs


ANY_SPEC = pl.BlockSpec(memory_space=pl.ANY)


def _my_place():
    return lax.axis_index("x"), lax.axis_index("y"), lax.axis_index("c")


def _other_chips(x, y):
    return [(1 - x, y), (x, 1 - y), (1 - x, 1 - y)]


def _remote(src, dst, send_sem, recv_sem, device):
    return pltpu.make_async_remote_copy(src_ref=src, dst_ref=dst, send_sem=send_sem, recv_sem=recv_sem,
                                        device_id=device, device_id_type=MESH)


def _staged_copies(srcs, dsts, stage, sems):
    loads = [pltpu.make_async_copy(srcs[i], stage[i], sems.at[i]) for i in range(len(srcs))]
    for cp in loads:
        cp.start()
    stores = []
    for i, cp in enumerate(loads):
        cp.wait()
        stores.append(pltpu.make_async_copy(stage[i], dsts[i], sems.at[i]))
        stores[-1].start()
    return stores


def _gather_weights(name, shards):
    n = len(shards)

    def body(*refs):
        ins, outs = refs[:n], refs[n:2 * n]
        ici_send, ici_recv, d2d_send, d2d_recv, local_sems = refs[2 * n:2 * n + 5]
        stage = refs[2 * n + 5:]
        x, y, c = _my_place()
        me = 2 * x + y
        chips = _other_chips(x, y)

        def half(i, which):
            h = ins[i].shape[0] // 2
            return pl.ds(which * h, h)

        sends = []
        for i in range(n):
            for j, (px, py) in enumerate(chips):
                sends.append(_remote(ins[i].at[half(i, c), :], outs[i].at[me, half(i, c), :],
                                     ici_send.at[3 * i + j], ici_recv.at[3 * i + j], (px, py, c)))
        for cp in sends:
            cp.start()
        local = _staged_copies(ins, [outs[i].at[me] for i in range(n)], stage, local_sems)
        for i in range(n):
            for j, (px, py) in enumerate(chips):
                landed = outs[i].at[2 * px + py, half(i, c), :]
                _remote(landed, landed, ici_send.at[3 * i + j], ici_recv.at[3 * i + j], (px, py, c)).wait_recv()
                forward = _remote(landed, landed, d2d_send.at[3 * i + j], d2d_recv.at[3 * i + j], (x, y, 1 - c))
                forward.start()
                sends.append(forward)
        for i in range(n):
            for j, (px, py) in enumerate(chips):
                other = outs[i].at[2 * px + py, half(i, 1 - c), :]
                _remote(other, other, d2d_send.at[3 * i + j], d2d_recv.at[3 * i + j], (x, y, 1 - c)).wait_recv()
        for cp in sends:
            cp.wait_send()
        for cp in local:
            cp.wait()

    return pl.pallas_call(
        body, name=name, in_specs=[ANY_SPEC] * n, out_specs=[ANY_SPEC] * n,
        out_shape=[jax.ShapeDtypeStruct((N_CHIPS,) + s.shape, s.dtype) for s in shards],
        scratch_shapes=([pltpu.SemaphoreType.DMA((3 * n,))] * 4 + [pltpu.SemaphoreType.DMA((n,))]
                        + [pltpu.VMEM(s.shape, s.dtype) for s in shards]),
        compiler_params=pltpu.CompilerParams(vmem_limit_bytes=VMEM_LIMIT),
    )(*shards)


HBM_SPEC = pl.BlockSpec(memory_space=pltpu.HBM)
SEM_SPEC = pl.BlockSpec(memory_space=pltpu.SEMAPHORE)
SPLIT_PARAMS = pltpu.CompilerParams(has_side_effects=pltpu.SideEffectType.DATAFLOW_SIDE_EFFECTING)
N_PEERS = {"gather": N_CHIPS - 1, "scatter": N_DEV - 1}


def _split_copies(ins, lands, send_sems, recv_sems, kind):
    x, y, c = _my_place()
    pairs = []
    for i in range(len(ins)):
        if kind == "gather":
            me = 2 * x + y
            for j, (px, py) in enumerate(_other_chips(x, y)):
                sems = (send_sems.at[3 * i + j], recv_sems.at[3 * i + j], (px, py, c))
                pairs.append((_remote(ins[i], lands[i].at[me], *sems),
                              _remote(ins[i], lands[i].at[2 * px + py], *sems)))
        else:
            me = 4 * x + 2 * y + c
            h = ins[i].shape[1] // 2
            for k in range(1, N_DEV):
                px, py, pc = (x + (k >> 2)) % 2, (y + ((k >> 1) & 1)) % 2, (c + (k & 1)) % 2
                src = ins[i].at[2 * px + py, pl.ds(pc * h, h), :]
                sems = (send_sems.at[7 * i + k - 1], recv_sems.at[7 * i + k - 1], (px, py, pc))
                pairs.append((_remote(src, lands[i].at[me], *sems),
                              _remote(src, lands[i].at[4 * px + 2 * py + pc], *sems)))
    return pairs


def _exchange_start(name, srcs, lands, kind, after=None):
    n = len(srcs)
    n_sems = N_PEERS[kind] * n
    extra = [] if after is None else [after]

    def body(*refs):
        ins, land_refs = refs[:n], refs[n:2 * n]
        send_sems, recv_sems = refs[2 * n + len(extra):2 * n + len(extra) + 2]
        token = refs[-1]
        for send, _ in _split_copies(ins, land_refs, send_sems, recv_sems, kind):
            send.start()
        token[...] = jnp.zeros_like(token)

    arrays = list(srcs) + list(lands)
    outs = pl.pallas_call(
        body, name=name,
        in_specs=[HBM_SPEC] * (2 * n) + [ANY_SPEC] * len(extra),
        out_shape=([pltpu.SemaphoreType.DMA((n_sems,))] * 2 + [pltpu.HBM(a.shape, a.dtype) for a in arrays]
                   + [jax.ShapeDtypeStruct((8, 128), F32)]),
        out_specs=[SEM_SPEC] * 2 + [HBM_SPEC] * (2 * n) + [pl.BlockSpec(memory_space=pltpu.VMEM)],
        input_output_aliases={i: 2 + i for i in range(2 * n)},
        compiler_params=SPLIT_PARAMS,
    )(*[pltpu.with_memory_space_constraint(a, pltpu.HBM) for a in arrays], *extra)
    return outs[:2], outs[2:2 + 2 * n], outs[-1]


def _exchange_wait(name, sems, passed, kind, after):
    n = len(passed) // 2

    def body(*refs):
        ins, land_refs = refs[:n], refs[n:2 * n]
        send_sems, recv_sems = refs[2 * n:2 * n + 2]
        for send, arrive in _split_copies(ins, land_refs, send_sems, recv_sems, kind):
            send.wait_send()
            arrive.wait_recv()

    outs = pl.pallas_call(
        body, name=name,
        in_specs=[HBM_SPEC] * (2 * n) + [SEM_SPEC] * 2 + [ANY_SPEC],
        out_shape=[pltpu.HBM(a.shape, a.dtype) for a in passed],
        out_specs=[HBM_SPEC] * (2 * n),
        input_output_aliases={i: i for i in range(2 * n)},
        compiler_params=SPLIT_PARAMS,
    )(*passed, *sems, after)
    return outs[:n], outs[n:]


def _own_slot(name, own, me):
    r, cc = own.shape
    th = min(r, 512)

    def body(me_ref, x_ref, o_ref):
        del me_ref
        o_ref[...] = x_ref[...]

    grid_spec = pltpu.PrefetchScalarGridSpec(
        num_scalar_prefetch=1, grid=(r // th,),
        in_specs=[pl.BlockSpec((th, cc), lambda i, me_ref: (i, 0))],
        out_specs=pl.BlockSpec((None, th, cc), lambda i, me_ref: (me_ref[0], i, 0)))
    return pl.pallas_call(
        body, name=name, grid_spec=grid_spec,
        out_shape=jax.ShapeDtypeStruct((N_CHIPS, r, cc), own.dtype), compiler_params=_params(1),
    )(me, own)


def _sum_devices(name, landed, own, place):
    n_dev, h, cc = landed.shape
    th = min(h, 256)
    nb = h // th

    def body(place_ref, l_ref, own_ref, o_ref):
        total = None
        for d in range(n_dev):
            piece = jnp.where(place_ref[0] == d, own_ref[...], l_ref[d]).astype(F32)
            total = piece if total is None else total + piece
        o_ref[...] = total

    grid_spec = pltpu.PrefetchScalarGridSpec(
        num_scalar_prefetch=1, grid=(nb,),
        in_specs=[pl.BlockSpec((n_dev, th, cc), lambda i, p: (0, i, 0)),
                  pl.BlockSpec((None, th, cc), lambda i, p: (p[1], p[2] * nb + i, 0))],
        out_specs=pl.BlockSpec((th, cc), lambda i, p: (i, 0)))
    return pl.pallas_call(
        body, name=name, grid_spec=grid_spec,
        out_shape=jax.ShapeDtypeStruct((h, cc), F32), compiler_params=_params(1),
    )(place, landed, own)


def _share_halves(name, halves):
    flat = [t for per_weight in halves for t in per_weight]
    n = len(flat)
    n_w = len(halves)

    def body(*refs):
        ins, outs = refs[:n], refs[n:n + n_w]
        send_sems, recv_sems, local_sems = refs[n + n_w:n + n_w + 3]
        stage = refs[n + n_w + 3:]
        x, y, c = _my_place()
        sends, own = [], []
        for i in range(n):
            w, l = divmod(i, DEPTH)
            h = ins[i].shape[0]
            own.append(outs[w].at[l, pl.ds(c * h, h), :])
            sends.append(_remote(ins[i], own[i], send_sems.at[i], recv_sems.at[i], (x, y, 1 - c)))
        for cp in sends:
            cp.start()
        local = _staged_copies(ins, own, stage, local_sems)
        for i in range(n):
            w, l = divmod(i, DEPTH)
            h = ins[i].shape[0]
            _remote(ins[i], outs[w].at[l, pl.ds((1 - c) * h, h), :], send_sems.at[i], recv_sems.at[i],
                    (x, y, 1 - c)).wait_recv()
        for cp in sends:
            cp.wait_send()
        for cp in local:
            cp.wait()

    return pl.pallas_call(
        body, name=name, in_specs=[ANY_SPEC] * n, out_specs=[ANY_SPEC] * n_w,
        out_shape=[jax.ShapeDtypeStruct((DEPTH, 2 * per_weight[0].shape[0], per_weight[0].shape[1]), F32)
                   for per_weight in halves],
        scratch_shapes=([pltpu.SemaphoreType.DMA((n,))] * 3 + [pltpu.VMEM(t.shape, t.dtype) for t in flat]),
        compiler_params=pltpu.CompilerParams(vmem_limit_bytes=VMEM_LIMIT),
    )(*flat)


def _all_reduce_small(pack, after):
    def body(p_ref, after_ref, o_ref, recv, send_sems, recv_sems):
        del after_ref
        x, y, c = _my_place()
        me = 4 * x + 2 * y + c
        recv[me] = p_ref[...]
        peers = []
        for k in range(1, N_DEV):
            px, py, pc = (x + (k >> 2)) % 2, (y + ((k >> 1) & 1)) % 2, (c + (k & 1)) % 2
            peers.append((px, py, pc))
        sends = [_remote(p_ref, recv.at[me], send_sems.at[k], recv_sems.at[k], peer)
                 for k, peer in enumerate(peers)]
        for cp in sends:
            cp.start()
        for k, (px, py, pc) in enumerate(peers):
            _remote(p_ref, recv.at[4 * px + 2 * py + pc], send_sems.at[k], recv_sems.at[k],
                    (px, py, pc)).wait_recv()
        for cp in sends:
            cp.wait_send()
        total = recv[0]
        for d in range(1, N_DEV):
            total = total + recv[d]
        o_ref[...] = total

    vmem = pl.BlockSpec(memory_space=pltpu.VMEM)
    return pl.pallas_call(
        body, name="all_reduce_small", in_specs=[vmem, ANY_SPEC], out_specs=vmem,
        out_shape=jax.ShapeDtypeStruct(pack.shape, F32),
        scratch_shapes=[pltpu.VMEM((N_DEV,) + pack.shape, F32),
                        pltpu.SemaphoreType.DMA((N_DEV - 1,)), pltpu.SemaphoreType.DMA((N_DEV - 1,))],
    )(pack, after)


def _adamw(name, w, g, m, v):
    r, cc = w.shape
    th = min(r, 256)

    def body(w_ref, g_ref, m_ref, v_ref, d_ref, m_out, v_out):
        gv = g_ref[...]
        m2 = ADAM_B1 * m_ref[...] + (1.0 - ADAM_B1) * gv
        v2 = ADAM_B2 * v_ref[...] + (1.0 - ADAM_B2) * (gv * gv)
        m_hat = m2 / (1.0 - ADAM_B1 ** ADAM_STEP)
        v_hat = v2 / (1.0 - ADAM_B2 ** ADAM_STEP)
        d_ref[...] = -ADAM_LR * (m_hat / (jnp.sqrt(v_hat) + ADAM_EPS) + ADAM_WD * w_ref[...])
        m_out[...] = m2
        v_out[...] = v2

    tile = _spec((th, cc), lambda i: (i, 0))
    return pl.pallas_call(
        body, name=name, grid=(r // th,), in_specs=[tile] * 4, out_specs=[tile] * 3,
        out_shape=[jax.ShapeDtypeStruct((r, cc), F32)] * 3, compiler_params=_params(1),
    )(w, g, m, v)


def _lower_bounds(lb_logits):
    p = jax.nn.softmax(lb_logits.astype(F32), axis=0)
    return jnp.cumsum(p, axis=0) - p[0]


def _layer_forward(l, stream, small, weights, consts, next_gain, after=None):
    win, rest = weights
    cos_t, sin_t, stack, _, _ = consts
    tm = MM_TILE
    x_in, h, h_t = stream
    saved = {"x_in": x_in}

    proj = _mm_pieces(f"proj{l}", h, win, False, tm, after=after)
    saved.update(h_t=h_t, proj=proj)

    qkv = _attn_prep(f"attn_prep{l}", proj, cos_t, sin_t)
    outs, lses = [], []
    for p, d in enumerate(DILATIONS):
        o, lse = _attn_fwd(f"attn_fwd{l}_{d}", *qkv[p], SEQ // d // SPAN)
        outs.append(o)
        lses.append(lse)
    mixed, mixed_t, attn, lse = _attn_merge(f"attn_merge{l}", outs, lses, small["attn_out_gain"][l][None, :])
    saved.update(qkv=qkv, attn=attn, lse=lse)

    lb3 = small["lower"][l].reshape(HGRN_HEADS, 1, HGRN_DIM)
    mixed, mixed_t, o_pre, states, scores = _hgrn_fwd(f"hgrn_fwd{l}", proj, lb3, small["hgrn_out_gain"][l][None, :],
                                                      stack, mixed, mixed_t)
    wo, wu, wd = rest(mixed)
    saved.update(mixed_t=mixed_t, o_pre=o_pre, states=states, scores=scores, lb3=lb3, weights=(win, wo, wu, wd))

    x_mid, h2, h2_t = _mm_accum(f"out_proj{l}", mixed, wo, False, tm, x_in, next_gain=small["norm_mlp"][l][None, :])
    saved["x_mid"] = x_mid

    a, relu_u, a_t = _mm_pieces(f"up{l}", h2, wu, False, tm, epilogue="relu2")
    if next_gain is None:
        new_stream = (_mm_accum(f"down{l}", a, wd, False, tm, x_mid), None, None)
    else:
        new_stream = tuple(_mm_accum(f"down{l}", a, wd, False, tm, x_mid, next_gain=next_gain))
    saved.update(h2_t=h2_t, relu_u=relu_u, a_t=a_t)
    return new_stream, saved


def _layer_backward(l, dx, saved, small, consts, on_grads, after=None):
    win, wo, wu, wd = saved["weights"]
    cos_t, sin_t, stack, stack_t, head_sum = consts
    tm = MM_TILE

    dx, dx_b = dx
    du = _mm_pieces(f"d_u{l}", dx_b, wd, True, tm, epilogue="relu2_grad", extra=saved["relu_u"], after=after)
    d_wd = _mm_dw(f"d_wdown{l}", saved["a_t"], dx_b, False, tm)
    dxm, dxm_b, dg_mlp = _mm_accum(f"d_h2_{l}", du, wu, True, tm, dx,
                                   norm=(saved["x_mid"], small["norm_mlp"][l][None, :]))
    d_wu = _mm_dw(f"d_wup{l}", saved["h2_t"], du, True, tm)
    after_mlp = on_grads(l, "mlp", (d_wu, d_wd))

    d_mixed = _mm_pieces(f"d_mixed{l}", dxm_b, wo, True, tm, after=after_mlp)
    d_wo = _mm_dw(f"d_wout{l}", saved["mixed_t"], dxm_b, False, tm)
    d_rec = d_mixed

    d_out, delta, lses, dg_attn = _attn_bwd_prep(f"attn_bwd_prep{l}", d_mixed, saved["attn"], saved["lse"],
                                                 small["attn_out_gain"][l][None, :], head_sum)
    grads = []
    for p, d in enumerate(DILATIONS):
        grads.append(_attn_bwd(f"attn_bwd{l}_{d}", *saved["qkv"][p], d_out[p], delta[p], lses[p],
                               SEQ // d // SPAN))
    dp_attn = _attn_bwd_post(f"attn_bwd_post{l}", grads, cos_t, sin_t)

    dq_h, df_h, di_h, dg_h, d_lower, dg_hgrn = _hgrn_bwd(
        f"hgrn_bwd{l}", saved["proj"], d_rec, saved["o_pre"], saved["states"], saved["scores"],
        saved["lb3"], small["hgrn_out_gain"][l][None, :], stack, stack_t)
    dproj = [dp_attn, dq_h, df_h, di_h, dg_h]

    d_win = _mm_dw(f"d_win{l}", saved["h_t"], dproj, True, tm)
    after_mix = on_grads(l, "mix", (d_win, d_wo))
    dx_in, dx_in_b, dg_mix = _mm_accum(f"d_h{l}", dproj, win, True, tm, dxm,
                                       norm=(saved["x_in"], small["norm_mix"][l][None, :]), after=after_mix)

    small_grads = {"norm_mix": dg_mix[0], "attn_out_gain": dg_attn[0],
                   "lower": d_lower.reshape(HGRN_WIDTH),
                   "hgrn_out_gain": jnp.sum(dg_hgrn, axis=0).reshape(HGRN_DIM), "norm_mlp": dg_mlp[0]}
    return (dx_in, dx_in_b), after_mix, small_grads


def _local_step(xs, target, small, get_weights, on_grads):
    consts = _rope_tables() + _hgrn_consts() + (_head_sum_matrix(),)
    stream = (xs,) + tuple(_rms_fwd("norm_mix0", xs, small["norm_mix"][0][None, :]))
    saved = []
    for l in range(DEPTH):
        w, after = get_weights(l, stream[0])
        next_gain = small["norm_mix"][l + 1][None, :] if l + 1 < DEPTH else None
        stream, s = _layer_forward(l, stream, small, w, consts, next_gain, after=after)
        saved.append(s)
    dx_f, dx_b, dg_final, loss = _loss_head(stream[0], small["norm_final"][None, :], target)
    dx = (dx_f, dx_b)
    small_grads = [None] * DEPTH
    after = None
    for l in reversed(range(DEPTH)):
        dx, after, small_grads[l] = _layer_backward(l, dx, saved[l], small, consts, on_grads, after=after)
    return loss, dx[0], dg_final[0], small_grads, after


def _pack_small(norm_mix, attn_out_gain, lb, hgrn_out_gain, norm_mlp, norm_final, last_row):
    rows = [norm_mix, attn_out_gain.reshape(1, D_MODEL), lb.reshape(1, D_MODEL),
            jnp.pad(hgrn_out_gain.reshape(1, DEPTH * HGRN_DIM), ((0, 0), (0, D_MODEL - DEPTH * HGRN_DIM))),
            norm_mlp, norm_final.reshape(1, D_MODEL), last_row.reshape(1, D_MODEL)]
    pack = jnp.concatenate(rows, axis=0)
    return jnp.pad(pack, ((0, PACK_ROWS - pack.shape[0]), (0, 0)))


def _unpack_small(pack):
    return (pack[0:2], pack[2].reshape(DEPTH, ATTN_WIDTH), pack[3].reshape(DEPTH, HGRN_WIDTH),
            pack[4, :DEPTH * HGRN_DIM].reshape(DEPTH, HGRN_DIM), pack[5:7], pack[7], pack[8])


def kernel(x, norm_mix, w_in, attn_out_gain, hgrn_lb_logits, hgrn_out_gain, w_out, norm_mlp, w_up, w_down, norm_final, loss_target, m_norm_mix, m_w_in, m_attn_out_gain, m_hgrn_lb_logits, m_hgrn_out_gain, m_w_out, m_norm_mlp, m_w_up, m_w_down, m_norm_final, v_norm_mix, v_w_in, v_attn_out_gain, v_hgrn_lb_logits, v_hgrn_out_gain, v_w_out, v_norm_mlp, v_w_up, v_w_down, v_norm_final):
    lower, lower_vjp = jax.vjp(_lower_bounds, hgrn_lb_logits)
    small = {"norm_mix": norm_mix, "attn_out_gain": attn_out_gain, "lower": lower,
             "hgrn_out_gain": hgrn_out_gain, "norm_mlp": norm_mlp, "norm_final": norm_final}
    big_w = (w_in, w_out, w_up, w_down)

    x_pos, y_pos, core = lax.axis_index("x"), lax.axis_index("y"), lax.axis_index("c")
    me = (2 * x_pos + y_pos).astype(jnp.int32).reshape(1)
    place = jnp.stack([4 * x_pos + 2 * y_pos + core, 2 * x_pos + y_pos, core]).astype(jnp.int32)
    shards = [[w[l].astype(BF16) for w in big_w] for l in range(DEPTH)]
    in_flight = {}

    def start_gather(name, some, after):
        lands = [_own_slot(f"own_{name}_{i}", s, me) for i, s in enumerate(some)]
        sems, passed, token = _exchange_start(f"start_{name}", some, lands, "gather", after)
        in_flight[name] = (sems, passed)
        return token

    def finish_gather(name, after):
        return _exchange_wait(f"wait_{name}", *in_flight.pop(name), "gather", after)[1]

    def get_weights(l, stream):
        if l == 0:
            (win,) = _gather_weights("gather_w_in0", shards[0][:1])
            token = start_gather("gather_rest0", shards[0][1:], win)
            token = start_gather("gather_w_in1", shards[1][:1], token)
            token = start_gather("gather_rest1", shards[1][1:], token)
            return (win, lambda after: finish_gather("gather_rest0", after)), token
        (win,) = finish_gather("gather_w_in1", stream)
        return (win, lambda after: finish_gather("gather_rest1", after)), None

    reduced = {}

    def start_exchange(name, grads):
        srcs, lands = [g for g, _ in grads], [land for _, land in grads]
        sems, passed, token = _exchange_start(f"start_{name}", srcs, lands, "scatter")
        in_flight[name] = (sems, passed)
        return token

    def finish_exchange(name, after):
        own, landed = _exchange_wait(f"wait_{name}", *in_flight.pop(name), "scatter", after)
        return [_sum_devices(f"sum_{name}_{i}", p, g, place) for i, (p, g) in enumerate(zip(landed, own))]

    def on_grads(l, group, grads):
        if (l, group) == (1, "mlp"):
            return start_exchange("mlp1", grads)
        if (l, group) == (1, "mix"):
            return start_exchange("mix1", grads)
        if (l, group) == (0, "mlp"):
            token = start_exchange("mlp0", grads)
            reduced[(1, "mlp")] = finish_exchange("mlp1", token)
            reduced[(1, "mix")] = finish_exchange("mix1", token)
            return token
        token = start_exchange("mix0", grads)
        reduced[(0, "mlp")] = finish_exchange("mlp0", token)
        return token

    loss, dx, dg_final, sg, last_started = _local_step(x[0], loss_target[0], small, get_weights, on_grads)

    big_m = (m_w_in, m_w_out, m_w_up, m_w_down)
    big_v = (v_w_in, v_w_out, v_w_up, v_w_down)
    names = ("w_in", "w_out", "w_up", "w_down")
    big_g, big_delta, big_new_m, big_new_v = [None] * 4, [None] * 4, [None] * 4, [None] * 4

    def finish_weights(group, which):
        whole = _share_halves(f"share_{group}", [[reduced[(l, group)][i] for l in range(DEPTH)] for i in range(2)])
        for i, w in enumerate(which):
            shape = big_w[w].shape
            flat = lambda arr: arr.reshape(shape[0] * shape[1], shape[2])
            d, m2, v2 = _adamw(f"adamw_{names[w]}", flat(big_w[w]), flat(whole[i]), flat(big_m[w]), flat(big_v[w]))
            big_g[w], big_delta[w] = whole[i], d.reshape(shape)
            big_new_m[w], big_new_v[w] = m2.reshape(shape), v2.reshape(shape)

    finish_weights("mlp", (2, 3))
    reduced[(0, "mix")] = finish_exchange("mix0", big_delta[3])
    finish_weights("mix", (0, 1))

    stack2 = lambda key: jnp.stack([sg[l][key] for l in range(DEPTH)])
    pack = _pack_small(stack2("norm_mix"), stack2("attn_out_gain"), stack2("lower"), stack2("hgrn_out_gain"),
                       stack2("norm_mlp"), dg_final, jnp.broadcast_to(loss[0, 0], (D_MODEL,)))
    g_mix, g_attn, g_lower, g_hgrn, g_mlp, g_final, loss_row = _unpack_small(_all_reduce_small(pack, big_delta[0]))
    (g_logits,) = lower_vjp(g_lower)

    zeros_row = jnp.zeros((D_MODEL,), F32)
    small_w = (norm_mix, attn_out_gain, hgrn_lb_logits, hgrn_out_gain, norm_mlp, norm_final)
    small_m = (m_norm_mix, m_attn_out_gain, m_hgrn_lb_logits, m_hgrn_out_gain, m_norm_mlp, m_norm_final)
    small_v = (v_norm_mix, v_attn_out_gain, v_hgrn_lb_logits, v_hgrn_out_gain, v_norm_mlp, v_norm_final)
    small_g = (g_mix, g_attn, g_logits, g_hgrn, g_mlp, g_final)
    packs = [_pack_small(*t, zeros_row) for t in (small_w, small_g, small_m, small_v)]
    small_delta, small_new_m, small_new_v = [_unpack_small(p)[:6] for p in _adamw("adamw_small", *packs)]

    def ordered(small6, big4):
        mix, attn, lbl, hg, mlp, fin = small6
        return (mix, big4[0], attn, lbl, hg, big4[1], mlp, big4[2], big4[3], fin)

    return ((loss_row[0], dx[None]) + ordered(small_g, big_g) + ordered(small_delta, big_delta)
            + ordered(small_new_m, big_new_m) + ordered(small_new_v, big_new_v))
```

```python
import numpy as np
import jax
import jax.numpy as jnp
from jax import lax
from jax.experimental import pallas as pl
from jax.experimental.pallas import tpu as pltpu

F32 = jnp.float32
BF16 = jnp.bfloat16
MESH = pl.DeviceIdType.MESH

SEQ = 4096
D_MODEL = 1024
DEPTH = 2
ATTN_WIDTH = 512
HEAD_DIM = 64
HGRN_HEADS = 4
HGRN_DIM = 128
HGRN_WIDTH = 512
IN_W = 3584
MLP_HIDDEN = 4096
N_CHIPS = 4
N_DEV = 8
DILATIONS = (1, 4, 16)
SPAN = 128
ROPE_THETA = 10000.0
NORM_EPS = 1e-6
MASK_VALUE = -1e30
CHUNK = 128
ROW_TILE = 512
MM_TILE = 512
VMEM_LIMIT = 52 * 1024 * 1024

ADAM_LR = 0.001
ADAM_B1 = 0.9
ADAM_B2 = 0.999
ADAM_EPS = 1e-08
ADAM_WD = 0.01
ADAM_STEP = 10

PACK_ROWS = 16


def _params(n_axes):
    return pltpu.CompilerParams(dimension_semantics=("arbitrary",) * n_axes,
                                vmem_limit_bytes=VMEM_LIMIT)


def _dot(a, b):
    return jnp.dot(a.astype(BF16), b.astype(BF16), preferred_element_type=F32)


def _dot_nt(a, b):
    return lax.dot_general(a.astype(BF16), b.astype(BF16), (((1,), (1,)), ((), ())),
                           preferred_element_type=F32)


def _dot_tn(a, b):
    return lax.dot_general(a.astype(BF16), b.astype(BF16), (((0,), (0,)), ((), ())),
                           preferred_element_type=F32)


def _sigmoid(x):
    return 1.0 / (1.0 + jnp.exp(-x))


def _spec(shape, index_map):
    return pl.BlockSpec(shape, index_map)


def _mm_pieces(name, a, w, nt, tm, epilogue="none", extra=None, after=None):
    s = a.shape[0]
    pw = w.shape[1] if nt else w.shape[2]
    width = N_CHIPS * pw

    def body(a_ref, w_ref, *rest):
        e_ref = rest[0] if extra is not None else None
        outs = rest[-3:] if epilogue == "relu2" else rest[-1:]
        av = a_ref[...].astype(BF16)
        for j in range(N_CHIPS):
            cols = slice(j * pw, (j + 1) * pw)
            r = _dot_nt(av, w_ref[j]) if nt else _dot(av, w_ref[j])
            if epilogue == "relu2":
                relu = jnp.maximum(r, 0.0)
                r = relu * relu
                outs[1][:, cols] = relu.astype(BF16)
                outs[2][cols, :] = r.T.astype(BF16)
            elif epilogue == "relu2_grad":
                r = r * (2.0 * e_ref[:, cols].astype(F32))
            outs[0][:, cols] = r.astype(outs[0].dtype)

    row = lambda width_: _spec((tm, width_), lambda i: (i, 0))
    in_specs = [row(a.shape[1]), _spec(w.shape, lambda i: (0, 0, 0))]
    args = [a, w]
    if extra is not None:
        in_specs.append(row(width))
        args.append(extra)
    if after is not None:
        in_specs.append(pl.BlockSpec(memory_space=pl.ANY))
        args.append(after)
    if epilogue == "relu2":
        out_specs = [row(width), row(width), _spec((width, tm), lambda i: (0, i))]
        out_shape = [jax.ShapeDtypeStruct((s, width), BF16)] * 2 + [jax.ShapeDtypeStruct((width, s), BF16)]
    else:
        out_specs = row(width)
        out_shape = jax.ShapeDtypeStruct((s, width), BF16 if epilogue == "relu2_grad" else F32)
    return pl.pallas_call(body, name=name, grid=(s // tm,), in_specs=in_specs, out_specs=out_specs,
                          out_shape=out_shape, compiler_params=_params(1))(*args)


def _mm_accum(name, a, w, nt, tm, resid, norm=None, after=None, next_gain=None, loss=None):
    pieces = list(a) if isinstance(a, (list, tuple)) else [a]
    n_a = len(pieces)
    s = pieces[0].shape[0]
    pk = w.shape[2] if nt else w.shape[1]
    d = w.shape[1] if nt else w.shape[2]

    def body(*refs):
        a_refs, w_ref, resid_ref, rest = refs[:n_a], refs[n_a], refs[n_a + 1], refs[n_a + 2:]
        av = a_refs[0][...] if n_a == 1 else jnp.concatenate([ref[...] for ref in a_refs], axis=1)
        r = None
        for j in range(N_CHIPS):
            piece = av[:, j * pk:(j + 1) * pk].astype(BF16)
            term = _dot_nt(piece, w_ref[j]) if nt else _dot(piece, w_ref[j])
            r = term if r is None else r + term
        if loss is not None:
            g_ref, t_ref = rest[:2]
            dx_ref, dxb_ref, dg_ref, loss_ref, acc = rest[-5:]
            i = pl.program_id(0)

            @pl.when(i == 0)
            def _():
                dg_ref[...] = jnp.zeros_like(dg_ref)
                acc[...] = jnp.zeros_like(acc)

            xv = r + resid_ref[...]
            g = g_ref[...]
            rs = lax.rsqrt(jnp.mean(xv * xv, axis=1, keepdims=True) + NORM_EPS)
            xhat = xv * rs
            err = xhat * g - t_ref[...]
            acc[...] += jnp.sum(err * err, axis=0, keepdims=True)
            dy = err * (1.0 / d)
            dyg = dy * g
            dx = rs * (dyg - xhat * jnp.mean(dyg * xhat, axis=1, keepdims=True))
            dx_ref[...] = dx
            dxb_ref[...] = dx.astype(BF16)
            dg_ref[...] += jnp.sum(dy * xhat, axis=0, keepdims=True)

            @pl.when(i == s // tm - 1)
            def _():
                total = jnp.sum(acc[...], axis=1, keepdims=True) * (0.5 / d)
                loss_ref[...] = jnp.broadcast_to(total, loss_ref.shape)

            return
        if norm is None and next_gain is None:
            rest[-1][...] = r + resid_ref[...]
            return
        if norm is None:
            g_ref = rest[0]
            x_out, h_out, ht_out = rest[-3:]
            xv = r + resid_ref[...]
            x_out[...] = xv
            h = xv * lax.rsqrt(jnp.mean(xv * xv, axis=1, keepdims=True) + NORM_EPS) * g_ref[...]
            h_out[...] = h.astype(BF16)
            ht_out[...] = h.T.astype(BF16)
            return
        x_ref, g_ref = rest[:2]
        dx_ref, dxb_ref, dg_ref = rest[-3:]

        @pl.when(pl.program_id(0) == 0)
        def _():
            dg_ref[...] = jnp.zeros_like(dg_ref)

        xv = x_ref[...]
        rs = lax.rsqrt(jnp.mean(xv * xv, axis=1, keepdims=True) + NORM_EPS)
        xhat = xv * rs
        rg = r * g_ref[...]
        dx = resid_ref[...] + rs * (rg - xhat * jnp.mean(rg * xhat, axis=1, keepdims=True))
        dx_ref[...] = dx
        dxb_ref[...] = dx.astype(BF16)
        dg_ref[...] += jnp.sum(r * xhat, axis=0, keepdims=True)

    row = lambda width: _spec((tm, width), lambda i: (i, 0))
    in_specs = [row(p.shape[1]) for p in pieces] + [_spec(w.shape, lambda i: (0, 0, 0)), row(d)]
    args = pieces + [w, resid]
    scratch = []
    if loss is not None:
        in_specs += [_spec((1, d), lambda i: (0, 0)), row(d)]
        args += list(loss)
        out_specs = [row(d), row(d), _spec((1, d), lambda i: (0, 0)), _spec((1, 128), lambda i: (0, 0))]
        out_shape = [jax.ShapeDtypeStruct((s, d), F32), jax.ShapeDtypeStruct((s, d), BF16),
                     jax.ShapeDtypeStruct((1, d), F32), jax.ShapeDtypeStruct((1, 128), F32)]
        scratch = [pltpu.VMEM((1, d), F32)]
    elif norm is None and next_gain is None:
        out_specs, out_shape = row(d), jax.ShapeDtypeStruct((s, d), F32)
    elif norm is None:
        in_specs.append(_spec((1, d), lambda i: (0, 0)))
        args.append(next_gain)
        out_specs = [row(d), row(d), _spec((d, tm), lambda i: (0, i))]
        out_shape = [jax.ShapeDtypeStruct((s, d), F32), jax.ShapeDtypeStruct((s, d), BF16),
                     jax.ShapeDtypeStruct((d, s), BF16)]
    else:
        in_specs += [row(d), _spec((1, d), lambda i: (0, 0))]
        args += list(norm)
        out_specs = [row(d), row(d), _spec((1, d), lambda i: (0, 0))]
        out_shape = [jax.ShapeDtypeStruct((s, d), F32), jax.ShapeDtypeStruct((s, d), BF16),
                     jax.ShapeDtypeStruct((1, d), F32)]
    if after is not None:
        in_specs.append(pl.BlockSpec(memory_space=pl.ANY))
        args.append(after)
    return pl.pallas_call(body, name=name, grid=(s // tm,), in_specs=in_specs, out_specs=out_specs,
                          out_shape=out_shape, scratch_shapes=scratch, compiler_params=_params(1))(*args)


def _mm_dw(name, a_t, b, by_cols, tk):
    pieces = list(b) if isinstance(b, (list, tuple)) else [b]
    n_b = len(pieces)
    m, s = a_t.shape
    n = sum(p.shape[1] for p in pieces)
    shape = (N_CHIPS, m, n // N_CHIPS) if by_cols else (N_CHIPS, m // N_CHIPS, n)
    n_steps = s // tk

    def body(a_ref, *rest):
        b_refs, o_ref, acc = rest[:n_b], rest[n_b], rest[-1]

        @pl.when(pl.program_id(0) == 0)
        def _():
            acc[...] = jnp.zeros_like(acc)

        bv = b_refs[0][...] if n_b == 1 else jnp.concatenate([ref[...] for ref in b_refs], axis=1)
        for j in range(N_CHIPS):
            if by_cols:
                acc[j] += _dot(a_ref[...], bv[:, j * shape[2]:(j + 1) * shape[2]])
            else:
                acc[j] += _dot(a_ref[j * shape[1]:(j + 1) * shape[1], :], bv)

        @pl.when(pl.program_id(0) == n_steps - 1)
        def _():
            o_ref[...] = acc[...].astype(BF16)

    return pl.pallas_call(
        body, name=name, grid=(n_steps,),
        in_specs=[_spec((m, tk), lambda k: (0, k))] + [_spec((tk, p.shape[1]), lambda k: (k, 0)) for p in pieces],
        out_specs=[_spec(shape, lambda k: (0, 0, 0)), ANY_SPEC],
        out_shape=[jax.ShapeDtypeStruct(shape, BF16),
                   jax.ShapeDtypeStruct((N_DEV, shape[1] // 2, shape[2]), BF16)],
        scratch_shapes=[pltpu.VMEM(shape, F32)],
        compiler_params=_params(1))(a_t, *pieces)


def _rms_fwd(name, x, gain):
    s, d = x.shape
    t = ROW_TILE

    def body(x_ref, g_ref, h_ref, ht_ref):
        xv = x_ref[...]
        r = lax.rsqrt(jnp.mean(xv * xv, axis=1, keepdims=True) + NORM_EPS)
        h = xv * r * g_ref[...]
        h_ref[...] = h.astype(BF16)
        ht_ref[...] = h.T.astype(BF16)

    return pl.pallas_call(
        body, name=name, grid=(s // t,),
        in_specs=[_spec((t, d), lambda i: (i, 0)), _spec((1, d), lambda i: (0, 0))],
        out_specs=[_spec((t, d), lambda i: (i, 0)), _spec((d, t), lambda i: (0, i))],
        out_shape=[jax.ShapeDtypeStruct((s, d), BF16), jax.ShapeDtypeStruct((d, s), BF16)],
        compiler_params=_params(1),
    )(x, gain)


def _rope_tables():
    half = HEAD_DIM // 2
    inv_freq = ROPE_THETA ** (-jnp.arange(half, dtype=F32) / half)
    ang = jnp.arange(SEQ, dtype=jnp.int32).astype(F32)[:, None] * inv_freq[None, :]
    cos, sin = jnp.cos(ang), jnp.sin(ang)
    cos_t = jnp.concatenate([cos, cos, cos, cos], axis=1)
    sin_t = jnp.concatenate([-sin, sin, -sin, sin], axis=1)
    return cos_t, sin_t


def _swap_halves(x):
    lane = lax.broadcasted_iota(jnp.int32, x.shape, 1)
    first = (lane % HEAD_DIM) < (HEAD_DIM // 2)
    return jnp.where(first, pltpu.roll(x, 128 - HEAD_DIM // 2, 1), pltpu.roll(x, HEAD_DIM // 2, 1))


def _permuted_specs(t, width):
    specs = [_spec((t, width), lambda i: (i, 0))]
    for d in DILATIONS[1:]:
        specs.append(_spec((d, t // d, width), lambda i: (0, i, 0)))
    return specs


def _permuted_shapes(width, dtype):
    shapes = [jax.ShapeDtypeStruct((SEQ, width), dtype)]
    for d in DILATIONS[1:]:
        shapes.append(jax.ShapeDtypeStruct((d, SEQ // d, width), dtype))
    return shapes


def _attn_prep(name, proj, cos_t, sin_t):
    t = ROW_TILE
    w = ATTN_WIDTH

    def body(q_ref, k_ref, v_ref, cos_ref, sin_ref, *rest):
        outs, scr = rest[:9], rest[9]
        cosv, sinv = cos_ref[...], sin_ref[...]
        for a, (src, roped, scale) in enumerate(((q_ref, True, HEAD_DIM ** -0.5),
                                                 (k_ref, True, 1.0), (v_ref, False, 1.0))):
            o1, o4, o16 = outs[3 * a:3 * a + 3]
            for cb in range(w // 128):
                cols = slice(cb * 128, (cb + 1) * 128)
                val = src[:, cols]
                if roped:
                    val = (val * cosv + _swap_halves(val) * sinv) * scale
                scr[...] = val
                o1[:, cols] = val.astype(BF16)
                for o_ref, d in ((o4, 4), (o16, 16)):
                    for r in range(d):
                        o_ref[r, :, cols] = scr[pl.ds(r, t // d, stride=d), :].astype(BF16)

    out_specs = _permuted_specs(t, w) * 3
    out_shape = _permuted_shapes(w, BF16) * 3
    outs = pl.pallas_call(
        body, name=name, grid=(SEQ // t,),
        in_specs=[_spec((t, w), lambda i: (i, 0)), _spec((t, w), lambda i: (i, 1)),
                  _spec((t, w), lambda i: (i, 2)),
                  _spec((t, 128), lambda i: (i, 0)), _spec((t, 128), lambda i: (i, 0))],
        out_specs=out_specs, out_shape=out_shape,
        scratch_shapes=[pltpu.VMEM((t, 128), F32)],
        compiler_params=_params(1),
    )(proj, proj, proj, cos_t, sin_t)
    q, k, v = outs[0:3], outs[3:6], outs[6:9]
    flat = lambda arr: arr.reshape(SEQ, w)
    return [(flat(q[p]), flat(k[p]), flat(v[p])) for p in range(3)]


def _band_masks():
    row = lax.broadcasted_iota(jnp.int32, (2 * SPAN, 2 * SPAN), 0) % SPAN
    col = lax.broadcasted_iota(jnp.int32, (2 * SPAN, 2 * SPAN), 1)
    is_prev = col < SPAN
    band = (is_prev & (col >= row)) | (~is_prev & (col - SPAN <= row))
    head0 = lax.broadcasted_iota(jnp.int32, (SPAN, 128), 1) < HEAD_DIM
    return band, is_prev, head0


def _stack_heads(x, head0):
    zero = jnp.zeros_like(x)
    return jnp.concatenate([jnp.where(head0, x, zero), jnp.where(head0, zero, x)], axis=0)


def _attn_fwd(name, q, k, v, seg_blocks):
    n_blocks = SEQ // SPAN

    def body(q_ref, k_ref, v_ref, o_ref, lse_ref):
        band, is_prev, head0 = _band_masks()

        def step(b, carry):
            cur = pl.ds(pl.multiple_of(b * SPAN, SPAN), SPAN)
            prev = pl.ds(pl.multiple_of(jnp.maximum(b - 1, 0) * SPAN, SPAN), SPAN)
            qs = _stack_heads(q_ref[cur, :], head0)
            kcat = jnp.concatenate([k_ref[prev, :], k_ref[cur, :]], axis=0)
            vcat = jnp.concatenate([v_ref[prev, :], v_ref[cur, :]], axis=0)
            ok = band & (((b % seg_blocks) != 0) | ~is_prev)
            s = jnp.where(ok, _dot_nt(qs, kcat), MASK_VALUE)
            m = jnp.max(s, axis=1, keepdims=True)
            p = jnp.exp(s - m)
            l = jnp.sum(p, axis=1, keepdims=True)
            pv = _dot(p, vcat) * (1.0 / l)
            lse = m + jnp.log(l)
            o_ref[cur, :] = jnp.where(head0, pv[:SPAN], pv[SPAN:])
            lse_ref[cur, :] = jnp.where(head0, lse[:SPAN], lse[SPAN:])
            return carry

        lax.fori_loop(0, n_blocks, step, 0, unroll=4)

    col = _spec((SEQ, 128), lambda j: (0, j))
    return pl.pallas_call(
        body, name=name, grid=(ATTN_WIDTH // 128,),
        in_specs=[col, col, col], out_specs=[col, col],
        out_shape=[jax.ShapeDtypeStruct((SEQ, ATTN_WIDTH), F32)] * 2,
        compiler_params=_params(1),
    )(q, k, v)


def _unpermute(dst, src_ref, d, cols):
    n = dst.shape[0] // d
    for r in range(d):
        dst[pl.ds(r, n, stride=d), :] = src_ref[r, :, cols].astype(dst.dtype)


def _attn_merge(name, outs, lses, gain):
    t = ROW_TILE
    w = ATTN_WIDTH

    def body(o1, o4, o16, l1, l4, l16, g_ref, an_ref, ant_ref, attn_ref, lse_ref, so4, so16, sl4, sl16):
        for cb in range(w // 128):
            cols = slice(cb * 128, (cb + 1) * 128)
            _unpermute(so4, o4, 4, cols)
            _unpermute(so16, o16, 16, cols)
            _unpermute(sl4, l4, 4, cols)
            _unpermute(sl16, l16, 16, cols)
            la, lb, lc = l1[:, cols], sl4[...], sl16[...]
            m = jnp.maximum(jnp.maximum(la, lb), lc)
            ea, eb, ec = jnp.exp(la - m), jnp.exp(lb - m), jnp.exp(lc - m)
            tot = ea + eb + ec
            attn_ref[:, cols] = (ea * o1[:, cols] + eb * so4[...] + ec * so16[...]) / tot
            lse_ref[:, cols] = m + jnp.log(tot)
        attn = attn_ref[...]
        r = lax.rsqrt(jnp.mean(attn * attn, axis=1, keepdims=True) + NORM_EPS)
        an = attn * r * g_ref[...]
        an_ref[...] = an.astype(BF16)
        ant_ref[...] = an.T.astype(BF16)

    views = lambda arrs: [arrs[0], arrs[1].reshape(4, SEQ // 4, w), arrs[2].reshape(16, SEQ // 16, w)]
    row = _spec((t, w), lambda i: (i, 0))
    return pl.pallas_call(
        body, name=name, grid=(SEQ // t,),
        in_specs=_permuted_specs(t, w) * 2 + [_spec((1, w), lambda i: (0, 0))],
        out_specs=[row, _spec((w, t), lambda i: (0, i)), row, row],
        out_shape=[jax.ShapeDtypeStruct((SEQ, 2 * w), BF16), jax.ShapeDtypeStruct((2 * w, SEQ), BF16),
                   jax.ShapeDtypeStruct((SEQ, w), F32), jax.ShapeDtypeStruct((SEQ, w), F32)],
        scratch_shapes=[pltpu.VMEM((t, 128), F32)] * 4,
        compiler_params=_params(1),
    )(*views(outs), *views(lses), gain)


def _head_sum_matrix():
    i = np.arange(ATTN_WIDTH)
    return jnp.asarray((i[:, None] // HEAD_DIM) == (i[None, :] // HEAD_DIM), dtype=F32)


def _attn_bwd_prep(name, d_an, attn, lse, gain, head_sum):
    t = ROW_TILE
    w = ATTN_WIDTH

    def body(dan_ref, attn_ref, lse_ref, g_ref, hs_ref, *rest):
        (do1, do4, do16, dl1, dl4, dl16, ls4, ls16, dg_ref), (sdo, sdl, sls) = rest[:9], rest[9:]

        @pl.when(pl.program_id(0) == 0)
        def _():
            dg_ref[...] = jnp.zeros_like(dg_ref)

        attn = attn_ref[...]
        dan = dan_ref[...]
        r = lax.rsqrt(jnp.mean(attn * attn, axis=1, keepdims=True) + NORM_EPS)
        xhat = attn * r
        dg_ref[...] += jnp.sum(dan * xhat, axis=0, keepdims=True)
        dang = dan * g_ref[...]
        d_o = r * (dang - xhat * jnp.mean(dang * xhat, axis=1, keepdims=True))
        delta = jnp.dot(d_o * attn, hs_ref[...], preferred_element_type=F32,
                        precision=lax.Precision.HIGHEST)
        do1[...] = d_o.astype(BF16)
        dl1[...] = delta
        for cb in range(w // 128):
            cols = slice(cb * 128, (cb + 1) * 128)
            sdo[...] = d_o[:, cols]
            sdl[...] = delta[:, cols]
            sls[...] = lse_ref[:, cols]
            for d, o_do, o_dl, o_ls in ((4, do4, dl4, ls4), (16, do16, dl16, ls16)):
                for rr in range(d):
                    rows = pl.ds(rr, t // d, stride=d)
                    o_do[rr, :, cols] = sdo[rows, :].astype(BF16)
                    o_dl[rr, :, cols] = sdl[rows, :]
                    o_ls[rr, :, cols] = sls[rows, :]

    row = _spec((t, w), lambda i: (i, 0))
    perm = _permuted_specs(t, w)
    outs = pl.pallas_call(
        body, name=name, grid=(SEQ // t,),
        in_specs=[row, row, row, _spec((1, w), lambda i: (0, 0)), _spec((w, w), lambda i: (0, 0))],
        out_specs=perm + perm + perm[1:] + [_spec((1, w), lambda i: (0, 0))],
        out_shape=(_permuted_shapes(w, BF16) + _permuted_shapes(w, F32) + _permuted_shapes(w, F32)[1:]
                   + [jax.ShapeDtypeStruct((1, w), F32)]),
        scratch_shapes=[pltpu.VMEM((t, 128), F32)] * 3,
        compiler_params=_params(1),
    )(d_an, attn, lse, gain, head_sum)
    flat = lambda arr: arr.reshape(SEQ, w)
    d_out = [flat(a) for a in outs[0:3]]
    delta = [flat(a) for a in outs[3:6]]
    lses = [lse, flat(outs[6]), flat(outs[7])]
    return d_out, delta, lses, outs[8]


def _attn_bwd(name, q, k, v, d_out, delta, lse, seg_blocks):
    n_blocks = SEQ // SPAN

    def body(q_ref, k_ref, v_ref, do_ref, dl_ref, lse_ref, dq_ref, dk_out, dv_out, dk_ref, dv_ref):
        band, is_prev, head0 = _band_masks()
        dk_ref[...] = jnp.zeros_like(dk_ref)
        dv_ref[...] = jnp.zeros_like(dv_ref)

        def per_head(x):
            return jnp.concatenate([x[:, 0:1], x[:, HEAD_DIM:HEAD_DIM + 1]], axis=0)

        def step(b, carry):
            cur = pl.ds(pl.multiple_of(b * SPAN, SPAN), SPAN)
            prev = pl.ds(pl.multiple_of(jnp.maximum(b - 1, 0) * SPAN, SPAN), SPAN)
            qs = _stack_heads(q_ref[cur, :], head0)
            dos = _stack_heads(do_ref[cur, :], head0)
            kcat = jnp.concatenate([k_ref[prev, :], k_ref[cur, :]], axis=0)
            vcat = jnp.concatenate([v_ref[prev, :], v_ref[cur, :]], axis=0)
            ok = band & (((b % seg_blocks) != 0) | ~is_prev)
            p = jnp.where(ok, jnp.exp(_dot_nt(qs, kcat) - per_head(lse_ref[cur, :])), 0.0)
            ds = p * (_dot_nt(dos, vcat) - per_head(dl_ref[cur, :]))
            dq = _dot(ds, kcat)
            dq_ref[cur, :] = jnp.where(head0, dq[:SPAN], dq[SPAN:]).astype(BF16)
            dk = _dot_tn(ds, qs)
            dv = _dot_tn(p, dos)
            dk_ref[prev, :] += dk[:SPAN]
            dv_ref[prev, :] += dv[:SPAN]
            dk_ref[cur, :] += dk[SPAN:]
            dv_ref[cur, :] += dv[SPAN:]
            return carry

        lax.fori_loop(0, n_blocks, step, 0, unroll=4)
        dk_out[...] = dk_ref[...].astype(BF16)
        dv_out[...] = dv_ref[...].astype(BF16)

    col = _spec((SEQ, 128), lambda j: (0, j))
    return pl.pallas_call(
        body, name=name, grid=(ATTN_WIDTH // 128,),
        in_specs=[col] * 6, out_specs=[col] * 3,
        out_shape=[jax.ShapeDtypeStruct((SEQ, ATTN_WIDTH), BF16)] * 3,
        scratch_shapes=[pltpu.VMEM((SEQ, 128), F32)] * 2,
        compiler_params=_params(1),
    )(q, k, v, d_out, delta, lse)


def _attn_bwd_post(name, grads, cos_t, sin_t):
    t = ROW_TILE
    w = ATTN_WIDTH

    def body(*refs):
        ins, cos_ref, sin_ref, out_ref, s4, s16 = refs[:9], refs[9], refs[10], refs[11], refs[12], refs[13]
        cosv, sinv = cos_ref[...], sin_ref[...]
        for a in range(3):
            g1, g4, g16 = ins[a], ins[3 + a], ins[6 + a]
            for cb in range(w // 128):
                cols = slice(cb * 128, (cb + 1) * 128)
                _unpermute(s4, g4, 4, cols)
                _unpermute(s16, g16, 16, cols)
                val = g1[:, cols].astype(F32) + s4[...] + s16[...]
                if a < 2:
                    val = val * cosv + _swap_halves(val * sinv)
                if a == 0:
                    val = val * (HEAD_DIM ** -0.5)
                out_ref[:, a * w + cb * 128:a * w + (cb + 1) * 128] = val.astype(BF16)

    views = []
    for p, d in enumerate(DILATIONS):
        for a in range(3):
            views.append(grads[p][a] if d == 1 else grads[p][a].reshape(d, SEQ // d, w))
    perm = _permuted_specs(t, w)
    in_specs = [perm[0]] * 3 + [perm[1]] * 3 + [perm[2]] * 3
    return pl.pallas_call(
        body, name=name, grid=(SEQ // t,),
        in_specs=in_specs + [_spec((t, 128), lambda i: (i, 0))] * 2,
        out_specs=_spec((t, 3 * w), lambda i: (i, 0)),
        out_shape=jax.ShapeDtypeStruct((SEQ, 3 * w), BF16),
        scratch_shapes=[pltpu.VMEM((t, 128), F32)] * 2,
        compiler_params=_params(1),
    )(*views, cos_t, sin_t)


N_LEVELS = 7
HGRN_PAIR = 2


def _hgrn_consts():
    c = CHUNK
    i = np.arange(c)[:, None]
    s = np.arange(c)[None, :]
    blocks = [s <= i]
    for lv in range(N_LEVELS):
        bs = c >> lv
        h = bs // 2
        m = (i // bs) * bs + h - 1
        second = (i % bs) >= h
        blocks.append((second & (s > m) & (s <= i)) | (~second & (s > i) & (s <= m)))
    blocks.append(s > i)
    stack = np.concatenate(blocks, axis=0).astype(np.float32)
    return jnp.asarray(stack, dtype=BF16), jnp.asarray(stack.T, dtype=BF16)


def _exact_dot(m01, x):
    hi = x.astype(BF16)
    lo = (x - hi.astype(F32)).astype(BF16)
    n = x.shape[1]
    full = jnp.dot(m01, jnp.concatenate([hi, lo], axis=1), preferred_element_type=F32)
    return full[:, :n] + full[:, n:]


def _hgrn_gates(qh, z, lb):
    sq = _sigmoid(qh)
    q = qh * sq * (HGRN_DIM ** -0.5)
    sig = _sigmoid(z)
    sigm = _sigmoid(-z)
    f = lb + (1.0 - lb) * sig
    k = (1.0 - lb) * sigm
    return q, k, f, sq, sig, sigm


def _level_masks(lv):
    row = lax.broadcasted_iota(jnp.int32, (CHUNK, CHUNK), 0)
    col = lax.broadcasted_iota(jnp.int32, (CHUNK, CHUNK), 1)
    shift = N_LEVELS - lv
    second = (row & (CHUNK >> (lv + 1))) != 0
    same = (row >> shift) == (col >> shift)
    return second, same


def _hgrn_fwd(name, proj, lb, gain, stack, mixed, mixed_t):
    t = ROW_TILE
    per = t // CHUNK
    n_rb = SEQ // t
    n_chunks = SEQ // CHUNK
    col0 = 3 * ATTN_WIDTH // 128
    pair_w = HGRN_PAIR * HGRN_DIM

    def body(q_ref, f_ref, i_ref, g_ref, lb_ref, gain_ref, stack_ref, mixed_in, mixed_t_in,
             rec_ref, rect_ref, o_ref, st_out, a_out, st):
        del mixed_in, mixed_t_in

        @pl.when(pl.program_id(1) == 0)
        def _():
            st[...] = jnp.zeros_like(st)

        row = lax.broadcasted_iota(jnp.int32, (CHUNK, CHUNK), 0)
        col = lax.broadcasted_iota(jnp.int32, (CHUNK, CHUNK), 1)
        for c, hh in [(c, hh) for c in range(per) for hh in range(HGRN_PAIR)]:
            rows = slice(c * CHUNK, (c + 1) * CHUNK)
            lanes = slice(hh * HGRN_DIM, (hh + 1) * HGRN_DIM)
            lbv = lb_ref[hh]
            qh, z, v, gh = q_ref[rows, lanes], f_ref[rows, lanes], i_ref[rows, lanes], g_ref[rows, lanes]
            q, k, f, _, _, _ = _hgrn_gates(qh, z, lbv)
            dec = _exact_dot(stack_ref[...], jnp.log(f))
            g = dec[0:CHUNK]
            to_end = dec[(N_LEVELS + 1) * CHUNK:(N_LEVELS + 2) * CHUNK]
            a = jnp.where(row == col, jnp.sum(q * k, axis=1, keepdims=True), 0.0)
            for lv in range(N_LEVELS):
                e = jnp.exp(dec[(lv + 1) * CHUNK:(lv + 2) * CHUNK])
                second, same = _level_masks(lv)
                qt = jnp.where(second, q * e, 0.0)
                kt = jnp.where(second, 0.0, k * e)
                a = a + jnp.where(same, _dot_nt(qt, kt), 0.0)
            st_prev = st[hh]
            st_out[hh, c] = st_prev
            a_out[hh, c] = a
            o = _dot(a, v) + _dot_nt(q * jnp.exp(g), st_prev)
            k_end = k * jnp.exp(to_end)
            st[hh] = st_prev * jnp.exp(g[CHUNK - 1:CHUNK, :]) + _dot(v.T, k_end)
            o_ref[rows, lanes] = o
            r = lax.rsqrt(jnp.mean(o * o, axis=1, keepdims=True) + NORM_EPS)
            rec = o * r * gain_ref[...] * (gh * _sigmoid(gh))
            rec_ref[rows, lanes] = rec.astype(BF16)
            rect_ref[lanes, rows] = rec.T.astype(BF16)

    def col_spec(tt):
        return _spec((t, pair_w), lambda h, rb: (rb, (col0 + HGRN_HEADS * tt) // HGRN_PAIR + h))

    chunk_spec = _spec((HGRN_PAIR, per, CHUNK, CHUNK), lambda h, rb: (h, rb, 0, 0))
    return pl.pallas_call(
        body, name=name, grid=(HGRN_HEADS // HGRN_PAIR, n_rb),
        in_specs=[col_spec(0), col_spec(1), col_spec(2), col_spec(3),
                  _spec((HGRN_PAIR, 1, HGRN_DIM), lambda h, rb: (h, 0, 0)),
                  _spec((1, HGRN_DIM), lambda h, rb: (0, 0)),
                  _spec(stack.shape, lambda h, rb: (0, 0)), ANY_SPEC, ANY_SPEC],
        out_specs=[_spec((t, pair_w), lambda h, rb: (rb, ATTN_WIDTH // pair_w + h)),
                   _spec((pair_w, t), lambda h, rb: (ATTN_WIDTH // pair_w + h, rb)),
                   _spec((t, pair_w), lambda h, rb: (rb, h)),
                   chunk_spec, chunk_spec],
        out_shape=[jax.ShapeDtypeStruct(mixed.shape, BF16),
                   jax.ShapeDtypeStruct(mixed_t.shape, BF16),
                   jax.ShapeDtypeStruct((SEQ, HGRN_WIDTH), F32),
                   jax.ShapeDtypeStruct((HGRN_HEADS, n_chunks, CHUNK, CHUNK), F32),
                   jax.ShapeDtypeStruct((HGRN_HEADS, n_chunks, CHUNK, CHUNK), F32)],
        scratch_shapes=[pltpu.VMEM((HGRN_PAIR, CHUNK, CHUNK), F32)],
        input_output_aliases={7: 0, 8: 1},
        compiler_params=_params(2),
    )(proj, proj, proj, proj, lb, gain, stack, mixed, mixed_t)


def _hgrn_bwd(name, proj, d_rec, o_pre, states, scores, lb, gain, stack, stack_t):
    t = ROW_TILE
    per = t // CHUNK
    n_rb = SEQ // t
    col0 = 3 * ATTN_WIDTH // 128
    pair_w = HGRN_PAIR * HGRN_DIM

    def body(q_ref, f_ref, i_ref, g_ref, drec_ref, o_ref, st_ref, a_ref, lb_ref, gain_ref,
             stack_ref, stack_t_ref, dq_ref, df_ref, di_ref, dg_ref, dlb_ref, dgain_ref, dst):
        @pl.when(pl.program_id(1) == 0)
        def _():
            dst[...] = jnp.zeros_like(dst)
            dlb_ref[...] = jnp.zeros_like(dlb_ref)
            dgain_ref[...] = jnp.zeros_like(dgain_ref)

        gain_v = gain_ref[...]
        row = lax.broadcasted_iota(jnp.int32, (CHUNK, CHUNK), 0)
        col = lax.broadcasted_iota(jnp.int32, (CHUNK, CHUNK), 1)
        for c, hh in [(c, hh) for c in reversed(range(per)) for hh in range(HGRN_PAIR)]:
            rows = slice(c * CHUNK, (c + 1) * CHUNK)
            lanes = slice(hh * HGRN_DIM, (hh + 1) * HGRN_DIM)
            lbv = lb_ref[hh]
            qh, z, v, gh = q_ref[rows, lanes], f_ref[rows, lanes], i_ref[rows, lanes], g_ref[rows, lanes]
            q, k, f, sq, sig, sigm = _hgrn_gates(qh, z, lbv)
            dec = _exact_dot(stack_ref[...], jnp.log(f))
            g = dec[0:CHUNK]
            to_end = dec[(N_LEVELS + 1) * CHUNK:(N_LEVELS + 2) * CHUNK]
            e_g = jnp.exp(g)
            e_end = jnp.exp(to_end)
            e_last = jnp.exp(g[CHUNK - 1:CHUNK, :])
            q_in = q * e_g
            k_end = k * e_end
            st_prev = st_ref[hh, c]
            a = a_ref[hh, c]
            dst_new = dst[hh]

            o = o_ref[rows, lanes]
            drec = drec_ref[rows, lanes]
            sg = _sigmoid(gh)
            r = lax.rsqrt(jnp.mean(o * o, axis=1, keepdims=True) + NORM_EPS)
            ohat = o * r
            d_gh = drec * (ohat * gain_v) * (sg * (1.0 + gh * (1.0 - sg)))
            d_on = drec * (gh * sg)
            dgain_ref[hh] += jnp.sum(d_on * ohat, axis=0, keepdims=True)
            d_ohat = d_on * gain_v
            d_o = r * (d_ohat - ohat * jnp.mean(d_ohat * ohat, axis=1, keepdims=True))

            d_a = jnp.where(row >= col, _dot_nt(d_o, v), 0.0)
            d_at = jnp.where(col >= row, _dot_nt(v, d_o), 0.0)
            d_v = _dot(a.T, d_o) + _dot_nt(k_end, dst_new)
            d_q_in = _dot(d_o, st_prev)
            d_k_end = _dot(v, dst_new)
            d_q = d_q_in * e_g
            d_k = d_k_end * e_end
            diag = jnp.sum(d_o * v, axis=1, keepdims=True)
            d_q = d_q + diag * k
            d_k = d_k + diag * q
            d_dec = [q_in * d_q_in]
            for lv in range(N_LEVELS):
                e = jnp.exp(dec[(lv + 1) * CHUNK:(lv + 2) * CHUNK])
                second, same = _level_masks(lv)
                qt = jnp.where(second, q * e, 0.0)
                kt = jnp.where(second, 0.0, k * e)
                d_qt = _dot(jnp.where(same, d_a, 0.0), kt)
                d_kt = _dot(jnp.where(same, d_at, 0.0), qt)
                d_q = d_q + jnp.where(second, d_qt * e, 0.0)
                d_k = d_k + jnp.where(second, 0.0, d_kt * e)
                d_dec.append(jnp.where(second, qt * d_qt, kt * d_kt))
            d_dec.append(k_end * d_k_end)
            flux = jnp.sum(dst_new * st_prev, axis=0, keepdims=True) * e_last
            d_lf = _exact_dot(stack_t_ref[...], jnp.concatenate(d_dec, axis=0)) + flux
            dst[hh] = dst_new * e_last + _dot(d_o.T, q_in)

            d_f = d_lf / f - d_k
            dlb_ref[hh] += jnp.sum(d_f * sigm, axis=0, keepdims=True)
            dq_ref[rows, lanes] = (d_q * (HGRN_DIM ** -0.5) * (sq * (1.0 + qh * (1.0 - sq)))).astype(BF16)
            df_ref[rows, lanes] = (d_f * (1.0 - lbv) * sig * sigm).astype(BF16)
            di_ref[rows, lanes] = d_v.astype(BF16)
            dg_ref[rows, lanes] = d_gh.astype(BF16)

    last = n_rb - 1

    def col_spec(tt):
        return _spec((t, pair_w), lambda h, rb: (last - rb, (col0 + HGRN_HEADS * tt) // HGRN_PAIR + h))

    head_col = _spec((t, pair_w), lambda h, rb: (last - rb, h))
    rec_col0 = (d_rec.shape[1] - HGRN_WIDTH) // pair_w
    d_rec_col = _spec((t, pair_w), lambda h, rb: (last - rb, rec_col0 + h))
    chunk_spec = _spec((HGRN_PAIR, per, CHUNK, CHUNK), lambda h, rb: (h, last - rb, 0, 0))
    vec_spec = _spec((HGRN_PAIR, 1, HGRN_DIM), lambda h, rb: (h, 0, 0))
    outs = pl.pallas_call(
        body, name=name, grid=(HGRN_HEADS // HGRN_PAIR, n_rb),
        in_specs=[col_spec(0), col_spec(1), col_spec(2), col_spec(3), d_rec_col, head_col,
                  chunk_spec, chunk_spec, vec_spec,
                  _spec((1, HGRN_DIM), lambda h, rb: (0, 0)),
                  _spec(stack.shape, lambda h, rb: (0, 0)), _spec(stack_t.shape, lambda h, rb: (0, 0))],
        out_specs=[head_col] * 4 + [vec_spec, vec_spec],
        out_shape=[jax.ShapeDtypeStruct((SEQ, HGRN_WIDTH), BF16)] * 4
                  + [jax.ShapeDtypeStruct((HGRN_HEADS, 1, HGRN_DIM), F32)] * 2,
        scratch_shapes=[pltpu.VMEM((HGRN_PAIR, CHUNK, CHUNK), F32)],
        compiler_params=_params(2),
    )(proj, proj, proj, proj, d_rec, o_pre, states, scores, lb, gain, stack, stack_t)
    return outs


ANY_SPEC = pl.BlockSpec(memory_space=pl.ANY)


def _my_place():
    return lax.axis_index("x"), lax.axis_index("y"), lax.axis_index("c")


def _other_chips(x, y):
    return [(1 - x, y), (x, 1 - y), (1 - x, 1 - y)]


def _remote(src, dst, send_sem, recv_sem, device):
    return pltpu.make_async_remote_copy(src_ref=src, dst_ref=dst, send_sem=send_sem, recv_sem=recv_sem,
                                        device_id=device, device_id_type=MESH)


def _staged_copies(srcs, dsts, stage, sems):
    loads = [pltpu.make_async_copy(srcs[i], stage[i], sems.at[i]) for i in range(len(srcs))]
    for cp in loads:
        cp.start()
    stores = []
    for i, cp in enumerate(loads):
        cp.wait()
        stores.append(pltpu.make_async_copy(stage[i], dsts[i], sems.at[i]))
        stores[-1].start()
    return stores


def _gather_weights(name, shards):
    n = len(shards)

    def body(*refs):
        ins, outs = refs[:n], refs[n:2 * n]
        ici_send, ici_recv, d2d_send, d2d_recv, local_sems = refs[2 * n:2 * n + 5]
        stage = refs[2 * n + 5:]
        x, y, c = _my_place()
        me = 2 * x + y
        chips = _other_chips(x, y)

        def half(i, which):
            h = ins[i].shape[0] // 2
            return pl.ds(which * h, h)

        sends = []
        for i in range(n):
            for j, (px, py) in enumerate(chips):
                sends.append(_remote(ins[i].at[half(i, c), :], outs[i].at[me, half(i, c), :],
                                     ici_send.at[3 * i + j], ici_recv.at[3 * i + j], (px, py, c)))
        for cp in sends:
            cp.start()
        local = _staged_copies(ins, [outs[i].at[me] for i in range(n)], stage, local_sems)
        for i in range(n):
            for j, (px, py) in enumerate(chips):
                landed = outs[i].at[2 * px + py, half(i, c), :]
                _remote(landed, landed, ici_send.at[3 * i + j], ici_recv.at[3 * i + j], (px, py, c)).wait_recv()
                forward = _remote(landed, landed, d2d_send.at[3 * i + j], d2d_recv.at[3 * i + j], (x, y, 1 - c))
                forward.start()
                sends.append(forward)
        for i in range(n):
            for j, (px, py) in enumerate(chips):
                other = outs[i].at[2 * px + py, half(i, 1 - c), :]
                _remote(other, other, d2d_send.at[3 * i + j], d2d_recv.at[3 * i + j], (x, y, 1 - c)).wait_recv()
        for cp in sends:
            cp.wait_send()
        for cp in local:
            cp.wait()

    return pl.pallas_call(
        body, name=name, in_specs=[ANY_SPEC] * n, out_specs=[ANY_SPEC] * n,
        out_shape=[jax.ShapeDtypeStruct((N_CHIPS,) + s.shape, s.dtype) for s in shards],
        scratch_shapes=([pltpu.SemaphoreType.DMA((3 * n,))] * 4 + [pltpu.SemaphoreType.DMA((n,))]
                        + [pltpu.VMEM(s.shape, s.dtype) for s in shards]),
        compiler_params=pltpu.CompilerParams(vmem_limit_bytes=VMEM_LIMIT),
    )(*shards)


HBM_SPEC = pl.BlockSpec(memory_space=pltpu.HBM)
SEM_SPEC = pl.BlockSpec(memory_space=pltpu.SEMAPHORE)
SPLIT_PARAMS = pltpu.CompilerParams(has_side_effects=pltpu.SideEffectType.DATAFLOW_SIDE_EFFECTING)
N_PEERS = {"gather": N_CHIPS - 1, "scatter": N_DEV - 1}


def _split_copies(ins, lands, send_sems, recv_sems, kind):
    x, y, c = _my_place()
    pairs = []
    for i in range(len(ins)):
        if kind == "gather":
            me = 2 * x + y
            for j, (px, py) in enumerate(_other_chips(x, y)):
                sems = (send_sems.at[3 * i + j], recv_sems.at[3 * i + j], (px, py, c))
                pairs.append((_remote(ins[i], lands[i].at[me], *sems),
                              _remote(ins[i], lands[i].at[2 * px + py], *sems)))
        else:
            me = 4 * x + 2 * y + c
            h = ins[i].shape[1] // 2
            for k in range(1, N_DEV):
                px, py, pc = (x + (k >> 2)) % 2, (y + ((k >> 1) & 1)) % 2, (c + (k & 1)) % 2
                src = ins[i].at[2 * px + py, pl.ds(pc * h, h), :]
                sems = (send_sems.at[7 * i + k - 1], recv_sems.at[7 * i + k - 1], (px, py, pc))
                pairs.append((_remote(src, lands[i].at[me], *sems),
                              _remote(src, lands[i].at[4 * px + 2 * py + pc], *sems)))
    return pairs


def _exchange_start(name, srcs, lands, kind, after=None):
    n = len(srcs)
    n_sems = N_PEERS[kind] * n
    extra = [] if after is None else [after]

    def body(*refs):
        ins, land_refs = refs[:n], refs[n:2 * n]
        send_sems, recv_sems = refs[2 * n + len(extra):2 * n + len(extra) + 2]
        token = refs[-1]
        for send, _ in _split_copies(ins, land_refs, send_sems, recv_sems, kind):
            send.start()
        token[...] = jnp.zeros_like(token)

    arrays = list(srcs) + list(lands)
    outs = pl.pallas_call(
        body, name=name,
        in_specs=[HBM_SPEC] * (2 * n) + [ANY_SPEC] * len(extra),
        out_shape=([pltpu.SemaphoreType.DMA((n_sems,))] * 2 + [pltpu.HBM(a.shape, a.dtype) for a in arrays]
                   + [jax.ShapeDtypeStruct((8, 128), F32)]),
        out_specs=[SEM_SPEC] * 2 + [HBM_SPEC] * (2 * n) + [pl.BlockSpec(memory_space=pltpu.VMEM)],
        input_output_aliases={i: 2 + i for i in range(2 * n)},
        compiler_params=SPLIT_PARAMS,
    )(*[pltpu.with_memory_space_constraint(a, pltpu.HBM) for a in arrays], *extra)
    return outs[:2], outs[2:2 + 2 * n], outs[-1]


def _exchange_wait(name, sems, passed, kind, after):
    n = len(passed) // 2

    def body(*refs):
        ins, land_refs = refs[:n], refs[n:2 * n]
        send_sems, recv_sems = refs[2 * n:2 * n + 2]
        for send, arrive in _split_copies(ins, land_refs, send_sems, recv_sems, kind):
            send.wait_send()
            arrive.wait_recv()

    outs = pl.pallas_call(
        body, name=name,
        in_specs=[HBM_SPEC] * (2 * n) + [SEM_SPEC] * 2 + [ANY_SPEC],
        out_shape=[pltpu.HBM(a.shape, a.dtype) for a in passed],
        out_specs=[HBM_SPEC] * (2 * n),
        input_output_aliases={i: i for i in range(2 * n)},
        compiler_params=SPLIT_PARAMS,
    )(*passed, *sems, after)
    return outs[:n], outs[n:]


def _own_slot(name, own, me):
    r, cc = own.shape
    th = min(r, 512)

    def body(me_ref, x_ref, o_ref):
        del me_ref
        o_ref[...] = x_ref[...]

    grid_spec = pltpu.PrefetchScalarGridSpec(
        num_scalar_prefetch=1, grid=(r // th,),
        in_specs=[pl.BlockSpec((th, cc), lambda i, me_ref: (i, 0))],
        out_specs=pl.BlockSpec((None, th, cc), lambda i, me_ref: (me_ref[0], i, 0)))
    return pl.pallas_call(
        body, name=name, grid_spec=grid_spec,
        out_shape=jax.ShapeDtypeStruct((N_CHIPS, r, cc), own.dtype), compiler_params=_params(1),
    )(me, own)


def _sum_devices(name, landed, own, place):
    n_dev, h, cc = landed.shape
    th = min(h, 256)
    nb = h // th

    def body(place_ref, l_ref, own_ref, o_ref):
        total = None
        for d in range(n_dev):
            piece = jnp.where(place_ref[0] == d, own_ref[...], l_ref[d]).astype(F32)
            total = piece if total is None else total + piece
        o_ref[...] = total

    grid_spec = pltpu.PrefetchScalarGridSpec(
        num_scalar_prefetch=1, grid=(nb,),
        in_specs=[pl.BlockSpec((n_dev, th, cc), lambda i, p: (0, i, 0)),
                  pl.BlockSpec((None, th, cc), lambda i, p: (p[1], p[2] * nb + i, 0))],
        out_specs=pl.BlockSpec((th, cc), lambda i, p: (i, 0)))
    return pl.pallas_call(
        body, name=name, grid_spec=grid_spec,
        out_shape=jax.ShapeDtypeStruct((h, cc), F32), compiler_params=_params(1),
    )(place, landed, own)


def _share_halves(name, halves):
    flat = [t for per_weight in halves for t in per_weight]
    n = len(flat)
    n_w = len(halves)

    def body(*refs):
        ins, outs = refs[:n], refs[n:n + n_w]
        send_sems, recv_sems, local_sems = refs[n + n_w:n + n_w + 3]
        stage = refs[n + n_w + 3:]
        x, y, c = _my_place()
        sends, own = [], []
        for i in range(n):
            w, l = divmod(i, DEPTH)
            h = ins[i].shape[0]
            own.append(outs[w].at[l, pl.ds(c * h, h), :])
            sends.append(_remote(ins[i], own[i], send_sems.at[i], recv_sems.at[i], (x, y, 1 - c)))
        for cp in sends:
            cp.start()
        local = _staged_copies(ins, own, stage, local_sems)
        for i in range(n):
            w, l = divmod(i, DEPTH)
            h = ins[i].shape[0]
            _remote(ins[i], outs[w].at[l, pl.ds((1 - c) * h, h), :], send_sems.at[i], recv_sems.at[i],
                    (x, y, 1 - c)).wait_recv()
        for cp in sends:
            cp.wait_send()
        for cp in local:
            cp.wait()

    return pl.pallas_call(
        body, name=name, in_specs=[ANY_SPEC] * n, out_specs=[ANY_SPEC] * n_w,
        out_shape=[jax.ShapeDtypeStruct((DEPTH, 2 * per_weight[0].shape[0], per_weight[0].shape[1]), F32)
                   for per_weight in halves],
        scratch_shapes=([pltpu.SemaphoreType.DMA((n,))] * 3 + [pltpu.VMEM(t.shape, t.dtype) for t in flat]),
        compiler_params=pltpu.CompilerParams(vmem_limit_bytes=VMEM_LIMIT),
    )(*flat)


def _all_reduce_small(pack, after):
    def body(p_ref, after_ref, o_ref, recv, send_sems, recv_sems):
        del after_ref
        x, y, c = _my_place()
        me = 4 * x + 2 * y + c
        recv[me] = p_ref[...]
        peers = []
        for k in range(1, N_DEV):
            px, py, pc = (x + (k >> 2)) % 2, (y + ((k >> 1) & 1)) % 2, (c + (k & 1)) % 2
            peers.append((px, py, pc))
        sends = [_remote(p_ref, recv.at[me], send_sems.at[k], recv_sems.at[k], peer)
                 for k, peer in enumerate(peers)]
        for cp in sends:
            cp.start()
        for k, (px, py, pc) in enumerate(peers):
            _remote(p_ref, recv.at[4 * px + 2 * py + pc], send_sems.at[k], recv_sems.at[k],
                    (px, py, pc)).wait_recv()
        for cp in sends:
            cp.wait_send()
        total = recv[0]
        for d in range(1, N_DEV):
            total = total + recv[d]
        o_ref[...] = total

    vmem = pl.BlockSpec(memory_space=pltpu.VMEM)
    return pl.pallas_call(
        body, name="all_reduce_small", in_specs=[vmem, ANY_SPEC], out_specs=vmem,
        out_shape=jax.ShapeDtypeStruct(pack.shape, F32),
        scratch_shapes=[pltpu.VMEM((N_DEV,) + pack.shape, F32),
                        pltpu.SemaphoreType.DMA((N_DEV - 1,)), pltpu.SemaphoreType.DMA((N_DEV - 1,))],
    )(pack, after)


def _adamw(name, w, g, m, v):
    r, cc = w.shape
    th = min(r, 256)

    def body(w_ref, g_ref, m_ref, v_ref, d_ref, m_out, v_out):
        gv = g_ref[...]
        m2 = ADAM_B1 * m_ref[...] + (1.0 - ADAM_B1) * gv
        v2 = ADAM_B2 * v_ref[...] + (1.0 - ADAM_B2) * (gv * gv)
        m_hat = m2 / (1.0 - ADAM_B1 ** ADAM_STEP)
        v_hat = v2 / (1.0 - ADAM_B2 ** ADAM_STEP)
        d_ref[...] = -ADAM_LR * (m_hat / (jnp.sqrt(v_hat) + ADAM_EPS) + ADAM_WD * w_ref[...])
        m_out[...] = m2
        v_out[...] = v2

    tile = _spec((th, cc), lambda i: (i, 0))
    return pl.pallas_call(
        body, name=name, grid=(r // th,), in_specs=[tile] * 4, out_specs=[tile] * 3,
        out_shape=[jax.ShapeDtypeStruct((r, cc), F32)] * 3, compiler_params=_params(1),
    )(w, g, m, v)


def _lower_bounds(lb_logits):
    p = jax.nn.softmax(lb_logits.astype(F32), axis=0)
    return jnp.cumsum(p, axis=0) - p[0]


def _layer_forward(l, stream, small, weights, consts, next_gain=None, loss=None, after=None):
    win, rest = weights
    cos_t, sin_t, stack, _, _ = consts
    tm = MM_TILE
    x_in, h, h_t = stream
    saved = {"x_in": x_in}

    proj = _mm_pieces(f"proj{l}", h, win, False, tm, after=after)
    saved.update(h_t=h_t, proj=proj)

    qkv = _attn_prep(f"attn_prep{l}", proj, cos_t, sin_t)
    outs, lses = [], []
    for p, d in enumerate(DILATIONS):
        o, lse = _attn_fwd(f"attn_fwd{l}_{d}", *qkv[p], SEQ // d // SPAN)
        outs.append(o)
        lses.append(lse)
    mixed, mixed_t, attn, lse = _attn_merge(f"attn_merge{l}", outs, lses, small["attn_out_gain"][l][None, :])
    saved.update(qkv=qkv, attn=attn, lse=lse)

    lb3 = small["lower"][l].reshape(HGRN_HEADS, 1, HGRN_DIM)
    mixed, mixed_t, o_pre, states, scores = _hgrn_fwd(f"hgrn_fwd{l}", proj, lb3, small["hgrn_out_gain"][l][None, :],
                                                      stack, mixed, mixed_t)
    wo, wu, wd = rest(mixed)
    saved.update(mixed_t=mixed_t, o_pre=o_pre, states=states, scores=scores, lb3=lb3, weights=(win, wo, wu, wd))

    x_mid, h2, h2_t = _mm_accum(f"out_proj{l}", mixed, wo, False, tm, x_in, next_gain=small["norm_mlp"][l][None, :])
    saved["x_mid"] = x_mid

    a, relu_u, a_t = _mm_pieces(f"up{l}", h2, wu, False, tm, epilogue="relu2")
    new_stream = tuple(_mm_accum(f"down{l}", a, wd, False, tm, x_mid, next_gain=next_gain, loss=loss))
    saved.update(h2_t=h2_t, relu_u=relu_u, a_t=a_t)
    return new_stream, saved


def _layer_backward(l, dx, saved, small, consts, on_grads, after=None):
    win, wo, wu, wd = saved["weights"]
    cos_t, sin_t, stack, stack_t, head_sum = consts
    tm = MM_TILE

    dx, dx_b = dx
    du = _mm_pieces(f"d_u{l}", dx_b, wd, True, tm, epilogue="relu2_grad", extra=saved["relu_u"], after=after)
    d_wd = _mm_dw(f"d_wdown{l}", saved["a_t"], dx_b, False, tm)
    dxm, dxm_b, dg_mlp = _mm_accum(f"d_h2_{l}", du, wu, True, tm, dx,
                                   norm=(saved["x_mid"], small["norm_mlp"][l][None, :]))
    d_wu = _mm_dw(f"d_wup{l}", saved["h2_t"], du, True, tm)
    after_mlp = on_grads(l, "mlp", (d_wu, d_wd))

    d_mixed = _mm_pieces(f"d_mixed{l}", dxm_b, wo, True, tm, after=after_mlp)
    d_wo = _mm_dw(f"d_wout{l}", saved["mixed_t"], dxm_b, False, tm)
    d_rec = d_mixed

    d_out, delta, lses, dg_attn = _attn_bwd_prep(f"attn_bwd_prep{l}", d_mixed, saved["attn"], saved["lse"],
                                                 small["attn_out_gain"][l][None, :], head_sum)
    grads = []
    for p, d in enumerate(DILATIONS):
        grads.append(_attn_bwd(f"attn_bwd{l}_{d}", *saved["qkv"][p], d_out[p], delta[p], lses[p],
                               SEQ // d // SPAN))
    dp_attn = _attn_bwd_post(f"attn_bwd_post{l}", grads, cos_t, sin_t)

    dq_h, df_h, di_h, dg_h, d_lower, dg_hgrn = _hgrn_bwd(
        f"hgrn_bwd{l}", saved["proj"], d_rec, saved["o_pre"], saved["states"], saved["scores"],
        saved["lb3"], small["hgrn_out_gain"][l][None, :], stack, stack_t)
    dproj = [dp_attn, dq_h, df_h, di_h, dg_h]

    d_win = _mm_dw(f"d_win{l}", saved["h_t"], dproj, True, tm)
    after_mix = on_grads(l, "mix", (d_win, d_wo))
    dx_in, dx_in_b, dg_mix = _mm_accum(f"d_h{l}", dproj, win, True, tm, dxm,
                                       norm=(saved["x_in"], small["norm_mix"][l][None, :]), after=after_mix)

    small_grads = {"norm_mix": dg_mix[0], "attn_out_gain": dg_attn[0],
                   "lower": d_lower.reshape(HGRN_WIDTH),
                   "hgrn_out_gain": jnp.sum(dg_hgrn, axis=0).reshape(HGRN_DIM), "norm_mlp": dg_mlp[0]}
    return (dx_in, dx_in_b), after_mix, small_grads


def _local_step(xs, target, small, get_weights, on_grads):
    consts = _rope_tables() + _hgrn_consts() + (_head_sum_matrix(),)
    stream = (xs,) + tuple(_rms_fwd("norm_mix0", xs, small["norm_mix"][0][None, :]))
    saved = []
    for l in range(DEPTH):
        w, after = get_weights(l, stream[0])
        if l + 1 < DEPTH:
            stream, s = _layer_forward(l, stream, small, w, consts, next_gain=small["norm_mix"][l + 1][None, :],
                                       after=after)
        else:
            stream, s = _layer_forward(l, stream, small, w, consts, loss=(small["norm_final"][None, :], target),
                                       after=after)
        saved.append(s)
    dx_f, dx_b, dg_final, loss = stream
    dx = (dx_f, dx_b)
    small_grads = [None] * DEPTH
    after = None
    for l in reversed(range(DEPTH)):
        dx, after, small_grads[l] = _layer_backward(l, dx, saved[l], small, consts, on_grads, after=after)
    return loss, dx[0], dg_final[0], small_grads


def _pack_small(norm_mix, attn_out_gain, lb, hgrn_out_gain, norm_mlp, norm_final, last_row):
    rows = [norm_mix, attn_out_gain.reshape(1, D_MODEL), lb.reshape(1, D_MODEL),
            jnp.pad(hgrn_out_gain.reshape(1, DEPTH * HGRN_DIM), ((0, 0), (0, D_MODEL - DEPTH * HGRN_DIM))),
            norm_mlp, norm_final.reshape(1, D_MODEL), last_row.reshape(1, D_MODEL)]
    pack = jnp.concatenate(rows, axis=0)
    return jnp.pad(pack, ((0, PACK_ROWS - pack.shape[0]), (0, 0)))


def _unpack_small(pack):
    return (pack[0:2], pack[2].reshape(DEPTH, ATTN_WIDTH), pack[3].reshape(DEPTH, HGRN_WIDTH),
            pack[4, :DEPTH * HGRN_DIM].reshape(DEPTH, HGRN_DIM), pack[5:7], pack[7], pack[8])


def kernel(x, norm_mix, w_in, attn_out_gain, hgrn_lb_logits, hgrn_out_gain, w_out, norm_mlp, w_up, w_down, norm_final, loss_target, m_norm_mix, m_w_in, m_attn_out_gain, m_hgrn_lb_logits, m_hgrn_out_gain, m_w_out, m_norm_mlp, m_w_up, m_w_down, m_norm_final, v_norm_mix, v_w_in, v_attn_out_gain, v_hgrn_lb_logits, v_hgrn_out_gain, v_w_out, v_norm_mlp, v_w_up, v_w_down, v_norm_final):
    lower, lower_vjp = jax.vjp(_lower_bounds, hgrn_lb_logits)
    small = {"norm_mix": norm_mix, "attn_out_gain": attn_out_gain, "lower": lower,
             "hgrn_out_gain": hgrn_out_gain, "norm_mlp": norm_mlp, "norm_final": norm_final}
    big_w = (w_in, w_out, w_up, w_down)

    x_pos, y_pos, core = lax.axis_index("x"), lax.axis_index("y"), lax.axis_index("c")
    me = (2 * x_pos + y_pos).astype(jnp.int32).reshape(1)
    place = jnp.stack([4 * x_pos + 2 * y_pos + core, 2 * x_pos + y_pos, core]).astype(jnp.int32)
    shards = [[w[l].astype(BF16) for w in big_w] for l in range(DEPTH)]
    in_flight = {}

    def start_gather(name, some, after):
        lands = [_own_slot(f"own_{name}_{i}", s, me) for i, s in enumerate(some)]
        sems, passed, token = _exchange_start(f"start_{name}", some, lands, "gather", after)
        in_flight[name] = (sems, passed)
        return token

    def finish_gather(name, after):
        return _exchange_wait(f"wait_{name}", *in_flight.pop(name), "gather", after)[1]

    def get_weights(l, stream):
        if l == 0:
            (win,) = _gather_weights("gather_w_in0", shards[0][:1])
            token = start_gather("gather_rest0", shards[0][1:], win)
            token = start_gather("gather_w_in1", shards[1][:1], token)
            token = start_gather("gather_rest1", shards[1][1:], token)
            return (win, lambda after: finish_gather("gather_rest0", after)), token
        (win,) = finish_gather("gather_w_in1", stream)
        return (win, lambda after: finish_gather("gather_rest1", after)), None

    reduced = {}

    def start_exchange(name, grads):
        srcs, lands = [g for g, _ in grads], [land for _, land in grads]
        sems, passed, token = _exchange_start(f"start_{name}", srcs, lands, "scatter")
        in_flight[name] = (sems, passed)
        return token

    def finish_exchange(name, after):
        own, landed = _exchange_wait(f"wait_{name}", *in_flight.pop(name), "scatter", after)
        return [_sum_devices(f"sum_{name}_{i}", p, g, place) for i, (p, g) in enumerate(zip(landed, own))]

    def on_grads(l, group, grads):
        if (l, group) == (1, "mlp"):
            return start_exchange("mlp1", grads)
        if (l, group) == (1, "mix"):
            return start_exchange("mix1", grads)
        if (l, group) == (0, "mlp"):
            token = start_exchange("mlp0", grads)
            reduced[(1, "mlp")] = finish_exchange("mlp1", token)
            reduced[(1, "mix")] = finish_exchange("mix1", token)
            return token
        token = start_exchange("mix0", grads)
        reduced[(0, "mlp")] = finish_exchange("mlp0", token)
        return token

    loss, dx, dg_final, sg = _local_step(x[0], loss_target[0], small, get_weights, on_grads)

    big_m = (m_w_in, m_w_out, m_w_up, m_w_down)
    big_v = (v_w_in, v_w_out, v_w_up, v_w_down)
    names = ("w_in", "w_out", "w_up", "w_down")
    big_g, big_delta, big_new_m, big_new_v = [None] * 4, [None] * 4, [None] * 4, [None] * 4

    def finish_weights(group, which):
        whole = _share_halves(f"share_{group}", [[reduced[(l, group)][i] for l in range(DEPTH)] for i in range(2)])
        for i, w in enumerate(which):
            shape = big_w[w].shape
            flat = lambda arr: arr.reshape(shape[0] * shape[1], shape[2])
            d, m2, v2 = _adamw(f"adamw_{names[w]}", flat(big_w[w]), flat(whole[i]), flat(big_m[w]), flat(big_v[w]))
            big_g[w], big_delta[w] = whole[i], d.reshape(shape)
            big_new_m[w], big_new_v[w] = m2.reshape(shape), v2.reshape(shape)

    finish_weights("mlp", (2, 3))
    reduced[(0, "mix")] = finish_exchange("mix0", big_delta[3])
    finish_weights("mix", (0, 1))

    stack2 = lambda key: jnp.stack([sg[l][key] for l in range(DEPTH)])
    pack = _pack_small(stack2("norm_mix"), stack2("attn_out_gain"), stack2("lower"), stack2("hgrn_out_gain"),
                       stack2("norm_mlp"), dg_final, jnp.broadcast_to(loss[0, 0], (D_MODEL,)))
    g_mix, g_attn, g_lower, g_hgrn, g_mlp, g_final, loss_row = _unpack_small(_all_reduce_small(pack, big_delta[0]))
    (g_logits,) = lower_vjp(g_lower)

    zeros_row = jnp.zeros((D_MODEL,), F32)
    small_w = (norm_mix, attn_out_gain, hgrn_lb_logits, hgrn_out_gain, norm_mlp, norm_final)
    small_m = (m_norm_mix, m_attn_out_gain, m_hgrn_lb_logits, m_hgrn_out_gain, m_norm_mlp, m_norm_final)
    small_v = (v_norm_mix, v_attn_out_gain, v_hgrn_lb_logits, v_hgrn_out_gain, v_norm_mlp, v_norm_final)
    small_g = (g_mix, g_attn, g_logits, g_hgrn, g_mlp, g_final)
    packs = [_pack_small(*t, zeros_row) for t in (small_w, small_g, small_m, small_v)]
    small_delta, small_new_m, small_new_v = [_unpack_small(p)[:6] for p in _adamw("adamw_small", *packs)]

    def ordered(small6, big4):
        mix, attn, lbl, hg, mlp, fin = small6
        return (mix, big4[0], attn, lbl, hg, big4[1], mlp, big4[2], big4[3], fin)

    return ((loss_row[0], dx[None]) + ordered(small_g, big_g) + ordered(small_delta, big_delta)
            + ordered(small_new_m, big_new_m) + ordered(small_new_v, big_new_v))
```

```python
import numpy as np
import jax
import jax.numpy as jnp
from jax import lax
from jax.experimental import pallas as pl
from jax.experimental.pallas import tpu as pltpu

F32 = jnp.float32
BF16 = jnp.bfloat16
MESH = pl.DeviceIdType.MESH

SEQ = 4096
D_MODEL = 1024
DEPTH = 2
ATTN_WIDTH = 512
HEAD_DIM = 64
HGRN_HEADS = 4
HGRN_DIM = 128
HGRN_WIDTH = 512
IN_W = 3584
MLP_HIDDEN = 4096
N_CHIPS = 4
N_DEV = 8
DILATIONS = (1, 4, 16)
SPAN = 128
ROPE_THETA = 10000.0
NORM_EPS = 1e-6
MASK_VALUE = -1e30
CHUNK = 128
ROW_TILE = 512
MM_TILE = 512
VMEM_LIMIT = 52 * 1024 * 1024

ADAM_LR = 0.001
ADAM_B1 = 0.9
ADAM_B2 = 0.999
ADAM_EPS = 1e-08
ADAM_WD = 0.01
ADAM_STEP = 10

PACK_ROWS = 16


def _params(n_axes):
    return pltpu.CompilerParams(dimension_semantics=("arbitrary",) * n_axes,
                                vmem_limit_bytes=VMEM_LIMIT)


def _dot(a, b):
    return jnp.dot(a.astype(BF16), b.astype(BF16), preferred_element_type=F32)


def _dot_nt(a, b):
    return lax.dot_general(a.astype(BF16), b.astype(BF16), (((1,), (1,)), ((), ())),
                           preferred_element_type=F32)


def _dot_tn(a, b):
    return lax.dot_general(a.astype(BF16), b.astype(BF16), (((0,), (0,)), ((), ())),
                           preferred_element_type=F32)


def _sigmoid(x):
    return 1.0 / (1.0 + jnp.exp(-x))


def _spec(shape, index_map):
    return pl.BlockSpec(shape, index_map)


def _mm_pieces(name, a, w, nt, tm, epilogue="none", extra=None, after=None):
    s = a.shape[0]
    pw = w.shape[1] if nt else w.shape[2]
    width = N_CHIPS * pw

    def body(a_ref, w_ref, *rest):
        e_ref = rest[0] if extra is not None else None
        outs = rest[-3:] if epilogue == "relu2" else rest[-1:]
        av = a_ref[...].astype(BF16)
        for j in range(N_CHIPS):
            cols = slice(j * pw, (j + 1) * pw)
            r = _dot_nt(av, w_ref[j]) if nt else _dot(av, w_ref[j])
            if epilogue == "relu2":
                relu = jnp.maximum(r, 0.0)
                r = relu * relu
                outs[1][:, cols] = relu.astype(BF16)
                outs[2][cols, :] = r.T.astype(BF16)
            elif epilogue == "relu2_grad":
                r = r * (2.0 * e_ref[:, cols].astype(F32))
            outs[0][:, cols] = r.astype(outs[0].dtype)

    row = lambda width_: _spec((tm, width_), lambda i: (i, 0))
    in_specs = [row(a.shape[1]), _spec(w.shape, lambda i: (0, 0, 0))]
    args = [a, w]
    if extra is not None:
        in_specs.append(row(width))
        args.append(extra)
    if after is not None:
        in_specs.append(pl.BlockSpec(memory_space=pl.ANY))
        args.append(after)
    if epilogue == "relu2":
        out_specs = [row(width), row(width), _spec((width, tm), lambda i: (0, i))]
        out_shape = [jax.ShapeDtypeStruct((s, width), BF16)] * 2 + [jax.ShapeDtypeStruct((width, s), BF16)]
    else:
        out_specs = row(width)
        out_shape = jax.ShapeDtypeStruct((s, width), BF16 if epilogue == "relu2_grad" else F32)
    return pl.pallas_call(body, name=name, grid=(s // tm,), in_specs=in_specs, out_specs=out_specs,
                          out_shape=out_shape, compiler_params=_params(1))(*args)


def _mm_accum(name, a, w, nt, tm, resid, norm=None, after=None, next_gain=None, loss=None):
    pieces = list(a) if isinstance(a, (list, tuple)) else [a]
    n_a = len(pieces)
    s = pieces[0].shape[0]
    pk = w.shape[2] if nt else w.shape[1]
    d = w.shape[1] if nt else w.shape[2]

    def body(*refs):
        a_refs, w_ref, resid_ref, rest = refs[:n_a], refs[n_a], refs[n_a + 1], refs[n_a + 2:]
        av = a_refs[0][...] if n_a == 1 else jnp.concatenate([ref[...] for ref in a_refs], axis=1)
        r = None
        for j in range(N_CHIPS):
            piece = av[:, j * pk:(j + 1) * pk].astype(BF16)
            term = _dot_nt(piece, w_ref[j]) if nt else _dot(piece, w_ref[j])
            r = term if r is None else r + term
        if loss is not None:
            g_ref, t_ref = rest[:2]
            dx_ref, dxb_ref, dg_ref, loss_ref, acc = rest[-5:]
            i = pl.program_id(0)

            @pl.when(i == 0)
            def _():
                dg_ref[...] = jnp.zeros_like(dg_ref)
                acc[...] = jnp.zeros_like(acc)

            xv = r + resid_ref[...]
            g = g_ref[...]
            rs = lax.rsqrt(jnp.mean(xv * xv, axis=1, keepdims=True) + NORM_EPS)
            xhat = xv * rs
            err = xhat * g - t_ref[...]
            acc[...] += jnp.sum(err * err, axis=0, keepdims=True)
            dy = err * (1.0 / d)
            dyg = dy * g
            dx = rs * (dyg - xhat * jnp.mean(dyg * xhat, axis=1, keepdims=True))
            dx_ref[...] = dx
            dxb_ref[...] = dx.astype(BF16)
            dg_ref[...] += jnp.sum(dy * xhat, axis=0, keepdims=True)

            @pl.when(i == s // tm - 1)
            def _():
                total = jnp.sum(acc[...], axis=1, keepdims=True) * (0.5 / d)
                loss_ref[...] = jnp.broadcast_to(total, loss_ref.shape)

            return
        if norm is None and next_gain is None:
            rest[-1][...] = r + resid_ref[...]
            return
        if norm is None:
            g_ref = rest[0]
            x_out, h_out, ht_out = rest[-3:]
            xv = r + resid_ref[...]
            x_out[...] = xv
            h = xv * lax.rsqrt(jnp.mean(xv * xv, axis=1, keepdims=True) + NORM_EPS) * g_ref[...]
            h_out[...] = h.astype(BF16)
            ht_out[...] = h.T.astype(BF16)
            return
        x_ref, g_ref = rest[:2]
        dx_ref, dxb_ref, dg_ref = rest[-3:]

        @pl.when(pl.program_id(0) == 0)
        def _():
            dg_ref[...] = jnp.zeros_like(dg_ref)

        xv = x_ref[...]
        rs = lax.rsqrt(jnp.mean(xv * xv, axis=1, keepdims=True) + NORM_EPS)
        xhat = xv * rs
        rg = r * g_ref[...]
        dx = resid_ref[...] + rs * (rg - xhat * jnp.mean(rg * xhat, axis=1, keepdims=True))
        dx_ref[...] = dx
        dxb_ref[...] = dx.astype(BF16)
        dg_ref[...] += jnp.sum(r * xhat, axis=0, keepdims=True)

    row = lambda width: _spec((tm, width), lambda i: (i, 0))
    in_specs = [row(p.shape[1]) for p in pieces] + [_spec(w.shape, lambda i: (0, 0, 0)), row(d)]
    args = pieces + [w, resid]
    scratch = []
    if loss is not None:
        in_specs += [_spec((1, d), lambda i: (0, 0)), row(d)]
        args += list(loss)
        out_specs = [row(d), row(d), _spec((1, d), lambda i: (0, 0)), _spec((1, 128), lambda i: (0, 0))]
        out_shape = [jax.ShapeDtypeStruct((s, d), F32), jax.ShapeDtypeStruct((s, d), BF16),
                     jax.ShapeDtypeStruct((1, d), F32), jax.ShapeDtypeStruct((1, 128), F32)]
        scratch = [pltpu.VMEM((1, d), F32)]
    elif norm is None and next_gain is None:
        out_specs, out_shape = row(d), jax.ShapeDtypeStruct((s, d), F32)
    elif norm is None:
        in_specs.append(_spec((1, d), lambda i: (0, 0)))
        args.append(next_gain)
        out_specs = [row(d), row(d), _spec((d, tm), lambda i: (0, i))]
        out_shape = [jax.ShapeDtypeStruct((s, d), F32), jax.ShapeDtypeStruct((s, d), BF16),
                     jax.ShapeDtypeStruct((d, s), BF16)]
    else:
        in_specs += [row(d), _spec((1, d), lambda i: (0, 0))]
        args += list(norm)
        out_specs = [row(d), row(d), _spec((1, d), lambda i: (0, 0))]
        out_shape = [jax.ShapeDtypeStruct((s, d), F32), jax.ShapeDtypeStruct((s, d), BF16),
                     jax.ShapeDtypeStruct((1, d), F32)]
    if after is not None:
        in_specs.append(pl.BlockSpec(memory_space=pl.ANY))
        args.append(after)
    return pl.pallas_call(body, name=name, grid=(s // tm,), in_specs=in_specs, out_specs=out_specs,
                          out_shape=out_shape, scratch_shapes=scratch, compiler_params=_params(1))(*args)


def _mm_dw(name, a_t, b, by_cols, tk):
    pieces = list(b) if isinstance(b, (list, tuple)) else [b]
    n_b = len(pieces)
    m, s = a_t.shape
    n = sum(p.shape[1] for p in pieces)
    shape = (N_CHIPS, m, n // N_CHIPS) if by_cols else (N_CHIPS, m // N_CHIPS, n)
    n_steps = s // tk

    def body(a_ref, *rest):
        b_refs, o_ref, acc = rest[:n_b], rest[n_b], rest[-1]

        @pl.when(pl.program_id(0) == 0)
        def _():
            acc[...] = jnp.zeros_like(acc)

        bv = b_refs[0][...] if n_b == 1 else jnp.concatenate([ref[...] for ref in b_refs], axis=1)
        for j in range(N_CHIPS):
            if by_cols:
                acc[j] += _dot(a_ref[...], bv[:, j * shape[2]:(j + 1) * shape[2]])
            else:
                acc[j] += _dot(a_ref[j * shape[1]:(j + 1) * shape[1], :], bv)

        @pl.when(pl.program_id(0) == n_steps - 1)
        def _():
            o_ref[...] = acc[...].astype(BF16)

    return pl.pallas_call(
        body, name=name, grid=(n_steps,),
        in_specs=[_spec((m, tk), lambda k: (0, k))] + [_spec((tk, p.shape[1]), lambda k: (k, 0)) for p in pieces],
        out_specs=[_spec(shape, lambda k: (0, 0, 0)), ANY_SPEC],
        out_shape=[jax.ShapeDtypeStruct(shape, BF16),
                   jax.ShapeDtypeStruct((N_DEV, shape[1] // 2, shape[2]), BF16)],
        scratch_shapes=[pltpu.VMEM(shape, F32)],
        compiler_params=_params(1))(a_t, *pieces)


def _rms_fwd(name, x, gain):
    s, d = x.shape
    t = ROW_TILE

    def body(x_ref, g_ref, h_ref, ht_ref):
        xv = x_ref[...]
        r = lax.rsqrt(jnp.mean(xv * xv, axis=1, keepdims=True) + NORM_EPS)
        h = xv * r * g_ref[...]
        h_ref[...] = h.astype(BF16)
        ht_ref[...] = h.T.astype(BF16)

    return pl.pallas_call(
        body, name=name, grid=(s // t,),
        in_specs=[_spec((t, d), lambda i: (i, 0)), _spec((1, d), lambda i: (0, 0))],
        out_specs=[_spec((t, d), lambda i: (i, 0)), _spec((d, t), lambda i: (0, i))],
        out_shape=[jax.ShapeDtypeStruct((s, d), BF16), jax.ShapeDtypeStruct((d, s), BF16)],
        compiler_params=_params(1),
    )(x, gain)


def _rope_tables():
    half = HEAD_DIM // 2
    inv_freq = ROPE_THETA ** (-jnp.arange(half, dtype=F32) / half)
    ang = jnp.arange(SEQ, dtype=jnp.int32).astype(F32)[:, None] * inv_freq[None, :]
    cos, sin = jnp.cos(ang), jnp.sin(ang)
    cos_t = jnp.concatenate([cos, cos, cos, cos], axis=1)
    sin_t = jnp.concatenate([-sin, sin, -sin, sin], axis=1)
    return cos_t, sin_t


def _swap_halves(x):
    lane = lax.broadcasted_iota(jnp.int32, x.shape, 1)
    first = (lane % HEAD_DIM) < (HEAD_DIM // 2)
    return jnp.where(first, pltpu.roll(x, 128 - HEAD_DIM // 2, 1), pltpu.roll(x, HEAD_DIM // 2, 1))


def _permuted_specs(t, width):
    specs = [_spec((t, width), lambda i: (i, 0))]
    for d in DILATIONS[1:]:
        specs.append(_spec((d, t // d, width), lambda i: (0, i, 0)))
    return specs


def _permuted_shapes(width, dtype):
    shapes = [jax.ShapeDtypeStruct((SEQ, width), dtype)]
    for d in DILATIONS[1:]:
        shapes.append(jax.ShapeDtypeStruct((d, SEQ // d, width), dtype))
    return shapes


def _attn_prep(name, proj, cos_t, sin_t):
    t = ROW_TILE
    w = ATTN_WIDTH

    def body(q_ref, k_ref, v_ref, cos_ref, sin_ref, *rest):
        outs, scr = rest[:9], rest[9]
        cosv, sinv = cos_ref[...], sin_ref[...]
        for a, (src, roped, scale) in enumerate(((q_ref, True, HEAD_DIM ** -0.5),
                                                 (k_ref, True, 1.0), (v_ref, False, 1.0))):
            o1, o4, o16 = outs[3 * a:3 * a + 3]
            for cb in range(w // 128):
                cols = slice(cb * 128, (cb + 1) * 128)
                val = src[:, cols]
                if roped:
                    val = (val * cosv + _swap_halves(val) * sinv) * scale
                scr[...] = val
                o1[:, cols] = val.astype(BF16)
                for o_ref, d in ((o4, 4), (o16, 16)):
                    for r in range(d):
                        o_ref[r, :, cols] = scr[pl.ds(r, t // d, stride=d), :].astype(BF16)

    out_specs = _permuted_specs(t, w) * 3
    out_shape = _permuted_shapes(w, BF16) * 3
    outs = pl.pallas_call(
        body, name=name, grid=(SEQ // t,),
        in_specs=[_spec((t, w), lambda i: (i, 0)), _spec((t, w), lambda i: (i, 1)),
                  _spec((t, w), lambda i: (i, 2)),
                  _spec((t, 128), lambda i: (i, 0)), _spec((t, 128), lambda i: (i, 0))],
        out_specs=out_specs, out_shape=out_shape,
        scratch_shapes=[pltpu.VMEM((t, 128), F32)],
        compiler_params=_params(1),
    )(proj, proj, proj, cos_t, sin_t)
    q, k, v = outs[0:3], outs[3:6], outs[6:9]
    flat = lambda arr: arr.reshape(SEQ, w)
    return [(flat(q[p]), flat(k[p]), flat(v[p])) for p in range(3)]


def _band_masks():
    row = lax.broadcasted_iota(jnp.int32, (2 * SPAN, 2 * SPAN), 0) % SPAN
    col = lax.broadcasted_iota(jnp.int32, (2 * SPAN, 2 * SPAN), 1)
    is_prev = col < SPAN
    band = (is_prev & (col >= row)) | (~is_prev & (col - SPAN <= row))
    head0 = lax.broadcasted_iota(jnp.int32, (SPAN, 128), 1) < HEAD_DIM
    return band, is_prev, head0


def _stack_heads(x, head0):
    zero = jnp.zeros_like(x)
    return jnp.concatenate([jnp.where(head0, x, zero), jnp.where(head0, zero, x)], axis=0)


ATTN_UNROLL = 4


def _for_each_block(block, seg_blocks):
    def trip(i, carry):
        for u in range(ATTN_UNROLL):
            static = seg_blocks <= ATTN_UNROLL
            block(i * ATTN_UNROLL + u, (u % seg_blocks == 0) if static else None)
        return carry

    lax.fori_loop(0, SEQ // SPAN // ATTN_UNROLL, trip, 0)


def _attn_fwd(name, q, k, v, seg_blocks):
    def body(q_ref, k_ref, v_ref, o_ref, lse_ref):
        band, is_prev, head0 = _band_masks()

        def block(b, first):
            cur = pl.ds(pl.multiple_of(b * SPAN, SPAN), SPAN)
            qs = _stack_heads(q_ref[cur, :], head0)
            if first is True:
                kcat, vcat, ok = k_ref[cur, :], v_ref[cur, :], band[:, SPAN:]
            else:
                prev = pl.ds(pl.multiple_of(jnp.maximum(b - 1, 0) * SPAN, SPAN), SPAN)
                kcat = jnp.concatenate([k_ref[prev, :], k_ref[cur, :]], axis=0)
                vcat = jnp.concatenate([v_ref[prev, :], v_ref[cur, :]], axis=0)
                ok = band if first is False else band & (((b % seg_blocks) != 0) | ~is_prev)
            s = jnp.where(ok, _dot_nt(qs, kcat), MASK_VALUE)
            m = jnp.max(s, axis=1, keepdims=True)
            p = jnp.exp(s - m)
            l = jnp.sum(p, axis=1, keepdims=True)
            pv = _dot(p, vcat) * (1.0 / l)
            lse = m + jnp.log(l)
            o_ref[cur, :] = jnp.where(head0, pv[:SPAN], pv[SPAN:])
            lse_ref[cur, :] = jnp.where(head0, lse[:SPAN], lse[SPAN:])

        _for_each_block(block, seg_blocks)

    col = _spec((SEQ, 128), lambda j: (0, j))
    return pl.pallas_call(
        body, name=name, grid=(ATTN_WIDTH // 128,),
        in_specs=[col, col, col], out_specs=[col, col],
        out_shape=[jax.ShapeDtypeStruct((SEQ, ATTN_WIDTH), F32)] * 2,
        compiler_params=_params(1),
    )(q, k, v)


def _unpermute(dst, src_ref, d, cols):
    n = dst.shape[0] // d
    for r in range(d):
        dst[pl.ds(r, n, stride=d), :] = src_ref[r, :, cols].astype(dst.dtype)


def _attn_merge(name, outs, lses, gain):
    t = ROW_TILE
    w = ATTN_WIDTH

    def body(o1, o4, o16, l1, l4, l16, g_ref, an_ref, ant_ref, attn_ref, lse_ref, so4, so16, sl4, sl16):
        for cb in range(w // 128):
            cols = slice(cb * 128, (cb + 1) * 128)
            _unpermute(so4, o4, 4, cols)
            _unpermute(so16, o16, 16, cols)
            _unpermute(sl4, l4, 4, cols)
            _unpermute(sl16, l16, 16, cols)
            la, lb, lc = l1[:, cols], sl4[...], sl16[...]
            m = jnp.maximum(jnp.maximum(la, lb), lc)
            ea, eb, ec = jnp.exp(la - m), jnp.exp(lb - m), jnp.exp(lc - m)
            tot = ea + eb + ec
            attn_ref[:, cols] = (ea * o1[:, cols] + eb * so4[...] + ec * so16[...]) / tot
            lse_ref[:, cols] = m + jnp.log(tot)
        attn = attn_ref[...]
        r = lax.rsqrt(jnp.mean(attn * attn, axis=1, keepdims=True) + NORM_EPS)
        an = attn * r * g_ref[...]
        an_ref[...] = an.astype(BF16)
        ant_ref[...] = an.T.astype(BF16)

    views = lambda arrs: [arrs[0], arrs[1].reshape(4, SEQ // 4, w), arrs[2].reshape(16, SEQ // 16, w)]
    row = _spec((t, w), lambda i: (i, 0))
    return pl.pallas_call(
        body, name=name, grid=(SEQ // t,),
        in_specs=_permuted_specs(t, w) * 2 + [_spec((1, w), lambda i: (0, 0))],
        out_specs=[row, _spec((w, t), lambda i: (0, i)), row, row],
        out_shape=[jax.ShapeDtypeStruct((SEQ, 2 * w), BF16), jax.ShapeDtypeStruct((2 * w, SEQ), BF16),
                   jax.ShapeDtypeStruct((SEQ, w), F32), jax.ShapeDtypeStruct((SEQ, w), F32)],
        scratch_shapes=[pltpu.VMEM((t, 128), F32)] * 4,
        compiler_params=_params(1),
    )(*views(outs), *views(lses), gain)


def _head_sum_matrix():
    i = np.arange(ATTN_WIDTH)
    return jnp.asarray((i[:, None] // HEAD_DIM) == (i[None, :] // HEAD_DIM), dtype=F32)


def _attn_bwd_prep(name, d_an, attn, lse, gain, head_sum):
    t = ROW_TILE
    w = ATTN_WIDTH

    def body(dan_ref, attn_ref, lse_ref, g_ref, hs_ref, *rest):
        (do1, do4, do16, dl1, dl4, dl16, ls4, ls16, dg_ref), (sdo, sdl, sls) = rest[:9], rest[9:]

        @pl.when(pl.program_id(0) == 0)
        def _():
            dg_ref[...] = jnp.zeros_like(dg_ref)

        attn = attn_ref[...]
        dan = dan_ref[...]
        r = lax.rsqrt(jnp.mean(attn * attn, axis=1, keepdims=True) + NORM_EPS)
        xhat = attn * r
        dg_ref[...] += jnp.sum(dan * xhat, axis=0, keepdims=True)
        dang = dan * g_ref[...]
        d_o = r * (dang - xhat * jnp.mean(dang * xhat, axis=1, keepdims=True))
        delta = jnp.dot(d_o * attn, hs_ref[...], preferred_element_type=F32,
                        precision=lax.Precision.HIGHEST)
        do1[...] = d_o.astype(BF16)
        dl1[...] = delta
        for cb in range(w // 128):
            cols = slice(cb * 128, (cb + 1) * 128)
            sdo[...] = d_o[:, cols]
            sdl[...] = delta[:, cols]
            sls[...] = lse_ref[:, cols]
            for d, o_do, o_dl, o_ls in ((4, do4, dl4, ls4), (16, do16, dl16, ls16)):
                for rr in range(d):
                    rows = pl.ds(rr, t // d, stride=d)
                    o_do[rr, :, cols] = sdo[rows, :].astype(BF16)
                    o_dl[rr, :, cols] = sdl[rows, :]
                    o_ls[rr, :, cols] = sls[rows, :]

    row = _spec((t, w), lambda i: (i, 0))
    perm = _permuted_specs(t, w)
    outs = pl.pallas_call(
        body, name=name, grid=(SEQ // t,),
        in_specs=[row, row, row, _spec((1, w), lambda i: (0, 0)), _spec((w, w), lambda i: (0, 0))],
        out_specs=perm + perm + perm[1:] + [_spec((1, w), lambda i: (0, 0))],
        out_shape=(_permuted_shapes(w, BF16) + _permuted_shapes(w, F32) + _permuted_shapes(w, F32)[1:]
                   + [jax.ShapeDtypeStruct((1, w), F32)]),
        scratch_shapes=[pltpu.VMEM((t, 128), F32)] * 3,
        compiler_params=_params(1),
    )(d_an, attn, lse, gain, head_sum)
    flat = lambda arr: arr.reshape(SEQ, w)
    d_out = [flat(a) for a in outs[0:3]]
    delta = [flat(a) for a in outs[3:6]]
    lses = [lse, flat(outs[6]), flat(outs[7])]
    return d_out, delta, lses, outs[8]


def _attn_bwd(name, q, k, v, d_out, delta, lse, seg_blocks):
    def body(q_ref, k_ref, v_ref, do_ref, dl_ref, lse_ref, dq_ref, dk_out, dv_out, dk_ref, dv_ref):
        band, is_prev, head0 = _band_masks()
        dk_ref[...] = jnp.zeros_like(dk_ref)
        dv_ref[...] = jnp.zeros_like(dv_ref)

        def per_head(x):
            return jnp.concatenate([x[:, 0:1], x[:, HEAD_DIM:HEAD_DIM + 1]], axis=0)

        def block(b, first):
            cur = pl.ds(pl.multiple_of(b * SPAN, SPAN), SPAN)
            qs = _stack_heads(q_ref[cur, :], head0)
            dos = _stack_heads(do_ref[cur, :], head0)
            if first is True:
                kcat, vcat, ok = k_ref[cur, :], v_ref[cur, :], band[:, SPAN:]
            else:
                prev = pl.ds(pl.multiple_of(jnp.maximum(b - 1, 0) * SPAN, SPAN), SPAN)
                kcat = jnp.concatenate([k_ref[prev, :], k_ref[cur, :]], axis=0)
                vcat = jnp.concatenate([v_ref[prev, :], v_ref[cur, :]], axis=0)
                ok = band if first is False else band & (((b % seg_blocks) != 0) | ~is_prev)
            p = jnp.where(ok, jnp.exp(_dot_nt(qs, kcat) - per_head(lse_ref[cur, :])), 0.0)
            ds = p * (_dot_nt(dos, vcat) - per_head(dl_ref[cur, :]))
            dq = _dot(ds, kcat)
            dq_ref[cur, :] = jnp.where(head0, dq[:SPAN], dq[SPAN:]).astype(BF16)
            dk = _dot_tn(ds, qs)
            dv = _dot_tn(p, dos)
            if first is not True:
                dk_ref[prev, :] += dk[:SPAN]
                dv_ref[prev, :] += dv[:SPAN]
            dk_ref[cur, :] += dk[-SPAN:]
            dv_ref[cur, :] += dv[-SPAN:]

        _for_each_block(block, seg_blocks)
        dk_out[...] = dk_ref[...].astype(BF16)
        dv_out[...] = dv_ref[...].astype(BF16)

    col = _spec((SEQ, 128), lambda j: (0, j))
    return pl.pallas_call(
        body, name=name, grid=(ATTN_WIDTH // 128,),
        in_specs=[col] * 6, out_specs=[col] * 3,
        out_shape=[jax.ShapeDtypeStruct((SEQ, ATTN_WIDTH), BF16)] * 3,
        scratch_shapes=[pltpu.VMEM((SEQ, 128), F32)] * 2,
        compiler_params=_params(1),
    )(q, k, v, d_out, delta, lse)


def _attn_bwd_post(name, grads, cos_t, sin_t):
    t = ROW_TILE
    w = ATTN_WIDTH

    def body(*refs):
        ins, cos_ref, sin_ref, out_ref, s4, s16 = refs[:9], refs[9], refs[10], refs[11], refs[12], refs[13]
        cosv, sinv = cos_ref[...], sin_ref[...]
        for a in range(3):
            g1, g4, g16 = ins[a], ins[3 + a], ins[6 + a]
            for cb in range(w // 128):
                cols = slice(cb * 128, (cb + 1) * 128)
                _unpermute(s4, g4, 4, cols)
                _unpermute(s16, g16, 16, cols)
                val = g1[:, cols].astype(F32) + s4[...] + s16[...]
                if a < 2:
                    val = val * cosv + _swap_halves(val * sinv)
                if a == 0:
                    val = val * (HEAD_DIM ** -0.5)
                out_ref[:, a * w + cb * 128:a * w + (cb + 1) * 128] = val.astype(BF16)

    views = []
    for p, d in enumerate(DILATIONS):
        for a in range(3):
            views.append(grads[p][a] if d == 1 else grads[p][a].reshape(d, SEQ // d, w))
    perm = _permuted_specs(t, w)
    in_specs = [perm[0]] * 3 + [perm[1]] * 3 + [perm[2]] * 3
    return pl.pallas_call(
        body, name=name, grid=(SEQ // t,),
        in_specs=in_specs + [_spec((t, 128), lambda i: (i, 0))] * 2,
        out_specs=_spec((t, 3 * w), lambda i: (i, 0)),
        out_shape=jax.ShapeDtypeStruct((SEQ, 3 * w), BF16),
        scratch_shapes=[pltpu.VMEM((t, 128), F32)] * 2,
        compiler_params=_params(1),
    )(*views, cos_t, sin_t)


N_LEVELS = 7
HGRN_PAIR = 2


def _hgrn_consts():
    c = CHUNK
    i = np.arange(c)[:, None]
    s = np.arange(c)[None, :]
    blocks = [s <= i]
    for lv in range(N_LEVELS):
        bs = c >> lv
        h = bs // 2
        m = (i // bs) * bs + h - 1
        second = (i % bs) >= h
        blocks.append((second & (s > m) & (s <= i)) | (~second & (s > i) & (s <= m)))
    blocks.append(s > i)
    stack = np.concatenate(blocks, axis=0).astype(np.float32)
    return jnp.asarray(stack, dtype=BF16), jnp.asarray(stack.T, dtype=BF16)


def _exact_dot(m01, x):
    hi = x.astype(BF16)
    lo = (x - hi.astype(F32)).astype(BF16)
    n = x.shape[1]
    full = jnp.dot(m01, jnp.concatenate([hi, lo], axis=1), preferred_element_type=F32)
    return full[:, :n] + full[:, n:]


def _hgrn_gates(qh, z, lb):
    sq = _sigmoid(qh)
    q = qh * sq * (HGRN_DIM ** -0.5)
    sig = _sigmoid(z)
    sigm = _sigmoid(-z)
    f = lb + (1.0 - lb) * sig
    k = (1.0 - lb) * sigm
    return q, k, f, sq, sig, sigm


def _level_masks(lv):
    row = lax.broadcasted_iota(jnp.int32, (CHUNK, CHUNK), 0)
    col = lax.broadcasted_iota(jnp.int32, (CHUNK, CHUNK), 1)
    shift = N_LEVELS - lv
    second = (row & (CHUNK >> (lv + 1))) != 0
    same = (row >> shift) == (col >> shift)
    return second, same


def _hgrn_fwd(name, proj, lb, gain, stack, mixed, mixed_t):
    t = ROW_TILE
    per = t // CHUNK
    n_rb = SEQ // t
    n_chunks = SEQ // CHUNK
    col0 = 3 * ATTN_WIDTH // 128
    pair_w = HGRN_PAIR * HGRN_DIM

    def body(q_ref, f_ref, i_ref, g_ref, lb_ref, gain_ref, stack_ref, mixed_in, mixed_t_in,
             rec_ref, rect_ref, o_ref, st_out, a_out, st):
        del mixed_in, mixed_t_in

        @pl.when(pl.program_id(1) == 0)
        def _():
            st[...] = jnp.zeros_like(st)

        row = lax.broadcasted_iota(jnp.int32, (CHUNK, CHUNK), 0)
        col = lax.broadcasted_iota(jnp.int32, (CHUNK, CHUNK), 1)
        for c, hh in [(c, hh) for c in range(per) for hh in range(HGRN_PAIR)]:
            rows = slice(c * CHUNK, (c + 1) * CHUNK)
            lanes = slice(hh * HGRN_DIM, (hh + 1) * HGRN_DIM)
            lbv = lb_ref[hh]
            qh, z, v, gh = q_ref[rows, lanes], f_ref[rows, lanes], i_ref[rows, lanes], g_ref[rows, lanes]
            q, k, f, _, _, _ = _hgrn_gates(qh, z, lbv)
            dec = _exact_dot(stack_ref[...], jnp.log(f))
            g = dec[0:CHUNK]
            to_end = dec[(N_LEVELS + 1) * CHUNK:(N_LEVELS + 2) * CHUNK]
            a = jnp.where(row == col, jnp.sum(q * k, axis=1, keepdims=True), 0.0)
            for lv in range(N_LEVELS):
                e = jnp.exp(dec[(lv + 1) * CHUNK:(lv + 2) * CHUNK])
                second, same = _level_masks(lv)
                qt = jnp.where(second, q * e, 0.0)
                kt = jnp.where(second, 0.0, k * e)
                a = a + jnp.where(same, _dot_nt(qt, kt), 0.0)
            st_prev = st[hh]
            st_out[hh, c] = st_prev
            a_out[hh, c] = a
            o = _dot(a, v) + _dot_nt(q * jnp.exp(g), st_prev)
            k_end = k * jnp.exp(to_end)
            st[hh] = st_prev * jnp.exp(g[CHUNK - 1:CHUNK, :]) + _dot(v.T, k_end)
            o_ref[rows, lanes] = o
            r = lax.rsqrt(jnp.mean(o * o, axis=1, keepdims=True) + NORM_EPS)
            rec = o * r * gain_ref[...] * (gh * _sigmoid(gh))
            rec_ref[rows, lanes] = rec.astype(BF16)
            rect_ref[lanes, rows] = rec.T.astype(BF16)

    def col_spec(tt):
        return _spec((t, pair_w), lambda h, rb: (rb, (col0 + HGRN_HEADS * tt) // HGRN_PAIR + h))

    chunk_spec = _spec((HGRN_PAIR, per, CHUNK, CHUNK), lambda h, rb: (h, rb, 0, 0))
    return pl.pallas_call(
        body, name=name, grid=(HGRN_HEADS // HGRN_PAIR, n_rb),
        in_specs=[col_spec(0), col_spec(1), col_spec(2), col_spec(3),
                  _spec((HGRN_PAIR, 1, HGRN_DIM), lambda h, rb: (h, 0, 0)),
                  _spec((1, HGRN_DIM), lambda h, rb: (0, 0)),
                  _spec(stack.shape, lambda h, rb: (0, 0)), ANY_SPEC, ANY_SPEC],
        out_specs=[_spec((t, pair_w), lambda h, rb: (rb, ATTN_WIDTH // pair_w + h)),
                   _spec((pair_w, t), lambda h, rb: (ATTN_WIDTH // pair_w + h, rb)),
                   _spec((t, pair_w), lambda h, rb: (rb, h)),
                   chunk_spec, chunk_spec],
        out_shape=[jax.ShapeDtypeStruct(mixed.shape, BF16),
                   jax.ShapeDtypeStruct(mixed_t.shape, BF16),
                   jax.ShapeDtypeStruct((SEQ, HGRN_WIDTH), F32),
                   jax.ShapeDtypeStruct((HGRN_HEADS, n_chunks, CHUNK, CHUNK), F32),
                   jax.ShapeDtypeStruct((HGRN_HEADS, n_chunks, CHUNK, CHUNK), F32)],
        scratch_shapes=[pltpu.VMEM((HGRN_PAIR, CHUNK, CHUNK), F32)],
        input_output_aliases={7: 0, 8: 1},
        compiler_params=_params(2),
    )(proj, proj, proj, proj, lb, gain, stack, mixed, mixed_t)


def _hgrn_bwd(name, proj, d_rec, o_pre, states, scores, lb, gain, stack, stack_t):
    t = ROW_TILE
    per = t // CHUNK
    n_rb = SEQ // t
    col0 = 3 * ATTN_WIDTH // 128
    pair_w = HGRN_PAIR * HGRN_DIM

    def body(q_ref, f_ref, i_ref, g_ref, drec_ref, o_ref, st_ref, a_ref, lb_ref, gain_ref,
             stack_ref, stack_t_ref, dq_ref, df_ref, di_ref, dg_ref, dlb_ref, dgain_ref, dst):
        @pl.when(pl.program_id(1) == 0)
        def _():
            dst[...] = jnp.zeros_like(dst)
            dlb_ref[...] = jnp.zeros_like(dlb_ref)
            dgain_ref[...] = jnp.zeros_like(dgain_ref)

        gain_v = gain_ref[...]
        row = lax.broadcasted_iota(jnp.int32, (CHUNK, CHUNK), 0)
        col = lax.broadcasted_iota(jnp.int32, (CHUNK, CHUNK), 1)
        for c, hh in [(c, hh) for c in reversed(range(per)) for hh in range(HGRN_PAIR)]:
            rows = slice(c * CHUNK, (c + 1) * CHUNK)
            lanes = slice(hh * HGRN_DIM, (hh + 1) * HGRN_DIM)
            lbv = lb_ref[hh]
            qh, z, v, gh = q_ref[rows, lanes], f_ref[rows, lanes], i_ref[rows, lanes], g_ref[rows, lanes]
            q, k, f, sq, sig, sigm = _hgrn_gates(qh, z, lbv)
            dec = _exact_dot(stack_ref[...], jnp.log(f))
            g = dec[0:CHUNK]
            to_end = dec[(N_LEVELS + 1) * CHUNK:(N_LEVELS + 2) * CHUNK]
            e_g = jnp.exp(g)
            e_end = jnp.exp(to_end)
            e_last = jnp.exp(g[CHUNK - 1:CHUNK, :])
            q_in = q * e_g
            k_end = k * e_end
            st_prev = st_ref[hh, c]
            a = a_ref[hh, c]
            dst_new = dst[hh]

            o = o_ref[rows, lanes]
            drec = drec_ref[rows, lanes]
            sg = _sigmoid(gh)
            r = lax.rsqrt(jnp.mean(o * o, axis=1, keepdims=True) + NORM_EPS)
            ohat = o * r
            d_gh = drec * (ohat * gain_v) * (sg * (1.0 + gh * (1.0 - sg)))
            d_on = drec * (gh * sg)
            dgain_ref[hh] += jnp.sum(d_on * ohat, axis=0, keepdims=True)
            d_ohat = d_on * gain_v
            d_o = r * (d_ohat - ohat * jnp.mean(d_ohat * ohat, axis=1, keepdims=True))

            d_a = jnp.where(row >= col, _dot_nt(d_o, v), 0.0)
            d_at = jnp.where(col >= row, _dot_nt(v, d_o), 0.0)
            d_v = _dot(a.T, d_o) + _dot_nt(k_end, dst_new)
            d_q_in = _dot(d_o, st_prev)
            d_k_end = _dot(v, dst_new)
            d_q = d_q_in * e_g
            d_k = d_k_end * e_end
            diag = jnp.sum(d_o * v, axis=1, keepdims=True)
            d_q = d_q + diag * k
            d_k = d_k + diag * q
            d_dec = [q_in * d_q_in]
            for lv in range(N_LEVELS):
                e = jnp.exp(dec[(lv + 1) * CHUNK:(lv + 2) * CHUNK])
                second, same = _level_masks(lv)
                qt = jnp.where(second, q * e, 0.0)
                kt = jnp.where(second, 0.0, k * e)
                d_qt = _dot(jnp.where(same, d_a, 0.0), kt)
                d_kt = _dot(jnp.where(same, d_at, 0.0), qt)
                d_q = d_q + jnp.where(second, d_qt * e, 0.0)
                d_k = d_k + jnp.where(second, 0.0, d_kt * e)
                d_dec.append(jnp.where(second, qt * d_qt, kt * d_kt))
            d_dec.append(k_end * d_k_end)
            flux = jnp.sum(dst_new * st_prev, axis=0, keepdims=True) * e_last
            d_lf = _exact_dot(stack_t_ref[...], jnp.concatenate(d_dec, axis=0)) + flux
            dst[hh] = dst_new * e_last + _dot(d_o.T, q_in)

            d_f = d_lf / f - d_k
            dlb_ref[hh] += jnp.sum(d_f * sigm, axis=0, keepdims=True)
            dq_ref[rows, lanes] = (d_q * (HGRN_DIM ** -0.5) * (sq * (1.0 + qh * (1.0 - sq)))).astype(BF16)
            df_ref[rows, lanes] = (d_f * (1.0 - lbv) * sig * sigm).astype(BF16)
            di_ref[rows, lanes] = d_v.astype(BF16)
            dg_ref[rows, lanes] = d_gh.astype(BF16)

    last = n_rb - 1

    def col_spec(tt):
        return _spec((t, pair_w), lambda h, rb: (last - rb, (col0 + HGRN_HEADS * tt) // HGRN_PAIR + h))

    head_col = _spec((t, pair_w), lambda h, rb: (last - rb, h))
    rec_col0 = (d_rec.shape[1] - HGRN_WIDTH) // pair_w
    d_rec_col = _spec((t, pair_w), lambda h, rb: (last - rb, rec_col0 + h))
    chunk_spec = _spec((HGRN_PAIR, per, CHUNK, CHUNK), lambda h, rb: (h, last - rb, 0, 0))
    vec_spec = _spec((HGRN_PAIR, 1, HGRN_DIM), lambda h, rb: (h, 0, 0))
    outs = pl.pallas_call(
        body, name=name, grid=(HGRN_HEADS // HGRN_PAIR, n_rb),
        in_specs=[col_spec(0), col_spec(1), col_spec(2), col_spec(3), d_rec_col, head_col,
                  chunk_spec, chunk_spec, vec_spec,
                  _spec((1, HGRN_DIM), lambda h, rb: (0, 0)),
                  _spec(stack.shape, lambda h, rb: (0, 0)), _spec(stack_t.shape, lambda h, rb: (0, 0))],
        out_specs=[head_col] * 4 + [vec_spec, vec_spec],
        out_shape=[jax.ShapeDtypeStruct((SEQ, HGRN_WIDTH), BF16)] * 4
                  + [jax.ShapeDtypeStruct((HGRN_HEADS, 1, HGRN_DIM), F32)] * 2,
        scratch_shapes=[pltpu.VMEM((HGRN_PAIR, CHUNK, CHUNK), F32)],
        compiler_params=_params(2),
    )(proj, proj, proj, proj, d_rec, o_pre, states, scores, lb, gain, stack, stack_t)
    return outs


ANY_SPEC = pl.BlockSpec(memory_space=pl.ANY)


def _my_place():
    return lax.axis_index("x"), lax.axis_index("y"), lax.axis_index("c")


def _other_chips(x, y):
    return [(1 - x, y), (x, 1 - y), (1 - x, 1 - y)]


def _remote(src, dst, send_sem, recv_sem, device):
    return pltpu.make_async_remote_copy(src_ref=src, dst_ref=dst, send_sem=send_sem, recv_sem=recv_sem,
                                        device_id=device, device_id_type=MESH)


def _staged_copies(srcs, dsts, stage, sems):
    loads = [pltpu.make_async_copy(srcs[i], stage[i], sems.at[i]) for i in range(len(srcs))]
    for cp in loads:
        cp.start()
    stores = []
    for i, cp in enumerate(loads):
        cp.wait()
        stores.append(pltpu.make_async_copy(stage[i], dsts[i], sems.at[i]))
        stores[-1].start()
    return stores


def _gather_weights(name, shards):
    n = len(shards)

    def body(*refs):
        ins, outs = refs[:n], refs[n:2 * n]
        ici_send, ici_recv, d2d_send, d2d_recv, local_sems = refs[2 * n:2 * n + 5]
        stage = refs[2 * n + 5:]
        x, y, c = _my_place()
        me = 2 * x + y
        chips = _other_chips(x, y)

        def half(i, which):
            h = ins[i].shape[0] // 2
            return pl.ds(which * h, h)

        sends = []
        for i in range(n):
            for j, (px, py) in enumerate(chips):
                sends.append(_remote(ins[i].at[half(i, c), :], outs[i].at[me, half(i, c), :],
                                     ici_send.at[3 * i + j], ici_recv.at[3 * i + j], (px, py, c)))
        for cp in sends:
            cp.start()
        local = _staged_copies(ins, [outs[i].at[me] for i in range(n)], stage, local_sems)
        for i in range(n):
            for j, (px, py) in enumerate(chips):
                landed = outs[i].at[2 * px + py, half(i, c), :]
                _remote(landed, landed, ici_send.at[3 * i + j], ici_recv.at[3 * i + j], (px, py, c)).wait_recv()
                forward = _remote(landed, landed, d2d_send.at[3 * i + j], d2d_recv.at[3 * i + j], (x, y, 1 - c))
                forward.start()
                sends.append(forward)
        for i in range(n):
            for j, (px, py) in enumerate(chips):
                other = outs[i].at[2 * px + py, half(i, 1 - c), :]
                _remote(other, other, d2d_send.at[3 * i + j], d2d_recv.at[3 * i + j], (x, y, 1 - c)).wait_recv()
        for cp in sends:
            cp.wait_send()
        for cp in local:
            cp.wait()

    return pl.pallas_call(
        body, name=name, in_specs=[ANY_SPEC] * n, out_specs=[ANY_SPEC] * n,
        out_shape=[jax.ShapeDtypeStruct((N_CHIPS,) + s.shape, s.dtype) for s in shards],
        scratch_shapes=([pltpu.SemaphoreType.DMA((3 * n,))] * 4 + [pltpu.SemaphoreType.DMA((n,))]
                        + [pltpu.VMEM(s.shape, s.dtype) for s in shards]),
        compiler_params=pltpu.CompilerParams(vmem_limit_bytes=VMEM_LIMIT),
    )(*shards)


HBM_SPEC = pl.BlockSpec(memory_space=pltpu.HBM)
SEM_SPEC = pl.BlockSpec(memory_space=pltpu.SEMAPHORE)
SPLIT_PARAMS = pltpu.CompilerParams(has_side_effects=pltpu.SideEffectType.DATAFLOW_SIDE_EFFECTING)
N_PEERS = {"gather": N_CHIPS - 1, "scatter": N_DEV - 1}


def _split_copies(ins, lands, send_sems, recv_sems, kind):
    x, y, c = _my_place()
    pairs = []
    for i in range(len(ins)):
        if kind == "gather":
            me = 2 * x + y
            for j, (px, py) in enumerate(_other_chips(x, y)):
                sems = (send_sems.at[3 * i + j], recv_sems.at[3 * i + j], (px, py, c))
                pairs.append((_remote(ins[i], lands[i].at[me], *sems),
                              _remote(ins[i], lands[i].at[2 * px + py], *sems)))
        else:
            me = 4 * x + 2 * y + c
            h = ins[i].shape[1] // 2
            for k in range(1, N_DEV):
                px, py, pc = (x + (k >> 2)) % 2, (y + ((k >> 1) & 1)) % 2, (c + (k & 1)) % 2
                src = ins[i].at[2 * px + py, pl.ds(pc * h, h), :]
                sems = (send_sems.at[7 * i + k - 1], recv_sems.at[7 * i + k - 1], (px, py, pc))
                pairs.append((_remote(src, lands[i].at[me], *sems),
                              _remote(src, lands[i].at[4 * px + 2 * py + pc], *sems)))
    return pairs


def _exchange_start(name, srcs, lands, kind, after=None):
    n = len(srcs)
    n_sems = N_PEERS[kind] * n
    extra = [] if after is None else [after]

    def body(*refs):
        ins, land_refs = refs[:n], refs[n:2 * n]
        send_sems, recv_sems = refs[2 * n + len(extra):2 * n + len(extra) + 2]
        token = refs[-1]
        for send, _ in _split_copies(ins, land_refs, send_sems, recv_sems, kind):
            send.start()
        token[...] = jnp.zeros_like(token)

    arrays = list(srcs) + list(lands)
    outs = pl.pallas_call(
        body, name=name,
        in_specs=[HBM_SPEC] * (2 * n) + [ANY_SPEC] * len(extra),
        out_shape=([pltpu.SemaphoreType.DMA((n_sems,))] * 2 + [pltpu.HBM(a.shape, a.dtype) for a in arrays]
                   + [jax.ShapeDtypeStruct((8, 128), F32)]),
        out_specs=[SEM_SPEC] * 2 + [HBM_SPEC] * (2 * n) + [pl.BlockSpec(memory_space=pltpu.VMEM)],
        input_output_aliases={i: 2 + i for i in range(2 * n)},
        compiler_params=SPLIT_PARAMS,
    )(*[pltpu.with_memory_space_constraint(a, pltpu.HBM) for a in arrays], *extra)
    return outs[:2], outs[2:2 + 2 * n], outs[-1]


def _exchange_wait(name, sems, passed, kind, after):
    n = len(passed) // 2

    def body(*refs):
        ins, land_refs = refs[:n], refs[n:2 * n]
        send_sems, recv_sems = refs[2 * n:2 * n + 2]
        for send, arrive in _split_copies(ins, land_refs, send_sems, recv_sems, kind):
            send.wait_send()
            arrive.wait_recv()

    outs = pl.pallas_call(
        body, name=name,
        in_specs=[HBM_SPEC] * (2 * n) + [SEM_SPEC] * 2 + [ANY_SPEC],
        out_shape=[pltpu.HBM(a.shape, a.dtype) for a in passed],
        out_specs=[HBM_SPEC] * (2 * n),
        input_output_aliases={i: i for i in range(2 * n)},
        compiler_params=SPLIT_PARAMS,
    )(*passed, *sems, after)
    return outs[:n], outs[n:]


def _own_slot(name, own, me):
    r, cc = own.shape
    th = min(r, 512)

    def body(me_ref, x_ref, o_ref):
        del me_ref
        o_ref[...] = x_ref[...]

    grid_spec = pltpu.PrefetchScalarGridSpec(
        num_scalar_prefetch=1, grid=(r // th,),
        in_specs=[pl.BlockSpec((th, cc), lambda i, me_ref: (i, 0))],
        out_specs=pl.BlockSpec((None, th, cc), lambda i, me_ref: (me_ref[0], i, 0)))
    return pl.pallas_call(
        body, name=name, grid_spec=grid_spec,
        out_shape=jax.ShapeDtypeStruct((N_CHIPS, r, cc), own.dtype), compiler_params=_params(1),
    )(me, own)


def _sum_devices(name, landed, own, place):
    n_dev, h, cc = landed.shape
    th = min(h, 256)
    nb = h // th

    def body(place_ref, l_ref, own_ref, o_ref):
        total = None
        for d in range(n_dev):
            piece = jnp.where(place_ref[0] == d, own_ref[...], l_ref[d]).astype(F32)
            total = piece if total is None else total + piece
        o_ref[...] = total

    grid_spec = pltpu.PrefetchScalarGridSpec(
        num_scalar_prefetch=1, grid=(nb,),
        in_specs=[pl.BlockSpec((n_dev, th, cc), lambda i, p: (0, i, 0)),
                  pl.BlockSpec((None, th, cc), lambda i, p: (p[1], p[2] * nb + i, 0))],
        out_specs=pl.BlockSpec((th, cc), lambda i, p: (i, 0)))
    return pl.pallas_call(
        body, name=name, grid_spec=grid_spec,
        out_shape=jax.ShapeDtypeStruct((h, cc), F32), compiler_params=_params(1),
    )(place, landed, own)


def _share_halves(name, halves):
    flat = [t for per_weight in halves for t in per_weight]
    n = len(flat)
    n_w = len(halves)

    def body(*refs):
        ins, outs = refs[:n], refs[n:n + n_w]
        send_sems, recv_sems, local_sems = refs[n + n_w:n + n_w + 3]
        stage = refs[n + n_w + 3:]
        x, y, c = _my_place()
        sends, own = [], []
        for i in range(n):
            w, l = divmod(i, DEPTH)
            h = ins[i].shape[0]
            own.append(outs[w].at[l, pl.ds(c * h, h), :])
            sends.append(_remote(ins[i], own[i], send_sems.at[i], recv_sems.at[i], (x, y, 1 - c)))
        for cp in sends:
            cp.start()
        local = _staged_copies(ins, own, stage, local_sems)
        for i in range(n):
            w, l = divmod(i, DEPTH)
            h = ins[i].shape[0]
            _remote(ins[i], outs[w].at[l, pl.ds((1 - c) * h, h), :], send_sems.at[i], recv_sems.at[i],
                    (x, y, 1 - c)).wait_recv()
        for cp in sends:
            cp.wait_send()
        for cp in local:
            cp.wait()

    return pl.pallas_call(
        body, name=name, in_specs=[ANY_SPEC] * n, out_specs=[ANY_SPEC] * n_w,
        out_shape=[jax.ShapeDtypeStruct((DEPTH, 2 * per_weight[0].shape[0], per_weight[0].shape[1]), F32)
                   for per_weight in halves],
        scratch_shapes=([pltpu.SemaphoreType.DMA((n,))] * 3 + [pltpu.VMEM(t.shape, t.dtype) for t in flat]),
        compiler_params=pltpu.CompilerParams(vmem_limit_bytes=VMEM_LIMIT),
    )(*flat)


def _all_reduce_small(pack, after):
    def body(p_ref, after_ref, o_ref, recv, send_sems, recv_sems):
        del after_ref
        x, y, c = _my_place()
        me = 4 * x + 2 * y + c
        recv[me] = p_ref[...]
        peers = []
        for k in range(1, N_DEV):
            px, py, pc = (x + (k >> 2)) % 2, (y + ((k >> 1) & 1)) % 2, (c + (k & 1)) % 2
            peers.append((px, py, pc))
        sends = [_remote(p_ref, recv.at[me], send_sems.at[k], recv_sems.at[k], peer)
                 for k, peer in enumerate(peers)]
        for cp in sends:
            cp.start()
        for k, (px, py, pc) in enumerate(peers):
            _remote(p_ref, recv.at[4 * px + 2 * py + pc], send_sems.at[k], recv_sems.at[k],
                    (px, py, pc)).wait_recv()
        for cp in sends:
            cp.wait_send()
        total = recv[0]
        for d in range(1, N_DEV):
            total = total + recv[d]
        o_ref[...] = total

    vmem = pl.BlockSpec(memory_space=pltpu.VMEM)
    return pl.pallas_call(
        body, name="all_reduce_small", in_specs=[vmem, ANY_SPEC], out_specs=vmem,
        out_shape=jax.ShapeDtypeStruct(pack.shape, F32),
        scratch_shapes=[pltpu.VMEM((N_DEV,) + pack.shape, F32),
                        pltpu.SemaphoreType.DMA((N_DEV - 1,)), pltpu.SemaphoreType.DMA((N_DEV - 1,))],
    )(pack, after)


def _adamw(name, w, g, m, v):
    r, cc = w.shape
    th = min(r, 256)

    def body(w_ref, g_ref, m_ref, v_ref, d_ref, m_out, v_out):
        gv = g_ref[...]
        m2 = ADAM_B1 * m_ref[...] + (1.0 - ADAM_B1) * gv
        v2 = ADAM_B2 * v_ref[...] + (1.0 - ADAM_B2) * (gv * gv)
        m_hat = m2 / (1.0 - ADAM_B1 ** ADAM_STEP)
        v_hat = v2 / (1.0 - ADAM_B2 ** ADAM_STEP)
        d_ref[...] = -ADAM_LR * (m_hat / (jnp.sqrt(v_hat) + ADAM_EPS) + ADAM_WD * w_ref[...])
        m_out[...] = m2
        v_out[...] = v2

    tile = _spec((th, cc), lambda i: (i, 0))
    return pl.pallas_call(
        body, name=name, grid=(r // th,), in_specs=[tile] * 4, out_specs=[tile] * 3,
        out_shape=[jax.ShapeDtypeStruct((r, cc), F32)] * 3, compiler_params=_params(1),
    )(w, g, m, v)


def _lower_bounds(lb_logits):
    p = jax.nn.softmax(lb_logits.astype(F32), axis=0)
    return jnp.cumsum(p, axis=0) - p[0]


def _layer_forward(l, stream, small, weights, consts, next_gain=None, loss=None, after=None):
    win, rest = weights
    cos_t, sin_t, stack, _, _ = consts
    tm = MM_TILE
    x_in, h, h_t = stream
    saved = {"x_in": x_in}

    proj = _mm_pieces(f"proj{l}", h, win, False, tm, after=after)
    saved.update(h_t=h_t, proj=proj)

    qkv = _attn_prep(f"attn_prep{l}", proj, cos_t, sin_t)
    outs, lses = [], []
    for p, d in enumerate(DILATIONS):
        o, lse = _attn_fwd(f"attn_fwd{l}_{d}", *qkv[p], SEQ // d // SPAN)
        outs.append(o)
        lses.append(lse)
    mixed, mixed_t, attn, lse = _attn_merge(f"attn_merge{l}", outs, lses, small["attn_out_gain"][l][None, :])
    saved.update(qkv=qkv, attn=attn, lse=lse)

    lb3 = small["lower"][l].reshape(HGRN_HEADS, 1, HGRN_DIM)
    mixed, mixed_t, o_pre, states, scores = _hgrn_fwd(f"hgrn_fwd{l}", proj, lb3, small["hgrn_out_gain"][l][None, :],
                                                      stack, mixed, mixed_t)
    wo, wu, wd = rest(mixed)
    saved.update(mixed_t=mixed_t, o_pre=o_pre, states=states, scores=scores, lb3=lb3, weights=(win, wo, wu, wd))

    x_mid, h2, h2_t = _mm_accum(f"out_proj{l}", mixed, wo, False, tm, x_in, next_gain=small["norm_mlp"][l][None, :])
    saved["x_mid"] = x_mid

    a, relu_u, a_t = _mm_pieces(f"up{l}", h2, wu, False, tm, epilogue="relu2")
    new_stream = tuple(_mm_accum(f"down{l}", a, wd, False, tm, x_mid, next_gain=next_gain, loss=loss))
    saved.update(h2_t=h2_t, relu_u=relu_u, a_t=a_t)
    return new_stream, saved


def _layer_backward(l, dx, saved, small, consts, on_grads, after=None):
    win, wo, wu, wd = saved["weights"]
    cos_t, sin_t, stack, stack_t, head_sum = consts
    tm = MM_TILE

    dx, dx_b = dx
    du = _mm_pieces(f"d_u{l}", dx_b, wd, True, tm, epilogue="relu2_grad", extra=saved["relu_u"], after=after)
    d_wd = _mm_dw(f"d_wdown{l}", saved["a_t"], dx_b, False, tm)
    dxm, dxm_b, dg_mlp = _mm_accum(f"d_h2_{l}", du, wu, True, tm, dx,
                                   norm=(saved["x_mid"], small["norm_mlp"][l][None, :]))
    d_wu = _mm_dw(f"d_wup{l}", saved["h2_t"], du, True, tm)
    after_mlp = on_grads(l, "mlp", (d_wu, d_wd))

    d_mixed = _mm_pieces(f"d_mixed{l}", dxm_b, wo, True, tm, after=after_mlp)
    d_wo = _mm_dw(f"d_wout{l}", saved["mixed_t"], dxm_b, False, tm)
    d_rec = d_mixed

    d_out, delta, lses, dg_attn = _attn_bwd_prep(f"attn_bwd_prep{l}", d_mixed, saved["attn"], saved["lse"],
                                                 small["attn_out_gain"][l][None, :], head_sum)
    grads = []
    for p, d in enumerate(DILATIONS):
        grads.append(_attn_bwd(f"attn_bwd{l}_{d}", *saved["qkv"][p], d_out[p], delta[p], lses[p],
                               SEQ // d // SPAN))
    dp_attn = _attn_bwd_post(f"attn_bwd_post{l}", grads, cos_t, sin_t)

    dq_h, df_h, di_h, dg_h, d_lower, dg_hgrn = _hgrn_bwd(
        f"hgrn_bwd{l}", saved["proj"], d_rec, saved["o_pre"], saved["states"], saved["scores"],
        saved["lb3"], small["hgrn_out_gain"][l][None, :], stack, stack_t)
    dproj = [dp_attn, dq_h, df_h, di_h, dg_h]

    d_win = _mm_dw(f"d_win{l}", saved["h_t"], dproj, True, tm)
    after_mix = on_grads(l, "mix", (d_win, d_wo))
    dx_in, dx_in_b, dg_mix = _mm_accum(f"d_h{l}", dproj, win, True, tm, dxm,
                                       norm=(saved["x_in"], small["norm_mix"][l][None, :]), after=after_mix)

    small_grads = {"norm_mix": dg_mix[0], "attn_out_gain": dg_attn[0],
                   "lower": d_lower.reshape(HGRN_WIDTH),
                   "hgrn_out_gain": jnp.sum(dg_hgrn, axis=0).reshape(HGRN_DIM), "norm_mlp": dg_mlp[0]}
    return (dx_in, dx_in_b), after_mix, small_grads


def _local_step(xs, target, small, get_weights, on_grads):
    consts = _rope_tables() + _hgrn_consts() + (_head_sum_matrix(),)
    stream = (xs,) + tuple(_rms_fwd("norm_mix0", xs, small["norm_mix"][0][None, :]))
    saved = []
    for l in range(DEPTH):
        w, after = get_weights(l, stream[0])
        if l + 1 < DEPTH:
            stream, s = _layer_forward(l, stream, small, w, consts, next_gain=small["norm_mix"][l + 1][None, :],
                                       after=after)
        else:
            stream, s = _layer_forward(l, stream, small, w, consts, loss=(small["norm_final"][None, :], target),
                                       after=after)
        saved.append(s)
    dx_f, dx_b, dg_final, loss = stream
    dx = (dx_f, dx_b)
    small_grads = [None] * DEPTH
    after = None
    for l in reversed(range(DEPTH)):
        dx, after, small_grads[l] = _layer_backward(l, dx, saved[l], small, consts, on_grads, after=after)
    return loss, dx[0], dg_final[0], small_grads


def _pack_small(norm_mix, attn_out_gain, lb, hgrn_out_gain, norm_mlp, norm_final, last_row):
    rows = [norm_mix, attn_out_gain.reshape(1, D_MODEL), lb.reshape(1, D_MODEL),
            jnp.pad(hgrn_out_gain.reshape(1, DEPTH * HGRN_DIM), ((0, 0), (0, D_MODEL - DEPTH * HGRN_DIM))),
            norm_mlp, norm_final.reshape(1, D_MODEL), last_row.reshape(1, D_MODEL)]
    pack = jnp.concatenate(rows, axis=0)
    return jnp.pad(pack, ((0, PACK_ROWS - pack.shape[0]), (0, 0)))


def _unpack_small(pack):
    return (pack[0:2], pack[2].reshape(DEPTH, ATTN_WIDTH), pack[3].reshape(DEPTH, HGRN_WIDTH),
            pack[4, :DEPTH * HGRN_DIM].reshape(DEPTH, HGRN_DIM), pack[5:7], pack[7], pack[8])


def kernel(x, norm_mix, w_in, attn_out_gain, hgrn_lb_logits, hgrn_out_gain, w_out, norm_mlp, w_up, w_down, norm_final, loss_target, m_norm_mix, m_w_in, m_attn_out_gain, m_hgrn_lb_logits, m_hgrn_out_gain, m_w_out, m_norm_mlp, m_w_up, m_w_down, m_norm_final, v_norm_mix, v_w_in, v_attn_out_gain, v_hgrn_lb_logits, v_hgrn_out_gain, v_w_out, v_norm_mlp, v_w_up, v_w_down, v_norm_final):
    lower, lower_vjp = jax.vjp(_lower_bounds, hgrn_lb_logits)
    small = {"norm_mix": norm_mix, "attn_out_gain": attn_out_gain, "lower": lower,
             "hgrn_out_gain": hgrn_out_gain, "norm_mlp": norm_mlp, "norm_final": norm_final}
    big_w = (w_in, w_out, w_up, w_down)

    x_pos, y_pos, core = lax.axis_index("x"), lax.axis_index("y"), lax.axis_index("c")
    me = (2 * x_pos + y_pos).astype(jnp.int32).reshape(1)
    place = jnp.stack([4 * x_pos + 2 * y_pos + core, 2 * x_pos + y_pos, core]).astype(jnp.int32)
    shards = [[w[l].astype(BF16) for w in big_w] for l in range(DEPTH)]
    in_flight = {}

    def start_gather(name, some, after):
        lands = [_own_slot(f"own_{name}_{i}", s, me) for i, s in enumerate(some)]
        sems, passed, token = _exchange_start(f"start_{name}", some, lands, "gather", after)
        in_flight[name] = (sems, passed)
        return token

    def finish_gather(name, after):
        return _exchange_wait(f"wait_{name}", *in_flight.pop(name), "gather", after)[1]

    def get_weights(l, stream):
        if l == 0:
            (win,) = _gather_weights("gather_w_in0", shards[0][:1])
            token = start_gather("gather_rest0", shards[0][1:], win)
            token = start_gather("gather_w_in1", shards[1][:1], token)
            token = start_gather("gather_rest1", shards[1][1:], token)
            return (win, lambda after: finish_gather("gather_rest0", after)), token
        (win,) = finish_gather("gather_w_in1", stream)
        return (win, lambda after: finish_gather("gather_rest1", after)), None

    reduced = {}

    def start_exchange(name, grads):
        srcs, lands = [g for g, _ in grads], [land for _, land in grads]
        sems, passed, token = _exchange_start(f"start_{name}", srcs, lands, "scatter")
        in_flight[name] = (sems, passed)
        return token

    def finish_exchange(name, after):
        own, landed = _exchange_wait(f"wait_{name}", *in_flight.pop(name), "scatter", after)
        return [_sum_devices(f"sum_{name}_{i}", p, g, place) for i, (p, g) in enumerate(zip(landed, own))]

    def on_grads(l, group, grads):
        if (l, group) == (1, "mlp"):
            return start_exchange("mlp1", grads)
        if (l, group) == (1, "mix"):
            return start_exchange("mix1", grads)
        if (l, group) == (0, "mlp"):
            token = start_exchange("mlp0", grads)
            reduced[(1, "mlp")] = finish_exchange("mlp1", token)
            reduced[(1, "mix")] = finish_exchange("mix1", token)
            return token
        token = start_exchange("mix0", grads)
        reduced[(0, "mlp")] = finish_exchange("mlp0", token)
        return token

    loss, dx, dg_final, sg = _local_step(x[0], loss_target[0], small, get_weights, on_grads)

    big_m = (m_w_in, m_w_out, m_w_up, m_w_down)
    big_v = (v_w_in, v_w_out, v_w_up, v_w_down)
    names = ("w_in", "w_out", "w_up", "w_down")
    big_g, big_delta, big_new_m, big_new_v = [None] * 4, [None] * 4, [None] * 4, [None] * 4

    def finish_weights(group, which):
        whole = _share_halves(f"share_{group}", [[reduced[(l, group)][i] for l in range(DEPTH)] for i in range(2)])
        for i, w in enumerate(which):
            shape = big_w[w].shape
            flat = lambda arr: arr.reshape(shape[0] * shape[1], shape[2])
            d, m2, v2 = _adamw(f"adamw_{names[w]}", flat(big_w[w]), flat(whole[i]), flat(big_m[w]), flat(big_v[w]))
            big_g[w], big_delta[w] = whole[i], d.reshape(shape)
            big_new_m[w], big_new_v[w] = m2.reshape(shape), v2.reshape(shape)

    finish_weights("mlp", (2, 3))
    reduced[(0, "mix")] = finish_exchange("mix0", big_delta[3])
    finish_weights("mix", (0, 1))

    stack2 = lambda key: jnp.stack([sg[l][key] for l in range(DEPTH)])
    pack = _pack_small(stack2("norm_mix"), stack2("attn_out_gain"), stack2("lower"), stack2("hgrn_out_gain"),
                       stack2("norm_mlp"), dg_final, jnp.broadcast_to(loss[0, 0], (D_MODEL,)))
    g_mix, g_attn, g_lower, g_hgrn, g_mlp, g_final, loss_row = _unpack_small(_all_reduce_small(pack, big_delta[0]))
    (g_logits,) = lower_vjp(g_lower)

    zeros_row = jnp.zeros((D_MODEL,), F32)
    small_w = (norm_mix, attn_out_gain, hgrn_lb_logits, hgrn_out_gain, norm_mlp, norm_final)
    small_m = (m_norm_mix, m_attn_out_gain, m_hgrn_lb_logits, m_hgrn_out_gain, m_norm_mlp, m_norm_final)
    small_v = (v_norm_mix, v_attn_out_gain, v_hgrn_lb_logits, v_hgrn_out_gain, v_norm_mlp, v_norm_final)
    small_g = (g_mix, g_attn, g_logits, g_hgrn, g_mlp, g_final)
    packs = [_pack_small(*t, zeros_row) for t in (small_w, small_g, small_m, small_v)]
    small_delta, small_new_m, small_new_v = [_unpack_small(p)[:6] for p in _adamw("adamw_small", *packs)]

    def ordered(small6, big4):
        mix, attn, lbl, hg, mlp, fin = small6
        return (mix, big4[0], attn, lbl, hg, big4[1], mlp, big4[2], big4[3], fin)

    return ((loss_row[0], dx[None]) + ordered(small_g, big_g) + ordered(small_delta, big_delta)
            + ordered(small_new_m, big_new_m) + ordered(small_new_v, big_new_v))
```

```python
import numpy as np
import jax
import jax.numpy as jnp
from jax import lax
from jax.experimental import pallas as pl
from jax.experimental.pallas import tpu as pltpu

F32 = jnp.float32
BF16 = jnp.bfloat16
MESH = pl.DeviceIdType.MESH

SEQ = 4096
D_MODEL = 1024
DEPTH = 2
ATTN_WIDTH = 512
HEAD_DIM = 64
HGRN_HEADS = 4
HGRN_DIM = 128
HGRN_WIDTH = 512
IN_W = 3584
MLP_HIDDEN = 4096
N_CHIPS = 4
N_DEV = 8
DILATIONS = (1, 4, 16)
SPAN = 128
ROPE_THETA = 10000.0
NORM_EPS = 1e-6
MASK_VALUE = -1e30
CHUNK = 128
ROW_TILE = 512
MM_TILE = 512
VMEM_LIMIT = 52 * 1024 * 1024

ADAM_LR = 0.001
ADAM_B1 = 0.9
ADAM_B2 = 0.999
ADAM_EPS = 1e-08
ADAM_WD = 0.01
ADAM_STEP = 10

PACK_ROWS = 16


def _params(n_axes):
    return pltpu.CompilerParams(dimension_semantics=("arbitrary",) * n_axes,
                                vmem_limit_bytes=VMEM_LIMIT)


def _dot(a, b):
    return jnp.dot(a.astype(BF16), b.astype(BF16), preferred_element_type=F32)


def _dot_nt(a, b):
    return lax.dot_general(a.astype(BF16), b.astype(BF16), (((1,), (1,)), ((), ())),
                           preferred_element_type=F32)


def _dot_tn(a, b):
    return lax.dot_general(a.astype(BF16), b.astype(BF16), (((0,), (0,)), ((), ())),
                           preferred_element_type=F32)


def _sigmoid(x):
    return 1.0 / (1.0 + jnp.exp(-x))


def _spec(shape, index_map):
    return pl.BlockSpec(shape, index_map)


def _mm_pieces(name, a, w, nt, tm, epilogue="none", extra=None, after=None):
    s = a.shape[0]
    pw = w.shape[1] if nt else w.shape[2]
    width = N_CHIPS * pw

    def body(a_ref, w_ref, *rest):
        e_ref = rest[0] if extra is not None else None
        outs = rest[-3:] if epilogue == "relu2" else rest[-1:]
        av = a_ref[...].astype(BF16)
        for j in range(N_CHIPS):
            cols = slice(j * pw, (j + 1) * pw)
            r = _dot_nt(av, w_ref[j]) if nt else _dot(av, w_ref[j])
            if epilogue == "relu2":
                relu = jnp.maximum(r, 0.0)
                r = relu * relu
                outs[1][:, cols] = relu.astype(BF16)
                outs[2][cols, :] = r.T.astype(BF16)
            elif epilogue == "relu2_grad":
                r = r * (2.0 * e_ref[:, cols].astype(F32))
            outs[0][:, cols] = r.astype(outs[0].dtype)

    row = lambda width_: _spec((tm, width_), lambda i: (i, 0))
    in_specs = [row(a.shape[1]), _spec(w.shape, lambda i: (0, 0, 0))]
    args = [a, w]
    if extra is not None:
        in_specs.append(row(width))
        args.append(extra)
    if after is not None:
        in_specs.append(pl.BlockSpec(memory_space=pl.ANY))
        args.append(after)
    if epilogue == "relu2":
        out_specs = [row(width), row(width), _spec((width, tm), lambda i: (0, i))]
        out_shape = [jax.ShapeDtypeStruct((s, width), BF16)] * 2 + [jax.ShapeDtypeStruct((width, s), BF16)]
    else:
        out_specs = row(width)
        out_shape = jax.ShapeDtypeStruct((s, width), BF16 if epilogue == "relu2_grad" else F32)
    return pl.pallas_call(body, name=name, grid=(s // tm,), in_specs=in_specs, out_specs=out_specs,
                          out_shape=out_shape, compiler_params=_params(1))(*args)


def _mm_accum(name, a, w, nt, tm, resid, norm=None, after=None, next_gain=None, loss=None):
    pieces = list(a) if isinstance(a, (list, tuple)) else [a]
    n_a = len(pieces)
    s = pieces[0].shape[0]
    pk = w.shape[2] if nt else w.shape[1]
    d = w.shape[1] if nt else w.shape[2]

    def body(*refs):
        a_refs, w_ref, resid_ref, rest = refs[:n_a], refs[n_a], refs[n_a + 1], refs[n_a + 2:]
        av = a_refs[0][...] if n_a == 1 else jnp.concatenate([ref[...] for ref in a_refs], axis=1)
        r = None
        for j in range(N_CHIPS):
            piece = av[:, j * pk:(j + 1) * pk].astype(BF16)
            term = _dot_nt(piece, w_ref[j]) if nt else _dot(piece, w_ref[j])
            r = term if r is None else r + term
        if loss is not None:
            g_ref, t_ref = rest[:2]
            dx_ref, dxb_ref, dg_ref, loss_ref, acc = rest[-5:]
            i = pl.program_id(0)

            @pl.when(i == 0)
            def _():
                dg_ref[...] = jnp.zeros_like(dg_ref)
                acc[...] = jnp.zeros_like(acc)

            xv = r + resid_ref[...]
            g = g_ref[...]
            rs = lax.rsqrt(jnp.mean(xv * xv, axis=1, keepdims=True) + NORM_EPS)
            xhat = xv * rs
            err = xhat * g - t_ref[...]
            acc[...] += jnp.sum(err * err, axis=0, keepdims=True)
            dy = err * (1.0 / d)
            dyg = dy * g
            dx = rs * (dyg - xhat * jnp.mean(dyg * xhat, axis=1, keepdims=True))
            dx_ref[...] = dx
            dxb_ref[...] = dx.astype(BF16)
            dg_ref[...] += jnp.sum(dy * xhat, axis=0, keepdims=True)

            @pl.when(i == s // tm - 1)
            def _():
                total = jnp.sum(acc[...], axis=1, keepdims=True) * (0.5 / d)
                loss_ref[...] = jnp.broadcast_to(total, loss_ref.shape)

            return
        if norm is None and next_gain is None:
            rest[-1][...] = r + resid_ref[...]
            return
        if norm is None:
            g_ref = rest[0]
            x_out, h_out, ht_out = rest[-3:]
            xv = r + resid_ref[...]
            x_out[...] = xv
            h = xv * lax.rsqrt(jnp.mean(xv * xv, axis=1, keepdims=True) + NORM_EPS) * g_ref[...]
            h_out[...] = h.astype(BF16)
            ht_out[...] = h.T.astype(BF16)
            return
        x_ref, g_ref = rest[:2]
        dx_ref, dxb_ref, dg_ref = rest[-3:]

        @pl.when(pl.program_id(0) == 0)
        def _():
            dg_ref[...] = jnp.zeros_like(dg_ref)

        xv = x_ref[...]
        rs = lax.rsqrt(jnp.mean(xv * xv, axis=1, keepdims=True) + NORM_EPS)
        xhat = xv * rs
        rg = r * g_ref[...]
        dx = resid_ref[...] + rs * (rg - xhat * jnp.mean(rg * xhat, axis=1, keepdims=True))
        dx_ref[...] = dx
        dxb_ref[...] = dx.astype(BF16)
        dg_ref[...] += jnp.sum(r * xhat, axis=0, keepdims=True)

    row = lambda width: _spec((tm, width), lambda i: (i, 0))
    in_specs = [row(p.shape[1]) for p in pieces] + [_spec(w.shape, lambda i: (0, 0, 0)), row(d)]
    args = pieces + [w, resid]
    scratch = []
    if loss is not None:
        in_specs += [_spec((1, d), lambda i: (0, 0)), row(d)]
        args += list(loss)
        out_specs = [row(d), row(d), _spec((1, d), lambda i: (0, 0)), _spec((1, 128), lambda i: (0, 0))]
        out_shape = [jax.ShapeDtypeStruct((s, d), F32), jax.ShapeDtypeStruct((s, d), BF16),
                     jax.ShapeDtypeStruct((1, d), F32), jax.ShapeDtypeStruct((1, 128), F32)]
        scratch = [pltpu.VMEM((1, d), F32)]
    elif norm is None and next_gain is None:
        out_specs, out_shape = row(d), jax.ShapeDtypeStruct((s, d), F32)
    elif norm is None:
        in_specs.append(_spec((1, d), lambda i: (0, 0)))
        args.append(next_gain)
        out_specs = [row(d), row(d), _spec((d, tm), lambda i: (0, i))]
        out_shape = [jax.ShapeDtypeStruct((s, d), F32), jax.ShapeDtypeStruct((s, d), BF16),
                     jax.ShapeDtypeStruct((d, s), BF16)]
    else:
        in_specs += [row(d), _spec((1, d), lambda i: (0, 0))]
        args += list(norm)
        out_specs = [row(d), row(d), _spec((1, d), lambda i: (0, 0))]
        out_shape = [jax.ShapeDtypeStruct((s, d), F32), jax.ShapeDtypeStruct((s, d), BF16),
                     jax.ShapeDtypeStruct((1, d), F32)]
    if after is not None:
        in_specs.append(pl.BlockSpec(memory_space=pl.ANY))
        args.append(after)
    return pl.pallas_call(body, name=name, grid=(s // tm,), in_specs=in_specs, out_specs=out_specs,
                          out_shape=out_shape, scratch_shapes=scratch, compiler_params=_params(1))(*args)


def _mm_dw(name, a_t, b, by_cols, tk):
    pieces = list(b) if isinstance(b, (list, tuple)) else [b]
    n_b = len(pieces)
    m, s = a_t.shape
    n = sum(p.shape[1] for p in pieces)
    shape = (N_CHIPS, m, n // N_CHIPS) if by_cols else (N_CHIPS, m // N_CHIPS, n)
    n_steps = s // tk

    def body(a_ref, *rest):
        b_refs, o_ref, acc = rest[:n_b], rest[n_b], rest[-1]

        @pl.when(pl.program_id(0) == 0)
        def _():
            acc[...] = jnp.zeros_like(acc)

        bv = b_refs[0][...] if n_b == 1 else jnp.concatenate([ref[...] for ref in b_refs], axis=1)
        for j in range(N_CHIPS):
            if by_cols:
                acc[j] += _dot(a_ref[...], bv[:, j * shape[2]:(j + 1) * shape[2]])
            else:
                acc[j] += _dot(a_ref[j * shape[1]:(j + 1) * shape[1], :], bv)

        @pl.when(pl.program_id(0) == n_steps - 1)
        def _():
            o_ref[...] = acc[...].astype(BF16)

    return pl.pallas_call(
        body, name=name, grid=(n_steps,),
        in_specs=[_spec((m, tk), lambda k: (0, k))] + [_spec((tk, p.shape[1]), lambda k: (k, 0)) for p in pieces],
        out_specs=[_spec(shape, lambda k: (0, 0, 0)), ANY_SPEC],
        out_shape=[jax.ShapeDtypeStruct(shape, BF16),
                   jax.ShapeDtypeStruct((N_DEV, shape[1] // 2, shape[2]), BF16)],
        scratch_shapes=[pltpu.VMEM(shape, F32)],
        compiler_params=_params(1))(a_t, *pieces)


def _rms_fwd(name, x, gain):
    s, d = x.shape
    t = ROW_TILE

    def body(x_ref, g_ref, h_ref, ht_ref):
        xv = x_ref[...]
        r = lax.rsqrt(jnp.mean(xv * xv, axis=1, keepdims=True) + NORM_EPS)
        h = xv * r * g_ref[...]
        h_ref[...] = h.astype(BF16)
        ht_ref[...] = h.T.astype(BF16)

    return pl.pallas_call(
        body, name=name, grid=(s // t,),
        in_specs=[_spec((t, d), lambda i: (i, 0)), _spec((1, d), lambda i: (0, 0))],
        out_specs=[_spec((t, d), lambda i: (i, 0)), _spec((d, t), lambda i: (0, i))],
        out_shape=[jax.ShapeDtypeStruct((s, d), BF16), jax.ShapeDtypeStruct((d, s), BF16)],
        compiler_params=_params(1),
    )(x, gain)


def _rope_tables():
    half = HEAD_DIM // 2
    inv_freq = ROPE_THETA ** (-jnp.arange(half, dtype=F32) / half)
    ang = jnp.arange(SEQ, dtype=jnp.int32).astype(F32)[:, None] * inv_freq[None, :]
    cos, sin = jnp.cos(ang), jnp.sin(ang)
    cos_t = jnp.concatenate([cos, cos, cos, cos], axis=1)
    sin_t = jnp.concatenate([-sin, sin, -sin, sin], axis=1)
    return cos_t, sin_t


def _swap_halves(x):
    lane = lax.broadcasted_iota(jnp.int32, x.shape, 1)
    first = (lane % HEAD_DIM) < (HEAD_DIM // 2)
    return jnp.where(first, pltpu.roll(x, 128 - HEAD_DIM // 2, 1), pltpu.roll(x, HEAD_DIM // 2, 1))


def _permuted_specs(t, width):
    specs = [_spec((t, width), lambda i: (i, 0))]
    for d in DILATIONS[1:]:
        specs.append(_spec((d, t // d, width), lambda i: (0, i, 0)))
    return specs


def _permuted_shapes(width, dtype):
    shapes = [jax.ShapeDtypeStruct((SEQ, width), dtype)]
    for d in DILATIONS[1:]:
        shapes.append(jax.ShapeDtypeStruct((d, SEQ // d, width), dtype))
    return shapes


def _attn_prep(name, proj, cos_t, sin_t):
    t = ROW_TILE
    w = ATTN_WIDTH

    def body(q_ref, k_ref, v_ref, cos_ref, sin_ref, *rest):
        outs, scr = rest[:9], rest[9]
        cosv, sinv = cos_ref[...], sin_ref[...]
        for a, (src, roped, scale) in enumerate(((q_ref, True, HEAD_DIM ** -0.5),
                                                 (k_ref, True, 1.0), (v_ref, False, 1.0))):
            o1, o4, o16 = outs[3 * a:3 * a + 3]
            for cb in range(w // 128):
                cols = slice(cb * 128, (cb + 1) * 128)
                val = src[:, cols]
                if roped:
                    val = (val * cosv + _swap_halves(val) * sinv) * scale
                scr[...] = val
                o1[:, cols] = val.astype(BF16)
                for o_ref, d in ((o4, 4), (o16, 16)):
                    for r in range(d):
                        o_ref[r, :, cols] = scr[pl.ds(r, t // d, stride=d), :].astype(BF16)

    out_specs = _permuted_specs(t, w) * 3
    out_shape = _permuted_shapes(w, BF16) * 3
    outs = pl.pallas_call(
        body, name=name, grid=(SEQ // t,),
        in_specs=[_spec((t, w), lambda i: (i, 0)), _spec((t, w), lambda i: (i, 1)),
                  _spec((t, w), lambda i: (i, 2)),
                  _spec((t, 128), lambda i: (i, 0)), _spec((t, 128), lambda i: (i, 0))],
        out_specs=out_specs, out_shape=out_shape,
        scratch_shapes=[pltpu.VMEM((t, 128), F32)],
        compiler_params=_params(1),
    )(proj, proj, proj, cos_t, sin_t)
    q, k, v = outs[0:3], outs[3:6], outs[6:9]
    flat = lambda arr: arr.reshape(SEQ, w)
    return [(flat(q[p]), flat(k[p]), flat(v[p])) for p in range(3)]


def _band_masks():
    row = lax.broadcasted_iota(jnp.int32, (2 * SPAN, 2 * SPAN), 0) % SPAN
    col = lax.broadcasted_iota(jnp.int32, (2 * SPAN, 2 * SPAN), 1)
    is_prev = col < SPAN
    band = (is_prev & (col >= row)) | (~is_prev & (col - SPAN <= row))
    head0 = lax.broadcasted_iota(jnp.int32, (SPAN, 128), 1) < HEAD_DIM
    return band, is_prev, head0


def _stack_heads(x, head0):
    zero = jnp.zeros_like(x)
    return jnp.concatenate([jnp.where(head0, x, zero), jnp.where(head0, zero, x)], axis=0)


ATTN_UNROLL = 4


def _for_each_block(block, seg_blocks):
    def trip(i, carry):
        for u in range(ATTN_UNROLL):
            static = seg_blocks <= ATTN_UNROLL
            block(i * ATTN_UNROLL + u, (u % seg_blocks == 0) if static else None)
        return carry

    lax.fori_loop(0, SEQ // SPAN // ATTN_UNROLL, trip, 0)


def _attn_fwd(name, q, k, v, seg_blocks):
    def body(q_ref, k_ref, v_ref, o_ref, lse_ref):
        band, is_prev, head0 = _band_masks()

        def block(b, first):
            cur = pl.ds(pl.multiple_of(b * SPAN, SPAN), SPAN)
            qs = _stack_heads(q_ref[cur, :], head0)
            if first is True:
                kcat, vcat, ok = k_ref[cur, :], v_ref[cur, :], band[:, SPAN:]
            else:
                prev = pl.ds(pl.multiple_of(jnp.maximum(b - 1, 0) * SPAN, SPAN), SPAN)
                kcat = jnp.concatenate([k_ref[prev, :], k_ref[cur, :]], axis=0)
                vcat = jnp.concatenate([v_ref[prev, :], v_ref[cur, :]], axis=0)
                ok = band if first is False else band & (((b % seg_blocks) != 0) | ~is_prev)
            s = jnp.where(ok, _dot_nt(qs, kcat), MASK_VALUE)
            m = jnp.max(s, axis=1, keepdims=True)
            p = jnp.exp(s - m)
            l = jnp.sum(p, axis=1, keepdims=True)
            pv = _dot(p, vcat) * (1.0 / l)
            lse = m + jnp.log(l)
            o_ref[cur, :] = jnp.where(head0, pv[:SPAN], pv[SPAN:])
            lse_ref[cur, :] = jnp.where(head0, lse[:SPAN], lse[SPAN:])

        _for_each_block(block, seg_blocks)

    col = _spec((SEQ, 128), lambda j: (0, j))
    return pl.pallas_call(
        body, name=name, grid=(ATTN_WIDTH // 128,),
        in_specs=[col, col, col], out_specs=[col, col],
        out_shape=[jax.ShapeDtypeStruct((SEQ, ATTN_WIDTH), F32)] * 2,
        compiler_params=_params(1),
    )(q, k, v)


def _unpermute(dst, src_ref, d, cols):
    n = dst.shape[0] // d
    for r in range(d):
        dst[pl.ds(r, n, stride=d), :] = src_ref[r, :, cols].astype(dst.dtype)


def _attn_merge(name, outs, lses, gain):
    t = ROW_TILE
    w = ATTN_WIDTH

    def body(o1, o4, o16, l1, l4, l16, g_ref, an_ref, ant_ref, attn_ref, lse_ref, so4, so16, sl4, sl16):
        for cb in range(w // 128):
            cols = slice(cb * 128, (cb + 1) * 128)
            _unpermute(so4, o4, 4, cols)
            _unpermute(so16, o16, 16, cols)
            _unpermute(sl4, l4, 4, cols)
            _unpermute(sl16, l16, 16, cols)
            la, lb, lc = l1[:, cols], sl4[...], sl16[...]
            m = jnp.maximum(jnp.maximum(la, lb), lc)
            ea, eb, ec = jnp.exp(la - m), jnp.exp(lb - m), jnp.exp(lc - m)
            tot = ea + eb + ec
            attn_ref[:, cols] = (ea * o1[:, cols] + eb * so4[...] + ec * so16[...]) / tot
            lse_ref[:, cols] = m + jnp.log(tot)
        attn = attn_ref[...]
        r = lax.rsqrt(jnp.mean(attn * attn, axis=1, keepdims=True) + NORM_EPS)
        an = attn * r * g_ref[...]
        an_ref[...] = an.astype(BF16)
        ant_ref[...] = an.T.astype(BF16)

    views = lambda arrs: [arrs[0], arrs[1].reshape(4, SEQ // 4, w), arrs[2].reshape(16, SEQ // 16, w)]
    row = _spec((t, w), lambda i: (i, 0))
    return pl.pallas_call(
        body, name=name, grid=(SEQ // t,),
        in_specs=_permuted_specs(t, w) * 2 + [_spec((1, w), lambda i: (0, 0))],
        out_specs=[row, _spec((w, t), lambda i: (0, i)), row, row],
        out_shape=[jax.ShapeDtypeStruct((SEQ, 2 * w), BF16), jax.ShapeDtypeStruct((2 * w, SEQ), BF16),
                   jax.ShapeDtypeStruct((SEQ, w), F32), jax.ShapeDtypeStruct((SEQ, w), F32)],
        scratch_shapes=[pltpu.VMEM((t, 128), F32)] * 4,
        compiler_params=_params(1),
    )(*views(outs), *views(lses), gain)


def _head_sum_matrix():
    i = np.arange(ATTN_WIDTH)
    return jnp.asarray((i[:, None] // HEAD_DIM) == (i[None, :] // HEAD_DIM), dtype=F32)


def _attn_bwd_prep(name, d_an, attn, lse, gain, head_sum):
    t = ROW_TILE
    w = ATTN_WIDTH

    def body(dan_ref, attn_ref, lse_ref, g_ref, hs_ref, *rest):
        (do1, do4, do16, dl1, dl4, dl16, ls4, ls16, dg_ref), (sdo, sdl, sls) = rest[:9], rest[9:]

        @pl.when(pl.program_id(0) == 0)
        def _():
            dg_ref[...] = jnp.zeros_like(dg_ref)

        attn = attn_ref[...]
        dan = dan_ref[...]
        r = lax.rsqrt(jnp.mean(attn * attn, axis=1, keepdims=True) + NORM_EPS)
        xhat = attn * r
        dg_ref[...] += jnp.sum(dan * xhat, axis=0, keepdims=True)
        dang = dan * g_ref[...]
        d_o = r * (dang - xhat * jnp.mean(dang * xhat, axis=1, keepdims=True))
        delta = jnp.dot(d_o * attn, hs_ref[...], preferred_element_type=F32,
                        precision=lax.Precision.HIGHEST)
        do1[...] = d_o.astype(BF16)
        dl1[...] = delta
        for cb in range(w // 128):
            cols = slice(cb * 128, (cb + 1) * 128)
            sdo[...] = d_o[:, cols]
            sdl[...] = delta[:, cols]
            sls[...] = lse_ref[:, cols]
            for d, o_do, o_dl, o_ls in ((4, do4, dl4, ls4), (16, do16, dl16, ls16)):
                for rr in range(d):
                    rows = pl.ds(rr, t // d, stride=d)
                    o_do[rr, :, cols] = sdo[rows, :].astype(BF16)
                    o_dl[rr, :, cols] = sdl[rows, :]
                    o_ls[rr, :, cols] = sls[rows, :]

    row = _spec((t, w), lambda i: (i, 0))
    perm = _permuted_specs(t, w)
    outs = pl.pallas_call(
        body, name=name, grid=(SEQ // t,),
        in_specs=[row, row, row, _spec((1, w), lambda i: (0, 0)), _spec((w, w), lambda i: (0, 0))],
        out_specs=perm + perm + perm[1:] + [_spec((1, w), lambda i: (0, 0))],
        out_shape=(_permuted_shapes(w, BF16) + _permuted_shapes(w, F32) + _permuted_shapes(w, F32)[1:]
                   + [jax.ShapeDtypeStruct((1, w), F32)]),
        scratch_shapes=[pltpu.VMEM((t, 128), F32)] * 3,
        compiler_params=_params(1),
    )(d_an, attn, lse, gain, head_sum)
    flat = lambda arr: arr.reshape(SEQ, w)
    d_out = [flat(a) for a in outs[0:3]]
    delta = [flat(a) for a in outs[3:6]]
    lses = [lse, flat(outs[6]), flat(outs[7])]
    return d_out, delta, lses, outs[8]


def _attn_bwd(name, q, k, v, d_out, delta, lse, seg_blocks):
    def body(q_ref, k_ref, v_ref, do_ref, dl_ref, lse_ref, dq_ref, dk_out, dv_out, dk_ref, dv_ref):
        band, is_prev, head0 = _band_masks()
        dk_ref[...] = jnp.zeros_like(dk_ref)
        dv_ref[...] = jnp.zeros_like(dv_ref)

        def per_head(x):
            return jnp.concatenate([x[:, 0:1], x[:, HEAD_DIM:HEAD_DIM + 1]], axis=0)

        def block(b, first):
            cur = pl.ds(pl.multiple_of(b * SPAN, SPAN), SPAN)
            qs = _stack_heads(q_ref[cur, :], head0)
            dos = _stack_heads(do_ref[cur, :], head0)
            if first is True:
                kcat, vcat, ok = k_ref[cur, :], v_ref[cur, :], band[:, SPAN:]
            else:
                prev = pl.ds(pl.multiple_of(jnp.maximum(b - 1, 0) * SPAN, SPAN), SPAN)
                kcat = jnp.concatenate([k_ref[prev, :], k_ref[cur, :]], axis=0)
                vcat = jnp.concatenate([v_ref[prev, :], v_ref[cur, :]], axis=0)
                ok = band if first is False else band & (((b % seg_blocks) != 0) | ~is_prev)
            p = jnp.where(ok, jnp.exp(_dot_nt(qs, kcat) - per_head(lse_ref[cur, :])), 0.0)
            ds = p * (_dot_nt(dos, vcat) - per_head(dl_ref[cur, :]))
            dq = _dot(ds, kcat)
            dq_ref[cur, :] = jnp.where(head0, dq[:SPAN], dq[SPAN:]).astype(BF16)
            dk = _dot_tn(ds, qs)
            dv = _dot_tn(p, dos)
            if first is not True:
                dk_ref[prev, :] += dk[:SPAN]
                dv_ref[prev, :] += dv[:SPAN]
            dk_ref[cur, :] += dk[-SPAN:]
            dv_ref[cur, :] += dv[-SPAN:]

        _for_each_block(block, seg_blocks)
        dk_out[...] = dk_ref[...].astype(BF16)
        dv_out[...] = dv_ref[...].astype(BF16)

    col = _spec((SEQ, 128), lambda j: (0, j))
    return pl.pallas_call(
        body, name=name, grid=(ATTN_WIDTH // 128,),
        in_specs=[col] * 6, out_specs=[col] * 3,
        out_shape=[jax.ShapeDtypeStruct((SEQ, ATTN_WIDTH), BF16)] * 3,
        scratch_shapes=[pltpu.VMEM((SEQ, 128), F32)] * 2,
        compiler_params=_params(1),
    )(q, k, v, d_out, delta, lse)


def _attn_bwd_post(name, grads, cos_t, sin_t):
    t = ROW_TILE
    w = ATTN_WIDTH

    def body(*refs):
        ins, cos_ref, sin_ref, out_ref, s4, s16 = refs[:9], refs[9], refs[10], refs[11], refs[12], refs[13]
        cosv, sinv = cos_ref[...], sin_ref[...]
        for a in range(3):
            g1, g4, g16 = ins[a], ins[3 + a], ins[6 + a]
            for cb in range(w // 128):
                cols = slice(cb * 128, (cb + 1) * 128)
                _unpermute(s4, g4, 4, cols)
                _unpermute(s16, g16, 16, cols)
                val = g1[:, cols].astype(F32) + s4[...] + s16[...]
                if a < 2:
                    val = val * cosv + _swap_halves(val * sinv)
                if a == 0:
                    val = val * (HEAD_DIM ** -0.5)
                out_ref[:, a * w + cb * 128:a * w + (cb + 1) * 128] = val.astype(BF16)

    views = []
    for p, d in enumerate(DILATIONS):
        for a in range(3):
            views.append(grads[p][a] if d == 1 else grads[p][a].reshape(d, SEQ // d, w))
    perm = _permuted_specs(t, w)
    in_specs = [perm[0]] * 3 + [perm[1]] * 3 + [perm[2]] * 3
    return pl.pallas_call(
        body, name=name, grid=(SEQ // t,),
        in_specs=in_specs + [_spec((t, 128), lambda i: (i, 0))] * 2,
        out_specs=_spec((t, 3 * w), lambda i: (i, 0)),
        out_shape=jax.ShapeDtypeStruct((SEQ, 3 * w), BF16),
        scratch_shapes=[pltpu.VMEM((t, 128), F32)] * 2,
        compiler_params=_params(1),
    )(*views, cos_t, sin_t)


N_LEVELS = 7
HGRN_PAIR = 2


def _hgrn_consts():
    c = CHUNK
    i = np.arange(c)[:, None]
    s = np.arange(c)[None, :]
    blocks = [s <= i]
    for lv in range(N_LEVELS):
        bs = c >> lv
        h = bs // 2
        m = (i // bs) * bs + h - 1
        second = (i % bs) >= h
        blocks.append((second & (s > m) & (s <= i)) | (~second & (s > i) & (s <= m)))
    blocks.append(s > i)
    stack = np.concatenate(blocks, axis=0).astype(np.float32)
    return jnp.asarray(stack, dtype=BF16), jnp.asarray(stack.T, dtype=BF16)


def _exact_dot(m01, x):
    hi = x.astype(BF16)
    lo = (x - hi.astype(F32)).astype(BF16)
    n = x.shape[1]
    full = jnp.dot(m01, jnp.concatenate([hi, lo], axis=1), preferred_element_type=F32)
    return full[:, :n] + full[:, n:]


def _hgrn_gates(qh, z, lb):
    sq = _sigmoid(qh)
    q = qh * sq * (HGRN_DIM ** -0.5)
    sig = _sigmoid(z)
    sigm = _sigmoid(-z)
    f = lb + (1.0 - lb) * sig
    k = (1.0 - lb) * sigm
    return q, k, f, sq, sig, sigm


def _level_masks(lv):
    row = lax.broadcasted_iota(jnp.int32, (CHUNK, CHUNK), 0)
    col = lax.broadcasted_iota(jnp.int32, (CHUNK, CHUNK), 1)
    shift = N_LEVELS - lv
    half = CHUNK >> (lv + 1)
    second = (row & half) != 0
    second_col = (col & half) != 0
    same = (row >> shift) == (col >> shift)
    return second, same & second & ~second_col, same & (second != second_col)


def _hgrn_fwd(name, proj, lb, gain, stack, mixed, mixed_t):
    t = ROW_TILE
    per = t // CHUNK
    n_rb = SEQ // t
    n_chunks = SEQ // CHUNK
    col0 = 3 * ATTN_WIDTH // 128
    pair_w = HGRN_PAIR * HGRN_DIM

    def body(q_ref, f_ref, i_ref, g_ref, lb_ref, gain_ref, stack_ref, mixed_in, mixed_t_in,
             rec_ref, rect_ref, o_ref, st_out, a_out, st):
        del mixed_in, mixed_t_in

        @pl.when(pl.program_id(1) == 0)
        def _():
            st[...] = jnp.zeros_like(st)

        row = lax.broadcasted_iota(jnp.int32, (CHUNK, CHUNK), 0)
        col = lax.broadcasted_iota(jnp.int32, (CHUNK, CHUNK), 1)
        for c, hh in [(c, hh) for c in range(per) for hh in range(HGRN_PAIR)]:
            rows = slice(c * CHUNK, (c + 1) * CHUNK)
            lanes = slice(hh * HGRN_DIM, (hh + 1) * HGRN_DIM)
            lbv = lb_ref[hh]
            qh, z, v, gh = q_ref[rows, lanes], f_ref[rows, lanes], i_ref[rows, lanes], g_ref[rows, lanes]
            q, k, f, _, _, _ = _hgrn_gates(qh, z, lbv)
            dec = _exact_dot(stack_ref[...], jnp.log(f))
            g = dec[0:CHUNK]
            to_end = dec[(N_LEVELS + 1) * CHUNK:(N_LEVELS + 2) * CHUNK]
            a = jnp.where(row == col, jnp.sum(q * k, axis=1, keepdims=True), 0.0)
            for lv in range(N_LEVELS):
                second, square, _ = _level_masks(lv)
                t = jnp.where(second, q, k) * jnp.exp(dec[(lv + 1) * CHUNK:(lv + 2) * CHUNK])
                a = a + jnp.where(square, _dot_nt(t, t), 0.0)
            st_prev = st[hh]
            st_out[hh, c] = st_prev
            a_out[hh, c] = a
            o = _dot(a, v) + _dot_nt(q * jnp.exp(g), st_prev)
            k_end = k * jnp.exp(to_end)
            st[hh] = st_prev * jnp.exp(g[CHUNK - 1:CHUNK, :]) + _dot(v.T, k_end)
            o_ref[rows, lanes] = o
            r = lax.rsqrt(jnp.mean(o * o, axis=1, keepdims=True) + NORM_EPS)
            rec = o * r * gain_ref[...] * (gh * _sigmoid(gh))
            rec_ref[rows, lanes] = rec.astype(BF16)
            rect_ref[lanes, rows] = rec.T.astype(BF16)

    def col_spec(tt):
        return _spec((t, pair_w), lambda h, rb: (rb, (col0 + HGRN_HEADS * tt) // HGRN_PAIR + h))

    chunk_spec = _spec((HGRN_PAIR, per, CHUNK, CHUNK), lambda h, rb: (h, rb, 0, 0))
    return pl.pallas_call(
        body, name=name, grid=(HGRN_HEADS // HGRN_PAIR, n_rb),
        in_specs=[col_spec(0), col_spec(1), col_spec(2), col_spec(3),
                  _spec((HGRN_PAIR, 1, HGRN_DIM), lambda h, rb: (h, 0, 0)),
                  _spec((1, HGRN_DIM), lambda h, rb: (0, 0)),
                  _spec(stack.shape, lambda h, rb: (0, 0)), ANY_SPEC, ANY_SPEC],
        out_specs=[_spec((t, pair_w), lambda h, rb: (rb, ATTN_WIDTH // pair_w + h)),
                   _spec((pair_w, t), lambda h, rb: (ATTN_WIDTH // pair_w + h, rb)),
                   _spec((t, pair_w), lambda h, rb: (rb, h)),
                   chunk_spec, chunk_spec],
        out_shape=[jax.ShapeDtypeStruct(mixed.shape, BF16),
                   jax.ShapeDtypeStruct(mixed_t.shape, BF16),
                   jax.ShapeDtypeStruct((SEQ, HGRN_WIDTH), F32),
                   jax.ShapeDtypeStruct((HGRN_HEADS, n_chunks, CHUNK, CHUNK), F32),
                   jax.ShapeDtypeStruct((HGRN_HEADS, n_chunks, CHUNK, CHUNK), F32)],
        scratch_shapes=[pltpu.VMEM((HGRN_PAIR, CHUNK, CHUNK), F32)],
        input_output_aliases={7: 0, 8: 1},
        compiler_params=_params(2),
    )(proj, proj, proj, proj, lb, gain, stack, mixed, mixed_t)


def _hgrn_bwd(name, proj, d_rec, o_pre, states, scores, lb, gain, stack, stack_t):
    t = ROW_TILE
    per = t // CHUNK
    n_rb = SEQ // t
    col0 = 3 * ATTN_WIDTH // 128
    pair_w = HGRN_PAIR * HGRN_DIM

    def body(q_ref, f_ref, i_ref, g_ref, drec_ref, o_ref, st_ref, a_ref, lb_ref, gain_ref,
             stack_ref, stack_t_ref, dq_ref, df_ref, di_ref, dg_ref, dlb_ref, dgain_ref, dst):
        @pl.when(pl.program_id(1) == 0)
        def _():
            dst[...] = jnp.zeros_like(dst)
            dlb_ref[...] = jnp.zeros_like(dlb_ref)
            dgain_ref[...] = jnp.zeros_like(dgain_ref)

        gain_v = gain_ref[...]
        row = lax.broadcasted_iota(jnp.int32, (CHUNK, CHUNK), 0)
        col = lax.broadcasted_iota(jnp.int32, (CHUNK, CHUNK), 1)
        for c, hh in [(c, hh) for c in reversed(range(per)) for hh in range(HGRN_PAIR)]:
            rows = slice(c * CHUNK, (c + 1) * CHUNK)
            lanes = slice(hh * HGRN_DIM, (hh + 1) * HGRN_DIM)
            lbv = lb_ref[hh]
            qh, z, v, gh = q_ref[rows, lanes], f_ref[rows, lanes], i_ref[rows, lanes], g_ref[rows, lanes]
            q, k, f, sq, sig, sigm = _hgrn_gates(qh, z, lbv)
            dec = _exact_dot(stack_ref[...], jnp.log(f))
            g = dec[0:CHUNK]
            to_end = dec[(N_LEVELS + 1) * CHUNK:(N_LEVELS + 2) * CHUNK]
            e_g = jnp.exp(g)
            e_end = jnp.exp(to_end)
            e_last = jnp.exp(g[CHUNK - 1:CHUNK, :])
            q_in = q * e_g
            k_end = k * e_end
            st_prev = st_ref[hh, c]
            a = a_ref[hh, c]
            dst_new = dst[hh]

            o = o_ref[rows, lanes]
            drec = drec_ref[rows, lanes]
            sg = _sigmoid(gh)
            r = lax.rsqrt(jnp.mean(o * o, axis=1, keepdims=True) + NORM_EPS)
            ohat = o * r
            d_gh = drec * (ohat * gain_v) * (sg * (1.0 + gh * (1.0 - sg)))
            d_on = drec * (gh * sg)
            dgain_ref[hh] += jnp.sum(d_on * ohat, axis=0, keepdims=True)
            d_ohat = d_on * gain_v
            d_o = r * (d_ohat - ohat * jnp.mean(d_ohat * ohat, axis=1, keepdims=True))

            d_sym = jnp.where(row >= col, _dot_nt(d_o, v), _dot_nt(v, d_o))
            d_v = _dot(a.T, d_o) + _dot_nt(k_end, dst_new)
            d_q_in = _dot(d_o, st_prev)
            d_k_end = _dot(v, dst_new)
            d_q = d_q_in * e_g
            d_k = d_k_end * e_end
            diag = jnp.sum(d_o * v, axis=1, keepdims=True)
            d_q = d_q + diag * k
            d_k = d_k + diag * q
            d_dec = [q_in * d_q_in]
            d_both, d_second = None, None
            for lv in range(N_LEVELS):
                e = jnp.exp(dec[(lv + 1) * CHUNK:(lv + 2) * CHUNK])
                second, _, mirrored = _level_masks(lv)
                t = jnp.where(second, q, k) * e
                d_t = _dot(jnp.where(mirrored, d_sym, 0.0), t)
                d_te = d_t * e
                d_both = d_te if d_both is None else d_both + d_te
                d_second = jnp.where(second, d_te, 0.0) if d_second is None else d_second + jnp.where(second, d_te, 0.0)
                d_dec.append(t * d_t)
            d_q = d_q + d_second
            d_k = d_k + (d_both - d_second)
            d_dec.append(k_end * d_k_end)
            flux = jnp.sum(dst_new * st_prev, axis=0, keepdims=True) * e_last
            d_lf = _exact_dot(stack_t_ref[...], jnp.concatenate(d_dec, axis=0)) + flux
            dst[hh] = dst_new * e_last + _dot(d_o.T, q_in)

            d_f = d_lf / f - d_k
            dlb_ref[hh] += jnp.sum(d_f * sigm, axis=0, keepdims=True)
            dq_ref[rows, lanes] = (d_q * (HGRN_DIM ** -0.5) * (sq * (1.0 + qh * (1.0 - sq)))).astype(BF16)
            df_ref[rows, lanes] = (d_f * (1.0 - lbv) * sig * sigm).astype(BF16)
            di_ref[rows, lanes] = d_v.astype(BF16)
            dg_ref[rows, lanes] = d_gh.astype(BF16)

    last = n_rb - 1

    def col_spec(tt):
        return _spec((t, pair_w), lambda h, rb: (last - rb, (col0 + HGRN_HEADS * tt) // HGRN_PAIR + h))

    head_col = _spec((t, pair_w), lambda h, rb: (last - rb, h))
    rec_col0 = (d_rec.shape[1] - HGRN_WIDTH) // pair_w
    d_rec_col = _spec((t, pair_w), lambda h, rb: (last - rb, rec_col0 + h))
    chunk_spec = _spec((HGRN_PAIR, per, CHUNK, CHUNK), lambda h, rb: (h, last - rb, 0, 0))
    vec_spec = _spec((HGRN_PAIR, 1, HGRN_DIM), lambda h, rb: (h, 0, 0))
    outs = pl.pallas_call(
        body, name=name, grid=(HGRN_HEADS // HGRN_PAIR, n_rb),
        in_specs=[col_spec(0), col_spec(1), col_spec(2), col_spec(3), d_rec_col, head_col,
                  chunk_spec, chunk_spec, vec_spec,
                  _spec((1, HGRN_DIM), lambda h, rb: (0, 0)),
                  _spec(stack.shape, lambda h, rb: (0, 0)), _spec(stack_t.shape, lambda h, rb: (0, 0))],
        out_specs=[head_col] * 4 + [vec_spec, vec_spec],
        out_shape=[jax.ShapeDtypeStruct((SEQ, HGRN_WIDTH), BF16)] * 4
                  + [jax.ShapeDtypeStruct((HGRN_HEADS, 1, HGRN_DIM), F32)] * 2,
        scratch_shapes=[pltpu.VMEM((HGRN_PAIR, CHUNK, CHUNK), F32)],
        compiler_params=_params(2),
    )(proj, proj, proj, proj, d_rec, o_pre, states, scores, lb, gain, stack, stack_t)
    return outs


ANY_SPEC = pl.BlockSpec(memory_space=pl.ANY)


def _my_place():
    return lax.axis_index("x"), lax.axis_index("y"), lax.axis_index("c")


def _other_chips(x, y):
    return [(1 - x, y), (x, 1 - y), (1 - x, 1 - y)]


def _remote(src, dst, send_sem, recv_sem, device):
    return pltpu.make_async_remote_copy(src_ref=src, dst_ref=dst, send_sem=send_sem, recv_sem=recv_sem,
                                        device_id=device, device_id_type=MESH)


def _staged_copies(srcs, dsts, stage, sems):
    loads = [pltpu.make_async_copy(srcs[i], stage[i], sems.at[i]) for i in range(len(srcs))]
    for cp in loads:
        cp.start()
    stores = []
    for i, cp in enumerate(loads):
        cp.wait()
        stores.append(pltpu.make_async_copy(stage[i], dsts[i], sems.at[i]))
        stores[-1].start()
    return stores


def _gather_weights(name, shards):
    n = len(shards)

    def body(*refs):
        ins, outs = refs[:n], refs[n:2 * n]
        ici_send, ici_recv, d2d_send, d2d_recv, local_sems = refs[2 * n:2 * n + 5]
        stage = refs[2 * n + 5:]
        x, y, c = _my_place()
        me = 2 * x + y
        chips = _other_chips(x, y)

        def half(i, which):
            h = ins[i].shape[0] // 2
            return pl.ds(which * h, h)

        sends = []
        for i in range(n):
            for j, (px, py) in enumerate(chips):
                sends.append(_remote(ins[i].at[half(i, c), :], outs[i].at[me, half(i, c), :],
                                     ici_send.at[3 * i + j], ici_recv.at[3 * i + j], (px, py, c)))
        for cp in sends:
            cp.start()
        local = _staged_copies(ins, [outs[i].at[me] for i in range(n)], stage, local_sems)
        for i in range(n):
            for j, (px, py) in enumerate(chips):
                landed = outs[i].at[2 * px + py, half(i, c), :]
                _remote(landed, landed, ici_send.at[3 * i + j], ici_recv.at[3 * i + j], (px, py, c)).wait_recv()
                forward = _remote(landed, landed, d2d_send.at[3 * i + j], d2d_recv.at[3 * i + j], (x, y, 1 - c))
                forward.start()
                sends.append(forward)
        for i in range(n):
            for j, (px, py) in enumerate(chips):
                other = outs[i].at[2 * px + py, half(i, 1 - c), :]
                _remote(other, other, d2d_send.at[3 * i + j], d2d_recv.at[3 * i + j], (x, y, 1 - c)).wait_recv()
        for cp in sends:
            cp.wait_send()
        for cp in local:
            cp.wait()

    return pl.pallas_call(
        body, name=name, in_specs=[ANY_SPEC] * n, out_specs=[ANY_SPEC] * n,
        out_shape=[jax.ShapeDtypeStruct((N_CHIPS,) + s.shape, s.dtype) for s in shards],
        scratch_shapes=([pltpu.SemaphoreType.DMA((3 * n,))] * 4 + [pltpu.SemaphoreType.DMA((n,))]
                        + [pltpu.VMEM(s.shape, s.dtype) for s in shards]),
        compiler_params=pltpu.CompilerParams(vmem_limit_bytes=VMEM_LIMIT),
    )(*shards)


HBM_SPEC = pl.BlockSpec(memory_space=pltpu.HBM)
SEM_SPEC = pl.BlockSpec(memory_space=pltpu.SEMAPHORE)
SPLIT_PARAMS = pltpu.CompilerParams(has_side_effects=pltpu.SideEffectType.DATAFLOW_SIDE_EFFECTING)
N_PEERS = {"gather": N_CHIPS - 1, "scatter": N_DEV - 1}


def _split_copies(ins, lands, send_sems, recv_sems, kind):
    x, y, c = _my_place()
    pairs = []
    for i in range(len(ins)):
        if kind == "gather":
            me = 2 * x + y
            for j, (px, py) in enumerate(_other_chips(x, y)):
                sems = (send_sems.at[3 * i + j], recv_sems.at[3 * i + j], (px, py, c))
                pairs.append((_remote(ins[i], lands[i].at[me], *sems),
                              _remote(ins[i], lands[i].at[2 * px + py], *sems)))
        else:
            me = 4 * x + 2 * y + c
            h = ins[i].shape[1] // 2
            for k in range(1, N_DEV):
                px, py, pc = (x + (k >> 2)) % 2, (y + ((k >> 1) & 1)) % 2, (c + (k & 1)) % 2
                src = ins[i].at[2 * px + py, pl.ds(pc * h, h), :]
                sems = (send_sems.at[7 * i + k - 1], recv_sems.at[7 * i + k - 1], (px, py, pc))
                pairs.append((_remote(src, lands[i].at[me], *sems),
                              _remote(src, lands[i].at[4 * px + 2 * py + pc], *sems)))
    return pairs


def _exchange_start(name, srcs, lands, kind, after=None):
    n = len(srcs)
    n_sems = N_PEERS[kind] * n
    extra = [] if after is None else [after]

    def body(*refs):
        ins, land_refs = refs[:n], refs[n:2 * n]
        send_sems, recv_sems = refs[2 * n + len(extra):2 * n + len(extra) + 2]
        token = refs[-1]
        for send, _ in _split_copies(ins, land_refs, send_sems, recv_sems, kind):
            send.start()
        token[...] = jnp.zeros_like(token)

    arrays = list(srcs) + list(lands)
    outs = pl.pallas_call(
        body, name=name,
        in_specs=[HBM_SPEC] * (2 * n) + [ANY_SPEC] * len(extra),
        out_shape=([pltpu.SemaphoreType.DMA((n_sems,))] * 2 + [pltpu.HBM(a.shape, a.dtype) for a in arrays]
                   + [jax.ShapeDtypeStruct((8, 128), F32)]),
        out_specs=[SEM_SPEC] * 2 + [HBM_SPEC] * (2 * n) + [pl.BlockSpec(memory_space=pltpu.VMEM)],
        input_output_aliases={i: 2 + i for i in range(2 * n)},
        compiler_params=SPLIT_PARAMS,
    )(*[pltpu.with_memory_space_constraint(a, pltpu.HBM) for a in arrays], *extra)
    return outs[:2], outs[2:2 + 2 * n], outs[-1]


def _exchange_wait(name, sems, passed, kind, after):
    n = len(passed) // 2

    def body(*refs):
        ins, land_refs = refs[:n], refs[n:2 * n]
        send_sems, recv_sems = refs[2 * n:2 * n + 2]
        for send, arrive in _split_copies(ins, land_refs, send_sems, recv_sems, kind):
            send.wait_send()
            arrive.wait_recv()

    outs = pl.pallas_call(
        body, name=name,
        in_specs=[HBM_SPEC] * (2 * n) + [SEM_SPEC] * 2 + [ANY_SPEC],
        out_shape=[pltpu.HBM(a.shape, a.dtype) for a in passed],
        out_specs=[HBM_SPEC] * (2 * n),
        input_output_aliases={i: i for i in range(2 * n)},
        compiler_params=SPLIT_PARAMS,
    )(*passed, *sems, after)
    return outs[:n], outs[n:]


def _own_slot(name, own, me):
    r, cc = own.shape
    th = min(r, 512)

    def body(me_ref, x_ref, o_ref):
        del me_ref
        o_ref[...] = x_ref[...]

    grid_spec = pltpu.PrefetchScalarGridSpec(
        num_scalar_prefetch=1, grid=(r // th,),
        in_specs=[pl.BlockSpec((th, cc), lambda i, me_ref: (i, 0))],
        out_specs=pl.BlockSpec((None, th, cc), lambda i, me_ref: (me_ref[0], i, 0)))
    return pl.pallas_call(
        body, name=name, grid_spec=grid_spec,
        out_shape=jax.ShapeDtypeStruct((N_CHIPS, r, cc), own.dtype), compiler_params=_params(1),
    )(me, own)


def _sum_devices(name, landed, own, place):
    n_dev, h, cc = landed.shape
    th = min(h, 256)
    nb = h // th

    def body(place_ref, l_ref, own_ref, o_ref):
        total = None
        for d in range(n_dev):
            piece = jnp.where(place_ref[0] == d, own_ref[...], l_ref[d]).astype(F32)
            total = piece if total is None else total + piece
        o_ref[...] = total

    grid_spec = pltpu.PrefetchScalarGridSpec(
        num_scalar_prefetch=1, grid=(nb,),
        in_specs=[pl.BlockSpec((n_dev, th, cc), lambda i, p: (0, i, 0)),
                  pl.BlockSpec((None, th, cc), lambda i, p: (p[1], p[2] * nb + i, 0))],
        out_specs=pl.BlockSpec((th, cc), lambda i, p: (i, 0)))
    return pl.pallas_call(
        body, name=name, grid_spec=grid_spec,
        out_shape=jax.ShapeDtypeStruct((h, cc), F32), compiler_params=_params(1),
    )(place, landed, own)


def _share_halves(name, halves):
    flat = [t for per_weight in halves for t in per_weight]
    n = len(flat)
    n_w = len(halves)

    def body(*refs):
        ins, outs = refs[:n], refs[n:n + n_w]
        send_sems, recv_sems, local_sems = refs[n + n_w:n + n_w + 3]
        stage = refs[n + n_w + 3:]
        x, y, c = _my_place()
        sends, own = [], []
        for i in range(n):
            w, l = divmod(i, DEPTH)
            h = ins[i].shape[0]
            own.append(outs[w].at[l, pl.ds(c * h, h), :])
            sends.append(_remote(ins[i], own[i], send_sems.at[i], recv_sems.at[i], (x, y, 1 - c)))
        for cp in sends:
            cp.start()
        local = _staged_copies(ins, own, stage, local_sems)
        for i in range(n):
            w, l = divmod(i, DEPTH)
            h = ins[i].shape[0]
            _remote(ins[i], outs[w].at[l, pl.ds((1 - c) * h, h), :], send_sems.at[i], recv_sems.at[i],
                    (x, y, 1 - c)).wait_recv()
        for cp in sends:
            cp.wait_send()
        for cp in local:
            cp.wait()

    return pl.pallas_call(
        body, name=name, in_specs=[ANY_SPEC] * n, out_specs=[ANY_SPEC] * n_w,
        out_shape=[jax.ShapeDtypeStruct((DEPTH, 2 * per_weight[0].shape[0], per_weight[0].shape[1]), F32)
                   for per_weight in halves],
        scratch_shapes=([pltpu.SemaphoreType.DMA((n,))] * 3 + [pltpu.VMEM(t.shape, t.dtype) for t in flat]),
        compiler_params=pltpu.CompilerParams(vmem_limit_bytes=VMEM_LIMIT),
    )(*flat)


def _all_reduce_small(pack, after):
    def body(p_ref, after_ref, o_ref, recv, send_sems, recv_sems):
        del after_ref
        x, y, c = _my_place()
        me = 4 * x + 2 * y + c
        recv[me] = p_ref[...]
        peers = []
        for k in range(1, N_DEV):
            px, py, pc = (x + (k >> 2)) % 2, (y + ((k >> 1) & 1)) % 2, (c + (k & 1)) % 2
            peers.append((px, py, pc))
        sends = [_remote(p_ref, recv.at[me], send_sems.at[k], recv_sems.at[k], peer)
                 for k, peer in enumerate(peers)]
        for cp in sends:
            cp.start()
        for k, (px, py, pc) in enumerate(peers):
            _remote(p_ref, recv.at[4 * px + 2 * py + pc], send_sems.at[k], recv_sems.at[k],
                    (px, py, pc)).wait_recv()
        for cp in sends:
            cp.wait_send()
        total = recv[0]
        for d in range(1, N_DEV):
            total = total + recv[d]
        o_ref[...] = total

    vmem = pl.BlockSpec(memory_space=pltpu.VMEM)
    return pl.pallas_call(
        body, name="all_reduce_small", in_specs=[vmem, ANY_SPEC], out_specs=vmem,
        out_shape=jax.ShapeDtypeStruct(pack.shape, F32),
        scratch_shapes=[pltpu.VMEM((N_DEV,) + pack.shape, F32),
                        pltpu.SemaphoreType.DMA((N_DEV - 1,)), pltpu.SemaphoreType.DMA((N_DEV - 1,))],
    )(pack, after)


def _adamw(name, w, g, m, v):
    r, cc = w.shape
    th = min(r, 256)

    def body(w_ref, g_ref, m_ref, v_ref, d_ref, m_out, v_out):
        gv = g_ref[...]
        m2 = ADAM_B1 * m_ref[...] + (1.0 - ADAM_B1) * gv
        v2 = ADAM_B2 * v_ref[...] + (1.0 - ADAM_B2) * (gv * gv)
        m_hat = m2 / (1.0 - ADAM_B1 ** ADAM_STEP)
        v_hat = v2 / (1.0 - ADAM_B2 ** ADAM_STEP)
        d_ref[...] = -ADAM_LR * (m_hat / (jnp.sqrt(v_hat) + ADAM_EPS) + ADAM_WD * w_ref[...])
        m_out[...] = m2
        v_out[...] = v2

    tile = _spec((th, cc), lambda i: (i, 0))
    return pl.pallas_call(
        body, name=name, grid=(r // th,), in_specs=[tile] * 4, out_specs=[tile] * 3,
        out_shape=[jax.ShapeDtypeStruct((r, cc), F32)] * 3, compiler_params=_params(1),
    )(w, g, m, v)


def _lower_bounds(lb_logits):
    p = jax.nn.softmax(lb_logits.astype(F32), axis=0)
    return jnp.cumsum(p, axis=0) - p[0]


def _layer_forward(l, stream, small, weights, consts, next_gain=None, loss=None, after=None):
    win, rest = weights
    cos_t, sin_t, stack, _, _ = consts
    tm = MM_TILE
    x_in, h, h_t = stream
    saved = {"x_in": x_in}

    proj = _mm_pieces(f"proj{l}", h, win, False, tm, after=after)
    saved.update(h_t=h_t, proj=proj)

    qkv = _attn_prep(f"attn_prep{l}", proj, cos_t, sin_t)
    outs, lses = [], []
    for p, d in enumerate(DILATIONS):
        o, lse = _attn_fwd(f"attn_fwd{l}_{d}", *qkv[p], SEQ // d // SPAN)
        outs.append(o)
        lses.append(lse)
    mixed, mixed_t, attn, lse = _attn_merge(f"attn_merge{l}", outs, lses, small["attn_out_gain"][l][None, :])
    saved.update(qkv=qkv, attn=attn, lse=lse)

    lb3 = small["lower"][l].reshape(HGRN_HEADS, 1, HGRN_DIM)
    mixed, mixed_t, o_pre, states, scores = _hgrn_fwd(f"hgrn_fwd{l}", proj, lb3, small["hgrn_out_gain"][l][None, :],
                                                      stack, mixed, mixed_t)
    wo, wu, wd = rest(mixed)
    saved.update(mixed_t=mixed_t, o_pre=o_pre, states=states, scores=scores, lb3=lb3, weights=(win, wo, wu, wd))

    x_mid, h2, h2_t = _mm_accum(f"out_proj{l}", mixed, wo, False, tm, x_in, next_gain=small["norm_mlp"][l][None, :])
    saved["x_mid"] = x_mid

    a, relu_u, a_t = _mm_pieces(f"up{l}", h2, wu, False, tm, epilogue="relu2")
    new_stream = tuple(_mm_accum(f"down{l}", a, wd, False, tm, x_mid, next_gain=next_gain, loss=loss))
    saved.update(h2_t=h2_t, relu_u=relu_u, a_t=a_t)
    return new_stream, saved


def _layer_backward(l, dx, saved, small, consts, on_grads, after=None):
    win, wo, wu, wd = saved["weights"]
    cos_t, sin_t, stack, stack_t, head_sum = consts
    tm = MM_TILE

    dx, dx_b = dx
    du = _mm_pieces(f"d_u{l}", dx_b, wd, True, tm, epilogue="relu2_grad", extra=saved["relu_u"], after=after)
    d_wd = _mm_dw(f"d_wdown{l}", saved["a_t"], dx_b, False, tm)
    dxm, dxm_b, dg_mlp = _mm_accum(f"d_h2_{l}", du, wu, True, tm, dx,
                                   norm=(saved["x_mid"], small["norm_mlp"][l][None, :]))
    d_wu = _mm_dw(f"d_wup{l}", saved["h2_t"], du, True, tm)
    after_mlp = on_grads(l, "mlp", (d_wu, d_wd))

    d_mixed = _mm_pieces(f"d_mixed{l}", dxm_b, wo, True, tm, after=after_mlp)
    d_wo = _mm_dw(f"d_wout{l}", saved["mixed_t"], dxm_b, False, tm)
    d_rec = d_mixed

    d_out, delta, lses, dg_attn = _attn_bwd_prep(f"attn_bwd_prep{l}", d_mixed, saved["attn"], saved["lse"],
                                                 small["attn_out_gain"][l][None, :], head_sum)
    grads = []
    for p, d in enumerate(DILATIONS):
        grads.append(_attn_bwd(f"attn_bwd{l}_{d}", *saved["qkv"][p], d_out[p], delta[p], lses[p],
                               SEQ // d // SPAN))
    dp_attn = _attn_bwd_post(f"attn_bwd_post{l}", grads, cos_t, sin_t)

    dq_h, df_h, di_h, dg_h, d_lower, dg_hgrn = _hgrn_bwd(
        f"hgrn_bwd{l}", saved["proj"], d_rec, saved["o_pre"], saved["states"], saved["scores"],
        saved["lb3"], small["hgrn_out_gain"][l][None, :], stack, stack_t)
    dproj = [dp_attn, dq_h, df_h, di_h, dg_h]

    d_win = _mm_dw(f"d_win{l}", saved["h_t"], dproj, True, tm)
    after_mix = on_grads(l, "mix", (d_win, d_wo))
    dx_in, dx_in_b, dg_mix = _mm_accum(f"d_h{l}", dproj, win, True, tm, dxm,
                                       norm=(saved["x_in"], small["norm_mix"][l][None, :]), after=after_mix)

    small_grads = {"norm_mix": dg_mix[0], "attn_out_gain": dg_attn[0],
                   "lower": d_lower.reshape(HGRN_WIDTH),
                   "hgrn_out_gain": jnp.sum(dg_hgrn, axis=0).reshape(HGRN_DIM), "norm_mlp": dg_mlp[0]}
    return (dx_in, dx_in_b), after_mix, small_grads


def _local_step(xs, target, small, get_weights, on_grads):
    consts = _rope_tables() + _hgrn_consts() + (_head_sum_matrix(),)
    stream = (xs,) + tuple(_rms_fwd("norm_mix0", xs, small["norm_mix"][0][None, :]))
    saved = []
    for l in range(DEPTH):
        w, after = get_weights(l, stream[0])
        if l + 1 < DEPTH:
            stream, s = _layer_forward(l, stream, small, w, consts, next_gain=small["norm_mix"][l + 1][None, :],
                                       after=after)
        else:
            stream, s = _layer_forward(l, stream, small, w, consts, loss=(small["norm_final"][None, :], target),
                                       after=after)
        saved.append(s)
    dx_f, dx_b, dg_final, loss = stream
    dx = (dx_f, dx_b)
    small_grads = [None] * DEPTH
    after = None
    for l in reversed(range(DEPTH)):
        dx, after, small_grads[l] = _layer_backward(l, dx, saved[l], small, consts, on_grads, after=after)
    return loss, dx[0], dg_final[0], small_grads


def _pack_small(norm_mix, attn_out_gain, lb, hgrn_out_gain, norm_mlp, norm_final, last_row):
    rows = [norm_mix, attn_out_gain.reshape(1, D_MODEL), lb.reshape(1, D_MODEL),
            jnp.pad(hgrn_out_gain.reshape(1, DEPTH * HGRN_DIM), ((0, 0), (0, D_MODEL - DEPTH * HGRN_DIM))),
            norm_mlp, norm_final.reshape(1, D_MODEL), last_row.reshape(1, D_MODEL)]
    pack = jnp.concatenate(rows, axis=0)
    return jnp.pad(pack, ((0, PACK_ROWS - pack.shape[0]), (0, 0)))


def _unpack_small(pack):
    return (pack[0:2], pack[2].reshape(DEPTH, ATTN_WIDTH), pack[3].reshape(DEPTH, HGRN_WIDTH),
            pack[4, :DEPTH * HGRN_DIM].reshape(DEPTH, HGRN_DIM), pack[5:7], pack[7], pack[8])


def kernel(x, norm_mix, w_in, attn_out_gain, hgrn_lb_logits, hgrn_out_gain, w_out, norm_mlp, w_up, w_down, norm_final, loss_target, m_norm_mix, m_w_in, m_attn_out_gain, m_hgrn_lb_logits, m_hgrn_out_gain, m_w_out, m_norm_mlp, m_w_up, m_w_down, m_norm_final, v_norm_mix, v_w_in, v_attn_out_gain, v_hgrn_lb_logits, v_hgrn_out_gain, v_w_out, v_norm_mlp, v_w_up, v_w_down, v_norm_final):
    lower, lower_vjp = jax.vjp(_lower_bounds, hgrn_lb_logits)
    small = {"norm_mix": norm_mix, "attn_out_gain": attn_out_gain, "lower": lower,
             "hgrn_out_gain": hgrn_out_gain, "norm_mlp": norm_mlp, "norm_final": norm_final}
    big_w = (w_in, w_out, w_up, w_down)

    x_pos, y_pos, core = lax.axis_index("x"), lax.axis_index("y"), lax.axis_index("c")
    me = (2 * x_pos + y_pos).astype(jnp.int32).reshape(1)
    place = jnp.stack([4 * x_pos + 2 * y_pos + core, 2 * x_pos + y_pos, core]).astype(jnp.int32)
    shards = [[w[l].astype(BF16) for w in big_w] for l in range(DEPTH)]
    in_flight = {}

    def start_gather(name, some, after):
        lands = [_own_slot(f"own_{name}_{i}", s, me) for i, s in enumerate(some)]
        sems, passed, token = _exchange_start(f"start_{name}", some, lands, "gather", after)
        in_flight[name] = (sems, passed)
        return token

    def finish_gather(name, after):
        return _exchange_wait(f"wait_{name}", *in_flight.pop(name), "gather", after)[1]

    def get_weights(l, stream):
        if l == 0:
            (win,) = _gather_weights("gather_w_in0", shards[0][:1])
            token = start_gather("gather_rest0", shards[0][1:], win)
            token = start_gather("gather_w_in1", shards[1][:1], token)
            token = start_gather("gather_rest1", shards[1][1:], token)
            return (win, lambda after: finish_gather("gather_rest0", after)), token
        (win,) = finish_gather("gather_w_in1", stream)
        return (win, lambda after: finish_gather("gather_rest1", after)), None

    reduced = {}

    def start_exchange(name, grads):
        srcs, lands = [g for g, _ in grads], [land for _, land in grads]
        sems, passed, token = _exchange_start(f"start_{name}", srcs, lands, "scatter")
        in_flight[name] = (sems, passed)
        return token

    def finish_exchange(name, after):
        own, landed = _exchange_wait(f"wait_{name}", *in_flight.pop(name), "scatter", after)
        return [_sum_devices(f"sum_{name}_{i}", p, g, place) for i, (p, g) in enumerate(zip(landed, own))]

    def on_grads(l, group, grads):
        if (l, group) == (1, "mlp"):
            return start_exchange("mlp1", grads)
        if (l, group) == (1, "mix"):
            return start_exchange("mix1", grads)
        if (l, group) == (0, "mlp"):
            token = start_exchange("mlp0", grads)
            reduced[(1, "mlp")] = finish_exchange("mlp1", token)
            reduced[(1, "mix")] = finish_exchange("mix1", token)
            return token
        token = start_exchange("mix0", grads)
        reduced[(0, "mlp")] = finish_exchange("mlp0", token)
        return token

    loss, dx, dg_final, sg = _local_step(x[0], loss_target[0], small, get_weights, on_grads)

    big_m = (m_w_in, m_w_out, m_w_up, m_w_down)
    big_v = (v_w_in, v_w_out, v_w_up, v_w_down)
    names = ("w_in", "w_out", "w_up", "w_down")
    big_g, big_delta, big_new_m, big_new_v = [None] * 4, [None] * 4, [None] * 4, [None] * 4

    def finish_weights(group, which):
        whole = _share_halves(f"share_{group}", [[reduced[(l, group)][i] for l in range(DEPTH)] for i in range(2)])
        for i, w in enumerate(which):
            shape = big_w[w].shape
            flat = lambda arr: arr.reshape(shape[0] * shape[1], shape[2])
            d, m2, v2 = _adamw(f"adamw_{names[w]}", flat(big_w[w]), flat(whole[i]), flat(big_m[w]), flat(big_v[w]))
            big_g[w], big_delta[w] = whole[i], d.reshape(shape)
            big_new_m[w], big_new_v[w] = m2.reshape(shape), v2.reshape(shape)

    finish_weights("mlp", (2, 3))
    reduced[(0, "mix")] = finish_exchange("mix0", big_delta[3])
    finish_weights("mix", (0, 1))

    stack2 = lambda key: jnp.stack([sg[l][key] for l in range(DEPTH)])
    pack = _pack_small(stack2("norm_mix"), stack2("attn_out_gain"), stack2("lower"), stack2("hgrn_out_gain"),
                       stack2("norm_mlp"), dg_final, jnp.broadcast_to(loss[0, 0], (D_MODEL,)))
    g_mix, g_attn, g_lower, g_hgrn, g_mlp, g_final, loss_row = _unpack_small(_all_reduce_small(pack, big_delta[0]))
    (g_logits,) = lower_vjp(g_lower)

    zeros_row = jnp.zeros((D_MODEL,), F32)
    small_w = (norm_mix, attn_out_gain, hgrn_lb_logits, hgrn_out_gain, norm_mlp, norm_final)
    small_m = (m_norm_mix, m_attn_out_gain, m_hgrn_lb_logits, m_hgrn_out_gain, m_norm_mlp, m_norm_final)
    small_v = (v_norm_mix, v_attn_out_gain, v_hgrn_lb_logits, v_hgrn_out_gain, v_norm_mlp, v_norm_final)
    small_g = (g_mix, g_attn, g_logits, g_hgrn, g_mlp, g_final)
    packs = [_pack_small(*t, zeros_row) for t in (small_w, small_g, small_m, small_v)]
    small_delta, small_new_m, small_new_v = [_unpack_small(p)[:6] for p in _adamw("adamw_small", *packs)]

    def ordered(small6, big4):
        mix, attn, lbl, hg, mlp, fin = small6
        return (mix, big4[0], attn, lbl, hg, big4[1], mlp, big4[2], big4[3], fin)

    return ((loss_row[0], dx[None]) + ordered(small_g, big_g) + ordered(small_delta, big_delta)
            + ordered(small_new_m, big_new_m) + ordered(small_new_v, big_new_v))
```

```python
import numpy as np
import jax
import jax.numpy as jnp
from jax import lax
from jax.experimental import pallas as pl
from jax.experimental.pallas import tpu as pltpu

F32 = jnp.float32
BF16 = jnp.bfloat16
MESH = pl.DeviceIdType.MESH

SEQ = 4096
D_MODEL = 1024
DEPTH = 2
ATTN_WIDTH = 512
HEAD_DIM = 64
HGRN_HEADS = 4
HGRN_DIM = 128
HGRN_WIDTH = 512
IN_W = 3584
MLP_HIDDEN = 4096
N_CHIPS = 4
N_DEV = 8
DILATIONS = (1, 4, 16)
SPAN = 128
ROPE_THETA = 10000.0
NORM_EPS = 1e-6
MASK_VALUE = -1e30
CHUNK = 128
ROW_TILE = 512
MM_TILE = 512
VMEM_LIMIT = 52 * 1024 * 1024

ADAM_LR = 0.001
ADAM_B1 = 0.9
ADAM_B2 = 0.999
ADAM_EPS = 1e-08
ADAM_WD = 0.01
ADAM_STEP = 10

PACK_ROWS = 16


def _params(n_axes):
    return pltpu.CompilerParams(dimension_semantics=("arbitrary",) * n_axes,
                                vmem_limit_bytes=VMEM_LIMIT)


def _dot(a, b):
    return jnp.dot(a.astype(BF16), b.astype(BF16), preferred_element_type=F32)


def _dot_nt(a, b):
    return lax.dot_general(a.astype(BF16), b.astype(BF16), (((1,), (1,)), ((), ())),
                           preferred_element_type=F32)


def _dot_tn(a, b):
    return lax.dot_general(a.astype(BF16), b.astype(BF16), (((0,), (0,)), ((), ())),
                           preferred_element_type=F32)


def _sigmoid(x):
    return 1.0 / (1.0 + jnp.exp(-x))


def _spec(shape, index_map):
    return pl.BlockSpec(shape, index_map)


def _mm_pieces(name, a, w, nt, tm, epilogue="none", extra=None, after=None):
    s = a.shape[0]
    pw = w.shape[1] if nt else w.shape[2]
    width = N_CHIPS * pw

    def body(a_ref, w_ref, *rest):
        e_ref = rest[0] if extra is not None else None
        outs = rest[-3:] if epilogue == "relu2" else rest[-1:]
        av = a_ref[...].astype(BF16)
        for j in range(N_CHIPS):
            cols = slice(j * pw, (j + 1) * pw)
            r = _dot_nt(av, w_ref[j]) if nt else _dot(av, w_ref[j])
            if epilogue == "relu2":
                relu = jnp.maximum(r, 0.0)
                r = relu * relu
                outs[1][:, cols] = relu.astype(BF16)
                outs[2][cols, :] = r.T.astype(BF16)
            elif epilogue == "relu2_grad":
                r = r * (2.0 * e_ref[:, cols].astype(F32))
            outs[0][:, cols] = r.astype(outs[0].dtype)

    row = lambda width_: _spec((tm, width_), lambda i: (i, 0))
    in_specs = [row(a.shape[1]), _spec(w.shape, lambda i: (0, 0, 0))]
    args = [a, w]
    if extra is not None:
        in_specs.append(row(width))
        args.append(extra)
    if after is not None:
        in_specs.append(pl.BlockSpec(memory_space=pl.ANY))
        args.append(after)
    if epilogue == "relu2":
        out_specs = [row(width), row(width), _spec((width, tm), lambda i: (0, i))]
        out_shape = [jax.ShapeDtypeStruct((s, width), BF16)] * 2 + [jax.ShapeDtypeStruct((width, s), BF16)]
    else:
        out_specs = row(width)
        out_shape = jax.ShapeDtypeStruct((s, width), BF16 if epilogue == "relu2_grad" else F32)
    return pl.pallas_call(body, name=name, grid=(s // tm,), in_specs=in_specs, out_specs=out_specs,
                          out_shape=out_shape, compiler_params=_params(1))(*args)


def _mm_accum(name, a, w, nt, tm, resid, norm=None, after=None, next_gain=None, loss=None):
    pieces = list(a) if isinstance(a, (list, tuple)) else [a]
    n_a = len(pieces)
    s = pieces[0].shape[0]
    pk = w.shape[2] if nt else w.shape[1]
    d = w.shape[1] if nt else w.shape[2]

    def body(*refs):
        a_refs, w_ref, resid_ref, rest = refs[:n_a], refs[n_a], refs[n_a + 1], refs[n_a + 2:]
        av = a_refs[0][...] if n_a == 1 else jnp.concatenate([ref[...] for ref in a_refs], axis=1)
        r = None
        for j in range(N_CHIPS):
            piece = av[:, j * pk:(j + 1) * pk].astype(BF16)
            term = _dot_nt(piece, w_ref[j]) if nt else _dot(piece, w_ref[j])
            r = term if r is None else r + term
        if loss is not None:
            g_ref, t_ref = rest[:2]
            dx_ref, dxb_ref, dg_ref, loss_ref, acc = rest[-5:]
            i = pl.program_id(0)

            @pl.when(i == 0)
            def _():
                dg_ref[...] = jnp.zeros_like(dg_ref)
                acc[...] = jnp.zeros_like(acc)

            xv = r + resid_ref[...]
            g = g_ref[...]
            rs = lax.rsqrt(jnp.mean(xv * xv, axis=1, keepdims=True) + NORM_EPS)
            xhat = xv * rs
            err = xhat * g - t_ref[...]
            acc[...] += jnp.sum(err * err, axis=0, keepdims=True)
            dy = err * (1.0 / d)
            dyg = dy * g
            dx = rs * (dyg - xhat * jnp.mean(dyg * xhat, axis=1, keepdims=True))
            dx_ref[...] = dx
            dxb_ref[...] = dx.astype(BF16)
            dg_ref[...] += jnp.sum(dy * xhat, axis=0, keepdims=True)

            @pl.when(i == s // tm - 1)
            def _():
                total = jnp.sum(acc[...], axis=1, keepdims=True) * (0.5 / d)
                loss_ref[...] = jnp.broadcast_to(total, loss_ref.shape)

            return
        if norm is None and next_gain is None:
            rest[-1][...] = r + resid_ref[...]
            return
        if norm is None:
            g_ref = rest[0]
            x_out, h_out, ht_out = rest[-3:]
            xv = r + resid_ref[...]
            x_out[...] = xv
            h = xv * lax.rsqrt(jnp.mean(xv * xv, axis=1, keepdims=True) + NORM_EPS) * g_ref[...]
            h_out[...] = h.astype(BF16)
            ht_out[...] = h.T.astype(BF16)
            return
        x_ref, g_ref = rest[:2]
        dx_ref, dxb_ref, dg_ref = rest[-3:]

        @pl.when(pl.program_id(0) == 0)
        def _():
            dg_ref[...] = jnp.zeros_like(dg_ref)

        xv = x_ref[...]
        rs = lax.rsqrt(jnp.mean(xv * xv, axis=1, keepdims=True) + NORM_EPS)
        xhat = xv * rs
        rg = r * g_ref[...]
        dx = resid_ref[...] + rs * (rg - xhat * jnp.mean(rg * xhat, axis=1, keepdims=True))
        dx_ref[...] = dx
        dxb_ref[...] = dx.astype(BF16)
        dg_ref[...] += jnp.sum(r * xhat, axis=0, keepdims=True)

    row = lambda width: _spec((tm, width), lambda i: (i, 0))
    in_specs = [row(p.shape[1]) for p in pieces] + [_spec(w.shape, lambda i: (0, 0, 0)), row(d)]
    args = pieces + [w, resid]
    scratch = []
    if loss is not None:
        in_specs += [_spec((1, d), lambda i: (0, 0)), row(d)]
        args += list(loss)
        out_specs = [row(d), row(d), _spec((1, d), lambda i: (0, 0)), _spec((1, 128), lambda i: (0, 0))]
        out_shape = [jax.ShapeDtypeStruct((s, d), F32), jax.ShapeDtypeStruct((s, d), BF16),
                     jax.ShapeDtypeStruct((1, d), F32), jax.ShapeDtypeStruct((1, 128), F32)]
        scratch = [pltpu.VMEM((1, d), F32)]
    elif norm is None and next_gain is None:
        out_specs, out_shape = row(d), jax.ShapeDtypeStruct((s, d), F32)
    elif norm is None:
        in_specs.append(_spec((1, d), lambda i: (0, 0)))
        args.append(next_gain)
        out_specs = [row(d), row(d), _spec((d, tm), lambda i: (0, i))]
        out_shape = [jax.ShapeDtypeStruct((s, d), F32), jax.ShapeDtypeStruct((s, d), BF16),
                     jax.ShapeDtypeStruct((d, s), BF16)]
    else:
        in_specs += [row(d), _spec((1, d), lambda i: (0, 0))]
        args += list(norm)
        out_specs = [row(d), row(d), _spec((1, d), lambda i: (0, 0))]
        out_shape = [jax.ShapeDtypeStruct((s, d), F32), jax.ShapeDtypeStruct((s, d), BF16),
                     jax.ShapeDtypeStruct((1, d), F32)]
    if after is not None:
        in_specs.append(pl.BlockSpec(memory_space=pl.ANY))
        args.append(after)
    return pl.pallas_call(body, name=name, grid=(s // tm,), in_specs=in_specs, out_specs=out_specs,
                          out_shape=out_shape, scratch_shapes=scratch, compiler_params=_params(1))(*args)


def _mm_dw(name, a_t, b, by_cols, tk):
    pieces = list(b) if isinstance(b, (list, tuple)) else [b]
    n_b = len(pieces)
    m, s = a_t.shape
    n = sum(p.shape[1] for p in pieces)
    shape = (N_CHIPS, m, n // N_CHIPS) if by_cols else (N_CHIPS, m // N_CHIPS, n)
    n_steps = s // tk

    def body(a_ref, *rest):
        b_refs, o_ref, acc = rest[:n_b], rest[n_b], rest[-1]

        @pl.when(pl.program_id(0) == 0)
        def _():
            acc[...] = jnp.zeros_like(acc)

        bv = b_refs[0][...] if n_b == 1 else jnp.concatenate([ref[...] for ref in b_refs], axis=1)
        for j in range(N_CHIPS):
            if by_cols:
                acc[j] += _dot(a_ref[...], bv[:, j * shape[2]:(j + 1) * shape[2]])
            else:
                acc[j] += _dot(a_ref[j * shape[1]:(j + 1) * shape[1], :], bv)

        @pl.when(pl.program_id(0) == n_steps - 1)
        def _():
            o_ref[...] = acc[...].astype(BF16)

    return pl.pallas_call(
        body, name=name, grid=(n_steps,),
        in_specs=[_spec((m, tk), lambda k: (0, k))] + [_spec((tk, p.shape[1]), lambda k: (k, 0)) for p in pieces],
        out_specs=[_spec(shape, lambda k: (0, 0, 0)), ANY_SPEC],
        out_shape=[jax.ShapeDtypeStruct(shape, BF16),
                   jax.ShapeDtypeStruct((N_DEV, shape[1] // 2, shape[2]), BF16)],
        scratch_shapes=[pltpu.VMEM(shape, F32)],
        compiler_params=_params(1))(a_t, *pieces)


def _rms_fwd(name, x, gain):
    s, d = x.shape
    t = ROW_TILE

    def body(x_ref, g_ref, h_ref, ht_ref):
        xv = x_ref[...]
        r = lax.rsqrt(jnp.mean(xv * xv, axis=1, keepdims=True) + NORM_EPS)
        h = xv * r * g_ref[...]
        h_ref[...] = h.astype(BF16)
        ht_ref[...] = h.T.astype(BF16)

    return pl.pallas_call(
        body, name=name, grid=(s // t,),
        in_specs=[_spec((t, d), lambda i: (i, 0)), _spec((1, d), lambda i: (0, 0))],
        out_specs=[_spec((t, d), lambda i: (i, 0)), _spec((d, t), lambda i: (0, i))],
        out_shape=[jax.ShapeDtypeStruct((s, d), BF16), jax.ShapeDtypeStruct((d, s), BF16)],
        compiler_params=_params(1),
    )(x, gain)


def _rope_tables():
    half = HEAD_DIM // 2
    inv_freq = ROPE_THETA ** (-jnp.arange(half, dtype=F32) / half)
    ang = jnp.arange(SEQ, dtype=jnp.int32).astype(F32)[:, None] * inv_freq[None, :]
    cos, sin = jnp.cos(ang), jnp.sin(ang)
    cos_t = jnp.concatenate([cos, cos, cos, cos], axis=1)
    sin_t = jnp.concatenate([-sin, sin, -sin, sin], axis=1)
    return cos_t, sin_t


def _swap_halves(x):
    lane = lax.broadcasted_iota(jnp.int32, x.shape, 1)
    first = (lane % HEAD_DIM) < (HEAD_DIM // 2)
    return jnp.where(first, pltpu.roll(x, 128 - HEAD_DIM // 2, 1), pltpu.roll(x, HEAD_DIM // 2, 1))


def _permuted_specs(t, width):
    specs = [_spec((t, width), lambda i: (i, 0))]
    for d in DILATIONS[1:]:
        specs.append(_spec((d, t // d, width), lambda i: (0, i, 0)))
    return specs


def _permuted_shapes(width, dtype):
    shapes = [jax.ShapeDtypeStruct((SEQ, width), dtype)]
    for d in DILATIONS[1:]:
        shapes.append(jax.ShapeDtypeStruct((d, SEQ // d, width), dtype))
    return shapes


def _attn_prep(name, proj, cos_t, sin_t):
    t = ROW_TILE
    w = ATTN_WIDTH

    def body(q_ref, k_ref, v_ref, cos_ref, sin_ref, *rest):
        outs, scr = rest[:9], rest[9]
        cosv, sinv = cos_ref[...], sin_ref[...]
        for a, (src, roped, scale) in enumerate(((q_ref, True, HEAD_DIM ** -0.5),
                                                 (k_ref, True, 1.0), (v_ref, False, 1.0))):
            o1, o4, o16 = outs[3 * a:3 * a + 3]
            for cb in range(w // 128):
                cols = slice(cb * 128, (cb + 1) * 128)
                val = src[:, cols]
                if roped:
                    val = (val * cosv + _swap_halves(val) * sinv) * scale
                scr[...] = val
                o1[:, cols] = val.astype(BF16)
                for o_ref, d in ((o4, 4), (o16, 16)):
                    for r in range(d):
                        o_ref[r, :, cols] = scr[pl.ds(r, t // d, stride=d), :].astype(BF16)

    out_specs = _permuted_specs(t, w) * 3
    out_shape = _permuted_shapes(w, BF16) * 3
    outs = pl.pallas_call(
        body, name=name, grid=(SEQ // t,),
        in_specs=[_spec((t, w), lambda i: (i, 0)), _spec((t, w), lambda i: (i, 1)),
                  _spec((t, w), lambda i: (i, 2)),
                  _spec((t, 128), lambda i: (i, 0)), _spec((t, 128), lambda i: (i, 0))],
        out_specs=out_specs, out_shape=out_shape,
        scratch_shapes=[pltpu.VMEM((t, 128), F32)],
        compiler_params=_params(1),
    )(proj, proj, proj, cos_t, sin_t)
    q, k, v = outs[0:3], outs[3:6], outs[6:9]
    flat = lambda arr: arr.reshape(SEQ, w)
    return [(flat(q[p]), flat(k[p]), flat(v[p])) for p in range(3)]


def _band_masks():
    row = lax.broadcasted_iota(jnp.int32, (2 * SPAN, 2 * SPAN), 0) % SPAN
    col = lax.broadcasted_iota(jnp.int32, (2 * SPAN, 2 * SPAN), 1)
    is_prev = col < SPAN
    band = (is_prev & (col >= row)) | (~is_prev & (col - SPAN <= row))
    head0 = lax.broadcasted_iota(jnp.int32, (SPAN, 128), 1) < HEAD_DIM
    return band, is_prev, head0


def _stack_heads(x, head0):
    zero = jnp.zeros_like(x)
    return jnp.concatenate([jnp.where(head0, x, zero), jnp.where(head0, zero, x)], axis=0)


ATTN_UNROLL = 4


def _for_each_block(block, seg_blocks):
    def trip(i, carry):
        for u in range(ATTN_UNROLL):
            static = seg_blocks <= ATTN_UNROLL
            block(i * ATTN_UNROLL + u, (u % seg_blocks == 0) if static else None)
        return carry

    lax.fori_loop(0, SEQ // SPAN // ATTN_UNROLL, trip, 0)


def _attn_fwd(name, q, k, v, seg_blocks):
    def body(q_ref, k_ref, v_ref, o_ref, lse_ref):
        band, is_prev, head0 = _band_masks()

        def block(b, first):
            cur = pl.ds(pl.multiple_of(b * SPAN, SPAN), SPAN)
            qs = _stack_heads(q_ref[cur, :], head0)
            if first is True:
                kcat, vcat, ok = k_ref[cur, :], v_ref[cur, :], band[:, SPAN:]
            else:
                prev = pl.ds(pl.multiple_of(jnp.maximum(b - 1, 0) * SPAN, SPAN), SPAN)
                kcat = jnp.concatenate([k_ref[prev, :], k_ref[cur, :]], axis=0)
                vcat = jnp.concatenate([v_ref[prev, :], v_ref[cur, :]], axis=0)
                ok = band if first is False else band & (((b % seg_blocks) != 0) | ~is_prev)
            s = jnp.where(ok, _dot_nt(qs, kcat), MASK_VALUE)
            m = jnp.max(s, axis=1, keepdims=True)
            p = jnp.exp(s - m)
            l = jnp.sum(p, axis=1, keepdims=True)
            pv = _dot(p, vcat) * (1.0 / l)
            lse = m + jnp.log(l)
            o_ref[cur, :] = jnp.where(head0, pv[:SPAN], pv[SPAN:])
            lse_ref[cur, :] = jnp.where(head0, lse[:SPAN], lse[SPAN:])

        _for_each_block(block, seg_blocks)

    col = _spec((SEQ, 128), lambda j: (0, j))
    return pl.pallas_call(
        body, name=name, grid=(ATTN_WIDTH // 128,),
        in_specs=[col, col, col], out_specs=[col, col],
        out_shape=[jax.ShapeDtypeStruct((SEQ, ATTN_WIDTH), F32)] * 2,
        compiler_params=_params(1),
    )(q, k, v)


def _unpermute(dst, src_ref, d, cols):
    n = dst.shape[0] // d
    for r in range(d):
        dst[pl.ds(r, n, stride=d), :] = src_ref[r, :, cols].astype(dst.dtype)


def _attn_merge(name, outs, lses, gain):
    t = ROW_TILE
    w = ATTN_WIDTH

    def body(o1, o4, o16, l1, l4, l16, g_ref, an_ref, ant_ref, attn_ref, lse_ref, so4, so16, sl4, sl16):
        for cb in range(w // 128):
            cols = slice(cb * 128, (cb + 1) * 128)
            _unpermute(so4, o4, 4, cols)
            _unpermute(so16, o16, 16, cols)
            _unpermute(sl4, l4, 4, cols)
            _unpermute(sl16, l16, 16, cols)
            la, lb, lc = l1[:, cols], sl4[...], sl16[...]
            m = jnp.maximum(jnp.maximum(la, lb), lc)
            ea, eb, ec = jnp.exp(la - m), jnp.exp(lb - m), jnp.exp(lc - m)
            tot = ea + eb + ec
            attn_ref[:, cols] = (ea * o1[:, cols] + eb * so4[...] + ec * so16[...]) / tot
            lse_ref[:, cols] = m + jnp.log(tot)
        attn = attn_ref[...]
        r = lax.rsqrt(jnp.mean(attn * attn, axis=1, keepdims=True) + NORM_EPS)
        an = attn * r * g_ref[...]
        an_ref[...] = an.astype(BF16)
        ant_ref[...] = an.T.astype(BF16)

    views = lambda arrs: [arrs[0], arrs[1].reshape(4, SEQ // 4, w), arrs[2].reshape(16, SEQ // 16, w)]
    row = _spec((t, w), lambda i: (i, 0))
    return pl.pallas_call(
        body, name=name, grid=(SEQ // t,),
        in_specs=_permuted_specs(t, w) * 2 + [_spec((1, w), lambda i: (0, 0))],
        out_specs=[row, _spec((w, t), lambda i: (0, i)), row, row],
        out_shape=[jax.ShapeDtypeStruct((SEQ, 2 * w), BF16), jax.ShapeDtypeStruct((2 * w, SEQ), BF16),
                   jax.ShapeDtypeStruct((SEQ, w), F32), jax.ShapeDtypeStruct((SEQ, w), F32)],
        scratch_shapes=[pltpu.VMEM((t, 128), F32)] * 4,
        compiler_params=_params(1),
    )(*views(outs), *views(lses), gain)


def _head_sum_matrix():
    i = np.arange(ATTN_WIDTH)
    return jnp.asarray((i[:, None] // HEAD_DIM) == (i[None, :] // HEAD_DIM), dtype=F32)


def _attn_bwd_prep(name, d_an, attn, lse, gain, head_sum):
    t = ROW_TILE
    w = ATTN_WIDTH

    def body(dan_ref, attn_ref, lse_ref, g_ref, hs_ref, *rest):
        (do1, do4, do16, dl1, dl4, dl16, ls4, ls16, dg_ref), (sdo, sdl, sls) = rest[:9], rest[9:]

        @pl.when(pl.program_id(0) == 0)
        def _():
            dg_ref[...] = jnp.zeros_like(dg_ref)

        attn = attn_ref[...]
        dan = dan_ref[...]
        r = lax.rsqrt(jnp.mean(attn * attn, axis=1, keepdims=True) + NORM_EPS)
        xhat = attn * r
        dg_ref[...] += jnp.sum(dan * xhat, axis=0, keepdims=True)
        dang = dan * g_ref[...]
        d_o = r * (dang - xhat * jnp.mean(dang * xhat, axis=1, keepdims=True))
        delta = jnp.dot(d_o * attn, hs_ref[...], preferred_element_type=F32,
                        precision=lax.Precision.HIGHEST)
        do1[...] = d_o.astype(BF16)
        dl1[...] = delta
        for cb in range(w // 128):
            cols = slice(cb * 128, (cb + 1) * 128)
            sdo[...] = d_o[:, cols]
            sdl[...] = delta[:, cols]
            sls[...] = lse_ref[:, cols]
            for d, o_do, o_dl, o_ls in ((4, do4, dl4, ls4), (16, do16, dl16, ls16)):
                for rr in range(d):
                    rows = pl.ds(rr, t // d, stride=d)
                    o_do[rr, :, cols] = sdo[rows, :].astype(BF16)
                    o_dl[rr, :, cols] = sdl[rows, :]
                    o_ls[rr, :, cols] = sls[rows, :]

    row = _spec((t, w), lambda i: (i, 0))
    perm = _permuted_specs(t, w)
    outs = pl.pallas_call(
        body, name=name, grid=(SEQ // t,),
        in_specs=[row, row, row, _spec((1, w), lambda i: (0, 0)), _spec((w, w), lambda i: (0, 0))],
        out_specs=perm + perm + perm[1:] + [_spec((1, w), lambda i: (0, 0))],
        out_shape=(_permuted_shapes(w, BF16) + _permuted_shapes(w, F32) + _permuted_shapes(w, F32)[1:]
                   + [jax.ShapeDtypeStruct((1, w), F32)]),
        scratch_shapes=[pltpu.VMEM((t, 128), F32)] * 3,
        compiler_params=_params(1),
    )(d_an, attn, lse, gain, head_sum)
    flat = lambda arr: arr.reshape(SEQ, w)
    d_out = [flat(a) for a in outs[0:3]]
    delta = [flat(a) for a in outs[3:6]]
    lses = [lse, flat(outs[6]), flat(outs[7])]
    return d_out, delta, lses, outs[8]


def _attn_bwd(name, q, k, v, d_out, delta, lse, seg_blocks):
    def body(q_ref, k_ref, v_ref, do_ref, dl_ref, lse_ref, dq_ref, dk_out, dv_out, dk_ref, dv_ref):
        band, is_prev, head0 = _band_masks()
        dk_ref[...] = jnp.zeros_like(dk_ref)
        dv_ref[...] = jnp.zeros_like(dv_ref)

        def per_head(x):
            return jnp.concatenate([x[:, 0:1], x[:, HEAD_DIM:HEAD_DIM + 1]], axis=0)

        def block(b, first):
            cur = pl.ds(pl.multiple_of(b * SPAN, SPAN), SPAN)
            qs = _stack_heads(q_ref[cur, :], head0)
            dos = _stack_heads(do_ref[cur, :], head0)
            if first is True:
                kcat, vcat, ok = k_ref[cur, :], v_ref[cur, :], band[:, SPAN:]
            else:
                prev = pl.ds(pl.multiple_of(jnp.maximum(b - 1, 0) * SPAN, SPAN), SPAN)
                kcat = jnp.concatenate([k_ref[prev, :], k_ref[cur, :]], axis=0)
                vcat = jnp.concatenate([v_ref[prev, :], v_ref[cur, :]], axis=0)
                ok = band if first is False else band & (((b % seg_blocks) != 0) | ~is_prev)
            p = jnp.where(ok, jnp.exp(_dot_nt(qs, kcat) - per_head(lse_ref[cur, :])), 0.0)
            ds = p * (_dot_nt(dos, vcat) - per_head(dl_ref[cur, :]))
            dq = _dot(ds, kcat)
            dq_ref[cur, :] = jnp.where(head0, dq[:SPAN], dq[SPAN:]).astype(BF16)
            dk = _dot_tn(ds, qs)
            dv = _dot_tn(p, dos)
            if first is not True:
                dk_ref[prev, :] += dk[:SPAN]
                dv_ref[prev, :] += dv[:SPAN]
            dk_ref[cur, :] += dk[-SPAN:]
            dv_ref[cur, :] += dv[-SPAN:]

        _for_each_block(block, seg_blocks)
        dk_out[...] = dk_ref[...].astype(BF16)
        dv_out[...] = dv_ref[...].astype(BF16)

    col = _spec((SEQ, 128), lambda j: (0, j))
    return pl.pallas_call(
        body, name=name, grid=(ATTN_WIDTH // 128,),
        in_specs=[col] * 6, out_specs=[col] * 3,
        out_shape=[jax.ShapeDtypeStruct((SEQ, ATTN_WIDTH), BF16)] * 3,
        scratch_shapes=[pltpu.VMEM((SEQ, 128), F32)] * 2,
        compiler_params=_params(1),
    )(q, k, v, d_out, delta, lse)


def _attn_bwd_post(name, grads, cos_t, sin_t):
    t = ROW_TILE
    w = ATTN_WIDTH

    def body(*refs):
        ins, cos_ref, sin_ref, out_ref, s4, s16 = refs[:9], refs[9], refs[10], refs[11], refs[12], refs[13]
        cosv, sinv = cos_ref[...], sin_ref[...]
        for a in range(3):
            g1, g4, g16 = ins[a], ins[3 + a], ins[6 + a]
            for cb in range(w // 128):
                cols = slice(cb * 128, (cb + 1) * 128)
                _unpermute(s4, g4, 4, cols)
                _unpermute(s16, g16, 16, cols)
                val = g1[:, cols].astype(F32) + s4[...] + s16[...]
                if a < 2:
                    val = val * cosv + _swap_halves(val * sinv)
                if a == 0:
                    val = val * (HEAD_DIM ** -0.5)
                out_ref[:, a * w + cb * 128:a * w + (cb + 1) * 128] = val.astype(BF16)

    views = []
    for p, d in enumerate(DILATIONS):
        for a in range(3):
            views.append(grads[p][a] if d == 1 else grads[p][a].reshape(d, SEQ // d, w))
    perm = _permuted_specs(t, w)
    in_specs = [perm[0]] * 3 + [perm[1]] * 3 + [perm[2]] * 3
    return pl.pallas_call(
        body, name=name, grid=(SEQ // t,),
        in_specs=in_specs + [_spec((t, 128), lambda i: (i, 0))] * 2,
        out_specs=_spec((t, 3 * w), lambda i: (i, 0)),
        out_shape=jax.ShapeDtypeStruct((SEQ, 3 * w), BF16),
        scratch_shapes=[pltpu.VMEM((t, 128), F32)] * 2,
        compiler_params=_params(1),
    )(*views, cos_t, sin_t)


N_LEVELS = 7
HGRN_PAIR = 2


def _hgrn_consts():
    c = CHUNK
    i = np.arange(c)[:, None]
    s = np.arange(c)[None, :]
    blocks = [s <= i]
    for lv in range(N_LEVELS):
        bs = c >> lv
        h = bs // 2
        m = (i // bs) * bs + h - 1
        second = (i % bs) >= h
        blocks.append((second & (s > m) & (s <= i)) | (~second & (s > i) & (s <= m)))
    blocks.append(s > i)
    stack = np.concatenate(blocks, axis=0).astype(np.float32)
    return jnp.asarray(stack, dtype=BF16), jnp.asarray(stack.T, dtype=BF16)


def _exact_dot(m01, x):
    hi = x.astype(BF16)
    lo = (x - hi.astype(F32)).astype(BF16)
    n = x.shape[1]
    full = jnp.dot(m01, jnp.concatenate([hi, lo], axis=1), preferred_element_type=F32)
    return full[:, :n] + full[:, n:]


def _hgrn_gates(qh, z, lb):
    sq = _sigmoid(qh)
    q = qh * sq * (HGRN_DIM ** -0.5)
    sig = _sigmoid(z)
    sigm = _sigmoid(-z)
    f = lb + (1.0 - lb) * sig
    k = (1.0 - lb) * sigm
    return q, k, f, sq, sig, sigm


def _level_masks(lv):
    row = lax.broadcasted_iota(jnp.int32, (CHUNK, CHUNK), 0)
    col = lax.broadcasted_iota(jnp.int32, (CHUNK, CHUNK), 1)
    shift = N_LEVELS - lv
    half = CHUNK >> (lv + 1)
    second = (row & half) != 0
    second_col = (col & half) != 0
    same = (row >> shift) == (col >> shift)
    return second, same & second & ~second_col, same & (second != second_col)


def _hgrn_fwd(name, proj, lb, gain, stack, mixed, mixed_t):
    t = ROW_TILE
    per = t // CHUNK
    n_rb = SEQ // t
    n_chunks = SEQ // CHUNK
    col0 = 3 * ATTN_WIDTH // 128
    pair_w = HGRN_PAIR * HGRN_DIM

    def body(q_ref, f_ref, i_ref, g_ref, lb_ref, gain_ref, stack_ref, mixed_in, mixed_t_in,
             rec_ref, rect_ref, o_ref, st_out, a_out, st):
        del mixed_in, mixed_t_in

        @pl.when(pl.program_id(1) == 0)
        def _():
            st[...] = jnp.zeros_like(st)

        row = lax.broadcasted_iota(jnp.int32, (CHUNK, CHUNK), 0)
        col = lax.broadcasted_iota(jnp.int32, (CHUNK, CHUNK), 1)
        for c, hh in [(c, hh) for c in range(per) for hh in range(HGRN_PAIR)]:
            rows = slice(c * CHUNK, (c + 1) * CHUNK)
            lanes = slice(hh * HGRN_DIM, (hh + 1) * HGRN_DIM)
            lbv = lb_ref[hh]
            qh, z, v, gh = q_ref[rows, lanes], f_ref[rows, lanes], i_ref[rows, lanes], g_ref[rows, lanes]
            q, k, f, _, _, _ = _hgrn_gates(qh, z, lbv)
            dec = _exact_dot(stack_ref[...], jnp.log(f))
            g = dec[0:CHUNK]
            to_end = dec[(N_LEVELS + 1) * CHUNK:(N_LEVELS + 2) * CHUNK]
            a = jnp.where(row == col, jnp.sum(q * k, axis=1, keepdims=True), 0.0)
            for lv in range(N_LEVELS):
                second, square, _ = _level_masks(lv)
                t = jnp.where(second, q, k) * jnp.exp(dec[(lv + 1) * CHUNK:(lv + 2) * CHUNK])
                a = a + jnp.where(square, _dot_nt(t, t), 0.0)
            st_prev = st[hh]
            st_out[hh, c] = st_prev
            a_out[hh, c] = a
            o = _dot(a, v) + _dot_nt(q * jnp.exp(g), st_prev)
            k_end = k * jnp.exp(to_end)
            st[hh] = st_prev * jnp.exp(g[CHUNK - 1:CHUNK, :]) + _dot(v.T, k_end)
            o_ref[rows, lanes] = o
            r = lax.rsqrt(jnp.mean(o * o, axis=1, keepdims=True) + NORM_EPS)
            rec = o * r * gain_ref[...] * (gh * _sigmoid(gh))
            rec_ref[rows, lanes] = rec.astype(BF16)
            rect_ref[lanes, rows] = rec.T.astype(BF16)

    def col_spec(tt):
        return _spec((t, pair_w), lambda h, rb: (rb, (col0 + HGRN_HEADS * tt) // HGRN_PAIR + h))

    chunk_spec = _spec((HGRN_PAIR, per, CHUNK, CHUNK), lambda h, rb: (h, rb, 0, 0))
    return pl.pallas_call(
        body, name=name, grid=(HGRN_HEADS // HGRN_PAIR, n_rb),
        in_specs=[col_spec(0), col_spec(1), col_spec(2), col_spec(3),
                  _spec((HGRN_PAIR, 1, HGRN_DIM), lambda h, rb: (h, 0, 0)),
                  _spec((1, HGRN_DIM), lambda h, rb: (0, 0)),
                  _spec(stack.shape, lambda h, rb: (0, 0)), ANY_SPEC, ANY_SPEC],
        out_specs=[_spec((t, pair_w), lambda h, rb: (rb, ATTN_WIDTH // pair_w + h)),
                   _spec((pair_w, t), lambda h, rb: (ATTN_WIDTH // pair_w + h, rb)),
                   _spec((t, pair_w), lambda h, rb: (rb, h)),
                   chunk_spec, chunk_spec],
        out_shape=[jax.ShapeDtypeStruct(mixed.shape, BF16),
                   jax.ShapeDtypeStruct(mixed_t.shape, BF16),
                   jax.ShapeDtypeStruct((SEQ, HGRN_WIDTH), F32),
                   jax.ShapeDtypeStruct((HGRN_HEADS, n_chunks, CHUNK, CHUNK), F32),
                   jax.ShapeDtypeStruct((HGRN_HEADS, n_chunks, CHUNK, CHUNK), F32)],
        scratch_shapes=[pltpu.VMEM((HGRN_PAIR, CHUNK, CHUNK), F32)],
        input_output_aliases={7: 0, 8: 1},
        compiler_params=_params(2),
    )(proj, proj, proj, proj, lb, gain, stack, mixed, mixed_t)


def _hgrn_bwd(name, proj, d_rec, o_pre, states, scores, lb, gain, stack, stack_t):
    t = ROW_TILE
    per = t // CHUNK
    n_rb = SEQ // t
    col0 = 3 * ATTN_WIDTH // 128
    pair_w = HGRN_PAIR * HGRN_DIM

    def body(q_ref, f_ref, i_ref, g_ref, drec_ref, o_ref, st_ref, a_ref, lb_ref, gain_ref,
             stack_ref, stack_t_ref, dq_ref, df_ref, di_ref, dg_ref, dlb_ref, dgain_ref, dst):
        @pl.when(pl.program_id(1) == 0)
        def _():
            dst[...] = jnp.zeros_like(dst)
            dlb_ref[...] = jnp.zeros_like(dlb_ref)
            dgain_ref[...] = jnp.zeros_like(dgain_ref)

        gain_v = gain_ref[...]
        row = lax.broadcasted_iota(jnp.int32, (CHUNK, CHUNK), 0)
        col = lax.broadcasted_iota(jnp.int32, (CHUNK, CHUNK), 1)
        for c, hh in [(c, hh) for c in reversed(range(per)) for hh in range(HGRN_PAIR)]:
            rows = slice(c * CHUNK, (c + 1) * CHUNK)
            lanes = slice(hh * HGRN_DIM, (hh + 1) * HGRN_DIM)
            lbv = lb_ref[hh]
            qh, z, v, gh = q_ref[rows, lanes], f_ref[rows, lanes], i_ref[rows, lanes], g_ref[rows, lanes]
            q, k, f, sq, sig, sigm = _hgrn_gates(qh, z, lbv)
            dec = _exact_dot(stack_ref[...], jnp.log(f))
            g = dec[0:CHUNK]
            to_end = dec[(N_LEVELS + 1) * CHUNK:(N_LEVELS + 2) * CHUNK]
            e_g = jnp.exp(g)
            e_end = jnp.exp(to_end)
            e_last = jnp.exp(g[CHUNK - 1:CHUNK, :])
            q_in = q * e_g
            k_end = k * e_end
            st_prev = st_ref[hh, c]
            a = a_ref[hh, c]
            dst_new = dst[hh]

            o = o_ref[rows, lanes]
            drec = drec_ref[rows, lanes]
            sg = _sigmoid(gh)
            r = lax.rsqrt(jnp.mean(o * o, axis=1, keepdims=True) + NORM_EPS)
            ohat = o * r
            d_gh = drec * (ohat * gain_v) * (sg * (1.0 + gh * (1.0 - sg)))
            d_on = drec * (gh * sg)
            dgain_ref[hh] += jnp.sum(d_on * ohat, axis=0, keepdims=True)
            d_ohat = d_on * gain_v
            d_o = r * (d_ohat - ohat * jnp.mean(d_ohat * ohat, axis=1, keepdims=True))

            d_sym = jnp.where(row >= col, _dot_nt(d_o, v), _dot_nt(v, d_o))
            d_v = _dot(a.T, d_o) + _dot_nt(k_end, dst_new)
            d_q_in = _dot(d_o, st_prev)
            d_k_end = _dot(v, dst_new)
            d_q = d_q_in * e_g
            d_k = d_k_end * e_end
            diag = jnp.sum(d_o * v, axis=1, keepdims=True)
            d_q = d_q + diag * k
            d_k = d_k + diag * q
            d_dec = [q_in * d_q_in]
            d_both, d_second = None, None
            for lv in range(N_LEVELS):
                e = jnp.exp(dec[(lv + 1) * CHUNK:(lv + 2) * CHUNK])
                second, _, mirrored = _level_masks(lv)
                t = jnp.where(second, q, k) * e
                d_t = _dot(jnp.where(mirrored, d_sym, 0.0), t)
                d_te = d_t * e
                d_both = d_te if d_both is None else d_both + d_te
                d_second = jnp.where(second, d_te, 0.0) if d_second is None else d_second + jnp.where(second, d_te, 0.0)
                d_dec.append(t * d_t)
            d_q = d_q + d_second
            d_k = d_k + (d_both - d_second)
            d_dec.append(k_end * d_k_end)
            flux = jnp.sum(dst_new * st_prev, axis=0, keepdims=True) * e_last
            d_lf = _exact_dot(stack_t_ref[...], jnp.concatenate(d_dec, axis=0)) + flux
            dst[hh] = dst_new * e_last + _dot(d_o.T, q_in)

            d_f = d_lf / f - d_k
            dlb_ref[hh] += jnp.sum(d_f * sigm, axis=0, keepdims=True)
            dq_ref[rows, lanes] = (d_q * (HGRN_DIM ** -0.5) * (sq * (1.0 + qh * (1.0 - sq)))).astype(BF16)
            df_ref[rows, lanes] = (d_f * (1.0 - lbv) * sig * sigm).astype(BF16)
            di_ref[rows, lanes] = d_v.astype(BF16)
            dg_ref[rows, lanes] = d_gh.astype(BF16)

    last = n_rb - 1

    def col_spec(tt):
        return _spec((t, pair_w), lambda h, rb: (last - rb, (col0 + HGRN_HEADS * tt) // HGRN_PAIR + h))

    head_col = _spec((t, pair_w), lambda h, rb: (last - rb, h))
    rec_col0 = (d_rec.shape[1] - HGRN_WIDTH) // pair_w
    d_rec_col = _spec((t, pair_w), lambda h, rb: (last - rb, rec_col0 + h))
    chunk_spec = _spec((HGRN_PAIR, per, CHUNK, CHUNK), lambda h, rb: (h, last - rb, 0, 0))
    vec_spec = _spec((HGRN_PAIR, 1, HGRN_DIM), lambda h, rb: (h, 0, 0))
    outs = pl.pallas_call(
        body, name=name, grid=(HGRN_HEADS // HGRN_PAIR, n_rb),
        in_specs=[col_spec(0), col_spec(1), col_spec(2), col_spec(3), d_rec_col, head_col,
                  chunk_spec, chunk_spec, vec_spec,
                  _spec((1, HGRN_DIM), lambda h, rb: (0, 0)),
                  _spec(stack.shape, lambda h, rb: (0, 0)), _spec(stack_t.shape, lambda h, rb: (0, 0))],
        out_specs=[head_col] * 4 + [vec_spec, vec_spec],
        out_shape=[jax.ShapeDtypeStruct((SEQ, HGRN_WIDTH), BF16)] * 4
                  + [jax.ShapeDtypeStruct((HGRN_HEADS, 1, HGRN_DIM), F32)] * 2,
        scratch_shapes=[pltpu.VMEM((HGRN_PAIR, CHUNK, CHUNK), F32)],
        compiler_params=_params(2),
    )(proj, proj, proj, proj, d_rec, o_pre, states, scores, lb, gain, stack, stack_t)
    return outs


ANY_SPEC = pl.BlockSpec(memory_space=pl.ANY)


def _my_place():
    return lax.axis_index("x"), lax.axis_index("y"), lax.axis_index("c")


def _other_chips(x, y):
    return [(1 - x, y), (x, 1 - y), (1 - x, 1 - y)]


def _remote(src, dst, send_sem, recv_sem, device):
    return pltpu.make_async_remote_copy(src_ref=src, dst_ref=dst, send_sem=send_sem, recv_sem=recv_sem,
                                        device_id=device, device_id_type=MESH)


def _staged_copies(srcs, dsts, stage, sems):
    loads = [pltpu.make_async_copy(srcs[i], stage[i], sems.at[i]) for i in range(len(srcs))]
    for cp in loads:
        cp.start()
    stores = []
    for i, cp in enumerate(loads):
        cp.wait()
        stores.append(pltpu.make_async_copy(stage[i], dsts[i], sems.at[i]))
        stores[-1].start()
    return stores


def _gather_weights(name, shards):
    n = len(shards)

    def body(*refs):
        ins, outs = refs[:n], refs[n:2 * n]
        ici_send, ici_recv, d2d_send, d2d_recv, local_sems = refs[2 * n:2 * n + 5]
        stage = refs[2 * n + 5:]
        x, y, c = _my_place()
        me = 2 * x + y
        chips = _other_chips(x, y)

        def half(i, which):
            h = ins[i].shape[0] // 2
            return pl.ds(which * h, h)

        sends = []
        for i in range(n):
            for j, (px, py) in enumerate(chips):
                sends.append(_remote(ins[i].at[half(i, c), :], outs[i].at[me, half(i, c), :],
                                     ici_send.at[3 * i + j], ici_recv.at[3 * i + j], (px, py, c)))
        for cp in sends:
            cp.start()
        local = _staged_copies(ins, [outs[i].at[me] for i in range(n)], stage, local_sems)
        for i in range(n):
            for j, (px, py) in enumerate(chips):
                landed = outs[i].at[2 * px + py, half(i, c), :]
                _remote(landed, landed, ici_send.at[3 * i + j], ici_recv.at[3 * i + j], (px, py, c)).wait_recv()
                forward = _remote(landed, landed, d2d_send.at[3 * i + j], d2d_recv.at[3 * i + j], (x, y, 1 - c))
                forward.start()
                sends.append(forward)
        for i in range(n):
            for j, (px, py) in enumerate(chips):
                other = outs[i].at[2 * px + py, half(i, 1 - c), :]
                _remote(other, other, d2d_send.at[3 * i + j], d2d_recv.at[3 * i + j], (x, y, 1 - c)).wait_recv()
        for cp in sends:
            cp.wait_send()
        for cp in local:
            cp.wait()

    return pl.pallas_call(
        body, name=name, in_specs=[ANY_SPEC] * n, out_specs=[ANY_SPEC] * n,
        out_shape=[jax.ShapeDtypeStruct((N_CHIPS,) + s.shape, s.dtype) for s in shards],
        scratch_shapes=([pltpu.SemaphoreType.DMA((3 * n,))] * 4 + [pltpu.SemaphoreType.DMA((n,))]
                        + [pltpu.VMEM(s.shape, s.dtype) for s in shards]),
        compiler_params=pltpu.CompilerParams(vmem_limit_bytes=VMEM_LIMIT),
    )(*shards)


HBM_SPEC = pl.BlockSpec(memory_space=pltpu.HBM)
SEM_SPEC = pl.BlockSpec(memory_space=pltpu.SEMAPHORE)
SPLIT_PARAMS = pltpu.CompilerParams(has_side_effects=pltpu.SideEffectType.DATAFLOW_SIDE_EFFECTING)
N_PEERS = {"gather": N_CHIPS - 1, "scatter": N_DEV - 1}


def _split_copies(ins, lands, send_sems, recv_sems, kind):
    x, y, c = _my_place()
    pairs = []
    for i in range(len(ins)):
        if kind == "gather":
            me = 2 * x + y
            for j, (px, py) in enumerate(_other_chips(x, y)):
                sems = (send_sems.at[3 * i + j], recv_sems.at[3 * i + j], (px, py, c))
                pairs.append((_remote(ins[i], lands[i].at[me], *sems),
                              _remote(ins[i], lands[i].at[2 * px + py], *sems)))
        else:
            me = 4 * x + 2 * y + c
            h = ins[i].shape[1] // 2
            for k in range(1, N_DEV):
                px, py, pc = (x + (k >> 2)) % 2, (y + ((k >> 1) & 1)) % 2, (c + (k & 1)) % 2
                src = ins[i].at[2 * px + py, pl.ds(pc * h, h), :]
                sems = (send_sems.at[7 * i + k - 1], recv_sems.at[7 * i + k - 1], (px, py, pc))
                pairs.append((_remote(src, lands[i].at[me], *sems),
                              _remote(src, lands[i].at[4 * px + 2 * py + pc], *sems)))
    return pairs


def _exchange_start(name, srcs, lands, kind, after=None):
    n = len(srcs)
    n_sems = N_PEERS[kind] * n
    extra = [] if after is None else [after]

    def body(*refs):
        ins, land_refs = refs[:n], refs[n:2 * n]
        send_sems, recv_sems = refs[2 * n + len(extra):2 * n + len(extra) + 2]
        token = refs[-1]
        for send, _ in _split_copies(ins, land_refs, send_sems, recv_sems, kind):
            send.start()
        token[...] = jnp.zeros_like(token)

    arrays = list(srcs) + list(lands)
    outs = pl.pallas_call(
        body, name=name,
        in_specs=[HBM_SPEC] * (2 * n) + [ANY_SPEC] * len(extra),
        out_shape=([pltpu.SemaphoreType.DMA((n_sems,))] * 2 + [pltpu.HBM(a.shape, a.dtype) for a in arrays]
                   + [jax.ShapeDtypeStruct((8, 128), F32)]),
        out_specs=[SEM_SPEC] * 2 + [HBM_SPEC] * (2 * n) + [pl.BlockSpec(memory_space=pltpu.VMEM)],
        input_output_aliases={i: 2 + i for i in range(2 * n)},
        compiler_params=SPLIT_PARAMS,
    )(*[pltpu.with_memory_space_constraint(a, pltpu.HBM) for a in arrays], *extra)
    return outs[:2], outs[2:2 + 2 * n], outs[-1]


def _exchange_wait(name, sems, passed, kind, after):
    n = len(passed) // 2

    def body(*refs):
        ins, land_refs = refs[:n], refs[n:2 * n]
        send_sems, recv_sems = refs[2 * n:2 * n + 2]
        for send, arrive in _split_copies(ins, land_refs, send_sems, recv_sems, kind):
            send.wait_send()
            arrive.wait_recv()

    outs = pl.pallas_call(
        body, name=name,
        in_specs=[HBM_SPEC] * (2 * n) + [SEM_SPEC] * 2 + [ANY_SPEC],
        out_shape=[pltpu.HBM(a.shape, a.dtype) for a in passed],
        out_specs=[HBM_SPEC] * (2 * n),
        input_output_aliases={i: i for i in range(2 * n)},
        compiler_params=SPLIT_PARAMS,
    )(*passed, *sems, after)
    return outs[:n], outs[n:]


def _own_slot(name, own, me):
    r, cc = own.shape
    th = min(r, 512)

    def body(me_ref, x_ref, o_ref):
        del me_ref
        o_ref[...] = x_ref[...]

    grid_spec = pltpu.PrefetchScalarGridSpec(
        num_scalar_prefetch=1, grid=(r // th,),
        in_specs=[pl.BlockSpec((th, cc), lambda i, me_ref: (i, 0))],
        out_specs=pl.BlockSpec((None, th, cc), lambda i, me_ref: (me_ref[0], i, 0)))
    return pl.pallas_call(
        body, name=name, grid_spec=grid_spec,
        out_shape=jax.ShapeDtypeStruct((N_CHIPS, r, cc), own.dtype), compiler_params=_params(1),
    )(me, own)


def _sum_devices(name, landed, own, place):
    n_dev, h, cc = landed.shape
    th = min(h, 256)
    nb = h // th

    def body(place_ref, l_ref, own_ref, o_ref):
        total = None
        for d in range(n_dev):
            piece = jnp.where(place_ref[0] == d, own_ref[...], l_ref[d]).astype(F32)
            total = piece if total is None else total + piece
        o_ref[...] = total

    grid_spec = pltpu.PrefetchScalarGridSpec(
        num_scalar_prefetch=1, grid=(nb,),
        in_specs=[pl.BlockSpec((n_dev, th, cc), lambda i, p: (0, i, 0)),
                  pl.BlockSpec((None, th, cc), lambda i, p: (p[1], p[2] * nb + i, 0))],
        out_specs=pl.BlockSpec((th, cc), lambda i, p: (i, 0)))
    return pl.pallas_call(
        body, name=name, grid_spec=grid_spec,
        out_shape=jax.ShapeDtypeStruct((h, cc), F32), compiler_params=_params(1),
    )(place, landed, own)


def _share_halves(name, halves):
    flat = [t for per_weight in halves for t in per_weight]
    n = len(flat)
    n_w = len(halves)

    def body(*refs):
        ins, outs = refs[:n], refs[n:n + n_w]
        send_sems, recv_sems, local_sems = refs[n + n_w:n + n_w + 3]
        stage = refs[n + n_w + 3:]
        x, y, c = _my_place()
        sends, own = [], []
        for i in range(n):
            w, l = divmod(i, DEPTH)
            h = ins[i].shape[0]
            own.append(outs[w].at[l, pl.ds(c * h, h), :])
            sends.append(_remote(ins[i], own[i], send_sems.at[i], recv_sems.at[i], (x, y, 1 - c)))
        for cp in sends:
            cp.start()
        local = _staged_copies(ins, own, stage, local_sems)
        for i in range(n):
            w, l = divmod(i, DEPTH)
            h = ins[i].shape[0]
            _remote(ins[i], outs[w].at[l, pl.ds((1 - c) * h, h), :], send_sems.at[i], recv_sems.at[i],
                    (x, y, 1 - c)).wait_recv()
        for cp in sends:
            cp.wait_send()
        for cp in local:
            cp.wait()

    return pl.pallas_call(
        body, name=name, in_specs=[ANY_SPEC] * n, out_specs=[ANY_SPEC] * n_w,
        out_shape=[jax.ShapeDtypeStruct((DEPTH, 2 * per_weight[0].shape[0], per_weight[0].shape[1]), F32)
                   for per_weight in halves],
        scratch_shapes=([pltpu.SemaphoreType.DMA((n,))] * 3 + [pltpu.VMEM(t.shape, t.dtype) for t in flat]),
        compiler_params=pltpu.CompilerParams(vmem_limit_bytes=VMEM_LIMIT),
    )(*flat)


def _all_reduce_small(pack, after):
    def body(p_ref, after_ref, o_ref, recv, send_sems, recv_sems):
        del after_ref
        x, y, c = _my_place()
        me = 4 * x + 2 * y + c
        recv[me] = p_ref[...]
        peers = []
        for k in range(1, N_DEV):
            px, py, pc = (x + (k >> 2)) % 2, (y + ((k >> 1) & 1)) % 2, (c + (k & 1)) % 2
            peers.append((px, py, pc))
        sends = [_remote(p_ref, recv.at[me], send_sems.at[k], recv_sems.at[k], peer)
                 for k, peer in enumerate(peers)]
        for cp in sends:
            cp.start()
        for k, (px, py, pc) in enumerate(peers):
            _remote(p_ref, recv.at[4 * px + 2 * py + pc], send_sems.at[k], recv_sems.at[k],
                    (px, py, pc)).wait_recv()
        for cp in sends:
            cp.wait_send()
        total = recv[0]
        for d in range(1, N_DEV):
            total = total + recv[d]
        o_ref[...] = total

    vmem = pl.BlockSpec(memory_space=pltpu.VMEM)
    return pl.pallas_call(
        body, name="all_reduce_small", in_specs=[vmem, ANY_SPEC], out_specs=vmem,
        out_shape=jax.ShapeDtypeStruct(pack.shape, F32),
        scratch_shapes=[pltpu.VMEM((N_DEV,) + pack.shape, F32),
                        pltpu.SemaphoreType.DMA((N_DEV - 1,)), pltpu.SemaphoreType.DMA((N_DEV - 1,))],
    )(pack, after)


def _adamw(name, w, g, m, v):
    r, cc = w.shape
    th = min(r, 256)

    def body(w_ref, g_ref, m_ref, v_ref, d_ref, m_out, v_out):
        gv = g_ref[...]
        m2 = ADAM_B1 * m_ref[...] + (1.0 - ADAM_B1) * gv
        v2 = ADAM_B2 * v_ref[...] + (1.0 - ADAM_B2) * (gv * gv)
        m_hat = m2 / (1.0 - ADAM_B1 ** ADAM_STEP)
        v_hat = v2 / (1.0 - ADAM_B2 ** ADAM_STEP)
        d_ref[...] = -ADAM_LR * (m_hat / (jnp.sqrt(v_hat) + ADAM_EPS) + ADAM_WD * w_ref[...])
        m_out[...] = m2
        v_out[...] = v2

    tile = _spec((th, cc), lambda i: (i, 0))
    return pl.pallas_call(
        body, name=name, grid=(r // th,), in_specs=[tile] * 4, out_specs=[tile] * 3,
        out_shape=[jax.ShapeDtypeStruct((r, cc), F32)] * 3, compiler_params=_params(1),
    )(w, g, m, v)


def _lower_bounds(lb_logits):
    p = jax.nn.softmax(lb_logits.astype(F32), axis=0)
    return jnp.cumsum(p, axis=0) - p[0]


def _layer_forward(l, stream, small, weights, consts, next_gain=None, loss=None, after=None):
    win, rest = weights
    cos_t, sin_t, stack, _, _ = consts
    tm = MM_TILE
    x_in, h, h_t = stream
    saved = {"x_in": x_in}

    proj = _mm_pieces(f"proj{l}", h, win, False, tm, after=after)
    saved.update(h_t=h_t, proj=proj)

    qkv = _attn_prep(f"attn_prep{l}", proj, cos_t, sin_t)
    outs, lses = [], []
    for p, d in enumerate(DILATIONS):
        o, lse = _attn_fwd(f"attn_fwd{l}_{d}", *qkv[p], SEQ // d // SPAN)
        outs.append(o)
        lses.append(lse)
    mixed, mixed_t, attn, lse = _attn_merge(f"attn_merge{l}", outs, lses, small["attn_out_gain"][l][None, :])
    saved.update(qkv=qkv, attn=attn, lse=lse)

    lb3 = small["lower"][l].reshape(HGRN_HEADS, 1, HGRN_DIM)
    mixed, mixed_t, o_pre, states, scores = _hgrn_fwd(f"hgrn_fwd{l}", proj, lb3, small["hgrn_out_gain"][l][None, :],
                                                      stack, mixed, mixed_t)
    wo, wu, wd = rest(mixed)
    saved.update(mixed_t=mixed_t, o_pre=o_pre, states=states, scores=scores, lb3=lb3, weights=(win, wo, wu, wd))

    x_mid, h2, h2_t = _mm_accum(f"out_proj{l}", mixed, wo, False, tm, x_in, next_gain=small["norm_mlp"][l][None, :])
    saved["x_mid"] = x_mid

    a, relu_u, a_t = _mm_pieces(f"up{l}", h2, wu, False, tm, epilogue="relu2")
    new_stream = tuple(_mm_accum(f"down{l}", a, wd, False, tm, x_mid, next_gain=next_gain, loss=loss))
    saved.update(h2_t=h2_t, relu_u=relu_u, a_t=a_t)
    return new_stream, saved


def _layer_backward(l, dx, saved, small, consts, on_grads, after=None):
    win, wo, wu, wd = saved["weights"]
    cos_t, sin_t, stack, stack_t, head_sum = consts
    tm = MM_TILE

    dx, dx_b = dx
    du = _mm_pieces(f"d_u{l}", dx_b, wd, True, tm, epilogue="relu2_grad", extra=saved["relu_u"], after=after)
    d_wd = _mm_dw(f"d_wdown{l}", saved["a_t"], dx_b, False, tm)
    dxm, dxm_b, dg_mlp = _mm_accum(f"d_h2_{l}", du, wu, True, tm, dx,
                                   norm=(saved["x_mid"], small["norm_mlp"][l][None, :]))
    d_wu = _mm_dw(f"d_wup{l}", saved["h2_t"], du, True, tm)
    d_wo = _mm_dw(f"d_wout{l}", saved["mixed_t"], dxm_b, False, tm)
    after_early = on_grads(l, "early", (d_wu, d_wd, d_wo))

    d_mixed = _mm_pieces(f"d_mixed{l}", dxm_b, wo, True, tm, after=after_early)
    d_rec = d_mixed

    d_out, delta, lses, dg_attn = _attn_bwd_prep(f"attn_bwd_prep{l}", d_mixed, saved["attn"], saved["lse"],
                                                 small["attn_out_gain"][l][None, :], head_sum)
    grads = []
    for p, d in enumerate(DILATIONS):
        grads.append(_attn_bwd(f"attn_bwd{l}_{d}", *saved["qkv"][p], d_out[p], delta[p], lses[p],
                               SEQ // d // SPAN))
    dp_attn = _attn_bwd_post(f"attn_bwd_post{l}", grads, cos_t, sin_t)

    dq_h, df_h, di_h, dg_h, d_lower, dg_hgrn = _hgrn_bwd(
        f"hgrn_bwd{l}", saved["proj"], d_rec, saved["o_pre"], saved["states"], saved["scores"],
        saved["lb3"], small["hgrn_out_gain"][l][None, :], stack, stack_t)
    dproj = [dp_attn, dq_h, df_h, di_h, dg_h]

    d_win = _mm_dw(f"d_win{l}", saved["h_t"], dproj, True, tm)
    after_last = on_grads(l, "last", (d_win,))
    dx_in, dx_in_b, dg_mix = _mm_accum(f"d_h{l}", dproj, win, True, tm, dxm,
                                       norm=(saved["x_in"], small["norm_mix"][l][None, :]), after=after_last)

    small_grads = {"norm_mix": dg_mix[0], "attn_out_gain": dg_attn[0],
                   "lower": d_lower.reshape(HGRN_WIDTH),
                   "hgrn_out_gain": jnp.sum(dg_hgrn, axis=0).reshape(HGRN_DIM), "norm_mlp": dg_mlp[0]}
    return (dx_in, dx_in_b), after_last, small_grads


def _local_step(xs, target, small, get_weights, on_grads):
    consts = _rope_tables() + _hgrn_consts() + (_head_sum_matrix(),)
    stream = (xs,) + tuple(_rms_fwd("norm_mix0", xs, small["norm_mix"][0][None, :]))
    saved = []
    for l in range(DEPTH):
        w, after = get_weights(l, stream[0])
        if l + 1 < DEPTH:
            stream, s = _layer_forward(l, stream, small, w, consts, next_gain=small["norm_mix"][l + 1][None, :],
                                       after=after)
        else:
            stream, s = _layer_forward(l, stream, small, w, consts, loss=(small["norm_final"][None, :], target),
                                       after=after)
        saved.append(s)
    dx_f, dx_b, dg_final, loss = stream
    dx = (dx_f, dx_b)
    small_grads = [None] * DEPTH
    after = None
    for l in reversed(range(DEPTH)):
        dx, after, small_grads[l] = _layer_backward(l, dx, saved[l], small, consts, on_grads, after=after)
    return loss, dx[0], dg_final[0], small_grads


def _pack_small(norm_mix, attn_out_gain, lb, hgrn_out_gain, norm_mlp, norm_final, last_row):
    rows = [norm_mix, attn_out_gain.reshape(1, D_MODEL), lb.reshape(1, D_MODEL),
            jnp.pad(hgrn_out_gain.reshape(1, DEPTH * HGRN_DIM), ((0, 0), (0, D_MODEL - DEPTH * HGRN_DIM))),
            norm_mlp, norm_final.reshape(1, D_MODEL), last_row.reshape(1, D_MODEL)]
    pack = jnp.concatenate(rows, axis=0)
    return jnp.pad(pack, ((0, PACK_ROWS - pack.shape[0]), (0, 0)))


def _unpack_small(pack):
    return (pack[0:2], pack[2].reshape(DEPTH, ATTN_WIDTH), pack[3].reshape(DEPTH, HGRN_WIDTH),
            pack[4, :DEPTH * HGRN_DIM].reshape(DEPTH, HGRN_DIM), pack[5:7], pack[7], pack[8])


def kernel(x, norm_mix, w_in, attn_out_gain, hgrn_lb_logits, hgrn_out_gain, w_out, norm_mlp, w_up, w_down, norm_final, loss_target, m_norm_mix, m_w_in, m_attn_out_gain, m_hgrn_lb_logits, m_hgrn_out_gain, m_w_out, m_norm_mlp, m_w_up, m_w_down, m_norm_final, v_norm_mix, v_w_in, v_attn_out_gain, v_hgrn_lb_logits, v_hgrn_out_gain, v_w_out, v_norm_mlp, v_w_up, v_w_down, v_norm_final):
    lower, lower_vjp = jax.vjp(_lower_bounds, hgrn_lb_logits)
    small = {"norm_mix": norm_mix, "attn_out_gain": attn_out_gain, "lower": lower,
             "hgrn_out_gain": hgrn_out_gain, "norm_mlp": norm_mlp, "norm_final": norm_final}
    big_w = (w_in, w_out, w_up, w_down)

    x_pos, y_pos, core = lax.axis_index("x"), lax.axis_index("y"), lax.axis_index("c")
    me = (2 * x_pos + y_pos).astype(jnp.int32).reshape(1)
    place = jnp.stack([4 * x_pos + 2 * y_pos + core, 2 * x_pos + y_pos, core]).astype(jnp.int32)
    shards = [[w[l].astype(BF16) for w in big_w] for l in range(DEPTH)]
    in_flight = {}

    def start_gather(name, some, after):
        lands = [_own_slot(f"own_{name}_{i}", s, me) for i, s in enumerate(some)]
        sems, passed, token = _exchange_start(f"start_{name}", some, lands, "gather", after)
        in_flight[name] = (sems, passed)
        return token

    def finish_gather(name, after):
        return _exchange_wait(f"wait_{name}", *in_flight.pop(name), "gather", after)[1]

    def get_weights(l, stream):
        if l == 0:
            (win,) = _gather_weights("gather_w_in0", shards[0][:1])
            token = start_gather("gather_rest0", shards[0][1:], win)
            token = start_gather("gather_w_in1", shards[1][:1], token)
            token = start_gather("gather_rest1", shards[1][1:], token)
            return (win, lambda after: finish_gather("gather_rest0", after)), token
        (win,) = finish_gather("gather_w_in1", stream)
        return (win, lambda after: finish_gather("gather_rest1", after)), None

    reduced = {}

    def start_exchange(name, grads):
        srcs, lands = [g for g, _ in grads], [land for _, land in grads]
        sems, passed, token = _exchange_start(f"start_{name}", srcs, lands, "scatter")
        in_flight[name] = (sems, passed)
        return token

    def finish_exchange(name, after):
        own, landed = _exchange_wait(f"wait_{name}", *in_flight.pop(name), "scatter", after)
        return [_sum_devices(f"sum_{name}_{i}", p, g, place) for i, (p, g) in enumerate(zip(landed, own))]

    def on_grads(l, group, grads):
        token = start_exchange(f"{group}{l}", grads)
        if (l, group) == (0, "early"):
            reduced[(1, "early")] = finish_exchange("early1", token)
            reduced[(1, "last")] = finish_exchange("last1", token)
        if (l, group) == (0, "last"):
            reduced[(0, "early")] = finish_exchange("early0", token)
        return token

    loss, dx, dg_final, sg = _local_step(x[0], loss_target[0], small, get_weights, on_grads)

    big_m = (m_w_in, m_w_out, m_w_up, m_w_down)
    big_v = (v_w_in, v_w_out, v_w_up, v_w_down)
    names = ("w_in", "w_out", "w_up", "w_down")
    big_g, big_delta, big_new_m, big_new_v = [None] * 4, [None] * 4, [None] * 4, [None] * 4

    def finish_weights(group, which):
        whole = _share_halves(f"share_{group}", [[reduced[(l, group)][i] for l in range(DEPTH)]
                                                 for i in range(len(which))])
        for i, w in enumerate(which):
            shape = big_w[w].shape
            flat = lambda arr: arr.reshape(shape[0] * shape[1], shape[2])
            d, m2, v2 = _adamw(f"adamw_{names[w]}", flat(big_w[w]), flat(whole[i]), flat(big_m[w]), flat(big_v[w]))
            big_g[w], big_delta[w] = whole[i], d.reshape(shape)
            big_new_m[w], big_new_v[w] = m2.reshape(shape), v2.reshape(shape)

    finish_weights("early", (2, 3, 1))
    reduced[(0, "last")] = finish_exchange("last0", big_delta[3])
    finish_weights("last", (0,))

    stack2 = lambda key: jnp.stack([sg[l][key] for l in range(DEPTH)])
    pack = _pack_small(stack2("norm_mix"), stack2("attn_out_gain"), stack2("lower"), stack2("hgrn_out_gain"),
                       stack2("norm_mlp"), dg_final, jnp.broadcast_to(loss[0, 0], (D_MODEL,)))
    g_mix, g_attn, g_lower, g_hgrn, g_mlp, g_final, loss_row = _unpack_small(_all_reduce_small(pack, big_delta[0]))
    (g_logits,) = lower_vjp(g_lower)

    zeros_row = jnp.zeros((D_MODEL,), F32)
    small_w = (norm_mix, attn_out_gain, hgrn_lb_logits, hgrn_out_gain, norm_mlp, norm_final)
    small_m = (m_norm_mix, m_attn_out_gain, m_hgrn_lb_logits, m_hgrn_out_gain, m_norm_mlp, m_norm_final)
    small_v = (v_norm_mix, v_attn_out_gain, v_hgrn_lb_logits, v_hgrn_out_gain, v_norm_mlp, v_norm_final)
    small_g = (g_mix, g_attn, g_logits, g_hgrn, g_mlp, g_final)
    packs = [_pack_small(*t, zeros_row) for t in (small_w, small_g, small_m, small_v)]
    small_delta, small_new_m, small_new_v = [_unpack_small(p)[:6] for p in _adamw("adamw_small", *packs)]

    def ordered(small6, big4):
        mix, attn, lbl, hg, mlp, fin = small6
        return (mix, big4[0], attn, lbl, hg, big4[1], mlp, big4[2], big4[3], fin)

    return ((loss_row[0], dx[None]) + ordered(small_g, big_g) + ordered(small_delta, big_delta)
            + ordered(small_new_m, big_new_m) + ordered(small_new_v, big_new_v))
```

```python
import numpy as np
import jax
import jax.numpy as jnp
from jax import lax
from jax.experimental import pallas as pl
from jax.experimental.pallas import tpu as pltpu

F32 = jnp.float32
BF16 = jnp.bfloat16
MESH = pl.DeviceIdType.MESH

SEQ = 4096
D_MODEL = 1024
DEPTH = 2
ATTN_WIDTH = 512
HEAD_DIM = 64
HGRN_HEADS = 4
HGRN_DIM = 128
HGRN_WIDTH = 512
IN_W = 3584
MLP_HIDDEN = 4096
N_CHIPS = 4
N_DEV = 8
DILATIONS = (1, 4, 16)
SPAN = 128
ROPE_THETA = 10000.0
NORM_EPS = 1e-6
MASK_VALUE = -1e30
CHUNK = 128
ROW_TILE = 512
MM_TILE = 512
VMEM_LIMIT = 52 * 1024 * 1024

ADAM_LR = 0.001
ADAM_B1 = 0.9
ADAM_B2 = 0.999
ADAM_EPS = 1e-08
ADAM_WD = 0.01
ADAM_STEP = 10

PACK_ROWS = 16


def _params(n_axes):
    return pltpu.CompilerParams(dimension_semantics=("arbitrary",) * n_axes,
                                vmem_limit_bytes=VMEM_LIMIT)


def _dot(a, b):
    return jnp.dot(a.astype(BF16), b.astype(BF16), preferred_element_type=F32)


def _dot_nt(a, b):
    return lax.dot_general(a.astype(BF16), b.astype(BF16), (((1,), (1,)), ((), ())),
                           preferred_element_type=F32)


def _dot_tn(a, b):
    return lax.dot_general(a.astype(BF16), b.astype(BF16), (((0,), (0,)), ((), ())),
                           preferred_element_type=F32)


def _sigmoid(x):
    return 1.0 / (1.0 + jnp.exp(-x))


def _spec(shape, index_map):
    return pl.BlockSpec(shape, index_map)


def _mm_pieces(name, a, w, nt, tm, epilogue="none", extra=None, after=None):
    s = a.shape[0]
    pw = w.shape[1] if nt else w.shape[2]
    width = N_CHIPS * pw

    def body(a_ref, w_ref, *rest):
        e_ref = rest[0] if extra is not None else None
        outs = rest[-3:] if epilogue == "relu2" else rest[-1:]
        av = a_ref[...].astype(BF16)
        for j in range(N_CHIPS):
            cols = slice(j * pw, (j + 1) * pw)
            r = _dot_nt(av, w_ref[j]) if nt else _dot(av, w_ref[j])
            if epilogue == "relu2":
                relu = jnp.maximum(r, 0.0)
                r = relu * relu
                outs[1][:, cols] = relu.astype(BF16)
                outs[2][cols, :] = r.T.astype(BF16)
            elif epilogue == "relu2_grad":
                r = r * (2.0 * e_ref[:, cols].astype(F32))
            outs[0][:, cols] = r.astype(outs[0].dtype)

    row = lambda width_: _spec((tm, width_), lambda i: (i, 0))
    in_specs = [row(a.shape[1]), _spec(w.shape, lambda i: (0, 0, 0))]
    args = [a, w]
    if extra is not None:
        in_specs.append(row(width))
        args.append(extra)
    if after is not None:
        in_specs.append(pl.BlockSpec(memory_space=pl.ANY))
        args.append(after)
    if epilogue == "relu2":
        out_specs = [row(width), row(width), _spec((width, tm), lambda i: (0, i))]
        out_shape = [jax.ShapeDtypeStruct((s, width), BF16)] * 2 + [jax.ShapeDtypeStruct((width, s), BF16)]
    else:
        out_specs = row(width)
        out_shape = jax.ShapeDtypeStruct((s, width), BF16 if epilogue == "relu2_grad" else F32)
    return pl.pallas_call(body, name=name, grid=(s // tm,), in_specs=in_specs, out_specs=out_specs,
                          out_shape=out_shape, compiler_params=_params(1))(*args)


def _mm_accum(name, a, w, nt, tm, resid, norm=None, after=None, next_gain=None, loss=None):
    pieces = list(a) if isinstance(a, (list, tuple)) else [a]
    n_a = len(pieces)
    s = pieces[0].shape[0]
    pk = w.shape[2] if nt else w.shape[1]
    d = w.shape[1] if nt else w.shape[2]

    def body(*refs):
        a_refs, w_ref, resid_ref, rest = refs[:n_a], refs[n_a], refs[n_a + 1], refs[n_a + 2:]
        av = a_refs[0][...] if n_a == 1 else jnp.concatenate([ref[...] for ref in a_refs], axis=1)
        r = None
        for j in range(N_CHIPS):
            piece = av[:, j * pk:(j + 1) * pk].astype(BF16)
            term = _dot_nt(piece, w_ref[j]) if nt else _dot(piece, w_ref[j])
            r = term if r is None else r + term
        if loss is not None:
            g_ref, t_ref = rest[:2]
            dx_ref, dxb_ref, dg_ref, loss_ref, acc = rest[-5:]
            i = pl.program_id(0)

            @pl.when(i == 0)
            def _():
                dg_ref[...] = jnp.zeros_like(dg_ref)
                acc[...] = jnp.zeros_like(acc)

            xv = r + resid_ref[...]
            g = g_ref[...]
            rs = lax.rsqrt(jnp.mean(xv * xv, axis=1, keepdims=True) + NORM_EPS)
            xhat = xv * rs
            err = xhat * g - t_ref[...]
            acc[...] += jnp.sum(err * err, axis=0, keepdims=True)
            dy = err * (1.0 / d)
            dyg = dy * g
            dx = rs * (dyg - xhat * jnp.mean(dyg * xhat, axis=1, keepdims=True))
            dx_ref[...] = dx
            dxb_ref[...] = dx.astype(BF16)
            dg_ref[...] += jnp.sum(dy * xhat, axis=0, keepdims=True)

            @pl.when(i == s // tm - 1)
            def _():
                total = jnp.sum(acc[...], axis=1, keepdims=True) * (0.5 / d)
                loss_ref[...] = jnp.broadcast_to(total, loss_ref.shape)

            return
        if norm is None and next_gain is None:
            rest[-1][...] = r + resid_ref[...]
            return
        if norm is None:
            g_ref = rest[0]
            x_out, h_out, ht_out = rest[-3:]
            xv = r + resid_ref[...]
            x_out[...] = xv
            h = xv * lax.rsqrt(jnp.mean(xv * xv, axis=1, keepdims=True) + NORM_EPS) * g_ref[...]
            h_out[...] = h.astype(BF16)
            ht_out[...] = h.T.astype(BF16)
            return
        x_ref, g_ref = rest[:2]
        dx_ref, dxb_ref, dg_ref = rest[-3:]

        @pl.when(pl.program_id(0) == 0)
        def _():
            dg_ref[...] = jnp.zeros_like(dg_ref)

        xv = x_ref[...]
        rs = lax.rsqrt(jnp.mean(xv * xv, axis=1, keepdims=True) + NORM_EPS)
        xhat = xv * rs
        rg = r * g_ref[...]
        dx = resid_ref[...] + rs * (rg - xhat * jnp.mean(rg * xhat, axis=1, keepdims=True))
        dx_ref[...] = dx
        dxb_ref[...] = dx.astype(BF16)
        dg_ref[...] += jnp.sum(r * xhat, axis=0, keepdims=True)

    row = lambda width: _spec((tm, width), lambda i: (i, 0))
    in_specs = [row(p.shape[1]) for p in pieces] + [_spec(w.shape, lambda i: (0, 0, 0)), row(d)]
    args = pieces + [w, resid]
    scratch = []
    if loss is not None:
        in_specs += [_spec((1, d), lambda i: (0, 0)), row(d)]
        args += list(loss)
        out_specs = [row(d), row(d), _spec((1, d), lambda i: (0, 0)), _spec((1, 128), lambda i: (0, 0))]
        out_shape = [jax.ShapeDtypeStruct((s, d), F32), jax.ShapeDtypeStruct((s, d), BF16),
                     jax.ShapeDtypeStruct((1, d), F32), jax.ShapeDtypeStruct((1, 128), F32)]
        scratch = [pltpu.VMEM((1, d), F32)]
    elif norm is None and next_gain is None:
        out_specs, out_shape = row(d), jax.ShapeDtypeStruct((s, d), F32)
    elif norm is None:
        in_specs.append(_spec((1, d), lambda i: (0, 0)))
        args.append(next_gain)
        out_specs = [row(d), row(d), _spec((d, tm), lambda i: (0, i))]
        out_shape = [jax.ShapeDtypeStruct((s, d), F32), jax.ShapeDtypeStruct((s, d), BF16),
                     jax.ShapeDtypeStruct((d, s), BF16)]
    else:
        in_specs += [row(d), _spec((1, d), lambda i: (0, 0))]
        args += list(norm)
        out_specs = [row(d), row(d), _spec((1, d), lambda i: (0, 0))]
        out_shape = [jax.ShapeDtypeStruct((s, d), F32), jax.ShapeDtypeStruct((s, d), BF16),
                     jax.ShapeDtypeStruct((1, d), F32)]
    if after is not None:
        in_specs.append(pl.BlockSpec(memory_space=pl.ANY))
        args.append(after)
    return pl.pallas_call(body, name=name, grid=(s // tm,), in_specs=in_specs, out_specs=out_specs,
                          out_shape=out_shape, scratch_shapes=scratch, compiler_params=_params(1))(*args)


def _mm_dw(name, a_t, b, by_cols, tk):
    pieces = list(b) if isinstance(b, (list, tuple)) else [b]
    n_b = len(pieces)
    m, s = a_t.shape
    n = sum(p.shape[1] for p in pieces)
    shape = (N_CHIPS, m, n // N_CHIPS) if by_cols else (N_CHIPS, m // N_CHIPS, n)
    n_steps = s // tk

    def body(a_ref, *rest):
        b_refs, o_ref, acc = rest[:n_b], rest[n_b], rest[-1]

        @pl.when(pl.program_id(0) == 0)
        def _():
            acc[...] = jnp.zeros_like(acc)

        bv = b_refs[0][...] if n_b == 1 else jnp.concatenate([ref[...] for ref in b_refs], axis=1)
        for j in range(N_CHIPS):
            if by_cols:
                acc[j] += _dot(a_ref[...], bv[:, j * shape[2]:(j + 1) * shape[2]])
            else:
                acc[j] += _dot(a_ref[j * shape[1]:(j + 1) * shape[1], :], bv)

        @pl.when(pl.program_id(0) == n_steps - 1)
        def _():
            o_ref[...] = acc[...].astype(BF16)

    return pl.pallas_call(
        body, name=name, grid=(n_steps,),
        in_specs=[_spec((m, tk), lambda k: (0, k))] + [_spec((tk, p.shape[1]), lambda k: (k, 0)) for p in pieces],
        out_specs=[_spec(shape, lambda k: (0, 0, 0)), ANY_SPEC],
        out_shape=[jax.ShapeDtypeStruct(shape, BF16),
                   jax.ShapeDtypeStruct((N_DEV, shape[1] // 2, shape[2]), BF16)],
        scratch_shapes=[pltpu.VMEM(shape, F32)],
        compiler_params=_params(1))(a_t, *pieces)


def _rms_fwd(name, x, gain):
    s, d = x.shape
    t = ROW_TILE

    def body(x_ref, g_ref, h_ref, ht_ref):
        xv = x_ref[...]
        r = lax.rsqrt(jnp.mean(xv * xv, axis=1, keepdims=True) + NORM_EPS)
        h = xv * r * g_ref[...]
        h_ref[...] = h.astype(BF16)
        ht_ref[...] = h.T.astype(BF16)

    return pl.pallas_call(
        body, name=name, grid=(s // t,),
        in_specs=[_spec((t, d), lambda i: (i, 0)), _spec((1, d), lambda i: (0, 0))],
        out_specs=[_spec((t, d), lambda i: (i, 0)), _spec((d, t), lambda i: (0, i))],
        out_shape=[jax.ShapeDtypeStruct((s, d), BF16), jax.ShapeDtypeStruct((d, s), BF16)],
        compiler_params=_params(1),
    )(x, gain)


def _rope_tables():
    half = HEAD_DIM // 2
    inv_freq = ROPE_THETA ** (-jnp.arange(half, dtype=F32) / half)
    ang = jnp.arange(SEQ, dtype=jnp.int32).astype(F32)[:, None] * inv_freq[None, :]
    cos, sin = jnp.cos(ang), jnp.sin(ang)
    cos_t = jnp.concatenate([cos, cos, cos, cos], axis=1)
    sin_t = jnp.concatenate([-sin, sin, -sin, sin], axis=1)
    return cos_t, sin_t


def _swap_halves(x):
    lane = lax.broadcasted_iota(jnp.int32, x.shape, 1)
    first = (lane % HEAD_DIM) < (HEAD_DIM // 2)
    return jnp.where(first, pltpu.roll(x, 128 - HEAD_DIM // 2, 1), pltpu.roll(x, HEAD_DIM // 2, 1))


def _permuted_specs(t, width):
    specs = [_spec((t, width), lambda i: (i, 0))]
    for d in DILATIONS[1:]:
        specs.append(_spec((d, t // d, width), lambda i: (0, i, 0)))
    return specs


def _permuted_shapes(width, dtype):
    shapes = [jax.ShapeDtypeStruct((SEQ, width), dtype)]
    for d in DILATIONS[1:]:
        shapes.append(jax.ShapeDtypeStruct((d, SEQ // d, width), dtype))
    return shapes


def _attn_prep(name, proj, cos_t, sin_t):
    t = ROW_TILE
    w = ATTN_WIDTH

    def body(q_ref, k_ref, v_ref, cos_ref, sin_ref, *rest):
        outs, scr = rest[:9], rest[9]
        cosv, sinv = cos_ref[...], sin_ref[...]
        for a, (src, roped, scale) in enumerate(((q_ref, True, HEAD_DIM ** -0.5),
                                                 (k_ref, True, 1.0), (v_ref, False, 1.0))):
            o1, o4, o16 = outs[3 * a:3 * a + 3]
            for cb in range(w // 128):
                cols = slice(cb * 128, (cb + 1) * 128)
                val = src[:, cols]
                if roped:
                    val = (val * cosv + _swap_halves(val) * sinv) * scale
                scr[...] = val
                o1[:, cols] = val.astype(BF16)
                for o_ref, d in ((o4, 4), (o16, 16)):
                    for r in range(d):
                        o_ref[r, :, cols] = scr[pl.ds(r, t // d, stride=d), :].astype(BF16)

    out_specs = _permuted_specs(t, w) * 3
    out_shape = _permuted_shapes(w, BF16) * 3
    outs = pl.pallas_call(
        body, name=name, grid=(SEQ // t,),
        in_specs=[_spec((t, w), lambda i: (i, 0)), _spec((t, w), lambda i: (i, 1)),
                  _spec((t, w), lambda i: (i, 2)),
                  _spec((t, 128), lambda i: (i, 0)), _spec((t, 128), lambda i: (i, 0))],
        out_specs=out_specs, out_shape=out_shape,
        scratch_shapes=[pltpu.VMEM((t, 128), F32)],
        compiler_params=_params(1),
    )(proj, proj, proj, cos_t, sin_t)
    q, k, v = outs[0:3], outs[3:6], outs[6:9]
    flat = lambda arr: arr.reshape(SEQ, w)
    return [(flat(q[p]), flat(k[p]), flat(v[p])) for p in range(3)]


def _band_masks():
    row = lax.broadcasted_iota(jnp.int32, (2 * SPAN, 2 * SPAN), 0) % SPAN
    col = lax.broadcasted_iota(jnp.int32, (2 * SPAN, 2 * SPAN), 1)
    is_prev = col < SPAN
    band = (is_prev & (col >= row)) | (~is_prev & (col - SPAN <= row))
    head0 = lax.broadcasted_iota(jnp.int32, (SPAN, 128), 1) < HEAD_DIM
    return band, is_prev, head0


def _stack_heads(x, head0):
    zero = jnp.zeros_like(x)
    return jnp.concatenate([jnp.where(head0, x, zero), jnp.where(head0, zero, x)], axis=0)


ATTN_UNROLL = 16


def _for_each_block(block, seg_blocks):
    def trip(i, carry):
        for u in range(ATTN_UNROLL):
            static = seg_blocks <= ATTN_UNROLL
            block(i * ATTN_UNROLL + u, (u % seg_blocks == 0) if static else None)
        return carry

    lax.fori_loop(0, SEQ // SPAN // ATTN_UNROLL, trip, 0)


def _attn_fwd(name, q, k, v, seg_blocks):
    def body(q_ref, k_ref, v_ref, o_ref, lse_ref):
        band, is_prev, head0 = _band_masks()

        def block(b, first):
            cur = pl.ds(pl.multiple_of(b * SPAN, SPAN), SPAN)
            qs = _stack_heads(q_ref[cur, :], head0)
            if first is True:
                kcat, vcat, ok = k_ref[cur, :], v_ref[cur, :], band[:, SPAN:]
            else:
                prev = pl.ds(pl.multiple_of(jnp.maximum(b - 1, 0) * SPAN, SPAN), SPAN)
                kcat = jnp.concatenate([k_ref[prev, :], k_ref[cur, :]], axis=0)
                vcat = jnp.concatenate([v_ref[prev, :], v_ref[cur, :]], axis=0)
                ok = band if first is False else band & (((b % seg_blocks) != 0) | ~is_prev)
            s = jnp.where(ok, _dot_nt(qs, kcat), MASK_VALUE)
            m = jnp.max(s, axis=1, keepdims=True)
            p = jnp.exp(s - m)
            l = jnp.sum(p, axis=1, keepdims=True)
            pv = _dot(p, vcat) * (1.0 / l)
            lse = m + jnp.log(l)
            o_ref[cur, :] = jnp.where(head0, pv[:SPAN], pv[SPAN:])
            lse_ref[cur, :] = jnp.where(head0, lse[:SPAN], lse[SPAN:])

        _for_each_block(block, seg_blocks)

    col = _spec((SEQ, 128), lambda j: (0, j))
    return pl.pallas_call(
        body, name=name, grid=(ATTN_WIDTH // 128,),
        in_specs=[col, col, col], out_specs=[col, col],
        out_shape=[jax.ShapeDtypeStruct((SEQ, ATTN_WIDTH), F32)] * 2,
        compiler_params=_params(1),
    )(q, k, v)


def _unpermute(dst, src_ref, d, cols):
    n = dst.shape[0] // d
    for r in range(d):
        dst[pl.ds(r, n, stride=d), :] = src_ref[r, :, cols].astype(dst.dtype)


def _attn_merge(name, outs, lses, gain):
    t = ROW_TILE
    w = ATTN_WIDTH

    def body(o1, o4, o16, l1, l4, l16, g_ref, an_ref, ant_ref, attn_ref, lse_ref, so4, so16, sl4, sl16):
        for cb in range(w // 128):
            cols = slice(cb * 128, (cb + 1) * 128)
            _unpermute(so4, o4, 4, cols)
            _unpermute(so16, o16, 16, cols)
            _unpermute(sl4, l4, 4, cols)
            _unpermute(sl16, l16, 16, cols)
            la, lb, lc = l1[:, cols], sl4[...], sl16[...]
            m = jnp.maximum(jnp.maximum(la, lb), lc)
            ea, eb, ec = jnp.exp(la - m), jnp.exp(lb - m), jnp.exp(lc - m)
            tot = ea + eb + ec
            attn_ref[:, cols] = (ea * o1[:, cols] + eb * so4[...] + ec * so16[...]) / tot
            lse_ref[:, cols] = m + jnp.log(tot)
        attn = attn_ref[...]
        r = lax.rsqrt(jnp.mean(attn * attn, axis=1, keepdims=True) + NORM_EPS)
        an = attn * r * g_ref[...]
        an_ref[...] = an.astype(BF16)
        ant_ref[...] = an.T.astype(BF16)

    views = lambda arrs: [arrs[0], arrs[1].reshape(4, SEQ // 4, w), arrs[2].reshape(16, SEQ // 16, w)]
    row = _spec((t, w), lambda i: (i, 0))
    return pl.pallas_call(
        body, name=name, grid=(SEQ // t,),
        in_specs=_permuted_specs(t, w) * 2 + [_spec((1, w), lambda i: (0, 0))],
        out_specs=[row, _spec((w, t), lambda i: (0, i)), row, row],
        out_shape=[jax.ShapeDtypeStruct((SEQ, 2 * w), BF16), jax.ShapeDtypeStruct((2 * w, SEQ), BF16),
                   jax.ShapeDtypeStruct((SEQ, w), F32), jax.ShapeDtypeStruct((SEQ, w), F32)],
        scratch_shapes=[pltpu.VMEM((t, 128), F32)] * 4,
        compiler_params=_params(1),
    )(*views(outs), *views(lses), gain)


def _head_sum_matrix():
    i = np.arange(ATTN_WIDTH)
    return jnp.asarray((i[:, None] // HEAD_DIM) == (i[None, :] // HEAD_DIM), dtype=F32)


def _attn_bwd_prep(name, d_an, attn, lse, gain, head_sum):
    t = ROW_TILE
    w = ATTN_WIDTH

    def body(dan_ref, attn_ref, lse_ref, g_ref, hs_ref, *rest):
        (do1, do4, do16, dl1, dl4, dl16, ls4, ls16, dg_ref), (sdo, sdl, sls) = rest[:9], rest[9:]

        @pl.when(pl.program_id(0) == 0)
        def _():
            dg_ref[...] = jnp.zeros_like(dg_ref)

        attn = attn_ref[...]
        dan = dan_ref[...]
        r = lax.rsqrt(jnp.mean(attn * attn, axis=1, keepdims=True) + NORM_EPS)
        xhat = attn * r
        dg_ref[...] += jnp.sum(dan * xhat, axis=0, keepdims=True)
        dang = dan * g_ref[...]
        d_o = r * (dang - xhat * jnp.mean(dang * xhat, axis=1, keepdims=True))
        delta = jnp.dot(d_o * attn, hs_ref[...], preferred_element_type=F32,
                        precision=lax.Precision.HIGHEST)
        do1[...] = d_o.astype(BF16)
        dl1[...] = delta
        for cb in range(w // 128):
            cols = slice(cb * 128, (cb + 1) * 128)
            sdo[...] = d_o[:, cols]
            sdl[...] = delta[:, cols]
            sls[...] = lse_ref[:, cols]
            for d, o_do, o_dl, o_ls in ((4, do4, dl4, ls4), (16, do16, dl16, ls16)):
                for rr in range(d):
                    rows = pl.ds(rr, t // d, stride=d)
                    o_do[rr, :, cols] = sdo[rows, :].astype(BF16)
                    o_dl[rr, :, cols] = sdl[rows, :]
                    o_ls[rr, :, cols] = sls[rows, :]

    row = _spec((t, w), lambda i: (i, 0))
    perm = _permuted_specs(t, w)
    outs = pl.pallas_call(
        body, name=name, grid=(SEQ // t,),
        in_specs=[row, row, row, _spec((1, w), lambda i: (0, 0)), _spec((w, w), lambda i: (0, 0))],
        out_specs=perm + perm + perm[1:] + [_spec((1, w), lambda i: (0, 0))],
        out_shape=(_permuted_shapes(w, BF16) + _permuted_shapes(w, F32) + _permuted_shapes(w, F32)[1:]
                   + [jax.ShapeDtypeStruct((1, w), F32)]),
        scratch_shapes=[pltpu.VMEM((t, 128), F32)] * 3,
        compiler_params=_params(1),
    )(d_an, attn, lse, gain, head_sum)
    flat = lambda arr: arr.reshape(SEQ, w)
    d_out = [flat(a) for a in outs[0:3]]
    delta = [flat(a) for a in outs[3:6]]
    lses = [lse, flat(outs[6]), flat(outs[7])]
    return d_out, delta, lses, outs[8]


def _attn_bwd(name, q, k, v, d_out, delta, lse, seg_blocks):
    def body(q_ref, k_ref, v_ref, do_ref, dl_ref, lse_ref, dq_ref, dk_out, dv_out, dk_ref, dv_ref):
        band, is_prev, head0 = _band_masks()
        dk_ref[...] = jnp.zeros_like(dk_ref)
        dv_ref[...] = jnp.zeros_like(dv_ref)

        def per_head(x):
            return jnp.concatenate([x[:, 0:1], x[:, HEAD_DIM:HEAD_DIM + 1]], axis=0)

        def block(b, first):
            cur = pl.ds(pl.multiple_of(b * SPAN, SPAN), SPAN)
            qs = _stack_heads(q_ref[cur, :], head0)
            dos = _stack_heads(do_ref[cur, :], head0)
            if first is True:
                kcat, vcat, ok = k_ref[cur, :], v_ref[cur, :], band[:, SPAN:]
            else:
                prev = pl.ds(pl.multiple_of(jnp.maximum(b - 1, 0) * SPAN, SPAN), SPAN)
                kcat = jnp.concatenate([k_ref[prev, :], k_ref[cur, :]], axis=0)
                vcat = jnp.concatenate([v_ref[prev, :], v_ref[cur, :]], axis=0)
                ok = band if first is False else band & (((b % seg_blocks) != 0) | ~is_prev)
            p = jnp.where(ok, jnp.exp(_dot_nt(qs, kcat) - per_head(lse_ref[cur, :])), 0.0)
            ds = p * (_dot_nt(dos, vcat) - per_head(dl_ref[cur, :]))
            dq = _dot(ds, kcat)
            dq_ref[cur, :] = jnp.where(head0, dq[:SPAN], dq[SPAN:]).astype(BF16)
            dk = _dot_tn(ds, qs)
            dv = _dot_tn(p, dos)
            if first is not True:
                dk_ref[prev, :] += dk[:SPAN]
                dv_ref[prev, :] += dv[:SPAN]
            dk_ref[cur, :] += dk[-SPAN:]
            dv_ref[cur, :] += dv[-SPAN:]

        _for_each_block(block, seg_blocks)
        dk_out[...] = dk_ref[...].astype(BF16)
        dv_out[...] = dv_ref[...].astype(BF16)

    col = _spec((SEQ, 128), lambda j: (0, j))
    return pl.pallas_call(
        body, name=name, grid=(ATTN_WIDTH // 128,),
        in_specs=[col] * 6, out_specs=[col] * 3,
        out_shape=[jax.ShapeDtypeStruct((SEQ, ATTN_WIDTH), BF16)] * 3,
        scratch_shapes=[pltpu.VMEM((SEQ, 128), F32)] * 2,
        compiler_params=_params(1),
    )(q, k, v, d_out, delta, lse)


def _attn_bwd_post(name, grads, cos_t, sin_t):
    t = ROW_TILE
    w = ATTN_WIDTH

    def body(*refs):
        ins, cos_ref, sin_ref, out_ref, s4, s16 = refs[:9], refs[9], refs[10], refs[11], refs[12], refs[13]
        cosv, sinv = cos_ref[...], sin_ref[...]
        for a in range(3):
            g1, g4, g16 = ins[a], ins[3 + a], ins[6 + a]
            for cb in range(w // 128):
                cols = slice(cb * 128, (cb + 1) * 128)
                _unpermute(s4, g4, 4, cols)
                _unpermute(s16, g16, 16, cols)
                val = g1[:, cols].astype(F32) + s4[...] + s16[...]
                if a < 2:
                    val = val * cosv + _swap_halves(val * sinv)
                if a == 0:
                    val = val * (HEAD_DIM ** -0.5)
                out_ref[:, a * w + cb * 128:a * w + (cb + 1) * 128] = val.astype(BF16)

    views = []
    for p, d in enumerate(DILATIONS):
        for a in range(3):
            views.append(grads[p][a] if d == 1 else grads[p][a].reshape(d, SEQ // d, w))
    perm = _permuted_specs(t, w)
    in_specs = [perm[0]] * 3 + [perm[1]] * 3 + [perm[2]] * 3
    return pl.pallas_call(
        body, name=name, grid=(SEQ // t,),
        in_specs=in_specs + [_spec((t, 128), lambda i: (i, 0))] * 2,
        out_specs=_spec((t, 3 * w), lambda i: (i, 0)),
        out_shape=jax.ShapeDtypeStruct((SEQ, 3 * w), BF16),
        scratch_shapes=[pltpu.VMEM((t, 128), F32)] * 2,
        compiler_params=_params(1),
    )(*views, cos_t, sin_t)


N_LEVELS = 7
HGRN_PAIR = 2


def _hgrn_consts():
    c = CHUNK
    i = np.arange(c)[:, None]
    s = np.arange(c)[None, :]
    blocks = [s <= i]
    for lv in range(N_LEVELS):
        bs = c >> lv
        h = bs // 2
        m = (i // bs) * bs + h - 1
        second = (i % bs) >= h
        blocks.append((second & (s > m) & (s <= i)) | (~second & (s > i) & (s <= m)))
    blocks.append(s > i)
    stack = np.concatenate(blocks, axis=0).astype(np.float32)
    return jnp.asarray(stack, dtype=BF16), jnp.asarray(stack.T, dtype=BF16)


def _exact_dot(m01, x):
    hi = x.astype(BF16)
    lo = (x - hi.astype(F32)).astype(BF16)
    n = x.shape[1]
    full = jnp.dot(m01, jnp.concatenate([hi, lo], axis=1), preferred_element_type=F32)
    return full[:, :n] + full[:, n:]


def _hgrn_gates(qh, z, lb):
    sq = _sigmoid(qh)
    q = qh * sq * (HGRN_DIM ** -0.5)
    sig = _sigmoid(z)
    sigm = _sigmoid(-z)
    f = lb + (1.0 - lb) * sig
    k = (1.0 - lb) * sigm
    return q, k, f, sq, sig, sigm


def _level_masks(lv):
    row = lax.broadcasted_iota(jnp.int32, (CHUNK, CHUNK), 0)
    col = lax.broadcasted_iota(jnp.int32, (CHUNK, CHUNK), 1)
    shift = N_LEVELS - lv
    half = CHUNK >> (lv + 1)
    second = (row & half) != 0
    second_col = (col & half) != 0
    same = (row >> shift) == (col >> shift)
    return second, same & second & ~second_col, same & (second != second_col)


def _hgrn_fwd(name, proj, lb, gain, stack, mixed, mixed_t):
    t = ROW_TILE
    per = t // CHUNK
    n_rb = SEQ // t
    n_chunks = SEQ // CHUNK
    col0 = 3 * ATTN_WIDTH // 128
    pair_w = HGRN_PAIR * HGRN_DIM

    def body(q_ref, f_ref, i_ref, g_ref, lb_ref, gain_ref, stack_ref, mixed_in, mixed_t_in,
             rec_ref, rect_ref, o_ref, st_out, a_out, st):
        del mixed_in, mixed_t_in

        @pl.when(pl.program_id(1) == 0)
        def _():
            st[...] = jnp.zeros_like(st)

        row = lax.broadcasted_iota(jnp.int32, (CHUNK, CHUNK), 0)
        col = lax.broadcasted_iota(jnp.int32, (CHUNK, CHUNK), 1)
        for c, hh in [(c, hh) for c in range(per) for hh in range(HGRN_PAIR)]:
            rows = slice(c * CHUNK, (c + 1) * CHUNK)
            lanes = slice(hh * HGRN_DIM, (hh + 1) * HGRN_DIM)
            lbv = lb_ref[hh]
            qh, z, v, gh = q_ref[rows, lanes], f_ref[rows, lanes], i_ref[rows, lanes], g_ref[rows, lanes]
            q, k, f, _, _, _ = _hgrn_gates(qh, z, lbv)
            dec = _exact_dot(stack_ref[...], jnp.log(f))
            g = dec[0:CHUNK]
            to_end = dec[(N_LEVELS + 1) * CHUNK:(N_LEVELS + 2) * CHUNK]
            a = jnp.where(row == col, jnp.sum(q * k, axis=1, keepdims=True), 0.0)
            for lv in range(N_LEVELS):
                second, square, _ = _level_masks(lv)
                t = jnp.where(second, q, k) * jnp.exp(dec[(lv + 1) * CHUNK:(lv + 2) * CHUNK])
                a = a + jnp.where(square, _dot_nt(t, t), 0.0)
            st_prev = st[hh]
            st_out[hh, c] = st_prev
            a_out[hh, c] = a
            o = _dot(a, v) + _dot_nt(q * jnp.exp(g), st_prev)
            k_end = k * jnp.exp(to_end)
            st[hh] = st_prev * jnp.exp(g[CHUNK - 1:CHUNK, :]) + _dot(v.T, k_end)
            o_ref[rows, lanes] = o
            r = lax.rsqrt(jnp.mean(o * o, axis=1, keepdims=True) + NORM_EPS)
            rec = o * r * gain_ref[...] * (gh * _sigmoid(gh))
            rec_ref[rows, lanes] = rec.astype(BF16)
            rect_ref[lanes, rows] = rec.T.astype(BF16)

    def col_spec(tt):
        return _spec((t, pair_w), lambda h, rb: (rb, (col0 + HGRN_HEADS * tt) // HGRN_PAIR + h))

    chunk_spec = _spec((HGRN_PAIR, per, CHUNK, CHUNK), lambda h, rb: (h, rb, 0, 0))
    return pl.pallas_call(
        body, name=name, grid=(HGRN_HEADS // HGRN_PAIR, n_rb),
        in_specs=[col_spec(0), col_spec(1), col_spec(2), col_spec(3),
                  _spec((HGRN_PAIR, 1, HGRN_DIM), lambda h, rb: (h, 0, 0)),
                  _spec((1, HGRN_DIM), lambda h, rb: (0, 0)),
                  _spec(stack.shape, lambda h, rb: (0, 0)), ANY_SPEC, ANY_SPEC],
        out_specs=[_spec((t, pair_w), lambda h, rb: (rb, ATTN_WIDTH // pair_w + h)),
                   _spec((pair_w, t), lambda h, rb: (ATTN_WIDTH // pair_w + h, rb)),
                   _spec((t, pair_w), lambda h, rb: (rb, h)),
                   chunk_spec, chunk_spec],
        out_shape=[jax.ShapeDtypeStruct(mixed.shape, BF16),
                   jax.ShapeDtypeStruct(mixed_t.shape, BF16),
                   jax.ShapeDtypeStruct((SEQ, HGRN_WIDTH), F32),
                   jax.ShapeDtypeStruct((HGRN_HEADS, n_chunks, CHUNK, CHUNK), F32),
                   jax.ShapeDtypeStruct((HGRN_HEADS, n_chunks, CHUNK, CHUNK), F32)],
        scratch_shapes=[pltpu.VMEM((HGRN_PAIR, CHUNK, CHUNK), F32)],
        input_output_aliases={7: 0, 8: 1},
        compiler_params=_params(2),
    )(proj, proj, proj, proj, lb, gain, stack, mixed, mixed_t)


def _hgrn_bwd(name, proj, d_rec, o_pre, states, scores, lb, gain, stack, stack_t):
    t = ROW_TILE
    per = t // CHUNK
    n_rb = SEQ // t
    col0 = 3 * ATTN_WIDTH // 128
    pair_w = HGRN_PAIR * HGRN_DIM

    def body(q_ref, f_ref, i_ref, g_ref, drec_ref, o_ref, st_ref, a_ref, lb_ref, gain_ref,
             stack_ref, stack_t_ref, dq_ref, df_ref, di_ref, dg_ref, dlb_ref, dgain_ref, dst):
        @pl.when(pl.program_id(1) == 0)
        def _():
            dst[...] = jnp.zeros_like(dst)
            dlb_ref[...] = jnp.zeros_like(dlb_ref)
            dgain_ref[...] = jnp.zeros_like(dgain_ref)

        gain_v = gain_ref[...]
        row = lax.broadcasted_iota(jnp.int32, (CHUNK, CHUNK), 0)
        col = lax.broadcasted_iota(jnp.int32, (CHUNK, CHUNK), 1)
        for c, hh in [(c, hh) for c in reversed(range(per)) for hh in range(HGRN_PAIR)]:
            rows = slice(c * CHUNK, (c + 1) * CHUNK)
            lanes = slice(hh * HGRN_DIM, (hh + 1) * HGRN_DIM)
            lbv = lb_ref[hh]
            qh, z, v, gh = q_ref[rows, lanes], f_ref[rows, lanes], i_ref[rows, lanes], g_ref[rows, lanes]
            q, k, f, sq, sig, sigm = _hgrn_gates(qh, z, lbv)
            dec = _exact_dot(stack_ref[...], jnp.log(f))
            g = dec[0:CHUNK]
            to_end = dec[(N_LEVELS + 1) * CHUNK:(N_LEVELS + 2) * CHUNK]
            e_g = jnp.exp(g)
            e_end = jnp.exp(to_end)
            e_last = jnp.exp(g[CHUNK - 1:CHUNK, :])
            q_in = q * e_g
            k_end = k * e_end
            st_prev = st_ref[hh, c]
            a = a_ref[hh, c]
            dst_new = dst[hh]

            o = o_ref[rows, lanes]
            drec = drec_ref[rows, lanes]
            sg = _sigmoid(gh)
            r = lax.rsqrt(jnp.mean(o * o, axis=1, keepdims=True) + NORM_EPS)
            ohat = o * r
            d_gh = drec * (ohat * gain_v) * (sg * (1.0 + gh * (1.0 - sg)))
            d_on = drec * (gh * sg)
            dgain_ref[hh] += jnp.sum(d_on * ohat, axis=0, keepdims=True)
            d_ohat = d_on * gain_v
            d_o = r * (d_ohat - ohat * jnp.mean(d_ohat * ohat, axis=1, keepdims=True))

            d_sym = jnp.where(row >= col, _dot_nt(d_o, v), _dot_nt(v, d_o))
            d_v = _dot(a.T, d_o) + _dot_nt(k_end, dst_new)
            d_q_in = _dot(d_o, st_prev)
            d_k_end = _dot(v, dst_new)
            d_q = d_q_in * e_g
            d_k = d_k_end * e_end
            diag = jnp.sum(d_o * v, axis=1, keepdims=True)
            d_q = d_q + diag * k
            d_k = d_k + diag * q
            d_dec = [q_in * d_q_in]
            d_both, d_second = None, None
            for lv in range(N_LEVELS):
                e = jnp.exp(dec[(lv + 1) * CHUNK:(lv + 2) * CHUNK])
                second, _, mirrored = _level_masks(lv)
                t = jnp.where(second, q, k) * e
                d_t = _dot(jnp.where(mirrored, d_sym, 0.0), t)
                d_te = d_t * e
                d_both = d_te if d_both is None else d_both + d_te
                d_second = jnp.where(second, d_te, 0.0) if d_second is None else d_second + jnp.where(second, d_te, 0.0)
                d_dec.append(t * d_t)
            d_q = d_q + d_second
            d_k = d_k + (d_both - d_second)
            d_dec.append(k_end * d_k_end)
            flux = jnp.sum(dst_new * st_prev, axis=0, keepdims=True) * e_last
            d_lf = _exact_dot(stack_t_ref[...], jnp.concatenate(d_dec, axis=0)) + flux
            dst[hh] = dst_new * e_last + _dot(d_o.T, q_in)

            d_f = d_lf / f - d_k
            dlb_ref[hh] += jnp.sum(d_f * sigm, axis=0, keepdims=True)
            dq_ref[rows, lanes] = (d_q * (HGRN_DIM ** -0.5) * (sq * (1.0 + qh * (1.0 - sq)))).astype(BF16)
            df_ref[rows, lanes] = (d_f * (1.0 - lbv) * sig * sigm).astype(BF16)
            di_ref[rows, lanes] = d_v.astype(BF16)
            dg_ref[rows, lanes] = d_gh.astype(BF16)

    last = n_rb - 1

    def col_spec(tt):
        return _spec((t, pair_w), lambda h, rb: (last - rb, (col0 + HGRN_HEADS * tt) // HGRN_PAIR + h))

    head_col = _spec((t, pair_w), lambda h, rb: (last - rb, h))
    rec_col0 = (d_rec.shape[1] - HGRN_WIDTH) // pair_w
    d_rec_col = _spec((t, pair_w), lambda h, rb: (last - rb, rec_col0 + h))
    chunk_spec = _spec((HGRN_PAIR, per, CHUNK, CHUNK), lambda h, rb: (h, last - rb, 0, 0))
    vec_spec = _spec((HGRN_PAIR, 1, HGRN_DIM), lambda h, rb: (h, 0, 0))
    outs = pl.pallas_call(
        body, name=name, grid=(HGRN_HEADS // HGRN_PAIR, n_rb),
        in_specs=[col_spec(0), col_spec(1), col_spec(2), col_spec(3), d_rec_col, head_col,
                  chunk_spec, chunk_spec, vec_spec,
                  _spec((1, HGRN_DIM), lambda h, rb: (0, 0)),
                  _spec(stack.shape, lambda h, rb: (0, 0)), _spec(stack_t.shape, lambda h, rb: (0, 0))],
        out_specs=[head_col] * 4 + [vec_spec, vec_spec],
        out_shape=[jax.ShapeDtypeStruct((SEQ, HGRN_WIDTH), BF16)] * 4
                  + [jax.ShapeDtypeStruct((HGRN_HEADS, 1, HGRN_DIM), F32)] * 2,
        scratch_shapes=[pltpu.VMEM((HGRN_PAIR, CHUNK, CHUNK), F32)],
        compiler_params=_params(2),
    )(proj, proj, proj, proj, d_rec, o_pre, states, scores, lb, gain, stack, stack_t)
    return outs


ANY_SPEC = pl.BlockSpec(memory_space=pl.ANY)


def _my_place():
    return lax.axis_index("x"), lax.axis_index("y"), lax.axis_index("c")


def _other_chips(x, y):
    return [(1 - x, y), (x, 1 - y), (1 - x, 1 - y)]


def _remote(src, dst, send_sem, recv_sem, device):
    return pltpu.make_async_remote_copy(src_ref=src, dst_ref=dst, send_sem=send_sem, recv_sem=recv_sem,
                                        device_id=device, device_id_type=MESH)


def _staged_copies(srcs, dsts, stage, sems):
    loads = [pltpu.make_async_copy(srcs[i], stage[i], sems.at[i]) for i in range(len(srcs))]
    for cp in loads:
        cp.start()
    stores = []
    for i, cp in enumerate(loads):
        cp.wait()
        stores.append(pltpu.make_async_copy(stage[i], dsts[i], sems.at[i]))
        stores[-1].start()
    return stores


def _gather_weights(name, shards):
    n = len(shards)

    def body(*refs):
        ins, outs = refs[:n], refs[n:2 * n]
        ici_send, ici_recv, d2d_send, d2d_recv, local_sems = refs[2 * n:2 * n + 5]
        stage = refs[2 * n + 5:]
        x, y, c = _my_place()
        me = 2 * x + y
        chips = _other_chips(x, y)

        def half(i, which):
            h = ins[i].shape[0] // 2
            return pl.ds(which * h, h)

        sends = []
        for i in range(n):
            for j, (px, py) in enumerate(chips):
                sends.append(_remote(ins[i].at[half(i, c), :], outs[i].at[me, half(i, c), :],
                                     ici_send.at[3 * i + j], ici_recv.at[3 * i + j], (px, py, c)))
        for cp in sends:
            cp.start()
        local = _staged_copies(ins, [outs[i].at[me] for i in range(n)], stage, local_sems)
        for i in range(n):
            for j, (px, py) in enumerate(chips):
                landed = outs[i].at[2 * px + py, half(i, c), :]
                _remote(landed, landed, ici_send.at[3 * i + j], ici_recv.at[3 * i + j], (px, py, c)).wait_recv()
                forward = _remote(landed, landed, d2d_send.at[3 * i + j], d2d_recv.at[3 * i + j], (x, y, 1 - c))
                forward.start()
                sends.append(forward)
        for i in range(n):
            for j, (px, py) in enumerate(chips):
                other = outs[i].at[2 * px + py, half(i, 1 - c), :]
                _remote(other, other, d2d_send.at[3 * i + j], d2d_recv.at[3 * i + j], (x, y, 1 - c)).wait_recv()
        for cp in sends:
            cp.wait_send()
        for cp in local:
            cp.wait()

    return pl.pallas_call(
        body, name=name, in_specs=[ANY_SPEC] * n, out_specs=[ANY_SPEC] * n,
        out_shape=[jax.ShapeDtypeStruct((N_CHIPS,) + s.shape, s.dtype) for s in shards],
        scratch_shapes=([pltpu.SemaphoreType.DMA((3 * n,))] * 4 + [pltpu.SemaphoreType.DMA((n,))]
                        + [pltpu.VMEM(s.shape, s.dtype) for s in shards]),
        compiler_params=pltpu.CompilerParams(vmem_limit_bytes=VMEM_LIMIT),
    )(*shards)


HBM_SPEC = pl.BlockSpec(memory_space=pltpu.HBM)
SEM_SPEC = pl.BlockSpec(memory_space=pltpu.SEMAPHORE)
SPLIT_PARAMS = pltpu.CompilerParams(has_side_effects=pltpu.SideEffectType.DATAFLOW_SIDE_EFFECTING)
N_PEERS = {"gather": N_CHIPS - 1, "scatter": N_DEV - 1}


def _split_copies(ins, lands, send_sems, recv_sems, kind):
    x, y, c = _my_place()
    pairs = []
    for i in range(len(ins)):
        if kind == "gather":
            me = 2 * x + y
            for j, (px, py) in enumerate(_other_chips(x, y)):
                sems = (send_sems.at[3 * i + j], recv_sems.at[3 * i + j], (px, py, c))
                pairs.append((_remote(ins[i], lands[i].at[me], *sems),
                              _remote(ins[i], lands[i].at[2 * px + py], *sems)))
        else:
            me = 4 * x + 2 * y + c
            h = ins[i].shape[1] // 2
            for k in range(1, N_DEV):
                px, py, pc = (x + (k >> 2)) % 2, (y + ((k >> 1) & 1)) % 2, (c + (k & 1)) % 2
                src = ins[i].at[2 * px + py, pl.ds(pc * h, h), :]
                sems = (send_sems.at[7 * i + k - 1], recv_sems.at[7 * i + k - 1], (px, py, pc))
                pairs.append((_remote(src, lands[i].at[me], *sems),
                              _remote(src, lands[i].at[4 * px + 2 * py + pc], *sems)))
    return pairs


def _exchange_start(name, srcs, lands, kind, after=None):
    n = len(srcs)
    n_sems = N_PEERS[kind] * n
    extra = [] if after is None else [after]

    def body(*refs):
        ins, land_refs = refs[:n], refs[n:2 * n]
        send_sems, recv_sems = refs[2 * n + len(extra):2 * n + len(extra) + 2]
        token = refs[-1]
        for send, _ in _split_copies(ins, land_refs, send_sems, recv_sems, kind):
            send.start()
        token[...] = jnp.zeros_like(token)

    arrays = list(srcs) + list(lands)
    outs = pl.pallas_call(
        body, name=name,
        in_specs=[HBM_SPEC] * (2 * n) + [ANY_SPEC] * len(extra),
        out_shape=([pltpu.SemaphoreType.DMA((n_sems,))] * 2 + [pltpu.HBM(a.shape, a.dtype) for a in arrays]
                   + [jax.ShapeDtypeStruct((8, 128), F32)]),
        out_specs=[SEM_SPEC] * 2 + [HBM_SPEC] * (2 * n) + [pl.BlockSpec(memory_space=pltpu.VMEM)],
        input_output_aliases={i: 2 + i for i in range(2 * n)},
        compiler_params=SPLIT_PARAMS,
    )(*[pltpu.with_memory_space_constraint(a, pltpu.HBM) for a in arrays], *extra)
    return outs[:2], outs[2:2 + 2 * n], outs[-1]


def _exchange_wait(name, sems, passed, kind, after):
    n = len(passed) // 2

    def body(*refs):
        ins, land_refs = refs[:n], refs[n:2 * n]
        send_sems, recv_sems = refs[2 * n:2 * n + 2]
        for send, arrive in _split_copies(ins, land_refs, send_sems, recv_sems, kind):
            send.wait_send()
            arrive.wait_recv()

    outs = pl.pallas_call(
        body, name=name,
        in_specs=[HBM_SPEC] * (2 * n) + [SEM_SPEC] * 2 + [ANY_SPEC],
        out_shape=[pltpu.HBM(a.shape, a.dtype) for a in passed],
        out_specs=[HBM_SPEC] * (2 * n),
        input_output_aliases={i: i for i in range(2 * n)},
        compiler_params=SPLIT_PARAMS,
    )(*passed, *sems, after)
    return outs[:n], outs[n:]


def _own_slot(name, own, me):
    r, cc = own.shape
    th = min(r, 512)

    def body(me_ref, x_ref, o_ref):
        del me_ref
        o_ref[...] = x_ref[...]

    grid_spec = pltpu.PrefetchScalarGridSpec(
        num_scalar_prefetch=1, grid=(r // th,),
        in_specs=[pl.BlockSpec((th, cc), lambda i, me_ref: (i, 0))],
        out_specs=pl.BlockSpec((None, th, cc), lambda i, me_ref: (me_ref[0], i, 0)))
    return pl.pallas_call(
        body, name=name, grid_spec=grid_spec,
        out_shape=jax.ShapeDtypeStruct((N_CHIPS, r, cc), own.dtype), compiler_params=_params(1),
    )(me, own)


def _sum_devices(name, landed, own, place):
    n_dev, h, cc = landed.shape
    th = min(h, 256)
    nb = h // th

    def body(place_ref, l_ref, own_ref, o_ref):
        total = None
        for d in range(n_dev):
            piece = jnp.where(place_ref[0] == d, own_ref[...], l_ref[d]).astype(F32)
            total = piece if total is None else total + piece
        o_ref[...] = total

    grid_spec = pltpu.PrefetchScalarGridSpec(
        num_scalar_prefetch=1, grid=(nb,),
        in_specs=[pl.BlockSpec((n_dev, th, cc), lambda i, p: (0, i, 0)),
                  pl.BlockSpec((None, th, cc), lambda i, p: (p[1], p[2] * nb + i, 0))],
        out_specs=pl.BlockSpec((th, cc), lambda i, p: (i, 0)))
    return pl.pallas_call(
        body, name=name, grid_spec=grid_spec,
        out_shape=jax.ShapeDtypeStruct((h, cc), F32), compiler_params=_params(1),
    )(place, landed, own)


def _share_halves(name, halves):
    flat = [t for per_weight in halves for t in per_weight]
    n = len(flat)
    n_w = len(halves)

    def body(*refs):
        ins, outs = refs[:n], refs[n:n + n_w]
        send_sems, recv_sems, local_sems = refs[n + n_w:n + n_w + 3]
        stage = refs[n + n_w + 3:]
        x, y, c = _my_place()
        sends, own = [], []
        for i in range(n):
            w, l = divmod(i, DEPTH)
            h = ins[i].shape[0]
            own.append(outs[w].at[l, pl.ds(c * h, h), :])
            sends.append(_remote(ins[i], own[i], send_sems.at[i], recv_sems.at[i], (x, y, 1 - c)))
        for cp in sends:
            cp.start()
        local = _staged_copies(ins, own, stage, local_sems)
        for i in range(n):
            w, l = divmod(i, DEPTH)
            h = ins[i].shape[0]
            _remote(ins[i], outs[w].at[l, pl.ds((1 - c) * h, h), :], send_sems.at[i], recv_sems.at[i],
                    (x, y, 1 - c)).wait_recv()
        for cp in sends:
            cp.wait_send()
        for cp in local:
            cp.wait()

    return pl.pallas_call(
        body, name=name, in_specs=[ANY_SPEC] * n, out_specs=[ANY_SPEC] * n_w,
        out_shape=[jax.ShapeDtypeStruct((DEPTH, 2 * per_weight[0].shape[0], per_weight[0].shape[1]), F32)
                   for per_weight in halves],
        scratch_shapes=([pltpu.SemaphoreType.DMA((n,))] * 3 + [pltpu.VMEM(t.shape, t.dtype) for t in flat]),
        compiler_params=pltpu.CompilerParams(vmem_limit_bytes=VMEM_LIMIT),
    )(*flat)


def _all_reduce_small(pack, after):
    def body(p_ref, after_ref, o_ref, recv, send_sems, recv_sems):
        del after_ref
        x, y, c = _my_place()
        me = 4 * x + 2 * y + c
        recv[me] = p_ref[...]
        peers = []
        for k in range(1, N_DEV):
            px, py, pc = (x + (k >> 2)) % 2, (y + ((k >> 1) & 1)) % 2, (c + (k & 1)) % 2
            peers.append((px, py, pc))
        sends = [_remote(p_ref, recv.at[me], send_sems.at[k], recv_sems.at[k], peer)
                 for k, peer in enumerate(peers)]
        for cp in sends:
            cp.start()
        for k, (px, py, pc) in enumerate(peers):
            _remote(p_ref, recv.at[4 * px + 2 * py + pc], send_sems.at[k], recv_sems.at[k],
                    (px, py, pc)).wait_recv()
        for cp in sends:
            cp.wait_send()
        total = recv[0]
        for d in range(1, N_DEV):
            total = total + recv[d]
        o_ref[...] = total

    vmem = pl.BlockSpec(memory_space=pltpu.VMEM)
    return pl.pallas_call(
        body, name="all_reduce_small", in_specs=[vmem, ANY_SPEC], out_specs=vmem,
        out_shape=jax.ShapeDtypeStruct(pack.shape, F32),
        scratch_shapes=[pltpu.VMEM((N_DEV,) + pack.shape, F32),
                        pltpu.SemaphoreType.DMA((N_DEV - 1,)), pltpu.SemaphoreType.DMA((N_DEV - 1,))],
    )(pack, after)


def _adamw(name, w, g, m, v):
    r, cc = w.shape
    th = min(r, 256)

    def body(w_ref, g_ref, m_ref, v_ref, d_ref, m_out, v_out):
        gv = g_ref[...]
        m2 = ADAM_B1 * m_ref[...] + (1.0 - ADAM_B1) * gv
        v2 = ADAM_B2 * v_ref[...] + (1.0 - ADAM_B2) * (gv * gv)
        m_hat = m2 / (1.0 - ADAM_B1 ** ADAM_STEP)
        v_hat = v2 / (1.0 - ADAM_B2 ** ADAM_STEP)
        d_ref[...] = -ADAM_LR * (m_hat / (jnp.sqrt(v_hat) + ADAM_EPS) + ADAM_WD * w_ref[...])
        m_out[...] = m2
        v_out[...] = v2

    tile = _spec((th, cc), lambda i: (i, 0))
    return pl.pallas_call(
        body, name=name, grid=(r // th,), in_specs=[tile] * 4, out_specs=[tile] * 3,
        out_shape=[jax.ShapeDtypeStruct((r, cc), F32)] * 3, compiler_params=_params(1),
    )(w, g, m, v)


def _lower_bounds(lb_logits):
    p = jax.nn.softmax(lb_logits.astype(F32), axis=0)
    return jnp.cumsum(p, axis=0) - p[0]


def _layer_forward(l, stream, small, weights, consts, next_gain=None, loss=None, after=None):
    win, rest = weights
    cos_t, sin_t, stack, _, _ = consts
    tm = MM_TILE
    x_in, h, h_t = stream
    saved = {"x_in": x_in}

    proj = _mm_pieces(f"proj{l}", h, win, False, tm, after=after)
    saved.update(h_t=h_t, proj=proj)

    qkv = _attn_prep(f"attn_prep{l}", proj, cos_t, sin_t)
    outs, lses = [], []
    for p, d in enumerate(DILATIONS):
        o, lse = _attn_fwd(f"attn_fwd{l}_{d}", *qkv[p], SEQ // d // SPAN)
        outs.append(o)
        lses.append(lse)
    mixed, mixed_t, attn, lse = _attn_merge(f"attn_merge{l}", outs, lses, small["attn_out_gain"][l][None, :])
    saved.update(qkv=qkv, attn=attn, lse=lse)

    lb3 = small["lower"][l].reshape(HGRN_HEADS, 1, HGRN_DIM)
    mixed, mixed_t, o_pre, states, scores = _hgrn_fwd(f"hgrn_fwd{l}", proj, lb3, small["hgrn_out_gain"][l][None, :],
                                                      stack, mixed, mixed_t)
    wo, wu, wd = rest(mixed)
    saved.update(mixed_t=mixed_t, o_pre=o_pre, states=states, scores=scores, lb3=lb3, weights=(win, wo, wu, wd))

    x_mid, h2, h2_t = _mm_accum(f"out_proj{l}", mixed, wo, False, tm, x_in, next_gain=small["norm_mlp"][l][None, :])
    saved["x_mid"] = x_mid

    a, relu_u, a_t = _mm_pieces(f"up{l}", h2, wu, False, tm, epilogue="relu2")
    new_stream = tuple(_mm_accum(f"down{l}", a, wd, False, tm, x_mid, next_gain=next_gain, loss=loss))
    saved.update(h2_t=h2_t, relu_u=relu_u, a_t=a_t)
    return new_stream, saved


def _layer_backward(l, dx, saved, small, consts, on_grads, after=None):
    win, wo, wu, wd = saved["weights"]
    cos_t, sin_t, stack, stack_t, head_sum = consts
    tm = MM_TILE

    dx, dx_b = dx
    du = _mm_pieces(f"d_u{l}", dx_b, wd, True, tm, epilogue="relu2_grad", extra=saved["relu_u"], after=after)
    d_wd = _mm_dw(f"d_wdown{l}", saved["a_t"], dx_b, False, tm)
    dxm, dxm_b, dg_mlp = _mm_accum(f"d_h2_{l}", du, wu, True, tm, dx,
                                   norm=(saved["x_mid"], small["norm_mlp"][l][None, :]))
    d_wu = _mm_dw(f"d_wup{l}", saved["h2_t"], du, True, tm)
    d_wo = _mm_dw(f"d_wout{l}", saved["mixed_t"], dxm_b, False, tm)
    after_early = on_grads(l, "early", (d_wu, d_wd, d_wo))

    d_mixed = _mm_pieces(f"d_mixed{l}", dxm_b, wo, True, tm, after=after_early)
    d_rec = d_mixed

    d_out, delta, lses, dg_attn = _attn_bwd_prep(f"attn_bwd_prep{l}", d_mixed, saved["attn"], saved["lse"],
                                                 small["attn_out_gain"][l][None, :], head_sum)
    grads = []
    for p, d in enumerate(DILATIONS):
        grads.append(_attn_bwd(f"attn_bwd{l}_{d}", *saved["qkv"][p], d_out[p], delta[p], lses[p],
                               SEQ // d // SPAN))
    dp_attn = _attn_bwd_post(f"attn_bwd_post{l}", grads, cos_t, sin_t)

    dq_h, df_h, di_h, dg_h, d_lower, dg_hgrn = _hgrn_bwd(
        f"hgrn_bwd{l}", saved["proj"], d_rec, saved["o_pre"], saved["states"], saved["scores"],
        saved["lb3"], small["hgrn_out_gain"][l][None, :], stack, stack_t)
    dproj = [dp_attn, dq_h, df_h, di_h, dg_h]

    d_win = _mm_dw(f"d_win{l}", saved["h_t"], dproj, True, tm)
    after_last = on_grads(l, "last", (d_win,))
    dx_in, dx_in_b, dg_mix = _mm_accum(f"d_h{l}", dproj, win, True, tm, dxm,
                                       norm=(saved["x_in"], small["norm_mix"][l][None, :]), after=after_last)

    small_grads = {"norm_mix": dg_mix[0], "attn_out_gain": dg_attn[0],
                   "lower": d_lower.reshape(HGRN_WIDTH),
                   "hgrn_out_gain": jnp.sum(dg_hgrn, axis=0).reshape(HGRN_DIM), "norm_mlp": dg_mlp[0]}
    return (dx_in, dx_in_b), after_last, small_grads


def _local_step(xs, target, small, get_weights, on_grads):
    consts = _rope_tables() + _hgrn_consts() + (_head_sum_matrix(),)
    stream = (xs,) + tuple(_rms_fwd("norm_mix0", xs, small["norm_mix"][0][None, :]))
    saved = []
    for l in range(DEPTH):
        w, after = get_weights(l, stream[0])
        if l + 1 < DEPTH:
            stream, s = _layer_forward(l, stream, small, w, consts, next_gain=small["norm_mix"][l + 1][None, :],
                                       after=after)
        else:
            stream, s = _layer_forward(l, stream, small, w, consts, loss=(small["norm_final"][None, :], target),
                                       after=after)
        saved.append(s)
    dx_f, dx_b, dg_final, loss = stream
    dx = (dx_f, dx_b)
    small_grads = [None] * DEPTH
    after = None
    for l in reversed(range(DEPTH)):
        dx, after, small_grads[l] = _layer_backward(l, dx, saved[l], small, consts, on_grads, after=after)
    return loss, dx[0], dg_final[0], small_grads


def _pack_small(norm_mix, attn_out_gain, lb, hgrn_out_gain, norm_mlp, norm_final, last_row):
    rows = [norm_mix, attn_out_gain.reshape(1, D_MODEL), lb.reshape(1, D_MODEL),
            jnp.pad(hgrn_out_gain.reshape(1, DEPTH * HGRN_DIM), ((0, 0), (0, D_MODEL - DEPTH * HGRN_DIM))),
            norm_mlp, norm_final.reshape(1, D_MODEL), last_row.reshape(1, D_MODEL)]
    pack = jnp.concatenate(rows, axis=0)
    return jnp.pad(pack, ((0, PACK_ROWS - pack.shape[0]), (0, 0)))


def _unpack_small(pack):
    return (pack[0:2], pack[2].reshape(DEPTH, ATTN_WIDTH), pack[3].reshape(DEPTH, HGRN_WIDTH),
            pack[4, :DEPTH * HGRN_DIM].reshape(DEPTH, HGRN_DIM), pack[5:7], pack[7], pack[8])


def kernel(x, norm_mix, w_in, attn_out_gain, hgrn_lb_logits, hgrn_out_gain, w_out, norm_mlp, w_up, w_down, norm_final, loss_target, m_norm_mix, m_w_in, m_attn_out_gain, m_hgrn_lb_logits, m_hgrn_out_gain, m_w_out, m_norm_mlp, m_w_up, m_w_down, m_norm_final, v_norm_mix, v_w_in, v_attn_out_gain, v_hgrn_lb_logits, v_hgrn_out_gain, v_w_out, v_norm_mlp, v_w_up, v_w_down, v_norm_final):
    lower, lower_vjp = jax.vjp(_lower_bounds, hgrn_lb_logits)
    small = {"norm_mix": norm_mix, "attn_out_gain": attn_out_gain, "lower": lower,
             "hgrn_out_gain": hgrn_out_gain, "norm_mlp": norm_mlp, "norm_final": norm_final}
    big_w = (w_in, w_out, w_up, w_down)

    x_pos, y_pos, core = lax.axis_index("x"), lax.axis_index("y"), lax.axis_index("c")
    me = (2 * x_pos + y_pos).astype(jnp.int32).reshape(1)
    place = jnp.stack([4 * x_pos + 2 * y_pos + core, 2 * x_pos + y_pos, core]).astype(jnp.int32)
    shards = [[w[l].astype(BF16) for w in big_w] for l in range(DEPTH)]
    in_flight = {}

    def start_gather(name, some, after):
        lands = [_own_slot(f"own_{name}_{i}", s, me) for i, s in enumerate(some)]
        sems, passed, token = _exchange_start(f"start_{name}", some, lands, "gather", after)
        in_flight[name] = (sems, passed)
        return token

    def finish_gather(name, after):
        return _exchange_wait(f"wait_{name}", *in_flight.pop(name), "gather", after)[1]

    def get_weights(l, stream):
        if l == 0:
            (win,) = _gather_weights("gather_w_in0", shards[0][:1])
            token = start_gather("gather_rest0", shards[0][1:], win)
            token = start_gather("gather_w_in1", shards[1][:1], token)
            token = start_gather("gather_rest1", shards[1][1:], token)
            return (win, lambda after: finish_gather("gather_rest0", after)), token
        (win,) = finish_gather("gather_w_in1", stream)
        return (win, lambda after: finish_gather("gather_rest1", after)), None

    reduced = {}

    def start_exchange(name, grads):
        srcs, lands = [g for g, _ in grads], [land for _, land in grads]
        sems, passed, token = _exchange_start(f"start_{name}", srcs, lands, "scatter")
        in_flight[name] = (sems, passed)
        return token

    def finish_exchange(name, after):
        own, landed = _exchange_wait(f"wait_{name}", *in_flight.pop(name), "scatter", after)
        return [_sum_devices(f"sum_{name}_{i}", p, g, place) for i, (p, g) in enumerate(zip(landed, own))]

    def on_grads(l, group, grads):
        token = start_exchange(f"{group}{l}", grads)
        if (l, group) == (0, "early"):
            reduced[(1, "early")] = finish_exchange("early1", token)
            reduced[(1, "last")] = finish_exchange("last1", token)
        if (l, group) == (0, "last"):
            reduced[(0, "early")] = finish_exchange("early0", token)
        return token

    loss, dx, dg_final, sg = _local_step(x[0], loss_target[0], small, get_weights, on_grads)

    big_m = (m_w_in, m_w_out, m_w_up, m_w_down)
    big_v = (v_w_in, v_w_out, v_w_up, v_w_down)
    names = ("w_in", "w_out", "w_up", "w_down")
    big_g, big_delta, big_new_m, big_new_v = [None] * 4, [None] * 4, [None] * 4, [None] * 4

    def finish_weights(group, which):
        whole = _share_halves(f"share_{group}", [[reduced[(l, group)][i] for l in range(DEPTH)]
                                                 for i in range(len(which))])
        for i, w in enumerate(which):
            shape = big_w[w].shape
            flat = lambda arr: arr.reshape(shape[0] * shape[1], shape[2])
            d, m2, v2 = _adamw(f"adamw_{names[w]}", flat(big_w[w]), flat(whole[i]), flat(big_m[w]), flat(big_v[w]))
            big_g[w], big_delta[w] = whole[i], d.reshape(shape)
            big_new_m[w], big_new_v[w] = m2.reshape(shape), v2.reshape(shape)

    finish_weights("early", (2, 3, 1))
    reduced[(0, "last")] = finish_exchange("last0", big_delta[3])
    finish_weights("last", (0,))

    stack2 = lambda key: jnp.stack([sg[l][key] for l in range(DEPTH)])
    pack = _pack_small(stack2("norm_mix"), stack2("attn_out_gain"), stack2("lower"), stack2("hgrn_out_gain"),
                       stack2("norm_mlp"), dg_final, jnp.broadcast_to(loss[0, 0], (D_MODEL,)))
    g_mix, g_attn, g_lower, g_hgrn, g_mlp, g_final, loss_row = _unpack_small(_all_reduce_small(pack, big_delta[0]))
    (g_logits,) = lower_vjp(g_lower)

    zeros_row = jnp.zeros((D_MODEL,), F32)
    small_w = (norm_mix, attn_out_gain, hgrn_lb_logits, hgrn_out_gain, norm_mlp, norm_final)
    small_m = (m_norm_mix, m_attn_out_gain, m_hgrn_lb_logits, m_hgrn_out_gain, m_norm_mlp, m_norm_final)
    small_v = (v_norm_mix, v_attn_out_gain, v_hgrn_lb_logits, v_hgrn_out_gain, v_norm_mlp, v_norm_final)
    small_g = (g_mix, g_attn, g_logits, g_hgrn, g_mlp, g_final)
    packs = [_pack_small(*t, zeros_row) for t in (small_w, small_g, small_m, small_v)]
    small_delta, small_new_m, small_new_v = [_unpack_small(p)[:6] for p in _adamw("adamw_small", *packs)]

    def ordered(small6, big4):
        mix, attn, lbl, hg, mlp, fin = small6
        return (mix, big4[0], attn, lbl, hg, big4[1], mlp, big4[2], big4[3], fin)

    return ((loss_row[0], dx[None]) + ordered(small_g, big_g) + ordered(small_delta, big_delta)
            + ordered(small_new_m, big_new_m) + ordered(small_new_v, big_new_v))
```

```python
import numpy as np
import jax
import jax.numpy as jnp
from jax import lax
from jax.experimental import pallas as pl
from jax.experimental.pallas import tpu as pltpu

F32 = jnp.float32
BF16 = jnp.bfloat16
MESH = pl.DeviceIdType.MESH

SEQ = 4096
D_MODEL = 1024
DEPTH = 2
ATTN_WIDTH = 512
HEAD_DIM = 64
HGRN_HEADS = 4
HGRN_DIM = 128
HGRN_WIDTH = 512
IN_W = 3584
MLP_HIDDEN = 4096
N_CHIPS = 4
N_DEV = 8
DILATIONS = (1, 4, 16)
SPAN = 128
ROPE_THETA = 10000.0
NORM_EPS = 1e-6
MASK_VALUE = -1e30
CHUNK = 128
ROW_TILE = 512
MM_TILE = 512
VMEM_LIMIT = 52 * 1024 * 1024

ADAM_LR = 0.001
ADAM_B1 = 0.9
ADAM_B2 = 0.999
ADAM_EPS = 1e-08
ADAM_WD = 0.01
ADAM_STEP = 10

PACK_ROWS = 16


def _params(n_axes):
    return pltpu.CompilerParams(dimension_semantics=("arbitrary",) * n_axes,
                                vmem_limit_bytes=VMEM_LIMIT)


def _dot(a, b):
    return jnp.dot(a.astype(BF16), b.astype(BF16), preferred_element_type=F32)


def _dot_nt(a, b):
    return lax.dot_general(a.astype(BF16), b.astype(BF16), (((1,), (1,)), ((), ())),
                           preferred_element_type=F32)


def _dot_tn(a, b):
    return lax.dot_general(a.astype(BF16), b.astype(BF16), (((0,), (0,)), ((), ())),
                           preferred_element_type=F32)


def _sigmoid(x):
    return 1.0 / (1.0 + jnp.exp(-x))


def _spec(shape, index_map):
    return pl.BlockSpec(shape, index_map)


def _mm_pieces(name, a, w, nt, tm, epilogue="none", extra=None, after=None):
    s = a.shape[0]
    pw = w.shape[1] if nt else w.shape[2]
    width = N_CHIPS * pw

    def body(a_ref, w_ref, *rest):
        e_ref = rest[0] if extra is not None else None
        outs = rest[-3:] if epilogue == "relu2" else rest[-1:]
        av = a_ref[...].astype(BF16)
        for j in range(N_CHIPS):
            cols = slice(j * pw, (j + 1) * pw)
            r = _dot_nt(av, w_ref[j]) if nt else _dot(av, w_ref[j])
            if epilogue == "relu2":
                relu = jnp.maximum(r, 0.0)
                r = relu * relu
                outs[1][:, cols] = relu.astype(BF16)
                outs[2][cols, :] = r.T.astype(BF16)
            elif epilogue == "relu2_grad":
                r = r * (2.0 * e_ref[:, cols].astype(F32))
            outs[0][:, cols] = r.astype(outs[0].dtype)

    row = lambda width_: _spec((tm, width_), lambda i: (i, 0))
    in_specs = [row(a.shape[1]), _spec(w.shape, lambda i: (0, 0, 0))]
    args = [a, w]
    if extra is not None:
        in_specs.append(row(width))
        args.append(extra)
    if after is not None:
        in_specs.append(pl.BlockSpec(memory_space=pl.ANY))
        args.append(after)
    if epilogue == "relu2":
        out_specs = [row(width), row(width), _spec((width, tm), lambda i: (0, i))]
        out_shape = [jax.ShapeDtypeStruct((s, width), BF16)] * 2 + [jax.ShapeDtypeStruct((width, s), BF16)]
    else:
        out_specs = row(width)
        out_shape = jax.ShapeDtypeStruct((s, width), BF16 if epilogue == "relu2_grad" else F32)
    return pl.pallas_call(body, name=name, grid=(s // tm,), in_specs=in_specs, out_specs=out_specs,
                          out_shape=out_shape, compiler_params=_params(1))(*args)


def _mm_accum(name, a, w, nt, tm, resid, norm=None, after=None, next_gain=None, loss=None):
    pieces = list(a) if isinstance(a, (list, tuple)) else [a]
    n_a = len(pieces)
    s = pieces[0].shape[0]
    pk = w.shape[2] if nt else w.shape[1]
    d = w.shape[1] if nt else w.shape[2]

    def body(*refs):
        a_refs, w_ref, resid_ref, rest = refs[:n_a], refs[n_a], refs[n_a + 1], refs[n_a + 2:]
        av = a_refs[0][...] if n_a == 1 else jnp.concatenate([ref[...] for ref in a_refs], axis=1)
        r = None
        for j in range(N_CHIPS):
            piece = av[:, j * pk:(j + 1) * pk].astype(BF16)
            term = _dot_nt(piece, w_ref[j]) if nt else _dot(piece, w_ref[j])
            r = term if r is None else r + term
        if loss is not None:
            g_ref, t_ref = rest[:2]
            dx_ref, dxb_ref, dg_ref, loss_ref, acc = rest[-5:]
            i = pl.program_id(0)

            @pl.when(i == 0)
            def _():
                dg_ref[...] = jnp.zeros_like(dg_ref)
                acc[...] = jnp.zeros_like(acc)

            xv = r + resid_ref[...]
            g = g_ref[...]
            rs = lax.rsqrt(jnp.mean(xv * xv, axis=1, keepdims=True) + NORM_EPS)
            xhat = xv * rs
            err = xhat * g - t_ref[...]
            acc[...] += jnp.sum(err * err, axis=0, keepdims=True)
            dy = err * (1.0 / d)
            dyg = dy * g
            dx = rs * (dyg - xhat * jnp.mean(dyg * xhat, axis=1, keepdims=True))
            dx_ref[...] = dx
            dxb_ref[...] = dx.astype(BF16)
            dg_ref[...] += jnp.sum(dy * xhat, axis=0, keepdims=True)

            @pl.when(i == s // tm - 1)
            def _():
                total = jnp.sum(acc[...], axis=1, keepdims=True) * (0.5 / d)
                loss_ref[...] = jnp.broadcast_to(total, loss_ref.shape)

            return
        if norm is None and next_gain is None:
            rest[-1][...] = r + resid_ref[...]
            return
        if norm is None:
            g_ref = rest[0]
            x_out, h_out, ht_out = rest[-3:]
            xv = r + resid_ref[...]
            x_out[...] = xv
            h = xv * lax.rsqrt(jnp.mean(xv * xv, axis=1, keepdims=True) + NORM_EPS) * g_ref[...]
            h_out[...] = h.astype(BF16)
            ht_out[...] = h.T.astype(BF16)
            return
        x_ref, g_ref = rest[:2]
        dx_ref, dxb_ref, dg_ref = rest[-3:]

        @pl.when(pl.program_id(0) == 0)
        def _():
            dg_ref[...] = jnp.zeros_like(dg_ref)

        xv = x_ref[...]
        rs = lax.rsqrt(jnp.mean(xv * xv, axis=1, keepdims=True) + NORM_EPS)
        xhat = xv * rs
        rg = r * g_ref[...]
        dx = resid_ref[...] + rs * (rg - xhat * jnp.mean(rg * xhat, axis=1, keepdims=True))
        dx_ref[...] = dx
        dxb_ref[...] = dx.astype(BF16)
        dg_ref[...] += jnp.sum(r * xhat, axis=0, keepdims=True)

    row = lambda width: _spec((tm, width), lambda i: (i, 0))
    in_specs = [row(p.shape[1]) for p in pieces] + [_spec(w.shape, lambda i: (0, 0, 0)), row(d)]
    args = pieces + [w, resid]
    scratch = []
    if loss is not None:
        in_specs += [_spec((1, d), lambda i: (0, 0)), row(d)]
        args += list(loss)
        out_specs = [row(d), row(d), _spec((1, d), lambda i: (0, 0)), _spec((1, 128), lambda i: (0, 0))]
        out_shape = [jax.ShapeDtypeStruct((s, d), F32), jax.ShapeDtypeStruct((s, d), BF16),
                     jax.ShapeDtypeStruct((1, d), F32), jax.ShapeDtypeStruct((1, 128), F32)]
        scratch = [pltpu.VMEM((1, d), F32)]
    elif norm is None and next_gain is None:
        out_specs, out_shape = row(d), jax.ShapeDtypeStruct((s, d), F32)
    elif norm is None:
        in_specs.append(_spec((1, d), lambda i: (0, 0)))
        args.append(next_gain)
        out_specs = [row(d), row(d), _spec((d, tm), lambda i: (0, i))]
        out_shape = [jax.ShapeDtypeStruct((s, d), F32), jax.ShapeDtypeStruct((s, d), BF16),
                     jax.ShapeDtypeStruct((d, s), BF16)]
    else:
        in_specs += [row(d), _spec((1, d), lambda i: (0, 0))]
        args += list(norm)
        out_specs = [row(d), row(d), _spec((1, d), lambda i: (0, 0))]
        out_shape = [jax.ShapeDtypeStruct((s, d), F32), jax.ShapeDtypeStruct((s, d), BF16),
                     jax.ShapeDtypeStruct((1, d), F32)]
    if after is not None:
        in_specs.append(pl.BlockSpec(memory_space=pl.ANY))
        args.append(after)
    return pl.pallas_call(body, name=name, grid=(s // tm,), in_specs=in_specs, out_specs=out_specs,
                          out_shape=out_shape, scratch_shapes=scratch, compiler_params=_params(1))(*args)


def _mm_dw(name, a_t, b, by_cols, tk):
    pieces = list(b) if isinstance(b, (list, tuple)) else [b]
    n_b = len(pieces)
    m, s = a_t.shape
    n = sum(p.shape[1] for p in pieces)
    shape = (N_CHIPS, m, n // N_CHIPS) if by_cols else (N_CHIPS, m // N_CHIPS, n)
    n_steps = s // tk

    def body(a_ref, *rest):
        b_refs, o_ref, acc = rest[:n_b], rest[n_b], rest[-1]

        @pl.when(pl.program_id(0) == 0)
        def _():
            acc[...] = jnp.zeros_like(acc)

        bv = b_refs[0][...] if n_b == 1 else jnp.concatenate([ref[...] for ref in b_refs], axis=1)
        for j in range(N_CHIPS):
            if by_cols:
                acc[j] += _dot(a_ref[...], bv[:, j * shape[2]:(j + 1) * shape[2]])
            else:
                acc[j] += _dot(a_ref[j * shape[1]:(j + 1) * shape[1], :], bv)

        @pl.when(pl.program_id(0) == n_steps - 1)
        def _():
            o_ref[...] = acc[...].astype(BF16)

    return pl.pallas_call(
        body, name=name, grid=(n_steps,),
        in_specs=[_spec((m, tk), lambda k: (0, k))] + [_spec((tk, p.shape[1]), lambda k: (k, 0)) for p in pieces],
        out_specs=[_spec(shape, lambda k: (0, 0, 0)), ANY_SPEC],
        out_shape=[jax.ShapeDtypeStruct(shape, BF16),
                   jax.ShapeDtypeStruct((N_DEV, shape[1] // 2, shape[2]), BF16)],
        scratch_shapes=[pltpu.VMEM(shape, F32)],
        compiler_params=_params(1))(a_t, *pieces)


def _rms_fwd(name, x, gain):
    s, d = x.shape
    t = ROW_TILE

    def body(x_ref, g_ref, h_ref, ht_ref):
        xv = x_ref[...]
        r = lax.rsqrt(jnp.mean(xv * xv, axis=1, keepdims=True) + NORM_EPS)
        h = xv * r * g_ref[...]
        h_ref[...] = h.astype(BF16)
        ht_ref[...] = h.T.astype(BF16)

    return pl.pallas_call(
        body, name=name, grid=(s // t,),
        in_specs=[_spec((t, d), lambda i: (i, 0)), _spec((1, d), lambda i: (0, 0))],
        out_specs=[_spec((t, d), lambda i: (i, 0)), _spec((d, t), lambda i: (0, i))],
        out_shape=[jax.ShapeDtypeStruct((s, d), BF16), jax.ShapeDtypeStruct((d, s), BF16)],
        compiler_params=_params(1),
    )(x, gain)


def _rope_tables():
    half = HEAD_DIM // 2
    inv_freq = ROPE_THETA ** (-jnp.arange(half, dtype=F32) / half)
    ang = jnp.arange(SEQ, dtype=jnp.int32).astype(F32)[:, None] * inv_freq[None, :]
    cos, sin = jnp.cos(ang), jnp.sin(ang)
    cos_t = jnp.concatenate([cos, cos, cos, cos], axis=1)
    sin_t = jnp.concatenate([-sin, sin, -sin, sin], axis=1)
    return cos_t, sin_t


def _swap_halves(x):
    lane = lax.broadcasted_iota(jnp.int32, x.shape, 1)
    first = (lane % HEAD_DIM) < (HEAD_DIM // 2)
    return jnp.where(first, pltpu.roll(x, 128 - HEAD_DIM // 2, 1), pltpu.roll(x, HEAD_DIM // 2, 1))


def _permuted_specs(t, width):
    specs = [_spec((t, width), lambda i: (i, 0))]
    for d in DILATIONS[1:]:
        specs.append(_spec((d, t // d, width), lambda i: (0, i, 0)))
    return specs


def _permuted_shapes(width, dtype):
    shapes = [jax.ShapeDtypeStruct((SEQ, width), dtype)]
    for d in DILATIONS[1:]:
        shapes.append(jax.ShapeDtypeStruct((d, SEQ // d, width), dtype))
    return shapes


def _attn_prep(name, proj, cos_t, sin_t):
    t = ROW_TILE
    w = ATTN_WIDTH

    def body(q_ref, k_ref, v_ref, cos_ref, sin_ref, *rest):
        outs, scr = rest[:9], rest[9]
        cosv, sinv = cos_ref[...], sin_ref[...]
        for a, (src, roped, scale) in enumerate(((q_ref, True, HEAD_DIM ** -0.5),
                                                 (k_ref, True, 1.0), (v_ref, False, 1.0))):
            o1, o4, o16 = outs[3 * a:3 * a + 3]
            for cb in range(w // 128):
                cols = slice(cb * 128, (cb + 1) * 128)
                val = src[:, cols]
                if roped:
                    val = (val * cosv + _swap_halves(val) * sinv) * scale
                scr[...] = val
                o1[:, cols] = val.astype(BF16)
                for o_ref, d in ((o4, 4), (o16, 16)):
                    for r in range(d):
                        o_ref[r, :, cols] = scr[pl.ds(r, t // d, stride=d), :].astype(BF16)

    out_specs = _permuted_specs(t, w) * 3
    out_shape = _permuted_shapes(w, BF16) * 3
    outs = pl.pallas_call(
        body, name=name, grid=(SEQ // t,),
        in_specs=[_spec((t, w), lambda i: (i, 0)), _spec((t, w), lambda i: (i, 1)),
                  _spec((t, w), lambda i: (i, 2)),
                  _spec((t, 128), lambda i: (i, 0)), _spec((t, 128), lambda i: (i, 0))],
        out_specs=out_specs, out_shape=out_shape,
        scratch_shapes=[pltpu.VMEM((t, 128), F32)],
        compiler_params=_params(1),
    )(proj, proj, proj, cos_t, sin_t)
    q, k, v = outs[0:3], outs[3:6], outs[6:9]
    flat = lambda arr: arr.reshape(SEQ, w)
    return [(flat(q[p]), flat(k[p]), flat(v[p])) for p in range(3)]


def _band_masks():
    row = lax.broadcasted_iota(jnp.int32, (2 * SPAN, 2 * SPAN), 0) % SPAN
    col = lax.broadcasted_iota(jnp.int32, (2 * SPAN, 2 * SPAN), 1)
    is_prev = col < SPAN
    band = (is_prev & (col >= row)) | (~is_prev & (col - SPAN <= row))
    head0 = lax.broadcasted_iota(jnp.int32, (SPAN, 128), 1) < HEAD_DIM
    return band, is_prev, head0


def _stack_heads(x, head0):
    zero = jnp.zeros_like(x)
    return jnp.concatenate([jnp.where(head0, x, zero), jnp.where(head0, zero, x)], axis=0)


ATTN_UNROLL = 32


def _for_each_block(block, seg_blocks):
    def trip(i, carry):
        for u in range(ATTN_UNROLL):
            static = seg_blocks <= ATTN_UNROLL
            block(i * ATTN_UNROLL + u, (u % seg_blocks == 0) if static else None)
        return carry

    lax.fori_loop(0, SEQ // SPAN // ATTN_UNROLL, trip, 0)


def _attn_fwd(name, q, k, v, seg_blocks):
    def body(q_ref, k_ref, v_ref, o_ref, lse_ref):
        band, is_prev, head0 = _band_masks()

        def block(b, first):
            cur = pl.ds(pl.multiple_of(b * SPAN, SPAN), SPAN)
            qs = _stack_heads(q_ref[cur, :], head0)
            if first is True:
                kcat, vcat, ok = k_ref[cur, :], v_ref[cur, :], band[:, SPAN:]
            else:
                prev = pl.ds(pl.multiple_of(jnp.maximum(b - 1, 0) * SPAN, SPAN), SPAN)
                kcat = jnp.concatenate([k_ref[prev, :], k_ref[cur, :]], axis=0)
                vcat = jnp.concatenate([v_ref[prev, :], v_ref[cur, :]], axis=0)
                ok = band if first is False else band & (((b % seg_blocks) != 0) | ~is_prev)
            s = jnp.where(ok, _dot_nt(qs, kcat), MASK_VALUE)
            m = jnp.max(s, axis=1, keepdims=True)
            p = jnp.exp(s - m)
            l = jnp.sum(p, axis=1, keepdims=True)
            pv = _dot(p, vcat) * (1.0 / l)
            lse = m + jnp.log(l)
            o_ref[cur, :] = jnp.where(head0, pv[:SPAN], pv[SPAN:])
            lse_ref[cur, :] = jnp.where(head0, lse[:SPAN], lse[SPAN:])

        _for_each_block(block, seg_blocks)

    col = _spec((SEQ, 128), lambda j: (0, j))
    return pl.pallas_call(
        body, name=name, grid=(ATTN_WIDTH // 128,),
        in_specs=[col, col, col], out_specs=[col, col],
        out_shape=[jax.ShapeDtypeStruct((SEQ, ATTN_WIDTH), F32)] * 2,
        compiler_params=_params(1),
    )(q, k, v)


def _unpermute(dst, src_ref, d, cols):
    n = dst.shape[0] // d
    for r in range(d):
        dst[pl.ds(r, n, stride=d), :] = src_ref[r, :, cols].astype(dst.dtype)


def _attn_merge(name, outs, lses, gain):
    t = ROW_TILE
    w = ATTN_WIDTH

    def body(o1, o4, o16, l1, l4, l16, g_ref, an_ref, ant_ref, attn_ref, lse_ref, so4, so16, sl4, sl16):
        for cb in range(w // 128):
            cols = slice(cb * 128, (cb + 1) * 128)
            _unpermute(so4, o4, 4, cols)
            _unpermute(so16, o16, 16, cols)
            _unpermute(sl4, l4, 4, cols)
            _unpermute(sl16, l16, 16, cols)
            la, lb, lc = l1[:, cols], sl4[...], sl16[...]
            m = jnp.maximum(jnp.maximum(la, lb), lc)
            ea, eb, ec = jnp.exp(la - m), jnp.exp(lb - m), jnp.exp(lc - m)
            tot = ea + eb + ec
            attn_ref[:, cols] = (ea * o1[:, cols] + eb * so4[...] + ec * so16[...]) / tot
            lse_ref[:, cols] = m + jnp.log(tot)
        attn = attn_ref[...]
        r = lax.rsqrt(jnp.mean(attn * attn, axis=1, keepdims=True) + NORM_EPS)
        an = attn * r * g_ref[...]
        an_ref[...] = an.astype(BF16)
        ant_ref[...] = an.T.astype(BF16)

    views = lambda arrs: [arrs[0], arrs[1].reshape(4, SEQ // 4, w), arrs[2].reshape(16, SEQ // 16, w)]
    row = _spec((t, w), lambda i: (i, 0))
    return pl.pallas_call(
        body, name=name, grid=(SEQ // t,),
        in_specs=_permuted_specs(t, w) * 2 + [_spec((1, w), lambda i: (0, 0))],
        out_specs=[row, _spec((w, t), lambda i: (0, i)), row, row],
        out_shape=[jax.ShapeDtypeStruct((SEQ, 2 * w), BF16), jax.ShapeDtypeStruct((2 * w, SEQ), BF16),
                   jax.ShapeDtypeStruct((SEQ, w), F32), jax.ShapeDtypeStruct((SEQ, w), F32)],
        scratch_shapes=[pltpu.VMEM((t, 128), F32)] * 4,
        compiler_params=_params(1),
    )(*views(outs), *views(lses), gain)


def _head_sum_matrix():
    i = np.arange(ATTN_WIDTH)
    return jnp.asarray((i[:, None] // HEAD_DIM) == (i[None, :] // HEAD_DIM), dtype=F32)


def _attn_bwd_prep(name, d_an, attn, lse, gain, head_sum):
    t = ROW_TILE
    w = ATTN_WIDTH

    def body(dan_ref, attn_ref, lse_ref, g_ref, hs_ref, *rest):
        (do1, do4, do16, dl1, dl4, dl16, ls4, ls16, dg_ref), (sdo, sdl, sls) = rest[:9], rest[9:]

        @pl.when(pl.program_id(0) == 0)
        def _():
            dg_ref[...] = jnp.zeros_like(dg_ref)

        attn = attn_ref[...]
        dan = dan_ref[...]
        r = lax.rsqrt(jnp.mean(attn * attn, axis=1, keepdims=True) + NORM_EPS)
        xhat = attn * r
        dg_ref[...] += jnp.sum(dan * xhat, axis=0, keepdims=True)
        dang = dan * g_ref[...]
        d_o = r * (dang - xhat * jnp.mean(dang * xhat, axis=1, keepdims=True))
        delta = jnp.dot(d_o * attn, hs_ref[...], preferred_element_type=F32,
                        precision=lax.Precision.HIGHEST)
        do1[...] = d_o.astype(BF16)
        dl1[...] = delta
        for cb in range(w // 128):
            cols = slice(cb * 128, (cb + 1) * 128)
            sdo[...] = d_o[:, cols]
            sdl[...] = delta[:, cols]
            sls[...] = lse_ref[:, cols]
            for d, o_do, o_dl, o_ls in ((4, do4, dl4, ls4), (16, do16, dl16, ls16)):
                for rr in range(d):
                    rows = pl.ds(rr, t // d, stride=d)
                    o_do[rr, :, cols] = sdo[rows, :].astype(BF16)
                    o_dl[rr, :, cols] = sdl[rows, :]
                    o_ls[rr, :, cols] = sls[rows, :]

    row = _spec((t, w), lambda i: (i, 0))
    perm = _permuted_specs(t, w)
    outs = pl.pallas_call(
        body, name=name, grid=(SEQ // t,),
        in_specs=[row, row, row, _spec((1, w), lambda i: (0, 0)), _spec((w, w), lambda i: (0, 0))],
        out_specs=perm + perm + perm[1:] + [_spec((1, w), lambda i: (0, 0))],
        out_shape=(_permuted_shapes(w, BF16) + _permuted_shapes(w, F32) + _permuted_shapes(w, F32)[1:]
                   + [jax.ShapeDtypeStruct((1, w), F32)]),
        scratch_shapes=[pltpu.VMEM((t, 128), F32)] * 3,
        compiler_params=_params(1),
    )(d_an, attn, lse, gain, head_sum)
    flat = lambda arr: arr.reshape(SEQ, w)
    d_out = [flat(a) for a in outs[0:3]]
    delta = [flat(a) for a in outs[3:6]]
    lses = [lse, flat(outs[6]), flat(outs[7])]
    return d_out, delta, lses, outs[8]


def _attn_bwd(name, q, k, v, d_out, delta, lse, seg_blocks):
    def body(q_ref, k_ref, v_ref, do_ref, dl_ref, lse_ref, dq_ref, dk_out, dv_out, dk_ref, dv_ref):
        band, is_prev, head0 = _band_masks()
        dk_ref[...] = jnp.zeros_like(dk_ref)
        dv_ref[...] = jnp.zeros_like(dv_ref)

        def per_head(x):
            return jnp.concatenate([x[:, 0:1], x[:, HEAD_DIM:HEAD_DIM + 1]], axis=0)

        def block(b, first):
            cur = pl.ds(pl.multiple_of(b * SPAN, SPAN), SPAN)
            qs = _stack_heads(q_ref[cur, :], head0)
            dos = _stack_heads(do_ref[cur, :], head0)
            if first is True:
                kcat, vcat, ok = k_ref[cur, :], v_ref[cur, :], band[:, SPAN:]
            else:
                prev = pl.ds(pl.multiple_of(jnp.maximum(b - 1, 0) * SPAN, SPAN), SPAN)
                kcat = jnp.concatenate([k_ref[prev, :], k_ref[cur, :]], axis=0)
                vcat = jnp.concatenate([v_ref[prev, :], v_ref[cur, :]], axis=0)
                ok = band if first is False else band & (((b % seg_blocks) != 0) | ~is_prev)
            p = jnp.where(ok, jnp.exp(_dot_nt(qs, kcat) - per_head(lse_ref[cur, :])), 0.0)
            ds = p * (_dot_nt(dos, vcat) - per_head(dl_ref[cur, :]))
            dq = _dot(ds, kcat)
            dq_ref[cur, :] = jnp.where(head0, dq[:SPAN], dq[SPAN:]).astype(BF16)
            dk = _dot_tn(ds, qs)
            dv = _dot_tn(p, dos)
            if first is not True:
                dk_ref[prev, :] += dk[:SPAN]
                dv_ref[prev, :] += dv[:SPAN]
            dk_ref[cur, :] += dk[-SPAN:]
            dv_ref[cur, :] += dv[-SPAN:]

        _for_each_block(block, seg_blocks)
        dk_out[...] = dk_ref[...].astype(BF16)
        dv_out[...] = dv_ref[...].astype(BF16)

    col = _spec((SEQ, 128), lambda j: (0, j))
    return pl.pallas_call(
        body, name=name, grid=(ATTN_WIDTH // 128,),
        in_specs=[col] * 6, out_specs=[col] * 3,
        out_shape=[jax.ShapeDtypeStruct((SEQ, ATTN_WIDTH), BF16)] * 3,
        scratch_shapes=[pltpu.VMEM((SEQ, 128), F32)] * 2,
        compiler_params=_params(1),
    )(q, k, v, d_out, delta, lse)


def _attn_bwd_post(name, grads, cos_t, sin_t):
    t = ROW_TILE
    w = ATTN_WIDTH

    def body(*refs):
        ins, cos_ref, sin_ref, out_ref, s4, s16 = refs[:9], refs[9], refs[10], refs[11], refs[12], refs[13]
        cosv, sinv = cos_ref[...], sin_ref[...]
        for a in range(3):
            g1, g4, g16 = ins[a], ins[3 + a], ins[6 + a]
            for cb in range(w // 128):
                cols = slice(cb * 128, (cb + 1) * 128)
                _unpermute(s4, g4, 4, cols)
                _unpermute(s16, g16, 16, cols)
                val = g1[:, cols].astype(F32) + s4[...] + s16[...]
                if a < 2:
                    val = val * cosv + _swap_halves(val * sinv)
                if a == 0:
                    val = val * (HEAD_DIM ** -0.5)
                out_ref[:, a * w + cb * 128:a * w + (cb + 1) * 128] = val.astype(BF16)

    views = []
    for p, d in enumerate(DILATIONS):
        for a in range(3):
            views.append(grads[p][a] if d == 1 else grads[p][a].reshape(d, SEQ // d, w))
    perm = _permuted_specs(t, w)
    in_specs = [perm[0]] * 3 + [perm[1]] * 3 + [perm[2]] * 3
    return pl.pallas_call(
        body, name=name, grid=(SEQ // t,),
        in_specs=in_specs + [_spec((t, 128), lambda i: (i, 0))] * 2,
        out_specs=_spec((t, 3 * w), lambda i: (i, 0)),
        out_shape=jax.ShapeDtypeStruct((SEQ, 3 * w), BF16),
        scratch_shapes=[pltpu.VMEM((t, 128), F32)] * 2,
        compiler_params=_params(1),
    )(*views, cos_t, sin_t)


N_LEVELS = 7
HGRN_PAIR = 4


def _hgrn_consts():
    c = CHUNK
    i = np.arange(c)[:, None]
    s = np.arange(c)[None, :]
    blocks = [s <= i]
    for lv in range(N_LEVELS):
        bs = c >> lv
        h = bs // 2
        m = (i // bs) * bs + h - 1
        second = (i % bs) >= h
        blocks.append((second & (s > m) & (s <= i)) | (~second & (s > i) & (s <= m)))
    blocks.append(s > i)
    stack = np.concatenate(blocks, axis=0).astype(np.float32)
    return jnp.asarray(stack, dtype=BF16), jnp.asarray(stack.T, dtype=BF16)


def _exact_dot(m01, x):
    hi = x.astype(BF16)
    lo = (x - hi.astype(F32)).astype(BF16)
    n = x.shape[1]
    full = jnp.dot(m01, jnp.concatenate([hi, lo], axis=1), preferred_element_type=F32)
    return full[:, :n] + full[:, n:]


def _hgrn_gates(qh, z, lb):
    sq = _sigmoid(qh)
    q = qh * sq * (HGRN_DIM ** -0.5)
    sig = _sigmoid(z)
    sigm = _sigmoid(-z)
    f = lb + (1.0 - lb) * sig
    k = (1.0 - lb) * sigm
    return q, k, f, sq, sig, sigm


def _level_masks(lv):
    row = lax.broadcasted_iota(jnp.int32, (CHUNK, CHUNK), 0)
    col = lax.broadcasted_iota(jnp.int32, (CHUNK, CHUNK), 1)
    shift = N_LEVELS - lv
    half = CHUNK >> (lv + 1)
    second = (row & half) != 0
    second_col = (col & half) != 0
    same = (row >> shift) == (col >> shift)
    return second, same & second & ~second_col, same & (second != second_col)


def _hgrn_fwd(name, proj, lb, gain, stack, mixed, mixed_t):
    t = ROW_TILE
    per = t // CHUNK
    n_rb = SEQ // t
    n_chunks = SEQ // CHUNK
    col0 = 3 * ATTN_WIDTH // 128
    pair_w = HGRN_PAIR * HGRN_DIM

    def body(q_ref, f_ref, i_ref, g_ref, lb_ref, gain_ref, stack_ref, mixed_in, mixed_t_in,
             rec_ref, rect_ref, o_ref, st_out, a_out, st):
        del mixed_in, mixed_t_in

        @pl.when(pl.program_id(1) == 0)
        def _():
            st[...] = jnp.zeros_like(st)

        row = lax.broadcasted_iota(jnp.int32, (CHUNK, CHUNK), 0)
        col = lax.broadcasted_iota(jnp.int32, (CHUNK, CHUNK), 1)
        for c, hh in [(c, hh) for c in range(per) for hh in range(HGRN_PAIR)]:
            rows = slice(c * CHUNK, (c + 1) * CHUNK)
            lanes = slice(hh * HGRN_DIM, (hh + 1) * HGRN_DIM)
            lbv = lb_ref[hh]
            qh, z, v, gh = q_ref[rows, lanes], f_ref[rows, lanes], i_ref[rows, lanes], g_ref[rows, lanes]
            q, k, f, _, _, _ = _hgrn_gates(qh, z, lbv)
            dec = _exact_dot(stack_ref[...], jnp.log(f))
            g = dec[0:CHUNK]
            to_end = dec[(N_LEVELS + 1) * CHUNK:(N_LEVELS + 2) * CHUNK]
            a = jnp.where(row == col, jnp.sum(q * k, axis=1, keepdims=True), 0.0)
            for lv in range(N_LEVELS):
                second, square, _ = _level_masks(lv)
                t = jnp.where(second, q, k) * jnp.exp(dec[(lv + 1) * CHUNK:(lv + 2) * CHUNK])
                a = a + jnp.where(square, _dot_nt(t, t), 0.0)
            st_prev = st[hh]
            st_out[hh, c] = st_prev
            a_out[hh, c] = a
            o = _dot(a, v) + _dot_nt(q * jnp.exp(g), st_prev)
            k_end = k * jnp.exp(to_end)
            st[hh] = st_prev * jnp.exp(g[CHUNK - 1:CHUNK, :]) + _dot(v.T, k_end)
            o_ref[rows, lanes] = o
            r = lax.rsqrt(jnp.mean(o * o, axis=1, keepdims=True) + NORM_EPS)
            rec = o * r * gain_ref[...] * (gh * _sigmoid(gh))
            rec_ref[rows, lanes] = rec.astype(BF16)
            rect_ref[lanes, rows] = rec.T.astype(BF16)

    def col_spec(tt):
        return _spec((t, pair_w), lambda h, rb: (rb, (col0 + HGRN_HEADS * tt) // HGRN_PAIR + h))

    chunk_spec = _spec((HGRN_PAIR, per, CHUNK, CHUNK), lambda h, rb: (h, rb, 0, 0))
    return pl.pallas_call(
        body, name=name, grid=(HGRN_HEADS // HGRN_PAIR, n_rb),
        in_specs=[col_spec(0), col_spec(1), col_spec(2), col_spec(3),
                  _spec((HGRN_PAIR, 1, HGRN_DIM), lambda h, rb: (h, 0, 0)),
                  _spec((1, HGRN_DIM), lambda h, rb: (0, 0)),
                  _spec(stack.shape, lambda h, rb: (0, 0)), ANY_SPEC, ANY_SPEC],
        out_specs=[_spec((t, pair_w), lambda h, rb: (rb, ATTN_WIDTH // pair_w + h)),
                   _spec((pair_w, t), lambda h, rb: (ATTN_WIDTH // pair_w + h, rb)),
                   _spec((t, pair_w), lambda h, rb: (rb, h)),
                   chunk_spec, chunk_spec],
        out_shape=[jax.ShapeDtypeStruct(mixed.shape, BF16),
                   jax.ShapeDtypeStruct(mixed_t.shape, BF16),
                   jax.ShapeDtypeStruct((SEQ, HGRN_WIDTH), F32),
                   jax.ShapeDtypeStruct((HGRN_HEADS, n_chunks, CHUNK, CHUNK), F32),
                   jax.ShapeDtypeStruct((HGRN_HEADS, n_chunks, CHUNK, CHUNK), F32)],
        scratch_shapes=[pltpu.VMEM((HGRN_PAIR, CHUNK, CHUNK), F32)],
        input_output_aliases={7: 0, 8: 1},
        compiler_params=_params(2),
    )(proj, proj, proj, proj, lb, gain, stack, mixed, mixed_t)


def _hgrn_bwd(name, proj, d_rec, o_pre, states, scores, lb, gain, stack, stack_t):
    t = ROW_TILE
    per = t // CHUNK
    n_rb = SEQ // t
    col0 = 3 * ATTN_WIDTH // 128
    pair_w = HGRN_PAIR * HGRN_DIM

    def body(q_ref, f_ref, i_ref, g_ref, drec_ref, o_ref, st_ref, a_ref, lb_ref, gain_ref,
             stack_ref, stack_t_ref, dq_ref, df_ref, di_ref, dg_ref, dlb_ref, dgain_ref, dst):
        @pl.when(pl.program_id(1) == 0)
        def _():
            dst[...] = jnp.zeros_like(dst)
            dlb_ref[...] = jnp.zeros_like(dlb_ref)
            dgain_ref[...] = jnp.zeros_like(dgain_ref)

        gain_v = gain_ref[...]
        row = lax.broadcasted_iota(jnp.int32, (CHUNK, CHUNK), 0)
        col = lax.broadcasted_iota(jnp.int32, (CHUNK, CHUNK), 1)
        for c, hh in [(c, hh) for c in reversed(range(per)) for hh in range(HGRN_PAIR)]:
            rows = slice(c * CHUNK, (c + 1) * CHUNK)
            lanes = slice(hh * HGRN_DIM, (hh + 1) * HGRN_DIM)
            lbv = lb_ref[hh]
            qh, z, v, gh = q_ref[rows, lanes], f_ref[rows, lanes], i_ref[rows, lanes], g_ref[rows, lanes]
            q, k, f, sq, sig, sigm = _hgrn_gates(qh, z, lbv)
            dec = _exact_dot(stack_ref[...], jnp.log(f))
            g = dec[0:CHUNK]
            to_end = dec[(N_LEVELS + 1) * CHUNK:(N_LEVELS + 2) * CHUNK]
            e_g = jnp.exp(g)
            e_end = jnp.exp(to_end)
            e_last = jnp.exp(g[CHUNK - 1:CHUNK, :])
            q_in = q * e_g
            k_end = k * e_end
            st_prev = st_ref[hh, c]
            a = a_ref[hh, c]
            dst_new = dst[hh]

            o = o_ref[rows, lanes]
            drec = drec_ref[rows, lanes]
            sg = _sigmoid(gh)
            r = lax.rsqrt(jnp.mean(o * o, axis=1, keepdims=True) + NORM_EPS)
            ohat = o * r
            d_gh = drec * (ohat * gain_v) * (sg * (1.0 + gh * (1.0 - sg)))
            d_on = drec * (gh * sg)
            dgain_ref[hh] += jnp.sum(d_on * ohat, axis=0, keepdims=True)
            d_ohat = d_on * gain_v
            d_o = r * (d_ohat - ohat * jnp.mean(d_ohat * ohat, axis=1, keepdims=True))

            d_sym = jnp.where(row >= col, _dot_nt(d_o, v), _dot_nt(v, d_o))
            d_v = _dot(a.T, d_o) + _dot_nt(k_end, dst_new)
            d_q_in = _dot(d_o, st_prev)
            d_k_end = _dot(v, dst_new)
            d_q = d_q_in * e_g
            d_k = d_k_end * e_end
            diag = jnp.sum(d_o * v, axis=1, keepdims=True)
            d_q = d_q + diag * k
            d_k = d_k + diag * q
            d_dec = [q_in * d_q_in]
            d_both, d_second = None, None
            for lv in range(N_LEVELS):
                e = jnp.exp(dec[(lv + 1) * CHUNK:(lv + 2) * CHUNK])
                second, _, mirrored = _level_masks(lv)
                t = jnp.where(second, q, k) * e
                d_t = _dot(jnp.where(mirrored, d_sym, 0.0), t)
                d_te = d_t * e
                d_both = d_te if d_both is None else d_both + d_te
                d_second = jnp.where(second, d_te, 0.0) if d_second is None else d_second + jnp.where(second, d_te, 0.0)
                d_dec.append(t * d_t)
            d_q = d_q + d_second
            d_k = d_k + (d_both - d_second)
            d_dec.append(k_end * d_k_end)
            flux = jnp.sum(dst_new * st_prev, axis=0, keepdims=True) * e_last
            d_lf = _exact_dot(stack_t_ref[...], jnp.concatenate(d_dec, axis=0)) + flux
            dst[hh] = dst_new * e_last + _dot(d_o.T, q_in)

            d_f = d_lf / f - d_k
            dlb_ref[hh] += jnp.sum(d_f * sigm, axis=0, keepdims=True)
            dq_ref[rows, lanes] = (d_q * (HGRN_DIM ** -0.5) * (sq * (1.0 + qh * (1.0 - sq)))).astype(BF16)
            df_ref[rows, lanes] = (d_f * (1.0 - lbv) * sig * sigm).astype(BF16)
            di_ref[rows, lanes] = d_v.astype(BF16)
            dg_ref[rows, lanes] = d_gh.astype(BF16)

    last = n_rb - 1

    def col_spec(tt):
        return _spec((t, pair_w), lambda h, rb: (last - rb, (col0 + HGRN_HEADS * tt) // HGRN_PAIR + h))

    head_col = _spec((t, pair_w), lambda h, rb: (last - rb, h))
    rec_col0 = (d_rec.shape[1] - HGRN_WIDTH) // pair_w
    d_rec_col = _spec((t, pair_w), lambda h, rb: (last - rb, rec_col0 + h))
    chunk_spec = _spec((HGRN_PAIR, per, CHUNK, CHUNK), lambda h, rb: (h, last - rb, 0, 0))
    vec_spec = _spec((HGRN_PAIR, 1, HGRN_DIM), lambda h, rb: (h, 0, 0))
    outs = pl.pallas_call(
        body, name=name, grid=(HGRN_HEADS // HGRN_PAIR, n_rb),
        in_specs=[col_spec(0), col_spec(1), col_spec(2), col_spec(3), d_rec_col, head_col,
                  chunk_spec, chunk_spec, vec_spec,
                  _spec((1, HGRN_DIM), lambda h, rb: (0, 0)),
                  _spec(stack.shape, lambda h, rb: (0, 0)), _spec(stack_t.shape, lambda h, rb: (0, 0))],
        out_specs=[head_col] * 4 + [vec_spec, vec_spec],
        out_shape=[jax.ShapeDtypeStruct((SEQ, HGRN_WIDTH), BF16)] * 4
                  + [jax.ShapeDtypeStruct((HGRN_HEADS, 1, HGRN_DIM), F32)] * 2,
        scratch_shapes=[pltpu.VMEM((HGRN_PAIR, CHUNK, CHUNK), F32)],
        compiler_params=_params(2),
    )(proj, proj, proj, proj, d_rec, o_pre, states, scores, lb, gain, stack, stack_t)
    return outs


ANY_SPEC = pl.BlockSpec(memory_space=pl.ANY)


def _my_place():
    return lax.axis_index("x"), lax.axis_index("y"), lax.axis_index("c")


def _other_chips(x, y):
    return [(1 - x, y), (x, 1 - y), (1 - x, 1 - y)]


def _remote(src, dst, send_sem, recv_sem, device):
    return pltpu.make_async_remote_copy(src_ref=src, dst_ref=dst, send_sem=send_sem, recv_sem=recv_sem,
                                        device_id=device, device_id_type=MESH)


def _staged_copies(srcs, dsts, stage, sems):
    loads = [pltpu.make_async_copy(srcs[i], stage[i], sems.at[i]) for i in range(len(srcs))]
    for cp in loads:
        cp.start()
    stores = []
    for i, cp in enumerate(loads):
        cp.wait()
        stores.append(pltpu.make_async_copy(stage[i], dsts[i], sems.at[i]))
        stores[-1].start()
    return stores


def _gather_weights(name, shards):
    n = len(shards)

    def body(*refs):
        ins, outs = refs[:n], refs[n:2 * n]
        ici_send, ici_recv, d2d_send, d2d_recv, local_sems = refs[2 * n:2 * n + 5]
        stage = refs[2 * n + 5:]
        x, y, c = _my_place()
        me = 2 * x + y
        chips = _other_chips(x, y)

        def half(i, which):
            h = ins[i].shape[0] // 2
            return pl.ds(which * h, h)

        sends = []
        for i in range(n):
            for j, (px, py) in enumerate(chips):
                sends.append(_remote(ins[i].at[half(i, c), :], outs[i].at[me, half(i, c), :],
                                     ici_send.at[3 * i + j], ici_recv.at[3 * i + j], (px, py, c)))
        for cp in sends:
            cp.start()
        local = _staged_copies(ins, [outs[i].at[me] for i in range(n)], stage, local_sems)
        for i in range(n):
            for j, (px, py) in enumerate(chips):
                landed = outs[i].at[2 * px + py, half(i, c), :]
                _remote(landed, landed, ici_send.at[3 * i + j], ici_recv.at[3 * i + j], (px, py, c)).wait_recv()
                forward = _remote(landed, landed, d2d_send.at[3 * i + j], d2d_recv.at[3 * i + j], (x, y, 1 - c))
                forward.start()
                sends.append(forward)
        for i in range(n):
            for j, (px, py) in enumerate(chips):
                other = outs[i].at[2 * px + py, half(i, 1 - c), :]
                _remote(other, other, d2d_send.at[3 * i + j], d2d_recv.at[3 * i + j], (x, y, 1 - c)).wait_recv()
        for cp in sends:
            cp.wait_send()
        for cp in local:
            cp.wait()

    return pl.pallas_call(
        body, name=name, in_specs=[ANY_SPEC] * n, out_specs=[ANY_SPEC] * n,
        out_shape=[jax.ShapeDtypeStruct((N_CHIPS,) + s.shape, s.dtype) for s in shards],
        scratch_shapes=([pltpu.SemaphoreType.DMA((3 * n,))] * 4 + [pltpu.SemaphoreType.DMA((n,))]
                        + [pltpu.VMEM(s.shape, s.dtype) for s in shards]),
        compiler_params=pltpu.CompilerParams(vmem_limit_bytes=VMEM_LIMIT),
    )(*shards)


HBM_SPEC = pl.BlockSpec(memory_space=pltpu.HBM)
SEM_SPEC = pl.BlockSpec(memory_space=pltpu.SEMAPHORE)
SPLIT_PARAMS = pltpu.CompilerParams(has_side_effects=pltpu.SideEffectType.DATAFLOW_SIDE_EFFECTING)
N_PEERS = {"gather": N_CHIPS - 1, "scatter": N_DEV - 1}


def _split_copies(ins, lands, send_sems, recv_sems, kind):
    x, y, c = _my_place()
    pairs = []
    for i in range(len(ins)):
        if kind == "gather":
            me = 2 * x + y
            for j, (px, py) in enumerate(_other_chips(x, y)):
                sems = (send_sems.at[3 * i + j], recv_sems.at[3 * i + j], (px, py, c))
                pairs.append((_remote(ins[i], lands[i].at[me], *sems),
                              _remote(ins[i], lands[i].at[2 * px + py], *sems)))
        else:
            me = 4 * x + 2 * y + c
            h = ins[i].shape[1] // 2
            for k in range(1, N_DEV):
                px, py, pc = (x + (k >> 2)) % 2, (y + ((k >> 1) & 1)) % 2, (c + (k & 1)) % 2
                src = ins[i].at[2 * px + py, pl.ds(pc * h, h), :]
                sems = (send_sems.at[7 * i + k - 1], recv_sems.at[7 * i + k - 1], (px, py, pc))
                pairs.append((_remote(src, lands[i].at[me], *sems),
                              _remote(src, lands[i].at[4 * px + 2 * py + pc], *sems)))
    return pairs


def _exchange_start(name, srcs, lands, kind, after=None):
    n = len(srcs)
    n_sems = N_PEERS[kind] * n
    extra = [] if after is None else [after]

    def body(*refs):
        ins, land_refs = refs[:n], refs[n:2 * n]
        send_sems, recv_sems = refs[2 * n + len(extra):2 * n + len(extra) + 2]
        token = refs[-1]
        for send, _ in _split_copies(ins, land_refs, send_sems, recv_sems, kind):
            send.start()
        token[...] = jnp.zeros_like(token)

    arrays = list(srcs) + list(lands)
    outs = pl.pallas_call(
        body, name=name,
        in_specs=[HBM_SPEC] * (2 * n) + [ANY_SPEC] * len(extra),
        out_shape=([pltpu.SemaphoreType.DMA((n_sems,))] * 2 + [pltpu.HBM(a.shape, a.dtype) for a in arrays]
                   + [jax.ShapeDtypeStruct((8, 128), F32)]),
        out_specs=[SEM_SPEC] * 2 + [HBM_SPEC] * (2 * n) + [pl.BlockSpec(memory_space=pltpu.VMEM)],
        input_output_aliases={i: 2 + i for i in range(2 * n)},
        compiler_params=SPLIT_PARAMS,
    )(*[pltpu.with_memory_space_constraint(a, pltpu.HBM) for a in arrays], *extra)
    return outs[:2], outs[2:2 + 2 * n], outs[-1]


def _exchange_wait(name, sems, passed, kind, after):
    n = len(passed) // 2

    def body(*refs):
        ins, land_refs = refs[:n], refs[n:2 * n]
        send_sems, recv_sems = refs[2 * n:2 * n + 2]
        for send, arrive in _split_copies(ins, land_refs, send_sems, recv_sems, kind):
            send.wait_send()
            arrive.wait_recv()

    outs = pl.pallas_call(
        body, name=name,
        in_specs=[HBM_SPEC] * (2 * n) + [SEM_SPEC] * 2 + [ANY_SPEC],
        out_shape=[pltpu.HBM(a.shape, a.dtype) for a in passed],
        out_specs=[HBM_SPEC] * (2 * n),
        input_output_aliases={i: i for i in range(2 * n)},
        compiler_params=SPLIT_PARAMS,
    )(*passed, *sems, after)
    return outs[:n], outs[n:]


def _own_slot(name, own, me):
    r, cc = own.shape
    th = min(r, 512)

    def body(me_ref, x_ref, o_ref):
        del me_ref
        o_ref[...] = x_ref[...]

    grid_spec = pltpu.PrefetchScalarGridSpec(
        num_scalar_prefetch=1, grid=(r // th,),
        in_specs=[pl.BlockSpec((th, cc), lambda i, me_ref: (i, 0))],
        out_specs=pl.BlockSpec((None, th, cc), lambda i, me_ref: (me_ref[0], i, 0)))
    return pl.pallas_call(
        body, name=name, grid_spec=grid_spec,
        out_shape=jax.ShapeDtypeStruct((N_CHIPS, r, cc), own.dtype), compiler_params=_params(1),
    )(me, own)


def _sum_devices(name, landed, own, place):
    n_dev, h, cc = landed.shape
    th = min(h, 256)
    nb = h // th

    def body(place_ref, l_ref, own_ref, o_ref):
        total = None
        for d in range(n_dev):
            piece = jnp.where(place_ref[0] == d, own_ref[...], l_ref[d]).astype(F32)
            total = piece if total is None else total + piece
        o_ref[...] = total

    grid_spec = pltpu.PrefetchScalarGridSpec(
        num_scalar_prefetch=1, grid=(nb,),
        in_specs=[pl.BlockSpec((n_dev, th, cc), lambda i, p: (0, i, 0)),
                  pl.BlockSpec((None, th, cc), lambda i, p: (p[1], p[2] * nb + i, 0))],
        out_specs=pl.BlockSpec((th, cc), lambda i, p: (i, 0)))
    return pl.pallas_call(
        body, name=name, grid_spec=grid_spec,
        out_shape=jax.ShapeDtypeStruct((h, cc), F32), compiler_params=_params(1),
    )(place, landed, own)


def _share_halves(name, halves):
    flat = [t for per_weight in halves for t in per_weight]
    n = len(flat)
    n_w = len(halves)

    def body(*refs):
        ins, outs = refs[:n], refs[n:n + n_w]
        send_sems, recv_sems, local_sems = refs[n + n_w:n + n_w + 3]
        stage = refs[n + n_w + 3:]
        x, y, c = _my_place()
        sends, own = [], []
        for i in range(n):
            w, l = divmod(i, DEPTH)
            h = ins[i].shape[0]
            own.append(outs[w].at[l, pl.ds(c * h, h), :])
            sends.append(_remote(ins[i], own[i], send_sems.at[i], recv_sems.at[i], (x, y, 1 - c)))
        for cp in sends:
            cp.start()
        local = _staged_copies(ins, own, stage, local_sems)
        for i in range(n):
            w, l = divmod(i, DEPTH)
            h = ins[i].shape[0]
            _remote(ins[i], outs[w].at[l, pl.ds((1 - c) * h, h), :], send_sems.at[i], recv_sems.at[i],
                    (x, y, 1 - c)).wait_recv()
        for cp in sends:
            cp.wait_send()
        for cp in local:
            cp.wait()

    return pl.pallas_call(
        body, name=name, in_specs=[ANY_SPEC] * n, out_specs=[ANY_SPEC] * n_w,
        out_shape=[jax.ShapeDtypeStruct((DEPTH, 2 * per_weight[0].shape[0], per_weight[0].shape[1]), F32)
                   for per_weight in halves],
        scratch_shapes=([pltpu.SemaphoreType.DMA((n,))] * 3 + [pltpu.VMEM(t.shape, t.dtype) for t in flat]),
        compiler_params=pltpu.CompilerParams(vmem_limit_bytes=VMEM_LIMIT),
    )(*flat)


def _all_reduce_small(pack, after):
    def body(p_ref, after_ref, o_ref, recv, send_sems, recv_sems):
        del after_ref
        x, y, c = _my_place()
        me = 4 * x + 2 * y + c
        recv[me] = p_ref[...]
        peers = []
        for k in range(1, N_DEV):
            px, py, pc = (x + (k >> 2)) % 2, (y + ((k >> 1) & 1)) % 2, (c + (k & 1)) % 2
            peers.append((px, py, pc))
        sends = [_remote(p_ref, recv.at[me], send_sems.at[k], recv_sems.at[k], peer)
                 for k, peer in enumerate(peers)]
        for cp in sends:
            cp.start()
        for k, (px, py, pc) in enumerate(peers):
            _remote(p_ref, recv.at[4 * px + 2 * py + pc], send_sems.at[k], recv_sems.at[k],
                    (px, py, pc)).wait_recv()
        for cp in sends:
            cp.wait_send()
        total = recv[0]
        for d in range(1, N_DEV):
            total = total + recv[d]
        o_ref[...] = total

    vmem = pl.BlockSpec(memory_space=pltpu.VMEM)
    return pl.pallas_call(
        body, name="all_reduce_small", in_specs=[vmem, ANY_SPEC], out_specs=vmem,
        out_shape=jax.ShapeDtypeStruct(pack.shape, F32),
        scratch_shapes=[pltpu.VMEM((N_DEV,) + pack.shape, F32),
                        pltpu.SemaphoreType.DMA((N_DEV - 1,)), pltpu.SemaphoreType.DMA((N_DEV - 1,))],
    )(pack, after)


def _adamw(name, w, g, m, v):
    r, cc = w.shape
    th = min(r, 256)

    def body(w_ref, g_ref, m_ref, v_ref, d_ref, m_out, v_out):
        gv = g_ref[...]
        m2 = ADAM_B1 * m_ref[...] + (1.0 - ADAM_B1) * gv
        v2 = ADAM_B2 * v_ref[...] + (1.0 - ADAM_B2) * (gv * gv)
        m_hat = m2 / (1.0 - ADAM_B1 ** ADAM_STEP)
        v_hat = v2 / (1.0 - ADAM_B2 ** ADAM_STEP)
        d_ref[...] = -ADAM_LR * (m_hat / (jnp.sqrt(v_hat) + ADAM_EPS) + ADAM_WD * w_ref[...])
        m_out[...] = m2
        v_out[...] = v2

    tile = _spec((th, cc), lambda i: (i, 0))
    return pl.pallas_call(
        body, name=name, grid=(r // th,), in_specs=[tile] * 4, out_specs=[tile] * 3,
        out_shape=[jax.ShapeDtypeStruct((r, cc), F32)] * 3, compiler_params=_params(1),
    )(w, g, m, v)


def _lower_bounds(lb_logits):
    p = jax.nn.softmax(lb_logits.astype(F32), axis=0)
    return jnp.cumsum(p, axis=0) - p[0]


def _layer_forward(l, stream, small, weights, consts, next_gain=None, loss=None, after=None):
    win, rest = weights
    cos_t, sin_t, stack, _, _ = consts
    tm = MM_TILE
    x_in, h, h_t = stream
    saved = {"x_in": x_in}

    proj = _mm_pieces(f"proj{l}", h, win, False, tm, after=after)
    saved.update(h_t=h_t, proj=proj)

    qkv = _attn_prep(f"attn_prep{l}", proj, cos_t, sin_t)
    outs, lses = [], []
    for p, d in enumerate(DILATIONS):
        o, lse = _attn_fwd(f"attn_fwd{l}_{d}", *qkv[p], SEQ // d // SPAN)
        outs.append(o)
        lses.append(lse)
    mixed, mixed_t, attn, lse = _attn_merge(f"attn_merge{l}", outs, lses, small["attn_out_gain"][l][None, :])
    saved.update(qkv=qkv, attn=attn, lse=lse)

    lb3 = small["lower"][l].reshape(HGRN_HEADS, 1, HGRN_DIM)
    mixed, mixed_t, o_pre, states, scores = _hgrn_fwd(f"hgrn_fwd{l}", proj, lb3, small["hgrn_out_gain"][l][None, :],
                                                      stack, mixed, mixed_t)
    wo, wu, wd = rest(mixed)
    saved.update(mixed_t=mixed_t, o_pre=o_pre, states=states, scores=scores, lb3=lb3, weights=(win, wo, wu, wd))

    x_mid, h2, h2_t = _mm_accum(f"out_proj{l}", mixed, wo, False, tm, x_in, next_gain=small["norm_mlp"][l][None, :])
    saved["x_mid"] = x_mid

    a, relu_u, a_t = _mm_pieces(f"up{l}", h2, wu, False, tm, epilogue="relu2")
    new_stream = tuple(_mm_accum(f"down{l}", a, wd, False, tm, x_mid, next_gain=next_gain, loss=loss))
    saved.update(h2_t=h2_t, relu_u=relu_u, a_t=a_t)
    return new_stream, saved


def _layer_backward(l, dx, saved, small, consts, on_grads, after=None):
    win, wo, wu, wd = saved["weights"]
    cos_t, sin_t, stack, stack_t, head_sum = consts
    tm = MM_TILE

    dx, dx_b = dx
    du = _mm_pieces(f"d_u{l}", dx_b, wd, True, tm, epilogue="relu2_grad", extra=saved["relu_u"], after=after)
    d_wd = _mm_dw(f"d_wdown{l}", saved["a_t"], dx_b, False, tm)
    dxm, dxm_b, dg_mlp = _mm_accum(f"d_h2_{l}", du, wu, True, tm, dx,
                                   norm=(saved["x_mid"], small["norm_mlp"][l][None, :]))
    d_wu = _mm_dw(f"d_wup{l}", saved["h2_t"], du, True, tm)
    d_wo = _mm_dw(f"d_wout{l}", saved["mixed_t"], dxm_b, False, tm)
    after_early = on_grads(l, "early", (d_wu, d_wd, d_wo))

    d_mixed = _mm_pieces(f"d_mixed{l}", dxm_b, wo, True, tm, after=after_early)
    d_rec = d_mixed

    d_out, delta, lses, dg_attn = _attn_bwd_prep(f"attn_bwd_prep{l}", d_mixed, saved["attn"], saved["lse"],
                                                 small["attn_out_gain"][l][None, :], head_sum)
    grads = []
    for p, d in enumerate(DILATIONS):
        grads.append(_attn_bwd(f"attn_bwd{l}_{d}", *saved["qkv"][p], d_out[p], delta[p], lses[p],
                               SEQ // d // SPAN))
    dp_attn = _attn_bwd_post(f"attn_bwd_post{l}", grads, cos_t, sin_t)

    dq_h, df_h, di_h, dg_h, d_lower, dg_hgrn = _hgrn_bwd(
        f"hgrn_bwd{l}", saved["proj"], d_rec, saved["o_pre"], saved["states"], saved["scores"],
        saved["lb3"], small["hgrn_out_gain"][l][None, :], stack, stack_t)
    dproj = [dp_attn, dq_h, df_h, di_h, dg_h]

    d_win = _mm_dw(f"d_win{l}", saved["h_t"], dproj, True, tm)
    after_last = on_grads(l, "last", (d_win,))
    dx_in, dx_in_b, dg_mix = _mm_accum(f"d_h{l}", dproj, win, True, tm, dxm,
                                       norm=(saved["x_in"], small["norm_mix"][l][None, :]), after=after_last)

    small_grads = {"norm_mix": dg_mix[0], "attn_out_gain": dg_attn[0],
                   "lower": d_lower.reshape(HGRN_WIDTH),
                   "hgrn_out_gain": jnp.sum(dg_hgrn, axis=0).reshape(HGRN_DIM), "norm_mlp": dg_mlp[0]}
    return (dx_in, dx_in_b), after_last, small_grads


def _local_step(xs, target, small, get_weights, on_grads):
    consts = _rope_tables() + _hgrn_consts() + (_head_sum_matrix(),)
    stream = (xs,) + tuple(_rms_fwd("norm_mix0", xs, small["norm_mix"][0][None, :]))
    saved = []
    for l in range(DEPTH):
        w, after = get_weights(l, stream[0])
        if l + 1 < DEPTH:
            stream, s = _layer_forward(l, stream, small, w, consts, next_gain=small["norm_mix"][l + 1][None, :],
                                       after=after)
        else:
            stream, s = _layer_forward(l, stream, small, w, consts, loss=(small["norm_final"][None, :], target),
                                       after=after)
        saved.append(s)
    dx_f, dx_b, dg_final, loss = stream
    dx = (dx_f, dx_b)
    small_grads = [None] * DEPTH
    after = None
    for l in reversed(range(DEPTH)):
        dx, after, small_grads[l] = _layer_backward(l, dx, saved[l], small, consts, on_grads, after=after)
    return loss, dx[0], dg_final[0], small_grads


def _pack_small(norm_mix, attn_out_gain, lb, hgrn_out_gain, norm_mlp, norm_final, last_row):
    rows = [norm_mix, attn_out_gain.reshape(1, D_MODEL), lb.reshape(1, D_MODEL),
            jnp.pad(hgrn_out_gain.reshape(1, DEPTH * HGRN_DIM), ((0, 0), (0, D_MODEL - DEPTH * HGRN_DIM))),
            norm_mlp, norm_final.reshape(1, D_MODEL), last_row.reshape(1, D_MODEL)]
    pack = jnp.concatenate(rows, axis=0)
    return jnp.pad(pack, ((0, PACK_ROWS - pack.shape[0]), (0, 0)))


def _unpack_small(pack):
    return (pack[0:2], pack[2].reshape(DEPTH, ATTN_WIDTH), pack[3].reshape(DEPTH, HGRN_WIDTH),
            pack[4, :DEPTH * HGRN_DIM].reshape(DEPTH, HGRN_DIM), pack[5:7], pack[7], pack[8])


def kernel(x, norm_mix, w_in, attn_out_gain, hgrn_lb_logits, hgrn_out_gain, w_out, norm_mlp, w_up, w_down, norm_final, loss_target, m_norm_mix, m_w_in, m_attn_out_gain, m_hgrn_lb_logits, m_hgrn_out_gain, m_w_out, m_norm_mlp, m_w_up, m_w_down, m_norm_final, v_norm_mix, v_w_in, v_attn_out_gain, v_hgrn_lb_logits, v_hgrn_out_gain, v_w_out, v_norm_mlp, v_w_up, v_w_down, v_norm_final):
    lower, lower_vjp = jax.vjp(_lower_bounds, hgrn_lb_logits)
    small = {"norm_mix": norm_mix, "attn_out_gain": attn_out_gain, "lower": lower,
             "hgrn_out_gain": hgrn_out_gain, "norm_mlp": norm_mlp, "norm_final": norm_final}
    big_w = (w_in, w_out, w_up, w_down)

    x_pos, y_pos, core = lax.axis_index("x"), lax.axis_index("y"), lax.axis_index("c")
    me = (2 * x_pos + y_pos).astype(jnp.int32).reshape(1)
    place = jnp.stack([4 * x_pos + 2 * y_pos + core, 2 * x_pos + y_pos, core]).astype(jnp.int32)
    shards = [[w[l].astype(BF16) for w in big_w] for l in range(DEPTH)]
    in_flight = {}

    def start_gather(name, some, after):
        lands = [_own_slot(f"own_{name}_{i}", s, me) for i, s in enumerate(some)]
        sems, passed, token = _exchange_start(f"start_{name}", some, lands, "gather", after)
        in_flight[name] = (sems, passed)
        return token

    def finish_gather(name, after):
        return _exchange_wait(f"wait_{name}", *in_flight.pop(name), "gather", after)[1]

    def get_weights(l, stream):
        if l == 0:
            (win,) = _gather_weights("gather_w_in0", shards[0][:1])
            token = start_gather("gather_rest0", shards[0][1:], win)
            token = start_gather("gather_w_in1", shards[1][:1], token)
            token = start_gather("gather_rest1", shards[1][1:], token)
            return (win, lambda after: finish_gather("gather_rest0", after)), token
        (win,) = finish_gather("gather_w_in1", stream)
        return (win, lambda after: finish_gather("gather_rest1", after)), None

    reduced = {}

    def start_exchange(name, grads):
        srcs, lands = [g for g, _ in grads], [land for _, land in grads]
        sems, passed, token = _exchange_start(f"start_{name}", srcs, lands, "scatter")
        in_flight[name] = (sems, passed)
        return token

    def finish_exchange(name, after):
        own, landed = _exchange_wait(f"wait_{name}", *in_flight.pop(name), "scatter", after)
        return [_sum_devices(f"sum_{name}_{i}", p, g, place) for i, (p, g) in enumerate(zip(landed, own))]

    def on_grads(l, group, grads):
        token = start_exchange(f"{group}{l}", grads)
        if (l, group) == (0, "early"):
            reduced[(1, "early")] = finish_exchange("early1", token)
            reduced[(1, "last")] = finish_exchange("last1", token)
        if (l, group) == (0, "last"):
            reduced[(0, "early")] = finish_exchange("early0", token)
        return token

    loss, dx, dg_final, sg = _local_step(x[0], loss_target[0], small, get_weights, on_grads)

    big_m = (m_w_in, m_w_out, m_w_up, m_w_down)
    big_v = (v_w_in, v_w_out, v_w_up, v_w_down)
    names = ("w_in", "w_out", "w_up", "w_down")
    big_g, big_delta, big_new_m, big_new_v = [None] * 4, [None] * 4, [None] * 4, [None] * 4

    def finish_weights(group, which):
        whole = _share_halves(f"share_{group}", [[reduced[(l, group)][i] for l in range(DEPTH)]
                                                 for i in range(len(which))])
        for i, w in enumerate(which):
            shape = big_w[w].shape
            flat = lambda arr: arr.reshape(shape[0] * shape[1], shape[2])
            d, m2, v2 = _adamw(f"adamw_{names[w]}", flat(big_w[w]), flat(whole[i]), flat(big_m[w]), flat(big_v[w]))
            big_g[w], big_delta[w] = whole[i], d.reshape(shape)
            big_new_m[w], big_new_v[w] = m2.reshape(shape), v2.reshape(shape)

    finish_weights("early", (2, 3, 1))
    reduced[(0, "last")] = finish_exchange("last0", big_delta[3])
    finish_weights("last", (0,))

    stack2 = lambda key: jnp.stack([sg[l][key] for l in range(DEPTH)])
    pack = _pack_small(stack2("norm_mix"), stack2("attn_out_gain"), stack2("lower"), stack2("hgrn_out_gain"),
                       stack2("norm_mlp"), dg_final, jnp.broadcast_to(loss[0, 0], (D_MODEL,)))
    g_mix, g_attn, g_lower, g_hgrn, g_mlp, g_final, loss_row = _unpack_small(_all_reduce_small(pack, big_delta[0]))
    (g_logits,) = lower_vjp(g_lower)

    zeros_row = jnp.zeros((D_MODEL,), F32)
    small_w = (norm_mix, attn_out_gain, hgrn_lb_logits, hgrn_out_gain, norm_mlp, norm_final)
    small_m = (m_norm_mix, m_attn_out_gain, m_hgrn_lb_logits, m_hgrn_out_gain, m_norm_mlp, m_norm_final)
    small_v = (v_norm_mix, v_attn_out_gain, v_hgrn_lb_logits, v_hgrn_out_gain, v_norm_mlp, v_norm_final)
    small_g = (g_mix, g_attn, g_logits, g_hgrn, g_mlp, g_final)
    packs = [_pack_small(*t, zeros_row) for t in (small_w, small_g, small_m, small_v)]
    small_delta, small_new_m, small_new_v = [_unpack_small(p)[:6] for p in _adamw("adamw_small", *packs)]

    def ordered(small6, big4):
        mix, attn, lbl, hg, mlp, fin = small6
        return (mix, big4[0], attn, lbl, hg, big4[1], mlp, big4[2], big4[3], fin)

    return ((loss_row[0], dx[None]) + ordered(small_g, big_g) + ordered(small_delta, big_delta)
            + ordered(small_new_m, big_new_m) + ordered(small_new_v, big_new_v))
```

```python
import numpy as np
import jax
import jax.numpy as jnp
from jax import lax
from jax.experimental import pallas as pl
from jax.experimental.pallas import tpu as pltpu

F32 = jnp.float32
BF16 = jnp.bfloat16
MESH = pl.DeviceIdType.MESH

SEQ = 4096
D_MODEL = 1024
DEPTH = 2
ATTN_WIDTH = 512
HEAD_DIM = 64
HGRN_HEADS = 4
HGRN_DIM = 128
HGRN_WIDTH = 512
IN_W = 3584
MLP_HIDDEN = 4096
N_CHIPS = 4
N_DEV = 8
DILATIONS = (1, 4, 16)
SPAN = 128
ROPE_THETA = 10000.0
NORM_EPS = 1e-6
MASK_VALUE = -1e30
CHUNK = 128
ROW_TILE = 512
MM_TILE = 512
VMEM_LIMIT = 52 * 1024 * 1024

ADAM_LR = 0.001
ADAM_B1 = 0.9
ADAM_B2 = 0.999
ADAM_EPS = 1e-08
ADAM_WD = 0.01
ADAM_STEP = 10

PACK_ROWS = 16


def _params(n_axes):
    return pltpu.CompilerParams(dimension_semantics=("arbitrary",) * n_axes,
                                vmem_limit_bytes=VMEM_LIMIT)


def _dot(a, b):
    return jnp.dot(a.astype(BF16), b.astype(BF16), preferred_element_type=F32)


def _dot_nt(a, b):
    return lax.dot_general(a.astype(BF16), b.astype(BF16), (((1,), (1,)), ((), ())),
                           preferred_element_type=F32)


def _dot_tn(a, b):
    return lax.dot_general(a.astype(BF16), b.astype(BF16), (((0,), (0,)), ((), ())),
                           preferred_element_type=F32)


def _sigmoid(x):
    return 1.0 / (1.0 + jnp.exp(-x))


def _spec(shape, index_map):
    return pl.BlockSpec(shape, index_map)


def _mm_pieces(name, a, w, nt, tm, epilogue="none", extra=None, after=None):
    s = a.shape[0]
    pw = w.shape[1] if nt else w.shape[2]
    width = N_CHIPS * pw

    def body(a_ref, w_ref, *rest):
        e_ref = rest[0] if extra is not None else None
        outs = rest[-3:] if epilogue == "relu2" else rest[-1:]
        av = a_ref[...].astype(BF16)
        for j in range(N_CHIPS):
            cols = slice(j * pw, (j + 1) * pw)
            r = _dot_nt(av, w_ref[j]) if nt else _dot(av, w_ref[j])
            if epilogue == "relu2":
                relu = jnp.maximum(r, 0.0)
                r = relu * relu
                outs[1][:, cols] = relu.astype(BF16)
                outs[2][cols, :] = r.T.astype(BF16)
            elif epilogue == "relu2_grad":
                r = r * (2.0 * e_ref[:, cols].astype(F32))
            outs[0][:, cols] = r.astype(outs[0].dtype)

    row = lambda width_: _spec((tm, width_), lambda i: (i, 0))
    in_specs = [row(a.shape[1]), _spec(w.shape, lambda i: (0, 0, 0))]
    args = [a, w]
    if extra is not None:
        in_specs.append(row(width))
        args.append(extra)
    if after is not None:
        in_specs.append(pl.BlockSpec(memory_space=pl.ANY))
        args.append(after)
    if epilogue == "relu2":
        out_specs = [row(width), row(width), _spec((width, tm), lambda i: (0, i))]
        out_shape = [jax.ShapeDtypeStruct((s, width), BF16)] * 2 + [jax.ShapeDtypeStruct((width, s), BF16)]
    else:
        out_specs = row(width)
        out_shape = jax.ShapeDtypeStruct((s, width), BF16 if epilogue == "relu2_grad" else F32)
    return pl.pallas_call(body, name=name, grid=(s // tm,), in_specs=in_specs, out_specs=out_specs,
                          out_shape=out_shape, compiler_params=_params(1))(*args)


def _mm_accum(name, a, w, nt, tm, resid, norm=None, after=None, next_gain=None, loss=None):
    pieces = list(a) if isinstance(a, (list, tuple)) else [a]
    n_a = len(pieces)
    s = pieces[0].shape[0]
    pk = w.shape[2] if nt else w.shape[1]
    d = w.shape[1] if nt else w.shape[2]

    def body(*refs):
        a_refs, w_ref, resid_ref, rest = refs[:n_a], refs[n_a], refs[n_a + 1], refs[n_a + 2:]
        av = a_refs[0][...] if n_a == 1 else jnp.concatenate([ref[...] for ref in a_refs], axis=1)
        r = None
        for j in range(N_CHIPS):
            piece = av[:, j * pk:(j + 1) * pk].astype(BF16)
            term = _dot_nt(piece, w_ref[j]) if nt else _dot(piece, w_ref[j])
            r = term if r is None else r + term
        if loss is not None:
            g_ref, t_ref = rest[:2]
            dx_ref, dxb_ref, dg_ref, loss_ref, acc = rest[-5:]
            i = pl.program_id(0)

            @pl.when(i == 0)
            def _():
                dg_ref[...] = jnp.zeros_like(dg_ref)
                acc[...] = jnp.zeros_like(acc)

            xv = r + resid_ref[...]
            g = g_ref[...]
            rs = lax.rsqrt(jnp.mean(xv * xv, axis=1, keepdims=True) + NORM_EPS)
            xhat = xv * rs
            err = xhat * g - t_ref[...]
            acc[...] += jnp.sum(err * err, axis=0, keepdims=True)
            dy = err * (1.0 / d)
            dyg = dy * g
            dx = rs * (dyg - xhat * jnp.mean(dyg * xhat, axis=1, keepdims=True))
            dx_ref[...] = dx
            dxb_ref[...] = dx.astype(BF16)
            dg_ref[...] += jnp.sum(dy * xhat, axis=0, keepdims=True)

            @pl.when(i == s // tm - 1)
            def _():
                total = jnp.sum(acc[...], axis=1, keepdims=True) * (0.5 / d)
                loss_ref[...] = jnp.broadcast_to(total, loss_ref.shape)

            return
        if norm is None and next_gain is None:
            rest[-1][...] = r + resid_ref[...]
            return
        if norm is None:
            g_ref = rest[0]
            x_out, h_out, ht_out = rest[-3:]
            xv = r + resid_ref[...]
            x_out[...] = xv
            h = xv * lax.rsqrt(jnp.mean(xv * xv, axis=1, keepdims=True) + NORM_EPS) * g_ref[...]
            h_out[...] = h.astype(BF16)
            ht_out[...] = h.T.astype(BF16)
            return
        x_ref, g_ref = rest[:2]
        dx_ref, dxb_ref, dg_ref = rest[-3:]

        @pl.when(pl.program_id(0) == 0)
        def _():
            dg_ref[...] = jnp.zeros_like(dg_ref)

        xv = x_ref[...]
        rs = lax.rsqrt(jnp.mean(xv * xv, axis=1, keepdims=True) + NORM_EPS)
        xhat = xv * rs
        rg = r * g_ref[...]
        dx = resid_ref[...] + rs * (rg - xhat * jnp.mean(rg * xhat, axis=1, keepdims=True))
        dx_ref[...] = dx
        dxb_ref[...] = dx.astype(BF16)
        dg_ref[...] += jnp.sum(r * xhat, axis=0, keepdims=True)

    row = lambda width: _spec((tm, width), lambda i: (i, 0))
    in_specs = [row(p.shape[1]) for p in pieces] + [_spec(w.shape, lambda i: (0, 0, 0)), row(d)]
    args = pieces + [w, resid]
    scratch = []
    if loss is not None:
        in_specs += [_spec((1, d), lambda i: (0, 0)), row(d)]
        args += list(loss)
        out_specs = [row(d), row(d), _spec((1, d), lambda i: (0, 0)), _spec((1, 128), lambda i: (0, 0))]
        out_shape = [jax.ShapeDtypeStruct((s, d), F32), jax.ShapeDtypeStruct((s, d), BF16),
                     jax.ShapeDtypeStruct((1, d), F32), jax.ShapeDtypeStruct((1, 128), F32)]
        scratch = [pltpu.VMEM((1, d), F32)]
    elif norm is None and next_gain is None:
        out_specs, out_shape = row(d), jax.ShapeDtypeStruct((s, d), F32)
    elif norm is None:
        in_specs.append(_spec((1, d), lambda i: (0, 0)))
        args.append(next_gain)
        out_specs = [row(d), row(d), _spec((d, tm), lambda i: (0, i))]
        out_shape = [jax.ShapeDtypeStruct((s, d), F32), jax.ShapeDtypeStruct((s, d), BF16),
                     jax.ShapeDtypeStruct((d, s), BF16)]
    else:
        in_specs += [row(d), _spec((1, d), lambda i: (0, 0))]
        args += list(norm)
        out_specs = [row(d), row(d), _spec((1, d), lambda i: (0, 0))]
        out_shape = [jax.ShapeDtypeStruct((s, d), F32), jax.ShapeDtypeStruct((s, d), BF16),
                     jax.ShapeDtypeStruct((1, d), F32)]
    if after is not None:
        in_specs.append(pl.BlockSpec(memory_space=pl.ANY))
        args.append(after)
    return pl.pallas_call(body, name=name, grid=(s // tm,), in_specs=in_specs, out_specs=out_specs,
                          out_shape=out_shape, scratch_shapes=scratch, compiler_params=_params(1))(*args)


def _mm_dw(name, a_t, b, by_cols, tk):
    pieces = list(b) if isinstance(b, (list, tuple)) else [b]
    n_b = len(pieces)
    m, s = a_t.shape
    n = sum(p.shape[1] for p in pieces)
    shape = (N_CHIPS, m, n // N_CHIPS) if by_cols else (N_CHIPS, m // N_CHIPS, n)
    n_steps = s // tk

    def body(a_ref, *rest):
        b_refs, o_ref, acc = rest[:n_b], rest[n_b], rest[-1]

        @pl.when(pl.program_id(0) == 0)
        def _():
            acc[...] = jnp.zeros_like(acc)

        bv = b_refs[0][...] if n_b == 1 else jnp.concatenate([ref[...] for ref in b_refs], axis=1)
        for j in range(N_CHIPS):
            if by_cols:
                acc[j] += _dot(a_ref[...], bv[:, j * shape[2]:(j + 1) * shape[2]])
            else:
                acc[j] += _dot(a_ref[j * shape[1]:(j + 1) * shape[1], :], bv)

        @pl.when(pl.program_id(0) == n_steps - 1)
        def _():
            o_ref[...] = acc[...].astype(BF16)

    return pl.pallas_call(
        body, name=name, grid=(n_steps,),
        in_specs=[_spec((m, tk), lambda k: (0, k))] + [_spec((tk, p.shape[1]), lambda k: (k, 0)) for p in pieces],
        out_specs=[_spec(shape, lambda k: (0, 0, 0)), ANY_SPEC],
        out_shape=[jax.ShapeDtypeStruct(shape, BF16),
                   jax.ShapeDtypeStruct((N_DEV, shape[1] // 2, shape[2]), BF16)],
        scratch_shapes=[pltpu.VMEM(shape, F32)],
        compiler_params=_params(1))(a_t, *pieces)


def _rms_fwd(name, x, gain):
    s, d = x.shape
    t = ROW_TILE

    def body(x_ref, g_ref, h_ref, ht_ref):
        xv = x_ref[...]
        r = lax.rsqrt(jnp.mean(xv * xv, axis=1, keepdims=True) + NORM_EPS)
        h = xv * r * g_ref[...]
        h_ref[...] = h.astype(BF16)
        ht_ref[...] = h.T.astype(BF16)

    return pl.pallas_call(
        body, name=name, grid=(s // t,),
        in_specs=[_spec((t, d), lambda i: (i, 0)), _spec((1, d), lambda i: (0, 0))],
        out_specs=[_spec((t, d), lambda i: (i, 0)), _spec((d, t), lambda i: (0, i))],
        out_shape=[jax.ShapeDtypeStruct((s, d), BF16), jax.ShapeDtypeStruct((d, s), BF16)],
        compiler_params=_params(1),
    )(x, gain)


def _rope_tables():
    half = HEAD_DIM // 2
    inv_freq = ROPE_THETA ** (-jnp.arange(half, dtype=F32) / half)
    ang = jnp.arange(SEQ, dtype=jnp.int32).astype(F32)[:, None] * inv_freq[None, :]
    cos, sin = jnp.cos(ang), jnp.sin(ang)
    cos_t = jnp.concatenate([cos, cos, cos, cos], axis=1)
    sin_t = jnp.concatenate([-sin, sin, -sin, sin], axis=1)
    return cos_t, sin_t


def _swap_halves(x):
    lane = lax.broadcasted_iota(jnp.int32, x.shape, 1)
    first = (lane % HEAD_DIM) < (HEAD_DIM // 2)
    return jnp.where(first, pltpu.roll(x, 128 - HEAD_DIM // 2, 1), pltpu.roll(x, HEAD_DIM // 2, 1))


def _permuted_specs(t, width):
    specs = [_spec((t, width), lambda i: (i, 0))]
    for d in DILATIONS[1:]:
        specs.append(_spec((d, t // d, width), lambda i: (0, i, 0)))
    return specs


def _permuted_shapes(width, dtype):
    shapes = [jax.ShapeDtypeStruct((SEQ, width), dtype)]
    for d in DILATIONS[1:]:
        shapes.append(jax.ShapeDtypeStruct((d, SEQ // d, width), dtype))
    return shapes


def _attn_prep(name, proj, cos_t, sin_t):
    t = ROW_TILE
    w = ATTN_WIDTH

    def body(q_ref, k_ref, v_ref, cos_ref, sin_ref, *rest):
        outs, scr = rest[:9], rest[9]
        cosv, sinv = cos_ref[...], sin_ref[...]
        for a, (src, roped, scale) in enumerate(((q_ref, True, HEAD_DIM ** -0.5),
                                                 (k_ref, True, 1.0), (v_ref, False, 1.0))):
            o1, o4, o16 = outs[3 * a:3 * a + 3]
            for cb in range(w // 128):
                cols = slice(cb * 128, (cb + 1) * 128)
                val = src[:, cols]
                if roped:
                    val = (val * cosv + _swap_halves(val) * sinv) * scale
                scr[...] = val
                o1[:, cols] = val.astype(BF16)
                for o_ref, d in ((o4, 4), (o16, 16)):
                    for r in range(d):
                        o_ref[r, :, cols] = scr[pl.ds(r, t // d, stride=d), :].astype(BF16)

    out_specs = _permuted_specs(t, w) * 3
    out_shape = _permuted_shapes(w, BF16) * 3
    outs = pl.pallas_call(
        body, name=name, grid=(SEQ // t,),
        in_specs=[_spec((t, w), lambda i: (i, 0)), _spec((t, w), lambda i: (i, 1)),
                  _spec((t, w), lambda i: (i, 2)),
                  _spec((t, 128), lambda i: (i, 0)), _spec((t, 128), lambda i: (i, 0))],
        out_specs=out_specs, out_shape=out_shape,
        scratch_shapes=[pltpu.VMEM((t, 128), F32)],
        compiler_params=_params(1),
    )(proj, proj, proj, cos_t, sin_t)
    q, k, v = outs[0:3], outs[3:6], outs[6:9]
    flat = lambda arr: arr.reshape(SEQ, w)
    return [(flat(q[p]), flat(k[p]), flat(v[p])) for p in range(3)]


def _band_masks():
    row = lax.broadcasted_iota(jnp.int32, (2 * SPAN, 2 * SPAN), 0) % SPAN
    col = lax.broadcasted_iota(jnp.int32, (2 * SPAN, 2 * SPAN), 1)
    is_prev = col < SPAN
    band = (is_prev & (col >= row)) | (~is_prev & (col - SPAN <= row))
    head0 = lax.broadcasted_iota(jnp.int32, (SPAN, 128), 1) < HEAD_DIM
    return band, is_prev, head0


def _stack_heads(x, head0):
    zero = jnp.zeros_like(x)
    return jnp.concatenate([jnp.where(head0, x, zero), jnp.where(head0, zero, x)], axis=0)


ATTN_UNROLL = 32


def _for_each_block(block, seg_blocks):
    def trip(i, carry):
        for u in range(ATTN_UNROLL):
            static = seg_blocks <= ATTN_UNROLL
            block(i * ATTN_UNROLL + u, (u % seg_blocks == 0) if static else None)
        return carry

    lax.fori_loop(0, SEQ // SPAN // ATTN_UNROLL, trip, 0)


def _attn_fwd(name, q, k, v, seg_blocks):
    def body(q_ref, k_ref, v_ref, o_ref, lse_ref):
        band, is_prev, head0 = _band_masks()

        def block(b, first):
            cur = pl.ds(pl.multiple_of(b * SPAN, SPAN), SPAN)
            qs = _stack_heads(q_ref[cur, :], head0)
            if first is True:
                kcat, vcat, ok = k_ref[cur, :], v_ref[cur, :], band[:, SPAN:]
            else:
                prev = pl.ds(pl.multiple_of(jnp.maximum(b - 1, 0) * SPAN, SPAN), SPAN)
                kcat = jnp.concatenate([k_ref[prev, :], k_ref[cur, :]], axis=0)
                vcat = jnp.concatenate([v_ref[prev, :], v_ref[cur, :]], axis=0)
                ok = band if first is False else band & (((b % seg_blocks) != 0) | ~is_prev)
            s = jnp.where(ok, _dot_nt(qs, kcat), MASK_VALUE)
            m = jnp.max(s, axis=1, keepdims=True)
            p = jnp.exp(s - m)
            l = jnp.sum(p, axis=1, keepdims=True)
            pv = _dot(p, vcat) * (1.0 / l)
            lse = m + jnp.log(l)
            o_ref[cur, :] = jnp.where(head0, pv[:SPAN], pv[SPAN:])
            lse_ref[cur, :] = jnp.where(head0, lse[:SPAN], lse[SPAN:])

        _for_each_block(block, seg_blocks)

    col = _spec((SEQ, 128), lambda j: (0, j))
    return pl.pallas_call(
        body, name=name, grid=(ATTN_WIDTH // 128,),
        in_specs=[col, col, col], out_specs=[col, col],
        out_shape=[jax.ShapeDtypeStruct((SEQ, ATTN_WIDTH), F32)] * 2,
        compiler_params=_params(1),
    )(q, k, v)


def _unpermute(dst, src_ref, d, cols):
    n = dst.shape[0] // d
    for r in range(d):
        dst[pl.ds(r, n, stride=d), :] = src_ref[r, :, cols].astype(dst.dtype)


def _attn_merge(name, outs, lses, gain):
    t = ROW_TILE
    w = ATTN_WIDTH

    def body(o1, o4, o16, l1, l4, l16, g_ref, an_ref, ant_ref, attn_ref, lse_ref, so4, so16, sl4, sl16):
        for cb in range(w // 128):
            cols = slice(cb * 128, (cb + 1) * 128)
            _unpermute(so4, o4, 4, cols)
            _unpermute(so16, o16, 16, cols)
            _unpermute(sl4, l4, 4, cols)
            _unpermute(sl16, l16, 16, cols)
            la, lb, lc = l1[:, cols], sl4[...], sl16[...]
            m = jnp.maximum(jnp.maximum(la, lb), lc)
            ea, eb, ec = jnp.exp(la - m), jnp.exp(lb - m), jnp.exp(lc - m)
            tot = ea + eb + ec
            attn_ref[:, cols] = (ea * o1[:, cols] + eb * so4[...] + ec * so16[...]) / tot
            lse_ref[:, cols] = m + jnp.log(tot)
        attn = attn_ref[...]
        r = lax.rsqrt(jnp.mean(attn * attn, axis=1, keepdims=True) + NORM_EPS)
        an = attn * r * g_ref[...]
        an_ref[...] = an.astype(BF16)
        ant_ref[...] = an.T.astype(BF16)

    views = lambda arrs: [arrs[0], arrs[1].reshape(4, SEQ // 4, w), arrs[2].reshape(16, SEQ // 16, w)]
    row = _spec((t, w), lambda i: (i, 0))
    return pl.pallas_call(
        body, name=name, grid=(SEQ // t,),
        in_specs=_permuted_specs(t, w) * 2 + [_spec((1, w), lambda i: (0, 0))],
        out_specs=[row, _spec((w, t), lambda i: (0, i)), row, row],
        out_shape=[jax.ShapeDtypeStruct((SEQ, 2 * w), BF16), jax.ShapeDtypeStruct((2 * w, SEQ), BF16),
                   jax.ShapeDtypeStruct((SEQ, w), F32), jax.ShapeDtypeStruct((SEQ, w), F32)],
        scratch_shapes=[pltpu.VMEM((t, 128), F32)] * 4,
        compiler_params=_params(1),
    )(*views(outs), *views(lses), gain)


def _head_sum_matrix():
    i = np.arange(ATTN_WIDTH)
    return jnp.asarray((i[:, None] // HEAD_DIM) == (i[None, :] // HEAD_DIM), dtype=F32)


def _attn_bwd_prep(name, d_an, attn, lse, gain, head_sum):
    t = ROW_TILE
    w = ATTN_WIDTH

    def body(dan_ref, attn_ref, lse_ref, g_ref, hs_ref, *rest):
        (do1, do4, do16, dl1, dl4, dl16, ls4, ls16, dg_ref), (sdo, sdl, sls) = rest[:9], rest[9:]

        @pl.when(pl.program_id(0) == 0)
        def _():
            dg_ref[...] = jnp.zeros_like(dg_ref)

        attn = attn_ref[...]
        dan = dan_ref[...]
        r = lax.rsqrt(jnp.mean(attn * attn, axis=1, keepdims=True) + NORM_EPS)
        xhat = attn * r
        dg_ref[...] += jnp.sum(dan * xhat, axis=0, keepdims=True)
        dang = dan * g_ref[...]
        d_o = r * (dang - xhat * jnp.mean(dang * xhat, axis=1, keepdims=True))
        delta = jnp.dot(d_o * attn, hs_ref[...], preferred_element_type=F32,
                        precision=lax.Precision.HIGHEST)
        do1[...] = d_o.astype(BF16)
        dl1[...] = delta
        for cb in range(w // 128):
            cols = slice(cb * 128, (cb + 1) * 128)
            sdo[...] = d_o[:, cols]
            sdl[...] = delta[:, cols]
            sls[...] = lse_ref[:, cols]
            for d, o_do, o_dl, o_ls in ((4, do4, dl4, ls4), (16, do16, dl16, ls16)):
                for rr in range(d):
                    rows = pl.ds(rr, t // d, stride=d)
                    o_do[rr, :, cols] = sdo[rows, :].astype(BF16)
                    o_dl[rr, :, cols] = sdl[rows, :]
                    o_ls[rr, :, cols] = sls[rows, :]

    row = _spec((t, w), lambda i: (i, 0))
    perm = _permuted_specs(t, w)
    outs = pl.pallas_call(
        body, name=name, grid=(SEQ // t,),
        in_specs=[row, row, row, _spec((1, w), lambda i: (0, 0)), _spec((w, w), lambda i: (0, 0))],
        out_specs=perm + perm + perm[1:] + [_spec((1, w), lambda i: (0, 0))],
        out_shape=(_permuted_shapes(w, BF16) + _permuted_shapes(w, F32) + _permuted_shapes(w, F32)[1:]
                   + [jax.ShapeDtypeStruct((1, w), F32)]),
        scratch_shapes=[pltpu.VMEM((t, 128), F32)] * 3,
        compiler_params=_params(1),
    )(d_an, attn, lse, gain, head_sum)
    flat = lambda arr: arr.reshape(SEQ, w)
    d_out = [flat(a) for a in outs[0:3]]
    delta = [flat(a) for a in outs[3:6]]
    lses = [lse, flat(outs[6]), flat(outs[7])]
    return d_out, delta, lses, outs[8]


def _attn_bwd(name, q, k, v, d_out, delta, lse, seg_blocks):
    def body(q_ref, k_ref, v_ref, do_ref, dl_ref, lse_ref, dq_ref, dk_out, dv_out, dk_ref, dv_ref):
        band, is_prev, head0 = _band_masks()
        dk_ref[...] = jnp.zeros_like(dk_ref)
        dv_ref[...] = jnp.zeros_like(dv_ref)

        def per_head(x):
            return jnp.concatenate([x[:, 0:1], x[:, HEAD_DIM:HEAD_DIM + 1]], axis=0)

        def block(b, first):
            cur = pl.ds(pl.multiple_of(b * SPAN, SPAN), SPAN)
            qs = _stack_heads(q_ref[cur, :], head0)
            dos = _stack_heads(do_ref[cur, :], head0)
            if first is True:
                kcat, vcat, ok = k_ref[cur, :], v_ref[cur, :], band[:, SPAN:]
            else:
                prev = pl.ds(pl.multiple_of(jnp.maximum(b - 1, 0) * SPAN, SPAN), SPAN)
                kcat = jnp.concatenate([k_ref[prev, :], k_ref[cur, :]], axis=0)
                vcat = jnp.concatenate([v_ref[prev, :], v_ref[cur, :]], axis=0)
                ok = band if first is False else band & (((b % seg_blocks) != 0) | ~is_prev)
            p = jnp.where(ok, jnp.exp(_dot_nt(qs, kcat) - per_head(lse_ref[cur, :])), 0.0)
            ds = p * (_dot_nt(dos, vcat) - per_head(dl_ref[cur, :]))
            dq = _dot(ds, kcat)
            dq_ref[cur, :] = jnp.where(head0, dq[:SPAN], dq[SPAN:]).astype(BF16)
            dk = _dot_tn(ds, qs)
            dv = _dot_tn(p, dos)
            if first is not True:
                dk_ref[prev, :] += dk[:SPAN]
                dv_ref[prev, :] += dv[:SPAN]
            dk_ref[cur, :] += dk[-SPAN:]
            dv_ref[cur, :] += dv[-SPAN:]

        _for_each_block(block, seg_blocks)
        dk_out[...] = dk_ref[...].astype(BF16)
        dv_out[...] = dv_ref[...].astype(BF16)

    col = _spec((SEQ, 128), lambda j: (0, j))
    return pl.pallas_call(
        body, name=name, grid=(ATTN_WIDTH // 128,),
        in_specs=[col] * 6, out_specs=[col] * 3,
        out_shape=[jax.ShapeDtypeStruct((SEQ, ATTN_WIDTH), BF16)] * 3,
        scratch_shapes=[pltpu.VMEM((SEQ, 128), F32)] * 2,
        compiler_params=_params(1),
    )(q, k, v, d_out, delta, lse)


def _attn_bwd_post(name, grads, cos_t, sin_t):
    t = ROW_TILE
    w = ATTN_WIDTH

    def body(*refs):
        ins, cos_ref, sin_ref, out_ref, s4, s16 = refs[:9], refs[9], refs[10], refs[11], refs[12], refs[13]
        cosv, sinv = cos_ref[...], sin_ref[...]
        for a in range(3):
            g1, g4, g16 = ins[a], ins[3 + a], ins[6 + a]
            for cb in range(w // 128):
                cols = slice(cb * 128, (cb + 1) * 128)
                _unpermute(s4, g4, 4, cols)
                _unpermute(s16, g16, 16, cols)
                val = g1[:, cols].astype(F32) + s4[...] + s16[...]
                if a < 2:
                    val = val * cosv + _swap_halves(val * sinv)
                if a == 0:
                    val = val * (HEAD_DIM ** -0.5)
                out_ref[:, a * w + cb * 128:a * w + (cb + 1) * 128] = val.astype(BF16)

    views = []
    for p, d in enumerate(DILATIONS):
        for a in range(3):
            views.append(grads[p][a] if d == 1 else grads[p][a].reshape(d, SEQ // d, w))
    perm = _permuted_specs(t, w)
    in_specs = [perm[0]] * 3 + [perm[1]] * 3 + [perm[2]] * 3
    return pl.pallas_call(
        body, name=name, grid=(SEQ // t,),
        in_specs=in_specs + [_spec((t, 128), lambda i: (i, 0))] * 2,
        out_specs=_spec((t, 3 * w), lambda i: (i, 0)),
        out_shape=jax.ShapeDtypeStruct((SEQ, 3 * w), BF16),
        scratch_shapes=[pltpu.VMEM((t, 128), F32)] * 2,
        compiler_params=_params(1),
    )(*views, cos_t, sin_t)


N_LEVELS = 7
HGRN_PAIR = 4


def _hgrn_consts():
    c = CHUNK
    i = np.arange(c)[:, None]
    s = np.arange(c)[None, :]
    blocks = [s <= i]
    for lv in range(N_LEVELS):
        bs = c >> lv
        h = bs // 2
        m = (i // bs) * bs + h - 1
        second = (i % bs) >= h
        blocks.append((second & (s > m) & (s <= i)) | (~second & (s > i) & (s <= m)))
    blocks.append(s > i)
    stack = np.concatenate(blocks, axis=0).astype(np.float32)
    twice = np.concatenate([stack, stack], axis=1)
    return jnp.asarray(twice, dtype=BF16), jnp.asarray(stack.T, dtype=BF16)


def _split(x):
    hi = x.astype(BF16)
    return hi, (x - hi.astype(F32)).astype(BF16)


def _exact_dot(m01, x):
    n = x.shape[1]
    full = jnp.dot(m01, jnp.concatenate(_split(x), axis=1), preferred_element_type=F32)
    return full[:, :n] + full[:, n:]


def _exact_dot_twice(m01_twice, x):
    return jnp.dot(m01_twice, jnp.concatenate(_split(x), axis=0), preferred_element_type=F32)


def _hgrn_gates(qh, z, lb):
    sq = _sigmoid(qh)
    q = qh * sq * (HGRN_DIM ** -0.5)
    sig = _sigmoid(z)
    sigm = _sigmoid(-z)
    f = lb + (1.0 - lb) * sig
    k = (1.0 - lb) * sigm
    return q, k, f, sq, sig, sigm


def _level_masks(lv):
    row = lax.broadcasted_iota(jnp.int32, (CHUNK, CHUNK), 0)
    col = lax.broadcasted_iota(jnp.int32, (CHUNK, CHUNK), 1)
    shift = N_LEVELS - lv
    half = CHUNK >> (lv + 1)
    second = (row & half) != 0
    second_col = (col & half) != 0
    same = (row >> shift) == (col >> shift)
    return second, same & second & ~second_col, same & (second != second_col)


def _hgrn_fwd(name, proj, lb, gain, stack, mixed, mixed_t):
    t = ROW_TILE
    per = t // CHUNK
    n_rb = SEQ // t
    n_chunks = SEQ // CHUNK
    col0 = 3 * ATTN_WIDTH // 128
    pair_w = HGRN_PAIR * HGRN_DIM

    def body(q_ref, f_ref, i_ref, g_ref, lb_ref, gain_ref, stack_ref, mixed_in, mixed_t_in,
             rec_ref, rect_ref, o_ref, st_out, a_out, st):
        del mixed_in, mixed_t_in

        @pl.when(pl.program_id(1) == 0)
        def _():
            st[...] = jnp.zeros_like(st)

        row = lax.broadcasted_iota(jnp.int32, (CHUNK, CHUNK), 0)
        col = lax.broadcasted_iota(jnp.int32, (CHUNK, CHUNK), 1)
        masks = [_level_masks(lv) for lv in range(N_LEVELS)]
        for c, hh in [(c, hh) for c in range(per) for hh in range(HGRN_PAIR)]:
            rows = slice(c * CHUNK, (c + 1) * CHUNK)
            lanes = slice(hh * HGRN_DIM, (hh + 1) * HGRN_DIM)
            lbv = lb_ref[hh]
            qh, z, v, gh = q_ref[rows, lanes], f_ref[rows, lanes], i_ref[rows, lanes], g_ref[rows, lanes]
            q, k, f, _, _, _ = _hgrn_gates(qh, z, lbv)
            dec = _exact_dot_twice(stack_ref[...], jnp.log(f))
            g = dec[0:CHUNK]
            to_end = dec[(N_LEVELS + 1) * CHUNK:(N_LEVELS + 2) * CHUNK]
            a = jnp.where(row == col, jnp.sum(q * k, axis=1, keepdims=True), 0.0)
            for lv in range(N_LEVELS):
                second, square, _ = masks[lv]
                t = jnp.where(second, q, k) * jnp.exp(dec[(lv + 1) * CHUNK:(lv + 2) * CHUNK])
                a = a + jnp.where(square, _dot_nt(t, t), 0.0)
            st_prev = st[hh]
            st_out[hh, c] = st_prev
            a_out[hh, c] = a
            o = _dot(a, v) + _dot_nt(q * jnp.exp(g), st_prev)
            k_end = k * jnp.exp(to_end)
            st[hh] = st_prev * jnp.exp(g[CHUNK - 1:CHUNK, :]) + _dot(v.T, k_end)
            o_ref[rows, lanes] = o
            r = lax.rsqrt(jnp.mean(o * o, axis=1, keepdims=True) + NORM_EPS)
            rec = o * r * gain_ref[...] * (gh * _sigmoid(gh))
            rec_ref[rows, lanes] = rec.astype(BF16)
            rect_ref[lanes, rows] = rec.T.astype(BF16)

    def col_spec(tt):
        return _spec((t, pair_w), lambda h, rb: (rb, (col0 + HGRN_HEADS * tt) // HGRN_PAIR + h))

    chunk_spec = _spec((HGRN_PAIR, per, CHUNK, CHUNK), lambda h, rb: (h, rb, 0, 0))
    return pl.pallas_call(
        body, name=name, grid=(HGRN_HEADS // HGRN_PAIR, n_rb),
        in_specs=[col_spec(0), col_spec(1), col_spec(2), col_spec(3),
                  _spec((HGRN_PAIR, 1, HGRN_DIM), lambda h, rb: (h, 0, 0)),
                  _spec((1, HGRN_DIM), lambda h, rb: (0, 0)),
                  _spec(stack.shape, lambda h, rb: (0, 0)), ANY_SPEC, ANY_SPEC],
        out_specs=[_spec((t, pair_w), lambda h, rb: (rb, ATTN_WIDTH // pair_w + h)),
                   _spec((pair_w, t), lambda h, rb: (ATTN_WIDTH // pair_w + h, rb)),
                   _spec((t, pair_w), lambda h, rb: (rb, h)),
                   chunk_spec, chunk_spec],
        out_shape=[jax.ShapeDtypeStruct(mixed.shape, BF16),
                   jax.ShapeDtypeStruct(mixed_t.shape, BF16),
                   jax.ShapeDtypeStruct((SEQ, HGRN_WIDTH), F32),
                   jax.ShapeDtypeStruct((HGRN_HEADS, n_chunks, CHUNK, CHUNK), F32),
                   jax.ShapeDtypeStruct((HGRN_HEADS, n_chunks, CHUNK, CHUNK), F32)],
        scratch_shapes=[pltpu.VMEM((HGRN_PAIR, CHUNK, CHUNK), F32)],
        input_output_aliases={7: 0, 8: 1},
        compiler_params=_params(2),
    )(proj, proj, proj, proj, lb, gain, stack, mixed, mixed_t)


def _hgrn_bwd(name, proj, d_rec, o_pre, states, scores, lb, gain, stack, stack_t):
    t = ROW_TILE
    per = t // CHUNK
    n_rb = SEQ // t
    col0 = 3 * ATTN_WIDTH // 128
    pair_w = HGRN_PAIR * HGRN_DIM

    def body(q_ref, f_ref, i_ref, g_ref, drec_ref, o_ref, st_ref, a_ref, lb_ref, gain_ref,
             stack_ref, stack_t_ref, dq_ref, df_ref, di_ref, dg_ref, dlb_ref, dgain_ref, dst):
        @pl.when(pl.program_id(1) == 0)
        def _():
            dst[...] = jnp.zeros_like(dst)
            dlb_ref[...] = jnp.zeros_like(dlb_ref)
            dgain_ref[...] = jnp.zeros_like(dgain_ref)

        gain_v = gain_ref[...]
        row = lax.broadcasted_iota(jnp.int32, (CHUNK, CHUNK), 0)
        col = lax.broadcasted_iota(jnp.int32, (CHUNK, CHUNK), 1)
        masks = [_level_masks(lv) for lv in range(N_LEVELS)]
        for c, hh in [(c, hh) for c in reversed(range(per)) for hh in range(HGRN_PAIR)]:
            rows = slice(c * CHUNK, (c + 1) * CHUNK)
            lanes = slice(hh * HGRN_DIM, (hh + 1) * HGRN_DIM)
            lbv = lb_ref[hh]
            qh, z, v, gh = q_ref[rows, lanes], f_ref[rows, lanes], i_ref[rows, lanes], g_ref[rows, lanes]
            q, k, f, sq, sig, sigm = _hgrn_gates(qh, z, lbv)
            dec = _exact_dot_twice(stack_ref[...], jnp.log(f))
            g = dec[0:CHUNK]
            to_end = dec[(N_LEVELS + 1) * CHUNK:(N_LEVELS + 2) * CHUNK]
            e_g = jnp.exp(g)
            e_end = jnp.exp(to_end)
            e_last = jnp.exp(g[CHUNK - 1:CHUNK, :])
            q_in = q * e_g
            k_end = k * e_end
            st_prev = st_ref[hh, c]
            a = a_ref[hh, c]
            dst_new = dst[hh]

            o = o_ref[rows, lanes]
            drec = drec_ref[rows, lanes]
            sg = _sigmoid(gh)
            r = lax.rsqrt(jnp.mean(o * o, axis=1, keepdims=True) + NORM_EPS)
            ohat = o * r
            d_gh = drec * (ohat * gain_v) * (sg * (1.0 + gh * (1.0 - sg)))
            d_on = drec * (gh * sg)
            dgain_ref[hh] += jnp.sum(d_on * ohat, axis=0, keepdims=True)
            d_ohat = d_on * gain_v
            d_o = r * (d_ohat - ohat * jnp.mean(d_ohat * ohat, axis=1, keepdims=True))

            d_sym = jnp.where(row >= col, _dot_nt(d_o, v), _dot_nt(v, d_o))
            d_v = _dot(a.T, d_o) + _dot_nt(k_end, dst_new)
            d_q_in = _dot(d_o, st_prev)
            d_k_end = _dot(v, dst_new)
            d_q = d_q_in * e_g
            d_k = d_k_end * e_end
            diag = jnp.sum(d_o * v, axis=1, keepdims=True)
            d_q = d_q + diag * k
            d_k = d_k + diag * q
            d_dec = [q_in * d_q_in]
            d_both, d_second = None, None
            for lv in range(N_LEVELS):
                e = jnp.exp(dec[(lv + 1) * CHUNK:(lv + 2) * CHUNK])
                second, _, mirrored = masks[lv]
                t = jnp.where(second, q, k) * e
                d_t = _dot(jnp.where(mirrored, d_sym, 0.0), t)
                d_te = d_t * e
                d_both = d_te if d_both is None else d_both + d_te
                d_second = jnp.where(second, d_te, 0.0) if d_second is None else d_second + jnp.where(second, d_te, 0.0)
                d_dec.append(t * d_t)
            d_q = d_q + d_second
            d_k = d_k + (d_both - d_second)
            d_dec.append(k_end * d_k_end)
            flux = jnp.sum(dst_new * st_prev, axis=0, keepdims=True) * e_last
            d_lf = _exact_dot(stack_t_ref[...], jnp.concatenate(d_dec, axis=0)) + flux
            dst[hh] = dst_new * e_last + _dot(d_o.T, q_in)

            d_f = d_lf / f - d_k
            dlb_ref[hh] += jnp.sum(d_f * sigm, axis=0, keepdims=True)
            dq_ref[rows, lanes] = (d_q * (HGRN_DIM ** -0.5) * (sq * (1.0 + qh * (1.0 - sq)))).astype(BF16)
            df_ref[rows, lanes] = (d_f * (1.0 - lbv) * sig * sigm).astype(BF16)
            di_ref[rows, lanes] = d_v.astype(BF16)
            dg_ref[rows, lanes] = d_gh.astype(BF16)

    last = n_rb - 1

    def col_spec(tt):
        return _spec((t, pair_w), lambda h, rb: (last - rb, (col0 + HGRN_HEADS * tt) // HGRN_PAIR + h))

    head_col = _spec((t, pair_w), lambda h, rb: (last - rb, h))
    rec_col0 = (d_rec.shape[1] - HGRN_WIDTH) // pair_w
    d_rec_col = _spec((t, pair_w), lambda h, rb: (last - rb, rec_col0 + h))
    chunk_spec = _spec((HGRN_PAIR, per, CHUNK, CHUNK), lambda h, rb: (h, last - rb, 0, 0))
    vec_spec = _spec((HGRN_PAIR, 1, HGRN_DIM), lambda h, rb: (h, 0, 0))
    outs = pl.pallas_call(
        body, name=name, grid=(HGRN_HEADS // HGRN_PAIR, n_rb),
        in_specs=[col_spec(0), col_spec(1), col_spec(2), col_spec(3), d_rec_col, head_col,
                  chunk_spec, chunk_spec, vec_spec,
                  _spec((1, HGRN_DIM), lambda h, rb: (0, 0)),
                  _spec(stack.shape, lambda h, rb: (0, 0)), _spec(stack_t.shape, lambda h, rb: (0, 0))],
        out_specs=[head_col] * 4 + [vec_spec, vec_spec],
        out_shape=[jax.ShapeDtypeStruct((SEQ, HGRN_WIDTH), BF16)] * 4
                  + [jax.ShapeDtypeStruct((HGRN_HEADS, 1, HGRN_DIM), F32)] * 2,
        scratch_shapes=[pltpu.VMEM((HGRN_PAIR, CHUNK, CHUNK), F32)],
        compiler_params=_params(2),
    )(proj, proj, proj, proj, d_rec, o_pre, states, scores, lb, gain, stack, stack_t)
    return outs


ANY_SPEC = pl.BlockSpec(memory_space=pl.ANY)


def _my_place():
    return lax.axis_index("x"), lax.axis_index("y"), lax.axis_index("c")


def _other_chips(x, y):
    return [(1 - x, y), (x, 1 - y), (1 - x, 1 - y)]


def _remote(src, dst, send_sem, recv_sem, device):
    return pltpu.make_async_remote_copy(src_ref=src, dst_ref=dst, send_sem=send_sem, recv_sem=recv_sem,
                                        device_id=device, device_id_type=MESH)


def _staged_copies(srcs, dsts, stage, sems):
    loads = [pltpu.make_async_copy(srcs[i], stage[i], sems.at[i]) for i in range(len(srcs))]
    for cp in loads:
        cp.start()
    stores = []
    for i, cp in enumerate(loads):
        cp.wait()
        stores.append(pltpu.make_async_copy(stage[i], dsts[i], sems.at[i]))
        stores[-1].start()
    return stores


def _gather_weights(name, shards):
    n = len(shards)

    def body(*refs):
        ins, outs = refs[:n], refs[n:2 * n]
        ici_send, ici_recv, d2d_send, d2d_recv, local_sems = refs[2 * n:2 * n + 5]
        stage = refs[2 * n + 5:]
        x, y, c = _my_place()
        me = 2 * x + y
        chips = _other_chips(x, y)

        def half(i, which):
            h = ins[i].shape[0] // 2
            return pl.ds(which * h, h)

        sends = []
        for i in range(n):
            for j, (px, py) in enumerate(chips):
                sends.append(_remote(ins[i].at[half(i, c), :], outs[i].at[me, half(i, c), :],
                                     ici_send.at[3 * i + j], ici_recv.at[3 * i + j], (px, py, c)))
        for cp in sends:
            cp.start()
        local = _staged_copies(ins, [outs[i].at[me] for i in range(n)], stage, local_sems)
        for i in range(n):
            for j, (px, py) in enumerate(chips):
                landed = outs[i].at[2 * px + py, half(i, c), :]
                _remote(landed, landed, ici_send.at[3 * i + j], ici_recv.at[3 * i + j], (px, py, c)).wait_recv()
                forward = _remote(landed, landed, d2d_send.at[3 * i + j], d2d_recv.at[3 * i + j], (x, y, 1 - c))
                forward.start()
                sends.append(forward)
        for i in range(n):
            for j, (px, py) in enumerate(chips):
                other = outs[i].at[2 * px + py, half(i, 1 - c), :]
                _remote(other, other, d2d_send.at[3 * i + j], d2d_recv.at[3 * i + j], (x, y, 1 - c)).wait_recv()
        for cp in sends:
            cp.wait_send()
        for cp in local:
            cp.wait()

    return pl.pallas_call(
        body, name=name, in_specs=[ANY_SPEC] * n, out_specs=[ANY_SPEC] * n,
        out_shape=[jax.ShapeDtypeStruct((N_CHIPS,) + s.shape, s.dtype) for s in shards],
        scratch_shapes=([pltpu.SemaphoreType.DMA((3 * n,))] * 4 + [pltpu.SemaphoreType.DMA((n,))]
                        + [pltpu.VMEM(s.shape, s.dtype) for s in shards]),
        compiler_params=pltpu.CompilerParams(vmem_limit_bytes=VMEM_LIMIT),
    )(*shards)


HBM_SPEC = pl.BlockSpec(memory_space=pltpu.HBM)
SEM_SPEC = pl.BlockSpec(memory_space=pltpu.SEMAPHORE)
SPLIT_PARAMS = pltpu.CompilerParams(has_side_effects=pltpu.SideEffectType.DATAFLOW_SIDE_EFFECTING)
N_PEERS = {"gather": N_CHIPS - 1, "scatter": N_DEV - 1}


def _split_copies(ins, lands, send_sems, recv_sems, kind):
    x, y, c = _my_place()
    pairs = []
    for i in range(len(ins)):
        if kind == "gather":
            me = 2 * x + y
            for j, (px, py) in enumerate(_other_chips(x, y)):
                sems = (send_sems.at[3 * i + j], recv_sems.at[3 * i + j], (px, py, c))
                pairs.append((_remote(ins[i], lands[i].at[me], *sems),
                              _remote(ins[i], lands[i].at[2 * px + py], *sems)))
        else:
            me = 4 * x + 2 * y + c
            h = ins[i].shape[1] // 2
            for k in range(1, N_DEV):
                px, py, pc = (x + (k >> 2)) % 2, (y + ((k >> 1) & 1)) % 2, (c + (k & 1)) % 2
                src = ins[i].at[2 * px + py, pl.ds(pc * h, h), :]
                sems = (send_sems.at[7 * i + k - 1], recv_sems.at[7 * i + k - 1], (px, py, pc))
                pairs.append((_remote(src, lands[i].at[me], *sems),
                              _remote(src, lands[i].at[4 * px + 2 * py + pc], *sems)))
    return pairs


def _exchange_start(name, srcs, lands, kind, after=None):
    n = len(srcs)
    n_sems = N_PEERS[kind] * n
    extra = [] if after is None else [after]

    def body(*refs):
        ins, land_refs = refs[:n], refs[n:2 * n]
        send_sems, recv_sems = refs[2 * n + len(extra):2 * n + len(extra) + 2]
        token = refs[-1]
        for send, _ in _split_copies(ins, land_refs, send_sems, recv_sems, kind):
            send.start()
        token[...] = jnp.zeros_like(token)

    arrays = list(srcs) + list(lands)
    outs = pl.pallas_call(
        body, name=name,
        in_specs=[HBM_SPEC] * (2 * n) + [ANY_SPEC] * len(extra),
        out_shape=([pltpu.SemaphoreType.DMA((n_sems,))] * 2 + [pltpu.HBM(a.shape, a.dtype) for a in arrays]
                   + [jax.ShapeDtypeStruct((8, 128), F32)]),
        out_specs=[SEM_SPEC] * 2 + [HBM_SPEC] * (2 * n) + [pl.BlockSpec(memory_space=pltpu.VMEM)],
        input_output_aliases={i: 2 + i for i in range(2 * n)},
        compiler_params=SPLIT_PARAMS,
    )(*[pltpu.with_memory_space_constraint(a, pltpu.HBM) for a in arrays], *extra)
    return outs[:2], outs[2:2 + 2 * n], outs[-1]


def _exchange_wait(name, sems, passed, kind, after):
    n = len(passed) // 2

    def body(*refs):
        ins, land_refs = refs[:n], refs[n:2 * n]
        send_sems, recv_sems = refs[2 * n:2 * n + 2]
        for send, arrive in _split_copies(ins, land_refs, send_sems, recv_sems, kind):
            send.wait_send()
            arrive.wait_recv()

    outs = pl.pallas_call(
        body, name=name,
        in_specs=[HBM_SPEC] * (2 * n) + [SEM_SPEC] * 2 + [ANY_SPEC],
        out_shape=[pltpu.HBM(a.shape, a.dtype) for a in passed],
        out_specs=[HBM_SPEC] * (2 * n),
        input_output_aliases={i: i for i in range(2 * n)},
        compiler_params=SPLIT_PARAMS,
    )(*passed, *sems, after)
    return outs[:n], outs[n:]


def _own_slot(name, own, me):
    r, cc = own.shape
    th = min(r, 512)

    def body(me_ref, x_ref, o_ref):
        del me_ref
        o_ref[...] = x_ref[...]

    grid_spec = pltpu.PrefetchScalarGridSpec(
        num_scalar_prefetch=1, grid=(r // th,),
        in_specs=[pl.BlockSpec((th, cc), lambda i, me_ref: (i, 0))],
        out_specs=pl.BlockSpec((None, th, cc), lambda i, me_ref: (me_ref[0], i, 0)))
    return pl.pallas_call(
        body, name=name, grid_spec=grid_spec,
        out_shape=jax.ShapeDtypeStruct((N_CHIPS, r, cc), own.dtype), compiler_params=_params(1),
    )(me, own)


def _sum_devices(name, landed, own, place):
    n_dev, h, cc = landed.shape
    th = min(h, 256)
    nb = h // th

    def body(place_ref, l_ref, own_ref, o_ref):
        total = None
        for d in range(n_dev):
            piece = jnp.where(place_ref[0] == d, own_ref[...], l_ref[d]).astype(F32)
            total = piece if total is None else total + piece
        o_ref[...] = total

    grid_spec = pltpu.PrefetchScalarGridSpec(
        num_scalar_prefetch=1, grid=(nb,),
        in_specs=[pl.BlockSpec((n_dev, th, cc), lambda i, p: (0, i, 0)),
                  pl.BlockSpec((None, th, cc), lambda i, p: (p[1], p[2] * nb + i, 0))],
        out_specs=pl.BlockSpec((th, cc), lambda i, p: (i, 0)))
    return pl.pallas_call(
        body, name=name, grid_spec=grid_spec,
        out_shape=jax.ShapeDtypeStruct((h, cc), F32), compiler_params=_params(1),
    )(place, landed, own)


def _share_halves(name, halves):
    flat = [t for per_weight in halves for t in per_weight]
    n = len(flat)
    n_w = len(halves)

    def body(*refs):
        ins, outs = refs[:n], refs[n:n + n_w]
        send_sems, recv_sems, local_sems = refs[n + n_w:n + n_w + 3]
        stage = refs[n + n_w + 3:]
        x, y, c = _my_place()
        sends, own = [], []
        for i in range(n):
            w, l = divmod(i, DEPTH)
            h = ins[i].shape[0]
            own.append(outs[w].at[l, pl.ds(c * h, h), :])
            sends.append(_remote(ins[i], own[i], send_sems.at[i], recv_sems.at[i], (x, y, 1 - c)))
        for cp in sends:
            cp.start()
        local = _staged_copies(ins, own, stage, local_sems)
        for i in range(n):
            w, l = divmod(i, DEPTH)
            h = ins[i].shape[0]
            _remote(ins[i], outs[w].at[l, pl.ds((1 - c) * h, h), :], send_sems.at[i], recv_sems.at[i],
                    (x, y, 1 - c)).wait_recv()
        for cp in sends:
            cp.wait_send()
        for cp in local:
            cp.wait()

    return pl.pallas_call(
        body, name=name, in_specs=[ANY_SPEC] * n, out_specs=[ANY_SPEC] * n_w,
        out_shape=[jax.ShapeDtypeStruct((DEPTH, 2 * per_weight[0].shape[0], per_weight[0].shape[1]), F32)
                   for per_weight in halves],
        scratch_shapes=([pltpu.SemaphoreType.DMA((n,))] * 3 + [pltpu.VMEM(t.shape, t.dtype) for t in flat]),
        compiler_params=pltpu.CompilerParams(vmem_limit_bytes=VMEM_LIMIT),
    )(*flat)


def _all_reduce_small(pack, after):
    def body(p_ref, after_ref, o_ref, recv, send_sems, recv_sems):
        del after_ref
        x, y, c = _my_place()
        me = 4 * x + 2 * y + c
        recv[me] = p_ref[...]
        peers = []
        for k in range(1, N_DEV):
            px, py, pc = (x + (k >> 2)) % 2, (y + ((k >> 1) & 1)) % 2, (c + (k & 1)) % 2
            peers.append((px, py, pc))
        sends = [_remote(p_ref, recv.at[me], send_sems.at[k], recv_sems.at[k], peer)
                 for k, peer in enumerate(peers)]
        for cp in sends:
            cp.start()
        for k, (px, py, pc) in enumerate(peers):
            _remote(p_ref, recv.at[4 * px + 2 * py + pc], send_sems.at[k], recv_sems.at[k],
                    (px, py, pc)).wait_recv()
        for cp in sends:
            cp.wait_send()
        total = recv[0]
        for d in range(1, N_DEV):
            total = total + recv[d]
        o_ref[...] = total

    vmem = pl.BlockSpec(memory_space=pltpu.VMEM)
    return pl.pallas_call(
        body, name="all_reduce_small", in_specs=[vmem, ANY_SPEC], out_specs=vmem,
        out_shape=jax.ShapeDtypeStruct(pack.shape, F32),
        scratch_shapes=[pltpu.VMEM((N_DEV,) + pack.shape, F32),
                        pltpu.SemaphoreType.DMA((N_DEV - 1,)), pltpu.SemaphoreType.DMA((N_DEV - 1,))],
    )(pack, after)


def _adamw(name, w, g, m, v):
    r, cc = w.shape
    th = min(r, 256)

    def body(w_ref, g_ref, m_ref, v_ref, d_ref, m_out, v_out):
        gv = g_ref[...]
        m2 = ADAM_B1 * m_ref[...] + (1.0 - ADAM_B1) * gv
        v2 = ADAM_B2 * v_ref[...] + (1.0 - ADAM_B2) * (gv * gv)
        m_hat = m2 / (1.0 - ADAM_B1 ** ADAM_STEP)
        v_hat = v2 / (1.0 - ADAM_B2 ** ADAM_STEP)
        d_ref[...] = -ADAM_LR * (m_hat / (jnp.sqrt(v_hat) + ADAM_EPS) + ADAM_WD * w_ref[...])
        m_out[...] = m2
        v_out[...] = v2

    tile = _spec((th, cc), lambda i: (i, 0))
    return pl.pallas_call(
        body, name=name, grid=(r // th,), in_specs=[tile] * 4, out_specs=[tile] * 3,
        out_shape=[jax.ShapeDtypeStruct((r, cc), F32)] * 3, compiler_params=_params(1),
    )(w, g, m, v)


def _lower_bounds(lb_logits):
    p = jax.nn.softmax(lb_logits.astype(F32), axis=0)
    return jnp.cumsum(p, axis=0) - p[0]


def _layer_forward(l, stream, small, weights, consts, next_gain=None, loss=None, after=None):
    win, rest = weights
    cos_t, sin_t, stack, _, _ = consts
    tm = MM_TILE
    x_in, h, h_t = stream
    saved = {"x_in": x_in}

    proj = _mm_pieces(f"proj{l}", h, win, False, tm, after=after)
    saved.update(h_t=h_t, proj=proj)

    qkv = _attn_prep(f"attn_prep{l}", proj, cos_t, sin_t)
    outs, lses = [], []
    for p, d in enumerate(DILATIONS):
        o, lse = _attn_fwd(f"attn_fwd{l}_{d}", *qkv[p], SEQ // d // SPAN)
        outs.append(o)
        lses.append(lse)
    mixed, mixed_t, attn, lse = _attn_merge(f"attn_merge{l}", outs, lses, small["attn_out_gain"][l][None, :])
    saved.update(qkv=qkv, attn=attn, lse=lse)

    lb3 = small["lower"][l].reshape(HGRN_HEADS, 1, HGRN_DIM)
    mixed, mixed_t, o_pre, states, scores = _hgrn_fwd(f"hgrn_fwd{l}", proj, lb3, small["hgrn_out_gain"][l][None, :],
                                                      stack, mixed, mixed_t)
    wo, wu, wd = rest(mixed)
    saved.update(mixed_t=mixed_t, o_pre=o_pre, states=states, scores=scores, lb3=lb3, weights=(win, wo, wu, wd))

    x_mid, h2, h2_t = _mm_accum(f"out_proj{l}", mixed, wo, False, tm, x_in, next_gain=small["norm_mlp"][l][None, :])
    saved["x_mid"] = x_mid

    a, relu_u, a_t = _mm_pieces(f"up{l}", h2, wu, False, tm, epilogue="relu2")
    new_stream = tuple(_mm_accum(f"down{l}", a, wd, False, tm, x_mid, next_gain=next_gain, loss=loss))
    saved.update(h2_t=h2_t, relu_u=relu_u, a_t=a_t)
    return new_stream, saved


def _layer_backward(l, dx, saved, small, consts, on_grads, after=None):
    win, wo, wu, wd = saved["weights"]
    cos_t, sin_t, stack, stack_t, head_sum = consts
    tm = MM_TILE

    dx, dx_b = dx
    du = _mm_pieces(f"d_u{l}", dx_b, wd, True, tm, epilogue="relu2_grad", extra=saved["relu_u"], after=after)
    d_wd = _mm_dw(f"d_wdown{l}", saved["a_t"], dx_b, False, tm)
    dxm, dxm_b, dg_mlp = _mm_accum(f"d_h2_{l}", du, wu, True, tm, dx,
                                   norm=(saved["x_mid"], small["norm_mlp"][l][None, :]))
    d_wu = _mm_dw(f"d_wup{l}", saved["h2_t"], du, True, tm)
    d_wo = _mm_dw(f"d_wout{l}", saved["mixed_t"], dxm_b, False, tm)
    after_early = on_grads(l, "early", (d_wu, d_wd, d_wo))

    d_mixed = _mm_pieces(f"d_mixed{l}", dxm_b, wo, True, tm, after=after_early)
    d_rec = d_mixed

    d_out, delta, lses, dg_attn = _attn_bwd_prep(f"attn_bwd_prep{l}", d_mixed, saved["attn"], saved["lse"],
                                                 small["attn_out_gain"][l][None, :], head_sum)
    grads = []
    for p, d in enumerate(DILATIONS):
        grads.append(_attn_bwd(f"attn_bwd{l}_{d}", *saved["qkv"][p], d_out[p], delta[p], lses[p],
                               SEQ // d // SPAN))
    dp_attn = _attn_bwd_post(f"attn_bwd_post{l}", grads, cos_t, sin_t)

    dq_h, df_h, di_h, dg_h, d_lower, dg_hgrn = _hgrn_bwd(
        f"hgrn_bwd{l}", saved["proj"], d_rec, saved["o_pre"], saved["states"], saved["scores"],
        saved["lb3"], small["hgrn_out_gain"][l][None, :], stack, stack_t)
    dproj = [dp_attn, dq_h, df_h, di_h, dg_h]

    d_win = _mm_dw(f"d_win{l}", saved["h_t"], dproj, True, tm)
    after_last = on_grads(l, "last", (d_win,))
    dx_in, dx_in_b, dg_mix = _mm_accum(f"d_h{l}", dproj, win, True, tm, dxm,
                                       norm=(saved["x_in"], small["norm_mix"][l][None, :]), after=after_last)

    small_grads = {"norm_mix": dg_mix[0], "attn_out_gain": dg_attn[0],
                   "lower": d_lower.reshape(HGRN_WIDTH),
                   "hgrn_out_gain": jnp.sum(dg_hgrn, axis=0).reshape(HGRN_DIM), "norm_mlp": dg_mlp[0]}
    return (dx_in, dx_in_b), after_last, small_grads


def _local_step(xs, target, small, get_weights, on_grads):
    consts = _rope_tables() + _hgrn_consts() + (_head_sum_matrix(),)
    stream = (xs,) + tuple(_rms_fwd("norm_mix0", xs, small["norm_mix"][0][None, :]))
    saved = []
    for l in range(DEPTH):
        w, after = get_weights(l, stream[0])
        if l + 1 < DEPTH:
            stream, s = _layer_forward(l, stream, small, w, consts, next_gain=small["norm_mix"][l + 1][None, :],
                                       after=after)
        else:
            stream, s = _layer_forward(l, stream, small, w, consts, loss=(small["norm_final"][None, :], target),
                                       after=after)
        saved.append(s)
    dx_f, dx_b, dg_final, loss = stream
    dx = (dx_f, dx_b)
    small_grads = [None] * DEPTH
    after = None
    for l in reversed(range(DEPTH)):
        dx, after, small_grads[l] = _layer_backward(l, dx, saved[l], small, consts, on_grads, after=after)
    return loss, dx[0], dg_final[0], small_grads


def _pack_small(norm_mix, attn_out_gain, lb, hgrn_out_gain, norm_mlp, norm_final, last_row):
    rows = [norm_mix, attn_out_gain.reshape(1, D_MODEL), lb.reshape(1, D_MODEL),
            jnp.pad(hgrn_out_gain.reshape(1, DEPTH * HGRN_DIM), ((0, 0), (0, D_MODEL - DEPTH * HGRN_DIM))),
            norm_mlp, norm_final.reshape(1, D_MODEL), last_row.reshape(1, D_MODEL)]
    pack = jnp.concatenate(rows, axis=0)
    return jnp.pad(pack, ((0, PACK_ROWS - pack.shape[0]), (0, 0)))


def _unpack_small(pack):
    return (pack[0:2], pack[2].reshape(DEPTH, ATTN_WIDTH), pack[3].reshape(DEPTH, HGRN_WIDTH),
            pack[4, :DEPTH * HGRN_DIM].reshape(DEPTH, HGRN_DIM), pack[5:7], pack[7], pack[8])


def kernel(x, norm_mix, w_in, attn_out_gain, hgrn_lb_logits, hgrn_out_gain, w_out, norm_mlp, w_up, w_down, norm_final, loss_target, m_norm_mix, m_w_in, m_attn_out_gain, m_hgrn_lb_logits, m_hgrn_out_gain, m_w_out, m_norm_mlp, m_w_up, m_w_down, m_norm_final, v_norm_mix, v_w_in, v_attn_out_gain, v_hgrn_lb_logits, v_hgrn_out_gain, v_w_out, v_norm_mlp, v_w_up, v_w_down, v_norm_final):
    lower, lower_vjp = jax.vjp(_lower_bounds, hgrn_lb_logits)
    small = {"norm_mix": norm_mix, "attn_out_gain": attn_out_gain, "lower": lower,
             "hgrn_out_gain": hgrn_out_gain, "norm_mlp": norm_mlp, "norm_final": norm_final}
    big_w = (w_in, w_out, w_up, w_down)

    x_pos, y_pos, core = lax.axis_index("x"), lax.axis_index("y"), lax.axis_index("c")
    me = (2 * x_pos + y_pos).astype(jnp.int32).reshape(1)
    place = jnp.stack([4 * x_pos + 2 * y_pos + core, 2 * x_pos + y_pos, core]).astype(jnp.int32)
    shards = [[w[l].astype(BF16) for w in big_w] for l in range(DEPTH)]
    in_flight = {}

    def start_gather(name, some, after):
        lands = [_own_slot(f"own_{name}_{i}", s, me) for i, s in enumerate(some)]
        sems, passed, token = _exchange_start(f"start_{name}", some, lands, "gather", after)
        in_flight[name] = (sems, passed)
        return token

    def finish_gather(name, after):
        return _exchange_wait(f"wait_{name}", *in_flight.pop(name), "gather", after)[1]

    def get_weights(l, stream):
        if l == 0:
            (win,) = _gather_weights("gather_w_in0", shards[0][:1])
            token = start_gather("gather_rest0", shards[0][1:], win)
            token = start_gather("gather_w_in1", shards[1][:1], token)
            token = start_gather("gather_rest1", shards[1][1:], token)
            return (win, lambda after: finish_gather("gather_rest0", after)), token
        (win,) = finish_gather("gather_w_in1", stream)
        return (win, lambda after: finish_gather("gather_rest1", after)), None

    reduced = {}

    def start_exchange(name, grads):
        srcs, lands = [g for g, _ in grads], [land for _, land in grads]
        sems, passed, token = _exchange_start(f"start_{name}", srcs, lands, "scatter")
        in_flight[name] = (sems, passed)
        return token

    def finish_exchange(name, after):
        own, landed = _exchange_wait(f"wait_{name}", *in_flight.pop(name), "scatter", after)
        return [_sum_devices(f"sum_{name}_{i}", p, g, place) for i, (p, g) in enumerate(zip(landed, own))]

    def on_grads(l, group, grads):
        token = start_exchange(f"{group}{l}", grads)
        if (l, group) == (0, "early"):
            reduced[(1, "early")] = finish_exchange("early1", token)
            reduced[(1, "last")] = finish_exchange("last1", token)
        if (l, group) == (0, "last"):
            reduced[(0, "early")] = finish_exchange("early0", token)
        return token

    loss, dx, dg_final, sg = _local_step(x[0], loss_target[0], small, get_weights, on_grads)

    big_m = (m_w_in, m_w_out, m_w_up, m_w_down)
    big_v = (v_w_in, v_w_out, v_w_up, v_w_down)
    names = ("w_in", "w_out", "w_up", "w_down")
    big_g, big_delta, big_new_m, big_new_v = [None] * 4, [None] * 4, [None] * 4, [None] * 4

    def finish_weights(group, which):
        whole = _share_halves(f"share_{group}", [[reduced[(l, group)][i] for l in range(DEPTH)]
                                                 for i in range(len(which))])
        for i, w in enumerate(which):
            shape = big_w[w].shape
            flat = lambda arr: arr.reshape(shape[0] * shape[1], shape[2])
            d, m2, v2 = _adamw(f"adamw_{names[w]}", flat(big_w[w]), flat(whole[i]), flat(big_m[w]), flat(big_v[w]))
            big_g[w], big_delta[w] = whole[i], d.reshape(shape)
            big_new_m[w], big_new_v[w] = m2.reshape(shape), v2.reshape(shape)

    finish_weights("early", (2, 3, 1))
    reduced[(0, "last")] = finish_exchange("last0", big_delta[3])
    finish_weights("last", (0,))

    stack2 = lambda key: jnp.stack([sg[l][key] for l in range(DEPTH)])
    pack = _pack_small(stack2("norm_mix"), stack2("attn_out_gain"), stack2("lower"), stack2("hgrn_out_gain"),
                       stack2("norm_mlp"), dg_final, jnp.broadcast_to(loss[0, 0], (D_MODEL,)))
    g_mix, g_attn, g_lower, g_hgrn, g_mlp, g_final, loss_row = _unpack_small(_all_reduce_small(pack, big_delta[0]))
    (g_logits,) = lower_vjp(g_lower)

    zeros_row = jnp.zeros((D_MODEL,), F32)
    small_w = (norm_mix, attn_out_gain, hgrn_lb_logits, hgrn_out_gain, norm_mlp, norm_final)
    small_m = (m_norm_mix, m_attn_out_gain, m_hgrn_lb_logits, m_hgrn_out_gain, m_norm_mlp, m_norm_final)
    small_v = (v_norm_mix, v_attn_out_gain, v_hgrn_lb_logits, v_hgrn_out_gain, v_norm_mlp, v_norm_final)
    small_g = (g_mix, g_attn, g_logits, g_hgrn, g_mlp, g_final)
    packs = [_pack_small(*t, zeros_row) for t in (small_w, small_g, small_m, small_v)]
    small_delta, small_new_m, small_new_v = [_unpack_small(p)[:6] for p in _adamw("adamw_small", *packs)]

    def ordered(small6, big4):
        mix, attn, lbl, hg, mlp, fin = small6
        return (mix, big4[0], attn, lbl, hg, big4[1], mlp, big4[2], big4[3], fin)

    return ((loss_row[0], dx[None]) + ordered(small_g, big_g) + ordered(small_delta, big_delta)
            + ordered(small_new_m, big_new_m) + ordered(small_new_v, big_new_v))
```

```python
import numpy as np
import jax
import jax.numpy as jnp
from jax import lax
from jax.experimental import pallas as pl
from jax.experimental.pallas import tpu as pltpu

F32 = jnp.float32
BF16 = jnp.bfloat16
MESH = pl.DeviceIdType.MESH

SEQ = 4096
D_MODEL = 1024
DEPTH = 2
ATTN_WIDTH = 512
HEAD_DIM = 64
HGRN_HEADS = 4
HGRN_DIM = 128
HGRN_WIDTH = 512
IN_W = 3584
MLP_HIDDEN = 4096
N_CHIPS = 4
N_DEV = 8
DILATIONS = (1, 4, 16)
SPAN = 128
ROPE_THETA = 10000.0
NORM_EPS = 1e-6
MASK_VALUE = -1e30
CHUNK = 128
ROW_TILE = 512
MM_TILE = 512
VMEM_LIMIT = 52 * 1024 * 1024

ADAM_LR = 0.001
ADAM_B1 = 0.9
ADAM_B2 = 0.999
ADAM_EPS = 1e-08
ADAM_WD = 0.01
ADAM_STEP = 10

PACK_ROWS = 16


def _params(n_axes):
    return pltpu.CompilerParams(dimension_semantics=("arbitrary",) * n_axes,
                                vmem_limit_bytes=VMEM_LIMIT)


def _dot(a, b):
    return jnp.dot(a.astype(BF16), b.astype(BF16), preferred_element_type=F32)


def _dot_nt(a, b):
    return lax.dot_general(a.astype(BF16), b.astype(BF16), (((1,), (1,)), ((), ())),
                           preferred_element_type=F32)


def _dot_tn(a, b):
    return lax.dot_general(a.astype(BF16), b.astype(BF16), (((0,), (0,)), ((), ())),
                           preferred_element_type=F32)


def _sigmoid(x):
    return 1.0 / (1.0 + jnp.exp(-x))


def _spec(shape, index_map):
    return pl.BlockSpec(shape, index_map)


def _mm_pieces(name, a, w, nt, tm, epilogue="none", extra=None, after=None):
    s = a.shape[0]
    pw = w.shape[1] if nt else w.shape[2]
    width = N_CHIPS * pw

    def body(a_ref, w_ref, *rest):
        e_ref = rest[0] if extra is not None else None
        outs = rest[-3:] if epilogue == "relu2" else rest[-1:]
        av = a_ref[...].astype(BF16)
        for j in range(N_CHIPS):
            cols = slice(j * pw, (j + 1) * pw)
            r = _dot_nt(av, w_ref[j]) if nt else _dot(av, w_ref[j])
            if epilogue == "relu2":
                relu = jnp.maximum(r, 0.0)
                r = relu * relu
                outs[1][:, cols] = relu.astype(BF16)
                outs[2][cols, :] = r.T.astype(BF16)
            elif epilogue == "relu2_grad":
                r = r * (2.0 * e_ref[:, cols].astype(F32))
            outs[0][:, cols] = r.astype(outs[0].dtype)

    row = lambda width_: _spec((tm, width_), lambda i: (i, 0))
    in_specs = [row(a.shape[1]), _spec(w.shape, lambda i: (0, 0, 0))]
    args = [a, w]
    if extra is not None:
        in_specs.append(row(width))
        args.append(extra)
    if after is not None:
        in_specs.append(pl.BlockSpec(memory_space=pl.ANY))
        args.append(after)
    if epilogue == "relu2":
        out_specs = [row(width), row(width), _spec((width, tm), lambda i: (0, i))]
        out_shape = [jax.ShapeDtypeStruct((s, width), BF16)] * 2 + [jax.ShapeDtypeStruct((width, s), BF16)]
    else:
        out_specs = row(width)
        out_shape = jax.ShapeDtypeStruct((s, width), BF16 if epilogue == "relu2_grad" else F32)
    return pl.pallas_call(body, name=name, grid=(s // tm,), in_specs=in_specs, out_specs=out_specs,
                          out_shape=out_shape, compiler_params=_params(1))(*args)


def _mm_accum(name, a, w, nt, tm, resid, norm=None, after=None, next_gain=None, loss=None):
    pieces = list(a) if isinstance(a, (list, tuple)) else [a]
    n_a = len(pieces)
    s = pieces[0].shape[0]
    pk = w.shape[2] if nt else w.shape[1]
    d = w.shape[1] if nt else w.shape[2]

    def body(*refs):
        a_refs, w_ref, resid_ref, rest = refs[:n_a], refs[n_a], refs[n_a + 1], refs[n_a + 2:]
        av = a_refs[0][...] if n_a == 1 else jnp.concatenate([ref[...] for ref in a_refs], axis=1)
        r = None
        for j in range(N_CHIPS):
            piece = av[:, j * pk:(j + 1) * pk].astype(BF16)
            term = _dot_nt(piece, w_ref[j]) if nt else _dot(piece, w_ref[j])
            r = term if r is None else r + term
        if loss is not None:
            g_ref, t_ref = rest[:2]
            dx_ref, dxb_ref, dg_ref, loss_ref, acc = rest[-5:]
            i = pl.program_id(0)

            @pl.when(i == 0)
            def _():
                dg_ref[...] = jnp.zeros_like(dg_ref)
                acc[...] = jnp.zeros_like(acc)

            xv = r + resid_ref[...]
            g = g_ref[...]
            rs = lax.rsqrt(jnp.mean(xv * xv, axis=1, keepdims=True) + NORM_EPS)
            xhat = xv * rs
            err = xhat * g - t_ref[...]
            acc[...] += jnp.sum(err * err, axis=0, keepdims=True)
            dy = err * (1.0 / d)
            dyg = dy * g
            dx = rs * (dyg - xhat * jnp.mean(dyg * xhat, axis=1, keepdims=True))
            dx_ref[...] = dx
            dxb_ref[...] = dx.astype(BF16)
            dg_ref[...] += jnp.sum(dy * xhat, axis=0, keepdims=True)

            @pl.when(i == s // tm - 1)
            def _():
                total = jnp.sum(acc[...], axis=1, keepdims=True) * (0.5 / d)
                loss_ref[...] = jnp.broadcast_to(total, loss_ref.shape)

            return
        if norm is None and next_gain is None:
            rest[-1][...] = r + resid_ref[...]
            return
        if norm is None:
            g_ref = rest[0]
            x_out, h_out, ht_out = rest[-3:]
            xv = r + resid_ref[...]
            x_out[...] = xv
            h = xv * lax.rsqrt(jnp.mean(xv * xv, axis=1, keepdims=True) + NORM_EPS) * g_ref[...]
            h_out[...] = h.astype(BF16)
            ht_out[...] = h.T.astype(BF16)
            return
        x_ref, g_ref = rest[:2]
        dx_ref, dxb_ref, dg_ref = rest[-3:]

        @pl.when(pl.program_id(0) == 0)
        def _():
            dg_ref[...] = jnp.zeros_like(dg_ref)

        xv = x_ref[...]
        rs = lax.rsqrt(jnp.mean(xv * xv, axis=1, keepdims=True) + NORM_EPS)
        xhat = xv * rs
        rg = r * g_ref[...]
        dx = resid_ref[...] + rs * (rg - xhat * jnp.mean(rg * xhat, axis=1, keepdims=True))
        dx_ref[...] = dx
        dxb_ref[...] = dx.astype(BF16)
        dg_ref[...] += jnp.sum(r * xhat, axis=0, keepdims=True)

    row = lambda width: _spec((tm, width), lambda i: (i, 0))
    in_specs = [row(p.shape[1]) for p in pieces] + [_spec(w.shape, lambda i: (0, 0, 0)), row(d)]
    args = pieces + [w, resid]
    scratch = []
    if loss is not None:
        in_specs += [_spec((1, d), lambda i: (0, 0)), row(d)]
        args += list(loss)
        out_specs = [row(d), row(d), _spec((1, d), lambda i: (0, 0)), _spec((1, 128), lambda i: (0, 0))]
        out_shape = [jax.ShapeDtypeStruct((s, d), F32), jax.ShapeDtypeStruct((s, d), BF16),
                     jax.ShapeDtypeStruct((1, d), F32), jax.ShapeDtypeStruct((1, 128), F32)]
        scratch = [pltpu.VMEM((1, d), F32)]
    elif norm is None and next_gain is None:
        out_specs, out_shape = row(d), jax.ShapeDtypeStruct((s, d), F32)
    elif norm is None:
        in_specs.append(_spec((1, d), lambda i: (0, 0)))
        args.append(next_gain)
        out_specs = [row(d), row(d), _spec((d, tm), lambda i: (0, i))]
        out_shape = [jax.ShapeDtypeStruct((s, d), F32), jax.ShapeDtypeStruct((s, d), BF16),
                     jax.ShapeDtypeStruct((d, s), BF16)]
    else:
        in_specs += [row(d), _spec((1, d), lambda i: (0, 0))]
        args += list(norm)
        out_specs = [row(d), row(d), _spec((1, d), lambda i: (0, 0))]
        out_shape = [jax.ShapeDtypeStruct((s, d), F32), jax.ShapeDtypeStruct((s, d), BF16),
                     jax.ShapeDtypeStruct((1, d), F32)]
    if after is not None:
        in_specs.append(pl.BlockSpec(memory_space=pl.ANY))
        args.append(after)
    return pl.pallas_call(body, name=name, grid=(s // tm,), in_specs=in_specs, out_specs=out_specs,
                          out_shape=out_shape, scratch_shapes=scratch, compiler_params=_params(1))(*args)


def _mm_dw(name, a_t, b, by_cols, tk):
    pieces = list(b) if isinstance(b, (list, tuple)) else [b]
    n_b = len(pieces)
    m, s = a_t.shape
    n = sum(p.shape[1] for p in pieces)
    shape = (N_CHIPS, m, n // N_CHIPS) if by_cols else (N_CHIPS, m // N_CHIPS, n)
    n_steps = s // tk

    def body(a_ref, *rest):
        b_refs, o_ref, acc = rest[:n_b], rest[n_b], rest[-1]

        @pl.when(pl.program_id(0) == 0)
        def _():
            acc[...] = jnp.zeros_like(acc)

        bv = b_refs[0][...] if n_b == 1 else jnp.concatenate([ref[...] for ref in b_refs], axis=1)
        for j in range(N_CHIPS):
            if by_cols:
                acc[j] += _dot(a_ref[...], bv[:, j * shape[2]:(j + 1) * shape[2]])
            else:
                acc[j] += _dot(a_ref[j * shape[1]:(j + 1) * shape[1], :], bv)

        @pl.when(pl.program_id(0) == n_steps - 1)
        def _():
            o_ref[...] = acc[...].astype(BF16)

    return pl.pallas_call(
        body, name=name, grid=(n_steps,),
        in_specs=[_spec((m, tk), lambda k: (0, k))] + [_spec((tk, p.shape[1]), lambda k: (k, 0)) for p in pieces],
        out_specs=[_spec(shape, lambda k: (0, 0, 0)), ANY_SPEC],
        out_shape=[jax.ShapeDtypeStruct(shape, BF16),
                   jax.ShapeDtypeStruct((N_DEV, shape[1] // 2, shape[2]), BF16)],
        scratch_shapes=[pltpu.VMEM(shape, F32)],
        compiler_params=_params(1))(a_t, *pieces)


def _rms_fwd(name, x, gain):
    s, d = x.shape
    t = ROW_TILE

    def body(x_ref, g_ref, h_ref, ht_ref):
        xv = x_ref[...]
        r = lax.rsqrt(jnp.mean(xv * xv, axis=1, keepdims=True) + NORM_EPS)
        h = xv * r * g_ref[...]
        h_ref[...] = h.astype(BF16)
        ht_ref[...] = h.T.astype(BF16)

    return pl.pallas_call(
        body, name=name, grid=(s // t,),
        in_specs=[_spec((t, d), lambda i: (i, 0)), _spec((1, d), lambda i: (0, 0))],
        out_specs=[_spec((t, d), lambda i: (i, 0)), _spec((d, t), lambda i: (0, i))],
        out_shape=[jax.ShapeDtypeStruct((s, d), BF16), jax.ShapeDtypeStruct((d, s), BF16)],
        compiler_params=_params(1),
    )(x, gain)


def _rope_tables():
    half = HEAD_DIM // 2
    inv_freq = ROPE_THETA ** (-jnp.arange(half, dtype=F32) / half)
    ang = jnp.arange(SEQ, dtype=jnp.int32).astype(F32)[:, None] * inv_freq[None, :]
    cos, sin = jnp.cos(ang), jnp.sin(ang)
    cos_t = jnp.concatenate([cos, cos, cos, cos], axis=1)
    sin_t = jnp.concatenate([-sin, sin, -sin, sin], axis=1)
    return cos_t, sin_t


def _swap_halves(x):
    lane = lax.broadcasted_iota(jnp.int32, x.shape, 1)
    first = (lane % HEAD_DIM) < (HEAD_DIM // 2)
    return jnp.where(first, pltpu.roll(x, 128 - HEAD_DIM // 2, 1), pltpu.roll(x, HEAD_DIM // 2, 1))


def _permuted_specs(t, width):
    specs = [_spec((t, width), lambda i: (i, 0))]
    for d in DILATIONS[1:]:
        specs.append(_spec((d, t // d, width), lambda i: (0, i, 0)))
    return specs


def _permuted_shapes(width, dtype):
    shapes = [jax.ShapeDtypeStruct((SEQ, width), dtype)]
    for d in DILATIONS[1:]:
        shapes.append(jax.ShapeDtypeStruct((d, SEQ // d, width), dtype))
    return shapes


def _attn_prep(name, proj, cos_t, sin_t):
    t = ROW_TILE
    w = ATTN_WIDTH

    def body(q_ref, k_ref, v_ref, cos_ref, sin_ref, *rest):
        outs, scr = rest[:9], rest[9]
        cosv, sinv = cos_ref[...], sin_ref[...]
        for a, (src, roped, scale) in enumerate(((q_ref, True, HEAD_DIM ** -0.5),
                                                 (k_ref, True, 1.0), (v_ref, False, 1.0))):
            o1, o4, o16 = outs[3 * a:3 * a + 3]
            for cb in range(w // 128):
                cols = slice(cb * 128, (cb + 1) * 128)
                val = src[:, cols]
                if roped:
                    val = (val * cosv + _swap_halves(val) * sinv) * scale
                scr[...] = val
                o1[:, cols] = val.astype(BF16)
                for o_ref, d in ((o4, 4), (o16, 16)):
                    for r in range(d):
                        o_ref[r, :, cols] = scr[pl.ds(r, t // d, stride=d), :].astype(BF16)

    out_specs = _permuted_specs(t, w) * 3
    out_shape = _permuted_shapes(w, BF16) * 3
    outs = pl.pallas_call(
        body, name=name, grid=(SEQ // t,),
        in_specs=[_spec((t, w), lambda i: (i, 0)), _spec((t, w), lambda i: (i, 1)),
                  _spec((t, w), lambda i: (i, 2)),
                  _spec((t, 128), lambda i: (i, 0)), _spec((t, 128), lambda i: (i, 0))],
        out_specs=out_specs, out_shape=out_shape,
        scratch_shapes=[pltpu.VMEM((t, 128), F32)],
        compiler_params=_params(1),
    )(proj, proj, proj, cos_t, sin_t)
    q, k, v = outs[0:3], outs[3:6], outs[6:9]
    flat = lambda arr: arr.reshape(SEQ, w)
    return [(flat(q[p]), flat(k[p]), flat(v[p])) for p in range(3)]


def _band_masks():
    row = lax.broadcasted_iota(jnp.int32, (2 * SPAN, 2 * SPAN), 0) % SPAN
    col = lax.broadcasted_iota(jnp.int32, (2 * SPAN, 2 * SPAN), 1)
    is_prev = col < SPAN
    band = (is_prev & (col >= row)) | (~is_prev & (col - SPAN <= row))
    head0 = lax.broadcasted_iota(jnp.int32, (SPAN, 128), 1) < HEAD_DIM
    return band, is_prev, head0


def _stack_heads(x, head0):
    zero = jnp.zeros_like(x)
    return jnp.concatenate([jnp.where(head0, x, zero), jnp.where(head0, zero, x)], axis=0)


ATTN_UNROLL = 32


def _for_each_block(block, seg_blocks):
    def trip(i, carry):
        for u in range(ATTN_UNROLL):
            static = seg_blocks <= ATTN_UNROLL
            block(i * ATTN_UNROLL + u, (u % seg_blocks == 0) if static else None)
        return carry

    lax.fori_loop(0, SEQ // SPAN // ATTN_UNROLL, trip, 0)


def _attn_fwd(name, q, k, v, seg_blocks):
    def body(q_ref, k_ref, v_ref, o_ref, lse_ref):
        band, is_prev, head0 = _band_masks()

        def block(b, first):
            cur = pl.ds(pl.multiple_of(b * SPAN, SPAN), SPAN)
            qs = _stack_heads(q_ref[cur, :], head0)
            if first is True:
                kcat, vcat, ok = k_ref[cur, :], v_ref[cur, :], band[:, SPAN:]
            else:
                prev = pl.ds(pl.multiple_of(jnp.maximum(b - 1, 0) * SPAN, SPAN), SPAN)
                kcat = jnp.concatenate([k_ref[prev, :], k_ref[cur, :]], axis=0)
                vcat = jnp.concatenate([v_ref[prev, :], v_ref[cur, :]], axis=0)
                ok = band if first is False else band & (((b % seg_blocks) != 0) | ~is_prev)
            s = jnp.where(ok, _dot_nt(qs, kcat), MASK_VALUE)
            m = jnp.max(s, axis=1, keepdims=True)
            p = jnp.exp(s - m)
            l = jnp.sum(p, axis=1, keepdims=True)
            pv = _dot(p, vcat) * (1.0 / l)
            lse = m + jnp.log(l)
            o_ref[cur, :] = jnp.where(head0, pv[:SPAN], pv[SPAN:]).astype(BF16)
            lse_ref[cur, :] = jnp.where(head0, lse[:SPAN], lse[SPAN:])

        _for_each_block(block, seg_blocks)

    col = _spec((SEQ, 128), lambda j: (0, j))
    return pl.pallas_call(
        body, name=name, grid=(ATTN_WIDTH // 128,),
        in_specs=[col, col, col], out_specs=[col, col],
        out_shape=[jax.ShapeDtypeStruct((SEQ, ATTN_WIDTH), BF16), jax.ShapeDtypeStruct((SEQ, ATTN_WIDTH), F32)],
        compiler_params=_params(1),
    )(q, k, v)


def _unpermute(dst, src_ref, d, cols):
    n = dst.shape[0] // d
    for r in range(d):
        dst[pl.ds(r, n, stride=d), :] = src_ref[r, :, cols].astype(dst.dtype)


def _attn_merge(name, outs, lses, gain):
    t = ROW_TILE
    w = ATTN_WIDTH

    def body(o1, o4, o16, l1, l4, l16, g_ref, an_ref, ant_ref, attn_ref, lse_ref, so4, so16, sl4, sl16):
        for cb in range(w // 128):
            cols = slice(cb * 128, (cb + 1) * 128)
            _unpermute(so4, o4, 4, cols)
            _unpermute(so16, o16, 16, cols)
            _unpermute(sl4, l4, 4, cols)
            _unpermute(sl16, l16, 16, cols)
            la, lb, lc = l1[:, cols], sl4[...], sl16[...]
            m = jnp.maximum(jnp.maximum(la, lb), lc)
            ea, eb, ec = jnp.exp(la - m), jnp.exp(lb - m), jnp.exp(lc - m)
            tot = ea + eb + ec
            attn_ref[:, cols] = (ea * o1[:, cols].astype(F32) + eb * so4[...] + ec * so16[...]) / tot
            lse_ref[:, cols] = m + jnp.log(tot)
        attn = attn_ref[...]
        r = lax.rsqrt(jnp.mean(attn * attn, axis=1, keepdims=True) + NORM_EPS)
        an = attn * r * g_ref[...]
        an_ref[...] = an.astype(BF16)
        ant_ref[...] = an.T.astype(BF16)

    views = lambda arrs: [arrs[0], arrs[1].reshape(4, SEQ // 4, w), arrs[2].reshape(16, SEQ // 16, w)]
    row = _spec((t, w), lambda i: (i, 0))
    return pl.pallas_call(
        body, name=name, grid=(SEQ // t,),
        in_specs=_permuted_specs(t, w) * 2 + [_spec((1, w), lambda i: (0, 0))],
        out_specs=[row, _spec((w, t), lambda i: (0, i)), row, row],
        out_shape=[jax.ShapeDtypeStruct((SEQ, 2 * w), BF16), jax.ShapeDtypeStruct((2 * w, SEQ), BF16),
                   jax.ShapeDtypeStruct((SEQ, w), F32), jax.ShapeDtypeStruct((SEQ, w), F32)],
        scratch_shapes=[pltpu.VMEM((t, 128), F32)] * 4,
        compiler_params=_params(1),
    )(*views(outs), *views(lses), gain)


def _head_sum_matrix():
    i = np.arange(ATTN_WIDTH)
    return jnp.asarray((i[:, None] // HEAD_DIM) == (i[None, :] // HEAD_DIM), dtype=F32)


def _attn_bwd_prep(name, d_an, attn, lse, gain, head_sum):
    t = ROW_TILE
    w = ATTN_WIDTH

    def body(dan_ref, attn_ref, lse_ref, g_ref, hs_ref, *rest):
        (do1, do4, do16, dl1, dl4, dl16, ls4, ls16, dg_ref), (sdo, sdl, sls) = rest[:9], rest[9:]

        @pl.when(pl.program_id(0) == 0)
        def _():
            dg_ref[...] = jnp.zeros_like(dg_ref)

        attn = attn_ref[...]
        dan = dan_ref[...]
        r = lax.rsqrt(jnp.mean(attn * attn, axis=1, keepdims=True) + NORM_EPS)
        xhat = attn * r
        dg_ref[...] += jnp.sum(dan * xhat, axis=0, keepdims=True)
        dang = dan * g_ref[...]
        d_o = r * (dang - xhat * jnp.mean(dang * xhat, axis=1, keepdims=True))
        delta = jnp.dot(d_o * attn, hs_ref[...], preferred_element_type=F32,
                        precision=lax.Precision.HIGHEST)
        do1[...] = d_o.astype(BF16)
        dl1[...] = delta
        for cb in range(w // 128):
            cols = slice(cb * 128, (cb + 1) * 128)
            sdo[...] = d_o[:, cols]
            sdl[...] = delta[:, cols]
            sls[...] = lse_ref[:, cols]
            for d, o_do, o_dl, o_ls in ((4, do4, dl4, ls4), (16, do16, dl16, ls16)):
                for rr in range(d):
                    rows = pl.ds(rr, t // d, stride=d)
                    o_do[rr, :, cols] = sdo[rows, :].astype(BF16)
                    o_dl[rr, :, cols] = sdl[rows, :]
                    o_ls[rr, :, cols] = sls[rows, :]

    row = _spec((t, w), lambda i: (i, 0))
    perm = _permuted_specs(t, w)
    outs = pl.pallas_call(
        body, name=name, grid=(SEQ // t,),
        in_specs=[row, row, row, _spec((1, w), lambda i: (0, 0)), _spec((w, w), lambda i: (0, 0))],
        out_specs=perm + perm + perm[1:] + [_spec((1, w), lambda i: (0, 0))],
        out_shape=(_permuted_shapes(w, BF16) + _permuted_shapes(w, F32) + _permuted_shapes(w, F32)[1:]
                   + [jax.ShapeDtypeStruct((1, w), F32)]),
        scratch_shapes=[pltpu.VMEM((t, 128), F32)] * 3,
        compiler_params=_params(1),
    )(d_an, attn, lse, gain, head_sum)
    flat = lambda arr: arr.reshape(SEQ, w)
    d_out = [flat(a) for a in outs[0:3]]
    delta = [flat(a) for a in outs[3:6]]
    lses = [lse, flat(outs[6]), flat(outs[7])]
    return d_out, delta, lses, outs[8]


def _attn_bwd(name, q, k, v, d_out, delta, lse, seg_blocks):
    def body(q_ref, k_ref, v_ref, do_ref, dl_ref, lse_ref, dq_ref, dk_out, dv_out, dk_ref, dv_ref):
        band, is_prev, head0 = _band_masks()
        dk_ref[...] = jnp.zeros_like(dk_ref)
        dv_ref[...] = jnp.zeros_like(dv_ref)

        def per_head(x):
            return jnp.concatenate([x[:, 0:1], x[:, HEAD_DIM:HEAD_DIM + 1]], axis=0)

        def block(b, first):
            cur = pl.ds(pl.multiple_of(b * SPAN, SPAN), SPAN)
            qs = _stack_heads(q_ref[cur, :], head0)
            dos = _stack_heads(do_ref[cur, :], head0)
            if first is True:
                kcat, vcat, ok = k_ref[cur, :], v_ref[cur, :], band[:, SPAN:]
            else:
                prev = pl.ds(pl.multiple_of(jnp.maximum(b - 1, 0) * SPAN, SPAN), SPAN)
                kcat = jnp.concatenate([k_ref[prev, :], k_ref[cur, :]], axis=0)
                vcat = jnp.concatenate([v_ref[prev, :], v_ref[cur, :]], axis=0)
                ok = band if first is False else band & (((b % seg_blocks) != 0) | ~is_prev)
            p = jnp.where(ok, jnp.exp(_dot_nt(qs, kcat) - per_head(lse_ref[cur, :])), 0.0)
            ds = p * (_dot_nt(dos, vcat) - per_head(dl_ref[cur, :]))
            dq = _dot(ds, kcat)
            dq_ref[cur, :] = jnp.where(head0, dq[:SPAN], dq[SPAN:]).astype(BF16)
            dk = _dot_tn(ds, qs)
            dv = _dot_tn(p, dos)
            if first is not True:
                dk_ref[prev, :] += dk[:SPAN]
                dv_ref[prev, :] += dv[:SPAN]
            dk_ref[cur, :] += dk[-SPAN:]
            dv_ref[cur, :] += dv[-SPAN:]

        _for_each_block(block, seg_blocks)
        dk_out[...] = dk_ref[...].astype(BF16)
        dv_out[...] = dv_ref[...].astype(BF16)

    col = _spec((SEQ, 128), lambda j: (0, j))
    return pl.pallas_call(
        body, name=name, grid=(ATTN_WIDTH // 128,),
        in_specs=[col] * 6, out_specs=[col] * 3,
        out_shape=[jax.ShapeDtypeStruct((SEQ, ATTN_WIDTH), BF16)] * 3,
        scratch_shapes=[pltpu.VMEM((SEQ, 128), F32)] * 2,
        compiler_params=_params(1),
    )(q, k, v, d_out, delta, lse)


def _attn_bwd_post(name, grads, cos_t, sin_t):
    t = ROW_TILE
    w = ATTN_WIDTH

    def body(*refs):
        ins, cos_ref, sin_ref, out_ref, s4, s16 = refs[:9], refs[9], refs[10], refs[11], refs[12], refs[13]
        cosv, sinv = cos_ref[...], sin_ref[...]
        for a in range(3):
            g1, g4, g16 = ins[a], ins[3 + a], ins[6 + a]
            for cb in range(w // 128):
                cols = slice(cb * 128, (cb + 1) * 128)
                _unpermute(s4, g4, 4, cols)
                _unpermute(s16, g16, 16, cols)
                val = g1[:, cols].astype(F32) + s4[...] + s16[...]
                if a < 2:
                    val = val * cosv + _swap_halves(val * sinv)
                if a == 0:
                    val = val * (HEAD_DIM ** -0.5)
                out_ref[:, a * w + cb * 128:a * w + (cb + 1) * 128] = val.astype(BF16)

    views = []
    for p, d in enumerate(DILATIONS):
        for a in range(3):
            views.append(grads[p][a] if d == 1 else grads[p][a].reshape(d, SEQ // d, w))
    perm = _permuted_specs(t, w)
    in_specs = [perm[0]] * 3 + [perm[1]] * 3 + [perm[2]] * 3
    return pl.pallas_call(
        body, name=name, grid=(SEQ // t,),
        in_specs=in_specs + [_spec((t, 128), lambda i: (i, 0))] * 2,
        out_specs=_spec((t, 3 * w), lambda i: (i, 0)),
        out_shape=jax.ShapeDtypeStruct((SEQ, 3 * w), BF16),
        scratch_shapes=[pltpu.VMEM((t, 128), F32)] * 2,
        compiler_params=_params(1),
    )(*views, cos_t, sin_t)


N_LEVELS = 7
HGRN_PAIR = 4


def _hgrn_consts():
    c = CHUNK
    i = np.arange(c)[:, None]
    s = np.arange(c)[None, :]
    blocks = [s <= i]
    for lv in range(N_LEVELS):
        bs = c >> lv
        h = bs // 2
        m = (i // bs) * bs + h - 1
        second = (i % bs) >= h
        blocks.append((second & (s > m) & (s <= i)) | (~second & (s > i) & (s <= m)))
    blocks.append(s > i)
    stack = np.concatenate(blocks, axis=0).astype(np.float32)
    twice = np.concatenate([stack, stack], axis=1)
    return jnp.asarray(twice, dtype=BF16), jnp.asarray(stack.T, dtype=BF16)


def _split(x):
    hi = x.astype(BF16)
    return hi, (x - hi.astype(F32)).astype(BF16)


def _exact_dot(m01, x):
    n = x.shape[1]
    full = jnp.dot(m01, jnp.concatenate(_split(x), axis=1), preferred_element_type=F32)
    return full[:, :n] + full[:, n:]


def _exact_dot_twice(m01_twice, x):
    return jnp.dot(m01_twice, jnp.concatenate(_split(x), axis=0), preferred_element_type=F32)


def _hgrn_gates(qh, z, lb):
    sq = _sigmoid(qh)
    q = qh * sq * (HGRN_DIM ** -0.5)
    sig = _sigmoid(z)
    sigm = _sigmoid(-z)
    f = lb + (1.0 - lb) * sig
    k = (1.0 - lb) * sigm
    return q, k, f, sq, sig, sigm


def _level_masks(lv):
    row = lax.broadcasted_iota(jnp.int32, (CHUNK, CHUNK), 0)
    col = lax.broadcasted_iota(jnp.int32, (CHUNK, CHUNK), 1)
    shift = N_LEVELS - lv
    half = CHUNK >> (lv + 1)
    second = (row & half) != 0
    second_col = (col & half) != 0
    same = (row >> shift) == (col >> shift)
    return second, same & second & ~second_col, same & (second != second_col)


def _hgrn_fwd(name, proj, lb, gain, stack, mixed, mixed_t):
    t = ROW_TILE
    per = t // CHUNK
    n_rb = SEQ // t
    n_chunks = SEQ // CHUNK
    col0 = 3 * ATTN_WIDTH // 128
    pair_w = HGRN_PAIR * HGRN_DIM

    def body(q_ref, f_ref, i_ref, g_ref, lb_ref, gain_ref, stack_ref, mixed_in, mixed_t_in,
             rec_ref, rect_ref, o_ref, st_out, a_out, st):
        del mixed_in, mixed_t_in

        @pl.when(pl.program_id(1) == 0)
        def _():
            st[...] = jnp.zeros_like(st)

        row = lax.broadcasted_iota(jnp.int32, (CHUNK, CHUNK), 0)
        col = lax.broadcasted_iota(jnp.int32, (CHUNK, CHUNK), 1)
        masks = [_level_masks(lv) for lv in range(N_LEVELS)]
        for c, hh in [(c, hh) for c in range(per) for hh in range(HGRN_PAIR)]:
            rows = slice(c * CHUNK, (c + 1) * CHUNK)
            lanes = slice(hh * HGRN_DIM, (hh + 1) * HGRN_DIM)
            lbv = lb_ref[hh]
            qh, z, v, gh = q_ref[rows, lanes], f_ref[rows, lanes], i_ref[rows, lanes], g_ref[rows, lanes]
            q, k, f, _, _, _ = _hgrn_gates(qh, z, lbv)
            dec = _exact_dot_twice(stack_ref[...], jnp.log(f))
            g = dec[0:CHUNK]
            to_end = dec[(N_LEVELS + 1) * CHUNK:(N_LEVELS + 2) * CHUNK]
            a = jnp.where(row == col, jnp.sum(q * k, axis=1, keepdims=True), 0.0)
            for lv in range(N_LEVELS):
                second, square, _ = masks[lv]
                t = jnp.where(second, q, k) * jnp.exp(dec[(lv + 1) * CHUNK:(lv + 2) * CHUNK])
                a = a + jnp.where(square, _dot_nt(t, t), 0.0)
            st_prev = st[hh]
            st_out[hh, c] = st_prev
            a_out[hh, c] = a
            o = _dot(a, v) + _dot_nt(q * jnp.exp(g), st_prev)
            k_end = k * jnp.exp(to_end)
            st[hh] = st_prev * jnp.exp(g[CHUNK - 1:CHUNK, :]) + _dot(v.T, k_end)
            o_ref[rows, lanes] = o
            r = lax.rsqrt(jnp.mean(o * o, axis=1, keepdims=True) + NORM_EPS)
            rec = o * r * gain_ref[...] * (gh * _sigmoid(gh))
            rec_ref[rows, lanes] = rec.astype(BF16)
            rect_ref[lanes, rows] = rec.T.astype(BF16)

    def col_spec(tt):
        return _spec((t, pair_w), lambda h, rb: (rb, (col0 + HGRN_HEADS * tt) // HGRN_PAIR + h))

    chunk_spec = _spec((HGRN_PAIR, per, CHUNK, CHUNK), lambda h, rb: (h, rb, 0, 0))
    return pl.pallas_call(
        body, name=name, grid=(HGRN_HEADS // HGRN_PAIR, n_rb),
        in_specs=[col_spec(0), col_spec(1), col_spec(2), col_spec(3),
                  _spec((HGRN_PAIR, 1, HGRN_DIM), lambda h, rb: (h, 0, 0)),
                  _spec((1, HGRN_DIM), lambda h, rb: (0, 0)),
                  _spec(stack.shape, lambda h, rb: (0, 0)), ANY_SPEC, ANY_SPEC],
        out_specs=[_spec((t, pair_w), lambda h, rb: (rb, ATTN_WIDTH // pair_w + h)),
                   _spec((pair_w, t), lambda h, rb: (ATTN_WIDTH // pair_w + h, rb)),
                   _spec((t, pair_w), lambda h, rb: (rb, h)),
                   chunk_spec, chunk_spec],
        out_shape=[jax.ShapeDtypeStruct(mixed.shape, BF16),
                   jax.ShapeDtypeStruct(mixed_t.shape, BF16),
                   jax.ShapeDtypeStruct((SEQ, HGRN_WIDTH), F32),
                   jax.ShapeDtypeStruct((HGRN_HEADS, n_chunks, CHUNK, CHUNK), F32),
                   jax.ShapeDtypeStruct((HGRN_HEADS, n_chunks, CHUNK, CHUNK), F32)],
        scratch_shapes=[pltpu.VMEM((HGRN_PAIR, CHUNK, CHUNK), F32)],
        input_output_aliases={7: 0, 8: 1},
        compiler_params=_params(2),
    )(proj, proj, proj, proj, lb, gain, stack, mixed, mixed_t)


def _hgrn_bwd(name, proj, d_rec, o_pre, states, scores, lb, gain, stack, stack_t):
    t = ROW_TILE
    per = t // CHUNK
    n_rb = SEQ // t
    col0 = 3 * ATTN_WIDTH // 128
    pair_w = HGRN_PAIR * HGRN_DIM

    def body(q_ref, f_ref, i_ref, g_ref, drec_ref, o_ref, st_ref, a_ref, lb_ref, gain_ref,
             stack_ref, stack_t_ref, dq_ref, df_ref, di_ref, dg_ref, dlb_ref, dgain_ref, dst):
        @pl.when(pl.program_id(1) == 0)
        def _():
            dst[...] = jnp.zeros_like(dst)
            dlb_ref[...] = jnp.zeros_like(dlb_ref)
            dgain_ref[...] = jnp.zeros_like(dgain_ref)

        gain_v = gain_ref[...]
        row = lax.broadcasted_iota(jnp.int32, (CHUNK, CHUNK), 0)
        col = lax.broadcasted_iota(jnp.int32, (CHUNK, CHUNK), 1)
        masks = [_level_masks(lv) for lv in range(N_LEVELS)]
        for c, hh in [(c, hh) for c in reversed(range(per)) for hh in range(HGRN_PAIR)]:
            rows = slice(c * CHUNK, (c + 1) * CHUNK)
            lanes = slice(hh * HGRN_DIM, (hh + 1) * HGRN_DIM)
            lbv = lb_ref[hh]
            qh, z, v, gh = q_ref[rows, lanes], f_ref[rows, lanes], i_ref[rows, lanes], g_ref[rows, lanes]
            q, k, f, sq, sig, sigm = _hgrn_gates(qh, z, lbv)
            dec = _exact_dot_twice(stack_ref[...], jnp.log(f))
            g = dec[0:CHUNK]
            to_end = dec[(N_LEVELS + 1) * CHUNK:(N_LEVELS + 2) * CHUNK]
            e_g = jnp.exp(g)
            e_end = jnp.exp(to_end)
            e_last = jnp.exp(g[CHUNK - 1:CHUNK, :])
            q_in = q * e_g
            k_end = k * e_end
            st_prev = st_ref[hh, c]
            a = a_ref[hh, c]
            dst_new = dst[hh]

            o = o_ref[rows, lanes]
            drec = drec_ref[rows, lanes]
            sg = _sigmoid(gh)
            r = lax.rsqrt(jnp.mean(o * o, axis=1, keepdims=True) + NORM_EPS)
            ohat = o * r
            d_gh = drec * (ohat * gain_v) * (sg * (1.0 + gh * (1.0 - sg)))
            d_on = drec * (gh * sg)
            dgain_ref[hh] += jnp.sum(d_on * ohat, axis=0, keepdims=True)
            d_ohat = d_on * gain_v
            d_o = r * (d_ohat - ohat * jnp.mean(d_ohat * ohat, axis=1, keepdims=True))

            d_sym = jnp.where(row >= col, _dot_nt(d_o, v), _dot_nt(v, d_o))
            d_v = _dot(a.T, d_o) + _dot_nt(k_end, dst_new)
            d_q_in = _dot(d_o, st_prev)
            d_k_end = _dot(v, dst_new)
            d_q = d_q_in * e_g
            d_k = d_k_end * e_end
            diag = jnp.sum(d_o * v, axis=1, keepdims=True)
            d_q = d_q + diag * k
            d_k = d_k + diag * q
            d_dec = [q_in * d_q_in]
            d_both, d_second = None, None
            for lv in range(N_LEVELS):
                e = jnp.exp(dec[(lv + 1) * CHUNK:(lv + 2) * CHUNK])
                second, _, mirrored = masks[lv]
                t = jnp.where(second, q, k) * e
                d_t = _dot(jnp.where(mirrored, d_sym, 0.0), t)
                d_te = d_t * e
                d_both = d_te if d_both is None else d_both + d_te
                d_second = jnp.where(second, d_te, 0.0) if d_second is None else d_second + jnp.where(second, d_te, 0.0)
                d_dec.append(t * d_t)
            d_q = d_q + d_second
            d_k = d_k + (d_both - d_second)
            d_dec.append(k_end * d_k_end)
            flux = jnp.sum(dst_new * st_prev, axis=0, keepdims=True) * e_last
            d_lf = _exact_dot(stack_t_ref[...], jnp.concatenate(d_dec, axis=0)) + flux
            dst[hh] = dst_new * e_last + _dot(d_o.T, q_in)

            d_f = d_lf / f - d_k
            dlb_ref[hh] += jnp.sum(d_f * sigm, axis=0, keepdims=True)
            dq_ref[rows, lanes] = (d_q * (HGRN_DIM ** -0.5) * (sq * (1.0 + qh * (1.0 - sq)))).astype(BF16)
            df_ref[rows, lanes] = (d_f * (1.0 - lbv) * sig * sigm).astype(BF16)
            di_ref[rows, lanes] = d_v.astype(BF16)
            dg_ref[rows, lanes] = d_gh.astype(BF16)

    last = n_rb - 1

    def col_spec(tt):
        return _spec((t, pair_w), lambda h, rb: (last - rb, (col0 + HGRN_HEADS * tt) // HGRN_PAIR + h))

    head_col = _spec((t, pair_w), lambda h, rb: (last - rb, h))
    rec_col0 = (d_rec.shape[1] - HGRN_WIDTH) // pair_w
    d_rec_col = _spec((t, pair_w), lambda h, rb: (last - rb, rec_col0 + h))
    chunk_spec = _spec((HGRN_PAIR, per, CHUNK, CHUNK), lambda h, rb: (h, last - rb, 0, 0))
    vec_spec = _spec((HGRN_PAIR, 1, HGRN_DIM), lambda h, rb: (h, 0, 0))
    outs = pl.pallas_call(
        body, name=name, grid=(HGRN_HEADS // HGRN_PAIR, n_rb),
        in_specs=[col_spec(0), col_spec(1), col_spec(2), col_spec(3), d_rec_col, head_col,
                  chunk_spec, chunk_spec, vec_spec,
                  _spec((1, HGRN_DIM), lambda h, rb: (0, 0)),
                  _spec(stack.shape, lambda h, rb: (0, 0)), _spec(stack_t.shape, lambda h, rb: (0, 0))],
        out_specs=[head_col] * 4 + [vec_spec, vec_spec],
        out_shape=[jax.ShapeDtypeStruct((SEQ, HGRN_WIDTH), BF16)] * 4
                  + [jax.ShapeDtypeStruct((HGRN_HEADS, 1, HGRN_DIM), F32)] * 2,
        scratch_shapes=[pltpu.VMEM((HGRN_PAIR, CHUNK, CHUNK), F32)],
        compiler_params=_params(2),
    )(proj, proj, proj, proj, d_rec, o_pre, states, scores, lb, gain, stack, stack_t)
    return outs


ANY_SPEC = pl.BlockSpec(memory_space=pl.ANY)


def _my_place():
    return lax.axis_index("x"), lax.axis_index("y"), lax.axis_index("c")


def _other_chips(x, y):
    return [(1 - x, y), (x, 1 - y), (1 - x, 1 - y)]


def _remote(src, dst, send_sem, recv_sem, device):
    return pltpu.make_async_remote_copy(src_ref=src, dst_ref=dst, send_sem=send_sem, recv_sem=recv_sem,
                                        device_id=device, device_id_type=MESH)


def _staged_copies(srcs, dsts, stage, sems):
    loads = [pltpu.make_async_copy(srcs[i], stage[i], sems.at[i]) for i in range(len(srcs))]
    for cp in loads:
        cp.start()
    stores = []
    for i, cp in enumerate(loads):
        cp.wait()
        stores.append(pltpu.make_async_copy(stage[i], dsts[i], sems.at[i]))
        stores[-1].start()
    return stores


def _gather_weights(name, shards):
    n = len(shards)

    def body(*refs):
        ins, outs = refs[:n], refs[n:2 * n]
        ici_send, ici_recv, d2d_send, d2d_recv, local_sems = refs[2 * n:2 * n + 5]
        stage = refs[2 * n + 5:]
        x, y, c = _my_place()
        me = 2 * x + y
        chips = _other_chips(x, y)

        def half(i, which):
            h = ins[i].shape[0] // 2
            return pl.ds(which * h, h)

        sends = []
        for i in range(n):
            for j, (px, py) in enumerate(chips):
                sends.append(_remote(ins[i].at[half(i, c), :], outs[i].at[me, half(i, c), :],
                                     ici_send.at[3 * i + j], ici_recv.at[3 * i + j], (px, py, c)))
        for cp in sends:
            cp.start()
        local = _staged_copies(ins, [outs[i].at[me] for i in range(n)], stage, local_sems)
        for i in range(n):
            for j, (px, py) in enumerate(chips):
                landed = outs[i].at[2 * px + py, half(i, c), :]
                _remote(landed, landed, ici_send.at[3 * i + j], ici_recv.at[3 * i + j], (px, py, c)).wait_recv()
                forward = _remote(landed, landed, d2d_send.at[3 * i + j], d2d_recv.at[3 * i + j], (x, y, 1 - c))
                forward.start()
                sends.append(forward)
        for i in range(n):
            for j, (px, py) in enumerate(chips):
                other = outs[i].at[2 * px + py, half(i, 1 - c), :]
                _remote(other, other, d2d_send.at[3 * i + j], d2d_recv.at[3 * i + j], (x, y, 1 - c)).wait_recv()
        for cp in sends:
            cp.wait_send()
        for cp in local:
            cp.wait()

    return pl.pallas_call(
        body, name=name, in_specs=[ANY_SPEC] * n, out_specs=[ANY_SPEC] * n,
        out_shape=[jax.ShapeDtypeStruct((N_CHIPS,) + s.shape, s.dtype) for s in shards],
        scratch_shapes=([pltpu.SemaphoreType.DMA((3 * n,))] * 4 + [pltpu.SemaphoreType.DMA((n,))]
                        + [pltpu.VMEM(s.shape, s.dtype) for s in shards]),
        compiler_params=pltpu.CompilerParams(vmem_limit_bytes=VMEM_LIMIT),
    )(*shards)


HBM_SPEC = pl.BlockSpec(memory_space=pltpu.HBM)
SEM_SPEC = pl.BlockSpec(memory_space=pltpu.SEMAPHORE)
SPLIT_PARAMS = pltpu.CompilerParams(has_side_effects=pltpu.SideEffectType.DATAFLOW_SIDE_EFFECTING)
N_PEERS = {"gather": N_CHIPS - 1, "scatter": N_DEV - 1}


def _split_copies(ins, lands, send_sems, recv_sems, kind):
    x, y, c = _my_place()
    pairs = []
    for i in range(len(ins)):
        if kind == "gather":
            me = 2 * x + y
            for j, (px, py) in enumerate(_other_chips(x, y)):
                sems = (send_sems.at[3 * i + j], recv_sems.at[3 * i + j], (px, py, c))
                pairs.append((_remote(ins[i], lands[i].at[me], *sems),
                              _remote(ins[i], lands[i].at[2 * px + py], *sems)))
        else:
            me = 4 * x + 2 * y + c
            h = ins[i].shape[1] // 2
            for k in range(1, N_DEV):
                px, py, pc = (x + (k >> 2)) % 2, (y + ((k >> 1) & 1)) % 2, (c + (k & 1)) % 2
                src = ins[i].at[2 * px + py, pl.ds(pc * h, h), :]
                sems = (send_sems.at[7 * i + k - 1], recv_sems.at[7 * i + k - 1], (px, py, pc))
                pairs.append((_remote(src, lands[i].at[me], *sems),
                              _remote(src, lands[i].at[4 * px + 2 * py + pc], *sems)))
    return pairs


def _exchange_start(name, srcs, lands, kind, after=None):
    n = len(srcs)
    n_sems = N_PEERS[kind] * n
    extra = [] if after is None else [after]

    def body(*refs):
        ins, land_refs = refs[:n], refs[n:2 * n]
        send_sems, recv_sems = refs[2 * n + len(extra):2 * n + len(extra) + 2]
        token = refs[-1]
        for send, _ in _split_copies(ins, land_refs, send_sems, recv_sems, kind):
            send.start()
        token[...] = jnp.zeros_like(token)

    arrays = list(srcs) + list(lands)
    outs = pl.pallas_call(
        body, name=name,
        in_specs=[HBM_SPEC] * (2 * n) + [ANY_SPEC] * len(extra),
        out_shape=([pltpu.SemaphoreType.DMA((n_sems,))] * 2 + [pltpu.HBM(a.shape, a.dtype) for a in arrays]
                   + [jax.ShapeDtypeStruct((8, 128), F32)]),
        out_specs=[SEM_SPEC] * 2 + [HBM_SPEC] * (2 * n) + [pl.BlockSpec(memory_space=pltpu.VMEM)],
        input_output_aliases={i: 2 + i for i in range(2 * n)},
        compiler_params=SPLIT_PARAMS,
    )(*[pltpu.with_memory_space_constraint(a, pltpu.HBM) for a in arrays], *extra)
    return outs[:2], outs[2:2 + 2 * n], outs[-1]


def _exchange_wait(name, sems, passed, kind, after):
    n = len(passed) // 2

    def body(*refs):
        ins, land_refs = refs[:n], refs[n:2 * n]
        send_sems, recv_sems = refs[2 * n:2 * n + 2]
        for send, arrive in _split_copies(ins, land_refs, send_sems, recv_sems, kind):
            send.wait_send()
            arrive.wait_recv()

    outs = pl.pallas_call(
        body, name=name,
        in_specs=[HBM_SPEC] * (2 * n) + [SEM_SPEC] * 2 + [ANY_SPEC],
        out_shape=[pltpu.HBM(a.shape, a.dtype) for a in passed],
        out_specs=[HBM_SPEC] * (2 * n),
        input_output_aliases={i: i for i in range(2 * n)},
        compiler_params=SPLIT_PARAMS,
    )(*passed, *sems, after)
    return outs[:n], outs[n:]


def _own_slot(name, own, me):
    r, cc = own.shape
    th = min(r, 512)

    def body(me_ref, x_ref, o_ref):
        del me_ref
        o_ref[...] = x_ref[...]

    grid_spec = pltpu.PrefetchScalarGridSpec(
        num_scalar_prefetch=1, grid=(r // th,),
        in_specs=[pl.BlockSpec((th, cc), lambda i, me_ref: (i, 0))],
        out_specs=pl.BlockSpec((None, th, cc), lambda i, me_ref: (me_ref[0], i, 0)))
    return pl.pallas_call(
        body, name=name, grid_spec=grid_spec,
        out_shape=jax.ShapeDtypeStruct((N_CHIPS, r, cc), own.dtype), compiler_params=_params(1),
    )(me, own)


def _sum_devices(name, landed, own, place):
    n_dev, h, cc = landed.shape
    th = min(h, 256)
    nb = h // th

    def body(place_ref, l_ref, own_ref, o_ref):
        total = None
        for d in range(n_dev):
            piece = jnp.where(place_ref[0] == d, own_ref[...], l_ref[d]).astype(F32)
            total = piece if total is None else total + piece
        o_ref[...] = total

    grid_spec = pltpu.PrefetchScalarGridSpec(
        num_scalar_prefetch=1, grid=(nb,),
        in_specs=[pl.BlockSpec((n_dev, th, cc), lambda i, p: (0, i, 0)),
                  pl.BlockSpec((None, th, cc), lambda i, p: (p[1], p[2] * nb + i, 0))],
        out_specs=pl.BlockSpec((th, cc), lambda i, p: (i, 0)))
    return pl.pallas_call(
        body, name=name, grid_spec=grid_spec,
        out_shape=jax.ShapeDtypeStruct((h, cc), F32), compiler_params=_params(1),
    )(place, landed, own)


def _share_halves(name, halves):
    flat = [t for per_weight in halves for t in per_weight]
    n = len(flat)
    n_w = len(halves)

    def body(*refs):
        ins, outs = refs[:n], refs[n:n + n_w]
        send_sems, recv_sems, local_sems = refs[n + n_w:n + n_w + 3]
        stage = refs[n + n_w + 3:]
        x, y, c = _my_place()
        sends, own = [], []
        for i in range(n):
            w, l = divmod(i, DEPTH)
            h = ins[i].shape[0]
            own.append(outs[w].at[l, pl.ds(c * h, h), :])
            sends.append(_remote(ins[i], own[i], send_sems.at[i], recv_sems.at[i], (x, y, 1 - c)))
        for cp in sends:
            cp.start()
        local = _staged_copies(ins, own, stage, local_sems)
        for i in range(n):
            w, l = divmod(i, DEPTH)
            h = ins[i].shape[0]
            _remote(ins[i], outs[w].at[l, pl.ds((1 - c) * h, h), :], send_sems.at[i], recv_sems.at[i],
                    (x, y, 1 - c)).wait_recv()
        for cp in sends:
            cp.wait_send()
        for cp in local:
            cp.wait()

    return pl.pallas_call(
        body, name=name, in_specs=[ANY_SPEC] * n, out_specs=[ANY_SPEC] * n_w,
        out_shape=[jax.ShapeDtypeStruct((DEPTH, 2 * per_weight[0].shape[0], per_weight[0].shape[1]), F32)
                   for per_weight in halves],
        scratch_shapes=([pltpu.SemaphoreType.DMA((n,))] * 3 + [pltpu.VMEM(t.shape, t.dtype) for t in flat]),
        compiler_params=pltpu.CompilerParams(vmem_limit_bytes=VMEM_LIMIT),
    )(*flat)


def _all_reduce_small(pack, after):
    def body(p_ref, after_ref, o_ref, recv, send_sems, recv_sems):
        del after_ref
        x, y, c = _my_place()
        me = 4 * x + 2 * y + c
        recv[me] = p_ref[...]
        peers = []
        for k in range(1, N_DEV):
            px, py, pc = (x + (k >> 2)) % 2, (y + ((k >> 1) & 1)) % 2, (c + (k & 1)) % 2
            peers.append((px, py, pc))
        sends = [_remote(p_ref, recv.at[me], send_sems.at[k], recv_sems.at[k], peer)
                 for k, peer in enumerate(peers)]
        for cp in sends:
            cp.start()
        for k, (px, py, pc) in enumerate(peers):
            _remote(p_ref, recv.at[4 * px + 2 * py + pc], send_sems.at[k], recv_sems.at[k],
                    (px, py, pc)).wait_recv()
        for cp in sends:
            cp.wait_send()
        total = recv[0]
        for d in range(1, N_DEV):
            total = total + recv[d]
        o_ref[...] = total

    vmem = pl.BlockSpec(memory_space=pltpu.VMEM)
    return pl.pallas_call(
        body, name="all_reduce_small", in_specs=[vmem, ANY_SPEC], out_specs=vmem,
        out_shape=jax.ShapeDtypeStruct(pack.shape, F32),
        scratch_shapes=[pltpu.VMEM((N_DEV,) + pack.shape, F32),
                        pltpu.SemaphoreType.DMA((N_DEV - 1,)), pltpu.SemaphoreType.DMA((N_DEV - 1,))],
    )(pack, after)


def _adamw(name, w, g, m, v):
    r, cc = w.shape
    th = min(r, 256)

    def body(w_ref, g_ref, m_ref, v_ref, d_ref, m_out, v_out):
        gv = g_ref[...]
        m2 = ADAM_B1 * m_ref[...] + (1.0 - ADAM_B1) * gv
        v2 = ADAM_B2 * v_ref[...] + (1.0 - ADAM_B2) * (gv * gv)
        m_hat = m2 / (1.0 - ADAM_B1 ** ADAM_STEP)
        v_hat = v2 / (1.0 - ADAM_B2 ** ADAM_STEP)
        d_ref[...] = -ADAM_LR * (m_hat / (jnp.sqrt(v_hat) + ADAM_EPS) + ADAM_WD * w_ref[...])
        m_out[...] = m2
        v_out[...] = v2

    tile = _spec((th, cc), lambda i: (i, 0))
    return pl.pallas_call(
        body, name=name, grid=(r // th,), in_specs=[tile] * 4, out_specs=[tile] * 3,
        out_shape=[jax.ShapeDtypeStruct((r, cc), F32)] * 3, compiler_params=_params(1),
    )(w, g, m, v)


def _lower_bounds(lb_logits):
    p = jax.nn.softmax(lb_logits.astype(F32), axis=0)
    return jnp.cumsum(p, axis=0) - p[0]


def _layer_forward(l, stream, small, weights, consts, next_gain=None, loss=None, after=None):
    win, rest = weights
    cos_t, sin_t, stack, _, _ = consts
    tm = MM_TILE
    x_in, h, h_t = stream
    saved = {"x_in": x_in}

    proj = _mm_pieces(f"proj{l}", h, win, False, tm, after=after)
    saved.update(h_t=h_t, proj=proj)

    qkv = _attn_prep(f"attn_prep{l}", proj, cos_t, sin_t)
    outs, lses = [], []
    for p, d in enumerate(DILATIONS):
        o, lse = _attn_fwd(f"attn_fwd{l}_{d}", *qkv[p], SEQ // d // SPAN)
        outs.append(o)
        lses.append(lse)
    mixed, mixed_t, attn, lse = _attn_merge(f"attn_merge{l}", outs, lses, small["attn_out_gain"][l][None, :])
    saved.update(qkv=qkv, attn=attn, lse=lse)

    lb3 = small["lower"][l].reshape(HGRN_HEADS, 1, HGRN_DIM)
    mixed, mixed_t, o_pre, states, scores = _hgrn_fwd(f"hgrn_fwd{l}", proj, lb3, small["hgrn_out_gain"][l][None, :],
                                                      stack, mixed, mixed_t)
    wo, wu, wd = rest(mixed)
    saved.update(mixed_t=mixed_t, o_pre=o_pre, states=states, scores=scores, lb3=lb3, weights=(win, wo, wu, wd))

    x_mid, h2, h2_t = _mm_accum(f"out_proj{l}", mixed, wo, False, tm, x_in, next_gain=small["norm_mlp"][l][None, :])
    saved["x_mid"] = x_mid

    a, relu_u, a_t = _mm_pieces(f"up{l}", h2, wu, False, tm, epilogue="relu2")
    new_stream = tuple(_mm_accum(f"down{l}", a, wd, False, tm, x_mid, next_gain=next_gain, loss=loss))
    saved.update(h2_t=h2_t, relu_u=relu_u, a_t=a_t)
    return new_stream, saved


def _layer_backward(l, dx, saved, small, consts, on_grads, after=None):
    win, wo, wu, wd = saved["weights"]
    cos_t, sin_t, stack, stack_t, head_sum = consts
    tm = MM_TILE

    dx, dx_b = dx
    du = _mm_pieces(f"d_u{l}", dx_b, wd, True, tm, epilogue="relu2_grad", extra=saved["relu_u"], after=after)
    d_wd = _mm_dw(f"d_wdown{l}", saved["a_t"], dx_b, False, tm)
    dxm, dxm_b, dg_mlp = _mm_accum(f"d_h2_{l}", du, wu, True, tm, dx,
                                   norm=(saved["x_mid"], small["norm_mlp"][l][None, :]))
    d_wu = _mm_dw(f"d_wup{l}", saved["h2_t"], du, True, tm)
    d_wo = _mm_dw(f"d_wout{l}", saved["mixed_t"], dxm_b, False, tm)
    after_early = on_grads(l, "early", (d_wu, d_wd, d_wo))

    d_mixed = _mm_pieces(f"d_mixed{l}", dxm_b, wo, True, tm, after=after_early)
    d_rec = d_mixed

    d_out, delta, lses, dg_attn = _attn_bwd_prep(f"attn_bwd_prep{l}", d_mixed, saved["attn"], saved["lse"],
                                                 small["attn_out_gain"][l][None, :], head_sum)
    grads = []
    for p, d in enumerate(DILATIONS):
        grads.append(_attn_bwd(f"attn_bwd{l}_{d}", *saved["qkv"][p], d_out[p], delta[p], lses[p],
                               SEQ // d // SPAN))
    dp_attn = _attn_bwd_post(f"attn_bwd_post{l}", grads, cos_t, sin_t)

    dq_h, df_h, di_h, dg_h, d_lower, dg_hgrn = _hgrn_bwd(
        f"hgrn_bwd{l}", saved["proj"], d_rec, saved["o_pre"], saved["states"], saved["scores"],
        saved["lb3"], small["hgrn_out_gain"][l][None, :], stack, stack_t)
    dproj = [dp_attn, dq_h, df_h, di_h, dg_h]

    d_win = _mm_dw(f"d_win{l}", saved["h_t"], dproj, True, tm)
    after_last = on_grads(l, "last", (d_win,))
    dx_in, dx_in_b, dg_mix = _mm_accum(f"d_h{l}", dproj, win, True, tm, dxm,
                                       norm=(saved["x_in"], small["norm_mix"][l][None, :]), after=after_last)

    small_grads = {"norm_mix": dg_mix[0], "attn_out_gain": dg_attn[0],
                   "lower": d_lower.reshape(HGRN_WIDTH),
                   "hgrn_out_gain": jnp.sum(dg_hgrn, axis=0).reshape(HGRN_DIM), "norm_mlp": dg_mlp[0]}
    return (dx_in, dx_in_b), after_last, small_grads


def _local_step(xs, target, small, get_weights, on_grads):
    consts = _rope_tables() + _hgrn_consts() + (_head_sum_matrix(),)
    stream = (xs,) + tuple(_rms_fwd("norm_mix0", xs, small["norm_mix"][0][None, :]))
    saved = []
    for l in range(DEPTH):
        w, after = get_weights(l, stream[0])
        if l + 1 < DEPTH:
            stream, s = _layer_forward(l, stream, small, w, consts, next_gain=small["norm_mix"][l + 1][None, :],
                                       after=after)
        else:
            stream, s = _layer_forward(l, stream, small, w, consts, loss=(small["norm_final"][None, :], target),
                                       after=after)
        saved.append(s)
    dx_f, dx_b, dg_final, loss = stream
    dx = (dx_f, dx_b)
    small_grads = [None] * DEPTH
    after = None
    for l in reversed(range(DEPTH)):
        dx, after, small_grads[l] = _layer_backward(l, dx, saved[l], small, consts, on_grads, after=after)
    return loss, dx[0], dg_final[0], small_grads


def _pack_small(norm_mix, attn_out_gain, lb, hgrn_out_gain, norm_mlp, norm_final, last_row):
    rows = [norm_mix, attn_out_gain.reshape(1, D_MODEL), lb.reshape(1, D_MODEL),
            jnp.pad(hgrn_out_gain.reshape(1, DEPTH * HGRN_DIM), ((0, 0), (0, D_MODEL - DEPTH * HGRN_DIM))),
            norm_mlp, norm_final.reshape(1, D_MODEL), last_row.reshape(1, D_MODEL)]
    pack = jnp.concatenate(rows, axis=0)
    return jnp.pad(pack, ((0, PACK_ROWS - pack.shape[0]), (0, 0)))


def _unpack_small(pack):
    return (pack[0:2], pack[2].reshape(DEPTH, ATTN_WIDTH), pack[3].reshape(DEPTH, HGRN_WIDTH),
            pack[4, :DEPTH * HGRN_DIM].reshape(DEPTH, HGRN_DIM), pack[5:7], pack[7], pack[8])


def kernel(x, norm_mix, w_in, attn_out_gain, hgrn_lb_logits, hgrn_out_gain, w_out, norm_mlp, w_up, w_down, norm_final, loss_target, m_norm_mix, m_w_in, m_attn_out_gain, m_hgrn_lb_logits, m_hgrn_out_gain, m_w_out, m_norm_mlp, m_w_up, m_w_down, m_norm_final, v_norm_mix, v_w_in, v_attn_out_gain, v_hgrn_lb_logits, v_hgrn_out_gain, v_w_out, v_norm_mlp, v_w_up, v_w_down, v_norm_final):
    lower, lower_vjp = jax.vjp(_lower_bounds, hgrn_lb_logits)
    small = {"norm_mix": norm_mix, "attn_out_gain": attn_out_gain, "lower": lower,
             "hgrn_out_gain": hgrn_out_gain, "norm_mlp": norm_mlp, "norm_final": norm_final}
    big_w = (w_in, w_out, w_up, w_down)

    x_pos, y_pos, core = lax.axis_index("x"), lax.axis_index("y"), lax.axis_index("c")
    me = (2 * x_pos + y_pos).astype(jnp.int32).reshape(1)
    place = jnp.stack([4 * x_pos + 2 * y_pos + core, 2 * x_pos + y_pos, core]).astype(jnp.int32)
    shards = [[w[l].astype(BF16) for w in big_w] for l in range(DEPTH)]
    in_flight = {}

    def start_gather(name, some, after):
        lands = [_own_slot(f"own_{name}_{i}", s, me) for i, s in enumerate(some)]
        sems, passed, token = _exchange_start(f"start_{name}", some, lands, "gather", after)
        in_flight[name] = (sems, passed)
        return token

    def finish_gather(name, after):
        return _exchange_wait(f"wait_{name}", *in_flight.pop(name), "gather", after)[1]

    def get_weights(l, stream):
        if l == 0:
            (win,) = _gather_weights("gather_w_in0", shards[0][:1])
            token = start_gather("gather_rest0", shards[0][1:], win)
            token = start_gather("gather_w_in1", shards[1][:1], token)
            token = start_gather("gather_rest1", shards[1][1:], token)
            return (win, lambda after: finish_gather("gather_rest0", after)), token
        (win,) = finish_gather("gather_w_in1", stream)
        return (win, lambda after: finish_gather("gather_rest1", after)), None

    reduced = {}

    def start_exchange(name, grads):
        srcs, lands = [g for g, _ in grads], [land for _, land in grads]
        sems, passed, token = _exchange_start(f"start_{name}", srcs, lands, "scatter")
        in_flight[name] = (sems, passed)
        return token

    def finish_exchange(name, after):
        own, landed = _exchange_wait(f"wait_{name}", *in_flight.pop(name), "scatter", after)
        return [_sum_devices(f"sum_{name}_{i}", p, g, place) for i, (p, g) in enumerate(zip(landed, own))]

    def on_grads(l, group, grads):
        token = start_exchange(f"{group}{l}", grads)
        if (l, group) == (0, "early"):
            reduced[(1, "early")] = finish_exchange("early1", token)
            reduced[(1, "last")] = finish_exchange("last1", token)
        if (l, group) == (0, "last"):
            reduced[(0, "early")] = finish_exchange("early0", token)
        return token

    loss, dx, dg_final, sg = _local_step(x[0], loss_target[0], small, get_weights, on_grads)

    big_m = (m_w_in, m_w_out, m_w_up, m_w_down)
    big_v = (v_w_in, v_w_out, v_w_up, v_w_down)
    names = ("w_in", "w_out", "w_up", "w_down")
    big_g, big_delta, big_new_m, big_new_v = [None] * 4, [None] * 4, [None] * 4, [None] * 4

    def finish_weights(group, which):
        whole = _share_halves(f"share_{group}", [[reduced[(l, group)][i] for l in range(DEPTH)]
                                                 for i in range(len(which))])
        for i, w in enumerate(which):
            shape = big_w[w].shape
            flat = lambda arr: arr.reshape(shape[0] * shape[1], shape[2])
            d, m2, v2 = _adamw(f"adamw_{names[w]}", flat(big_w[w]), flat(whole[i]), flat(big_m[w]), flat(big_v[w]))
            big_g[w], big_delta[w] = whole[i], d.reshape(shape)
            big_new_m[w], big_new_v[w] = m2.reshape(shape), v2.reshape(shape)

    finish_weights("early", (2, 3, 1))
    reduced[(0, "last")] = finish_exchange("last0", big_delta[3])
    finish_weights("last", (0,))

    stack2 = lambda key: jnp.stack([sg[l][key] for l in range(DEPTH)])
    pack = _pack_small(stack2("norm_mix"), stack2("attn_out_gain"), stack2("lower"), stack2("hgrn_out_gain"),
                       stack2("norm_mlp"), dg_final, jnp.broadcast_to(loss[0, 0], (D_MODEL,)))
    g_mix, g_attn, g_lower, g_hgrn, g_mlp, g_final, loss_row = _unpack_small(_all_reduce_small(pack, big_delta[0]))
    (g_logits,) = lower_vjp(g_lower)

    zeros_row = jnp.zeros((D_MODEL,), F32)
    small_w = (norm_mix, attn_out_gain, hgrn_lb_logits, hgrn_out_gain, norm_mlp, norm_final)
    small_m = (m_norm_mix, m_attn_out_gain, m_hgrn_lb_logits, m_hgrn_out_gain, m_norm_mlp, m_norm_final)
    small_v = (v_norm_mix, v_attn_out_gain, v_hgrn_lb_logits, v_hgrn_out_gain, v_norm_mlp, v_norm_final)
    small_g = (g_mix, g_attn, g_logits, g_hgrn, g_mlp, g_final)
    packs = [_pack_small(*t, zeros_row) for t in (small_w, small_g, small_m, small_v)]
    small_delta, small_new_m, small_new_v = [_unpack_small(p)[:6] for p in _adamw("adamw_small", *packs)]

    def ordered(small6, big4):
        mix, attn, lbl, hg, mlp, fin = small6
        return (mix, big4[0], attn, lbl, hg, big4[1], mlp, big4[2], big4[3], fin)

    return ((loss_row[0], dx[None]) + ordered(small_g, big_g) + ordered(small_delta, big_delta)
            + ordered(small_new_m, big_new_m) + ordered(small_new_v, big_new_v))
```

```python
import numpy as np
import jax
import jax.numpy as jnp
from jax import lax
from jax.experimental import pallas as pl
from jax.experimental.pallas import tpu as pltpu

F32 = jnp.float32
BF16 = jnp.bfloat16
MESH = pl.DeviceIdType.MESH

SEQ = 4096
D_MODEL = 1024
DEPTH = 2
ATTN_WIDTH = 512
HEAD_DIM = 64
HGRN_HEADS = 4
HGRN_DIM = 128
HGRN_WIDTH = 512
IN_W = 3584
MLP_HIDDEN = 4096
N_CHIPS = 4
N_DEV = 8
DILATIONS = (1, 4, 16)
SPAN = 128
ROPE_THETA = 10000.0
NORM_EPS = 1e-6
MASK_VALUE = -1e30
CHUNK = 128
ROW_TILE = 512
MM_TILE = 512
VMEM_LIMIT = 52 * 1024 * 1024

ADAM_LR = 0.001
ADAM_B1 = 0.9
ADAM_B2 = 0.999
ADAM_EPS = 1e-08
ADAM_WD = 0.01
ADAM_STEP = 10

PACK_ROWS = 16


def _params(n_axes):
    return pltpu.CompilerParams(dimension_semantics=("arbitrary",) * n_axes,
                                vmem_limit_bytes=VMEM_LIMIT)


def _dot(a, b):
    return jnp.dot(a.astype(BF16), b.astype(BF16), preferred_element_type=F32)


def _dot_nt(a, b):
    return lax.dot_general(a.astype(BF16), b.astype(BF16), (((1,), (1,)), ((), ())),
                           preferred_element_type=F32)


def _dot_tn(a, b):
    return lax.dot_general(a.astype(BF16), b.astype(BF16), (((0,), (0,)), ((), ())),
                           preferred_element_type=F32)


def _sigmoid(x):
    return 1.0 / (1.0 + jnp.exp(-x))


def _spec(shape, index_map):
    return pl.BlockSpec(shape, index_map)


def _mm_pieces(name, a, w, nt, tm, epilogue="none", extra=None, after=None):
    s = a.shape[0]
    pw = w.shape[1] if nt else w.shape[2]
    width = N_CHIPS * pw

    def body(a_ref, w_ref, *rest):
        e_ref = rest[0] if extra is not None else None
        outs = rest[-3:] if epilogue == "relu2" else rest[-1:]
        av = a_ref[...].astype(BF16)
        for j in range(N_CHIPS):
            cols = slice(j * pw, (j + 1) * pw)
            r = _dot_nt(av, w_ref[j]) if nt else _dot(av, w_ref[j])
            if epilogue == "relu2":
                relu = jnp.maximum(r, 0.0)
                r = relu * relu
                outs[1][:, cols] = relu.astype(BF16)
                outs[2][cols, :] = r.T.astype(BF16)
            elif epilogue == "relu2_grad":
                r = r * (2.0 * e_ref[:, cols].astype(F32))
            outs[0][:, cols] = r.astype(outs[0].dtype)

    row = lambda width_: _spec((tm, width_), lambda i: (i, 0))
    in_specs = [row(a.shape[1]), _spec(w.shape, lambda i: (0, 0, 0))]
    args = [a, w]
    if extra is not None:
        in_specs.append(row(width))
        args.append(extra)
    if after is not None:
        in_specs.append(pl.BlockSpec(memory_space=pl.ANY))
        args.append(after)
    if epilogue == "relu2":
        out_specs = [row(width), row(width), _spec((width, tm), lambda i: (0, i))]
        out_shape = [jax.ShapeDtypeStruct((s, width), BF16)] * 2 + [jax.ShapeDtypeStruct((width, s), BF16)]
    else:
        out_specs = row(width)
        out_shape = jax.ShapeDtypeStruct((s, width), BF16 if epilogue == "relu2_grad" else F32)
    return pl.pallas_call(body, name=name, grid=(s // tm,), in_specs=in_specs, out_specs=out_specs,
                          out_shape=out_shape, compiler_params=_params(1))(*args)


def _mm_accum(name, a, w, nt, tm, resid, norm=None, after=None, next_gain=None, loss=None):
    pieces = list(a) if isinstance(a, (list, tuple)) else [a]
    n_a = len(pieces)
    s = pieces[0].shape[0]
    pk = w.shape[2] if nt else w.shape[1]
    d = w.shape[1] if nt else w.shape[2]

    def body(*refs):
        a_refs, w_ref, resid_ref, rest = refs[:n_a], refs[n_a], refs[n_a + 1], refs[n_a + 2:]
        av = a_refs[0][...] if n_a == 1 else jnp.concatenate([ref[...] for ref in a_refs], axis=1)
        r = None
        for j in range(N_CHIPS):
            piece = av[:, j * pk:(j + 1) * pk].astype(BF16)
            term = _dot_nt(piece, w_ref[j]) if nt else _dot(piece, w_ref[j])
            r = term if r is None else r + term
        if loss is not None:
            g_ref, t_ref = rest[:2]
            dx_ref, dxb_ref, dg_ref, loss_ref, acc = rest[-5:]
            i = pl.program_id(0)

            @pl.when(i == 0)
            def _():
                dg_ref[...] = jnp.zeros_like(dg_ref)
                acc[...] = jnp.zeros_like(acc)

            xv = r + resid_ref[...]
            g = g_ref[...]
            rs = lax.rsqrt(jnp.mean(xv * xv, axis=1, keepdims=True) + NORM_EPS)
            xhat = xv * rs
            err = xhat * g - t_ref[...]
            acc[...] += jnp.sum(err * err, axis=0, keepdims=True)
            dy = err * (1.0 / d)
            dyg = dy * g
            dx = rs * (dyg - xhat * jnp.mean(dyg * xhat, axis=1, keepdims=True))
            dx_ref[...] = dx
            dxb_ref[...] = dx.astype(BF16)
            dg_ref[...] += jnp.sum(dy * xhat, axis=0, keepdims=True)

            @pl.when(i == s // tm - 1)
            def _():
                total = jnp.sum(acc[...], axis=1, keepdims=True) * (0.5 / d)
                loss_ref[...] = jnp.broadcast_to(total, loss_ref.shape)

            return
        if norm is None and next_gain is None:
            rest[-1][...] = r + resid_ref[...]
            return
        if norm is None:
            g_ref = rest[0]
            x_out, h_out, ht_out = rest[-3:]
            xv = r + resid_ref[...]
            x_out[...] = xv
            h = xv * lax.rsqrt(jnp.mean(xv * xv, axis=1, keepdims=True) + NORM_EPS) * g_ref[...]
            h_out[...] = h.astype(BF16)
            ht_out[...] = h.T.astype(BF16)
            return
        x_ref, g_ref = rest[:2]
        dx_ref, dxb_ref, dg_ref = rest[-3:]

        @pl.when(pl.program_id(0) == 0)
        def _():
            dg_ref[...] = jnp.zeros_like(dg_ref)

        xv = x_ref[...]
        rs = lax.rsqrt(jnp.mean(xv * xv, axis=1, keepdims=True) + NORM_EPS)
        xhat = xv * rs
        rg = r * g_ref[...]
        dx = resid_ref[...] + rs * (rg - xhat * jnp.mean(rg * xhat, axis=1, keepdims=True))
        dx_ref[...] = dx
        dxb_ref[...] = dx.astype(BF16)
        dg_ref[...] += jnp.sum(r * xhat, axis=0, keepdims=True)

    row = lambda width: _spec((tm, width), lambda i: (i, 0))
    in_specs = [row(p.shape[1]) for p in pieces] + [_spec(w.shape, lambda i: (0, 0, 0)), row(d)]
    args = pieces + [w, resid]
    scratch = []
    if loss is not None:
        in_specs += [_spec((1, d), lambda i: (0, 0)), row(d)]
        args += list(loss)
        out_specs = [row(d), row(d), _spec((1, d), lambda i: (0, 0)), _spec((1, 128), lambda i: (0, 0))]
        out_shape = [jax.ShapeDtypeStruct((s, d), F32), jax.ShapeDtypeStruct((s, d), BF16),
                     jax.ShapeDtypeStruct((1, d), F32), jax.ShapeDtypeStruct((1, 128), F32)]
        scratch = [pltpu.VMEM((1, d), F32)]
    elif norm is None and next_gain is None:
        out_specs, out_shape = row(d), jax.ShapeDtypeStruct((s, d), F32)
    elif norm is None:
        in_specs.append(_spec((1, d), lambda i: (0, 0)))
        args.append(next_gain)
        out_specs = [row(d), row(d), _spec((d, tm), lambda i: (0, i))]
        out_shape = [jax.ShapeDtypeStruct((s, d), F32), jax.ShapeDtypeStruct((s, d), BF16),
                     jax.ShapeDtypeStruct((d, s), BF16)]
    else:
        in_specs += [row(d), _spec((1, d), lambda i: (0, 0))]
        args += list(norm)
        out_specs = [row(d), row(d), _spec((1, d), lambda i: (0, 0))]
        out_shape = [jax.ShapeDtypeStruct((s, d), F32), jax.ShapeDtypeStruct((s, d), BF16),
                     jax.ShapeDtypeStruct((1, d), F32)]
    if after is not None:
        in_specs.append(pl.BlockSpec(memory_space=pl.ANY))
        args.append(after)
    return pl.pallas_call(body, name=name, grid=(s // tm,), in_specs=in_specs, out_specs=out_specs,
                          out_shape=out_shape, scratch_shapes=scratch, compiler_params=_params(1))(*args)


def _mm_dw(name, a_t, b, by_cols, tk):
    pieces = list(b) if isinstance(b, (list, tuple)) else [b]
    n_b = len(pieces)
    m, s = a_t.shape
    n = sum(p.shape[1] for p in pieces)
    shape = (N_CHIPS, m, n // N_CHIPS) if by_cols else (N_CHIPS, m // N_CHIPS, n)
    n_steps = s // tk

    def body(a_ref, *rest):
        b_refs, o_ref, acc = rest[:n_b], rest[n_b], rest[-1]

        @pl.when(pl.program_id(0) == 0)
        def _():
            acc[...] = jnp.zeros_like(acc)

        bv = b_refs[0][...] if n_b == 1 else jnp.concatenate([ref[...] for ref in b_refs], axis=1)
        for j in range(N_CHIPS):
            if by_cols:
                acc[j] += _dot(a_ref[...], bv[:, j * shape[2]:(j + 1) * shape[2]])
            else:
                acc[j] += _dot(a_ref[j * shape[1]:(j + 1) * shape[1], :], bv)

        @pl.when(pl.program_id(0) == n_steps - 1)
        def _():
            o_ref[...] = acc[...].astype(BF16)

    return pl.pallas_call(
        body, name=name, grid=(n_steps,),
        in_specs=[_spec((m, tk), lambda k: (0, k))] + [_spec((tk, p.shape[1]), lambda k: (k, 0)) for p in pieces],
        out_specs=[_spec(shape, lambda k: (0, 0, 0)), ANY_SPEC],
        out_shape=[jax.ShapeDtypeStruct(shape, BF16),
                   jax.ShapeDtypeStruct((N_DEV, shape[1] // 2, shape[2]), BF16)],
        scratch_shapes=[pltpu.VMEM(shape, F32)],
        compiler_params=_params(1))(a_t, *pieces)


def _rms_fwd(name, x, gain):
    s, d = x.shape
    t = ROW_TILE

    def body(x_ref, g_ref, h_ref, ht_ref):
        xv = x_ref[...]
        r = lax.rsqrt(jnp.mean(xv * xv, axis=1, keepdims=True) + NORM_EPS)
        h = xv * r * g_ref[...]
        h_ref[...] = h.astype(BF16)
        ht_ref[...] = h.T.astype(BF16)

    return pl.pallas_call(
        body, name=name, grid=(s // t,),
        in_specs=[_spec((t, d), lambda i: (i, 0)), _spec((1, d), lambda i: (0, 0))],
        out_specs=[_spec((t, d), lambda i: (i, 0)), _spec((d, t), lambda i: (0, i))],
        out_shape=[jax.ShapeDtypeStruct((s, d), BF16), jax.ShapeDtypeStruct((d, s), BF16)],
        compiler_params=_params(1),
    )(x, gain)


def _rope_tables():
    half = HEAD_DIM // 2
    inv_freq = ROPE_THETA ** (-jnp.arange(half, dtype=F32) / half)
    ang = jnp.arange(SEQ, dtype=jnp.int32).astype(F32)[:, None] * inv_freq[None, :]
    cos, sin = jnp.cos(ang), jnp.sin(ang)
    cos_t = jnp.concatenate([cos, cos, cos, cos], axis=1)
    sin_t = jnp.concatenate([-sin, sin, -sin, sin], axis=1)
    return cos_t, sin_t


def _swap_halves(x):
    lane = lax.broadcasted_iota(jnp.int32, x.shape, 1)
    first = (lane % HEAD_DIM) < (HEAD_DIM // 2)
    return jnp.where(first, pltpu.roll(x, 128 - HEAD_DIM // 2, 1), pltpu.roll(x, HEAD_DIM // 2, 1))


def _permuted_specs(t, width):
    specs = [_spec((t, width), lambda i: (i, 0))]
    for d in DILATIONS[1:]:
        specs.append(_spec((d, t // d, width), lambda i: (0, i, 0)))
    return specs


def _permuted_shapes(width, dtype):
    shapes = [jax.ShapeDtypeStruct((SEQ, width), dtype)]
    for d in DILATIONS[1:]:
        shapes.append(jax.ShapeDtypeStruct((d, SEQ // d, width), dtype))
    return shapes


def _attn_prep(name, proj, cos_t, sin_t):
    t = ROW_TILE
    w = ATTN_WIDTH

    def body(q_ref, k_ref, v_ref, cos_ref, sin_ref, *rest):
        outs, scr = rest[:9], rest[9]
        cosv, sinv = cos_ref[...], sin_ref[...]
        for a, (src, roped, scale) in enumerate(((q_ref, True, HEAD_DIM ** -0.5),
                                                 (k_ref, True, 1.0), (v_ref, False, 1.0))):
            o1, o4, o16 = outs[3 * a:3 * a + 3]
            for cb in range(w // 128):
                cols = slice(cb * 128, (cb + 1) * 128)
                val = src[:, cols]
                if roped:
                    val = (val * cosv + _swap_halves(val) * sinv) * scale
                scr[...] = val
                o1[:, cols] = val.astype(BF16)
                for o_ref, d in ((o4, 4), (o16, 16)):
                    for r in range(d):
                        o_ref[r, :, cols] = scr[pl.ds(r, t // d, stride=d), :].astype(BF16)

    out_specs = _permuted_specs(t, w) * 3
    out_shape = _permuted_shapes(w, BF16) * 3
    outs = pl.pallas_call(
        body, name=name, grid=(SEQ // t,),
        in_specs=[_spec((t, w), lambda i: (i, 0)), _spec((t, w), lambda i: (i, 1)),
                  _spec((t, w), lambda i: (i, 2)),
                  _spec((t, 128), lambda i: (i, 0)), _spec((t, 128), lambda i: (i, 0))],
        out_specs=out_specs, out_shape=out_shape,
        scratch_shapes=[pltpu.VMEM((t, 128), F32)],
        compiler_params=_params(1),
    )(proj, proj, proj, cos_t, sin_t)
    q, k, v = outs[0:3], outs[3:6], outs[6:9]
    flat = lambda arr: arr.reshape(SEQ, w)
    return [(flat(q[p]), flat(k[p]), flat(v[p])) for p in range(3)]


def _band_masks():
    row = lax.broadcasted_iota(jnp.int32, (2 * SPAN, 2 * SPAN), 0) % SPAN
    col = lax.broadcasted_iota(jnp.int32, (2 * SPAN, 2 * SPAN), 1)
    is_prev = col < SPAN
    band = (is_prev & (col >= row)) | (~is_prev & (col - SPAN <= row))
    head0 = lax.broadcasted_iota(jnp.int32, (SPAN, 128), 1) < HEAD_DIM
    return band, head0


def _stack_heads(x, head0):
    zero = jnp.zeros_like(x)
    return jnp.concatenate([jnp.where(head0, x, zero), jnp.where(head0, zero, x)], axis=0)


def _for_each_block(block, seg_blocks):
    for b in range(SEQ // SPAN):
        block(b, b % seg_blocks == 0)


def _attn_fwd(name, q, k, v, seg_blocks):
    def body(q_ref, k_ref, v_ref, o_ref, lse_ref):
        band, head0 = _band_masks()

        def block(b, first):
            cur = pl.ds(b * SPAN, SPAN)
            qs = _stack_heads(q_ref[cur, :], head0)
            if first:
                kcat, vcat, ok = k_ref[cur, :], v_ref[cur, :], band[:, SPAN:]
            else:
                both = pl.ds((b - 1) * SPAN, 2 * SPAN)
                kcat, vcat, ok = k_ref[both, :], v_ref[both, :], band
            s = jnp.where(ok, _dot_nt(qs, kcat), MASK_VALUE)
            m = jnp.max(s, axis=1, keepdims=True)
            p = jnp.exp(s - m)
            l = jnp.sum(p, axis=1, keepdims=True)
            pv = _dot(p, vcat) * (1.0 / l)
            lse = m + jnp.log(l)
            o_ref[cur, :] = jnp.where(head0, pv[:SPAN], pv[SPAN:])
            lse_ref[cur, :] = jnp.where(head0, lse[:SPAN], lse[SPAN:])

        _for_each_block(block, seg_blocks)

    col = _spec((SEQ, 128), lambda j: (0, j))
    return pl.pallas_call(
        body, name=name, grid=(ATTN_WIDTH // 128,),
        in_specs=[col, col, col], out_specs=[col, col],
        out_shape=[jax.ShapeDtypeStruct((SEQ, ATTN_WIDTH), F32)] * 2,
        compiler_params=_params(1),
    )(q, k, v)


def _unpermute(dst, src_ref, d, cols):
    n = dst.shape[0] // d
    for r in range(d):
        dst[pl.ds(r, n, stride=d), :] = src_ref[r, :, cols].astype(dst.dtype)


def _attn_merge(name, outs, lses, gain):
    t = ROW_TILE
    w = ATTN_WIDTH

    def body(o1, o4, o16, l1, l4, l16, g_ref, an_ref, ant_ref, attn_ref, lse_ref, so4, so16, sl4, sl16):
        for cb in range(w // 128):
            cols = slice(cb * 128, (cb + 1) * 128)
            _unpermute(so4, o4, 4, cols)
            _unpermute(so16, o16, 16, cols)
            _unpermute(sl4, l4, 4, cols)
            _unpermute(sl16, l16, 16, cols)
            la, lb, lc = l1[:, cols], sl4[...], sl16[...]
            m = jnp.maximum(jnp.maximum(la, lb), lc)
            ea, eb, ec = jnp.exp(la - m), jnp.exp(lb - m), jnp.exp(lc - m)
            tot = ea + eb + ec
            attn_ref[:, cols] = (ea * o1[:, cols] + eb * so4[...] + ec * so16[...]) / tot
            lse_ref[:, cols] = m + jnp.log(tot)
        attn = attn_ref[...]
        r = lax.rsqrt(jnp.mean(attn * attn, axis=1, keepdims=True) + NORM_EPS)
        an = attn * r * g_ref[...]
        an_ref[...] = an.astype(BF16)
        ant_ref[...] = an.T.astype(BF16)

    views = lambda arrs: [arrs[0], arrs[1].reshape(4, SEQ // 4, w), arrs[2].reshape(16, SEQ // 16, w)]
    row = _spec((t, w), lambda i: (i, 0))
    return pl.pallas_call(
        body, name=name, grid=(SEQ // t,),
        in_specs=_permuted_specs(t, w) * 2 + [_spec((1, w), lambda i: (0, 0))],
        out_specs=[row, _spec((w, t), lambda i: (0, i)), row, row],
        out_shape=[jax.ShapeDtypeStruct((SEQ, 2 * w), BF16), jax.ShapeDtypeStruct((2 * w, SEQ), BF16),
                   jax.ShapeDtypeStruct((SEQ, w), F32), jax.ShapeDtypeStruct((SEQ, w), F32)],
        scratch_shapes=[pltpu.VMEM((t, 128), F32)] * 4,
        compiler_params=_params(1),
    )(*views(outs), *views(lses), gain)


def _head_sum_matrix():
    i = np.arange(ATTN_WIDTH)
    return jnp.asarray((i[:, None] // HEAD_DIM) == (i[None, :] // HEAD_DIM), dtype=F32)


def _attn_bwd_prep(name, d_an, attn, lse, gain, head_sum):
    t = ROW_TILE
    w = ATTN_WIDTH

    def body(dan_ref, attn_ref, lse_ref, g_ref, hs_ref, *rest):
        (do1, do4, do16, dl1, dl4, dl16, ls4, ls16, dg_ref), (sdo, sdl, sls) = rest[:9], rest[9:]

        @pl.when(pl.program_id(0) == 0)
        def _():
            dg_ref[...] = jnp.zeros_like(dg_ref)

        attn = attn_ref[...]
        dan = dan_ref[...]
        r = lax.rsqrt(jnp.mean(attn * attn, axis=1, keepdims=True) + NORM_EPS)
        xhat = attn * r
        dg_ref[...] += jnp.sum(dan * xhat, axis=0, keepdims=True)
        dang = dan * g_ref[...]
        d_o = r * (dang - xhat * jnp.mean(dang * xhat, axis=1, keepdims=True))
        delta = jnp.dot(d_o * attn, hs_ref[...], preferred_element_type=F32,
                        precision=lax.Precision.HIGHEST)
        do1[...] = d_o.astype(BF16)
        dl1[...] = delta
        for cb in range(w // 128):
            cols = slice(cb * 128, (cb + 1) * 128)
            sdo[...] = d_o[:, cols]
            sdl[...] = delta[:, cols]
            sls[...] = lse_ref[:, cols]
            for d, o_do, o_dl, o_ls in ((4, do4, dl4, ls4), (16, do16, dl16, ls16)):
                for rr in range(d):
                    rows = pl.ds(rr, t // d, stride=d)
                    o_do[rr, :, cols] = sdo[rows, :].astype(BF16)
                    o_dl[rr, :, cols] = sdl[rows, :]
                    o_ls[rr, :, cols] = sls[rows, :]

    row = _spec((t, w), lambda i: (i, 0))
    perm = _permuted_specs(t, w)
    outs = pl.pallas_call(
        body, name=name, grid=(SEQ // t,),
        in_specs=[row, row, row, _spec((1, w), lambda i: (0, 0)), _spec((w, w), lambda i: (0, 0))],
        out_specs=perm + perm + perm[1:] + [_spec((1, w), lambda i: (0, 0))],
        out_shape=(_permuted_shapes(w, BF16) + _permuted_shapes(w, F32) + _permuted_shapes(w, F32)[1:]
                   + [jax.ShapeDtypeStruct((1, w), F32)]),
        scratch_shapes=[pltpu.VMEM((t, 128), F32)] * 3,
        compiler_params=_params(1),
    )(d_an, attn, lse, gain, head_sum)
    flat = lambda arr: arr.reshape(SEQ, w)
    d_out = [flat(a) for a in outs[0:3]]
    delta = [flat(a) for a in outs[3:6]]
    lses = [lse, flat(outs[6]), flat(outs[7])]
    return d_out, delta, lses, outs[8]


def _attn_bwd(name, q, k, v, d_out, delta, lse, seg_blocks):
    def body(q_ref, k_ref, v_ref, do_ref, dl_ref, lse_ref, dq_ref, dk_out, dv_out, dk_ref, dv_ref):
        band, head0 = _band_masks()
        dk_ref[...] = jnp.zeros_like(dk_ref)
        dv_ref[...] = jnp.zeros_like(dv_ref)

        def per_head(x):
            return jnp.concatenate([x[:, 0:1], x[:, HEAD_DIM:HEAD_DIM + 1]], axis=0)

        def block(b, first):
            cur = pl.ds(b * SPAN, SPAN)
            qs = _stack_heads(q_ref[cur, :], head0)
            dos = _stack_heads(do_ref[cur, :], head0)
            if first:
                kcat, vcat, ok = k_ref[cur, :], v_ref[cur, :], band[:, SPAN:]
            else:
                both = pl.ds((b - 1) * SPAN, 2 * SPAN)
                kcat, vcat, ok = k_ref[both, :], v_ref[both, :], band
            p = jnp.where(ok, jnp.exp(_dot_nt(qs, kcat) - per_head(lse_ref[cur, :])), 0.0)
            ds = p * (_dot_nt(dos, vcat) - per_head(dl_ref[cur, :]))
            dq = _dot(ds, kcat)
            dq_ref[cur, :] = jnp.where(head0, dq[:SPAN], dq[SPAN:]).astype(BF16)
            dk = _dot_tn(ds, qs)
            dv = _dot_tn(p, dos)
            if first:
                dk_ref[cur, :] += dk
                dv_ref[cur, :] += dv
            else:
                dk_ref[both, :] += dk
                dv_ref[both, :] += dv

        _for_each_block(block, seg_blocks)
        dk_out[...] = dk_ref[...].astype(BF16)
        dv_out[...] = dv_ref[...].astype(BF16)

    col = _spec((SEQ, 128), lambda j: (0, j))
    return pl.pallas_call(
        body, name=name, grid=(ATTN_WIDTH // 128,),
        in_specs=[col] * 6, out_specs=[col] * 3,
        out_shape=[jax.ShapeDtypeStruct((SEQ, ATTN_WIDTH), BF16)] * 3,
        scratch_shapes=[pltpu.VMEM((SEQ, 128), F32)] * 2,
        compiler_params=_params(1),
    )(q, k, v, d_out, delta, lse)


def _attn_bwd_post(name, grads, cos_t, sin_t):
    t = ROW_TILE
    w = ATTN_WIDTH

    def body(*refs):
        ins, cos_ref, sin_ref, out_ref, s4, s16 = refs[:9], refs[9], refs[10], refs[11], refs[12], refs[13]
        cosv, sinv = cos_ref[...], sin_ref[...]
        for a in range(3):
            g1, g4, g16 = ins[a], ins[3 + a], ins[6 + a]
            for cb in range(w // 128):
                cols = slice(cb * 128, (cb + 1) * 128)
                _unpermute(s4, g4, 4, cols)
                _unpermute(s16, g16, 16, cols)
                val = g1[:, cols].astype(F32) + s4[...] + s16[...]
                if a < 2:
                    val = val * cosv + _swap_halves(val * sinv)
                if a == 0:
                    val = val * (HEAD_DIM ** -0.5)
                out_ref[:, a * w + cb * 128:a * w + (cb + 1) * 128] = val.astype(BF16)

    views = []
    for p, d in enumerate(DILATIONS):
        for a in range(3):
            views.append(grads[p][a] if d == 1 else grads[p][a].reshape(d, SEQ // d, w))
    perm = _permuted_specs(t, w)
    in_specs = [perm[0]] * 3 + [perm[1]] * 3 + [perm[2]] * 3
    return pl.pallas_call(
        body, name=name, grid=(SEQ // t,),
        in_specs=in_specs + [_spec((t, 128), lambda i: (i, 0))] * 2,
        out_specs=_spec((t, 3 * w), lambda i: (i, 0)),
        out_shape=jax.ShapeDtypeStruct((SEQ, 3 * w), BF16),
        scratch_shapes=[pltpu.VMEM((t, 128), F32)] * 2,
        compiler_params=_params(1),
    )(*views, cos_t, sin_t)


N_LEVELS = 7
HGRN_PAIR = 4


def _hgrn_consts():
    c = CHUNK
    i = np.arange(c)[:, None]
    s = np.arange(c)[None, :]
    blocks = [s <= i]
    for lv in range(N_LEVELS):
        bs = c >> lv
        h = bs // 2
        m = (i // bs) * bs + h - 1
        second = (i % bs) >= h
        blocks.append((second & (s > m) & (s <= i)) | (~second & (s > i) & (s <= m)))
    blocks.append(s > i)
    stack = np.concatenate(blocks, axis=0).astype(np.float32)
    twice = np.concatenate([stack, stack], axis=1)
    return jnp.asarray(twice, dtype=BF16), jnp.asarray(stack.T, dtype=BF16)


def _split(x):
    hi = x.astype(BF16)
    return hi, (x - hi.astype(F32)).astype(BF16)


def _exact_dot(m01, x):
    n = x.shape[1]
    full = jnp.dot(m01, jnp.concatenate(_split(x), axis=1), preferred_element_type=F32)
    return full[:, :n] + full[:, n:]


def _exact_dot_twice(m01_twice, x):
    return jnp.dot(m01_twice, jnp.concatenate(_split(x), axis=0), preferred_element_type=F32)


def _hgrn_gates(qh, z, lb):
    sq = _sigmoid(qh)
    q = qh * sq * (HGRN_DIM ** -0.5)
    sig = _sigmoid(z)
    sigm = _sigmoid(-z)
    f = lb + (1.0 - lb) * sig
    k = (1.0 - lb) * sigm
    return q, k, f, sq, sig, sigm


def _level_masks(lv):
    row = lax.broadcasted_iota(jnp.int32, (CHUNK, CHUNK), 0)
    col = lax.broadcasted_iota(jnp.int32, (CHUNK, CHUNK), 1)
    shift = N_LEVELS - lv
    half = CHUNK >> (lv + 1)
    second = (row & half) != 0
    second_col = (col & half) != 0
    same = (row >> shift) == (col >> shift)
    return second, same & second & ~second_col, same & (second != second_col)


def _hgrn_fwd(name, proj, lb, gain, stack, mixed, mixed_t):
    t = ROW_TILE
    per = t // CHUNK
    n_rb = SEQ // t
    n_chunks = SEQ // CHUNK
    col0 = 3 * ATTN_WIDTH // 128
    pair_w = HGRN_PAIR * HGRN_DIM

    def body(q_ref, f_ref, i_ref, g_ref, lb_ref, gain_ref, stack_ref, mixed_in, mixed_t_in,
             rec_ref, rect_ref, o_ref, st_out, a_out, st):
        del mixed_in, mixed_t_in

        @pl.when(pl.program_id(1) == 0)
        def _():
            st[...] = jnp.zeros_like(st)

        row = lax.broadcasted_iota(jnp.int32, (CHUNK, CHUNK), 0)
        col = lax.broadcasted_iota(jnp.int32, (CHUNK, CHUNK), 1)
        masks = [_level_masks(lv) for lv in range(N_LEVELS)]
        for c, hh in [(c, hh) for c in range(per) for hh in range(HGRN_PAIR)]:
            rows = slice(c * CHUNK, (c + 1) * CHUNK)
            lanes = slice(hh * HGRN_DIM, (hh + 1) * HGRN_DIM)
            lbv = lb_ref[hh]
            qh, z, v, gh = q_ref[rows, lanes], f_ref[rows, lanes], i_ref[rows, lanes], g_ref[rows, lanes]
            q, k, f, _, _, _ = _hgrn_gates(qh, z, lbv)
            dec = _exact_dot_twice(stack_ref[...], jnp.log(f))
            g = dec[0:CHUNK]
            to_end = dec[(N_LEVELS + 1) * CHUNK:(N_LEVELS + 2) * CHUNK]
            a = jnp.where(row == col, jnp.sum(q * k, axis=1, keepdims=True), 0.0)
            for lv in range(N_LEVELS):
                second, square, _ = masks[lv]
                t = jnp.where(second, q, k) * jnp.exp(dec[(lv + 1) * CHUNK:(lv + 2) * CHUNK])
                a = a + jnp.where(square, _dot_nt(t, t), 0.0)
            st_prev = st[hh]
            st_out[hh, c] = st_prev
            a_out[hh, c] = a
            o = _dot(a, v) + _dot_nt(q * jnp.exp(g), st_prev)
            k_end = k * jnp.exp(to_end)
            st[hh] = st_prev * jnp.exp(g[CHUNK - 1:CHUNK, :]) + _dot(v.T, k_end)
            o_ref[rows, lanes] = o
            r = lax.rsqrt(jnp.mean(o * o, axis=1, keepdims=True) + NORM_EPS)
            rec = o * r * gain_ref[...] * (gh * _sigmoid(gh))
            rec_ref[rows, lanes] = rec.astype(BF16)
            rect_ref[lanes, rows] = rec.T.astype(BF16)

    def col_spec(tt):
        return _spec((t, pair_w), lambda h, rb: (rb, (col0 + HGRN_HEADS * tt) // HGRN_PAIR + h))

    chunk_spec = _spec((HGRN_PAIR, per, CHUNK, CHUNK), lambda h, rb: (h, rb, 0, 0))
    return pl.pallas_call(
        body, name=name, grid=(HGRN_HEADS // HGRN_PAIR, n_rb),
        in_specs=[col_spec(0), col_spec(1), col_spec(2), col_spec(3),
                  _spec((HGRN_PAIR, 1, HGRN_DIM), lambda h, rb: (h, 0, 0)),
                  _spec((1, HGRN_DIM), lambda h, rb: (0, 0)),
                  _spec(stack.shape, lambda h, rb: (0, 0)), ANY_SPEC, ANY_SPEC],
        out_specs=[_spec((t, pair_w), lambda h, rb: (rb, ATTN_WIDTH // pair_w + h)),
                   _spec((pair_w, t), lambda h, rb: (ATTN_WIDTH // pair_w + h, rb)),
                   _spec((t, pair_w), lambda h, rb: (rb, h)),
                   chunk_spec, chunk_spec],
        out_shape=[jax.ShapeDtypeStruct(mixed.shape, BF16),
                   jax.ShapeDtypeStruct(mixed_t.shape, BF16),
                   jax.ShapeDtypeStruct((SEQ, HGRN_WIDTH), F32),
                   jax.ShapeDtypeStruct((HGRN_HEADS, n_chunks, CHUNK, CHUNK), F32),
                   jax.ShapeDtypeStruct((HGRN_HEADS, n_chunks, CHUNK, CHUNK), F32)],
        scratch_shapes=[pltpu.VMEM((HGRN_PAIR, CHUNK, CHUNK), F32)],
        input_output_aliases={7: 0, 8: 1},
        compiler_params=_params(2),
    )(proj, proj, proj, proj, lb, gain, stack, mixed, mixed_t)


def _hgrn_bwd(name, proj, d_rec, o_pre, states, scores, lb, gain, stack, stack_t):
    t = ROW_TILE
    per = t // CHUNK
    n_rb = SEQ // t
    col0 = 3 * ATTN_WIDTH // 128
    pair_w = HGRN_PAIR * HGRN_DIM

    def body(q_ref, f_ref, i_ref, g_ref, drec_ref, o_ref, st_ref, a_ref, lb_ref, gain_ref,
             stack_ref, stack_t_ref, dq_ref, df_ref, di_ref, dg_ref, dlb_ref, dgain_ref, dst):
        @pl.when(pl.program_id(1) == 0)
        def _():
            dst[...] = jnp.zeros_like(dst)
            dlb_ref[...] = jnp.zeros_like(dlb_ref)
            dgain_ref[...] = jnp.zeros_like(dgain_ref)

        gain_v = gain_ref[...]
        row = lax.broadcasted_iota(jnp.int32, (CHUNK, CHUNK), 0)
        col = lax.broadcasted_iota(jnp.int32, (CHUNK, CHUNK), 1)
        masks = [_level_masks(lv) for lv in range(N_LEVELS)]
        for c, hh in [(c, hh) for c in reversed(range(per)) for hh in range(HGRN_PAIR)]:
            rows = slice(c * CHUNK, (c + 1) * CHUNK)
            lanes = slice(hh * HGRN_DIM, (hh + 1) * HGRN_DIM)
            lbv = lb_ref[hh]
            qh, z, v, gh = q_ref[rows, lanes], f_ref[rows, lanes], i_ref[rows, lanes], g_ref[rows, lanes]
            q, k, f, sq, sig, sigm = _hgrn_gates(qh, z, lbv)
            dec = _exact_dot_twice(stack_ref[...], jnp.log(f))
            g = dec[0:CHUNK]
            to_end = dec[(N_LEVELS + 1) * CHUNK:(N_LEVELS + 2) * CHUNK]
            e_g = jnp.exp(g)
            e_end = jnp.exp(to_end)
            e_last = jnp.exp(g[CHUNK - 1:CHUNK, :])
            q_in = q * e_g
            k_end = k * e_end
            st_prev = st_ref[hh, c]
            a = a_ref[hh, c]
            dst_new = dst[hh]

            o = o_ref[rows, lanes]
            drec = drec_ref[rows, lanes]
            sg = _sigmoid(gh)
            r = lax.rsqrt(jnp.mean(o * o, axis=1, keepdims=True) + NORM_EPS)
            ohat = o * r
            d_gh = drec * (ohat * gain_v) * (sg * (1.0 + gh * (1.0 - sg)))
            d_on = drec * (gh * sg)
            dgain_ref[hh] += jnp.sum(d_on * ohat, axis=0, keepdims=True)
            d_ohat = d_on * gain_v
            d_o = r * (d_ohat - ohat * jnp.mean(d_ohat * ohat, axis=1, keepdims=True))

            d_sym = jnp.where(row >= col, _dot_nt(d_o, v), _dot_nt(v, d_o))
            d_v = _dot(a.T, d_o) + _dot_nt(k_end, dst_new)
            d_q_in = _dot(d_o, st_prev)
            d_k_end = _dot(v, dst_new)
            d_q = d_q_in * e_g
            d_k = d_k_end * e_end
            diag = jnp.sum(d_o * v, axis=1, keepdims=True)
            d_q = d_q + diag * k
            d_k = d_k + diag * q
            d_dec = [q_in * d_q_in]
            d_both, d_second = None, None
            for lv in range(N_LEVELS):
                e = jnp.exp(dec[(lv + 1) * CHUNK:(lv + 2) * CHUNK])
                second, _, mirrored = masks[lv]
                t = jnp.where(second, q, k) * e
                d_t = _dot(jnp.where(mirrored, d_sym, 0.0), t)
                d_te = d_t * e
                d_both = d_te if d_both is None else d_both + d_te
                d_second = jnp.where(second, d_te, 0.0) if d_second is None else d_second + jnp.where(second, d_te, 0.0)
                d_dec.append(t * d_t)
            d_q = d_q + d_second
            d_k = d_k + (d_both - d_second)
            d_dec.append(k_end * d_k_end)
            flux = jnp.sum(dst_new * st_prev, axis=0, keepdims=True) * e_last
            d_lf = _exact_dot(stack_t_ref[...], jnp.concatenate(d_dec, axis=0)) + flux
            dst[hh] = dst_new * e_last + _dot(d_o.T, q_in)

            d_f = d_lf / f - d_k
            dlb_ref[hh] += jnp.sum(d_f * sigm, axis=0, keepdims=True)
            dq_ref[rows, lanes] = (d_q * (HGRN_DIM ** -0.5) * (sq * (1.0 + qh * (1.0 - sq)))).astype(BF16)
            df_ref[rows, lanes] = (d_f * (1.0 - lbv) * sig * sigm).astype(BF16)
            di_ref[rows, lanes] = d_v.astype(BF16)
            dg_ref[rows, lanes] = d_gh.astype(BF16)

    last = n_rb - 1

    def col_spec(tt):
        return _spec((t, pair_w), lambda h, rb: (last - rb, (col0 + HGRN_HEADS * tt) // HGRN_PAIR + h))

    head_col = _spec((t, pair_w), lambda h, rb: (last - rb, h))
    rec_col0 = (d_rec.shape[1] - HGRN_WIDTH) // pair_w
    d_rec_col = _spec((t, pair_w), lambda h, rb: (last - rb, rec_col0 + h))
    chunk_spec = _spec((HGRN_PAIR, per, CHUNK, CHUNK), lambda h, rb: (h, last - rb, 0, 0))
    vec_spec = _spec((HGRN_PAIR, 1, HGRN_DIM), lambda h, rb: (h, 0, 0))
    outs = pl.pallas_call(
        body, name=name, grid=(HGRN_HEADS // HGRN_PAIR, n_rb),
        in_specs=[col_spec(0), col_spec(1), col_spec(2), col_spec(3), d_rec_col, head_col,
                  chunk_spec, chunk_spec, vec_spec,
                  _spec((1, HGRN_DIM), lambda h, rb: (0, 0)),
                  _spec(stack.shape, lambda h, rb: (0, 0)), _spec(stack_t.shape, lambda h, rb: (0, 0))],
        out_specs=[head_col] * 4 + [vec_spec, vec_spec],
        out_shape=[jax.ShapeDtypeStruct((SEQ, HGRN_WIDTH), BF16)] * 4
                  + [jax.ShapeDtypeStruct((HGRN_HEADS, 1, HGRN_DIM), F32)] * 2,
        scratch_shapes=[pltpu.VMEM((HGRN_PAIR, CHUNK, CHUNK), F32)],
        compiler_params=_params(2),
    )(proj, proj, proj, proj, d_rec, o_pre, states, scores, lb, gain, stack, stack_t)
    return outs


ANY_SPEC = pl.BlockSpec(memory_space=pl.ANY)


def _my_place():
    return lax.axis_index("x"), lax.axis_index("y"), lax.axis_index("c")


def _other_chips(x, y):
    return [(1 - x, y), (x, 1 - y), (1 - x, 1 - y)]


def _remote(src, dst, send_sem, recv_sem, device):
    return pltpu.make_async_remote_copy(src_ref=src, dst_ref=dst, send_sem=send_sem, recv_sem=recv_sem,
                                        device_id=device, device_id_type=MESH)


def _staged_copies(srcs, dsts, stage, sems):
    loads = [pltpu.make_async_copy(srcs[i], stage[i], sems.at[i]) for i in range(len(srcs))]
    for cp in loads:
        cp.start()
    stores = []
    for i, cp in enumerate(loads):
        cp.wait()
        stores.append(pltpu.make_async_copy(stage[i], dsts[i], sems.at[i]))
        stores[-1].start()
    return stores


def _gather_weights(name, shards):
    n = len(shards)

    def body(*refs):
        ins, outs = refs[:n], refs[n:2 * n]
        ici_send, ici_recv, d2d_send, d2d_recv, local_sems = refs[2 * n:2 * n + 5]
        stage = refs[2 * n + 5:]
        x, y, c = _my_place()
        me = 2 * x + y
        chips = _other_chips(x, y)

        def half(i, which):
            h = ins[i].shape[0] // 2
            return pl.ds(which * h, h)

        sends = []
        for i in range(n):
            for j, (px, py) in enumerate(chips):
                sends.append(_remote(ins[i].at[half(i, c), :], outs[i].at[me, half(i, c), :],
                                     ici_send.at[3 * i + j], ici_recv.at[3 * i + j], (px, py, c)))
        for cp in sends:
            cp.start()
        local = _staged_copies(ins, [outs[i].at[me] for i in range(n)], stage, local_sems)
        for i in range(n):
            for j, (px, py) in enumerate(chips):
                landed = outs[i].at[2 * px + py, half(i, c), :]
                _remote(landed, landed, ici_send.at[3 * i + j], ici_recv.at[3 * i + j], (px, py, c)).wait_recv()
                forward = _remote(landed, landed, d2d_send.at[3 * i + j], d2d_recv.at[3 * i + j], (x, y, 1 - c))
                forward.start()
                sends.append(forward)
        for i in range(n):
            for j, (px, py) in enumerate(chips):
                other = outs[i].at[2 * px + py, half(i, 1 - c), :]
                _remote(other, other, d2d_send.at[3 * i + j], d2d_recv.at[3 * i + j], (x, y, 1 - c)).wait_recv()
        for cp in sends:
            cp.wait_send()
        for cp in local:
            cp.wait()

    return pl.pallas_call(
        body, name=name, in_specs=[ANY_SPEC] * n, out_specs=[ANY_SPEC] * n,
        out_shape=[jax.ShapeDtypeStruct((N_CHIPS,) + s.shape, s.dtype) for s in shards],
        scratch_shapes=([pltpu.SemaphoreType.DMA((3 * n,))] * 4 + [pltpu.SemaphoreType.DMA((n,))]
                        + [pltpu.VMEM(s.shape, s.dtype) for s in shards]),
        compiler_params=pltpu.CompilerParams(vmem_limit_bytes=VMEM_LIMIT),
    )(*shards)


HBM_SPEC = pl.BlockSpec(memory_space=pltpu.HBM)
SEM_SPEC = pl.BlockSpec(memory_space=pltpu.SEMAPHORE)
SPLIT_PARAMS = pltpu.CompilerParams(has_side_effects=pltpu.SideEffectType.DATAFLOW_SIDE_EFFECTING)
N_PEERS = {"gather": N_CHIPS - 1, "scatter": N_DEV - 1}


def _split_copies(ins, lands, send_sems, recv_sems, kind):
    x, y, c = _my_place()
    pairs = []
    for i in range(len(ins)):
        if kind == "gather":
            me = 2 * x + y
            for j, (px, py) in enumerate(_other_chips(x, y)):
                sems = (send_sems.at[3 * i + j], recv_sems.at[3 * i + j], (px, py, c))
                pairs.append((_remote(ins[i], lands[i].at[me], *sems),
                              _remote(ins[i], lands[i].at[2 * px + py], *sems)))
        else:
            me = 4 * x + 2 * y + c
            h = ins[i].shape[1] // 2
            for k in range(1, N_DEV):
                px, py, pc = (x + (k >> 2)) % 2, (y + ((k >> 1) & 1)) % 2, (c + (k & 1)) % 2
                src = ins[i].at[2 * px + py, pl.ds(pc * h, h), :]
                sems = (send_sems.at[7 * i + k - 1], recv_sems.at[7 * i + k - 1], (px, py, pc))
                pairs.append((_remote(src, lands[i].at[me], *sems),
                              _remote(src, lands[i].at[4 * px + 2 * py + pc], *sems)))
    return pairs


def _exchange_start(name, srcs, lands, kind, after=None):
    n = len(srcs)
    n_sems = N_PEERS[kind] * n
    extra = [] if after is None else [after]

    def body(*refs):
        ins, land_refs = refs[:n], refs[n:2 * n]
        send_sems, recv_sems = refs[2 * n + len(extra):2 * n + len(extra) + 2]
        token = refs[-1]
        for send, _ in _split_copies(ins, land_refs, send_sems, recv_sems, kind):
            send.start()
        token[...] = jnp.zeros_like(token)

    arrays = list(srcs) + list(lands)
    outs = pl.pallas_call(
        body, name=name,
        in_specs=[HBM_SPEC] * (2 * n) + [ANY_SPEC] * len(extra),
        out_shape=([pltpu.SemaphoreType.DMA((n_sems,))] * 2 + [pltpu.HBM(a.shape, a.dtype) for a in arrays]
                   + [jax.ShapeDtypeStruct((8, 128), F32)]),
        out_specs=[SEM_SPEC] * 2 + [HBM_SPEC] * (2 * n) + [pl.BlockSpec(memory_space=pltpu.VMEM)],
        input_output_aliases={i: 2 + i for i in range(2 * n)},
        compiler_params=SPLIT_PARAMS,
    )(*[pltpu.with_memory_space_constraint(a, pltpu.HBM) for a in arrays], *extra)
    return outs[:2], outs[2:2 + 2 * n], outs[-1]


def _exchange_wait(name, sems, passed, kind, after):
    n = len(passed) // 2

    def body(*refs):
        ins, land_refs = refs[:n], refs[n:2 * n]
        send_sems, recv_sems = refs[2 * n:2 * n + 2]
        for send, arrive in _split_copies(ins, land_refs, send_sems, recv_sems, kind):
            send.wait_send()
            arrive.wait_recv()

    outs = pl.pallas_call(
        body, name=name,
        in_specs=[HBM_SPEC] * (2 * n) + [SEM_SPEC] * 2 + [ANY_SPEC],
        out_shape=[pltpu.HBM(a.shape, a.dtype) for a in passed],
        out_specs=[HBM_SPEC] * (2 * n),
        input_output_aliases={i: i for i in range(2 * n)},
        compiler_params=SPLIT_PARAMS,
    )(*passed, *sems, after)
    return outs[:n], outs[n:]


def _own_slot(name, own, me):
    r, cc = own.shape
    th = min(r, 512)

    def body(me_ref, x_ref, o_ref):
        del me_ref
        o_ref[...] = x_ref[...]

    grid_spec = pltpu.PrefetchScalarGridSpec(
        num_scalar_prefetch=1, grid=(r // th,),
        in_specs=[pl.BlockSpec((th, cc), lambda i, me_ref: (i, 0))],
        out_specs=pl.BlockSpec((None, th, cc), lambda i, me_ref: (me_ref[0], i, 0)))
    return pl.pallas_call(
        body, name=name, grid_spec=grid_spec,
        out_shape=jax.ShapeDtypeStruct((N_CHIPS, r, cc), own.dtype), compiler_params=_params(1),
    )(me, own)


def _sum_devices(name, landed, own, place):
    n_dev, h, cc = landed.shape
    th = min(h, 256)
    nb = h // th

    def body(place_ref, l_ref, own_ref, o_ref):
        total = None
        for d in range(n_dev):
            piece = jnp.where(place_ref[0] == d, own_ref[...], l_ref[d]).astype(F32)
            total = piece if total is None else total + piece
        o_ref[...] = total

    grid_spec = pltpu.PrefetchScalarGridSpec(
        num_scalar_prefetch=1, grid=(nb,),
        in_specs=[pl.BlockSpec((n_dev, th, cc), lambda i, p: (0, i, 0)),
                  pl.BlockSpec((None, th, cc), lambda i, p: (p[1], p[2] * nb + i, 0))],
        out_specs=pl.BlockSpec((th, cc), lambda i, p: (i, 0)))
    return pl.pallas_call(
        body, name=name, grid_spec=grid_spec,
        out_shape=jax.ShapeDtypeStruct((h, cc), F32), compiler_params=_params(1),
    )(place, landed, own)


def _share_halves(name, halves):
    flat = [t for per_weight in halves for t in per_weight]
    n = len(flat)
    n_w = len(halves)

    def body(*refs):
        ins, outs = refs[:n], refs[n:n + n_w]
        send_sems, recv_sems, local_sems = refs[n + n_w:n + n_w + 3]
        stage = refs[n + n_w + 3:]
        x, y, c = _my_place()
        sends, own = [], []
        for i in range(n):
            w, l = divmod(i, DEPTH)
            h = ins[i].shape[0]
            own.append(outs[w].at[l, pl.ds(c * h, h), :])
            sends.append(_remote(ins[i], own[i], send_sems.at[i], recv_sems.at[i], (x, y, 1 - c)))
        for cp in sends:
            cp.start()
        local = _staged_copies(ins, own, stage, local_sems)
        for i in range(n):
            w, l = divmod(i, DEPTH)
            h = ins[i].shape[0]
            _remote(ins[i], outs[w].at[l, pl.ds((1 - c) * h, h), :], send_sems.at[i], recv_sems.at[i],
                    (x, y, 1 - c)).wait_recv()
        for cp in sends:
            cp.wait_send()
        for cp in local:
            cp.wait()

    return pl.pallas_call(
        body, name=name, in_specs=[ANY_SPEC] * n, out_specs=[ANY_SPEC] * n_w,
        out_shape=[jax.ShapeDtypeStruct((DEPTH, 2 * per_weight[0].shape[0], per_weight[0].shape[1]), F32)
                   for per_weight in halves],
        scratch_shapes=([pltpu.SemaphoreType.DMA((n,))] * 3 + [pltpu.VMEM(t.shape, t.dtype) for t in flat]),
        compiler_params=pltpu.CompilerParams(vmem_limit_bytes=VMEM_LIMIT),
    )(*flat)


def _all_reduce_small(pack, after):
    def body(p_ref, after_ref, o_ref, recv, send_sems, recv_sems):
        del after_ref
        x, y, c = _my_place()
        me = 4 * x + 2 * y + c
        recv[me] = p_ref[...]
        peers = []
        for k in range(1, N_DEV):
            px, py, pc = (x + (k >> 2)) % 2, (y + ((k >> 1) & 1)) % 2, (c + (k & 1)) % 2
            peers.append((px, py, pc))
        sends = [_remote(p_ref, recv.at[me], send_sems.at[k], recv_sems.at[k], peer)
                 for k, peer in enumerate(peers)]
        for cp in sends:
            cp.start()
        for k, (px, py, pc) in enumerate(peers):
            _remote(p_ref, recv.at[4 * px + 2 * py + pc], send_sems.at[k], recv_sems.at[k],
                    (px, py, pc)).wait_recv()
        for cp in sends:
            cp.wait_send()
        total = recv[0]
        for d in range(1, N_DEV):
            total = total + recv[d]
        o_ref[...] = total

    vmem = pl.BlockSpec(memory_space=pltpu.VMEM)
    return pl.pallas_call(
        body, name="all_reduce_small", in_specs=[vmem, ANY_SPEC], out_specs=vmem,
        out_shape=jax.ShapeDtypeStruct(pack.shape, F32),
        scratch_shapes=[pltpu.VMEM((N_DEV,) + pack.shape, F32),
                        pltpu.SemaphoreType.DMA((N_DEV - 1,)), pltpu.SemaphoreType.DMA((N_DEV - 1,))],
    )(pack, after)


def _adamw(name, w, g, m, v):
    r, cc = w.shape
    th = min(r, 256)

    def body(w_ref, g_ref, m_ref, v_ref, d_ref, m_out, v_out):
        gv = g_ref[...]
        m2 = ADAM_B1 * m_ref[...] + (1.0 - ADAM_B1) * gv
        v2 = ADAM_B2 * v_ref[...] + (1.0 - ADAM_B2) * (gv * gv)
        m_hat = m2 / (1.0 - ADAM_B1 ** ADAM_STEP)
        v_hat = v2 / (1.0 - ADAM_B2 ** ADAM_STEP)
        d_ref[...] = -ADAM_LR * (m_hat / (jnp.sqrt(v_hat) + ADAM_EPS) + ADAM_WD * w_ref[...])
        m_out[...] = m2
        v_out[...] = v2

    tile = _spec((th, cc), lambda i: (i, 0))
    return pl.pallas_call(
        body, name=name, grid=(r // th,), in_specs=[tile] * 4, out_specs=[tile] * 3,
        out_shape=[jax.ShapeDtypeStruct((r, cc), F32)] * 3, compiler_params=_params(1),
    )(w, g, m, v)


def _lower_bounds(lb_logits):
    p = jax.nn.softmax(lb_logits.astype(F32), axis=0)
    return jnp.cumsum(p, axis=0) - p[0]


def _layer_forward(l, stream, small, weights, consts, next_gain=None, loss=None, after=None):
    win, rest = weights
    cos_t, sin_t, stack, _, _ = consts
    tm = MM_TILE
    x_in, h, h_t = stream
    saved = {"x_in": x_in}

    proj = _mm_pieces(f"proj{l}", h, win, False, tm, after=after)
    saved.update(h_t=h_t, proj=proj)

    qkv = _attn_prep(f"attn_prep{l}", proj, cos_t, sin_t)
    outs, lses = [], []
    for p, d in enumerate(DILATIONS):
        o, lse = _attn_fwd(f"attn_fwd{l}_{d}", *qkv[p], SEQ // d // SPAN)
        outs.append(o)
        lses.append(lse)
    mixed, mixed_t, attn, lse = _attn_merge(f"attn_merge{l}", outs, lses, small["attn_out_gain"][l][None, :])
    saved.update(qkv=qkv, attn=attn, lse=lse)

    lb3 = small["lower"][l].reshape(HGRN_HEADS, 1, HGRN_DIM)
    mixed, mixed_t, o_pre, states, scores = _hgrn_fwd(f"hgrn_fwd{l}", proj, lb3, small["hgrn_out_gain"][l][None, :],
                                                      stack, mixed, mixed_t)
    wo, wu, wd = rest(mixed)
    saved.update(mixed_t=mixed_t, o_pre=o_pre, states=states, scores=scores, lb3=lb3, weights=(win, wo, wu, wd))

    x_mid, h2, h2_t = _mm_accum(f"out_proj{l}", mixed, wo, False, tm, x_in, next_gain=small["norm_mlp"][l][None, :])
    saved["x_mid"] = x_mid

    a, relu_u, a_t = _mm_pieces(f"up{l}", h2, wu, False, tm, epilogue="relu2")
    new_stream = tuple(_mm_accum(f"down{l}", a, wd, False, tm, x_mid, next_gain=next_gain, loss=loss))
    saved.update(h2_t=h2_t, relu_u=relu_u, a_t=a_t)
    return new_stream, saved


def _layer_backward(l, dx, saved, small, consts, on_grads, after=None):
    win, wo, wu, wd = saved["weights"]
    cos_t, sin_t, stack, stack_t, head_sum = consts
    tm = MM_TILE

    dx, dx_b = dx
    du = _mm_pieces(f"d_u{l}", dx_b, wd, True, tm, epilogue="relu2_grad", extra=saved["relu_u"], after=after)
    d_wd = _mm_dw(f"d_wdown{l}", saved["a_t"], dx_b, False, tm)
    dxm, dxm_b, dg_mlp = _mm_accum(f"d_h2_{l}", du, wu, True, tm, dx,
                                   norm=(saved["x_mid"], small["norm_mlp"][l][None, :]))
    d_wu = _mm_dw(f"d_wup{l}", saved["h2_t"], du, True, tm)
    d_wo = _mm_dw(f"d_wout{l}", saved["mixed_t"], dxm_b, False, tm)
    after_early = on_grads(l, "early", (d_wu, d_wd, d_wo))

    d_mixed = _mm_pieces(f"d_mixed{l}", dxm_b, wo, True, tm, after=after_early)
    d_rec = d_mixed

    d_out, delta, lses, dg_attn = _attn_bwd_prep(f"attn_bwd_prep{l}", d_mixed, saved["attn"], saved["lse"],
                                                 small["attn_out_gain"][l][None, :], head_sum)
    grads = []
    for p, d in enumerate(DILATIONS):
        grads.append(_attn_bwd(f"attn_bwd{l}_{d}", *saved["qkv"][p], d_out[p], delta[p], lses[p],
                               SEQ // d // SPAN))
    dp_attn = _attn_bwd_post(f"attn_bwd_post{l}", grads, cos_t, sin_t)

    dq_h, df_h, di_h, dg_h, d_lower, dg_hgrn = _hgrn_bwd(
        f"hgrn_bwd{l}", saved["proj"], d_rec, saved["o_pre"], saved["states"], saved["scores"],
        saved["lb3"], small["hgrn_out_gain"][l][None, :], stack, stack_t)
    dproj = [dp_attn, dq_h, df_h, di_h, dg_h]

    d_win = _mm_dw(f"d_win{l}", saved["h_t"], dproj, True, tm)
    after_last = on_grads(l, "last", (d_win,))
    dx_in, dx_in_b, dg_mix = _mm_accum(f"d_h{l}", dproj, win, True, tm, dxm,
                                       norm=(saved["x_in"], small["norm_mix"][l][None, :]), after=after_last)

    small_grads = {"norm_mix": dg_mix[0], "attn_out_gain": dg_attn[0],
                   "lower": d_lower.reshape(HGRN_WIDTH),
                   "hgrn_out_gain": jnp.sum(dg_hgrn, axis=0).reshape(HGRN_DIM), "norm_mlp": dg_mlp[0]}
    return (dx_in, dx_in_b), after_last, small_grads


def _local_step(xs, target, small, get_weights, on_grads):
    consts = _rope_tables() + _hgrn_consts() + (_head_sum_matrix(),)
    stream = (xs,) + tuple(_rms_fwd("norm_mix0", xs, small["norm_mix"][0][None, :]))
    saved = []
    for l in range(DEPTH):
        w, after = get_weights(l, stream[0])
        if l + 1 < DEPTH:
            stream, s = _layer_forward(l, stream, small, w, consts, next_gain=small["norm_mix"][l + 1][None, :],
                                       after=after)
        else:
            stream, s = _layer_forward(l, stream, small, w, consts, loss=(small["norm_final"][None, :], target),
                                       after=after)
        saved.append(s)
    dx_f, dx_b, dg_final, loss = stream
    dx = (dx_f, dx_b)
    small_grads = [None] * DEPTH
    after = None
    for l in reversed(range(DEPTH)):
        dx, after, small_grads[l] = _layer_backward(l, dx, saved[l], small, consts, on_grads, after=after)
    return loss, dx[0], dg_final[0], small_grads


def _pack_small(norm_mix, attn_out_gain, lb, hgrn_out_gain, norm_mlp, norm_final, last_row):
    rows = [norm_mix, attn_out_gain.reshape(1, D_MODEL), lb.reshape(1, D_MODEL),
            jnp.pad(hgrn_out_gain.reshape(1, DEPTH * HGRN_DIM), ((0, 0), (0, D_MODEL - DEPTH * HGRN_DIM))),
            norm_mlp, norm_final.reshape(1, D_MODEL), last_row.reshape(1, D_MODEL)]
    pack = jnp.concatenate(rows, axis=0)
    return jnp.pad(pack, ((0, PACK_ROWS - pack.shape[0]), (0, 0)))


def _unpack_small(pack):
    return (pack[0:2], pack[2].reshape(DEPTH, ATTN_WIDTH), pack[3].reshape(DEPTH, HGRN_WIDTH),
            pack[4, :DEPTH * HGRN_DIM].reshape(DEPTH, HGRN_DIM), pack[5:7], pack[7], pack[8])


def kernel(x, norm_mix, w_in, attn_out_gain, hgrn_lb_logits, hgrn_out_gain, w_out, norm_mlp, w_up, w_down, norm_final, loss_target, m_norm_mix, m_w_in, m_attn_out_gain, m_hgrn_lb_logits, m_hgrn_out_gain, m_w_out, m_norm_mlp, m_w_up, m_w_down, m_norm_final, v_norm_mix, v_w_in, v_attn_out_gain, v_hgrn_lb_logits, v_hgrn_out_gain, v_w_out, v_norm_mlp, v_w_up, v_w_down, v_norm_final):
    lower, lower_vjp = jax.vjp(_lower_bounds, hgrn_lb_logits)
    small = {"norm_mix": norm_mix, "attn_out_gain": attn_out_gain, "lower": lower,
             "hgrn_out_gain": hgrn_out_gain, "norm_mlp": norm_mlp, "norm_final": norm_final}
    big_w = (w_in, w_out, w_up, w_down)

    x_pos, y_pos, core = lax.axis_index("x"), lax.axis_index("y"), lax.axis_index("c")
    me = (2 * x_pos + y_pos).astype(jnp.int32).reshape(1)
    place = jnp.stack([4 * x_pos + 2 * y_pos + core, 2 * x_pos + y_pos, core]).astype(jnp.int32)
    shards = [[w[l].astype(BF16) for w in big_w] for l in range(DEPTH)]
    in_flight = {}

    def start_gather(name, some, after):
        lands = [_own_slot(f"own_{name}_{i}", s, me) for i, s in enumerate(some)]
        sems, passed, token = _exchange_start(f"start_{name}", some, lands, "gather", after)
        in_flight[name] = (sems, passed)
        return token

    def finish_gather(name, after):
        return _exchange_wait(f"wait_{name}", *in_flight.pop(name), "gather", after)[1]

    def get_weights(l, stream):
        if l == 0:
            (win,) = _gather_weights("gather_w_in0", shards[0][:1])
            token = start_gather("gather_rest0", shards[0][1:], win)
            token = start_gather("gather_w_in1", shards[1][:1], token)
            token = start_gather("gather_rest1", shards[1][1:], token)
            return (win, lambda after: finish_gather("gather_rest0", after)), token
        (win,) = finish_gather("gather_w_in1", stream)
        return (win, lambda after: finish_gather("gather_rest1", after)), None

    reduced = {}

    def start_exchange(name, grads):
        srcs, lands = [g for g, _ in grads], [land for _, land in grads]
        sems, passed, token = _exchange_start(f"start_{name}", srcs, lands, "scatter")
        in_flight[name] = (sems, passed)
        return token

    def finish_exchange(name, after):
        own, landed = _exchange_wait(f"wait_{name}", *in_flight.pop(name), "scatter", after)
        return [_sum_devices(f"sum_{name}_{i}", p, g, place) for i, (p, g) in enumerate(zip(landed, own))]

    def on_grads(l, group, grads):
        token = start_exchange(f"{group}{l}", grads)
        if (l, group) == (0, "early"):
            reduced[(1, "early")] = finish_exchange("early1", token)
            reduced[(1, "last")] = finish_exchange("last1", token)
        if (l, group) == (0, "last"):
            reduced[(0, "early")] = finish_exchange("early0", token)
        return token

    loss, dx, dg_final, sg = _local_step(x[0], loss_target[0], small, get_weights, on_grads)

    big_m = (m_w_in, m_w_out, m_w_up, m_w_down)
    big_v = (v_w_in, v_w_out, v_w_up, v_w_down)
    names = ("w_in", "w_out", "w_up", "w_down")
    big_g, big_delta, big_new_m, big_new_v = [None] * 4, [None] * 4, [None] * 4, [None] * 4

    def finish_weights(group, which):
        whole = _share_halves(f"share_{group}", [[reduced[(l, group)][i] for l in range(DEPTH)]
                                                 for i in range(len(which))])
        for i, w in enumerate(which):
            shape = big_w[w].shape
            flat = lambda arr: arr.reshape(shape[0] * shape[1], shape[2])
            d, m2, v2 = _adamw(f"adamw_{names[w]}", flat(big_w[w]), flat(whole[i]), flat(big_m[w]), flat(big_v[w]))
            big_g[w], big_delta[w] = whole[i], d.reshape(shape)
            big_new_m[w], big_new_v[w] = m2.reshape(shape), v2.reshape(shape)

    finish_weights("early", (2, 3, 1))
    reduced[(0, "last")] = finish_exchange("last0", big_delta[3])
    finish_weights("last", (0,))

    stack2 = lambda key: jnp.stack([sg[l][key] for l in range(DEPTH)])
    pack = _pack_small(stack2("norm_mix"), stack2("attn_out_gain"), stack2("lower"), stack2("hgrn_out_gain"),
                       stack2("norm_mlp"), dg_final, jnp.broadcast_to(loss[0, 0], (D_MODEL,)))
    g_mix, g_attn, g_lower, g_hgrn, g_mlp, g_final, loss_row = _unpack_small(_all_reduce_small(pack, big_delta[0]))
    (g_logits,) = lower_vjp(g_lower)

    zeros_row = jnp.zeros((D_MODEL,), F32)
    small_w = (norm_mix, attn_out_gain, hgrn_lb_logits, hgrn_out_gain, norm_mlp, norm_final)
    small_m = (m_norm_mix, m_attn_out_gain, m_hgrn_lb_logits, m_hgrn_out_gain, m_norm_mlp, m_norm_final)
    small_v = (v_norm_mix, v_attn_out_gain, v_hgrn_lb_logits, v_hgrn_out_gain, v_norm_mlp, v_norm_final)
    small_g = (g_mix, g_attn, g_logits, g_hgrn, g_mlp, g_final)
    packs = [_pack_small(*t, zeros_row) for t in (small_w, small_g, small_m, small_v)]
    small_delta, small_new_m, small_new_v = [_unpack_small(p)[:6] for p in _adamw("adamw_small", *packs)]

    def ordered(small6, big4):
        mix, attn, lbl, hg, mlp, fin = small6
        return (mix, big4[0], attn, lbl, hg, big4[1], mlp, big4[2], big4[3], fin)

    return ((loss_row[0], dx[None]) + ordered(small_g, big_g) + ordered(small_delta, big_delta)
            + ordered(small_new_m, big_new_m) + ordered(small_new_v, big_new_v))
```

```python
import numpy as np
import jax
import jax.numpy as jnp
from jax import lax
from jax.experimental import pallas as pl
from jax.experimental.pallas import tpu as pltpu

F32 = jnp.float32
BF16 = jnp.bfloat16
MESH = pl.DeviceIdType.MESH

SEQ = 4096
D_MODEL = 1024
DEPTH = 2
ATTN_WIDTH = 512
HEAD_DIM = 64
HGRN_HEADS = 4
HGRN_DIM = 128
HGRN_WIDTH = 512
IN_W = 3584
MLP_HIDDEN = 4096
N_CHIPS = 4
N_DEV = 8
DILATIONS = (1, 4, 16)
SPAN = 128
ROPE_THETA = 10000.0
NORM_EPS = 1e-6
MASK_VALUE = -1e30
CHUNK = 128
ROW_TILE = 512
PERMUTE_TILE = 1024
MM_TILE = 512
VMEM_LIMIT = 52 * 1024 * 1024

ADAM_LR = 0.001
ADAM_B1 = 0.9
ADAM_B2 = 0.999
ADAM_EPS = 1e-08
ADAM_WD = 0.01
ADAM_STEP = 10

PACK_ROWS = 16


def _params(n_axes):
    return pltpu.CompilerParams(dimension_semantics=("arbitrary",) * n_axes,
                                vmem_limit_bytes=VMEM_LIMIT)


def _dot(a, b):
    return jnp.dot(a.astype(BF16), b.astype(BF16), preferred_element_type=F32)


def _dot_nt(a, b):
    return lax.dot_general(a.astype(BF16), b.astype(BF16), (((1,), (1,)), ((), ())),
                           preferred_element_type=F32)


def _dot_tn(a, b):
    return lax.dot_general(a.astype(BF16), b.astype(BF16), (((0,), (0,)), ((), ())),
                           preferred_element_type=F32)


def _sigmoid(x):
    return 1.0 / (1.0 + jnp.exp(-x))


def _spec(shape, index_map):
    return pl.BlockSpec(shape, index_map)


def _mm_pieces(name, a, w, nt, tm, epilogue="none", extra=None, after=None):
    s = a.shape[0]
    pw = w.shape[1] if nt else w.shape[2]
    width = N_CHIPS * pw

    def body(a_ref, w_ref, *rest):
        e_ref = rest[0] if extra is not None else None
        outs = rest[-3:] if epilogue == "relu2" else rest[-1:]
        av = a_ref[...].astype(BF16)
        for j in range(N_CHIPS):
            cols = slice(j * pw, (j + 1) * pw)
            r = _dot_nt(av, w_ref[j]) if nt else _dot(av, w_ref[j])
            if epilogue == "relu2":
                relu = jnp.maximum(r, 0.0)
                r = relu * relu
                outs[1][:, cols] = relu.astype(BF16)
                outs[2][cols, :] = r.T.astype(BF16)
            elif epilogue == "relu2_grad":
                r = r * (2.0 * e_ref[:, cols].astype(F32))
            outs[0][:, cols] = r.astype(outs[0].dtype)

    row = lambda width_: _spec((tm, width_), lambda i: (i, 0))
    in_specs = [row(a.shape[1]), _spec(w.shape, lambda i: (0, 0, 0))]
    args = [a, w]
    if extra is not None:
        in_specs.append(row(width))
        args.append(extra)
    if after is not None:
        in_specs.append(pl.BlockSpec(memory_space=pl.ANY))
        args.append(after)
    if epilogue == "relu2":
        out_specs = [row(width), row(width), _spec((width, tm), lambda i: (0, i))]
        out_shape = [jax.ShapeDtypeStruct((s, width), BF16)] * 2 + [jax.ShapeDtypeStruct((width, s), BF16)]
    else:
        out_specs = row(width)
        out_shape = jax.ShapeDtypeStruct((s, width), BF16 if epilogue == "relu2_grad" else F32)
    return pl.pallas_call(body, name=name, grid=(s // tm,), in_specs=in_specs, out_specs=out_specs,
                          out_shape=out_shape, compiler_params=_params(1))(*args)


def _mm_accum(name, a, w, nt, tm, resid, norm=None, after=None, next_gain=None, loss=None):
    pieces = list(a) if isinstance(a, (list, tuple)) else [a]
    n_a = len(pieces)
    s = pieces[0].shape[0]
    pk = w.shape[2] if nt else w.shape[1]
    d = w.shape[1] if nt else w.shape[2]

    def body(*refs):
        a_refs, w_ref, resid_ref, rest = refs[:n_a], refs[n_a], refs[n_a + 1], refs[n_a + 2:]
        av = a_refs[0][...] if n_a == 1 else jnp.concatenate([ref[...] for ref in a_refs], axis=1)
        r = None
        for j in range(N_CHIPS):
            piece = av[:, j * pk:(j + 1) * pk].astype(BF16)
            term = _dot_nt(piece, w_ref[j]) if nt else _dot(piece, w_ref[j])
            r = term if r is None else r + term
        if loss is not None:
            g_ref, t_ref = rest[:2]
            dx_ref, dxb_ref, dg_ref, loss_ref, acc = rest[-5:]
            i = pl.program_id(0)

            @pl.when(i == 0)
            def _():
                dg_ref[...] = jnp.zeros_like(dg_ref)
                acc[...] = jnp.zeros_like(acc)

            xv = r + resid_ref[...]
            g = g_ref[...]
            rs = lax.rsqrt(jnp.mean(xv * xv, axis=1, keepdims=True) + NORM_EPS)
            xhat = xv * rs
            err = xhat * g - t_ref[...]
            acc[...] += jnp.sum(err * err, axis=0, keepdims=True)
            dy = err * (1.0 / d)
            dyg = dy * g
            dx = rs * (dyg - xhat * jnp.mean(dyg * xhat, axis=1, keepdims=True))
            dx_ref[...] = dx
            dxb_ref[...] = dx.astype(BF16)
            dg_ref[...] += jnp.sum(dy * xhat, axis=0, keepdims=True)

            @pl.when(i == s // tm - 1)
            def _():
                total = jnp.sum(acc[...], axis=1, keepdims=True) * (0.5 / d)
                loss_ref[...] = jnp.broadcast_to(total, loss_ref.shape)

            return
        if norm is None and next_gain is None:
            rest[-1][...] = r + resid_ref[...]
            return
        if norm is None:
            g_ref = rest[0]
            x_out, h_out, ht_out = rest[-3:]
            xv = r + resid_ref[...]
            x_out[...] = xv
            h = xv * lax.rsqrt(jnp.mean(xv * xv, axis=1, keepdims=True) + NORM_EPS) * g_ref[...]
            h_out[...] = h.astype(BF16)
            ht_out[...] = h.T.astype(BF16)
            return
        x_ref, g_ref = rest[:2]
        dx_ref, dxb_ref, dg_ref = rest[-3:]

        @pl.when(pl.program_id(0) == 0)
        def _():
            dg_ref[...] = jnp.zeros_like(dg_ref)

        xv = x_ref[...]
        rs = lax.rsqrt(jnp.mean(xv * xv, axis=1, keepdims=True) + NORM_EPS)
        xhat = xv * rs
        rg = r * g_ref[...]
        dx = resid_ref[...] + rs * (rg - xhat * jnp.mean(rg * xhat, axis=1, keepdims=True))
        dx_ref[...] = dx
        dxb_ref[...] = dx.astype(BF16)
        dg_ref[...] += jnp.sum(r * xhat, axis=0, keepdims=True)

    row = lambda width: _spec((tm, width), lambda i: (i, 0))
    in_specs = [row(p.shape[1]) for p in pieces] + [_spec(w.shape, lambda i: (0, 0, 0)), row(d)]
    args = pieces + [w, resid]
    scratch = []
    if loss is not None:
        in_specs += [_spec((1, d), lambda i: (0, 0)), row(d)]
        args += list(loss)
        out_specs = [row(d), row(d), _spec((1, d), lambda i: (0, 0)), _spec((1, 128), lambda i: (0, 0))]
        out_shape = [jax.ShapeDtypeStruct((s, d), F32), jax.ShapeDtypeStruct((s, d), BF16),
                     jax.ShapeDtypeStruct((1, d), F32), jax.ShapeDtypeStruct((1, 128), F32)]
        scratch = [pltpu.VMEM((1, d), F32)]
    elif norm is None and next_gain is None:
        out_specs, out_shape = row(d), jax.ShapeDtypeStruct((s, d), F32)
    elif norm is None:
        in_specs.append(_spec((1, d), lambda i: (0, 0)))
        args.append(next_gain)
        out_specs = [row(d), row(d), _spec((d, tm), lambda i: (0, i))]
        out_shape = [jax.ShapeDtypeStruct((s, d), F32), jax.ShapeDtypeStruct((s, d), BF16),
                     jax.ShapeDtypeStruct((d, s), BF16)]
    else:
        in_specs += [row(d), _spec((1, d), lambda i: (0, 0))]
        args += list(norm)
        out_specs = [row(d), row(d), _spec((1, d), lambda i: (0, 0))]
        out_shape = [jax.ShapeDtypeStruct((s, d), F32), jax.ShapeDtypeStruct((s, d), BF16),
                     jax.ShapeDtypeStruct((1, d), F32)]
    if after is not None:
        in_specs.append(pl.BlockSpec(memory_space=pl.ANY))
        args.append(after)
    return pl.pallas_call(body, name=name, grid=(s // tm,), in_specs=in_specs, out_specs=out_specs,
                          out_shape=out_shape, scratch_shapes=scratch, compiler_params=_params(1))(*args)


def _mm_dw(name, a_t, b, by_cols, tk):
    pieces = list(b) if isinstance(b, (list, tuple)) else [b]
    n_b = len(pieces)
    m, s = a_t.shape
    n = sum(p.shape[1] for p in pieces)
    shape = (N_CHIPS, m, n // N_CHIPS) if by_cols else (N_CHIPS, m // N_CHIPS, n)
    n_steps = s // tk

    def body(a_ref, *rest):
        b_refs, o_ref, acc = rest[:n_b], rest[n_b], rest[-1]

        @pl.when(pl.program_id(0) == 0)
        def _():
            acc[...] = jnp.zeros_like(acc)

        bv = b_refs[0][...] if n_b == 1 else jnp.concatenate([ref[...] for ref in b_refs], axis=1)
        for j in range(N_CHIPS):
            if by_cols:
                acc[j] += _dot(a_ref[...], bv[:, j * shape[2]:(j + 1) * shape[2]])
            else:
                acc[j] += _dot(a_ref[j * shape[1]:(j + 1) * shape[1], :], bv)

        @pl.when(pl.program_id(0) == n_steps - 1)
        def _():
            o_ref[...] = acc[...].astype(BF16)

    return pl.pallas_call(
        body, name=name, grid=(n_steps,),
        in_specs=[_spec((m, tk), lambda k: (0, k))] + [_spec((tk, p.shape[1]), lambda k: (k, 0)) for p in pieces],
        out_specs=[_spec(shape, lambda k: (0, 0, 0)), ANY_SPEC],
        out_shape=[jax.ShapeDtypeStruct(shape, BF16),
                   jax.ShapeDtypeStruct((N_DEV, shape[1] // 2, shape[2]), BF16)],
        scratch_shapes=[pltpu.VMEM(shape, F32)],
        compiler_params=_params(1))(a_t, *pieces)


def _rms_fwd(name, x, gain):
    s, d = x.shape
    t = ROW_TILE

    def body(x_ref, g_ref, h_ref, ht_ref):
        xv = x_ref[...]
        r = lax.rsqrt(jnp.mean(xv * xv, axis=1, keepdims=True) + NORM_EPS)
        h = xv * r * g_ref[...]
        h_ref[...] = h.astype(BF16)
        ht_ref[...] = h.T.astype(BF16)

    return pl.pallas_call(
        body, name=name, grid=(s // t,),
        in_specs=[_spec((t, d), lambda i: (i, 0)), _spec((1, d), lambda i: (0, 0))],
        out_specs=[_spec((t, d), lambda i: (i, 0)), _spec((d, t), lambda i: (0, i))],
        out_shape=[jax.ShapeDtypeStruct((s, d), BF16), jax.ShapeDtypeStruct((d, s), BF16)],
        compiler_params=_params(1),
    )(x, gain)


def _rope_tables():
    half = HEAD_DIM // 2
    inv_freq = ROPE_THETA ** (-jnp.arange(half, dtype=F32) / half)
    ang = jnp.arange(SEQ, dtype=jnp.int32).astype(F32)[:, None] * inv_freq[None, :]
    cos, sin = jnp.cos(ang), jnp.sin(ang)
    cos_t = jnp.concatenate([cos, cos, cos, cos], axis=1)
    sin_t = jnp.concatenate([-sin, sin, -sin, sin], axis=1)
    return cos_t, sin_t


def _swap_halves(x):
    lane = lax.broadcasted_iota(jnp.int32, x.shape, 1)
    first = (lane % HEAD_DIM) < (HEAD_DIM // 2)
    return jnp.where(first, pltpu.roll(x, 128 - HEAD_DIM // 2, 1), pltpu.roll(x, HEAD_DIM // 2, 1))


def _permuted_specs(t, width):
    specs = [_spec((t, width), lambda i: (i, 0))]
    for d in DILATIONS[1:]:
        specs.append(_spec((d, t // d, width), lambda i: (0, i, 0)))
    return specs


def _permuted_shapes(width, dtype):
    shapes = [jax.ShapeDtypeStruct((SEQ, width), dtype)]
    for d in DILATIONS[1:]:
        shapes.append(jax.ShapeDtypeStruct((d, SEQ // d, width), dtype))
    return shapes


def _attn_prep(name, proj, cos_t, sin_t):
    t = PERMUTE_TILE
    w = ATTN_WIDTH

    def body(q_ref, k_ref, v_ref, cos_ref, sin_ref, *rest):
        outs, scr = rest[:9], rest[9]
        cosv, sinv = cos_ref[...], sin_ref[...]
        for a, (src, roped, scale) in enumerate(((q_ref, True, HEAD_DIM ** -0.5),
                                                 (k_ref, True, 1.0), (v_ref, False, 1.0))):
            o1, o4, o16 = outs[3 * a:3 * a + 3]
            for cb in range(w // 128):
                cols = slice(cb * 128, (cb + 1) * 128)
                val = src[:, cols]
                if roped:
                    val = (val * cosv + _swap_halves(val) * sinv) * scale
                scr[...] = val
                o1[:, cols] = val.astype(BF16)
                for o_ref, d in ((o4, 4), (o16, 16)):
                    for r in range(d):
                        o_ref[r, :, cols] = scr[pl.ds(r, t // d, stride=d), :].astype(BF16)

    out_specs = _permuted_specs(t, w) * 3
    out_shape = _permuted_shapes(w, BF16) * 3
    outs = pl.pallas_call(
        body, name=name, grid=(SEQ // t,),
        in_specs=[_spec((t, w), lambda i: (i, 0)), _spec((t, w), lambda i: (i, 1)),
                  _spec((t, w), lambda i: (i, 2)),
                  _spec((t, 128), lambda i: (i, 0)), _spec((t, 128), lambda i: (i, 0))],
        out_specs=out_specs, out_shape=out_shape,
        scratch_shapes=[pltpu.VMEM((t, 128), F32)],
        compiler_params=_params(1),
    )(proj, proj, proj, cos_t, sin_t)
    q, k, v = outs[0:3], outs[3:6], outs[6:9]
    flat = lambda arr: arr.reshape(SEQ, w)
    return [(flat(q[p]), flat(k[p]), flat(v[p])) for p in range(3)]


def _band_masks():
    row = lax.broadcasted_iota(jnp.int32, (2 * SPAN, 2 * SPAN), 0) % SPAN
    col = lax.broadcasted_iota(jnp.int32, (2 * SPAN, 2 * SPAN), 1)
    is_prev = col < SPAN
    band = (is_prev & (col >= row)) | (~is_prev & (col - SPAN <= row))
    head0 = lax.broadcasted_iota(jnp.int32, (SPAN, 128), 1) < HEAD_DIM
    return band, head0


def _stack_heads(x, head0):
    zero = jnp.zeros_like(x)
    return jnp.concatenate([jnp.where(head0, x, zero), jnp.where(head0, zero, x)], axis=0)


def _for_each_block(block, seg_blocks):
    for b in range(SEQ // SPAN):
        block(b, b % seg_blocks == 0)


def _attn_fwd(name, q, k, v, seg_blocks):
    def body(q_ref, k_ref, v_ref, o_ref, lse_ref):
        band, head0 = _band_masks()

        def block(b, first):
            cur = pl.ds(b * SPAN, SPAN)
            qs = _stack_heads(q_ref[cur, :], head0)
            if first:
                kcat, vcat, ok = k_ref[cur, :], v_ref[cur, :], band[:, SPAN:]
            else:
                both = pl.ds((b - 1) * SPAN, 2 * SPAN)
                kcat, vcat, ok = k_ref[both, :], v_ref[both, :], band
            s = jnp.where(ok, _dot_nt(qs, kcat), MASK_VALUE)
            m = jnp.max(s, axis=1, keepdims=True)
            p = jnp.exp(s - m)
            l = jnp.sum(p, axis=1, keepdims=True)
            pv = _dot(p, vcat) * (1.0 / l)
            lse = m + jnp.log(l)
            o_ref[cur, :] = jnp.where(head0, pv[:SPAN], pv[SPAN:])
            lse_ref[cur, :] = jnp.where(head0, lse[:SPAN], lse[SPAN:])

        _for_each_block(block, seg_blocks)

    col = _spec((SEQ, 128), lambda j: (0, j))
    return pl.pallas_call(
        body, name=name, grid=(ATTN_WIDTH // 128,),
        in_specs=[col, col, col], out_specs=[col, col],
        out_shape=[jax.ShapeDtypeStruct((SEQ, ATTN_WIDTH), F32)] * 2,
        compiler_params=_params(1),
    )(q, k, v)


def _unpermute(dst, src_ref, d, cols):
    n = dst.shape[0] // d
    for r in range(d):
        dst[pl.ds(r, n, stride=d), :] = src_ref[r, :, cols].astype(dst.dtype)


def _attn_merge(name, outs, lses, gain):
    t = PERMUTE_TILE
    w = ATTN_WIDTH

    def body(o1, o4, o16, l1, l4, l16, g_ref, an_ref, ant_ref, attn_ref, lse_ref, so4, so16, sl4, sl16):
        for cb in range(w // 128):
            cols = slice(cb * 128, (cb + 1) * 128)
            _unpermute(so4, o4, 4, cols)
            _unpermute(so16, o16, 16, cols)
            _unpermute(sl4, l4, 4, cols)
            _unpermute(sl16, l16, 16, cols)
            la, lb, lc = l1[:, cols], sl4[...], sl16[...]
            m = jnp.maximum(jnp.maximum(la, lb), lc)
            ea, eb, ec = jnp.exp(la - m), jnp.exp(lb - m), jnp.exp(lc - m)
            tot = ea + eb + ec
            attn_ref[:, cols] = (ea * o1[:, cols] + eb * so4[...] + ec * so16[...]) / tot
            lse_ref[:, cols] = m + jnp.log(tot)
        attn = attn_ref[...]
        r = lax.rsqrt(jnp.mean(attn * attn, axis=1, keepdims=True) + NORM_EPS)
        an = attn * r * g_ref[...]
        an_ref[...] = an.astype(BF16)
        ant_ref[...] = an.T.astype(BF16)

    views = lambda arrs: [arrs[0], arrs[1].reshape(4, SEQ // 4, w), arrs[2].reshape(16, SEQ // 16, w)]
    row = _spec((t, w), lambda i: (i, 0))
    return pl.pallas_call(
        body, name=name, grid=(SEQ // t,),
        in_specs=_permuted_specs(t, w) * 2 + [_spec((1, w), lambda i: (0, 0))],
        out_specs=[row, _spec((w, t), lambda i: (0, i)), row, row],
        out_shape=[jax.ShapeDtypeStruct((SEQ, 2 * w), BF16), jax.ShapeDtypeStruct((2 * w, SEQ), BF16),
                   jax.ShapeDtypeStruct((SEQ, w), F32), jax.ShapeDtypeStruct((SEQ, w), F32)],
        scratch_shapes=[pltpu.VMEM((t, 128), F32)] * 4,
        compiler_params=_params(1),
    )(*views(outs), *views(lses), gain)


def _head_sum_matrix():
    i = np.arange(ATTN_WIDTH)
    return jnp.asarray((i[:, None] // HEAD_DIM) == (i[None, :] // HEAD_DIM), dtype=F32)


def _attn_bwd_prep(name, d_an, attn, lse, gain, head_sum):
    t = PERMUTE_TILE
    w = ATTN_WIDTH

    def body(dan_ref, attn_ref, lse_ref, g_ref, hs_ref, *rest):
        (do1, do4, do16, dl1, dl4, dl16, ls4, ls16, dg_ref), (sdo, sdl, sls) = rest[:9], rest[9:]

        @pl.when(pl.program_id(0) == 0)
        def _():
            dg_ref[...] = jnp.zeros_like(dg_ref)

        attn = attn_ref[...]
        dan = dan_ref[...]
        r = lax.rsqrt(jnp.mean(attn * attn, axis=1, keepdims=True) + NORM_EPS)
        xhat = attn * r
        dg_ref[...] += jnp.sum(dan * xhat, axis=0, keepdims=True)
        dang = dan * g_ref[...]
        d_o = r * (dang - xhat * jnp.mean(dang * xhat, axis=1, keepdims=True))
        delta = jnp.dot(d_o * attn, hs_ref[...], preferred_element_type=F32,
                        precision=lax.Precision.HIGHEST)
        do1[...] = d_o.astype(BF16)
        dl1[...] = delta
        for cb in range(w // 128):
            cols = slice(cb * 128, (cb + 1) * 128)
            sdo[...] = d_o[:, cols]
            sdl[...] = delta[:, cols]
            sls[...] = lse_ref[:, cols]
            for d, o_do, o_dl, o_ls in ((4, do4, dl4, ls4), (16, do16, dl16, ls16)):
                for rr in range(d):
                    rows = pl.ds(rr, t // d, stride=d)
                    o_do[rr, :, cols] = sdo[rows, :].astype(BF16)
                    o_dl[rr, :, cols] = sdl[rows, :]
                    o_ls[rr, :, cols] = sls[rows, :]

    row = _spec((t, w), lambda i: (i, 0))
    perm = _permuted_specs(t, w)
    outs = pl.pallas_call(
        body, name=name, grid=(SEQ // t,),
        in_specs=[row, row, row, _spec((1, w), lambda i: (0, 0)), _spec((w, w), lambda i: (0, 0))],
        out_specs=perm + perm + perm[1:] + [_spec((1, w), lambda i: (0, 0))],
        out_shape=(_permuted_shapes(w, BF16) + _permuted_shapes(w, F32) + _permuted_shapes(w, F32)[1:]
                   + [jax.ShapeDtypeStruct((1, w), F32)]),
        scratch_shapes=[pltpu.VMEM((t, 128), F32)] * 3,
        compiler_params=_params(1),
    )(d_an, attn, lse, gain, head_sum)
    flat = lambda arr: arr.reshape(SEQ, w)
    d_out = [flat(a) for a in outs[0:3]]
    delta = [flat(a) for a in outs[3:6]]
    lses = [lse, flat(outs[6]), flat(outs[7])]
    return d_out, delta, lses, outs[8]


def _attn_bwd(name, q, k, v, d_out, delta, lse, seg_blocks):
    def body(q_ref, k_ref, v_ref, do_ref, dl_ref, lse_ref, dq_ref, dk_out, dv_out, dk_ref, dv_ref):
        band, head0 = _band_masks()
        dk_ref[...] = jnp.zeros_like(dk_ref)
        dv_ref[...] = jnp.zeros_like(dv_ref)

        def per_head(x):
            return jnp.concatenate([x[:, 0:1], x[:, HEAD_DIM:HEAD_DIM + 1]], axis=0)

        def block(b, first):
            cur = pl.ds(b * SPAN, SPAN)
            qs = _stack_heads(q_ref[cur, :], head0)
            dos = _stack_heads(do_ref[cur, :], head0)
            if first:
                kcat, vcat, ok = k_ref[cur, :], v_ref[cur, :], band[:, SPAN:]
            else:
                both = pl.ds((b - 1) * SPAN, 2 * SPAN)
                kcat, vcat, ok = k_ref[both, :], v_ref[both, :], band
            p = jnp.where(ok, jnp.exp(_dot_nt(qs, kcat) - per_head(lse_ref[cur, :])), 0.0)
            ds = p * (_dot_nt(dos, vcat) - per_head(dl_ref[cur, :]))
            dq = _dot(ds, kcat)
            dq_ref[cur, :] = jnp.where(head0, dq[:SPAN], dq[SPAN:]).astype(BF16)
            dk = _dot_tn(ds, qs)
            dv = _dot_tn(p, dos)
            if first:
                dk_ref[cur, :] += dk
                dv_ref[cur, :] += dv
            else:
                dk_ref[both, :] += dk
                dv_ref[both, :] += dv

        _for_each_block(block, seg_blocks)
        dk_out[...] = dk_ref[...].astype(BF16)
        dv_out[...] = dv_ref[...].astype(BF16)

    col = _spec((SEQ, 128), lambda j: (0, j))
    return pl.pallas_call(
        body, name=name, grid=(ATTN_WIDTH // 128,),
        in_specs=[col] * 6, out_specs=[col] * 3,
        out_shape=[jax.ShapeDtypeStruct((SEQ, ATTN_WIDTH), BF16)] * 3,
        scratch_shapes=[pltpu.VMEM((SEQ, 128), F32)] * 2,
        compiler_params=_params(1),
    )(q, k, v, d_out, delta, lse)


def _attn_bwd_post(name, grads, cos_t, sin_t):
    t = PERMUTE_TILE
    w = ATTN_WIDTH

    def body(*refs):
        ins, cos_ref, sin_ref, out_ref, s4, s16 = refs[:9], refs[9], refs[10], refs[11], refs[12], refs[13]
        cosv, sinv = cos_ref[...], sin_ref[...]
        for a in range(3):
            g1, g4, g16 = ins[a], ins[3 + a], ins[6 + a]
            for cb in range(w // 128):
                cols = slice(cb * 128, (cb + 1) * 128)
                _unpermute(s4, g4, 4, cols)
                _unpermute(s16, g16, 16, cols)
                val = g1[:, cols].astype(F32) + s4[...] + s16[...]
                if a < 2:
                    val = val * cosv + _swap_halves(val * sinv)
                if a == 0:
                    val = val * (HEAD_DIM ** -0.5)
                out_ref[:, a * w + cb * 128:a * w + (cb + 1) * 128] = val.astype(BF16)

    views = []
    for p, d in enumerate(DILATIONS):
        for a in range(3):
            views.append(grads[p][a] if d == 1 else grads[p][a].reshape(d, SEQ // d, w))
    perm = _permuted_specs(t, w)
    in_specs = [perm[0]] * 3 + [perm[1]] * 3 + [perm[2]] * 3
    return pl.pallas_call(
        body, name=name, grid=(SEQ // t,),
        in_specs=in_specs + [_spec((t, 128), lambda i: (i, 0))] * 2,
        out_specs=_spec((t, 3 * w), lambda i: (i, 0)),
        out_shape=jax.ShapeDtypeStruct((SEQ, 3 * w), BF16),
        scratch_shapes=[pltpu.VMEM((t, 128), F32)] * 2,
        compiler_params=_params(1),
    )(*views, cos_t, sin_t)


N_LEVELS = 7
HGRN_PAIR = 4


def _hgrn_consts():
    c = CHUNK
    i = np.arange(c)[:, None]
    s = np.arange(c)[None, :]
    blocks = [s <= i]
    for lv in range(N_LEVELS):
        bs = c >> lv
        h = bs // 2
        m = (i // bs) * bs + h - 1
        second = (i % bs) >= h
        blocks.append((second & (s > m) & (s <= i)) | (~second & (s > i) & (s <= m)))
    blocks.append(s > i)
    stack = np.concatenate(blocks, axis=0).astype(np.float32)
    twice = np.concatenate([stack, stack], axis=1)
    return jnp.asarray(twice, dtype=BF16), jnp.asarray(stack.T, dtype=BF16)


def _split(x):
    hi = x.astype(BF16)
    return hi, (x - hi.astype(F32)).astype(BF16)


def _exact_dot(m01, x):
    n = x.shape[1]
    full = jnp.dot(m01, jnp.concatenate(_split(x), axis=1), preferred_element_type=F32)
    return full[:, :n] + full[:, n:]


def _exact_dot_twice(m01_twice, x):
    return jnp.dot(m01_twice, jnp.concatenate(_split(x), axis=0), preferred_element_type=F32)


def _hgrn_gates(qh, z, lb):
    sq = _sigmoid(qh)
    q = qh * sq * (HGRN_DIM ** -0.5)
    sig = _sigmoid(z)
    sigm = _sigmoid(-z)
    f = lb + (1.0 - lb) * sig
    k = (1.0 - lb) * sigm
    return q, k, f, sq, sig, sigm


def _level_masks(lv):
    row = lax.broadcasted_iota(jnp.int32, (CHUNK, CHUNK), 0)
    col = lax.broadcasted_iota(jnp.int32, (CHUNK, CHUNK), 1)
    shift = N_LEVELS - lv
    half = CHUNK >> (lv + 1)
    second = (row & half) != 0
    second_col = (col & half) != 0
    same = (row >> shift) == (col >> shift)
    return second, same & second & ~second_col, same & (second != second_col)


def _hgrn_fwd(name, proj, lb, gain, stack, mixed, mixed_t):
    t = ROW_TILE
    per = t // CHUNK
    n_rb = SEQ // t
    n_chunks = SEQ // CHUNK
    col0 = 3 * ATTN_WIDTH // 128
    pair_w = HGRN_PAIR * HGRN_DIM

    def body(q_ref, f_ref, i_ref, g_ref, lb_ref, gain_ref, stack_ref, mixed_in, mixed_t_in,
             rec_ref, rect_ref, o_ref, st_out, a_out, st):
        del mixed_in, mixed_t_in

        @pl.when(pl.program_id(1) == 0)
        def _():
            st[...] = jnp.zeros_like(st)

        row = lax.broadcasted_iota(jnp.int32, (CHUNK, CHUNK), 0)
        col = lax.broadcasted_iota(jnp.int32, (CHUNK, CHUNK), 1)
        masks = [_level_masks(lv) for lv in range(N_LEVELS)]
        for c, hh in [(c, hh) for c in range(per) for hh in range(HGRN_PAIR)]:
            rows = slice(c * CHUNK, (c + 1) * CHUNK)
            lanes = slice(hh * HGRN_DIM, (hh + 1) * HGRN_DIM)
            lbv = lb_ref[hh]
            qh, z, v, gh = q_ref[rows, lanes], f_ref[rows, lanes], i_ref[rows, lanes], g_ref[rows, lanes]
            q, k, f, _, _, _ = _hgrn_gates(qh, z, lbv)
            dec = _exact_dot_twice(stack_ref[...], jnp.log(f))
            g = dec[0:CHUNK]
            to_end = dec[(N_LEVELS + 1) * CHUNK:(N_LEVELS + 2) * CHUNK]
            a = jnp.where(row == col, jnp.sum(q * k, axis=1, keepdims=True), 0.0)
            for lv in range(N_LEVELS):
                second, square, _ = masks[lv]
                t = jnp.where(second, q, k) * jnp.exp(dec[(lv + 1) * CHUNK:(lv + 2) * CHUNK])
                a = a + jnp.where(square, _dot_nt(t, t), 0.0)
            st_prev = st[hh]
            st_out[hh, c] = st_prev
            a_out[hh, c] = a
            o = _dot(a, v) + _dot_nt(q * jnp.exp(g), st_prev)
            k_end = k * jnp.exp(to_end)
            st[hh] = st_prev * jnp.exp(g[CHUNK - 1:CHUNK, :]) + _dot(v.T, k_end)
            o_ref[rows, lanes] = o
            r = lax.rsqrt(jnp.mean(o * o, axis=1, keepdims=True) + NORM_EPS)
            rec = o * r * gain_ref[...] * (gh * _sigmoid(gh))
            rec_ref[rows, lanes] = rec.astype(BF16)
            rect_ref[lanes, rows] = rec.T.astype(BF16)

    def col_spec(tt):
        return _spec((t, pair_w), lambda h, rb: (rb, (col0 + HGRN_HEADS * tt) // HGRN_PAIR + h))

    chunk_spec = _spec((HGRN_PAIR, per, CHUNK, CHUNK), lambda h, rb: (h, rb, 0, 0))
    return pl.pallas_call(
        body, name=name, grid=(HGRN_HEADS // HGRN_PAIR, n_rb),
        in_specs=[col_spec(0), col_spec(1), col_spec(2), col_spec(3),
                  _spec((HGRN_PAIR, 1, HGRN_DIM), lambda h, rb: (h, 0, 0)),
                  _spec((1, HGRN_DIM), lambda h, rb: (0, 0)),
                  _spec(stack.shape, lambda h, rb: (0, 0)), ANY_SPEC, ANY_SPEC],
        out_specs=[_spec((t, pair_w), lambda h, rb: (rb, ATTN_WIDTH // pair_w + h)),
                   _spec((pair_w, t), lambda h, rb: (ATTN_WIDTH // pair_w + h, rb)),
                   _spec((t, pair_w), lambda h, rb: (rb, h)),
                   chunk_spec, chunk_spec],
        out_shape=[jax.ShapeDtypeStruct(mixed.shape, BF16),
                   jax.ShapeDtypeStruct(mixed_t.shape, BF16),
                   jax.ShapeDtypeStruct((SEQ, HGRN_WIDTH), F32),
                   jax.ShapeDtypeStruct((HGRN_HEADS, n_chunks, CHUNK, CHUNK), F32),
                   jax.ShapeDtypeStruct((HGRN_HEADS, n_chunks, CHUNK, CHUNK), F32)],
        scratch_shapes=[pltpu.VMEM((HGRN_PAIR, CHUNK, CHUNK), F32)],
        input_output_aliases={7: 0, 8: 1},
        compiler_params=_params(2),
    )(proj, proj, proj, proj, lb, gain, stack, mixed, mixed_t)


def _hgrn_bwd(name, proj, d_rec, o_pre, states, scores, lb, gain, stack, stack_t):
    t = ROW_TILE
    per = t // CHUNK
    n_rb = SEQ // t
    col0 = 3 * ATTN_WIDTH // 128
    pair_w = HGRN_PAIR * HGRN_DIM

    def body(q_ref, f_ref, i_ref, g_ref, drec_ref, o_ref, st_ref, a_ref, lb_ref, gain_ref,
             stack_ref, stack_t_ref, dq_ref, df_ref, di_ref, dg_ref, dlb_ref, dgain_ref, dst):
        @pl.when(pl.program_id(1) == 0)
        def _():
            dst[...] = jnp.zeros_like(dst)
            dlb_ref[...] = jnp.zeros_like(dlb_ref)
            dgain_ref[...] = jnp.zeros_like(dgain_ref)

        gain_v = gain_ref[...]
        row = lax.broadcasted_iota(jnp.int32, (CHUNK, CHUNK), 0)
        col = lax.broadcasted_iota(jnp.int32, (CHUNK, CHUNK), 1)
        masks = [_level_masks(lv) for lv in range(N_LEVELS)]
        for c, hh in [(c, hh) for c in reversed(range(per)) for hh in range(HGRN_PAIR)]:
            rows = slice(c * CHUNK, (c + 1) * CHUNK)
            lanes = slice(hh * HGRN_DIM, (hh + 1) * HGRN_DIM)
            lbv = lb_ref[hh]
            qh, z, v, gh = q_ref[rows, lanes], f_ref[rows, lanes], i_ref[rows, lanes], g_ref[rows, lanes]
            q, k, f, sq, sig, sigm = _hgrn_gates(qh, z, lbv)
            dec = _exact_dot_twice(stack_ref[...], jnp.log(f))
            g = dec[0:CHUNK]
            to_end = dec[(N_LEVELS + 1) * CHUNK:(N_LEVELS + 2) * CHUNK]
            e_g = jnp.exp(g)
            e_end = jnp.exp(to_end)
            e_last = jnp.exp(g[CHUNK - 1:CHUNK, :])
            q_in = q * e_g
            k_end = k * e_end
            st_prev = st_ref[hh, c]
            a = a_ref[hh, c]
            dst_new = dst[hh]

            o = o_ref[rows, lanes]
            drec = drec_ref[rows, lanes]
            sg = _sigmoid(gh)
            r = lax.rsqrt(jnp.mean(o * o, axis=1, keepdims=True) + NORM_EPS)
            ohat = o * r
            d_gh = drec * (ohat * gain_v) * (sg * (1.0 + gh * (1.0 - sg)))
            d_on = drec * (gh * sg)
            dgain_ref[hh] += jnp.sum(d_on * ohat, axis=0, keepdims=True)
            d_ohat = d_on * gain_v
            d_o = r * (d_ohat - ohat * jnp.mean(d_ohat * ohat, axis=1, keepdims=True))

            d_sym = jnp.where(row >= col, _dot_nt(d_o, v), _dot_nt(v, d_o))
            d_v = _dot(a.T, d_o) + _dot_nt(k_end, dst_new)
            d_q_in = _dot(d_o, st_prev)
            d_k_end = _dot(v, dst_new)
            d_q = d_q_in * e_g
            d_k = d_k_end * e_end
            diag = jnp.sum(d_o * v, axis=1, keepdims=True)
            d_q = d_q + diag * k
            d_k = d_k + diag * q
            d_dec = [q_in * d_q_in]
            d_both, d_second = None, None
            for lv in range(N_LEVELS):
                e = jnp.exp(dec[(lv + 1) * CHUNK:(lv + 2) * CHUNK])
                second, _, mirrored = masks[lv]
                t = jnp.where(second, q, k) * e
                d_t = _dot(jnp.where(mirrored, d_sym, 0.0), t)
                d_te = d_t * e
                d_both = d_te if d_both is None else d_both + d_te
                d_second = jnp.where(second, d_te, 0.0) if d_second is None else d_second + jnp.where(second, d_te, 0.0)
                d_dec.append(t * d_t)
            d_q = d_q + d_second
            d_k = d_k + (d_both - d_second)
            d_dec.append(k_end * d_k_end)
            flux = jnp.sum(dst_new * st_prev, axis=0, keepdims=True) * e_last
            d_lf = _exact_dot(stack_t_ref[...], jnp.concatenate(d_dec, axis=0)) + flux
            dst[hh] = dst_new * e_last + _dot(d_o.T, q_in)

            d_f = d_lf / f - d_k
            dlb_ref[hh] += jnp.sum(d_f * sigm, axis=0, keepdims=True)
            dq_ref[rows, lanes] = (d_q * (HGRN_DIM ** -0.5) * (sq * (1.0 + qh * (1.0 - sq)))).astype(BF16)
            df_ref[rows, lanes] = (d_f * (1.0 - lbv) * sig * sigm).astype(BF16)
            di_ref[rows, lanes] = d_v.astype(BF16)
            dg_ref[rows, lanes] = d_gh.astype(BF16)

    last = n_rb - 1

    def col_spec(tt):
        return _spec((t, pair_w), lambda h, rb: (last - rb, (col0 + HGRN_HEADS * tt) // HGRN_PAIR + h))

    head_col = _spec((t, pair_w), lambda h, rb: (last - rb, h))
    rec_col0 = (d_rec.shape[1] - HGRN_WIDTH) // pair_w
    d_rec_col = _spec((t, pair_w), lambda h, rb: (last - rb, rec_col0 + h))
    chunk_spec = _spec((HGRN_PAIR, per, CHUNK, CHUNK), lambda h, rb: (h, last - rb, 0, 0))
    vec_spec = _spec((HGRN_PAIR, 1, HGRN_DIM), lambda h, rb: (h, 0, 0))
    outs = pl.pallas_call(
        body, name=name, grid=(HGRN_HEADS // HGRN_PAIR, n_rb),
        in_specs=[col_spec(0), col_spec(1), col_spec(2), col_spec(3), d_rec_col, head_col,
                  chunk_spec, chunk_spec, vec_spec,
                  _spec((1, HGRN_DIM), lambda h, rb: (0, 0)),
                  _spec(stack.shape, lambda h, rb: (0, 0)), _spec(stack_t.shape, lambda h, rb: (0, 0))],
        out_specs=[head_col] * 4 + [vec_spec, vec_spec],
        out_shape=[jax.ShapeDtypeStruct((SEQ, HGRN_WIDTH), BF16)] * 4
                  + [jax.ShapeDtypeStruct((HGRN_HEADS, 1, HGRN_DIM), F32)] * 2,
        scratch_shapes=[pltpu.VMEM((HGRN_PAIR, CHUNK, CHUNK), F32)],
        compiler_params=_params(2),
    )(proj, proj, proj, proj, d_rec, o_pre, states, scores, lb, gain, stack, stack_t)
    return outs


ANY_SPEC = pl.BlockSpec(memory_space=pl.ANY)


def _my_place():
    return lax.axis_index("x"), lax.axis_index("y"), lax.axis_index("c")


def _other_chips(x, y):
    return [(1 - x, y), (x, 1 - y), (1 - x, 1 - y)]


def _remote(src, dst, send_sem, recv_sem, device):
    return pltpu.make_async_remote_copy(src_ref=src, dst_ref=dst, send_sem=send_sem, recv_sem=recv_sem,
                                        device_id=device, device_id_type=MESH)


def _staged_copies(srcs, dsts, stage, sems):
    loads = [pltpu.make_async_copy(srcs[i], stage[i], sems.at[i]) for i in range(len(srcs))]
    for cp in loads:
        cp.start()
    stores = []
    for i, cp in enumerate(loads):
        cp.wait()
        stores.append(pltpu.make_async_copy(stage[i], dsts[i], sems.at[i]))
        stores[-1].start()
    return stores


def _gather_weights(name, shards):
    n = len(shards)

    def body(*refs):
        ins, outs = refs[:n], refs[n:2 * n]
        ici_send, ici_recv, d2d_send, d2d_recv, local_sems = refs[2 * n:2 * n + 5]
        stage = refs[2 * n + 5:]
        x, y, c = _my_place()
        me = 2 * x + y
        chips = _other_chips(x, y)

        def half(i, which):
            h = ins[i].shape[0] // 2
            return pl.ds(which * h, h)

        sends = []
        for i in range(n):
            for j, (px, py) in enumerate(chips):
                sends.append(_remote(ins[i].at[half(i, c), :], outs[i].at[me, half(i, c), :],
                                     ici_send.at[3 * i + j], ici_recv.at[3 * i + j], (px, py, c)))
        for cp in sends:
            cp.start()
        local = _staged_copies(ins, [outs[i].at[me] for i in range(n)], stage, local_sems)
        for i in range(n):
            for j, (px, py) in enumerate(chips):
                landed = outs[i].at[2 * px + py, half(i, c), :]
                _remote(landed, landed, ici_send.at[3 * i + j], ici_recv.at[3 * i + j], (px, py, c)).wait_recv()
                forward = _remote(landed, landed, d2d_send.at[3 * i + j], d2d_recv.at[3 * i + j], (x, y, 1 - c))
                forward.start()
                sends.append(forward)
        for i in range(n):
            for j, (px, py) in enumerate(chips):
                other = outs[i].at[2 * px + py, half(i, 1 - c), :]
                _remote(other, other, d2d_send.at[3 * i + j], d2d_recv.at[3 * i + j], (x, y, 1 - c)).wait_recv()
        for cp in sends:
            cp.wait_send()
        for cp in local:
            cp.wait()

    return pl.pallas_call(
        body, name=name, in_specs=[ANY_SPEC] * n, out_specs=[ANY_SPEC] * n,
        out_shape=[jax.ShapeDtypeStruct((N_CHIPS,) + s.shape, s.dtype) for s in shards],
        scratch_shapes=([pltpu.SemaphoreType.DMA((3 * n,))] * 4 + [pltpu.SemaphoreType.DMA((n,))]
                        + [pltpu.VMEM(s.shape, s.dtype) for s in shards]),
        compiler_params=pltpu.CompilerParams(vmem_limit_bytes=VMEM_LIMIT),
    )(*shards)


HBM_SPEC = pl.BlockSpec(memory_space=pltpu.HBM)
SEM_SPEC = pl.BlockSpec(memory_space=pltpu.SEMAPHORE)
SPLIT_PARAMS = pltpu.CompilerParams(has_side_effects=pltpu.SideEffectType.DATAFLOW_SIDE_EFFECTING)
N_PEERS = {"gather": N_CHIPS - 1, "scatter": N_DEV - 1}


def _split_copies(ins, lands, send_sems, recv_sems, kind):
    x, y, c = _my_place()
    pairs = []
    for i in range(len(ins)):
        if kind == "gather":
            me = 2 * x + y
            for j, (px, py) in enumerate(_other_chips(x, y)):
                sems = (send_sems.at[3 * i + j], recv_sems.at[3 * i + j], (px, py, c))
                pairs.append((_remote(ins[i], lands[i].at[me], *sems),
                              _remote(ins[i], lands[i].at[2 * px + py], *sems)))
        else:
            me = 4 * x + 2 * y + c
            h = ins[i].shape[1] // 2
            for k in range(1, N_DEV):
                px, py, pc = (x + (k >> 2)) % 2, (y + ((k >> 1) & 1)) % 2, (c + (k & 1)) % 2
                src = ins[i].at[2 * px + py, pl.ds(pc * h, h), :]
                sems = (send_sems.at[7 * i + k - 1], recv_sems.at[7 * i + k - 1], (px, py, pc))
                pairs.append((_remote(src, lands[i].at[me], *sems),
                              _remote(src, lands[i].at[4 * px + 2 * py + pc], *sems)))
    return pairs


def _exchange_start(name, srcs, lands, kind, after=None):
    n = len(srcs)
    n_sems = N_PEERS[kind] * n
    extra = [] if after is None else [after]

    def body(*refs):
        ins, land_refs = refs[:n], refs[n:2 * n]
        send_sems, recv_sems = refs[2 * n + len(extra):2 * n + len(extra) + 2]
        token = refs[-1]
        for send, _ in _split_copies(ins, land_refs, send_sems, recv_sems, kind):
            send.start()
        token[...] = jnp.zeros_like(token)

    arrays = list(srcs) + list(lands)
    outs = pl.pallas_call(
        body, name=name,
        in_specs=[HBM_SPEC] * (2 * n) + [ANY_SPEC] * len(extra),
        out_shape=([pltpu.SemaphoreType.DMA((n_sems,))] * 2 + [pltpu.HBM(a.shape, a.dtype) for a in arrays]
                   + [jax.ShapeDtypeStruct((8, 128), F32)]),
        out_specs=[SEM_SPEC] * 2 + [HBM_SPEC] * (2 * n) + [pl.BlockSpec(memory_space=pltpu.VMEM)],
        input_output_aliases={i: 2 + i for i in range(2 * n)},
        compiler_params=SPLIT_PARAMS,
    )(*[pltpu.with_memory_space_constraint(a, pltpu.HBM) for a in arrays], *extra)
    return outs[:2], outs[2:2 + 2 * n], outs[-1]


def _exchange_wait(name, sems, passed, kind, after):
    n = len(passed) // 2

    def body(*refs):
        ins, land_refs = refs[:n], refs[n:2 * n]
        send_sems, recv_sems = refs[2 * n:2 * n + 2]
        for send, arrive in _split_copies(ins, land_refs, send_sems, recv_sems, kind):
            send.wait_send()
            arrive.wait_recv()

    outs = pl.pallas_call(
        body, name=name,
        in_specs=[HBM_SPEC] * (2 * n) + [SEM_SPEC] * 2 + [ANY_SPEC],
        out_shape=[pltpu.HBM(a.shape, a.dtype) for a in passed],
        out_specs=[HBM_SPEC] * (2 * n),
        input_output_aliases={i: i for i in range(2 * n)},
        compiler_params=SPLIT_PARAMS,
    )(*passed, *sems, after)
    return outs[:n], outs[n:]


def _own_slot(name, own, me):
    r, cc = own.shape
    th = min(r, 512)

    def body(me_ref, x_ref, o_ref):
        del me_ref
        o_ref[...] = x_ref[...]

    grid_spec = pltpu.PrefetchScalarGridSpec(
        num_scalar_prefetch=1, grid=(r // th,),
        in_specs=[pl.BlockSpec((th, cc), lambda i, me_ref: (i, 0))],
        out_specs=pl.BlockSpec((None, th, cc), lambda i, me_ref: (me_ref[0], i, 0)))
    return pl.pallas_call(
        body, name=name, grid_spec=grid_spec,
        out_shape=jax.ShapeDtypeStruct((N_CHIPS, r, cc), own.dtype), compiler_params=_params(1),
    )(me, own)


def _sum_devices(name, landed, own, place):
    n_dev, h, cc = landed.shape
    th = min(h, 256)
    nb = h // th

    def body(place_ref, l_ref, own_ref, o_ref):
        total = None
        for d in range(n_dev):
            piece = jnp.where(place_ref[0] == d, own_ref[...], l_ref[d]).astype(F32)
            total = piece if total is None else total + piece
        o_ref[...] = total

    grid_spec = pltpu.PrefetchScalarGridSpec(
        num_scalar_prefetch=1, grid=(nb,),
        in_specs=[pl.BlockSpec((n_dev, th, cc), lambda i, p: (0, i, 0)),
                  pl.BlockSpec((None, th, cc), lambda i, p: (p[1], p[2] * nb + i, 0))],
        out_specs=pl.BlockSpec((th, cc), lambda i, p: (i, 0)))
    return pl.pallas_call(
        body, name=name, grid_spec=grid_spec,
        out_shape=jax.ShapeDtypeStruct((h, cc), F32), compiler_params=_params(1),
    )(place, landed, own)


def _share_halves(name, halves):
    flat = [t for per_weight in halves for t in per_weight]
    n = len(flat)
    n_w = len(halves)

    def body(*refs):
        ins, outs = refs[:n], refs[n:n + n_w]
        send_sems, recv_sems, local_sems = refs[n + n_w:n + n_w + 3]
        stage = refs[n + n_w + 3:]
        x, y, c = _my_place()
        sends, own = [], []
        for i in range(n):
            w, l = divmod(i, DEPTH)
            h = ins[i].shape[0]
            own.append(outs[w].at[l, pl.ds(c * h, h), :])
            sends.append(_remote(ins[i], own[i], send_sems.at[i], recv_sems.at[i], (x, y, 1 - c)))
        for cp in sends:
            cp.start()
        local = _staged_copies(ins, own, stage, local_sems)
        for i in range(n):
            w, l = divmod(i, DEPTH)
            h = ins[i].shape[0]
            _remote(ins[i], outs[w].at[l, pl.ds((1 - c) * h, h), :], send_sems.at[i], recv_sems.at[i],
                    (x, y, 1 - c)).wait_recv()
        for cp in sends:
            cp.wait_send()
        for cp in local:
            cp.wait()

    return pl.pallas_call(
        body, name=name, in_specs=[ANY_SPEC] * n, out_specs=[ANY_SPEC] * n_w,
        out_shape=[jax.ShapeDtypeStruct((DEPTH, 2 * per_weight[0].shape[0], per_weight[0].shape[1]), F32)
                   for per_weight in halves],
        scratch_shapes=([pltpu.SemaphoreType.DMA((n,))] * 3 + [pltpu.VMEM(t.shape, t.dtype) for t in flat]),
        compiler_params=pltpu.CompilerParams(vmem_limit_bytes=VMEM_LIMIT),
    )(*flat)


def _all_reduce_small(pack, after):
    def body(p_ref, after_ref, o_ref, recv, send_sems, recv_sems):
        del after_ref
        x, y, c = _my_place()
        me = 4 * x + 2 * y + c
        recv[me] = p_ref[...]
        peers = []
        for k in range(1, N_DEV):
            px, py, pc = (x + (k >> 2)) % 2, (y + ((k >> 1) & 1)) % 2, (c + (k & 1)) % 2
            peers.append((px, py, pc))
        sends = [_remote(p_ref, recv.at[me], send_sems.at[k], recv_sems.at[k], peer)
                 for k, peer in enumerate(peers)]
        for cp in sends:
            cp.start()
        for k, (px, py, pc) in enumerate(peers):
            _remote(p_ref, recv.at[4 * px + 2 * py + pc], send_sems.at[k], recv_sems.at[k],
                    (px, py, pc)).wait_recv()
        for cp in sends:
            cp.wait_send()
        total = recv[0]
        for d in range(1, N_DEV):
            total = total + recv[d]
        o_ref[...] = total

    vmem = pl.BlockSpec(memory_space=pltpu.VMEM)
    return pl.pallas_call(
        body, name="all_reduce_small", in_specs=[vmem, ANY_SPEC], out_specs=vmem,
        out_shape=jax.ShapeDtypeStruct(pack.shape, F32),
        scratch_shapes=[pltpu.VMEM((N_DEV,) + pack.shape, F32),
                        pltpu.SemaphoreType.DMA((N_DEV - 1,)), pltpu.SemaphoreType.DMA((N_DEV - 1,))],
    )(pack, after)


def _adamw(name, w, g, m, v):
    r, cc = w.shape
    th = min(r, 256)

    def body(w_ref, g_ref, m_ref, v_ref, d_ref, m_out, v_out):
        gv = g_ref[...]
        m2 = ADAM_B1 * m_ref[...] + (1.0 - ADAM_B1) * gv
        v2 = ADAM_B2 * v_ref[...] + (1.0 - ADAM_B2) * (gv * gv)
        m_hat = m2 / (1.0 - ADAM_B1 ** ADAM_STEP)
        v_hat = v2 / (1.0 - ADAM_B2 ** ADAM_STEP)
        d_ref[...] = -ADAM_LR * (m_hat / (jnp.sqrt(v_hat) + ADAM_EPS) + ADAM_WD * w_ref[...])
        m_out[...] = m2
        v_out[...] = v2

    tile = _spec((th, cc), lambda i: (i, 0))
    return pl.pallas_call(
        body, name=name, grid=(r // th,), in_specs=[tile] * 4, out_specs=[tile] * 3,
        out_shape=[jax.ShapeDtypeStruct((r, cc), F32)] * 3, compiler_params=_params(1),
    )(w, g, m, v)


def _lower_bounds(lb_logits):
    p = jax.nn.softmax(lb_logits.astype(F32), axis=0)
    return jnp.cumsum(p, axis=0) - p[0]


def _layer_forward(l, stream, small, weights, consts, next_gain=None, loss=None, after=None):
    win, rest = weights
    cos_t, sin_t, stack, _, _ = consts
    tm = MM_TILE
    x_in, h, h_t = stream
    saved = {"x_in": x_in}

    proj = _mm_pieces(f"proj{l}", h, win, False, tm, after=after)
    saved.update(h_t=h_t, proj=proj)

    qkv = _attn_prep(f"attn_prep{l}", proj, cos_t, sin_t)
    outs, lses = [], []
    for p, d in enumerate(DILATIONS):
        o, lse = _attn_fwd(f"attn_fwd{l}_{d}", *qkv[p], SEQ // d // SPAN)
        outs.append(o)
        lses.append(lse)
    mixed, mixed_t, attn, lse = _attn_merge(f"attn_merge{l}", outs, lses, small["attn_out_gain"][l][None, :])
    saved.update(qkv=qkv, attn=attn, lse=lse)

    lb3 = small["lower"][l].reshape(HGRN_HEADS, 1, HGRN_DIM)
    mixed, mixed_t, o_pre, states, scores = _hgrn_fwd(f"hgrn_fwd{l}", proj, lb3, small["hgrn_out_gain"][l][None, :],
                                                      stack, mixed, mixed_t)
    wo, wu, wd = rest(mixed)
    saved.update(mixed_t=mixed_t, o_pre=o_pre, states=states, scores=scores, lb3=lb3, weights=(win, wo, wu, wd))

    x_mid, h2, h2_t = _mm_accum(f"out_proj{l}", mixed, wo, False, tm, x_in, next_gain=small["norm_mlp"][l][None, :])
    saved["x_mid"] = x_mid

    a, relu_u, a_t = _mm_pieces(f"up{l}", h2, wu, False, tm, epilogue="relu2")
    new_stream = tuple(_mm_accum(f"down{l}", a, wd, False, tm, x_mid, next_gain=next_gain, loss=loss))
    saved.update(h2_t=h2_t, relu_u=relu_u, a_t=a_t)
    return new_stream, saved


def _layer_backward(l, dx, saved, small, consts, on_grads, after=None):
    win, wo, wu, wd = saved["weights"]
    cos_t, sin_t, stack, stack_t, head_sum = consts
    tm = MM_TILE

    dx, dx_b = dx
    du = _mm_pieces(f"d_u{l}", dx_b, wd, True, tm, epilogue="relu2_grad", extra=saved["relu_u"], after=after)
    d_wd = _mm_dw(f"d_wdown{l}", saved["a_t"], dx_b, False, tm)
    dxm, dxm_b, dg_mlp = _mm_accum(f"d_h2_{l}", du, wu, True, tm, dx,
                                   norm=(saved["x_mid"], small["norm_mlp"][l][None, :]))
    d_wu = _mm_dw(f"d_wup{l}", saved["h2_t"], du, True, tm)
    d_wo = _mm_dw(f"d_wout{l}", saved["mixed_t"], dxm_b, False, tm)
    after_early = on_grads(l, "early", (d_wu, d_wd, d_wo))

    d_mixed = _mm_pieces(f"d_mixed{l}", dxm_b, wo, True, tm, after=after_early)
    d_rec = d_mixed

    d_out, delta, lses, dg_attn = _attn_bwd_prep(f"attn_bwd_prep{l}", d_mixed, saved["attn"], saved["lse"],
                                                 small["attn_out_gain"][l][None, :], head_sum)
    grads = []
    for p, d in enumerate(DILATIONS):
        grads.append(_attn_bwd(f"attn_bwd{l}_{d}", *saved["qkv"][p], d_out[p], delta[p], lses[p],
                               SEQ // d // SPAN))
    dp_attn = _attn_bwd_post(f"attn_bwd_post{l}", grads, cos_t, sin_t)

    dq_h, df_h, di_h, dg_h, d_lower, dg_hgrn = _hgrn_bwd(
        f"hgrn_bwd{l}", saved["proj"], d_rec, saved["o_pre"], saved["states"], saved["scores"],
        saved["lb3"], small["hgrn_out_gain"][l][None, :], stack, stack_t)
    dproj = [dp_attn, dq_h, df_h, di_h, dg_h]

    d_win = _mm_dw(f"d_win{l}", saved["h_t"], dproj, True, tm)
    after_last = on_grads(l, "last", (d_win,))
    dx_in, dx_in_b, dg_mix = _mm_accum(f"d_h{l}", dproj, win, True, tm, dxm,
                                       norm=(saved["x_in"], small["norm_mix"][l][None, :]), after=after_last)

    small_grads = {"norm_mix": dg_mix[0], "attn_out_gain": dg_attn[0],
                   "lower": d_lower.reshape(HGRN_WIDTH),
                   "hgrn_out_gain": jnp.sum(dg_hgrn, axis=0).reshape(HGRN_DIM), "norm_mlp": dg_mlp[0]}
    return (dx_in, dx_in_b), after_last, small_grads


def _local_step(xs, target, small, get_weights, on_grads):
    consts = _rope_tables() + _hgrn_consts() + (_head_sum_matrix(),)
    stream = (xs,) + tuple(_rms_fwd("norm_mix0", xs, small["norm_mix"][0][None, :]))
    saved = []
    for l in range(DEPTH):
        w, after = get_weights(l, stream[0])
        if l + 1 < DEPTH:
            stream, s = _layer_forward(l, stream, small, w, consts, next_gain=small["norm_mix"][l + 1][None, :],
                                       after=after)
        else:
            stream, s = _layer_forward(l, stream, small, w, consts, loss=(small["norm_final"][None, :], target),
                                       after=after)
        saved.append(s)
    dx_f, dx_b, dg_final, loss = stream
    dx = (dx_f, dx_b)
    small_grads = [None] * DEPTH
    after = None
    for l in reversed(range(DEPTH)):
        dx, after, small_grads[l] = _layer_backward(l, dx, saved[l], small, consts, on_grads, after=after)
    return loss, dx[0], dg_final[0], small_grads


def _pack_small(norm_mix, attn_out_gain, lb, hgrn_out_gain, norm_mlp, norm_final, last_row):
    rows = [norm_mix, attn_out_gain.reshape(1, D_MODEL), lb.reshape(1, D_MODEL),
            jnp.pad(hgrn_out_gain.reshape(1, DEPTH * HGRN_DIM), ((0, 0), (0, D_MODEL - DEPTH * HGRN_DIM))),
            norm_mlp, norm_final.reshape(1, D_MODEL), last_row.reshape(1, D_MODEL)]
    pack = jnp.concatenate(rows, axis=0)
    return jnp.pad(pack, ((0, PACK_ROWS - pack.shape[0]), (0, 0)))


def _unpack_small(pack):
    return (pack[0:2], pack[2].reshape(DEPTH, ATTN_WIDTH), pack[3].reshape(DEPTH, HGRN_WIDTH),
            pack[4, :DEPTH * HGRN_DIM].reshape(DEPTH, HGRN_DIM), pack[5:7], pack[7], pack[8])


def kernel(x, norm_mix, w_in, attn_out_gain, hgrn_lb_logits, hgrn_out_gain, w_out, norm_mlp, w_up, w_down, norm_final, loss_target, m_norm_mix, m_w_in, m_attn_out_gain, m_hgrn_lb_logits, m_hgrn_out_gain, m_w_out, m_norm_mlp, m_w_up, m_w_down, m_norm_final, v_norm_mix, v_w_in, v_attn_out_gain, v_hgrn_lb_logits, v_hgrn_out_gain, v_w_out, v_norm_mlp, v_w_up, v_w_down, v_norm_final):
    lower, lower_vjp = jax.vjp(_lower_bounds, hgrn_lb_logits)
    small = {"norm_mix": norm_mix, "attn_out_gain": attn_out_gain, "lower": lower,
             "hgrn_out_gain": hgrn_out_gain, "norm_mlp": norm_mlp, "norm_final": norm_final}
    big_w = (w_in, w_out, w_up, w_down)

    x_pos, y_pos, core = lax.axis_index("x"), lax.axis_index("y"), lax.axis_index("c")
    me = (2 * x_pos + y_pos).astype(jnp.int32).reshape(1)
    place = jnp.stack([4 * x_pos + 2 * y_pos + core, 2 * x_pos + y_pos, core]).astype(jnp.int32)
    shards = [[w[l].astype(BF16) for w in big_w] for l in range(DEPTH)]
    in_flight = {}

    def start_gather(name, some, after):
        lands = [_own_slot(f"own_{name}_{i}", s, me) for i, s in enumerate(some)]
        sems, passed, token = _exchange_start(f"start_{name}", some, lands, "gather", after)
        in_flight[name] = (sems, passed)
        return token

    def finish_gather(name, after):
        return _exchange_wait(f"wait_{name}", *in_flight.pop(name), "gather", after)[1]

    def get_weights(l, stream):
        if l == 0:
            (win,) = _gather_weights("gather_w_in0", shards[0][:1])
            token = start_gather("gather_rest0", shards[0][1:], win)
            token = start_gather("gather_w_in1", shards[1][:1], token)
            token = start_gather("gather_rest1", shards[1][1:], token)
            return (win, lambda after: finish_gather("gather_rest0", after)), token
        (win,) = finish_gather("gather_w_in1", stream)
        return (win, lambda after: finish_gather("gather_rest1", after)), None

    reduced = {}

    def start_exchange(name, grads):
        srcs, lands = [g for g, _ in grads], [land for _, land in grads]
        sems, passed, token = _exchange_start(f"start_{name}", srcs, lands, "scatter")
        in_flight[name] = (sems, passed)
        return token

    def finish_exchange(name, after):
        own, landed = _exchange_wait(f"wait_{name}", *in_flight.pop(name), "scatter", after)
        return [_sum_devices(f"sum_{name}_{i}", p, g, place) for i, (p, g) in enumerate(zip(landed, own))]

    def on_grads(l, group, grads):
        token = start_exchange(f"{group}{l}", grads)
        if (l, group) == (0, "early"):
            reduced[(1, "early")] = finish_exchange("early1", token)
            reduced[(1, "last")] = finish_exchange("last1", token)
        if (l, group) == (0, "last"):
            reduced[(0, "early")] = finish_exchange("early0", token)
        return token

    loss, dx, dg_final, sg = _local_step(x[0], loss_target[0], small, get_weights, on_grads)

    big_m = (m_w_in, m_w_out, m_w_up, m_w_down)
    big_v = (v_w_in, v_w_out, v_w_up, v_w_down)
    names = ("w_in", "w_out", "w_up", "w_down")
    big_g, big_delta, big_new_m, big_new_v = [None] * 4, [None] * 4, [None] * 4, [None] * 4

    def finish_weights(group, which):
        whole = _share_halves(f"share_{group}", [[reduced[(l, group)][i] for l in range(DEPTH)]
                                                 for i in range(len(which))])
        for i, w in enumerate(which):
            shape = big_w[w].shape
            flat = lambda arr: arr.reshape(shape[0] * shape[1], shape[2])
            d, m2, v2 = _adamw(f"adamw_{names[w]}", flat(big_w[w]), flat(whole[i]), flat(big_m[w]), flat(big_v[w]))
            big_g[w], big_delta[w] = whole[i], d.reshape(shape)
            big_new_m[w], big_new_v[w] = m2.reshape(shape), v2.reshape(shape)

    finish_weights("early", (2, 3, 1))
    reduced[(0, "last")] = finish_exchange("last0", big_delta[3])
    finish_weights("last", (0,))

    stack2 = lambda key: jnp.stack([sg[l][key] for l in range(DEPTH)])
    pack = _pack_small(stack2("norm_mix"), stack2("attn_out_gain"), stack2("lower"), stack2("hgrn_out_gain"),
                       stack2("norm_mlp"), dg_final, jnp.broadcast_to(loss[0, 0], (D_MODEL,)))
    g_mix, g_attn, g_lower, g_hgrn, g_mlp, g_final, loss_row = _unpack_small(_all_reduce_small(pack, big_delta[0]))
    (g_logits,) = lower_vjp(g_lower)

    zeros_row = jnp.zeros((D_MODEL,), F32)
    small_w = (norm_mix, attn_out_gain, hgrn_lb_logits, hgrn_out_gain, norm_mlp, norm_final)
    small_m = (m_norm_mix, m_attn_out_gain, m_hgrn_lb_logits, m_hgrn_out_gain, m_norm_mlp, m_norm_final)
    small_v = (v_norm_mix, v_attn_out_gain, v_hgrn_lb_logits, v_hgrn_out_gain, v_norm_mlp, v_norm_final)
    small_g = (g_mix, g_attn, g_logits, g_hgrn, g_mlp, g_final)
    packs = [_pack_small(*t, zeros_row) for t in (small_w, small_g, small_m, small_v)]
    small_delta, small_new_m, small_new_v = [_unpack_small(p)[:6] for p in _adamw("adamw_small", *packs)]

    def ordered(small6, big4):
        mix, attn, lbl, hg, mlp, fin = small6
        return (mix, big4[0], attn, lbl, hg, big4[1], mlp, big4[2], big4[3], fin)

    return ((loss_row[0], dx[None]) + ordered(small_g, big_g) + ordered(small_delta, big_delta)
            + ordered(small_new_m, big_new_m) + ordered(small_new_v, big_new_v))
```

```python
import numpy as np
import jax
import jax.numpy as jnp
from jax import lax
from jax.experimental import pallas as pl
from jax.experimental.pallas import tpu as pltpu

F32 = jnp.float32
BF16 = jnp.bfloat16
MESH = pl.DeviceIdType.MESH

SEQ = 4096
D_MODEL = 1024
DEPTH = 2
ATTN_WIDTH = 512
HEAD_DIM = 64
HGRN_HEADS = 4
HGRN_DIM = 128
HGRN_WIDTH = 512
IN_W = 3584
MLP_HIDDEN = 4096
N_CHIPS = 4
N_DEV = 8
DILATIONS = (1, 4, 16)
SPAN = 128
ROPE_THETA = 10000.0
NORM_EPS = 1e-6
MASK_VALUE = -1e30
CHUNK = 128
ROW_TILE = 512
MM_TILE = 512
VMEM_LIMIT = 52 * 1024 * 1024

ADAM_LR = 0.001
ADAM_B1 = 0.9
ADAM_B2 = 0.999
ADAM_EPS = 1e-08
ADAM_WD = 0.01
ADAM_STEP = 10

PACK_ROWS = 16


def _params(n_axes):
    return pltpu.CompilerParams(dimension_semantics=("arbitrary",) * n_axes,
                                vmem_limit_bytes=VMEM_LIMIT)


def _dot(a, b):
    return jnp.dot(a.astype(BF16), b.astype(BF16), preferred_element_type=F32)


def _dot_nt(a, b):
    return lax.dot_general(a.astype(BF16), b.astype(BF16), (((1,), (1,)), ((), ())),
                           preferred_element_type=F32)


def _dot_tn(a, b):
    return lax.dot_general(a.astype(BF16), b.astype(BF16), (((0,), (0,)), ((), ())),
                           preferred_element_type=F32)


def _sigmoid(x):
    return 1.0 / (1.0 + jnp.exp(-x))


def _spec(shape, index_map):
    return pl.BlockSpec(shape, index_map)


def _resident(w):
    return pl.BlockSpec(w.shape, lambda i: (0, 0, 0), pipeline_mode=pl.Buffered(1))


def _mm_pieces(name, a, w, nt, tm, epilogue="none", extra=None, after=None):
    s = a.shape[0]
    pw = w.shape[1] if nt else w.shape[2]
    width = N_CHIPS * pw

    def body(a_ref, w_ref, *rest):
        e_ref = rest[0] if extra is not None else None
        outs = rest[-3:] if epilogue == "relu2" else rest[-1:]
        av = a_ref[...].astype(BF16)
        for j in range(N_CHIPS):
            cols = slice(j * pw, (j + 1) * pw)
            r = _dot_nt(av, w_ref[j]) if nt else _dot(av, w_ref[j])
            if epilogue == "relu2":
                relu = jnp.maximum(r, 0.0)
                r = relu * relu
                outs[1][:, cols] = relu.astype(BF16)
                outs[2][cols, :] = r.T.astype(BF16)
            elif epilogue == "relu2_grad":
                r = r * (2.0 * e_ref[:, cols].astype(F32))
            outs[0][:, cols] = r.astype(outs[0].dtype)

    row = lambda width_: _spec((tm, width_), lambda i: (i, 0))
    in_specs = [row(a.shape[1]), _resident(w)]
    args = [a, w]
    if extra is not None:
        in_specs.append(row(width))
        args.append(extra)
    if after is not None:
        in_specs.append(pl.BlockSpec(memory_space=pl.ANY))
        args.append(after)
    if epilogue == "relu2":
        out_specs = [row(width), row(width), _spec((width, tm), lambda i: (0, i))]
        out_shape = [jax.ShapeDtypeStruct((s, width), BF16)] * 2 + [jax.ShapeDtypeStruct((width, s), BF16)]
    else:
        out_specs = row(width)
        out_shape = jax.ShapeDtypeStruct((s, width), BF16 if epilogue == "relu2_grad" else F32)
    return pl.pallas_call(body, name=name, grid=(s // tm,), in_specs=in_specs, out_specs=out_specs,
                          out_shape=out_shape, compiler_params=_params(1))(*args)


def _mm_accum(name, a, w, nt, tm, resid, norm=None, after=None, next_gain=None, loss=None):
    pieces = list(a) if isinstance(a, (list, tuple)) else [a]
    n_a = len(pieces)
    s = pieces[0].shape[0]
    pk = w.shape[2] if nt else w.shape[1]
    d = w.shape[1] if nt else w.shape[2]

    def body(*refs):
        a_refs, w_ref, resid_ref, rest = refs[:n_a], refs[n_a], refs[n_a + 1], refs[n_a + 2:]
        av = a_refs[0][...] if n_a == 1 else jnp.concatenate([ref[...] for ref in a_refs], axis=1)
        r = None
        for j in range(N_CHIPS):
            piece = av[:, j * pk:(j + 1) * pk].astype(BF16)
            term = _dot_nt(piece, w_ref[j]) if nt else _dot(piece, w_ref[j])
            r = term if r is None else r + term
        if loss is not None:
            g_ref, t_ref = rest[:2]
            dx_ref, dxb_ref, dg_ref, loss_ref, acc = rest[-5:]
            i = pl.program_id(0)

            @pl.when(i == 0)
            def _():
                dg_ref[...] = jnp.zeros_like(dg_ref)
                acc[...] = jnp.zeros_like(acc)

            xv = r + resid_ref[...]
            g = g_ref[...]
            rs = lax.rsqrt(jnp.mean(xv * xv, axis=1, keepdims=True) + NORM_EPS)
            xhat = xv * rs
            err = xhat * g - t_ref[...]
            acc[...] += jnp.sum(err * err, axis=0, keepdims=True)
            dy = err * (1.0 / d)
            dyg = dy * g
            dx = rs * (dyg - xhat * jnp.mean(dyg * xhat, axis=1, keepdims=True))
            dx_ref[...] = dx
            dxb_ref[...] = dx.astype(BF16)
            dg_ref[...] += jnp.sum(dy * xhat, axis=0, keepdims=True)

            @pl.when(i == s // tm - 1)
            def _():
                total = jnp.sum(acc[...], axis=1, keepdims=True) * (0.5 / d)
                loss_ref[...] = jnp.broadcast_to(total, loss_ref.shape)

            return
        if norm is None and next_gain is None:
            rest[-1][...] = r + resid_ref[...]
            return
        if norm is None:
            g_ref = rest[0]
            x_out, h_out, ht_out = rest[-3:]
            xv = r + resid_ref[...]
            x_out[...] = xv
            h = xv * lax.rsqrt(jnp.mean(xv * xv, axis=1, keepdims=True) + NORM_EPS) * g_ref[...]
            h_out[...] = h.astype(BF16)
            ht_out[...] = h.T.astype(BF16)
            return
        x_ref, g_ref = rest[:2]
        dx_ref, dxb_ref, dg_ref = rest[-3:]

        @pl.when(pl.program_id(0) == 0)
        def _():
            dg_ref[...] = jnp.zeros_like(dg_ref)

        xv = x_ref[...]
        rs = lax.rsqrt(jnp.mean(xv * xv, axis=1, keepdims=True) + NORM_EPS)
        xhat = xv * rs
        rg = r * g_ref[...]
        dx = resid_ref[...] + rs * (rg - xhat * jnp.mean(rg * xhat, axis=1, keepdims=True))
        dx_ref[...] = dx
        dxb_ref[...] = dx.astype(BF16)
        dg_ref[...] += jnp.sum(r * xhat, axis=0, keepdims=True)

    row = lambda width: _spec((tm, width), lambda i: (i, 0))
    in_specs = [row(p.shape[1]) for p in pieces] + [_resident(w), row(d)]
    args = pieces + [w, resid]
    scratch = []
    if loss is not None:
        in_specs += [_spec((1, d), lambda i: (0, 0)), row(d)]
        args += list(loss)
        out_specs = [row(d), row(d), _spec((1, d), lambda i: (0, 0)), _spec((1, 128), lambda i: (0, 0))]
        out_shape = [jax.ShapeDtypeStruct((s, d), F32), jax.ShapeDtypeStruct((s, d), BF16),
                     jax.ShapeDtypeStruct((1, d), F32), jax.ShapeDtypeStruct((1, 128), F32)]
        scratch = [pltpu.VMEM((1, d), F32)]
    elif norm is None and next_gain is None:
        out_specs, out_shape = row(d), jax.ShapeDtypeStruct((s, d), F32)
    elif norm is None:
        in_specs.append(_spec((1, d), lambda i: (0, 0)))
        args.append(next_gain)
        out_specs = [row(d), row(d), _spec((d, tm), lambda i: (0, i))]
        out_shape = [jax.ShapeDtypeStruct((s, d), F32), jax.ShapeDtypeStruct((s, d), BF16),
                     jax.ShapeDtypeStruct((d, s), BF16)]
    else:
        in_specs += [row(d), _spec((1, d), lambda i: (0, 0))]
        args += list(norm)
        out_specs = [row(d), row(d), _spec((1, d), lambda i: (0, 0))]
        out_shape = [jax.ShapeDtypeStruct((s, d), F32), jax.ShapeDtypeStruct((s, d), BF16),
                     jax.ShapeDtypeStruct((1, d), F32)]
    if after is not None:
        in_specs.append(pl.BlockSpec(memory_space=pl.ANY))
        args.append(after)
    return pl.pallas_call(body, name=name, grid=(s // tm,), in_specs=in_specs, out_specs=out_specs,
                          out_shape=out_shape, scratch_shapes=scratch, compiler_params=_params(1))(*args)


def _mm_dw(name, a_t, b, by_cols, tk):
    pieces = list(b) if isinstance(b, (list, tuple)) else [b]
    n_b = len(pieces)
    m, s = a_t.shape
    n = sum(p.shape[1] for p in pieces)
    shape = (N_CHIPS, m, n // N_CHIPS) if by_cols else (N_CHIPS, m // N_CHIPS, n)
    n_steps = s // tk

    def body(a_ref, *rest):
        b_refs, o_ref, acc = rest[:n_b], rest[n_b], rest[-1]

        @pl.when(pl.program_id(0) == 0)
        def _():
            acc[...] = jnp.zeros_like(acc)

        bv = b_refs[0][...] if n_b == 1 else jnp.concatenate([ref[...] for ref in b_refs], axis=1)
        for j in range(N_CHIPS):
            if by_cols:
                acc[j] += _dot(a_ref[...], bv[:, j * shape[2]:(j + 1) * shape[2]])
            else:
                acc[j] += _dot(a_ref[j * shape[1]:(j + 1) * shape[1], :], bv)

        @pl.when(pl.program_id(0) == n_steps - 1)
        def _():
            o_ref[...] = acc[...].astype(BF16)

    return pl.pallas_call(
        body, name=name, grid=(n_steps,),
        in_specs=[_spec((m, tk), lambda k: (0, k))] + [_spec((tk, p.shape[1]), lambda k: (k, 0)) for p in pieces],
        out_specs=[_spec(shape, lambda k: (0, 0, 0)), ANY_SPEC],
        out_shape=[jax.ShapeDtypeStruct(shape, BF16),
                   jax.ShapeDtypeStruct((N_DEV, shape[1] // 2, shape[2]), BF16)],
        scratch_shapes=[pltpu.VMEM(shape, F32)],
        compiler_params=_params(1))(a_t, *pieces)


def _rms_fwd(name, x, gain):
    s, d = x.shape
    t = ROW_TILE

    def body(x_ref, g_ref, h_ref, ht_ref):
        xv = x_ref[...]
        r = lax.rsqrt(jnp.mean(xv * xv, axis=1, keepdims=True) + NORM_EPS)
        h = xv * r * g_ref[...]
        h_ref[...] = h.astype(BF16)
        ht_ref[...] = h.T.astype(BF16)

    return pl.pallas_call(
        body, name=name, grid=(s // t,),
        in_specs=[_spec((t, d), lambda i: (i, 0)), _spec((1, d), lambda i: (0, 0))],
        out_specs=[_spec((t, d), lambda i: (i, 0)), _spec((d, t), lambda i: (0, i))],
        out_shape=[jax.ShapeDtypeStruct((s, d), BF16), jax.ShapeDtypeStruct((d, s), BF16)],
        compiler_params=_params(1),
    )(x, gain)


def _rope_tables():
    half = HEAD_DIM // 2
    inv_freq = ROPE_THETA ** (-jnp.arange(half, dtype=F32) / half)
    ang = jnp.arange(SEQ, dtype=jnp.int32).astype(F32)[:, None] * inv_freq[None, :]
    cos, sin = jnp.cos(ang), jnp.sin(ang)
    cos_t = jnp.concatenate([cos, cos, cos, cos], axis=1)
    sin_t = jnp.concatenate([-sin, sin, -sin, sin], axis=1)
    return cos_t, sin_t


def _swap_halves(x):
    lane = lax.broadcasted_iota(jnp.int32, x.shape, 1)
    first = (lane % HEAD_DIM) < (HEAD_DIM // 2)
    return jnp.where(first, pltpu.roll(x, 128 - HEAD_DIM // 2, 1), pltpu.roll(x, HEAD_DIM // 2, 1))


def _permuted_specs(t, width):
    specs = [_spec((t, width), lambda i: (i, 0))]
    for d in DILATIONS[1:]:
        specs.append(_spec((d, t // d, width), lambda i: (0, i, 0)))
    return specs


def _permuted_shapes(width, dtype):
    shapes = [jax.ShapeDtypeStruct((SEQ, width), dtype)]
    for d in DILATIONS[1:]:
        shapes.append(jax.ShapeDtypeStruct((d, SEQ // d, width), dtype))
    return shapes


def _attn_prep(name, proj, cos_t, sin_t):
    t = ROW_TILE
    w = ATTN_WIDTH

    def body(q_ref, k_ref, v_ref, cos_ref, sin_ref, *rest):
        outs, scr = rest[:9], rest[9]
        cosv, sinv = cos_ref[...], sin_ref[...]
        for a, (src, roped, scale) in enumerate(((q_ref, True, HEAD_DIM ** -0.5),
                                                 (k_ref, True, 1.0), (v_ref, False, 1.0))):
            o1, o4, o16 = outs[3 * a:3 * a + 3]
            for cb in range(w // 128):
                cols = slice(cb * 128, (cb + 1) * 128)
                val = src[:, cols]
                if roped:
                    val = (val * cosv + _swap_halves(val) * sinv) * scale
                scr[...] = val
                o1[:, cols] = val.astype(BF16)
                for o_ref, d in ((o4, 4), (o16, 16)):
                    for r in range(d):
                        o_ref[r, :, cols] = scr[pl.ds(r, t // d, stride=d), :].astype(BF16)

    out_specs = _permuted_specs(t, w) * 3
    out_shape = _permuted_shapes(w, BF16) * 3
    outs = pl.pallas_call(
        body, name=name, grid=(SEQ // t,),
        in_specs=[_spec((t, w), lambda i: (i, 0)), _spec((t, w), lambda i: (i, 1)),
                  _spec((t, w), lambda i: (i, 2)),
                  _spec((t, 128), lambda i: (i, 0)), _spec((t, 128), lambda i: (i, 0))],
        out_specs=out_specs, out_shape=out_shape,
        scratch_shapes=[pltpu.VMEM((t, 128), F32)],
        compiler_params=_params(1),
    )(proj, proj, proj, cos_t, sin_t)
    q, k, v = outs[0:3], outs[3:6], outs[6:9]
    flat = lambda arr: arr.reshape(SEQ, w)
    return [(flat(q[p]), flat(k[p]), flat(v[p])) for p in range(3)]


def _band_masks():
    row = lax.broadcasted_iota(jnp.int32, (2 * SPAN, 2 * SPAN), 0) % SPAN
    col = lax.broadcasted_iota(jnp.int32, (2 * SPAN, 2 * SPAN), 1)
    is_prev = col < SPAN
    band = (is_prev & (col >= row)) | (~is_prev & (col - SPAN <= row))
    head0 = lax.broadcasted_iota(jnp.int32, (SPAN, 128), 1) < HEAD_DIM
    return band, head0


def _stack_heads(x, head0):
    zero = jnp.zeros_like(x)
    return jnp.concatenate([jnp.where(head0, x, zero), jnp.where(head0, zero, x)], axis=0)


def _for_each_block(block, seg_blocks):
    for b in range(SEQ // SPAN):
        block(b, b % seg_blocks == 0)


def _attn_fwd(name, q, k, v, seg_blocks):
    def body(q_ref, k_ref, v_ref, o_ref, lse_ref):
        band, head0 = _band_masks()

        def block(b, first):
            cur = pl.ds(b * SPAN, SPAN)
            qs = _stack_heads(q_ref[cur, :], head0)
            if first:
                kcat, vcat, ok = k_ref[cur, :], v_ref[cur, :], band[:, SPAN:]
            else:
                both = pl.ds((b - 1) * SPAN, 2 * SPAN)
                kcat, vcat, ok = k_ref[both, :], v_ref[both, :], band
            s = jnp.where(ok, _dot_nt(qs, kcat), MASK_VALUE)
            m = jnp.max(s, axis=1, keepdims=True)
            p = jnp.exp(s - m)
            l = jnp.sum(p, axis=1, keepdims=True)
            pv = _dot(p, vcat) * (1.0 / l)
            lse = m + jnp.log(l)
            o_ref[cur, :] = jnp.where(head0, pv[:SPAN], pv[SPAN:])
            lse_ref[cur, :] = jnp.where(head0, lse[:SPAN], lse[SPAN:])

        _for_each_block(block, seg_blocks)

    col = _spec((SEQ, 128), lambda j: (0, j))
    return pl.pallas_call(
        body, name=name, grid=(ATTN_WIDTH // 128,),
        in_specs=[col, col, col], out_specs=[col, col],
        out_shape=[jax.ShapeDtypeStruct((SEQ, ATTN_WIDTH), F32)] * 2,
        compiler_params=_params(1),
    )(q, k, v)


def _unpermute(dst, src_ref, d, cols):
    n = dst.shape[0] // d
    for r in range(d):
        dst[pl.ds(r, n, stride=d), :] = src_ref[r, :, cols].astype(dst.dtype)


def _attn_merge(name, outs, lses, gain):
    t = ROW_TILE
    w = ATTN_WIDTH

    def body(o1, o4, o16, l1, l4, l16, g_ref, an_ref, ant_ref, attn_ref, lse_ref, so4, so16, sl4, sl16):
        for cb in range(w // 128):
            cols = slice(cb * 128, (cb + 1) * 128)
            _unpermute(so4, o4, 4, cols)
            _unpermute(so16, o16, 16, cols)
            _unpermute(sl4, l4, 4, cols)
            _unpermute(sl16, l16, 16, cols)
            la, lb, lc = l1[:, cols], sl4[...], sl16[...]
            m = jnp.maximum(jnp.maximum(la, lb), lc)
            ea, eb, ec = jnp.exp(la - m), jnp.exp(lb - m), jnp.exp(lc - m)
            tot = ea + eb + ec
            attn_ref[:, cols] = (ea * o1[:, cols] + eb * so4[...] + ec * so16[...]) / tot
            lse_ref[:, cols] = m + jnp.log(tot)
        attn = attn_ref[...]
        r = lax.rsqrt(jnp.mean(attn * attn, axis=1, keepdims=True) + NORM_EPS)
        an = attn * r * g_ref[...]
        an_ref[...] = an.astype(BF16)
        ant_ref[...] = an.T.astype(BF16)

    views = lambda arrs: [arrs[0], arrs[1].reshape(4, SEQ // 4, w), arrs[2].reshape(16, SEQ // 16, w)]
    row = _spec((t, w), lambda i: (i, 0))
    return pl.pallas_call(
        body, name=name, grid=(SEQ // t,),
        in_specs=_permuted_specs(t, w) * 2 + [_spec((1, w), lambda i: (0, 0))],
        out_specs=[row, _spec((w, t), lambda i: (0, i)), row, row],
        out_shape=[jax.ShapeDtypeStruct((SEQ, 2 * w), BF16), jax.ShapeDtypeStruct((2 * w, SEQ), BF16),
                   jax.ShapeDtypeStruct((SEQ, w), F32), jax.ShapeDtypeStruct((SEQ, w), F32)],
        scratch_shapes=[pltpu.VMEM((t, 128), F32)] * 4,
        compiler_params=_params(1),
    )(*views(outs), *views(lses), gain)


def _head_sum_matrix():
    i = np.arange(ATTN_WIDTH)
    return jnp.asarray((i[:, None] // HEAD_DIM) == (i[None, :] // HEAD_DIM), dtype=F32)


def _attn_bwd_prep(name, d_an, attn, lse, gain, head_sum):
    t = ROW_TILE
    w = ATTN_WIDTH

    def body(dan_ref, attn_ref, lse_ref, g_ref, hs_ref, *rest):
        (do1, do4, do16, dl1, dl4, dl16, ls4, ls16, dg_ref), (sdo, sdl, sls) = rest[:9], rest[9:]

        @pl.when(pl.program_id(0) == 0)
        def _():
            dg_ref[...] = jnp.zeros_like(dg_ref)

        attn = attn_ref[...]
        dan = dan_ref[...]
        r = lax.rsqrt(jnp.mean(attn * attn, axis=1, keepdims=True) + NORM_EPS)
        xhat = attn * r
        dg_ref[...] += jnp.sum(dan * xhat, axis=0, keepdims=True)
        dang = dan * g_ref[...]
        d_o = r * (dang - xhat * jnp.mean(dang * xhat, axis=1, keepdims=True))
        delta = jnp.dot(d_o * attn, hs_ref[...], preferred_element_type=F32,
                        precision=lax.Precision.HIGHEST)
        do1[...] = d_o.astype(BF16)
        dl1[...] = delta
        for cb in range(w // 128):
            cols = slice(cb * 128, (cb + 1) * 128)
            sdo[...] = d_o[:, cols]
            sdl[...] = delta[:, cols]
            sls[...] = lse_ref[:, cols]
            for d, o_do, o_dl, o_ls in ((4, do4, dl4, ls4), (16, do16, dl16, ls16)):
                for rr in range(d):
                    rows = pl.ds(rr, t // d, stride=d)
                    o_do[rr, :, cols] = sdo[rows, :].astype(BF16)
                    o_dl[rr, :, cols] = sdl[rows, :]
                    o_ls[rr, :, cols] = sls[rows, :]

    row = _spec((t, w), lambda i: (i, 0))
    perm = _permuted_specs(t, w)
    outs = pl.pallas_call(
        body, name=name, grid=(SEQ // t,),
        in_specs=[row, row, row, _spec((1, w), lambda i: (0, 0)), _spec((w, w), lambda i: (0, 0))],
        out_specs=perm + perm + perm[1:] + [_spec((1, w), lambda i: (0, 0))],
        out_shape=(_permuted_shapes(w, BF16) + _permuted_shapes(w, F32) + _permuted_shapes(w, F32)[1:]
                   + [jax.ShapeDtypeStruct((1, w), F32)]),
        scratch_shapes=[pltpu.VMEM((t, 128), F32)] * 3,
        compiler_params=_params(1),
    )(d_an, attn, lse, gain, head_sum)
    flat = lambda arr: arr.reshape(SEQ, w)
    d_out = [flat(a) for a in outs[0:3]]
    delta = [flat(a) for a in outs[3:6]]
    lses = [lse, flat(outs[6]), flat(outs[7])]
    return d_out, delta, lses, outs[8]


def _attn_bwd(name, q, k, v, d_out, delta, lse, seg_blocks):
    def body(q_ref, k_ref, v_ref, do_ref, dl_ref, lse_ref, dq_ref, dk_out, dv_out, dk_ref, dv_ref):
        band, head0 = _band_masks()
        dk_ref[...] = jnp.zeros_like(dk_ref)
        dv_ref[...] = jnp.zeros_like(dv_ref)

        def per_head(x):
            return jnp.concatenate([x[:, 0:1], x[:, HEAD_DIM:HEAD_DIM + 1]], axis=0)

        def block(b, first):
            cur = pl.ds(b * SPAN, SPAN)
            qs = _stack_heads(q_ref[cur, :], head0)
            dos = _stack_heads(do_ref[cur, :], head0)
            if first:
                kcat, vcat, ok = k_ref[cur, :], v_ref[cur, :], band[:, SPAN:]
            else:
                both = pl.ds((b - 1) * SPAN, 2 * SPAN)
                kcat, vcat, ok = k_ref[both, :], v_ref[both, :], band
            p = jnp.where(ok, jnp.exp(_dot_nt(qs, kcat) - per_head(lse_ref[cur, :])), 0.0)
            ds = p * (_dot_nt(dos, vcat) - per_head(dl_ref[cur, :]))
            dq = _dot(ds, kcat)
            dq_ref[cur, :] = jnp.where(head0, dq[:SPAN], dq[SPAN:]).astype(BF16)
            dk = _dot_tn(ds, qs)
            dv = _dot_tn(p, dos)
            if first:
                dk_ref[cur, :] += dk
                dv_ref[cur, :] += dv
            else:
                dk_ref[both, :] += dk
                dv_ref[both, :] += dv

        _for_each_block(block, seg_blocks)
        dk_out[...] = dk_ref[...].astype(BF16)
        dv_out[...] = dv_ref[...].astype(BF16)

    col = _spec((SEQ, 128), lambda j: (0, j))
    return pl.pallas_call(
        body, name=name, grid=(ATTN_WIDTH // 128,),
        in_specs=[col] * 6, out_specs=[col] * 3,
        out_shape=[jax.ShapeDtypeStruct((SEQ, ATTN_WIDTH), BF16)] * 3,
        scratch_shapes=[pltpu.VMEM((SEQ, 128), F32)] * 2,
        compiler_params=_params(1),
    )(q, k, v, d_out, delta, lse)


def _attn_bwd_post(name, grads, cos_t, sin_t):
    t = ROW_TILE
    w = ATTN_WIDTH

    def body(*refs):
        ins, cos_ref, sin_ref, out_ref, s4, s16 = refs[:9], refs[9], refs[10], refs[11], refs[12], refs[13]
        cosv, sinv = cos_ref[...], sin_ref[...]
        for a in range(3):
            g1, g4, g16 = ins[a], ins[3 + a], ins[6 + a]
            for cb in range(w // 128):
                cols = slice(cb * 128, (cb + 1) * 128)
                _unpermute(s4, g4, 4, cols)
                _unpermute(s16, g16, 16, cols)
                val = g1[:, cols].astype(F32) + s4[...] + s16[...]
                if a < 2:
                    val = val * cosv + _swap_halves(val * sinv)
                if a == 0:
                    val = val * (HEAD_DIM ** -0.5)
                out_ref[:, a * w + cb * 128:a * w + (cb + 1) * 128] = val.astype(BF16)

    views = []
    for p, d in enumerate(DILATIONS):
        for a in range(3):
            views.append(grads[p][a] if d == 1 else grads[p][a].reshape(d, SEQ // d, w))
    perm = _permuted_specs(t, w)
    in_specs = [perm[0]] * 3 + [perm[1]] * 3 + [perm[2]] * 3
    return pl.pallas_call(
        body, name=name, grid=(SEQ // t,),
        in_specs=in_specs + [_spec((t, 128), lambda i: (i, 0))] * 2,
        out_specs=_spec((t, 3 * w), lambda i: (i, 0)),
        out_shape=jax.ShapeDtypeStruct((SEQ, 3 * w), BF16),
        scratch_shapes=[pltpu.VMEM((t, 128), F32)] * 2,
        compiler_params=_params(1),
    )(*views, cos_t, sin_t)


N_LEVELS = 7
HGRN_PAIR = 4


def _hgrn_consts():
    c = CHUNK
    i = np.arange(c)[:, None]
    s = np.arange(c)[None, :]
    blocks = [s <= i]
    for lv in range(N_LEVELS):
        bs = c >> lv
        h = bs // 2
        m = (i // bs) * bs + h - 1
        second = (i % bs) >= h
        blocks.append((second & (s > m) & (s <= i)) | (~second & (s > i) & (s <= m)))
    blocks.append(s > i)
    stack = np.concatenate(blocks, axis=0).astype(np.float32)
    twice = np.concatenate([stack, stack], axis=1)
    return jnp.asarray(twice, dtype=BF16), jnp.asarray(stack.T, dtype=BF16)


def _split(x):
    hi = x.astype(BF16)
    return hi, (x - hi.astype(F32)).astype(BF16)


def _exact_dot(m01, x):
    n = x.shape[1]
    full = jnp.dot(m01, jnp.concatenate(_split(x), axis=1), preferred_element_type=F32)
    return full[:, :n] + full[:, n:]


def _exact_dot_twice(m01_twice, x):
    return jnp.dot(m01_twice, jnp.concatenate(_split(x), axis=0), preferred_element_type=F32)


def _hgrn_gates(qh, z, lb):
    sq = _sigmoid(qh)
    q = qh * sq * (HGRN_DIM ** -0.5)
    sig = _sigmoid(z)
    sigm = _sigmoid(-z)
    f = lb + (1.0 - lb) * sig
    k = (1.0 - lb) * sigm
    return q, k, f, sq, sig, sigm


def _level_masks(lv):
    row = lax.broadcasted_iota(jnp.int32, (CHUNK, CHUNK), 0)
    col = lax.broadcasted_iota(jnp.int32, (CHUNK, CHUNK), 1)
    shift = N_LEVELS - lv
    half = CHUNK >> (lv + 1)
    second = (row & half) != 0
    second_col = (col & half) != 0
    same = (row >> shift) == (col >> shift)
    return second, same & second & ~second_col, same & (second != second_col)


def _hgrn_fwd(name, proj, lb, gain, stack, mixed, mixed_t):
    t = ROW_TILE
    per = t // CHUNK
    n_rb = SEQ // t
    n_chunks = SEQ // CHUNK
    col0 = 3 * ATTN_WIDTH // 128
    pair_w = HGRN_PAIR * HGRN_DIM

    def body(q_ref, f_ref, i_ref, g_ref, lb_ref, gain_ref, stack_ref, mixed_in, mixed_t_in,
             rec_ref, rect_ref, o_ref, st_out, a_out, st):
        del mixed_in, mixed_t_in

        @pl.when(pl.program_id(1) == 0)
        def _():
            st[...] = jnp.zeros_like(st)

        row = lax.broadcasted_iota(jnp.int32, (CHUNK, CHUNK), 0)
        col = lax.broadcasted_iota(jnp.int32, (CHUNK, CHUNK), 1)
        masks = [_level_masks(lv) for lv in range(N_LEVELS)]
        for c, hh in [(c, hh) for c in range(per) for hh in range(HGRN_PAIR)]:
            rows = slice(c * CHUNK, (c + 1) * CHUNK)
            lanes = slice(hh * HGRN_DIM, (hh + 1) * HGRN_DIM)
            lbv = lb_ref[hh]
            qh, z, v, gh = q_ref[rows, lanes], f_ref[rows, lanes], i_ref[rows, lanes], g_ref[rows, lanes]
            q, k, f, _, _, _ = _hgrn_gates(qh, z, lbv)
            dec = _exact_dot_twice(stack_ref[...], jnp.log(f))
            g = dec[0:CHUNK]
            to_end = dec[(N_LEVELS + 1) * CHUNK:(N_LEVELS + 2) * CHUNK]
            a = jnp.where(row == col, jnp.sum(q * k, axis=1, keepdims=True), 0.0)
            for lv in range(N_LEVELS):
                second, square, _ = masks[lv]
                t = jnp.where(second, q, k) * jnp.exp(dec[(lv + 1) * CHUNK:(lv + 2) * CHUNK])
                a = a + jnp.where(square, _dot_nt(t, t), 0.0)
            st_prev = st[hh]
            st_out[hh, c] = st_prev
            a_out[hh, c] = a
            o = _dot(a, v) + _dot_nt(q * jnp.exp(g), st_prev)
            k_end = k * jnp.exp(to_end)
            st[hh] = st_prev * jnp.exp(g[CHUNK - 1:CHUNK, :]) + _dot(v.T, k_end)
            o_ref[rows, lanes] = o
            r = lax.rsqrt(jnp.mean(o * o, axis=1, keepdims=True) + NORM_EPS)
            rec = o * r * gain_ref[...] * (gh * _sigmoid(gh))
            rec_ref[rows, lanes] = rec.astype(BF16)
            rect_ref[lanes, rows] = rec.T.astype(BF16)

    def col_spec(tt):
        return _spec((t, pair_w), lambda h, rb: (rb, (col0 + HGRN_HEADS * tt) // HGRN_PAIR + h))

    chunk_spec = _spec((HGRN_PAIR, per, CHUNK, CHUNK), lambda h, rb: (h, rb, 0, 0))
    return pl.pallas_call(
        body, name=name, grid=(HGRN_HEADS // HGRN_PAIR, n_rb),
        in_specs=[col_spec(0), col_spec(1), col_spec(2), col_spec(3),
                  _spec((HGRN_PAIR, 1, HGRN_DIM), lambda h, rb: (h, 0, 0)),
                  _spec((1, HGRN_DIM), lambda h, rb: (0, 0)),
                  _spec(stack.shape, lambda h, rb: (0, 0)), ANY_SPEC, ANY_SPEC],
        out_specs=[_spec((t, pair_w), lambda h, rb: (rb, ATTN_WIDTH // pair_w + h)),
                   _spec((pair_w, t), lambda h, rb: (ATTN_WIDTH // pair_w + h, rb)),
                   _spec((t, pair_w), lambda h, rb: (rb, h)),
                   chunk_spec, chunk_spec],
        out_shape=[jax.ShapeDtypeStruct(mixed.shape, BF16),
                   jax.ShapeDtypeStruct(mixed_t.shape, BF16),
                   jax.ShapeDtypeStruct((SEQ, HGRN_WIDTH), F32),
                   jax.ShapeDtypeStruct((HGRN_HEADS, n_chunks, CHUNK, CHUNK), F32),
                   jax.ShapeDtypeStruct((HGRN_HEADS, n_chunks, CHUNK, CHUNK), F32)],
        scratch_shapes=[pltpu.VMEM((HGRN_PAIR, CHUNK, CHUNK), F32)],
        input_output_aliases={7: 0, 8: 1},
        compiler_params=_params(2),
    )(proj, proj, proj, proj, lb, gain, stack, mixed, mixed_t)


def _hgrn_bwd(name, proj, d_rec, o_pre, states, scores, lb, gain, stack, stack_t):
    t = ROW_TILE
    per = t // CHUNK
    n_rb = SEQ // t
    col0 = 3 * ATTN_WIDTH // 128
    pair_w = HGRN_PAIR * HGRN_DIM

    def body(q_ref, f_ref, i_ref, g_ref, drec_ref, o_ref, st_ref, a_ref, lb_ref, gain_ref,
             stack_ref, stack_t_ref, dq_ref, df_ref, di_ref, dg_ref, dlb_ref, dgain_ref, dst):
        @pl.when(pl.program_id(1) == 0)
        def _():
            dst[...] = jnp.zeros_like(dst)
            dlb_ref[...] = jnp.zeros_like(dlb_ref)
            dgain_ref[...] = jnp.zeros_like(dgain_ref)

        gain_v = gain_ref[...]
        row = lax.broadcasted_iota(jnp.int32, (CHUNK, CHUNK), 0)
        col = lax.broadcasted_iota(jnp.int32, (CHUNK, CHUNK), 1)
        masks = [_level_masks(lv) for lv in range(N_LEVELS)]
        for c, hh in [(c, hh) for c in reversed(range(per)) for hh in range(HGRN_PAIR)]:
            rows = slice(c * CHUNK, (c + 1) * CHUNK)
            lanes = slice(hh * HGRN_DIM, (hh + 1) * HGRN_DIM)
            lbv = lb_ref[hh]
            qh, z, v, gh = q_ref[rows, lanes], f_ref[rows, lanes], i_ref[rows, lanes], g_ref[rows, lanes]
            q, k, f, sq, sig, sigm = _hgrn_gates(qh, z, lbv)
            dec = _exact_dot_twice(stack_ref[...], jnp.log(f))
            g = dec[0:CHUNK]
            to_end = dec[(N_LEVELS + 1) * CHUNK:(N_LEVELS + 2) * CHUNK]
            e_g = jnp.exp(g)
            e_end = jnp.exp(to_end)
            e_last = jnp.exp(g[CHUNK - 1:CHUNK, :])
            q_in = q * e_g
            k_end = k * e_end
            st_prev = st_ref[hh, c]
            a = a_ref[hh, c]
            dst_new = dst[hh]

            o = o_ref[rows, lanes]
            drec = drec_ref[rows, lanes]
            sg = _sigmoid(gh)
            r = lax.rsqrt(jnp.mean(o * o, axis=1, keepdims=True) + NORM_EPS)
            ohat = o * r
            d_gh = drec * (ohat * gain_v) * (sg * (1.0 + gh * (1.0 - sg)))
            d_on = drec * (gh * sg)
            dgain_ref[hh] += jnp.sum(d_on * ohat, axis=0, keepdims=True)
            d_ohat = d_on * gain_v
            d_o = r * (d_ohat - ohat * jnp.mean(d_ohat * ohat, axis=1, keepdims=True))

            d_sym = jnp.where(row >= col, _dot_nt(d_o, v), _dot_nt(v, d_o))
            d_v = _dot(a.T, d_o) + _dot_nt(k_end, dst_new)
            d_q_in = _dot(d_o, st_prev)
            d_k_end = _dot(v, dst_new)
            d_q = d_q_in * e_g
            d_k = d_k_end * e_end
            diag = jnp.sum(d_o * v, axis=1, keepdims=True)
            d_q = d_q + diag * k
            d_k = d_k + diag * q
            d_dec = [q_in * d_q_in]
            d_both, d_second = None, None
            for lv in range(N_LEVELS):
                e = jnp.exp(dec[(lv + 1) * CHUNK:(lv + 2) * CHUNK])
                second, _, mirrored = masks[lv]
                t = jnp.where(second, q, k) * e
                d_t = _dot(jnp.where(mirrored, d_sym, 0.0), t)
                d_te = d_t * e
                d_both = d_te if d_both is None else d_both + d_te
                d_second = jnp.where(second, d_te, 0.0) if d_second is None else d_second + jnp.where(second, d_te, 0.0)
                d_dec.append(t * d_t)
            d_q = d_q + d_second
            d_k = d_k + (d_both - d_second)
            d_dec.append(k_end * d_k_end)
            flux = jnp.sum(dst_new * st_prev, axis=0, keepdims=True) * e_last
            d_lf = _exact_dot(stack_t_ref[...], jnp.concatenate(d_dec, axis=0)) + flux
            dst[hh] = dst_new * e_last + _dot(d_o.T, q_in)

            d_f = d_lf / f - d_k
            dlb_ref[hh] += jnp.sum(d_f * sigm, axis=0, keepdims=True)
            dq_ref[rows, lanes] = (d_q * (HGRN_DIM ** -0.5) * (sq * (1.0 + qh * (1.0 - sq)))).astype(BF16)
            df_ref[rows, lanes] = (d_f * (1.0 - lbv) * sig * sigm).astype(BF16)
            di_ref[rows, lanes] = d_v.astype(BF16)
            dg_ref[rows, lanes] = d_gh.astype(BF16)

    last = n_rb - 1

    def col_spec(tt):
        return _spec((t, pair_w), lambda h, rb: (last - rb, (col0 + HGRN_HEADS * tt) // HGRN_PAIR + h))

    head_col = _spec((t, pair_w), lambda h, rb: (last - rb, h))
    rec_col0 = (d_rec.shape[1] - HGRN_WIDTH) // pair_w
    d_rec_col = _spec((t, pair_w), lambda h, rb: (last - rb, rec_col0 + h))
    chunk_spec = _spec((HGRN_PAIR, per, CHUNK, CHUNK), lambda h, rb: (h, last - rb, 0, 0))
    vec_spec = _spec((HGRN_PAIR, 1, HGRN_DIM), lambda h, rb: (h, 0, 0))
    outs = pl.pallas_call(
        body, name=name, grid=(HGRN_HEADS // HGRN_PAIR, n_rb),
        in_specs=[col_spec(0), col_spec(1), col_spec(2), col_spec(3), d_rec_col, head_col,
                  chunk_spec, chunk_spec, vec_spec,
                  _spec((1, HGRN_DIM), lambda h, rb: (0, 0)),
                  _spec(stack.shape, lambda h, rb: (0, 0)), _spec(stack_t.shape, lambda h, rb: (0, 0))],
        out_specs=[head_col] * 4 + [vec_spec, vec_spec],
        out_shape=[jax.ShapeDtypeStruct((SEQ, HGRN_WIDTH), BF16)] * 4
                  + [jax.ShapeDtypeStruct((HGRN_HEADS, 1, HGRN_DIM), F32)] * 2,
        scratch_shapes=[pltpu.VMEM((HGRN_PAIR, CHUNK, CHUNK), F32)],
        compiler_params=_params(2),
    )(proj, proj, proj, proj, d_rec, o_pre, states, scores, lb, gain, stack, stack_t)
    return outs


ANY_SPEC = pl.BlockSpec(memory_space=pl.ANY)


def _my_place():
    return lax.axis_index("x"), lax.axis_index("y"), lax.axis_index("c")


def _other_chips(x, y):
    return [(1 - x, y), (x, 1 - y), (1 - x, 1 - y)]


def _remote(src, dst, send_sem, recv_sem, device):
    return pltpu.make_async_remote_copy(src_ref=src, dst_ref=dst, send_sem=send_sem, recv_sem=recv_sem,
                                        device_id=device, device_id_type=MESH)


def _staged_copies(srcs, dsts, stage, sems):
    loads = [pltpu.make_async_copy(srcs[i], stage[i], sems.at[i]) for i in range(len(srcs))]
    for cp in loads:
        cp.start()
    stores = []
    for i, cp in enumerate(loads):
        cp.wait()
        stores.append(pltpu.make_async_copy(stage[i], dsts[i], sems.at[i]))
        stores[-1].start()
    return stores


def _gather_weights(name, shards):
    n = len(shards)

    def body(*refs):
        ins, outs = refs[:n], refs[n:2 * n]
        ici_send, ici_recv, d2d_send, d2d_recv, local_sems = refs[2 * n:2 * n + 5]
        stage = refs[2 * n + 5:]
        x, y, c = _my_place()
        me = 2 * x + y
        chips = _other_chips(x, y)

        def half(i, which):
            h = ins[i].shape[0] // 2
            return pl.ds(which * h, h)

        sends = []
        for i in range(n):
            for j, (px, py) in enumerate(chips):
                sends.append(_remote(ins[i].at[half(i, c), :], outs[i].at[me, half(i, c), :],
                                     ici_send.at[3 * i + j], ici_recv.at[3 * i + j], (px, py, c)))
        for cp in sends:
            cp.start()
        local = _staged_copies(ins, [outs[i].at[me] for i in range(n)], stage, local_sems)
        for i in range(n):
            for j, (px, py) in enumerate(chips):
                landed = outs[i].at[2 * px + py, half(i, c), :]
                _remote(landed, landed, ici_send.at[3 * i + j], ici_recv.at[3 * i + j], (px, py, c)).wait_recv()
                forward = _remote(landed, landed, d2d_send.at[3 * i + j], d2d_recv.at[3 * i + j], (x, y, 1 - c))
                forward.start()
                sends.append(forward)
        for i in range(n):
            for j, (px, py) in enumerate(chips):
                other = outs[i].at[2 * px + py, half(i, 1 - c), :]
                _remote(other, other, d2d_send.at[3 * i + j], d2d_recv.at[3 * i + j], (x, y, 1 - c)).wait_recv()
        for cp in sends:
            cp.wait_send()
        for cp in local:
            cp.wait()

    return pl.pallas_call(
        body, name=name, in_specs=[ANY_SPEC] * n, out_specs=[ANY_SPEC] * n,
        out_shape=[jax.ShapeDtypeStruct((N_CHIPS,) + s.shape, s.dtype) for s in shards],
        scratch_shapes=([pltpu.SemaphoreType.DMA((3 * n,))] * 4 + [pltpu.SemaphoreType.DMA((n,))]
                        + [pltpu.VMEM(s.shape, s.dtype) for s in shards]),
        compiler_params=pltpu.CompilerParams(vmem_limit_bytes=VMEM_LIMIT),
    )(*shards)


HBM_SPEC = pl.BlockSpec(memory_space=pltpu.HBM)
SEM_SPEC = pl.BlockSpec(memory_space=pltpu.SEMAPHORE)
SPLIT_PARAMS = pltpu.CompilerParams(has_side_effects=pltpu.SideEffectType.DATAFLOW_SIDE_EFFECTING)
N_PEERS = {"gather": N_CHIPS - 1, "scatter": N_DEV - 1}


def _split_copies(ins, lands, send_sems, recv_sems, kind):
    x, y, c = _my_place()
    pairs = []
    for i in range(len(ins)):
        if kind == "gather":
            me = 2 * x + y
            for j, (px, py) in enumerate(_other_chips(x, y)):
                sems = (send_sems.at[3 * i + j], recv_sems.at[3 * i + j], (px, py, c))
                pairs.append((_remote(ins[i], lands[i].at[me], *sems),
                              _remote(ins[i], lands[i].at[2 * px + py], *sems)))
        else:
            me = 4 * x + 2 * y + c
            h = ins[i].shape[1] // 2
            for k in range(1, N_DEV):
                px, py, pc = (x + (k >> 2)) % 2, (y + ((k >> 1) & 1)) % 2, (c + (k & 1)) % 2
                src = ins[i].at[2 * px + py, pl.ds(pc * h, h), :]
                sems = (send_sems.at[7 * i + k - 1], recv_sems.at[7 * i + k - 1], (px, py, pc))
                pairs.append((_remote(src, lands[i].at[me], *sems),
                              _remote(src, lands[i].at[4 * px + 2 * py + pc], *sems)))
    return pairs


def _exchange_start(name, srcs, lands, kind, after=None):
    n = len(srcs)
    n_sems = N_PEERS[kind] * n
    extra = [] if after is None else [after]

    def body(*refs):
        ins, land_refs = refs[:n], refs[n:2 * n]
        send_sems, recv_sems = refs[2 * n + len(extra):2 * n + len(extra) + 2]
        token = refs[-1]
        for send, _ in _split_copies(ins, land_refs, send_sems, recv_sems, kind):
            send.start()
        token[...] = jnp.zeros_like(token)

    arrays = list(srcs) + list(lands)
    outs = pl.pallas_call(
        body, name=name,
        in_specs=[HBM_SPEC] * (2 * n) + [ANY_SPEC] * len(extra),
        out_shape=([pltpu.SemaphoreType.DMA((n_sems,))] * 2 + [pltpu.HBM(a.shape, a.dtype) for a in arrays]
                   + [jax.ShapeDtypeStruct((8, 128), F32)]),
        out_specs=[SEM_SPEC] * 2 + [HBM_SPEC] * (2 * n) + [pl.BlockSpec(memory_space=pltpu.VMEM)],
        input_output_aliases={i: 2 + i for i in range(2 * n)},
        compiler_params=SPLIT_PARAMS,
    )(*[pltpu.with_memory_space_constraint(a, pltpu.HBM) for a in arrays], *extra)
    return outs[:2], outs[2:2 + 2 * n], outs[-1]


def _exchange_wait(name, sems, passed, kind, after):
    n = len(passed) // 2

    def body(*refs):
        ins, land_refs = refs[:n], refs[n:2 * n]
        send_sems, recv_sems = refs[2 * n:2 * n + 2]
        for send, arrive in _split_copies(ins, land_refs, send_sems, recv_sems, kind):
            send.wait_send()
            arrive.wait_recv()

    outs = pl.pallas_call(
        body, name=name,
        in_specs=[HBM_SPEC] * (2 * n) + [SEM_SPEC] * 2 + [ANY_SPEC],
        out_shape=[pltpu.HBM(a.shape, a.dtype) for a in passed],
        out_specs=[HBM_SPEC] * (2 * n),
        input_output_aliases={i: i for i in range(2 * n)},
        compiler_params=SPLIT_PARAMS,
    )(*passed, *sems, after)
    return outs[:n], outs[n:]


def _own_slot(name, own, me):
    r, cc = own.shape
    th = min(r, 512)

    def body(me_ref, x_ref, o_ref):
        del me_ref
        o_ref[...] = x_ref[...]

    grid_spec = pltpu.PrefetchScalarGridSpec(
        num_scalar_prefetch=1, grid=(r // th,),
        in_specs=[pl.BlockSpec((th, cc), lambda i, me_ref: (i, 0))],
        out_specs=pl.BlockSpec((None, th, cc), lambda i, me_ref: (me_ref[0], i, 0)))
    return pl.pallas_call(
        body, name=name, grid_spec=grid_spec,
        out_shape=jax.ShapeDtypeStruct((N_CHIPS, r, cc), own.dtype), compiler_params=_params(1),
    )(me, own)


def _sum_devices(name, landed, own, place):
    n_dev, h, cc = landed.shape
    th = min(h, 256)
    nb = h // th

    def body(place_ref, l_ref, own_ref, o_ref):
        total = None
        for d in range(n_dev):
            piece = jnp.where(place_ref[0] == d, own_ref[...], l_ref[d]).astype(F32)
            total = piece if total is None else total + piece
        o_ref[...] = total

    grid_spec = pltpu.PrefetchScalarGridSpec(
        num_scalar_prefetch=1, grid=(nb,),
        in_specs=[pl.BlockSpec((n_dev, th, cc), lambda i, p: (0, i, 0)),
                  pl.BlockSpec((None, th, cc), lambda i, p: (p[1], p[2] * nb + i, 0))],
        out_specs=pl.BlockSpec((th, cc), lambda i, p: (i, 0)))
    return pl.pallas_call(
        body, name=name, grid_spec=grid_spec,
        out_shape=jax.ShapeDtypeStruct((h, cc), F32), compiler_params=_params(1),
    )(place, landed, own)


def _share_halves(name, halves):
    flat = [t for per_weight in halves for t in per_weight]
    n = len(flat)
    n_w = len(halves)

    def body(*refs):
        ins, outs = refs[:n], refs[n:n + n_w]
        send_sems, recv_sems, local_sems = refs[n + n_w:n + n_w + 3]
        stage = refs[n + n_w + 3:]
        x, y, c = _my_place()
        sends, own = [], []
        for i in range(n):
            w, l = divmod(i, DEPTH)
            h = ins[i].shape[0]
            own.append(outs[w].at[l, pl.ds(c * h, h), :])
            sends.append(_remote(ins[i], own[i], send_sems.at[i], recv_sems.at[i], (x, y, 1 - c)))
        for cp in sends:
            cp.start()
        local = _staged_copies(ins, own, stage, local_sems)
        for i in range(n):
            w, l = divmod(i, DEPTH)
            h = ins[i].shape[0]
            _remote(ins[i], outs[w].at[l, pl.ds((1 - c) * h, h), :], send_sems.at[i], recv_sems.at[i],
                    (x, y, 1 - c)).wait_recv()
        for cp in sends:
            cp.wait_send()
        for cp in local:
            cp.wait()

    return pl.pallas_call(
        body, name=name, in_specs=[ANY_SPEC] * n, out_specs=[ANY_SPEC] * n_w,
        out_shape=[jax.ShapeDtypeStruct((DEPTH, 2 * per_weight[0].shape[0], per_weight[0].shape[1]), F32)
                   for per_weight in halves],
        scratch_shapes=([pltpu.SemaphoreType.DMA((n,))] * 3 + [pltpu.VMEM(t.shape, t.dtype) for t in flat]),
        compiler_params=pltpu.CompilerParams(vmem_limit_bytes=VMEM_LIMIT),
    )(*flat)


def _all_reduce_small(pack, after):
    def body(p_ref, after_ref, o_ref, recv, send_sems, recv_sems):
        del after_ref
        x, y, c = _my_place()
        me = 4 * x + 2 * y + c
        recv[me] = p_ref[...]
        peers = []
        for k in range(1, N_DEV):
            px, py, pc = (x + (k >> 2)) % 2, (y + ((k >> 1) & 1)) % 2, (c + (k & 1)) % 2
            peers.append((px, py, pc))
        sends = [_remote(p_ref, recv.at[me], send_sems.at[k], recv_sems.at[k], peer)
                 for k, peer in enumerate(peers)]
        for cp in sends:
            cp.start()
        for k, (px, py, pc) in enumerate(peers):
            _remote(p_ref, recv.at[4 * px + 2 * py + pc], send_sems.at[k], recv_sems.at[k],
                    (px, py, pc)).wait_recv()
        for cp in sends:
            cp.wait_send()
        total = recv[0]
        for d in range(1, N_DEV):
            total = total + recv[d]
        o_ref[...] = total

    vmem = pl.BlockSpec(memory_space=pltpu.VMEM)
    return pl.pallas_call(
        body, name="all_reduce_small", in_specs=[vmem, ANY_SPEC], out_specs=vmem,
        out_shape=jax.ShapeDtypeStruct(pack.shape, F32),
        scratch_shapes=[pltpu.VMEM((N_DEV,) + pack.shape, F32),
                        pltpu.SemaphoreType.DMA((N_DEV - 1,)), pltpu.SemaphoreType.DMA((N_DEV - 1,))],
    )(pack, after)


def _adamw(name, w, g, m, v):
    r, cc = w.shape
    th = min(r, 256)

    def body(w_ref, g_ref, m_ref, v_ref, d_ref, m_out, v_out):
        gv = g_ref[...]
        m2 = ADAM_B1 * m_ref[...] + (1.0 - ADAM_B1) * gv
        v2 = ADAM_B2 * v_ref[...] + (1.0 - ADAM_B2) * (gv * gv)
        m_hat = m2 / (1.0 - ADAM_B1 ** ADAM_STEP)
        v_hat = v2 / (1.0 - ADAM_B2 ** ADAM_STEP)
        d_ref[...] = -ADAM_LR * (m_hat / (jnp.sqrt(v_hat) + ADAM_EPS) + ADAM_WD * w_ref[...])
        m_out[...] = m2
        v_out[...] = v2

    tile = _spec((th, cc), lambda i: (i, 0))
    return pl.pallas_call(
        body, name=name, grid=(r // th,), in_specs=[tile] * 4, out_specs=[tile] * 3,
        out_shape=[jax.ShapeDtypeStruct((r, cc), F32)] * 3, compiler_params=_params(1),
    )(w, g, m, v)


def _lower_bounds(lb_logits):
    p = jax.nn.softmax(lb_logits.astype(F32), axis=0)
    return jnp.cumsum(p, axis=0) - p[0]


def _layer_forward(l, stream, small, weights, consts, next_gain=None, loss=None, after=None):
    win, rest = weights
    cos_t, sin_t, stack, _, _ = consts
    tm = MM_TILE
    x_in, h, h_t = stream
    saved = {"x_in": x_in}

    proj = _mm_pieces(f"proj{l}", h, win, False, 2 * tm, after=after)
    saved.update(h_t=h_t, proj=proj)

    qkv = _attn_prep(f"attn_prep{l}", proj, cos_t, sin_t)
    outs, lses = [], []
    for p, d in enumerate(DILATIONS):
        o, lse = _attn_fwd(f"attn_fwd{l}_{d}", *qkv[p], SEQ // d // SPAN)
        outs.append(o)
        lses.append(lse)
    mixed, mixed_t, attn, lse = _attn_merge(f"attn_merge{l}", outs, lses, small["attn_out_gain"][l][None, :])
    saved.update(qkv=qkv, attn=attn, lse=lse)

    lb3 = small["lower"][l].reshape(HGRN_HEADS, 1, HGRN_DIM)
    mixed, mixed_t, o_pre, states, scores = _hgrn_fwd(f"hgrn_fwd{l}", proj, lb3, small["hgrn_out_gain"][l][None, :],
                                                      stack, mixed, mixed_t)
    wo, wu, wd = rest(mixed)
    saved.update(mixed_t=mixed_t, o_pre=o_pre, states=states, scores=scores, lb3=lb3, weights=(win, wo, wu, wd))

    x_mid, h2, h2_t = _mm_accum(f"out_proj{l}", mixed, wo, False, 2 * tm, x_in,
                                next_gain=small["norm_mlp"][l][None, :])
    saved["x_mid"] = x_mid

    a, relu_u, a_t = _mm_pieces(f"up{l}", h2, wu, False, tm, epilogue="relu2")
    new_stream = tuple(_mm_accum(f"down{l}", a, wd, False, tm, x_mid, next_gain=next_gain, loss=loss))
    saved.update(h2_t=h2_t, relu_u=relu_u, a_t=a_t)
    return new_stream, saved


def _layer_backward(l, dx, saved, small, consts, on_grads, after=None):
    win, wo, wu, wd = saved["weights"]
    cos_t, sin_t, stack, stack_t, head_sum = consts
    tm = MM_TILE

    dx, dx_b = dx
    du = _mm_pieces(f"d_u{l}", dx_b, wd, True, 2 * tm, epilogue="relu2_grad", extra=saved["relu_u"], after=after)
    d_wd = _mm_dw(f"d_wdown{l}", saved["a_t"], dx_b, False, tm)
    dxm, dxm_b, dg_mlp = _mm_accum(f"d_h2_{l}", du, wu, True, tm, dx,
                                   norm=(saved["x_mid"], small["norm_mlp"][l][None, :]))
    d_wu = _mm_dw(f"d_wup{l}", saved["h2_t"], du, True, tm)
    d_wo = _mm_dw(f"d_wout{l}", saved["mixed_t"], dxm_b, False, tm)
    after_early = on_grads(l, "early", (d_wu, d_wd, d_wo))

    d_mixed = _mm_pieces(f"d_mixed{l}", dxm_b, wo, True, 2 * tm, after=after_early)
    d_rec = d_mixed

    d_out, delta, lses, dg_attn = _attn_bwd_prep(f"attn_bwd_prep{l}", d_mixed, saved["attn"], saved["lse"],
                                                 small["attn_out_gain"][l][None, :], head_sum)
    grads = []
    for p, d in enumerate(DILATIONS):
        grads.append(_attn_bwd(f"attn_bwd{l}_{d}", *saved["qkv"][p], d_out[p], delta[p], lses[p],
                               SEQ // d // SPAN))
    dp_attn = _attn_bwd_post(f"attn_bwd_post{l}", grads, cos_t, sin_t)

    dq_h, df_h, di_h, dg_h, d_lower, dg_hgrn = _hgrn_bwd(
        f"hgrn_bwd{l}", saved["proj"], d_rec, saved["o_pre"], saved["states"], saved["scores"],
        saved["lb3"], small["hgrn_out_gain"][l][None, :], stack, stack_t)
    dproj = [dp_attn, dq_h, df_h, di_h, dg_h]

    d_win = _mm_dw(f"d_win{l}", saved["h_t"], dproj, True, tm)
    after_last = on_grads(l, "last", (d_win,))
    dx_in, dx_in_b, dg_mix = _mm_accum(f"d_h{l}", dproj, win, True, tm, dxm,
                                       norm=(saved["x_in"], small["norm_mix"][l][None, :]), after=after_last)

    small_grads = {"norm_mix": dg_mix[0], "attn_out_gain": dg_attn[0],
                   "lower": d_lower.reshape(HGRN_WIDTH),
                   "hgrn_out_gain": jnp.sum(dg_hgrn, axis=0).reshape(HGRN_DIM), "norm_mlp": dg_mlp[0]}
    return (dx_in, dx_in_b), after_last, small_grads


def _local_step(xs, target, small, get_weights, on_grads):
    consts = _rope_tables() + _hgrn_consts() + (_head_sum_matrix(),)
    stream = (xs,) + tuple(_rms_fwd("norm_mix0", xs, small["norm_mix"][0][None, :]))
    saved = []
    for l in range(DEPTH):
        w, after = get_weights(l, stream[0])
        if l + 1 < DEPTH:
            stream, s = _layer_forward(l, stream, small, w, consts, next_gain=small["norm_mix"][l + 1][None, :],
                                       after=after)
        else:
            stream, s = _layer_forward(l, stream, small, w, consts, loss=(small["norm_final"][None, :], target),
                                       after=after)
        saved.append(s)
    dx_f, dx_b, dg_final, loss = stream
    dx = (dx_f, dx_b)
    small_grads = [None] * DEPTH
    after = None
    for l in reversed(range(DEPTH)):
        dx, after, small_grads[l] = _layer_backward(l, dx, saved[l], small, consts, on_grads, after=after)
    return loss, dx[0], dg_final[0], small_grads


def _pack_small(norm_mix, attn_out_gain, lb, hgrn_out_gain, norm_mlp, norm_final, last_row):
    rows = [norm_mix, attn_out_gain.reshape(1, D_MODEL), lb.reshape(1, D_MODEL),
            jnp.pad(hgrn_out_gain.reshape(1, DEPTH * HGRN_DIM), ((0, 0), (0, D_MODEL - DEPTH * HGRN_DIM))),
            norm_mlp, norm_final.reshape(1, D_MODEL), last_row.reshape(1, D_MODEL)]
    pack = jnp.concatenate(rows, axis=0)
    return jnp.pad(pack, ((0, PACK_ROWS - pack.shape[0]), (0, 0)))


def _unpack_small(pack):
    return (pack[0:2], pack[2].reshape(DEPTH, ATTN_WIDTH), pack[3].reshape(DEPTH, HGRN_WIDTH),
            pack[4, :DEPTH * HGRN_DIM].reshape(DEPTH, HGRN_DIM), pack[5:7], pack[7], pack[8])


def kernel(x, norm_mix, w_in, attn_out_gain, hgrn_lb_logits, hgrn_out_gain, w_out, norm_mlp, w_up, w_down, norm_final, loss_target, m_norm_mix, m_w_in, m_attn_out_gain, m_hgrn_lb_logits, m_hgrn_out_gain, m_w_out, m_norm_mlp, m_w_up, m_w_down, m_norm_final, v_norm_mix, v_w_in, v_attn_out_gain, v_hgrn_lb_logits, v_hgrn_out_gain, v_w_out, v_norm_mlp, v_w_up, v_w_down, v_norm_final):
    lower, lower_vjp = jax.vjp(_lower_bounds, hgrn_lb_logits)
    small = {"norm_mix": norm_mix, "attn_out_gain": attn_out_gain, "lower": lower,
             "hgrn_out_gain": hgrn_out_gain, "norm_mlp": norm_mlp, "norm_final": norm_final}
    big_w = (w_in, w_out, w_up, w_down)

    x_pos, y_pos, core = lax.axis_index("x"), lax.axis_index("y"), lax.axis_index("c")
    me = (2 * x_pos + y_pos).astype(jnp.int32).reshape(1)
    place = jnp.stack([4 * x_pos + 2 * y_pos + core, 2 * x_pos + y_pos, core]).astype(jnp.int32)
    shards = [[w[l].astype(BF16) for w in big_w] for l in range(DEPTH)]
    in_flight = {}

    def start_gather(name, some, after):
        lands = [_own_slot(f"own_{name}_{i}", s, me) for i, s in enumerate(some)]
        sems, passed, token = _exchange_start(f"start_{name}", some, lands, "gather", after)
        in_flight[name] = (sems, passed)
        return token

    def finish_gather(name, after):
        return _exchange_wait(f"wait_{name}", *in_flight.pop(name), "gather", after)[1]

    def get_weights(l, stream):
        if l == 0:
            (win,) = _gather_weights("gather_w_in0", shards[0][:1])
            token = start_gather("gather_rest0", shards[0][1:], win)
            token = start_gather("gather_w_in1", shards[1][:1], token)
            token = start_gather("gather_rest1", shards[1][1:], token)
            return (win, lambda after: finish_gather("gather_rest0", after)), token
        (win,) = finish_gather("gather_w_in1", stream)
        return (win, lambda after: finish_gather("gather_rest1", after)), None

    reduced = {}

    def start_exchange(name, grads):
        srcs, lands = [g for g, _ in grads], [land for _, land in grads]
        sems, passed, token = _exchange_start(f"start_{name}", srcs, lands, "scatter")
        in_flight[name] = (sems, passed)
        return token

    def finish_exchange(name, after):
        own, landed = _exchange_wait(f"wait_{name}", *in_flight.pop(name), "scatter", after)
        return [_sum_devices(f"sum_{name}_{i}", p, g, place) for i, (p, g) in enumerate(zip(landed, own))]

    def on_grads(l, group, grads):
        token = start_exchange(f"{group}{l}", grads)
        if (l, group) == (0, "early"):
            reduced[(1, "early")] = finish_exchange("early1", token)
            reduced[(1, "last")] = finish_exchange("last1", token)
        if (l, group) == (0, "last"):
            reduced[(0, "early")] = finish_exchange("early0", token)
        return token

    loss, dx, dg_final, sg = _local_step(x[0], loss_target[0], small, get_weights, on_grads)

    big_m = (m_w_in, m_w_out, m_w_up, m_w_down)
    big_v = (v_w_in, v_w_out, v_w_up, v_w_down)
    names = ("w_in", "w_out", "w_up", "w_down")
    big_g, big_delta, big_new_m, big_new_v = [None] * 4, [None] * 4, [None] * 4, [None] * 4

    def finish_weights(group, which):
        whole = _share_halves(f"share_{group}", [[reduced[(l, group)][i] for l in range(DEPTH)]
                                                 for i in range(len(which))])
        for i, w in enumerate(which):
            shape = big_w[w].shape
            flat = lambda arr: arr.reshape(shape[0] * shape[1], shape[2])
            d, m2, v2 = _adamw(f"adamw_{names[w]}", flat(big_w[w]), flat(whole[i]), flat(big_m[w]), flat(big_v[w]))
            big_g[w], big_delta[w] = whole[i], d.reshape(shape)
            big_new_m[w], big_new_v[w] = m2.reshape(shape), v2.reshape(shape)

    finish_weights("early", (2, 3, 1))
    reduced[(0, "last")] = finish_exchange("last0", big_delta[3])
    finish_weights("last", (0,))

    stack2 = lambda key: jnp.stack([sg[l][key] for l in range(DEPTH)])
    pack = _pack_small(stack2("norm_mix"), stack2("attn_out_gain"), stack2("lower"), stack2("hgrn_out_gain"),
                       stack2("norm_mlp"), dg_final, jnp.broadcast_to(loss[0, 0], (D_MODEL,)))
    g_mix, g_attn, g_lower, g_hgrn, g_mlp, g_final, loss_row = _unpack_small(_all_reduce_small(pack, big_delta[0]))
    (g_logits,) = lower_vjp(g_lower)

    zeros_row = jnp.zeros((D_MODEL,), F32)
    small_w = (norm_mix, attn_out_gain, hgrn_lb_logits, hgrn_out_gain, norm_mlp, norm_final)
    small_m = (m_norm_mix, m_attn_out_gain, m_hgrn_lb_logits, m_hgrn_out_gain, m_norm_mlp, m_norm_final)
    small_v = (v_norm_mix, v_attn_out_gain, v_hgrn_lb_logits, v_hgrn_out_gain, v_norm_mlp, v_norm_final)
    small_g = (g_mix, g_attn, g_logits, g_hgrn, g_mlp, g_final)
    packs = [_pack_small(*t, zeros_row) for t in (small_w, small_g, small_m, small_v)]
    small_delta, small_new_m, small_new_v = [_unpack_small(p)[:6] for p in _adamw("adamw_small", *packs)]

    def ordered(small6, big4):
        mix, attn, lbl, hg, mlp, fin = small6
        return (mix, big4[0], attn, lbl, hg, big4[1], mlp, big4[2], big4[3], fin)

    return ((loss_row[0], dx[None]) + ordered(small_g, big_g) + ordered(small_delta, big_delta)
            + ordered(small_new_m, big_new_m) + ordered(small_new_v, big_new_v))
```
